```python
import jax, jax.numpy as jnp
from jax import lax
import numpy as np

D_MODEL = 1024
BATCH = 16
SEQ = 2048
DEPTH = 1

GRID_W = 64
CTX_LEN = 256

RET_HEADS = 4
RET_DK = 256
RET_DV = 512
RET_CHUNK = 128
RET_QK_W = RET_HEADS * RET_DK
RET_V_W = RET_HEADS * RET_DV

ATT_HEADS = 8
ATT_KV_HEADS = 2
ATT_HEAD_DIM = 128
ATT_Q_W = ATT_HEADS * ATT_HEAD_DIM
ATT_KV_W = ATT_KV_HEADS * ATT_HEAD_DIM
ROPE_THETA = 10000.0
Q_BLOCK = 128

NORM_EPS = 1e-6

IN_SPLITS = (RET_QK_W, RET_V_W, ATT_KV_W, ATT_KV_W,
             RET_QK_W, RET_V_W, ATT_Q_W, ATT_Q_W, D_MODEL, D_MODEL)
KV_COLS = RET_QK_W + RET_V_W + 2 * ATT_KV_W
IN_COLS = KV_COLS + RET_QK_W + RET_V_W + 2 * ATT_Q_W + 2 * D_MODEL

kernel_name = "hybrid_retention_gqa_prefix_dit_block"


def rms_norm(x, w=None):
    xf = x.astype(jnp.float32)
    y = xf * lax.rsqrt(jnp.mean(xf * xf, axis=-1, keepdims=True) + NORM_EPS)
    if w is not None:
        y = y * w.astype(jnp.float32)
    return y.astype(x.dtype)


def split_cols(p, widths):
    out, start = [], 0
    for w in widths:
        out.append(p[..., start:start + w])
        start += w
    return out


def adaln_params(cvec, w_ada, b_ada):
    mod = jax.nn.silu(cvec) @ w_ada + b_ada
    return split_cols(mod, (D_MODEL, D_MODEL, D_MODEL))


def axial_rope(L):
    rows = L // GRID_W
    row = jnp.repeat(jnp.arange(rows, dtype=jnp.float32), GRID_W)
    col = jnp.tile(jnp.arange(GRID_W, dtype=jnp.float32), rows)
    half = ATT_HEAD_DIM // 2
    freqs = ROPE_THETA ** (-jnp.arange(0, half, 2, dtype=jnp.float32) / half)
    ang = jnp.concatenate([row[:, None] * freqs, col[:, None] * freqs], axis=-1)
    return jnp.cos(ang), jnp.sin(ang)


def apply_rope(x, cos, sin):
    B, L, H, D = x.shape
    xp = x.astype(jnp.float32).reshape(B, L, H, D // 2, 2)
    x0, x1 = xp[..., 0], xp[..., 1]
    c, s = cos[None, :, None, :], sin[None, :, None, :]
    out = jnp.stack([x0 * c - x1 * s, x0 * s + x1 * c], axis=-1)
    return out.reshape(B, L, H, D).astype(x.dtype)


def ret_kv(k_flat, v_flat):
    B, L, _ = k_flat.shape
    k = k_flat.reshape(B, L, RET_HEADS, RET_DK) * (RET_DK ** -0.5)
    v = v_flat.reshape(B, L, RET_HEADS, RET_DV)
    return k, v


def retention_scan(q, k, v, log_gamma, s0):
    B, L, H, _ = q.shape
    n = L // RET_CHUNK

    def chunks(t):
        return t.astype(jnp.float32).reshape(B, n, RET_CHUNK, H, t.shape[-1]).transpose(1, 0, 3, 2, 4)

    qc, kc, vc = chunks(q), chunks(k), chunks(v)
    idx = jnp.arange(RET_CHUNK, dtype=jnp.float32)
    lg = log_gamma.astype(jnp.float32)[:, None]
    rel = idx[:, None] - idx[None, :]
    decay_mask = jnp.where(rel >= 0, jnp.exp(lg[:, :, None] * jnp.maximum(rel, 0.0)), 0.0)
    q_decay = jnp.exp(lg * (idx + 1.0))
    k_decay = jnp.exp(lg * (RET_CHUNK - 1.0 - idx))
    chunk_decay = jnp.exp(lg[:, 0] * RET_CHUNK)

    def step(s, inp):
        qi, ki, vi = inp
        scores = jnp.einsum('bhid,bhjd->bhij', qi, ki) * decay_mask
        intra = jnp.einsum('bhij,bhje->bhie', scores, vi)
        cross = jnp.einsum('bhid,bhde->bhie', qi * q_decay[..., None], s)
        s_new = s * chunk_decay[:, None, None] + jnp.einsum('bhjd,bhje->bhde', ki * k_decay[..., None], vi)
        return s_new, intra + cross

    _, out = lax.scan(step, s0.astype(jnp.float32), (qc, kc, vc))
    return out.transpose(1, 0, 3, 2, 4).reshape(B, L, H, v.shape[-1])


def ctx_final_states(k, v, log_gamma):
    L = k.shape[1]
    pos = jnp.arange(L, dtype=jnp.float32)
    w_f = jnp.exp(log_gamma[0][:, None] * (L - 1.0 - pos))
    w_b = jnp.exp(log_gamma[1][:, None] * pos)
    kf, vf = k.astype(jnp.float32), v.astype(jnp.float32)
    s_f = jnp.einsum('blhd,hl,blhe->bhde', kf, w_f, vf)
    s_b = jnp.einsum('blhd,hl,blhe->bhde', kf, w_b, vf)
    return s_f, s_b


def retention_branch(q_flat, k, v, gate_flat, log_gamma, states):
    B, L, _ = q_flat.shape
    q = q_flat.reshape(B, L, RET_HEADS, RET_DK)
    o_f = retention_scan(q, k, v, log_gamma[0], states[0])
    o_b = retention_scan(q[:, ::-1], k[:, ::-1], v[:, ::-1], log_gamma[1], states[1])[:, ::-1]
    o = rms_norm(o_f + o_b)
    return o.reshape(B, L, RET_V_W).astype(gate_flat.dtype) * jax.nn.silu(gate_flat)


def att_heads(t_flat, n_heads, norm_w, rope):
    B, L, _ = t_flat.shape
    t = rms_norm(t_flat.reshape(B, L, n_heads, ATT_HEAD_DIM), norm_w)
    if rope is not None:
        t = apply_rope(t, rope[0], rope[1])
    return t


def gqa_block_attention(q, k_all, v_all):
    B, S, Hq, D = q.shape
    nb = S // Q_BLOCK
    rep = Hq // ATT_KV_HEADS
    qb = q.reshape(B, nb, Q_BLOCK, ATT_KV_HEADS, rep, D).transpose(1, 0, 2, 3, 4, 5)
    scale = D ** -0.5

    def one_block(qi):
        s = jnp.einsum('bqgrd,bkgd->bgrqk', qi, k_all).astype(jnp.float32) * scale
        p = jax.nn.softmax(s, axis=-1)
        return jnp.einsum('bgrqk,bkgd->bqgrd', p.astype(v_all.dtype), v_all)

    o = lax.map(one_block, qb)
    return o.transpose(1, 0, 2, 3, 4, 5).reshape(B, S, Hq * D)


def merge_branches(y_ret, y_att, mg_ret, mg_att, w_o_ret, w_o_att, w_out):
    y = jax.nn.sigmoid(mg_ret) * (y_ret @ w_o_ret) + jax.nn.sigmoid(mg_att) * (y_att @ w_o_att)
    return y @ w_out


def trunk_layer(x, ctx, c, c_ctx, norm_w, w_ada, b_ada, w_in, ret_log2_decay,
                q_norm_w, k_norm_w, w_o_ret, w_o_att, w_out, update_ctx):
    B, S, _ = x.shape
    shift, scale, gate = adaln_params(c, w_ada, b_ada)
    shift_c, scale_c, gate_c = adaln_params(c_ctx, w_ada, b_ada)
    hx = rms_norm(x, norm_w) * (1 + scale[:, None]) + shift[:, None]
    hc = rms_norm(ctx, norm_w) * (1 + scale_c) + shift_c
    log_gamma = jnp.log1p(-jnp.exp2(ret_log2_decay.astype(jnp.float32)))

    px = hx @ w_in
    ret_k, ret_v, att_k, att_v, ret_q, ret_g, att_q, att_g, mg_ret, mg_att = split_cols(px, IN_SPLITS)
    pc = hc @ (w_in if update_ctx else w_in[:, :KV_COLS])
    c_ret_k, c_ret_v, c_att_k, c_att_v = split_cols(pc, IN_SPLITS[:4])

    kc, vc = ret_kv(c_ret_k, c_ret_v)
    ctx_states = ctx_final_states(kc, vc, log_gamma)
    kca = att_heads(c_att_k, ATT_KV_HEADS, k_norm_w, None)
    vca = c_att_v.reshape(B, CTX_LEN, ATT_KV_HEADS, ATT_HEAD_DIM)

    kx, vx = ret_kv(ret_k, ret_v)
    y_ret = retention_branch(ret_q, kx, vx, ret_g, log_gamma, ctx_states)

    rope = axial_rope(S)
    qxa = att_heads(att_q, ATT_HEADS, q_norm_w, rope)
    kxa = att_heads(att_k, ATT_KV_HEADS, k_norm_w, rope)
    vxa = att_v.reshape(B, S, ATT_KV_HEADS, ATT_HEAD_DIM)
    k_all = jnp.concatenate([kxa, kca], axis=1)
    v_all = jnp.concatenate([vxa, vca], axis=1)
    y_att = gqa_block_attention(qxa, k_all, v_all) * jax.nn.silu(att_g)

    out = merge_branches(y_ret, y_att, mg_ret, mg_att, w_o_ret, w_o_att, w_out)
    x_new = x + gate[:, None] * out

    if update_ctx:
        _, _, _, _, c_ret_q, c_ret_g, c_att_q, c_att_g, c_mg_ret, c_mg_att = split_cols(pc, IN_SPLITS)
        zero = jnp.zeros((B, RET_HEADS, RET_DK, RET_DV), jnp.float32)
        yc_ret = retention_branch(c_ret_q, kc, vc, c_ret_g, log_gamma, (zero, zero))
        qca = att_heads(c_att_q, ATT_HEADS, q_norm_w, None)
        yc_att = gqa_block_attention(qca, kca, vca) * jax.nn.silu(c_att_g)
        out_c = merge_branches(yc_ret, yc_att, c_mg_ret, c_mg_att, w_o_ret, w_o_att, w_out)
        ctx = ctx + gate_c * out_c
    return x_new, ctx


def _fwd_setup_inputs(seed: int = 0) -> dict:
    key = jax.random.key(seed)
    ks = jax.random.split(key, 15)
    f32 = jnp.float32
    nrm = lambda k, shape: jax.random.normal(k, shape, f32)
    x = nrm(ks[0], (BATCH, SEQ, D_MODEL))
    c = nrm(ks[1], (BATCH, D_MODEL))
    ctx = nrm(ks[2], (BATCH, CTX_LEN, D_MODEL))
    c_ctx = nrm(ks[3], (D_MODEL,))
    norm_w = 1.0 + 0.02 * nrm(ks[4], (DEPTH, D_MODEL))
    w_ada = nrm(ks[5], (DEPTH, D_MODEL, 3 * D_MODEL)) * (0.5 * D_MODEL ** -0.5)
    b_ada = 0.02 * nrm(ks[6], (DEPTH, 3 * D_MODEL))
    w_in = nrm(ks[7], (DEPTH, D_MODEL, IN_COLS)) * (D_MODEL ** -0.5)
    base = -5.0 - jnp.arange(RET_HEADS, dtype=f32)
    ret_log2_decay = base[None, None, :] + 0.1 * nrm(ks[8], (DEPTH, 2, RET_HEADS))
    q_norm_w = 1.0 + 0.02 * nrm(ks[9], (DEPTH, ATT_HEAD_DIM))
    k_norm_w = 1.0 + 0.02 * nrm(ks[10], (DEPTH, ATT_HEAD_DIM))
    w_o_ret = nrm(ks[11], (DEPTH, RET_V_W, D_MODEL)) * (RET_V_W ** -0.5)
    w_o_att = nrm(ks[12], (DEPTH, ATT_Q_W, D_MODEL)) * (ATT_Q_W ** -0.5)
    w_out = nrm(ks[13], (DEPTH, D_MODEL, D_MODEL)) * (D_MODEL ** -0.5)
    return {"x": x, "c": c, "ctx": ctx, "c_ctx": c_ctx, "norm_w": norm_w,
            "w_ada": w_ada, "b_ada": b_ada, "w_in": w_in, "ret_log2_decay": ret_log2_decay,
            "q_norm_w": q_norm_w, "k_norm_w": k_norm_w, "w_o_ret": w_o_ret,
            "w_o_att": w_o_att, "w_out": w_out}


def _fwd_reference(x, c, ctx, c_ctx, norm_w, w_ada, b_ada, w_in, ret_log2_decay,
              q_norm_w, k_norm_w, w_o_ret, w_o_att, w_out):
    for layer in range(DEPTH):
        x, ctx = trunk_layer(x, ctx, c, c_ctx, norm_w[layer], w_ada[layer], b_ada[layer],
                             w_in[layer], ret_log2_decay[layer], q_norm_w[layer], k_norm_w[layer],
                             w_o_ret[layer], w_o_att[layer], w_out[layer],
                             update_ctx=(layer < DEPTH - 1))
    return x


import jax as _jax
import jax.numpy as _jnp

TWIN_FORMAT = 'train_step'
FWD_PARAMS = ['x', 'c', 'ctx', 'c_ctx', 'norm_w', 'w_ada', 'b_ada', 'w_in', 'ret_log2_decay', 'q_norm_w', 'k_norm_w', 'w_o_ret', 'w_o_att', 'w_out']
TWIN_WEIGHTS = ['c_ctx', 'norm_w', 'w_ada', 'b_ada', 'w_in', 'ret_log2_decay', 'q_norm_w', 'k_norm_w', 'w_o_ret', 'w_o_att', 'w_out']
TWIN_DIFF_INPUT = 'x'
TWIN_INPUTS = ['x', 'c', 'ctx', 'c_ctx', 'norm_w', 'w_ada', 'b_ada', 'w_in', 'ret_log2_decay', 'q_norm_w', 'k_norm_w', 'w_o_ret', 'w_o_att', 'w_out', 'loss_target', 'm_c_ctx', 'm_norm_w', 'm_w_ada', 'm_b_ada', 'm_w_in', 'm_ret_log2_decay', 'm_q_norm_w', 'm_k_norm_w', 'm_w_o_ret', 'm_w_o_att', 'm_w_out', 'v_c_ctx', 'v_norm_w', 'v_w_ada', 'v_b_ada', 'v_w_in', 'v_ret_log2_decay', 'v_q_norm_w', 'v_k_norm_w', 'v_w_o_ret', 'v_w_o_att', 'v_w_out']
TWIN_OUTPUTS = ['loss', 'grad_x', 'grad_c_ctx', 'grad_norm_w', 'grad_w_ada', 'grad_b_ada', 'grad_w_in', 'grad_ret_log2_decay', 'grad_q_norm_w', 'grad_k_norm_w', 'grad_w_o_ret', 'grad_w_o_att', 'grad_w_out', 'delta_c_ctx', 'delta_norm_w', 'delta_w_ada', 'delta_b_ada', 'delta_w_in', 'delta_ret_log2_decay', 'delta_q_norm_w', 'delta_k_norm_w', 'delta_w_o_ret', 'delta_w_o_att', 'delta_w_out', 'new_m_c_ctx', 'new_m_norm_w', 'new_m_w_ada', 'new_m_b_ada', 'new_m_w_in', 'new_m_ret_log2_decay', 'new_m_q_norm_w', 'new_m_k_norm_w', 'new_m_w_o_ret', 'new_m_w_o_att', 'new_m_w_out', 'new_v_c_ctx', 'new_v_norm_w', 'new_v_w_ada', 'new_v_b_ada', 'new_v_w_in', 'new_v_ret_log2_decay', 'new_v_q_norm_w', 'new_v_k_norm_w', 'new_v_w_o_ret', 'new_v_w_o_att', 'new_v_w_out']
TWIN_LEAF_KINDS = {'loss': 'loss', 'grad_x': 'grad_x', 'grad_c_ctx': 'grad_w', 'grad_norm_w': 'grad_w', 'grad_w_ada': 'grad_w', 'grad_b_ada': 'grad_w', 'grad_w_in': 'grad_w', 'grad_ret_log2_decay': 'grad_w', 'grad_q_norm_w': 'grad_w', 'grad_k_norm_w': 'grad_w', 'grad_w_o_ret': 'grad_w', 'grad_w_o_att': 'grad_w', 'grad_w_out': 'grad_w', 'delta_c_ctx': 'delta_w', 'delta_norm_w': 'delta_w', 'delta_w_ada': 'delta_w', 'delta_b_ada': 'delta_w', 'delta_w_in': 'delta_w', 'delta_ret_log2_decay': 'delta_w', 'delta_q_norm_w': 'delta_w', 'delta_k_norm_w': 'delta_w', 'delta_w_o_ret': 'delta_w', 'delta_w_o_att': 'delta_w', 'delta_w_out': 'delta_w', 'new_m_c_ctx': 'new_m', 'new_m_norm_w': 'new_m', 'new_m_w_ada': 'new_m', 'new_m_b_ada': 'new_m', 'new_m_w_in': 'new_m', 'new_m_ret_log2_decay': 'new_m', 'new_m_q_norm_w': 'new_m', 'new_m_k_norm_w': 'new_m', 'new_m_w_o_ret': 'new_m', 'new_m_w_o_att': 'new_m', 'new_m_w_out': 'new_m', 'new_v_c_ctx': 'new_v', 'new_v_norm_w': 'new_v', 'new_v_w_ada': 'new_v', 'new_v_b_ada': 'new_v', 'new_v_w_in': 'new_v', 'new_v_ret_log2_decay': 'new_v', 'new_v_q_norm_w': 'new_v', 'new_v_k_norm_w': 'new_v', 'new_v_w_o_ret': 'new_v', 'new_v_w_o_att': 'new_v', 'new_v_w_out': 'new_v'}


def _forward(args):
    return _fwd_reference(*[args[k] for k in FWD_PARAMS])


def _output_shape():
    out = _jax.eval_shape(lambda: _forward(_fwd_setup_inputs(0)))
    return out.shape, out.dtype

N_MICROBATCH = 1
ADAM_LR = 0.001
ADAM_B1 = 0.9
ADAM_B2 = 0.999
ADAM_EPS = 1e-08
ADAM_WD = 0.01
ADAM_STEP = 10
PER_EXAMPLE_BATCH_AXIS = {'x': 0, 'c': 0, 'ctx': 0, 'loss_target': 0}
SHARED_INPUTS = []
_WEIGHT_DTYPES = {'c_ctx': _jnp.float32, 'norm_w': _jnp.float32, 'w_ada': _jnp.float32, 'b_ada': _jnp.float32, 'w_in': _jnp.float32, 'ret_log2_decay': _jnp.float32, 'q_norm_w': _jnp.float32, 'k_norm_w': _jnp.float32, 'w_o_ret': _jnp.float32, 'w_o_att': _jnp.float32, 'w_out': _jnp.float32}
MOMENT_SCALE = {'c_ctx': 1.648034e-02, 'norm_w': 4.708044e-01, 'w_ada': 2.282142e-01, 'b_ada': 4.653795e-01, 'w_in': 2.459199e-02, 'ret_log2_decay': 8.521348e-02, 'q_norm_w': 7.933596e-03, 'k_norm_w': 7.802238e-03, 'w_o_ret': 3.154113e-02, 'w_o_att': 1.334411e-02, 'w_out': 3.425674e-02}


def _to_microbatches(a, axis):
    t = _jnp.moveaxis(a, axis, 0)
    t = t.reshape((N_MICROBATCH, t.shape[0] // N_MICROBATCH) + t.shape[1:])
    return _jnp.moveaxis(t, 1, axis + 1)


def setup_inputs(seed: int = 0) -> dict:
    inp = _fwd_setup_inputs(seed)
    key = _jax.random.fold_in(_jax.random.key(seed), 7919)
    shape, _ = _output_shape()
    out = dict(inp)
    out["loss_target"] = _jax.random.normal(_jax.random.fold_in(key, 0), shape, _jnp.float32)
    for i, name in enumerate(TWIN_WEIGHTS):
        w = inp[name].astype(_jnp.float32)
        if MOMENT_SCALE is None:
            s = _jnp.sqrt(_jnp.mean(_jnp.square(w)) + 1e-30)
        else:
            s = MOMENT_SCALE[name]
        km, kv = _jax.random.split(_jax.random.fold_in(key, i + 1))
        out[name] = w
        out["m_" + name] = s * _jax.random.normal(km, w.shape, _jnp.float32)
        out["v_" + name] = (s * s) * _jax.random.uniform(kv, w.shape, _jnp.float32, 0.5, 1.5)
    if N_MICROBATCH > 1:
        for name, axis in PER_EXAMPLE_BATCH_AXIS.items():
            out[name] = _to_microbatches(out[name], axis)
    return {'x': out['x'], 'c': out['c'], 'ctx': out['ctx'], 'c_ctx': out['c_ctx'], 'norm_w': out['norm_w'], 'w_ada': out['w_ada'], 'b_ada': out['b_ada'], 'w_in': out['w_in'], 'ret_log2_decay': out['ret_log2_decay'], 'q_norm_w': out['q_norm_w'], 'k_norm_w': out['k_norm_w'], 'w_o_ret': out['w_o_ret'], 'w_o_att': out['w_o_att'], 'w_out': out['w_out'], 'loss_target': out['loss_target'], 'm_c_ctx': out['m_c_ctx'], 'm_norm_w': out['m_norm_w'], 'm_w_ada': out['m_w_ada'], 'm_b_ada': out['m_b_ada'], 'm_w_in': out['m_w_in'], 'm_ret_log2_decay': out['m_ret_log2_decay'], 'm_q_norm_w': out['m_q_norm_w'], 'm_k_norm_w': out['m_k_norm_w'], 'm_w_o_ret': out['m_w_o_ret'], 'm_w_o_att': out['m_w_o_att'], 'm_w_out': out['m_w_out'], 'v_c_ctx': out['v_c_ctx'], 'v_norm_w': out['v_norm_w'], 'v_w_ada': out['v_w_ada'], 'v_b_ada': out['v_b_ada'], 'v_w_in': out['v_w_in'], 'v_ret_log2_decay': out['v_ret_log2_decay'], 'v_q_norm_w': out['v_q_norm_w'], 'v_k_norm_w': out['v_k_norm_w'], 'v_w_o_ret': out['v_w_o_ret'], 'v_w_o_att': out['v_w_o_att'], 'v_w_out': out['v_w_out']}


def _loss(weights, diff, rest, loss_target):
    with _jax.named_scope("forward"):
        args = {**rest, TWIN_DIFF_INPUT: diff, **{k: w.astype(_WEIGHT_DTYPES[k]) for k, w in weights.items()}}
        y = _forward(args)
    with _jax.named_scope("loss_head"):
        err = _jnp.square(y.astype(_jnp.float32) - loss_target)
        return 0.5 * _jnp.sum(_jnp.mean(err, axis=-1)) if err.ndim else 0.5 * err


def _adamw(w, g, m, v):
    m = ADAM_B1 * m + (1.0 - ADAM_B1) * g
    v = ADAM_B2 * v + (1.0 - ADAM_B2) * _jnp.square(g)
    m_hat = m / (1.0 - ADAM_B1 ** ADAM_STEP)
    v_hat = v / (1.0 - ADAM_B2 ** ADAM_STEP)
    delta = -ADAM_LR * (m_hat / (_jnp.sqrt(v_hat) + ADAM_EPS) + ADAM_WD * w)
    return delta, m, v


def reference(x, c, ctx, c_ctx, norm_w, w_ada, b_ada, w_in, ret_log2_decay, q_norm_w, k_norm_w, w_o_ret, w_o_att, w_out, loss_target, m_c_ctx, m_norm_w, m_w_ada, m_b_ada, m_w_in, m_ret_log2_decay, m_q_norm_w, m_k_norm_w, m_w_o_ret, m_w_o_att, m_w_out, v_c_ctx, v_norm_w, v_w_ada, v_b_ada, v_w_in, v_ret_log2_decay, v_q_norm_w, v_k_norm_w, v_w_o_ret, v_w_o_att, v_w_out):
    given = dict(x=x, c=c, ctx=ctx, c_ctx=c_ctx, norm_w=norm_w, w_ada=w_ada, b_ada=b_ada, w_in=w_in, ret_log2_decay=ret_log2_decay, q_norm_w=q_norm_w, k_norm_w=k_norm_w, w_o_ret=w_o_ret, w_o_att=w_o_att, w_out=w_out, loss_target=loss_target, m_c_ctx=m_c_ctx, m_norm_w=m_norm_w, m_w_ada=m_w_ada, m_b_ada=m_b_ada, m_w_in=m_w_in, m_ret_log2_decay=m_ret_log2_decay, m_q_norm_w=m_q_norm_w, m_k_norm_w=m_k_norm_w, m_w_o_ret=m_w_o_ret, m_w_o_att=m_w_o_att, m_w_out=m_w_out, v_c_ctx=v_c_ctx, v_norm_w=v_norm_w, v_w_ada=v_w_ada, v_b_ada=v_b_ada, v_w_in=v_w_in, v_ret_log2_decay=v_ret_log2_decay, v_q_norm_w=v_q_norm_w, v_k_norm_w=v_k_norm_w, v_w_o_ret=v_w_o_ret, v_w_o_att=v_w_o_att, v_w_out=v_w_out)
    weights = {n: given[n] for n in TWIN_WEIGHTS}
    shared = {n: given[n] for n in SHARED_INPUTS}
    per_example = {n: given[n] for n in ['x', 'c', 'ctx']}
    grad_fn = _jax.value_and_grad(_loss, argnums=(0, 1))

    def one_microbatch(ex, loss_target):
        ex = dict(ex)
        diff = ex.pop(TWIN_DIFF_INPUT)
        return grad_fn(weights, diff, {**shared, **ex}, loss_target)

    if N_MICROBATCH == 1:
        loss, (grad_w, grad_x) = one_microbatch(per_example, given["loss_target"])
    else:
        def body(carry, xs):
            loss_sum, grad_sum = carry
            l_k, (gw_k, gx_k) = one_microbatch(xs[0], xs[1])
            with _jax.named_scope("update"):
                return (loss_sum + l_k, _jax.tree.map(_jnp.add, grad_sum, gw_k)), gx_k

        init = (_jnp.zeros((), _jnp.float32), _jax.tree.map(_jnp.zeros_like, weights))
        (loss, grad_w), grad_x = _jax.lax.scan(body, init, (per_example, given["loss_target"]))
    with _jax.named_scope("update"):
        delta_w, new_m, new_v = {}, {}, {}
        for n in TWIN_WEIGHTS:
            delta_w[n], new_m[n], new_v[n] = _adamw(weights[n], grad_w[n], given["m_" + n], given["v_" + n])
    return (loss, grad_x, *[grad_w[n] for n in TWIN_WEIGHTS], *[delta_w[n] for n in TWIN_WEIGHTS],
            *[new_m[n] for n in TWIN_WEIGHTS], *[new_v[n] for n in TWIN_WEIGHTS])
```

```python
import numpy as np
import jax
import jax.numpy as jnp
from jax import lax
from jax.experimental import pallas as pl
from jax.experimental.pallas import tpu as pltpu

F32 = jnp.float32
BF16 = jnp.bfloat16

D = 1024
RH, DK, DV, CH = 4, 256, 512, 128
HQ, HKV, HD = 8, 2, 128
GRID_W = 64
ROPE_THETA = 10000.0
EPS = 1e-6
RK, RV, AK, AV, RQ, RG, AQ, AG, MR, MA = 0, 1024, 3072, 3328, 3584, 4608, 6656, 7680, 8704, 9728
IN_COLS = 10752
KV_COLS = 3584
N_DEV = 8
LR, B1, B2, ADAM_EPS, WD, STEP = 0.001, 0.9, 0.999, 1e-08, 0.01, 10
PAY_ROWS = 16
VMEM_LIMIT = 56 * 1024 * 1024
MESH_T = pl.DeviceIdType.MESH

NT = (((1,), (1,)), ((), ()))
TN = (((0,), (0,)), ((), ()))


def _params(sem):
    return pltpu.CompilerParams(dimension_semantics=sem, vmem_limit_bytes=VMEM_LIMIT)


def _pick(n, target, mult=8):
    best = None
    for t in range(mult, min(n, target) + 1, mult):
        if n % t == 0:
            best = t
    return best or n


def _dot(a, b, dn=None):
    if dn is None:
        return jnp.dot(a, b, preferred_element_type=F32)
    return lax.dot_general(a, b, dn, preferred_element_type=F32)


def _sig(v):
    return jax.nn.sigmoid(v)


def _silu(v):
    return v * _sig(v)


def _dsilu(v):
    s = _sig(v)
    return s * (1.0 + v * (1.0 - s))


def _sds(shape, dtype):
    return jax.ShapeDtypeStruct(shape, dtype)


def _matmul(a, b, *, ta=False, tb=False, tm, tn, tk, out_dtype, name):
    m = a.shape[1] if ta else a.shape[0]
    kdim = a.shape[0] if ta else a.shape[1]
    n = b.shape[0] if tb else b.shape[1]
    tm, tn, tk = _pick(m, tm, 128), _pick(n, tn, 128), _pick(kdim, tk, 128)
    nk = kdim // tk
    dn = (((0 if ta else 1,), (1 if tb else 0,)), ((), ()))

    def body(a_ref, b_ref, o_ref, acc_ref):
        k = pl.program_id(2)
        part = _dot(a_ref[...].astype(BF16), b_ref[...].astype(BF16), dn)
        if nk == 1:
            o_ref[...] = part.astype(o_ref.dtype)
        else:
            @pl.when(k == 0)
            def _():
                acc_ref[...] = part

            @pl.when(k > 0)
            def _():
                acc_ref[...] += part

            @pl.when(k == nk - 1)
            def _():
                o_ref[...] = acc_ref[...].astype(o_ref.dtype)

    a_spec = pl.BlockSpec((tk, tm), lambda i, j, k: (k, i)) if ta else pl.BlockSpec((tm, tk), lambda i, j, k: (i, k))
    b_spec = pl.BlockSpec((tn, tk), lambda i, j, k: (j, k)) if tb else pl.BlockSpec((tk, tn), lambda i, j, k: (k, j))
    return pl.pallas_call(
        body, name=name, grid=(m // tm, n // tn, nk),
        in_specs=[a_spec, b_spec], out_specs=pl.BlockSpec((tm, tn), lambda i, j, k: (i, j)),
        out_shape=_sds((m, n), out_dtype),
        scratch_shapes=[pltpu.VMEM((tm, tn) if nk > 1 else (8, 128), F32)],
        compiler_params=_params(("parallel", "parallel", "arbitrary")),
    )(a, b)


def _log_gamma(r):
    rp = jnp.full((8, 128), -1.0, F32).at[:2, :RH].set(r.reshape(2, RH))

    def body(r_ref, o_ref):
        o_ref[...] = jnp.log1p(-jnp.exp2(r_ref[...]))

    out = pl.pallas_call(body, name="log_gamma", out_shape=_sds((8, 128), F32))(rp)
    return out[:2, :RH]


def _mod_fwd(c8, w_ada16, b_ada):
    def body(c_ref, w_ref, b_ref, o_ref):
        o_ref[...] = _dot(_silu(c_ref[...]).astype(BF16), w_ref[...]) + b_ref[...]

    return pl.pallas_call(
        body, name="mod_fwd", grid=(3,),
        in_specs=[pl.BlockSpec((8, D), lambda j: (0, 0)), pl.BlockSpec((D, D), lambda j: (0, j)),
                  pl.BlockSpec((1, D), lambda j: (0, j))],
        out_specs=pl.BlockSpec((8, D), lambda j: (0, j)), out_shape=_sds((8, 3 * D), F32),
        compiler_params=_params(("arbitrary",)),
    )(c8, w_ada16, b_ada)


def _norm_fwd(x_all, mod3, norm_w, n_lat_blocks, blocks_per_batch, ctx_row, tm):
    rows = x_all.shape[0]

    def mrow(i):
        return jnp.where(i < n_lat_blocks, i // blocks_per_batch, ctx_row)

    def body(x_ref, sh_ref, sc_ref, nw_ref, o_ref):
        xv = x_ref[...]
        r = lax.rsqrt(jnp.mean(xv * xv, axis=-1, keepdims=True) + EPS)
        o_ref[...] = ((xv * r) * nw_ref[...] * (1.0 + sc_ref[...]) + sh_ref[...]).astype(BF16)

    return pl.pallas_call(
        body, name="norm_fwd", grid=(rows // tm,),
        in_specs=[pl.BlockSpec((tm, D), lambda i: (i, 0)),
                  pl.BlockSpec((None, 1, D), lambda i: (mrow(i), 0, 0)),
                  pl.BlockSpec((None, 1, D), lambda i: (mrow(i), 0, 1)),
                  pl.BlockSpec((1, D), lambda i: (0, 0))],
        out_specs=pl.BlockSpec((tm, D), lambda i: (i, 0)), out_shape=_sds((rows, D), BF16),
        compiler_params=_params(("parallel",)),
    )(x_all, mod3, mod3, norm_w)


def _decays(lg, fwd):
    ii = lax.broadcasted_iota(jnp.int32, (CH, CH), 0)
    jj = lax.broadcasted_iota(jnp.int32, (CH, CH), 1)
    ri = lax.broadcasted_iota(jnp.int32, (CH, 1), 0).astype(F32)
    rel = (ii - jj) if fwd else (jj - ii)
    relf = jnp.maximum(rel, 0).astype(F32)
    mask = jnp.where(rel >= 0, jnp.exp(lg * relf), 0.0)
    qe = (ri + 1.0) if fwd else (CH - ri)
    ke = (CH - 1.0 - ri) if fwd else ri
    return mask, relf, jnp.exp(lg * qe), qe, jnp.exp(lg * ke), ke


def _head_spec(rowf, width, col0):
    return pl.BlockSpec((CH, width), lambda b, h, c: (rowf(b, h, c), col0 // width + h))


def _qkv_specs(rowf):
    return [_head_spec(rowf, DK, RQ), _head_spec(rowf, DK, RK), _head_spec(rowf, DV, RV)]


def _ctx_state(px, lg, nb, t_rows, cx):
    rb = t_rows // cx

    def body(lg_ref, k_ref, v_ref, sf_ref, sb_ref):
        h = pl.program_id(1)
        pos = lax.broadcasted_iota(jnp.int32, (cx, 1), 0).astype(F32)
        k = k_ref[...] * (DK ** -0.5)
        v16 = v_ref[...].astype(BF16)
        wf = jnp.exp(lg_ref[0, h] * (cx - 1.0 - pos))
        wb = jnp.exp(lg_ref[1, h] * pos)
        sf_ref[...] = _dot((k * wf).astype(BF16), v16, TN)
        sb_ref[...] = _dot((k * wb).astype(BF16), v16, TN)

    st = pl.BlockSpec((None, None, DK, DV), lambda b, h: (b, h, 0, 0))
    return pl.pallas_call(
        body, name="ctx_state", grid=(nb, RH),
        in_specs=[pl.BlockSpec(memory_space=pltpu.SMEM),
                  pl.BlockSpec((cx, DK), lambda b, h: (rb + b, RK // DK + h)),
                  pl.BlockSpec((cx, DV), lambda b, h: (rb + b, RV // DV + h))],
        out_specs=[st, st], out_shape=[_sds((nb, RH, DK, DV), F32)] * 2,
        compiler_params=_params(("parallel", "parallel")),
    )(lg, px, px)


def _ret_fwd(px, lg, s0f, s0b, nb, nc):
    t_rows = nb * nc * CH

    def body(lg_ref, qf_ref, kf_ref, vf_ref, qb_ref, kb_ref, vb_ref, s0f_ref, s0b_ref,
             of_ref, ob_ref, hf_ref, hb_ref, sf, sb):
        h = pl.program_id(1)
        c = pl.program_id(2)

        @pl.when(c == 0)
        def _():
            sf[...] = s0f_ref[...]
            sb[...] = s0b_ref[...]

        for d, (q_ref, k_ref, v_ref, o_ref, h_ref, s) in enumerate(
                ((qf_ref, kf_ref, vf_ref, of_ref, hf_ref, sf), (qb_ref, kb_ref, vb_ref, ob_ref, hb_ref, sb))):
            lg_d = lg_ref[d, h]
            mask, _, qd, _, kd, _ = _decays(lg_d, d == 0)
            q = q_ref[...]
            k = k_ref[...] * (DK ** -0.5)
            v16 = v_ref[...].astype(BF16)
            a = _dot(q.astype(BF16), k.astype(BF16), NT)
            st = s[...]
            st16 = st.astype(BF16)
            h_ref[...] = st16
            o_ref[...] = _dot((a * mask).astype(BF16), v16) + _dot((q * qd).astype(BF16), st16)
            s[...] = st * jnp.exp(lg_d * CH) + _dot((k * kd).astype(BF16), v16, TN)

    def fw(b, h, c):
        return b * nc + c

    def bw(b, h, c):
        return b * nc + nc - 1 - c

    st = pl.BlockSpec((None, None, DK, DV), lambda b, h, c: (b, h, 0, 0))
    in_specs = [pl.BlockSpec(memory_space=pltpu.SMEM)]
    for rowf in (fw, bw):
        in_specs += _qkv_specs(rowf)
    in_specs += [st, st]
    out_specs = [pl.BlockSpec((CH, DV), lambda b, h, c: (fw(b, h, c), h)),
                 pl.BlockSpec((CH, DV), lambda b, h, c: (bw(b, h, c), h)),
                 pl.BlockSpec((None, None, None, DK, DV), lambda b, h, c: (b, h, c, 0, 0)),
                 pl.BlockSpec((None, None, None, DK, DV), lambda b, h, c: (b, h, nc - 1 - c, 0, 0))]
    return pl.pallas_call(
        body, name="ret_fwd", grid=(nb, RH, nc), in_specs=in_specs, out_specs=out_specs,
        out_shape=[_sds((t_rows, RH * DV), F32)] * 2 + [_sds((nb, RH, nc, DK, DV), BF16)] * 2,
        scratch_shapes=[pltpu.VMEM((DK, DV), F32), pltpu.VMEM((DK, DV), F32)],
        compiler_params=_params(("parallel", "parallel", "arbitrary")),
    )(lg, px, px, px, px, px, px, s0f, s0b)


def _ret_post(o_f, o_b, px, tm):
    t_rows = o_f.shape[0]

    def body(of_ref, ob_ref, g_ref, y_ref):
        o = of_ref[...] + ob_ref[...]
        r = lax.rsqrt(jnp.mean(o * o, axis=-1, keepdims=True) + EPS)
        y_ref[...] = ((o * r) * _silu(g_ref[...])).astype(BF16)

    blk = pl.BlockSpec((tm, DV), lambda i, h: (i, h))
    return pl.pallas_call(
        body, name="ret_post", grid=(t_rows // tm, RH),
        in_specs=[blk, blk, pl.BlockSpec((tm, DV), lambda i, h: (i, RG // DV + h))],
        out_specs=blk, out_shape=_sds((t_rows, RH * DV), BF16),
        compiler_params=_params(("parallel", "parallel")),
    )(o_f, o_b, px)


def _rope_tables(seq):
    rows = seq // GRID_W
    row = np.repeat(np.arange(rows, dtype=np.float32), GRID_W)
    col = np.tile(np.arange(GRID_W, dtype=np.float32), rows)
    half = HD // 2
    freqs = (ROPE_THETA ** (-np.arange(0, half, 2, dtype=np.float32) / half)).astype(np.float32)
    ang = np.concatenate([row[:, None] * freqs, col[:, None] * freqs], axis=-1).astype(np.float32)
    cos = np.repeat(np.cos(ang), 2, axis=-1).astype(np.float32)
    sin = np.repeat(np.sin(ang), 2, axis=-1).astype(np.float32)
    sign = np.tile(np.array([-1.0, 1.0], np.float32), HD // 2)
    return jnp.asarray(cos), jnp.asarray(sin * sign)


def _swap_pairs(v):
    lane = lax.broadcasted_iota(jnp.int32, v.shape, 1)
    return jnp.where((lane & 1) == 0, pltpu.roll(v, HD - 1, 1), pltpu.roll(v, 1, 1))


def _qk_prep(px, nw, cos, sin, rows, row_off, col_off, heads, seq, tm, name):
    rope = cos is not None
    rb0 = row_off // tm
    pb = seq // tm if rope else 1

    def body(*refs):
        if rope:
            x_ref, w_ref, c_ref, s_ref, o_ref = refs
        else:
            x_ref, w_ref, o_ref = refs
        xv = x_ref[...]
        r = lax.rsqrt(jnp.mean(xv * xv, axis=-1, keepdims=True) + EPS)
        t = (xv * r) * w_ref[...]
        if rope:
            t = t * c_ref[...] + _swap_pairs(t) * s_ref[...]
        o_ref[...] = t.astype(BF16)

    in_specs = [pl.BlockSpec((tm, HD), lambda i, h: (rb0 + i, col_off // HD + h)),
                pl.BlockSpec((1, HD), lambda i, h: (0, 0))]
    args = [px, nw]
    if rope:
        in_specs += [pl.BlockSpec((tm, HD), lambda i, h: (i % pb, 0))] * 2
        args += [cos, sin]
    return pl.pallas_call(
        body, name=name, grid=(rows // tm, heads), in_specs=in_specs,
        out_specs=pl.BlockSpec((tm, HD), lambda i, h: (i, h)), out_shape=_sds((rows, heads * HD), BF16),
        compiler_params=_params(("parallel", "parallel")),
    )(*args)


def _softmax_parts(q, kx, kc):
    scale = HD ** -0.5
    s1 = _dot(q, kx, NT) * scale
    s2 = _dot(q, kc, NT) * scale
    m = jnp.maximum(jnp.max(s1, axis=-1, keepdims=True), jnp.max(s2, axis=-1, keepdims=True))
    e1 = jnp.exp(s1 - m)
    e2 = jnp.exp(s2 - m)
    inv = 1.0 / (jnp.sum(e1, axis=-1, keepdims=True) + jnp.sum(e2, axis=-1, keepdims=True))
    return e1 * inv, e2 * inv


def _att_fwd(q16, kx16, kc16, px, nb, seq, cx, tq):
    t_rows = nb * seq
    nq = seq // tq
    rep = HQ // HKV
    gw = rep * HD

    def body(q_ref, kx_ref, kc_ref, vx_ref, vc_ref, g_ref, o_ref, y_ref):
        kx = kx_ref[...]
        kc = kc_ref[...]
        vx = vx_ref[...].astype(BF16)
        vc = vc_ref[...].astype(BF16)
        for r in range(rep):
            sl = slice(r * HD, (r + 1) * HD)
            p1, p2 = _softmax_parts(q_ref[:, sl], kx, kc)
            o = _dot(p1.astype(BF16), vx) + _dot(p2.astype(BF16), vc)
            o_ref[:, sl] = o
            y_ref[:, sl] = (o * _silu(g_ref[:, sl])).astype(BF16)

    qblk = pl.BlockSpec((tq, gw), lambda b, g, i: (b * nq + i, g))
    return pl.pallas_call(
        body, name="att_fwd", grid=(nb, HKV, nq),
        in_specs=[qblk,
                  pl.BlockSpec((seq, HD), lambda b, g, i: (b, g)),
                  pl.BlockSpec((cx, HD), lambda b, g, i: (b, g)),
                  pl.BlockSpec((seq, HD), lambda b, g, i: (b, AV // HD + g)),
                  pl.BlockSpec((cx, HD), lambda b, g, i: (t_rows // cx + b, AV // HD + g)),
                  pl.BlockSpec((tq, gw), lambda b, g, i: (b * nq + i, AG // gw + g))],
        out_specs=[qblk, qblk], out_shape=[_sds((t_rows, D), F32), _sds((t_rows, D), BF16)],
        compiler_params=_params(("parallel", "parallel", "parallel")),
    )(q16, kx16, kc16, px, px, px)


def _merge(yret16, yatt16, px, w_o_ret16, w_o_att16, tm):
    t_rows = yret16.shape[0]
    hw = D // 2

    def body(yr_ref, wr_ref, ya_ref, wa_ref, mr_ref, ma_ref, ar_ref, aa_ref, y_ref):
        ar = _dot(yr_ref[...], wr_ref[...])
        aa = _dot(ya_ref[...], wa_ref[...])
        ar_ref[...] = ar
        aa_ref[...] = aa
        y_ref[...] = (_sig(mr_ref[...]) * ar + _sig(ma_ref[...]) * aa).astype(BF16)

    half = pl.BlockSpec((tm, hw), lambda i, j: (i, j))
    return pl.pallas_call(
        body, name="merge", grid=(t_rows // tm, 2),
        in_specs=[pl.BlockSpec((tm, RH * DV), lambda i, j: (i, 0)), pl.BlockSpec((RH * DV, hw), lambda i, j: (0, j)),
                  pl.BlockSpec((tm, D), lambda i, j: (i, 0)), pl.BlockSpec((D, hw), lambda i, j: (0, j)),
                  pl.BlockSpec((tm, hw), lambda i, j: (i, MR // hw + j)),
                  pl.BlockSpec((tm, hw), lambda i, j: (i, MA // hw + j))],
        out_specs=[half, half, half],
        out_shape=[_sds((t_rows, D), F32), _sds((t_rows, D), F32), _sds((t_rows, D), BF16)],
        compiler_params=_params(("parallel", "parallel")),
    )(yret16, w_o_ret16, yatt16, w_o_att16, px, px)


def _outproj(y16, w_out16, x_all, tgt, mod3, nb, seq, tm):
    t_rows = nb * seq
    bpb = seq // tm

    def body(y_ref, w_ref, x_ref, t_ref, g_ref, dxn_ref, dout_ref, dg_ref, loss_ref):
        i = pl.program_id(1)
        out = _dot(y_ref[...], w_ref[...])
        gate = g_ref[...]
        diff = x_ref[...] + gate * out - t_ref[...]
        dxn = diff * (1.0 / D)
        dxn_ref[...] = dxn
        dout_ref[...] = (gate * dxn).astype(BF16)
        dg = jnp.sum(dxn * out, axis=0, keepdims=True)
        ls = jnp.broadcast_to(jnp.sum(diff * diff) * (0.5 / D), (1, 128))

        @pl.when(i == 0)
        def _():
            dg_ref[...] = dg
            loss_ref[...] = ls

        @pl.when(i > 0)
        def _():
            dg_ref[...] += dg
            loss_ref[...] += ls

    row = pl.BlockSpec((tm, D), lambda b, i: (b * bpb + i, 0))
    return pl.pallas_call(
        body, name="outproj", grid=(nb, bpb),
        in_specs=[row, pl.BlockSpec((D, D), lambda b, i: (0, 0)), row, row,
                  pl.BlockSpec((None, 1, D), lambda b, i: (b, 0, 2))],
        out_specs=[row, row, pl.BlockSpec((None, 1, D), lambda b, i: (b, 0, 0)),
                   pl.BlockSpec((None, 1, 128), lambda b, i: (b, 0, 0))],
        out_shape=[_sds((t_rows, D), F32), _sds((t_rows, D), BF16), _sds((nb, 1, D), F32), _sds((nb, 1, 128), F32)],
        compiler_params=_params(("parallel", "arbitrary")),
    )(y16, w_out16, x_all, tgt, mod3)


def _bwd_merge(dout16, w_out16, px, a_ret, a_att, tm):
    t_rows = dout16.shape[0]
    hw = D // 2

    def body(do_ref, w_ref, mr_ref, ma_ref, ar_ref, aa_ref, dar_ref, daa_ref, dmr_ref, dma_ref):
        dy = _dot(do_ref[...], w_ref[...], NT)
        sr = _sig(mr_ref[...])
        sa = _sig(ma_ref[...])
        dar_ref[...] = (dy * sr).astype(BF16)
        daa_ref[...] = (dy * sa).astype(BF16)
        dmr_ref[...] = (dy * ar_ref[...] * sr * (1.0 - sr)).astype(BF16)
        dma_ref[...] = (dy * aa_ref[...] * sa * (1.0 - sa)).astype(BF16)

    half = pl.BlockSpec((tm, hw), lambda i, j: (i, j))
    return pl.pallas_call(
        body, name="bwd_merge", grid=(t_rows // tm, 2),
        in_specs=[pl.BlockSpec((tm, D), lambda i, j: (i, 0)), pl.BlockSpec((hw, D), lambda i, j: (j, 0)),
                  pl.BlockSpec((tm, hw), lambda i, j: (i, MR // hw + j)),
                  pl.BlockSpec((tm, hw), lambda i, j: (i, MA // hw + j)), half, half],
        out_specs=[half] * 4, out_shape=[_sds((t_rows, D), BF16)] * 4,
        compiler_params=_params(("parallel", "parallel")),
    )(dout16, w_out16, px, px, a_ret, a_att)


def _bwd_branch_ret(da_ret16, w_o_ret16, px, o_f, o_b, tm):
    t_rows = da_ret16.shape[0]

    def body(da_ref, w_ref, g_ref, of_ref, ob_ref, do_ref, dg_ref):
        dy = _dot(da_ref[...], w_ref[...], NT)
        g = g_ref[...]
        o = of_ref[...] + ob_ref[...]
        r = lax.rsqrt(jnp.mean(o * o, axis=-1, keepdims=True) + EPS)
        on = o * r
        don = dy * _silu(g)
        dg_ref[...] = (dy * on * _dsilu(g)).astype(BF16)
        do_ref[...] = (r * (don - on * jnp.mean(on * don, axis=-1, keepdims=True))).astype(BF16)

    blk = pl.BlockSpec((tm, DV), lambda i, h: (i, h))
    return pl.pallas_call(
        body, name="bwd_branch_ret", grid=(t_rows // tm, RH),
        in_specs=[pl.BlockSpec((tm, D), lambda i, h: (i, 0)), pl.BlockSpec((DV, D), lambda i, h: (h, 0)),
                  pl.BlockSpec((tm, DV), lambda i, h: (i, RG // DV + h)), blk, blk],
        out_specs=[blk, blk], out_shape=[_sds((t_rows, RH * DV), BF16)] * 2,
        compiler_params=_params(("parallel", "parallel")),
    )(da_ret16, w_o_ret16, px, o_f, o_b)


def _bwd_branch_att(da_att16, w_o_att16, px, o_att, tm):
    t_rows = da_att16.shape[0]
    hw = D // 2

    def body(da_ref, w_ref, g_ref, o_ref, dao_ref, dg_ref):
        dy = _dot(da_ref[...], w_ref[...], NT)
        g = g_ref[...]
        dao_ref[...] = dy * _silu(g)
        dg_ref[...] = (dy * o_ref[...] * _dsilu(g)).astype(BF16)

    half = pl.BlockSpec((tm, hw), lambda i, j: (i, j))
    return pl.pallas_call(
        body, name="bwd_branch_att", grid=(t_rows // tm, 2),
        in_specs=[pl.BlockSpec((tm, D), lambda i, j: (i, 0)), pl.BlockSpec((hw, D), lambda i, j: (j, 0)),
                  pl.BlockSpec((tm, hw), lambda i, j: (i, AG // hw + j)), half],
        out_specs=[half, half], out_shape=[_sds((t_rows, D), F32), _sds((t_rows, D), BF16)],
        compiler_params=_params(("parallel", "parallel")),
    )(da_att16, w_o_att16, px, o_att)


def _att_bwd(q16, kx16, kc16, px, dao, o_att, nb, seq, cx, tq):
    t_rows = nb * seq
    nq = seq // tq
    rep = HQ // HKV
    gw = rep * HD
    scale = HD ** -0.5

    def body(q_ref, kx_ref, kc_ref, vx_ref, vc_ref, dao_ref, o_ref, dq_ref, dkx_ref, dvx_ref, dkc_ref, dvc_ref):
        i = pl.program_id(2)
        kx = kx_ref[...]
        kc = kc_ref[...]
        vx = vx_ref[...].astype(BF16)
        vc = vc_ref[...].astype(BF16)
        dkx = jnp.zeros((seq, HD), F32)
        dvx = jnp.zeros((seq, HD), F32)
        dkc = jnp.zeros((cx, HD), F32)
        dvc = jnp.zeros((cx, HD), F32)
        for r in range(rep):
            sl = slice(r * HD, (r + 1) * HD)
            q = q_ref[:, sl]
            p1, p2 = _softmax_parts(q, kx, kc)
            da = dao_ref[:, sl]
            da16 = da.astype(BF16)
            delta = jnp.sum(da * o_ref[:, sl], axis=-1, keepdims=True)
            ds1 = (p1 * (_dot(da16, vx, NT) - delta) * scale).astype(BF16)
            ds2 = (p2 * (_dot(da16, vc, NT) - delta) * scale).astype(BF16)
            dq_ref[:, sl] = _dot(ds1, kx) + _dot(ds2, kc)
            dkx += _dot(ds1, q, TN)
            dkc += _dot(ds2, q, TN)
            dvx += _dot(p1.astype(BF16), da16, TN)
            dvc += _dot(p2.astype(BF16), da16, TN)

        @pl.when(i == 0)
        def _():
            dkx_ref[...] = dkx
            dvx_ref[...] = dvx
            dkc_ref[...] = dkc
            dvc_ref[...] = dvc

        @pl.when(i > 0)
        def _():
            dkx_ref[...] += dkx
            dvx_ref[...] += dvx
            dkc_ref[...] += dkc
            dvc_ref[...] += dvc

    qblk = pl.BlockSpec((tq, gw), lambda b, g, i: (b * nq + i, g))
    kxb = pl.BlockSpec((None, seq, HD), lambda b, g, i: (b, 0, g))
    kcb = pl.BlockSpec((None, cx, HD), lambda b, g, i: (b, 0, g))
    return pl.pallas_call(
        body, name="att_bwd", grid=(nb, HKV, nq),
        in_specs=[qblk,
                  pl.BlockSpec((seq, HD), lambda b, g, i: (b, g)),
                  pl.BlockSpec((cx, HD), lambda b, g, i: (b, g)),
                  pl.BlockSpec((seq, HD), lambda b, g, i: (b, AV // HD + g)),
                  pl.BlockSpec((cx, HD), lambda b, g, i: (t_rows // cx + b, AV // HD + g)),
                  qblk, qblk],
        out_specs=[qblk, kxb, kxb, kcb, kcb],
        out_shape=[_sds((t_rows, D), F32), _sds((nb, seq, HKV * HD), F32), _sds((nb, seq, HKV * HD), F32),
                   _sds((nb, cx, HKV * HD), F32), _sds((nb, cx, HKV * HD), F32)],
        compiler_params=_params(("parallel", "parallel", "arbitrary")),
    )(q16, kx16, kc16, px, px, dao, o_att)


def _qk_prep_bwd(dt, px, nw, cos, sin, rows, row_off, col_off, heads, seq, tm, name):
    rope = cos is not None
    rb0 = row_off // tm
    pb = seq // tm if rope else 1

    def body(*refs):
        if rope:
            d_ref, x_ref, w_ref, c_ref, s_ref, dx_ref, dw_ref = refs
        else:
            d_ref, x_ref, w_ref, dx_ref, dw_ref = refs
        first = jnp.logical_and(pl.program_id(0) == 0, pl.program_id(1) == 0)
        dtv = d_ref[...]
        if rope:
            dtv = dtv * c_ref[...] + _swap_pairs(dtv * s_ref[...])
        xv = x_ref[...]
        r = lax.rsqrt(jnp.mean(xv * xv, axis=-1, keepdims=True) + EPS)
        xh = xv * r
        dxh = dtv * w_ref[...]
        dx_ref[...] = (r * (dxh - xh * jnp.mean(dxh * xh, axis=-1, keepdims=True))).astype(BF16)
        dw = jnp.sum(dtv * xh, axis=0, keepdims=True)

        @pl.when(first)
        def _():
            dw_ref[...] = dw

        @pl.when(jnp.logical_not(first))
        def _():
            dw_ref[...] += dw

    blk = pl.BlockSpec((tm, HD), lambda i, h: (i, h))
    in_specs = [blk, pl.BlockSpec((tm, HD), lambda i, h: (rb0 + i, col_off // HD + h)),
                pl.BlockSpec((1, HD), lambda i, h: (0, 0))]
    args = [dt, px, nw]
    if rope:
        in_specs += [pl.BlockSpec((tm, HD), lambda i, h: (i % pb, 0))] * 2
        args += [cos, sin]
    return pl.pallas_call(
        body, name=name, grid=(rows // tm, heads), in_specs=in_specs,
        out_specs=[blk, pl.BlockSpec((1, HD), lambda i, h: (0, 0))],
        out_shape=[_sds((rows, heads * HD), BF16), _sds((1, HD), F32)],
        compiler_params=_params(("arbitrary", "arbitrary")),
    )(*args)


def _ret_bwd(px, lg, do16, hist_f, hist_b, nb, nc):
    t_rows = nb * nc * CH

    def body(lg_ref, qf_ref, kf_ref, vf_ref, dof_ref, hf_ref, qb_ref, kb_ref, vb_ref, dob_ref, hb_ref,
             dqf_ref, dkf_ref, dvf_ref, dqb_ref, dkb_ref, dvb_ref, dsf_ref, dsb_ref, dlg_ref, dsf, dsb):
        h = pl.program_id(1)
        c = pl.program_id(2)

        @pl.when(c == 0)
        def _():
            dsf[...] = jnp.zeros_like(dsf)
            dsb[...] = jnp.zeros_like(dsb)
            dlg_ref[...] = jnp.zeros_like(dlg_ref)

        for d, (q_ref, k_ref, v_ref, do_ref, h_ref, dq_ref, dk_ref, dv_ref, ds, ds_out) in enumerate((
                (qf_ref, kf_ref, vf_ref, dof_ref, hf_ref, dqf_ref, dkf_ref, dvf_ref, dsf, dsf_ref),
                (qb_ref, kb_ref, vb_ref, dob_ref, hb_ref, dqb_ref, dkb_ref, dvb_ref, dsb, dsb_ref))):
            lg_d = lg_ref[d, h]
            mask, relf, qd, qe, kd, ke = _decays(lg_d, d == 0)
            g_ch = jnp.exp(lg_d * CH)
            q = q_ref[...]
            k = k_ref[...] * (DK ** -0.5)
            q16 = q.astype(BF16)
            k16 = k.astype(BF16)
            v16 = v_ref[...].astype(BF16)
            do16v = do_ref[...]
            st16 = h_ref[...]
            dst = ds[...]
            dst16 = dst.astype(BF16)
            a = _dot(q16, k16, NT) * mask
            dp = _dot(do16v, v16, NT)
            da16 = (dp * mask).astype(BF16)
            dq_cross = _dot(do16v, st16, NT) * qd
            dq_ref[...] = _dot(da16, k16) + dq_cross
            dk_state = _dot(v16, dst16, NT) * kd
            dk_ref[...] = (_dot(da16, q16, TN) + dk_state) * (DK ** -0.5)
            dv_ref[...] = _dot(a.astype(BF16), do16v, TN) + _dot((k * kd).astype(BF16), dst16)
            dlg = (jnp.sum(relf * a * dp)
                   + jnp.sum(qe * jnp.sum(q * dq_cross, axis=-1, keepdims=True))
                   + jnp.sum(ke * jnp.sum(k * dk_state, axis=-1, keepdims=True))
                   + CH * g_ch * jnp.sum(dst * st16.astype(F32)))
            dlg_ref[d:d + 1, :] += jnp.broadcast_to(dlg, (1, 128))
            ds_new = g_ch * dst + _dot((q * qd).astype(BF16), do16v, TN)
            ds[...] = ds_new

            @pl.when(c == nc - 1)
            def _():
                ds_out[...] = ds_new

    def fw(b, h, c):
        return b * nc + nc - 1 - c

    def bw(b, h, c):
        return b * nc + c

    st = pl.BlockSpec((None, None, DK, DV), lambda b, h, c: (b, h, 0, 0))
    in_specs = [pl.BlockSpec(memory_space=pltpu.SMEM)]
    out_specs = []
    def hist_spec(rowf):
        return pl.BlockSpec((None, None, None, DK, DV), lambda b, h, c: (b, h, rowf(0, h, c), 0, 0))

    for rowf in (fw, bw):
        in_specs += _qkv_specs(rowf) + [_head_spec(rowf, DV, 0), hist_spec(rowf)]
        out_specs += [_head_spec(rowf, DK, 0), _head_spec(rowf, DK, 0), _head_spec(rowf, DV, 0)]
    out_specs += [st, st, pl.BlockSpec((None, None, 8, 128), lambda b, h, c: (b, h, 0, 0))]
    qk = _sds((t_rows, RH * DK), F32)
    vv = _sds((t_rows, RH * DV), F32)
    return pl.pallas_call(
        body, name="ret_bwd", grid=(nb, RH, nc), in_specs=in_specs, out_specs=out_specs,
        out_shape=[qk, qk, vv, qk, qk, vv, _sds((nb, RH, DK, DV), F32), _sds((nb, RH, DK, DV), F32),
                   _sds((nb, RH, 8, 128), F32)],
        scratch_shapes=[pltpu.VMEM((DK, DV), F32), pltpu.VMEM((DK, DV), F32)],
        compiler_params=_params(("parallel", "parallel", "arbitrary")),
    )(lg, px, px, px, do16, hist_f, px, px, px, do16, hist_b)


def _ctx_state_bwd(px, lg, ds_f, ds_b, nb, t_rows, cx):
    rb = t_rows // cx

    def body(lg_ref, k_ref, v_ref, dsf_ref, dsb_ref, dk_ref, dv_ref, dlg_ref):
        h = pl.program_id(1)
        pos = lax.broadcasted_iota(jnp.int32, (cx, 1), 0).astype(F32)
        k = k_ref[...] * (DK ** -0.5)
        v16 = v_ref[...].astype(BF16)
        dk = jnp.zeros((cx, DK), F32)
        dv = jnp.zeros((cx, DV), F32)
        dlg_ref[...] = jnp.zeros_like(dlg_ref)
        for d, (ds_ref, e) in enumerate(((dsf_ref, cx - 1.0 - pos), (dsb_ref, pos))):
            w = jnp.exp(lg_ref[d, h] * e)
            ds16 = ds_ref[...].astype(BF16)
            t = _dot(v16, ds16, NT)
            dk += t * w
            dv += _dot((k * w).astype(BF16), ds16)
            dlg = jnp.sum(e * w * jnp.sum(k * t, axis=-1, keepdims=True))
            dlg_ref[d:d + 1, :] = jnp.broadcast_to(dlg, (1, 128))
        dk_ref[...] = (dk * (DK ** -0.5)).astype(BF16)
        dv_ref[...] = dv.astype(BF16)

    st = pl.BlockSpec((None, None, DK, DV), lambda b, h: (b, h, 0, 0))
    return pl.pallas_call(
        body, name="ctx_state_bwd", grid=(nb, RH),
        in_specs=[pl.BlockSpec(memory_space=pltpu.SMEM),
                  pl.BlockSpec((cx, DK), lambda b, h: (rb + b, RK // DK + h)),
                  pl.BlockSpec((cx, DV), lambda b, h: (rb + b, RV // DV + h)), st, st],
        out_specs=[pl.BlockSpec((cx, DK), lambda b, h: (b, h)), pl.BlockSpec((cx, DV), lambda b, h: (b, h)),
                   pl.BlockSpec((None, None, 8, 128), lambda b, h: (b, h, 0, 0))],
        out_shape=[_sds((nb * cx, RH * DK), BF16), _sds((nb * cx, RH * DV), BF16), _sds((nb, RH, 8, 128), F32)],
        compiler_params=_params(("parallel", "parallel")),
    )(lg, px, px, ds_f, ds_b)


def _assemble_lat(rows_all, dk_f, dk_b, dv_f, dv_b, dak16, dvx, dq_f, dq_b, drg16, daq16, dag16, dmr16, dma16, tm):
    t_rows = dk_f.shape[0]

    def body(dkf, dkb, dvf, dvb, dak, dav, dqf, dqb, drg, daq, dag, dmr, dma, o_ref):
        o_ref[:, RK:RK + RH * DK] = (dkf[...] + dkb[...]).astype(BF16)
        o_ref[:, RV:RV + RH * DV] = (dvf[...] + dvb[...]).astype(BF16)
        o_ref[:, AK:AK + HKV * HD] = dak[...]
        o_ref[:, AV:AV + HKV * HD] = dav[...].astype(BF16)
        o_ref[:, RQ:RQ + RH * DK] = (dqf[...] + dqb[...]).astype(BF16)
        o_ref[:, RG:RG + RH * DV] = drg[...]
        o_ref[:, AQ:AQ + D] = daq[...]
        o_ref[:, AG:AG + D] = dag[...]
        o_ref[:, MR:MR + D] = dmr[...]
        o_ref[:, MA:MA + D] = dma[...]

    args = (dk_f, dk_b, dv_f, dv_b, dak16, dvx, dq_f, dq_b, drg16, daq16, dag16, dmr16, dma16)
    return pl.pallas_call(
        body, name="assemble_lat", grid=(t_rows // tm,),
        in_specs=[pl.BlockSpec((tm, a.shape[1]), lambda i: (i, 0)) for a in args],
        out_specs=pl.BlockSpec((tm, IN_COLS), lambda i: (i, 0)), out_shape=_sds((rows_all, IN_COLS), BF16),
        compiler_params=_params(("parallel",)),
    )(*args)


def _assemble_ctx(dp_all, dck16, dcv16, dcak16, dvc, t_rows, tm):
    c_rows = dck16.shape[0]
    rb = t_rows // tm

    def body(_, dck, dcv, dcak, dcav, o_ref):
        o_ref[:, RK:RK + RH * DK] = dck[...]
        o_ref[:, RV:RV + RH * DV] = dcv[...]
        o_ref[:, AK:AK + HKV * HD] = dcak[...]
        o_ref[:, AV:AV + HKV * HD] = dcav[...].astype(BF16)
        o_ref[:, KV_COLS:] = jnp.zeros((tm, IN_COLS - KV_COLS), BF16)

    args = (dck16, dcv16, dcak16, dvc)
    return pl.pallas_call(
        body, name="assemble_ctx", grid=(c_rows // tm,),
        in_specs=[pl.BlockSpec(memory_space=pl.ANY)]
        + [pl.BlockSpec((tm, a.shape[1]), lambda i: (i, 0)) for a in args],
        out_specs=pl.BlockSpec((tm, IN_COLS), lambda i: (rb + i, 0)), out_shape=_sds(dp_all.shape, BF16),
        input_output_aliases={0: 0},
        compiler_params=_params(("parallel",)),
    )(dp_all, *args)


def _norm_bwd(dh, x_all, mod3, norm_w, dxn, row_off, rows, rows_per_group, group0, tm, name):
    with_dx = dxn is not None
    rb0 = row_off // tm
    bpg = rows_per_group // tm
    ngroups = rows // rows_per_group

    def body(*refs):
        if with_dx:
            dh_ref, x_ref, sc_ref, nw_ref, dxn_ref, dx_ref, dsh_ref, dsc_ref, dnw_ref = refs
        else:
            dh_ref, x_ref, sc_ref, nw_ref, dsh_ref, dsc_ref, dnw_ref = refs
        i = pl.program_id(0)
        dhv = dh_ref[...]
        xv = x_ref[...]
        nw = nw_ref[...]
        r = lax.rsqrt(jnp.mean(xv * xv, axis=-1, keepdims=True) + EPS)
        xh = xv * r
        dm = dhv * (1.0 + sc_ref[...])
        dsh = jnp.sum(dhv, axis=0, keepdims=True)
        dsc = jnp.sum(dhv * (xh * nw), axis=0, keepdims=True)
        dnw = jnp.sum(dm * xh, axis=0, keepdims=True)
        if with_dx:
            dxh = dm * nw
            dx_ref[...] = dxn_ref[...] + r * (dxh - xh * jnp.mean(dxh * xh, axis=-1, keepdims=True))

        @pl.when(i % bpg == 0)
        def _():
            dsh_ref[...] = dsh
            dsc_ref[...] = dsc

        @pl.when(i % bpg != 0)
        def _():
            dsh_ref[...] += dsh
            dsc_ref[...] += dsc

        @pl.when(i == 0)
        def _():
            dnw_ref[...] = dnw

        @pl.when(i > 0)
        def _():
            dnw_ref[...] += dnw

    off = pl.BlockSpec((tm, D), lambda i: (rb0 + i, 0))
    grp = pl.BlockSpec((None, 1, D), lambda i: (i // bpg, 0, 0))
    in_specs = [off, off, pl.BlockSpec((None, 1, D), lambda i: (group0 + i // bpg, 0, 1)),
                pl.BlockSpec((1, D), lambda i: (0, 0))]
    args = [dh, x_all, mod3, norm_w]
    out_specs = [grp, grp, pl.BlockSpec((1, D), lambda i: (0, 0))]
    out_shape = [_sds((ngroups, 1, D), F32), _sds((ngroups, 1, D), F32), _sds((1, D), F32)]
    if with_dx:
        in_specs.append(pl.BlockSpec((tm, D), lambda i: (i, 0)))
        args.append(dxn)
        out_specs.insert(0, pl.BlockSpec((tm, D), lambda i: (i, 0)))
        out_shape.insert(0, _sds((rows, D), F32))
    return pl.pallas_call(
        body, name=name, grid=(rows // tm,), in_specs=in_specs, out_specs=out_specs, out_shape=out_shape,
        compiler_params=_params(("arbitrary",)),
    )(*args)


def _small_final(dmod_all, dmodc_parts, c_rows, dm_loc_rows, nw_parts, misc_parts, c_ctx, r_pad, w_ada16):
    loc = dm_loc_rows.shape[1]

    def body(dm_ref, dmc_ref, c_ref, dml_ref, nwp_ref, mp_ref, cc_ref, r_ref, w_ref,
             gb_ref, gc_ref, gnw_ref, misc_ref, gwa_ref):
        dmc = jnp.sum(dmc_ref[...], axis=0, keepdims=True)
        gb_ref[...] = jnp.sum(dm_ref[...], axis=0, keepdims=True) + dmc
        dsc = _dot(jnp.broadcast_to(dmc, (8, 3 * D)).astype(BF16), w_ref[...], NT)[0:1, :]
        gc_ref[...] = dsc * _dsilu(cc_ref[...])
        gnw_ref[...] = jnp.sum(nwp_ref[...], axis=0, keepdims=True)
        misc = jnp.sum(mp_ref[...], axis=0, keepdims=True)
        y = jnp.exp2(r_ref[...])
        lane = lax.broadcasted_iota(jnp.int32, (1, D), 1)
        is_decay = jnp.logical_and(lane >= 2 * HD, lane < 2 * HD + 2 * RH)
        misc_ref[...] = misc * jnp.where(is_decay, -(y * np.float32(np.log(2.0))) / (1.0 - y), 1.0)
        gwa_ref[...] = _dot(_silu(c_ref[...]).astype(BF16), dml_ref[...].astype(BF16), TN)

    return pl.pallas_call(
        body, name="small_final",
        out_shape=[_sds((1, 3 * D), F32), _sds((1, D), F32), _sds((1, D), F32), _sds((1, D), F32), _sds((D, loc), F32)],
        compiler_params=pltpu.CompilerParams(vmem_limit_bytes=VMEM_LIMIT),
    )(dmod_all, dmodc_parts, c_rows, dm_loc_rows, nw_parts, misc_parts, c_ctx, r_pad, w_ada16)


def _adamw(w, g, m, v, name):
    rows, cols = w.shape
    tm = _pick(rows, 256, 8)
    bc1 = 1.0 - B1 ** STEP
    bc2 = 1.0 - B2 ** STEP

    def body(w_ref, g_ref, m_ref, v_ref, d_ref, nm_ref, nv_ref):
        g_ = g_ref[...]
        nm = B1 * m_ref[...] + (1.0 - B1) * g_
        nv = B2 * v_ref[...] + (1.0 - B2) * (g_ * g_)
        nm_ref[...] = nm
        nv_ref[...] = nv
        d_ref[...] = -LR * ((nm / bc1) / (jnp.sqrt(nv / bc2) + ADAM_EPS) + WD * w_ref[...])

    blk = pl.BlockSpec((tm, cols), lambda i: (i, 0))
    return pl.pallas_call(
        body, name=name, grid=(rows // tm,), in_specs=[blk] * 4, out_specs=[blk] * 3,
        out_shape=[_sds((rows, cols), F32)] * 3, compiler_params=_params(("parallel",)),
    )(w, g, m, v)


def _mesh_pos():
    return lax.axis_index("x"), lax.axis_index("y"), lax.axis_index("c")


def _all_gather(arrs, name):
    n = len(arrs)

    def body(*refs):
        ins, outs = refs[:n], refs[n:2 * n]
        send_sems, recv_sems, local_sems = refs[2 * n:]
        x, y, c = _mesh_pos()
        me, sib = (x, y, c), (x, y, 1 - c)
        chips = [(1 - x, y), (x, 1 - y), (1 - x, 1 - y)]

        def slot(p):
            return 4 * p[0] + 2 * p[1] + p[2]

        def copy(a, k, block, to, own):
            dst = outs[a].at[slot(block)]
            return pltpu.make_async_remote_copy(
                src_ref=ins[a] if own else dst, dst_ref=dst, send_sem=send_sems.at[a, k], recv_sem=recv_sems.at[a, k],
                device_id=to, device_id_type=MESH_T)

        mine = [pltpu.make_async_copy(ins[a], outs[a].at[slot(me)], local_sems.at[a]) for a in range(n)]
        for cp in mine:
            cp.start()
        first = []
        for a in range(n):
            first.append(copy(a, 0, me, sib, True))
            first += [copy(a, 1 + j, me, (*chip, c), True) for j, chip in enumerate(chips)]
        for cp in first:
            cp.start()
        passed = []
        for j, chip in enumerate(chips):
            for a in range(n):
                copy(a, 1 + j, (*chip, c), me, False).wait_recv()
                fwd = copy(a, 4 + j, (*chip, c), sib, False)
                fwd.start()
                passed.append(fwd)
        for a in range(n):
            copy(a, 0, sib, me, False).wait_recv()
            for j, chip in enumerate(chips):
                copy(a, 4 + j, (*chip, 1 - c), me, False).wait_recv()
        for cp in first + passed:
            cp.wait_send()
        for cp in mine:
            cp.wait()

    hbm = pl.BlockSpec(memory_space=pl.ANY)
    return pl.pallas_call(
        body, name=name, in_specs=[hbm] * n, out_specs=[hbm] * n,
        out_shape=[_sds((N_DEV,) + a.shape, a.dtype) for a in arrs],
        scratch_shapes=[pltpu.SemaphoreType.DMA((n, 7)), pltpu.SemaphoreType.DMA((n, 7)), pltpu.SemaphoreType.DMA((n,))],
    )(*arrs)


def _scatter_blocks(arrs, name):
    n = len(arrs)

    def body(*refs):
        ins, outs = refs[:n], refs[n:2 * n]
        send_sems, recv_sems, local_sems = refs[2 * n:]
        x, y, c = _mesh_pos()
        me = 4 * x + 2 * y + c
        peers = []
        for k in range(1, N_DEV):
            fx, fy, fc = (k >> 2) & 1, (k >> 1) & 1, k & 1
            peers.append((x ^ fx, y ^ fy, c ^ fc))

        def slot(p):
            return 4 * p[0] + 2 * p[1] + p[2]

        mine = [pltpu.make_async_copy(ins[a].at[me], outs[a].at[me], local_sems.at[a]) for a in range(n)]
        for cp in mine:
            cp.start()
        sends = []
        for a in range(n):
            for k, p in enumerate(peers):
                sends.append(pltpu.make_async_remote_copy(
                    src_ref=ins[a].at[slot(p)], dst_ref=outs[a].at[me], send_sem=send_sems.at[a, k],
                    recv_sem=recv_sems.at[a, k], device_id=p, device_id_type=MESH_T))
        for cp in sends:
            cp.start()
        for a in range(n):
            for k, p in enumerate(peers):
                pltpu.make_async_remote_copy(
                    src_ref=ins[a].at[me], dst_ref=outs[a].at[slot(p)], send_sem=send_sems.at[a, k],
                    recv_sem=recv_sems.at[a, k], device_id=p, device_id_type=MESH_T).wait_recv()
        for cp in sends:
            cp.wait_send()
        for cp in mine:
            cp.wait()

    hbm = pl.BlockSpec(memory_space=pl.ANY)
    return pl.pallas_call(
        body, name=name, in_specs=[hbm] * n, out_specs=[hbm] * n,
        out_shape=[_sds(a.shape, a.dtype) for a in arrs],
        scratch_shapes=[pltpu.SemaphoreType.DMA((n, 7)), pltpu.SemaphoreType.DMA((n, 7)), pltpu.SemaphoreType.DMA((n,))],
    )(*arrs)


def _sum_slots(land, name):
    _, rows, cols = land.shape
    tm = _pick(rows, 256, 16)

    def body(l_ref, o_ref):
        acc = l_ref[0].astype(F32)
        for p in range(1, N_DEV):
            acc = acc + l_ref[p].astype(F32)
        o_ref[...] = acc

    return pl.pallas_call(
        body, name=name, grid=(rows // tm,),
        in_specs=[pl.BlockSpec((N_DEV, tm, cols), lambda i: (0, i, 0))],
        out_specs=pl.BlockSpec((tm, cols), lambda i: (i, 0)), out_shape=_sds((rows, cols), F32),
        compiler_params=_params(("parallel",)),
    )(land)


def _local_step(x, c, ctx, c_ctx, norm_w, b_ada, ret_log2_decay, q_norm_w, k_norm_w, loss_target,
                w_ada16, w_in16, w_o_ret16, w_o_att16, w_out16):
    nb, seq, _ = x.shape
    cx = ctx.shape[1]
    t_rows, c_rows = nb * seq, nb * cx
    rows_all = t_rows + c_rows
    nc = seq // CH
    tm = _pick(seq, 256, 128)
    assert cx % tm == 0 and t_rows % cx == 0 and seq % GRID_W == 0

    x_all = jnp.concatenate([x.reshape(t_rows, D), ctx.reshape(c_rows, D)], axis=0)
    tgt = loss_target.reshape(t_rows, D)
    c8 = jnp.zeros((8, D), F32).at[:nb].set(c).at[nb].set(c_ctx)
    lg = _log_gamma(ret_log2_decay)
    cos, sin = _rope_tables(seq)

    mod = _mod_fwd(c8, w_ada16, b_ada)
    mod3 = mod[:, None, :]
    h_all = _norm_fwd(x_all, mod3, norm_w, t_rows // tm, seq // tm, nb, tm)
    px = _matmul(h_all, w_in16, tm=1536, tn=1536, tk=D, out_dtype=F32, name="in_proj")
    s0f, s0b = _ctx_state(px, lg, nb, t_rows, cx)
    o_f, o_b, hist_f, hist_b = _ret_fwd(px, lg, s0f, s0b, nb, nc)
    yret16 = _ret_post(o_f, o_b, px, tm)
    q16 = _qk_prep(px, q_norm_w, cos, sin, t_rows, 0, AQ, HQ, seq, tm, "q_prep")
    kx16 = _qk_prep(px, k_norm_w, cos, sin, t_rows, 0, AK, HKV, seq, tm, "k_prep")
    kc16 = _qk_prep(px, k_norm_w, None, None, c_rows, t_rows, AK, HKV, seq, tm, "kc_prep")
    o_att, yatt16 = _att_fwd(q16, kx16, kc16, px, nb, seq, cx, tm)
    a_ret, a_att, y16 = _merge(yret16, yatt16, px, w_o_ret16, w_o_att16, tm)
    dxn, dout16, dgate, loss_b = _outproj(y16, w_out16, x_all, tgt, mod3, nb, seq, tm)
    loss = jnp.sum(loss_b[:, 0, 0])

    gw_out = _matmul(y16, dout16, ta=True, tm=D, tn=D, tk=D, out_dtype=F32, name="gw_out")
    da_ret16, da_att16, dmr16, dma16 = _bwd_merge(dout16, w_out16, px, a_ret, a_att, tm)
    gw_o_ret = _matmul(yret16, da_ret16, ta=True, tm=D, tn=D, tk=D, out_dtype=F32, name="gw_o_ret")
    gw_o_att = _matmul(yatt16, da_att16, ta=True, tm=D, tn=D, tk=D, out_dtype=F32, name="gw_o_att")
    do16, drg16 = _bwd_branch_ret(da_ret16, w_o_ret16, px, o_f, o_b, tm)
    dao, dag16 = _bwd_branch_att(da_att16, w_o_att16, px, o_att, tm)
    dq_rot, dkx, dvx, dkc, dvc = _att_bwd(q16, kx16, kc16, px, dao, o_att, nb, seq, cx, tm)
    daq16, gq = _qk_prep_bwd(dq_rot, px, q_norm_w, cos, sin, t_rows, 0, AQ, HQ, seq, tm, "q_prep_bwd")
    dak16, gk_lat = _qk_prep_bwd(dkx.reshape(t_rows, HKV * HD), px, k_norm_w, cos, sin, t_rows, 0, AK, HKV, seq, tm,
                                 "k_prep_bwd")
    dcak16, gk_ctx = _qk_prep_bwd(dkc.reshape(c_rows, HKV * HD), px, k_norm_w, None, None, c_rows, t_rows, AK, HKV, seq,
                                  tm, "kc_prep_bwd")
    dq_f, dk_f, dv_f, dq_b, dk_b, dv_b, ds_f, ds_b, dlg_scan = _ret_bwd(px, lg, do16, hist_f, hist_b, nb, nc)
    dck16, dcv16, dlg_ctx = _ctx_state_bwd(px, lg, ds_f, ds_b, nb, t_rows, cx)
    dp_all = _assemble_lat(rows_all, dk_f, dk_b, dv_f, dv_b, dak16, dvx.reshape(t_rows, HKV * HD), dq_f, dq_b, drg16,
                           daq16, dag16, dmr16, dma16, tm)
    dp_all = _assemble_ctx(dp_all, dck16, dcv16, dcak16, dvc.reshape(c_rows, HKV * HD), t_rows, tm)
    gw_in = _matmul(h_all, dp_all, ta=True, tm=D, tn=1536, tk=1536, out_dtype=F32, name="gw_in")
    dh = _matmul(dp_all, w_in16, tb=True, tm=1536, tn=D, tk=1536, out_dtype=F32, name="d_h")
    grad_x, dsh, dsc, gnw_lat = _norm_bwd(dh, x_all, mod3, norm_w, dxn, 0, t_rows, seq, 0, tm, "norm_bwd")
    dsh_c, dsc_c, gnw_ctx = _norm_bwd(dh, x_all, mod3, norm_w, None, t_rows, c_rows, c_rows, nb, tm, "norm_bwd_ctx")

    dlg = jnp.sum(dlg_scan[:, :, :2, 0] + dlg_ctx[:, :, :2, 0], axis=0).T.reshape(1, 2 * RH)
    misc = jnp.zeros((1, D), F32).at[:, :HD].set(gq).at[:, HD:2 * HD].set(gk_lat + gk_ctx).at[:, 2 * HD:2 * HD + 2 * RH].set(dlg)
    rows = []
    for b in range(nb):
        rows += [dsh[b], dsc[b], dgate[b]]
    rows += [dsh_c[0], dsc_c[0]] + [c[b:b + 1] for b in range(nb)] + [gnw_lat + gnw_ctx, misc]
    payload = jnp.concatenate(rows + [jnp.zeros((PAY_ROWS - len(rows), D), F32)], axis=0)
    return loss, grad_x.reshape(nb, seq, D), gw_in, gw_o_ret, gw_o_att, gw_out, payload


def _finish_small(gathered, nb, c_ctx, ret_log2_decay, w_ada16, dev):
    n_dev = gathered.shape[0]
    loc = 3 * D // n_dev
    dmod_all = gathered[:, :3 * nb].reshape(n_dev * nb, 3 * D)
    dmodc_parts = jnp.concatenate([gathered[:, 3 * nb:3 * nb + 2].reshape(n_dev, 2 * D), jnp.zeros((n_dev, D), F32)], axis=1)
    c_all = gathered[:, 3 * nb + 2:4 * nb + 2].reshape(n_dev * nb, D)
    nw_parts = gathered[:, 4 * nb + 2]
    misc_parts = gathered[:, 4 * nb + 3]
    n_rows = n_dev * nb + n_dev
    pad = (-n_rows) % 16
    c_rows = jnp.concatenate([c_all, jnp.broadcast_to(c_ctx.reshape(1, D), (n_dev, D)), jnp.zeros((pad, D), F32)], axis=0)
    dm_rows = jnp.concatenate([dmod_all, dmodc_parts, jnp.zeros((pad, 3 * D), F32)], axis=0)
    dm_loc_rows = lax.dynamic_slice_in_dim(dm_rows, dev * loc, loc, axis=1)
    r_pad = jnp.full((1, D), -1.0, F32).at[:, 2 * HD:2 * HD + 2 * RH].set(ret_log2_decay.reshape(1, 2 * RH))
    gb, gc, gnw, misc, gwa = _small_final(dmod_all, dmodc_parts, c_rows, dm_loc_rows, nw_parts, misc_parts,
                                          c_ctx.reshape(1, D), r_pad, w_ada16)
    return gb, gc, gnw, misc[:, :HD], misc[:, HD:2 * HD], misc[:, 2 * HD:2 * HD + 2 * RH], gwa


def kernel(x, c, ctx, c_ctx, norm_w, w_ada, b_ada, w_in, ret_log2_decay, q_norm_w, k_norm_w, w_o_ret, w_o_att, w_out, loss_target, m_c_ctx, m_norm_w, m_w_ada, m_b_ada, m_w_in, m_ret_log2_decay, m_q_norm_w, m_k_norm_w, m_w_o_ret, m_w_o_att, m_w_out, v_c_ctx, v_norm_w, v_w_ada, v_b_ada, v_w_in, v_ret_log2_decay, v_q_norm_w, v_k_norm_w, v_w_o_ret, v_w_o_att, v_w_out):
    nb = x.shape[0]
    dev = 4 * lax.axis_index("x") + 2 * lax.axis_index("y") + lax.axis_index("c")

    shards = [w_in[0].astype(BF16), w_ada[0].astype(BF16), w_o_ret[0].astype(BF16), w_o_att[0].astype(BF16),
              w_out[0].astype(BF16)]
    g_in, g_ada, g_ret, g_att, g_out = _all_gather(shards, "gather_weights")
    w_in16 = jnp.transpose(g_in, (1, 0, 2)).reshape(D, IN_COLS)
    w_ada16 = jnp.transpose(g_ada, (1, 0, 2)).reshape(D, 3 * D)
    w_o_ret16 = g_ret.reshape(RH * DV, D)
    w_o_att16 = g_att.reshape(D, D)
    w_out16 = g_out.reshape(D, D)

    loss, grad_x, gw_in, gw_o_ret, gw_o_att, gw_out, payload = _local_step(
        x, c, ctx, c_ctx, norm_w, b_ada, ret_log2_decay, q_norm_w, k_norm_w, loss_target,
        w_ada16, w_in16, w_o_ret16, w_o_att16, w_out16)
    loss = lax.psum(loss, ("x", "y", "c"))

    (gathered,) = _all_gather([payload], "gather_small")
    gb, gc, gnw, gq, gk, gr, gwa = _finish_small(gathered, nb, c_ctx, ret_log2_decay, w_ada16, dev)

    parts = [jnp.transpose(gw_in.reshape(D, N_DEV, IN_COLS // N_DEV), (1, 0, 2)).astype(BF16),
             gw_o_ret.reshape(N_DEV, RH * DV // N_DEV, D).astype(BF16),
             gw_o_att.reshape(N_DEV, D // N_DEV, D).astype(BF16),
             gw_out.reshape(N_DEV, D // N_DEV, D).astype(BF16)]
    landed = _scatter_blocks(parts, "scatter_grads")
    g_w_in, g_w_o_ret, g_w_o_att, g_w_out = [_sum_slots(l, "sum_grads_%d" % i) for i, l in enumerate(landed)]

    grads = [gc.reshape(c_ctx.shape), gnw, gwa[None], gb, g_w_in[None], gr.reshape(ret_log2_decay.shape), gq, gk,
             g_w_o_ret[None], g_w_o_att[None], g_w_out[None]]
    weights = [c_ctx, norm_w, w_ada, b_ada, w_in, ret_log2_decay, q_norm_w, k_norm_w, w_o_ret, w_o_att, w_out]
    ms = [m_c_ctx, m_norm_w, m_w_ada, m_b_ada, m_w_in, m_ret_log2_decay, m_q_norm_w, m_k_norm_w, m_w_o_ret, m_w_o_att, m_w_out]
    vs = [v_c_ctx, v_norm_w, v_w_ada, v_b_ada, v_w_in, v_ret_log2_decay, v_q_norm_w, v_k_norm_w, v_w_o_ret, v_w_o_att, v_w_out]
    deltas, new_ms, new_vs = [], [], []
    for i, (w, g, m, v) in enumerate(zip(weights, grads, ms, vs)):
        shape2 = (-1, w.shape[-1])
        dl, nm, nv = _adamw(w.reshape(shape2), g.reshape(shape2), m.reshape(shape2), v.reshape(shape2), "adamw_%d" % i)
        deltas.append(dl.reshape(w.shape))
        new_ms.append(nm.reshape(w.shape))
        new_vs.append(nv.reshape(w.shape))
    return (loss, grad_x, *grads, *deltas, *new_ms, *new_vs)
```

```python
import numpy as np
import jax
import jax.numpy as jnp
from jax import lax
from jax.experimental import pallas as pl
from jax.experimental.pallas import tpu as pltpu

F32 = jnp.float32
BF16 = jnp.bfloat16

D = 1024
RH, DK, DV, CH = 4, 256, 512, 128
HQ, HKV, HD = 8, 2, 128
GRID_W = 64
ROPE_THETA = 10000.0
EPS = 1e-6
RK, RV, AK, AV, RQ, RG, AQ, AG, MR, MA = 0, 1024, 3072, 3328, 3584, 4608, 6656, 7680, 8704, 9728
IN_COLS = 10752
KV_COLS = 3584
N_DEV = 8
LR, B1, B2, ADAM_EPS, WD, STEP = 0.001, 0.9, 0.999, 1e-08, 0.01, 10
PAY_ROWS = 16
VMEM_LIMIT = 56 * 1024 * 1024
MESH_T = pl.DeviceIdType.MESH

NT = (((1,), (1,)), ((), ()))
TN = (((0,), (0,)), ((), ()))
SM_C = (HD ** -0.5) * float(np.log2(np.e))


def _params(sem):
    return pltpu.CompilerParams(dimension_semantics=sem, vmem_limit_bytes=VMEM_LIMIT)


def _pick(n, target, mult=8):
    best = None
    for t in range(mult, min(n, target) + 1, mult):
        if n % t == 0:
            best = t
    return best or n


def _dot(a, b, dn=None):
    if dn is None:
        return jnp.dot(a, b, preferred_element_type=F32)
    return lax.dot_general(a, b, dn, preferred_element_type=F32)


def _sig(v):
    return jax.nn.sigmoid(v)


def _silu(v):
    return v * _sig(v)


def _dsilu(v):
    s = _sig(v)
    return s * (1.0 + v * (1.0 - s))


def _sds(shape, dtype):
    return jax.ShapeDtypeStruct(shape, dtype)


def _matmul(a, b, *, ta=False, tb=False, tm, tn, tk, out_dtype, name):
    m = a.shape[1] if ta else a.shape[0]
    kdim = a.shape[0] if ta else a.shape[1]
    n = b.shape[0] if tb else b.shape[1]
    tm, tn, tk = _pick(m, tm, 128), _pick(n, tn, 128), _pick(kdim, tk, 128)
    nk = kdim // tk
    dn = (((0 if ta else 1,), (1 if tb else 0,)), ((), ()))

    def body(a_ref, b_ref, o_ref, acc_ref):
        k = pl.program_id(2)
        part = _dot(a_ref[...].astype(BF16), b_ref[...].astype(BF16), dn)
        if nk == 1:
            o_ref[...] = part.astype(o_ref.dtype)
        else:
            @pl.when(k == 0)
            def _():
                acc_ref[...] = part

            @pl.when(k > 0)
            def _():
                acc_ref[...] += part

            @pl.when(k == nk - 1)
            def _():
                o_ref[...] = acc_ref[...].astype(o_ref.dtype)

    a_spec = pl.BlockSpec((tk, tm), lambda i, j, k: (k, i)) if ta else pl.BlockSpec((tm, tk), lambda i, j, k: (i, k))
    b_spec = pl.BlockSpec((tn, tk), lambda i, j, k: (j, k)) if tb else pl.BlockSpec((tk, tn), lambda i, j, k: (k, j))
    return pl.pallas_call(
        body, name=name, grid=(m // tm, n // tn, nk),
        in_specs=[a_spec, b_spec], out_specs=pl.BlockSpec((tm, tn), lambda i, j, k: (i, j)),
        out_shape=_sds((m, n), out_dtype),
        scratch_shapes=[pltpu.VMEM((tm, tn) if nk > 1 else (8, 128), F32)],
        compiler_params=_params(("parallel", "parallel", "arbitrary")),
    )(a, b)


def _log_gamma(r):
    rp = jnp.full((8, 128), -1.0, F32).at[:2, :RH].set(r.reshape(2, RH))

    def body(r_ref, o_ref):
        o_ref[...] = jnp.log1p(-jnp.exp2(r_ref[...]))

    out = pl.pallas_call(body, name="log_gamma", out_shape=_sds((8, 128), F32))(rp)
    return out[:2, :RH]


def _mod_fwd(c8, w_ada16, b_ada):
    def body(c_ref, w_ref, b_ref, o_ref):
        o_ref[...] = _dot(_silu(c_ref[...]).astype(BF16), w_ref[...]) + b_ref[...]

    return pl.pallas_call(
        body, name="mod_fwd", grid=(3,),
        in_specs=[pl.BlockSpec((8, D), lambda j: (0, 0)), pl.BlockSpec((D, D), lambda j: (0, j)),
                  pl.BlockSpec((1, D), lambda j: (0, j))],
        out_specs=pl.BlockSpec((8, D), lambda j: (0, j)), out_shape=_sds((8, 3 * D), F32),
        compiler_params=_params(("arbitrary",)),
    )(c8, w_ada16, b_ada)


def _norm_fwd(x2, mod3, norm_w, rows_all, row_off, rows_per_group, group0, h_prev, tm, name):
    rows = x2.shape[0]
    rb0 = row_off // tm
    bpg = rows_per_group // tm

    def body(*refs):
        x_ref, sh_ref, sc_ref, nw_ref, o_ref = refs[-5:]
        xv = x_ref[...]
        r = lax.rsqrt(jnp.mean(xv * xv, axis=-1, keepdims=True) + EPS)
        o_ref[...] = ((xv * r) * nw_ref[...] * (1.0 + sc_ref[...]) + sh_ref[...]).astype(BF16)

    in_specs = [pl.BlockSpec((tm, D), lambda i: (i, 0)),
                pl.BlockSpec((None, 1, D), lambda i: (group0 + i // bpg, 0, 0)),
                pl.BlockSpec((None, 1, D), lambda i: (group0 + i // bpg, 0, 1)),
                pl.BlockSpec((1, D), lambda i: (0, 0))]
    args = [x2, mod3, mod3, norm_w]
    alias = {}
    if h_prev is not None:
        in_specs.insert(0, pl.BlockSpec(memory_space=pl.ANY))
        args.insert(0, h_prev)
        alias = {0: 0}
    return pl.pallas_call(
        body, name=name, grid=(rows // tm,), in_specs=in_specs,
        out_specs=pl.BlockSpec((tm, D), lambda i: (rb0 + i, 0)), out_shape=_sds((rows_all, D), BF16),
        input_output_aliases=alias, compiler_params=_params(("parallel",)),
    )(*args)


def _decays(lg, fwd):
    ii = lax.broadcasted_iota(jnp.int32, (CH, CH), 0)
    jj = lax.broadcasted_iota(jnp.int32, (CH, CH), 1)
    ri = lax.broadcasted_iota(jnp.int32, (CH, 1), 0).astype(F32)
    rel = (ii - jj) if fwd else (jj - ii)
    relf = jnp.maximum(rel, 0).astype(F32)
    mask = jnp.where(rel >= 0, jnp.exp(lg * relf), 0.0)
    qe = (ri + 1.0) if fwd else (CH - ri)
    ke = (CH - 1.0 - ri) if fwd else ri
    return mask, relf, jnp.exp(lg * qe), qe, jnp.exp(lg * ke), ke


def _wide_specs(rowf):
    return [pl.BlockSpec((CH, 2 * DK), lambda b, c: (rowf(b, c), RQ // (2 * DK))),
            pl.BlockSpec((CH, 2 * DK), lambda b, c: (rowf(b, c), RQ // (2 * DK) + 1)),
            pl.BlockSpec((CH, RH * DK), lambda b, c: (rowf(b, c), RK // (RH * DK))),
            pl.BlockSpec((CH, 2 * DV), lambda b, c: (rowf(b, c), RV // (2 * DV))),
            pl.BlockSpec((CH, 2 * DV), lambda b, c: (rowf(b, c), RV // (2 * DV) + 1))]


def _head_qkv(refs, h):
    q0, q1, k, v0, v1 = refs
    lo = h % 2
    q = (q0, q1)[h // 2][:, lo * DK:(lo + 1) * DK]
    kk = k[:, h * DK:(h + 1) * DK] * (DK ** -0.5)
    v16 = (v0, v1)[h // 2][:, lo * DV:(lo + 1) * DV].astype(BF16)
    return q, kk, v16


def _ctx_state(px, lg, nb, t_rows, cx):
    rb = t_rows // cx

    def body(lg_ref, k_ref, v_ref, sf_ref, sb_ref):
        h = pl.program_id(1)
        pos = lax.broadcasted_iota(jnp.int32, (cx, 1), 0).astype(F32)
        k = k_ref[...] * (DK ** -0.5)
        v16 = v_ref[...].astype(BF16)
        wf = jnp.exp(lg_ref[0, h] * (cx - 1.0 - pos))
        wb = jnp.exp(lg_ref[1, h] * pos)
        sf_ref[...] = _dot((k * wf).astype(BF16), v16, TN)
        sb_ref[...] = _dot((k * wb).astype(BF16), v16, TN)

    st = pl.BlockSpec((None, None, DK, DV), lambda b, h: (b, h, 0, 0))
    return pl.pallas_call(
        body, name="ctx_state", grid=(nb, RH),
        in_specs=[pl.BlockSpec(memory_space=pltpu.SMEM),
                  pl.BlockSpec((cx, DK), lambda b, h: (rb + b, RK // DK + h)),
                  pl.BlockSpec((cx, DV), lambda b, h: (rb + b, RV // DV + h))],
        out_specs=[st, st], out_shape=[_sds((nb, RH, DK, DV), F32)] * 2,
        compiler_params=_params(("parallel", "parallel")),
    )(lg, px, px)


def _ret_fwd(px, lg, s0f, s0b, nb, nc):
    t_rows = nb * nc * CH

    def body(lg_ref, *refs):
        ins = (refs[0:5], refs[5:10])
        s0f_ref, s0b_ref, of_ref, ob_ref, hf_ref, hb_ref, sf, sb = refs[10:]
        c = pl.program_id(1)

        @pl.when(c == 0)
        def _():
            sf[...] = s0f_ref[...]
            sb[...] = s0b_ref[...]

        for d, (o_ref, h_ref, s) in enumerate(((of_ref, hf_ref, sf), (ob_ref, hb_ref, sb))):
            for h in range(RH):
                lg_d = lg_ref[d, h]
                mask, _, qd, _, kd, _ = _decays(lg_d, d == 0)
                q, k, v16 = _head_qkv(ins[d], h)
                a = _dot(q.astype(BF16), k.astype(BF16), NT)
                st = s[h]
                st16 = st.astype(BF16)
                h_ref[h] = st16
                o_ref[:, h * DV:(h + 1) * DV] = _dot((a * mask).astype(BF16), v16) + _dot((q * qd).astype(BF16), st16)
                s[h] = st * jnp.exp(lg_d * CH) + _dot((k * kd).astype(BF16), v16, TN)

    def fw(b, c):
        return b * nc + c

    def bw(b, c):
        return b * nc + nc - 1 - c

    st = pl.BlockSpec((None, RH, DK, DV), lambda b, c: (b, 0, 0, 0))
    in_specs = [pl.BlockSpec(memory_space=pltpu.SMEM)] + _wide_specs(fw) + _wide_specs(bw) + [st, st]
    out_specs = [pl.BlockSpec((CH, RH * DV), lambda b, c: (fw(b, c), 0)),
                 pl.BlockSpec((CH, RH * DV), lambda b, c: (bw(b, c), 0)),
                 pl.BlockSpec((None, None, RH, DK, DV), lambda b, c: (b, c, 0, 0, 0)),
                 pl.BlockSpec((None, None, RH, DK, DV), lambda b, c: (b, nc - 1 - c, 0, 0, 0))]
    return pl.pallas_call(
        body, name="ret_fwd", grid=(nb, nc), in_specs=in_specs, out_specs=out_specs,
        out_shape=[_sds((t_rows, RH * DV), F32)] * 2 + [_sds((nb, nc, RH, DK, DV), BF16)] * 2,
        scratch_shapes=[pltpu.VMEM((RH, DK, DV), F32), pltpu.VMEM((RH, DK, DV), F32)],
        compiler_params=_params(("parallel", "arbitrary")),
    )(lg, *([px] * 10), s0f, s0b)


def _ret_post(o_f, o_b, px, tm):
    t_rows = o_f.shape[0]

    def body(of_ref, ob_ref, g_ref, y_ref):
        o = of_ref[...] + ob_ref[...]
        r = lax.rsqrt(jnp.mean(o * o, axis=-1, keepdims=True) + EPS)
        y_ref[...] = ((o * r) * _silu(g_ref[...])).astype(BF16)

    blk = pl.BlockSpec((tm, DV), lambda i, h: (i, h))
    return pl.pallas_call(
        body, name="ret_post", grid=(t_rows // tm, RH),
        in_specs=[blk, blk, pl.BlockSpec((tm, DV), lambda i, h: (i, RG // DV + h))],
        out_specs=blk, out_shape=_sds((t_rows, RH * DV), BF16),
        compiler_params=_params(("parallel", "parallel")),
    )(o_f, o_b, px)


def _rope_tables(seq):
    rows = seq // GRID_W
    row = np.repeat(np.arange(rows, dtype=np.float32), GRID_W)
    col = np.tile(np.arange(GRID_W, dtype=np.float32), rows)
    half = HD // 2
    freqs = (ROPE_THETA ** (-np.arange(0, half, 2, dtype=np.float32) / half)).astype(np.float32)
    ang = np.concatenate([row[:, None] * freqs, col[:, None] * freqs], axis=-1).astype(np.float32)
    cos = np.repeat(np.cos(ang), 2, axis=-1).astype(np.float32)
    sin = np.repeat(np.sin(ang), 2, axis=-1).astype(np.float32)
    sign = np.tile(np.array([-1.0, 1.0], np.float32), HD // 2)
    return jnp.asarray(cos), jnp.asarray(sin * sign)


def _swap_pairs(v):
    lane = lax.broadcasted_iota(jnp.int32, v.shape, 1)
    return jnp.where((lane & 1) == 0, pltpu.roll(v, HD - 1, 1), pltpu.roll(v, 1, 1))


def _qk_prep(px, nw, cos, sin, rows, row_off, col_off, heads, hb, seq, tm, name):
    rope = cos is not None
    rb0 = row_off // tm
    pb = seq // tm if rope else 1
    bw = hb * HD

    def body(*refs):
        if rope:
            x_ref, w_ref, c_ref, s_ref, o_ref = refs
        else:
            x_ref, w_ref, o_ref = refs
        for h in range(hb):
            sl = slice(h * HD, (h + 1) * HD)
            xv = x_ref[:, sl]
            r = lax.rsqrt(jnp.mean(xv * xv, axis=-1, keepdims=True) + EPS)
            t = (xv * r) * w_ref[...]
            if rope:
                t = t * c_ref[...] + _swap_pairs(t) * s_ref[...]
            o_ref[:, sl] = t.astype(BF16)

    in_specs = [pl.BlockSpec((tm, bw), lambda i, j: (rb0 + i, col_off // bw + j)),
                pl.BlockSpec((1, HD), lambda i, j: (0, 0))]
    args = [px, nw]
    if rope:
        in_specs += [pl.BlockSpec((tm, HD), lambda i, j: (i % pb, 0))] * 2
        args += [cos, sin]
    return pl.pallas_call(
        body, name=name, grid=(rows // tm, heads // hb), in_specs=in_specs,
        out_specs=pl.BlockSpec((tm, bw), lambda i, j: (i, j)), out_shape=_sds((rows, heads * HD), BF16),
        compiler_params=_params(("parallel", "parallel")),
    )(*args)


def _att_fwd(q16, kx16, kc16, px, nb, seq, cx, tq):
    t_rows = nb * seq
    nq = seq // tq
    rep = HQ // HKV
    gw = rep * HD

    def body(q_ref, kx_ref, kc_ref, vx_ref, vc_ref, g_ref, o_ref, y_ref, l_ref):
        kx = kx_ref[...]
        kc = kc_ref[...]
        vx = vx_ref[...].astype(BF16)
        vc = vc_ref[...].astype(BF16)
        l_ref[...] = jnp.zeros_like(l_ref)
        for r in range(rep):
            sl = slice(r * HD, (r + 1) * HD)
            q = q_ref[:, sl]
            s1 = _dot(q, kx, NT)
            s2 = _dot(q, kc, NT)
            m = jnp.maximum(jnp.max(s1, axis=-1, keepdims=True), jnp.max(s2, axis=-1, keepdims=True))
            e1 = jnp.exp2((s1 - m) * SM_C)
            e2 = jnp.exp2((s2 - m) * SM_C)
            tot = jnp.sum(e1, axis=-1, keepdims=True) + jnp.sum(e2, axis=-1, keepdims=True)
            o = (_dot(e1.astype(BF16), vx) + _dot(e2.astype(BF16), vc)) * (1.0 / tot)
            o_ref[:, sl] = o
            y_ref[:, sl] = (o * _silu(g_ref[:, sl])).astype(BF16)
            l_ref[:, r:r + 1] = m * SM_C + jnp.log(tot) * float(np.log2(np.e))

    qblk = pl.BlockSpec((tq, gw), lambda b, g, i: (b * nq + i, g))
    return pl.pallas_call(
        body, name="att_fwd", grid=(nb, HKV, nq),
        in_specs=[qblk,
                  pl.BlockSpec((seq, HD), lambda b, g, i: (b, g)),
                  pl.BlockSpec((cx, HD), lambda b, g, i: (b, g)),
                  pl.BlockSpec((seq, HD), lambda b, g, i: (b, AV // HD + g)),
                  pl.BlockSpec((cx, HD), lambda b, g, i: (t_rows // cx + b, AV // HD + g)),
                  pl.BlockSpec((tq, gw), lambda b, g, i: (b * nq + i, AG // gw + g))],
        out_specs=[qblk, qblk, pl.BlockSpec((tq, 128), lambda b, g, i: (b * nq + i, g))],
        out_shape=[_sds((t_rows, D), F32), _sds((t_rows, D), BF16), _sds((t_rows, HKV * 128), F32)],
        compiler_params=_params(("parallel", "parallel", "parallel")),
    )(q16, kx16, kc16, px, px, px)


def _merge(yret16, yatt16, px, w_o_ret16, w_o_att16, tm):
    t_rows = yret16.shape[0]
    hw = D // 2

    def body(yr_ref, wr_ref, ya_ref, wa_ref, mr_ref, ma_ref, ar_ref, aa_ref, y_ref):
        ar = _dot(yr_ref[...], wr_ref[...])
        aa = _dot(ya_ref[...], wa_ref[...])
        ar_ref[...] = ar
        aa_ref[...] = aa
        y_ref[...] = (_sig(mr_ref[...]) * ar + _sig(ma_ref[...]) * aa).astype(BF16)

    half = pl.BlockSpec((tm, hw), lambda i, j: (i, j))
    return pl.pallas_call(
        body, name="merge", grid=(t_rows // tm, 2),
        in_specs=[pl.BlockSpec((tm, RH * DV), lambda i, j: (i, 0)), pl.BlockSpec((RH * DV, hw), lambda i, j: (0, j)),
                  pl.BlockSpec((tm, D), lambda i, j: (i, 0)), pl.BlockSpec((D, hw), lambda i, j: (0, j)),
                  pl.BlockSpec((tm, hw), lambda i, j: (i, MR // hw + j)),
                  pl.BlockSpec((tm, hw), lambda i, j: (i, MA // hw + j))],
        out_specs=[half, half, half],
        out_shape=[_sds((t_rows, D), F32), _sds((t_rows, D), F32), _sds((t_rows, D), BF16)],
        compiler_params=_params(("parallel", "parallel")),
    )(yret16, w_o_ret16, yatt16, w_o_att16, px, px)


def _outproj(y16, w_out16, x2, tgt, mod3, nb, seq, tm):
    t_rows = nb * seq
    bpb = seq // tm

    def body(y_ref, w_ref, x_ref, t_ref, g_ref, dxn_ref, dout_ref, dg_ref, loss_ref):
        i = pl.program_id(1)
        out = _dot(y_ref[...], w_ref[...])
        gate = g_ref[...]
        diff = x_ref[...] + gate * out - t_ref[...]
        dxn = diff * (1.0 / D)
        dxn_ref[...] = dxn
        dout_ref[...] = (gate * dxn).astype(BF16)
        dg = jnp.sum(dxn * out, axis=0, keepdims=True)
        ls = jnp.broadcast_to(jnp.sum(diff * diff) * (0.5 / D), (1, 128))

        @pl.when(i == 0)
        def _():
            dg_ref[...] = dg
            loss_ref[...] = ls

        @pl.when(i > 0)
        def _():
            dg_ref[...] += dg
            loss_ref[...] += ls

    row = pl.BlockSpec((tm, D), lambda b, i: (b * bpb + i, 0))
    return pl.pallas_call(
        body, name="outproj", grid=(nb, bpb),
        in_specs=[row, pl.BlockSpec((D, D), lambda b, i: (0, 0)), row, row,
                  pl.BlockSpec((None, 1, D), lambda b, i: (b, 0, 2))],
        out_specs=[row, row, pl.BlockSpec((None, 1, D), lambda b, i: (b, 0, 0)),
                   pl.BlockSpec((None, 1, 128), lambda b, i: (b, 0, 0))],
        out_shape=[_sds((t_rows, D), F32), _sds((t_rows, D), BF16), _sds((nb, 1, D), F32), _sds((nb, 1, 128), F32)],
        compiler_params=_params(("parallel", "arbitrary")),
    )(y16, w_out16, x2, tgt, mod3)


def _bwd_merge(dout16, w_out16, px, a_ret, a_att, tm):
    t_rows = dout16.shape[0]
    hw = D // 2

    def body(do_ref, w_ref, mr_ref, ma_ref, ar_ref, aa_ref, dar_ref, daa_ref, dmr_ref, dma_ref):
        dy = _dot(do_ref[...], w_ref[...], NT)
        sr = _sig(mr_ref[...])
        sa = _sig(ma_ref[...])
        dar_ref[...] = (dy * sr).astype(BF16)
        daa_ref[...] = (dy * sa).astype(BF16)
        dmr_ref[...] = (dy * ar_ref[...] * sr * (1.0 - sr)).astype(BF16)
        dma_ref[...] = (dy * aa_ref[...] * sa * (1.0 - sa)).astype(BF16)

    half = pl.BlockSpec((tm, hw), lambda i, j: (i, j))
    return pl.pallas_call(
        body, name="bwd_merge", grid=(t_rows // tm, 2),
        in_specs=[pl.BlockSpec((tm, D), lambda i, j: (i, 0)), pl.BlockSpec((hw, D), lambda i, j: (j, 0)),
                  pl.BlockSpec((tm, hw), lambda i, j: (i, MR // hw + j)),
                  pl.BlockSpec((tm, hw), lambda i, j: (i, MA // hw + j)), half, half],
        out_specs=[half] * 4, out_shape=[_sds((t_rows, D), BF16)] * 4,
        compiler_params=_params(("parallel", "parallel")),
    )(dout16, w_out16, px, px, a_ret, a_att)


def _bwd_branch_ret(da_ret16, w_o_ret16, px, o_f, o_b, tm):
    t_rows = da_ret16.shape[0]

    def body(da_ref, w_ref, g_ref, of_ref, ob_ref, do_ref, dg_ref):
        dy = _dot(da_ref[...], w_ref[...], NT)
        g = g_ref[...]
        o = of_ref[...] + ob_ref[...]
        r = lax.rsqrt(jnp.mean(o * o, axis=-1, keepdims=True) + EPS)
        on = o * r
        don = dy * _silu(g)
        dg_ref[...] = (dy * on * _dsilu(g)).astype(BF16)
        do_ref[...] = (r * (don - on * jnp.mean(on * don, axis=-1, keepdims=True))).astype(BF16)

    blk = pl.BlockSpec((tm, DV), lambda i, h: (i, h))
    return pl.pallas_call(
        body, name="bwd_branch_ret", grid=(t_rows // tm, RH),
        in_specs=[pl.BlockSpec((tm, D), lambda i, h: (i, 0)), pl.BlockSpec((DV, D), lambda i, h: (h, 0)),
                  pl.BlockSpec((tm, DV), lambda i, h: (i, RG // DV + h)), blk, blk],
        out_specs=[blk, blk], out_shape=[_sds((t_rows, RH * DV), BF16)] * 2,
        compiler_params=_params(("parallel", "parallel")),
    )(da_ret16, w_o_ret16, px, o_f, o_b)


def _bwd_branch_att(da_att16, w_o_att16, px, o_att, tm):
    t_rows = da_att16.shape[0]
    hw = D // 2

    def body(da_ref, w_ref, g_ref, o_ref, dao_ref, dg_ref):
        dy = _dot(da_ref[...], w_ref[...], NT)
        g = g_ref[...]
        dao_ref[...] = dy * _silu(g)
        dg_ref[...] = (dy * o_ref[...] * _dsilu(g)).astype(BF16)

    half = pl.BlockSpec((tm, hw), lambda i, j: (i, j))
    return pl.pallas_call(
        body, name="bwd_branch_att", grid=(t_rows // tm, 2),
        in_specs=[pl.BlockSpec((tm, D), lambda i, j: (i, 0)), pl.BlockSpec((hw, D), lambda i, j: (j, 0)),
                  pl.BlockSpec((tm, hw), lambda i, j: (i, AG // hw + j)), half],
        out_specs=[half, half], out_shape=[_sds((t_rows, D), F32), _sds((t_rows, D), BF16)],
        compiler_params=_params(("parallel", "parallel")),
    )(da_att16, w_o_att16, px, o_att)


def _att_bwd(q16, kx16, kc16, px, dao, o_att, lse, nb, seq, cx, tq):
    t_rows = nb * seq
    nq = seq // tq
    rep = HQ // HKV
    gw = rep * HD
    scale = HD ** -0.5

    def body(q_ref, kx_ref, kc_ref, vx_ref, vc_ref, dao_ref, o_ref, l_ref, dq_ref, dkx_ref, dvx_ref, dkc_ref, dvc_ref):
        i = pl.program_id(2)
        kx = kx_ref[...]
        kc = kc_ref[...]
        vx = vx_ref[...].astype(BF16)
        vc = vc_ref[...].astype(BF16)
        dkx = jnp.zeros((seq, HD), F32)
        dvx = jnp.zeros((seq, HD), F32)
        dkc = jnp.zeros((cx, HD), F32)
        dvc = jnp.zeros((cx, HD), F32)
        for r in range(rep):
            sl = slice(r * HD, (r + 1) * HD)
            q = q_ref[:, sl]
            lr = l_ref[:, r:r + 1]
            p1 = jnp.exp2(_dot(q, kx, NT) * SM_C - lr)
            p2 = jnp.exp2(_dot(q, kc, NT) * SM_C - lr)
            da = dao_ref[:, sl]
            da16 = da.astype(BF16)
            delta = jnp.sum(da * o_ref[:, sl], axis=-1, keepdims=True)
            ds1 = (p1 * (_dot(da16, vx, NT) - delta)).astype(BF16)
            ds2 = (p2 * (_dot(da16, vc, NT) - delta)).astype(BF16)
            dq_ref[:, sl] = (_dot(ds1, kx) + _dot(ds2, kc)) * scale
            dkx += _dot(ds1, q, TN)
            dkc += _dot(ds2, q, TN)
            dvx += _dot(p1.astype(BF16), da16, TN)
            dvc += _dot(p2.astype(BF16), da16, TN)
        dkx = dkx * scale
        dkc = dkc * scale

        @pl.when(i == 0)
        def _():
            dkx_ref[...] = dkx
            dvx_ref[...] = dvx
            dkc_ref[...] = dkc
            dvc_ref[...] = dvc

        @pl.when(i > 0)
        def _():
            dkx_ref[...] += dkx
            dvx_ref[...] += dvx
            dkc_ref[...] += dkc
            dvc_ref[...] += dvc

    qblk = pl.BlockSpec((tq, gw), lambda b, g, i: (b * nq + i, g))
    kxb = pl.BlockSpec((None, seq, HD), lambda b, g, i: (b, 0, g))
    kcb = pl.BlockSpec((None, cx, HD), lambda b, g, i: (b, 0, g))
    return pl.pallas_call(
        body, name="att_bwd", grid=(nb, HKV, nq),
        in_specs=[qblk,
                  pl.BlockSpec((seq, HD), lambda b, g, i: (b, g)),
                  pl.BlockSpec((cx, HD), lambda b, g, i: (b, g)),
                  pl.BlockSpec((seq, HD), lambda b, g, i: (b, AV // HD + g)),
                  pl.BlockSpec((cx, HD), lambda b, g, i: (t_rows // cx + b, AV // HD + g)),
                  qblk, qblk, pl.BlockSpec((tq, 128), lambda b, g, i: (b * nq + i, g))],
        out_specs=[qblk, kxb, kxb, kcb, kcb],
        out_shape=[_sds((t_rows, D), F32), _sds((nb, seq, HKV * HD), F32), _sds((nb, seq, HKV * HD), F32),
                   _sds((nb, cx, HKV * HD), F32), _sds((nb, cx, HKV * HD), F32)],
        compiler_params=_params(("parallel", "parallel", "arbitrary")),
    )(q16, kx16, kc16, px, px, dao, o_att, lse)


def _qk_prep_bwd(dt, px, nw, cos, sin, rows, row_off, col_off, heads, hb, seq, tm, name):
    rope = cos is not None
    rb0 = row_off // tm
    pb = seq // tm if rope else 1
    bw = hb * HD

    def body(*refs):
        if rope:
            d_ref, x_ref, w_ref, c_ref, s_ref, dx_ref, dw_ref = refs
        else:
            d_ref, x_ref, w_ref, dx_ref, dw_ref = refs
        first = jnp.logical_and(pl.program_id(0) == 0, pl.program_id(1) == 0)
        dw = jnp.zeros((1, HD), F32)
        for h in range(hb):
            sl = slice(h * HD, (h + 1) * HD)
            dtv = d_ref[:, sl]
            if rope:
                dtv = dtv * c_ref[...] + _swap_pairs(dtv * s_ref[...])
            xv = x_ref[:, sl]
            r = lax.rsqrt(jnp.mean(xv * xv, axis=-1, keepdims=True) + EPS)
            xh = xv * r
            dxh = dtv * w_ref[...]
            dx_ref[:, sl] = (r * (dxh - xh * jnp.mean(dxh * xh, axis=-1, keepdims=True))).astype(BF16)
            dw += jnp.sum(dtv * xh, axis=0, keepdims=True)

        @pl.when(first)
        def _():
            dw_ref[...] = dw

        @pl.when(jnp.logical_not(first))
        def _():
            dw_ref[...] += dw

    blk = pl.BlockSpec((tm, bw), lambda i, j: (i, j))
    in_specs = [blk, pl.BlockSpec((tm, bw), lambda i, j: (rb0 + i, col_off // bw + j)),
                pl.BlockSpec((1, HD), lambda i, j: (0, 0))]
    args = [dt, px, nw]
    if rope:
        in_specs += [pl.BlockSpec((tm, HD), lambda i, j: (i % pb, 0))] * 2
        args += [cos, sin]
    return pl.pallas_call(
        body, name=name, grid=(rows // tm, heads // hb), in_specs=in_specs,
        out_specs=[blk, pl.BlockSpec((1, HD), lambda i, j: (0, 0))],
        out_shape=[_sds((rows, heads * HD), BF16), _sds((1, HD), F32)],
        compiler_params=_params(("arbitrary", "arbitrary")),
    )(*args)


def _ret_bwd(px, lg, do16, hist_f, hist_b, nb, nc):
    t_rows = nb * nc * CH

    def body(lg_ref, *refs):
        ins = (refs[0:5], refs[7:12])
        do_refs = (refs[5], refs[12])
        h_refs = (refs[6], refs[13])
        outs = (refs[14:17], refs[17:20])
        ds_outs = (refs[20], refs[21])
        dlg_ref = refs[22]
        dss = (refs[23], refs[24])
        c = pl.program_id(1)

        @pl.when(c == 0)
        def _():
            dss[0][...] = jnp.zeros_like(dss[0])
            dss[1][...] = jnp.zeros_like(dss[1])
            dlg_ref[...] = jnp.zeros_like(dlg_ref)

        for d in range(2):
            dq_ref, dk_ref, dv_ref = outs[d]
            for h in range(RH):
                lg_d = lg_ref[d, h]
                mask, relf, qd, qe, kd, ke = _decays(lg_d, d == 0)
                g_ch = jnp.exp(lg_d * CH)
                q, k, v16 = _head_qkv(ins[d], h)
                q16 = q.astype(BF16)
                k16 = k.astype(BF16)
                do16v = do_refs[d][:, h * DV:(h + 1) * DV]
                st16 = h_refs[d][h]
                dst = dss[d][h]
                dst16 = dst.astype(BF16)
                a = _dot(q16, k16, NT) * mask
                dp = _dot(do16v, v16, NT)
                da16 = (dp * mask).astype(BF16)
                dq_cross = _dot(do16v, st16, NT) * qd
                dq_ref[:, h * DK:(h + 1) * DK] = _dot(da16, k16) + dq_cross
                dk_state = _dot(v16, dst16, NT) * kd
                dk_ref[:, h * DK:(h + 1) * DK] = (_dot(da16, q16, TN) + dk_state) * (DK ** -0.5)
                dv_ref[:, h * DV:(h + 1) * DV] = _dot(a.astype(BF16), do16v, TN) + _dot((k * kd).astype(BF16), dst16)
                dlg = (jnp.sum(relf * a * dp)
                       + jnp.sum(qe * jnp.sum(q * dq_cross, axis=-1, keepdims=True))
                       + jnp.sum(ke * jnp.sum(k * dk_state, axis=-1, keepdims=True))
                       + CH * g_ch * jnp.sum(dst * st16.astype(F32)))
                row = d * RH + h
                dlg_ref[row:row + 1, :] += jnp.broadcast_to(dlg, (1, 128))
                ds_new = g_ch * dst + _dot((q * qd).astype(BF16), do16v, TN)
                dss[d][h] = ds_new

                @pl.when(c == nc - 1)
                def _():
                    ds_outs[d][h] = ds_new

    def fw(b, c):
        return b * nc + nc - 1 - c

    def bw(b, c):
        return b * nc + c

    def rows(rowf, width):
        return pl.BlockSpec((CH, width), lambda b, c: (rowf(b, c), 0))

    def hist(rowf):
        return pl.BlockSpec((None, None, RH, DK, DV), lambda b, c: (b, rowf(0, c), 0, 0, 0))

    st = pl.BlockSpec((None, RH, DK, DV), lambda b, c: (b, 0, 0, 0))
    in_specs = [pl.BlockSpec(memory_space=pltpu.SMEM)]
    out_specs = []
    for rowf in (fw, bw):
        in_specs += _wide_specs(rowf) + [rows(rowf, RH * DV), hist(rowf)]
        out_specs += [rows(rowf, RH * DK), rows(rowf, RH * DK), rows(rowf, RH * DV)]
    out_specs += [st, st, pl.BlockSpec((None, 8, 128), lambda b, c: (b, 0, 0))]
    qk = _sds((t_rows, RH * DK), F32)
    vv = _sds((t_rows, RH * DV), F32)
    return pl.pallas_call(
        body, name="ret_bwd", grid=(nb, nc), in_specs=in_specs, out_specs=out_specs,
        out_shape=[qk, qk, vv, qk, qk, vv, _sds((nb, RH, DK, DV), F32), _sds((nb, RH, DK, DV), F32),
                   _sds((nb, 8, 128), F32)],
        scratch_shapes=[pltpu.VMEM((RH, DK, DV), F32), pltpu.VMEM((RH, DK, DV), F32)],
        compiler_params=_params(("parallel", "arbitrary")),
    )(lg, *([px] * 5), do16, hist_f, *([px] * 5), do16, hist_b)


def _ctx_state_bwd(px, lg, ds_f, ds_b, nb, t_rows, cx):
    rb = t_rows // cx

    def body(lg_ref, k_ref, v_ref, dsf_ref, dsb_ref, dk_ref, dv_ref, dlg_ref):
        h = pl.program_id(1)
        pos = lax.broadcasted_iota(jnp.int32, (cx, 1), 0).astype(F32)
        k = k_ref[...] * (DK ** -0.5)
        v16 = v_ref[...].astype(BF16)
        dk = jnp.zeros((cx, DK), F32)
        dv = jnp.zeros((cx, DV), F32)
        dlg_ref[...] = jnp.zeros_like(dlg_ref)
        for d, (ds_ref, e) in enumerate(((dsf_ref, cx - 1.0 - pos), (dsb_ref, pos))):
            w = jnp.exp(lg_ref[d, h] * e)
            ds16 = ds_ref[...].astype(BF16)
            t = _dot(v16, ds16, NT)
            dk += t * w
            dv += _dot((k * w).astype(BF16), ds16)
            dlg = jnp.sum(e * w * jnp.sum(k * t, axis=-1, keepdims=True))
            dlg_ref[d:d + 1, :] = jnp.broadcast_to(dlg, (1, 128))
        dk_ref[...] = (dk * (DK ** -0.5)).astype(BF16)
        dv_ref[...] = dv.astype(BF16)

    st = pl.BlockSpec((None, None, DK, DV), lambda b, h: (b, h, 0, 0))
    return pl.pallas_call(
        body, name="ctx_state_bwd", grid=(nb, RH),
        in_specs=[pl.BlockSpec(memory_space=pltpu.SMEM),
                  pl.BlockSpec((cx, DK), lambda b, h: (rb + b, RK // DK + h)),
                  pl.BlockSpec((cx, DV), lambda b, h: (rb + b, RV // DV + h)), st, st],
        out_specs=[pl.BlockSpec((cx, DK), lambda b, h: (b, h)), pl.BlockSpec((cx, DV), lambda b, h: (b, h)),
                   pl.BlockSpec((None, None, 8, 128), lambda b, h: (b, h, 0, 0))],
        out_shape=[_sds((nb * cx, RH * DK), BF16), _sds((nb * cx, RH * DV), BF16), _sds((nb, RH, 8, 128), F32)],
        compiler_params=_params(("parallel", "parallel")),
    )(lg, px, px, ds_f, ds_b)


def _assemble_lat(rows_all, dk_f, dk_b, dv_f, dv_b, dak16, dvx, dq_f, dq_b, drg16, daq16, dag16, dmr16, dma16, tm):
    t_rows = dk_f.shape[0]

    def body(dkf, dkb, dvf, dvb, dak, dav, dqf, dqb, drg, daq, dag, dmr, dma, o_ref):
        o_ref[:, RK:RK + RH * DK] = (dkf[...] + dkb[...]).astype(BF16)
        o_ref[:, RV:RV + RH * DV] = (dvf[...] + dvb[...]).astype(BF16)
        o_ref[:, AK:AK + HKV * HD] = dak[...]
        o_ref[:, AV:AV + HKV * HD] = dav[...].astype(BF16)
        o_ref[:, RQ:RQ + RH * DK] = (dqf[...] + dqb[...]).astype(BF16)
        o_ref[:, RG:RG + RH * DV] = drg[...]
        o_ref[:, AQ:AQ + D] = daq[...]
        o_ref[:, AG:AG + D] = dag[...]
        o_ref[:, MR:MR + D] = dmr[...]
        o_ref[:, MA:MA + D] = dma[...]

    args = (dk_f, dk_b, dv_f, dv_b, dak16, dvx, dq_f, dq_b, drg16, daq16, dag16, dmr16, dma16)
    return pl.pallas_call(
        body, name="assemble_lat", grid=(t_rows // tm,),
        in_specs=[pl.BlockSpec((tm, a.shape[1]), lambda i: (i, 0)) for a in args],
        out_specs=pl.BlockSpec((tm, IN_COLS), lambda i: (i, 0)), out_shape=_sds((rows_all, IN_COLS), BF16),
        compiler_params=_params(("parallel",)),
    )(*args)


def _assemble_ctx(dp_all, dck16, dcv16, dcak16, dvc, t_rows, tm):
    c_rows = dck16.shape[0]
    rb = t_rows // tm

    def body(_, dck, dcv, dcak, dcav, o_ref):
        o_ref[:, RK:RK + RH * DK] = dck[...]
        o_ref[:, RV:RV + RH * DV] = dcv[...]
        o_ref[:, AK:AK + HKV * HD] = dcak[...]
        o_ref[:, AV:AV + HKV * HD] = dcav[...].astype(BF16)
        o_ref[:, KV_COLS:] = jnp.zeros((tm, IN_COLS - KV_COLS), BF16)

    args = (dck16, dcv16, dcak16, dvc)
    return pl.pallas_call(
        body, name="assemble_ctx", grid=(c_rows // tm,),
        in_specs=[pl.BlockSpec(memory_space=pl.ANY)]
        + [pl.BlockSpec((tm, a.shape[1]), lambda i: (i, 0)) for a in args],
        out_specs=pl.BlockSpec((tm, IN_COLS), lambda i: (rb + i, 0)), out_shape=_sds(dp_all.shape, BF16),
        input_output_aliases={0: 0},
        compiler_params=_params(("parallel",)),
    )(dp_all, *args)


def _norm_bwd(dh, x2, mod3, norm_w, dxn, row_off, rows_per_group, group0, tm, name):
    with_dx = dxn is not None
    rows = x2.shape[0]
    rb0 = row_off // tm
    bpg = rows_per_group // tm
    ngroups = rows // rows_per_group

    def body(*refs):
        if with_dx:
            dh_ref, x_ref, sc_ref, nw_ref, dxn_ref, dx_ref, dsh_ref, dsc_ref, dnw_ref = refs
        else:
            dh_ref, x_ref, sc_ref, nw_ref, dsh_ref, dsc_ref, dnw_ref = refs
        i = pl.program_id(0)
        dhv = dh_ref[...]
        xv = x_ref[...]
        nw = nw_ref[...]
        r = lax.rsqrt(jnp.mean(xv * xv, axis=-1, keepdims=True) + EPS)
        xh = xv * r
        dm = dhv * (1.0 + sc_ref[...])
        dsh = jnp.sum(dhv, axis=0, keepdims=True)
        dsc = jnp.sum(dhv * (xh * nw), axis=0, keepdims=True)
        dnw = jnp.sum(dm * xh, axis=0, keepdims=True)
        if with_dx:
            dxh = dm * nw
            dx_ref[...] = dxn_ref[...] + r * (dxh - xh * jnp.mean(dxh * xh, axis=-1, keepdims=True))

        @pl.when(i % bpg == 0)
        def _():
            dsh_ref[...] = dsh
            dsc_ref[...] = dsc

        @pl.when(i % bpg != 0)
        def _():
            dsh_ref[...] += dsh
            dsc_ref[...] += dsc

        @pl.when(i == 0)
        def _():
            dnw_ref[...] = dnw

        @pl.when(i > 0)
        def _():
            dnw_ref[...] += dnw

    grp = pl.BlockSpec((None, 1, D), lambda i: (i // bpg, 0, 0))
    in_specs = [pl.BlockSpec((tm, D), lambda i: (rb0 + i, 0)), pl.BlockSpec((tm, D), lambda i: (i, 0)),
                pl.BlockSpec((None, 1, D), lambda i: (group0 + i // bpg, 0, 1)),
                pl.BlockSpec((1, D), lambda i: (0, 0))]
    args = [dh, x2, mod3, norm_w]
    out_specs = [grp, grp, pl.BlockSpec((1, D), lambda i: (0, 0))]
    out_shape = [_sds((ngroups, 1, D), F32), _sds((ngroups, 1, D), F32), _sds((1, D), F32)]
    if with_dx:
        in_specs.append(pl.BlockSpec((tm, D), lambda i: (i, 0)))
        args.append(dxn)
        out_specs.insert(0, pl.BlockSpec((tm, D), lambda i: (i, 0)))
        out_shape.insert(0, _sds((rows, D), F32))
    return pl.pallas_call(
        body, name=name, grid=(rows // tm,), in_specs=in_specs, out_specs=out_specs, out_shape=out_shape,
        compiler_params=_params(("arbitrary",)),
    )(*args)


def _small_final(dmod_all, dmodc_parts, c_rows, dm_loc_rows, nw_parts, misc_parts, c_ctx, r_pad, w_ada16):
    loc = dm_loc_rows.shape[1]

    def body(dm_ref, dmc_ref, c_ref, dml_ref, nwp_ref, mp_ref, cc_ref, r_ref, w_ref,
             gb_ref, gc_ref, gnw_ref, misc_ref, gwa_ref):
        dmc = jnp.sum(dmc_ref[...], axis=0, keepdims=True)
        gb_ref[...] = jnp.sum(dm_ref[...], axis=0, keepdims=True) + dmc
        dsc = _dot(jnp.broadcast_to(dmc, (8, 3 * D)).astype(BF16), w_ref[...], NT)[0:1, :]
        gc_ref[...] = dsc * _dsilu(cc_ref[...])
        gnw_ref[...] = jnp.sum(nwp_ref[...], axis=0, keepdims=True)
        misc = jnp.sum(mp_ref[...], axis=0, keepdims=True)
        y = jnp.exp2(r_ref[...])
        lane = lax.broadcasted_iota(jnp.int32, (1, D), 1)
        is_decay = jnp.logical_and(lane >= 2 * HD, lane < 2 * HD + 2 * RH)
        misc_ref[...] = misc * jnp.where(is_decay, -(y * np.float32(np.log(2.0))) / (1.0 - y), 1.0)
        gwa_ref[...] = _dot(_silu(c_ref[...]).astype(BF16), dml_ref[...].astype(BF16), TN)

    return pl.pallas_call(
        body, name="small_final",
        out_shape=[_sds((1, 3 * D), F32), _sds((1, D), F32), _sds((1, D), F32), _sds((1, D), F32), _sds((D, loc), F32)],
        compiler_params=pltpu.CompilerParams(vmem_limit_bytes=VMEM_LIMIT),
    )(dmod_all, dmodc_parts, c_rows, dm_loc_rows, nw_parts, misc_parts, c_ctx, r_pad, w_ada16)


def _adamw(w, g, m, v, name):
    rows, cols = w.shape
    tm = _pick(rows, 256, 8)
    bc1 = 1.0 - B1 ** STEP
    bc2 = 1.0 - B2 ** STEP

    def body(w_ref, g_ref, m_ref, v_ref, d_ref, nm_ref, nv_ref):
        g_ = g_ref[...]
        nm = B1 * m_ref[...] + (1.0 - B1) * g_
        nv = B2 * v_ref[...] + (1.0 - B2) * (g_ * g_)
        nm_ref[...] = nm
        nv_ref[...] = nv
        d_ref[...] = -LR * ((nm / bc1) / (jnp.sqrt(nv / bc2) + ADAM_EPS) + WD * w_ref[...])

    blk = pl.BlockSpec((tm, cols), lambda i: (i, 0))
    return pl.pallas_call(
        body, name=name, grid=(rows // tm,), in_specs=[blk] * 4, out_specs=[blk] * 3,
        out_shape=[_sds((rows, cols), F32)] * 3, compiler_params=_params(("parallel",)),
    )(w, g, m, v)


def _mesh_pos():
    return lax.axis_index("x"), lax.axis_index("y"), lax.axis_index("c")


def _all_gather(arrs, name):
    n = len(arrs)

    def body(*refs):
        ins, outs = refs[:n], refs[n:2 * n]
        send_sems, recv_sems, local_sems = refs[2 * n:]
        x, y, c = _mesh_pos()
        me, sib = (x, y, c), (x, y, 1 - c)
        chips = [(1 - x, y), (x, 1 - y), (1 - x, 1 - y)]

        def slot(p):
            return 4 * p[0] + 2 * p[1] + p[2]

        def copy(a, k, block, to, own):
            dst = outs[a].at[slot(block)]
            return pltpu.make_async_remote_copy(
                src_ref=ins[a] if own else dst, dst_ref=dst, send_sem=send_sems.at[a, k], recv_sem=recv_sems.at[a, k],
                device_id=to, device_id_type=MESH_T)

        mine = [pltpu.make_async_copy(ins[a], outs[a].at[slot(me)], local_sems.at[a]) for a in range(n)]
        for cp in mine:
            cp.start()
        first = []
        for a in range(n):
            first.append(copy(a, 0, me, sib, True))
            first += [copy(a, 1 + j, me, (*chip, c), True) for j, chip in enumerate(chips)]
        for cp in first:
            cp.start()
        passed = []
        for j, chip in enumerate(chips):
            for a in range(n):
                copy(a, 1 + j, (*chip, c), me, False).wait_recv()
                fwd = copy(a, 4 + j, (*chip, c), sib, False)
                fwd.start()
                passed.append(fwd)
        for a in range(n):
            copy(a, 0, sib, me, False).wait_recv()
            for j, chip in enumerate(chips):
                copy(a, 4 + j, (*chip, 1 - c), me, False).wait_recv()
        for cp in first + passed:
            cp.wait_send()
        for cp in mine:
            cp.wait()

    hbm = pl.BlockSpec(memory_space=pl.ANY)
    return pl.pallas_call(
        body, name=name, in_specs=[hbm] * n, out_specs=[hbm] * n,
        out_shape=[_sds((N_DEV,) + a.shape, a.dtype) for a in arrs],
        scratch_shapes=[pltpu.SemaphoreType.DMA((n, 7)), pltpu.SemaphoreType.DMA((n, 7)), pltpu.SemaphoreType.DMA((n,))],
    )(*arrs)


def _scatter_blocks(arrs, name):
    n = len(arrs)

    def body(*refs):
        ins, outs = refs[:n], refs[n:2 * n]
        send_sems, recv_sems, local_sems = refs[2 * n:]
        x, y, c = _mesh_pos()
        me = 4 * x + 2 * y + c
        peers = []
        for k in range(1, N_DEV):
            fx, fy, fc = (k >> 2) & 1, (k >> 1) & 1, k & 1
            peers.append((x ^ fx, y ^ fy, c ^ fc))

        def slot(p):
            return 4 * p[0] + 2 * p[1] + p[2]

        mine = [pltpu.make_async_copy(ins[a].at[me], outs[a].at[me], local_sems.at[a]) for a in range(n)]
        for cp in mine:
            cp.start()
        sends = []
        for a in range(n):
            for k, p in enumerate(peers):
                sends.append(pltpu.make_async_remote_copy(
                    src_ref=ins[a].at[slot(p)], dst_ref=outs[a].at[me], send_sem=send_sems.at[a, k],
                    recv_sem=recv_sems.at[a, k], device_id=p, device_id_type=MESH_T))
        for cp in sends:
            cp.start()
        for a in range(n):
            for k, p in enumerate(peers):
                pltpu.make_async_remote_copy(
                    src_ref=ins[a].at[me], dst_ref=outs[a].at[slot(p)], send_sem=send_sems.at[a, k],
                    recv_sem=recv_sems.at[a, k], device_id=p, device_id_type=MESH_T).wait_recv()
        for cp in sends:
            cp.wait_send()
        for cp in mine:
            cp.wait()

    hbm = pl.BlockSpec(memory_space=pl.ANY)
    return pl.pallas_call(
        body, name=name, in_specs=[hbm] * n, out_specs=[hbm] * n,
        out_shape=[_sds(a.shape, a.dtype) for a in arrs],
        scratch_shapes=[pltpu.SemaphoreType.DMA((n, 7)), pltpu.SemaphoreType.DMA((n, 7)), pltpu.SemaphoreType.DMA((n,))],
    )(*arrs)


def _sum_slots(land, name):
    _, rows, cols = land.shape
    tm = _pick(rows, 256, 16)

    def body(l_ref, o_ref):
        acc = l_ref[0].astype(F32)
        for p in range(1, N_DEV):
            acc = acc + l_ref[p].astype(F32)
        o_ref[...] = acc

    return pl.pallas_call(
        body, name=name, grid=(rows // tm,),
        in_specs=[pl.BlockSpec((N_DEV, tm, cols), lambda i: (0, i, 0))],
        out_specs=pl.BlockSpec((tm, cols), lambda i: (i, 0)), out_shape=_sds((rows, cols), F32),
        compiler_params=_params(("parallel",)),
    )(land)


def _local_step(x, c, ctx, c_ctx, norm_w, b_ada, ret_log2_decay, q_norm_w, k_norm_w, loss_target,
                w_ada16, w_in_t16, w_o_ret16, w_o_att16, w_out16):
    nb, seq, _ = x.shape
    cx = ctx.shape[1]
    t_rows, c_rows = nb * seq, nb * cx
    rows_all = t_rows + c_rows
    nc = seq // CH
    tm = _pick(seq, 256, 128)
    te = _pick(seq, 512, 128)
    assert cx % tm == 0 and t_rows % cx == 0 and seq % GRID_W == 0

    x2 = x.reshape(t_rows, D)
    ctx2 = ctx.reshape(c_rows, D)
    tgt = loss_target.reshape(t_rows, D)
    c8 = jnp.zeros((8, D), F32).at[:nb].set(c).at[nb].set(c_ctx)
    lg = _log_gamma(ret_log2_decay)
    cos, sin = _rope_tables(seq)

    mod = _mod_fwd(c8, w_ada16, b_ada)
    mod3 = mod[:, None, :]
    h_all = _norm_fwd(x2, mod3, norm_w, rows_all, 0, seq, 0, None, te, "norm_fwd")
    h_all = _norm_fwd(ctx2, mod3, norm_w, rows_all, t_rows, c_rows, nb, h_all, tm, "norm_fwd_ctx")
    px = _matmul(h_all, w_in_t16, tb=True, tm=1536, tn=1536, tk=D, out_dtype=F32, name="in_proj")
    s0f, s0b = _ctx_state(px, lg, nb, t_rows, cx)
    o_f, o_b, hist_f, hist_b = _ret_fwd(px, lg, s0f, s0b, nb, nc)
    yret16 = _ret_post(o_f, o_b, px, te)
    q16 = _qk_prep(px, q_norm_w, cos, sin, t_rows, 0, AQ, HQ, 4, seq, te, "q_prep")
    kx16 = _qk_prep(px, k_norm_w, cos, sin, t_rows, 0, AK, HKV, HKV, seq, te, "k_prep")
    kc16 = _qk_prep(px, k_norm_w, None, None, c_rows, t_rows, AK, HKV, HKV, seq, tm, "kc_prep")
    o_att, yatt16, lse = _att_fwd(q16, kx16, kc16, px, nb, seq, cx, tm)
    a_ret, a_att, y16 = _merge(yret16, yatt16, px, w_o_ret16, w_o_att16, te)
    dxn, dout16, dgate, loss_b = _outproj(y16, w_out16, x2, tgt, mod3, nb, seq, te)

    gw_out = _matmul(y16, dout16, ta=True, tm=D, tn=D, tk=D, out_dtype=BF16, name="gw_out")
    da_ret16, da_att16, dmr16, dma16 = _bwd_merge(dout16, w_out16, px, a_ret, a_att, te)
    gw_o_ret = _matmul(yret16, da_ret16, ta=True, tm=D, tn=D, tk=D, out_dtype=BF16, name="gw_o_ret")
    gw_o_att = _matmul(yatt16, da_att16, ta=True, tm=D, tn=D, tk=D, out_dtype=BF16, name="gw_o_att")
    do16, drg16 = _bwd_branch_ret(da_ret16, w_o_ret16, px, o_f, o_b, te)
    dao, dag16 = _bwd_branch_att(da_att16, w_o_att16, px, o_att, te)
    dq_rot, dkx, dvx, dkc, dvc = _att_bwd(q16, kx16, kc16, px, dao, o_att, lse, nb, seq, cx, tm)
    daq16, gq = _qk_prep_bwd(dq_rot, px, q_norm_w, cos, sin, t_rows, 0, AQ, HQ, 4, seq, te, "q_prep_bwd")
    dak16, gk_lat = _qk_prep_bwd(dkx.reshape(t_rows, HKV * HD), px, k_norm_w, cos, sin, t_rows, 0, AK, HKV, HKV, seq, te,
                                 "k_prep_bwd")
    dcak16, gk_ctx = _qk_prep_bwd(dkc.reshape(c_rows, HKV * HD), px, k_norm_w, None, None, c_rows, t_rows, AK, HKV, HKV,
                                  seq, tm, "kc_prep_bwd")
    dq_f, dk_f, dv_f, dq_b, dk_b, dv_b, ds_f, ds_b, dlg_scan = _ret_bwd(px, lg, do16, hist_f, hist_b, nb, nc)
    dck16, dcv16, dlg_ctx = _ctx_state_bwd(px, lg, ds_f, ds_b, nb, t_rows, cx)
    dp_all = _assemble_lat(rows_all, dk_f, dk_b, dv_f, dv_b, dak16, dvx.reshape(t_rows, HKV * HD), dq_f, dq_b, drg16,
                           daq16, dag16, dmr16, dma16, tm)
    dp_all = _assemble_ctx(dp_all, dck16, dcv16, dcak16, dvc.reshape(c_rows, HKV * HD), t_rows, tm)
    gw_in_t = _matmul(dp_all, h_all, ta=True, tm=1536, tn=D, tk=1536, out_dtype=BF16, name="gw_in")
    dh = _matmul(dp_all, w_in_t16, tm=1536, tn=D, tk=1536, out_dtype=F32, name="d_h")
    grad_x, dsh, dsc, gnw_lat = _norm_bwd(dh, x2, mod3, norm_w, dxn, 0, seq, 0, te, "norm_bwd")
    dsh_c, dsc_c, gnw_ctx = _norm_bwd(dh, ctx2, mod3, norm_w, None, t_rows, c_rows, nb, tm, "norm_bwd_ctx")

    dlg = (jnp.sum(dlg_scan[:, :, 0], axis=0) + jnp.sum(dlg_ctx[:, :, :2, 0], axis=0).T.reshape(2 * RH)).reshape(1, 2 * RH)
    misc = jnp.concatenate([gq, gk_lat + gk_ctx, dlg, jnp.sum(loss_b[:, 0, 0]).reshape(1, 1),
                            jnp.zeros((1, D - 2 * HD - 2 * RH - 1), F32)], axis=1)
    rows = []
    for b in range(nb):
        rows += [dsh[b], dsc[b], dgate[b]]
    rows += [dsh_c[0], dsc_c[0]] + [c[b:b + 1] for b in range(nb)] + [gnw_lat + gnw_ctx, misc]
    payload = jnp.concatenate(rows + [jnp.zeros((PAY_ROWS - len(rows), D), F32)], axis=0)
    return grad_x.reshape(nb, seq, D), gw_in_t, gw_o_ret, gw_o_att, gw_out, payload


def _finish_small(gathered, nb, c_ctx, ret_log2_decay, w_ada16, dev):
    n_dev = gathered.shape[0]
    loc = 3 * D // n_dev
    dmod_all = gathered[:, :3 * nb].reshape(n_dev * nb, 3 * D)
    dmodc_parts = jnp.concatenate([gathered[:, 3 * nb:3 * nb + 2].reshape(n_dev, 2 * D), jnp.zeros((n_dev, D), F32)], axis=1)
    c_all = gathered[:, 3 * nb + 2:4 * nb + 2].reshape(n_dev * nb, D)
    nw_parts = gathered[:, 4 * nb + 2]
    misc_parts = gathered[:, 4 * nb + 3]
    n_rows = n_dev * nb + n_dev
    pad = (-n_rows) % 16
    c_rows = jnp.concatenate([c_all, jnp.broadcast_to(c_ctx.reshape(1, D), (n_dev, D)), jnp.zeros((pad, D), F32)], axis=0)
    dm_rows = jnp.concatenate([dmod_all, dmodc_parts, jnp.zeros((pad, 3 * D), F32)], axis=0)
    dm_loc_rows = lax.dynamic_slice_in_dim(dm_rows, dev * loc, loc, axis=1)
    r_pad = jnp.full((1, D), -1.0, F32).at[:, 2 * HD:2 * HD + 2 * RH].set(ret_log2_decay.reshape(1, 2 * RH))
    gb, gc, gnw, misc, gwa = _small_final(dmod_all, dmodc_parts, c_rows, dm_loc_rows, nw_parts, misc_parts,
                                          c_ctx.reshape(1, D), r_pad, w_ada16)
    return (gb, gc, gnw, misc[:, :HD], misc[:, HD:2 * HD], misc[:, 2 * HD:2 * HD + 2 * RH], gwa,
            misc[0, 2 * HD + 2 * RH])


def kernel(x, c, ctx, c_ctx, norm_w, w_ada, b_ada, w_in, ret_log2_decay, q_norm_w, k_norm_w, w_o_ret, w_o_att, w_out, loss_target, m_c_ctx, m_norm_w, m_w_ada, m_b_ada, m_w_in, m_ret_log2_decay, m_q_norm_w, m_k_norm_w, m_w_o_ret, m_w_o_att, m_w_out, v_c_ctx, v_norm_w, v_w_ada, v_b_ada, v_w_in, v_ret_log2_decay, v_q_norm_w, v_k_norm_w, v_w_o_ret, v_w_o_att, v_w_out):
    nb = x.shape[0]
    dev = 4 * lax.axis_index("x") + 2 * lax.axis_index("y") + lax.axis_index("c")

    w_in_t = jnp.transpose(w_in[0])
    shards = [w_in_t.astype(BF16), w_ada[0].astype(BF16), w_o_ret[0].astype(BF16), w_o_att[0].astype(BF16),
              w_out[0].astype(BF16)]
    g_in, g_ada, g_ret, g_att, g_out = _all_gather(shards, "gather_weights")
    w_in_t16 = g_in.reshape(IN_COLS, D)
    w_ada16 = jnp.transpose(g_ada, (1, 0, 2)).reshape(D, 3 * D)
    w_o_ret16 = g_ret.reshape(RH * DV, D)
    w_o_att16 = g_att.reshape(D, D)
    w_out16 = g_out.reshape(D, D)

    grad_x, gw_in_t, gw_o_ret, gw_o_att, gw_out, payload = _local_step(
        x, c, ctx, c_ctx, norm_w, b_ada, ret_log2_decay, q_norm_w, k_norm_w, loss_target,
        w_ada16, w_in_t16, w_o_ret16, w_o_att16, w_out16)

    (gathered,) = _all_gather([payload], "gather_small")
    gb, gc, gnw, gq, gk, gr, gwa, loss = _finish_small(gathered, nb, c_ctx, ret_log2_decay, w_ada16, dev)

    parts = [gw_in_t.reshape(N_DEV, IN_COLS // N_DEV, D), gw_o_ret.reshape(N_DEV, RH * DV // N_DEV, D),
             gw_o_att.reshape(N_DEV, D // N_DEV, D), gw_out.reshape(N_DEV, D // N_DEV, D)]
    landed = _scatter_blocks(parts, "scatter_grads")
    g_w_in_t, g_w_o_ret, g_w_o_att, g_w_out = [_sum_slots(l, "sum_grads_%d" % i) for i, l in enumerate(landed)]

    grads = [gc.reshape(c_ctx.shape), gnw, gwa[None], gb, g_w_in_t, gr.reshape(ret_log2_decay.shape), gq, gk,
             g_w_o_ret[None], g_w_o_att[None], g_w_out[None]]
    weights = [c_ctx, norm_w, w_ada, b_ada, w_in_t, ret_log2_decay, q_norm_w, k_norm_w, w_o_ret, w_o_att, w_out]
    ms = [m_c_ctx, m_norm_w, m_w_ada, m_b_ada, jnp.transpose(m_w_in[0]), m_ret_log2_decay, m_q_norm_w, m_k_norm_w,
          m_w_o_ret, m_w_o_att, m_w_out]
    vs = [v_c_ctx, v_norm_w, v_w_ada, v_b_ada, jnp.transpose(v_w_in[0]), v_ret_log2_decay, v_q_norm_w, v_k_norm_w,
          v_w_o_ret, v_w_o_att, v_w_out]
    deltas, new_ms, new_vs = [], [], []
    for i, (w, g, m, v) in enumerate(zip(weights, grads, ms, vs)):
        shape2 = (-1, w.shape[-1])
        res = _adamw(w.reshape(shape2), g.reshape(shape2), m.reshape(shape2), v.reshape(shape2), "adamw_%d" % i)
        for lst, r in zip((deltas, new_ms, new_vs), res):
            lst.append(jnp.transpose(r)[None] if i == 4 else r.reshape(w.shape))
    grads[4] = jnp.transpose(g_w_in_t)[None]
    return (loss, grad_x, *grads, *deltas, *new_ms, *new_vs)
```

```python
import numpy as np
import jax
import jax.numpy as jnp
from jax import lax
from jax.experimental import pallas as pl
from jax.experimental.pallas import tpu as pltpu

F32 = jnp.float32
BF16 = jnp.bfloat16

D = 1024
RH, DK, DV, CH = 4, 256, 512, 128
HQ, HKV, HD = 8, 2, 128
GRID_W = 64
ROPE_THETA = 10000.0
EPS = 1e-6
RK, RV, AK, AV, RQ, RG, AQ, AG, MR, MA = 0, 1024, 3072, 3328, 3584, 4608, 6656, 7680, 8704, 9728
IN_COLS = 10752
KV_COLS = 3584
N_DEV = 8
LR, B1, B2, ADAM_EPS, WD, STEP = 0.001, 0.9, 0.999, 1e-08, 0.01, 10
PAY_ROWS = 16
VMEM_LIMIT = 56 * 1024 * 1024
MESH_T = pl.DeviceIdType.MESH

NT = (((1,), (1,)), ((), ()))
TN = (((0,), (0,)), ((), ()))
SM_C = (HD ** -0.5) * float(np.log2(np.e))


def _params(sem):
    return pltpu.CompilerParams(dimension_semantics=sem, vmem_limit_bytes=VMEM_LIMIT)


def _pick(n, target, mult=8):
    best = None
    for t in range(mult, min(n, target) + 1, mult):
        if n % t == 0:
            best = t
    return best or n


def _dot(a, b, dn=None):
    if dn is None:
        return jnp.dot(a, b, preferred_element_type=F32)
    return lax.dot_general(a, b, dn, preferred_element_type=F32)


def _sig(v):
    return jax.nn.sigmoid(v)


def _silu(v):
    return v * _sig(v)


def _dsilu(v):
    s = _sig(v)
    return s * (1.0 + v * (1.0 - s))


def _sds(shape, dtype):
    return jax.ShapeDtypeStruct(shape, dtype)


def _matmul(a, b, *, ta=False, tb=False, tm, tn, tk, out_dtype, name):
    m = a.shape[1] if ta else a.shape[0]
    kdim = a.shape[0] if ta else a.shape[1]
    n = b.shape[0] if tb else b.shape[1]
    tm, tn, tk = _pick(m, tm, 128), _pick(n, tn, 128), _pick(kdim, tk, 128)
    nk = kdim // tk
    dn = (((0 if ta else 1,), (1 if tb else 0,)), ((), ()))

    def body(a_ref, b_ref, o_ref, acc_ref):
        k = pl.program_id(2)
        part = _dot(a_ref[...].astype(BF16), b_ref[...].astype(BF16), dn)
        if nk == 1:
            o_ref[...] = part.astype(o_ref.dtype)
        else:
            @pl.when(k == 0)
            def _():
                acc_ref[...] = part

            @pl.when(k > 0)
            def _():
                acc_ref[...] += part

            @pl.when(k == nk - 1)
            def _():
                o_ref[...] = acc_ref[...].astype(o_ref.dtype)

    a_spec = pl.BlockSpec((tk, tm), lambda i, j, k: (k, i)) if ta else pl.BlockSpec((tm, tk), lambda i, j, k: (i, k))
    b_spec = pl.BlockSpec((tn, tk), lambda i, j, k: (j, k)) if tb else pl.BlockSpec((tk, tn), lambda i, j, k: (k, j))
    return pl.pallas_call(
        body, name=name, grid=(m // tm, n // tn, nk),
        in_specs=[a_spec, b_spec], out_specs=pl.BlockSpec((tm, tn), lambda i, j, k: (i, j)),
        out_shape=_sds((m, n), out_dtype),
        scratch_shapes=[pltpu.VMEM((tm, tn) if nk > 1 else (8, 128), F32)],
        compiler_params=_params(("parallel", "parallel", "arbitrary")),
    )(a, b)


def _log_gamma(r):
    rp = jnp.full((8, 128), -1.0, F32).at[:2, :RH].set(r.reshape(2, RH))

    def body(r_ref, o_ref):
        o_ref[...] = jnp.log1p(-jnp.exp2(r_ref[...]))

    out = pl.pallas_call(body, name="log_gamma", out_shape=_sds((8, 128), F32))(rp)
    return out[:2, :RH]


def _mod_fwd(c8, w_ada16, b_ada):
    def body(c_ref, w_ref, b_ref, o_ref):
        o_ref[...] = _dot(_silu(c_ref[...]).astype(BF16), w_ref[...]) + b_ref[...]

    return pl.pallas_call(
        body, name="mod_fwd", grid=(3,),
        in_specs=[pl.BlockSpec((8, D), lambda j: (0, 0)), pl.BlockSpec((D, D), lambda j: (0, j)),
                  pl.BlockSpec((1, D), lambda j: (0, j))],
        out_specs=pl.BlockSpec((8, D), lambda j: (0, j)), out_shape=_sds((8, 3 * D), F32),
        compiler_params=_params(("arbitrary",)),
    )(c8, w_ada16, b_ada)


def _norm_fwd(x2, mod3, norm_w, rows_all, row_off, rows_per_group, group0, h_prev, tm, name):
    rows = x2.shape[0]
    rb0 = row_off // tm
    bpg = rows_per_group // tm

    def body(*refs):
        x_ref, sh_ref, sc_ref, nw_ref, o_ref = refs[-5:]
        xv = x_ref[...]
        r = lax.rsqrt(jnp.mean(xv * xv, axis=-1, keepdims=True) + EPS)
        o_ref[...] = ((xv * r) * nw_ref[...] * (1.0 + sc_ref[...]) + sh_ref[...]).astype(BF16)

    in_specs = [pl.BlockSpec((tm, D), lambda i: (i, 0)),
                pl.BlockSpec((None, 1, D), lambda i: (group0 + i // bpg, 0, 0)),
                pl.BlockSpec((None, 1, D), lambda i: (group0 + i // bpg, 0, 1)),
                pl.BlockSpec((1, D), lambda i: (0, 0))]
    args = [x2, mod3, mod3, norm_w]
    alias = {}
    if h_prev is not None:
        in_specs.insert(0, pl.BlockSpec(memory_space=pl.ANY))
        args.insert(0, h_prev)
        alias = {0: 0}
    return pl.pallas_call(
        body, name=name, grid=(rows // tm,), in_specs=in_specs,
        out_specs=pl.BlockSpec((tm, D), lambda i: (rb0 + i, 0)), out_shape=_sds((rows_all, D), BF16),
        input_output_aliases=alias, compiler_params=_params(("parallel",)),
    )(*args)


def _decays(lg, fwd):
    ii = lax.broadcasted_iota(jnp.int32, (CH, CH), 0)
    jj = lax.broadcasted_iota(jnp.int32, (CH, CH), 1)
    ri = lax.broadcasted_iota(jnp.int32, (CH, 1), 0).astype(F32)
    rel = (ii - jj) if fwd else (jj - ii)
    relf = jnp.maximum(rel, 0).astype(F32)
    mask = jnp.where(rel >= 0, jnp.exp(lg * relf), 0.0)
    qe = (ri + 1.0) if fwd else (CH - ri)
    ke = (CH - 1.0 - ri) if fwd else ri
    return mask, relf, jnp.exp(lg * qe), qe, jnp.exp(lg * ke), ke


def _wide_specs(rowf):
    return [pl.BlockSpec((CH, 2 * DK), lambda b, c: (rowf(b, c), RQ // (2 * DK))),
            pl.BlockSpec((CH, 2 * DK), lambda b, c: (rowf(b, c), RQ // (2 * DK) + 1)),
            pl.BlockSpec((CH, RH * DK), lambda b, c: (rowf(b, c), RK // (RH * DK))),
            pl.BlockSpec((CH, 2 * DV), lambda b, c: (rowf(b, c), RV // (2 * DV))),
            pl.BlockSpec((CH, 2 * DV), lambda b, c: (rowf(b, c), RV // (2 * DV) + 1))]


def _head_qkv(refs, h):
    q0, q1, k, v0, v1 = refs
    lo = h % 2
    q = (q0, q1)[h // 2][:, lo * DK:(lo + 1) * DK]
    kk = k[:, h * DK:(h + 1) * DK] * (DK ** -0.5)
    v16 = (v0, v1)[h // 2][:, lo * DV:(lo + 1) * DV].astype(BF16)
    return q, kk, v16


def _ctx_state(px, lg, nb, t_rows, cx):
    rb = t_rows // cx

    def body(lg_ref, k_ref, v_ref, sf_ref, sb_ref):
        h = pl.program_id(1)
        pos = lax.broadcasted_iota(jnp.int32, (cx, 1), 0).astype(F32)
        k = k_ref[...] * (DK ** -0.5)
        v16 = v_ref[...].astype(BF16)
        wf = jnp.exp(lg_ref[0, h] * (cx - 1.0 - pos))
        wb = jnp.exp(lg_ref[1, h] * pos)
        sf_ref[...] = _dot((k * wf).astype(BF16), v16, TN)
        sb_ref[...] = _dot((k * wb).astype(BF16), v16, TN)

    st = pl.BlockSpec((None, None, DK, DV), lambda b, h: (b, h, 0, 0))
    return pl.pallas_call(
        body, name="ctx_state", grid=(nb, RH),
        in_specs=[pl.BlockSpec(memory_space=pltpu.SMEM),
                  pl.BlockSpec((cx, DK), lambda b, h: (rb + b, RK // DK + h)),
                  pl.BlockSpec((cx, DV), lambda b, h: (rb + b, RV // DV + h))],
        out_specs=[st, st], out_shape=[_sds((nb, RH, DK, DV), F32)] * 2,
        compiler_params=_params(("parallel", "parallel")),
    )(lg, px, px)


def _ret_fwd(px, lg, s0f, s0b, nb, nc):
    t_rows = nb * nc * CH

    def body(lg_ref, *refs):
        ins = (refs[0:5], refs[5:10])
        s0f_ref, s0b_ref, of_ref, ob_ref, hf_ref, hb_ref, sf, sb = refs[10:]
        c = pl.program_id(1)

        @pl.when(c == 0)
        def _():
            sf[...] = s0f_ref[...]
            sb[...] = s0b_ref[...]

        for d, (o_ref, h_ref, s) in enumerate(((of_ref, hf_ref, sf), (ob_ref, hb_ref, sb))):
            for h in range(RH):
                lg_d = lg_ref[d, h]
                mask, _, qd, _, kd, _ = _decays(lg_d, d == 0)
                q, k, v16 = _head_qkv(ins[d], h)
                a = _dot(q.astype(BF16), k.astype(BF16), NT)
                st = s[h]
                st16 = st.astype(BF16)
                h_ref[h] = st16
                o_ref[:, h * DV:(h + 1) * DV] = _dot((a * mask).astype(BF16), v16) + _dot((q * qd).astype(BF16), st16)
                s[h] = st * jnp.exp(lg_d * CH) + _dot((k * kd).astype(BF16), v16, TN)

    def fw(b, c):
        return b * nc + c

    def bw(b, c):
        return b * nc + nc - 1 - c

    st = pl.BlockSpec((None, RH, DK, DV), lambda b, c: (b, 0, 0, 0))
    in_specs = [pl.BlockSpec(memory_space=pltpu.SMEM)] + _wide_specs(fw) + _wide_specs(bw) + [st, st]
    out_specs = [pl.BlockSpec((CH, RH * DV), lambda b, c: (fw(b, c), 0)),
                 pl.BlockSpec((CH, RH * DV), lambda b, c: (bw(b, c), 0)),
                 pl.BlockSpec((None, None, RH, DK, DV), lambda b, c: (b, c, 0, 0, 0)),
                 pl.BlockSpec((None, None, RH, DK, DV), lambda b, c: (b, nc - 1 - c, 0, 0, 0))]
    return pl.pallas_call(
        body, name="ret_fwd", grid=(nb, nc), in_specs=in_specs, out_specs=out_specs,
        out_shape=[_sds((t_rows, RH * DV), F32)] * 2 + [_sds((nb, nc, RH, DK, DV), BF16)] * 2,
        scratch_shapes=[pltpu.VMEM((RH, DK, DV), F32), pltpu.VMEM((RH, DK, DV), F32)],
        compiler_params=_params(("parallel", "arbitrary")),
    )(lg, *([px] * 10), s0f, s0b)


def _ret_post(o_f, o_b, px, tm):
    t_rows = o_f.shape[0]

    def body(of_ref, ob_ref, g_ref, y_ref):
        o = of_ref[...] + ob_ref[...]
        r = lax.rsqrt(jnp.mean(o * o, axis=-1, keepdims=True) + EPS)
        y_ref[...] = ((o * r) * _silu(g_ref[...])).astype(BF16)

    blk = pl.BlockSpec((tm, DV), lambda i, h: (i, h))
    return pl.pallas_call(
        body, name="ret_post", grid=(t_rows // tm, RH),
        in_specs=[blk, blk, pl.BlockSpec((tm, DV), lambda i, h: (i, RG // DV + h))],
        out_specs=blk, out_shape=_sds((t_rows, RH * DV), BF16),
        compiler_params=_params(("parallel", "parallel")),
    )(o_f, o_b, px)


def _rope_tables(seq):
    rows = seq // GRID_W
    row = np.repeat(np.arange(rows, dtype=np.float32), GRID_W)
    col = np.tile(np.arange(GRID_W, dtype=np.float32), rows)
    half = HD // 2
    freqs = (ROPE_THETA ** (-np.arange(0, half, 2, dtype=np.float32) / half)).astype(np.float32)
    ang = np.concatenate([row[:, None] * freqs, col[:, None] * freqs], axis=-1).astype(np.float32)
    cos = np.repeat(np.cos(ang), 2, axis=-1).astype(np.float32)
    sin = np.repeat(np.sin(ang), 2, axis=-1).astype(np.float32)
    sign = np.tile(np.array([-1.0, 1.0], np.float32), HD // 2)
    return jnp.asarray(cos), jnp.asarray(sin * sign)


def _swap_pairs(v):
    lane = lax.broadcasted_iota(jnp.int32, v.shape, 1)
    return jnp.where((lane & 1) == 0, pltpu.roll(v, HD - 1, 1), pltpu.roll(v, 1, 1))


def _qk_prep(px, nw, cos, sin, rows, row_off, col_off, heads, hb, seq, tm, name):
    rope = cos is not None
    rb0 = row_off // tm
    pb = seq // tm if rope else 1
    bw = hb * HD

    def body(*refs):
        if rope:
            x_ref, w_ref, c_ref, s_ref, o_ref = refs
        else:
            x_ref, w_ref, o_ref = refs
        for h in range(hb):
            sl = slice(h * HD, (h + 1) * HD)
            xv = x_ref[:, sl]
            r = lax.rsqrt(jnp.mean(xv * xv, axis=-1, keepdims=True) + EPS)
            t = (xv * r) * w_ref[...]
            if rope:
                t = t * c_ref[...] + _swap_pairs(t) * s_ref[...]
            o_ref[:, sl] = t.astype(BF16)

    in_specs = [pl.BlockSpec((tm, bw), lambda i, j: (rb0 + i, col_off // bw + j)),
                pl.BlockSpec((1, HD), lambda i, j: (0, 0))]
    args = [px, nw]
    if rope:
        in_specs += [pl.BlockSpec((tm, HD), lambda i, j: (i % pb, 0))] * 2
        args += [cos, sin]
    return pl.pallas_call(
        body, name=name, grid=(rows // tm, heads // hb), in_specs=in_specs,
        out_specs=pl.BlockSpec((tm, bw), lambda i, j: (i, j)), out_shape=_sds((rows, heads * HD), BF16),
        compiler_params=_params(("parallel", "parallel")),
    )(*args)


def _att_fwd(q16, kx16, kc16, px, nb, seq, cx, tq):
    t_rows = nb * seq
    nq = seq // tq
    rep = HQ // HKV
    gw = rep * HD

    def body(q_ref, kx_ref, kc_ref, vx_ref, vc_ref, g_ref, o_ref, y_ref, l_ref):
        kx = kx_ref[...]
        kc = kc_ref[...]
        vx = vx_ref[...].astype(BF16)
        vc = vc_ref[...].astype(BF16)
        l_ref[...] = jnp.zeros_like(l_ref)
        for r in range(rep):
            sl = slice(r * HD, (r + 1) * HD)
            q = q_ref[:, sl]
            s1 = _dot(q, kx, NT)
            s2 = _dot(q, kc, NT)
            m = jnp.maximum(jnp.max(s1, axis=-1, keepdims=True), jnp.max(s2, axis=-1, keepdims=True))
            e1 = jnp.exp2((s1 - m) * SM_C)
            e2 = jnp.exp2((s2 - m) * SM_C)
            tot = jnp.sum(e1, axis=-1, keepdims=True) + jnp.sum(e2, axis=-1, keepdims=True)
            o = (_dot(e1.astype(BF16), vx) + _dot(e2.astype(BF16), vc)) * (1.0 / tot)
            o_ref[:, sl] = o
            y_ref[:, sl] = (o * _silu(g_ref[:, sl])).astype(BF16)
            l_ref[:, r:r + 1] = m * SM_C + jnp.log(tot) * float(np.log2(np.e))

    qblk = pl.BlockSpec((tq, gw), lambda b, g, i: (b * nq + i, g))
    return pl.pallas_call(
        body, name="att_fwd", grid=(nb, HKV, nq),
        in_specs=[qblk,
                  pl.BlockSpec((seq, HD), lambda b, g, i: (b, g)),
                  pl.BlockSpec((cx, HD), lambda b, g, i: (b, g)),
                  pl.BlockSpec((seq, HD), lambda b, g, i: (b, AV // HD + g)),
                  pl.BlockSpec((cx, HD), lambda b, g, i: (t_rows // cx + b, AV // HD + g)),
                  pl.BlockSpec((tq, gw), lambda b, g, i: (b * nq + i, AG // gw + g))],
        out_specs=[qblk, qblk, pl.BlockSpec((tq, 128), lambda b, g, i: (b * nq + i, g))],
        out_shape=[_sds((t_rows, D), F32), _sds((t_rows, D), BF16), _sds((t_rows, HKV * 128), F32)],
        compiler_params=_params(("parallel", "parallel", "parallel")),
    )(q16, kx16, kc16, px, px, px)


def _merge(yret16, yatt16, px, w_o_ret16, w_o_att16, tm):
    t_rows = yret16.shape[0]
    hw = D // 2

    def body(yr_ref, wr_ref, ya_ref, wa_ref, mr_ref, ma_ref, ar_ref, aa_ref, y_ref):
        ar = _dot(yr_ref[...], wr_ref[...])
        aa = _dot(ya_ref[...], wa_ref[...])
        ar_ref[...] = ar
        aa_ref[...] = aa
        y_ref[...] = (_sig(mr_ref[...]) * ar + _sig(ma_ref[...]) * aa).astype(BF16)

    half = pl.BlockSpec((tm, hw), lambda i, j: (i, j))
    return pl.pallas_call(
        body, name="merge", grid=(t_rows // tm, 2),
        in_specs=[pl.BlockSpec((tm, RH * DV), lambda i, j: (i, 0)), pl.BlockSpec((RH * DV, hw), lambda i, j: (0, j)),
                  pl.BlockSpec((tm, D), lambda i, j: (i, 0)), pl.BlockSpec((D, hw), lambda i, j: (0, j)),
                  pl.BlockSpec((tm, hw), lambda i, j: (i, MR // hw + j)),
                  pl.BlockSpec((tm, hw), lambda i, j: (i, MA // hw + j))],
        out_specs=[half, half, half],
        out_shape=[_sds((t_rows, D), F32), _sds((t_rows, D), F32), _sds((t_rows, D), BF16)],
        compiler_params=_params(("parallel", "parallel")),
    )(yret16, w_o_ret16, yatt16, w_o_att16, px, px)


def _outproj(y16, w_out16, x2, tgt, mod3, nb, seq, tm):
    t_rows = nb * seq
    bpb = seq // tm

    def body(y_ref, w_ref, x_ref, t_ref, g_ref, dxn_ref, dout_ref, dg_ref, loss_ref):
        i = pl.program_id(1)
        out = _dot(y_ref[...], w_ref[...])
        gate = g_ref[...]
        diff = x_ref[...] + gate * out - t_ref[...]
        dxn = diff * (1.0 / D)
        dxn_ref[...] = dxn
        dout_ref[...] = (gate * dxn).astype(BF16)
        dg = jnp.sum(dxn * out, axis=0, keepdims=True)
        ls = jnp.broadcast_to(jnp.sum(diff * diff) * (0.5 / D), (1, 128))

        @pl.when(i == 0)
        def _():
            dg_ref[...] = dg
            loss_ref[...] = ls

        @pl.when(i > 0)
        def _():
            dg_ref[...] += dg
            loss_ref[...] += ls

    row = pl.BlockSpec((tm, D), lambda b, i: (b * bpb + i, 0))
    return pl.pallas_call(
        body, name="outproj", grid=(nb, bpb),
        in_specs=[row, pl.BlockSpec((D, D), lambda b, i: (0, 0)), row, row,
                  pl.BlockSpec((None, 1, D), lambda b, i: (b, 0, 2))],
        out_specs=[row, row, pl.BlockSpec((None, 1, D), lambda b, i: (b, 0, 0)),
                   pl.BlockSpec((None, 1, 128), lambda b, i: (b, 0, 0))],
        out_shape=[_sds((t_rows, D), F32), _sds((t_rows, D), BF16), _sds((nb, 1, D), F32), _sds((nb, 1, 128), F32)],
        compiler_params=_params(("parallel", "arbitrary")),
    )(y16, w_out16, x2, tgt, mod3)


def _bwd_merge(dout16, w_out16, px, a_ret, a_att, tm):
    t_rows = dout16.shape[0]
    hw = D // 2

    def body(do_ref, w_ref, mr_ref, ma_ref, ar_ref, aa_ref, dar_ref, daa_ref, dmr_ref, dma_ref):
        dy = _dot(do_ref[...], w_ref[...], NT)
        sr = _sig(mr_ref[...])
        sa = _sig(ma_ref[...])
        dar_ref[...] = (dy * sr).astype(BF16)
        daa_ref[...] = (dy * sa).astype(BF16)
        dmr_ref[...] = (dy * ar_ref[...] * sr * (1.0 - sr)).astype(BF16)
        dma_ref[...] = (dy * aa_ref[...] * sa * (1.0 - sa)).astype(BF16)

    half = pl.BlockSpec((tm, hw), lambda i, j: (i, j))
    return pl.pallas_call(
        body, name="bwd_merge", grid=(t_rows // tm, 2),
        in_specs=[pl.BlockSpec((tm, D), lambda i, j: (i, 0)), pl.BlockSpec((hw, D), lambda i, j: (j, 0)),
                  pl.BlockSpec((tm, hw), lambda i, j: (i, MR // hw + j)),
                  pl.BlockSpec((tm, hw), lambda i, j: (i, MA // hw + j)), half, half],
        out_specs=[half] * 4, out_shape=[_sds((t_rows, D), BF16)] * 4,
        compiler_params=_params(("parallel", "parallel")),
    )(dout16, w_out16, px, px, a_ret, a_att)


def _bwd_branch_ret(da_ret16, w_o_ret16, px, o_f, o_b, tm):
    t_rows = da_ret16.shape[0]

    def body(da_ref, w_ref, g_ref, of_ref, ob_ref, do_ref, dg_ref):
        dy = _dot(da_ref[...], w_ref[...], NT)
        g = g_ref[...]
        o = of_ref[...] + ob_ref[...]
        r = lax.rsqrt(jnp.mean(o * o, axis=-1, keepdims=True) + EPS)
        on = o * r
        don = dy * _silu(g)
        dg_ref[...] = (dy * on * _dsilu(g)).astype(BF16)
        do_ref[...] = (r * (don - on * jnp.mean(on * don, axis=-1, keepdims=True))).astype(BF16)

    blk = pl.BlockSpec((tm, DV), lambda i, h: (i, h))
    return pl.pallas_call(
        body, name="bwd_branch_ret", grid=(t_rows // tm, RH),
        in_specs=[pl.BlockSpec((tm, D), lambda i, h: (i, 0)), pl.BlockSpec((DV, D), lambda i, h: (h, 0)),
                  pl.BlockSpec((tm, DV), lambda i, h: (i, RG // DV + h)), blk, blk],
        out_specs=[blk, blk], out_shape=[_sds((t_rows, RH * DV), BF16)] * 2,
        compiler_params=_params(("parallel", "parallel")),
    )(da_ret16, w_o_ret16, px, o_f, o_b)


def _bwd_branch_att(da_att16, w_o_att16, px, o_att, tm):
    t_rows = da_att16.shape[0]
    hw = D // 2

    def body(da_ref, w_ref, g_ref, o_ref, dao_ref, dg_ref):
        dy = _dot(da_ref[...], w_ref[...], NT)
        g = g_ref[...]
        dao_ref[...] = dy * _silu(g)
        dg_ref[...] = (dy * o_ref[...] * _dsilu(g)).astype(BF16)

    half = pl.BlockSpec((tm, hw), lambda i, j: (i, j))
    return pl.pallas_call(
        body, name="bwd_branch_att", grid=(t_rows // tm, 2),
        in_specs=[pl.BlockSpec((tm, D), lambda i, j: (i, 0)), pl.BlockSpec((hw, D), lambda i, j: (j, 0)),
                  pl.BlockSpec((tm, hw), lambda i, j: (i, AG // hw + j)), half],
        out_specs=[half, half], out_shape=[_sds((t_rows, D), F32), _sds((t_rows, D), BF16)],
        compiler_params=_params(("parallel", "parallel")),
    )(da_att16, w_o_att16, px, o_att)


def _att_bwd(q16, kx16, kc16, px, dao, o_att, lse, nb, seq, cx, tq):
    t_rows = nb * seq
    nq = seq // tq
    rep = HQ // HKV
    gw = rep * HD
    scale = HD ** -0.5

    def body(q_ref, kx_ref, kc_ref, vx_ref, vc_ref, dao_ref, o_ref, l_ref, dq_ref, dkx_ref, dvx_ref, dkc_ref, dvc_ref):
        i = pl.program_id(2)
        kx = kx_ref[...]
        kc = kc_ref[...]
        vx = vx_ref[...].astype(BF16)
        vc = vc_ref[...].astype(BF16)
        dkx = jnp.zeros((seq, HD), F32)
        dvx = jnp.zeros((seq, HD), F32)
        dkc = jnp.zeros((cx, HD), F32)
        dvc = jnp.zeros((cx, HD), F32)
        for r in range(rep):
            sl = slice(r * HD, (r + 1) * HD)
            q = q_ref[:, sl]
            lr = l_ref[:, r:r + 1]
            p1 = jnp.exp2(_dot(q, kx, NT) * SM_C - lr)
            p2 = jnp.exp2(_dot(q, kc, NT) * SM_C - lr)
            da = dao_ref[:, sl]
            da16 = da.astype(BF16)
            delta = jnp.sum(da * o_ref[:, sl], axis=-1, keepdims=True)
            ds1 = (p1 * (_dot(da16, vx, NT) - delta)).astype(BF16)
            ds2 = (p2 * (_dot(da16, vc, NT) - delta)).astype(BF16)
            dq_ref[:, sl] = (_dot(ds1, kx) + _dot(ds2, kc)) * scale
            dkx += _dot(ds1, q, TN)
            dkc += _dot(ds2, q, TN)
            dvx += _dot(p1.astype(BF16), da16, TN)
            dvc += _dot(p2.astype(BF16), da16, TN)
        dkx = dkx * scale
        dkc = dkc * scale

        @pl.when(i == 0)
        def _():
            dkx_ref[...] = dkx
            dvx_ref[...] = dvx
            dkc_ref[...] = dkc
            dvc_ref[...] = dvc

        @pl.when(i > 0)
        def _():
            dkx_ref[...] += dkx
            dvx_ref[...] += dvx
            dkc_ref[...] += dkc
            dvc_ref[...] += dvc

    qblk = pl.BlockSpec((tq, gw), lambda b, g, i: (b * nq + i, g))
    kxb = pl.BlockSpec((None, seq, HD), lambda b, g, i: (b, 0, g))
    kcb = pl.BlockSpec((None, cx, HD), lambda b, g, i: (b, 0, g))
    return pl.pallas_call(
        body, name="att_bwd", grid=(nb, HKV, nq),
        in_specs=[qblk,
                  pl.BlockSpec((seq, HD), lambda b, g, i: (b, g)),
                  pl.BlockSpec((cx, HD), lambda b, g, i: (b, g)),
                  pl.BlockSpec((seq, HD), lambda b, g, i: (b, AV // HD + g)),
                  pl.BlockSpec((cx, HD), lambda b, g, i: (t_rows // cx + b, AV // HD + g)),
                  qblk, qblk, pl.BlockSpec((tq, 128), lambda b, g, i: (b * nq + i, g))],
        out_specs=[qblk, kxb, kxb, kcb, kcb],
        out_shape=[_sds((t_rows, D), F32), _sds((nb, seq, HKV * HD), F32), _sds((nb, seq, HKV * HD), F32),
                   _sds((nb, cx, HKV * HD), F32), _sds((nb, cx, HKV * HD), F32)],
        compiler_params=_params(("parallel", "parallel", "arbitrary")),
    )(q16, kx16, kc16, px, px, dao, o_att, lse)


def _qk_prep_bwd(dt, px, nw, cos, sin, rows, row_off, col_off, heads, hb, seq, tm, name):
    rope = cos is not None
    rb0 = row_off // tm
    pb = seq // tm if rope else 1
    bw = hb * HD

    def body(*refs):
        if rope:
            d_ref, x_ref, w_ref, c_ref, s_ref, dx_ref, dw_ref = refs
        else:
            d_ref, x_ref, w_ref, dx_ref, dw_ref = refs
        first = jnp.logical_and(pl.program_id(0) == 0, pl.program_id(1) == 0)
        dw = jnp.zeros((1, HD), F32)
        for h in range(hb):
            sl = slice(h * HD, (h + 1) * HD)
            dtv = d_ref[:, sl]
            if rope:
                dtv = dtv * c_ref[...] + _swap_pairs(dtv * s_ref[...])
            xv = x_ref[:, sl]
            r = lax.rsqrt(jnp.mean(xv * xv, axis=-1, keepdims=True) + EPS)
            xh = xv * r
            dxh = dtv * w_ref[...]
            dx_ref[:, sl] = (r * (dxh - xh * jnp.mean(dxh * xh, axis=-1, keepdims=True))).astype(BF16)
            dw += jnp.sum(dtv * xh, axis=0, keepdims=True)

        @pl.when(first)
        def _():
            dw_ref[...] = dw

        @pl.when(jnp.logical_not(first))
        def _():
            dw_ref[...] += dw

    blk = pl.BlockSpec((tm, bw), lambda i, j: (i, j))
    in_specs = [blk, pl.BlockSpec((tm, bw), lambda i, j: (rb0 + i, col_off // bw + j)),
                pl.BlockSpec((1, HD), lambda i, j: (0, 0))]
    args = [dt, px, nw]
    if rope:
        in_specs += [pl.BlockSpec((tm, HD), lambda i, j: (i % pb, 0))] * 2
        args += [cos, sin]
    return pl.pallas_call(
        body, name=name, grid=(rows // tm, heads // hb), in_specs=in_specs,
        out_specs=[blk, pl.BlockSpec((1, HD), lambda i, j: (0, 0))],
        out_shape=[_sds((rows, heads * HD), BF16), _sds((1, HD), F32)],
        compiler_params=_params(("arbitrary", "arbitrary")),
    )(*args)


def _ret_bwd(px, lg, do16, hist_f, hist_b, nb, nc):
    t_rows = nb * nc * CH

    def body(lg_ref, *refs):
        ins = (refs[0:5], refs[7:12])
        do_refs = (refs[5], refs[12])
        h_refs = (refs[6], refs[13])
        outs = (refs[14:17], refs[17:20])
        ds_outs = (refs[20], refs[21])
        dlg_ref = refs[22]
        dss = (refs[23], refs[24])
        c = pl.program_id(1)

        @pl.when(c == 0)
        def _():
            dss[0][...] = jnp.zeros_like(dss[0])
            dss[1][...] = jnp.zeros_like(dss[1])
            dlg_ref[...] = jnp.zeros_like(dlg_ref)

        for d in range(2):
            dq_ref, dk_ref, dv_ref = outs[d]
            for h in range(RH):
                lg_d = lg_ref[d, h]
                mask, relf, qd, qe, kd, ke = _decays(lg_d, d == 0)
                g_ch = jnp.exp(lg_d * CH)
                q, k, v16 = _head_qkv(ins[d], h)
                q16 = q.astype(BF16)
                k16 = k.astype(BF16)
                do16v = do_refs[d][:, h * DV:(h + 1) * DV]
                st16 = h_refs[d][h]
                dst = dss[d][h]
                dst16 = dst.astype(BF16)
                a = _dot(q16, k16, NT) * mask
                dp = _dot(do16v, v16, NT)
                da16 = (dp * mask).astype(BF16)
                dq_cross = _dot(do16v, st16, NT) * qd
                dq_ref[:, h * DK:(h + 1) * DK] = _dot(da16, k16) + dq_cross
                dk_state = _dot(v16, dst16, NT) * kd
                dk_ref[:, h * DK:(h + 1) * DK] = (_dot(da16, q16, TN) + dk_state) * (DK ** -0.5)
                dv_ref[:, h * DV:(h + 1) * DV] = _dot(a.astype(BF16), do16v, TN) + _dot((k * kd).astype(BF16), dst16)
                dlg = (jnp.sum(relf * a * dp)
                       + jnp.sum(qe * jnp.sum(q * dq_cross, axis=-1, keepdims=True))
                       + jnp.sum(ke * jnp.sum(k * dk_state, axis=-1, keepdims=True))
                       + CH * g_ch * jnp.sum(dst * st16.astype(F32)))
                row = d * RH + h
                dlg_ref[row:row + 1, :] += jnp.broadcast_to(dlg, (1, 128))
                ds_new = g_ch * dst + _dot((q * qd).astype(BF16), do16v, TN)
                dss[d][h] = ds_new

                @pl.when(c == nc - 1)
                def _():
                    ds_outs[d][h] = ds_new

    def fw(b, c):
        return b * nc + nc - 1 - c

    def bw(b, c):
        return b * nc + c

    def rows(rowf, width):
        return pl.BlockSpec((CH, width), lambda b, c: (rowf(b, c), 0))

    def hist(rowf):
        return pl.BlockSpec((None, None, RH, DK, DV), lambda b, c: (b, rowf(0, c), 0, 0, 0))

    st = pl.BlockSpec((None, RH, DK, DV), lambda b, c: (b, 0, 0, 0))
    in_specs = [pl.BlockSpec(memory_space=pltpu.SMEM)]
    out_specs = []
    for rowf in (fw, bw):
        in_specs += _wide_specs(rowf) + [rows(rowf, RH * DV), hist(rowf)]
        out_specs += [rows(rowf, RH * DK), rows(rowf, RH * DK), rows(rowf, RH * DV)]
    out_specs += [st, st, pl.BlockSpec((None, 8, 128), lambda b, c: (b, 0, 0))]
    qk = _sds((t_rows, RH * DK), F32)
    vv = _sds((t_rows, RH * DV), F32)
    return pl.pallas_call(
        body, name="ret_bwd", grid=(nb, nc), in_specs=in_specs, out_specs=out_specs,
        out_shape=[qk, qk, vv, qk, qk, vv, _sds((nb, RH, DK, DV), F32), _sds((nb, RH, DK, DV), F32),
                   _sds((nb, 8, 128), F32)],
        scratch_shapes=[pltpu.VMEM((RH, DK, DV), F32), pltpu.VMEM((RH, DK, DV), F32)],
        compiler_params=_params(("parallel", "arbitrary")),
    )(lg, *([px] * 5), do16, hist_f, *([px] * 5), do16, hist_b)


def _ctx_state_bwd(px, lg, ds_f, ds_b, nb, t_rows, cx):
    rb = t_rows // cx

    def body(lg_ref, k_ref, v_ref, dsf_ref, dsb_ref, dk_ref, dv_ref, dlg_ref):
        h = pl.program_id(1)
        pos = lax.broadcasted_iota(jnp.int32, (cx, 1), 0).astype(F32)
        k = k_ref[...] * (DK ** -0.5)
        v16 = v_ref[...].astype(BF16)
        dk = jnp.zeros((cx, DK), F32)
        dv = jnp.zeros((cx, DV), F32)
        dlg_ref[...] = jnp.zeros_like(dlg_ref)
        for d, (ds_ref, e) in enumerate(((dsf_ref, cx - 1.0 - pos), (dsb_ref, pos))):
            w = jnp.exp(lg_ref[d, h] * e)
            ds16 = ds_ref[...].astype(BF16)
            t = _dot(v16, ds16, NT)
            dk += t * w
            dv += _dot((k * w).astype(BF16), ds16)
            dlg = jnp.sum(e * w * jnp.sum(k * t, axis=-1, keepdims=True))
            dlg_ref[d:d + 1, :] = jnp.broadcast_to(dlg, (1, 128))
        dk_ref[...] = (dk * (DK ** -0.5)).astype(BF16)
        dv_ref[...] = dv.astype(BF16)

    st = pl.BlockSpec((None, None, DK, DV), lambda b, h: (b, h, 0, 0))
    return pl.pallas_call(
        body, name="ctx_state_bwd", grid=(nb, RH),
        in_specs=[pl.BlockSpec(memory_space=pltpu.SMEM),
                  pl.BlockSpec((cx, DK), lambda b, h: (rb + b, RK // DK + h)),
                  pl.BlockSpec((cx, DV), lambda b, h: (rb + b, RV // DV + h)), st, st],
        out_specs=[pl.BlockSpec((cx, DK), lambda b, h: (b, h)), pl.BlockSpec((cx, DV), lambda b, h: (b, h)),
                   pl.BlockSpec((None, None, 8, 128), lambda b, h: (b, h, 0, 0))],
        out_shape=[_sds((nb * cx, RH * DK), BF16), _sds((nb * cx, RH * DV), BF16), _sds((nb, RH, 8, 128), F32)],
        compiler_params=_params(("parallel", "parallel")),
    )(lg, px, px, ds_f, ds_b)


def _assemble_lat(rows_all, dk_f, dk_b, dv_f, dv_b, dak16, dvx, dq_f, dq_b, drg16, daq16, dag16, dmr16, dma16, tm):
    t_rows = dk_f.shape[0]

    def body(dkf, dkb, dvf, dvb, dak, dav, dqf, dqb, drg, daq, dag, dmr, dma, o_ref):
        o_ref[:, RK:RK + RH * DK] = (dkf[...] + dkb[...]).astype(BF16)
        o_ref[:, RV:RV + RH * DV] = (dvf[...] + dvb[...]).astype(BF16)
        o_ref[:, AK:AK + HKV * HD] = dak[...]
        o_ref[:, AV:AV + HKV * HD] = dav[...].astype(BF16)
        o_ref[:, RQ:RQ + RH * DK] = (dqf[...] + dqb[...]).astype(BF16)
        o_ref[:, RG:RG + RH * DV] = drg[...]
        o_ref[:, AQ:AQ + D] = daq[...]
        o_ref[:, AG:AG + D] = dag[...]
        o_ref[:, MR:MR + D] = dmr[...]
        o_ref[:, MA:MA + D] = dma[...]

    args = (dk_f, dk_b, dv_f, dv_b, dak16, dvx, dq_f, dq_b, drg16, daq16, dag16, dmr16, dma16)
    return pl.pallas_call(
        body, name="assemble_lat", grid=(t_rows // tm,),
        in_specs=[pl.BlockSpec((tm, a.shape[1]), lambda i: (i, 0)) for a in args],
        out_specs=pl.BlockSpec((tm, IN_COLS), lambda i: (i, 0)), out_shape=_sds((rows_all, IN_COLS), BF16),
        compiler_params=_params(("parallel",)),
    )(*args)


def _assemble_ctx(dp_all, dck16, dcv16, dcak16, dvc, t_rows, tm):
    c_rows = dck16.shape[0]
    rb = t_rows // tm

    def body(_, dck, dcv, dcak, dcav, o_ref):
        o_ref[:, RK:RK + RH * DK] = dck[...]
        o_ref[:, RV:RV + RH * DV] = dcv[...]
        o_ref[:, AK:AK + HKV * HD] = dcak[...]
        o_ref[:, AV:AV + HKV * HD] = dcav[...].astype(BF16)
        o_ref[:, KV_COLS:] = jnp.zeros((tm, IN_COLS - KV_COLS), BF16)

    args = (dck16, dcv16, dcak16, dvc)
    return pl.pallas_call(
        body, name="assemble_ctx", grid=(c_rows // tm,),
        in_specs=[pl.BlockSpec(memory_space=pl.ANY)]
        + [pl.BlockSpec((tm, a.shape[1]), lambda i: (i, 0)) for a in args],
        out_specs=pl.BlockSpec((tm, IN_COLS), lambda i: (rb + i, 0)), out_shape=_sds(dp_all.shape, BF16),
        input_output_aliases={0: 0},
        compiler_params=_params(("parallel",)),
    )(dp_all, *args)


def _norm_bwd(dh, x2, mod3, norm_w, dxn, row_off, rows_per_group, group0, tm, name):
    with_dx = dxn is not None
    rows = x2.shape[0]
    rb0 = row_off // tm
    bpg = rows_per_group // tm
    ngroups = rows // rows_per_group

    def body(*refs):
        if with_dx:
            dh_ref, x_ref, sc_ref, nw_ref, dxn_ref, dx_ref, dsh_ref, dsc_ref, dnw_ref = refs
        else:
            dh_ref, x_ref, sc_ref, nw_ref, dsh_ref, dsc_ref, dnw_ref = refs
        i = pl.program_id(0)
        dhv = dh_ref[...]
        xv = x_ref[...]
        nw = nw_ref[...]
        r = lax.rsqrt(jnp.mean(xv * xv, axis=-1, keepdims=True) + EPS)
        xh = xv * r
        dm = dhv * (1.0 + sc_ref[...])
        dsh = jnp.sum(dhv, axis=0, keepdims=True)
        dsc = jnp.sum(dhv * (xh * nw), axis=0, keepdims=True)
        dnw = jnp.sum(dm * xh, axis=0, keepdims=True)
        if with_dx:
            dxh = dm * nw
            dx_ref[...] = dxn_ref[...] + r * (dxh - xh * jnp.mean(dxh * xh, axis=-1, keepdims=True))

        @pl.when(i % bpg == 0)
        def _():
            dsh_ref[...] = dsh
            dsc_ref[...] = dsc

        @pl.when(i % bpg != 0)
        def _():
            dsh_ref[...] += dsh
            dsc_ref[...] += dsc

        @pl.when(i == 0)
        def _():
            dnw_ref[...] = dnw

        @pl.when(i > 0)
        def _():
            dnw_ref[...] += dnw

    grp = pl.BlockSpec((None, 1, D), lambda i: (i // bpg, 0, 0))
    in_specs = [pl.BlockSpec((tm, D), lambda i: (rb0 + i, 0)), pl.BlockSpec((tm, D), lambda i: (i, 0)),
                pl.BlockSpec((None, 1, D), lambda i: (group0 + i // bpg, 0, 1)),
                pl.BlockSpec((1, D), lambda i: (0, 0))]
    args = [dh, x2, mod3, norm_w]
    out_specs = [grp, grp, pl.BlockSpec((1, D), lambda i: (0, 0))]
    out_shape = [_sds((ngroups, 1, D), F32), _sds((ngroups, 1, D), F32), _sds((1, D), F32)]
    if with_dx:
        in_specs.append(pl.BlockSpec((tm, D), lambda i: (i, 0)))
        args.append(dxn)
        out_specs.insert(0, pl.BlockSpec((tm, D), lambda i: (i, 0)))
        out_shape.insert(0, _sds((rows, D), F32))
    return pl.pallas_call(
        body, name=name, grid=(rows // tm,), in_specs=in_specs, out_specs=out_specs, out_shape=out_shape,
        compiler_params=_params(("arbitrary",)),
    )(*args)


def _small_final(dmod_all, dmodc_parts, c_rows, dm_loc_rows, nw_parts, misc_parts, c_ctx, r_pad, w_ada16):
    loc = dm_loc_rows.shape[1]

    def body(dm_ref, dmc_ref, c_ref, dml_ref, nwp_ref, mp_ref, cc_ref, r_ref, w_ref,
             gb_ref, gc_ref, gnw_ref, misc_ref, gwa_ref):
        dmc = jnp.sum(dmc_ref[...], axis=0, keepdims=True)
        gb_ref[...] = jnp.sum(dm_ref[...], axis=0, keepdims=True) + dmc
        dsc = _dot(jnp.broadcast_to(dmc, (8, 3 * D)).astype(BF16), w_ref[...], NT)[0:1, :]
        gc_ref[...] = dsc * _dsilu(cc_ref[...])
        gnw_ref[...] = jnp.sum(nwp_ref[...], axis=0, keepdims=True)
        misc = jnp.sum(mp_ref[...], axis=0, keepdims=True)
        y = jnp.exp2(r_ref[...])
        lane = lax.broadcasted_iota(jnp.int32, (1, D), 1)
        is_decay = jnp.logical_and(lane >= 2 * HD, lane < 2 * HD + 2 * RH)
        misc_ref[...] = misc * jnp.where(is_decay, -(y * np.float32(np.log(2.0))) / (1.0 - y), 1.0)
        gwa_ref[...] = _dot(_silu(c_ref[...]).astype(BF16), dml_ref[...].astype(BF16), TN)

    return pl.pallas_call(
        body, name="small_final",
        out_shape=[_sds((1, 3 * D), F32), _sds((1, D), F32), _sds((1, D), F32), _sds((1, D), F32), _sds((D, loc), F32)],
        compiler_params=pltpu.CompilerParams(vmem_limit_bytes=VMEM_LIMIT),
    )(dmod_all, dmodc_parts, c_rows, dm_loc_rows, nw_parts, misc_parts, c_ctx, r_pad, w_ada16)


def _adamw(w, g, m, v, name):
    rows, cols = w.shape
    tm = _pick(rows, 256, 8)
    bc1 = 1.0 - B1 ** STEP
    bc2 = 1.0 - B2 ** STEP

    def body(w_ref, g_ref, m_ref, v_ref, d_ref, nm_ref, nv_ref):
        g_ = g_ref[...]
        nm = B1 * m_ref[...] + (1.0 - B1) * g_
        nv = B2 * v_ref[...] + (1.0 - B2) * (g_ * g_)
        nm_ref[...] = nm
        nv_ref[...] = nv
        d_ref[...] = -LR * ((nm / bc1) / (jnp.sqrt(nv / bc2) + ADAM_EPS) + WD * w_ref[...])

    blk = pl.BlockSpec((tm, cols), lambda i: (i, 0))
    return pl.pallas_call(
        body, name=name, grid=(rows // tm,), in_specs=[blk] * 4, out_specs=[blk] * 3,
        out_shape=[_sds((rows, cols), F32)] * 3, compiler_params=_params(("parallel",)),
    )(w, g, m, v)


def _mesh_pos():
    return lax.axis_index("x"), lax.axis_index("y"), lax.axis_index("c")


def _all_gather(arrs, name):
    n = len(arrs)

    def body(*refs):
        ins, outs = refs[:n], refs[n:2 * n]
        send_sems, recv_sems, local_sems = refs[2 * n:]
        x, y, c = _mesh_pos()
        me, sib = (x, y, c), (x, y, 1 - c)
        chips = [(1 - x, y), (x, 1 - y), (1 - x, 1 - y)]

        def slot(p):
            return 4 * p[0] + 2 * p[1] + p[2]

        def copy(a, k, block, to, own):
            dst = outs[a].at[slot(block)]
            return pltpu.make_async_remote_copy(
                src_ref=ins[a] if own else dst, dst_ref=dst, send_sem=send_sems.at[a, k], recv_sem=recv_sems.at[a, k],
                device_id=to, device_id_type=MESH_T)

        mine = [pltpu.make_async_copy(ins[a], outs[a].at[slot(me)], local_sems.at[a]) for a in range(n)]
        for cp in mine:
            cp.start()
        first = []
        for a in range(n):
            first.append(copy(a, 0, me, sib, True))
            first += [copy(a, 1 + j, me, (*chip, c), True) for j, chip in enumerate(chips)]
        for cp in first:
            cp.start()
        passed = []
        for j, chip in enumerate(chips):
            for a in range(n):
                copy(a, 1 + j, (*chip, c), me, False).wait_recv()
                fwd = copy(a, 4 + j, (*chip, c), sib, False)
                fwd.start()
                passed.append(fwd)
        for a in range(n):
            copy(a, 0, sib, me, False).wait_recv()
            for j, chip in enumerate(chips):
                copy(a, 4 + j, (*chip, 1 - c), me, False).wait_recv()
        for cp in first + passed:
            cp.wait_send()
        for cp in mine:
            cp.wait()

    hbm = pl.BlockSpec(memory_space=pl.ANY)
    return pl.pallas_call(
        body, name=name, in_specs=[hbm] * n, out_specs=[hbm] * n,
        out_shape=[_sds((N_DEV,) + a.shape, a.dtype) for a in arrs],
        scratch_shapes=[pltpu.SemaphoreType.DMA((n, 7)), pltpu.SemaphoreType.DMA((n, 7)), pltpu.SemaphoreType.DMA((n,))],
    )(*arrs)


def _pair_exchange(arrs, name):
    n = len(arrs)

    def body(*refs):
        ins, outs = refs[:n], refs[n:2 * n]
        send_sems, recv_sems = refs[2 * n:]
        x, y, c = _mesh_pos()
        sib = (x, y, 1 - c)
        sends = []
        for a in range(n):
            for k in range(4):
                sends.append(pltpu.make_async_remote_copy(
                    src_ref=ins[a].at[2 * k + 1 - c], dst_ref=outs[a].at[k], send_sem=send_sems.at[a, k],
                    recv_sem=recv_sems.at[a, k], device_id=sib, device_id_type=MESH_T))
        for cp in sends:
            cp.start()
        for cp in sends:
            cp.wait_recv()
        for cp in sends:
            cp.wait_send()

    hbm = pl.BlockSpec(memory_space=pl.ANY)
    return pl.pallas_call(
        body, name=name, in_specs=[hbm] * n, out_specs=[hbm] * n,
        out_shape=[_sds((4,) + a.shape[1:], a.dtype) for a in arrs],
        scratch_shapes=[pltpu.SemaphoreType.DMA((n, 4)), pltpu.SemaphoreType.DMA((n, 4))],
    )(*arrs)


def _pair_add(part, got, core, name):
    _, rows, cols = part.shape
    tm = _pick(rows, 256, 16)
    p4 = part.reshape(4, 2, rows, cols)

    def body(core_ref, p_ref, g_ref, o_ref):
        o_ref[...] = (p_ref[...].astype(F32) + g_ref[...].astype(F32)).astype(BF16)

    blk = pl.BlockSpec((None, tm, cols), lambda k, i, cr: (k, i, 0))
    return pl.pallas_call(
        body, name=name,
        grid_spec=pltpu.PrefetchScalarGridSpec(
            num_scalar_prefetch=1, grid=(4, rows // tm),
            in_specs=[pl.BlockSpec((None, None, tm, cols), lambda k, i, cr: (k, cr[0], i, 0)), blk], out_specs=blk),
        out_shape=_sds((4, rows, cols), BF16), compiler_params=_params(("parallel", "parallel")),
    )(core, p4, got)


def _chip_sum(pair_sums, landed, chip, name):
    _, rows, cols = pair_sums.shape
    tm = _pick(rows, 256, 16)

    def body(chip_ref, s_ref, l_ref, o_ref):
        acc = s_ref[...].astype(F32)
        for j in range(3):
            acc = acc + l_ref[j].astype(F32)
        o_ref[...] = acc

    return pl.pallas_call(
        body, name=name,
        grid_spec=pltpu.PrefetchScalarGridSpec(
            num_scalar_prefetch=1, grid=(rows // tm,),
            in_specs=[pl.BlockSpec((None, tm, cols), lambda i, ch: (ch[0], i, 0)),
                      pl.BlockSpec((3, tm, cols), lambda i, ch: (0, i, 0))],
            out_specs=pl.BlockSpec((tm, cols), lambda i, ch: (i, 0))),
        out_shape=_sds((rows, cols), F32), compiler_params=_params(("parallel",)),
    )(chip, pair_sums, landed)


_HBM = pl.BlockSpec(memory_space=pltpu.HBM)
_SEM = pl.BlockSpec(memory_space=pltpu.SEMAPHORE)
_EFFECT = pltpu.SideEffectType.DATAFLOW_SIDE_EFFECTING


def _chip_routes(n):
    def plan(x, y, c):
        routes = []
        for a in range(n):
            for j in range(1, 4):
                px, py = x ^ (j >> 1), y ^ (j & 1)
                routes.append((a, 2 * px + py, (px, py, c), j - 1))
        return routes
    return plan, 3 * n


def _bcast_routes(n):
    def plan(x, y, c):
        routes = []
        for a in range(n):
            for k in range(1, N_DEV):
                peer = (x ^ ((k >> 2) & 1), y ^ ((k >> 1) & 1), c ^ (k & 1))
                routes.append((a, 0, peer, 4 * x + 2 * y + c))
        return routes
    return plan, 7 * n


def _route_copies(srcs, lands, send_sems, recv_sems, routes):
    return [pltpu.make_async_remote_copy(
        src_ref=srcs[a].at[sb], dst_ref=lands[a].at[lb], send_sem=send_sems.at[r], recv_sem=recv_sems.at[r],
        device_id=peer, device_id_type=MESH_T) for r, (a, sb, peer, lb) in enumerate(routes)]


def _exchange_start(srcs, lands, routes, name):
    plan, count = routes
    n = len(srcs)

    def body(*refs):
        send_sems, recv_sems = refs[2 * n], refs[2 * n + 1]
        token = refs[-1]
        for cp in _route_copies(refs[:n], refs[n:2 * n], send_sems, recv_sems, plan(*_mesh_pos())):
            cp.start()
        token[...] = jnp.zeros_like(token)

    args = [pltpu.with_memory_space_constraint(a, pltpu.HBM) for a in list(srcs) + list(lands)]
    out = pl.pallas_call(
        body, name=name,
        out_shape=(pltpu.SemaphoreType.DMA((count,)), pltpu.SemaphoreType.DMA((count,)),
                   *[pltpu.HBM(a.shape, a.dtype) for a in args], _sds((8, 128), F32)),
        in_specs=[_HBM] * (2 * n), out_specs=(_SEM, _SEM, *([_HBM] * (2 * n)), pl.BlockSpec(memory_space=pltpu.VMEM)),
        input_output_aliases={i: 2 + i for i in range(2 * n)},
        compiler_params=pltpu.CompilerParams(has_side_effects=_EFFECT),
    )(*args)
    return (out[0], out[1], list(out[2:2 + 2 * n]), routes), out[-1]


def _exchange_wait(state, after, name):
    send_sems, recv_sems, bufs, (plan, count) = state
    n = len(bufs) // 2

    def body(*refs):
        send_s, recv_s = refs[2 * n], refs[2 * n + 1]
        for cp in _route_copies(refs[:n], refs[n:2 * n], send_s, recv_s, plan(*_mesh_pos())):
            cp.wait_send()
            cp.wait_recv()

    out = pl.pallas_call(
        body, name=name, out_shape=tuple(pltpu.HBM(a.shape, a.dtype) for a in bufs),
        in_specs=[_HBM] * (2 * n) + [_SEM, _SEM, pl.BlockSpec(memory_space=pl.ANY)], out_specs=tuple([_HBM] * (2 * n)),
        input_output_aliases={i: i for i in range(2 * n)},
        compiler_params=pltpu.CompilerParams(has_side_effects=_EFFECT),
    )(*bufs, send_sems, recv_sems, after)
    return list(out[:n]), list(out[n:])


def _tie(token, arr):
    return lax.optimization_barrier((token, arr))[1]


def _reduce_scatter_start(parts, core, name):
    got = _pair_exchange(parts, name + "_pair")
    sums = [_pair_add(p, g, core, "%s_add_%d" % (name, i)) for i, (p, g) in enumerate(zip(parts, got))]
    lands = [lax.empty((3,) + s_.shape[1:], BF16) for s_ in sums]
    return _exchange_start(sums, lands, _chip_routes(len(sums)), name + "_start")


def _reduce_scatter_finish(rs_state, after, chip, name):
    sums, landed = _exchange_wait(rs_state, after, name + "_wait")
    return [_chip_sum(s_, l_, chip, "%s_sum_%d" % (name, i)) for i, (s_, l_) in enumerate(zip(sums, landed))]


def _local_step(x, c, ctx, c_ctx, norm_w, b_ada, ret_log2_decay, q_norm_w, k_norm_w, loss_target,
                w_ada16, w_in_t16, get_w_o, on_out_grads, on_in_grad):
    nb, seq, _ = x.shape
    cx = ctx.shape[1]
    t_rows, c_rows = nb * seq, nb * cx
    rows_all = t_rows + c_rows
    nc = seq // CH
    tm = _pick(seq, 256, 128)
    te = _pick(seq, 512, 128)
    assert cx % tm == 0 and t_rows % cx == 0 and seq % GRID_W == 0

    x2 = x.reshape(t_rows, D)
    ctx2 = ctx.reshape(c_rows, D)
    tgt = loss_target.reshape(t_rows, D)
    c8 = jnp.zeros((8, D), F32).at[:nb].set(c).at[nb].set(c_ctx)
    lg = _log_gamma(ret_log2_decay)
    cos, sin = _rope_tables(seq)

    mod = _mod_fwd(c8, w_ada16, b_ada)
    mod3 = mod[:, None, :]
    h_all = _norm_fwd(x2, mod3, norm_w, rows_all, 0, seq, 0, None, te, "norm_fwd")
    h_all = _norm_fwd(ctx2, mod3, norm_w, rows_all, t_rows, c_rows, nb, h_all, tm, "norm_fwd_ctx")
    px = _matmul(h_all, w_in_t16, tb=True, tm=1536, tn=1536, tk=D, out_dtype=F32, name="in_proj")
    s0f, s0b = _ctx_state(px, lg, nb, t_rows, cx)
    o_f, o_b, hist_f, hist_b = _ret_fwd(px, lg, s0f, s0b, nb, nc)
    yret16 = _ret_post(o_f, o_b, px, te)
    q16 = _qk_prep(px, q_norm_w, cos, sin, t_rows, 0, AQ, HQ, 4, seq, te, "q_prep")
    kx16 = _qk_prep(px, k_norm_w, cos, sin, t_rows, 0, AK, HKV, HKV, seq, te, "k_prep")
    kc16 = _qk_prep(px, k_norm_w, None, None, c_rows, t_rows, AK, HKV, HKV, seq, tm, "kc_prep")
    o_att, yatt16, lse = _att_fwd(q16, kx16, kc16, px, nb, seq, cx, tm)
    w_o_ret16, w_o_att16, w_out16 = get_w_o(lse)
    a_ret, a_att, y16 = _merge(yret16, yatt16, px, w_o_ret16, w_o_att16, te)
    dxn, dout16, dgate, loss_b = _outproj(y16, w_out16, x2, tgt, mod3, nb, seq, te)

    gw_out = _matmul(y16, dout16, ta=True, tm=D, tn=D, tk=D, out_dtype=BF16, name="gw_out")
    da_ret16, da_att16, dmr16, dma16 = _bwd_merge(dout16, w_out16, px, a_ret, a_att, te)
    gw_o_ret = _matmul(yret16, da_ret16, ta=True, tm=D, tn=D, tk=D, out_dtype=BF16, name="gw_o_ret")
    gw_o_att = _matmul(yatt16, da_att16, ta=True, tm=D, tn=D, tk=D, out_dtype=BF16, name="gw_o_att")
    out_state, da_ret16 = on_out_grads([gw_o_ret, gw_o_att, gw_out], da_ret16)
    do16, drg16 = _bwd_branch_ret(da_ret16, w_o_ret16, px, o_f, o_b, te)
    dao, dag16 = _bwd_branch_att(da_att16, w_o_att16, px, o_att, te)
    dq_rot, dkx, dvx, dkc, dvc = _att_bwd(q16, kx16, kc16, px, dao, o_att, lse, nb, seq, cx, tm)
    daq16, gq = _qk_prep_bwd(dq_rot, px, q_norm_w, cos, sin, t_rows, 0, AQ, HQ, 4, seq, te, "q_prep_bwd")
    dak16, gk_lat = _qk_prep_bwd(dkx.reshape(t_rows, HKV * HD), px, k_norm_w, cos, sin, t_rows, 0, AK, HKV, HKV, seq, te,
                                 "k_prep_bwd")
    dcak16, gk_ctx = _qk_prep_bwd(dkc.reshape(c_rows, HKV * HD), px, k_norm_w, None, None, c_rows, t_rows, AK, HKV, HKV,
                                  seq, tm, "kc_prep_bwd")
    dq_f, dk_f, dv_f, dq_b, dk_b, dv_b, ds_f, ds_b, dlg_scan = _ret_bwd(px, lg, do16, hist_f, hist_b, nb, nc)
    dck16, dcv16, dlg_ctx = _ctx_state_bwd(px, lg, ds_f, ds_b, nb, t_rows, cx)
    dp_all = _assemble_lat(rows_all, dk_f, dk_b, dv_f, dv_b, dak16, dvx.reshape(t_rows, HKV * HD), dq_f, dq_b, drg16,
                           daq16, dag16, dmr16, dma16, tm)
    dp_all = _assemble_ctx(dp_all, dck16, dcv16, dcak16, dvc.reshape(c_rows, HKV * HD), t_rows, tm)
    gw_in_t = _matmul(dp_all, h_all, ta=True, tm=1536, tn=D, tk=1536, out_dtype=BF16, name="gw_in")
    in_state, dp_all = on_in_grad(gw_in_t, dp_all)
    dh = _matmul(dp_all, w_in_t16, tm=1536, tn=D, tk=1536, out_dtype=F32, name="d_h")
    grad_x, dsh, dsc, gnw_lat = _norm_bwd(dh, x2, mod3, norm_w, dxn, 0, seq, 0, te, "norm_bwd")
    dsh_c, dsc_c, gnw_ctx = _norm_bwd(dh, ctx2, mod3, norm_w, None, t_rows, c_rows, nb, tm, "norm_bwd_ctx")

    dlg = (jnp.sum(dlg_scan[:, :, 0], axis=0) + jnp.sum(dlg_ctx[:, :, :2, 0], axis=0).T.reshape(2 * RH)).reshape(1, 2 * RH)
    misc = jnp.concatenate([gq, gk_lat + gk_ctx, dlg, jnp.sum(loss_b[:, 0, 0]).reshape(1, 1),
                            jnp.zeros((1, D - 2 * HD - 2 * RH - 1), F32)], axis=1)
    rows = []
    for b in range(nb):
        rows += [dsh[b], dsc[b], dgate[b]]
    rows += [dsh_c[0], dsc_c[0]] + [c[b:b + 1] for b in range(nb)] + [gnw_lat + gnw_ctx, misc]
    payload = jnp.concatenate(rows + [jnp.zeros((PAY_ROWS - len(rows), D), F32)], axis=0)
    return grad_x.reshape(nb, seq, D), out_state, in_state, payload


def _finish_small(gathered, nb, c_ctx, ret_log2_decay, w_ada16, dev):
    n_dev = gathered.shape[0]
    loc = 3 * D // n_dev
    dmod_all = gathered[:, :3 * nb].reshape(n_dev * nb, 3 * D)
    dmodc_parts = jnp.concatenate([gathered[:, 3 * nb:3 * nb + 2].reshape(n_dev, 2 * D), jnp.zeros((n_dev, D), F32)], axis=1)
    c_all = gathered[:, 3 * nb + 2:4 * nb + 2].reshape(n_dev * nb, D)
    nw_parts = gathered[:, 4 * nb + 2]
    misc_parts = gathered[:, 4 * nb + 3]
    n_rows = n_dev * nb + n_dev
    pad = (-n_rows) % 16
    c_rows = jnp.concatenate([c_all, jnp.broadcast_to(c_ctx.reshape(1, D), (n_dev, D)), jnp.zeros((pad, D), F32)], axis=0)
    dm_rows = jnp.concatenate([dmod_all, dmodc_parts, jnp.zeros((pad, 3 * D), F32)], axis=0)
    dm_loc_rows = lax.dynamic_slice_in_dim(dm_rows, dev * loc, loc, axis=1)
    r_pad = jnp.full((1, D), -1.0, F32).at[:, 2 * HD:2 * HD + 2 * RH].set(ret_log2_decay.reshape(1, 2 * RH))
    gb, gc, gnw, misc, gwa = _small_final(dmod_all, dmodc_parts, c_rows, dm_loc_rows, nw_parts, misc_parts,
                                          c_ctx.reshape(1, D), r_pad, w_ada16)
    return (gb, gc, gnw, misc[:, :HD], misc[:, HD:2 * HD], misc[:, 2 * HD:2 * HD + 2 * RH], gwa,
            misc[0, 2 * HD + 2 * RH])


def kernel(x, c, ctx, c_ctx, norm_w, w_ada, b_ada, w_in, ret_log2_decay, q_norm_w, k_norm_w, w_o_ret, w_o_att, w_out, loss_target, m_c_ctx, m_norm_w, m_w_ada, m_b_ada, m_w_in, m_ret_log2_decay, m_q_norm_w, m_k_norm_w, m_w_o_ret, m_w_o_att, m_w_out, v_c_ctx, v_norm_w, v_w_ada, v_b_ada, v_w_in, v_ret_log2_decay, v_q_norm_w, v_k_norm_w, v_w_o_ret, v_w_o_att, v_w_out):
    nb = x.shape[0]
    mx, my, mc = _mesh_pos()
    dev = 4 * mx + 2 * my + mc
    core = jnp.reshape(mc, (1,)).astype(jnp.int32)
    chip = jnp.reshape(2 * mx + my, (1,)).astype(jnp.int32)

    w_in_t = jnp.transpose(w_in[0])
    g_in, g_ada = _all_gather([w_in_t.astype(BF16), w_ada[0].astype(BF16)], "gather_weights")
    w_in_t16 = g_in.reshape(IN_COLS, D)
    w_ada16 = jnp.transpose(g_ada, (1, 0, 2)).reshape(D, 3 * D)

    wo_shards = [w_[0].astype(BF16) for w_ in (w_o_ret, w_o_att, w_out)]
    wo_shards = list(lax.optimization_barrier((g_in, *wo_shards))[1:])
    wo_lands = [lax.dynamic_update_slice(lax.empty((N_DEV,) + s_.shape, BF16), s_[None], (dev, 0, 0)) for s_ in wo_shards]
    wo_state, wo_token = _exchange_start([s_[None] for s_ in wo_shards], wo_lands, _bcast_routes(3), "gather_wo_start")
    w_in_t16 = _tie(wo_token, w_in_t16)

    def get_w_o(after):
        _, (l_ret, l_att, l_out) = _exchange_wait(wo_state, after, "gather_wo_wait")
        return l_ret.reshape(RH * DV, D), l_att.reshape(D, D), l_out.reshape(D, D)

    def on_out_grads(grads, x_):
        parts = [g_.reshape(N_DEV, g_.shape[0] // N_DEV, D) for g_ in grads]
        state, token = _reduce_scatter_start(parts, core, "rs_out")
        return state, _tie(token, x_)

    def on_in_grad(grad, x_):
        state, token = _reduce_scatter_start([grad.reshape(N_DEV, IN_COLS // N_DEV, D)], core, "rs_in")
        return state, _tie(token, x_)

    grad_x, out_state, in_state, payload = _local_step(
        x, c, ctx, c_ctx, norm_w, b_ada, ret_log2_decay, q_norm_w, k_norm_w, loss_target,
        w_ada16, w_in_t16, get_w_o, on_out_grads, on_in_grad)

    (gathered,) = _all_gather([payload], "gather_small")
    gb, gc, gnw, gq, gk, gr, gwa, loss = _finish_small(gathered, nb, c_ctx, ret_log2_decay, w_ada16, dev)

    g_w_o_ret, g_w_o_att, g_w_out = _reduce_scatter_finish(out_state, gathered, chip, "rs_out")
    (g_w_in_t,) = _reduce_scatter_finish(in_state, gathered, chip, "rs_in")

    grads = [gc.reshape(c_ctx.shape), gnw, gwa[None], gb, g_w_in_t, gr.reshape(ret_log2_decay.shape), gq, gk,
             g_w_o_ret[None], g_w_o_att[None], g_w_out[None]]
    weights = [c_ctx, norm_w, w_ada, b_ada, w_in_t, ret_log2_decay, q_norm_w, k_norm_w, w_o_ret, w_o_att, w_out]
    ms = [m_c_ctx, m_norm_w, m_w_ada, m_b_ada, jnp.transpose(m_w_in[0]), m_ret_log2_decay, m_q_norm_w, m_k_norm_w,
          m_w_o_ret, m_w_o_att, m_w_out]
    vs = [v_c_ctx, v_norm_w, v_w_ada, v_b_ada, jnp.transpose(v_w_in[0]), v_ret_log2_decay, v_q_norm_w, v_k_norm_w,
          v_w_o_ret, v_w_o_att, v_w_out]
    deltas, new_ms, new_vs = [], [], []
    for i, (w, g, m, v) in enumerate(zip(weights, grads, ms, vs)):
        shape2 = (-1, w.shape[-1])
        res = _adamw(w.reshape(shape2), g.reshape(shape2), m.reshape(shape2), v.reshape(shape2), "adamw_%d" % i)
        for lst, r in zip((deltas, new_ms, new_vs), res):
            lst.append(jnp.transpose(r)[None] if i == 4 else r.reshape(w.shape))
    grads[4] = jnp.transpose(g_w_in_t)[None]
    return (loss, grad_x, *grads, *deltas, *new_ms, *new_vs)
```

```python
import numpy as np
import jax
import jax.numpy as jnp
from jax import lax
from jax.experimental import pallas as pl
from jax.experimental.pallas import tpu as pltpu

F32 = jnp.float32
BF16 = jnp.bfloat16

D = 1024
RH, DK, DV, CH = 4, 256, 512, 128
HQ, HKV, HD = 8, 2, 128
GRID_W = 64
ROPE_THETA = 10000.0
EPS = 1e-6
RK, RV, AK, AV, RQ, RG, AQ, AG, MR, MA = 0, 1024, 3072, 3328, 3584, 4608, 6656, 7680, 8704, 9728
IN_COLS = 10752
KV_COLS = 3584
N_DEV = 8
LR, B1, B2, ADAM_EPS, WD, STEP = 0.001, 0.9, 0.999, 1e-08, 0.01, 10
PAY_ROWS = 16
VMEM_LIMIT = 56 * 1024 * 1024
MESH_T = pl.DeviceIdType.MESH

NT = (((1,), (1,)), ((), ()))
TN = (((0,), (0,)), ((), ()))
SM_C = (HD ** -0.5) * float(np.log2(np.e))


def _params(sem):
    return pltpu.CompilerParams(dimension_semantics=sem, vmem_limit_bytes=VMEM_LIMIT)


def _pick(n, target, mult=8):
    best = None
    for t in range(mult, min(n, target) + 1, mult):
        if n % t == 0:
            best = t
    return best or n


def _dot(a, b, dn=None):
    if dn is None:
        return jnp.dot(a, b, preferred_element_type=F32)
    return lax.dot_general(a, b, dn, preferred_element_type=F32)


def _sig(v):
    return jax.nn.sigmoid(v)


def _silu(v):
    return v * _sig(v)


def _dsilu(v):
    s = _sig(v)
    return s * (1.0 + v * (1.0 - s))


def _sds(shape, dtype):
    return jax.ShapeDtypeStruct(shape, dtype)


def _matmul(a, b, *, ta=False, tb=False, tm, tn, tk, out_dtype, name, after=()):
    m = a.shape[1] if ta else a.shape[0]
    kdim = a.shape[0] if ta else a.shape[1]
    n = b.shape[0] if tb else b.shape[1]
    tm, tn, tk = _pick(m, tm, 128), _pick(n, tn, 128), _pick(kdim, tk, 128)
    nk = kdim // tk
    dn = (((0 if ta else 1,), (1 if tb else 0,)), ((), ()))

    def body(a_ref, b_ref, *rest):
        o_ref, acc_ref = rest[-2:]
        k = pl.program_id(2)
        part = _dot(a_ref[...].astype(BF16), b_ref[...].astype(BF16), dn)
        if nk == 1:
            o_ref[...] = part.astype(o_ref.dtype)
        else:
            @pl.when(k == 0)
            def _():
                acc_ref[...] = part

            @pl.when(k > 0)
            def _():
                acc_ref[...] += part

            @pl.when(k == nk - 1)
            def _():
                o_ref[...] = acc_ref[...].astype(o_ref.dtype)

    a_spec = pl.BlockSpec((tk, tm), lambda i, j, k: (k, i)) if ta else pl.BlockSpec((tm, tk), lambda i, j, k: (i, k))
    b_spec = pl.BlockSpec((tn, tk), lambda i, j, k: (j, k)) if tb else pl.BlockSpec((tk, tn), lambda i, j, k: (k, j))
    return pl.pallas_call(
        body, name=name, grid=(m // tm, n // tn, nk),
        in_specs=[a_spec, b_spec] + [pl.BlockSpec(memory_space=pl.ANY)] * len(after),
        out_specs=pl.BlockSpec((tm, tn), lambda i, j, k: (i, j)), out_shape=_sds((m, n), out_dtype),
        scratch_shapes=[pltpu.VMEM((tm, tn) if nk > 1 else (8, 128), F32)],
        compiler_params=_params(("parallel", "parallel", "arbitrary")),
    )(a, b, *after)


def _log_gamma(r):
    rp = jnp.full((8, 128), -1.0, F32).at[:2, :RH].set(r.reshape(2, RH))

    def body(r_ref, o_ref):
        o_ref[...] = jnp.log1p(-jnp.exp2(r_ref[...]))

    out = pl.pallas_call(body, name="log_gamma", out_shape=_sds((8, 128), F32))(rp)
    return out[:2, :RH]


def _mod_fwd(c8, w_ada16, b_ada):
    def body(c_ref, w_ref, b_ref, o_ref):
        o_ref[...] = _dot(_silu(c_ref[...]).astype(BF16), w_ref[...]) + b_ref[...]

    return pl.pallas_call(
        body, name="mod_fwd", grid=(3,),
        in_specs=[pl.BlockSpec((8, D), lambda j: (0, 0)), pl.BlockSpec((D, D), lambda j: (0, j)),
                  pl.BlockSpec((1, D), lambda j: (0, j))],
        out_specs=pl.BlockSpec((8, D), lambda j: (0, j)), out_shape=_sds((8, 3 * D), F32),
        compiler_params=_params(("arbitrary",)),
    )(c8, w_ada16, b_ada)


def _norm_fwd(x2, mod3, norm_w, rows_all, row_off, rows_per_group, group0, h_prev, tm, name):
    rows = x2.shape[0]
    rb0 = row_off // tm
    bpg = rows_per_group // tm

    def body(*refs):
        x_ref, sh_ref, sc_ref, nw_ref, o_ref = refs[-5:]
        xv = x_ref[...]
        r = lax.rsqrt(jnp.mean(xv * xv, axis=-1, keepdims=True) + EPS)
        o_ref[...] = ((xv * r) * nw_ref[...] * (1.0 + sc_ref[...]) + sh_ref[...]).astype(BF16)

    in_specs = [pl.BlockSpec((tm, D), lambda i: (i, 0)),
                pl.BlockSpec((None, 1, D), lambda i: (group0 + i // bpg, 0, 0)),
                pl.BlockSpec((None, 1, D), lambda i: (group0 + i // bpg, 0, 1)),
                pl.BlockSpec((1, D), lambda i: (0, 0))]
    args = [x2, mod3, mod3, norm_w]
    alias = {}
    if h_prev is not None:
        in_specs.insert(0, pl.BlockSpec(memory_space=pl.ANY))
        args.insert(0, h_prev)
        alias = {0: 0}
    return pl.pallas_call(
        body, name=name, grid=(rows // tm,), in_specs=in_specs,
        out_specs=pl.BlockSpec((tm, D), lambda i: (rb0 + i, 0)), out_shape=_sds((rows_all, D), BF16),
        input_output_aliases=alias, compiler_params=_params(("parallel",)),
    )(*args)


def _decays(lg, fwd):
    ii = lax.broadcasted_iota(jnp.int32, (CH, CH), 0)
    jj = lax.broadcasted_iota(jnp.int32, (CH, CH), 1)
    ri = lax.broadcasted_iota(jnp.int32, (CH, 1), 0).astype(F32)
    rel = (ii - jj) if fwd else (jj - ii)
    relf = jnp.maximum(rel, 0).astype(F32)
    mask = jnp.where(rel >= 0, jnp.exp(lg * relf), 0.0)
    qe = (ri + 1.0) if fwd else (CH - ri)
    ke = (CH - 1.0 - ri) if fwd else ri
    return mask, relf, jnp.exp(lg * qe), qe, jnp.exp(lg * ke), ke


def _wide_specs(rowf):
    return [pl.BlockSpec((CH, 2 * DK), lambda b, c: (rowf(b, c), RQ // (2 * DK))),
            pl.BlockSpec((CH, 2 * DK), lambda b, c: (rowf(b, c), RQ // (2 * DK) + 1)),
            pl.BlockSpec((CH, RH * DK), lambda b, c: (rowf(b, c), RK // (RH * DK))),
            pl.BlockSpec((CH, 2 * DV), lambda b, c: (rowf(b, c), RV // (2 * DV))),
            pl.BlockSpec((CH, 2 * DV), lambda b, c: (rowf(b, c), RV // (2 * DV) + 1))]


def _head_qkv(refs, h):
    q0, q1, k, v0, v1 = refs
    lo = h % 2
    q = (q0, q1)[h // 2][:, lo * DK:(lo + 1) * DK]
    kk = k[:, h * DK:(h + 1) * DK] * (DK ** -0.5)
    v16 = (v0, v1)[h // 2][:, lo * DV:(lo + 1) * DV].astype(BF16)
    return q, kk, v16


def _ctx_state(px, lg, nb, t_rows, cx):
    rb = t_rows // cx

    def body(lg_ref, k_ref, v_ref, sf_ref, sb_ref):
        h = pl.program_id(1)
        pos = lax.broadcasted_iota(jnp.int32, (cx, 1), 0).astype(F32)
        k = k_ref[...] * (DK ** -0.5)
        v16 = v_ref[...].astype(BF16)
        wf = jnp.exp(lg_ref[0, h] * (cx - 1.0 - pos))
        wb = jnp.exp(lg_ref[1, h] * pos)
        sf_ref[...] = _dot((k * wf).astype(BF16), v16, TN)
        sb_ref[...] = _dot((k * wb).astype(BF16), v16, TN)

    st = pl.BlockSpec((None, None, DK, DV), lambda b, h: (b, h, 0, 0))
    return pl.pallas_call(
        body, name="ctx_state", grid=(nb, RH),
        in_specs=[pl.BlockSpec(memory_space=pltpu.SMEM),
                  pl.BlockSpec((cx, DK), lambda b, h: (rb + b, RK // DK + h)),
                  pl.BlockSpec((cx, DV), lambda b, h: (rb + b, RV // DV + h))],
        out_specs=[st, st], out_shape=[_sds((nb, RH, DK, DV), F32)] * 2,
        compiler_params=_params(("parallel", "parallel")),
    )(lg, px, px)


def _ret_fwd(px, lg, s0f, s0b, nb, nc):
    t_rows = nb * nc * CH

    def body(lg_ref, *refs):
        ins = (refs[0:5], refs[5:10])
        s0f_ref, s0b_ref, of_ref, ob_ref, hf_ref, hb_ref, sf, sb = refs[10:]
        c = pl.program_id(1)

        @pl.when(c == 0)
        def _():
            sf[...] = s0f_ref[...]
            sb[...] = s0b_ref[...]

        for d, (o_ref, h_ref, s) in enumerate(((of_ref, hf_ref, sf), (ob_ref, hb_ref, sb))):
            for h in range(RH):
                lg_d = lg_ref[d, h]
                mask, _, qd, _, kd, _ = _decays(lg_d, d == 0)
                q, k, v16 = _head_qkv(ins[d], h)
                a = _dot(q.astype(BF16), k.astype(BF16), NT)
                st = s[h]
                st16 = st.astype(BF16)
                h_ref[h] = st16
                o_ref[:, h * DV:(h + 1) * DV] = _dot((a * mask).astype(BF16), v16) + _dot((q * qd).astype(BF16), st16)
                s[h] = st * jnp.exp(lg_d * CH) + _dot((k * kd).astype(BF16), v16, TN)

    def fw(b, c):
        return b * nc + c

    def bw(b, c):
        return b * nc + nc - 1 - c

    st = pl.BlockSpec((None, RH, DK, DV), lambda b, c: (b, 0, 0, 0))
    in_specs = [pl.BlockSpec(memory_space=pltpu.SMEM)] + _wide_specs(fw) + _wide_specs(bw) + [st, st]
    out_specs = [pl.BlockSpec((CH, RH * DV), lambda b, c: (fw(b, c), 0)),
                 pl.BlockSpec((CH, RH * DV), lambda b, c: (bw(b, c), 0)),
                 pl.BlockSpec((None, None, RH, DK, DV), lambda b, c: (b, c, 0, 0, 0)),
                 pl.BlockSpec((None, None, RH, DK, DV), lambda b, c: (b, nc - 1 - c, 0, 0, 0))]
    return pl.pallas_call(
        body, name="ret_fwd", grid=(nb, nc), in_specs=in_specs, out_specs=out_specs,
        out_shape=[_sds((t_rows, RH * DV), F32)] * 2 + [_sds((nb, nc, RH, DK, DV), BF16)] * 2,
        scratch_shapes=[pltpu.VMEM((RH, DK, DV), F32), pltpu.VMEM((RH, DK, DV), F32)],
        compiler_params=_params(("parallel", "arbitrary")),
    )(lg, *([px] * 10), s0f, s0b)


def _ret_post(o_f, o_b, px, tm):
    t_rows = o_f.shape[0]

    def body(of_ref, ob_ref, g_ref, y_ref):
        o = of_ref[...] + ob_ref[...]
        r = lax.rsqrt(jnp.mean(o * o, axis=-1, keepdims=True) + EPS)
        y_ref[...] = ((o * r) * _silu(g_ref[...])).astype(BF16)

    blk = pl.BlockSpec((tm, DV), lambda i, h: (i, h))
    return pl.pallas_call(
        body, name="ret_post", grid=(t_rows // tm, RH),
        in_specs=[blk, blk, pl.BlockSpec((tm, DV), lambda i, h: (i, RG // DV + h))],
        out_specs=blk, out_shape=_sds((t_rows, RH * DV), BF16),
        compiler_params=_params(("parallel", "parallel")),
    )(o_f, o_b, px)


def _rope_tables(seq):
    rows = seq // GRID_W
    row = np.repeat(np.arange(rows, dtype=np.float32), GRID_W)
    col = np.tile(np.arange(GRID_W, dtype=np.float32), rows)
    half = HD // 2
    freqs = (ROPE_THETA ** (-np.arange(0, half, 2, dtype=np.float32) / half)).astype(np.float32)
    ang = np.concatenate([row[:, None] * freqs, col[:, None] * freqs], axis=-1).astype(np.float32)
    cos = np.repeat(np.cos(ang), 2, axis=-1).astype(np.float32)
    sin = np.repeat(np.sin(ang), 2, axis=-1).astype(np.float32)
    sign = np.tile(np.array([-1.0, 1.0], np.float32), HD // 2)
    return jnp.asarray(cos), jnp.asarray(sin * sign)


def _swap_pairs(v):
    lane = lax.broadcasted_iota(jnp.int32, v.shape, 1)
    return jnp.where((lane & 1) == 0, pltpu.roll(v, HD - 1, 1), pltpu.roll(v, 1, 1))


def _qk_prep(px, nw, cos, sin, rows, row_off, col_off, heads, hb, seq, tm, name):
    rope = cos is not None
    rb0 = row_off // tm
    pb = seq // tm if rope else 1
    bw = hb * HD

    def body(*refs):
        if rope:
            x_ref, w_ref, c_ref, s_ref, o_ref = refs
        else:
            x_ref, w_ref, o_ref = refs
        for h in range(hb):
            sl = slice(h * HD, (h + 1) * HD)
            xv = x_ref[:, sl]
            r = lax.rsqrt(jnp.mean(xv * xv, axis=-1, keepdims=True) + EPS)
            t = (xv * r) * w_ref[...]
            if rope:
                t = t * c_ref[...] + _swap_pairs(t) * s_ref[...]
            o_ref[:, sl] = t.astype(BF16)

    in_specs = [pl.BlockSpec((tm, bw), lambda i, j: (rb0 + i, col_off // bw + j)),
                pl.BlockSpec((1, HD), lambda i, j: (0, 0))]
    args = [px, nw]
    if rope:
        in_specs += [pl.BlockSpec((tm, HD), lambda i, j: (i % pb, 0))] * 2
        args += [cos, sin]
    return pl.pallas_call(
        body, name=name, grid=(rows // tm, heads // hb), in_specs=in_specs,
        out_specs=pl.BlockSpec((tm, bw), lambda i, j: (i, j)), out_shape=_sds((rows, heads * HD), BF16),
        compiler_params=_params(("parallel", "parallel")),
    )(*args)


def _att_fwd(q16, kx16, kc16, px, nb, seq, cx, tq):
    t_rows = nb * seq
    nq = seq // tq
    rep = HQ // HKV
    gw = rep * HD

    def body(q_ref, kx_ref, kc_ref, vx_ref, vc_ref, g_ref, o_ref, y_ref, l_ref):
        kx = kx_ref[...]
        kc = kc_ref[...]
        vx = vx_ref[...].astype(BF16)
        vc = vc_ref[...].astype(BF16)
        l_ref[...] = jnp.zeros_like(l_ref)
        for r in range(rep):
            sl = slice(r * HD, (r + 1) * HD)
            q = q_ref[:, sl]
            s1 = _dot(q, kx, NT)
            s2 = _dot(q, kc, NT)
            m = jnp.maximum(jnp.max(s1, axis=-1, keepdims=True), jnp.max(s2, axis=-1, keepdims=True))
            e1 = jnp.exp2((s1 - m) * SM_C)
            e2 = jnp.exp2((s2 - m) * SM_C)
            tot = jnp.sum(e1, axis=-1, keepdims=True) + jnp.sum(e2, axis=-1, keepdims=True)
            o = (_dot(e1.astype(BF16), vx) + _dot(e2.astype(BF16), vc)) * (1.0 / tot)
            o_ref[:, sl] = o
            y_ref[:, sl] = (o * _silu(g_ref[:, sl])).astype(BF16)
            l_ref[:, r:r + 1] = m * SM_C + jnp.log(tot) * float(np.log2(np.e))

    qblk = pl.BlockSpec((tq, gw), lambda b, g, i: (b * nq + i, g))
    return pl.pallas_call(
        body, name="att_fwd", grid=(nb, HKV, nq),
        in_specs=[qblk,
                  pl.BlockSpec((seq, HD), lambda b, g, i: (b, g)),
                  pl.BlockSpec((cx, HD), lambda b, g, i: (b, g)),
                  pl.BlockSpec((seq, HD), lambda b, g, i: (b, AV // HD + g)),
                  pl.BlockSpec((cx, HD), lambda b, g, i: (t_rows // cx + b, AV // HD + g)),
                  pl.BlockSpec((tq, gw), lambda b, g, i: (b * nq + i, AG // gw + g))],
        out_specs=[qblk, qblk, pl.BlockSpec((tq, 128), lambda b, g, i: (b * nq + i, g))],
        out_shape=[_sds((t_rows, D), F32), _sds((t_rows, D), BF16), _sds((t_rows, HKV * 128), F32)],
        compiler_params=_params(("parallel", "parallel", "parallel")),
    )(q16, kx16, kc16, px, px, px)


def _merge(yret16, yatt16, px, w_o_ret16, w_o_att16, tm):
    t_rows = yret16.shape[0]
    hw = D // 2

    def body(yr_ref, wr_ref, ya_ref, wa_ref, mr_ref, ma_ref, ar_ref, aa_ref, y_ref):
        ar = _dot(yr_ref[...], wr_ref[...])
        aa = _dot(ya_ref[...], wa_ref[...])
        ar_ref[...] = ar
        aa_ref[...] = aa
        y_ref[...] = (_sig(mr_ref[...]) * ar + _sig(ma_ref[...]) * aa).astype(BF16)

    half = pl.BlockSpec((tm, hw), lambda i, j: (i, j))
    return pl.pallas_call(
        body, name="merge", grid=(t_rows // tm, 2),
        in_specs=[pl.BlockSpec((tm, RH * DV), lambda i, j: (i, 0)), pl.BlockSpec((RH * DV, hw), lambda i, j: (0, j)),
                  pl.BlockSpec((tm, D), lambda i, j: (i, 0)), pl.BlockSpec((D, hw), lambda i, j: (0, j)),
                  pl.BlockSpec((tm, hw), lambda i, j: (i, MR // hw + j)),
                  pl.BlockSpec((tm, hw), lambda i, j: (i, MA // hw + j))],
        out_specs=[half, half, half],
        out_shape=[_sds((t_rows, D), F32), _sds((t_rows, D), F32), _sds((t_rows, D), BF16)],
        compiler_params=_params(("parallel", "parallel")),
    )(yret16, w_o_ret16, yatt16, w_o_att16, px, px)


def _outproj(y16, w_out16, x2, tgt, mod3, nb, seq, tm):
    t_rows = nb * seq
    bpb = seq // tm

    def body(y_ref, w_ref, x_ref, t_ref, g_ref, dxn_ref, dout_ref, dg_ref, loss_ref):
        i = pl.program_id(1)
        out = _dot(y_ref[...], w_ref[...])
        gate = g_ref[...]
        diff = x_ref[...] + gate * out - t_ref[...]
        dxn = diff * (1.0 / D)
        dxn_ref[...] = dxn
        dout_ref[...] = (gate * dxn).astype(BF16)
        dg = jnp.sum(dxn * out, axis=0, keepdims=True)
        ls = jnp.broadcast_to(jnp.sum(diff * diff) * (0.5 / D), (1, 128))

        @pl.when(i == 0)
        def _():
            dg_ref[...] = dg
            loss_ref[...] = ls

        @pl.when(i > 0)
        def _():
            dg_ref[...] += dg
            loss_ref[...] += ls

    row = pl.BlockSpec((tm, D), lambda b, i: (b * bpb + i, 0))
    return pl.pallas_call(
        body, name="outproj", grid=(nb, bpb),
        in_specs=[row, pl.BlockSpec((D, D), lambda b, i: (0, 0)), row, row,
                  pl.BlockSpec((None, 1, D), lambda b, i: (b, 0, 2))],
        out_specs=[row, row, pl.BlockSpec((None, 1, D), lambda b, i: (b, 0, 0)),
                   pl.BlockSpec((None, 1, 128), lambda b, i: (b, 0, 0))],
        out_shape=[_sds((t_rows, D), F32), _sds((t_rows, D), BF16), _sds((nb, 1, D), F32), _sds((nb, 1, 128), F32)],
        compiler_params=_params(("parallel", "arbitrary")),
    )(y16, w_out16, x2, tgt, mod3)


def _bwd_merge(dout16, w_out16, px, a_ret, a_att, tm):
    t_rows = dout16.shape[0]
    hw = D // 2

    def body(do_ref, w_ref, mr_ref, ma_ref, ar_ref, aa_ref, dar_ref, daa_ref, dmr_ref, dma_ref):
        dy = _dot(do_ref[...], w_ref[...], NT)
        sr = _sig(mr_ref[...])
        sa = _sig(ma_ref[...])
        dar_ref[...] = (dy * sr).astype(BF16)
        daa_ref[...] = (dy * sa).astype(BF16)
        dmr_ref[...] = (dy * ar_ref[...] * sr * (1.0 - sr)).astype(BF16)
        dma_ref[...] = (dy * aa_ref[...] * sa * (1.0 - sa)).astype(BF16)

    half = pl.BlockSpec((tm, hw), lambda i, j: (i, j))
    return pl.pallas_call(
        body, name="bwd_merge", grid=(t_rows // tm, 2),
        in_specs=[pl.BlockSpec((tm, D), lambda i, j: (i, 0)), pl.BlockSpec((hw, D), lambda i, j: (j, 0)),
                  pl.BlockSpec((tm, hw), lambda i, j: (i, MR // hw + j)),
                  pl.BlockSpec((tm, hw), lambda i, j: (i, MA // hw + j)), half, half],
        out_specs=[half] * 4, out_shape=[_sds((t_rows, D), BF16)] * 4,
        compiler_params=_params(("parallel", "parallel")),
    )(dout16, w_out16, px, px, a_ret, a_att)


def _bwd_branch_ret(da_ret16, w_o_ret16, px, o_f, o_b, tm, after=()):
    t_rows = da_ret16.shape[0]

    def body(da_ref, w_ref, g_ref, of_ref, ob_ref, *rest):
        do_ref, dg_ref = rest[-2:]
        dy = _dot(da_ref[...], w_ref[...], NT)
        g = g_ref[...]
        o = of_ref[...] + ob_ref[...]
        r = lax.rsqrt(jnp.mean(o * o, axis=-1, keepdims=True) + EPS)
        on = o * r
        don = dy * _silu(g)
        dg_ref[...] = (dy * on * _dsilu(g)).astype(BF16)
        do_ref[...] = (r * (don - on * jnp.mean(on * don, axis=-1, keepdims=True))).astype(BF16)

    blk = pl.BlockSpec((tm, DV), lambda i, h: (i, h))
    return pl.pallas_call(
        body, name="bwd_branch_ret", grid=(t_rows // tm, RH),
        in_specs=[pl.BlockSpec((tm, D), lambda i, h: (i, 0)), pl.BlockSpec((DV, D), lambda i, h: (h, 0)),
                  pl.BlockSpec((tm, DV), lambda i, h: (i, RG // DV + h)), blk, blk]
        + [pl.BlockSpec(memory_space=pl.ANY)] * len(after),
        out_specs=[blk, blk], out_shape=[_sds((t_rows, RH * DV), BF16)] * 2,
        compiler_params=_params(("parallel", "parallel")),
    )(da_ret16, w_o_ret16, px, o_f, o_b, *after)


def _bwd_branch_att(da_att16, w_o_att16, px, o_att, tm):
    t_rows = da_att16.shape[0]
    hw = D // 2

    def body(da_ref, w_ref, g_ref, o_ref, dao_ref, dg_ref):
        dy = _dot(da_ref[...], w_ref[...], NT)
        g = g_ref[...]
        dao_ref[...] = dy * _silu(g)
        dg_ref[...] = (dy * o_ref[...] * _dsilu(g)).astype(BF16)

    half = pl.BlockSpec((tm, hw), lambda i, j: (i, j))
    return pl.pallas_call(
        body, name="bwd_branch_att", grid=(t_rows // tm, 2),
        in_specs=[pl.BlockSpec((tm, D), lambda i, j: (i, 0)), pl.BlockSpec((hw, D), lambda i, j: (j, 0)),
                  pl.BlockSpec((tm, hw), lambda i, j: (i, AG // hw + j)), half],
        out_specs=[half, half], out_shape=[_sds((t_rows, D), F32), _sds((t_rows, D), BF16)],
        compiler_params=_params(("parallel", "parallel")),
    )(da_att16, w_o_att16, px, o_att)


def _att_bwd(q16, kx16, kc16, px, dao, o_att, lse, nb, seq, cx, tq):
    t_rows = nb * seq
    nq = seq // tq
    rep = HQ // HKV
    gw = rep * HD
    scale = HD ** -0.5

    def body(q_ref, kx_ref, kc_ref, vx_ref, vc_ref, dao_ref, o_ref, l_ref, dq_ref, dkx_ref, dvx_ref, dkc_ref, dvc_ref):
        i = pl.program_id(2)
        kx = kx_ref[...]
        kc = kc_ref[...]
        vx = vx_ref[...].astype(BF16)
        vc = vc_ref[...].astype(BF16)
        dkx = jnp.zeros((seq, HD), F32)
        dvx = jnp.zeros((seq, HD), F32)
        dkc = jnp.zeros((cx, HD), F32)
        dvc = jnp.zeros((cx, HD), F32)
        for r in range(rep):
            sl = slice(r * HD, (r + 1) * HD)
            q = q_ref[:, sl]
            lr = l_ref[:, r:r + 1]
            p1 = jnp.exp2(_dot(q, kx, NT) * SM_C - lr)
            p2 = jnp.exp2(_dot(q, kc, NT) * SM_C - lr)
            da = dao_ref[:, sl]
            da16 = da.astype(BF16)
            delta = jnp.sum(da * o_ref[:, sl], axis=-1, keepdims=True)
            ds1 = (p1 * (_dot(da16, vx, NT) - delta)).astype(BF16)
            ds2 = (p2 * (_dot(da16, vc, NT) - delta)).astype(BF16)
            dq_ref[:, sl] = (_dot(ds1, kx) + _dot(ds2, kc)) * scale
            dkx += _dot(ds1, q, TN)
            dkc += _dot(ds2, q, TN)
            dvx += _dot(p1.astype(BF16), da16, TN)
            dvc += _dot(p2.astype(BF16), da16, TN)
        dkx = dkx * scale
        dkc = dkc * scale

        @pl.when(i == 0)
        def _():
            dkx_ref[...] = dkx
            dvx_ref[...] = dvx
            dkc_ref[...] = dkc
            dvc_ref[...] = dvc

        @pl.when(i > 0)
        def _():
            dkx_ref[...] += dkx
            dvx_ref[...] += dvx
            dkc_ref[...] += dkc
            dvc_ref[...] += dvc

    qblk = pl.BlockSpec((tq, gw), lambda b, g, i: (b * nq + i, g))
    kxb = pl.BlockSpec((None, seq, HD), lambda b, g, i: (b, 0, g))
    kcb = pl.BlockSpec((None, cx, HD), lambda b, g, i: (b, 0, g))
    return pl.pallas_call(
        body, name="att_bwd", grid=(nb, HKV, nq),
        in_specs=[qblk,
                  pl.BlockSpec((seq, HD), lambda b, g, i: (b, g)),
                  pl.BlockSpec((cx, HD), lambda b, g, i: (b, g)),
                  pl.BlockSpec((seq, HD), lambda b, g, i: (b, AV // HD + g)),
                  pl.BlockSpec((cx, HD), lambda b, g, i: (t_rows // cx + b, AV // HD + g)),
                  qblk, qblk, pl.BlockSpec((tq, 128), lambda b, g, i: (b * nq + i, g))],
        out_specs=[qblk, kxb, kxb, kcb, kcb],
        out_shape=[_sds((t_rows, D), F32), _sds((nb, seq, HKV * HD), F32), _sds((nb, seq, HKV * HD), F32),
                   _sds((nb, cx, HKV * HD), F32), _sds((nb, cx, HKV * HD), F32)],
        compiler_params=_params(("parallel", "parallel", "arbitrary")),
    )(q16, kx16, kc16, px, px, dao, o_att, lse)


def _qk_prep_bwd(dt, px, nw, cos, sin, rows, row_off, col_off, heads, hb, seq, tm, name):
    rope = cos is not None
    rb0 = row_off // tm
    pb = seq // tm if rope else 1
    bw = hb * HD

    def body(*refs):
        if rope:
            d_ref, x_ref, w_ref, c_ref, s_ref, dx_ref, dw_ref = refs
        else:
            d_ref, x_ref, w_ref, dx_ref, dw_ref = refs
        first = jnp.logical_and(pl.program_id(0) == 0, pl.program_id(1) == 0)
        dw = jnp.zeros((1, HD), F32)
        for h in range(hb):
            sl = slice(h * HD, (h + 1) * HD)
            dtv = d_ref[:, sl]
            if rope:
                dtv = dtv * c_ref[...] + _swap_pairs(dtv * s_ref[...])
            xv = x_ref[:, sl]
            r = lax.rsqrt(jnp.mean(xv * xv, axis=-1, keepdims=True) + EPS)
            xh = xv * r
            dxh = dtv * w_ref[...]
            dx_ref[:, sl] = (r * (dxh - xh * jnp.mean(dxh * xh, axis=-1, keepdims=True))).astype(BF16)
            dw += jnp.sum(dtv * xh, axis=0, keepdims=True)

        @pl.when(first)
        def _():
            dw_ref[...] = dw

        @pl.when(jnp.logical_not(first))
        def _():
            dw_ref[...] += dw

    blk = pl.BlockSpec((tm, bw), lambda i, j: (i, j))
    in_specs = [blk, pl.BlockSpec((tm, bw), lambda i, j: (rb0 + i, col_off // bw + j)),
                pl.BlockSpec((1, HD), lambda i, j: (0, 0))]
    args = [dt, px, nw]
    if rope:
        in_specs += [pl.BlockSpec((tm, HD), lambda i, j: (i % pb, 0))] * 2
        args += [cos, sin]
    return pl.pallas_call(
        body, name=name, grid=(rows // tm, heads // hb), in_specs=in_specs,
        out_specs=[blk, pl.BlockSpec((1, HD), lambda i, j: (0, 0))],
        out_shape=[_sds((rows, heads * HD), BF16), _sds((1, HD), F32)],
        compiler_params=_params(("arbitrary", "arbitrary")),
    )(*args)


def _ret_bwd(px, lg, do16, hist_f, hist_b, nb, nc):
    t_rows = nb * nc * CH

    def body(lg_ref, *refs):
        ins = (refs[0:5], refs[7:12])
        do_refs = (refs[5], refs[12])
        h_refs = (refs[6], refs[13])
        outs = (refs[14:17], refs[17:20])
        ds_outs = (refs[20], refs[21])
        dlg_ref = refs[22]
        dss = (refs[23], refs[24])
        c = pl.program_id(1)

        @pl.when(c == 0)
        def _():
            dss[0][...] = jnp.zeros_like(dss[0])
            dss[1][...] = jnp.zeros_like(dss[1])
            dlg_ref[...] = jnp.zeros_like(dlg_ref)

        for d in range(2):
            dq_ref, dk_ref, dv_ref = outs[d]
            for h in range(RH):
                lg_d = lg_ref[d, h]
                mask, relf, qd, qe, kd, ke = _decays(lg_d, d == 0)
                g_ch = jnp.exp(lg_d * CH)
                q, k, v16 = _head_qkv(ins[d], h)
                q16 = q.astype(BF16)
                k16 = k.astype(BF16)
                do16v = do_refs[d][:, h * DV:(h + 1) * DV]
                st16 = h_refs[d][h]
                dst = dss[d][h]
                dst16 = dst.astype(BF16)
                a = _dot(q16, k16, NT) * mask
                dp = _dot(do16v, v16, NT)
                da16 = (dp * mask).astype(BF16)
                dq_cross = _dot(do16v, st16, NT) * qd
                dq_ref[:, h * DK:(h + 1) * DK] = _dot(da16, k16) + dq_cross
                dk_state = _dot(v16, dst16, NT) * kd
                dk_ref[:, h * DK:(h + 1) * DK] = (_dot(da16, q16, TN) + dk_state) * (DK ** -0.5)
                dv_ref[:, h * DV:(h + 1) * DV] = _dot(a.astype(BF16), do16v, TN) + _dot((k * kd).astype(BF16), dst16)
                dlg = (jnp.sum(relf * a * dp)
                       + jnp.sum(qe * jnp.sum(q * dq_cross, axis=-1, keepdims=True))
                       + jnp.sum(ke * jnp.sum(k * dk_state, axis=-1, keepdims=True))
                       + CH * g_ch * jnp.sum(dst * st16.astype(F32)))
                row = d * RH + h
                dlg_ref[row:row + 1, :] += jnp.broadcast_to(dlg, (1, 128))
                ds_new = g_ch * dst + _dot((q * qd).astype(BF16), do16v, TN)
                dss[d][h] = ds_new

                @pl.when(c == nc - 1)
                def _():
                    ds_outs[d][h] = ds_new

    def fw(b, c):
        return b * nc + nc - 1 - c

    def bw(b, c):
        return b * nc + c

    def rows(rowf, width):
        return pl.BlockSpec((CH, width), lambda b, c: (rowf(b, c), 0))

    def hist(rowf):
        return pl.BlockSpec((None, None, RH, DK, DV), lambda b, c: (b, rowf(0, c), 0, 0, 0))

    st = pl.BlockSpec((None, RH, DK, DV), lambda b, c: (b, 0, 0, 0))
    in_specs = [pl.BlockSpec(memory_space=pltpu.SMEM)]
    out_specs = []
    for rowf in (fw, bw):
        in_specs += _wide_specs(rowf) + [rows(rowf, RH * DV), hist(rowf)]
        out_specs += [rows(rowf, RH * DK), rows(rowf, RH * DK), rows(rowf, RH * DV)]
    out_specs += [st, st, pl.BlockSpec((None, 8, 128), lambda b, c: (b, 0, 0))]
    qk = _sds((t_rows, RH * DK), F32)
    vv = _sds((t_rows, RH * DV), F32)
    return pl.pallas_call(
        body, name="ret_bwd", grid=(nb, nc), in_specs=in_specs, out_specs=out_specs,
        out_shape=[qk, qk, vv, qk, qk, vv, _sds((nb, RH, DK, DV), F32), _sds((nb, RH, DK, DV), F32),
                   _sds((nb, 8, 128), F32)],
        scratch_shapes=[pltpu.VMEM((RH, DK, DV), F32), pltpu.VMEM((RH, DK, DV), F32)],
        compiler_params=_params(("parallel", "arbitrary")),
    )(lg, *([px] * 5), do16, hist_f, *([px] * 5), do16, hist_b)


def _ctx_state_bwd(px, lg, ds_f, ds_b, nb, t_rows, cx):
    rb = t_rows // cx

    def body(lg_ref, k_ref, v_ref, dsf_ref, dsb_ref, dk_ref, dv_ref, dlg_ref):
        h = pl.program_id(1)
        pos = lax.broadcasted_iota(jnp.int32, (cx, 1), 0).astype(F32)
        k = k_ref[...] * (DK ** -0.5)
        v16 = v_ref[...].astype(BF16)
        dk = jnp.zeros((cx, DK), F32)
        dv = jnp.zeros((cx, DV), F32)
        dlg_ref[...] = jnp.zeros_like(dlg_ref)
        for d, (ds_ref, e) in enumerate(((dsf_ref, cx - 1.0 - pos), (dsb_ref, pos))):
            w = jnp.exp(lg_ref[d, h] * e)
            ds16 = ds_ref[...].astype(BF16)
            t = _dot(v16, ds16, NT)
            dk += t * w
            dv += _dot((k * w).astype(BF16), ds16)
            dlg = jnp.sum(e * w * jnp.sum(k * t, axis=-1, keepdims=True))
            dlg_ref[d:d + 1, :] = jnp.broadcast_to(dlg, (1, 128))
        dk_ref[...] = (dk * (DK ** -0.5)).astype(BF16)
        dv_ref[...] = dv.astype(BF16)

    st = pl.BlockSpec((None, None, DK, DV), lambda b, h: (b, h, 0, 0))
    return pl.pallas_call(
        body, name="ctx_state_bwd", grid=(nb, RH),
        in_specs=[pl.BlockSpec(memory_space=pltpu.SMEM),
                  pl.BlockSpec((cx, DK), lambda b, h: (rb + b, RK // DK + h)),
                  pl.BlockSpec((cx, DV), lambda b, h: (rb + b, RV // DV + h)), st, st],
        out_specs=[pl.BlockSpec((cx, DK), lambda b, h: (b, h)), pl.BlockSpec((cx, DV), lambda b, h: (b, h)),
                   pl.BlockSpec((None, None, 8, 128), lambda b, h: (b, h, 0, 0))],
        out_shape=[_sds((nb * cx, RH * DK), BF16), _sds((nb * cx, RH * DV), BF16), _sds((nb, RH, 8, 128), F32)],
        compiler_params=_params(("parallel", "parallel")),
    )(lg, px, px, ds_f, ds_b)


def _assemble_lat(rows_all, dk_f, dk_b, dv_f, dv_b, dak16, dvx, dq_f, dq_b, drg16, daq16, dag16, dmr16, dma16, tm):
    t_rows = dk_f.shape[0]

    def body(dkf, dkb, dvf, dvb, dak, dav, dqf, dqb, drg, daq, dag, dmr, dma, o_ref):
        o_ref[:, RK:RK + RH * DK] = (dkf[...] + dkb[...]).astype(BF16)
        o_ref[:, RV:RV + RH * DV] = (dvf[...] + dvb[...]).astype(BF16)
        o_ref[:, AK:AK + HKV * HD] = dak[...]
        o_ref[:, AV:AV + HKV * HD] = dav[...].astype(BF16)
        o_ref[:, RQ:RQ + RH * DK] = (dqf[...] + dqb[...]).astype(BF16)
        o_ref[:, RG:RG + RH * DV] = drg[...]
        o_ref[:, AQ:AQ + D] = daq[...]
        o_ref[:, AG:AG + D] = dag[...]
        o_ref[:, MR:MR + D] = dmr[...]
        o_ref[:, MA:MA + D] = dma[...]

    args = (dk_f, dk_b, dv_f, dv_b, dak16, dvx, dq_f, dq_b, drg16, daq16, dag16, dmr16, dma16)
    return pl.pallas_call(
        body, name="assemble_lat", grid=(t_rows // tm,),
        in_specs=[pl.BlockSpec((tm, a.shape[1]), lambda i: (i, 0)) for a in args],
        out_specs=pl.BlockSpec((tm, IN_COLS), lambda i: (i, 0)), out_shape=_sds((rows_all, IN_COLS), BF16),
        compiler_params=_params(("parallel",)),
    )(*args)


def _assemble_ctx(dp_all, dck16, dcv16, dcak16, dvc, t_rows, tm):
    c_rows = dck16.shape[0]
    rb = t_rows // tm

    def body(_, dck, dcv, dcak, dcav, o_ref):
        o_ref[:, RK:RK + RH * DK] = dck[...]
        o_ref[:, RV:RV + RH * DV] = dcv[...]
        o_ref[:, AK:AK + HKV * HD] = dcak[...]
        o_ref[:, AV:AV + HKV * HD] = dcav[...].astype(BF16)
        o_ref[:, KV_COLS:] = jnp.zeros((tm, IN_COLS - KV_COLS), BF16)

    args = (dck16, dcv16, dcak16, dvc)
    return pl.pallas_call(
        body, name="assemble_ctx", grid=(c_rows // tm,),
        in_specs=[pl.BlockSpec(memory_space=pl.ANY)]
        + [pl.BlockSpec((tm, a.shape[1]), lambda i: (i, 0)) for a in args],
        out_specs=pl.BlockSpec((tm, IN_COLS), lambda i: (rb + i, 0)), out_shape=_sds(dp_all.shape, BF16),
        input_output_aliases={0: 0},
        compiler_params=_params(("parallel",)),
    )(dp_all, *args)


def _norm_bwd(dh, x2, mod3, norm_w, dxn, row_off, rows_per_group, group0, tm, name):
    with_dx = dxn is not None
    rows = x2.shape[0]
    rb0 = row_off // tm
    bpg = rows_per_group // tm
    ngroups = rows // rows_per_group

    def body(*refs):
        if with_dx:
            dh_ref, x_ref, sc_ref, nw_ref, dxn_ref, dx_ref, dsh_ref, dsc_ref, dnw_ref = refs
        else:
            dh_ref, x_ref, sc_ref, nw_ref, dsh_ref, dsc_ref, dnw_ref = refs
        i = pl.program_id(0)
        dhv = dh_ref[...]
        xv = x_ref[...]
        nw = nw_ref[...]
        r = lax.rsqrt(jnp.mean(xv * xv, axis=-1, keepdims=True) + EPS)
        xh = xv * r
        dm = dhv * (1.0 + sc_ref[...])
        dsh = jnp.sum(dhv, axis=0, keepdims=True)
        dsc = jnp.sum(dhv * (xh * nw), axis=0, keepdims=True)
        dnw = jnp.sum(dm * xh, axis=0, keepdims=True)
        if with_dx:
            dxh = dm * nw
            dx_ref[...] = dxn_ref[...] + r * (dxh - xh * jnp.mean(dxh * xh, axis=-1, keepdims=True))

        @pl.when(i % bpg == 0)
        def _():
            dsh_ref[...] = dsh
            dsc_ref[...] = dsc

        @pl.when(i % bpg != 0)
        def _():
            dsh_ref[...] += dsh
            dsc_ref[...] += dsc

        @pl.when(i == 0)
        def _():
            dnw_ref[...] = dnw

        @pl.when(i > 0)
        def _():
            dnw_ref[...] += dnw

    grp = pl.BlockSpec((None, 1, D), lambda i: (i // bpg, 0, 0))
    in_specs = [pl.BlockSpec((tm, D), lambda i: (rb0 + i, 0)), pl.BlockSpec((tm, D), lambda i: (i, 0)),
                pl.BlockSpec((None, 1, D), lambda i: (group0 + i // bpg, 0, 1)),
                pl.BlockSpec((1, D), lambda i: (0, 0))]
    args = [dh, x2, mod3, norm_w]
    out_specs = [grp, grp, pl.BlockSpec((1, D), lambda i: (0, 0))]
    out_shape = [_sds((ngroups, 1, D), F32), _sds((ngroups, 1, D), F32), _sds((1, D), F32)]
    if with_dx:
        in_specs.append(pl.BlockSpec((tm, D), lambda i: (i, 0)))
        args.append(dxn)
        out_specs.insert(0, pl.BlockSpec((tm, D), lambda i: (i, 0)))
        out_shape.insert(0, _sds((rows, D), F32))
    return pl.pallas_call(
        body, name=name, grid=(rows // tm,), in_specs=in_specs, out_specs=out_specs, out_shape=out_shape,
        compiler_params=_params(("arbitrary",)),
    )(*args)


def _small_final(dmod_all, dmodc_parts, c_rows, dm_loc_rows, nw_parts, misc_parts, c_ctx, r_pad, w_ada16):
    loc = dm_loc_rows.shape[1]

    def body(dm_ref, dmc_ref, c_ref, dml_ref, nwp_ref, mp_ref, cc_ref, r_ref, w_ref,
             gb_ref, gc_ref, gnw_ref, misc_ref, gwa_ref):
        dmc = jnp.sum(dmc_ref[...], axis=0, keepdims=True)
        gb_ref[...] = jnp.sum(dm_ref[...], axis=0, keepdims=True) + dmc
        dsc = _dot(jnp.broadcast_to(dmc, (8, 3 * D)).astype(BF16), w_ref[...], NT)[0:1, :]
        gc_ref[...] = dsc * _dsilu(cc_ref[...])
        gnw_ref[...] = jnp.sum(nwp_ref[...], axis=0, keepdims=True)
        misc = jnp.sum(mp_ref[...], axis=0, keepdims=True)
        y = jnp.exp2(r_ref[...])
        lane = lax.broadcasted_iota(jnp.int32, (1, D), 1)
        is_decay = jnp.logical_and(lane >= 2 * HD, lane < 2 * HD + 2 * RH)
        misc_ref[...] = misc * jnp.where(is_decay, -(y * np.float32(np.log(2.0))) / (1.0 - y), 1.0)
        gwa_ref[...] = _dot(_silu(c_ref[...]).astype(BF16), dml_ref[...].astype(BF16), TN)

    return pl.pallas_call(
        body, name="small_final",
        out_shape=[_sds((1, 3 * D), F32), _sds((1, D), F32), _sds((1, D), F32), _sds((1, D), F32), _sds((D, loc), F32)],
        compiler_params=pltpu.CompilerParams(vmem_limit_bytes=VMEM_LIMIT),
    )(dmod_all, dmodc_parts, c_rows, dm_loc_rows, nw_parts, misc_parts, c_ctx, r_pad, w_ada16)


def _adamw(w, g, m, v, name):
    rows, cols = w.shape
    tm = _pick(rows, 256, 8)
    bc1 = 1.0 - B1 ** STEP
    bc2 = 1.0 - B2 ** STEP

    def body(w_ref, g_ref, m_ref, v_ref, d_ref, nm_ref, nv_ref):
        g_ = g_ref[...]
        nm = B1 * m_ref[...] + (1.0 - B1) * g_
        nv = B2 * v_ref[...] + (1.0 - B2) * (g_ * g_)
        nm_ref[...] = nm
        nv_ref[...] = nv
        d_ref[...] = -LR * ((nm / bc1) / (jnp.sqrt(nv / bc2) + ADAM_EPS) + WD * w_ref[...])

    blk = pl.BlockSpec((tm, cols), lambda i: (i, 0))
    return pl.pallas_call(
        body, name=name, grid=(rows // tm,), in_specs=[blk] * 4, out_specs=[blk] * 3,
        out_shape=[_sds((rows, cols), F32)] * 3, compiler_params=_params(("parallel",)),
    )(w, g, m, v)


def _mesh_pos():
    return lax.axis_index("x"), lax.axis_index("y"), lax.axis_index("c")


def _all_gather(arrs, name):
    n = len(arrs)

    def body(*refs):
        ins, outs = refs[:n], refs[n:2 * n]
        send_sems, recv_sems, local_sems = refs[2 * n:]
        x, y, c = _mesh_pos()
        me, sib = (x, y, c), (x, y, 1 - c)
        chips = [(1 - x, y), (x, 1 - y), (1 - x, 1 - y)]

        def slot(p):
            return 4 * p[0] + 2 * p[1] + p[2]

        def copy(a, k, block, to, own):
            dst = outs[a].at[slot(block)]
            return pltpu.make_async_remote_copy(
                src_ref=ins[a] if own else dst, dst_ref=dst, send_sem=send_sems.at[a, k], recv_sem=recv_sems.at[a, k],
                device_id=to, device_id_type=MESH_T)

        mine = [pltpu.make_async_copy(ins[a], outs[a].at[slot(me)], local_sems.at[a]) for a in range(n)]
        for cp in mine:
            cp.start()
        first = []
        for a in range(n):
            first.append(copy(a, 0, me, sib, True))
            first += [copy(a, 1 + j, me, (*chip, c), True) for j, chip in enumerate(chips)]
        for cp in first:
            cp.start()
        passed = []
        for j, chip in enumerate(chips):
            for a in range(n):
                copy(a, 1 + j, (*chip, c), me, False).wait_recv()
                fwd = copy(a, 4 + j, (*chip, c), sib, False)
                fwd.start()
                passed.append(fwd)
        for a in range(n):
            copy(a, 0, sib, me, False).wait_recv()
            for j, chip in enumerate(chips):
                copy(a, 4 + j, (*chip, 1 - c), me, False).wait_recv()
        for cp in first + passed:
            cp.wait_send()
        for cp in mine:
            cp.wait()

    hbm = pl.BlockSpec(memory_space=pl.ANY)
    return pl.pallas_call(
        body, name=name, in_specs=[hbm] * n, out_specs=[hbm] * n,
        out_shape=[_sds((N_DEV,) + a.shape, a.dtype) for a in arrs],
        scratch_shapes=[pltpu.SemaphoreType.DMA((n, 7)), pltpu.SemaphoreType.DMA((n, 7)), pltpu.SemaphoreType.DMA((n,))],
    )(*arrs)


def _pair_exchange(arrs, name):
    n = len(arrs)

    def body(*refs):
        ins, outs = refs[:n], refs[n:2 * n]
        send_sems, recv_sems = refs[2 * n:]
        x, y, c = _mesh_pos()
        sib = (x, y, 1 - c)
        sends = []
        for a in range(n):
            for k in range(4):
                sends.append(pltpu.make_async_remote_copy(
                    src_ref=ins[a].at[2 * k + 1 - c], dst_ref=outs[a].at[k], send_sem=send_sems.at[a, k],
                    recv_sem=recv_sems.at[a, k], device_id=sib, device_id_type=MESH_T))
        for cp in sends:
            cp.start()
        for cp in sends:
            cp.wait_recv()
        for cp in sends:
            cp.wait_send()

    hbm = pl.BlockSpec(memory_space=pl.ANY)
    return pl.pallas_call(
        body, name=name, in_specs=[hbm] * n, out_specs=[hbm] * n,
        out_shape=[_sds((4,) + a.shape[1:], a.dtype) for a in arrs],
        scratch_shapes=[pltpu.SemaphoreType.DMA((n, 4)), pltpu.SemaphoreType.DMA((n, 4))],
    )(*arrs)


def _pair_add(part, got, core, name):
    _, rows, cols = part.shape
    tm = _pick(rows, 256, 16)
    p4 = part.reshape(4, 2, rows, cols)

    def body(core_ref, p_ref, g_ref, o_ref):
        o_ref[...] = (p_ref[...].astype(F32) + g_ref[...].astype(F32)).astype(BF16)

    blk = pl.BlockSpec((None, tm, cols), lambda k, i, cr: (k, i, 0))
    return pl.pallas_call(
        body, name=name,
        grid_spec=pltpu.PrefetchScalarGridSpec(
            num_scalar_prefetch=1, grid=(4, rows // tm),
            in_specs=[pl.BlockSpec((None, None, tm, cols), lambda k, i, cr: (k, cr[0], i, 0)), blk], out_specs=blk),
        out_shape=_sds((4, rows, cols), BF16), compiler_params=_params(("parallel", "parallel")),
    )(core, p4, got)


def _chip_sum(pair_sums, landed, chip, name):
    _, rows, cols = pair_sums.shape
    tm = _pick(rows, 256, 16)

    def body(chip_ref, s_ref, l_ref, o_ref):
        acc = s_ref[...].astype(F32)
        for j in range(3):
            acc = acc + l_ref[j].astype(F32)
        o_ref[...] = acc

    return pl.pallas_call(
        body, name=name,
        grid_spec=pltpu.PrefetchScalarGridSpec(
            num_scalar_prefetch=1, grid=(rows // tm,),
            in_specs=[pl.BlockSpec((None, tm, cols), lambda i, ch: (ch[0], i, 0)),
                      pl.BlockSpec((3, tm, cols), lambda i, ch: (0, i, 0))],
            out_specs=pl.BlockSpec((tm, cols), lambda i, ch: (i, 0))),
        out_shape=_sds((rows, cols), F32), compiler_params=_params(("parallel",)),
    )(chip, pair_sums, landed)


_HBM = pl.BlockSpec(memory_space=pltpu.HBM)
_SEM = pl.BlockSpec(memory_space=pltpu.SEMAPHORE)
_EFFECT = pltpu.SideEffectType.DATAFLOW_SIDE_EFFECTING


def _chip_routes(n):
    def plan(x, y, c):
        routes = []
        for a in range(n):
            for j in range(1, 4):
                px, py = x ^ (j >> 1), y ^ (j & 1)
                routes.append((a, 2 * px + py, (px, py, c), j - 1))
        return routes
    return plan, 3 * n


def _bcast_routes(n):
    def plan(x, y, c):
        routes = []
        for a in range(n):
            for k in range(1, N_DEV):
                peer = (x ^ ((k >> 2) & 1), y ^ ((k >> 1) & 1), c ^ (k & 1))
                routes.append((a, 0, peer, 4 * x + 2 * y + c))
        return routes
    return plan, 7 * n


def _route_copies(srcs, lands, send_sems, recv_sems, routes):
    return [pltpu.make_async_remote_copy(
        src_ref=srcs[a].at[sb], dst_ref=lands[a].at[lb], send_sem=send_sems.at[r], recv_sem=recv_sems.at[r],
        device_id=peer, device_id_type=MESH_T) for r, (a, sb, peer, lb) in enumerate(routes)]


def _exchange_start(srcs, lands, routes, name):
    plan, count = routes
    n = len(srcs)

    def body(*refs):
        send_sems, recv_sems = refs[2 * n], refs[2 * n + 1]
        token = refs[-1]
        for cp in _route_copies(refs[:n], refs[n:2 * n], send_sems, recv_sems, plan(*_mesh_pos())):
            cp.start()
        token[...] = jnp.zeros_like(token)

    args = [pltpu.with_memory_space_constraint(a, pltpu.HBM) for a in list(srcs) + list(lands)]
    out = pl.pallas_call(
        body, name=name,
        out_shape=(pltpu.SemaphoreType.DMA((count,)), pltpu.SemaphoreType.DMA((count,)),
                   *[pltpu.HBM(a.shape, a.dtype) for a in args], _sds((8, 128), F32)),
        in_specs=[_HBM] * (2 * n), out_specs=(_SEM, _SEM, *([_HBM] * (2 * n)), pl.BlockSpec(memory_space=pltpu.VMEM)),
        input_output_aliases={i: 2 + i for i in range(2 * n)},
        compiler_params=pltpu.CompilerParams(has_side_effects=_EFFECT),
    )(*args)
    return (out[0], out[1], list(out[2:2 + 2 * n]), routes), out[-1]


def _exchange_wait(state, after, name):
    send_sems, recv_sems, bufs, (plan, count) = state
    n = len(bufs) // 2

    def body(*refs):
        send_s, recv_s = refs[2 * n], refs[2 * n + 1]
        for cp in _route_copies(refs[:n], refs[n:2 * n], send_s, recv_s, plan(*_mesh_pos())):
            cp.wait_send()
            cp.wait_recv()

    out = pl.pallas_call(
        body, name=name, out_shape=tuple(pltpu.HBM(a.shape, a.dtype) for a in bufs),
        in_specs=[_HBM] * (2 * n) + [_SEM, _SEM, pl.BlockSpec(memory_space=pl.ANY)], out_specs=tuple([_HBM] * (2 * n)),
        input_output_aliases={i: i for i in range(2 * n)},
        compiler_params=pltpu.CompilerParams(has_side_effects=_EFFECT),
    )(*bufs, send_sems, recv_sems, after)
    return list(out[:n]), list(out[n:])


def _reduce_scatter_start(parts, core, name):
    got = _pair_exchange(parts, name + "_pair")
    sums = [_pair_add(p, g, core, "%s_add_%d" % (name, i)) for i, (p, g) in enumerate(zip(parts, got))]
    lands = [lax.empty((3,) + s_.shape[1:], BF16) for s_ in sums]
    return _exchange_start(sums, lands, _chip_routes(len(sums)), name + "_start")


def _reduce_scatter_finish(rs_state, after, chip, name):
    sums, landed = _exchange_wait(rs_state, after, name + "_wait")
    return [_chip_sum(s_, l_, chip, "%s_sum_%d" % (name, i)) for i, (s_, l_) in enumerate(zip(sums, landed))]


def _local_step(x, c, ctx, c_ctx, norm_w, b_ada, ret_log2_decay, q_norm_w, k_norm_w, loss_target,
                w_ada16, w_in_t16, get_w_o, on_out_grads, on_in_grad, started=()):
    nb, seq, _ = x.shape
    cx = ctx.shape[1]
    t_rows, c_rows = nb * seq, nb * cx
    rows_all = t_rows + c_rows
    nc = seq // CH
    tm = _pick(seq, 256, 128)
    te = _pick(seq, 512, 128)
    assert cx % tm == 0 and t_rows % cx == 0 and seq % GRID_W == 0

    x2 = x.reshape(t_rows, D)
    ctx2 = ctx.reshape(c_rows, D)
    tgt = loss_target.reshape(t_rows, D)
    c8 = jnp.zeros((8, D), F32).at[:nb].set(c).at[nb].set(c_ctx)
    lg = _log_gamma(ret_log2_decay)
    cos, sin = _rope_tables(seq)

    mod = _mod_fwd(c8, w_ada16, b_ada)
    mod3 = mod[:, None, :]
    h_all = _norm_fwd(x2, mod3, norm_w, rows_all, 0, seq, 0, None, te, "norm_fwd")
    h_all = _norm_fwd(ctx2, mod3, norm_w, rows_all, t_rows, c_rows, nb, h_all, tm, "norm_fwd_ctx")
    px = _matmul(h_all, w_in_t16, tb=True, tm=1536, tn=1536, tk=D, out_dtype=F32, name="in_proj", after=started)
    s0f, s0b = _ctx_state(px, lg, nb, t_rows, cx)
    o_f, o_b, hist_f, hist_b = _ret_fwd(px, lg, s0f, s0b, nb, nc)
    yret16 = _ret_post(o_f, o_b, px, te)
    q16 = _qk_prep(px, q_norm_w, cos, sin, t_rows, 0, AQ, HQ, 4, seq, te, "q_prep")
    kx16 = _qk_prep(px, k_norm_w, cos, sin, t_rows, 0, AK, HKV, HKV, seq, te, "k_prep")
    kc16 = _qk_prep(px, k_norm_w, None, None, c_rows, t_rows, AK, HKV, HKV, seq, tm, "kc_prep")
    o_att, yatt16, lse = _att_fwd(q16, kx16, kc16, px, nb, seq, cx, tm)
    w_o_ret16, w_o_att16, w_out16 = get_w_o(lse)
    a_ret, a_att, y16 = _merge(yret16, yatt16, px, w_o_ret16, w_o_att16, te)
    dxn, dout16, dgate, loss_b = _outproj(y16, w_out16, x2, tgt, mod3, nb, seq, te)

    gw_out = _matmul(y16, dout16, ta=True, tm=D, tn=D, tk=D, out_dtype=BF16, name="gw_out")
    da_ret16, da_att16, dmr16, dma16 = _bwd_merge(dout16, w_out16, px, a_ret, a_att, te)
    gw_o_ret = _matmul(yret16, da_ret16, ta=True, tm=D, tn=D, tk=D, out_dtype=BF16, name="gw_o_ret")
    gw_o_att = _matmul(yatt16, da_att16, ta=True, tm=D, tn=D, tk=D, out_dtype=BF16, name="gw_o_att")
    out_state, out_started = on_out_grads([gw_o_ret, gw_o_att, gw_out])
    do16, drg16 = _bwd_branch_ret(da_ret16, w_o_ret16, px, o_f, o_b, te, after=out_started)
    dao, dag16 = _bwd_branch_att(da_att16, w_o_att16, px, o_att, te)
    dq_rot, dkx, dvx, dkc, dvc = _att_bwd(q16, kx16, kc16, px, dao, o_att, lse, nb, seq, cx, tm)
    daq16, gq = _qk_prep_bwd(dq_rot, px, q_norm_w, cos, sin, t_rows, 0, AQ, HQ, 4, seq, te, "q_prep_bwd")
    dak16, gk_lat = _qk_prep_bwd(dkx.reshape(t_rows, HKV * HD), px, k_norm_w, cos, sin, t_rows, 0, AK, HKV, HKV, seq, te,
                                 "k_prep_bwd")
    dcak16, gk_ctx = _qk_prep_bwd(dkc.reshape(c_rows, HKV * HD), px, k_norm_w, None, None, c_rows, t_rows, AK, HKV, HKV,
                                  seq, tm, "kc_prep_bwd")
    dq_f, dk_f, dv_f, dq_b, dk_b, dv_b, ds_f, ds_b, dlg_scan = _ret_bwd(px, lg, do16, hist_f, hist_b, nb, nc)
    dck16, dcv16, dlg_ctx = _ctx_state_bwd(px, lg, ds_f, ds_b, nb, t_rows, cx)
    dp_all = _assemble_lat(rows_all, dk_f, dk_b, dv_f, dv_b, dak16, dvx.reshape(t_rows, HKV * HD), dq_f, dq_b, drg16,
                           daq16, dag16, dmr16, dma16, tm)
    dp_all = _assemble_ctx(dp_all, dck16, dcv16, dcak16, dvc.reshape(c_rows, HKV * HD), t_rows, tm)
    gw_in_t = _matmul(dp_all, h_all, ta=True, tm=1536, tn=D, tk=1536, out_dtype=BF16, name="gw_in")
    in_state, in_started = on_in_grad(gw_in_t)
    dh = _matmul(dp_all, w_in_t16, tm=1536, tn=D, tk=1536, out_dtype=F32, name="d_h", after=in_started)
    grad_x, dsh, dsc, gnw_lat = _norm_bwd(dh, x2, mod3, norm_w, dxn, 0, seq, 0, te, "norm_bwd")
    dsh_c, dsc_c, gnw_ctx = _norm_bwd(dh, ctx2, mod3, norm_w, None, t_rows, c_rows, nb, tm, "norm_bwd_ctx")

    dlg = (jnp.sum(dlg_scan[:, :, 0], axis=0) + jnp.sum(dlg_ctx[:, :, :2, 0], axis=0).T.reshape(2 * RH)).reshape(1, 2 * RH)
    misc = jnp.concatenate([gq, gk_lat + gk_ctx, dlg, jnp.sum(loss_b[:, 0, 0]).reshape(1, 1),
                            jnp.zeros((1, D - 2 * HD - 2 * RH - 1), F32)], axis=1)
    rows = []
    for b in range(nb):
        rows += [dsh[b], dsc[b], dgate[b]]
    rows += [dsh_c[0], dsc_c[0]] + [c[b:b + 1] for b in range(nb)] + [gnw_lat + gnw_ctx, misc]
    payload = jnp.concatenate(rows + [jnp.zeros((PAY_ROWS - len(rows), D), F32)], axis=0)
    return grad_x.reshape(nb, seq, D), out_state, in_state, payload


def _finish_small(gathered, nb, c_ctx, ret_log2_decay, w_ada16, dev):
    n_dev = gathered.shape[0]
    loc = 3 * D // n_dev
    dmod_all = gathered[:, :3 * nb].reshape(n_dev * nb, 3 * D)
    dmodc_parts = jnp.concatenate([gathered[:, 3 * nb:3 * nb + 2].reshape(n_dev, 2 * D), jnp.zeros((n_dev, D), F32)], axis=1)
    c_all = gathered[:, 3 * nb + 2:4 * nb + 2].reshape(n_dev * nb, D)
    nw_parts = gathered[:, 4 * nb + 2]
    misc_parts = gathered[:, 4 * nb + 3]
    n_rows = n_dev * nb + n_dev
    pad = (-n_rows) % 16
    c_rows = jnp.concatenate([c_all, jnp.broadcast_to(c_ctx.reshape(1, D), (n_dev, D)), jnp.zeros((pad, D), F32)], axis=0)
    dm_rows = jnp.concatenate([dmod_all, dmodc_parts, jnp.zeros((pad, 3 * D), F32)], axis=0)
    dm_loc_rows = lax.dynamic_slice_in_dim(dm_rows, dev * loc, loc, axis=1)
    r_pad = jnp.full((1, D), -1.0, F32).at[:, 2 * HD:2 * HD + 2 * RH].set(ret_log2_decay.reshape(1, 2 * RH))
    gb, gc, gnw, misc, gwa = _small_final(dmod_all, dmodc_parts, c_rows, dm_loc_rows, nw_parts, misc_parts,
                                          c_ctx.reshape(1, D), r_pad, w_ada16)
    return (gb, gc, gnw, misc[:, :HD], misc[:, HD:2 * HD], misc[:, 2 * HD:2 * HD + 2 * RH], gwa,
            misc[0, 2 * HD + 2 * RH])


def kernel(x, c, ctx, c_ctx, norm_w, w_ada, b_ada, w_in, ret_log2_decay, q_norm_w, k_norm_w, w_o_ret, w_o_att, w_out, loss_target, m_c_ctx, m_norm_w, m_w_ada, m_b_ada, m_w_in, m_ret_log2_decay, m_q_norm_w, m_k_norm_w, m_w_o_ret, m_w_o_att, m_w_out, v_c_ctx, v_norm_w, v_w_ada, v_b_ada, v_w_in, v_ret_log2_decay, v_q_norm_w, v_k_norm_w, v_w_o_ret, v_w_o_att, v_w_out):
    nb = x.shape[0]
    mx, my, mc = _mesh_pos()
    dev = 4 * mx + 2 * my + mc
    core = jnp.reshape(mc, (1,)).astype(jnp.int32)
    chip = jnp.reshape(2 * mx + my, (1,)).astype(jnp.int32)

    w_in_t = jnp.transpose(w_in[0])
    g_in, g_ada = _all_gather([w_in_t.astype(BF16), w_ada[0].astype(BF16)], "gather_weights")
    w_in_t16 = g_in.reshape(IN_COLS, D)
    w_ada16 = jnp.transpose(g_ada, (1, 0, 2)).reshape(D, 3 * D)

    wo_shards = [w_[0].astype(BF16) for w_ in (w_o_ret, w_o_att, w_out)]
    wo_shards = list(lax.optimization_barrier((g_in, *wo_shards))[1:])
    wo_lands = [lax.dynamic_update_slice(lax.empty((N_DEV,) + s_.shape, BF16), s_[None], (dev, 0, 0)) for s_ in wo_shards]
    wo_state, wo_token = _exchange_start([s_[None] for s_ in wo_shards], wo_lands, _bcast_routes(3), "gather_wo_start")

    def get_w_o(after):
        _, (l_ret, l_att, l_out) = _exchange_wait(wo_state, after, "gather_wo_wait")
        return l_ret.reshape(RH * DV, D), l_att.reshape(D, D), l_out.reshape(D, D)

    def on_out_grads(grads):
        parts = [g_.reshape(N_DEV, g_.shape[0] // N_DEV, D) for g_ in grads]
        state, token = _reduce_scatter_start(parts, core, "rs_out")
        return state, (token,)

    def on_in_grad(grad):
        state, token = _reduce_scatter_start([grad.reshape(N_DEV, IN_COLS // N_DEV, D)], core, "rs_in")
        return state, (token,)

    grad_x, out_state, in_state, payload = _local_step(
        x, c, ctx, c_ctx, norm_w, b_ada, ret_log2_decay, q_norm_w, k_norm_w, loss_target,
        w_ada16, w_in_t16, get_w_o, on_out_grads, on_in_grad, started=(wo_token,))

    (gathered,) = _all_gather([payload], "gather_small")
    gb, gc, gnw, gq, gk, gr, gwa, loss = _finish_small(gathered, nb, c_ctx, ret_log2_decay, w_ada16, dev)

    g_w_o_ret, g_w_o_att, g_w_out = _reduce_scatter_finish(out_state, gathered, chip, "rs_out")
    (g_w_in_t,) = _reduce_scatter_finish(in_state, gathered, chip, "rs_in")

    grads = [gc.reshape(c_ctx.shape), gnw, gwa[None], gb, g_w_in_t, gr.reshape(ret_log2_decay.shape), gq, gk,
             g_w_o_ret[None], g_w_o_att[None], g_w_out[None]]
    weights = [c_ctx, norm_w, w_ada, b_ada, w_in_t, ret_log2_decay, q_norm_w, k_norm_w, w_o_ret, w_o_att, w_out]
    ms = [m_c_ctx, m_norm_w, m_w_ada, m_b_ada, jnp.transpose(m_w_in[0]), m_ret_log2_decay, m_q_norm_w, m_k_norm_w,
          m_w_o_ret, m_w_o_att, m_w_out]
    vs = [v_c_ctx, v_norm_w, v_w_ada, v_b_ada, jnp.transpose(v_w_in[0]), v_ret_log2_decay, v_q_norm_w, v_k_norm_w,
          v_w_o_ret, v_w_o_att, v_w_out]
    deltas, new_ms, new_vs = [], [], []
    for i, (w, g, m, v) in enumerate(zip(weights, grads, ms, vs)):
        shape2 = (-1, w.shape[-1])
        res = _adamw(w.reshape(shape2), g.reshape(shape2), m.reshape(shape2), v.reshape(shape2), "adamw_%d" % i)
        for lst, r in zip((deltas, new_ms, new_vs), res):
            lst.append(jnp.transpose(r)[None] if i == 4 else r.reshape(w.shape))
    grads[4] = jnp.transpose(g_w_in_t)[None]
    return (loss, grad_x, *grads, *deltas, *new_ms, *new_vs)
```

```python
import numpy as np
import jax
import jax.numpy as jnp
from jax import lax
from jax.experimental import pallas as pl
from jax.experimental.pallas import tpu as pltpu

F32 = jnp.float32
BF16 = jnp.bfloat16

D = 1024
RH, DK, DV, CH = 4, 256, 512, 256
HQ, HKV, HD = 8, 2, 128
GRID_W = 64
ROPE_THETA = 10000.0
EPS = 1e-6
RK, RV, AK, AV, RQ, RG, AQ, AG, MR, MA = 0, 1024, 3072, 3328, 3584, 4608, 6656, 7680, 8704, 9728
IN_COLS = 10752
KV_COLS = 3584
N_DEV = 8
LR, B1, B2, ADAM_EPS, WD, STEP = 0.001, 0.9, 0.999, 1e-08, 0.01, 10
PAY_ROWS = 16
VMEM_LIMIT = 56 * 1024 * 1024
MESH_T = pl.DeviceIdType.MESH

NT = (((1,), (1,)), ((), ()))
TN = (((0,), (0,)), ((), ()))
SM_C = (HD ** -0.5) * float(np.log2(np.e))


def _params(sem):
    return pltpu.CompilerParams(dimension_semantics=sem, vmem_limit_bytes=VMEM_LIMIT)


def _pick(n, target, mult=8):
    best = None
    for t in range(mult, min(n, target) + 1, mult):
        if n % t == 0:
            best = t
    return best or n


def _dot(a, b, dn=None):
    if dn is None:
        return jnp.dot(a, b, preferred_element_type=F32)
    return lax.dot_general(a, b, dn, preferred_element_type=F32)


def _sig(v):
    return jax.nn.sigmoid(v)


def _silu(v):
    return v * _sig(v)


def _dsilu(v):
    s = _sig(v)
    return s * (1.0 + v * (1.0 - s))


def _sds(shape, dtype):
    return jax.ShapeDtypeStruct(shape, dtype)


def _matmul(a, b, *, ta=False, tb=False, tm, tn, tk, out_dtype, name, after=()):
    m = a.shape[1] if ta else a.shape[0]
    kdim = a.shape[0] if ta else a.shape[1]
    n = b.shape[0] if tb else b.shape[1]
    tm, tn, tk = _pick(m, tm, 128), _pick(n, tn, 128), _pick(kdim, tk, 128)
    nk = kdim // tk
    dn = (((0 if ta else 1,), (1 if tb else 0,)), ((), ()))

    def body(a_ref, b_ref, *rest):
        o_ref, acc_ref = rest[-2:]
        k = pl.program_id(2)
        part = _dot(a_ref[...].astype(BF16), b_ref[...].astype(BF16), dn)
        if nk == 1:
            o_ref[...] = part.astype(o_ref.dtype)
        else:
            @pl.when(k == 0)
            def _():
                acc_ref[...] = part

            @pl.when(k > 0)
            def _():
                acc_ref[...] += part

            @pl.when(k == nk - 1)
            def _():
                o_ref[...] = acc_ref[...].astype(o_ref.dtype)

    a_spec = pl.BlockSpec((tk, tm), lambda i, j, k: (k, i)) if ta else pl.BlockSpec((tm, tk), lambda i, j, k: (i, k))
    b_spec = pl.BlockSpec((tn, tk), lambda i, j, k: (j, k)) if tb else pl.BlockSpec((tk, tn), lambda i, j, k: (k, j))
    return pl.pallas_call(
        body, name=name, grid=(m // tm, n // tn, nk),
        in_specs=[a_spec, b_spec] + [pl.BlockSpec(memory_space=pl.ANY)] * len(after),
        out_specs=pl.BlockSpec((tm, tn), lambda i, j, k: (i, j)), out_shape=_sds((m, n), out_dtype),
        scratch_shapes=[pltpu.VMEM((tm, tn) if nk > 1 else (8, 128), F32)],
        compiler_params=_params(("parallel", "parallel", "arbitrary")),
    )(a, b, *after)


def _log_gamma(r):
    rp = jnp.full((8, 128), -1.0, F32).at[:2, :RH].set(r.reshape(2, RH))

    def body(r_ref, o_ref):
        o_ref[...] = jnp.log1p(-jnp.exp2(r_ref[...]))

    out = pl.pallas_call(body, name="log_gamma", out_shape=_sds((8, 128), F32))(rp)
    return out[:2, :RH]


def _mod_fwd(c8, w_ada16, b_ada):
    def body(c_ref, w_ref, b_ref, o_ref):
        o_ref[...] = _dot(_silu(c_ref[...]).astype(BF16), w_ref[...]) + b_ref[...]

    return pl.pallas_call(
        body, name="mod_fwd", grid=(3,),
        in_specs=[pl.BlockSpec((8, D), lambda j: (0, 0)), pl.BlockSpec((D, D), lambda j: (0, j)),
                  pl.BlockSpec((1, D), lambda j: (0, j))],
        out_specs=pl.BlockSpec((8, D), lambda j: (0, j)), out_shape=_sds((8, 3 * D), F32),
        compiler_params=_params(("arbitrary",)),
    )(c8, w_ada16, b_ada)


def _norm_fwd(x2, mod3, norm_w, rows_all, row_off, rows_per_group, group0, h_prev, tm, name):
    rows = x2.shape[0]
    rb0 = row_off // tm
    bpg = rows_per_group // tm

    def body(*refs):
        x_ref, sh_ref, sc_ref, nw_ref, o_ref = refs[-5:]
        xv = x_ref[...]
        r = lax.rsqrt(jnp.mean(xv * xv, axis=-1, keepdims=True) + EPS)
        o_ref[...] = ((xv * r) * nw_ref[...] * (1.0 + sc_ref[...]) + sh_ref[...]).astype(BF16)

    in_specs = [pl.BlockSpec((tm, D), lambda i: (i, 0)),
                pl.BlockSpec((None, 1, D), lambda i: (group0 + i // bpg, 0, 0)),
                pl.BlockSpec((None, 1, D), lambda i: (group0 + i // bpg, 0, 1)),
                pl.BlockSpec((1, D), lambda i: (0, 0))]
    args = [x2, mod3, mod3, norm_w]
    alias = {}
    if h_prev is not None:
        in_specs.insert(0, pl.BlockSpec(memory_space=pl.ANY))
        args.insert(0, h_prev)
        alias = {0: 0}
    return pl.pallas_call(
        body, name=name, grid=(rows // tm,), in_specs=in_specs,
        out_specs=pl.BlockSpec((tm, D), lambda i: (rb0 + i, 0)), out_shape=_sds((rows_all, D), BF16),
        input_output_aliases=alias, compiler_params=_params(("parallel",)),
    )(*args)


def _decays(lg, fwd):
    ii = lax.broadcasted_iota(jnp.int32, (CH, CH), 0)
    jj = lax.broadcasted_iota(jnp.int32, (CH, CH), 1)
    ri = lax.broadcasted_iota(jnp.int32, (CH, 1), 0).astype(F32)
    rel = (ii - jj) if fwd else (jj - ii)
    relf = jnp.maximum(rel, 0).astype(F32)
    mask = jnp.where(rel >= 0, jnp.exp(lg * relf), 0.0)
    qe = (ri + 1.0) if fwd else (CH - ri)
    ke = (CH - 1.0 - ri) if fwd else ri
    return mask, relf, jnp.exp(lg * qe), qe, jnp.exp(lg * ke), ke


def _wide_specs(rowf):
    return [pl.BlockSpec((CH, 2 * DK), lambda b, c: (rowf(b, c), RQ // (2 * DK))),
            pl.BlockSpec((CH, 2 * DK), lambda b, c: (rowf(b, c), RQ // (2 * DK) + 1)),
            pl.BlockSpec((CH, RH * DK), lambda b, c: (rowf(b, c), RK // (RH * DK))),
            pl.BlockSpec((CH, 2 * DV), lambda b, c: (rowf(b, c), RV // (2 * DV))),
            pl.BlockSpec((CH, 2 * DV), lambda b, c: (rowf(b, c), RV // (2 * DV) + 1))]


def _head_qkv(refs, h):
    q0, q1, k, v0, v1 = refs
    lo = h % 2
    q = (q0, q1)[h // 2][:, lo * DK:(lo + 1) * DK]
    kk = k[:, h * DK:(h + 1) * DK] * (DK ** -0.5)
    v16 = (v0, v1)[h // 2][:, lo * DV:(lo + 1) * DV].astype(BF16)
    return q, kk, v16


def _ctx_state(px, lg, nb, t_rows, cx):
    rb = t_rows // cx

    def body(lg_ref, k_ref, v_ref, sf_ref, sb_ref):
        h = pl.program_id(1)
        pos = lax.broadcasted_iota(jnp.int32, (cx, 1), 0).astype(F32)
        k = k_ref[...] * (DK ** -0.5)
        v16 = v_ref[...].astype(BF16)
        wf = jnp.exp(lg_ref[0, h] * (cx - 1.0 - pos))
        wb = jnp.exp(lg_ref[1, h] * pos)
        sf_ref[...] = _dot((k * wf).astype(BF16), v16, TN)
        sb_ref[...] = _dot((k * wb).astype(BF16), v16, TN)

    st = pl.BlockSpec((None, None, DK, DV), lambda b, h: (b, h, 0, 0))
    return pl.pallas_call(
        body, name="ctx_state", grid=(nb, RH),
        in_specs=[pl.BlockSpec(memory_space=pltpu.SMEM),
                  pl.BlockSpec((cx, DK), lambda b, h: (rb + b, RK // DK + h)),
                  pl.BlockSpec((cx, DV), lambda b, h: (rb + b, RV // DV + h))],
        out_specs=[st, st], out_shape=[_sds((nb, RH, DK, DV), F32)] * 2,
        compiler_params=_params(("parallel", "parallel")),
    )(lg, px, px)


def _ret_fwd(px, lg, s0f, s0b, nb, nc):
    t_rows = nb * nc * CH

    def body(lg_ref, *refs):
        ins = (refs[0:5], refs[5:10])
        s0f_ref, s0b_ref, of_ref, ob_ref, hf_ref, hb_ref, sf, sb = refs[10:]
        c = pl.program_id(1)

        @pl.when(c == 0)
        def _():
            sf[...] = s0f_ref[...]
            sb[...] = s0b_ref[...]

        for d, (o_ref, h_ref, s) in enumerate(((of_ref, hf_ref, sf), (ob_ref, hb_ref, sb))):
            for h in range(RH):
                lg_d = lg_ref[d, h]
                mask, _, qd, _, kd, _ = _decays(lg_d, d == 0)
                q, k, v16 = _head_qkv(ins[d], h)
                a = _dot(q.astype(BF16), k.astype(BF16), NT)
                st = s[h]
                st16 = st.astype(BF16)
                h_ref[h] = st16
                o_ref[:, h * DV:(h + 1) * DV] = _dot((a * mask).astype(BF16), v16) + _dot((q * qd).astype(BF16), st16)
                s[h] = st * jnp.exp(lg_d * CH) + _dot((k * kd).astype(BF16), v16, TN)

    def fw(b, c):
        return b * nc + c

    def bw(b, c):
        return b * nc + nc - 1 - c

    st = pl.BlockSpec((None, RH, DK, DV), lambda b, c: (b, 0, 0, 0))
    in_specs = [pl.BlockSpec(memory_space=pltpu.SMEM)] + _wide_specs(fw) + _wide_specs(bw) + [st, st]
    out_specs = [pl.BlockSpec((CH, RH * DV), lambda b, c: (fw(b, c), 0)),
                 pl.BlockSpec((CH, RH * DV), lambda b, c: (bw(b, c), 0)),
                 pl.BlockSpec((None, None, RH, DK, DV), lambda b, c: (b, c, 0, 0, 0)),
                 pl.BlockSpec((None, None, RH, DK, DV), lambda b, c: (b, nc - 1 - c, 0, 0, 0))]
    return pl.pallas_call(
        body, name="ret_fwd", grid=(nb, nc), in_specs=in_specs, out_specs=out_specs,
        out_shape=[_sds((t_rows, RH * DV), F32)] * 2 + [_sds((nb, nc, RH, DK, DV), BF16)] * 2,
        scratch_shapes=[pltpu.VMEM((RH, DK, DV), F32), pltpu.VMEM((RH, DK, DV), F32)],
        compiler_params=_params(("parallel", "arbitrary")),
    )(lg, *([px] * 10), s0f, s0b)


def _ret_post(o_f, o_b, px, tm):
    t_rows = o_f.shape[0]

    def body(of_ref, ob_ref, g_ref, y_ref):
        o = of_ref[...] + ob_ref[...]
        r = lax.rsqrt(jnp.mean(o * o, axis=-1, keepdims=True) + EPS)
        y_ref[...] = ((o * r) * _silu(g_ref[...])).astype(BF16)

    blk = pl.BlockSpec((tm, DV), lambda i, h: (i, h))
    return pl.pallas_call(
        body, name="ret_post", grid=(t_rows // tm, RH),
        in_specs=[blk, blk, pl.BlockSpec((tm, DV), lambda i, h: (i, RG // DV + h))],
        out_specs=blk, out_shape=_sds((t_rows, RH * DV), BF16),
        compiler_params=_params(("parallel", "parallel")),
    )(o_f, o_b, px)


def _rope_tables(seq):
    rows = seq // GRID_W
    row = np.repeat(np.arange(rows, dtype=np.float32), GRID_W)
    col = np.tile(np.arange(GRID_W, dtype=np.float32), rows)
    half = HD // 2
    freqs = (ROPE_THETA ** (-np.arange(0, half, 2, dtype=np.float32) / half)).astype(np.float32)
    ang = np.concatenate([row[:, None] * freqs, col[:, None] * freqs], axis=-1).astype(np.float32)
    cos = np.repeat(np.cos(ang), 2, axis=-1).astype(np.float32)
    sin = np.repeat(np.sin(ang), 2, axis=-1).astype(np.float32)
    sign = np.tile(np.array([-1.0, 1.0], np.float32), HD // 2)
    return jnp.asarray(cos), jnp.asarray(sin * sign)


def _swap_pairs(v):
    lane = lax.broadcasted_iota(jnp.int32, v.shape, 1)
    return jnp.where((lane & 1) == 0, pltpu.roll(v, HD - 1, 1), pltpu.roll(v, 1, 1))


def _qk_prep(px, nw, cos, sin, rows, row_off, col_off, heads, hb, seq, tm, name):
    rope = cos is not None
    rb0 = row_off // tm
    pb = seq // tm if rope else 1
    bw = hb * HD

    def body(*refs):
        if rope:
            x_ref, w_ref, c_ref, s_ref, o_ref = refs
        else:
            x_ref, w_ref, o_ref = refs
        for h in range(hb):
            sl = slice(h * HD, (h + 1) * HD)
            xv = x_ref[:, sl]
            r = lax.rsqrt(jnp.mean(xv * xv, axis=-1, keepdims=True) + EPS)
            t = (xv * r) * w_ref[...]
            if rope:
                t = t * c_ref[...] + _swap_pairs(t) * s_ref[...]
            o_ref[:, sl] = t.astype(BF16)

    in_specs = [pl.BlockSpec((tm, bw), lambda i, j: (rb0 + i, col_off // bw + j)),
                pl.BlockSpec((1, HD), lambda i, j: (0, 0))]
    args = [px, nw]
    if rope:
        in_specs += [pl.BlockSpec((tm, HD), lambda i, j: (i % pb, 0))] * 2
        args += [cos, sin]
    return pl.pallas_call(
        body, name=name, grid=(rows // tm, heads // hb), in_specs=in_specs,
        out_specs=pl.BlockSpec((tm, bw), lambda i, j: (i, j)), out_shape=_sds((rows, heads * HD), BF16),
        compiler_params=_params(("parallel", "parallel")),
    )(*args)


def _att_fwd(q16, kx16, kc16, px, nb, seq, cx, tq):
    t_rows = nb * seq
    nq = seq // tq
    rep = HQ // HKV
    gw = rep * HD

    def body(q_ref, kx_ref, kc_ref, vx_ref, vc_ref, g_ref, o_ref, y_ref, l_ref):
        kx = kx_ref[...]
        kc = kc_ref[...]
        vx = vx_ref[...].astype(BF16)
        vc = vc_ref[...].astype(BF16)
        l_ref[...] = jnp.zeros_like(l_ref)
        for r in range(rep):
            sl = slice(r * HD, (r + 1) * HD)
            q = q_ref[:, sl]
            s1 = _dot(q, kx, NT)
            s2 = _dot(q, kc, NT)
            m = jnp.maximum(jnp.max(s1, axis=-1, keepdims=True), jnp.max(s2, axis=-1, keepdims=True))
            e1 = jnp.exp2((s1 - m) * SM_C)
            e2 = jnp.exp2((s2 - m) * SM_C)
            tot = jnp.sum(e1, axis=-1, keepdims=True) + jnp.sum(e2, axis=-1, keepdims=True)
            o = (_dot(e1.astype(BF16), vx) + _dot(e2.astype(BF16), vc)) * (1.0 / tot)
            o_ref[:, sl] = o
            y_ref[:, sl] = (o * _silu(g_ref[:, sl])).astype(BF16)
            l_ref[:, r:r + 1] = m * SM_C + jnp.log(tot) * float(np.log2(np.e))

    qblk = pl.BlockSpec((tq, gw), lambda b, g, i: (b * nq + i, g))
    return pl.pallas_call(
        body, name="att_fwd", grid=(nb, HKV, nq),
        in_specs=[qblk,
                  pl.BlockSpec((seq, HD), lambda b, g, i: (b, g)),
                  pl.BlockSpec((cx, HD), lambda b, g, i: (b, g)),
                  pl.BlockSpec((seq, HD), lambda b, g, i: (b, AV // HD + g)),
                  pl.BlockSpec((cx, HD), lambda b, g, i: (t_rows // cx + b, AV // HD + g)),
                  pl.BlockSpec((tq, gw), lambda b, g, i: (b * nq + i, AG // gw + g))],
        out_specs=[qblk, qblk, pl.BlockSpec((tq, 128), lambda b, g, i: (b * nq + i, g))],
        out_shape=[_sds((t_rows, D), F32), _sds((t_rows, D), BF16), _sds((t_rows, HKV * 128), F32)],
        compiler_params=_params(("parallel", "parallel", "parallel")),
    )(q16, kx16, kc16, px, px, px)


def _merge(yret16, yatt16, px, w_o_ret16, w_o_att16, tm):
    t_rows = yret16.shape[0]
    hw = D // 2

    def body(yr_ref, wr_ref, ya_ref, wa_ref, mr_ref, ma_ref, ar_ref, aa_ref, y_ref):
        ar = _dot(yr_ref[...], wr_ref[...])
        aa = _dot(ya_ref[...], wa_ref[...])
        ar_ref[...] = ar
        aa_ref[...] = aa
        y_ref[...] = (_sig(mr_ref[...]) * ar + _sig(ma_ref[...]) * aa).astype(BF16)

    half = pl.BlockSpec((tm, hw), lambda i, j: (i, j))
    return pl.pallas_call(
        body, name="merge", grid=(t_rows // tm, 2),
        in_specs=[pl.BlockSpec((tm, RH * DV), lambda i, j: (i, 0)), pl.BlockSpec((RH * DV, hw), lambda i, j: (0, j)),
                  pl.BlockSpec((tm, D), lambda i, j: (i, 0)), pl.BlockSpec((D, hw), lambda i, j: (0, j)),
                  pl.BlockSpec((tm, hw), lambda i, j: (i, MR // hw + j)),
                  pl.BlockSpec((tm, hw), lambda i, j: (i, MA // hw + j))],
        out_specs=[half, half, half],
        out_shape=[_sds((t_rows, D), F32), _sds((t_rows, D), F32), _sds((t_rows, D), BF16)],
        compiler_params=_params(("parallel", "parallel")),
    )(yret16, w_o_ret16, yatt16, w_o_att16, px, px)


def _outproj(y16, w_out16, x2, tgt, mod3, nb, seq, tm):
    t_rows = nb * seq
    bpb = seq // tm

    def body(y_ref, w_ref, x_ref, t_ref, g_ref, dxn_ref, dout_ref, dg_ref, loss_ref):
        i = pl.program_id(1)
        out = _dot(y_ref[...], w_ref[...])
        gate = g_ref[...]
        diff = x_ref[...] + gate * out - t_ref[...]
        dxn = diff * (1.0 / D)
        dxn_ref[...] = dxn
        dout_ref[...] = (gate * dxn).astype(BF16)
        dg = jnp.sum(dxn * out, axis=0, keepdims=True)
        ls = jnp.broadcast_to(jnp.sum(diff * diff) * (0.5 / D), (1, 128))

        @pl.when(i == 0)
        def _():
            dg_ref[...] = dg
            loss_ref[...] = ls

        @pl.when(i > 0)
        def _():
            dg_ref[...] += dg
            loss_ref[...] += ls

    row = pl.BlockSpec((tm, D), lambda b, i: (b * bpb + i, 0))
    return pl.pallas_call(
        body, name="outproj", grid=(nb, bpb),
        in_specs=[row, pl.BlockSpec((D, D), lambda b, i: (0, 0)), row, row,
                  pl.BlockSpec((None, 1, D), lambda b, i: (b, 0, 2))],
        out_specs=[row, row, pl.BlockSpec((None, 1, D), lambda b, i: (b, 0, 0)),
                   pl.BlockSpec((None, 1, 128), lambda b, i: (b, 0, 0))],
        out_shape=[_sds((t_rows, D), F32), _sds((t_rows, D), BF16), _sds((nb, 1, D), F32), _sds((nb, 1, 128), F32)],
        compiler_params=_params(("parallel", "arbitrary")),
    )(y16, w_out16, x2, tgt, mod3)


def _bwd_merge(dout16, w_out16, px, a_ret, a_att, tm):
    t_rows = dout16.shape[0]
    hw = D // 2

    def body(do_ref, w_ref, mr_ref, ma_ref, ar_ref, aa_ref, dar_ref, daa_ref, dmr_ref, dma_ref):
        dy = _dot(do_ref[...], w_ref[...], NT)
        sr = _sig(mr_ref[...])
        sa = _sig(ma_ref[...])
        dar_ref[...] = (dy * sr).astype(BF16)
        daa_ref[...] = (dy * sa).astype(BF16)
        dmr_ref[...] = (dy * ar_ref[...] * sr * (1.0 - sr)).astype(BF16)
        dma_ref[...] = (dy * aa_ref[...] * sa * (1.0 - sa)).astype(BF16)

    half = pl.BlockSpec((tm, hw), lambda i, j: (i, j))
    return pl.pallas_call(
        body, name="bwd_merge", grid=(t_rows // tm, 2),
        in_specs=[pl.BlockSpec((tm, D), lambda i, j: (i, 0)), pl.BlockSpec((hw, D), lambda i, j: (j, 0)),
                  pl.BlockSpec((tm, hw), lambda i, j: (i, MR // hw + j)),
                  pl.BlockSpec((tm, hw), lambda i, j: (i, MA // hw + j)), half, half],
        out_specs=[half] * 4, out_shape=[_sds((t_rows, D), BF16)] * 4,
        compiler_params=_params(("parallel", "parallel")),
    )(dout16, w_out16, px, px, a_ret, a_att)


def _bwd_branch_ret(da_ret16, w_o_ret16, px, o_f, o_b, tm, after=()):
    t_rows = da_ret16.shape[0]

    def body(da_ref, w_ref, g_ref, of_ref, ob_ref, *rest):
        do_ref, dg_ref = rest[-2:]
        dy = _dot(da_ref[...], w_ref[...], NT)
        g = g_ref[...]
        o = of_ref[...] + ob_ref[...]
        r = lax.rsqrt(jnp.mean(o * o, axis=-1, keepdims=True) + EPS)
        on = o * r
        don = dy * _silu(g)
        dg_ref[...] = (dy * on * _dsilu(g)).astype(BF16)
        do_ref[...] = (r * (don - on * jnp.mean(on * don, axis=-1, keepdims=True))).astype(BF16)

    blk = pl.BlockSpec((tm, DV), lambda i, h: (i, h))
    return pl.pallas_call(
        body, name="bwd_branch_ret", grid=(t_rows // tm, RH),
        in_specs=[pl.BlockSpec((tm, D), lambda i, h: (i, 0)), pl.BlockSpec((DV, D), lambda i, h: (h, 0)),
                  pl.BlockSpec((tm, DV), lambda i, h: (i, RG // DV + h)), blk, blk]
        + [pl.BlockSpec(memory_space=pl.ANY)] * len(after),
        out_specs=[blk, blk], out_shape=[_sds((t_rows, RH * DV), BF16)] * 2,
        compiler_params=_params(("parallel", "parallel")),
    )(da_ret16, w_o_ret16, px, o_f, o_b, *after)


def _bwd_branch_att(da_att16, w_o_att16, px, o_att, tm):
    t_rows = da_att16.shape[0]
    hw = D // 2

    def body(da_ref, w_ref, g_ref, o_ref, dao_ref, dg_ref):
        dy = _dot(da_ref[...], w_ref[...], NT)
        g = g_ref[...]
        dao_ref[...] = dy * _silu(g)
        dg_ref[...] = (dy * o_ref[...] * _dsilu(g)).astype(BF16)

    half = pl.BlockSpec((tm, hw), lambda i, j: (i, j))
    return pl.pallas_call(
        body, name="bwd_branch_att", grid=(t_rows // tm, 2),
        in_specs=[pl.BlockSpec((tm, D), lambda i, j: (i, 0)), pl.BlockSpec((hw, D), lambda i, j: (j, 0)),
                  pl.BlockSpec((tm, hw), lambda i, j: (i, AG // hw + j)), half],
        out_specs=[half, half], out_shape=[_sds((t_rows, D), F32), _sds((t_rows, D), BF16)],
        compiler_params=_params(("parallel", "parallel")),
    )(da_att16, w_o_att16, px, o_att)


def _att_bwd(q16, kx16, kc16, px, dao, o_att, lse, nb, seq, cx, tq):
    t_rows = nb * seq
    nq = seq // tq
    rep = HQ // HKV
    gw = rep * HD
    scale = HD ** -0.5

    def body(q_ref, kx_ref, kc_ref, vx_ref, vc_ref, dao_ref, o_ref, l_ref, dq_ref, dkx_ref, dvx_ref, dkc_ref, dvc_ref):
        i = pl.program_id(2)
        kx = kx_ref[...]
        kc = kc_ref[...]
        vx = vx_ref[...].astype(BF16)
        vc = vc_ref[...].astype(BF16)
        dkx = jnp.zeros((seq, HD), F32)
        dvx = jnp.zeros((seq, HD), F32)
        dkc = jnp.zeros((cx, HD), F32)
        dvc = jnp.zeros((cx, HD), F32)
        for r in range(rep):
            sl = slice(r * HD, (r + 1) * HD)
            q = q_ref[:, sl]
            lr = l_ref[:, r:r + 1]
            p1 = jnp.exp2(_dot(q, kx, NT) * SM_C - lr)
            p2 = jnp.exp2(_dot(q, kc, NT) * SM_C - lr)
            da = dao_ref[:, sl]
            da16 = da.astype(BF16)
            delta = jnp.sum(da * o_ref[:, sl], axis=-1, keepdims=True)
            ds1 = (p1 * (_dot(da16, vx, NT) - delta)).astype(BF16)
            ds2 = (p2 * (_dot(da16, vc, NT) - delta)).astype(BF16)
            dq_ref[:, sl] = (_dot(ds1, kx) + _dot(ds2, kc)) * scale
            dkx += _dot(ds1, q, TN)
            dkc += _dot(ds2, q, TN)
            dvx += _dot(p1.astype(BF16), da16, TN)
            dvc += _dot(p2.astype(BF16), da16, TN)
        dkx = dkx * scale
        dkc = dkc * scale

        @pl.when(i == 0)
        def _():
            dkx_ref[...] = dkx
            dvx_ref[...] = dvx
            dkc_ref[...] = dkc
            dvc_ref[...] = dvc

        @pl.when(i > 0)
        def _():
            dkx_ref[...] += dkx
            dvx_ref[...] += dvx
            dkc_ref[...] += dkc
            dvc_ref[...] += dvc

    qblk = pl.BlockSpec((tq, gw), lambda b, g, i: (b * nq + i, g))
    kxb = pl.BlockSpec((None, seq, HD), lambda b, g, i: (b, 0, g))
    kcb = pl.BlockSpec((None, cx, HD), lambda b, g, i: (b, 0, g))
    return pl.pallas_call(
        body, name="att_bwd", grid=(nb, HKV, nq),
        in_specs=[qblk,
                  pl.BlockSpec((seq, HD), lambda b, g, i: (b, g)),
                  pl.BlockSpec((cx, HD), lambda b, g, i: (b, g)),
                  pl.BlockSpec((seq, HD), lambda b, g, i: (b, AV // HD + g)),
                  pl.BlockSpec((cx, HD), lambda b, g, i: (t_rows // cx + b, AV // HD + g)),
                  qblk, qblk, pl.BlockSpec((tq, 128), lambda b, g, i: (b * nq + i, g))],
        out_specs=[qblk, kxb, kxb, kcb, kcb],
        out_shape=[_sds((t_rows, D), F32), _sds((nb, seq, HKV * HD), F32), _sds((nb, seq, HKV * HD), F32),
                   _sds((nb, cx, HKV * HD), F32), _sds((nb, cx, HKV * HD), F32)],
        compiler_params=_params(("parallel", "parallel", "arbitrary")),
    )(q16, kx16, kc16, px, px, dao, o_att, lse)


def _qk_prep_bwd(dt, px, nw, cos, sin, rows, row_off, col_off, heads, hb, seq, tm, name):
    rope = cos is not None
    rb0 = row_off // tm
    pb = seq // tm if rope else 1
    bw = hb * HD

    def body(*refs):
        if rope:
            d_ref, x_ref, w_ref, c_ref, s_ref, dx_ref, dw_ref = refs
        else:
            d_ref, x_ref, w_ref, dx_ref, dw_ref = refs
        first = jnp.logical_and(pl.program_id(0) == 0, pl.program_id(1) == 0)
        dw = jnp.zeros((1, HD), F32)
        for h in range(hb):
            sl = slice(h * HD, (h + 1) * HD)
            dtv = d_ref[:, sl]
            if rope:
                dtv = dtv * c_ref[...] + _swap_pairs(dtv * s_ref[...])
            xv = x_ref[:, sl]
            r = lax.rsqrt(jnp.mean(xv * xv, axis=-1, keepdims=True) + EPS)
            xh = xv * r
            dxh = dtv * w_ref[...]
            dx_ref[:, sl] = (r * (dxh - xh * jnp.mean(dxh * xh, axis=-1, keepdims=True))).astype(BF16)
            dw += jnp.sum(dtv * xh, axis=0, keepdims=True)

        @pl.when(first)
        def _():
            dw_ref[...] = dw

        @pl.when(jnp.logical_not(first))
        def _():
            dw_ref[...] += dw

    blk = pl.BlockSpec((tm, bw), lambda i, j: (i, j))
    in_specs = [blk, pl.BlockSpec((tm, bw), lambda i, j: (rb0 + i, col_off // bw + j)),
                pl.BlockSpec((1, HD), lambda i, j: (0, 0))]
    args = [dt, px, nw]
    if rope:
        in_specs += [pl.BlockSpec((tm, HD), lambda i, j: (i % pb, 0))] * 2
        args += [cos, sin]
    return pl.pallas_call(
        body, name=name, grid=(rows // tm, heads // hb), in_specs=in_specs,
        out_specs=[blk, pl.BlockSpec((1, HD), lambda i, j: (0, 0))],
        out_shape=[_sds((rows, heads * HD), BF16), _sds((1, HD), F32)],
        compiler_params=_params(("arbitrary", "arbitrary")),
    )(*args)


def _ret_bwd(px, lg, do16, hist_f, hist_b, nb, nc):
    t_rows = nb * nc * CH

    def body(lg_ref, *refs):
        ins = (refs[0:5], refs[7:12])
        do_refs = (refs[5], refs[12])
        h_refs = (refs[6], refs[13])
        outs = (refs[14:17], refs[17:20])
        ds_outs = (refs[20], refs[21])
        dlg_ref = refs[22]
        dss = (refs[23], refs[24])
        c = pl.program_id(1)

        @pl.when(c == 0)
        def _():
            dss[0][...] = jnp.zeros_like(dss[0])
            dss[1][...] = jnp.zeros_like(dss[1])
            dlg_ref[...] = jnp.zeros_like(dlg_ref)

        for d in range(2):
            dq_ref, dk_ref, dv_ref = outs[d]
            for h in range(RH):
                lg_d = lg_ref[d, h]
                mask, relf, qd, qe, kd, ke = _decays(lg_d, d == 0)
                g_ch = jnp.exp(lg_d * CH)
                q, k, v16 = _head_qkv(ins[d], h)
                q16 = q.astype(BF16)
                k16 = k.astype(BF16)
                do16v = do_refs[d][:, h * DV:(h + 1) * DV]
                st16 = h_refs[d][h]
                dst = dss[d][h]
                dst16 = dst.astype(BF16)
                a = _dot(q16, k16, NT) * mask
                dp = _dot(do16v, v16, NT)
                da16 = (dp * mask).astype(BF16)
                dq_cross = _dot(do16v, st16, NT) * qd
                dq_ref[:, h * DK:(h + 1) * DK] = _dot(da16, k16) + dq_cross
                dk_state = _dot(v16, dst16, NT) * kd
                dk_ref[:, h * DK:(h + 1) * DK] = (_dot(da16, q16, TN) + dk_state) * (DK ** -0.5)
                dv_ref[:, h * DV:(h + 1) * DV] = _dot(a.astype(BF16), do16v, TN) + _dot((k * kd).astype(BF16), dst16)
                dlg = (jnp.sum(relf * a * dp)
                       + jnp.sum(qe * jnp.sum(q * dq_cross, axis=-1, keepdims=True))
                       + jnp.sum(ke * jnp.sum(k * dk_state, axis=-1, keepdims=True))
                       + CH * g_ch * jnp.sum(dst * st16.astype(F32)))
                row = d * RH + h
                dlg_ref[row:row + 1, :] += jnp.broadcast_to(dlg, (1, 128))
                ds_new = g_ch * dst + _dot((q * qd).astype(BF16), do16v, TN)
                dss[d][h] = ds_new

                @pl.when(c == nc - 1)
                def _():
                    ds_outs[d][h] = ds_new

    def fw(b, c):
        return b * nc + nc - 1 - c

    def bw(b, c):
        return b * nc + c

    def rows(rowf, width):
        return pl.BlockSpec((CH, width), lambda b, c: (rowf(b, c), 0))

    def hist(rowf):
        return pl.BlockSpec((None, None, RH, DK, DV), lambda b, c: (b, rowf(0, c), 0, 0, 0))

    st = pl.BlockSpec((None, RH, DK, DV), lambda b, c: (b, 0, 0, 0))
    in_specs = [pl.BlockSpec(memory_space=pltpu.SMEM)]
    out_specs = []
    for rowf in (fw, bw):
        in_specs += _wide_specs(rowf) + [rows(rowf, RH * DV), hist(rowf)]
        out_specs += [rows(rowf, RH * DK), rows(rowf, RH * DK), rows(rowf, RH * DV)]
    out_specs += [st, st, pl.BlockSpec((None, 8, 128), lambda b, c: (b, 0, 0))]
    qk = _sds((t_rows, RH * DK), F32)
    vv = _sds((t_rows, RH * DV), F32)
    return pl.pallas_call(
        body, name="ret_bwd", grid=(nb, nc), in_specs=in_specs, out_specs=out_specs,
        out_shape=[qk, qk, vv, qk, qk, vv, _sds((nb, RH, DK, DV), F32), _sds((nb, RH, DK, DV), F32),
                   _sds((nb, 8, 128), F32)],
        scratch_shapes=[pltpu.VMEM((RH, DK, DV), F32), pltpu.VMEM((RH, DK, DV), F32)],
        compiler_params=_params(("parallel", "arbitrary")),
    )(lg, *([px] * 5), do16, hist_f, *([px] * 5), do16, hist_b)


def _ctx_state_bwd(px, lg, ds_f, ds_b, nb, t_rows, cx):
    rb = t_rows // cx

    def body(lg_ref, k_ref, v_ref, dsf_ref, dsb_ref, dk_ref, dv_ref, dlg_ref):
        h = pl.program_id(1)
        pos = lax.broadcasted_iota(jnp.int32, (cx, 1), 0).astype(F32)
        k = k_ref[...] * (DK ** -0.5)
        v16 = v_ref[...].astype(BF16)
        dk = jnp.zeros((cx, DK), F32)
        dv = jnp.zeros((cx, DV), F32)
        dlg_ref[...] = jnp.zeros_like(dlg_ref)
        for d, (ds_ref, e) in enumerate(((dsf_ref, cx - 1.0 - pos), (dsb_ref, pos))):
            w = jnp.exp(lg_ref[d, h] * e)
            ds16 = ds_ref[...].astype(BF16)
            t = _dot(v16, ds16, NT)
            dk += t * w
            dv += _dot((k * w).astype(BF16), ds16)
            dlg = jnp.sum(e * w * jnp.sum(k * t, axis=-1, keepdims=True))
            dlg_ref[d:d + 1, :] = jnp.broadcast_to(dlg, (1, 128))
        dk_ref[...] = (dk * (DK ** -0.5)).astype(BF16)
        dv_ref[...] = dv.astype(BF16)

    st = pl.BlockSpec((None, None, DK, DV), lambda b, h: (b, h, 0, 0))
    return pl.pallas_call(
        body, name="ctx_state_bwd", grid=(nb, RH),
        in_specs=[pl.BlockSpec(memory_space=pltpu.SMEM),
                  pl.BlockSpec((cx, DK), lambda b, h: (rb + b, RK // DK + h)),
                  pl.BlockSpec((cx, DV), lambda b, h: (rb + b, RV // DV + h)), st, st],
        out_specs=[pl.BlockSpec((cx, DK), lambda b, h: (b, h)), pl.BlockSpec((cx, DV), lambda b, h: (b, h)),
                   pl.BlockSpec((None, None, 8, 128), lambda b, h: (b, h, 0, 0))],
        out_shape=[_sds((nb * cx, RH * DK), BF16), _sds((nb * cx, RH * DV), BF16), _sds((nb, RH, 8, 128), F32)],
        compiler_params=_params(("parallel", "parallel")),
    )(lg, px, px, ds_f, ds_b)


def _assemble_lat(rows_all, dk_f, dk_b, dv_f, dv_b, dak16, dvx, dq_f, dq_b, drg16, daq16, dag16, dmr16, dma16, tm):
    t_rows = dk_f.shape[0]

    def body(dkf, dkb, dvf, dvb, dak, dav, dqf, dqb, drg, daq, dag, dmr, dma, o_ref):
        o_ref[:, RK:RK + RH * DK] = (dkf[...] + dkb[...]).astype(BF16)
        o_ref[:, RV:RV + RH * DV] = (dvf[...] + dvb[...]).astype(BF16)
        o_ref[:, AK:AK + HKV * HD] = dak[...]
        o_ref[:, AV:AV + HKV * HD] = dav[...].astype(BF16)
        o_ref[:, RQ:RQ + RH * DK] = (dqf[...] + dqb[...]).astype(BF16)
        o_ref[:, RG:RG + RH * DV] = drg[...]
        o_ref[:, AQ:AQ + D] = daq[...]
        o_ref[:, AG:AG + D] = dag[...]
        o_ref[:, MR:MR + D] = dmr[...]
        o_ref[:, MA:MA + D] = dma[...]

    args = (dk_f, dk_b, dv_f, dv_b, dak16, dvx, dq_f, dq_b, drg16, daq16, dag16, dmr16, dma16)
    return pl.pallas_call(
        body, name="assemble_lat", grid=(t_rows // tm,),
        in_specs=[pl.BlockSpec((tm, a.shape[1]), lambda i: (i, 0)) for a in args],
        out_specs=pl.BlockSpec((tm, IN_COLS), lambda i: (i, 0)), out_shape=_sds((rows_all, IN_COLS), BF16),
        compiler_params=_params(("parallel",)),
    )(*args)


def _assemble_ctx(dp_all, dck16, dcv16, dcak16, dvc, t_rows, tm):
    c_rows = dck16.shape[0]
    rb = t_rows // tm

    def body(_, dck, dcv, dcak, dcav, o_ref):
        o_ref[:, RK:RK + RH * DK] = dck[...]
        o_ref[:, RV:RV + RH * DV] = dcv[...]
        o_ref[:, AK:AK + HKV * HD] = dcak[...]
        o_ref[:, AV:AV + HKV * HD] = dcav[...].astype(BF16)
        o_ref[:, KV_COLS:] = jnp.zeros((tm, IN_COLS - KV_COLS), BF16)

    args = (dck16, dcv16, dcak16, dvc)
    return pl.pallas_call(
        body, name="assemble_ctx", grid=(c_rows // tm,),
        in_specs=[pl.BlockSpec(memory_space=pl.ANY)]
        + [pl.BlockSpec((tm, a.shape[1]), lambda i: (i, 0)) for a in args],
        out_specs=pl.BlockSpec((tm, IN_COLS), lambda i: (rb + i, 0)), out_shape=_sds(dp_all.shape, BF16),
        input_output_aliases={0: 0},
        compiler_params=_params(("parallel",)),
    )(dp_all, *args)


def _norm_bwd(dh, x2, mod3, norm_w, dxn, row_off, rows_per_group, group0, tm, name):
    with_dx = dxn is not None
    rows = x2.shape[0]
    rb0 = row_off // tm
    bpg = rows_per_group // tm
    ngroups = rows // rows_per_group

    def body(*refs):
        if with_dx:
            dh_ref, x_ref, sc_ref, nw_ref, dxn_ref, dx_ref, dsh_ref, dsc_ref, dnw_ref = refs
        else:
            dh_ref, x_ref, sc_ref, nw_ref, dsh_ref, dsc_ref, dnw_ref = refs
        i = pl.program_id(0)
        dhv = dh_ref[...]
        xv = x_ref[...]
        nw = nw_ref[...]
        r = lax.rsqrt(jnp.mean(xv * xv, axis=-1, keepdims=True) + EPS)
        xh = xv * r
        dm = dhv * (1.0 + sc_ref[...])
        dsh = jnp.sum(dhv, axis=0, keepdims=True)
        dsc = jnp.sum(dhv * (xh * nw), axis=0, keepdims=True)
        dnw = jnp.sum(dm * xh, axis=0, keepdims=True)
        if with_dx:
            dxh = dm * nw
            dx_ref[...] = dxn_ref[...] + r * (dxh - xh * jnp.mean(dxh * xh, axis=-1, keepdims=True))

        @pl.when(i % bpg == 0)
        def _():
            dsh_ref[...] = dsh
            dsc_ref[...] = dsc

        @pl.when(i % bpg != 0)
        def _():
            dsh_ref[...] += dsh
            dsc_ref[...] += dsc

        @pl.when(i == 0)
        def _():
            dnw_ref[...] = dnw

        @pl.when(i > 0)
        def _():
            dnw_ref[...] += dnw

    grp = pl.BlockSpec((None, 1, D), lambda i: (i // bpg, 0, 0))
    in_specs = [pl.BlockSpec((tm, D), lambda i: (rb0 + i, 0)), pl.BlockSpec((tm, D), lambda i: (i, 0)),
                pl.BlockSpec((None, 1, D), lambda i: (group0 + i // bpg, 0, 1)),
                pl.BlockSpec((1, D), lambda i: (0, 0))]
    args = [dh, x2, mod3, norm_w]
    out_specs = [grp, grp, pl.BlockSpec((1, D), lambda i: (0, 0))]
    out_shape = [_sds((ngroups, 1, D), F32), _sds((ngroups, 1, D), F32), _sds((1, D), F32)]
    if with_dx:
        in_specs.append(pl.BlockSpec((tm, D), lambda i: (i, 0)))
        args.append(dxn)
        out_specs.insert(0, pl.BlockSpec((tm, D), lambda i: (i, 0)))
        out_shape.insert(0, _sds((rows, D), F32))
    return pl.pallas_call(
        body, name=name, grid=(rows // tm,), in_specs=in_specs, out_specs=out_specs, out_shape=out_shape,
        compiler_params=_params(("arbitrary",)),
    )(*args)


def _small_final(dmod_all, dmodc_parts, c_rows, dm_loc_rows, nw_parts, misc_parts, c_ctx, r_pad, w_ada16):
    loc = dm_loc_rows.shape[1]

    def body(dm_ref, dmc_ref, c_ref, dml_ref, nwp_ref, mp_ref, cc_ref, r_ref, w_ref,
             gb_ref, gc_ref, gnw_ref, misc_ref, gwa_ref):
        dmc = jnp.sum(dmc_ref[...], axis=0, keepdims=True)
        gb_ref[...] = jnp.sum(dm_ref[...], axis=0, keepdims=True) + dmc
        dsc = _dot(jnp.broadcast_to(dmc, (8, 3 * D)).astype(BF16), w_ref[...], NT)[0:1, :]
        gc_ref[...] = dsc * _dsilu(cc_ref[...])
        gnw_ref[...] = jnp.sum(nwp_ref[...], axis=0, keepdims=True)
        misc = jnp.sum(mp_ref[...], axis=0, keepdims=True)
        y = jnp.exp2(r_ref[...])
        lane = lax.broadcasted_iota(jnp.int32, (1, D), 1)
        is_decay = jnp.logical_and(lane >= 2 * HD, lane < 2 * HD + 2 * RH)
        misc_ref[...] = misc * jnp.where(is_decay, -(y * np.float32(np.log(2.0))) / (1.0 - y), 1.0)
        gwa_ref[...] = _dot(_silu(c_ref[...]).astype(BF16), dml_ref[...].astype(BF16), TN)

    return pl.pallas_call(
        body, name="small_final",
        out_shape=[_sds((1, 3 * D), F32), _sds((1, D), F32), _sds((1, D), F32), _sds((1, D), F32), _sds((D, loc), F32)],
        compiler_params=pltpu.CompilerParams(vmem_limit_bytes=VMEM_LIMIT),
    )(dmod_all, dmodc_parts, c_rows, dm_loc_rows, nw_parts, misc_parts, c_ctx, r_pad, w_ada16)


def _adamw(w, g, m, v, name):
    rows, cols = w.shape
    tm = _pick(rows, 256, 8)
    bc1 = 1.0 - B1 ** STEP
    bc2 = 1.0 - B2 ** STEP

    def body(w_ref, g_ref, m_ref, v_ref, d_ref, nm_ref, nv_ref):
        g_ = g_ref[...]
        nm = B1 * m_ref[...] + (1.0 - B1) * g_
        nv = B2 * v_ref[...] + (1.0 - B2) * (g_ * g_)
        nm_ref[...] = nm
        nv_ref[...] = nv
        d_ref[...] = -LR * ((nm / bc1) / (jnp.sqrt(nv / bc2) + ADAM_EPS) + WD * w_ref[...])

    blk = pl.BlockSpec((tm, cols), lambda i: (i, 0))
    return pl.pallas_call(
        body, name=name, grid=(rows // tm,), in_specs=[blk] * 4, out_specs=[blk] * 3,
        out_shape=[_sds((rows, cols), F32)] * 3, compiler_params=_params(("parallel",)),
    )(w, g, m, v)


def _mesh_pos():
    return lax.axis_index("x"), lax.axis_index("y"), lax.axis_index("c")


def _all_gather(arrs, name):
    n = len(arrs)

    def body(*refs):
        ins, outs = refs[:n], refs[n:2 * n]
        send_sems, recv_sems, local_sems = refs[2 * n:]
        x, y, c = _mesh_pos()
        me, sib = (x, y, c), (x, y, 1 - c)
        chips = [(1 - x, y), (x, 1 - y), (1 - x, 1 - y)]

        def slot(p):
            return 4 * p[0] + 2 * p[1] + p[2]

        def copy(a, k, block, to, own):
            dst = outs[a].at[slot(block)]
            return pltpu.make_async_remote_copy(
                src_ref=ins[a] if own else dst, dst_ref=dst, send_sem=send_sems.at[a, k], recv_sem=recv_sems.at[a, k],
                device_id=to, device_id_type=MESH_T)

        mine = [pltpu.make_async_copy(ins[a], outs[a].at[slot(me)], local_sems.at[a]) for a in range(n)]
        for cp in mine:
            cp.start()
        first = []
        for a in range(n):
            first.append(copy(a, 0, me, sib, True))
            first += [copy(a, 1 + j, me, (*chip, c), True) for j, chip in enumerate(chips)]
        for cp in first:
            cp.start()
        passed = []
        for j, chip in enumerate(chips):
            for a in range(n):
                copy(a, 1 + j, (*chip, c), me, False).wait_recv()
                fwd = copy(a, 4 + j, (*chip, c), sib, False)
                fwd.start()
                passed.append(fwd)
        for a in range(n):
            copy(a, 0, sib, me, False).wait_recv()
            for j, chip in enumerate(chips):
                copy(a, 4 + j, (*chip, 1 - c), me, False).wait_recv()
        for cp in first + passed:
            cp.wait_send()
        for cp in mine:
            cp.wait()

    hbm = pl.BlockSpec(memory_space=pl.ANY)
    return pl.pallas_call(
        body, name=name, in_specs=[hbm] * n, out_specs=[hbm] * n,
        out_shape=[_sds((N_DEV,) + a.shape, a.dtype) for a in arrs],
        scratch_shapes=[pltpu.SemaphoreType.DMA((n, 7)), pltpu.SemaphoreType.DMA((n, 7)), pltpu.SemaphoreType.DMA((n,))],
    )(*arrs)


def _pair_exchange(arrs, name):
    n = len(arrs)

    def body(*refs):
        ins, outs = refs[:n], refs[n:2 * n]
        send_sems, recv_sems = refs[2 * n:]
        x, y, c = _mesh_pos()
        sib = (x, y, 1 - c)
        sends = []
        for a in range(n):
            for k in range(4):
                sends.append(pltpu.make_async_remote_copy(
                    src_ref=ins[a].at[2 * k + 1 - c], dst_ref=outs[a].at[k], send_sem=send_sems.at[a, k],
                    recv_sem=recv_sems.at[a, k], device_id=sib, device_id_type=MESH_T))
        for cp in sends:
            cp.start()
        for cp in sends:
            cp.wait_recv()
        for cp in sends:
            cp.wait_send()

    hbm = pl.BlockSpec(memory_space=pl.ANY)
    return pl.pallas_call(
        body, name=name, in_specs=[hbm] * n, out_specs=[hbm] * n,
        out_shape=[_sds((4,) + a.shape[1:], a.dtype) for a in arrs],
        scratch_shapes=[pltpu.SemaphoreType.DMA((n, 4)), pltpu.SemaphoreType.DMA((n, 4))],
    )(*arrs)


def _pair_add(part, got, core, name):
    _, rows, cols = part.shape
    tm = _pick(rows, 256, 16)
    p4 = part.reshape(4, 2, rows, cols)

    def body(core_ref, p_ref, g_ref, o_ref):
        o_ref[...] = (p_ref[...].astype(F32) + g_ref[...].astype(F32)).astype(BF16)

    blk = pl.BlockSpec((None, tm, cols), lambda k, i, cr: (k, i, 0))
    return pl.pallas_call(
        body, name=name,
        grid_spec=pltpu.PrefetchScalarGridSpec(
            num_scalar_prefetch=1, grid=(4, rows // tm),
            in_specs=[pl.BlockSpec((None, None, tm, cols), lambda k, i, cr: (k, cr[0], i, 0)), blk], out_specs=blk),
        out_shape=_sds((4, rows, cols), BF16), compiler_params=_params(("parallel", "parallel")),
    )(core, p4, got)


def _chip_sum(pair_sums, landed, chip, name):
    _, rows, cols = pair_sums.shape
    tm = _pick(rows, 256, 16)

    def body(chip_ref, s_ref, l_ref, o_ref):
        acc = s_ref[...].astype(F32)
        for j in range(3):
            acc = acc + l_ref[j].astype(F32)
        o_ref[...] = acc

    return pl.pallas_call(
        body, name=name,
        grid_spec=pltpu.PrefetchScalarGridSpec(
            num_scalar_prefetch=1, grid=(rows // tm,),
            in_specs=[pl.BlockSpec((None, tm, cols), lambda i, ch: (ch[0], i, 0)),
                      pl.BlockSpec((3, tm, cols), lambda i, ch: (0, i, 0))],
            out_specs=pl.BlockSpec((tm, cols), lambda i, ch: (i, 0))),
        out_shape=_sds((rows, cols), F32), compiler_params=_params(("parallel",)),
    )(chip, pair_sums, landed)


_HBM = pl.BlockSpec(memory_space=pltpu.HBM)
_SEM = pl.BlockSpec(memory_space=pltpu.SEMAPHORE)
_EFFECT = pltpu.SideEffectType.DATAFLOW_SIDE_EFFECTING


def _chip_routes(n):
    def plan(x, y, c):
        routes = []
        for a in range(n):
            for j in range(1, 4):
                px, py = x ^ (j >> 1), y ^ (j & 1)
                routes.append((a, 2 * px + py, (px, py, c), j - 1))
        return routes
    return plan, 3 * n


def _bcast_routes(n):
    def plan(x, y, c):
        routes = []
        for a in range(n):
            for k in range(1, N_DEV):
                peer = (x ^ ((k >> 2) & 1), y ^ ((k >> 1) & 1), c ^ (k & 1))
                routes.append((a, 0, peer, 4 * x + 2 * y + c))
        return routes
    return plan, 7 * n


def _route_copies(srcs, lands, send_sems, recv_sems, routes):
    return [pltpu.make_async_remote_copy(
        src_ref=srcs[a].at[sb], dst_ref=lands[a].at[lb], send_sem=send_sems.at[r], recv_sem=recv_sems.at[r],
        device_id=peer, device_id_type=MESH_T) for r, (a, sb, peer, lb) in enumerate(routes)]


def _exchange_start(srcs, lands, routes, name):
    plan, count = routes
    n = len(srcs)

    def body(*refs):
        send_sems, recv_sems = refs[2 * n], refs[2 * n + 1]
        token = refs[-1]
        for cp in _route_copies(refs[:n], refs[n:2 * n], send_sems, recv_sems, plan(*_mesh_pos())):
            cp.start()
        token[...] = jnp.zeros_like(token)

    args = [pltpu.with_memory_space_constraint(a, pltpu.HBM) for a in list(srcs) + list(lands)]
    out = pl.pallas_call(
        body, name=name,
        out_shape=(pltpu.SemaphoreType.DMA((count,)), pltpu.SemaphoreType.DMA((count,)),
                   *[pltpu.HBM(a.shape, a.dtype) for a in args], _sds((8, 128), F32)),
        in_specs=[_HBM] * (2 * n), out_specs=(_SEM, _SEM, *([_HBM] * (2 * n)), pl.BlockSpec(memory_space=pltpu.VMEM)),
        input_output_aliases={i: 2 + i for i in range(2 * n)},
        compiler_params=pltpu.CompilerParams(has_side_effects=_EFFECT),
    )(*args)
    return (out[0], out[1], list(out[2:2 + 2 * n]), routes), out[-1]


def _exchange_wait(state, after, name):
    send_sems, recv_sems, bufs, (plan, count) = state
    n = len(bufs) // 2

    def body(*refs):
        send_s, recv_s = refs[2 * n], refs[2 * n + 1]
        for cp in _route_copies(refs[:n], refs[n:2 * n], send_s, recv_s, plan(*_mesh_pos())):
            cp.wait_send()
            cp.wait_recv()

    out = pl.pallas_call(
        body, name=name, out_shape=tuple(pltpu.HBM(a.shape, a.dtype) for a in bufs),
        in_specs=[_HBM] * (2 * n) + [_SEM, _SEM, pl.BlockSpec(memory_space=pl.ANY)], out_specs=tuple([_HBM] * (2 * n)),
        input_output_aliases={i: i for i in range(2 * n)},
        compiler_params=pltpu.CompilerParams(has_side_effects=_EFFECT),
    )(*bufs, send_sems, recv_sems, after)
    return list(out[:n]), list(out[n:])


def _reduce_scatter_start(parts, core, name):
    got = _pair_exchange(parts, name + "_pair")
    sums = [_pair_add(p, g, core, "%s_add_%d" % (name, i)) for i, (p, g) in enumerate(zip(parts, got))]
    lands = [lax.empty((3,) + s_.shape[1:], BF16) for s_ in sums]
    return _exchange_start(sums, lands, _chip_routes(len(sums)), name + "_start")


def _reduce_scatter_finish(rs_state, after, chip, name):
    sums, landed = _exchange_wait(rs_state, after, name + "_wait")
    return [_chip_sum(s_, l_, chip, "%s_sum_%d" % (name, i)) for i, (s_, l_) in enumerate(zip(sums, landed))]


def _local_step(x, c, ctx, c_ctx, norm_w, b_ada, ret_log2_decay, q_norm_w, k_norm_w, loss_target,
                w_ada16, w_in_t16, get_w_o, on_out_grads, on_in_grad, started=()):
    nb, seq, _ = x.shape
    cx = ctx.shape[1]
    t_rows, c_rows = nb * seq, nb * cx
    rows_all = t_rows + c_rows
    nc = seq // CH
    tm = _pick(seq, 256, 128)
    te = _pick(seq, 512, 128)
    assert cx % tm == 0 and t_rows % cx == 0 and seq % GRID_W == 0

    x2 = x.reshape(t_rows, D)
    ctx2 = ctx.reshape(c_rows, D)
    tgt = loss_target.reshape(t_rows, D)
    c8 = jnp.zeros((8, D), F32).at[:nb].set(c).at[nb].set(c_ctx)
    lg = _log_gamma(ret_log2_decay)
    cos, sin = _rope_tables(seq)

    mod = _mod_fwd(c8, w_ada16, b_ada)
    mod3 = mod[:, None, :]
    h_all = _norm_fwd(x2, mod3, norm_w, rows_all, 0, seq, 0, None, te, "norm_fwd")
    h_all = _norm_fwd(ctx2, mod3, norm_w, rows_all, t_rows, c_rows, nb, h_all, tm, "norm_fwd_ctx")
    px = _matmul(h_all, w_in_t16, tb=True, tm=1536, tn=1536, tk=D, out_dtype=F32, name="in_proj", after=started)
    s0f, s0b = _ctx_state(px, lg, nb, t_rows, cx)
    o_f, o_b, hist_f, hist_b = _ret_fwd(px, lg, s0f, s0b, nb, nc)
    yret16 = _ret_post(o_f, o_b, px, te)
    q16 = _qk_prep(px, q_norm_w, cos, sin, t_rows, 0, AQ, HQ, 4, seq, te, "q_prep")
    kx16 = _qk_prep(px, k_norm_w, cos, sin, t_rows, 0, AK, HKV, HKV, seq, te, "k_prep")
    kc16 = _qk_prep(px, k_norm_w, None, None, c_rows, t_rows, AK, HKV, HKV, seq, tm, "kc_prep")
    o_att, yatt16, lse = _att_fwd(q16, kx16, kc16, px, nb, seq, cx, tm)
    w_o_ret16, w_o_att16, w_out16 = get_w_o(lse)
    a_ret, a_att, y16 = _merge(yret16, yatt16, px, w_o_ret16, w_o_att16, te)
    dxn, dout16, dgate, loss_b = _outproj(y16, w_out16, x2, tgt, mod3, nb, seq, te)

    gw_out = _matmul(y16, dout16, ta=True, tm=D, tn=D, tk=D, out_dtype=BF16, name="gw_out")
    da_ret16, da_att16, dmr16, dma16 = _bwd_merge(dout16, w_out16, px, a_ret, a_att, te)
    gw_o_ret = _matmul(yret16, da_ret16, ta=True, tm=D, tn=D, tk=D, out_dtype=BF16, name="gw_o_ret")
    gw_o_att = _matmul(yatt16, da_att16, ta=True, tm=D, tn=D, tk=D, out_dtype=BF16, name="gw_o_att")
    out_state, out_started = on_out_grads([gw_o_ret, gw_o_att, gw_out])
    do16, drg16 = _bwd_branch_ret(da_ret16, w_o_ret16, px, o_f, o_b, te, after=out_started)
    dao, dag16 = _bwd_branch_att(da_att16, w_o_att16, px, o_att, te)
    dq_rot, dkx, dvx, dkc, dvc = _att_bwd(q16, kx16, kc16, px, dao, o_att, lse, nb, seq, cx, tm)
    daq16, gq = _qk_prep_bwd(dq_rot, px, q_norm_w, cos, sin, t_rows, 0, AQ, HQ, 4, seq, te, "q_prep_bwd")
    dak16, gk_lat = _qk_prep_bwd(dkx.reshape(t_rows, HKV * HD), px, k_norm_w, cos, sin, t_rows, 0, AK, HKV, HKV, seq, te,
                                 "k_prep_bwd")
    dcak16, gk_ctx = _qk_prep_bwd(dkc.reshape(c_rows, HKV * HD), px, k_norm_w, None, None, c_rows, t_rows, AK, HKV, HKV,
                                  seq, tm, "kc_prep_bwd")
    dq_f, dk_f, dv_f, dq_b, dk_b, dv_b, ds_f, ds_b, dlg_scan = _ret_bwd(px, lg, do16, hist_f, hist_b, nb, nc)
    dck16, dcv16, dlg_ctx = _ctx_state_bwd(px, lg, ds_f, ds_b, nb, t_rows, cx)
    dp_all = _assemble_lat(rows_all, dk_f, dk_b, dv_f, dv_b, dak16, dvx.reshape(t_rows, HKV * HD), dq_f, dq_b, drg16,
                           daq16, dag16, dmr16, dma16, tm)
    dp_all = _assemble_ctx(dp_all, dck16, dcv16, dcak16, dvc.reshape(c_rows, HKV * HD), t_rows, tm)
    gw_in_t = _matmul(dp_all, h_all, ta=True, tm=1536, tn=D, tk=1536, out_dtype=BF16, name="gw_in")
    in_state, in_started = on_in_grad(gw_in_t)
    dh = _matmul(dp_all, w_in_t16, tm=1536, tn=D, tk=1536, out_dtype=F32, name="d_h", after=in_started)
    grad_x, dsh, dsc, gnw_lat = _norm_bwd(dh, x2, mod3, norm_w, dxn, 0, seq, 0, te, "norm_bwd")
    dsh_c, dsc_c, gnw_ctx = _norm_bwd(dh, ctx2, mod3, norm_w, None, t_rows, c_rows, nb, tm, "norm_bwd_ctx")

    dlg = (jnp.sum(dlg_scan[:, :, 0], axis=0) + jnp.sum(dlg_ctx[:, :, :2, 0], axis=0).T.reshape(2 * RH)).reshape(1, 2 * RH)
    misc = jnp.concatenate([gq, gk_lat + gk_ctx, dlg, jnp.sum(loss_b[:, 0, 0]).reshape(1, 1),
                            jnp.zeros((1, D - 2 * HD - 2 * RH - 1), F32)], axis=1)
    rows = []
    for b in range(nb):
        rows += [dsh[b], dsc[b], dgate[b]]
    rows += [dsh_c[0], dsc_c[0]] + [c[b:b + 1] for b in range(nb)] + [gnw_lat + gnw_ctx, misc]
    payload = jnp.concatenate(rows + [jnp.zeros((PAY_ROWS - len(rows), D), F32)], axis=0)
    return grad_x.reshape(nb, seq, D), out_state, in_state, payload


def _finish_small(gathered, nb, c_ctx, ret_log2_decay, w_ada16, dev):
    n_dev = gathered.shape[0]
    loc = 3 * D // n_dev
    dmod_all = gathered[:, :3 * nb].reshape(n_dev * nb, 3 * D)
    dmodc_parts = jnp.concatenate([gathered[:, 3 * nb:3 * nb + 2].reshape(n_dev, 2 * D), jnp.zeros((n_dev, D), F32)], axis=1)
    c_all = gathered[:, 3 * nb + 2:4 * nb + 2].reshape(n_dev * nb, D)
    nw_parts = gathered[:, 4 * nb + 2]
    misc_parts = gathered[:, 4 * nb + 3]
    n_rows = n_dev * nb + n_dev
    pad = (-n_rows) % 16
    c_rows = jnp.concatenate([c_all, jnp.broadcast_to(c_ctx.reshape(1, D), (n_dev, D)), jnp.zeros((pad, D), F32)], axis=0)
    dm_rows = jnp.concatenate([dmod_all, dmodc_parts, jnp.zeros((pad, 3 * D), F32)], axis=0)
    dm_loc_rows = lax.dynamic_slice_in_dim(dm_rows, dev * loc, loc, axis=1)
    r_pad = jnp.full((1, D), -1.0, F32).at[:, 2 * HD:2 * HD + 2 * RH].set(ret_log2_decay.reshape(1, 2 * RH))
    gb, gc, gnw, misc, gwa = _small_final(dmod_all, dmodc_parts, c_rows, dm_loc_rows, nw_parts, misc_parts,
                                          c_ctx.reshape(1, D), r_pad, w_ada16)
    return (gb, gc, gnw, misc[:, :HD], misc[:, HD:2 * HD], misc[:, 2 * HD:2 * HD + 2 * RH], gwa,
            misc[0, 2 * HD + 2 * RH])


def kernel(x, c, ctx, c_ctx, norm_w, w_ada, b_ada, w_in, ret_log2_decay, q_norm_w, k_norm_w, w_o_ret, w_o_att, w_out, loss_target, m_c_ctx, m_norm_w, m_w_ada, m_b_ada, m_w_in, m_ret_log2_decay, m_q_norm_w, m_k_norm_w, m_w_o_ret, m_w_o_att, m_w_out, v_c_ctx, v_norm_w, v_w_ada, v_b_ada, v_w_in, v_ret_log2_decay, v_q_norm_w, v_k_norm_w, v_w_o_ret, v_w_o_att, v_w_out):
    nb = x.shape[0]
    mx, my, mc = _mesh_pos()
    dev = 4 * mx + 2 * my + mc
    core = jnp.reshape(mc, (1,)).astype(jnp.int32)
    chip = jnp.reshape(2 * mx + my, (1,)).astype(jnp.int32)

    w_in_t = jnp.transpose(w_in[0])
    g_in, g_ada = _all_gather([w_in_t.astype(BF16), w_ada[0].astype(BF16)], "gather_weights")
    w_in_t16 = g_in.reshape(IN_COLS, D)
    w_ada16 = jnp.transpose(g_ada, (1, 0, 2)).reshape(D, 3 * D)

    wo_shards = [w_[0].astype(BF16) for w_ in (w_o_ret, w_o_att, w_out)]
    wo_shards = list(lax.optimization_barrier((g_in, *wo_shards))[1:])
    wo_lands = [lax.dynamic_update_slice(lax.empty((N_DEV,) + s_.shape, BF16), s_[None], (dev, 0, 0)) for s_ in wo_shards]
    wo_state, wo_token = _exchange_start([s_[None] for s_ in wo_shards], wo_lands, _bcast_routes(3), "gather_wo_start")

    def get_w_o(after):
        _, (l_ret, l_att, l_out) = _exchange_wait(wo_state, after, "gather_wo_wait")
        return l_ret.reshape(RH * DV, D), l_att.reshape(D, D), l_out.reshape(D, D)

    def on_out_grads(grads):
        parts = [g_.reshape(N_DEV, g_.shape[0] // N_DEV, D) for g_ in grads]
        state, token = _reduce_scatter_start(parts, core, "rs_out")
        return state, (token,)

    def on_in_grad(grad):
        state, token = _reduce_scatter_start([grad.reshape(N_DEV, IN_COLS // N_DEV, D)], core, "rs_in")
        return state, (token,)

    grad_x, out_state, in_state, payload = _local_step(
        x, c, ctx, c_ctx, norm_w, b_ada, ret_log2_decay, q_norm_w, k_norm_w, loss_target,
        w_ada16, w_in_t16, get_w_o, on_out_grads, on_in_grad, started=(wo_token,))

    (gathered,) = _all_gather([payload], "gather_small")
    gb, gc, gnw, gq, gk, gr, gwa, loss = _finish_small(gathered, nb, c_ctx, ret_log2_decay, w_ada16, dev)

    g_w_o_ret, g_w_o_att, g_w_out = _reduce_scatter_finish(out_state, gathered, chip, "rs_out")
    (g_w_in_t,) = _reduce_scatter_finish(in_state, gathered, chip, "rs_in")

    grads = [gc.reshape(c_ctx.shape), gnw, gwa[None], gb, g_w_in_t, gr.reshape(ret_log2_decay.shape), gq, gk,
             g_w_o_ret[None], g_w_o_att[None], g_w_out[None]]
    weights = [c_ctx, norm_w, w_ada, b_ada, w_in_t, ret_log2_decay, q_norm_w, k_norm_w, w_o_ret, w_o_att, w_out]
    ms = [m_c_ctx, m_norm_w, m_w_ada, m_b_ada, jnp.transpose(m_w_in[0]), m_ret_log2_decay, m_q_norm_w, m_k_norm_w,
          m_w_o_ret, m_w_o_att, m_w_out]
    vs = [v_c_ctx, v_norm_w, v_w_ada, v_b_ada, jnp.transpose(v_w_in[0]), v_ret_log2_decay, v_q_norm_w, v_k_norm_w,
          v_w_o_ret, v_w_o_att, v_w_out]
    deltas, new_ms, new_vs = [], [], []
    for i, (w, g, m, v) in enumerate(zip(weights, grads, ms, vs)):
        shape2 = (-1, w.shape[-1])
        res = _adamw(w.reshape(shape2), g.reshape(shape2), m.reshape(shape2), v.reshape(shape2), "adamw_%d" % i)
        for lst, r in zip((deltas, new_ms, new_vs), res):
            lst.append(jnp.transpose(r)[None] if i == 4 else r.reshape(w.shape))
    grads[4] = jnp.transpose(g_w_in_t)[None]
    return (loss, grad_x, *grads, *deltas, *new_ms, *new_vs)
```

```python
import numpy as np
import jax
import jax.numpy as jnp
from jax import lax
from jax.experimental import pallas as pl
from jax.experimental.pallas import tpu as pltpu

F32 = jnp.float32
BF16 = jnp.bfloat16

D = 1024
RH, DK, DV, CH = 4, 256, 512, 256
HQ, HKV, HD = 8, 2, 128
GRID_W = 64
ROPE_THETA = 10000.0
EPS = 1e-6
RK, RV, AK, AV, RQ, RG, AQ, AG, MR, MA = 0, 1024, 3072, 3328, 3584, 4608, 6656, 7680, 8704, 9728
IN_COLS = 10752
KV_COLS = 3584
N_DEV = 8
LR, B1, B2, ADAM_EPS, WD, STEP = 0.001, 0.9, 0.999, 1e-08, 0.01, 10
PAY_ROWS = 16
VMEM_LIMIT = 56 * 1024 * 1024
MESH_T = pl.DeviceIdType.MESH

NT = (((1,), (1,)), ((), ()))
TN = (((0,), (0,)), ((), ()))
SM_C = (HD ** -0.5) * float(np.log2(np.e))


def _params(sem):
    return pltpu.CompilerParams(dimension_semantics=sem, vmem_limit_bytes=VMEM_LIMIT)


def _pick(n, target, mult=8):
    best = None
    for t in range(mult, min(n, target) + 1, mult):
        if n % t == 0:
            best = t
    return best or n


def _dot(a, b, dn=None):
    if dn is None:
        return jnp.dot(a, b, preferred_element_type=F32)
    return lax.dot_general(a, b, dn, preferred_element_type=F32)


def _sig(v):
    return jax.nn.sigmoid(v)


def _silu(v):
    return v * _sig(v)


def _dsilu(v):
    s = _sig(v)
    return s * (1.0 + v * (1.0 - s))


def _sds(shape, dtype):
    return jax.ShapeDtypeStruct(shape, dtype)


def _matmul(a, b, *, ta=False, tb=False, tm, tn, tk, out_dtype, name, after=()):
    m = a.shape[1] if ta else a.shape[0]
    kdim = a.shape[0] if ta else a.shape[1]
    n = b.shape[0] if tb else b.shape[1]
    tm, tn, tk = _pick(m, tm, 128), _pick(n, tn, 128), _pick(kdim, tk, 128)
    nk = kdim // tk
    dn = (((0 if ta else 1,), (1 if tb else 0,)), ((), ()))

    def body(a_ref, b_ref, *rest):
        o_ref, acc_ref = rest[-2:]
        k = pl.program_id(2)
        part = _dot(a_ref[...].astype(BF16), b_ref[...].astype(BF16), dn)
        if nk == 1:
            o_ref[...] = part.astype(o_ref.dtype)
        else:
            @pl.when(k == 0)
            def _():
                acc_ref[...] = part

            @pl.when(k > 0)
            def _():
                acc_ref[...] += part

            @pl.when(k == nk - 1)
            def _():
                o_ref[...] = acc_ref[...].astype(o_ref.dtype)

    a_spec = pl.BlockSpec((tk, tm), lambda i, j, k: (k, i)) if ta else pl.BlockSpec((tm, tk), lambda i, j, k: (i, k))
    b_spec = pl.BlockSpec((tn, tk), lambda i, j, k: (j, k)) if tb else pl.BlockSpec((tk, tn), lambda i, j, k: (k, j))
    return pl.pallas_call(
        body, name=name, grid=(m // tm, n // tn, nk),
        in_specs=[a_spec, b_spec] + [pl.BlockSpec(memory_space=pl.ANY)] * len(after),
        out_specs=pl.BlockSpec((tm, tn), lambda i, j, k: (i, j)), out_shape=_sds((m, n), out_dtype),
        scratch_shapes=[pltpu.VMEM((tm, tn) if nk > 1 else (8, 128), F32)],
        compiler_params=_params(("parallel", "parallel", "arbitrary")),
    )(a, b, *after)


def _log_gamma(r):
    rp = jnp.full((8, 128), -1.0, F32).at[:2, :RH].set(r.reshape(2, RH))

    def body(r_ref, o_ref):
        o_ref[...] = jnp.log1p(-jnp.exp2(r_ref[...]))

    out = pl.pallas_call(body, name="log_gamma", out_shape=_sds((8, 128), F32))(rp)
    return out[:2, :RH]


def _mod_fwd(c8, w_ada16, b_ada):
    def body(c_ref, w_ref, b_ref, o_ref):
        o_ref[...] = _dot(_silu(c_ref[...]).astype(BF16), w_ref[...]) + b_ref[...]

    return pl.pallas_call(
        body, name="mod_fwd", grid=(3,),
        in_specs=[pl.BlockSpec((8, D), lambda j: (0, 0)), pl.BlockSpec((D, D), lambda j: (0, j)),
                  pl.BlockSpec((1, D), lambda j: (0, j))],
        out_specs=pl.BlockSpec((8, D), lambda j: (0, j)), out_shape=_sds((8, 3 * D), F32),
        compiler_params=_params(("arbitrary",)),
    )(c8, w_ada16, b_ada)


def _norm_fwd(x2, mod3, norm_w, rows_all, row_off, rows_per_group, group0, h_prev, tm, name):
    rows = x2.shape[0]
    rb0 = row_off // tm
    bpg = rows_per_group // tm

    def body(*refs):
        x_ref, sh_ref, sc_ref, nw_ref, o_ref = refs[-5:]
        xv = x_ref[...]
        r = lax.rsqrt(jnp.mean(xv * xv, axis=-1, keepdims=True) + EPS)
        o_ref[...] = ((xv * r) * nw_ref[...] * (1.0 + sc_ref[...]) + sh_ref[...]).astype(BF16)

    in_specs = [pl.BlockSpec((tm, D), lambda i: (i, 0)),
                pl.BlockSpec((None, 1, D), lambda i: (group0 + i // bpg, 0, 0)),
                pl.BlockSpec((None, 1, D), lambda i: (group0 + i // bpg, 0, 1)),
                pl.BlockSpec((1, D), lambda i: (0, 0))]
    args = [x2, mod3, mod3, norm_w]
    alias = {}
    if h_prev is not None:
        in_specs.insert(0, pl.BlockSpec(memory_space=pl.ANY))
        args.insert(0, h_prev)
        alias = {0: 0}
    return pl.pallas_call(
        body, name=name, grid=(rows // tm,), in_specs=in_specs,
        out_specs=pl.BlockSpec((tm, D), lambda i: (rb0 + i, 0)), out_shape=_sds((rows_all, D), BF16),
        input_output_aliases=alias, compiler_params=_params(("parallel",)),
    )(*args)


def _decays(lg, fwd):
    ii = lax.broadcasted_iota(jnp.int32, (CH, CH), 0)
    jj = lax.broadcasted_iota(jnp.int32, (CH, CH), 1)
    ri = lax.broadcasted_iota(jnp.int32, (CH, 1), 0).astype(F32)
    rel = (ii - jj) if fwd else (jj - ii)
    relf = jnp.maximum(rel, 0).astype(F32)
    mask = jnp.where(rel >= 0, jnp.exp(lg * relf), 0.0)
    qe = (ri + 1.0) if fwd else (CH - ri)
    ke = (CH - 1.0 - ri) if fwd else ri
    return mask, relf, jnp.exp(lg * qe), qe, jnp.exp(lg * ke), ke


def _wide_specs(rowf):
    return [pl.BlockSpec((CH, 2 * DK), lambda b, c: (rowf(b, c), RQ // (2 * DK))),
            pl.BlockSpec((CH, 2 * DK), lambda b, c: (rowf(b, c), RQ // (2 * DK) + 1)),
            pl.BlockSpec((CH, RH * DK), lambda b, c: (rowf(b, c), RK // (RH * DK))),
            pl.BlockSpec((CH, 2 * DV), lambda b, c: (rowf(b, c), RV // (2 * DV))),
            pl.BlockSpec((CH, 2 * DV), lambda b, c: (rowf(b, c), RV // (2 * DV) + 1))]


def _head_qkv(refs, h):
    q0, q1, k, v0, v1 = refs
    lo = h % 2
    q = (q0, q1)[h // 2][:, lo * DK:(lo + 1) * DK].astype(F32)
    kk = k[:, h * DK:(h + 1) * DK].astype(F32) * (DK ** -0.5)
    v16 = (v0, v1)[h // 2][:, lo * DV:(lo + 1) * DV].astype(BF16)
    return q, kk, v16


def _ctx_state(px, lg, nb, t_rows, cx):
    rb = t_rows // cx

    def body(lg_ref, k_ref, v_ref, sf_ref, sb_ref):
        h = pl.program_id(1)
        pos = lax.broadcasted_iota(jnp.int32, (cx, 1), 0).astype(F32)
        k = k_ref[...].astype(F32) * (DK ** -0.5)
        v16 = v_ref[...].astype(BF16)
        wf = jnp.exp(lg_ref[0, h] * (cx - 1.0 - pos))
        wb = jnp.exp(lg_ref[1, h] * pos)
        sf_ref[...] = _dot((k * wf).astype(BF16), v16, TN)
        sb_ref[...] = _dot((k * wb).astype(BF16), v16, TN)

    st = pl.BlockSpec((None, None, DK, DV), lambda b, h: (b, h, 0, 0))
    return pl.pallas_call(
        body, name="ctx_state", grid=(nb, RH),
        in_specs=[pl.BlockSpec(memory_space=pltpu.SMEM),
                  pl.BlockSpec((cx, DK), lambda b, h: (rb + b, RK // DK + h)),
                  pl.BlockSpec((cx, DV), lambda b, h: (rb + b, RV // DV + h))],
        out_specs=[st, st], out_shape=[_sds((nb, RH, DK, DV), F32)] * 2,
        compiler_params=_params(("parallel", "parallel")),
    )(lg, px, px)


def _ret_fwd(px, lg, s0f, s0b, nb, nc):
    t_rows = nb * nc * CH

    def body(lg_ref, *refs):
        ins = (refs[0:5], refs[5:10])
        s0f_ref, s0b_ref, of_ref, ob_ref, hf_ref, hb_ref, sf, sb = refs[10:]
        c = pl.program_id(1)

        @pl.when(c == 0)
        def _():
            sf[...] = s0f_ref[...]
            sb[...] = s0b_ref[...]

        for d, (o_ref, h_ref, s) in enumerate(((of_ref, hf_ref, sf), (ob_ref, hb_ref, sb))):
            for h in range(RH):
                lg_d = lg_ref[d, h]
                mask, _, qd, _, kd, _ = _decays(lg_d, d == 0)
                q, k, v16 = _head_qkv(ins[d], h)
                a = _dot(q.astype(BF16), k.astype(BF16), NT)
                st = s[h]
                st16 = st.astype(BF16)
                h_ref[h] = st16
                o = _dot((a * mask).astype(BF16), v16) + _dot((q * qd).astype(BF16), st16)
                o_ref[:, h * DV:(h + 1) * DV] = o.astype(BF16)
                s[h] = st * jnp.exp(lg_d * CH) + _dot((k * kd).astype(BF16), v16, TN)

    def fw(b, c):
        return b * nc + c

    def bw(b, c):
        return b * nc + nc - 1 - c

    st = pl.BlockSpec((None, RH, DK, DV), lambda b, c: (b, 0, 0, 0))
    in_specs = [pl.BlockSpec(memory_space=pltpu.SMEM)] + _wide_specs(fw) + _wide_specs(bw) + [st, st]
    out_specs = [pl.BlockSpec((CH, RH * DV), lambda b, c: (fw(b, c), 0)),
                 pl.BlockSpec((CH, RH * DV), lambda b, c: (bw(b, c), 0)),
                 pl.BlockSpec((None, None, RH, DK, DV), lambda b, c: (b, c, 0, 0, 0)),
                 pl.BlockSpec((None, None, RH, DK, DV), lambda b, c: (b, nc - 1 - c, 0, 0, 0))]
    return pl.pallas_call(
        body, name="ret_fwd", grid=(nb, nc), in_specs=in_specs, out_specs=out_specs,
        out_shape=[_sds((t_rows, RH * DV), BF16)] * 2 + [_sds((nb, nc, RH, DK, DV), BF16)] * 2,
        scratch_shapes=[pltpu.VMEM((RH, DK, DV), F32), pltpu.VMEM((RH, DK, DV), F32)],
        compiler_params=_params(("parallel", "arbitrary")),
    )(lg, *([px] * 10), s0f, s0b)


def _ret_post(o_f, o_b, px, tm):
    t_rows = o_f.shape[0]

    def body(of_ref, ob_ref, g_ref, y_ref):
        o = of_ref[...].astype(F32) + ob_ref[...].astype(F32)
        r = lax.rsqrt(jnp.mean(o * o, axis=-1, keepdims=True) + EPS)
        y_ref[...] = ((o * r) * _silu(g_ref[...].astype(F32))).astype(BF16)

    blk = pl.BlockSpec((tm, DV), lambda i, h: (i, h))
    return pl.pallas_call(
        body, name="ret_post", grid=(t_rows // tm, RH),
        in_specs=[blk, blk, pl.BlockSpec((tm, DV), lambda i, h: (i, RG // DV + h))],
        out_specs=blk, out_shape=_sds((t_rows, RH * DV), BF16),
        compiler_params=_params(("parallel", "parallel")),
    )(o_f, o_b, px)


def _rope_tables(seq):
    rows = seq // GRID_W
    row = np.repeat(np.arange(rows, dtype=np.float32), GRID_W)
    col = np.tile(np.arange(GRID_W, dtype=np.float32), rows)
    half = HD // 2
    freqs = (ROPE_THETA ** (-np.arange(0, half, 2, dtype=np.float32) / half)).astype(np.float32)
    ang = np.concatenate([row[:, None] * freqs, col[:, None] * freqs], axis=-1).astype(np.float32)
    cos = np.repeat(np.cos(ang), 2, axis=-1).astype(np.float32)
    sin = np.repeat(np.sin(ang), 2, axis=-1).astype(np.float32)
    sign = np.tile(np.array([-1.0, 1.0], np.float32), HD // 2)
    return jnp.asarray(cos), jnp.asarray(sin * sign)


def _swap_pairs(v):
    lane = lax.broadcasted_iota(jnp.int32, v.shape, 1)
    return jnp.where((lane & 1) == 0, pltpu.roll(v, HD - 1, 1), pltpu.roll(v, 1, 1))


def _qk_prep(px, nw, cos, sin, rows, row_off, col_off, heads, hb, seq, tm, name):
    rope = cos is not None
    rb0 = row_off // tm
    pb = seq // tm if rope else 1
    bw = hb * HD

    def body(*refs):
        if rope:
            x_ref, w_ref, c_ref, s_ref, o_ref = refs
        else:
            x_ref, w_ref, o_ref = refs
        for h in range(hb):
            sl = slice(h * HD, (h + 1) * HD)
            xv = x_ref[:, sl].astype(F32)
            r = lax.rsqrt(jnp.mean(xv * xv, axis=-1, keepdims=True) + EPS)
            t = (xv * r) * w_ref[...]
            if rope:
                t = t * c_ref[...] + _swap_pairs(t) * s_ref[...]
            o_ref[:, sl] = t.astype(BF16)

    in_specs = [pl.BlockSpec((tm, bw), lambda i, j: (rb0 + i, col_off // bw + j)),
                pl.BlockSpec((1, HD), lambda i, j: (0, 0))]
    args = [px, nw]
    if rope:
        in_specs += [pl.BlockSpec((tm, HD), lambda i, j: (i % pb, 0))] * 2
        args += [cos, sin]
    return pl.pallas_call(
        body, name=name, grid=(rows // tm, heads // hb), in_specs=in_specs,
        out_specs=pl.BlockSpec((tm, bw), lambda i, j: (i, j)), out_shape=_sds((rows, heads * HD), BF16),
        compiler_params=_params(("parallel", "parallel")),
    )(*args)


def _att_fwd(q16, kx16, kc16, px, nb, seq, cx, tq):
    t_rows = nb * seq
    nq = seq // tq
    rep = HQ // HKV
    gw = rep * HD

    def body(q_ref, kx_ref, kc_ref, vx_ref, vc_ref, g_ref, o_ref, y_ref, l_ref):
        kx = kx_ref[...]
        kc = kc_ref[...]
        vx = vx_ref[...].astype(BF16)
        vc = vc_ref[...].astype(BF16)
        l_ref[...] = jnp.zeros_like(l_ref)
        for r in range(rep):
            sl = slice(r * HD, (r + 1) * HD)
            q = q_ref[:, sl]
            s1 = _dot(q, kx, NT)
            s2 = _dot(q, kc, NT)
            m = jnp.maximum(jnp.max(s1, axis=-1, keepdims=True), jnp.max(s2, axis=-1, keepdims=True))
            e1 = jnp.exp2((s1 - m) * SM_C)
            e2 = jnp.exp2((s2 - m) * SM_C)
            tot = jnp.sum(e1, axis=-1, keepdims=True) + jnp.sum(e2, axis=-1, keepdims=True)
            o = (_dot(e1.astype(BF16), vx) + _dot(e2.astype(BF16), vc)) * (1.0 / tot)
            o_ref[:, sl] = o
            y_ref[:, sl] = (o * _silu(g_ref[:, sl].astype(F32))).astype(BF16)
            l_ref[:, r:r + 1] = m * SM_C + jnp.log(tot) * float(np.log2(np.e))

    qblk = pl.BlockSpec((tq, gw), lambda b, g, i: (b * nq + i, g))
    return pl.pallas_call(
        body, name="att_fwd", grid=(nb, HKV, nq),
        in_specs=[qblk,
                  pl.BlockSpec((seq, HD), lambda b, g, i: (b, g)),
                  pl.BlockSpec((cx, HD), lambda b, g, i: (b, g)),
                  pl.BlockSpec((seq, HD), lambda b, g, i: (b, AV // HD + g)),
                  pl.BlockSpec((cx, HD), lambda b, g, i: (t_rows // cx + b, AV // HD + g)),
                  pl.BlockSpec((tq, gw), lambda b, g, i: (b * nq + i, AG // gw + g))],
        out_specs=[qblk, qblk, pl.BlockSpec((tq, 128), lambda b, g, i: (b * nq + i, g))],
        out_shape=[_sds((t_rows, D), F32), _sds((t_rows, D), BF16), _sds((t_rows, HKV * 128), F32)],
        compiler_params=_params(("parallel", "parallel", "parallel")),
    )(q16, kx16, kc16, px, px, px)


def _merge(yret16, yatt16, px, w_o_ret16, w_o_att16, tm):
    t_rows = yret16.shape[0]
    hw = D // 2

    def body(yr_ref, wr_ref, ya_ref, wa_ref, mr_ref, ma_ref, ar_ref, aa_ref, y_ref):
        ar = _dot(yr_ref[...], wr_ref[...])
        aa = _dot(ya_ref[...], wa_ref[...])
        ar_ref[...] = ar
        aa_ref[...] = aa
        y_ref[...] = (_sig(mr_ref[...].astype(F32)) * ar + _sig(ma_ref[...].astype(F32)) * aa).astype(BF16)

    half = pl.BlockSpec((tm, hw), lambda i, j: (i, j))
    return pl.pallas_call(
        body, name="merge", grid=(t_rows // tm, 2),
        in_specs=[pl.BlockSpec((tm, RH * DV), lambda i, j: (i, 0)), pl.BlockSpec((RH * DV, hw), lambda i, j: (0, j)),
                  pl.BlockSpec((tm, D), lambda i, j: (i, 0)), pl.BlockSpec((D, hw), lambda i, j: (0, j)),
                  pl.BlockSpec((tm, hw), lambda i, j: (i, MR // hw + j)),
                  pl.BlockSpec((tm, hw), lambda i, j: (i, MA // hw + j))],
        out_specs=[half, half, half],
        out_shape=[_sds((t_rows, D), F32), _sds((t_rows, D), F32), _sds((t_rows, D), BF16)],
        compiler_params=_params(("parallel", "parallel")),
    )(yret16, w_o_ret16, yatt16, w_o_att16, px, px)


def _outproj(y16, w_out16, x2, tgt, mod3, nb, seq, tm):
    t_rows = nb * seq
    bpb = seq // tm

    def body(y_ref, w_ref, x_ref, t_ref, g_ref, dxn_ref, dout_ref, dg_ref, loss_ref):
        i = pl.program_id(1)
        out = _dot(y_ref[...], w_ref[...])
        gate = g_ref[...]
        diff = x_ref[...] + gate * out - t_ref[...]
        dxn = diff * (1.0 / D)
        dxn_ref[...] = dxn
        dout_ref[...] = (gate * dxn).astype(BF16)
        dg = jnp.sum(dxn * out, axis=0, keepdims=True)
        ls = jnp.broadcast_to(jnp.sum(diff * diff) * (0.5 / D), (1, 128))

        @pl.when(i == 0)
        def _():
            dg_ref[...] = dg
            loss_ref[...] = ls

        @pl.when(i > 0)
        def _():
            dg_ref[...] += dg
            loss_ref[...] += ls

    row = pl.BlockSpec((tm, D), lambda b, i: (b * bpb + i, 0))
    return pl.pallas_call(
        body, name="outproj", grid=(nb, bpb),
        in_specs=[row, pl.BlockSpec((D, D), lambda b, i: (0, 0)), row, row,
                  pl.BlockSpec((None, 1, D), lambda b, i: (b, 0, 2))],
        out_specs=[row, row, pl.BlockSpec((None, 1, D), lambda b, i: (b, 0, 0)),
                   pl.BlockSpec((None, 1, 128), lambda b, i: (b, 0, 0))],
        out_shape=[_sds((t_rows, D), F32), _sds((t_rows, D), BF16), _sds((nb, 1, D), F32), _sds((nb, 1, 128), F32)],
        compiler_params=_params(("parallel", "arbitrary")),
    )(y16, w_out16, x2, tgt, mod3)


def _bwd_merge(dout16, w_out16, px, a_ret, a_att, tm):
    t_rows = dout16.shape[0]
    hw = D // 2

    def body(do_ref, w_ref, mr_ref, ma_ref, ar_ref, aa_ref, dar_ref, daa_ref, dmr_ref, dma_ref):
        dy = _dot(do_ref[...], w_ref[...], NT)
        sr = _sig(mr_ref[...].astype(F32))
        sa = _sig(ma_ref[...].astype(F32))
        dar_ref[...] = (dy * sr).astype(BF16)
        daa_ref[...] = (dy * sa).astype(BF16)
        dmr_ref[...] = (dy * ar_ref[...] * sr * (1.0 - sr)).astype(BF16)
        dma_ref[...] = (dy * aa_ref[...] * sa * (1.0 - sa)).astype(BF16)

    half = pl.BlockSpec((tm, hw), lambda i, j: (i, j))
    return pl.pallas_call(
        body, name="bwd_merge", grid=(t_rows // tm, 2),
        in_specs=[pl.BlockSpec((tm, D), lambda i, j: (i, 0)), pl.BlockSpec((hw, D), lambda i, j: (j, 0)),
                  pl.BlockSpec((tm, hw), lambda i, j: (i, MR // hw + j)),
                  pl.BlockSpec((tm, hw), lambda i, j: (i, MA // hw + j)), half, half],
        out_specs=[half] * 4, out_shape=[_sds((t_rows, D), BF16)] * 4,
        compiler_params=_params(("parallel", "parallel")),
    )(dout16, w_out16, px, px, a_ret, a_att)


def _bwd_branch_ret(da_ret16, w_o_ret16, px, o_f, o_b, tm, after=()):
    t_rows = da_ret16.shape[0]

    def body(da_ref, w_ref, g_ref, of_ref, ob_ref, *rest):
        do_ref, dg_ref = rest[-2:]
        dy = _dot(da_ref[...], w_ref[...], NT)
        g = g_ref[...].astype(F32)
        o = of_ref[...].astype(F32) + ob_ref[...].astype(F32)
        r = lax.rsqrt(jnp.mean(o * o, axis=-1, keepdims=True) + EPS)
        on = o * r
        don = dy * _silu(g)
        dg_ref[...] = (dy * on * _dsilu(g)).astype(BF16)
        do_ref[...] = (r * (don - on * jnp.mean(on * don, axis=-1, keepdims=True))).astype(BF16)

    blk = pl.BlockSpec((tm, DV), lambda i, h: (i, h))
    return pl.pallas_call(
        body, name="bwd_branch_ret", grid=(t_rows // tm, RH),
        in_specs=[pl.BlockSpec((tm, D), lambda i, h: (i, 0)), pl.BlockSpec((DV, D), lambda i, h: (h, 0)),
                  pl.BlockSpec((tm, DV), lambda i, h: (i, RG // DV + h)), blk, blk]
        + [pl.BlockSpec(memory_space=pl.ANY)] * len(after),
        out_specs=[blk, blk], out_shape=[_sds((t_rows, RH * DV), BF16)] * 2,
        compiler_params=_params(("parallel", "parallel")),
    )(da_ret16, w_o_ret16, px, o_f, o_b, *after)


def _bwd_branch_att(da_att16, w_o_att16, px, o_att, tm):
    t_rows = da_att16.shape[0]
    hw = D // 2

    def body(da_ref, w_ref, g_ref, o_ref, dao_ref, dg_ref):
        dy = _dot(da_ref[...], w_ref[...], NT)
        g = g_ref[...].astype(F32)
        dao_ref[...] = dy * _silu(g)
        dg_ref[...] = (dy * o_ref[...] * _dsilu(g)).astype(BF16)

    half = pl.BlockSpec((tm, hw), lambda i, j: (i, j))
    return pl.pallas_call(
        body, name="bwd_branch_att", grid=(t_rows // tm, 2),
        in_specs=[pl.BlockSpec((tm, D), lambda i, j: (i, 0)), pl.BlockSpec((hw, D), lambda i, j: (j, 0)),
                  pl.BlockSpec((tm, hw), lambda i, j: (i, AG // hw + j)), half],
        out_specs=[half, half], out_shape=[_sds((t_rows, D), F32), _sds((t_rows, D), BF16)],
        compiler_params=_params(("parallel", "parallel")),
    )(da_att16, w_o_att16, px, o_att)


def _att_bwd(q16, kx16, kc16, px, dao, o_att, lse, nb, seq, cx, tq):
    t_rows = nb * seq
    nq = seq // tq
    rep = HQ // HKV
    gw = rep * HD
    scale = HD ** -0.5

    def body(q_ref, kx_ref, kc_ref, vx_ref, vc_ref, dao_ref, o_ref, l_ref, dq_ref, dkx_ref, dvx_ref, dkc_ref, dvc_ref):
        i = pl.program_id(2)
        kx = kx_ref[...]
        kc = kc_ref[...]
        vx = vx_ref[...].astype(BF16)
        vc = vc_ref[...].astype(BF16)
        dkx = jnp.zeros((seq, HD), F32)
        dvx = jnp.zeros((seq, HD), F32)
        dkc = jnp.zeros((cx, HD), F32)
        dvc = jnp.zeros((cx, HD), F32)
        for r in range(rep):
            sl = slice(r * HD, (r + 1) * HD)
            q = q_ref[:, sl]
            lr = l_ref[:, r:r + 1]
            p1 = jnp.exp2(_dot(q, kx, NT) * SM_C - lr)
            p2 = jnp.exp2(_dot(q, kc, NT) * SM_C - lr)
            da = dao_ref[:, sl]
            da16 = da.astype(BF16)
            delta = jnp.sum(da * o_ref[:, sl], axis=-1, keepdims=True)
            ds1 = (p1 * (_dot(da16, vx, NT) - delta)).astype(BF16)
            ds2 = (p2 * (_dot(da16, vc, NT) - delta)).astype(BF16)
            dq_ref[:, sl] = (_dot(ds1, kx) + _dot(ds2, kc)) * scale
            dkx += _dot(ds1, q, TN)
            dkc += _dot(ds2, q, TN)
            dvx += _dot(p1.astype(BF16), da16, TN)
            dvc += _dot(p2.astype(BF16), da16, TN)
        dkx = dkx * scale
        dkc = dkc * scale

        @pl.when(i == 0)
        def _():
            dkx_ref[...] = dkx
            dvx_ref[...] = dvx
            dkc_ref[...] = dkc
            dvc_ref[...] = dvc

        @pl.when(i > 0)
        def _():
            dkx_ref[...] += dkx
            dvx_ref[...] += dvx
            dkc_ref[...] += dkc
            dvc_ref[...] += dvc

    qblk = pl.BlockSpec((tq, gw), lambda b, g, i: (b * nq + i, g))
    kxb = pl.BlockSpec((None, seq, HD), lambda b, g, i: (b, 0, g))
    kcb = pl.BlockSpec((None, cx, HD), lambda b, g, i: (b, 0, g))
    return pl.pallas_call(
        body, name="att_bwd", grid=(nb, HKV, nq),
        in_specs=[qblk,
                  pl.BlockSpec((seq, HD), lambda b, g, i: (b, g)),
                  pl.BlockSpec((cx, HD), lambda b, g, i: (b, g)),
                  pl.BlockSpec((seq, HD), lambda b, g, i: (b, AV // HD + g)),
                  pl.BlockSpec((cx, HD), lambda b, g, i: (t_rows // cx + b, AV // HD + g)),
                  qblk, qblk, pl.BlockSpec((tq, 128), lambda b, g, i: (b * nq + i, g))],
        out_specs=[qblk, kxb, kxb, kcb, kcb],
        out_shape=[_sds((t_rows, D), F32), _sds((nb, seq, HKV * HD), F32), _sds((nb, seq, HKV * HD), F32),
                   _sds((nb, cx, HKV * HD), F32), _sds((nb, cx, HKV * HD), F32)],
        compiler_params=_params(("parallel", "parallel", "arbitrary")),
    )(q16, kx16, kc16, px, px, dao, o_att, lse)


def _qk_prep_bwd(dt, px, nw, cos, sin, rows, row_off, col_off, heads, hb, seq, tm, name):
    rope = cos is not None
    rb0 = row_off // tm
    pb = seq // tm if rope else 1
    bw = hb * HD

    def body(*refs):
        if rope:
            d_ref, x_ref, w_ref, c_ref, s_ref, dx_ref, dw_ref = refs
        else:
            d_ref, x_ref, w_ref, dx_ref, dw_ref = refs
        first = jnp.logical_and(pl.program_id(0) == 0, pl.program_id(1) == 0)
        dw = jnp.zeros((1, HD), F32)
        for h in range(hb):
            sl = slice(h * HD, (h + 1) * HD)
            dtv = d_ref[:, sl]
            if rope:
                dtv = dtv * c_ref[...] + _swap_pairs(dtv * s_ref[...])
            xv = x_ref[:, sl].astype(F32)
            r = lax.rsqrt(jnp.mean(xv * xv, axis=-1, keepdims=True) + EPS)
            xh = xv * r
            dxh = dtv * w_ref[...]
            dx_ref[:, sl] = (r * (dxh - xh * jnp.mean(dxh * xh, axis=-1, keepdims=True))).astype(BF16)
            dw += jnp.sum(dtv * xh, axis=0, keepdims=True)

        @pl.when(first)
        def _():
            dw_ref[...] = dw

        @pl.when(jnp.logical_not(first))
        def _():
            dw_ref[...] += dw

    blk = pl.BlockSpec((tm, bw), lambda i, j: (i, j))
    in_specs = [blk, pl.BlockSpec((tm, bw), lambda i, j: (rb0 + i, col_off // bw + j)),
                pl.BlockSpec((1, HD), lambda i, j: (0, 0))]
    args = [dt, px, nw]
    if rope:
        in_specs += [pl.BlockSpec((tm, HD), lambda i, j: (i % pb, 0))] * 2
        args += [cos, sin]
    return pl.pallas_call(
        body, name=name, grid=(rows // tm, heads // hb), in_specs=in_specs,
        out_specs=[blk, pl.BlockSpec((1, HD), lambda i, j: (0, 0))],
        out_shape=[_sds((rows, heads * HD), BF16), _sds((1, HD), F32)],
        compiler_params=_params(("arbitrary", "arbitrary")),
    )(*args)


def _ret_bwd(px, lg, do16, hist_f, hist_b, nb, nc):
    t_rows = nb * nc * CH

    def body(lg_ref, *refs):
        ins = (refs[0:5], refs[7:12])
        do_refs = (refs[5], refs[12])
        h_refs = (refs[6], refs[13])
        outs = (refs[14:17], refs[17:20])
        ds_outs = (refs[20], refs[21])
        dlg_ref = refs[22]
        dss = (refs[23], refs[24])
        c = pl.program_id(1)

        @pl.when(c == 0)
        def _():
            dss[0][...] = jnp.zeros_like(dss[0])
            dss[1][...] = jnp.zeros_like(dss[1])
            dlg_ref[...] = jnp.zeros_like(dlg_ref)

        for d in range(2):
            dq_ref, dk_ref, dv_ref = outs[d]
            for h in range(RH):
                lg_d = lg_ref[d, h]
                mask, relf, qd, qe, kd, ke = _decays(lg_d, d == 0)
                g_ch = jnp.exp(lg_d * CH)
                q, k, v16 = _head_qkv(ins[d], h)
                q16 = q.astype(BF16)
                k16 = k.astype(BF16)
                do16v = do_refs[d][:, h * DV:(h + 1) * DV]
                st16 = h_refs[d][h]
                dst = dss[d][h]
                dst16 = dst.astype(BF16)
                a = _dot(q16, k16, NT) * mask
                dp = _dot(do16v, v16, NT)
                da16 = (dp * mask).astype(BF16)
                dq_cross = _dot(do16v, st16, NT) * qd
                dq_ref[:, h * DK:(h + 1) * DK] = (_dot(da16, k16) + dq_cross).astype(BF16)
                dk_state = _dot(v16, dst16, NT) * kd
                dk_ref[:, h * DK:(h + 1) * DK] = ((_dot(da16, q16, TN) + dk_state) * (DK ** -0.5)).astype(BF16)
                dv = _dot(a.astype(BF16), do16v, TN) + _dot((k * kd).astype(BF16), dst16)
                dv_ref[:, h * DV:(h + 1) * DV] = dv.astype(BF16)
                dlg = (jnp.sum(relf * a * dp)
                       + jnp.sum(qe * jnp.sum(q * dq_cross, axis=-1, keepdims=True))
                       + jnp.sum(ke * jnp.sum(k * dk_state, axis=-1, keepdims=True))
                       + CH * g_ch * jnp.sum(dst * st16.astype(F32)))
                row = d * RH + h
                dlg_ref[row:row + 1, :] += jnp.broadcast_to(dlg, (1, 128))
                ds_new = g_ch * dst + _dot((q * qd).astype(BF16), do16v, TN)
                dss[d][h] = ds_new

                @pl.when(c == nc - 1)
                def _():
                    ds_outs[d][h] = ds_new

    def fw(b, c):
        return b * nc + nc - 1 - c

    def bw(b, c):
        return b * nc + c

    def rows(rowf, width):
        return pl.BlockSpec((CH, width), lambda b, c: (rowf(b, c), 0))

    def hist(rowf):
        return pl.BlockSpec((None, None, RH, DK, DV), lambda b, c: (b, rowf(0, c), 0, 0, 0))

    st = pl.BlockSpec((None, RH, DK, DV), lambda b, c: (b, 0, 0, 0))
    in_specs = [pl.BlockSpec(memory_space=pltpu.SMEM)]
    out_specs = []
    for rowf in (fw, bw):
        in_specs += _wide_specs(rowf) + [rows(rowf, RH * DV), hist(rowf)]
        out_specs += [rows(rowf, RH * DK), rows(rowf, RH * DK), rows(rowf, RH * DV)]
    out_specs += [st, st, pl.BlockSpec((None, 8, 128), lambda b, c: (b, 0, 0))]
    qk = _sds((t_rows, RH * DK), BF16)
    vv = _sds((t_rows, RH * DV), BF16)
    return pl.pallas_call(
        body, name="ret_bwd", grid=(nb, nc), in_specs=in_specs, out_specs=out_specs,
        out_shape=[qk, qk, vv, qk, qk, vv, _sds((nb, RH, DK, DV), F32), _sds((nb, RH, DK, DV), F32),
                   _sds((nb, 8, 128), F32)],
        scratch_shapes=[pltpu.VMEM((RH, DK, DV), F32), pltpu.VMEM((RH, DK, DV), F32)],
        compiler_params=_params(("parallel", "arbitrary")),
    )(lg, *([px] * 5), do16, hist_f, *([px] * 5), do16, hist_b)


def _ctx_state_bwd(px, lg, ds_f, ds_b, nb, t_rows, cx):
    rb = t_rows // cx

    def body(lg_ref, k_ref, v_ref, dsf_ref, dsb_ref, dk_ref, dv_ref, dlg_ref):
        h = pl.program_id(1)
        pos = lax.broadcasted_iota(jnp.int32, (cx, 1), 0).astype(F32)
        k = k_ref[...].astype(F32) * (DK ** -0.5)
        v16 = v_ref[...].astype(BF16)
        dk = jnp.zeros((cx, DK), F32)
        dv = jnp.zeros((cx, DV), F32)
        dlg_ref[...] = jnp.zeros_like(dlg_ref)
        for d, (ds_ref, e) in enumerate(((dsf_ref, cx - 1.0 - pos), (dsb_ref, pos))):
            w = jnp.exp(lg_ref[d, h] * e)
            ds16 = ds_ref[...].astype(BF16)
            t = _dot(v16, ds16, NT)
            dk += t * w
            dv += _dot((k * w).astype(BF16), ds16)
            dlg = jnp.sum(e * w * jnp.sum(k * t, axis=-1, keepdims=True))
            dlg_ref[d:d + 1, :] = jnp.broadcast_to(dlg, (1, 128))
        dk_ref[...] = (dk * (DK ** -0.5)).astype(BF16)
        dv_ref[...] = dv.astype(BF16)

    st = pl.BlockSpec((None, None, DK, DV), lambda b, h: (b, h, 0, 0))
    return pl.pallas_call(
        body, name="ctx_state_bwd", grid=(nb, RH),
        in_specs=[pl.BlockSpec(memory_space=pltpu.SMEM),
                  pl.BlockSpec((cx, DK), lambda b, h: (rb + b, RK // DK + h)),
                  pl.BlockSpec((cx, DV), lambda b, h: (rb + b, RV // DV + h)), st, st],
        out_specs=[pl.BlockSpec((cx, DK), lambda b, h: (b, h)), pl.BlockSpec((cx, DV), lambda b, h: (b, h)),
                   pl.BlockSpec((None, None, 8, 128), lambda b, h: (b, h, 0, 0))],
        out_shape=[_sds((nb * cx, RH * DK), BF16), _sds((nb * cx, RH * DV), BF16), _sds((nb, RH, 8, 128), F32)],
        compiler_params=_params(("parallel", "parallel")),
    )(lg, px, px, ds_f, ds_b)


def _assemble_lat(rows_all, dk_f, dk_b, dv_f, dv_b, dak16, dvx, dq_f, dq_b, drg16, daq16, dag16, dmr16, dma16, tm):
    t_rows = dk_f.shape[0]

    def body(dkf, dkb, dvf, dvb, dak, dav, dqf, dqb, drg, daq, dag, dmr, dma, o_ref):
        o_ref[:, RK:RK + RH * DK] = (dkf[...].astype(F32) + dkb[...].astype(F32)).astype(BF16)
        o_ref[:, RV:RV + RH * DV] = (dvf[...].astype(F32) + dvb[...].astype(F32)).astype(BF16)
        o_ref[:, AK:AK + HKV * HD] = dak[...]
        o_ref[:, AV:AV + HKV * HD] = dav[...].astype(BF16)
        o_ref[:, RQ:RQ + RH * DK] = (dqf[...].astype(F32) + dqb[...].astype(F32)).astype(BF16)
        o_ref[:, RG:RG + RH * DV] = drg[...]
        o_ref[:, AQ:AQ + D] = daq[...]
        o_ref[:, AG:AG + D] = dag[...]
        o_ref[:, MR:MR + D] = dmr[...]
        o_ref[:, MA:MA + D] = dma[...]

    args = (dk_f, dk_b, dv_f, dv_b, dak16, dvx, dq_f, dq_b, drg16, daq16, dag16, dmr16, dma16)
    return pl.pallas_call(
        body, name="assemble_lat", grid=(t_rows // tm,),
        in_specs=[pl.BlockSpec((tm, a.shape[1]), lambda i: (i, 0)) for a in args],
        out_specs=pl.BlockSpec((tm, IN_COLS), lambda i: (i, 0)), out_shape=_sds((rows_all, IN_COLS), BF16),
        compiler_params=_params(("parallel",)),
    )(*args)


def _assemble_ctx(dp_all, dck16, dcv16, dcak16, dvc, t_rows, tm):
    c_rows = dck16.shape[0]
    rb = t_rows // tm

    def body(_, dck, dcv, dcak, dcav, o_ref):
        o_ref[:, RK:RK + RH * DK] = dck[...]
        o_ref[:, RV:RV + RH * DV] = dcv[...]
        o_ref[:, AK:AK + HKV * HD] = dcak[...]
        o_ref[:, AV:AV + HKV * HD] = dcav[...].astype(BF16)
        o_ref[:, KV_COLS:] = jnp.zeros((tm, IN_COLS - KV_COLS), BF16)

    args = (dck16, dcv16, dcak16, dvc)
    return pl.pallas_call(
        body, name="assemble_ctx", grid=(c_rows // tm,),
        in_specs=[pl.BlockSpec(memory_space=pl.ANY)]
        + [pl.BlockSpec((tm, a.shape[1]), lambda i: (i, 0)) for a in args],
        out_specs=pl.BlockSpec((tm, IN_COLS), lambda i: (rb + i, 0)), out_shape=_sds(dp_all.shape, BF16),
        input_output_aliases={0: 0},
        compiler_params=_params(("parallel",)),
    )(dp_all, *args)


def _norm_bwd(dh, x2, mod3, norm_w, dxn, row_off, rows_per_group, group0, tm, name):
    with_dx = dxn is not None
    rows = x2.shape[0]
    rb0 = row_off // tm
    bpg = rows_per_group // tm
    ngroups = rows // rows_per_group

    def body(*refs):
        if with_dx:
            dh_ref, x_ref, sc_ref, nw_ref, dxn_ref, dx_ref, dsh_ref, dsc_ref, dnw_ref = refs
        else:
            dh_ref, x_ref, sc_ref, nw_ref, dsh_ref, dsc_ref, dnw_ref = refs
        i = pl.program_id(0)
        dhv = dh_ref[...]
        xv = x_ref[...]
        nw = nw_ref[...]
        r = lax.rsqrt(jnp.mean(xv * xv, axis=-1, keepdims=True) + EPS)
        xh = xv * r
        dm = dhv * (1.0 + sc_ref[...])
        dsh = jnp.sum(dhv, axis=0, keepdims=True)
        dsc = jnp.sum(dhv * (xh * nw), axis=0, keepdims=True)
        dnw = jnp.sum(dm * xh, axis=0, keepdims=True)
        if with_dx:
            dxh = dm * nw
            dx_ref[...] = dxn_ref[...] + r * (dxh - xh * jnp.mean(dxh * xh, axis=-1, keepdims=True))

        @pl.when(i % bpg == 0)
        def _():
            dsh_ref[...] = dsh
            dsc_ref[...] = dsc

        @pl.when(i % bpg != 0)
        def _():
            dsh_ref[...] += dsh
            dsc_ref[...] += dsc

        @pl.when(i == 0)
        def _():
            dnw_ref[...] = dnw

        @pl.when(i > 0)
        def _():
            dnw_ref[...] += dnw

    grp = pl.BlockSpec((None, 1, D), lambda i: (i // bpg, 0, 0))
    in_specs = [pl.BlockSpec((tm, D), lambda i: (rb0 + i, 0)), pl.BlockSpec((tm, D), lambda i: (i, 0)),
                pl.BlockSpec((None, 1, D), lambda i: (group0 + i // bpg, 0, 1)),
                pl.BlockSpec((1, D), lambda i: (0, 0))]
    args = [dh, x2, mod3, norm_w]
    out_specs = [grp, grp, pl.BlockSpec((1, D), lambda i: (0, 0))]
    out_shape = [_sds((ngroups, 1, D), F32), _sds((ngroups, 1, D), F32), _sds((1, D), F32)]
    if with_dx:
        in_specs.append(pl.BlockSpec((tm, D), lambda i: (i, 0)))
        args.append(dxn)
        out_specs.insert(0, pl.BlockSpec((tm, D), lambda i: (i, 0)))
        out_shape.insert(0, _sds((rows, D), F32))
    return pl.pallas_call(
        body, name=name, grid=(rows // tm,), in_specs=in_specs, out_specs=out_specs, out_shape=out_shape,
        compiler_params=_params(("arbitrary",)),
    )(*args)


def _small_final(dmod_all, dmodc_parts, c_rows, dm_loc_rows, nw_parts, misc_parts, c_ctx, r_pad, w_ada16):
    loc = dm_loc_rows.shape[1]

    def body(dm_ref, dmc_ref, c_ref, dml_ref, nwp_ref, mp_ref, cc_ref, r_ref, w_ref,
             gb_ref, gc_ref, gnw_ref, misc_ref, gwa_ref):
        dmc = jnp.sum(dmc_ref[...], axis=0, keepdims=True)
        gb_ref[...] = jnp.sum(dm_ref[...], axis=0, keepdims=True) + dmc
        dsc = _dot(jnp.broadcast_to(dmc, (8, 3 * D)).astype(BF16), w_ref[...], NT)[0:1, :]
        gc_ref[...] = dsc * _dsilu(cc_ref[...])
        gnw_ref[...] = jnp.sum(nwp_ref[...], axis=0, keepdims=True)
        misc = jnp.sum(mp_ref[...], axis=0, keepdims=True)
        y = jnp.exp2(r_ref[...])
        lane = lax.broadcasted_iota(jnp.int32, (1, D), 1)
        is_decay = jnp.logical_and(lane >= 2 * HD, lane < 2 * HD + 2 * RH)
        misc_ref[...] = misc * jnp.where(is_decay, -(y * np.float32(np.log(2.0))) / (1.0 - y), 1.0)
        gwa_ref[...] = _dot(_silu(c_ref[...]).astype(BF16), dml_ref[...].astype(BF16), TN)

    return pl.pallas_call(
        body, name="small_final",
        out_shape=[_sds((1, 3 * D), F32), _sds((1, D), F32), _sds((1, D), F32), _sds((1, D), F32), _sds((D, loc), F32)],
        compiler_params=pltpu.CompilerParams(vmem_limit_bytes=VMEM_LIMIT),
    )(dmod_all, dmodc_parts, c_rows, dm_loc_rows, nw_parts, misc_parts, c_ctx, r_pad, w_ada16)


def _adamw(w, g, m, v, name):
    rows, cols = w.shape
    tm = _pick(rows, 256, 8)
    bc1 = 1.0 - B1 ** STEP
    bc2 = 1.0 - B2 ** STEP

    def body(w_ref, g_ref, m_ref, v_ref, d_ref, nm_ref, nv_ref):
        g_ = g_ref[...]
        nm = B1 * m_ref[...] + (1.0 - B1) * g_
        nv = B2 * v_ref[...] + (1.0 - B2) * (g_ * g_)
        nm_ref[...] = nm
        nv_ref[...] = nv
        d_ref[...] = -LR * ((nm / bc1) / (jnp.sqrt(nv / bc2) + ADAM_EPS) + WD * w_ref[...])

    blk = pl.BlockSpec((tm, cols), lambda i: (i, 0))
    return pl.pallas_call(
        body, name=name, grid=(rows // tm,), in_specs=[blk] * 4, out_specs=[blk] * 3,
        out_shape=[_sds((rows, cols), F32)] * 3, compiler_params=_params(("parallel",)),
    )(w, g, m, v)


def _mesh_pos():
    return lax.axis_index("x"), lax.axis_index("y"), lax.axis_index("c")


def _all_gather(arrs, name):
    n = len(arrs)

    def body(*refs):
        ins, outs = refs[:n], refs[n:2 * n]
        send_sems, recv_sems, local_sems = refs[2 * n:]
        x, y, c = _mesh_pos()
        me, sib = (x, y, c), (x, y, 1 - c)
        chips = [(1 - x, y), (x, 1 - y), (1 - x, 1 - y)]

        def slot(p):
            return 4 * p[0] + 2 * p[1] + p[2]

        def copy(a, k, block, to, own):
            dst = outs[a].at[slot(block)]
            return pltpu.make_async_remote_copy(
                src_ref=ins[a] if own else dst, dst_ref=dst, send_sem=send_sems.at[a, k], recv_sem=recv_sems.at[a, k],
                device_id=to, device_id_type=MESH_T)

        mine = [pltpu.make_async_copy(ins[a], outs[a].at[slot(me)], local_sems.at[a]) for a in range(n)]
        for cp in mine:
            cp.start()
        first = []
        for a in range(n):
            first.append(copy(a, 0, me, sib, True))
            first += [copy(a, 1 + j, me, (*chip, c), True) for j, chip in enumerate(chips)]
        for cp in first:
            cp.start()
        passed = []
        for j, chip in enumerate(chips):
            for a in range(n):
                copy(a, 1 + j, (*chip, c), me, False).wait_recv()
                fwd = copy(a, 4 + j, (*chip, c), sib, False)
                fwd.start()
                passed.append(fwd)
        for a in range(n):
            copy(a, 0, sib, me, False).wait_recv()
            for j, chip in enumerate(chips):
                copy(a, 4 + j, (*chip, 1 - c), me, False).wait_recv()
        for cp in first + passed:
            cp.wait_send()
        for cp in mine:
            cp.wait()

    hbm = pl.BlockSpec(memory_space=pl.ANY)
    return pl.pallas_call(
        body, name=name, in_specs=[hbm] * n, out_specs=[hbm] * n,
        out_shape=[_sds((N_DEV,) + a.shape, a.dtype) for a in arrs],
        scratch_shapes=[pltpu.SemaphoreType.DMA((n, 7)), pltpu.SemaphoreType.DMA((n, 7)), pltpu.SemaphoreType.DMA((n,))],
    )(*arrs)


def _pair_exchange(arrs, name):
    n = len(arrs)

    def body(*refs):
        ins, outs = refs[:n], refs[n:2 * n]
        send_sems, recv_sems = refs[2 * n:]
        x, y, c = _mesh_pos()
        sib = (x, y, 1 - c)
        sends = []
        for a in range(n):
            for k in range(4):
                sends.append(pltpu.make_async_remote_copy(
                    src_ref=ins[a].at[2 * k + 1 - c], dst_ref=outs[a].at[k], send_sem=send_sems.at[a, k],
                    recv_sem=recv_sems.at[a, k], device_id=sib, device_id_type=MESH_T))
        for cp in sends:
            cp.start()
        for cp in sends:
            cp.wait_recv()
        for cp in sends:
            cp.wait_send()

    hbm = pl.BlockSpec(memory_space=pl.ANY)
    return pl.pallas_call(
        body, name=name, in_specs=[hbm] * n, out_specs=[hbm] * n,
        out_shape=[_sds((4,) + a.shape[1:], a.dtype) for a in arrs],
        scratch_shapes=[pltpu.SemaphoreType.DMA((n, 4)), pltpu.SemaphoreType.DMA((n, 4))],
    )(*arrs)


def _pair_add(part, got, core, name):
    _, rows, cols = part.shape
    tm = _pick(rows, 256, 16)
    p4 = part.reshape(4, 2, rows, cols)

    def body(core_ref, p_ref, g_ref, o_ref):
        o_ref[...] = (p_ref[...].astype(F32) + g_ref[...].astype(F32)).astype(BF16)

    blk = pl.BlockSpec((None, tm, cols), lambda k, i, cr: (k, i, 0))
    return pl.pallas_call(
        body, name=name,
        grid_spec=pltpu.PrefetchScalarGridSpec(
            num_scalar_prefetch=1, grid=(4, rows // tm),
            in_specs=[pl.BlockSpec((None, None, tm, cols), lambda k, i, cr: (k, cr[0], i, 0)), blk], out_specs=blk),
        out_shape=_sds((4, rows, cols), BF16), compiler_params=_params(("parallel", "parallel")),
    )(core, p4, got)


def _chip_sum(pair_sums, landed, chip, name):
    _, rows, cols = pair_sums.shape
    tm = _pick(rows, 256, 16)

    def body(chip_ref, s_ref, l_ref, o_ref):
        acc = s_ref[...].astype(F32)
        for j in range(3):
            acc = acc + l_ref[j].astype(F32)
        o_ref[...] = acc

    return pl.pallas_call(
        body, name=name,
        grid_spec=pltpu.PrefetchScalarGridSpec(
            num_scalar_prefetch=1, grid=(rows // tm,),
            in_specs=[pl.BlockSpec((None, tm, cols), lambda i, ch: (ch[0], i, 0)),
                      pl.BlockSpec((3, tm, cols), lambda i, ch: (0, i, 0))],
            out_specs=pl.BlockSpec((tm, cols), lambda i, ch: (i, 0))),
        out_shape=_sds((rows, cols), F32), compiler_params=_params(("parallel",)),
    )(chip, pair_sums, landed)


_HBM = pl.BlockSpec(memory_space=pltpu.HBM)
_SEM = pl.BlockSpec(memory_space=pltpu.SEMAPHORE)
_EFFECT = pltpu.SideEffectType.DATAFLOW_SIDE_EFFECTING


def _chip_routes(n):
    def plan(x, y, c):
        routes = []
        for a in range(n):
            for j in range(1, 4):
                px, py = x ^ (j >> 1), y ^ (j & 1)
                routes.append((a, 2 * px + py, (px, py, c), j - 1))
        return routes
    return plan, 3 * n


def _bcast_routes(n):
    def plan(x, y, c):
        routes = []
        for a in range(n):
            for k in range(1, N_DEV):
                peer = (x ^ ((k >> 2) & 1), y ^ ((k >> 1) & 1), c ^ (k & 1))
                routes.append((a, 0, peer, 4 * x + 2 * y + c))
        return routes
    return plan, 7 * n


def _route_copies(srcs, lands, send_sems, recv_sems, routes):
    return [pltpu.make_async_remote_copy(
        src_ref=srcs[a].at[sb], dst_ref=lands[a].at[lb], send_sem=send_sems.at[r], recv_sem=recv_sems.at[r],
        device_id=peer, device_id_type=MESH_T) for r, (a, sb, peer, lb) in enumerate(routes)]


def _exchange_start(srcs, lands, routes, name):
    plan, count = routes
    n = len(srcs)

    def body(*refs):
        send_sems, recv_sems = refs[2 * n], refs[2 * n + 1]
        token = refs[-1]
        for cp in _route_copies(refs[:n], refs[n:2 * n], send_sems, recv_sems, plan(*_mesh_pos())):
            cp.start()
        token[...] = jnp.zeros_like(token)

    args = [pltpu.with_memory_space_constraint(a, pltpu.HBM) for a in list(srcs) + list(lands)]
    out = pl.pallas_call(
        body, name=name,
        out_shape=(pltpu.SemaphoreType.DMA((count,)), pltpu.SemaphoreType.DMA((count,)),
                   *[pltpu.HBM(a.shape, a.dtype) for a in args], _sds((8, 128), F32)),
        in_specs=[_HBM] * (2 * n), out_specs=(_SEM, _SEM, *([_HBM] * (2 * n)), pl.BlockSpec(memory_space=pltpu.VMEM)),
        input_output_aliases={i: 2 + i for i in range(2 * n)},
        compiler_params=pltpu.CompilerParams(has_side_effects=_EFFECT),
    )(*args)
    return (out[0], out[1], list(out[2:2 + 2 * n]), routes), out[-1]


def _exchange_wait(state, after, name):
    send_sems, recv_sems, bufs, (plan, count) = state
    n = len(bufs) // 2

    def body(*refs):
        send_s, recv_s = refs[2 * n], refs[2 * n + 1]
        for cp in _route_copies(refs[:n], refs[n:2 * n], send_s, recv_s, plan(*_mesh_pos())):
            cp.wait_send()
            cp.wait_recv()

    out = pl.pallas_call(
        body, name=name, out_shape=tuple(pltpu.HBM(a.shape, a.dtype) for a in bufs),
        in_specs=[_HBM] * (2 * n) + [_SEM, _SEM, pl.BlockSpec(memory_space=pl.ANY)], out_specs=tuple([_HBM] * (2 * n)),
        input_output_aliases={i: i for i in range(2 * n)},
        compiler_params=pltpu.CompilerParams(has_side_effects=_EFFECT),
    )(*bufs, send_sems, recv_sems, after)
    return list(out[:n]), list(out[n:])


def _reduce_scatter_start(parts, core, name):
    got = _pair_exchange(parts, name + "_pair")
    sums = [_pair_add(p, g, core, "%s_add_%d" % (name, i)) for i, (p, g) in enumerate(zip(parts, got))]
    lands = [lax.empty((3,) + s_.shape[1:], BF16) for s_ in sums]
    return _exchange_start(sums, lands, _chip_routes(len(sums)), name + "_start")


def _reduce_scatter_finish(rs_state, after, chip, name):
    sums, landed = _exchange_wait(rs_state, after, name + "_wait")
    return [_chip_sum(s_, l_, chip, "%s_sum_%d" % (name, i)) for i, (s_, l_) in enumerate(zip(sums, landed))]


def _local_step(x, c, ctx, c_ctx, norm_w, b_ada, ret_log2_decay, q_norm_w, k_norm_w, loss_target,
                w_ada16, w_in_t16, get_w_o, on_out_grads, on_in_grad, started=()):
    nb, seq, _ = x.shape
    cx = ctx.shape[1]
    t_rows, c_rows = nb * seq, nb * cx
    rows_all = t_rows + c_rows
    nc = seq // CH
    tm = _pick(seq, 256, 128)
    te = _pick(seq, 512, 128)
    assert cx % tm == 0 and t_rows % cx == 0 and seq % GRID_W == 0

    x2 = x.reshape(t_rows, D)
    ctx2 = ctx.reshape(c_rows, D)
    tgt = loss_target.reshape(t_rows, D)
    c8 = jnp.zeros((8, D), F32).at[:nb].set(c).at[nb].set(c_ctx)
    lg = _log_gamma(ret_log2_decay)
    cos, sin = _rope_tables(seq)

    mod = _mod_fwd(c8, w_ada16, b_ada)
    mod3 = mod[:, None, :]
    h_all = _norm_fwd(x2, mod3, norm_w, rows_all, 0, seq, 0, None, te, "norm_fwd")
    h_all = _norm_fwd(ctx2, mod3, norm_w, rows_all, t_rows, c_rows, nb, h_all, tm, "norm_fwd_ctx")
    px = _matmul(h_all, w_in_t16, tb=True, tm=1536, tn=1536, tk=D, out_dtype=BF16, name="in_proj", after=started)
    s0f, s0b = _ctx_state(px, lg, nb, t_rows, cx)
    o_f, o_b, hist_f, hist_b = _ret_fwd(px, lg, s0f, s0b, nb, nc)
    yret16 = _ret_post(o_f, o_b, px, te)
    q16 = _qk_prep(px, q_norm_w, cos, sin, t_rows, 0, AQ, HQ, 4, seq, te, "q_prep")
    kx16 = _qk_prep(px, k_norm_w, cos, sin, t_rows, 0, AK, HKV, HKV, seq, te, "k_prep")
    kc16 = _qk_prep(px, k_norm_w, None, None, c_rows, t_rows, AK, HKV, HKV, seq, tm, "kc_prep")
    o_att, yatt16, lse = _att_fwd(q16, kx16, kc16, px, nb, seq, cx, tm)
    w_o_ret16, w_o_att16, w_out16 = get_w_o(lse)
    a_ret, a_att, y16 = _merge(yret16, yatt16, px, w_o_ret16, w_o_att16, te)
    dxn, dout16, dgate, loss_b = _outproj(y16, w_out16, x2, tgt, mod3, nb, seq, te)

    gw_out = _matmul(y16, dout16, ta=True, tm=D, tn=D, tk=D, out_dtype=BF16, name="gw_out")
    da_ret16, da_att16, dmr16, dma16 = _bwd_merge(dout16, w_out16, px, a_ret, a_att, te)
    gw_o_ret = _matmul(yret16, da_ret16, ta=True, tm=D, tn=D, tk=D, out_dtype=BF16, name="gw_o_ret")
    gw_o_att = _matmul(yatt16, da_att16, ta=True, tm=D, tn=D, tk=D, out_dtype=BF16, name="gw_o_att")
    out_state, out_started = on_out_grads([gw_o_ret, gw_o_att, gw_out])
    do16, drg16 = _bwd_branch_ret(da_ret16, w_o_ret16, px, o_f, o_b, te, after=out_started)
    dao, dag16 = _bwd_branch_att(da_att16, w_o_att16, px, o_att, te)
    dq_rot, dkx, dvx, dkc, dvc = _att_bwd(q16, kx16, kc16, px, dao, o_att, lse, nb, seq, cx, tm)
    daq16, gq = _qk_prep_bwd(dq_rot, px, q_norm_w, cos, sin, t_rows, 0, AQ, HQ, 4, seq, te, "q_prep_bwd")
    dak16, gk_lat = _qk_prep_bwd(dkx.reshape(t_rows, HKV * HD), px, k_norm_w, cos, sin, t_rows, 0, AK, HKV, HKV, seq, te,
                                 "k_prep_bwd")
    dcak16, gk_ctx = _qk_prep_bwd(dkc.reshape(c_rows, HKV * HD), px, k_norm_w, None, None, c_rows, t_rows, AK, HKV, HKV,
                                  seq, tm, "kc_prep_bwd")
    dq_f, dk_f, dv_f, dq_b, dk_b, dv_b, ds_f, ds_b, dlg_scan = _ret_bwd(px, lg, do16, hist_f, hist_b, nb, nc)
    dck16, dcv16, dlg_ctx = _ctx_state_bwd(px, lg, ds_f, ds_b, nb, t_rows, cx)
    dp_all = _assemble_lat(rows_all, dk_f, dk_b, dv_f, dv_b, dak16, dvx.reshape(t_rows, HKV * HD), dq_f, dq_b, drg16,
                           daq16, dag16, dmr16, dma16, tm)
    dp_all = _assemble_ctx(dp_all, dck16, dcv16, dcak16, dvc.reshape(c_rows, HKV * HD), t_rows, tm)
    gw_in_t = _matmul(dp_all, h_all, ta=True, tm=1536, tn=D, tk=1536, out_dtype=BF16, name="gw_in")
    in_state, in_started = on_in_grad(gw_in_t)
    dh = _matmul(dp_all, w_in_t16, tm=1536, tn=D, tk=1536, out_dtype=F32, name="d_h", after=in_started)
    grad_x, dsh, dsc, gnw_lat = _norm_bwd(dh, x2, mod3, norm_w, dxn, 0, seq, 0, te, "norm_bwd")
    dsh_c, dsc_c, gnw_ctx = _norm_bwd(dh, ctx2, mod3, norm_w, None, t_rows, c_rows, nb, tm, "norm_bwd_ctx")

    dlg = (jnp.sum(dlg_scan[:, :, 0], axis=0) + jnp.sum(dlg_ctx[:, :, :2, 0], axis=0).T.reshape(2 * RH)).reshape(1, 2 * RH)
    misc = jnp.concatenate([gq, gk_lat + gk_ctx, dlg, jnp.sum(loss_b[:, 0, 0]).reshape(1, 1),
                            jnp.zeros((1, D - 2 * HD - 2 * RH - 1), F32)], axis=1)
    rows = []
    for b in range(nb):
        rows += [dsh[b], dsc[b], dgate[b]]
    rows += [dsh_c[0], dsc_c[0]] + [c[b:b + 1] for b in range(nb)] + [gnw_lat + gnw_ctx, misc]
    payload = jnp.concatenate(rows + [jnp.zeros((PAY_ROWS - len(rows), D), F32)], axis=0)
    return grad_x.reshape(nb, seq, D), out_state, in_state, payload


def _finish_small(gathered, nb, c_ctx, ret_log2_decay, w_ada16, dev):
    n_dev = gathered.shape[0]
    loc = 3 * D // n_dev
    dmod_all = gathered[:, :3 * nb].reshape(n_dev * nb, 3 * D)
    dmodc_parts = jnp.concatenate([gathered[:, 3 * nb:3 * nb + 2].reshape(n_dev, 2 * D), jnp.zeros((n_dev, D), F32)], axis=1)
    c_all = gathered[:, 3 * nb + 2:4 * nb + 2].reshape(n_dev * nb, D)
    nw_parts = gathered[:, 4 * nb + 2]
    misc_parts = gathered[:, 4 * nb + 3]
    n_rows = n_dev * nb + n_dev
    pad = (-n_rows) % 16
    c_rows = jnp.concatenate([c_all, jnp.broadcast_to(c_ctx.reshape(1, D), (n_dev, D)), jnp.zeros((pad, D), F32)], axis=0)
    dm_rows = jnp.concatenate([dmod_all, dmodc_parts, jnp.zeros((pad, 3 * D), F32)], axis=0)
    dm_loc_rows = lax.dynamic_slice_in_dim(dm_rows, dev * loc, loc, axis=1)
    r_pad = jnp.full((1, D), -1.0, F32).at[:, 2 * HD:2 * HD + 2 * RH].set(ret_log2_decay.reshape(1, 2 * RH))
    gb, gc, gnw, misc, gwa = _small_final(dmod_all, dmodc_parts, c_rows, dm_loc_rows, nw_parts, misc_parts,
                                          c_ctx.reshape(1, D), r_pad, w_ada16)
    return (gb, gc, gnw, misc[:, :HD], misc[:, HD:2 * HD], misc[:, 2 * HD:2 * HD + 2 * RH], gwa,
            misc[0, 2 * HD + 2 * RH])


def kernel(x, c, ctx, c_ctx, norm_w, w_ada, b_ada, w_in, ret_log2_decay, q_norm_w, k_norm_w, w_o_ret, w_o_att, w_out, loss_target, m_c_ctx, m_norm_w, m_w_ada, m_b_ada, m_w_in, m_ret_log2_decay, m_q_norm_w, m_k_norm_w, m_w_o_ret, m_w_o_att, m_w_out, v_c_ctx, v_norm_w, v_w_ada, v_b_ada, v_w_in, v_ret_log2_decay, v_q_norm_w, v_k_norm_w, v_w_o_ret, v_w_o_att, v_w_out):
    nb = x.shape[0]
    mx, my, mc = _mesh_pos()
    dev = 4 * mx + 2 * my + mc
    core = jnp.reshape(mc, (1,)).astype(jnp.int32)
    chip = jnp.reshape(2 * mx + my, (1,)).astype(jnp.int32)

    w_in_t = jnp.transpose(w_in[0])
    g_in, g_ada = _all_gather([w_in_t.astype(BF16), w_ada[0].astype(BF16)], "gather_weights")
    w_in_t16 = g_in.reshape(IN_COLS, D)
    w_ada16 = jnp.transpose(g_ada, (1, 0, 2)).reshape(D, 3 * D)

    wo_shards = [w_[0].astype(BF16) for w_ in (w_o_ret, w_o_att, w_out)]
    wo_shards = list(lax.optimization_barrier((g_in, *wo_shards))[1:])
    wo_lands = [lax.dynamic_update_slice(lax.empty((N_DEV,) + s_.shape, BF16), s_[None], (dev, 0, 0)) for s_ in wo_shards]
    wo_state, wo_token = _exchange_start([s_[None] for s_ in wo_shards], wo_lands, _bcast_routes(3), "gather_wo_start")

    def get_w_o(after):
        _, (l_ret, l_att, l_out) = _exchange_wait(wo_state, after, "gather_wo_wait")
        return l_ret.reshape(RH * DV, D), l_att.reshape(D, D), l_out.reshape(D, D)

    def on_out_grads(grads):
        parts = [g_.reshape(N_DEV, g_.shape[0] // N_DEV, D) for g_ in grads]
        state, token = _reduce_scatter_start(parts, core, "rs_out")
        return state, (token,)

    def on_in_grad(grad):
        state, token = _reduce_scatter_start([grad.reshape(N_DEV, IN_COLS // N_DEV, D)], core, "rs_in")
        return state, (token,)

    grad_x, out_state, in_state, payload = _local_step(
        x, c, ctx, c_ctx, norm_w, b_ada, ret_log2_decay, q_norm_w, k_norm_w, loss_target,
        w_ada16, w_in_t16, get_w_o, on_out_grads, on_in_grad, started=(wo_token,))

    (gathered,) = _all_gather([payload], "gather_small")
    gb, gc, gnw, gq, gk, gr, gwa, loss = _finish_small(gathered, nb, c_ctx, ret_log2_decay, w_ada16, dev)

    g_w_o_ret, g_w_o_att, g_w_out = _reduce_scatter_finish(out_state, gathered, chip, "rs_out")
    (g_w_in_t,) = _reduce_scatter_finish(in_state, gathered, chip, "rs_in")

    grads = [gc.reshape(c_ctx.shape), gnw, gwa[None], gb, g_w_in_t, gr.reshape(ret_log2_decay.shape), gq, gk,
             g_w_o_ret[None], g_w_o_att[None], g_w_out[None]]
    weights = [c_ctx, norm_w, w_ada, b_ada, w_in_t, ret_log2_decay, q_norm_w, k_norm_w, w_o_ret, w_o_att, w_out]
    ms = [m_c_ctx, m_norm_w, m_w_ada, m_b_ada, jnp.transpose(m_w_in[0]), m_ret_log2_decay, m_q_norm_w, m_k_norm_w,
          m_w_o_ret, m_w_o_att, m_w_out]
    vs = [v_c_ctx, v_norm_w, v_w_ada, v_b_ada, jnp.transpose(v_w_in[0]), v_ret_log2_decay, v_q_norm_w, v_k_norm_w,
          v_w_o_ret, v_w_o_att, v_w_out]
    deltas, new_ms, new_vs = [], [], []
    for i, (w, g, m, v) in enumerate(zip(weights, grads, ms, vs)):
        shape2 = (-1, w.shape[-1])
        res = _adamw(w.reshape(shape2), g.reshape(shape2), m.reshape(shape2), v.reshape(shape2), "adamw_%d" % i)
        for lst, r in zip((deltas, new_ms, new_vs), res):
            lst.append(jnp.transpose(r)[None] if i == 4 else r.reshape(w.shape))
    grads[4] = jnp.transpose(g_w_in_t)[None]
    return (loss, grad_x, *grads, *deltas, *new_ms, *new_vs)
```

```python
import numpy as np
import jax
import jax.numpy as jnp
from jax import lax
from jax.experimental import pallas as pl
from jax.experimental.pallas import tpu as pltpu

F32 = jnp.float32
BF16 = jnp.bfloat16

D = 1024
RH, DK, DV, CH = 4, 256, 512, 256
HQ, HKV, HD = 8, 2, 128
GRID_W = 64
ROPE_THETA = 10000.0
EPS = 1e-6
RK, RV, AK, AV, RQ, RG, AQ, AG, MR, MA = 0, 1024, 3072, 3328, 3584, 4608, 6656, 7680, 8704, 9728
IN_COLS = 10752
KV_COLS = 3584
N_DEV = 8
LR, B1, B2, ADAM_EPS, WD, STEP = 0.001, 0.9, 0.999, 1e-08, 0.01, 10
PAY_ROWS = 16
VMEM_LIMIT = 56 * 1024 * 1024
MESH_T = pl.DeviceIdType.MESH

NT = (((1,), (1,)), ((), ()))
TN = (((0,), (0,)), ((), ()))
SM_C = (HD ** -0.5) * float(np.log2(np.e))


def _params(sem):
    return pltpu.CompilerParams(dimension_semantics=sem, vmem_limit_bytes=VMEM_LIMIT)


def _pick(n, target, mult=8):
    best = None
    for t in range(mult, min(n, target) + 1, mult):
        if n % t == 0:
            best = t
    return best or n


def _dot(a, b, dn=None):
    if dn is None:
        return jnp.dot(a, b, preferred_element_type=F32)
    return lax.dot_general(a, b, dn, preferred_element_type=F32)


def _sig(v):
    return jax.nn.sigmoid(v)


def _silu(v):
    return v * _sig(v)


def _dsilu(v):
    s = _sig(v)
    return s * (1.0 + v * (1.0 - s))


def _sds(shape, dtype):
    return jax.ShapeDtypeStruct(shape, dtype)


def _matmul(a, b, *, ta=False, tb=False, tm, tn, tk, out_dtype, name, after=()):
    m = a.shape[1] if ta else a.shape[0]
    kdim = a.shape[0] if ta else a.shape[1]
    n = b.shape[0] if tb else b.shape[1]
    tm, tn, tk = _pick(m, tm, 128), _pick(n, tn, 128), _pick(kdim, tk, 128)
    nk = kdim // tk
    dn = (((0 if ta else 1,), (1 if tb else 0,)), ((), ()))

    def body(a_ref, b_ref, *rest):
        o_ref, acc_ref = rest[-2:]
        k = pl.program_id(2)
        part = _dot(a_ref[...].astype(BF16), b_ref[...].astype(BF16), dn)
        if nk == 1:
            o_ref[...] = part.astype(o_ref.dtype)
        else:
            @pl.when(k == 0)
            def _():
                acc_ref[...] = part

            @pl.when(k > 0)
            def _():
                acc_ref[...] += part

            @pl.when(k == nk - 1)
            def _():
                o_ref[...] = acc_ref[...].astype(o_ref.dtype)

    a_spec = pl.BlockSpec((tk, tm), lambda i, j, k: (k, i)) if ta else pl.BlockSpec((tm, tk), lambda i, j, k: (i, k))
    b_spec = pl.BlockSpec((tn, tk), lambda i, j, k: (j, k)) if tb else pl.BlockSpec((tk, tn), lambda i, j, k: (k, j))
    return pl.pallas_call(
        body, name=name, grid=(m // tm, n // tn, nk),
        in_specs=[a_spec, b_spec] + [pl.BlockSpec(memory_space=pl.ANY)] * len(after),
        out_specs=pl.BlockSpec((tm, tn), lambda i, j, k: (i, j)), out_shape=_sds((m, n), out_dtype),
        scratch_shapes=[pltpu.VMEM((tm, tn) if nk > 1 else (8, 128), F32)],
        compiler_params=_params(("parallel", "parallel", "arbitrary")),
    )(a, b, *after)


def _log_gamma(r):
    rp = jnp.full((8, 128), -1.0, F32).at[:2, :RH].set(r.reshape(2, RH))

    def body(r_ref, o_ref):
        o_ref[...] = jnp.log1p(-jnp.exp2(r_ref[...]))

    out = pl.pallas_call(body, name="log_gamma", out_shape=_sds((8, 128), F32))(rp)
    return out[:2, :RH]


def _mod_fwd(c8, w_ada16, b_ada):
    def body(c_ref, w_ref, b_ref, o_ref):
        o_ref[...] = _dot(_silu(c_ref[...]).astype(BF16), w_ref[...]) + b_ref[...]

    return pl.pallas_call(
        body, name="mod_fwd", grid=(3,),
        in_specs=[pl.BlockSpec((8, D), lambda j: (0, 0)), pl.BlockSpec((D, D), lambda j: (0, j)),
                  pl.BlockSpec((1, D), lambda j: (0, j))],
        out_specs=pl.BlockSpec((8, D), lambda j: (0, j)), out_shape=_sds((8, 3 * D), F32),
        compiler_params=_params(("arbitrary",)),
    )(c8, w_ada16, b_ada)


def _norm_fwd(x2, mod3, norm_w, rows_all, row_off, rows_per_group, group0, h_prev, tm, name):
    rows = x2.shape[0]
    rb0 = row_off // tm
    bpg = rows_per_group // tm

    def body(*refs):
        x_ref, sh_ref, sc_ref, nw_ref, o_ref = refs[-5:]
        xv = x_ref[...]
        r = lax.rsqrt(jnp.mean(xv * xv, axis=-1, keepdims=True) + EPS)
        o_ref[...] = ((xv * r) * nw_ref[...] * (1.0 + sc_ref[...]) + sh_ref[...]).astype(BF16)

    in_specs = [pl.BlockSpec((tm, D), lambda i: (i, 0)),
                pl.BlockSpec((None, 1, D), lambda i: (group0 + i // bpg, 0, 0)),
                pl.BlockSpec((None, 1, D), lambda i: (group0 + i // bpg, 0, 1)),
                pl.BlockSpec((1, D), lambda i: (0, 0))]
    args = [x2, mod3, mod3, norm_w]
    alias = {}
    if h_prev is not None:
        in_specs.insert(0, pl.BlockSpec(memory_space=pl.ANY))
        args.insert(0, h_prev)
        alias = {0: 0}
    return pl.pallas_call(
        body, name=name, grid=(rows // tm,), in_specs=in_specs,
        out_specs=pl.BlockSpec((tm, D), lambda i: (rb0 + i, 0)), out_shape=_sds((rows_all, D), BF16),
        input_output_aliases=alias, compiler_params=_params(("parallel",)),
    )(*args)


def _decays(lg, fwd):
    ii = lax.broadcasted_iota(jnp.int32, (CH, CH), 0)
    jj = lax.broadcasted_iota(jnp.int32, (CH, CH), 1)
    ri = lax.broadcasted_iota(jnp.int32, (CH, 1), 0).astype(F32)
    rel = (ii - jj) if fwd else (jj - ii)
    relf = jnp.maximum(rel, 0).astype(F32)
    mask = jnp.where(rel >= 0, jnp.exp(lg * relf), 0.0)
    qe = (ri + 1.0) if fwd else (CH - ri)
    ke = (CH - 1.0 - ri) if fwd else ri
    return mask, relf, jnp.exp(lg * qe), qe, jnp.exp(lg * ke), ke


def _wide_specs(rowf):
    return [pl.BlockSpec((CH, 2 * DK), lambda b, c: (rowf(b, c), RQ // (2 * DK))),
            pl.BlockSpec((CH, 2 * DK), lambda b, c: (rowf(b, c), RQ // (2 * DK) + 1)),
            pl.BlockSpec((CH, RH * DK), lambda b, c: (rowf(b, c), RK // (RH * DK))),
            pl.BlockSpec((CH, 2 * DV), lambda b, c: (rowf(b, c), RV // (2 * DV))),
            pl.BlockSpec((CH, 2 * DV), lambda b, c: (rowf(b, c), RV // (2 * DV) + 1))]


def _head_qkv(refs, h):
    q0, q1, k, v0, v1 = refs
    lo = h % 2
    q = (q0, q1)[h // 2][:, lo * DK:(lo + 1) * DK].astype(F32)
    kk = k[:, h * DK:(h + 1) * DK].astype(F32) * (DK ** -0.5)
    v16 = (v0, v1)[h // 2][:, lo * DV:(lo + 1) * DV].astype(BF16)
    return q, kk, v16


def _ctx_state(px, lg, nb, t_rows, cx):
    rb = t_rows // cx

    def body(lg_ref, k_ref, v_ref, sf_ref, sb_ref):
        h = pl.program_id(1)
        pos = lax.broadcasted_iota(jnp.int32, (cx, 1), 0).astype(F32)
        k = k_ref[...].astype(F32) * (DK ** -0.5)
        v16 = v_ref[...].astype(BF16)
        wf = jnp.exp(lg_ref[0, h] * (cx - 1.0 - pos))
        wb = jnp.exp(lg_ref[1, h] * pos)
        sf_ref[...] = _dot((k * wf).astype(BF16), v16, TN)
        sb_ref[...] = _dot((k * wb).astype(BF16), v16, TN)

    st = pl.BlockSpec((None, None, DK, DV), lambda b, h: (b, h, 0, 0))
    return pl.pallas_call(
        body, name="ctx_state", grid=(nb, RH),
        in_specs=[pl.BlockSpec(memory_space=pltpu.SMEM),
                  pl.BlockSpec((cx, DK), lambda b, h: (rb + b, RK // DK + h)),
                  pl.BlockSpec((cx, DV), lambda b, h: (rb + b, RV // DV + h))],
        out_specs=[st, st], out_shape=[_sds((nb, RH, DK, DV), F32)] * 2,
        compiler_params=_params(("parallel", "parallel")),
    )(lg, px, px)


def _ret_fwd(px, lg, s0f, s0b, nb, nc):
    t_rows = nb * nc * CH

    def body(lg_ref, *refs):
        ins = (refs[0:5], refs[5:10])
        s0f_ref, s0b_ref, of_ref, ob_ref, hf_ref, hb_ref, sf, sb = refs[10:]
        c = pl.program_id(1)

        @pl.when(c == 0)
        def _():
            sf[...] = s0f_ref[...]
            sb[...] = s0b_ref[...]

        for d, (o_ref, h_ref, s) in enumerate(((of_ref, hf_ref, sf), (ob_ref, hb_ref, sb))):
            for h in range(RH):
                lg_d = lg_ref[d, h]
                mask, _, qd, _, kd, _ = _decays(lg_d, d == 0)
                q, k, v16 = _head_qkv(ins[d], h)
                a = _dot(q.astype(BF16), k.astype(BF16), NT)
                st = s[h]
                st16 = st.astype(BF16)
                h_ref[h] = st16
                o = _dot((a * mask).astype(BF16), v16) + _dot((q * qd).astype(BF16), st16)
                o_ref[:, h * DV:(h + 1) * DV] = o.astype(BF16)
                s[h] = st * jnp.exp(lg_d * CH) + _dot((k * kd).astype(BF16), v16, TN)

    def fw(b, c):
        return b * nc + c

    def bw(b, c):
        return b * nc + nc - 1 - c

    st = pl.BlockSpec((None, RH, DK, DV), lambda b, c: (b, 0, 0, 0))
    in_specs = [pl.BlockSpec(memory_space=pltpu.SMEM)] + _wide_specs(fw) + _wide_specs(bw) + [st, st]
    out_specs = [pl.BlockSpec((CH, RH * DV), lambda b, c: (fw(b, c), 0)),
                 pl.BlockSpec((CH, RH * DV), lambda b, c: (bw(b, c), 0)),
                 pl.BlockSpec((None, None, RH, DK, DV), lambda b, c: (b, c, 0, 0, 0)),
                 pl.BlockSpec((None, None, RH, DK, DV), lambda b, c: (b, nc - 1 - c, 0, 0, 0))]
    return pl.pallas_call(
        body, name="ret_fwd", grid=(nb, nc), in_specs=in_specs, out_specs=out_specs,
        out_shape=[_sds((t_rows, RH * DV), BF16)] * 2 + [_sds((nb, nc, RH, DK, DV), BF16)] * 2,
        scratch_shapes=[pltpu.VMEM((RH, DK, DV), F32), pltpu.VMEM((RH, DK, DV), F32)],
        compiler_params=_params(("parallel", "arbitrary")),
    )(lg, *([px] * 10), s0f, s0b)


def _ret_post(o_f, o_b, px, tm):
    t_rows = o_f.shape[0]

    def body(of_ref, ob_ref, g0, g1, g2, g3, y_ref):
        for h, g_ref in enumerate((g0, g1, g2, g3)):
            sl = slice(h * DV, (h + 1) * DV)
            o = of_ref[:, sl].astype(F32) + ob_ref[:, sl].astype(F32)
            r = lax.rsqrt(jnp.mean(o * o, axis=-1, keepdims=True) + EPS)
            y_ref[:, sl] = ((o * r) * _silu(g_ref[...].astype(F32))).astype(BF16)

    def gate(h):
        return pl.BlockSpec((tm, DV), lambda i: (i, RG // DV + h))

    wide = pl.BlockSpec((tm, RH * DV), lambda i: (i, 0))
    return pl.pallas_call(
        body, name="ret_post", grid=(t_rows // tm,),
        in_specs=[wide, wide] + [gate(h) for h in range(RH)],
        out_specs=wide, out_shape=_sds((t_rows, RH * DV), BF16),
        compiler_params=_params(("parallel",)),
    )(o_f, o_b, *([px] * RH))


def _rope_tables(seq):
    rows = seq // GRID_W
    row = np.repeat(np.arange(rows, dtype=np.float32), GRID_W)
    col = np.tile(np.arange(GRID_W, dtype=np.float32), rows)
    half = HD // 2
    freqs = (ROPE_THETA ** (-np.arange(0, half, 2, dtype=np.float32) / half)).astype(np.float32)
    ang = np.concatenate([row[:, None] * freqs, col[:, None] * freqs], axis=-1).astype(np.float32)
    cos = np.repeat(np.cos(ang), 2, axis=-1).astype(np.float32)
    sin = np.repeat(np.sin(ang), 2, axis=-1).astype(np.float32)
    sign = np.tile(np.array([-1.0, 1.0], np.float32), HD // 2)
    return jnp.asarray(cos), jnp.asarray(sin * sign)


def _swap_pairs(v):
    lane = lax.broadcasted_iota(jnp.int32, v.shape, 1)
    return jnp.where((lane & 1) == 0, pltpu.roll(v, HD - 1, 1), pltpu.roll(v, 1, 1))


def _qk_prep(px, nw, cos, sin, rows, row_off, col_off, heads, hb, seq, tm, name):
    rope = cos is not None
    rb0 = row_off // tm
    pb = seq // tm if rope else 1
    bw = hb * HD

    def body(*refs):
        if rope:
            x_ref, w_ref, c_ref, s_ref, o_ref = refs
        else:
            x_ref, w_ref, o_ref = refs
        for h in range(hb):
            sl = slice(h * HD, (h + 1) * HD)
            xv = x_ref[:, sl].astype(F32)
            r = lax.rsqrt(jnp.mean(xv * xv, axis=-1, keepdims=True) + EPS)
            t = (xv * r) * w_ref[...]
            if rope:
                t = t * c_ref[...] + _swap_pairs(t) * s_ref[...]
            o_ref[:, sl] = t.astype(BF16)

    in_specs = [pl.BlockSpec((tm, bw), lambda i, j: (rb0 + i, col_off // bw + j)),
                pl.BlockSpec((1, HD), lambda i, j: (0, 0))]
    args = [px, nw]
    if rope:
        in_specs += [pl.BlockSpec((tm, HD), lambda i, j: (i % pb, 0))] * 2
        args += [cos, sin]
    return pl.pallas_call(
        body, name=name, grid=(rows // tm, heads // hb), in_specs=in_specs,
        out_specs=pl.BlockSpec((tm, bw), lambda i, j: (i, j)), out_shape=_sds((rows, heads * HD), BF16),
        compiler_params=_params(("parallel", "parallel")),
    )(*args)


def _att_fwd(q16, kx16, kc16, px, nb, seq, cx, tq):
    t_rows = nb * seq
    nq = seq // tq
    rep = HQ // HKV
    gw = rep * HD

    def body(q_ref, kx_ref, kc_ref, vx_ref, vc_ref, g_ref, o_ref, y_ref, l_ref):
        kx = kx_ref[...]
        kc = kc_ref[...]
        vx = vx_ref[...].astype(BF16)
        vc = vc_ref[...].astype(BF16)
        l_ref[...] = jnp.zeros_like(l_ref)
        for r in range(rep):
            sl = slice(r * HD, (r + 1) * HD)
            q = q_ref[:, sl]
            s1 = _dot(q, kx, NT)
            s2 = _dot(q, kc, NT)
            m = jnp.maximum(jnp.max(s1, axis=-1, keepdims=True), jnp.max(s2, axis=-1, keepdims=True))
            e1 = jnp.exp2((s1 - m) * SM_C)
            e2 = jnp.exp2((s2 - m) * SM_C)
            tot = jnp.sum(e1, axis=-1, keepdims=True) + jnp.sum(e2, axis=-1, keepdims=True)
            o = (_dot(e1.astype(BF16), vx) + _dot(e2.astype(BF16), vc)) * (1.0 / tot)
            o_ref[:, sl] = o
            y_ref[:, sl] = (o * _silu(g_ref[:, sl].astype(F32))).astype(BF16)
            l_ref[:, r:r + 1] = m * SM_C + jnp.log(tot) * float(np.log2(np.e))

    qblk = pl.BlockSpec((tq, gw), lambda b, g, i: (b * nq + i, g))
    return pl.pallas_call(
        body, name="att_fwd", grid=(nb, HKV, nq),
        in_specs=[qblk,
                  pl.BlockSpec((seq, HD), lambda b, g, i: (b, g)),
                  pl.BlockSpec((cx, HD), lambda b, g, i: (b, g)),
                  pl.BlockSpec((seq, HD), lambda b, g, i: (b, AV // HD + g)),
                  pl.BlockSpec((cx, HD), lambda b, g, i: (t_rows // cx + b, AV // HD + g)),
                  pl.BlockSpec((tq, gw), lambda b, g, i: (b * nq + i, AG // gw + g))],
        out_specs=[qblk, qblk, pl.BlockSpec((tq, 128), lambda b, g, i: (b * nq + i, g))],
        out_shape=[_sds((t_rows, D), F32), _sds((t_rows, D), BF16), _sds((t_rows, HKV * 128), F32)],
        compiler_params=_params(("parallel", "parallel", "parallel")),
    )(q16, kx16, kc16, px, px, px)


def _gate_specs(tm, col0):
    hw = D // 2
    return [pl.BlockSpec((tm, hw), lambda i: (i, col0 // hw)), pl.BlockSpec((tm, hw), lambda i: (i, col0 // hw + 1))]


def _merge(yret16, yatt16, px, w_o_ret16, w_o_att16, tm):
    t_rows = yret16.shape[0]
    hw = D // 2

    def body(yr_ref, wr_ref, ya_ref, wa_ref, mr0, mr1, ma0, ma1, ar_ref, aa_ref, y_ref):
        ar = _dot(yr_ref[...], wr_ref[...])
        aa = _dot(ya_ref[...], wa_ref[...])
        ar_ref[...] = ar.astype(BF16)
        aa_ref[...] = aa.astype(BF16)
        for j, (mr_ref, ma_ref) in enumerate(((mr0, ma0), (mr1, ma1))):
            sl = slice(j * hw, (j + 1) * hw)
            y_ref[:, sl] = (_sig(mr_ref[...].astype(F32)) * ar[:, sl]
                            + _sig(ma_ref[...].astype(F32)) * aa[:, sl]).astype(BF16)

    row = pl.BlockSpec((tm, D), lambda i: (i, 0))
    return pl.pallas_call(
        body, name="merge", grid=(t_rows // tm,),
        in_specs=[pl.BlockSpec((tm, RH * DV), lambda i: (i, 0)), pl.BlockSpec((RH * DV, D), lambda i: (0, 0)),
                  row, pl.BlockSpec((D, D), lambda i: (0, 0))] + _gate_specs(tm, MR) + _gate_specs(tm, MA),
        out_specs=[row, row, row], out_shape=[_sds((t_rows, D), BF16)] * 3,
        compiler_params=_params(("parallel",)),
    )(yret16, w_o_ret16, yatt16, w_o_att16, px, px, px, px)


def _outproj(y16, w_out16, x2, tgt, mod3, nb, seq, tm):
    t_rows = nb * seq
    bpb = seq // tm

    def body(y_ref, w_ref, x_ref, t_ref, g_ref, dxn_ref, dout_ref, dg_ref, loss_ref):
        i = pl.program_id(1)
        out = _dot(y_ref[...], w_ref[...])
        gate = g_ref[...]
        diff = x_ref[...] + gate * out - t_ref[...]
        dxn = diff * (1.0 / D)
        dxn_ref[...] = dxn
        dout_ref[...] = (gate * dxn).astype(BF16)
        dg = jnp.sum(dxn * out, axis=0, keepdims=True)
        ls = jnp.broadcast_to(jnp.sum(diff * diff) * (0.5 / D), (1, 128))

        @pl.when(i == 0)
        def _():
            dg_ref[...] = dg
            loss_ref[...] = ls

        @pl.when(i > 0)
        def _():
            dg_ref[...] += dg
            loss_ref[...] += ls

    row = pl.BlockSpec((tm, D), lambda b, i: (b * bpb + i, 0))
    return pl.pallas_call(
        body, name="outproj", grid=(nb, bpb),
        in_specs=[row, pl.BlockSpec((D, D), lambda b, i: (0, 0)), row, row,
                  pl.BlockSpec((None, 1, D), lambda b, i: (b, 0, 2))],
        out_specs=[row, row, pl.BlockSpec((None, 1, D), lambda b, i: (b, 0, 0)),
                   pl.BlockSpec((None, 1, 128), lambda b, i: (b, 0, 0))],
        out_shape=[_sds((t_rows, D), F32), _sds((t_rows, D), BF16), _sds((nb, 1, D), F32), _sds((nb, 1, 128), F32)],
        compiler_params=_params(("parallel", "arbitrary")),
    )(y16, w_out16, x2, tgt, mod3)


def _bwd_merge(dout16, w_out16, px, a_ret, a_att, tm):
    t_rows = dout16.shape[0]
    hw = D // 2

    def body(do_ref, w_ref, mr0, mr1, ma0, ma1, ar_ref, aa_ref, dar_ref, daa_ref, dmr_ref, dma_ref):
        dy_all = _dot(do_ref[...], w_ref[...], NT)
        for j, (mr_ref, ma_ref) in enumerate(((mr0, ma0), (mr1, ma1))):
            sl = slice(j * hw, (j + 1) * hw)
            dy = dy_all[:, sl]
            sr = _sig(mr_ref[...].astype(F32))
            sa = _sig(ma_ref[...].astype(F32))
            dar_ref[:, sl] = (dy * sr).astype(BF16)
            daa_ref[:, sl] = (dy * sa).astype(BF16)
            dmr_ref[:, sl] = (dy * ar_ref[:, sl].astype(F32) * sr * (1.0 - sr)).astype(BF16)
            dma_ref[:, sl] = (dy * aa_ref[:, sl].astype(F32) * sa * (1.0 - sa)).astype(BF16)

    row = pl.BlockSpec((tm, D), lambda i: (i, 0))
    return pl.pallas_call(
        body, name="bwd_merge", grid=(t_rows // tm,),
        in_specs=[row, pl.BlockSpec((D, D), lambda i: (0, 0))] + _gate_specs(tm, MR) + _gate_specs(tm, MA) + [row, row],
        out_specs=[row] * 4, out_shape=[_sds((t_rows, D), BF16)] * 4,
        compiler_params=_params(("parallel",)),
    )(dout16, w_out16, px, px, px, px, a_ret, a_att)


def _bwd_branch_ret(da_ret16, w_o_ret16, px, o_f, o_b, tm, after=()):
    t_rows = da_ret16.shape[0]

    def body(da_ref, w_ref, g0, g1, g2, g3, of_ref, ob_ref, *rest):
        do_ref, dg_ref = rest[-2:]
        da = da_ref[...]
        for h, g_ref in enumerate((g0, g1, g2, g3)):
            sl = slice(h * DV, (h + 1) * DV)
            dy = _dot(da, w_ref[sl, :], NT)
            g = g_ref[...].astype(F32)
            o = of_ref[:, sl].astype(F32) + ob_ref[:, sl].astype(F32)
            r = lax.rsqrt(jnp.mean(o * o, axis=-1, keepdims=True) + EPS)
            on = o * r
            don = dy * _silu(g)
            dg_ref[:, sl] = (dy * on * _dsilu(g)).astype(BF16)
            do_ref[:, sl] = (r * (don - on * jnp.mean(on * don, axis=-1, keepdims=True))).astype(BF16)

    def gate(h):
        return pl.BlockSpec((tm, DV), lambda i: (i, RG // DV + h))

    wide = pl.BlockSpec((tm, RH * DV), lambda i: (i, 0))
    return pl.pallas_call(
        body, name="bwd_branch_ret", grid=(t_rows // tm,),
        in_specs=[pl.BlockSpec((tm, D), lambda i: (i, 0)), pl.BlockSpec((RH * DV, D), lambda i: (0, 0))]
        + [gate(h) for h in range(RH)] + [wide, wide] + [pl.BlockSpec(memory_space=pl.ANY)] * len(after),
        out_specs=[wide, wide], out_shape=[_sds((t_rows, RH * DV), BF16)] * 2,
        compiler_params=_params(("parallel",)),
    )(da_ret16, w_o_ret16, *([px] * RH), o_f, o_b, *after)


def _bwd_branch_att(da_att16, w_o_att16, px, o_att, tm):
    t_rows = da_att16.shape[0]
    hw = D // 2

    def body(da_ref, w_ref, g0, g1, o_ref, dao_ref, dg_ref):
        dy_all = _dot(da_ref[...], w_ref[...], NT)
        for j, g_ref in enumerate((g0, g1)):
            sl = slice(j * hw, (j + 1) * hw)
            dy = dy_all[:, sl]
            g = g_ref[...].astype(F32)
            dao_ref[:, sl] = dy * _silu(g)
            dg_ref[:, sl] = (dy * o_ref[:, sl] * _dsilu(g)).astype(BF16)

    row = pl.BlockSpec((tm, D), lambda i: (i, 0))
    return pl.pallas_call(
        body, name="bwd_branch_att", grid=(t_rows // tm,),
        in_specs=[row, pl.BlockSpec((D, D), lambda i: (0, 0))] + _gate_specs(tm, AG) + [row],
        out_specs=[row, row], out_shape=[_sds((t_rows, D), F32), _sds((t_rows, D), BF16)],
        compiler_params=_params(("parallel",)),
    )(da_att16, w_o_att16, px, px, o_att)


def _att_bwd(q16, kx16, kc16, px, dao, o_att, lse, nb, seq, cx, tq):
    t_rows = nb * seq
    nq = seq // tq
    rep = HQ // HKV
    gw = rep * HD
    scale = HD ** -0.5

    def body(q_ref, kx_ref, kc_ref, vx_ref, vc_ref, dao_ref, o_ref, l_ref, dq_ref, dkx_ref, dvx_ref, dkc_ref, dvc_ref):
        i = pl.program_id(2)
        kx = kx_ref[...]
        kc = kc_ref[...]
        vx = vx_ref[...].astype(BF16)
        vc = vc_ref[...].astype(BF16)
        dkx = jnp.zeros((seq, HD), F32)
        dvx = jnp.zeros((seq, HD), F32)
        dkc = jnp.zeros((cx, HD), F32)
        dvc = jnp.zeros((cx, HD), F32)
        for r in range(rep):
            sl = slice(r * HD, (r + 1) * HD)
            q = q_ref[:, sl]
            lr = l_ref[:, r:r + 1]
            p1 = jnp.exp2(_dot(q, kx, NT) * SM_C - lr)
            p2 = jnp.exp2(_dot(q, kc, NT) * SM_C - lr)
            da = dao_ref[:, sl]
            da16 = da.astype(BF16)
            delta = jnp.sum(da * o_ref[:, sl], axis=-1, keepdims=True)
            ds1 = (p1 * (_dot(da16, vx, NT) - delta)).astype(BF16)
            ds2 = (p2 * (_dot(da16, vc, NT) - delta)).astype(BF16)
            dq_ref[:, sl] = (_dot(ds1, kx) + _dot(ds2, kc)) * scale
            dkx += _dot(ds1, q, TN)
            dkc += _dot(ds2, q, TN)
            dvx += _dot(p1.astype(BF16), da16, TN)
            dvc += _dot(p2.astype(BF16), da16, TN)
        dkx = dkx * scale
        dkc = dkc * scale

        @pl.when(i == 0)
        def _():
            dkx_ref[...] = dkx
            dvx_ref[...] = dvx
            dkc_ref[...] = dkc
            dvc_ref[...] = dvc

        @pl.when(i > 0)
        def _():
            dkx_ref[...] += dkx
            dvx_ref[...] += dvx
            dkc_ref[...] += dkc
            dvc_ref[...] += dvc

    qblk = pl.BlockSpec((tq, gw), lambda b, g, i: (b * nq + i, g))
    kxb = pl.BlockSpec((None, seq, HD), lambda b, g, i: (b, 0, g))
    kcb = pl.BlockSpec((None, cx, HD), lambda b, g, i: (b, 0, g))
    return pl.pallas_call(
        body, name="att_bwd", grid=(nb, HKV, nq),
        in_specs=[qblk,
                  pl.BlockSpec((seq, HD), lambda b, g, i: (b, g)),
                  pl.BlockSpec((cx, HD), lambda b, g, i: (b, g)),
                  pl.BlockSpec((seq, HD), lambda b, g, i: (b, AV // HD + g)),
                  pl.BlockSpec((cx, HD), lambda b, g, i: (t_rows // cx + b, AV // HD + g)),
                  qblk, qblk, pl.BlockSpec((tq, 128), lambda b, g, i: (b * nq + i, g))],
        out_specs=[qblk, kxb, kxb, kcb, kcb],
        out_shape=[_sds((t_rows, D), F32), _sds((nb, seq, HKV * HD), F32), _sds((nb, seq, HKV * HD), F32),
                   _sds((nb, cx, HKV * HD), F32), _sds((nb, cx, HKV * HD), F32)],
        compiler_params=_params(("parallel", "parallel", "arbitrary")),
    )(q16, kx16, kc16, px, px, dao, o_att, lse)


def _qk_prep_bwd(dt, px, nw, cos, sin, rows, row_off, col_off, heads, hb, seq, tm, name):
    rope = cos is not None
    rb0 = row_off // tm
    pb = seq // tm if rope else 1
    bw = hb * HD

    def body(*refs):
        if rope:
            d_ref, x_ref, w_ref, c_ref, s_ref, dx_ref, dw_ref = refs
        else:
            d_ref, x_ref, w_ref, dx_ref, dw_ref = refs
        first = jnp.logical_and(pl.program_id(0) == 0, pl.program_id(1) == 0)
        dw = jnp.zeros((1, HD), F32)
        for h in range(hb):
            sl = slice(h * HD, (h + 1) * HD)
            dtv = d_ref[:, sl]
            if rope:
                dtv = dtv * c_ref[...] + _swap_pairs(dtv * s_ref[...])
            xv = x_ref[:, sl].astype(F32)
            r = lax.rsqrt(jnp.mean(xv * xv, axis=-1, keepdims=True) + EPS)
            xh = xv * r
            dxh = dtv * w_ref[...]
            dx_ref[:, sl] = (r * (dxh - xh * jnp.mean(dxh * xh, axis=-1, keepdims=True))).astype(BF16)
            dw += jnp.sum(dtv * xh, axis=0, keepdims=True)

        @pl.when(first)
        def _():
            dw_ref[...] = dw

        @pl.when(jnp.logical_not(first))
        def _():
            dw_ref[...] += dw

    blk = pl.BlockSpec((tm, bw), lambda i, j: (i, j))
    in_specs = [blk, pl.BlockSpec((tm, bw), lambda i, j: (rb0 + i, col_off // bw + j)),
                pl.BlockSpec((1, HD), lambda i, j: (0, 0))]
    args = [dt, px, nw]
    if rope:
        in_specs += [pl.BlockSpec((tm, HD), lambda i, j: (i % pb, 0))] * 2
        args += [cos, sin]
    return pl.pallas_call(
        body, name=name, grid=(rows // tm, heads // hb), in_specs=in_specs,
        out_specs=[blk, pl.BlockSpec((1, HD), lambda i, j: (0, 0))],
        out_shape=[_sds((rows, heads * HD), BF16), _sds((1, HD), F32)],
        compiler_params=_params(("arbitrary", "arbitrary")),
    )(*args)


def _ret_bwd(px, lg, do16, hist_f, hist_b, nb, nc):
    t_rows = nb * nc * CH

    def body(lg_ref, *refs):
        ins = (refs[0:5], refs[7:12])
        do_refs = (refs[5], refs[12])
        h_refs = (refs[6], refs[13])
        outs = (refs[14:17], refs[17:20])
        ds_outs = (refs[20], refs[21])
        dlg_ref = refs[22]
        dss = (refs[23], refs[24])
        c = pl.program_id(1)

        @pl.when(c == 0)
        def _():
            dss[0][...] = jnp.zeros_like(dss[0])
            dss[1][...] = jnp.zeros_like(dss[1])
            dlg_ref[...] = jnp.zeros_like(dlg_ref)

        for d in range(2):
            dq_ref, dk_ref, dv_ref = outs[d]
            for h in range(RH):
                lg_d = lg_ref[d, h]
                mask, relf, qd, qe, kd, ke = _decays(lg_d, d == 0)
                g_ch = jnp.exp(lg_d * CH)
                q, k, v16 = _head_qkv(ins[d], h)
                q16 = q.astype(BF16)
                k16 = k.astype(BF16)
                do16v = do_refs[d][:, h * DV:(h + 1) * DV]
                st16 = h_refs[d][h]
                dst = dss[d][h]
                dst16 = dst.astype(BF16)
                a = _dot(q16, k16, NT) * mask
                dp = _dot(do16v, v16, NT)
                da16 = (dp * mask).astype(BF16)
                dq_cross = _dot(do16v, st16, NT) * qd
                dq_ref[:, h * DK:(h + 1) * DK] = (_dot(da16, k16) + dq_cross).astype(BF16)
                dk_state = _dot(v16, dst16, NT) * kd
                dk_ref[:, h * DK:(h + 1) * DK] = ((_dot(da16, q16, TN) + dk_state) * (DK ** -0.5)).astype(BF16)
                dv = _dot(a.astype(BF16), do16v, TN) + _dot((k * kd).astype(BF16), dst16)
                dv_ref[:, h * DV:(h + 1) * DV] = dv.astype(BF16)
                dlg = (jnp.sum(relf * a * dp)
                       + jnp.sum(qe * jnp.sum(q * dq_cross, axis=-1, keepdims=True))
                       + jnp.sum(ke * jnp.sum(k * dk_state, axis=-1, keepdims=True))
                       + CH * g_ch * jnp.sum(dst * st16.astype(F32)))
                row = d * RH + h
                dlg_ref[row:row + 1, :] += jnp.broadcast_to(dlg, (1, 128))
                ds_new = g_ch * dst + _dot((q * qd).astype(BF16), do16v, TN)
                dss[d][h] = ds_new

                @pl.when(c == nc - 1)
                def _():
                    ds_outs[d][h] = ds_new

    def fw(b, c):
        return b * nc + nc - 1 - c

    def bw(b, c):
        return b * nc + c

    def rows(rowf, width):
        return pl.BlockSpec((CH, width), lambda b, c: (rowf(b, c), 0))

    def hist(rowf):
        return pl.BlockSpec((None, None, RH, DK, DV), lambda b, c: (b, rowf(0, c), 0, 0, 0))

    st = pl.BlockSpec((None, RH, DK, DV), lambda b, c: (b, 0, 0, 0))
    in_specs = [pl.BlockSpec(memory_space=pltpu.SMEM)]
    out_specs = []
    for rowf in (fw, bw):
        in_specs += _wide_specs(rowf) + [rows(rowf, RH * DV), hist(rowf)]
        out_specs += [rows(rowf, RH * DK), rows(rowf, RH * DK), rows(rowf, RH * DV)]
    out_specs += [st, st, pl.BlockSpec((None, 8, 128), lambda b, c: (b, 0, 0))]
    qk = _sds((t_rows, RH * DK), BF16)
    vv = _sds((t_rows, RH * DV), BF16)
    return pl.pallas_call(
        body, name="ret_bwd", grid=(nb, nc), in_specs=in_specs, out_specs=out_specs,
        out_shape=[qk, qk, vv, qk, qk, vv, _sds((nb, RH, DK, DV), F32), _sds((nb, RH, DK, DV), F32),
                   _sds((nb, 8, 128), F32)],
        scratch_shapes=[pltpu.VMEM((RH, DK, DV), F32), pltpu.VMEM((RH, DK, DV), F32)],
        compiler_params=_params(("parallel", "arbitrary")),
    )(lg, *([px] * 5), do16, hist_f, *([px] * 5), do16, hist_b)


def _ctx_state_bwd(px, lg, ds_f, ds_b, nb, t_rows, cx):
    rb = t_rows // cx

    def body(lg_ref, k_ref, v_ref, dsf_ref, dsb_ref, dk_ref, dv_ref, dlg_ref):
        h = pl.program_id(1)
        pos = lax.broadcasted_iota(jnp.int32, (cx, 1), 0).astype(F32)
        k = k_ref[...].astype(F32) * (DK ** -0.5)
        v16 = v_ref[...].astype(BF16)
        dk = jnp.zeros((cx, DK), F32)
        dv = jnp.zeros((cx, DV), F32)
        dlg_ref[...] = jnp.zeros_like(dlg_ref)
        for d, (ds_ref, e) in enumerate(((dsf_ref, cx - 1.0 - pos), (dsb_ref, pos))):
            w = jnp.exp(lg_ref[d, h] * e)
            ds16 = ds_ref[...].astype(BF16)
            t = _dot(v16, ds16, NT)
            dk += t * w
            dv += _dot((k * w).astype(BF16), ds16)
            dlg = jnp.sum(e * w * jnp.sum(k * t, axis=-1, keepdims=True))
            dlg_ref[d:d + 1, :] = jnp.broadcast_to(dlg, (1, 128))
        dk_ref[...] = (dk * (DK ** -0.5)).astype(BF16)
        dv_ref[...] = dv.astype(BF16)

    st = pl.BlockSpec((None, None, DK, DV), lambda b, h: (b, h, 0, 0))
    return pl.pallas_call(
        body, name="ctx_state_bwd", grid=(nb, RH),
        in_specs=[pl.BlockSpec(memory_space=pltpu.SMEM),
                  pl.BlockSpec((cx, DK), lambda b, h: (rb + b, RK // DK + h)),
                  pl.BlockSpec((cx, DV), lambda b, h: (rb + b, RV // DV + h)), st, st],
        out_specs=[pl.BlockSpec((cx, DK), lambda b, h: (b, h)), pl.BlockSpec((cx, DV), lambda b, h: (b, h)),
                   pl.BlockSpec((None, None, 8, 128), lambda b, h: (b, h, 0, 0))],
        out_shape=[_sds((nb * cx, RH * DK), BF16), _sds((nb * cx, RH * DV), BF16), _sds((nb, RH, 8, 128), F32)],
        compiler_params=_params(("parallel", "parallel")),
    )(lg, px, px, ds_f, ds_b)


def _assemble_lat(rows_all, dk_f, dk_b, dv_f, dv_b, dak16, dvx, dq_f, dq_b, drg16, daq16, dag16, dmr16, dma16, tm):
    t_rows = dk_f.shape[0]

    def body(dkf, dkb, dvf, dvb, dak, dav, dqf, dqb, drg, daq, dag, dmr, dma, o_ref):
        o_ref[:, RK:RK + RH * DK] = (dkf[...].astype(F32) + dkb[...].astype(F32)).astype(BF16)
        o_ref[:, RV:RV + RH * DV] = (dvf[...].astype(F32) + dvb[...].astype(F32)).astype(BF16)
        o_ref[:, AK:AK + HKV * HD] = dak[...]
        o_ref[:, AV:AV + HKV * HD] = dav[...].astype(BF16)
        o_ref[:, RQ:RQ + RH * DK] = (dqf[...].astype(F32) + dqb[...].astype(F32)).astype(BF16)
        o_ref[:, RG:RG + RH * DV] = drg[...]
        o_ref[:, AQ:AQ + D] = daq[...]
        o_ref[:, AG:AG + D] = dag[...]
        o_ref[:, MR:MR + D] = dmr[...]
        o_ref[:, MA:MA + D] = dma[...]

    args = (dk_f, dk_b, dv_f, dv_b, dak16, dvx, dq_f, dq_b, drg16, daq16, dag16, dmr16, dma16)
    return pl.pallas_call(
        body, name="assemble_lat", grid=(t_rows // tm,),
        in_specs=[pl.BlockSpec((tm, a.shape[1]), lambda i: (i, 0)) for a in args],
        out_specs=pl.BlockSpec((tm, IN_COLS), lambda i: (i, 0)), out_shape=_sds((rows_all, IN_COLS), BF16),
        compiler_params=_params(("parallel",)),
    )(*args)


def _assemble_ctx(dp_all, dck16, dcv16, dcak16, dvc, t_rows, tm):
    c_rows = dck16.shape[0]
    rb = t_rows // tm

    def body(_, dck, dcv, dcak, dcav, o_ref):
        o_ref[:, RK:RK + RH * DK] = dck[...]
        o_ref[:, RV:RV + RH * DV] = dcv[...]
        o_ref[:, AK:AK + HKV * HD] = dcak[...]
        o_ref[:, AV:AV + HKV * HD] = dcav[...].astype(BF16)
        o_ref[:, KV_COLS:] = jnp.zeros((tm, IN_COLS - KV_COLS), BF16)

    args = (dck16, dcv16, dcak16, dvc)
    return pl.pallas_call(
        body, name="assemble_ctx", grid=(c_rows // tm,),
        in_specs=[pl.BlockSpec(memory_space=pl.ANY)]
        + [pl.BlockSpec((tm, a.shape[1]), lambda i: (i, 0)) for a in args],
        out_specs=pl.BlockSpec((tm, IN_COLS), lambda i: (rb + i, 0)), out_shape=_sds(dp_all.shape, BF16),
        input_output_aliases={0: 0},
        compiler_params=_params(("parallel",)),
    )(dp_all, *args)


def _norm_bwd(dh, x2, mod3, norm_w, dxn, row_off, rows_per_group, group0, tm, name):
    with_dx = dxn is not None
    rows = x2.shape[0]
    rb0 = row_off // tm
    bpg = rows_per_group // tm
    ngroups = rows // rows_per_group

    def body(*refs):
        if with_dx:
            dh_ref, x_ref, sc_ref, nw_ref, dxn_ref, dx_ref, dsh_ref, dsc_ref, dnw_ref = refs
        else:
            dh_ref, x_ref, sc_ref, nw_ref, dsh_ref, dsc_ref, dnw_ref = refs
        i = pl.program_id(0)
        dhv = dh_ref[...]
        xv = x_ref[...]
        nw = nw_ref[...]
        r = lax.rsqrt(jnp.mean(xv * xv, axis=-1, keepdims=True) + EPS)
        xh = xv * r
        dm = dhv * (1.0 + sc_ref[...])
        dsh = jnp.sum(dhv, axis=0, keepdims=True)
        dsc = jnp.sum(dhv * (xh * nw), axis=0, keepdims=True)
        dnw = jnp.sum(dm * xh, axis=0, keepdims=True)
        if with_dx:
            dxh = dm * nw
            dx_ref[...] = dxn_ref[...] + r * (dxh - xh * jnp.mean(dxh * xh, axis=-1, keepdims=True))

        @pl.when(i % bpg == 0)
        def _():
            dsh_ref[...] = dsh
            dsc_ref[...] = dsc

        @pl.when(i % bpg != 0)
        def _():
            dsh_ref[...] += dsh
            dsc_ref[...] += dsc

        @pl.when(i == 0)
        def _():
            dnw_ref[...] = dnw

        @pl.when(i > 0)
        def _():
            dnw_ref[...] += dnw

    grp = pl.BlockSpec((None, 1, D), lambda i: (i // bpg, 0, 0))
    in_specs = [pl.BlockSpec((tm, D), lambda i: (rb0 + i, 0)), pl.BlockSpec((tm, D), lambda i: (i, 0)),
                pl.BlockSpec((None, 1, D), lambda i: (group0 + i // bpg, 0, 1)),
                pl.BlockSpec((1, D), lambda i: (0, 0))]
    args = [dh, x2, mod3, norm_w]
    out_specs = [grp, grp, pl.BlockSpec((1, D), lambda i: (0, 0))]
    out_shape = [_sds((ngroups, 1, D), F32), _sds((ngroups, 1, D), F32), _sds((1, D), F32)]
    if with_dx:
        in_specs.append(pl.BlockSpec((tm, D), lambda i: (i, 0)))
        args.append(dxn)
        out_specs.insert(0, pl.BlockSpec((tm, D), lambda i: (i, 0)))
        out_shape.insert(0, _sds((rows, D), F32))
    return pl.pallas_call(
        body, name=name, grid=(rows // tm,), in_specs=in_specs, out_specs=out_specs, out_shape=out_shape,
        compiler_params=_params(("arbitrary",)),
    )(*args)


def _small_final(dmod_all, dmodc_parts, c_rows, dm_loc_rows, nw_parts, misc_parts, c_ctx, r_pad, w_ada16):
    loc = dm_loc_rows.shape[1]

    def body(dm_ref, dmc_ref, c_ref, dml_ref, nwp_ref, mp_ref, cc_ref, r_ref, w_ref,
             gb_ref, gc_ref, gnw_ref, misc_ref, gwa_ref):
        dmc = jnp.sum(dmc_ref[...], axis=0, keepdims=True)
        gb_ref[...] = jnp.sum(dm_ref[...], axis=0, keepdims=True) + dmc
        dsc = _dot(jnp.broadcast_to(dmc, (8, 3 * D)).astype(BF16), w_ref[...], NT)[0:1, :]
        gc_ref[...] = dsc * _dsilu(cc_ref[...])
        gnw_ref[...] = jnp.sum(nwp_ref[...], axis=0, keepdims=True)
        misc = jnp.sum(mp_ref[...], axis=0, keepdims=True)
        y = jnp.exp2(r_ref[...])
        lane = lax.broadcasted_iota(jnp.int32, (1, D), 1)
        is_decay = jnp.logical_and(lane >= 2 * HD, lane < 2 * HD + 2 * RH)
        misc_ref[...] = misc * jnp.where(is_decay, -(y * np.float32(np.log(2.0))) / (1.0 - y), 1.0)
        gwa_ref[...] = _dot(_silu(c_ref[...]).astype(BF16), dml_ref[...].astype(BF16), TN)

    return pl.pallas_call(
        body, name="small_final",
        out_shape=[_sds((1, 3 * D), F32), _sds((1, D), F32), _sds((1, D), F32), _sds((1, D), F32), _sds((D, loc), F32)],
        compiler_params=pltpu.CompilerParams(vmem_limit_bytes=VMEM_LIMIT),
    )(dmod_all, dmodc_parts, c_rows, dm_loc_rows, nw_parts, misc_parts, c_ctx, r_pad, w_ada16)


def _adamw(w, g, m, v, name):
    rows, cols = w.shape
    tm = _pick(rows, 256, 8)
    bc1 = 1.0 - B1 ** STEP
    bc2 = 1.0 - B2 ** STEP

    def body(w_ref, g_ref, m_ref, v_ref, d_ref, nm_ref, nv_ref):
        g_ = g_ref[...]
        nm = B1 * m_ref[...] + (1.0 - B1) * g_
        nv = B2 * v_ref[...] + (1.0 - B2) * (g_ * g_)
        nm_ref[...] = nm
        nv_ref[...] = nv
        d_ref[...] = -LR * ((nm / bc1) / (jnp.sqrt(nv / bc2) + ADAM_EPS) + WD * w_ref[...])

    blk = pl.BlockSpec((tm, cols), lambda i: (i, 0))
    return pl.pallas_call(
        body, name=name, grid=(rows // tm,), in_specs=[blk] * 4, out_specs=[blk] * 3,
        out_shape=[_sds((rows, cols), F32)] * 3, compiler_params=_params(("parallel",)),
    )(w, g, m, v)


def _mesh_pos():
    return lax.axis_index("x"), lax.axis_index("y"), lax.axis_index("c")


def _all_gather(arrs, name):
    n = len(arrs)

    def body(*refs):
        ins, outs = refs[:n], refs[n:2 * n]
        send_sems, recv_sems, local_sems = refs[2 * n:]
        x, y, c = _mesh_pos()
        me, sib = (x, y, c), (x, y, 1 - c)
        chips = [(1 - x, y), (x, 1 - y), (1 - x, 1 - y)]

        def slot(p):
            return 4 * p[0] + 2 * p[1] + p[2]

        def copy(a, k, block, to, own):
            dst = outs[a].at[slot(block)]
            return pltpu.make_async_remote_copy(
                src_ref=ins[a] if own else dst, dst_ref=dst, send_sem=send_sems.at[a, k], recv_sem=recv_sems.at[a, k],
                device_id=to, device_id_type=MESH_T)

        mine = [pltpu.make_async_copy(ins[a], outs[a].at[slot(me)], local_sems.at[a]) for a in range(n)]
        for cp in mine:
            cp.start()
        first = []
        for a in range(n):
            first.append(copy(a, 0, me, sib, True))
            first += [copy(a, 1 + j, me, (*chip, c), True) for j, chip in enumerate(chips)]
        for cp in first:
            cp.start()
        passed = []
        for j, chip in enumerate(chips):
            for a in range(n):
                copy(a, 1 + j, (*chip, c), me, False).wait_recv()
                fwd = copy(a, 4 + j, (*chip, c), sib, False)
                fwd.start()
                passed.append(fwd)
        for a in range(n):
            copy(a, 0, sib, me, False).wait_recv()
            for j, chip in enumerate(chips):
                copy(a, 4 + j, (*chip, 1 - c), me, False).wait_recv()
        for cp in first + passed:
            cp.wait_send()
        for cp in mine:
            cp.wait()

    hbm = pl.BlockSpec(memory_space=pl.ANY)
    return pl.pallas_call(
        body, name=name, in_specs=[hbm] * n, out_specs=[hbm] * n,
        out_shape=[_sds((N_DEV,) + a.shape, a.dtype) for a in arrs],
        scratch_shapes=[pltpu.SemaphoreType.DMA((n, 7)), pltpu.SemaphoreType.DMA((n, 7)), pltpu.SemaphoreType.DMA((n,))],
    )(*arrs)


def _pair_exchange(arrs, name):
    n = len(arrs)

    def body(*refs):
        ins, outs = refs[:n], refs[n:2 * n]
        send_sems, recv_sems = refs[2 * n:]
        x, y, c = _mesh_pos()
        sib = (x, y, 1 - c)
        sends = []
        for a in range(n):
            for k in range(4):
                sends.append(pltpu.make_async_remote_copy(
                    src_ref=ins[a].at[2 * k + 1 - c], dst_ref=outs[a].at[k], send_sem=send_sems.at[a, k],
                    recv_sem=recv_sems.at[a, k], device_id=sib, device_id_type=MESH_T))
        for cp in sends:
            cp.start()
        for cp in sends:
            cp.wait_recv()
        for cp in sends:
            cp.wait_send()

    hbm = pl.BlockSpec(memory_space=pl.ANY)
    return pl.pallas_call(
        body, name=name, in_specs=[hbm] * n, out_specs=[hbm] * n,
        out_shape=[_sds((4,) + a.shape[1:], a.dtype) for a in arrs],
        scratch_shapes=[pltpu.SemaphoreType.DMA((n, 4)), pltpu.SemaphoreType.DMA((n, 4))],
    )(*arrs)


def _pair_add(part, got, core, name):
    _, rows, cols = part.shape
    tm = _pick(rows, 256, 16)
    p4 = part.reshape(4, 2, rows, cols)

    def body(core_ref, p_ref, g_ref, o_ref):
        o_ref[...] = (p_ref[...].astype(F32) + g_ref[...].astype(F32)).astype(BF16)

    blk = pl.BlockSpec((None, tm, cols), lambda k, i, cr: (k, i, 0))
    return pl.pallas_call(
        body, name=name,
        grid_spec=pltpu.PrefetchScalarGridSpec(
            num_scalar_prefetch=1, grid=(4, rows // tm),
            in_specs=[pl.BlockSpec((None, None, tm, cols), lambda k, i, cr: (k, cr[0], i, 0)), blk], out_specs=blk),
        out_shape=_sds((4, rows, cols), BF16), compiler_params=_params(("parallel", "parallel")),
    )(core, p4, got)


def _chip_sum(pair_sums, landed, chip, name):
    _, rows, cols = pair_sums.shape
    tm = _pick(rows, 256, 16)

    def body(chip_ref, s_ref, l_ref, o_ref):
        acc = s_ref[...].astype(F32)
        for j in range(3):
            acc = acc + l_ref[j].astype(F32)
        o_ref[...] = acc

    return pl.pallas_call(
        body, name=name,
        grid_spec=pltpu.PrefetchScalarGridSpec(
            num_scalar_prefetch=1, grid=(rows // tm,),
            in_specs=[pl.BlockSpec((None, tm, cols), lambda i, ch: (ch[0], i, 0)),
                      pl.BlockSpec((3, tm, cols), lambda i, ch: (0, i, 0))],
            out_specs=pl.BlockSpec((tm, cols), lambda i, ch: (i, 0))),
        out_shape=_sds((rows, cols), F32), compiler_params=_params(("parallel",)),
    )(chip, pair_sums, landed)


_HBM = pl.BlockSpec(memory_space=pltpu.HBM)
_SEM = pl.BlockSpec(memory_space=pltpu.SEMAPHORE)
_EFFECT = pltpu.SideEffectType.DATAFLOW_SIDE_EFFECTING


def _chip_routes(n):
    def plan(x, y, c):
        routes = []
        for a in range(n):
            for j in range(1, 4):
                px, py = x ^ (j >> 1), y ^ (j & 1)
                routes.append((a, 2 * px + py, (px, py, c), j - 1))
        return routes
    return plan, 3 * n


def _bcast_routes(n):
    def plan(x, y, c):
        routes = []
        for a in range(n):
            for k in range(1, N_DEV):
                peer = (x ^ ((k >> 2) & 1), y ^ ((k >> 1) & 1), c ^ (k & 1))
                routes.append((a, 0, peer, 4 * x + 2 * y + c))
        return routes
    return plan, 7 * n


def _route_copies(srcs, lands, send_sems, recv_sems, routes):
    return [pltpu.make_async_remote_copy(
        src_ref=srcs[a].at[sb], dst_ref=lands[a].at[lb], send_sem=send_sems.at[r], recv_sem=recv_sems.at[r],
        device_id=peer, device_id_type=MESH_T) for r, (a, sb, peer, lb) in enumerate(routes)]


def _exchange_start(srcs, lands, routes, name):
    plan, count = routes
    n = len(srcs)

    def body(*refs):
        send_sems, recv_sems = refs[2 * n], refs[2 * n + 1]
        token = refs[-1]
        for cp in _route_copies(refs[:n], refs[n:2 * n], send_sems, recv_sems, plan(*_mesh_pos())):
            cp.start()
        token[...] = jnp.zeros_like(token)

    args = [pltpu.with_memory_space_constraint(a, pltpu.HBM) for a in list(srcs) + list(lands)]
    out = pl.pallas_call(
        body, name=name,
        out_shape=(pltpu.SemaphoreType.DMA((count,)), pltpu.SemaphoreType.DMA((count,)),
                   *[pltpu.HBM(a.shape, a.dtype) for a in args], _sds((8, 128), F32)),
        in_specs=[_HBM] * (2 * n), out_specs=(_SEM, _SEM, *([_HBM] * (2 * n)), pl.BlockSpec(memory_space=pltpu.VMEM)),
        input_output_aliases={i: 2 + i for i in range(2 * n)},
        compiler_params=pltpu.CompilerParams(has_side_effects=_EFFECT),
    )(*args)
    return (out[0], out[1], list(out[2:2 + 2 * n]), routes), out[-1]


def _exchange_wait(state, after, name):
    send_sems, recv_sems, bufs, (plan, count) = state
    n = len(bufs) // 2

    def body(*refs):
        send_s, recv_s = refs[2 * n], refs[2 * n + 1]
        for cp in _route_copies(refs[:n], refs[n:2 * n], send_s, recv_s, plan(*_mesh_pos())):
            cp.wait_send()
            cp.wait_recv()

    out = pl.pallas_call(
        body, name=name, out_shape=tuple(pltpu.HBM(a.shape, a.dtype) for a in bufs),
        in_specs=[_HBM] * (2 * n) + [_SEM, _SEM, pl.BlockSpec(memory_space=pl.ANY)], out_specs=tuple([_HBM] * (2 * n)),
        input_output_aliases={i: i for i in range(2 * n)},
        compiler_params=pltpu.CompilerParams(has_side_effects=_EFFECT),
    )(*bufs, send_sems, recv_sems, after)
    return list(out[:n]), list(out[n:])


def _reduce_scatter_start(parts, core, name):
    got = _pair_exchange(parts, name + "_pair")
    sums = [_pair_add(p, g, core, "%s_add_%d" % (name, i)) for i, (p, g) in enumerate(zip(parts, got))]
    lands = [lax.empty((3,) + s_.shape[1:], BF16) for s_ in sums]
    return _exchange_start(sums, lands, _chip_routes(len(sums)), name + "_start")


def _reduce_scatter_finish(rs_state, after, chip, name):
    sums, landed = _exchange_wait(rs_state, after, name + "_wait")
    return [_chip_sum(s_, l_, chip, "%s_sum_%d" % (name, i)) for i, (s_, l_) in enumerate(zip(sums, landed))]


def _local_step(x, c, ctx, c_ctx, norm_w, b_ada, ret_log2_decay, q_norm_w, k_norm_w, loss_target,
                w_ada16, w_in_t16, get_w_o, on_out_grads, on_in_grad, started=()):
    nb, seq, _ = x.shape
    cx = ctx.shape[1]
    t_rows, c_rows = nb * seq, nb * cx
    rows_all = t_rows + c_rows
    nc = seq // CH
    tm = _pick(seq, 256, 128)
    te = _pick(seq, 512, 128)
    assert cx % tm == 0 and t_rows % cx == 0 and seq % GRID_W == 0

    x2 = x.reshape(t_rows, D)
    ctx2 = ctx.reshape(c_rows, D)
    tgt = loss_target.reshape(t_rows, D)
    c8 = jnp.zeros((8, D), F32).at[:nb].set(c).at[nb].set(c_ctx)
    lg = _log_gamma(ret_log2_decay)
    cos, sin = _rope_tables(seq)

    mod = _mod_fwd(c8, w_ada16, b_ada)
    mod3 = mod[:, None, :]
    h_all = _norm_fwd(x2, mod3, norm_w, rows_all, 0, seq, 0, None, te, "norm_fwd")
    h_all = _norm_fwd(ctx2, mod3, norm_w, rows_all, t_rows, c_rows, nb, h_all, tm, "norm_fwd_ctx")
    px = _matmul(h_all, w_in_t16, tb=True, tm=1536, tn=1536, tk=D, out_dtype=BF16, name="in_proj", after=started)
    s0f, s0b = _ctx_state(px, lg, nb, t_rows, cx)
    o_f, o_b, hist_f, hist_b = _ret_fwd(px, lg, s0f, s0b, nb, nc)
    yret16 = _ret_post(o_f, o_b, px, te)
    q16 = _qk_prep(px, q_norm_w, cos, sin, t_rows, 0, AQ, HQ, 4, seq, te, "q_prep")
    kx16 = _qk_prep(px, k_norm_w, cos, sin, t_rows, 0, AK, HKV, HKV, seq, te, "k_prep")
    kc16 = _qk_prep(px, k_norm_w, None, None, c_rows, t_rows, AK, HKV, HKV, seq, tm, "kc_prep")
    o_att, yatt16, lse = _att_fwd(q16, kx16, kc16, px, nb, seq, cx, tm)
    w_o_ret16, w_o_att16, w_out16 = get_w_o(lse)
    a_ret, a_att, y16 = _merge(yret16, yatt16, px, w_o_ret16, w_o_att16, te)
    dxn, dout16, dgate, loss_b = _outproj(y16, w_out16, x2, tgt, mod3, nb, seq, te)

    gw_out = _matmul(y16, dout16, ta=True, tm=D, tn=D, tk=D, out_dtype=BF16, name="gw_out")
    da_ret16, da_att16, dmr16, dma16 = _bwd_merge(dout16, w_out16, px, a_ret, a_att, te)
    gw_o_ret = _matmul(yret16, da_ret16, ta=True, tm=D, tn=D, tk=D, out_dtype=BF16, name="gw_o_ret")
    gw_o_att = _matmul(yatt16, da_att16, ta=True, tm=D, tn=D, tk=D, out_dtype=BF16, name="gw_o_att")
    out_state, out_started = on_out_grads([gw_o_ret, gw_o_att, gw_out])
    do16, drg16 = _bwd_branch_ret(da_ret16, w_o_ret16, px, o_f, o_b, te, after=out_started)
    dao, dag16 = _bwd_branch_att(da_att16, w_o_att16, px, o_att, te)
    dq_rot, dkx, dvx, dkc, dvc = _att_bwd(q16, kx16, kc16, px, dao, o_att, lse, nb, seq, cx, tm)
    daq16, gq = _qk_prep_bwd(dq_rot, px, q_norm_w, cos, sin, t_rows, 0, AQ, HQ, 4, seq, te, "q_prep_bwd")
    dak16, gk_lat = _qk_prep_bwd(dkx.reshape(t_rows, HKV * HD), px, k_norm_w, cos, sin, t_rows, 0, AK, HKV, HKV, seq, te,
                                 "k_prep_bwd")
    dcak16, gk_ctx = _qk_prep_bwd(dkc.reshape(c_rows, HKV * HD), px, k_norm_w, None, None, c_rows, t_rows, AK, HKV, HKV,
                                  seq, tm, "kc_prep_bwd")
    dq_f, dk_f, dv_f, dq_b, dk_b, dv_b, ds_f, ds_b, dlg_scan = _ret_bwd(px, lg, do16, hist_f, hist_b, nb, nc)
    dck16, dcv16, dlg_ctx = _ctx_state_bwd(px, lg, ds_f, ds_b, nb, t_rows, cx)
    dp_all = _assemble_lat(rows_all, dk_f, dk_b, dv_f, dv_b, dak16, dvx.reshape(t_rows, HKV * HD), dq_f, dq_b, drg16,
                           daq16, dag16, dmr16, dma16, tm)
    dp_all = _assemble_ctx(dp_all, dck16, dcv16, dcak16, dvc.reshape(c_rows, HKV * HD), t_rows, tm)
    gw_in_t = _matmul(dp_all, h_all, ta=True, tm=1536, tn=D, tk=1536, out_dtype=BF16, name="gw_in")
    in_state, in_started = on_in_grad(gw_in_t)
    dh = _matmul(dp_all, w_in_t16, tm=1536, tn=D, tk=1536, out_dtype=F32, name="d_h", after=in_started)
    grad_x, dsh, dsc, gnw_lat = _norm_bwd(dh, x2, mod3, norm_w, dxn, 0, seq, 0, te, "norm_bwd")
    dsh_c, dsc_c, gnw_ctx = _norm_bwd(dh, ctx2, mod3, norm_w, None, t_rows, c_rows, nb, tm, "norm_bwd_ctx")

    dlg = (jnp.sum(dlg_scan[:, :, 0], axis=0) + jnp.sum(dlg_ctx[:, :, :2, 0], axis=0).T.reshape(2 * RH)).reshape(1, 2 * RH)
    misc = jnp.concatenate([gq, gk_lat + gk_ctx, dlg, jnp.sum(loss_b[:, 0, 0]).reshape(1, 1),
                            jnp.zeros((1, D - 2 * HD - 2 * RH - 1), F32)], axis=1)
    rows = []
    for b in range(nb):
        rows += [dsh[b], dsc[b], dgate[b]]
    rows += [dsh_c[0], dsc_c[0]] + [c[b:b + 1] for b in range(nb)] + [gnw_lat + gnw_ctx, misc]
    payload = jnp.concatenate(rows + [jnp.zeros((PAY_ROWS - len(rows), D), F32)], axis=0)
    return grad_x.reshape(nb, seq, D), out_state, in_state, payload


def _finish_small(gathered, nb, c_ctx, ret_log2_decay, w_ada16, dev):
    n_dev = gathered.shape[0]
    loc = 3 * D // n_dev
    dmod_all = gathered[:, :3 * nb].reshape(n_dev * nb, 3 * D)
    dmodc_parts = jnp.concatenate([gathered[:, 3 * nb:3 * nb + 2].reshape(n_dev, 2 * D), jnp.zeros((n_dev, D), F32)], axis=1)
    c_all = gathered[:, 3 * nb + 2:4 * nb + 2].reshape(n_dev * nb, D)
    nw_parts = gathered[:, 4 * nb + 2]
    misc_parts = gathered[:, 4 * nb + 3]
    n_rows = n_dev * nb + n_dev
    pad = (-n_rows) % 16
    c_rows = jnp.concatenate([c_all, jnp.broadcast_to(c_ctx.reshape(1, D), (n_dev, D)), jnp.zeros((pad, D), F32)], axis=0)
    dm_rows = jnp.concatenate([dmod_all, dmodc_parts, jnp.zeros((pad, 3 * D), F32)], axis=0)
    dm_loc_rows = lax.dynamic_slice_in_dim(dm_rows, dev * loc, loc, axis=1)
    r_pad = jnp.full((1, D), -1.0, F32).at[:, 2 * HD:2 * HD + 2 * RH].set(ret_log2_decay.reshape(1, 2 * RH))
    gb, gc, gnw, misc, gwa = _small_final(dmod_all, dmodc_parts, c_rows, dm_loc_rows, nw_parts, misc_parts,
                                          c_ctx.reshape(1, D), r_pad, w_ada16)
    return (gb, gc, gnw, misc[:, :HD], misc[:, HD:2 * HD], misc[:, 2 * HD:2 * HD + 2 * RH], gwa,
            misc[0, 2 * HD + 2 * RH])


def kernel(x, c, ctx, c_ctx, norm_w, w_ada, b_ada, w_in, ret_log2_decay, q_norm_w, k_norm_w, w_o_ret, w_o_att, w_out, loss_target, m_c_ctx, m_norm_w, m_w_ada, m_b_ada, m_w_in, m_ret_log2_decay, m_q_norm_w, m_k_norm_w, m_w_o_ret, m_w_o_att, m_w_out, v_c_ctx, v_norm_w, v_w_ada, v_b_ada, v_w_in, v_ret_log2_decay, v_q_norm_w, v_k_norm_w, v_w_o_ret, v_w_o_att, v_w_out):
    nb = x.shape[0]
    mx, my, mc = _mesh_pos()
    dev = 4 * mx + 2 * my + mc
    core = jnp.reshape(mc, (1,)).astype(jnp.int32)
    chip = jnp.reshape(2 * mx + my, (1,)).astype(jnp.int32)

    w_in_t = jnp.transpose(w_in[0])
    g_in, g_ada = _all_gather([w_in_t.astype(BF16), w_ada[0].astype(BF16)], "gather_weights")
    w_in_t16 = g_in.reshape(IN_COLS, D)
    w_ada16 = jnp.transpose(g_ada, (1, 0, 2)).reshape(D, 3 * D)

    wo_shards = [w_[0].astype(BF16) for w_ in (w_o_ret, w_o_att, w_out)]
    wo_shards = list(lax.optimization_barrier((g_in, *wo_shards))[1:])
    wo_lands = [lax.dynamic_update_slice(lax.empty((N_DEV,) + s_.shape, BF16), s_[None], (dev, 0, 0)) for s_ in wo_shards]
    wo_state, wo_token = _exchange_start([s_[None] for s_ in wo_shards], wo_lands, _bcast_routes(3), "gather_wo_start")

    def get_w_o(after):
        _, (l_ret, l_att, l_out) = _exchange_wait(wo_state, after, "gather_wo_wait")
        return l_ret.reshape(RH * DV, D), l_att.reshape(D, D), l_out.reshape(D, D)

    def on_out_grads(grads):
        parts = [g_.reshape(N_DEV, g_.shape[0] // N_DEV, D) for g_ in grads]
        state, token = _reduce_scatter_start(parts, core, "rs_out")
        return state, (token,)

    def on_in_grad(grad):
        state, token = _reduce_scatter_start([grad.reshape(N_DEV, IN_COLS // N_DEV, D)], core, "rs_in")
        return state, (token,)

    grad_x, out_state, in_state, payload = _local_step(
        x, c, ctx, c_ctx, norm_w, b_ada, ret_log2_decay, q_norm_w, k_norm_w, loss_target,
        w_ada16, w_in_t16, get_w_o, on_out_grads, on_in_grad, started=(wo_token,))

    (gathered,) = _all_gather([payload], "gather_small")
    gb, gc, gnw, gq, gk, gr, gwa, loss = _finish_small(gathered, nb, c_ctx, ret_log2_decay, w_ada16, dev)

    g_w_o_ret, g_w_o_att, g_w_out = _reduce_scatter_finish(out_state, gathered, chip, "rs_out")
    (g_w_in_t,) = _reduce_scatter_finish(in_state, gathered, chip, "rs_in")

    grads = [gc.reshape(c_ctx.shape), gnw, gwa[None], gb, g_w_in_t, gr.reshape(ret_log2_decay.shape), gq, gk,
             g_w_o_ret[None], g_w_o_att[None], g_w_out[None]]
    weights = [c_ctx, norm_w, w_ada, b_ada, w_in_t, ret_log2_decay, q_norm_w, k_norm_w, w_o_ret, w_o_att, w_out]
    ms = [m_c_ctx, m_norm_w, m_w_ada, m_b_ada, jnp.transpose(m_w_in[0]), m_ret_log2_decay, m_q_norm_w, m_k_norm_w,
          m_w_o_ret, m_w_o_att, m_w_out]
    vs = [v_c_ctx, v_norm_w, v_w_ada, v_b_ada, jnp.transpose(v_w_in[0]), v_ret_log2_decay, v_q_norm_w, v_k_norm_w,
          v_w_o_ret, v_w_o_att, v_w_out]
    deltas, new_ms, new_vs = [], [], []
    for i, (w, g, m, v) in enumerate(zip(weights, grads, ms, vs)):
        shape2 = (-1, w.shape[-1])
        res = _adamw(w.reshape(shape2), g.reshape(shape2), m.reshape(shape2), v.reshape(shape2), "adamw_%d" % i)
        for lst, r in zip((deltas, new_ms, new_vs), res):
            lst.append(jnp.transpose(r)[None] if i == 4 else r.reshape(w.shape))
    grads[4] = jnp.transpose(g_w_in_t)[None]
    return (loss, grad_x, *grads, *deltas, *new_ms, *new_vs)
```

```python
import numpy as np
import jax
import jax.numpy as jnp
from jax import lax
from jax.experimental import pallas as pl
from jax.experimental.pallas import tpu as pltpu

F32 = jnp.float32
BF16 = jnp.bfloat16

D = 1024
RH, DK, DV, CH = 4, 256, 512, 256
HQ, HKV, HD = 8, 2, 128
GRID_W = 64
ROPE_THETA = 10000.0
EPS = 1e-6
RK, RV, AK, AV, RQ, RG, AQ, AG, MR, MA = 0, 1024, 3072, 3328, 3584, 4608, 6656, 7680, 8704, 9728
IN_COLS = 10752
KV_COLS = 3584
N_DEV = 8
LR, B1, B2, ADAM_EPS, WD, STEP = 0.001, 0.9, 0.999, 1e-08, 0.01, 10
PAY_ROWS = 16
VMEM_LIMIT = 56 * 1024 * 1024
MESH_T = pl.DeviceIdType.MESH

NT = (((1,), (1,)), ((), ()))
TN = (((0,), (0,)), ((), ()))
SM_C = (HD ** -0.5) * float(np.log2(np.e))


def _params(sem):
    return pltpu.CompilerParams(dimension_semantics=sem, vmem_limit_bytes=VMEM_LIMIT)


def _pick(n, target, mult=8):
    best = None
    for t in range(mult, min(n, target) + 1, mult):
        if n % t == 0:
            best = t
    return best or n


def _dot(a, b, dn=None):
    if dn is None:
        return jnp.dot(a, b, preferred_element_type=F32)
    return lax.dot_general(a, b, dn, preferred_element_type=F32)


def _sig(v):
    return jax.nn.sigmoid(v)


def _silu(v):
    return v * _sig(v)


def _dsilu(v):
    s = _sig(v)
    return s * (1.0 + v * (1.0 - s))


def _sds(shape, dtype):
    return jax.ShapeDtypeStruct(shape, dtype)


def _matmul(a, b, *, ta=False, tb=False, tm, tn, tk, out_dtype, name, after=()):
    m = a.shape[1] if ta else a.shape[0]
    kdim = a.shape[0] if ta else a.shape[1]
    n = b.shape[0] if tb else b.shape[1]
    tm, tn, tk = _pick(m, tm, 128), _pick(n, tn, 128), _pick(kdim, tk, 128)
    nk = kdim // tk
    dn = (((0 if ta else 1,), (1 if tb else 0,)), ((), ()))

    def body(a_ref, b_ref, *rest):
        o_ref, acc_ref = rest[-2:]
        k = pl.program_id(2)
        part = _dot(a_ref[...].astype(BF16), b_ref[...].astype(BF16), dn)
        if nk == 1:
            o_ref[...] = part.astype(o_ref.dtype)
        else:
            @pl.when(k == 0)
            def _():
                acc_ref[...] = part

            @pl.when(k > 0)
            def _():
                acc_ref[...] += part

            @pl.when(k == nk - 1)
            def _():
                o_ref[...] = acc_ref[...].astype(o_ref.dtype)

    a_spec = pl.BlockSpec((tk, tm), lambda i, j, k: (k, i)) if ta else pl.BlockSpec((tm, tk), lambda i, j, k: (i, k))
    b_spec = pl.BlockSpec((tn, tk), lambda i, j, k: (j, k)) if tb else pl.BlockSpec((tk, tn), lambda i, j, k: (k, j))
    return pl.pallas_call(
        body, name=name, grid=(m // tm, n // tn, nk),
        in_specs=[a_spec, b_spec] + [pl.BlockSpec(memory_space=pl.ANY)] * len(after),
        out_specs=pl.BlockSpec((tm, tn), lambda i, j, k: (i, j)), out_shape=_sds((m, n), out_dtype),
        scratch_shapes=[pltpu.VMEM((tm, tn) if nk > 1 else (8, 128), F32)],
        compiler_params=_params(("parallel", "parallel", "arbitrary")),
    )(a, b, *after)


def _log_gamma(r):
    rp = jnp.full((8, 128), -1.0, F32).at[:2, :RH].set(r.reshape(2, RH))

    def body(r_ref, o_ref):
        o_ref[...] = jnp.log1p(-jnp.exp2(r_ref[...]))

    out = pl.pallas_call(body, name="log_gamma", out_shape=_sds((8, 128), F32))(rp)
    return out[:2, :RH]


def _mod_fwd(c8, w_ada16, b_ada):
    def body(c_ref, w_ref, b_ref, o_ref):
        o_ref[...] = _dot(_silu(c_ref[...]).astype(BF16), w_ref[...]) + b_ref[...]

    return pl.pallas_call(
        body, name="mod_fwd", grid=(3,),
        in_specs=[pl.BlockSpec((8, D), lambda j: (0, 0)), pl.BlockSpec((D, D), lambda j: (0, j)),
                  pl.BlockSpec((1, D), lambda j: (0, j))],
        out_specs=pl.BlockSpec((8, D), lambda j: (0, j)), out_shape=_sds((8, 3 * D), F32),
        compiler_params=_params(("arbitrary",)),
    )(c8, w_ada16, b_ada)


def _norm_fwd(x2, mod3, norm_w, rows_all, row_off, rows_per_group, group0, h_prev, tm, name):
    rows = x2.shape[0]
    rb0 = row_off // tm
    bpg = rows_per_group // tm

    def body(*refs):
        x_ref, sh_ref, sc_ref, nw_ref, o_ref = refs[-5:]
        xv = x_ref[...]
        r = lax.rsqrt(jnp.mean(xv * xv, axis=-1, keepdims=True) + EPS)
        o_ref[...] = ((xv * r) * nw_ref[...] * (1.0 + sc_ref[...]) + sh_ref[...]).astype(BF16)

    in_specs = [pl.BlockSpec((tm, D), lambda i: (i, 0)),
                pl.BlockSpec((None, 1, D), lambda i: (group0 + i // bpg, 0, 0)),
                pl.BlockSpec((None, 1, D), lambda i: (group0 + i // bpg, 0, 1)),
                pl.BlockSpec((1, D), lambda i: (0, 0))]
    args = [x2, mod3, mod3, norm_w]
    alias = {}
    if h_prev is not None:
        in_specs.insert(0, pl.BlockSpec(memory_space=pl.ANY))
        args.insert(0, h_prev)
        alias = {0: 0}
    return pl.pallas_call(
        body, name=name, grid=(rows // tm,), in_specs=in_specs,
        out_specs=pl.BlockSpec((tm, D), lambda i: (rb0 + i, 0)), out_shape=_sds((rows_all, D), BF16),
        input_output_aliases=alias, compiler_params=_params(("parallel",)),
    )(*args)


def _decays(lg, fwd):
    ii = lax.broadcasted_iota(jnp.int32, (CH, CH), 0)
    jj = lax.broadcasted_iota(jnp.int32, (CH, CH), 1)
    ri = lax.broadcasted_iota(jnp.int32, (CH, 1), 0).astype(F32)
    rel = (ii - jj) if fwd else (jj - ii)
    relf = jnp.maximum(rel, 0).astype(F32)
    mask = jnp.where(rel >= 0, jnp.exp(lg * relf), 0.0)
    qe = (ri + 1.0) if fwd else (CH - ri)
    ke = (CH - 1.0 - ri) if fwd else ri
    return mask, relf, jnp.exp(lg * qe), qe, jnp.exp(lg * ke), ke


def _wide_specs(rowf):
    return [pl.BlockSpec((CH, 2 * DK), lambda b, c: (rowf(b, c), RQ // (2 * DK))),
            pl.BlockSpec((CH, 2 * DK), lambda b, c: (rowf(b, c), RQ // (2 * DK) + 1)),
            pl.BlockSpec((CH, RH * DK), lambda b, c: (rowf(b, c), RK // (RH * DK))),
            pl.BlockSpec((CH, 2 * DV), lambda b, c: (rowf(b, c), RV // (2 * DV))),
            pl.BlockSpec((CH, 2 * DV), lambda b, c: (rowf(b, c), RV // (2 * DV) + 1))]


def _head_qkv(refs, h):
    q0, q1, k, v0, v1 = refs
    lo = h % 2
    q = (q0, q1)[h // 2][:, lo * DK:(lo + 1) * DK].astype(F32)
    kk = k[:, h * DK:(h + 1) * DK].astype(F32) * (DK ** -0.5)
    v16 = (v0, v1)[h // 2][:, lo * DV:(lo + 1) * DV].astype(BF16)
    return q, kk, v16


def _ctx_state(px, lg, nb, t_rows, cx):
    rb = t_rows // cx

    def body(lg_ref, k_ref, v_ref, sf_ref, sb_ref):
        h = pl.program_id(1)
        pos = lax.broadcasted_iota(jnp.int32, (cx, 1), 0).astype(F32)
        k = k_ref[...].astype(F32) * (DK ** -0.5)
        v16 = v_ref[...].astype(BF16)
        wf = jnp.exp(lg_ref[0, h] * (cx - 1.0 - pos))
        wb = jnp.exp(lg_ref[1, h] * pos)
        sf_ref[...] = _dot((k * wf).astype(BF16), v16, TN)
        sb_ref[...] = _dot((k * wb).astype(BF16), v16, TN)

    st = pl.BlockSpec((None, None, DK, DV), lambda b, h: (b, h, 0, 0))
    return pl.pallas_call(
        body, name="ctx_state", grid=(nb, RH),
        in_specs=[pl.BlockSpec(memory_space=pltpu.SMEM),
                  pl.BlockSpec((cx, DK), lambda b, h: (rb + b, RK // DK + h)),
                  pl.BlockSpec((cx, DV), lambda b, h: (rb + b, RV // DV + h))],
        out_specs=[st, st], out_shape=[_sds((nb, RH, DK, DV), F32)] * 2,
        compiler_params=_params(("parallel", "parallel")),
    )(lg, px, px)


def _ret_fwd(px, lg, s0f, s0b, nb, nc):
    t_rows = nb * nc * CH

    def body(lg_ref, *refs):
        ins = (refs[0:5], refs[5:10])
        s0f_ref, s0b_ref, of_ref, ob_ref, hf_ref, hb_ref, sf, sb = refs[10:]
        c = pl.program_id(1)

        @pl.when(c == 0)
        def _():
            sf[...] = s0f_ref[...]
            sb[...] = s0b_ref[...]

        for d, (o_ref, h_ref, s) in enumerate(((of_ref, hf_ref, sf), (ob_ref, hb_ref, sb))):
            for h in range(RH):
                lg_d = lg_ref[d, h]
                mask, _, qd, _, kd, _ = _decays(lg_d, d == 0)
                q, k, v16 = _head_qkv(ins[d], h)
                a = _dot(q.astype(BF16), k.astype(BF16), NT)
                st = s[h]
                st16 = st.astype(BF16)
                h_ref[h] = st16
                o = _dot((a * mask).astype(BF16), v16) + _dot((q * qd).astype(BF16), st16)
                o_ref[:, h * DV:(h + 1) * DV] = o.astype(BF16)
                s[h] = st * jnp.exp(lg_d * CH) + _dot((k * kd).astype(BF16), v16, TN)

    def fw(b, c):
        return b * nc + c

    def bw(b, c):
        return b * nc + nc - 1 - c

    st = pl.BlockSpec((None, RH, DK, DV), lambda b, c: (b, 0, 0, 0))
    in_specs = [pl.BlockSpec(memory_space=pltpu.SMEM)] + _wide_specs(fw) + _wide_specs(bw) + [st, st]
    out_specs = [pl.BlockSpec((CH, RH * DV), lambda b, c: (fw(b, c), 0)),
                 pl.BlockSpec((CH, RH * DV), lambda b, c: (bw(b, c), 0)),
                 pl.BlockSpec((None, None, RH, DK, DV), lambda b, c: (b, c, 0, 0, 0)),
                 pl.BlockSpec((None, None, RH, DK, DV), lambda b, c: (b, nc - 1 - c, 0, 0, 0))]
    return pl.pallas_call(
        body, name="ret_fwd", grid=(nb, nc), in_specs=in_specs, out_specs=out_specs,
        out_shape=[_sds((t_rows, RH * DV), BF16)] * 2 + [_sds((nb, nc, RH, DK, DV), BF16)] * 2,
        scratch_shapes=[pltpu.VMEM((RH, DK, DV), F32), pltpu.VMEM((RH, DK, DV), F32)],
        compiler_params=_params(("parallel", "arbitrary")),
    )(lg, *([px] * 10), s0f, s0b)


def _ret_post(o_f, o_b, px, tm):
    t_rows = o_f.shape[0]

    def body(of_ref, ob_ref, g0, g1, g2, g3, y_ref):
        for h, g_ref in enumerate((g0, g1, g2, g3)):
            sl = slice(h * DV, (h + 1) * DV)
            o = of_ref[:, sl].astype(F32) + ob_ref[:, sl].astype(F32)
            r = lax.rsqrt(jnp.mean(o * o, axis=-1, keepdims=True) + EPS)
            y_ref[:, sl] = ((o * r) * _silu(g_ref[...].astype(F32))).astype(BF16)

    def gate(h):
        return pl.BlockSpec((tm, DV), lambda i: (i, RG // DV + h))

    wide = pl.BlockSpec((tm, RH * DV), lambda i: (i, 0))
    return pl.pallas_call(
        body, name="ret_post", grid=(t_rows // tm,),
        in_specs=[wide, wide] + [gate(h) for h in range(RH)],
        out_specs=wide, out_shape=_sds((t_rows, RH * DV), BF16),
        compiler_params=_params(("parallel",)),
    )(o_f, o_b, *([px] * RH))


def _rope_tables(seq):
    rows = seq // GRID_W
    row = np.repeat(np.arange(rows, dtype=np.float32), GRID_W)
    col = np.tile(np.arange(GRID_W, dtype=np.float32), rows)
    half = HD // 2
    freqs = (ROPE_THETA ** (-np.arange(0, half, 2, dtype=np.float32) / half)).astype(np.float32)
    ang = np.concatenate([row[:, None] * freqs, col[:, None] * freqs], axis=-1).astype(np.float32)
    cos = np.repeat(np.cos(ang), 2, axis=-1).astype(np.float32)
    sin = np.repeat(np.sin(ang), 2, axis=-1).astype(np.float32)
    sign = np.tile(np.array([-1.0, 1.0], np.float32), HD // 2)
    return jnp.asarray(cos), jnp.asarray(sin * sign)


def _swap_pairs(v):
    lane = lax.broadcasted_iota(jnp.int32, v.shape, 1)
    return jnp.where((lane & 1) == 0, pltpu.roll(v, HD - 1, 1), pltpu.roll(v, 1, 1))


def _qk_prep(px, nw, cos, sin, rows, row_off, col_off, heads, hb, seq, tm, name):
    rope = cos is not None
    rb0 = row_off // tm
    pb = seq // tm if rope else 1
    bw = hb * HD

    def body(*refs):
        if rope:
            x_ref, w_ref, c_ref, s_ref, o_ref = refs
        else:
            x_ref, w_ref, o_ref = refs
        for h in range(hb):
            sl = slice(h * HD, (h + 1) * HD)
            xv = x_ref[:, sl].astype(F32)
            r = lax.rsqrt(jnp.mean(xv * xv, axis=-1, keepdims=True) + EPS)
            t = (xv * r) * w_ref[...]
            if rope:
                t = t * c_ref[...] + _swap_pairs(t) * s_ref[...]
            o_ref[:, sl] = t.astype(BF16)

    in_specs = [pl.BlockSpec((tm, bw), lambda i, j: (rb0 + i, col_off // bw + j)),
                pl.BlockSpec((1, HD), lambda i, j: (0, 0))]
    args = [px, nw]
    if rope:
        in_specs += [pl.BlockSpec((tm, HD), lambda i, j: (i % pb, 0))] * 2
        args += [cos, sin]
    return pl.pallas_call(
        body, name=name, grid=(rows // tm, heads // hb), in_specs=in_specs,
        out_specs=pl.BlockSpec((tm, bw), lambda i, j: (i, j)), out_shape=_sds((rows, heads * HD), BF16),
        compiler_params=_params(("parallel", "parallel")),
    )(*args)


def _att_fwd(q16, kx16, kc16, px, nb, seq, cx, tq):
    t_rows = nb * seq
    nq = seq // tq
    rep = HQ // HKV
    gw = rep * HD

    def body(q_ref, kx_ref, kc_ref, vx_ref, vc_ref, g_ref, o_ref, y_ref, l_ref):
        kx = kx_ref[...]
        kc = kc_ref[...]
        vx = vx_ref[...].astype(BF16)
        vc = vc_ref[...].astype(BF16)
        l_ref[...] = jnp.zeros_like(l_ref)
        for r in range(rep):
            sl = slice(r * HD, (r + 1) * HD)
            q = q_ref[:, sl]
            s1 = _dot(q, kx, NT)
            s2 = _dot(q, kc, NT)
            m = jnp.maximum(jnp.max(s1, axis=-1, keepdims=True), jnp.max(s2, axis=-1, keepdims=True))
            e1 = jnp.exp2((s1 - m) * SM_C)
            e2 = jnp.exp2((s2 - m) * SM_C)
            tot = jnp.sum(e1, axis=-1, keepdims=True) + jnp.sum(e2, axis=-1, keepdims=True)
            o = (_dot(e1.astype(BF16), vx) + _dot(e2.astype(BF16), vc)) * (1.0 / tot)
            o_ref[:, sl] = o
            y_ref[:, sl] = (o * _silu(g_ref[:, sl].astype(F32))).astype(BF16)
            l_ref[:, r:r + 1] = m * SM_C + jnp.log(tot) * float(np.log2(np.e))

    qblk = pl.BlockSpec((tq, gw), lambda b, g, i: (b * nq + i, g))
    return pl.pallas_call(
        body, name="att_fwd", grid=(nb, HKV, nq),
        in_specs=[qblk,
                  pl.BlockSpec((seq, HD), lambda b, g, i: (b, g)),
                  pl.BlockSpec((cx, HD), lambda b, g, i: (b, g)),
                  pl.BlockSpec((seq, HD), lambda b, g, i: (b, AV // HD + g)),
                  pl.BlockSpec((cx, HD), lambda b, g, i: (t_rows // cx + b, AV // HD + g)),
                  pl.BlockSpec((tq, gw), lambda b, g, i: (b * nq + i, AG // gw + g))],
        out_specs=[qblk, qblk, pl.BlockSpec((tq, 128), lambda b, g, i: (b * nq + i, g))],
        out_shape=[_sds((t_rows, D), F32), _sds((t_rows, D), BF16), _sds((t_rows, HKV * 128), F32)],
        compiler_params=_params(("parallel", "parallel", "parallel")),
    )(q16, kx16, kc16, px, px, px)


def _gate_specs(tm, col0):
    hw = D // 2
    return [pl.BlockSpec((tm, hw), lambda i: (i, col0 // hw)), pl.BlockSpec((tm, hw), lambda i: (i, col0 // hw + 1))]


def _merge(yret16, yatt16, px, w_o_ret16, w_o_att16, tm):
    t_rows = yret16.shape[0]
    hw = D // 2

    def body(yr_ref, wr_ref, ya_ref, wa_ref, mr0, mr1, ma0, ma1, ar_ref, aa_ref, y_ref):
        ar = _dot(yr_ref[...], wr_ref[...])
        aa = _dot(ya_ref[...], wa_ref[...])
        ar_ref[...] = ar.astype(BF16)
        aa_ref[...] = aa.astype(BF16)
        for j, (mr_ref, ma_ref) in enumerate(((mr0, ma0), (mr1, ma1))):
            sl = slice(j * hw, (j + 1) * hw)
            y_ref[:, sl] = (_sig(mr_ref[...].astype(F32)) * ar[:, sl]
                            + _sig(ma_ref[...].astype(F32)) * aa[:, sl]).astype(BF16)

    row = pl.BlockSpec((tm, D), lambda i: (i, 0))
    return pl.pallas_call(
        body, name="merge", grid=(t_rows // tm,),
        in_specs=[pl.BlockSpec((tm, RH * DV), lambda i: (i, 0)), pl.BlockSpec((RH * DV, D), lambda i: (0, 0)),
                  row, pl.BlockSpec((D, D), lambda i: (0, 0))] + _gate_specs(tm, MR) + _gate_specs(tm, MA),
        out_specs=[row, row, row], out_shape=[_sds((t_rows, D), BF16)] * 3,
        compiler_params=_params(("parallel",)),
    )(yret16, w_o_ret16, yatt16, w_o_att16, px, px, px, px)


def _outproj(y16, w_out16, x2, tgt, mod3, nb, seq, tm):
    t_rows = nb * seq
    bpb = seq // tm

    def body(y_ref, w_ref, x_ref, t_ref, g_ref, dxn_ref, dout_ref, dg_ref, loss_ref):
        i = pl.program_id(1)
        out = _dot(y_ref[...], w_ref[...])
        gate = g_ref[...]
        diff = x_ref[...] + gate * out - t_ref[...]
        dxn = diff * (1.0 / D)
        dxn_ref[...] = dxn
        dout_ref[...] = (gate * dxn).astype(BF16)
        dg = jnp.sum(dxn * out, axis=0, keepdims=True)
        ls = jnp.broadcast_to(jnp.sum(diff * diff) * (0.5 / D), (1, 128))

        @pl.when(i == 0)
        def _():
            dg_ref[...] = dg
            loss_ref[...] = ls

        @pl.when(i > 0)
        def _():
            dg_ref[...] += dg
            loss_ref[...] += ls

    row = pl.BlockSpec((tm, D), lambda b, i: (b * bpb + i, 0))
    return pl.pallas_call(
        body, name="outproj", grid=(nb, bpb),
        in_specs=[row, pl.BlockSpec((D, D), lambda b, i: (0, 0)), row, row,
                  pl.BlockSpec((None, 1, D), lambda b, i: (b, 0, 2))],
        out_specs=[row, row, pl.BlockSpec((None, 1, D), lambda b, i: (b, 0, 0)),
                   pl.BlockSpec((None, 1, 128), lambda b, i: (b, 0, 0))],
        out_shape=[_sds((t_rows, D), F32), _sds((t_rows, D), BF16), _sds((nb, 1, D), F32), _sds((nb, 1, 128), F32)],
        compiler_params=_params(("parallel", "arbitrary")),
    )(y16, w_out16, x2, tgt, mod3)


def _bwd_merge(dout16, w_out16, px, a_ret, a_att, tm):
    t_rows = dout16.shape[0]
    hw = D // 2

    def body(do_ref, w_ref, mr0, mr1, ma0, ma1, ar_ref, aa_ref, dar_ref, daa_ref, dmr_ref, dma_ref):
        dy_all = _dot(do_ref[...], w_ref[...], NT)
        for j, (mr_ref, ma_ref) in enumerate(((mr0, ma0), (mr1, ma1))):
            sl = slice(j * hw, (j + 1) * hw)
            dy = dy_all[:, sl]
            sr = _sig(mr_ref[...].astype(F32))
            sa = _sig(ma_ref[...].astype(F32))
            dar_ref[:, sl] = (dy * sr).astype(BF16)
            daa_ref[:, sl] = (dy * sa).astype(BF16)
            dmr_ref[:, sl] = (dy * ar_ref[:, sl].astype(F32) * sr * (1.0 - sr)).astype(BF16)
            dma_ref[:, sl] = (dy * aa_ref[:, sl].astype(F32) * sa * (1.0 - sa)).astype(BF16)

    row = pl.BlockSpec((tm, D), lambda i: (i, 0))
    return pl.pallas_call(
        body, name="bwd_merge", grid=(t_rows // tm,),
        in_specs=[row, pl.BlockSpec((D, D), lambda i: (0, 0))] + _gate_specs(tm, MR) + _gate_specs(tm, MA) + [row, row],
        out_specs=[row] * 4, out_shape=[_sds((t_rows, D), BF16)] * 4,
        compiler_params=_params(("parallel",)),
    )(dout16, w_out16, px, px, px, px, a_ret, a_att)


def _bwd_branch_ret(da_ret16, w_o_ret16, px, o_f, o_b, tm, after=()):
    t_rows = da_ret16.shape[0]

    def body(da_ref, w_ref, g0, g1, g2, g3, of_ref, ob_ref, *rest):
        do_ref, dg_ref = rest[-2:]
        da = da_ref[...]
        for h, g_ref in enumerate((g0, g1, g2, g3)):
            sl = slice(h * DV, (h + 1) * DV)
            dy = _dot(da, w_ref[sl, :], NT)
            g = g_ref[...].astype(F32)
            o = of_ref[:, sl].astype(F32) + ob_ref[:, sl].astype(F32)
            r = lax.rsqrt(jnp.mean(o * o, axis=-1, keepdims=True) + EPS)
            on = o * r
            don = dy * _silu(g)
            dg_ref[:, sl] = (dy * on * _dsilu(g)).astype(BF16)
            do_ref[:, sl] = (r * (don - on * jnp.mean(on * don, axis=-1, keepdims=True))).astype(BF16)

    def gate(h):
        return pl.BlockSpec((tm, DV), lambda i: (i, RG // DV + h))

    wide = pl.BlockSpec((tm, RH * DV), lambda i: (i, 0))
    return pl.pallas_call(
        body, name="bwd_branch_ret", grid=(t_rows // tm,),
        in_specs=[pl.BlockSpec((tm, D), lambda i: (i, 0)), pl.BlockSpec((RH * DV, D), lambda i: (0, 0))]
        + [gate(h) for h in range(RH)] + [wide, wide] + [pl.BlockSpec(memory_space=pl.ANY)] * len(after),
        out_specs=[wide, wide], out_shape=[_sds((t_rows, RH * DV), BF16)] * 2,
        compiler_params=_params(("parallel",)),
    )(da_ret16, w_o_ret16, *([px] * RH), o_f, o_b, *after)


def _bwd_branch_att(da_att16, w_o_att16, px, o_att, tm):
    t_rows = da_att16.shape[0]
    hw = D // 2

    def body(da_ref, w_ref, g0, g1, o_ref, dao_ref, dg_ref):
        dy_all = _dot(da_ref[...], w_ref[...], NT)
        for j, g_ref in enumerate((g0, g1)):
            sl = slice(j * hw, (j + 1) * hw)
            dy = dy_all[:, sl]
            g = g_ref[...].astype(F32)
            dao_ref[:, sl] = dy * _silu(g)
            dg_ref[:, sl] = (dy * o_ref[:, sl] * _dsilu(g)).astype(BF16)

    row = pl.BlockSpec((tm, D), lambda i: (i, 0))
    return pl.pallas_call(
        body, name="bwd_branch_att", grid=(t_rows // tm,),
        in_specs=[row, pl.BlockSpec((D, D), lambda i: (0, 0))] + _gate_specs(tm, AG) + [row],
        out_specs=[row, row], out_shape=[_sds((t_rows, D), F32), _sds((t_rows, D), BF16)],
        compiler_params=_params(("parallel",)),
    )(da_att16, w_o_att16, px, px, o_att)


def _att_bwd(q16, kx16, kc16, px, dao, o_att, lse, nb, seq, cx, tq):
    t_rows = nb * seq
    nq = seq // tq
    rep = HQ // HKV
    gw = rep * HD
    scale = HD ** -0.5

    def body(q_ref, kx_ref, kc_ref, vx_ref, vc_ref, dao_ref, o_ref, l_ref, dq_ref, dkx_ref, dvx_ref, dkc_ref, dvc_ref):
        i = pl.program_id(2)
        kx = kx_ref[...]
        kc = kc_ref[...]
        vx = vx_ref[...].astype(BF16)
        vc = vc_ref[...].astype(BF16)
        dkx = jnp.zeros((seq, HD), F32)
        dvx = jnp.zeros((seq, HD), F32)
        dkc = jnp.zeros((cx, HD), F32)
        dvc = jnp.zeros((cx, HD), F32)
        for r in range(rep):
            sl = slice(r * HD, (r + 1) * HD)
            q = q_ref[:, sl]
            lr = l_ref[:, r:r + 1]
            p1 = jnp.exp2(_dot(q, kx, NT) * SM_C - lr)
            p2 = jnp.exp2(_dot(q, kc, NT) * SM_C - lr)
            da = dao_ref[:, sl]
            da16 = da.astype(BF16)
            delta = jnp.sum(da * o_ref[:, sl], axis=-1, keepdims=True)
            ds1 = (p1 * (_dot(da16, vx, NT) - delta)).astype(BF16)
            ds2 = (p2 * (_dot(da16, vc, NT) - delta)).astype(BF16)
            dq_ref[:, sl] = (_dot(ds1, kx) + _dot(ds2, kc)) * scale
            dkx += _dot(ds1, q, TN)
            dkc += _dot(ds2, q, TN)
            dvx += _dot(p1.astype(BF16), da16, TN)
            dvc += _dot(p2.astype(BF16), da16, TN)
        dkx = dkx * scale
        dkc = dkc * scale

        @pl.when(i == 0)
        def _():
            dkx_ref[...] = dkx
            dvx_ref[...] = dvx
            dkc_ref[...] = dkc
            dvc_ref[...] = dvc

        @pl.when(i > 0)
        def _():
            dkx_ref[...] += dkx
            dvx_ref[...] += dvx
            dkc_ref[...] += dkc
            dvc_ref[...] += dvc

    qblk = pl.BlockSpec((tq, gw), lambda b, g, i: (b * nq + i, g))
    kxb = pl.BlockSpec((None, seq, HD), lambda b, g, i: (b, 0, g))
    kcb = pl.BlockSpec((None, cx, HD), lambda b, g, i: (b, 0, g))
    return pl.pallas_call(
        body, name="att_bwd", grid=(nb, HKV, nq),
        in_specs=[qblk,
                  pl.BlockSpec((seq, HD), lambda b, g, i: (b, g)),
                  pl.BlockSpec((cx, HD), lambda b, g, i: (b, g)),
                  pl.BlockSpec((seq, HD), lambda b, g, i: (b, AV // HD + g)),
                  pl.BlockSpec((cx, HD), lambda b, g, i: (t_rows // cx + b, AV // HD + g)),
                  qblk, qblk, pl.BlockSpec((tq, 128), lambda b, g, i: (b * nq + i, g))],
        out_specs=[qblk, kxb, kxb, kcb, kcb],
        out_shape=[_sds((t_rows, D), F32), _sds((nb, seq, HKV * HD), F32), _sds((nb, seq, HKV * HD), F32),
                   _sds((nb, cx, HKV * HD), F32), _sds((nb, cx, HKV * HD), F32)],
        compiler_params=_params(("parallel", "parallel", "arbitrary")),
    )(q16, kx16, kc16, px, px, dao, o_att, lse)


def _qk_prep_bwd(dt, px, nw, cos, sin, rows, row_off, col_off, heads, hb, seq, tm, name):
    rope = cos is not None
    rb0 = row_off // tm
    pb = seq // tm if rope else 1
    bw = hb * HD

    def body(*refs):
        if rope:
            d_ref, x_ref, w_ref, c_ref, s_ref, dx_ref, dw_ref = refs
        else:
            d_ref, x_ref, w_ref, dx_ref, dw_ref = refs
        first = jnp.logical_and(pl.program_id(0) == 0, pl.program_id(1) == 0)
        dw = jnp.zeros((1, HD), F32)
        for h in range(hb):
            sl = slice(h * HD, (h + 1) * HD)
            dtv = d_ref[:, sl]
            if rope:
                dtv = dtv * c_ref[...] + _swap_pairs(dtv * s_ref[...])
            xv = x_ref[:, sl].astype(F32)
            r = lax.rsqrt(jnp.mean(xv * xv, axis=-1, keepdims=True) + EPS)
            xh = xv * r
            dxh = dtv * w_ref[...]
            dx_ref[:, sl] = (r * (dxh - xh * jnp.mean(dxh * xh, axis=-1, keepdims=True))).astype(BF16)
            dw += jnp.sum(dtv * xh, axis=0, keepdims=True)

        @pl.when(first)
        def _():
            dw_ref[...] = dw

        @pl.when(jnp.logical_not(first))
        def _():
            dw_ref[...] += dw

    blk = pl.BlockSpec((tm, bw), lambda i, j: (i, j))
    in_specs = [blk, pl.BlockSpec((tm, bw), lambda i, j: (rb0 + i, col_off // bw + j)),
                pl.BlockSpec((1, HD), lambda i, j: (0, 0))]
    args = [dt, px, nw]
    if rope:
        in_specs += [pl.BlockSpec((tm, HD), lambda i, j: (i % pb, 0))] * 2
        args += [cos, sin]
    return pl.pallas_call(
        body, name=name, grid=(rows // tm, heads // hb), in_specs=in_specs,
        out_specs=[blk, pl.BlockSpec((1, HD), lambda i, j: (0, 0))],
        out_shape=[_sds((rows, heads * HD), BF16), _sds((1, HD), F32)],
        compiler_params=_params(("arbitrary", "arbitrary")),
    )(*args)


def _ret_bwd(px, lg, do16, hist_f, hist_b, nb, nc):
    t_rows = nb * nc * CH

    def body(lg_ref, *refs):
        ins = (refs[0:5], refs[7:12])
        do_refs = (refs[5], refs[12])
        h_refs = (refs[6], refs[13])
        outs = (refs[14:17], refs[17:20])
        ds_outs = (refs[20], refs[21])
        dlg_ref = refs[22]
        dss = (refs[23], refs[24])
        c = pl.program_id(1)

        @pl.when(c == 0)
        def _():
            dss[0][...] = jnp.zeros_like(dss[0])
            dss[1][...] = jnp.zeros_like(dss[1])
            dlg_ref[...] = jnp.zeros_like(dlg_ref)

        for d in range(2):
            dq_ref, dk_ref, dv_ref = outs[d]
            for h in range(RH):
                lg_d = lg_ref[d, h]
                mask, relf, qd, qe, kd, ke = _decays(lg_d, d == 0)
                g_ch = jnp.exp(lg_d * CH)
                q, k, v16 = _head_qkv(ins[d], h)
                q16 = q.astype(BF16)
                k16 = k.astype(BF16)
                do16v = do_refs[d][:, h * DV:(h + 1) * DV]
                st16 = h_refs[d][h]
                dst = dss[d][h]
                dst16 = dst.astype(BF16)
                a = _dot(q16, k16, NT) * mask
                dp = _dot(do16v, v16, NT)
                da16 = (dp * mask).astype(BF16)
                dq_cross = _dot(do16v, st16, NT) * qd
                dq_ref[:, h * DK:(h + 1) * DK] = (_dot(da16, k16) + dq_cross).astype(BF16)
                dk_state = _dot(v16, dst16, NT) * kd
                dk_ref[:, h * DK:(h + 1) * DK] = ((_dot(da16, q16, TN) + dk_state) * (DK ** -0.5)).astype(BF16)
                dv = _dot(a.astype(BF16), do16v, TN) + _dot((k * kd).astype(BF16), dst16)
                dv_ref[:, h * DV:(h + 1) * DV] = dv.astype(BF16)
                dlg = (jnp.sum(relf * a * dp)
                       + jnp.sum(qe * jnp.sum(q * dq_cross, axis=-1, keepdims=True))
                       + jnp.sum(ke * jnp.sum(k * dk_state, axis=-1, keepdims=True))
                       + CH * g_ch * jnp.sum(dst * st16.astype(F32)))
                row = d * RH + h
                dlg_ref[row:row + 1, :] += jnp.broadcast_to(dlg, (1, 128))
                ds_new = g_ch * dst + _dot((q * qd).astype(BF16), do16v, TN)
                dss[d][h] = ds_new

                @pl.when(c == nc - 1)
                def _():
                    ds_outs[d][h] = ds_new

    def fw(b, c):
        return b * nc + nc - 1 - c

    def bw(b, c):
        return b * nc + c

    def rows(rowf, width):
        return pl.BlockSpec((CH, width), lambda b, c: (rowf(b, c), 0))

    def hist(rowf):
        return pl.BlockSpec((None, None, RH, DK, DV), lambda b, c: (b, rowf(0, c), 0, 0, 0))

    st = pl.BlockSpec((None, RH, DK, DV), lambda b, c: (b, 0, 0, 0))
    in_specs = [pl.BlockSpec(memory_space=pltpu.SMEM)]
    out_specs = []
    for rowf in (fw, bw):
        in_specs += _wide_specs(rowf) + [rows(rowf, RH * DV), hist(rowf)]
        out_specs += [rows(rowf, RH * DK), rows(rowf, RH * DK), rows(rowf, RH * DV)]
    out_specs += [st, st, pl.BlockSpec((None, 8, 128), lambda b, c: (b, 0, 0))]
    qk = _sds((t_rows, RH * DK), BF16)
    vv = _sds((t_rows, RH * DV), BF16)
    return pl.pallas_call(
        body, name="ret_bwd", grid=(nb, nc), in_specs=in_specs, out_specs=out_specs,
        out_shape=[qk, qk, vv, qk, qk, vv, _sds((nb, RH, DK, DV), F32), _sds((nb, RH, DK, DV), F32),
                   _sds((nb, 8, 128), F32)],
        scratch_shapes=[pltpu.VMEM((RH, DK, DV), F32), pltpu.VMEM((RH, DK, DV), F32)],
        compiler_params=_params(("parallel", "arbitrary")),
    )(lg, *([px] * 5), do16, hist_f, *([px] * 5), do16, hist_b)


def _ctx_state_bwd(px, lg, ds_f, ds_b, nb, t_rows, cx):
    rb = t_rows // cx

    def body(lg_ref, k_ref, v_ref, dsf_ref, dsb_ref, dk_ref, dv_ref, dlg_ref):
        h = pl.program_id(1)
        pos = lax.broadcasted_iota(jnp.int32, (cx, 1), 0).astype(F32)
        k = k_ref[...].astype(F32) * (DK ** -0.5)
        v16 = v_ref[...].astype(BF16)
        dk = jnp.zeros((cx, DK), F32)
        dv = jnp.zeros((cx, DV), F32)
        dlg_ref[...] = jnp.zeros_like(dlg_ref)
        for d, (ds_ref, e) in enumerate(((dsf_ref, cx - 1.0 - pos), (dsb_ref, pos))):
            w = jnp.exp(lg_ref[d, h] * e)
            ds16 = ds_ref[...].astype(BF16)
            t = _dot(v16, ds16, NT)
            dk += t * w
            dv += _dot((k * w).astype(BF16), ds16)
            dlg = jnp.sum(e * w * jnp.sum(k * t, axis=-1, keepdims=True))
            dlg_ref[d:d + 1, :] = jnp.broadcast_to(dlg, (1, 128))
        dk_ref[...] = (dk * (DK ** -0.5)).astype(BF16)
        dv_ref[...] = dv.astype(BF16)

    st = pl.BlockSpec((None, None, DK, DV), lambda b, h: (b, h, 0, 0))
    return pl.pallas_call(
        body, name="ctx_state_bwd", grid=(nb, RH),
        in_specs=[pl.BlockSpec(memory_space=pltpu.SMEM),
                  pl.BlockSpec((cx, DK), lambda b, h: (rb + b, RK // DK + h)),
                  pl.BlockSpec((cx, DV), lambda b, h: (rb + b, RV // DV + h)), st, st],
        out_specs=[pl.BlockSpec((cx, DK), lambda b, h: (b, h)), pl.BlockSpec((cx, DV), lambda b, h: (b, h)),
                   pl.BlockSpec((None, None, 8, 128), lambda b, h: (b, h, 0, 0))],
        out_shape=[_sds((nb * cx, RH * DK), BF16), _sds((nb * cx, RH * DV), BF16), _sds((nb, RH, 8, 128), F32)],
        compiler_params=_params(("parallel", "parallel")),
    )(lg, px, px, ds_f, ds_b)


def _assemble_lat(rows_all, dk_f, dk_b, dv_f, dv_b, dak16, dvx, dq_f, dq_b, drg16, daq16, dag16, dmr16, dma16, tm):
    t_rows = dk_f.shape[0]

    def body(dkf, dkb, dvf, dvb, dak, dav, dqf, dqb, drg, daq, dag, dmr, dma, o_ref):
        o_ref[:, RK:RK + RH * DK] = (dkf[...].astype(F32) + dkb[...].astype(F32)).astype(BF16)
        o_ref[:, RV:RV + RH * DV] = (dvf[...].astype(F32) + dvb[...].astype(F32)).astype(BF16)
        o_ref[:, AK:AK + HKV * HD] = dak[...]
        o_ref[:, AV:AV + HKV * HD] = dav[...].astype(BF16)
        o_ref[:, RQ:RQ + RH * DK] = (dqf[...].astype(F32) + dqb[...].astype(F32)).astype(BF16)
        o_ref[:, RG:RG + RH * DV] = drg[...]
        o_ref[:, AQ:AQ + D] = daq[...]
        o_ref[:, AG:AG + D] = dag[...]
        o_ref[:, MR:MR + D] = dmr[...]
        o_ref[:, MA:MA + D] = dma[...]

    args = (dk_f, dk_b, dv_f, dv_b, dak16, dvx, dq_f, dq_b, drg16, daq16, dag16, dmr16, dma16)
    return pl.pallas_call(
        body, name="assemble_lat", grid=(t_rows // tm,),
        in_specs=[pl.BlockSpec((tm, a.shape[1]), lambda i: (i, 0)) for a in args],
        out_specs=pl.BlockSpec((tm, IN_COLS), lambda i: (i, 0)), out_shape=_sds((rows_all, IN_COLS), BF16),
        compiler_params=_params(("parallel",)),
    )(*args)


def _assemble_ctx(dp_all, dck16, dcv16, dcak16, dvc, t_rows, tm):
    c_rows = dck16.shape[0]
    rb = t_rows // tm

    def body(_, dck, dcv, dcak, dcav, o_ref):
        o_ref[:, RK:RK + RH * DK] = dck[...]
        o_ref[:, RV:RV + RH * DV] = dcv[...]
        o_ref[:, AK:AK + HKV * HD] = dcak[...]
        o_ref[:, AV:AV + HKV * HD] = dcav[...].astype(BF16)
        o_ref[:, KV_COLS:] = jnp.zeros((tm, IN_COLS - KV_COLS), BF16)

    args = (dck16, dcv16, dcak16, dvc)
    return pl.pallas_call(
        body, name="assemble_ctx", grid=(c_rows // tm,),
        in_specs=[pl.BlockSpec(memory_space=pl.ANY)]
        + [pl.BlockSpec((tm, a.shape[1]), lambda i: (i, 0)) for a in args],
        out_specs=pl.BlockSpec((tm, IN_COLS), lambda i: (rb + i, 0)), out_shape=_sds(dp_all.shape, BF16),
        input_output_aliases={0: 0},
        compiler_params=_params(("parallel",)),
    )(dp_all, *args)


def _norm_bwd(dh, x2, mod3, norm_w, dxn, row_off, rows_per_group, group0, tm, name):
    with_dx = dxn is not None
    rows = x2.shape[0]
    rb0 = row_off // tm
    bpg = rows_per_group // tm
    ngroups = rows // rows_per_group

    def body(*refs):
        if with_dx:
            dh_ref, x_ref, sc_ref, nw_ref, dxn_ref, dx_ref, dsh_ref, dsc_ref, dnw_ref = refs
        else:
            dh_ref, x_ref, sc_ref, nw_ref, dsh_ref, dsc_ref, dnw_ref = refs
        i = pl.program_id(0)
        dhv = dh_ref[...]
        xv = x_ref[...]
        nw = nw_ref[...]
        r = lax.rsqrt(jnp.mean(xv * xv, axis=-1, keepdims=True) + EPS)
        xh = xv * r
        dm = dhv * (1.0 + sc_ref[...])
        dsh = jnp.sum(dhv, axis=0, keepdims=True)
        dsc = jnp.sum(dhv * (xh * nw), axis=0, keepdims=True)
        dnw = jnp.sum(dm * xh, axis=0, keepdims=True)
        if with_dx:
            dxh = dm * nw
            dx_ref[...] = dxn_ref[...] + r * (dxh - xh * jnp.mean(dxh * xh, axis=-1, keepdims=True))

        @pl.when(i % bpg == 0)
        def _():
            dsh_ref[...] = dsh
            dsc_ref[...] = dsc

        @pl.when(i % bpg != 0)
        def _():
            dsh_ref[...] += dsh
            dsc_ref[...] += dsc

        @pl.when(i == 0)
        def _():
            dnw_ref[...] = dnw

        @pl.when(i > 0)
        def _():
            dnw_ref[...] += dnw

    grp = pl.BlockSpec((None, 1, D), lambda i: (i // bpg, 0, 0))
    in_specs = [pl.BlockSpec((tm, D), lambda i: (rb0 + i, 0)), pl.BlockSpec((tm, D), lambda i: (i, 0)),
                pl.BlockSpec((None, 1, D), lambda i: (group0 + i // bpg, 0, 1)),
                pl.BlockSpec((1, D), lambda i: (0, 0))]
    args = [dh, x2, mod3, norm_w]
    out_specs = [grp, grp, pl.BlockSpec((1, D), lambda i: (0, 0))]
    out_shape = [_sds((ngroups, 1, D), F32), _sds((ngroups, 1, D), F32), _sds((1, D), F32)]
    if with_dx:
        in_specs.append(pl.BlockSpec((tm, D), lambda i: (i, 0)))
        args.append(dxn)
        out_specs.insert(0, pl.BlockSpec((tm, D), lambda i: (i, 0)))
        out_shape.insert(0, _sds((rows, D), F32))
    return pl.pallas_call(
        body, name=name, grid=(rows // tm,), in_specs=in_specs, out_specs=out_specs, out_shape=out_shape,
        compiler_params=_params(("arbitrary",)),
    )(*args)


def _small_final(dmod_all, dmodc_parts, c_rows, dm_loc_rows, nw_parts, misc_parts, c_ctx, r_pad, w_ada16):
    loc = dm_loc_rows.shape[1]

    def body(dm_ref, dmc_ref, c_ref, dml_ref, nwp_ref, mp_ref, cc_ref, r_ref, w_ref,
             gb_ref, gc_ref, gnw_ref, misc_ref, gwa_ref):
        dmc = jnp.sum(dmc_ref[...], axis=0, keepdims=True)
        gb_ref[...] = jnp.sum(dm_ref[...], axis=0, keepdims=True) + dmc
        dsc = _dot(jnp.broadcast_to(dmc, (8, 3 * D)).astype(BF16), w_ref[...], NT)[0:1, :]
        gc_ref[...] = dsc * _dsilu(cc_ref[...])
        gnw_ref[...] = jnp.sum(nwp_ref[...], axis=0, keepdims=True)
        misc = jnp.sum(mp_ref[...], axis=0, keepdims=True)
        y = jnp.exp2(r_ref[...])
        lane = lax.broadcasted_iota(jnp.int32, (1, D), 1)
        is_decay = jnp.logical_and(lane >= 2 * HD, lane < 2 * HD + 2 * RH)
        misc_ref[...] = misc * jnp.where(is_decay, -(y * np.float32(np.log(2.0))) / (1.0 - y), 1.0)
        gwa_ref[...] = _dot(_silu(c_ref[...]).astype(BF16), dml_ref[...].astype(BF16), TN)

    return pl.pallas_call(
        body, name="small_final",
        out_shape=[_sds((1, 3 * D), F32), _sds((1, D), F32), _sds((1, D), F32), _sds((1, D), F32), _sds((D, loc), F32)],
        compiler_params=pltpu.CompilerParams(vmem_limit_bytes=VMEM_LIMIT),
    )(dmod_all, dmodc_parts, c_rows, dm_loc_rows, nw_parts, misc_parts, c_ctx, r_pad, w_ada16)


def _adamw(w, g, m, v, name):
    rows, cols = w.shape
    tm = _pick(rows, 256, 8)
    bc1 = 1.0 - B1 ** STEP
    bc2 = 1.0 - B2 ** STEP

    def body(w_ref, g_ref, m_ref, v_ref, d_ref, nm_ref, nv_ref):
        g_ = g_ref[...]
        nm = B1 * m_ref[...] + (1.0 - B1) * g_
        nv = B2 * v_ref[...] + (1.0 - B2) * (g_ * g_)
        nm_ref[...] = nm
        nv_ref[...] = nv
        d_ref[...] = -LR * ((nm / bc1) / (jnp.sqrt(nv / bc2) + ADAM_EPS) + WD * w_ref[...])

    blk = pl.BlockSpec((tm, cols), lambda i: (i, 0))
    return pl.pallas_call(
        body, name=name, grid=(rows // tm,), in_specs=[blk] * 4, out_specs=[blk] * 3,
        out_shape=[_sds((rows, cols), F32)] * 3, compiler_params=_params(("parallel",)),
    )(w, g, m, v)


def _mesh_pos():
    return lax.axis_index("x"), lax.axis_index("y"), lax.axis_index("c")


def _all_gather(arrs, name):
    n = len(arrs)

    def body(*refs):
        ins, outs = refs[:n], refs[n:2 * n]
        send_sems, recv_sems, local_sems = refs[2 * n:]
        x, y, c = _mesh_pos()
        me, sib = (x, y, c), (x, y, 1 - c)
        chips = [(1 - x, y), (x, 1 - y), (1 - x, 1 - y)]

        def slot(p):
            return 4 * p[0] + 2 * p[1] + p[2]

        def copy(a, k, block, to, own):
            dst = outs[a].at[slot(block)]
            return pltpu.make_async_remote_copy(
                src_ref=ins[a] if own else dst, dst_ref=dst, send_sem=send_sems.at[a, k], recv_sem=recv_sems.at[a, k],
                device_id=to, device_id_type=MESH_T)

        mine = [pltpu.make_async_copy(ins[a], outs[a].at[slot(me)], local_sems.at[a]) for a in range(n)]
        for cp in mine:
            cp.start()
        first = []
        for a in range(n):
            first.append(copy(a, 0, me, sib, True))
            first += [copy(a, 1 + j, me, (*chip, c), True) for j, chip in enumerate(chips)]
        for cp in first:
            cp.start()
        passed = []
        for j, chip in enumerate(chips):
            for a in range(n):
                copy(a, 1 + j, (*chip, c), me, False).wait_recv()
                fwd = copy(a, 4 + j, (*chip, c), sib, False)
                fwd.start()
                passed.append(fwd)
        for a in range(n):
            copy(a, 0, sib, me, False).wait_recv()
            for j, chip in enumerate(chips):
                copy(a, 4 + j, (*chip, 1 - c), me, False).wait_recv()
        for cp in first + passed:
            cp.wait_send()
        for cp in mine:
            cp.wait()

    hbm = pl.BlockSpec(memory_space=pl.ANY)
    return pl.pallas_call(
        body, name=name, in_specs=[hbm] * n, out_specs=[hbm] * n,
        out_shape=[_sds((N_DEV,) + a.shape, a.dtype) for a in arrs],
        scratch_shapes=[pltpu.SemaphoreType.DMA((n, 7)), pltpu.SemaphoreType.DMA((n, 7)), pltpu.SemaphoreType.DMA((n,))],
    )(*arrs)


def _pair_exchange(arrs, name):
    n = len(arrs)

    def body(*refs):
        ins, outs = refs[:n], refs[n:2 * n]
        send_sems, recv_sems = refs[2 * n:]
        x, y, c = _mesh_pos()
        sib = (x, y, 1 - c)
        sends = []
        for a in range(n):
            for k in range(4):
                sends.append(pltpu.make_async_remote_copy(
                    src_ref=ins[a].at[2 * k + 1 - c], dst_ref=outs[a].at[k], send_sem=send_sems.at[a, k],
                    recv_sem=recv_sems.at[a, k], device_id=sib, device_id_type=MESH_T))
        for cp in sends:
            cp.start()
        for cp in sends:
            cp.wait_recv()
        for cp in sends:
            cp.wait_send()

    hbm = pl.BlockSpec(memory_space=pl.ANY)
    return pl.pallas_call(
        body, name=name, in_specs=[hbm] * n, out_specs=[hbm] * n,
        out_shape=[_sds((4,) + a.shape[1:], a.dtype) for a in arrs],
        scratch_shapes=[pltpu.SemaphoreType.DMA((n, 4)), pltpu.SemaphoreType.DMA((n, 4))],
    )(*arrs)


def _pair_add(part, got, core, name):
    _, rows, cols = part.shape
    tm = _pick(rows, 256, 16)
    p4 = part.reshape(4, 2, rows, cols)

    def body(core_ref, p_ref, g_ref, o_ref):
        o_ref[...] = (p_ref[...].astype(F32) + g_ref[...].astype(F32)).astype(BF16)

    blk = pl.BlockSpec((None, tm, cols), lambda k, i, cr: (k, i, 0))
    return pl.pallas_call(
        body, name=name,
        grid_spec=pltpu.PrefetchScalarGridSpec(
            num_scalar_prefetch=1, grid=(4, rows // tm),
            in_specs=[pl.BlockSpec((None, None, tm, cols), lambda k, i, cr: (k, cr[0], i, 0)), blk], out_specs=blk),
        out_shape=_sds((4, rows, cols), BF16), compiler_params=_params(("parallel", "parallel")),
    )(core, p4, got)


def _chip_sum(pair_sums, landed, chip, name):
    _, rows, cols = pair_sums.shape
    tm = _pick(rows, 256, 16)

    def body(chip_ref, s_ref, l_ref, o_ref):
        acc = s_ref[...].astype(F32)
        for j in range(3):
            acc = acc + l_ref[j].astype(F32)
        o_ref[...] = acc

    return pl.pallas_call(
        body, name=name,
        grid_spec=pltpu.PrefetchScalarGridSpec(
            num_scalar_prefetch=1, grid=(rows // tm,),
            in_specs=[pl.BlockSpec((None, tm, cols), lambda i, ch: (ch[0], i, 0)),
                      pl.BlockSpec((3, tm, cols), lambda i, ch: (0, i, 0))],
            out_specs=pl.BlockSpec((tm, cols), lambda i, ch: (i, 0))),
        out_shape=_sds((rows, cols), F32), compiler_params=_params(("parallel",)),
    )(chip, pair_sums, landed)


_HBM = pl.BlockSpec(memory_space=pltpu.HBM)
_SEM = pl.BlockSpec(memory_space=pltpu.SEMAPHORE)
_EFFECT = pltpu.SideEffectType.DATAFLOW_SIDE_EFFECTING


def _chip_routes(n):
    def plan(x, y, c):
        routes = []
        for a in range(n):
            for j in range(1, 4):
                px, py = x ^ (j >> 1), y ^ (j & 1)
                routes.append((a, 2 * px + py, (px, py, c), j - 1))
        return routes
    return plan, 3 * n


def _bcast_routes(n):
    def plan(x, y, c):
        routes = []
        for a in range(n):
            for k in range(1, N_DEV):
                peer = (x ^ ((k >> 2) & 1), y ^ ((k >> 1) & 1), c ^ (k & 1))
                routes.append((a, 0, peer, 4 * x + 2 * y + c))
        return routes
    return plan, 7 * n


def _route_copies(srcs, lands, send_sems, recv_sems, routes):
    return [pltpu.make_async_remote_copy(
        src_ref=srcs[a].at[sb], dst_ref=lands[a].at[lb], send_sem=send_sems.at[r], recv_sem=recv_sems.at[r],
        device_id=peer, device_id_type=MESH_T) for r, (a, sb, peer, lb) in enumerate(routes)]


def _exchange_start(srcs, lands, routes, name, after=()):
    plan, count = routes
    n = len(srcs)
    n_in = 2 * n + len(after)

    def body(*refs):
        send_sems, recv_sems = refs[n_in], refs[n_in + 1]
        token = refs[-1]
        for cp in _route_copies(refs[:n], refs[n:2 * n], send_sems, recv_sems, plan(*_mesh_pos())):
            cp.start()
        token[...] = jnp.zeros_like(token)

    args = [pltpu.with_memory_space_constraint(a, pltpu.HBM) for a in list(srcs) + list(lands)]
    out = pl.pallas_call(
        body, name=name,
        out_shape=(pltpu.SemaphoreType.DMA((count,)), pltpu.SemaphoreType.DMA((count,)),
                   *[pltpu.HBM(a.shape, a.dtype) for a in args], _sds((8, 128), F32)),
        in_specs=[_HBM] * (2 * n) + [pl.BlockSpec(memory_space=pl.ANY)] * len(after),
        out_specs=(_SEM, _SEM, *([_HBM] * (2 * n)), pl.BlockSpec(memory_space=pltpu.VMEM)),
        input_output_aliases={i: 2 + i for i in range(2 * n)},
        compiler_params=pltpu.CompilerParams(has_side_effects=_EFFECT),
    )(*args, *after)
    return (out[0], out[1], list(out[2:2 + 2 * n]), routes), out[-1]


def _exchange_wait_some(state, after, only, name):
    send_sems, recv_sems, bufs, (plan, count) = state
    n = len(bufs) // 2

    def body(*refs):
        send_s, recv_s = refs[2 * n], refs[2 * n + 1]
        for r, cp in enumerate(_route_copies(refs[:n], refs[n:2 * n], send_s, recv_s, plan(*_mesh_pos()))):
            if only is None or r in only:
                cp.wait_send()
                cp.wait_recv()

    out = pl.pallas_call(
        body, name=name, out_shape=tuple(pltpu.HBM(a.shape, a.dtype) for a in bufs),
        in_specs=[_HBM] * (2 * n) + [_SEM, _SEM, pl.BlockSpec(memory_space=pl.ANY)], out_specs=tuple([_HBM] * (2 * n)),
        input_output_aliases={i: i for i in range(2 * n)},
        compiler_params=pltpu.CompilerParams(has_side_effects=_EFFECT),
    )(*bufs, send_sems, recv_sems, after)
    return (send_sems, recv_sems, list(out), (plan, count)), list(out[:n]), list(out[n:])


def _exchange_wait(state, after, name):
    _, srcs, lands = _exchange_wait_some(state, after, None, name)
    return srcs, lands


def _sibling_gather(shard, name, after=()):
    def body(in_ref, *rest):
        out_ref, send_sem, recv_sem, local_sem = rest[-4:]
        x, y, c = _mesh_pos()
        mine = pltpu.make_async_copy(in_ref, out_ref.at[c], local_sem)
        mine.start()
        to_sib = pltpu.make_async_remote_copy(src_ref=in_ref, dst_ref=out_ref.at[c], send_sem=send_sem, recv_sem=recv_sem,
                                              device_id=(x, y, 1 - c), device_id_type=MESH_T)
        to_sib.start()
        pltpu.make_async_remote_copy(src_ref=in_ref, dst_ref=out_ref.at[1 - c], send_sem=send_sem, recv_sem=recv_sem,
                                     device_id=(x, y, 1 - c), device_id_type=MESH_T).wait_recv()
        to_sib.wait_send()
        mine.wait()

    hbm = pl.BlockSpec(memory_space=pl.ANY)
    return pl.pallas_call(
        body, name=name, in_specs=[hbm] * (1 + len(after)), out_specs=hbm,
        out_shape=_sds((N_DEV,) + shard.shape, shard.dtype),
        scratch_shapes=[pltpu.SemaphoreType.DMA, pltpu.SemaphoreType.DMA, pltpu.SemaphoreType.DMA],
    )(shard, *after)


def _group_routes():
    def plan(x, y, c):
        return [(0, 0, (x ^ (j >> 1), y ^ (j & 1), c), 2 * j + c) for j in range(1, 4)]
    return plan, 3


def _pair_fill(groups, j, name):
    def body(_, g_ref, send_sem, recv_sem):
        x, y, c = _mesh_pos()
        mine = g_ref.at[2 * j + c]
        to_sib = pltpu.make_async_remote_copy(src_ref=mine, dst_ref=mine, send_sem=send_sem, recv_sem=recv_sem,
                                              device_id=(x, y, 1 - c), device_id_type=MESH_T)
        to_sib.start()
        pltpu.make_async_remote_copy(src_ref=mine, dst_ref=g_ref.at[2 * j + 1 - c], send_sem=send_sem, recv_sem=recv_sem,
                                     device_id=(x, y, 1 - c), device_id_type=MESH_T).wait_recv()
        to_sib.wait_send()

    hbm = pl.BlockSpec(memory_space=pl.ANY)
    return pl.pallas_call(
        body, name=name, in_specs=[hbm], out_specs=hbm, out_shape=_sds(groups.shape, groups.dtype),
        input_output_aliases={0: 0},
        scratch_shapes=[pltpu.SemaphoreType.DMA, pltpu.SemaphoreType.DMA],
    )(groups)


def _in_proj_group(h_all, groups, j, chip, px_prev, after, name):
    rows_all = h_all.shape[0]
    gcols = IN_COLS // 4
    tm = _pick(rows_all, 1536, 128)
    g4 = groups.reshape(4, gcols, D)

    n_lead = (1 if px_prev is not None else 0) + len(after)
    lead = ([px_prev] if px_prev is not None else []) + list(after)

    def body(chip_ref, *refs):
        h_ref, w_ref, o_ref = refs[n_lead:]
        o_ref[...] = _dot(h_ref[...], w_ref[...], NT).astype(BF16)
    return pl.pallas_call(
        body, name=name,
        grid_spec=pltpu.PrefetchScalarGridSpec(
            num_scalar_prefetch=1, grid=(rows_all // tm,),
            in_specs=[pl.BlockSpec(memory_space=pl.ANY)] * n_lead
            + [pl.BlockSpec((tm, D), lambda i, ch: (i, 0)), pl.BlockSpec((None, gcols, D), lambda i, ch: (j, 0, 0))],
            out_specs=pl.BlockSpec((tm, gcols), lambda i, ch: (i, ch[0] ^ j))),
        out_shape=_sds((rows_all, IN_COLS), BF16),
        input_output_aliases={1: 0} if px_prev is not None else {},
        compiler_params=_params(("parallel",)),
    )(chip, *lead, h_all, g4)


def _d_h_groups(dp_all, groups, chip, after):
    rows_all = dp_all.shape[0]
    gcols = IN_COLS // 4
    tm = _pick(rows_all, 1536, 128)
    g4 = groups.reshape(4, gcols, D)
    n_lead = len(after)

    def body(chip_ref, *refs):
        a_ref, w_ref, o_ref = refs[n_lead:]
        j = pl.program_id(1)
        part = _dot(a_ref[...], w_ref[...])

        @pl.when(j == 0)
        def _():
            o_ref[...] = part

        @pl.when(j > 0)
        def _():
            o_ref[...] += part

    return pl.pallas_call(
        body, name="d_h",
        grid_spec=pltpu.PrefetchScalarGridSpec(
            num_scalar_prefetch=1, grid=(rows_all // tm, 4),
            in_specs=[pl.BlockSpec(memory_space=pl.ANY)] * n_lead
            + [pl.BlockSpec((tm, gcols), lambda i, j, ch: (i, ch[0] ^ j)),
               pl.BlockSpec((None, gcols, D), lambda i, j, ch: (j, 0, 0))],
            out_specs=pl.BlockSpec((tm, D), lambda i, j, ch: (i, 0))),
        out_shape=_sds((rows_all, D), F32),
        compiler_params=_params(("parallel", "arbitrary")),
    )(chip, *after, dp_all, g4)


def _reduce_scatter_start(parts, core, name):
    got = _pair_exchange(parts, name + "_pair")
    sums = [_pair_add(p, g, core, "%s_add_%d" % (name, i)) for i, (p, g) in enumerate(zip(parts, got))]
    lands = [lax.empty((3,) + s_.shape[1:], BF16) for s_ in sums]
    return _exchange_start(sums, lands, _chip_routes(len(sums)), name + "_start")


def _reduce_scatter_finish(rs_state, after, chip, name):
    sums, landed = _exchange_wait(rs_state, after, name + "_wait")
    return [_chip_sum(s_, l_, chip, "%s_sum_%d" % (name, i)) for i, (s_, l_) in enumerate(zip(sums, landed))]


def _local_step(x, c, ctx, c_ctx, norm_w, b_ada, ret_log2_decay, q_norm_w, k_norm_w, loss_target,
                w_ada16, proj_in, proj_back, get_w_o, on_out_grads, on_in_grad):
    nb, seq, _ = x.shape
    cx = ctx.shape[1]
    t_rows, c_rows = nb * seq, nb * cx
    rows_all = t_rows + c_rows
    nc = seq // CH
    tm = _pick(seq, 256, 128)
    te = _pick(seq, 512, 128)
    assert cx % tm == 0 and t_rows % cx == 0 and seq % GRID_W == 0

    x2 = x.reshape(t_rows, D)
    ctx2 = ctx.reshape(c_rows, D)
    tgt = loss_target.reshape(t_rows, D)
    c8 = jnp.zeros((8, D), F32).at[:nb].set(c).at[nb].set(c_ctx)
    lg = _log_gamma(ret_log2_decay)
    cos, sin = _rope_tables(seq)

    mod = _mod_fwd(c8, w_ada16, b_ada)
    mod3 = mod[:, None, :]
    h_all = _norm_fwd(x2, mod3, norm_w, rows_all, 0, seq, 0, None, te, "norm_fwd")
    h_all = _norm_fwd(ctx2, mod3, norm_w, rows_all, t_rows, c_rows, nb, h_all, tm, "norm_fwd_ctx")
    px = proj_in(h_all)
    s0f, s0b = _ctx_state(px, lg, nb, t_rows, cx)
    o_f, o_b, hist_f, hist_b = _ret_fwd(px, lg, s0f, s0b, nb, nc)
    yret16 = _ret_post(o_f, o_b, px, te)
    q16 = _qk_prep(px, q_norm_w, cos, sin, t_rows, 0, AQ, HQ, 4, seq, te, "q_prep")
    kx16 = _qk_prep(px, k_norm_w, cos, sin, t_rows, 0, AK, HKV, HKV, seq, te, "k_prep")
    kc16 = _qk_prep(px, k_norm_w, None, None, c_rows, t_rows, AK, HKV, HKV, seq, tm, "kc_prep")
    o_att, yatt16, lse = _att_fwd(q16, kx16, kc16, px, nb, seq, cx, tm)
    w_o_ret16, w_o_att16, w_out16 = get_w_o(lse)
    a_ret, a_att, y16 = _merge(yret16, yatt16, px, w_o_ret16, w_o_att16, te)
    dxn, dout16, dgate, loss_b = _outproj(y16, w_out16, x2, tgt, mod3, nb, seq, te)

    gw_out = _matmul(y16, dout16, ta=True, tm=D, tn=D, tk=D, out_dtype=BF16, name="gw_out")
    da_ret16, da_att16, dmr16, dma16 = _bwd_merge(dout16, w_out16, px, a_ret, a_att, te)
    gw_o_ret = _matmul(yret16, da_ret16, ta=True, tm=D, tn=D, tk=D, out_dtype=BF16, name="gw_o_ret")
    gw_o_att = _matmul(yatt16, da_att16, ta=True, tm=D, tn=D, tk=D, out_dtype=BF16, name="gw_o_att")
    out_state, out_started = on_out_grads([gw_o_ret, gw_o_att, gw_out])
    do16, drg16 = _bwd_branch_ret(da_ret16, w_o_ret16, px, o_f, o_b, te, after=out_started)
    dao, dag16 = _bwd_branch_att(da_att16, w_o_att16, px, o_att, te)
    dq_rot, dkx, dvx, dkc, dvc = _att_bwd(q16, kx16, kc16, px, dao, o_att, lse, nb, seq, cx, tm)
    daq16, gq = _qk_prep_bwd(dq_rot, px, q_norm_w, cos, sin, t_rows, 0, AQ, HQ, 4, seq, te, "q_prep_bwd")
    dak16, gk_lat = _qk_prep_bwd(dkx.reshape(t_rows, HKV * HD), px, k_norm_w, cos, sin, t_rows, 0, AK, HKV, HKV, seq, te,
                                 "k_prep_bwd")
    dcak16, gk_ctx = _qk_prep_bwd(dkc.reshape(c_rows, HKV * HD), px, k_norm_w, None, None, c_rows, t_rows, AK, HKV, HKV,
                                  seq, tm, "kc_prep_bwd")
    dq_f, dk_f, dv_f, dq_b, dk_b, dv_b, ds_f, ds_b, dlg_scan = _ret_bwd(px, lg, do16, hist_f, hist_b, nb, nc)
    dck16, dcv16, dlg_ctx = _ctx_state_bwd(px, lg, ds_f, ds_b, nb, t_rows, cx)
    dp_all = _assemble_lat(rows_all, dk_f, dk_b, dv_f, dv_b, dak16, dvx.reshape(t_rows, HKV * HD), dq_f, dq_b, drg16,
                           daq16, dag16, dmr16, dma16, tm)
    dp_all = _assemble_ctx(dp_all, dck16, dcv16, dcak16, dvc.reshape(c_rows, HKV * HD), t_rows, tm)
    gw_in_t = _matmul(dp_all, h_all, ta=True, tm=1536, tn=D, tk=1536, out_dtype=BF16, name="gw_in")
    in_state, in_started = on_in_grad(gw_in_t)
    dh = proj_back(dp_all, in_started)
    grad_x, dsh, dsc, gnw_lat = _norm_bwd(dh, x2, mod3, norm_w, dxn, 0, seq, 0, te, "norm_bwd")
    dsh_c, dsc_c, gnw_ctx = _norm_bwd(dh, ctx2, mod3, norm_w, None, t_rows, c_rows, nb, tm, "norm_bwd_ctx")

    dlg = (jnp.sum(dlg_scan[:, :, 0], axis=0) + jnp.sum(dlg_ctx[:, :, :2, 0], axis=0).T.reshape(2 * RH)).reshape(1, 2 * RH)
    misc = jnp.concatenate([gq, gk_lat + gk_ctx, dlg, jnp.sum(loss_b[:, 0, 0]).reshape(1, 1),
                            jnp.zeros((1, D - 2 * HD - 2 * RH - 1), F32)], axis=1)
    rows = []
    for b in range(nb):
        rows += [dsh[b], dsc[b], dgate[b]]
    rows += [dsh_c[0], dsc_c[0]] + [c[b:b + 1] for b in range(nb)] + [gnw_lat + gnw_ctx, misc]
    payload = jnp.concatenate(rows + [jnp.zeros((PAY_ROWS - len(rows), D), F32)], axis=0)
    return grad_x.reshape(nb, seq, D), out_state, in_state, payload


def _finish_small(gathered, nb, c_ctx, ret_log2_decay, w_ada16, dev):
    n_dev = gathered.shape[0]
    loc = 3 * D // n_dev
    dmod_all = gathered[:, :3 * nb].reshape(n_dev * nb, 3 * D)
    dmodc_parts = jnp.concatenate([gathered[:, 3 * nb:3 * nb + 2].reshape(n_dev, 2 * D), jnp.zeros((n_dev, D), F32)], axis=1)
    c_all = gathered[:, 3 * nb + 2:4 * nb + 2].reshape(n_dev * nb, D)
    nw_parts = gathered[:, 4 * nb + 2]
    misc_parts = gathered[:, 4 * nb + 3]
    n_rows = n_dev * nb + n_dev
    pad = (-n_rows) % 16
    c_rows = jnp.concatenate([c_all, jnp.broadcast_to(c_ctx.reshape(1, D), (n_dev, D)), jnp.zeros((pad, D), F32)], axis=0)
    dm_rows = jnp.concatenate([dmod_all, dmodc_parts, jnp.zeros((pad, 3 * D), F32)], axis=0)
    dm_loc_rows = lax.dynamic_slice_in_dim(dm_rows, dev * loc, loc, axis=1)
    r_pad = jnp.full((1, D), -1.0, F32).at[:, 2 * HD:2 * HD + 2 * RH].set(ret_log2_decay.reshape(1, 2 * RH))
    gb, gc, gnw, misc, gwa = _small_final(dmod_all, dmodc_parts, c_rows, dm_loc_rows, nw_parts, misc_parts,
                                          c_ctx.reshape(1, D), r_pad, w_ada16)
    return (gb, gc, gnw, misc[:, :HD], misc[:, HD:2 * HD], misc[:, 2 * HD:2 * HD + 2 * RH], gwa,
            misc[0, 2 * HD + 2 * RH])


def kernel(x, c, ctx, c_ctx, norm_w, w_ada, b_ada, w_in, ret_log2_decay, q_norm_w, k_norm_w, w_o_ret, w_o_att, w_out, loss_target, m_c_ctx, m_norm_w, m_w_ada, m_b_ada, m_w_in, m_ret_log2_decay, m_q_norm_w, m_k_norm_w, m_w_o_ret, m_w_o_att, m_w_out, v_c_ctx, v_norm_w, v_w_ada, v_b_ada, v_w_in, v_ret_log2_decay, v_q_norm_w, v_k_norm_w, v_w_o_ret, v_w_o_att, v_w_out):
    nb = x.shape[0]
    mx, my, mc = _mesh_pos()
    dev = 4 * mx + 2 * my + mc
    core = jnp.reshape(mc, (1,)).astype(jnp.int32)
    chip = jnp.reshape(2 * mx + my, (1,)).astype(jnp.int32)

    (g_ada,) = _all_gather([w_ada[0].astype(BF16)], "gather_ada")
    w_ada16 = jnp.transpose(g_ada, (1, 0, 2)).reshape(D, 3 * D)

    w_in_t = jnp.transpose(w_in[0])
    in_shard = w_in_t.astype(BF16)
    groups0 = _sibling_gather(in_shard, "gather_in_pair", after=(g_ada,))
    gin_state, gin_token = _exchange_start([in_shard[None]], [groups0], _group_routes(), "gather_in_start")
    w_in_groups = []

    def proj_in(h_all):
        state = gin_state
        px = _in_proj_group(h_all, state[2][1], 0, chip, None, (gin_token, wo_token), "in_proj_0")
        for j in (1, 2, 3):
            state, _, (groups,) = _exchange_wait_some(state, px, (j - 1,), "gather_in_wait_%d" % j)
            groups = _pair_fill(groups, j, "gather_in_fill_%d" % j)
            state = (state[0], state[1], [state[2][0], groups], state[3])
            px = _in_proj_group(h_all, groups, j, chip, px, (), "in_proj_%d" % j)
        w_in_groups.append(groups)
        return px

    def proj_back(dp_all, after):
        return _d_h_groups(dp_all, w_in_groups[0], chip, after)

    wo_shards = [w_[0].astype(BF16) for w_ in (w_o_ret, w_o_att, w_out)]
    wo_lands = [lax.dynamic_update_slice(lax.empty((N_DEV,) + s_.shape, BF16), s_[None], (dev, 0, 0)) for s_ in wo_shards]
    wo_state, wo_token = _exchange_start([s_[None] for s_ in wo_shards], wo_lands, _bcast_routes(3), "gather_wo_start",
                                         after=(gin_token,))

    def get_w_o(after):
        _, (l_ret, l_att, l_out) = _exchange_wait(wo_state, after, "gather_wo_wait")
        return l_ret.reshape(RH * DV, D), l_att.reshape(D, D), l_out.reshape(D, D)

    def on_out_grads(grads):
        parts = [g_.reshape(N_DEV, g_.shape[0] // N_DEV, D) for g_ in grads]
        state, token = _reduce_scatter_start(parts, core, "rs_out")
        return state, (token,)

    def on_in_grad(grad):
        state, token = _reduce_scatter_start([grad.reshape(N_DEV, IN_COLS // N_DEV, D)], core, "rs_in")
        return state, (token,)

    grad_x, out_state, in_state, payload = _local_step(
        x, c, ctx, c_ctx, norm_w, b_ada, ret_log2_decay, q_norm_w, k_norm_w, loss_target,
        w_ada16, proj_in, proj_back, get_w_o, on_out_grads, on_in_grad)

    (gathered,) = _all_gather([payload], "gather_small")
    gb, gc, gnw, gq, gk, gr, gwa, loss = _finish_small(gathered, nb, c_ctx, ret_log2_decay, w_ada16, dev)

    g_w_o_ret, g_w_o_att, g_w_out = _reduce_scatter_finish(out_state, gathered, chip, "rs_out")
    (g_w_in_t,) = _reduce_scatter_finish(in_state, gathered, chip, "rs_in")

    grads = [gc.reshape(c_ctx.shape), gnw, gwa[None], gb, g_w_in_t, gr.reshape(ret_log2_decay.shape), gq, gk,
             g_w_o_ret[None], g_w_o_att[None], g_w_out[None]]
    weights = [c_ctx, norm_w, w_ada, b_ada, w_in_t, ret_log2_decay, q_norm_w, k_norm_w, w_o_ret, w_o_att, w_out]
    ms = [m_c_ctx, m_norm_w, m_w_ada, m_b_ada, jnp.transpose(m_w_in[0]), m_ret_log2_decay, m_q_norm_w, m_k_norm_w,
          m_w_o_ret, m_w_o_att, m_w_out]
    vs = [v_c_ctx, v_norm_w, v_w_ada, v_b_ada, jnp.transpose(v_w_in[0]), v_ret_log2_decay, v_q_norm_w, v_k_norm_w,
          v_w_o_ret, v_w_o_att, v_w_out]
    deltas, new_ms, new_vs = [], [], []
    for i, (w, g, m, v) in enumerate(zip(weights, grads, ms, vs)):
        shape2 = (-1, w.shape[-1])
        res = _adamw(w.reshape(shape2), g.reshape(shape2), m.reshape(shape2), v.reshape(shape2), "adamw_%d" % i)
        for lst, r in zip((deltas, new_ms, new_vs), res):
            lst.append(jnp.transpose(r)[None] if i == 4 else r.reshape(w.shape))
    grads[4] = jnp.transpose(g_w_in_t)[None]
    return (loss, grad_x, *grads, *deltas, *new_ms, *new_vs)
```

```python
import numpy as np
import jax
import jax.numpy as jnp
from jax import lax
from jax.experimental import pallas as pl
from jax.experimental.pallas import tpu as pltpu

F32 = jnp.float32
BF16 = jnp.bfloat16

D = 1024
RH, DK, DV, CH = 4, 256, 512, 256
HQ, HKV, HD = 8, 2, 128
GRID_W = 64
ROPE_THETA = 10000.0
EPS = 1e-6
RK, RV, AK, AV, RQ, RG, AQ, AG, MR, MA = 0, 1024, 3072, 3328, 3584, 4608, 6656, 7680, 8704, 9728
IN_COLS = 10752
KV_COLS = 3584
N_DEV = 8
LR, B1, B2, ADAM_EPS, WD, STEP = 0.001, 0.9, 0.999, 1e-08, 0.01, 10
PAY_ROWS = 16
VMEM_LIMIT = 56 * 1024 * 1024
MESH_T = pl.DeviceIdType.MESH

NT = (((1,), (1,)), ((), ()))
TN = (((0,), (0,)), ((), ()))
SM_C = (HD ** -0.5) * float(np.log2(np.e))


def _params(sem):
    return pltpu.CompilerParams(dimension_semantics=sem, vmem_limit_bytes=VMEM_LIMIT)


def _pick(n, target, mult=8):
    best = None
    for t in range(mult, min(n, target) + 1, mult):
        if n % t == 0:
            best = t
    return best or n


def _dot(a, b, dn=None):
    if dn is None:
        return jnp.dot(a, b, preferred_element_type=F32)
    return lax.dot_general(a, b, dn, preferred_element_type=F32)


def _sig(v):
    return jax.nn.sigmoid(v)


def _silu(v):
    return v * _sig(v)


def _dsilu(v):
    s = _sig(v)
    return s * (1.0 + v * (1.0 - s))


def _sds(shape, dtype):
    return jax.ShapeDtypeStruct(shape, dtype)


def _matmul(a, b, *, ta=False, tb=False, tm, tn, tk, out_dtype, name, after=()):
    m = a.shape[1] if ta else a.shape[0]
    kdim = a.shape[0] if ta else a.shape[1]
    n = b.shape[0] if tb else b.shape[1]
    tm, tn, tk = _pick(m, tm, 128), _pick(n, tn, 128), _pick(kdim, tk, 128)
    nk = kdim // tk
    dn = (((0 if ta else 1,), (1 if tb else 0,)), ((), ()))

    def body(a_ref, b_ref, *rest):
        o_ref, acc_ref = rest[-2:]
        k = pl.program_id(2)
        part = _dot(a_ref[...].astype(BF16), b_ref[...].astype(BF16), dn)
        if nk == 1:
            o_ref[...] = part.astype(o_ref.dtype)
        else:
            @pl.when(k == 0)
            def _():
                acc_ref[...] = part

            @pl.when(k > 0)
            def _():
                acc_ref[...] += part

            @pl.when(k == nk - 1)
            def _():
                o_ref[...] = acc_ref[...].astype(o_ref.dtype)

    a_spec = pl.BlockSpec((tk, tm), lambda i, j, k: (k, i)) if ta else pl.BlockSpec((tm, tk), lambda i, j, k: (i, k))
    b_spec = pl.BlockSpec((tn, tk), lambda i, j, k: (j, k)) if tb else pl.BlockSpec((tk, tn), lambda i, j, k: (k, j))
    return pl.pallas_call(
        body, name=name, grid=(m // tm, n // tn, nk),
        in_specs=[a_spec, b_spec] + [pl.BlockSpec(memory_space=pl.ANY)] * len(after),
        out_specs=pl.BlockSpec((tm, tn), lambda i, j, k: (i, j)), out_shape=_sds((m, n), out_dtype),
        scratch_shapes=[pltpu.VMEM((tm, tn) if nk > 1 else (8, 128), F32)],
        compiler_params=_params(("parallel", "parallel", "arbitrary")),
    )(a, b, *after)


def _log_gamma(r):
    rp = jnp.full((8, 128), -1.0, F32).at[:2, :RH].set(r.reshape(2, RH))

    def body(r_ref, o_ref):
        o_ref[...] = jnp.log1p(-jnp.exp2(r_ref[...]))

    out = pl.pallas_call(body, name="log_gamma", out_shape=_sds((8, 128), F32))(rp)
    return out[:2, :RH]


def _mod_fwd(c8, w_ada16, b_ada):
    def body(c_ref, w_ref, b_ref, o_ref):
        o_ref[...] = _dot(_silu(c_ref[...]).astype(BF16), w_ref[...]) + b_ref[...]

    return pl.pallas_call(
        body, name="mod_fwd", grid=(3,),
        in_specs=[pl.BlockSpec((8, D), lambda j: (0, 0)), pl.BlockSpec((D, D), lambda j: (0, j)),
                  pl.BlockSpec((1, D), lambda j: (0, j))],
        out_specs=pl.BlockSpec((8, D), lambda j: (0, j)), out_shape=_sds((8, 3 * D), F32),
        compiler_params=_params(("arbitrary",)),
    )(c8, w_ada16, b_ada)


def _norm_fwd(x2, mod3, norm_w, rows_all, row_off, rows_per_group, group0, h_prev, tm, name):
    rows = x2.shape[0]
    rb0 = row_off // tm
    bpg = rows_per_group // tm

    def body(*refs):
        x_ref, sh_ref, sc_ref, nw_ref, o_ref = refs[-5:]
        xv = x_ref[...]
        r = lax.rsqrt(jnp.mean(xv * xv, axis=-1, keepdims=True) + EPS)
        o_ref[...] = ((xv * r) * nw_ref[...] * (1.0 + sc_ref[...]) + sh_ref[...]).astype(BF16)

    in_specs = [pl.BlockSpec((tm, D), lambda i: (i, 0)),
                pl.BlockSpec((None, 1, D), lambda i: (group0 + i // bpg, 0, 0)),
                pl.BlockSpec((None, 1, D), lambda i: (group0 + i // bpg, 0, 1)),
                pl.BlockSpec((1, D), lambda i: (0, 0))]
    args = [x2, mod3, mod3, norm_w]
    alias = {}
    if h_prev is not None:
        in_specs.insert(0, pl.BlockSpec(memory_space=pl.ANY))
        args.insert(0, h_prev)
        alias = {0: 0}
    return pl.pallas_call(
        body, name=name, grid=(rows // tm,), in_specs=in_specs,
        out_specs=pl.BlockSpec((tm, D), lambda i: (rb0 + i, 0)), out_shape=_sds((rows_all, D), BF16),
        input_output_aliases=alias, compiler_params=_params(("parallel",)),
    )(*args)


def _decays(lg, fwd):
    ii = lax.broadcasted_iota(jnp.int32, (CH, CH), 0)
    jj = lax.broadcasted_iota(jnp.int32, (CH, CH), 1)
    ri = lax.broadcasted_iota(jnp.int32, (CH, 1), 0).astype(F32)
    rel = (ii - jj) if fwd else (jj - ii)
    relf = jnp.maximum(rel, 0).astype(F32)
    mask = jnp.where(rel >= 0, jnp.exp(lg * relf), 0.0)
    qe = (ri + 1.0) if fwd else (CH - ri)
    ke = (CH - 1.0 - ri) if fwd else ri
    return mask, relf, jnp.exp(lg * qe), qe, jnp.exp(lg * ke), ke


def _wide_specs(rowf):
    return [pl.BlockSpec((CH, 2 * DK), lambda b, c: (rowf(b, c), RQ // (2 * DK))),
            pl.BlockSpec((CH, 2 * DK), lambda b, c: (rowf(b, c), RQ // (2 * DK) + 1)),
            pl.BlockSpec((CH, RH * DK), lambda b, c: (rowf(b, c), RK // (RH * DK))),
            pl.BlockSpec((CH, 2 * DV), lambda b, c: (rowf(b, c), RV // (2 * DV))),
            pl.BlockSpec((CH, 2 * DV), lambda b, c: (rowf(b, c), RV // (2 * DV) + 1))]


def _head_qkv(refs, h):
    q0, q1, k, v0, v1 = refs
    lo = h % 2
    q = (q0, q1)[h // 2][:, lo * DK:(lo + 1) * DK].astype(F32)
    kk = k[:, h * DK:(h + 1) * DK].astype(F32) * (DK ** -0.5)
    v16 = (v0, v1)[h // 2][:, lo * DV:(lo + 1) * DV].astype(BF16)
    return q, kk, v16


def _ctx_state(px, lg, nb, t_rows, cx):
    rb = t_rows // cx

    def body(lg_ref, k_ref, v_ref, sf_ref, sb_ref):
        h = pl.program_id(1)
        pos = lax.broadcasted_iota(jnp.int32, (cx, 1), 0).astype(F32)
        k = k_ref[...].astype(F32) * (DK ** -0.5)
        v16 = v_ref[...].astype(BF16)
        wf = jnp.exp(lg_ref[0, h] * (cx - 1.0 - pos))
        wb = jnp.exp(lg_ref[1, h] * pos)
        sf_ref[...] = _dot((k * wf).astype(BF16), v16, TN)
        sb_ref[...] = _dot((k * wb).astype(BF16), v16, TN)

    st = pl.BlockSpec((None, None, DK, DV), lambda b, h: (b, h, 0, 0))
    return pl.pallas_call(
        body, name="ctx_state", grid=(nb, RH),
        in_specs=[pl.BlockSpec(memory_space=pltpu.SMEM),
                  pl.BlockSpec((cx, DK), lambda b, h: (rb + b, RK // DK + h)),
                  pl.BlockSpec((cx, DV), lambda b, h: (rb + b, RV // DV + h))],
        out_specs=[st, st], out_shape=[_sds((nb, RH, DK, DV), F32)] * 2,
        compiler_params=_params(("parallel", "parallel")),
    )(lg, px, px)


def _ret_fwd(px, lg, s0f, s0b, nb, nc):
    t_rows = nb * nc * CH

    def body(lg_ref, *refs):
        ins = (refs[0:5], refs[5:10])
        s0f_ref, s0b_ref, of_ref, ob_ref, hf_ref, hb_ref, sf, sb = refs[10:]
        c = pl.program_id(1)

        @pl.when(c == 0)
        def _():
            sf[...] = s0f_ref[...]
            sb[...] = s0b_ref[...]

        for d, (o_ref, h_ref, s) in enumerate(((of_ref, hf_ref, sf), (ob_ref, hb_ref, sb))):
            for h in range(RH):
                lg_d = lg_ref[d, h]
                mask, _, qd, _, kd, _ = _decays(lg_d, d == 0)
                q, k, v16 = _head_qkv(ins[d], h)
                a = _dot(q.astype(BF16), k.astype(BF16), NT)
                st = s[h]
                st16 = st.astype(BF16)
                h_ref[h] = st16
                o = _dot((a * mask).astype(BF16), v16) + _dot((q * qd).astype(BF16), st16)
                o_ref[:, h * DV:(h + 1) * DV] = o.astype(BF16)
                s[h] = st * jnp.exp(lg_d * CH) + _dot((k * kd).astype(BF16), v16, TN)

    def fw(b, c):
        return b * nc + c

    def bw(b, c):
        return b * nc + nc - 1 - c

    st = pl.BlockSpec((None, RH, DK, DV), lambda b, c: (b, 0, 0, 0))
    in_specs = [pl.BlockSpec(memory_space=pltpu.SMEM)] + _wide_specs(fw) + _wide_specs(bw) + [st, st]
    out_specs = [pl.BlockSpec((CH, RH * DV), lambda b, c: (fw(b, c), 0)),
                 pl.BlockSpec((CH, RH * DV), lambda b, c: (bw(b, c), 0)),
                 pl.BlockSpec((None, None, RH, DK, DV), lambda b, c: (b, c, 0, 0, 0)),
                 pl.BlockSpec((None, None, RH, DK, DV), lambda b, c: (b, nc - 1 - c, 0, 0, 0))]
    return pl.pallas_call(
        body, name="ret_fwd", grid=(nb, nc), in_specs=in_specs, out_specs=out_specs,
        out_shape=[_sds((t_rows, RH * DV), BF16)] * 2 + [_sds((nb, nc, RH, DK, DV), BF16)] * 2,
        scratch_shapes=[pltpu.VMEM((RH, DK, DV), F32), pltpu.VMEM((RH, DK, DV), F32)],
        compiler_params=_params(("parallel", "arbitrary")),
    )(lg, *([px] * 10), s0f, s0b)


def _ret_post(o_f, o_b, px, tm):
    t_rows = o_f.shape[0]

    def body(of_ref, ob_ref, g0, g1, g2, g3, y_ref):
        for h, g_ref in enumerate((g0, g1, g2, g3)):
            sl = slice(h * DV, (h + 1) * DV)
            o = of_ref[:, sl].astype(F32) + ob_ref[:, sl].astype(F32)
            r = lax.rsqrt(jnp.mean(o * o, axis=-1, keepdims=True) + EPS)
            y_ref[:, sl] = ((o * r) * _silu(g_ref[...].astype(F32))).astype(BF16)

    def gate(h):
        return pl.BlockSpec((tm, DV), lambda i: (i, RG // DV + h))

    wide = pl.BlockSpec((tm, RH * DV), lambda i: (i, 0))
    return pl.pallas_call(
        body, name="ret_post", grid=(t_rows // tm,),
        in_specs=[wide, wide] + [gate(h) for h in range(RH)],
        out_specs=wide, out_shape=_sds((t_rows, RH * DV), BF16),
        compiler_params=_params(("parallel",)),
    )(o_f, o_b, *([px] * RH))


def _rope_tables(seq):
    rows = seq // GRID_W
    row = np.repeat(np.arange(rows, dtype=np.float32), GRID_W)
    col = np.tile(np.arange(GRID_W, dtype=np.float32), rows)
    half = HD // 2
    freqs = (ROPE_THETA ** (-np.arange(0, half, 2, dtype=np.float32) / half)).astype(np.float32)
    ang = np.concatenate([row[:, None] * freqs, col[:, None] * freqs], axis=-1).astype(np.float32)
    cos = np.repeat(np.cos(ang), 2, axis=-1).astype(np.float32)
    sin = np.repeat(np.sin(ang), 2, axis=-1).astype(np.float32)
    sign = np.tile(np.array([-1.0, 1.0], np.float32), HD // 2)
    return jnp.asarray(cos), jnp.asarray(sin * sign)


def _swap_pairs(v):
    lane = lax.broadcasted_iota(jnp.int32, v.shape, 1)
    return jnp.where((lane & 1) == 0, pltpu.roll(v, HD - 1, 1), pltpu.roll(v, 1, 1))


def _qk_prep(px, nw, cos, sin, rows, row_off, col_off, heads, hb, seq, tm, name):
    rope = cos is not None
    rb0 = row_off // tm
    pb = seq // tm if rope else 1
    bw = hb * HD

    def body(*refs):
        if rope:
            x_ref, w_ref, c_ref, s_ref, o_ref = refs
        else:
            x_ref, w_ref, o_ref = refs
        for h in range(hb):
            sl = slice(h * HD, (h + 1) * HD)
            xv = x_ref[:, sl].astype(F32)
            r = lax.rsqrt(jnp.mean(xv * xv, axis=-1, keepdims=True) + EPS)
            t = (xv * r) * w_ref[...]
            if rope:
                t = t * c_ref[...] + _swap_pairs(t) * s_ref[...]
            o_ref[:, sl] = t.astype(BF16)

    in_specs = [pl.BlockSpec((tm, bw), lambda i, j: (rb0 + i, col_off // bw + j)),
                pl.BlockSpec((1, HD), lambda i, j: (0, 0))]
    args = [px, nw]
    if rope:
        in_specs += [pl.BlockSpec((tm, HD), lambda i, j: (i % pb, 0))] * 2
        args += [cos, sin]
    return pl.pallas_call(
        body, name=name, grid=(rows // tm, heads // hb), in_specs=in_specs,
        out_specs=pl.BlockSpec((tm, bw), lambda i, j: (i, j)), out_shape=_sds((rows, heads * HD), BF16),
        compiler_params=_params(("parallel", "parallel")),
    )(*args)


def _att_fwd(q16, kx16, kc16, px, nb, seq, cx, tq):
    t_rows = nb * seq
    nq = seq // tq
    rep = HQ // HKV
    gw = rep * HD

    def body(q_ref, kx_ref, kc_ref, vx_ref, vc_ref, g_ref, o_ref, y_ref, l_ref):
        kx = kx_ref[...]
        kc = kc_ref[...]
        vx = vx_ref[...].astype(BF16)
        vc = vc_ref[...].astype(BF16)
        l_ref[...] = jnp.zeros_like(l_ref)
        for r in range(rep):
            sl = slice(r * HD, (r + 1) * HD)
            q = q_ref[:, sl]
            s1 = _dot(q, kx, NT)
            s2 = _dot(q, kc, NT)
            m = jnp.maximum(jnp.max(s1, axis=-1, keepdims=True), jnp.max(s2, axis=-1, keepdims=True))
            e1 = jnp.exp2((s1 - m) * SM_C)
            e2 = jnp.exp2((s2 - m) * SM_C)
            tot = jnp.sum(e1, axis=-1, keepdims=True) + jnp.sum(e2, axis=-1, keepdims=True)
            o = (_dot(e1.astype(BF16), vx) + _dot(e2.astype(BF16), vc)) * (1.0 / tot)
            o_ref[:, sl] = o
            y_ref[:, sl] = (o * _silu(g_ref[:, sl].astype(F32))).astype(BF16)
            l_ref[:, r:r + 1] = m * SM_C + jnp.log(tot) * float(np.log2(np.e))

    qblk = pl.BlockSpec((tq, gw), lambda b, g, i: (b * nq + i, g))
    return pl.pallas_call(
        body, name="att_fwd", grid=(nb, HKV, nq),
        in_specs=[qblk,
                  pl.BlockSpec((seq, HD), lambda b, g, i: (b, g)),
                  pl.BlockSpec((cx, HD), lambda b, g, i: (b, g)),
                  pl.BlockSpec((seq, HD), lambda b, g, i: (b, AV // HD + g)),
                  pl.BlockSpec((cx, HD), lambda b, g, i: (t_rows // cx + b, AV // HD + g)),
                  pl.BlockSpec((tq, gw), lambda b, g, i: (b * nq + i, AG // gw + g))],
        out_specs=[qblk, qblk, pl.BlockSpec((tq, 128), lambda b, g, i: (b * nq + i, g))],
        out_shape=[_sds((t_rows, D), F32), _sds((t_rows, D), BF16), _sds((t_rows, HKV * 128), F32)],
        compiler_params=_params(("parallel", "parallel", "parallel")),
    )(q16, kx16, kc16, px, px, px)


def _gate_specs(tm, col0):
    hw = D // 2
    return [pl.BlockSpec((tm, hw), lambda i: (i, col0 // hw)), pl.BlockSpec((tm, hw), lambda i: (i, col0 // hw + 1))]


def _merge(yret16, yatt16, px, w_o_ret16, w_o_att16, tm):
    t_rows = yret16.shape[0]
    hw = D // 2

    def body(yr_ref, wr_ref, ya_ref, wa_ref, mr0, mr1, ma0, ma1, ar_ref, aa_ref, y_ref):
        ar = _dot(yr_ref[...], wr_ref[...])
        aa = _dot(ya_ref[...], wa_ref[...])
        ar_ref[...] = ar.astype(BF16)
        aa_ref[...] = aa.astype(BF16)
        for j, (mr_ref, ma_ref) in enumerate(((mr0, ma0), (mr1, ma1))):
            sl = slice(j * hw, (j + 1) * hw)
            y_ref[:, sl] = (_sig(mr_ref[...].astype(F32)) * ar[:, sl]
                            + _sig(ma_ref[...].astype(F32)) * aa[:, sl]).astype(BF16)

    row = pl.BlockSpec((tm, D), lambda i: (i, 0))
    return pl.pallas_call(
        body, name="merge", grid=(t_rows // tm,),
        in_specs=[pl.BlockSpec((tm, RH * DV), lambda i: (i, 0)), pl.BlockSpec((RH * DV, D), lambda i: (0, 0)),
                  row, pl.BlockSpec((D, D), lambda i: (0, 0))] + _gate_specs(tm, MR) + _gate_specs(tm, MA),
        out_specs=[row, row, row], out_shape=[_sds((t_rows, D), BF16)] * 3,
        compiler_params=_params(("parallel",)),
    )(yret16, w_o_ret16, yatt16, w_o_att16, px, px, px, px)


def _outproj(y16, w_out16, x2, tgt, mod3, nb, seq, tm):
    t_rows = nb * seq
    bpb = seq // tm

    def body(y_ref, w_ref, x_ref, t_ref, g_ref, dxn_ref, dout_ref, dg_ref, loss_ref):
        i = pl.program_id(1)
        out = _dot(y_ref[...], w_ref[...])
        gate = g_ref[...]
        diff = x_ref[...] + gate * out - t_ref[...]
        dxn = diff * (1.0 / D)
        dxn_ref[...] = dxn
        dout_ref[...] = (gate * dxn).astype(BF16)
        dg = jnp.sum(dxn * out, axis=0, keepdims=True)
        ls = jnp.broadcast_to(jnp.sum(diff * diff) * (0.5 / D), (1, 128))

        @pl.when(i == 0)
        def _():
            dg_ref[...] = dg
            loss_ref[...] = ls

        @pl.when(i > 0)
        def _():
            dg_ref[...] += dg
            loss_ref[...] += ls

    row = pl.BlockSpec((tm, D), lambda b, i: (b * bpb + i, 0))
    return pl.pallas_call(
        body, name="outproj", grid=(nb, bpb),
        in_specs=[row, pl.BlockSpec((D, D), lambda b, i: (0, 0)), row, row,
                  pl.BlockSpec((None, 1, D), lambda b, i: (b, 0, 2))],
        out_specs=[row, row, pl.BlockSpec((None, 1, D), lambda b, i: (b, 0, 0)),
                   pl.BlockSpec((None, 1, 128), lambda b, i: (b, 0, 0))],
        out_shape=[_sds((t_rows, D), F32), _sds((t_rows, D), BF16), _sds((nb, 1, D), F32), _sds((nb, 1, 128), F32)],
        compiler_params=_params(("parallel", "arbitrary")),
    )(y16, w_out16, x2, tgt, mod3)


def _bwd_merge(dout16, w_out16, px, a_ret, a_att, tm):
    t_rows = dout16.shape[0]
    hw = D // 2

    def body(do_ref, w_ref, mr0, mr1, ma0, ma1, ar_ref, aa_ref, dar_ref, daa_ref, dmr_ref, dma_ref):
        dy_all = _dot(do_ref[...], w_ref[...], NT)
        for j, (mr_ref, ma_ref) in enumerate(((mr0, ma0), (mr1, ma1))):
            sl = slice(j * hw, (j + 1) * hw)
            dy = dy_all[:, sl]
            sr = _sig(mr_ref[...].astype(F32))
            sa = _sig(ma_ref[...].astype(F32))
            dar_ref[:, sl] = (dy * sr).astype(BF16)
            daa_ref[:, sl] = (dy * sa).astype(BF16)
            dmr_ref[:, sl] = (dy * ar_ref[:, sl].astype(F32) * sr * (1.0 - sr)).astype(BF16)
            dma_ref[:, sl] = (dy * aa_ref[:, sl].astype(F32) * sa * (1.0 - sa)).astype(BF16)

    row = pl.BlockSpec((tm, D), lambda i: (i, 0))
    return pl.pallas_call(
        body, name="bwd_merge", grid=(t_rows // tm,),
        in_specs=[row, pl.BlockSpec((D, D), lambda i: (0, 0))] + _gate_specs(tm, MR) + _gate_specs(tm, MA) + [row, row],
        out_specs=[row] * 4, out_shape=[_sds((t_rows, D), BF16)] * 4,
        compiler_params=_params(("parallel",)),
    )(dout16, w_out16, px, px, px, px, a_ret, a_att)


def _bwd_branch_ret(da_ret16, w_o_ret16, px, o_f, o_b, tm, after=()):
    t_rows = da_ret16.shape[0]

    def body(da_ref, w_ref, g0, g1, g2, g3, of_ref, ob_ref, *rest):
        do_ref, dg_ref = rest[-2:]
        da = da_ref[...]
        for h, g_ref in enumerate((g0, g1, g2, g3)):
            sl = slice(h * DV, (h + 1) * DV)
            dy = _dot(da, w_ref[sl, :], NT)
            g = g_ref[...].astype(F32)
            o = of_ref[:, sl].astype(F32) + ob_ref[:, sl].astype(F32)
            r = lax.rsqrt(jnp.mean(o * o, axis=-1, keepdims=True) + EPS)
            on = o * r
            don = dy * _silu(g)
            dg_ref[:, sl] = (dy * on * _dsilu(g)).astype(BF16)
            do_ref[:, sl] = (r * (don - on * jnp.mean(on * don, axis=-1, keepdims=True))).astype(BF16)

    def gate(h):
        return pl.BlockSpec((tm, DV), lambda i: (i, RG // DV + h))

    wide = pl.BlockSpec((tm, RH * DV), lambda i: (i, 0))
    return pl.pallas_call(
        body, name="bwd_branch_ret", grid=(t_rows // tm,),
        in_specs=[pl.BlockSpec((tm, D), lambda i: (i, 0)), pl.BlockSpec((RH * DV, D), lambda i: (0, 0))]
        + [gate(h) for h in range(RH)] + [wide, wide] + [pl.BlockSpec(memory_space=pl.ANY)] * len(after),
        out_specs=[wide, wide], out_shape=[_sds((t_rows, RH * DV), BF16)] * 2,
        compiler_params=_params(("parallel",)),
    )(da_ret16, w_o_ret16, *([px] * RH), o_f, o_b, *after)


def _bwd_branch_att(da_att16, w_o_att16, px, o_att, tm):
    t_rows = da_att16.shape[0]
    hw = D // 2

    def body(da_ref, w_ref, g0, g1, o_ref, dao_ref, dg_ref):
        dy_all = _dot(da_ref[...], w_ref[...], NT)
        for j, g_ref in enumerate((g0, g1)):
            sl = slice(j * hw, (j + 1) * hw)
            dy = dy_all[:, sl]
            g = g_ref[...].astype(F32)
            dao_ref[:, sl] = dy * _silu(g)
            dg_ref[:, sl] = (dy * o_ref[:, sl] * _dsilu(g)).astype(BF16)

    row = pl.BlockSpec((tm, D), lambda i: (i, 0))
    return pl.pallas_call(
        body, name="bwd_branch_att", grid=(t_rows // tm,),
        in_specs=[row, pl.BlockSpec((D, D), lambda i: (0, 0))] + _gate_specs(tm, AG) + [row],
        out_specs=[row, row], out_shape=[_sds((t_rows, D), F32), _sds((t_rows, D), BF16)],
        compiler_params=_params(("parallel",)),
    )(da_att16, w_o_att16, px, px, o_att)


def _att_bwd(q16, kx16, kc16, px, dao, o_att, lse, nb, seq, cx, tq):
    t_rows = nb * seq
    nq = seq // tq
    rep = HQ // HKV
    gw = rep * HD
    scale = HD ** -0.5

    def body(q_ref, kx_ref, kc_ref, vx_ref, vc_ref, dao_ref, o_ref, l_ref, dq_ref, dkx_ref, dvx_ref, dkc_ref, dvc_ref):
        i = pl.program_id(2)
        kx = kx_ref[...]
        kc = kc_ref[...]
        vx = vx_ref[...].astype(BF16)
        vc = vc_ref[...].astype(BF16)
        dkx = jnp.zeros((seq, HD), F32)
        dvx = jnp.zeros((seq, HD), F32)
        dkc = jnp.zeros((cx, HD), F32)
        dvc = jnp.zeros((cx, HD), F32)
        for r in range(rep):
            sl = slice(r * HD, (r + 1) * HD)
            q = q_ref[:, sl]
            lr = l_ref[:, r:r + 1]
            p1 = jnp.exp2(_dot(q, kx, NT) * SM_C - lr)
            p2 = jnp.exp2(_dot(q, kc, NT) * SM_C - lr)
            da = dao_ref[:, sl]
            da16 = da.astype(BF16)
            delta = jnp.sum(da * o_ref[:, sl], axis=-1, keepdims=True)
            ds1 = (p1 * (_dot(da16, vx, NT) - delta)).astype(BF16)
            ds2 = (p2 * (_dot(da16, vc, NT) - delta)).astype(BF16)
            dq_ref[:, sl] = (_dot(ds1, kx) + _dot(ds2, kc)) * scale
            dkx += _dot(ds1, q, TN)
            dkc += _dot(ds2, q, TN)
            dvx += _dot(p1.astype(BF16), da16, TN)
            dvc += _dot(p2.astype(BF16), da16, TN)
        dkx = dkx * scale
        dkc = dkc * scale

        @pl.when(i == 0)
        def _():
            dkx_ref[...] = dkx
            dvx_ref[...] = dvx
            dkc_ref[...] = dkc
            dvc_ref[...] = dvc

        @pl.when(i > 0)
        def _():
            dkx_ref[...] += dkx
            dvx_ref[...] += dvx
            dkc_ref[...] += dkc
            dvc_ref[...] += dvc

    qblk = pl.BlockSpec((tq, gw), lambda b, g, i: (b * nq + i, g))
    kxb = pl.BlockSpec((None, seq, HD), lambda b, g, i: (b, 0, g))
    kcb = pl.BlockSpec((None, cx, HD), lambda b, g, i: (b, 0, g))
    return pl.pallas_call(
        body, name="att_bwd", grid=(nb, HKV, nq),
        in_specs=[qblk,
                  pl.BlockSpec((seq, HD), lambda b, g, i: (b, g)),
                  pl.BlockSpec((cx, HD), lambda b, g, i: (b, g)),
                  pl.BlockSpec((seq, HD), lambda b, g, i: (b, AV // HD + g)),
                  pl.BlockSpec((cx, HD), lambda b, g, i: (t_rows // cx + b, AV // HD + g)),
                  qblk, qblk, pl.BlockSpec((tq, 128), lambda b, g, i: (b * nq + i, g))],
        out_specs=[qblk, kxb, kxb, kcb, kcb],
        out_shape=[_sds((t_rows, D), F32), _sds((nb, seq, HKV * HD), F32), _sds((nb, seq, HKV * HD), F32),
                   _sds((nb, cx, HKV * HD), F32), _sds((nb, cx, HKV * HD), F32)],
        compiler_params=_params(("parallel", "parallel", "arbitrary")),
    )(q16, kx16, kc16, px, px, dao, o_att, lse)


def _qk_prep_bwd(dt, px, nw, cos, sin, rows, row_off, col_off, heads, hb, seq, tm, name):
    rope = cos is not None
    rb0 = row_off // tm
    pb = seq // tm if rope else 1
    bw = hb * HD

    def body(*refs):
        if rope:
            d_ref, x_ref, w_ref, c_ref, s_ref, dx_ref, dw_ref = refs
        else:
            d_ref, x_ref, w_ref, dx_ref, dw_ref = refs
        first = jnp.logical_and(pl.program_id(0) == 0, pl.program_id(1) == 0)
        dw = jnp.zeros((1, HD), F32)
        for h in range(hb):
            sl = slice(h * HD, (h + 1) * HD)
            dtv = d_ref[:, sl]
            if rope:
                dtv = dtv * c_ref[...] + _swap_pairs(dtv * s_ref[...])
            xv = x_ref[:, sl].astype(F32)
            r = lax.rsqrt(jnp.mean(xv * xv, axis=-1, keepdims=True) + EPS)
            xh = xv * r
            dxh = dtv * w_ref[...]
            dx_ref[:, sl] = (r * (dxh - xh * jnp.mean(dxh * xh, axis=-1, keepdims=True))).astype(BF16)
            dw += jnp.sum(dtv * xh, axis=0, keepdims=True)

        @pl.when(first)
        def _():
            dw_ref[...] = dw

        @pl.when(jnp.logical_not(first))
        def _():
            dw_ref[...] += dw

    blk = pl.BlockSpec((tm, bw), lambda i, j: (i, j))
    in_specs = [blk, pl.BlockSpec((tm, bw), lambda i, j: (rb0 + i, col_off // bw + j)),
                pl.BlockSpec((1, HD), lambda i, j: (0, 0))]
    args = [dt, px, nw]
    if rope:
        in_specs += [pl.BlockSpec((tm, HD), lambda i, j: (i % pb, 0))] * 2
        args += [cos, sin]
    return pl.pallas_call(
        body, name=name, grid=(rows // tm, heads // hb), in_specs=in_specs,
        out_specs=[blk, pl.BlockSpec((1, HD), lambda i, j: (0, 0))],
        out_shape=[_sds((rows, heads * HD), BF16), _sds((1, HD), F32)],
        compiler_params=_params(("arbitrary", "arbitrary")),
    )(*args)


def _ret_bwd(px, lg, do16, hist_f, hist_b, nb, nc):
    t_rows = nb * nc * CH

    def body(lg_ref, *refs):
        ins = (refs[0:5], refs[7:12])
        do_refs = (refs[5], refs[12])
        h_refs = (refs[6], refs[13])
        outs = (refs[14:17], refs[17:20])
        ds_outs = (refs[20], refs[21])
        dlg_ref = refs[22]
        dss = (refs[23], refs[24])
        c = pl.program_id(1)

        @pl.when(c == 0)
        def _():
            dss[0][...] = jnp.zeros_like(dss[0])
            dss[1][...] = jnp.zeros_like(dss[1])
            dlg_ref[...] = jnp.zeros_like(dlg_ref)

        for d in range(2):
            dq_ref, dk_ref, dv_ref = outs[d]
            for h in range(RH):
                lg_d = lg_ref[d, h]
                mask, relf, qd, qe, kd, ke = _decays(lg_d, d == 0)
                g_ch = jnp.exp(lg_d * CH)
                q, k, v16 = _head_qkv(ins[d], h)
                q16 = q.astype(BF16)
                k16 = k.astype(BF16)
                do16v = do_refs[d][:, h * DV:(h + 1) * DV]
                st16 = h_refs[d][h]
                dst = dss[d][h]
                dst16 = dst.astype(BF16)
                a = _dot(q16, k16, NT) * mask
                dp = _dot(do16v, v16, NT)
                da16 = (dp * mask).astype(BF16)
                dq_cross = _dot(do16v, st16, NT) * qd
                dq_ref[:, h * DK:(h + 1) * DK] = (_dot(da16, k16) + dq_cross).astype(BF16)
                dk_state = _dot(v16, dst16, NT) * kd
                dk_ref[:, h * DK:(h + 1) * DK] = ((_dot(da16, q16, TN) + dk_state) * (DK ** -0.5)).astype(BF16)
                dv = _dot(a.astype(BF16), do16v, TN) + _dot((k * kd).astype(BF16), dst16)
                dv_ref[:, h * DV:(h + 1) * DV] = dv.astype(BF16)
                dlg = (jnp.sum(relf * a * dp)
                       + jnp.sum(qe * jnp.sum(q * dq_cross, axis=-1, keepdims=True))
                       + jnp.sum(ke * jnp.sum(k * dk_state, axis=-1, keepdims=True))
                       + CH * g_ch * jnp.sum(dst * st16.astype(F32)))
                row = d * RH + h
                dlg_ref[row:row + 1, :] += jnp.broadcast_to(dlg, (1, 128))
                ds_new = g_ch * dst + _dot((q * qd).astype(BF16), do16v, TN)
                dss[d][h] = ds_new

                @pl.when(c == nc - 1)
                def _():
                    ds_outs[d][h] = ds_new

    def fw(b, c):
        return b * nc + nc - 1 - c

    def bw(b, c):
        return b * nc + c

    def rows(rowf, width):
        return pl.BlockSpec((CH, width), lambda b, c: (rowf(b, c), 0))

    def hist(rowf):
        return pl.BlockSpec((None, None, RH, DK, DV), lambda b, c: (b, rowf(0, c), 0, 0, 0))

    st = pl.BlockSpec((None, RH, DK, DV), lambda b, c: (b, 0, 0, 0))
    in_specs = [pl.BlockSpec(memory_space=pltpu.SMEM)]
    out_specs = []
    for rowf in (fw, bw):
        in_specs += _wide_specs(rowf) + [rows(rowf, RH * DV), hist(rowf)]
        out_specs += [rows(rowf, RH * DK), rows(rowf, RH * DK), rows(rowf, RH * DV)]
    out_specs += [st, st, pl.BlockSpec((None, 8, 128), lambda b, c: (b, 0, 0))]
    qk = _sds((t_rows, RH * DK), BF16)
    vv = _sds((t_rows, RH * DV), BF16)
    return pl.pallas_call(
        body, name="ret_bwd", grid=(nb, nc), in_specs=in_specs, out_specs=out_specs,
        out_shape=[qk, qk, vv, qk, qk, vv, _sds((nb, RH, DK, DV), F32), _sds((nb, RH, DK, DV), F32),
                   _sds((nb, 8, 128), F32)],
        scratch_shapes=[pltpu.VMEM((RH, DK, DV), F32), pltpu.VMEM((RH, DK, DV), F32)],
        compiler_params=_params(("parallel", "arbitrary")),
    )(lg, *([px] * 5), do16, hist_f, *([px] * 5), do16, hist_b)


def _ctx_state_bwd(px, lg, ds_f, ds_b, nb, t_rows, cx):
    rb = t_rows // cx

    def body(lg_ref, k_ref, v_ref, dsf_ref, dsb_ref, dk_ref, dv_ref, dlg_ref):
        h = pl.program_id(1)
        pos = lax.broadcasted_iota(jnp.int32, (cx, 1), 0).astype(F32)
        k = k_ref[...].astype(F32) * (DK ** -0.5)
        v16 = v_ref[...].astype(BF16)
        dk = jnp.zeros((cx, DK), F32)
        dv = jnp.zeros((cx, DV), F32)
        dlg_ref[...] = jnp.zeros_like(dlg_ref)
        for d, (ds_ref, e) in enumerate(((dsf_ref, cx - 1.0 - pos), (dsb_ref, pos))):
            w = jnp.exp(lg_ref[d, h] * e)
            ds16 = ds_ref[...].astype(BF16)
            t = _dot(v16, ds16, NT)
            dk += t * w
            dv += _dot((k * w).astype(BF16), ds16)
            dlg = jnp.sum(e * w * jnp.sum(k * t, axis=-1, keepdims=True))
            dlg_ref[d:d + 1, :] = jnp.broadcast_to(dlg, (1, 128))
        dk_ref[...] = (dk * (DK ** -0.5)).astype(BF16)
        dv_ref[...] = dv.astype(BF16)

    st = pl.BlockSpec((None, None, DK, DV), lambda b, h: (b, h, 0, 0))
    return pl.pallas_call(
        body, name="ctx_state_bwd", grid=(nb, RH),
        in_specs=[pl.BlockSpec(memory_space=pltpu.SMEM),
                  pl.BlockSpec((cx, DK), lambda b, h: (rb + b, RK // DK + h)),
                  pl.BlockSpec((cx, DV), lambda b, h: (rb + b, RV // DV + h)), st, st],
        out_specs=[pl.BlockSpec((cx, DK), lambda b, h: (b, h)), pl.BlockSpec((cx, DV), lambda b, h: (b, h)),
                   pl.BlockSpec((None, None, 8, 128), lambda b, h: (b, h, 0, 0))],
        out_shape=[_sds((nb * cx, RH * DK), BF16), _sds((nb * cx, RH * DV), BF16), _sds((nb, RH, 8, 128), F32)],
        compiler_params=_params(("parallel", "parallel")),
    )(lg, px, px, ds_f, ds_b)


def _assemble_lat(rows_all, dk_f, dk_b, dv_f, dv_b, dak16, dvx, dq_f, dq_b, drg16, daq16, dag16, dmr16, dma16, tm):
    t_rows = dk_f.shape[0]

    def body(dkf, dkb, dvf, dvb, dak, dav, dqf, dqb, drg, daq, dag, dmr, dma, o_ref):
        o_ref[:, RK:RK + RH * DK] = (dkf[...].astype(F32) + dkb[...].astype(F32)).astype(BF16)
        o_ref[:, RV:RV + RH * DV] = (dvf[...].astype(F32) + dvb[...].astype(F32)).astype(BF16)
        o_ref[:, AK:AK + HKV * HD] = dak[...]
        o_ref[:, AV:AV + HKV * HD] = dav[...].astype(BF16)
        o_ref[:, RQ:RQ + RH * DK] = (dqf[...].astype(F32) + dqb[...].astype(F32)).astype(BF16)
        o_ref[:, RG:RG + RH * DV] = drg[...]
        o_ref[:, AQ:AQ + D] = daq[...]
        o_ref[:, AG:AG + D] = dag[...]
        o_ref[:, MR:MR + D] = dmr[...]
        o_ref[:, MA:MA + D] = dma[...]

    args = (dk_f, dk_b, dv_f, dv_b, dak16, dvx, dq_f, dq_b, drg16, daq16, dag16, dmr16, dma16)
    return pl.pallas_call(
        body, name="assemble_lat", grid=(t_rows // tm,),
        in_specs=[pl.BlockSpec((tm, a.shape[1]), lambda i: (i, 0)) for a in args],
        out_specs=pl.BlockSpec((tm, IN_COLS), lambda i: (i, 0)), out_shape=_sds((rows_all, IN_COLS), BF16),
        compiler_params=_params(("parallel",)),
    )(*args)


def _assemble_ctx(dp_all, dck16, dcv16, dcak16, dvc, t_rows, tm):
    c_rows = dck16.shape[0]
    rb = t_rows // tm

    def body(_, dck, dcv, dcak, dcav, o_ref):
        o_ref[:, RK:RK + RH * DK] = dck[...]
        o_ref[:, RV:RV + RH * DV] = dcv[...]
        o_ref[:, AK:AK + HKV * HD] = dcak[...]
        o_ref[:, AV:AV + HKV * HD] = dcav[...].astype(BF16)
        o_ref[:, KV_COLS:] = jnp.zeros((tm, IN_COLS - KV_COLS), BF16)

    args = (dck16, dcv16, dcak16, dvc)
    return pl.pallas_call(
        body, name="assemble_ctx", grid=(c_rows // tm,),
        in_specs=[pl.BlockSpec(memory_space=pl.ANY)]
        + [pl.BlockSpec((tm, a.shape[1]), lambda i: (i, 0)) for a in args],
        out_specs=pl.BlockSpec((tm, IN_COLS), lambda i: (rb + i, 0)), out_shape=_sds(dp_all.shape, BF16),
        input_output_aliases={0: 0},
        compiler_params=_params(("parallel",)),
    )(dp_all, *args)


def _norm_bwd(dh, x2, mod3, norm_w, dxn, row_off, rows_per_group, group0, tm, name):
    with_dx = dxn is not None
    rows = x2.shape[0]
    rb0 = row_off // tm
    bpg = rows_per_group // tm
    ngroups = rows // rows_per_group

    def body(*refs):
        if with_dx:
            dh_ref, x_ref, sc_ref, nw_ref, dxn_ref, dx_ref, dsh_ref, dsc_ref, dnw_ref = refs
        else:
            dh_ref, x_ref, sc_ref, nw_ref, dsh_ref, dsc_ref, dnw_ref = refs
        i = pl.program_id(0)
        dhv = dh_ref[...]
        xv = x_ref[...]
        nw = nw_ref[...]
        r = lax.rsqrt(jnp.mean(xv * xv, axis=-1, keepdims=True) + EPS)
        xh = xv * r
        dm = dhv * (1.0 + sc_ref[...])
        dsh = jnp.sum(dhv, axis=0, keepdims=True)
        dsc = jnp.sum(dhv * (xh * nw), axis=0, keepdims=True)
        dnw = jnp.sum(dm * xh, axis=0, keepdims=True)
        if with_dx:
            dxh = dm * nw
            dx_ref[...] = dxn_ref[...] + r * (dxh - xh * jnp.mean(dxh * xh, axis=-1, keepdims=True))

        @pl.when(i % bpg == 0)
        def _():
            dsh_ref[...] = dsh
            dsc_ref[...] = dsc

        @pl.when(i % bpg != 0)
        def _():
            dsh_ref[...] += dsh
            dsc_ref[...] += dsc

        @pl.when(i == 0)
        def _():
            dnw_ref[...] = dnw

        @pl.when(i > 0)
        def _():
            dnw_ref[...] += dnw

    grp = pl.BlockSpec((None, 1, D), lambda i: (i // bpg, 0, 0))
    in_specs = [pl.BlockSpec((tm, D), lambda i: (rb0 + i, 0)), pl.BlockSpec((tm, D), lambda i: (i, 0)),
                pl.BlockSpec((None, 1, D), lambda i: (group0 + i // bpg, 0, 1)),
                pl.BlockSpec((1, D), lambda i: (0, 0))]
    args = [dh, x2, mod3, norm_w]
    out_specs = [grp, grp, pl.BlockSpec((1, D), lambda i: (0, 0))]
    out_shape = [_sds((ngroups, 1, D), F32), _sds((ngroups, 1, D), F32), _sds((1, D), F32)]
    if with_dx:
        in_specs.append(pl.BlockSpec((tm, D), lambda i: (i, 0)))
        args.append(dxn)
        out_specs.insert(0, pl.BlockSpec((tm, D), lambda i: (i, 0)))
        out_shape.insert(0, _sds((rows, D), F32))
    return pl.pallas_call(
        body, name=name, grid=(rows // tm,), in_specs=in_specs, out_specs=out_specs, out_shape=out_shape,
        compiler_params=_params(("arbitrary",)),
    )(*args)


def _small_final(dmod_all, dmodc_parts, c_rows, dm_loc_rows, nw_parts, misc_parts, c_ctx, r_pad, w_ada16):
    loc = dm_loc_rows.shape[1]

    def body(dm_ref, dmc_ref, c_ref, dml_ref, nwp_ref, mp_ref, cc_ref, r_ref, w_ref,
             gb_ref, gc_ref, gnw_ref, misc_ref, gwa_ref):
        dmc = jnp.sum(dmc_ref[...], axis=0, keepdims=True)
        gb_ref[...] = jnp.sum(dm_ref[...], axis=0, keepdims=True) + dmc
        dsc = _dot(jnp.broadcast_to(dmc, (8, 3 * D)).astype(BF16), w_ref[...], NT)[0:1, :]
        gc_ref[...] = dsc * _dsilu(cc_ref[...])
        gnw_ref[...] = jnp.sum(nwp_ref[...], axis=0, keepdims=True)
        misc = jnp.sum(mp_ref[...], axis=0, keepdims=True)
        y = jnp.exp2(r_ref[...])
        lane = lax.broadcasted_iota(jnp.int32, (1, D), 1)
        is_decay = jnp.logical_and(lane >= 2 * HD, lane < 2 * HD + 2 * RH)
        misc_ref[...] = misc * jnp.where(is_decay, -(y * np.float32(np.log(2.0))) / (1.0 - y), 1.0)
        gwa_ref[...] = _dot(_silu(c_ref[...]).astype(BF16), dml_ref[...].astype(BF16), TN)

    return pl.pallas_call(
        body, name="small_final",
        out_shape=[_sds((1, 3 * D), F32), _sds((1, D), F32), _sds((1, D), F32), _sds((1, D), F32), _sds((D, loc), F32)],
        compiler_params=pltpu.CompilerParams(vmem_limit_bytes=VMEM_LIMIT),
    )(dmod_all, dmodc_parts, c_rows, dm_loc_rows, nw_parts, misc_parts, c_ctx, r_pad, w_ada16)


def _adamw(w, g, m, v, name):
    rows, cols = w.shape
    tm = _pick(rows, 256, 8)
    bc1 = 1.0 - B1 ** STEP
    bc2 = 1.0 - B2 ** STEP

    def body(w_ref, g_ref, m_ref, v_ref, d_ref, nm_ref, nv_ref):
        g_ = g_ref[...]
        nm = B1 * m_ref[...] + (1.0 - B1) * g_
        nv = B2 * v_ref[...] + (1.0 - B2) * (g_ * g_)
        nm_ref[...] = nm
        nv_ref[...] = nv
        d_ref[...] = -LR * ((nm / bc1) / (jnp.sqrt(nv / bc2) + ADAM_EPS) + WD * w_ref[...])

    blk = pl.BlockSpec((tm, cols), lambda i: (i, 0))
    return pl.pallas_call(
        body, name=name, grid=(rows // tm,), in_specs=[blk] * 4, out_specs=[blk] * 3,
        out_shape=[_sds((rows, cols), F32)] * 3, compiler_params=_params(("parallel",)),
    )(w, g, m, v)


def _mesh_pos():
    return lax.axis_index("x"), lax.axis_index("y"), lax.axis_index("c")


def _all_gather(arrs, name):
    n = len(arrs)

    def body(*refs):
        ins, outs = refs[:n], refs[n:2 * n]
        send_sems, recv_sems, local_sems = refs[2 * n:]
        x, y, c = _mesh_pos()
        me, sib = (x, y, c), (x, y, 1 - c)
        chips = [(1 - x, y), (x, 1 - y), (1 - x, 1 - y)]

        def slot(p):
            return 4 * p[0] + 2 * p[1] + p[2]

        def copy(a, k, block, to, own):
            dst = outs[a].at[slot(block)]
            return pltpu.make_async_remote_copy(
                src_ref=ins[a] if own else dst, dst_ref=dst, send_sem=send_sems.at[a, k], recv_sem=recv_sems.at[a, k],
                device_id=to, device_id_type=MESH_T)

        mine = [pltpu.make_async_copy(ins[a], outs[a].at[slot(me)], local_sems.at[a]) for a in range(n)]
        for cp in mine:
            cp.start()
        first = []
        for a in range(n):
            first.append(copy(a, 0, me, sib, True))
            first += [copy(a, 1 + j, me, (*chip, c), True) for j, chip in enumerate(chips)]
        for cp in first:
            cp.start()
        passed = []
        for j, chip in enumerate(chips):
            for a in range(n):
                copy(a, 1 + j, (*chip, c), me, False).wait_recv()
                fwd = copy(a, 4 + j, (*chip, c), sib, False)
                fwd.start()
                passed.append(fwd)
        for a in range(n):
            copy(a, 0, sib, me, False).wait_recv()
            for j, chip in enumerate(chips):
                copy(a, 4 + j, (*chip, 1 - c), me, False).wait_recv()
        for cp in first + passed:
            cp.wait_send()
        for cp in mine:
            cp.wait()

    hbm = pl.BlockSpec(memory_space=pl.ANY)
    return pl.pallas_call(
        body, name=name, in_specs=[hbm] * n, out_specs=[hbm] * n,
        out_shape=[_sds((N_DEV,) + a.shape, a.dtype) for a in arrs],
        scratch_shapes=[pltpu.SemaphoreType.DMA((n, 7)), pltpu.SemaphoreType.DMA((n, 7)), pltpu.SemaphoreType.DMA((n,))],
    )(*arrs)


def _pair_exchange(arrs, name):
    n = len(arrs)

    def body(*refs):
        ins, outs = refs[:n], refs[n:2 * n]
        send_sems, recv_sems = refs[2 * n:]
        x, y, c = _mesh_pos()
        sib = (x, y, 1 - c)
        sends = []
        for a in range(n):
            for k in range(4):
                sends.append(pltpu.make_async_remote_copy(
                    src_ref=ins[a].at[2 * k + 1 - c], dst_ref=outs[a].at[k], send_sem=send_sems.at[a, k],
                    recv_sem=recv_sems.at[a, k], device_id=sib, device_id_type=MESH_T))
        for cp in sends:
            cp.start()
        for cp in sends:
            cp.wait_recv()
        for cp in sends:
            cp.wait_send()

    hbm = pl.BlockSpec(memory_space=pl.ANY)
    return pl.pallas_call(
        body, name=name, in_specs=[hbm] * n, out_specs=[hbm] * n,
        out_shape=[_sds((4,) + a.shape[1:], a.dtype) for a in arrs],
        scratch_shapes=[pltpu.SemaphoreType.DMA((n, 4)), pltpu.SemaphoreType.DMA((n, 4))],
    )(*arrs)


def _pair_add(part, got, core, name):
    _, rows, cols = part.shape
    tm = _pick(rows, 256, 16)
    p4 = part.reshape(4, 2, rows, cols)

    def body(core_ref, p_ref, g_ref, o_ref):
        o_ref[...] = (p_ref[...].astype(F32) + g_ref[...].astype(F32)).astype(BF16)

    blk = pl.BlockSpec((None, tm, cols), lambda k, i, cr: (k, i, 0))
    return pl.pallas_call(
        body, name=name,
        grid_spec=pltpu.PrefetchScalarGridSpec(
            num_scalar_prefetch=1, grid=(4, rows // tm),
            in_specs=[pl.BlockSpec((None, None, tm, cols), lambda k, i, cr: (k, cr[0], i, 0)), blk], out_specs=blk),
        out_shape=_sds((4, rows, cols), BF16), compiler_params=_params(("parallel", "parallel")),
    )(core, p4, got)


def _chip_sum(pair_sums, landed, chip, name):
    _, rows, cols = pair_sums.shape
    tm = _pick(rows, 256, 16)

    def body(chip_ref, s_ref, l_ref, o_ref):
        acc = s_ref[...].astype(F32)
        for j in range(3):
            acc = acc + l_ref[j].astype(F32)
        o_ref[...] = acc

    return pl.pallas_call(
        body, name=name,
        grid_spec=pltpu.PrefetchScalarGridSpec(
            num_scalar_prefetch=1, grid=(rows // tm,),
            in_specs=[pl.BlockSpec((None, tm, cols), lambda i, ch: (ch[0], i, 0)),
                      pl.BlockSpec((3, tm, cols), lambda i, ch: (0, i, 0))],
            out_specs=pl.BlockSpec((tm, cols), lambda i, ch: (i, 0))),
        out_shape=_sds((rows, cols), F32), compiler_params=_params(("parallel",)),
    )(chip, pair_sums, landed)


_HBM = pl.BlockSpec(memory_space=pltpu.HBM)
_SEM = pl.BlockSpec(memory_space=pltpu.SEMAPHORE)
_EFFECT = pltpu.SideEffectType.DATAFLOW_SIDE_EFFECTING


def _chip_routes(n):
    def plan(x, y, c):
        routes = []
        for a in range(n):
            for j in range(1, 4):
                px, py = x ^ (j >> 1), y ^ (j & 1)
                routes.append((a, 2 * px + py, (px, py, c), j - 1))
        return routes
    return plan, 3 * n


def _bcast_routes(n):
    def plan(x, y, c):
        routes = []
        for a in range(n):
            for k in range(1, N_DEV):
                peer = (x ^ ((k >> 2) & 1), y ^ ((k >> 1) & 1), c ^ (k & 1))
                routes.append((a, 0, peer, 4 * x + 2 * y + c))
        return routes
    return plan, 7 * n


def _route_copies(srcs, lands, send_sems, recv_sems, routes):
    return [pltpu.make_async_remote_copy(
        src_ref=srcs[a].at[sb], dst_ref=lands[a].at[lb], send_sem=send_sems.at[r], recv_sem=recv_sems.at[r],
        device_id=peer, device_id_type=MESH_T) for r, (a, sb, peer, lb) in enumerate(routes)]


def _exchange_start(srcs, lands, routes, name, after=()):
    plan, count = routes
    n = len(srcs)
    n_in = 2 * n + len(after)

    def body(*refs):
        send_sems, recv_sems = refs[n_in], refs[n_in + 1]
        token = refs[-1]
        for cp in _route_copies(refs[:n], refs[n:2 * n], send_sems, recv_sems, plan(*_mesh_pos())):
            cp.start()
        token[...] = jnp.zeros_like(token)

    args = [pltpu.with_memory_space_constraint(a, pltpu.HBM) for a in list(srcs) + list(lands)]
    out = pl.pallas_call(
        body, name=name,
        out_shape=(pltpu.SemaphoreType.DMA((count,)), pltpu.SemaphoreType.DMA((count,)),
                   *[pltpu.HBM(a.shape, a.dtype) for a in args], _sds((8, 128), F32)),
        in_specs=[_HBM] * (2 * n) + [pl.BlockSpec(memory_space=pl.ANY)] * len(after),
        out_specs=(_SEM, _SEM, *([_HBM] * (2 * n)), pl.BlockSpec(memory_space=pltpu.VMEM)),
        input_output_aliases={i: 2 + i for i in range(2 * n)},
        compiler_params=pltpu.CompilerParams(has_side_effects=_EFFECT),
    )(*args, *after)
    return (out[0], out[1], list(out[2:2 + 2 * n]), routes), out[-1]


def _exchange_wait_some(state, after, only, name):
    send_sems, recv_sems, bufs, (plan, count) = state
    n = len(bufs) // 2

    def body(*refs):
        send_s, recv_s = refs[2 * n], refs[2 * n + 1]
        for r, cp in enumerate(_route_copies(refs[:n], refs[n:2 * n], send_s, recv_s, plan(*_mesh_pos()))):
            if only is None or r in only:
                cp.wait_send()
                cp.wait_recv()

    out = pl.pallas_call(
        body, name=name, out_shape=tuple(pltpu.HBM(a.shape, a.dtype) for a in bufs),
        in_specs=[_HBM] * (2 * n) + [_SEM, _SEM, pl.BlockSpec(memory_space=pl.ANY)], out_specs=tuple([_HBM] * (2 * n)),
        input_output_aliases={i: i for i in range(2 * n)},
        compiler_params=pltpu.CompilerParams(has_side_effects=_EFFECT),
    )(*bufs, send_sems, recv_sems, after)
    return (send_sems, recv_sems, list(out), (plan, count)), list(out[:n]), list(out[n:])


def _exchange_wait(state, after, name):
    _, srcs, lands = _exchange_wait_some(state, after, None, name)
    return srcs, lands


def _group_routes(js):
    def plan(x, y, c):
        return [(0, 0, (x ^ (j >> 1), y ^ (j & 1), c), 2 * j + c) for j in js]
    return plan, len(js)


def _pair_fill(groups, j, name, after=()):
    def body(*refs):
        g_ref, send_sem, recv_sem = refs[-3:]
        x, y, c = _mesh_pos()
        mine = g_ref.at[2 * j + c]
        to_sib = pltpu.make_async_remote_copy(src_ref=mine, dst_ref=mine, send_sem=send_sem, recv_sem=recv_sem,
                                              device_id=(x, y, 1 - c), device_id_type=MESH_T)
        to_sib.start()
        pltpu.make_async_remote_copy(src_ref=mine, dst_ref=g_ref.at[2 * j + 1 - c], send_sem=send_sem, recv_sem=recv_sem,
                                     device_id=(x, y, 1 - c), device_id_type=MESH_T).wait_recv()
        to_sib.wait_send()

    hbm = pl.BlockSpec(memory_space=pl.ANY)
    return pl.pallas_call(
        body, name=name, in_specs=[hbm] * (1 + len(after)), out_specs=hbm, out_shape=_sds(groups.shape, groups.dtype),
        input_output_aliases={0: 0},
        scratch_shapes=[pltpu.SemaphoreType.DMA, pltpu.SemaphoreType.DMA],
    )(groups, *after)


def _in_proj_group(h_all, groups, j, chip, px_prev, after, name):
    rows_all = h_all.shape[0]
    gcols = IN_COLS // 4
    tm = _pick(rows_all, 1536, 128)
    g4 = groups.reshape(4, gcols, D)

    n_lead = (1 if px_prev is not None else 0) + len(after)
    lead = ([px_prev] if px_prev is not None else []) + list(after)

    def body(chip_ref, *refs):
        h_ref, w_ref, o_ref = refs[n_lead:]
        o_ref[...] = _dot(h_ref[...], w_ref[...], NT).astype(BF16)
    return pl.pallas_call(
        body, name=name,
        grid_spec=pltpu.PrefetchScalarGridSpec(
            num_scalar_prefetch=1, grid=(rows_all // tm,),
            in_specs=[pl.BlockSpec(memory_space=pl.ANY)] * n_lead
            + [pl.BlockSpec((tm, D), lambda i, ch: (i, 0)), pl.BlockSpec((None, gcols, D), lambda i, ch: (j, 0, 0))],
            out_specs=pl.BlockSpec((tm, gcols), lambda i, ch: (i, ch[0] ^ j))),
        out_shape=_sds((rows_all, IN_COLS), BF16),
        input_output_aliases={1: 0} if px_prev is not None else {},
        compiler_params=_params(("parallel",)),
    )(chip, *lead, h_all, g4)


def _d_h_groups(dp_all, groups, chip, after):
    rows_all = dp_all.shape[0]
    gcols = IN_COLS // 4
    tm = _pick(rows_all, 1536, 128)
    g4 = groups.reshape(4, gcols, D)
    n_lead = len(after)

    def body(chip_ref, *refs):
        a_ref, w_ref, o_ref = refs[n_lead:]
        j = pl.program_id(1)
        part = _dot(a_ref[...], w_ref[...])

        @pl.when(j == 0)
        def _():
            o_ref[...] = part

        @pl.when(j > 0)
        def _():
            o_ref[...] += part

    return pl.pallas_call(
        body, name="d_h",
        grid_spec=pltpu.PrefetchScalarGridSpec(
            num_scalar_prefetch=1, grid=(rows_all // tm, 4),
            in_specs=[pl.BlockSpec(memory_space=pl.ANY)] * n_lead
            + [pl.BlockSpec((tm, gcols), lambda i, j, ch: (i, ch[0] ^ j)),
               pl.BlockSpec((None, gcols, D), lambda i, j, ch: (j, 0, 0))],
            out_specs=pl.BlockSpec((tm, D), lambda i, j, ch: (i, 0))),
        out_shape=_sds((rows_all, D), F32),
        compiler_params=_params(("parallel", "arbitrary")),
    )(chip, *after, dp_all, g4)


def _reduce_scatter_start(parts, core, name):
    got = _pair_exchange(parts, name + "_pair")
    sums = [_pair_add(p, g, core, "%s_add_%d" % (name, i)) for i, (p, g) in enumerate(zip(parts, got))]
    lands = [lax.empty((3,) + s_.shape[1:], BF16) for s_ in sums]
    return _exchange_start(sums, lands, _chip_routes(len(sums)), name + "_start")


def _reduce_scatter_finish(rs_state, after, chip, name):
    sums, landed = _exchange_wait(rs_state, after, name + "_wait")
    return [_chip_sum(s_, l_, chip, "%s_sum_%d" % (name, i)) for i, (s_, l_) in enumerate(zip(sums, landed))]


def _local_step(x, c, ctx, c_ctx, norm_w, b_ada, ret_log2_decay, q_norm_w, k_norm_w, loss_target,
                w_ada16, proj_in, proj_back, get_w_o, on_out_grads, on_in_grad):
    nb, seq, _ = x.shape
    cx = ctx.shape[1]
    t_rows, c_rows = nb * seq, nb * cx
    rows_all = t_rows + c_rows
    nc = seq // CH
    tm = _pick(seq, 256, 128)
    te = _pick(seq, 512, 128)
    assert cx % tm == 0 and t_rows % cx == 0 and seq % GRID_W == 0

    x2 = x.reshape(t_rows, D)
    ctx2 = ctx.reshape(c_rows, D)
    tgt = loss_target.reshape(t_rows, D)
    c8 = jnp.zeros((8, D), F32).at[:nb].set(c).at[nb].set(c_ctx)
    lg = _log_gamma(ret_log2_decay)
    cos, sin = _rope_tables(seq)

    mod = _mod_fwd(c8, w_ada16, b_ada)
    mod3 = mod[:, None, :]
    h_all = _norm_fwd(x2, mod3, norm_w, rows_all, 0, seq, 0, None, te, "norm_fwd")
    h_all = _norm_fwd(ctx2, mod3, norm_w, rows_all, t_rows, c_rows, nb, h_all, tm, "norm_fwd_ctx")
    px = proj_in(h_all)
    s0f, s0b = _ctx_state(px, lg, nb, t_rows, cx)
    o_f, o_b, hist_f, hist_b = _ret_fwd(px, lg, s0f, s0b, nb, nc)
    yret16 = _ret_post(o_f, o_b, px, te)
    q16 = _qk_prep(px, q_norm_w, cos, sin, t_rows, 0, AQ, HQ, 4, seq, te, "q_prep")
    kx16 = _qk_prep(px, k_norm_w, cos, sin, t_rows, 0, AK, HKV, HKV, seq, te, "k_prep")
    kc16 = _qk_prep(px, k_norm_w, None, None, c_rows, t_rows, AK, HKV, HKV, seq, tm, "kc_prep")
    o_att, yatt16, lse = _att_fwd(q16, kx16, kc16, px, nb, seq, cx, tm)
    w_o_ret16, w_o_att16, w_out16 = get_w_o(lse)
    a_ret, a_att, y16 = _merge(yret16, yatt16, px, w_o_ret16, w_o_att16, te)
    dxn, dout16, dgate, loss_b = _outproj(y16, w_out16, x2, tgt, mod3, nb, seq, te)

    gw_out = _matmul(y16, dout16, ta=True, tm=D, tn=D, tk=D, out_dtype=BF16, name="gw_out")
    da_ret16, da_att16, dmr16, dma16 = _bwd_merge(dout16, w_out16, px, a_ret, a_att, te)
    gw_o_ret = _matmul(yret16, da_ret16, ta=True, tm=D, tn=D, tk=D, out_dtype=BF16, name="gw_o_ret")
    gw_o_att = _matmul(yatt16, da_att16, ta=True, tm=D, tn=D, tk=D, out_dtype=BF16, name="gw_o_att")
    out_state, out_started = on_out_grads([gw_o_ret, gw_o_att, gw_out])
    do16, drg16 = _bwd_branch_ret(da_ret16, w_o_ret16, px, o_f, o_b, te, after=out_started)
    dao, dag16 = _bwd_branch_att(da_att16, w_o_att16, px, o_att, te)
    dq_rot, dkx, dvx, dkc, dvc = _att_bwd(q16, kx16, kc16, px, dao, o_att, lse, nb, seq, cx, tm)
    daq16, gq = _qk_prep_bwd(dq_rot, px, q_norm_w, cos, sin, t_rows, 0, AQ, HQ, 4, seq, te, "q_prep_bwd")
    dak16, gk_lat = _qk_prep_bwd(dkx.reshape(t_rows, HKV * HD), px, k_norm_w, cos, sin, t_rows, 0, AK, HKV, HKV, seq, te,
                                 "k_prep_bwd")
    dcak16, gk_ctx = _qk_prep_bwd(dkc.reshape(c_rows, HKV * HD), px, k_norm_w, None, None, c_rows, t_rows, AK, HKV, HKV,
                                  seq, tm, "kc_prep_bwd")
    dq_f, dk_f, dv_f, dq_b, dk_b, dv_b, ds_f, ds_b, dlg_scan = _ret_bwd(px, lg, do16, hist_f, hist_b, nb, nc)
    dck16, dcv16, dlg_ctx = _ctx_state_bwd(px, lg, ds_f, ds_b, nb, t_rows, cx)
    dp_all = _assemble_lat(rows_all, dk_f, dk_b, dv_f, dv_b, dak16, dvx.reshape(t_rows, HKV * HD), dq_f, dq_b, drg16,
                           daq16, dag16, dmr16, dma16, tm)
    dp_all = _assemble_ctx(dp_all, dck16, dcv16, dcak16, dvc.reshape(c_rows, HKV * HD), t_rows, tm)
    gw_in_t = _matmul(dp_all, h_all, ta=True, tm=1536, tn=D, tk=1536, out_dtype=BF16, name="gw_in")
    in_state, in_started = on_in_grad(gw_in_t)
    dh = proj_back(dp_all, in_started)
    grad_x, dsh, dsc, gnw_lat = _norm_bwd(dh, x2, mod3, norm_w, dxn, 0, seq, 0, te, "norm_bwd")
    dsh_c, dsc_c, gnw_ctx = _norm_bwd(dh, ctx2, mod3, norm_w, None, t_rows, c_rows, nb, tm, "norm_bwd_ctx")

    dlg = (jnp.sum(dlg_scan[:, :, 0], axis=0) + jnp.sum(dlg_ctx[:, :, :2, 0], axis=0).T.reshape(2 * RH)).reshape(1, 2 * RH)
    misc = jnp.concatenate([gq, gk_lat + gk_ctx, dlg, jnp.sum(loss_b[:, 0, 0]).reshape(1, 1),
                            jnp.zeros((1, D - 2 * HD - 2 * RH - 1), F32)], axis=1)
    rows = []
    for b in range(nb):
        rows += [dsh[b], dsc[b], dgate[b]]
    rows += [dsh_c[0], dsc_c[0]] + [c[b:b + 1] for b in range(nb)] + [gnw_lat + gnw_ctx, misc]
    payload = jnp.concatenate(rows + [jnp.zeros((PAY_ROWS - len(rows), D), F32)], axis=0)
    return grad_x.reshape(nb, seq, D), out_state, in_state, payload


def _finish_small(gathered, nb, c_ctx, ret_log2_decay, w_ada16, dev):
    n_dev = gathered.shape[0]
    loc = 3 * D // n_dev
    dmod_all = gathered[:, :3 * nb].reshape(n_dev * nb, 3 * D)
    dmodc_parts = jnp.concatenate([gathered[:, 3 * nb:3 * nb + 2].reshape(n_dev, 2 * D), jnp.zeros((n_dev, D), F32)], axis=1)
    c_all = gathered[:, 3 * nb + 2:4 * nb + 2].reshape(n_dev * nb, D)
    nw_parts = gathered[:, 4 * nb + 2]
    misc_parts = gathered[:, 4 * nb + 3]
    n_rows = n_dev * nb + n_dev
    pad = (-n_rows) % 16
    c_rows = jnp.concatenate([c_all, jnp.broadcast_to(c_ctx.reshape(1, D), (n_dev, D)), jnp.zeros((pad, D), F32)], axis=0)
    dm_rows = jnp.concatenate([dmod_all, dmodc_parts, jnp.zeros((pad, 3 * D), F32)], axis=0)
    dm_loc_rows = lax.dynamic_slice_in_dim(dm_rows, dev * loc, loc, axis=1)
    r_pad = jnp.full((1, D), -1.0, F32).at[:, 2 * HD:2 * HD + 2 * RH].set(ret_log2_decay.reshape(1, 2 * RH))
    gb, gc, gnw, misc, gwa = _small_final(dmod_all, dmodc_parts, c_rows, dm_loc_rows, nw_parts, misc_parts,
                                          c_ctx.reshape(1, D), r_pad, w_ada16)
    return (gb, gc, gnw, misc[:, :HD], misc[:, HD:2 * HD], misc[:, 2 * HD:2 * HD + 2 * RH], gwa,
            misc[0, 2 * HD + 2 * RH])


def kernel(x, c, ctx, c_ctx, norm_w, w_ada, b_ada, w_in, ret_log2_decay, q_norm_w, k_norm_w, w_o_ret, w_o_att, w_out, loss_target, m_c_ctx, m_norm_w, m_w_ada, m_b_ada, m_w_in, m_ret_log2_decay, m_q_norm_w, m_k_norm_w, m_w_o_ret, m_w_o_att, m_w_out, v_c_ctx, v_norm_w, v_w_ada, v_b_ada, v_w_in, v_ret_log2_decay, v_q_norm_w, v_k_norm_w, v_w_o_ret, v_w_o_att, v_w_out):
    nb = x.shape[0]
    mx, my, mc = _mesh_pos()
    dev = 4 * mx + 2 * my + mc
    core = jnp.reshape(mc, (1,)).astype(jnp.int32)
    chip = jnp.reshape(2 * mx + my, (1,)).astype(jnp.int32)

    (g_ada,) = _all_gather([w_ada[0].astype(BF16)], "gather_ada")
    w_ada16 = jnp.transpose(g_ada, (1, 0, 2)).reshape(D, 3 * D)

    w_in_t = jnp.transpose(w_in[0])
    in_shard = w_in_t.astype(BF16)
    groups = lax.dynamic_update_slice(lax.empty((N_DEV,) + in_shard.shape, BF16), in_shard[None], (mc, 0, 0))
    groups = _pair_fill(groups, 0, "gather_in_pair", after=(g_ada,))
    (near_send, near_recv, near_bufs, near_routes), gin_token = _exchange_start(
        [in_shard[None]], [groups], _group_routes((1, 2)), "gather_in_start")
    w_in_groups = []

    def proj_in(h_all):
        src, groups = near_bufs
        px = _in_proj_group(h_all, groups, 0, chip, None, (gin_token, wo_token), "in_proj_0")
        (far_send, far_recv, (src, groups), far_routes), _ = _exchange_start(
            [src], [groups], _group_routes((3,)), "gather_in_start_far", after=(px,))
        for j in (1, 2, 3):
            state = ((near_send, near_recv, [src, groups], near_routes) if j < 3 else
                     (far_send, far_recv, [src, groups], far_routes))
            _, (src,), (groups,) = _exchange_wait_some(state, px, (j - 1,) if j < 3 else None, "gather_in_wait_%d" % j)
            groups = _pair_fill(groups, j, "gather_in_fill_%d" % j)
            px = _in_proj_group(h_all, groups, j, chip, px, (), "in_proj_%d" % j)
        w_in_groups.append(groups)
        return px

    def proj_back(dp_all, after):
        return _d_h_groups(dp_all, w_in_groups[0], chip, after)

    wo_shards = [w_[0].astype(BF16) for w_ in (w_o_ret, w_o_att, w_out)]
    wo_lands = [lax.dynamic_update_slice(lax.empty((N_DEV,) + s_.shape, BF16), s_[None], (dev, 0, 0)) for s_ in wo_shards]
    wo_state, wo_token = _exchange_start([s_[None] for s_ in wo_shards], wo_lands, _bcast_routes(3), "gather_wo_start",
                                         after=(gin_token,))

    def get_w_o(after):
        _, (l_ret, l_att, l_out) = _exchange_wait(wo_state, after, "gather_wo_wait")
        return l_ret.reshape(RH * DV, D), l_att.reshape(D, D), l_out.reshape(D, D)

    def on_out_grads(grads):
        parts = [g_.reshape(N_DEV, g_.shape[0] // N_DEV, D) for g_ in grads]
        state, token = _reduce_scatter_start(parts, core, "rs_out")
        return state, (token,)

    def on_in_grad(grad):
        state, token = _reduce_scatter_start([grad.reshape(N_DEV, IN_COLS // N_DEV, D)], core, "rs_in")
        return state, (token,)

    grad_x, out_state, in_state, payload = _local_step(
        x, c, ctx, c_ctx, norm_w, b_ada, ret_log2_decay, q_norm_w, k_norm_w, loss_target,
        w_ada16, proj_in, proj_back, get_w_o, on_out_grads, on_in_grad)

    (gathered,) = _all_gather([payload], "gather_small")
    gb, gc, gnw, gq, gk, gr, gwa, loss = _finish_small(gathered, nb, c_ctx, ret_log2_decay, w_ada16, dev)

    g_w_o_ret, g_w_o_att, g_w_out = _reduce_scatter_finish(out_state, gathered, chip, "rs_out")
    (g_w_in_t,) = _reduce_scatter_finish(in_state, gathered, chip, "rs_in")

    grads = [gc.reshape(c_ctx.shape), gnw, gwa[None], gb, g_w_in_t, gr.reshape(ret_log2_decay.shape), gq, gk,
             g_w_o_ret[None], g_w_o_att[None], g_w_out[None]]
    weights = [c_ctx, norm_w, w_ada, b_ada, w_in_t, ret_log2_decay, q_norm_w, k_norm_w, w_o_ret, w_o_att, w_out]
    ms = [m_c_ctx, m_norm_w, m_w_ada, m_b_ada, jnp.transpose(m_w_in[0]), m_ret_log2_decay, m_q_norm_w, m_k_norm_w,
          m_w_o_ret, m_w_o_att, m_w_out]
    vs = [v_c_ctx, v_norm_w, v_w_ada, v_b_ada, jnp.transpose(v_w_in[0]), v_ret_log2_decay, v_q_norm_w, v_k_norm_w,
          v_w_o_ret, v_w_o_att, v_w_out]
    deltas, new_ms, new_vs = [], [], []
    for i, (w, g, m, v) in enumerate(zip(weights, grads, ms, vs)):
        shape2 = (-1, w.shape[-1])
        res = _adamw(w.reshape(shape2), g.reshape(shape2), m.reshape(shape2), v.reshape(shape2), "adamw_%d" % i)
        for lst, r in zip((deltas, new_ms, new_vs), res):
            lst.append(jnp.transpose(r)[None] if i == 4 else r.reshape(w.shape))
    grads[4] = jnp.transpose(g_w_in_t)[None]
    return (loss, grad_x, *grads, *deltas, *new_ms, *new_vs)
```

```python
import numpy as np
import jax
import jax.numpy as jnp
from jax import lax
from jax.experimental import pallas as pl
from jax.experimental.pallas import tpu as pltpu

F32 = jnp.float32
BF16 = jnp.bfloat16

D = 1024
RH, DK, DV, CH = 4, 256, 512, 256
HQ, HKV, HD = 8, 2, 128
GRID_W = 64
ROPE_THETA = 10000.0
EPS = 1e-6
RK, RV, AK, AV, RQ, RG, AQ, AG, MR, MA = 0, 1024, 3072, 3328, 3584, 4608, 6656, 7680, 8704, 9728
IN_COLS = 10752
KV_COLS = 3584
N_DEV = 8
LR, B1, B2, ADAM_EPS, WD, STEP = 0.001, 0.9, 0.999, 1e-08, 0.01, 10
PAY_ROWS = 16
VMEM_LIMIT = 56 * 1024 * 1024
MESH_T = pl.DeviceIdType.MESH

NT = (((1,), (1,)), ((), ()))
TN = (((0,), (0,)), ((), ()))
SM_C = (HD ** -0.5) * float(np.log2(np.e))


def _params(sem):
    return pltpu.CompilerParams(dimension_semantics=sem, vmem_limit_bytes=VMEM_LIMIT)


def _pick(n, target, mult=8):
    best = None
    for t in range(mult, min(n, target) + 1, mult):
        if n % t == 0:
            best = t
    return best or n


def _dot(a, b, dn=None):
    if dn is None:
        return jnp.dot(a, b, preferred_element_type=F32)
    return lax.dot_general(a, b, dn, preferred_element_type=F32)


def _sig(v):
    return jax.nn.sigmoid(v)


def _silu(v):
    return v * _sig(v)


def _dsilu(v):
    s = _sig(v)
    return s * (1.0 + v * (1.0 - s))


def _sds(shape, dtype):
    return jax.ShapeDtypeStruct(shape, dtype)


def _matmul(a, b, *, ta=False, tb=False, tm, tn, tk, out_dtype, name, after=()):
    m = a.shape[1] if ta else a.shape[0]
    kdim = a.shape[0] if ta else a.shape[1]
    n = b.shape[0] if tb else b.shape[1]
    tm, tn, tk = _pick(m, tm, 128), _pick(n, tn, 128), _pick(kdim, tk, 128)
    nk = kdim // tk
    dn = (((0 if ta else 1,), (1 if tb else 0,)), ((), ()))

    def body(a_ref, b_ref, *rest):
        o_ref, acc_ref = rest[-2:]
        k = pl.program_id(2)
        part = _dot(a_ref[...].astype(BF16), b_ref[...].astype(BF16), dn)
        if nk == 1:
            o_ref[...] = part.astype(o_ref.dtype)
        else:
            @pl.when(k == 0)
            def _():
                acc_ref[...] = part

            @pl.when(k > 0)
            def _():
                acc_ref[...] += part

            @pl.when(k == nk - 1)
            def _():
                o_ref[...] = acc_ref[...].astype(o_ref.dtype)

    a_spec = pl.BlockSpec((tk, tm), lambda i, j, k: (k, i)) if ta else pl.BlockSpec((tm, tk), lambda i, j, k: (i, k))
    b_spec = pl.BlockSpec((tn, tk), lambda i, j, k: (j, k)) if tb else pl.BlockSpec((tk, tn), lambda i, j, k: (k, j))
    return pl.pallas_call(
        body, name=name, grid=(m // tm, n // tn, nk),
        in_specs=[a_spec, b_spec] + [pl.BlockSpec(memory_space=pl.ANY)] * len(after),
        out_specs=pl.BlockSpec((tm, tn), lambda i, j, k: (i, j)), out_shape=_sds((m, n), out_dtype),
        scratch_shapes=[pltpu.VMEM((tm, tn) if nk > 1 else (8, 128), F32)],
        compiler_params=_params(("parallel", "parallel", "arbitrary")),
    )(a, b, *after)


def _log_gamma(r):
    rp = jnp.full((8, 128), -1.0, F32).at[:2, :RH].set(r.reshape(2, RH))

    def body(r_ref, o_ref):
        o_ref[...] = jnp.log1p(-jnp.exp2(r_ref[...]))

    out = pl.pallas_call(body, name="log_gamma", out_shape=_sds((8, 128), F32))(rp)
    return out[:2, :RH]


def _mod_fwd(c8, w_ada16, b_ada):
    def body(c_ref, w_ref, b_ref, o_ref):
        o_ref[...] = _dot(_silu(c_ref[...]).astype(BF16), w_ref[...]) + b_ref[...]

    return pl.pallas_call(
        body, name="mod_fwd", grid=(3,),
        in_specs=[pl.BlockSpec((8, D), lambda j: (0, 0)), pl.BlockSpec((D, D), lambda j: (0, j)),
                  pl.BlockSpec((1, D), lambda j: (0, j))],
        out_specs=pl.BlockSpec((8, D), lambda j: (0, j)), out_shape=_sds((8, 3 * D), F32),
        compiler_params=_params(("arbitrary",)),
    )(c8, w_ada16, b_ada)


def _norm_fwd(x2, mod3, norm_w, rows_all, row_off, rows_per_group, group0, h_prev, tm, name):
    rows = x2.shape[0]
    rb0 = row_off // tm
    bpg = rows_per_group // tm

    def body(*refs):
        x_ref, sh_ref, sc_ref, nw_ref, o_ref = refs[-5:]
        xv = x_ref[...]
        r = lax.rsqrt(jnp.mean(xv * xv, axis=-1, keepdims=True) + EPS)
        o_ref[...] = ((xv * r) * nw_ref[...] * (1.0 + sc_ref[...]) + sh_ref[...]).astype(BF16)

    in_specs = [pl.BlockSpec((tm, D), lambda i: (i, 0)),
                pl.BlockSpec((None, 1, D), lambda i: (group0 + i // bpg, 0, 0)),
                pl.BlockSpec((None, 1, D), lambda i: (group0 + i // bpg, 0, 1)),
                pl.BlockSpec((1, D), lambda i: (0, 0))]
    args = [x2, mod3, mod3, norm_w]
    alias = {}
    if h_prev is not None:
        in_specs.insert(0, pl.BlockSpec(memory_space=pl.ANY))
        args.insert(0, h_prev)
        alias = {0: 0}
    return pl.pallas_call(
        body, name=name, grid=(rows // tm,), in_specs=in_specs,
        out_specs=pl.BlockSpec((tm, D), lambda i: (rb0 + i, 0)), out_shape=_sds((rows_all, D), BF16),
        input_output_aliases=alias, compiler_params=_params(("parallel",)),
    )(*args)


def _decays(lg, fwd):
    ii = lax.broadcasted_iota(jnp.int32, (CH, CH), 0)
    jj = lax.broadcasted_iota(jnp.int32, (CH, CH), 1)
    ri = lax.broadcasted_iota(jnp.int32, (CH, 1), 0).astype(F32)
    rel = (ii - jj) if fwd else (jj - ii)
    relf = jnp.maximum(rel, 0).astype(F32)
    mask = jnp.where(rel >= 0, jnp.exp(lg * relf), 0.0)
    qe = (ri + 1.0) if fwd else (CH - ri)
    ke = (CH - 1.0 - ri) if fwd else ri
    return mask, relf, jnp.exp(lg * qe), qe, jnp.exp(lg * ke), ke


def _wide_specs(rowf):
    return [pl.BlockSpec((CH, 2 * DK), lambda b, c: (rowf(b, c), RQ // (2 * DK))),
            pl.BlockSpec((CH, 2 * DK), lambda b, c: (rowf(b, c), RQ // (2 * DK) + 1)),
            pl.BlockSpec((CH, RH * DK), lambda b, c: (rowf(b, c), RK // (RH * DK))),
            pl.BlockSpec((CH, 2 * DV), lambda b, c: (rowf(b, c), RV // (2 * DV))),
            pl.BlockSpec((CH, 2 * DV), lambda b, c: (rowf(b, c), RV // (2 * DV) + 1))]


def _head_qkv(refs, h):
    q0, q1, k, v0, v1 = refs
    lo = h % 2
    q = (q0, q1)[h // 2][:, lo * DK:(lo + 1) * DK].astype(F32)
    kk = k[:, h * DK:(h + 1) * DK].astype(F32) * (DK ** -0.5)
    v16 = (v0, v1)[h // 2][:, lo * DV:(lo + 1) * DV].astype(BF16)
    return q, kk, v16


def _ctx_state(px, lg, nb, t_rows, cx):
    rb = t_rows // cx

    def body(lg_ref, k_ref, v_ref, sf_ref, sb_ref):
        h = pl.program_id(1)
        pos = lax.broadcasted_iota(jnp.int32, (cx, 1), 0).astype(F32)
        k = k_ref[...].astype(F32) * (DK ** -0.5)
        v16 = v_ref[...].astype(BF16)
        wf = jnp.exp(lg_ref[0, h] * (cx - 1.0 - pos))
        wb = jnp.exp(lg_ref[1, h] * pos)
        sf_ref[...] = _dot((k * wf).astype(BF16), v16, TN)
        sb_ref[...] = _dot((k * wb).astype(BF16), v16, TN)

    st = pl.BlockSpec((None, None, DK, DV), lambda b, h: (b, h, 0, 0))
    return pl.pallas_call(
        body, name="ctx_state", grid=(nb, RH),
        in_specs=[pl.BlockSpec(memory_space=pltpu.SMEM),
                  pl.BlockSpec((cx, DK), lambda b, h: (rb + b, RK // DK + h)),
                  pl.BlockSpec((cx, DV), lambda b, h: (rb + b, RV // DV + h))],
        out_specs=[st, st], out_shape=[_sds((nb, RH, DK, DV), F32)] * 2,
        compiler_params=_params(("parallel", "parallel")),
    )(lg, px, px)


def _ret_fwd(px, lg, s0f, s0b, nb, nc):
    t_rows = nb * nc * CH

    def body(lg_ref, *refs):
        ins = (refs[0:5], refs[5:10])
        s0f_ref, s0b_ref, of_ref, ob_ref, hf_ref, hb_ref, sf, sb = refs[10:]
        c = pl.program_id(1)

        @pl.when(c == 0)
        def _():
            sf[...] = s0f_ref[...]
            sb[...] = s0b_ref[...]

        for d, (o_ref, h_ref, s) in enumerate(((of_ref, hf_ref, sf), (ob_ref, hb_ref, sb))):
            for h in range(RH):
                lg_d = lg_ref[d, h]
                mask, _, qd, _, kd, _ = _decays(lg_d, d == 0)
                q, k, v16 = _head_qkv(ins[d], h)
                a = _dot(q.astype(BF16), k.astype(BF16), NT)
                st = s[h]
                st16 = st.astype(BF16)
                h_ref[h] = st16
                o = _dot((a * mask).astype(BF16), v16) + _dot((q * qd).astype(BF16), st16)
                o_ref[:, h * DV:(h + 1) * DV] = o.astype(BF16)
                s[h] = st * jnp.exp(lg_d * CH) + _dot((k * kd).astype(BF16), v16, TN)

    def fw(b, c):
        return b * nc + c

    def bw(b, c):
        return b * nc + nc - 1 - c

    st = pl.BlockSpec((None, RH, DK, DV), lambda b, c: (b, 0, 0, 0))
    in_specs = [pl.BlockSpec(memory_space=pltpu.SMEM)] + _wide_specs(fw) + _wide_specs(bw) + [st, st]
    out_specs = [pl.BlockSpec((CH, RH * DV), lambda b, c: (fw(b, c), 0)),
                 pl.BlockSpec((CH, RH * DV), lambda b, c: (bw(b, c), 0)),
                 pl.BlockSpec((None, None, RH, DK, DV), lambda b, c: (b, c, 0, 0, 0)),
                 pl.BlockSpec((None, None, RH, DK, DV), lambda b, c: (b, nc - 1 - c, 0, 0, 0))]
    return pl.pallas_call(
        body, name="ret_fwd", grid=(nb, nc), in_specs=in_specs, out_specs=out_specs,
        out_shape=[_sds((t_rows, RH * DV), BF16)] * 2 + [_sds((nb, nc, RH, DK, DV), BF16)] * 2,
        scratch_shapes=[pltpu.VMEM((RH, DK, DV), F32), pltpu.VMEM((RH, DK, DV), F32)],
        compiler_params=_params(("parallel", "arbitrary")),
    )(lg, *([px] * 10), s0f, s0b)


def _ret_post(o_f, o_b, px, tm):
    t_rows = o_f.shape[0]

    def body(of_ref, ob_ref, g0, g1, g2, g3, y_ref):
        for h, g_ref in enumerate((g0, g1, g2, g3)):
            sl = slice(h * DV, (h + 1) * DV)
            o = of_ref[:, sl].astype(F32) + ob_ref[:, sl].astype(F32)
            r = lax.rsqrt(jnp.mean(o * o, axis=-1, keepdims=True) + EPS)
            y_ref[:, sl] = ((o * r) * _silu(g_ref[...].astype(F32))).astype(BF16)

    def gate(h):
        return pl.BlockSpec((tm, DV), lambda i: (i, RG // DV + h))

    wide = pl.BlockSpec((tm, RH * DV), lambda i: (i, 0))
    return pl.pallas_call(
        body, name="ret_post", grid=(t_rows // tm,),
        in_specs=[wide, wide] + [gate(h) for h in range(RH)],
        out_specs=wide, out_shape=_sds((t_rows, RH * DV), BF16),
        compiler_params=_params(("parallel",)),
    )(o_f, o_b, *([px] * RH))


def _rope_tables(seq):
    rows = seq // GRID_W
    row = np.repeat(np.arange(rows, dtype=np.float32), GRID_W)
    col = np.tile(np.arange(GRID_W, dtype=np.float32), rows)
    half = HD // 2
    freqs = (ROPE_THETA ** (-np.arange(0, half, 2, dtype=np.float32) / half)).astype(np.float32)
    ang = np.concatenate([row[:, None] * freqs, col[:, None] * freqs], axis=-1).astype(np.float32)
    cos = np.repeat(np.cos(ang), 2, axis=-1).astype(np.float32)
    sin = np.repeat(np.sin(ang), 2, axis=-1).astype(np.float32)
    sign = np.tile(np.array([-1.0, 1.0], np.float32), HD // 2)
    return jnp.asarray(cos), jnp.asarray(sin * sign)


def _swap_pairs(v):
    lane = lax.broadcasted_iota(jnp.int32, v.shape, 1)
    return jnp.where((lane & 1) == 0, pltpu.roll(v, HD - 1, 1), pltpu.roll(v, 1, 1))


def _qk_prep(px, nw, cos, sin, rows, row_off, col_off, heads, hb, seq, tm, name):
    rope = cos is not None
    rb0 = row_off // tm
    pb = seq // tm if rope else 1
    bw = hb * HD

    def body(*refs):
        if rope:
            x_ref, w_ref, c_ref, s_ref, o_ref = refs
        else:
            x_ref, w_ref, o_ref = refs
        for h in range(hb):
            sl = slice(h * HD, (h + 1) * HD)
            xv = x_ref[:, sl].astype(F32)
            r = lax.rsqrt(jnp.mean(xv * xv, axis=-1, keepdims=True) + EPS)
            t = (xv * r) * w_ref[...]
            if rope:
                t = t * c_ref[...] + _swap_pairs(t) * s_ref[...]
            o_ref[:, sl] = t.astype(BF16)

    in_specs = [pl.BlockSpec((tm, bw), lambda i, j: (rb0 + i, col_off // bw + j)),
                pl.BlockSpec((1, HD), lambda i, j: (0, 0))]
    args = [px, nw]
    if rope:
        in_specs += [pl.BlockSpec((tm, HD), lambda i, j: (i % pb, 0))] * 2
        args += [cos, sin]
    return pl.pallas_call(
        body, name=name, grid=(rows // tm, heads // hb), in_specs=in_specs,
        out_specs=pl.BlockSpec((tm, bw), lambda i, j: (i, j)), out_shape=_sds((rows, heads * HD), BF16),
        compiler_params=_params(("parallel", "parallel")),
    )(*args)


def _att_fwd(q16, kx16, kc16, px, nb, seq, cx, tq):
    t_rows = nb * seq
    nq = seq // tq
    rep = HQ // HKV
    gw = rep * HD

    def body(q_ref, kx_ref, kc_ref, vx_ref, vc_ref, g_ref, o_ref, y_ref, l_ref):
        kx = kx_ref[...]
        kc = kc_ref[...]
        vx = vx_ref[...].astype(BF16)
        vc = vc_ref[...].astype(BF16)
        l_ref[...] = jnp.zeros_like(l_ref)
        for r in range(rep):
            sl = slice(r * HD, (r + 1) * HD)
            q = q_ref[:, sl]
            s1 = _dot(q, kx, NT)
            s2 = _dot(q, kc, NT)
            m = jnp.maximum(jnp.max(s1, axis=-1, keepdims=True), jnp.max(s2, axis=-1, keepdims=True))
            e1 = jnp.exp2((s1 - m) * SM_C)
            e2 = jnp.exp2((s2 - m) * SM_C)
            tot = jnp.sum(e1, axis=-1, keepdims=True) + jnp.sum(e2, axis=-1, keepdims=True)
            o = (_dot(e1.astype(BF16), vx) + _dot(e2.astype(BF16), vc)) * (1.0 / tot)
            o_ref[:, sl] = o
            y_ref[:, sl] = (o * _silu(g_ref[:, sl].astype(F32))).astype(BF16)
            l_ref[:, r:r + 1] = m * SM_C + jnp.log(tot) * float(np.log2(np.e))

    qblk = pl.BlockSpec((tq, gw), lambda b, g, i: (b * nq + i, g))
    return pl.pallas_call(
        body, name="att_fwd", grid=(nb, HKV, nq),
        in_specs=[qblk,
                  pl.BlockSpec((seq, HD), lambda b, g, i: (b, g)),
                  pl.BlockSpec((cx, HD), lambda b, g, i: (b, g)),
                  pl.BlockSpec((seq, HD), lambda b, g, i: (b, AV // HD + g)),
                  pl.BlockSpec((cx, HD), lambda b, g, i: (t_rows // cx + b, AV // HD + g)),
                  pl.BlockSpec((tq, gw), lambda b, g, i: (b * nq + i, AG // gw + g))],
        out_specs=[qblk, qblk, pl.BlockSpec((tq, 128), lambda b, g, i: (b * nq + i, g))],
        out_shape=[_sds((t_rows, D), F32), _sds((t_rows, D), BF16), _sds((t_rows, HKV * 128), F32)],
        compiler_params=_params(("parallel", "parallel", "parallel")),
    )(q16, kx16, kc16, px, px, px)


def _gate_specs(tm, col0):
    hw = D // 2
    return [pl.BlockSpec((tm, hw), lambda i: (i, col0 // hw)), pl.BlockSpec((tm, hw), lambda i: (i, col0 // hw + 1))]


def _merge(yret16, yatt16, px, w_o_ret16, w_o_att16, tm):
    t_rows = yret16.shape[0]
    hw = D // 2

    def body(yr_ref, wr_ref, ya_ref, wa_ref, mr0, mr1, ma0, ma1, ar_ref, aa_ref, y_ref):
        ar = _dot(yr_ref[...], wr_ref[...])
        aa = _dot(ya_ref[...], wa_ref[...])
        ar_ref[...] = ar.astype(BF16)
        aa_ref[...] = aa.astype(BF16)
        for j, (mr_ref, ma_ref) in enumerate(((mr0, ma0), (mr1, ma1))):
            sl = slice(j * hw, (j + 1) * hw)
            y_ref[:, sl] = (_sig(mr_ref[...].astype(F32)) * ar[:, sl]
                            + _sig(ma_ref[...].astype(F32)) * aa[:, sl]).astype(BF16)

    row = pl.BlockSpec((tm, D), lambda i: (i, 0))
    return pl.pallas_call(
        body, name="merge", grid=(t_rows // tm,),
        in_specs=[pl.BlockSpec((tm, RH * DV), lambda i: (i, 0)), pl.BlockSpec((RH * DV, D), lambda i: (0, 0)),
                  row, pl.BlockSpec((D, D), lambda i: (0, 0))] + _gate_specs(tm, MR) + _gate_specs(tm, MA),
        out_specs=[row, row, row], out_shape=[_sds((t_rows, D), BF16)] * 3,
        compiler_params=_params(("parallel",)),
    )(yret16, w_o_ret16, yatt16, w_o_att16, px, px, px, px)


def _outproj(y16, w_out16, x2, tgt, mod3, nb, seq, tm):
    t_rows = nb * seq
    bpb = seq // tm

    def body(y_ref, w_ref, x_ref, t_ref, g_ref, dxn_ref, dout_ref, dg_ref, loss_ref):
        i = pl.program_id(1)
        out = _dot(y_ref[...], w_ref[...])
        gate = g_ref[...]
        diff = x_ref[...] + gate * out - t_ref[...]
        dxn = diff * (1.0 / D)
        dxn_ref[...] = dxn
        dout_ref[...] = (gate * dxn).astype(BF16)
        dg = jnp.sum(dxn * out, axis=0, keepdims=True)
        ls = jnp.broadcast_to(jnp.sum(diff * diff) * (0.5 / D), (1, 128))

        @pl.when(i == 0)
        def _():
            dg_ref[...] = dg
            loss_ref[...] = ls

        @pl.when(i > 0)
        def _():
            dg_ref[...] += dg
            loss_ref[...] += ls

    row = pl.BlockSpec((tm, D), lambda b, i: (b * bpb + i, 0))
    return pl.pallas_call(
        body, name="outproj", grid=(nb, bpb),
        in_specs=[row, pl.BlockSpec((D, D), lambda b, i: (0, 0)), row, row,
                  pl.BlockSpec((None, 1, D), lambda b, i: (b, 0, 2))],
        out_specs=[row, row, pl.BlockSpec((None, 1, D), lambda b, i: (b, 0, 0)),
                   pl.BlockSpec((None, 1, 128), lambda b, i: (b, 0, 0))],
        out_shape=[_sds((t_rows, D), F32), _sds((t_rows, D), BF16), _sds((nb, 1, D), F32), _sds((nb, 1, 128), F32)],
        compiler_params=_params(("parallel", "arbitrary")),
    )(y16, w_out16, x2, tgt, mod3)


def _bwd_merge(dout16, w_out16, px, a_ret, a_att, tm):
    t_rows = dout16.shape[0]
    hw = D // 2

    def body(do_ref, w_ref, mr0, mr1, ma0, ma1, ar_ref, aa_ref, dar_ref, daa_ref, dmr_ref, dma_ref):
        dy_all = _dot(do_ref[...], w_ref[...], NT)
        for j, (mr_ref, ma_ref) in enumerate(((mr0, ma0), (mr1, ma1))):
            sl = slice(j * hw, (j + 1) * hw)
            dy = dy_all[:, sl]
            sr = _sig(mr_ref[...].astype(F32))
            sa = _sig(ma_ref[...].astype(F32))
            dar_ref[:, sl] = (dy * sr).astype(BF16)
            daa_ref[:, sl] = (dy * sa).astype(BF16)
            dmr_ref[:, sl] = (dy * ar_ref[:, sl].astype(F32) * sr * (1.0 - sr)).astype(BF16)
            dma_ref[:, sl] = (dy * aa_ref[:, sl].astype(F32) * sa * (1.0 - sa)).astype(BF16)

    row = pl.BlockSpec((tm, D), lambda i: (i, 0))
    return pl.pallas_call(
        body, name="bwd_merge", grid=(t_rows // tm,),
        in_specs=[row, pl.BlockSpec((D, D), lambda i: (0, 0))] + _gate_specs(tm, MR) + _gate_specs(tm, MA) + [row, row],
        out_specs=[row] * 4, out_shape=[_sds((t_rows, D), BF16)] * 4,
        compiler_params=_params(("parallel",)),
    )(dout16, w_out16, px, px, px, px, a_ret, a_att)


def _bwd_branch_ret(da_ret16, w_o_ret16, px, o_f, o_b, tm, after=()):
    t_rows = da_ret16.shape[0]

    def body(da_ref, w_ref, g0, g1, g2, g3, of_ref, ob_ref, *rest):
        do_ref, dg_ref = rest[-2:]
        da = da_ref[...]
        for h, g_ref in enumerate((g0, g1, g2, g3)):
            sl = slice(h * DV, (h + 1) * DV)
            dy = _dot(da, w_ref[sl, :], NT)
            g = g_ref[...].astype(F32)
            o = of_ref[:, sl].astype(F32) + ob_ref[:, sl].astype(F32)
            r = lax.rsqrt(jnp.mean(o * o, axis=-1, keepdims=True) + EPS)
            on = o * r
            don = dy * _silu(g)
            dg_ref[:, sl] = (dy * on * _dsilu(g)).astype(BF16)
            do_ref[:, sl] = (r * (don - on * jnp.mean(on * don, axis=-1, keepdims=True))).astype(BF16)

    def gate(h):
        return pl.BlockSpec((tm, DV), lambda i: (i, RG // DV + h))

    wide = pl.BlockSpec((tm, RH * DV), lambda i: (i, 0))
    return pl.pallas_call(
        body, name="bwd_branch_ret", grid=(t_rows // tm,),
        in_specs=[pl.BlockSpec((tm, D), lambda i: (i, 0)), pl.BlockSpec((RH * DV, D), lambda i: (0, 0))]
        + [gate(h) for h in range(RH)] + [wide, wide] + [pl.BlockSpec(memory_space=pl.ANY)] * len(after),
        out_specs=[wide, wide], out_shape=[_sds((t_rows, RH * DV), BF16)] * 2,
        compiler_params=_params(("parallel",)),
    )(da_ret16, w_o_ret16, *([px] * RH), o_f, o_b, *after)


def _bwd_branch_att(da_att16, w_o_att16, px, o_att, tm):
    t_rows = da_att16.shape[0]
    hw = D // 2

    def body(da_ref, w_ref, g0, g1, o_ref, dao_ref, dg_ref):
        dy_all = _dot(da_ref[...], w_ref[...], NT)
        for j, g_ref in enumerate((g0, g1)):
            sl = slice(j * hw, (j + 1) * hw)
            dy = dy_all[:, sl]
            g = g_ref[...].astype(F32)
            dao_ref[:, sl] = dy * _silu(g)
            dg_ref[:, sl] = (dy * o_ref[:, sl] * _dsilu(g)).astype(BF16)

    row = pl.BlockSpec((tm, D), lambda i: (i, 0))
    return pl.pallas_call(
        body, name="bwd_branch_att", grid=(t_rows // tm,),
        in_specs=[row, pl.BlockSpec((D, D), lambda i: (0, 0))] + _gate_specs(tm, AG) + [row],
        out_specs=[row, row], out_shape=[_sds((t_rows, D), F32), _sds((t_rows, D), BF16)],
        compiler_params=_params(("parallel",)),
    )(da_att16, w_o_att16, px, px, o_att)


def _att_bwd(q16, kx16, kc16, px, dao, o_att, lse, nb, seq, cx, tq):
    t_rows = nb * seq
    nq = seq // tq
    rep = HQ // HKV
    gw = rep * HD
    scale = HD ** -0.5

    def body(q_ref, kx_ref, kc_ref, vx_ref, vc_ref, dao_ref, o_ref, l_ref, dq_ref, dkx_ref, dvx_ref, dkc_ref, dvc_ref):
        i = pl.program_id(2)
        kx = kx_ref[...]
        kc = kc_ref[...]
        vx = vx_ref[...].astype(BF16)
        vc = vc_ref[...].astype(BF16)
        dkx = jnp.zeros((seq, HD), F32)
        dvx = jnp.zeros((seq, HD), F32)
        dkc = jnp.zeros((cx, HD), F32)
        dvc = jnp.zeros((cx, HD), F32)
        for r in range(rep):
            sl = slice(r * HD, (r + 1) * HD)
            q = q_ref[:, sl]
            lr = l_ref[:, r:r + 1]
            p1 = jnp.exp2(_dot(q, kx, NT) * SM_C - lr)
            p2 = jnp.exp2(_dot(q, kc, NT) * SM_C - lr)
            da = dao_ref[:, sl]
            da16 = da.astype(BF16)
            delta = jnp.sum(da * o_ref[:, sl], axis=-1, keepdims=True)
            ds1 = (p1 * (_dot(da16, vx, NT) - delta)).astype(BF16)
            ds2 = (p2 * (_dot(da16, vc, NT) - delta)).astype(BF16)
            dq_ref[:, sl] = (_dot(ds1, kx) + _dot(ds2, kc)) * scale
            dkx += _dot(ds1, q, TN)
            dkc += _dot(ds2, q, TN)
            dvx += _dot(p1.astype(BF16), da16, TN)
            dvc += _dot(p2.astype(BF16), da16, TN)
        dkx = dkx * scale
        dkc = dkc * scale

        @pl.when(i == 0)
        def _():
            dkx_ref[...] = dkx
            dvx_ref[...] = dvx
            dkc_ref[...] = dkc
            dvc_ref[...] = dvc

        @pl.when(i > 0)
        def _():
            dkx_ref[...] += dkx
            dvx_ref[...] += dvx
            dkc_ref[...] += dkc
            dvc_ref[...] += dvc

    qblk = pl.BlockSpec((tq, gw), lambda b, g, i: (b * nq + i, g))
    kxb = pl.BlockSpec((None, seq, HD), lambda b, g, i: (b, 0, g))
    kcb = pl.BlockSpec((None, cx, HD), lambda b, g, i: (b, 0, g))
    return pl.pallas_call(
        body, name="att_bwd", grid=(nb, HKV, nq),
        in_specs=[qblk,
                  pl.BlockSpec((seq, HD), lambda b, g, i: (b, g)),
                  pl.BlockSpec((cx, HD), lambda b, g, i: (b, g)),
                  pl.BlockSpec((seq, HD), lambda b, g, i: (b, AV // HD + g)),
                  pl.BlockSpec((cx, HD), lambda b, g, i: (t_rows // cx + b, AV // HD + g)),
                  qblk, qblk, pl.BlockSpec((tq, 128), lambda b, g, i: (b * nq + i, g))],
        out_specs=[qblk, kxb, kxb, kcb, kcb],
        out_shape=[_sds((t_rows, D), F32), _sds((nb, seq, HKV * HD), F32), _sds((nb, seq, HKV * HD), F32),
                   _sds((nb, cx, HKV * HD), F32), _sds((nb, cx, HKV * HD), F32)],
        compiler_params=_params(("parallel", "parallel", "arbitrary")),
    )(q16, kx16, kc16, px, px, dao, o_att, lse)


def _qk_prep_bwd(dt, px, nw, cos, sin, rows, row_off, col_off, heads, hb, seq, tm, name):
    rope = cos is not None
    rb0 = row_off // tm
    pb = seq // tm if rope else 1
    bw = hb * HD

    def body(*refs):
        if rope:
            d_ref, x_ref, w_ref, c_ref, s_ref, dx_ref, dw_ref = refs
        else:
            d_ref, x_ref, w_ref, dx_ref, dw_ref = refs
        first = jnp.logical_and(pl.program_id(0) == 0, pl.program_id(1) == 0)
        dw = jnp.zeros((1, HD), F32)
        for h in range(hb):
            sl = slice(h * HD, (h + 1) * HD)
            dtv = d_ref[:, sl]
            if rope:
                dtv = dtv * c_ref[...] + _swap_pairs(dtv * s_ref[...])
            xv = x_ref[:, sl].astype(F32)
            r = lax.rsqrt(jnp.mean(xv * xv, axis=-1, keepdims=True) + EPS)
            xh = xv * r
            dxh = dtv * w_ref[...]
            dx_ref[:, sl] = (r * (dxh - xh * jnp.mean(dxh * xh, axis=-1, keepdims=True))).astype(BF16)
            dw += jnp.sum(dtv * xh, axis=0, keepdims=True)

        @pl.when(first)
        def _():
            dw_ref[...] = dw

        @pl.when(jnp.logical_not(first))
        def _():
            dw_ref[...] += dw

    blk = pl.BlockSpec((tm, bw), lambda i, j: (i, j))
    in_specs = [blk, pl.BlockSpec((tm, bw), lambda i, j: (rb0 + i, col_off // bw + j)),
                pl.BlockSpec((1, HD), lambda i, j: (0, 0))]
    args = [dt, px, nw]
    if rope:
        in_specs += [pl.BlockSpec((tm, HD), lambda i, j: (i % pb, 0))] * 2
        args += [cos, sin]
    return pl.pallas_call(
        body, name=name, grid=(rows // tm, heads // hb), in_specs=in_specs,
        out_specs=[blk, pl.BlockSpec((1, HD), lambda i, j: (0, 0))],
        out_shape=[_sds((rows, heads * HD), BF16), _sds((1, HD), F32)],
        compiler_params=_params(("arbitrary", "arbitrary")),
    )(*args)


def _ret_bwd(px, lg, do16, hist_f, hist_b, nb, nc):
    t_rows = nb * nc * CH

    def body(lg_ref, *refs):
        ins = (refs[0:5], refs[7:12])
        do_refs = (refs[5], refs[12])
        h_refs = (refs[6], refs[13])
        outs = (refs[14:17], refs[17:20])
        ds_outs = (refs[20], refs[21])
        dlg_ref = refs[22]
        dss = (refs[23], refs[24])
        c = pl.program_id(1)

        @pl.when(c == 0)
        def _():
            dss[0][...] = jnp.zeros_like(dss[0])
            dss[1][...] = jnp.zeros_like(dss[1])
            dlg_ref[...] = jnp.zeros_like(dlg_ref)

        for d in range(2):
            dq_ref, dk_ref, dv_ref = outs[d]
            for h in range(RH):
                lg_d = lg_ref[d, h]
                mask, relf, qd, qe, kd, ke = _decays(lg_d, d == 0)
                g_ch = jnp.exp(lg_d * CH)
                q, k, v16 = _head_qkv(ins[d], h)
                q16 = q.astype(BF16)
                k16 = k.astype(BF16)
                do16v = do_refs[d][:, h * DV:(h + 1) * DV]
                st16 = h_refs[d][h]
                dst = dss[d][h]
                dst16 = dst.astype(BF16)
                a = _dot(q16, k16, NT) * mask
                dp = _dot(do16v, v16, NT)
                da16 = (dp * mask).astype(BF16)
                dq_cross = _dot(do16v, st16, NT) * qd
                dq_ref[:, h * DK:(h + 1) * DK] = (_dot(da16, k16) + dq_cross).astype(BF16)
                dk_state = _dot(v16, dst16, NT) * kd
                dk_ref[:, h * DK:(h + 1) * DK] = ((_dot(da16, q16, TN) + dk_state) * (DK ** -0.5)).astype(BF16)
                dv = _dot(a.astype(BF16), do16v, TN) + _dot((k * kd).astype(BF16), dst16)
                dv_ref[:, h * DV:(h + 1) * DV] = dv.astype(BF16)
                dlg = (jnp.sum(relf * a * dp)
                       + jnp.sum(qe * jnp.sum(q * dq_cross, axis=-1, keepdims=True))
                       + jnp.sum(ke * jnp.sum(k * dk_state, axis=-1, keepdims=True))
                       + CH * g_ch * jnp.sum(dst * st16.astype(F32)))
                row = d * RH + h
                dlg_ref[row:row + 1, :] += jnp.broadcast_to(dlg, (1, 128))
                ds_new = g_ch * dst + _dot((q * qd).astype(BF16), do16v, TN)
                dss[d][h] = ds_new

                @pl.when(c == nc - 1)
                def _():
                    ds_outs[d][h] = ds_new

    def fw(b, c):
        return b * nc + nc - 1 - c

    def bw(b, c):
        return b * nc + c

    def rows(rowf, width):
        return pl.BlockSpec((CH, width), lambda b, c: (rowf(b, c), 0))

    def hist(rowf):
        return pl.BlockSpec((None, None, RH, DK, DV), lambda b, c: (b, rowf(0, c), 0, 0, 0))

    st = pl.BlockSpec((None, RH, DK, DV), lambda b, c: (b, 0, 0, 0))
    in_specs = [pl.BlockSpec(memory_space=pltpu.SMEM)]
    out_specs = []
    for rowf in (fw, bw):
        in_specs += _wide_specs(rowf) + [rows(rowf, RH * DV), hist(rowf)]
        out_specs += [rows(rowf, RH * DK), rows(rowf, RH * DK), rows(rowf, RH * DV)]
    out_specs += [st, st, pl.BlockSpec((None, 8, 128), lambda b, c: (b, 0, 0))]
    qk = _sds((t_rows, RH * DK), BF16)
    vv = _sds((t_rows, RH * DV), BF16)
    return pl.pallas_call(
        body, name="ret_bwd", grid=(nb, nc), in_specs=in_specs, out_specs=out_specs,
        out_shape=[qk, qk, vv, qk, qk, vv, _sds((nb, RH, DK, DV), F32), _sds((nb, RH, DK, DV), F32),
                   _sds((nb, 8, 128), F32)],
        scratch_shapes=[pltpu.VMEM((RH, DK, DV), F32), pltpu.VMEM((RH, DK, DV), F32)],
        compiler_params=_params(("parallel", "arbitrary")),
    )(lg, *([px] * 5), do16, hist_f, *([px] * 5), do16, hist_b)


def _ctx_state_bwd(px, lg, ds_f, ds_b, nb, t_rows, cx):
    rb = t_rows // cx

    def body(lg_ref, k_ref, v_ref, dsf_ref, dsb_ref, dk_ref, dv_ref, dlg_ref):
        h = pl.program_id(1)
        pos = lax.broadcasted_iota(jnp.int32, (cx, 1), 0).astype(F32)
        k = k_ref[...].astype(F32) * (DK ** -0.5)
        v16 = v_ref[...].astype(BF16)
        dk = jnp.zeros((cx, DK), F32)
        dv = jnp.zeros((cx, DV), F32)
        dlg_ref[...] = jnp.zeros_like(dlg_ref)
        for d, (ds_ref, e) in enumerate(((dsf_ref, cx - 1.0 - pos), (dsb_ref, pos))):
            w = jnp.exp(lg_ref[d, h] * e)
            ds16 = ds_ref[...].astype(BF16)
            t = _dot(v16, ds16, NT)
            dk += t * w
            dv += _dot((k * w).astype(BF16), ds16)
            dlg = jnp.sum(e * w * jnp.sum(k * t, axis=-1, keepdims=True))
            dlg_ref[d:d + 1, :] = jnp.broadcast_to(dlg, (1, 128))
        dk_ref[...] = (dk * (DK ** -0.5)).astype(BF16)
        dv_ref[...] = dv.astype(BF16)

    st = pl.BlockSpec((None, None, DK, DV), lambda b, h: (b, h, 0, 0))
    return pl.pallas_call(
        body, name="ctx_state_bwd", grid=(nb, RH),
        in_specs=[pl.BlockSpec(memory_space=pltpu.SMEM),
                  pl.BlockSpec((cx, DK), lambda b, h: (rb + b, RK // DK + h)),
                  pl.BlockSpec((cx, DV), lambda b, h: (rb + b, RV // DV + h)), st, st],
        out_specs=[pl.BlockSpec((cx, DK), lambda b, h: (b, h)), pl.BlockSpec((cx, DV), lambda b, h: (b, h)),
                   pl.BlockSpec((None, None, 8, 128), lambda b, h: (b, h, 0, 0))],
        out_shape=[_sds((nb * cx, RH * DK), BF16), _sds((nb * cx, RH * DV), BF16), _sds((nb, RH, 8, 128), F32)],
        compiler_params=_params(("parallel", "parallel")),
    )(lg, px, px, ds_f, ds_b)


def _assemble_lat(rows_all, dk_f, dk_b, dv_f, dv_b, dak16, dvx, dq_f, dq_b, drg16, daq16, dag16, dmr16, dma16, tm):
    t_rows = dk_f.shape[0]

    def body(dkf, dkb, dvf, dvb, dak, dav, dqf, dqb, drg, daq, dag, dmr, dma, o_ref):
        o_ref[:, RK:RK + RH * DK] = (dkf[...].astype(F32) + dkb[...].astype(F32)).astype(BF16)
        o_ref[:, RV:RV + RH * DV] = (dvf[...].astype(F32) + dvb[...].astype(F32)).astype(BF16)
        o_ref[:, AK:AK + HKV * HD] = dak[...]
        o_ref[:, AV:AV + HKV * HD] = dav[...].astype(BF16)
        o_ref[:, RQ:RQ + RH * DK] = (dqf[...].astype(F32) + dqb[...].astype(F32)).astype(BF16)
        o_ref[:, RG:RG + RH * DV] = drg[...]
        o_ref[:, AQ:AQ + D] = daq[...]
        o_ref[:, AG:AG + D] = dag[...]
        o_ref[:, MR:MR + D] = dmr[...]
        o_ref[:, MA:MA + D] = dma[...]

    args = (dk_f, dk_b, dv_f, dv_b, dak16, dvx, dq_f, dq_b, drg16, daq16, dag16, dmr16, dma16)
    return pl.pallas_call(
        body, name="assemble_lat", grid=(t_rows // tm,),
        in_specs=[pl.BlockSpec((tm, a.shape[1]), lambda i: (i, 0)) for a in args],
        out_specs=pl.BlockSpec((tm, IN_COLS), lambda i: (i, 0)), out_shape=_sds((rows_all, IN_COLS), BF16),
        compiler_params=_params(("parallel",)),
    )(*args)


def _assemble_ctx(dp_all, dck16, dcv16, dcak16, dvc, t_rows, tm):
    c_rows = dck16.shape[0]
    rb = t_rows // tm

    def body(_, dck, dcv, dcak, dcav, o_ref):
        o_ref[:, RK:RK + RH * DK] = dck[...]
        o_ref[:, RV:RV + RH * DV] = dcv[...]
        o_ref[:, AK:AK + HKV * HD] = dcak[...]
        o_ref[:, AV:AV + HKV * HD] = dcav[...].astype(BF16)
        o_ref[:, KV_COLS:] = jnp.zeros((tm, IN_COLS - KV_COLS), BF16)

    args = (dck16, dcv16, dcak16, dvc)
    return pl.pallas_call(
        body, name="assemble_ctx", grid=(c_rows // tm,),
        in_specs=[pl.BlockSpec(memory_space=pl.ANY)]
        + [pl.BlockSpec((tm, a.shape[1]), lambda i: (i, 0)) for a in args],
        out_specs=pl.BlockSpec((tm, IN_COLS), lambda i: (rb + i, 0)), out_shape=_sds(dp_all.shape, BF16),
        input_output_aliases={0: 0},
        compiler_params=_params(("parallel",)),
    )(dp_all, *args)


def _norm_bwd(dh, x2, mod3, norm_w, dxn, row_off, rows_per_group, group0, tm, name):
    with_dx = dxn is not None
    rows = x2.shape[0]
    rb0 = row_off // tm
    bpg = rows_per_group // tm
    ngroups = rows // rows_per_group

    def body(*refs):
        if with_dx:
            dh_ref, x_ref, sc_ref, nw_ref, dxn_ref, dx_ref, dsh_ref, dsc_ref, dnw_ref = refs
        else:
            dh_ref, x_ref, sc_ref, nw_ref, dsh_ref, dsc_ref, dnw_ref = refs
        i = pl.program_id(0)
        dhv = dh_ref[...]
        xv = x_ref[...]
        nw = nw_ref[...]
        r = lax.rsqrt(jnp.mean(xv * xv, axis=-1, keepdims=True) + EPS)
        xh = xv * r
        dm = dhv * (1.0 + sc_ref[...])
        dsh = jnp.sum(dhv, axis=0, keepdims=True)
        dsc = jnp.sum(dhv * (xh * nw), axis=0, keepdims=True)
        dnw = jnp.sum(dm * xh, axis=0, keepdims=True)
        if with_dx:
            dxh = dm * nw
            dx_ref[...] = dxn_ref[...] + r * (dxh - xh * jnp.mean(dxh * xh, axis=-1, keepdims=True))

        @pl.when(i % bpg == 0)
        def _():
            dsh_ref[...] = dsh
            dsc_ref[...] = dsc

        @pl.when(i % bpg != 0)
        def _():
            dsh_ref[...] += dsh
            dsc_ref[...] += dsc

        @pl.when(i == 0)
        def _():
            dnw_ref[...] = dnw

        @pl.when(i > 0)
        def _():
            dnw_ref[...] += dnw

    grp = pl.BlockSpec((None, 1, D), lambda i: (i // bpg, 0, 0))
    in_specs = [pl.BlockSpec((tm, D), lambda i: (rb0 + i, 0)), pl.BlockSpec((tm, D), lambda i: (i, 0)),
                pl.BlockSpec((None, 1, D), lambda i: (group0 + i // bpg, 0, 1)),
                pl.BlockSpec((1, D), lambda i: (0, 0))]
    args = [dh, x2, mod3, norm_w]
    out_specs = [grp, grp, pl.BlockSpec((1, D), lambda i: (0, 0))]
    out_shape = [_sds((ngroups, 1, D), F32), _sds((ngroups, 1, D), F32), _sds((1, D), F32)]
    if with_dx:
        in_specs.append(pl.BlockSpec((tm, D), lambda i: (i, 0)))
        args.append(dxn)
        out_specs.insert(0, pl.BlockSpec((tm, D), lambda i: (i, 0)))
        out_shape.insert(0, _sds((rows, D), F32))
    return pl.pallas_call(
        body, name=name, grid=(rows // tm,), in_specs=in_specs, out_specs=out_specs, out_shape=out_shape,
        compiler_params=_params(("arbitrary",)),
    )(*args)


def _small_final(dmod_all, dmodc_parts, c_rows, dm_loc_rows, nw_parts, misc_parts, c_ctx, r_pad, w_ada16):
    loc = dm_loc_rows.shape[1]

    def body(dm_ref, dmc_ref, c_ref, dml_ref, nwp_ref, mp_ref, cc_ref, r_ref, w_ref,
             gb_ref, gc_ref, gnw_ref, misc_ref, gwa_ref):
        dmc = jnp.sum(dmc_ref[...], axis=0, keepdims=True)
        gb_ref[...] = jnp.sum(dm_ref[...], axis=0, keepdims=True) + dmc
        dsc = _dot(jnp.broadcast_to(dmc, (8, 3 * D)).astype(BF16), w_ref[...], NT)[0:1, :]
        gc_ref[...] = dsc * _dsilu(cc_ref[...])
        gnw_ref[...] = jnp.sum(nwp_ref[...], axis=0, keepdims=True)
        misc = jnp.sum(mp_ref[...], axis=0, keepdims=True)
        y = jnp.exp2(r_ref[...])
        lane = lax.broadcasted_iota(jnp.int32, (1, D), 1)
        is_decay = jnp.logical_and(lane >= 2 * HD, lane < 2 * HD + 2 * RH)
        misc_ref[...] = misc * jnp.where(is_decay, -(y * np.float32(np.log(2.0))) / (1.0 - y), 1.0)
        gwa_ref[...] = _dot(_silu(c_ref[...]).astype(BF16), dml_ref[...].astype(BF16), TN)

    return pl.pallas_call(
        body, name="small_final",
        out_shape=[_sds((1, 3 * D), F32), _sds((1, D), F32), _sds((1, D), F32), _sds((1, D), F32), _sds((D, loc), F32)],
        compiler_params=pltpu.CompilerParams(vmem_limit_bytes=VMEM_LIMIT),
    )(dmod_all, dmodc_parts, c_rows, dm_loc_rows, nw_parts, misc_parts, c_ctx, r_pad, w_ada16)


def _adamw(w, g, m, v, name):
    rows, cols = w.shape
    tm = _pick(rows, 256, 8)
    bc1 = 1.0 - B1 ** STEP
    bc2 = 1.0 - B2 ** STEP

    def body(w_ref, g_ref, m_ref, v_ref, d_ref, nm_ref, nv_ref):
        g_ = g_ref[...]
        nm = B1 * m_ref[...] + (1.0 - B1) * g_
        nv = B2 * v_ref[...] + (1.0 - B2) * (g_ * g_)
        nm_ref[...] = nm
        nv_ref[...] = nv
        d_ref[...] = -LR * ((nm / bc1) / (jnp.sqrt(nv / bc2) + ADAM_EPS) + WD * w_ref[...])

    blk = pl.BlockSpec((tm, cols), lambda i: (i, 0))
    return pl.pallas_call(
        body, name=name, grid=(rows // tm,), in_specs=[blk] * 4, out_specs=[blk] * 3,
        out_shape=[_sds((rows, cols), F32)] * 3, compiler_params=_params(("parallel",)),
    )(w, g, m, v)


def _mesh_pos():
    return lax.axis_index("x"), lax.axis_index("y"), lax.axis_index("c")


def _all_gather(arrs, name):
    n = len(arrs)

    def body(*refs):
        ins, outs = refs[:n], refs[n:2 * n]
        send_sems, recv_sems, local_sems = refs[2 * n:]
        x, y, c = _mesh_pos()
        me, sib = (x, y, c), (x, y, 1 - c)
        chips = [(1 - x, y), (x, 1 - y), (1 - x, 1 - y)]

        def slot(p):
            return 4 * p[0] + 2 * p[1] + p[2]

        def copy(a, k, block, to, own):
            dst = outs[a].at[slot(block)]
            return pltpu.make_async_remote_copy(
                src_ref=ins[a] if own else dst, dst_ref=dst, send_sem=send_sems.at[a, k], recv_sem=recv_sems.at[a, k],
                device_id=to, device_id_type=MESH_T)

        mine = [pltpu.make_async_copy(ins[a], outs[a].at[slot(me)], local_sems.at[a]) for a in range(n)]
        for cp in mine:
            cp.start()
        first = []
        for a in range(n):
            first.append(copy(a, 0, me, sib, True))
            first += [copy(a, 1 + j, me, (*chip, c), True) for j, chip in enumerate(chips)]
        for cp in first:
            cp.start()
        passed = []
        for j, chip in enumerate(chips):
            for a in range(n):
                copy(a, 1 + j, (*chip, c), me, False).wait_recv()
                fwd = copy(a, 4 + j, (*chip, c), sib, False)
                fwd.start()
                passed.append(fwd)
        for a in range(n):
            copy(a, 0, sib, me, False).wait_recv()
            for j, chip in enumerate(chips):
                copy(a, 4 + j, (*chip, 1 - c), me, False).wait_recv()
        for cp in first + passed:
            cp.wait_send()
        for cp in mine:
            cp.wait()

    hbm = pl.BlockSpec(memory_space=pl.ANY)
    return pl.pallas_call(
        body, name=name, in_specs=[hbm] * n, out_specs=[hbm] * n,
        out_shape=[_sds((N_DEV,) + a.shape, a.dtype) for a in arrs],
        scratch_shapes=[pltpu.SemaphoreType.DMA((n, 7)), pltpu.SemaphoreType.DMA((n, 7)), pltpu.SemaphoreType.DMA((n,))],
    )(*arrs)


def _pair_exchange(arrs, name):
    n = len(arrs)

    def body(*refs):
        ins, outs = refs[:n], refs[n:2 * n]
        send_sems, recv_sems = refs[2 * n:]
        x, y, c = _mesh_pos()
        sib = (x, y, 1 - c)
        sends = []
        for a in range(n):
            for k in range(4):
                sends.append(pltpu.make_async_remote_copy(
                    src_ref=ins[a].at[2 * k + 1 - c], dst_ref=outs[a].at[k], send_sem=send_sems.at[a, k],
                    recv_sem=recv_sems.at[a, k], device_id=sib, device_id_type=MESH_T))
        for cp in sends:
            cp.start()
        for cp in sends:
            cp.wait_recv()
        for cp in sends:
            cp.wait_send()

    hbm = pl.BlockSpec(memory_space=pl.ANY)
    return pl.pallas_call(
        body, name=name, in_specs=[hbm] * n, out_specs=[hbm] * n,
        out_shape=[_sds((4,) + a.shape[1:], a.dtype) for a in arrs],
        scratch_shapes=[pltpu.SemaphoreType.DMA((n, 4)), pltpu.SemaphoreType.DMA((n, 4))],
    )(*arrs)


def _pair_add(part, got, core, name):
    _, rows, cols = part.shape
    tm = _pick(rows, 256, 16)
    p4 = part.reshape(4, 2, rows, cols)

    def body(core_ref, p_ref, g_ref, o_ref):
        o_ref[...] = (p_ref[...].astype(F32) + g_ref[...].astype(F32)).astype(BF16)

    blk = pl.BlockSpec((None, tm, cols), lambda k, i, cr: (k, i, 0))
    return pl.pallas_call(
        body, name=name,
        grid_spec=pltpu.PrefetchScalarGridSpec(
            num_scalar_prefetch=1, grid=(4, rows // tm),
            in_specs=[pl.BlockSpec((None, None, tm, cols), lambda k, i, cr: (k, cr[0], i, 0)), blk], out_specs=blk),
        out_shape=_sds((4, rows, cols), BF16), compiler_params=_params(("parallel", "parallel")),
    )(core, p4, got)


def _chip_sum(pair_sums, landed, chip, name):
    _, rows, cols = pair_sums.shape
    tm = _pick(rows, 256, 16)

    def body(chip_ref, s_ref, l_ref, o_ref):
        acc = s_ref[...].astype(F32)
        for j in range(3):
            acc = acc + l_ref[j].astype(F32)
        o_ref[...] = acc

    return pl.pallas_call(
        body, name=name,
        grid_spec=pltpu.PrefetchScalarGridSpec(
            num_scalar_prefetch=1, grid=(rows // tm,),
            in_specs=[pl.BlockSpec((None, tm, cols), lambda i, ch: (ch[0], i, 0)),
                      pl.BlockSpec((3, tm, cols), lambda i, ch: (0, i, 0))],
            out_specs=pl.BlockSpec((tm, cols), lambda i, ch: (i, 0))),
        out_shape=_sds((rows, cols), F32), compiler_params=_params(("parallel",)),
    )(chip, pair_sums, landed)


_HBM = pl.BlockSpec(memory_space=pltpu.HBM)
_SEM = pl.BlockSpec(memory_space=pltpu.SEMAPHORE)
_EFFECT = pltpu.SideEffectType.DATAFLOW_SIDE_EFFECTING


def _chip_routes(n):
    def plan(x, y, c):
        routes = []
        for a in range(n):
            for j in range(1, 4):
                px, py = x ^ (j >> 1), y ^ (j & 1)
                routes.append((a, 2 * px + py, (px, py, c), j - 1))
        return routes
    return plan, 3 * n


def _bcast_routes(n):
    def plan(x, y, c):
        routes = []
        for a in range(n):
            for k in range(1, N_DEV):
                peer = (x ^ ((k >> 2) & 1), y ^ ((k >> 1) & 1), c ^ (k & 1))
                routes.append((a, 0, peer, 4 * x + 2 * y + c))
        return routes
    return plan, 7 * n


def _route_copies(srcs, lands, send_sems, recv_sems, routes):
    return [pltpu.make_async_remote_copy(
        src_ref=srcs[a].at[sb], dst_ref=lands[a].at[lb], send_sem=send_sems.at[r], recv_sem=recv_sems.at[r],
        device_id=peer, device_id_type=MESH_T) for r, (a, sb, peer, lb) in enumerate(routes)]


def _exchange_start(srcs, lands, routes, name, after=()):
    plan, count = routes
    n = len(srcs)
    n_in = 2 * n + len(after)

    def body(*refs):
        send_sems, recv_sems = refs[n_in], refs[n_in + 1]
        token = refs[-1]
        for cp in _route_copies(refs[:n], refs[n:2 * n], send_sems, recv_sems, plan(*_mesh_pos())):
            cp.start()
        token[...] = jnp.zeros_like(token)

    args = [pltpu.with_memory_space_constraint(a, pltpu.HBM) for a in list(srcs) + list(lands)]
    out = pl.pallas_call(
        body, name=name,
        out_shape=(pltpu.SemaphoreType.DMA((count,)), pltpu.SemaphoreType.DMA((count,)),
                   *[pltpu.HBM(a.shape, a.dtype) for a in args], _sds((8, 128), F32)),
        in_specs=[_HBM] * (2 * n) + [pl.BlockSpec(memory_space=pl.ANY)] * len(after),
        out_specs=(_SEM, _SEM, *([_HBM] * (2 * n)), pl.BlockSpec(memory_space=pltpu.VMEM)),
        input_output_aliases={i: 2 + i for i in range(2 * n)},
        compiler_params=pltpu.CompilerParams(has_side_effects=_EFFECT),
    )(*args, *after)
    return (out[0], out[1], list(out[2:2 + 2 * n]), routes), out[-1]


def _exchange_wait_some(state, after, only, name):
    send_sems, recv_sems, bufs, (plan, count) = state
    n = len(bufs) // 2

    def body(*refs):
        send_s, recv_s = refs[2 * n], refs[2 * n + 1]
        for r, cp in enumerate(_route_copies(refs[:n], refs[n:2 * n], send_s, recv_s, plan(*_mesh_pos()))):
            if only is None or r in only:
                cp.wait_send()
                cp.wait_recv()

    out = pl.pallas_call(
        body, name=name, out_shape=tuple(pltpu.HBM(a.shape, a.dtype) for a in bufs),
        in_specs=[_HBM] * (2 * n) + [_SEM, _SEM, pl.BlockSpec(memory_space=pl.ANY)], out_specs=tuple([_HBM] * (2 * n)),
        input_output_aliases={i: i for i in range(2 * n)},
        compiler_params=pltpu.CompilerParams(has_side_effects=_EFFECT),
    )(*bufs, send_sems, recv_sems, after)
    return (send_sems, recv_sems, list(out), (plan, count)), list(out[:n]), list(out[n:])


def _exchange_wait(state, after, name):
    _, srcs, lands = _exchange_wait_some(state, after, None, name)
    return srcs, lands


def _group_routes(js):
    def plan(x, y, c):
        return [(0, 0, (x ^ (j >> 1), y ^ (j & 1), c), 2 * j + c) for j in js]
    return plan, len(js)


def _pair_fill(groups, j, name, after=()):
    def body(*refs):
        g_ref, send_sem, recv_sem = refs[-3:]
        x, y, c = _mesh_pos()
        mine = g_ref.at[2 * j + c]
        to_sib = pltpu.make_async_remote_copy(src_ref=mine, dst_ref=mine, send_sem=send_sem, recv_sem=recv_sem,
                                              device_id=(x, y, 1 - c), device_id_type=MESH_T)
        to_sib.start()
        pltpu.make_async_remote_copy(src_ref=mine, dst_ref=g_ref.at[2 * j + 1 - c], send_sem=send_sem, recv_sem=recv_sem,
                                     device_id=(x, y, 1 - c), device_id_type=MESH_T).wait_recv()
        to_sib.wait_send()

    hbm = pl.BlockSpec(memory_space=pl.ANY)
    return pl.pallas_call(
        body, name=name, in_specs=[hbm] * (1 + len(after)), out_specs=hbm, out_shape=_sds(groups.shape, groups.dtype),
        input_output_aliases={0: 0},
        scratch_shapes=[pltpu.SemaphoreType.DMA, pltpu.SemaphoreType.DMA],
    )(groups, *after)


def _in_proj_group(h_all, groups, j, chip, px_prev, after, name):
    rows_all = h_all.shape[0]
    gcols = IN_COLS // 4
    tm = _pick(rows_all, 1536, 128)
    g4 = groups.reshape(4, gcols, D)

    n_lead = (1 if px_prev is not None else 0) + len(after)
    lead = ([px_prev] if px_prev is not None else []) + list(after)

    def body(chip_ref, *refs):
        h_ref, w_ref, o_ref = refs[n_lead:]
        o_ref[...] = _dot(h_ref[...], w_ref[...], NT).astype(BF16)
    return pl.pallas_call(
        body, name=name,
        grid_spec=pltpu.PrefetchScalarGridSpec(
            num_scalar_prefetch=1, grid=(rows_all // tm,),
            in_specs=[pl.BlockSpec(memory_space=pl.ANY)] * n_lead
            + [pl.BlockSpec((tm, D), lambda i, ch: (i, 0)), pl.BlockSpec((None, gcols, D), lambda i, ch: (j, 0, 0))],
            out_specs=pl.BlockSpec((tm, gcols), lambda i, ch: (i, ch[0] ^ j))),
        out_shape=_sds((rows_all, IN_COLS), BF16),
        input_output_aliases={1: 0} if px_prev is not None else {},
        compiler_params=_params(("parallel",)),
    )(chip, *lead, h_all, g4)


def _d_h_groups(dp_all, groups, chip, after):
    rows_all = dp_all.shape[0]
    gcols = IN_COLS // 4
    tm = _pick(rows_all, 1536, 128)
    g4 = groups.reshape(4, gcols, D)
    n_lead = len(after)

    def body(chip_ref, *refs):
        a_ref, w_ref, o_ref = refs[n_lead:]
        j = pl.program_id(1)
        part = _dot(a_ref[...], w_ref[...])

        @pl.when(j == 0)
        def _():
            o_ref[...] = part

        @pl.when(j > 0)
        def _():
            o_ref[...] += part

    return pl.pallas_call(
        body, name="d_h",
        grid_spec=pltpu.PrefetchScalarGridSpec(
            num_scalar_prefetch=1, grid=(rows_all // tm, 4),
            in_specs=[pl.BlockSpec(memory_space=pl.ANY)] * n_lead
            + [pl.BlockSpec((tm, gcols), lambda i, j, ch: (i, ch[0] ^ j)),
               pl.BlockSpec((None, gcols, D), lambda i, j, ch: (j, 0, 0))],
            out_specs=pl.BlockSpec((tm, D), lambda i, j, ch: (i, 0))),
        out_shape=_sds((rows_all, D), F32),
        compiler_params=_params(("parallel", "arbitrary")),
    )(chip, *after, dp_all, g4)


def _reduce_scatter_start(parts, core, name):
    got = _pair_exchange(parts, name + "_pair")
    sums = [_pair_add(p, g, core, "%s_add_%d" % (name, i)) for i, (p, g) in enumerate(zip(parts, got))]
    lands = [lax.empty((3,) + s_.shape[1:], BF16) for s_ in sums]
    return _exchange_start(sums, lands, _chip_routes(len(sums)), name + "_start")


def _reduce_scatter_finish(rs_state, after, chip, name):
    sums, landed = _exchange_wait(rs_state, after, name + "_wait")
    return [_chip_sum(s_, l_, chip, "%s_sum_%d" % (name, i)) for i, (s_, l_) in enumerate(zip(sums, landed))]


def _local_step(x, c, ctx, c_ctx, norm_w, b_ada, ret_log2_decay, q_norm_w, k_norm_w, loss_target,
                w_ada16, proj_in, proj_back, get_w_o, on_out_grads, on_in_grad):
    nb, seq, _ = x.shape
    cx = ctx.shape[1]
    t_rows, c_rows = nb * seq, nb * cx
    rows_all = t_rows + c_rows
    nc = seq // CH
    tm = _pick(seq, 256, 128)
    te = _pick(seq, 512, 128)
    assert cx % tm == 0 and t_rows % cx == 0 and seq % GRID_W == 0

    x2 = x.reshape(t_rows, D)
    ctx2 = ctx.reshape(c_rows, D)
    tgt = loss_target.reshape(t_rows, D)
    c8 = jnp.zeros((8, D), F32).at[:nb].set(c).at[nb].set(c_ctx)
    lg = _log_gamma(ret_log2_decay)
    cos, sin = _rope_tables(seq)

    mod = _mod_fwd(c8, w_ada16, b_ada)
    mod3 = mod[:, None, :]
    h_all = _norm_fwd(x2, mod3, norm_w, rows_all, 0, seq, 0, None, te, "norm_fwd")
    h_all = _norm_fwd(ctx2, mod3, norm_w, rows_all, t_rows, c_rows, nb, h_all, tm, "norm_fwd_ctx")
    px = proj_in(h_all)
    s0f, s0b = _ctx_state(px, lg, nb, t_rows, cx)
    o_f, o_b, hist_f, hist_b = _ret_fwd(px, lg, s0f, s0b, nb, nc)
    yret16 = _ret_post(o_f, o_b, px, te)
    q16 = _qk_prep(px, q_norm_w, cos, sin, t_rows, 0, AQ, HQ, 4, seq, te, "q_prep")
    kx16 = _qk_prep(px, k_norm_w, cos, sin, t_rows, 0, AK, HKV, HKV, seq, te, "k_prep")
    kc16 = _qk_prep(px, k_norm_w, None, None, c_rows, t_rows, AK, HKV, HKV, seq, tm, "kc_prep")
    o_att, yatt16, lse = _att_fwd(q16, kx16, kc16, px, nb, seq, cx, tm)
    w_o_ret16, w_o_att16, w_out16 = get_w_o(lse)
    a_ret, a_att, y16 = _merge(yret16, yatt16, px, w_o_ret16, w_o_att16, te)
    dxn, dout16, dgate, loss_b = _outproj(y16, w_out16, x2, tgt, mod3, nb, seq, te)

    gw_out = _matmul(y16, dout16, ta=True, tm=D, tn=D, tk=D, out_dtype=BF16, name="gw_out")
    da_ret16, da_att16, dmr16, dma16 = _bwd_merge(dout16, w_out16, px, a_ret, a_att, te)
    gw_o_ret = _matmul(yret16, da_ret16, ta=True, tm=D, tn=D, tk=D, out_dtype=BF16, name="gw_o_ret")
    gw_o_att = _matmul(yatt16, da_att16, ta=True, tm=D, tn=D, tk=D, out_dtype=BF16, name="gw_o_att")
    out_state, out_started = on_out_grads([gw_o_ret, gw_o_att, gw_out])
    do16, drg16 = _bwd_branch_ret(da_ret16, w_o_ret16, px, o_f, o_b, te, after=out_started)
    dao, dag16 = _bwd_branch_att(da_att16, w_o_att16, px, o_att, te)
    dq_rot, dkx, dvx, dkc, dvc = _att_bwd(q16, kx16, kc16, px, dao, o_att, lse, nb, seq, cx, tm)
    daq16, gq = _qk_prep_bwd(dq_rot, px, q_norm_w, cos, sin, t_rows, 0, AQ, HQ, 4, seq, te, "q_prep_bwd")
    dak16, gk_lat = _qk_prep_bwd(dkx.reshape(t_rows, HKV * HD), px, k_norm_w, cos, sin, t_rows, 0, AK, HKV, HKV, seq, te,
                                 "k_prep_bwd")
    dcak16, gk_ctx = _qk_prep_bwd(dkc.reshape(c_rows, HKV * HD), px, k_norm_w, None, None, c_rows, t_rows, AK, HKV, HKV,
                                  seq, tm, "kc_prep_bwd")
    dq_f, dk_f, dv_f, dq_b, dk_b, dv_b, ds_f, ds_b, dlg_scan = _ret_bwd(px, lg, do16, hist_f, hist_b, nb, nc)
    dck16, dcv16, dlg_ctx = _ctx_state_bwd(px, lg, ds_f, ds_b, nb, t_rows, cx)
    dp_all = _assemble_lat(rows_all, dk_f, dk_b, dv_f, dv_b, dak16, dvx.reshape(t_rows, HKV * HD), dq_f, dq_b, drg16,
                           daq16, dag16, dmr16, dma16, tm)
    dp_all = _assemble_ctx(dp_all, dck16, dcv16, dcak16, dvc.reshape(c_rows, HKV * HD), t_rows, tm)
    gw_in_t = _matmul(dp_all, h_all, ta=True, tm=1536, tn=D, tk=1536, out_dtype=BF16, name="gw_in")
    in_state, in_started = on_in_grad(gw_in_t)
    dh = proj_back(dp_all, in_started)
    grad_x, dsh, dsc, gnw_lat = _norm_bwd(dh, x2, mod3, norm_w, dxn, 0, seq, 0, te, "norm_bwd")
    dsh_c, dsc_c, gnw_ctx = _norm_bwd(dh, ctx2, mod3, norm_w, None, t_rows, c_rows, nb, tm, "norm_bwd_ctx")

    dlg = (jnp.sum(dlg_scan[:, :, 0], axis=0) + jnp.sum(dlg_ctx[:, :, :2, 0], axis=0).T.reshape(2 * RH)).reshape(1, 2 * RH)
    misc = jnp.concatenate([gq, gk_lat + gk_ctx, dlg, jnp.sum(loss_b[:, 0, 0]).reshape(1, 1),
                            jnp.zeros((1, D - 2 * HD - 2 * RH - 1), F32)], axis=1)
    rows = []
    for b in range(nb):
        rows += [dsh[b], dsc[b], dgate[b]]
    rows += [dsh_c[0], dsc_c[0]] + [c[b:b + 1] for b in range(nb)] + [gnw_lat + gnw_ctx, misc]
    payload = jnp.concatenate(rows + [jnp.zeros((PAY_ROWS - len(rows), D), F32)], axis=0)
    return grad_x.reshape(nb, seq, D), out_state, in_state, payload


def _finish_small(gathered, nb, c_ctx, ret_log2_decay, w_ada16, dev):
    n_dev = gathered.shape[0]
    loc = 3 * D // n_dev
    dmod_all = gathered[:, :3 * nb].reshape(n_dev * nb, 3 * D)
    dmodc_parts = jnp.concatenate([gathered[:, 3 * nb:3 * nb + 2].reshape(n_dev, 2 * D), jnp.zeros((n_dev, D), F32)], axis=1)
    c_all = gathered[:, 3 * nb + 2:4 * nb + 2].reshape(n_dev * nb, D)
    nw_parts = gathered[:, 4 * nb + 2]
    misc_parts = gathered[:, 4 * nb + 3]
    n_rows = n_dev * nb + n_dev
    pad = (-n_rows) % 16
    c_rows = jnp.concatenate([c_all, jnp.broadcast_to(c_ctx.reshape(1, D), (n_dev, D)), jnp.zeros((pad, D), F32)], axis=0)
    dm_rows = jnp.concatenate([dmod_all, dmodc_parts, jnp.zeros((pad, 3 * D), F32)], axis=0)
    dm_loc_rows = lax.dynamic_slice_in_dim(dm_rows, dev * loc, loc, axis=1)
    r_pad = jnp.full((1, D), -1.0, F32).at[:, 2 * HD:2 * HD + 2 * RH].set(ret_log2_decay.reshape(1, 2 * RH))
    gb, gc, gnw, misc, gwa = _small_final(dmod_all, dmodc_parts, c_rows, dm_loc_rows, nw_parts, misc_parts,
                                          c_ctx.reshape(1, D), r_pad, w_ada16)
    return (gb, gc, gnw, misc[:, :HD], misc[:, HD:2 * HD], misc[:, 2 * HD:2 * HD + 2 * RH], gwa,
            misc[0, 2 * HD + 2 * RH])


def kernel(x, c, ctx, c_ctx, norm_w, w_ada, b_ada, w_in, ret_log2_decay, q_norm_w, k_norm_w, w_o_ret, w_o_att, w_out, loss_target, m_c_ctx, m_norm_w, m_w_ada, m_b_ada, m_w_in, m_ret_log2_decay, m_q_norm_w, m_k_norm_w, m_w_o_ret, m_w_o_att, m_w_out, v_c_ctx, v_norm_w, v_w_ada, v_b_ada, v_w_in, v_ret_log2_decay, v_q_norm_w, v_k_norm_w, v_w_o_ret, v_w_o_att, v_w_out):
    nb = x.shape[0]
    mx, my, mc = _mesh_pos()
    dev = 4 * mx + 2 * my + mc
    core = jnp.reshape(mc, (1,)).astype(jnp.int32)
    chip = jnp.reshape(2 * mx + my, (1,)).astype(jnp.int32)

    (g_ada,) = _all_gather([w_ada[0].astype(BF16)], "gather_ada")
    w_ada16 = jnp.transpose(g_ada, (1, 0, 2)).reshape(D, 3 * D)

    w_in_t = jnp.transpose(w_in[0])
    in_shard = w_in_t.astype(BF16)
    groups = lax.dynamic_update_slice(lax.empty((N_DEV,) + in_shard.shape, BF16), in_shard[None], (mc, 0, 0))
    groups = _pair_fill(groups, 0, "gather_in_pair", after=(g_ada,))
    (near_send, near_recv, near_bufs, near_routes), gin_token = _exchange_start(
        [in_shard[None]], [groups], _group_routes((1, 2)), "gather_in_start")
    w_in_groups, wo_states = [], []
    wo_shards = [w_[0].astype(BF16) for w_ in (w_o_ret, w_o_att, w_out)]
    wo_lands = [lax.dynamic_update_slice(lax.empty((N_DEV,) + s_.shape, BF16), s_[None], (dev, 0, 0)) for s_ in wo_shards]

    def proj_in(h_all):
        src, groups = near_bufs
        px = _in_proj_group(h_all, groups, 0, chip, None, (gin_token,), "in_proj_0")
        (far_send, far_recv, (src, groups), far_routes), far_token = _exchange_start(
            [src], [groups], _group_routes((3,)), "gather_in_start_far", after=(px,))
        wo_state, wo_token = _exchange_start([s_[None] for s_ in wo_shards], wo_lands, _bcast_routes(3),
                                             "gather_wo_start", after=(far_token,))
        wo_states.append(wo_state)
        for j in (1, 2, 3):
            state = ((near_send, near_recv, [src, groups], near_routes) if j < 3 else
                     (far_send, far_recv, [src, groups], far_routes))
            _, (src,), (groups,) = _exchange_wait_some(state, wo_token if j == 1 else px, (j - 1,) if j < 3 else None,
                                                       "gather_in_wait_%d" % j)
            groups = _pair_fill(groups, j, "gather_in_fill_%d" % j)
            px = _in_proj_group(h_all, groups, j, chip, px, (), "in_proj_%d" % j)
        w_in_groups.append(groups)
        return px

    def proj_back(dp_all, after):
        return _d_h_groups(dp_all, w_in_groups[0], chip, after)

    def get_w_o(after):
        _, (l_ret, l_att, l_out) = _exchange_wait(wo_states[0], after, "gather_wo_wait")
        return l_ret.reshape(RH * DV, D), l_att.reshape(D, D), l_out.reshape(D, D)

    def on_out_grads(grads):
        parts = [g_.reshape(N_DEV, g_.shape[0] // N_DEV, D) for g_ in grads]
        state, token = _reduce_scatter_start(parts, core, "rs_out")
        return state, (token,)

    def on_in_grad(grad):
        state, token = _reduce_scatter_start([grad.reshape(N_DEV, IN_COLS // N_DEV, D)], core, "rs_in")
        return state, (token,)

    grad_x, out_state, in_state, payload = _local_step(
        x, c, ctx, c_ctx, norm_w, b_ada, ret_log2_decay, q_norm_w, k_norm_w, loss_target,
        w_ada16, proj_in, proj_back, get_w_o, on_out_grads, on_in_grad)

    (gathered,) = _all_gather([payload], "gather_small")
    gb, gc, gnw, gq, gk, gr, gwa, loss = _finish_small(gathered, nb, c_ctx, ret_log2_decay, w_ada16, dev)

    g_w_o_ret, g_w_o_att, g_w_out = _reduce_scatter_finish(out_state, gathered, chip, "rs_out")
    (g_w_in_t,) = _reduce_scatter_finish(in_state, gathered, chip, "rs_in")

    grads = [gc.reshape(c_ctx.shape), gnw, gwa[None], gb, g_w_in_t, gr.reshape(ret_log2_decay.shape), gq, gk,
             g_w_o_ret[None], g_w_o_att[None], g_w_out[None]]
    weights = [c_ctx, norm_w, w_ada, b_ada, w_in_t, ret_log2_decay, q_norm_w, k_norm_w, w_o_ret, w_o_att, w_out]
    ms = [m_c_ctx, m_norm_w, m_w_ada, m_b_ada, jnp.transpose(m_w_in[0]), m_ret_log2_decay, m_q_norm_w, m_k_norm_w,
          m_w_o_ret, m_w_o_att, m_w_out]
    vs = [v_c_ctx, v_norm_w, v_w_ada, v_b_ada, jnp.transpose(v_w_in[0]), v_ret_log2_decay, v_q_norm_w, v_k_norm_w,
          v_w_o_ret, v_w_o_att, v_w_out]
    deltas, new_ms, new_vs = [], [], []
    for i, (w, g, m, v) in enumerate(zip(weights, grads, ms, vs)):
        shape2 = (-1, w.shape[-1])
        res = _adamw(w.reshape(shape2), g.reshape(shape2), m.reshape(shape2), v.reshape(shape2), "adamw_%d" % i)
        for lst, r in zip((deltas, new_ms, new_vs), res):
            lst.append(jnp.transpose(r)[None] if i == 4 else r.reshape(w.shape))
    grads[4] = jnp.transpose(g_w_in_t)[None]
    return (loss, grad_x, *grads, *deltas, *new_ms, *new_vs)
```

```python
import numpy as np
import jax
import jax.numpy as jnp
from jax import lax
from jax.experimental import pallas as pl
from jax.experimental.pallas import tpu as pltpu

F32 = jnp.float32
BF16 = jnp.bfloat16

D = 1024
RH, DK, DV, CH = 4, 256, 512, 256
HQ, HKV, HD = 8, 2, 128
GRID_W = 64
ROPE_THETA = 10000.0
EPS = 1e-6
RK, RV, AK, AV, RQ, RG, AQ, AG, MR, MA = 0, 1024, 3072, 3328, 3584, 4608, 6656, 7680, 8704, 9728
IN_COLS = 10752
KV_COLS = 3584
N_DEV = 8
LR, B1, B2, ADAM_EPS, WD, STEP = 0.001, 0.9, 0.999, 1e-08, 0.01, 10
PAY_ROWS = 16
VMEM_LIMIT = 56 * 1024 * 1024
MESH_T = pl.DeviceIdType.MESH

NT = (((1,), (1,)), ((), ()))
TN = (((0,), (0,)), ((), ()))
SM_C = (HD ** -0.5) * float(np.log2(np.e))


def _params(sem):
    return pltpu.CompilerParams(dimension_semantics=sem, vmem_limit_bytes=VMEM_LIMIT)


def _pick(n, target, mult=8):
    best = None
    for t in range(mult, min(n, target) + 1, mult):
        if n % t == 0:
            best = t
    return best or n


def _dot(a, b, dn=None):
    if dn is None:
        return jnp.dot(a, b, preferred_element_type=F32)
    return lax.dot_general(a, b, dn, preferred_element_type=F32)


def _sig(v):
    return jax.nn.sigmoid(v)


def _silu(v):
    return v * _sig(v)


def _dsilu(v):
    s = _sig(v)
    return s * (1.0 + v * (1.0 - s))


def _sds(shape, dtype):
    return jax.ShapeDtypeStruct(shape, dtype)


def _matmul(a, b, *, ta=False, tb=False, tm, tn, tk, out_dtype, name, after=()):
    m = a.shape[1] if ta else a.shape[0]
    kdim = a.shape[0] if ta else a.shape[1]
    n = b.shape[0] if tb else b.shape[1]
    tm, tn, tk = _pick(m, tm, 128), _pick(n, tn, 128), _pick(kdim, tk, 128)
    nk = kdim // tk
    dn = (((0 if ta else 1,), (1 if tb else 0,)), ((), ()))

    def body(a_ref, b_ref, *rest):
        o_ref, acc_ref = rest[-2:]
        k = pl.program_id(2)
        part = _dot(a_ref[...].astype(BF16), b_ref[...].astype(BF16), dn)
        if nk == 1:
            o_ref[...] = part.astype(o_ref.dtype)
        else:
            @pl.when(k == 0)
            def _():
                acc_ref[...] = part

            @pl.when(k > 0)
            def _():
                acc_ref[...] += part

            @pl.when(k == nk - 1)
            def _():
                o_ref[...] = acc_ref[...].astype(o_ref.dtype)

    a_spec = pl.BlockSpec((tk, tm), lambda i, j, k: (k, i)) if ta else pl.BlockSpec((tm, tk), lambda i, j, k: (i, k))
    b_spec = pl.BlockSpec((tn, tk), lambda i, j, k: (j, k)) if tb else pl.BlockSpec((tk, tn), lambda i, j, k: (k, j))
    return pl.pallas_call(
        body, name=name, grid=(m // tm, n // tn, nk),
        in_specs=[a_spec, b_spec] + [pl.BlockSpec(memory_space=pl.ANY)] * len(after),
        out_specs=pl.BlockSpec((tm, tn), lambda i, j, k: (i, j)), out_shape=_sds((m, n), out_dtype),
        scratch_shapes=[pltpu.VMEM((tm, tn) if nk > 1 else (8, 128), F32)],
        compiler_params=_params(("parallel", "parallel", "arbitrary")),
    )(a, b, *after)


def _log_gamma(r):
    rp = jnp.full((8, 128), -1.0, F32).at[:2, :RH].set(r.reshape(2, RH))

    def body(r_ref, o_ref):
        o_ref[...] = jnp.log1p(-jnp.exp2(r_ref[...]))

    out = pl.pallas_call(body, name="log_gamma", out_shape=_sds((8, 128), F32))(rp)
    return out[:2, :RH]


def _mod_part(c_rows, w_ada_loc16, b_loc):
    def body(c_ref, w_ref, b_ref, o_ref):
        o_ref[...] = _dot(_silu(c_ref[...]).astype(BF16), w_ref[...]) + b_ref[...]

    return pl.pallas_call(
        body, name="mod_part", out_shape=_sds((c_rows.shape[0], w_ada_loc16.shape[1]), F32),
    )(c_rows, w_ada_loc16, b_loc)


def _norm_fwd(x2, mod3, norm_w, rows_all, row_off, rows_per_group, group0, h_prev, tm, name):
    rows = x2.shape[0]
    rb0 = row_off // tm
    bpg = rows_per_group // tm

    def body(*refs):
        x_ref, sh_ref, sc_ref, nw_ref, o_ref = refs[-5:]
        xv = x_ref[...]
        r = lax.rsqrt(jnp.mean(xv * xv, axis=-1, keepdims=True) + EPS)
        o_ref[...] = ((xv * r) * nw_ref[...] * (1.0 + sc_ref[...]) + sh_ref[...]).astype(BF16)

    in_specs = [pl.BlockSpec((tm, D), lambda i: (i, 0)),
                pl.BlockSpec((None, 1, D), lambda i: (group0 + i // bpg, 0, 0)),
                pl.BlockSpec((None, 1, D), lambda i: (group0 + i // bpg, 0, 1)),
                pl.BlockSpec((1, D), lambda i: (0, 0))]
    args = [x2, mod3, mod3, norm_w]
    alias = {}
    if h_prev is not None:
        in_specs.insert(0, pl.BlockSpec(memory_space=pl.ANY))
        args.insert(0, h_prev)
        alias = {0: 0}
    return pl.pallas_call(
        body, name=name, grid=(rows // tm,), in_specs=in_specs,
        out_specs=pl.BlockSpec((tm, D), lambda i: (rb0 + i, 0)), out_shape=_sds((rows_all, D), BF16),
        input_output_aliases=alias, compiler_params=_params(("parallel",)),
    )(*args)


def _decays(lg, fwd):
    ii = lax.broadcasted_iota(jnp.int32, (CH, CH), 0)
    jj = lax.broadcasted_iota(jnp.int32, (CH, CH), 1)
    ri = lax.broadcasted_iota(jnp.int32, (CH, 1), 0).astype(F32)
    rel = (ii - jj) if fwd else (jj - ii)
    relf = jnp.maximum(rel, 0).astype(F32)
    mask = jnp.where(rel >= 0, jnp.exp(lg * relf), 0.0)
    qe = (ri + 1.0) if fwd else (CH - ri)
    ke = (CH - 1.0 - ri) if fwd else ri
    return mask, relf, jnp.exp(lg * qe), qe, jnp.exp(lg * ke), ke


def _wide_specs(rowf):
    return [pl.BlockSpec((CH, 2 * DK), lambda b, c: (rowf(b, c), RQ // (2 * DK))),
            pl.BlockSpec((CH, 2 * DK), lambda b, c: (rowf(b, c), RQ // (2 * DK) + 1)),
            pl.BlockSpec((CH, RH * DK), lambda b, c: (rowf(b, c), RK // (RH * DK))),
            pl.BlockSpec((CH, 2 * DV), lambda b, c: (rowf(b, c), RV // (2 * DV))),
            pl.BlockSpec((CH, 2 * DV), lambda b, c: (rowf(b, c), RV // (2 * DV) + 1))]


def _head_qkv(refs, h):
    q0, q1, k, v0, v1 = refs
    lo = h % 2
    q = (q0, q1)[h // 2][:, lo * DK:(lo + 1) * DK].astype(F32)
    kk = k[:, h * DK:(h + 1) * DK].astype(F32) * (DK ** -0.5)
    v16 = (v0, v1)[h // 2][:, lo * DV:(lo + 1) * DV].astype(BF16)
    return q, kk, v16


def _ctx_state(px, lg, nb, t_rows, cx):
    rb = t_rows // cx

    def body(lg_ref, k_ref, v_ref, sf_ref, sb_ref):
        h = pl.program_id(1)
        pos = lax.broadcasted_iota(jnp.int32, (cx, 1), 0).astype(F32)
        k = k_ref[...].astype(F32) * (DK ** -0.5)
        v16 = v_ref[...].astype(BF16)
        wf = jnp.exp(lg_ref[0, h] * (cx - 1.0 - pos))
        wb = jnp.exp(lg_ref[1, h] * pos)
        sf_ref[...] = _dot((k * wf).astype(BF16), v16, TN)
        sb_ref[...] = _dot((k * wb).astype(BF16), v16, TN)

    st = pl.BlockSpec((None, None, DK, DV), lambda b, h: (b, h, 0, 0))
    return pl.pallas_call(
        body, name="ctx_state", grid=(nb, RH),
        in_specs=[pl.BlockSpec(memory_space=pltpu.SMEM),
                  pl.BlockSpec((cx, DK), lambda b, h: (rb + b, RK // DK + h)),
                  pl.BlockSpec((cx, DV), lambda b, h: (rb + b, RV // DV + h))],
        out_specs=[st, st], out_shape=[_sds((nb, RH, DK, DV), F32)] * 2,
        compiler_params=_params(("parallel", "parallel")),
    )(lg, px, px)


def _ret_fwd(px, lg, s0f, s0b, nb, nc):
    t_rows = nb * nc * CH

    def body(lg_ref, *refs):
        ins = (refs[0:5], refs[5:10])
        s0f_ref, s0b_ref, of_ref, ob_ref, hf_ref, hb_ref, sf, sb = refs[10:]
        c = pl.program_id(1)

        @pl.when(c == 0)
        def _():
            sf[...] = s0f_ref[...]
            sb[...] = s0b_ref[...]

        for d, (o_ref, h_ref, s) in enumerate(((of_ref, hf_ref, sf), (ob_ref, hb_ref, sb))):
            for h in range(RH):
                lg_d = lg_ref[d, h]
                mask, _, qd, _, kd, _ = _decays(lg_d, d == 0)
                q, k, v16 = _head_qkv(ins[d], h)
                a = _dot(q.astype(BF16), k.astype(BF16), NT)
                st = s[h]
                st16 = st.astype(BF16)
                h_ref[h] = st16
                o = _dot((a * mask).astype(BF16), v16) + _dot((q * qd).astype(BF16), st16)
                o_ref[:, h * DV:(h + 1) * DV] = o.astype(BF16)
                s[h] = st * jnp.exp(lg_d * CH) + _dot((k * kd).astype(BF16), v16, TN)

    def fw(b, c):
        return b * nc + c

    def bw(b, c):
        return b * nc + nc - 1 - c

    st = pl.BlockSpec((None, RH, DK, DV), lambda b, c: (b, 0, 0, 0))
    in_specs = [pl.BlockSpec(memory_space=pltpu.SMEM)] + _wide_specs(fw) + _wide_specs(bw) + [st, st]
    out_specs = [pl.BlockSpec((CH, RH * DV), lambda b, c: (fw(b, c), 0)),
                 pl.BlockSpec((CH, RH * DV), lambda b, c: (bw(b, c), 0)),
                 pl.BlockSpec((None, None, RH, DK, DV), lambda b, c: (b, c, 0, 0, 0)),
                 pl.BlockSpec((None, None, RH, DK, DV), lambda b, c: (b, nc - 1 - c, 0, 0, 0))]
    return pl.pallas_call(
        body, name="ret_fwd", grid=(nb, nc), in_specs=in_specs, out_specs=out_specs,
        out_shape=[_sds((t_rows, RH * DV), BF16)] * 2 + [_sds((nb, nc, RH, DK, DV), BF16)] * 2,
        scratch_shapes=[pltpu.VMEM((RH, DK, DV), F32), pltpu.VMEM((RH, DK, DV), F32)],
        compiler_params=_params(("parallel", "arbitrary")),
    )(lg, *([px] * 10), s0f, s0b)


def _ret_post(o_f, o_b, px, tm):
    t_rows = o_f.shape[0]

    def body(of_ref, ob_ref, g0, g1, g2, g3, y_ref):
        for h, g_ref in enumerate((g0, g1, g2, g3)):
            sl = slice(h * DV, (h + 1) * DV)
            o = of_ref[:, sl].astype(F32) + ob_ref[:, sl].astype(F32)
            r = lax.rsqrt(jnp.mean(o * o, axis=-1, keepdims=True) + EPS)
            y_ref[:, sl] = ((o * r) * _silu(g_ref[...].astype(F32))).astype(BF16)

    def gate(h):
        return pl.BlockSpec((tm, DV), lambda i: (i, RG // DV + h))

    wide = pl.BlockSpec((tm, RH * DV), lambda i: (i, 0))
    return pl.pallas_call(
        body, name="ret_post", grid=(t_rows // tm,),
        in_specs=[wide, wide] + [gate(h) for h in range(RH)],
        out_specs=wide, out_shape=_sds((t_rows, RH * DV), BF16),
        compiler_params=_params(("parallel",)),
    )(o_f, o_b, *([px] * RH))


def _rope_tables(seq):
    rows = seq // GRID_W
    row = np.repeat(np.arange(rows, dtype=np.float32), GRID_W)
    col = np.tile(np.arange(GRID_W, dtype=np.float32), rows)
    half = HD // 2
    freqs = (ROPE_THETA ** (-np.arange(0, half, 2, dtype=np.float32) / half)).astype(np.float32)
    ang = np.concatenate([row[:, None] * freqs, col[:, None] * freqs], axis=-1).astype(np.float32)
    cos = np.repeat(np.cos(ang), 2, axis=-1).astype(np.float32)
    sin = np.repeat(np.sin(ang), 2, axis=-1).astype(np.float32)
    sign = np.tile(np.array([-1.0, 1.0], np.float32), HD // 2)
    return jnp.asarray(cos), jnp.asarray(sin * sign)


def _swap_pairs(v):
    lane = lax.broadcasted_iota(jnp.int32, v.shape, 1)
    return jnp.where((lane & 1) == 0, pltpu.roll(v, HD - 1, 1), pltpu.roll(v, 1, 1))


def _qk_prep(px, nw, cos, sin, rows, row_off, col_off, heads, hb, seq, tm, name):
    rope = cos is not None
    rb0 = row_off // tm
    pb = seq // tm if rope else 1
    bw = hb * HD

    def body(*refs):
        if rope:
            x_ref, w_ref, c_ref, s_ref, o_ref = refs
        else:
            x_ref, w_ref, o_ref = refs
        for h in range(hb):
            sl = slice(h * HD, (h + 1) * HD)
            xv = x_ref[:, sl].astype(F32)
            r = lax.rsqrt(jnp.mean(xv * xv, axis=-1, keepdims=True) + EPS)
            t = (xv * r) * w_ref[...]
            if rope:
                t = t * c_ref[...] + _swap_pairs(t) * s_ref[...]
            o_ref[:, sl] = t.astype(BF16)

    in_specs = [pl.BlockSpec((tm, bw), lambda i, j: (rb0 + i, col_off // bw + j)),
                pl.BlockSpec((1, HD), lambda i, j: (0, 0))]
    args = [px, nw]
    if rope:
        in_specs += [pl.BlockSpec((tm, HD), lambda i, j: (i % pb, 0))] * 2
        args += [cos, sin]
    return pl.pallas_call(
        body, name=name, grid=(rows // tm, heads // hb), in_specs=in_specs,
        out_specs=pl.BlockSpec((tm, bw), lambda i, j: (i, j)), out_shape=_sds((rows, heads * HD), BF16),
        compiler_params=_params(("parallel", "parallel")),
    )(*args)


def _att_fwd(q16, kx16, kc16, px, nb, seq, cx, tq):
    t_rows = nb * seq
    nq = seq // tq
    rep = HQ // HKV
    gw = rep * HD

    def body(q_ref, kx_ref, kc_ref, vx_ref, vc_ref, g_ref, o_ref, y_ref, l_ref):
        kx = kx_ref[...]
        kc = kc_ref[...]
        vx = vx_ref[...].astype(BF16)
        vc = vc_ref[...].astype(BF16)
        l_ref[...] = jnp.zeros_like(l_ref)
        for r in range(rep):
            sl = slice(r * HD, (r + 1) * HD)
            q = q_ref[:, sl]
            s1 = _dot(q, kx, NT)
            s2 = _dot(q, kc, NT)
            m = jnp.maximum(jnp.max(s1, axis=-1, keepdims=True), jnp.max(s2, axis=-1, keepdims=True))
            e1 = jnp.exp2((s1 - m) * SM_C)
            e2 = jnp.exp2((s2 - m) * SM_C)
            tot = jnp.sum(e1, axis=-1, keepdims=True) + jnp.sum(e2, axis=-1, keepdims=True)
            o = (_dot(e1.astype(BF16), vx) + _dot(e2.astype(BF16), vc)) * (1.0 / tot)
            o_ref[:, sl] = o
            y_ref[:, sl] = (o * _silu(g_ref[:, sl].astype(F32))).astype(BF16)
            l_ref[:, r:r + 1] = m * SM_C + jnp.log(tot) * float(np.log2(np.e))

    qblk = pl.BlockSpec((tq, gw), lambda b, g, i: (b * nq + i, g))
    return pl.pallas_call(
        body, name="att_fwd", grid=(nb, HKV, nq),
        in_specs=[qblk,
                  pl.BlockSpec((seq, HD), lambda b, g, i: (b, g)),
                  pl.BlockSpec((cx, HD), lambda b, g, i: (b, g)),
                  pl.BlockSpec((seq, HD), lambda b, g, i: (b, AV // HD + g)),
                  pl.BlockSpec((cx, HD), lambda b, g, i: (t_rows // cx + b, AV // HD + g)),
                  pl.BlockSpec((tq, gw), lambda b, g, i: (b * nq + i, AG // gw + g))],
        out_specs=[qblk, qblk, pl.BlockSpec((tq, 128), lambda b, g, i: (b * nq + i, g))],
        out_shape=[_sds((t_rows, D), F32), _sds((t_rows, D), BF16), _sds((t_rows, HKV * 128), F32)],
        compiler_params=_params(("parallel", "parallel", "parallel")),
    )(q16, kx16, kc16, px, px, px)


def _gate_specs(tm, col0):
    hw = D // 2
    return [pl.BlockSpec((tm, hw), lambda i: (i, col0 // hw)), pl.BlockSpec((tm, hw), lambda i: (i, col0 // hw + 1))]


def _merge(yret16, yatt16, px, w_o_ret16, w_o_att16, tm):
    t_rows = yret16.shape[0]
    hw = D // 2

    def body(yr_ref, wr_ref, ya_ref, wa_ref, mr0, mr1, ma0, ma1, ar_ref, aa_ref, y_ref):
        ar = _dot(yr_ref[...], wr_ref[...])
        aa = _dot(ya_ref[...], wa_ref[...])
        ar_ref[...] = ar.astype(BF16)
        aa_ref[...] = aa.astype(BF16)
        for j, (mr_ref, ma_ref) in enumerate(((mr0, ma0), (mr1, ma1))):
            sl = slice(j * hw, (j + 1) * hw)
            y_ref[:, sl] = (_sig(mr_ref[...].astype(F32)) * ar[:, sl]
                            + _sig(ma_ref[...].astype(F32)) * aa[:, sl]).astype(BF16)

    row = pl.BlockSpec((tm, D), lambda i: (i, 0))
    return pl.pallas_call(
        body, name="merge", grid=(t_rows // tm,),
        in_specs=[pl.BlockSpec((tm, RH * DV), lambda i: (i, 0)), pl.BlockSpec((RH * DV, D), lambda i: (0, 0)),
                  row, pl.BlockSpec((D, D), lambda i: (0, 0))] + _gate_specs(tm, MR) + _gate_specs(tm, MA),
        out_specs=[row, row, row], out_shape=[_sds((t_rows, D), BF16)] * 3,
        compiler_params=_params(("parallel",)),
    )(yret16, w_o_ret16, yatt16, w_o_att16, px, px, px, px)


def _outproj(y16, w_out16, x2, tgt, mod3, nb, seq, tm):
    t_rows = nb * seq
    bpb = seq // tm

    def body(y_ref, w_ref, x_ref, t_ref, g_ref, dxn_ref, dout_ref, dg_ref, loss_ref):
        i = pl.program_id(1)
        out = _dot(y_ref[...], w_ref[...])
        gate = g_ref[...]
        diff = x_ref[...] + gate * out - t_ref[...]
        dxn = diff * (1.0 / D)
        dxn_ref[...] = dxn
        dout_ref[...] = (gate * dxn).astype(BF16)
        dg = jnp.sum(dxn * out, axis=0, keepdims=True)
        ls = jnp.broadcast_to(jnp.sum(diff * diff) * (0.5 / D), (1, 128))

        @pl.when(i == 0)
        def _():
            dg_ref[...] = dg
            loss_ref[...] = ls

        @pl.when(i > 0)
        def _():
            dg_ref[...] += dg
            loss_ref[...] += ls

    row = pl.BlockSpec((tm, D), lambda b, i: (b * bpb + i, 0))
    return pl.pallas_call(
        body, name="outproj", grid=(nb, bpb),
        in_specs=[row, pl.BlockSpec((D, D), lambda b, i: (0, 0)), row, row,
                  pl.BlockSpec((None, 1, D), lambda b, i: (b, 0, 2))],
        out_specs=[row, row, pl.BlockSpec((None, 1, D), lambda b, i: (b, 0, 0)),
                   pl.BlockSpec((None, 1, 128), lambda b, i: (b, 0, 0))],
        out_shape=[_sds((t_rows, D), F32), _sds((t_rows, D), BF16), _sds((nb, 1, D), F32), _sds((nb, 1, 128), F32)],
        compiler_params=_params(("parallel", "arbitrary")),
    )(y16, w_out16, x2, tgt, mod3)


def _bwd_merge(dout16, w_out16, px, a_ret, a_att, tm):
    t_rows = dout16.shape[0]
    hw = D // 2

    def body(do_ref, w_ref, mr0, mr1, ma0, ma1, ar_ref, aa_ref, dar_ref, daa_ref, dmr_ref, dma_ref):
        dy_all = _dot(do_ref[...], w_ref[...], NT)
        for j, (mr_ref, ma_ref) in enumerate(((mr0, ma0), (mr1, ma1))):
            sl = slice(j * hw, (j + 1) * hw)
            dy = dy_all[:, sl]
            sr = _sig(mr_ref[...].astype(F32))
            sa = _sig(ma_ref[...].astype(F32))
            dar_ref[:, sl] = (dy * sr).astype(BF16)
            daa_ref[:, sl] = (dy * sa).astype(BF16)
            dmr_ref[:, sl] = (dy * ar_ref[:, sl].astype(F32) * sr * (1.0 - sr)).astype(BF16)
            dma_ref[:, sl] = (dy * aa_ref[:, sl].astype(F32) * sa * (1.0 - sa)).astype(BF16)

    row = pl.BlockSpec((tm, D), lambda i: (i, 0))
    return pl.pallas_call(
        body, name="bwd_merge", grid=(t_rows // tm,),
        in_specs=[row, pl.BlockSpec((D, D), lambda i: (0, 0))] + _gate_specs(tm, MR) + _gate_specs(tm, MA) + [row, row],
        out_specs=[row] * 4, out_shape=[_sds((t_rows, D), BF16)] * 4,
        compiler_params=_params(("parallel",)),
    )(dout16, w_out16, px, px, px, px, a_ret, a_att)


def _bwd_branch_ret(da_ret16, w_o_ret16, px, o_f, o_b, tm, after=()):
    t_rows = da_ret16.shape[0]

    def body(da_ref, w_ref, g0, g1, g2, g3, of_ref, ob_ref, *rest):
        do_ref, dg_ref = rest[-2:]
        da = da_ref[...]
        for h, g_ref in enumerate((g0, g1, g2, g3)):
            sl = slice(h * DV, (h + 1) * DV)
            dy = _dot(da, w_ref[sl, :], NT)
            g = g_ref[...].astype(F32)
            o = of_ref[:, sl].astype(F32) + ob_ref[:, sl].astype(F32)
            r = lax.rsqrt(jnp.mean(o * o, axis=-1, keepdims=True) + EPS)
            on = o * r
            don = dy * _silu(g)
            dg_ref[:, sl] = (dy * on * _dsilu(g)).astype(BF16)
            do_ref[:, sl] = (r * (don - on * jnp.mean(on * don, axis=-1, keepdims=True))).astype(BF16)

    def gate(h):
        return pl.BlockSpec((tm, DV), lambda i: (i, RG // DV + h))

    wide = pl.BlockSpec((tm, RH * DV), lambda i: (i, 0))
    return pl.pallas_call(
        body, name="bwd_branch_ret", grid=(t_rows // tm,),
        in_specs=[pl.BlockSpec((tm, D), lambda i: (i, 0)), pl.BlockSpec((RH * DV, D), lambda i: (0, 0))]
        + [gate(h) for h in range(RH)] + [wide, wide] + [pl.BlockSpec(memory_space=pl.ANY)] * len(after),
        out_specs=[wide, wide], out_shape=[_sds((t_rows, RH * DV), BF16)] * 2,
        compiler_params=_params(("parallel",)),
    )(da_ret16, w_o_ret16, *([px] * RH), o_f, o_b, *after)


def _bwd_branch_att(da_att16, w_o_att16, px, o_att, tm):
    t_rows = da_att16.shape[0]
    hw = D // 2

    def body(da_ref, w_ref, g0, g1, o_ref, dao_ref, dg_ref):
        dy_all = _dot(da_ref[...], w_ref[...], NT)
        for j, g_ref in enumerate((g0, g1)):
            sl = slice(j * hw, (j + 1) * hw)
            dy = dy_all[:, sl]
            g = g_ref[...].astype(F32)
            dao_ref[:, sl] = dy * _silu(g)
            dg_ref[:, sl] = (dy * o_ref[:, sl] * _dsilu(g)).astype(BF16)

    row = pl.BlockSpec((tm, D), lambda i: (i, 0))
    return pl.pallas_call(
        body, name="bwd_branch_att", grid=(t_rows // tm,),
        in_specs=[row, pl.BlockSpec((D, D), lambda i: (0, 0))] + _gate_specs(tm, AG) + [row],
        out_specs=[row, row], out_shape=[_sds((t_rows, D), F32), _sds((t_rows, D), BF16)],
        compiler_params=_params(("parallel",)),
    )(da_att16, w_o_att16, px, px, o_att)


def _att_bwd(q16, kx16, kc16, px, dao, o_att, lse, nb, seq, cx, tq):
    t_rows = nb * seq
    nq = seq // tq
    rep = HQ // HKV
    gw = rep * HD
    scale = HD ** -0.5

    def body(q_ref, kx_ref, kc_ref, vx_ref, vc_ref, dao_ref, o_ref, l_ref, dq_ref, dkx_ref, dvx_ref, dkc_ref, dvc_ref):
        i = pl.program_id(2)
        kx = kx_ref[...]
        kc = kc_ref[...]
        vx = vx_ref[...].astype(BF16)
        vc = vc_ref[...].astype(BF16)
        dkx = jnp.zeros((seq, HD), F32)
        dvx = jnp.zeros((seq, HD), F32)
        dkc = jnp.zeros((cx, HD), F32)
        dvc = jnp.zeros((cx, HD), F32)
        for r in range(rep):
            sl = slice(r * HD, (r + 1) * HD)
            q = q_ref[:, sl]
            lr = l_ref[:, r:r + 1]
            p1 = jnp.exp2(_dot(q, kx, NT) * SM_C - lr)
            p2 = jnp.exp2(_dot(q, kc, NT) * SM_C - lr)
            da = dao_ref[:, sl]
            da16 = da.astype(BF16)
            delta = jnp.sum(da * o_ref[:, sl], axis=-1, keepdims=True)
            ds1 = (p1 * (_dot(da16, vx, NT) - delta)).astype(BF16)
            ds2 = (p2 * (_dot(da16, vc, NT) - delta)).astype(BF16)
            dq_ref[:, sl] = (_dot(ds1, kx) + _dot(ds2, kc)) * scale
            dkx += _dot(ds1, q, TN)
            dkc += _dot(ds2, q, TN)
            dvx += _dot(p1.astype(BF16), da16, TN)
            dvc += _dot(p2.astype(BF16), da16, TN)
        dkx = dkx * scale
        dkc = dkc * scale

        @pl.when(i == 0)
        def _():
            dkx_ref[...] = dkx
            dvx_ref[...] = dvx
            dkc_ref[...] = dkc
            dvc_ref[...] = dvc

        @pl.when(i > 0)
        def _():
            dkx_ref[...] += dkx
            dvx_ref[...] += dvx
            dkc_ref[...] += dkc
            dvc_ref[...] += dvc

    qblk = pl.BlockSpec((tq, gw), lambda b, g, i: (b * nq + i, g))
    kxb = pl.BlockSpec((None, seq, HD), lambda b, g, i: (b, 0, g))
    kcb = pl.BlockSpec((None, cx, HD), lambda b, g, i: (b, 0, g))
    return pl.pallas_call(
        body, name="att_bwd", grid=(nb, HKV, nq),
        in_specs=[qblk,
                  pl.BlockSpec((seq, HD), lambda b, g, i: (b, g)),
                  pl.BlockSpec((cx, HD), lambda b, g, i: (b, g)),
                  pl.BlockSpec((seq, HD), lambda b, g, i: (b, AV // HD + g)),
                  pl.BlockSpec((cx, HD), lambda b, g, i: (t_rows // cx + b, AV // HD + g)),
                  qblk, qblk, pl.BlockSpec((tq, 128), lambda b, g, i: (b * nq + i, g))],
        out_specs=[qblk, kxb, kxb, kcb, kcb],
        out_shape=[_sds((t_rows, D), F32), _sds((nb, seq, HKV * HD), F32), _sds((nb, seq, HKV * HD), F32),
                   _sds((nb, cx, HKV * HD), F32), _sds((nb, cx, HKV * HD), F32)],
        compiler_params=_params(("parallel", "parallel", "arbitrary")),
    )(q16, kx16, kc16, px, px, dao, o_att, lse)


def _qk_prep_bwd(dt, px, nw, cos, sin, rows, row_off, col_off, heads, hb, seq, tm, name):
    rope = cos is not None
    rb0 = row_off // tm
    pb = seq // tm if rope else 1
    bw = hb * HD

    def body(*refs):
        if rope:
            d_ref, x_ref, w_ref, c_ref, s_ref, dx_ref, dw_ref = refs
        else:
            d_ref, x_ref, w_ref, dx_ref, dw_ref = refs
        first = jnp.logical_and(pl.program_id(0) == 0, pl.program_id(1) == 0)
        dw = jnp.zeros((1, HD), F32)
        for h in range(hb):
            sl = slice(h * HD, (h + 1) * HD)
            dtv = d_ref[:, sl]
            if rope:
                dtv = dtv * c_ref[...] + _swap_pairs(dtv * s_ref[...])
            xv = x_ref[:, sl].astype(F32)
            r = lax.rsqrt(jnp.mean(xv * xv, axis=-1, keepdims=True) + EPS)
            xh = xv * r
            dxh = dtv * w_ref[...]
            dx_ref[:, sl] = (r * (dxh - xh * jnp.mean(dxh * xh, axis=-1, keepdims=True))).astype(BF16)
            dw += jnp.sum(dtv * xh, axis=0, keepdims=True)

        @pl.when(first)
        def _():
            dw_ref[...] = dw

        @pl.when(jnp.logical_not(first))
        def _():
            dw_ref[...] += dw

    blk = pl.BlockSpec((tm, bw), lambda i, j: (i, j))
    in_specs = [blk, pl.BlockSpec((tm, bw), lambda i, j: (rb0 + i, col_off // bw + j)),
                pl.BlockSpec((1, HD), lambda i, j: (0, 0))]
    args = [dt, px, nw]
    if rope:
        in_specs += [pl.BlockSpec((tm, HD), lambda i, j: (i % pb, 0))] * 2
        args += [cos, sin]
    return pl.pallas_call(
        body, name=name, grid=(rows // tm, heads // hb), in_specs=in_specs,
        out_specs=[blk, pl.BlockSpec((1, HD), lambda i, j: (0, 0))],
        out_shape=[_sds((rows, heads * HD), BF16), _sds((1, HD), F32)],
        compiler_params=_params(("arbitrary", "arbitrary")),
    )(*args)


def _ret_bwd(px, lg, do16, hist_f, hist_b, nb, nc):
    t_rows = nb * nc * CH

    def body(lg_ref, *refs):
        ins = (refs[0:5], refs[7:12])
        do_refs = (refs[5], refs[12])
        h_refs = (refs[6], refs[13])
        outs = (refs[14:17], refs[17:20])
        ds_outs = (refs[20], refs[21])
        dlg_ref = refs[22]
        dss = (refs[23], refs[24])
        c = pl.program_id(1)

        @pl.when(c == 0)
        def _():
            dss[0][...] = jnp.zeros_like(dss[0])
            dss[1][...] = jnp.zeros_like(dss[1])
            dlg_ref[...] = jnp.zeros_like(dlg_ref)

        for d in range(2):
            dq_ref, dk_ref, dv_ref = outs[d]
            for h in range(RH):
                lg_d = lg_ref[d, h]
                mask, relf, qd, qe, kd, ke = _decays(lg_d, d == 0)
                g_ch = jnp.exp(lg_d * CH)
                q, k, v16 = _head_qkv(ins[d], h)
                q16 = q.astype(BF16)
                k16 = k.astype(BF16)
                do16v = do_refs[d][:, h * DV:(h + 1) * DV]
                st16 = h_refs[d][h]
                dst = dss[d][h]
                dst16 = dst.astype(BF16)
                a = _dot(q16, k16, NT) * mask
                dp = _dot(do16v, v16, NT)
                da16 = (dp * mask).astype(BF16)
                dq_cross = _dot(do16v, st16, NT) * qd
                dq_ref[:, h * DK:(h + 1) * DK] = (_dot(da16, k16) + dq_cross).astype(BF16)
                dk_state = _dot(v16, dst16, NT) * kd
                dk_ref[:, h * DK:(h + 1) * DK] = ((_dot(da16, q16, TN) + dk_state) * (DK ** -0.5)).astype(BF16)
                dv = _dot(a.astype(BF16), do16v, TN) + _dot((k * kd).astype(BF16), dst16)
                dv_ref[:, h * DV:(h + 1) * DV] = dv.astype(BF16)
                dlg = (jnp.sum(relf * a * dp)
                       + jnp.sum(qe * jnp.sum(q * dq_cross, axis=-1, keepdims=True))
                       + jnp.sum(ke * jnp.sum(k * dk_state, axis=-1, keepdims=True))
                       + CH * g_ch * jnp.sum(dst * st16.astype(F32)))
                row = d * RH + h
                dlg_ref[row:row + 1, :] += jnp.broadcast_to(dlg, (1, 128))
                ds_new = g_ch * dst + _dot((q * qd).astype(BF16), do16v, TN)
                dss[d][h] = ds_new

                @pl.when(c == nc - 1)
                def _():
                    ds_outs[d][h] = ds_new

    def fw(b, c):
        return b * nc + nc - 1 - c

    def bw(b, c):
        return b * nc + c

    def rows(rowf, width):
        return pl.BlockSpec((CH, width), lambda b, c: (rowf(b, c), 0))

    def hist(rowf):
        return pl.BlockSpec((None, None, RH, DK, DV), lambda b, c: (b, rowf(0, c), 0, 0, 0))

    st = pl.BlockSpec((None, RH, DK, DV), lambda b, c: (b, 0, 0, 0))
    in_specs = [pl.BlockSpec(memory_space=pltpu.SMEM)]
    out_specs = []
    for rowf in (fw, bw):
        in_specs += _wide_specs(rowf) + [rows(rowf, RH * DV), hist(rowf)]
        out_specs += [rows(rowf, RH * DK), rows(rowf, RH * DK), rows(rowf, RH * DV)]
    out_specs += [st, st, pl.BlockSpec((None, 8, 128), lambda b, c: (b, 0, 0))]
    qk = _sds((t_rows, RH * DK), BF16)
    vv = _sds((t_rows, RH * DV), BF16)
    return pl.pallas_call(
        body, name="ret_bwd", grid=(nb, nc), in_specs=in_specs, out_specs=out_specs,
        out_shape=[qk, qk, vv, qk, qk, vv, _sds((nb, RH, DK, DV), F32), _sds((nb, RH, DK, DV), F32),
                   _sds((nb, 8, 128), F32)],
        scratch_shapes=[pltpu.VMEM((RH, DK, DV), F32), pltpu.VMEM((RH, DK, DV), F32)],
        compiler_params=_params(("parallel", "arbitrary")),
    )(lg, *([px] * 5), do16, hist_f, *([px] * 5), do16, hist_b)


def _ctx_state_bwd(px, lg, ds_f, ds_b, nb, t_rows, cx):
    rb = t_rows // cx

    def body(lg_ref, k_ref, v_ref, dsf_ref, dsb_ref, dk_ref, dv_ref, dlg_ref):
        h = pl.program_id(1)
        pos = lax.broadcasted_iota(jnp.int32, (cx, 1), 0).astype(F32)
        k = k_ref[...].astype(F32) * (DK ** -0.5)
        v16 = v_ref[...].astype(BF16)
        dk = jnp.zeros((cx, DK), F32)
        dv = jnp.zeros((cx, DV), F32)
        dlg_ref[...] = jnp.zeros_like(dlg_ref)
        for d, (ds_ref, e) in enumerate(((dsf_ref, cx - 1.0 - pos), (dsb_ref, pos))):
            w = jnp.exp(lg_ref[d, h] * e)
            ds16 = ds_ref[...].astype(BF16)
            t = _dot(v16, ds16, NT)
            dk += t * w
            dv += _dot((k * w).astype(BF16), ds16)
            dlg = jnp.sum(e * w * jnp.sum(k * t, axis=-1, keepdims=True))
            dlg_ref[d:d + 1, :] = jnp.broadcast_to(dlg, (1, 128))
        dk_ref[...] = (dk * (DK ** -0.5)).astype(BF16)
        dv_ref[...] = dv.astype(BF16)

    st = pl.BlockSpec((None, None, DK, DV), lambda b, h: (b, h, 0, 0))
    return pl.pallas_call(
        body, name="ctx_state_bwd", grid=(nb, RH),
        in_specs=[pl.BlockSpec(memory_space=pltpu.SMEM),
                  pl.BlockSpec((cx, DK), lambda b, h: (rb + b, RK // DK + h)),
                  pl.BlockSpec((cx, DV), lambda b, h: (rb + b, RV // DV + h)), st, st],
        out_specs=[pl.BlockSpec((cx, DK), lambda b, h: (b, h)), pl.BlockSpec((cx, DV), lambda b, h: (b, h)),
                   pl.BlockSpec((None, None, 8, 128), lambda b, h: (b, h, 0, 0))],
        out_shape=[_sds((nb * cx, RH * DK), BF16), _sds((nb * cx, RH * DV), BF16), _sds((nb, RH, 8, 128), F32)],
        compiler_params=_params(("parallel", "parallel")),
    )(lg, px, px, ds_f, ds_b)


def _assemble_lat(rows_all, dk_f, dk_b, dv_f, dv_b, dak16, dvx, dq_f, dq_b, drg16, daq16, dag16, dmr16, dma16, tm):
    t_rows = dk_f.shape[0]

    def body(dkf, dkb, dvf, dvb, dak, dav, dqf, dqb, drg, daq, dag, dmr, dma, o_ref):
        o_ref[:, RK:RK + RH * DK] = (dkf[...].astype(F32) + dkb[...].astype(F32)).astype(BF16)
        o_ref[:, RV:RV + RH * DV] = (dvf[...].astype(F32) + dvb[...].astype(F32)).astype(BF16)
        o_ref[:, AK:AK + HKV * HD] = dak[...]
        o_ref[:, AV:AV + HKV * HD] = dav[...].astype(BF16)
        o_ref[:, RQ:RQ + RH * DK] = (dqf[...].astype(F32) + dqb[...].astype(F32)).astype(BF16)
        o_ref[:, RG:RG + RH * DV] = drg[...]
        o_ref[:, AQ:AQ + D] = daq[...]
        o_ref[:, AG:AG + D] = dag[...]
        o_ref[:, MR:MR + D] = dmr[...]
        o_ref[:, MA:MA + D] = dma[...]

    args = (dk_f, dk_b, dv_f, dv_b, dak16, dvx, dq_f, dq_b, drg16, daq16, dag16, dmr16, dma16)
    return pl.pallas_call(
        body, name="assemble_lat", grid=(t_rows // tm,),
        in_specs=[pl.BlockSpec((tm, a.shape[1]), lambda i: (i, 0)) for a in args],
        out_specs=pl.BlockSpec((tm, IN_COLS), lambda i: (i, 0)), out_shape=_sds((rows_all, IN_COLS), BF16),
        compiler_params=_params(("parallel",)),
    )(*args)


def _assemble_ctx(dp_all, dck16, dcv16, dcak16, dvc, t_rows, tm):
    c_rows = dck16.shape[0]
    rb = t_rows // tm

    def body(_, dck, dcv, dcak, dcav, o_ref):
        o_ref[:, RK:RK + RH * DK] = dck[...]
        o_ref[:, RV:RV + RH * DV] = dcv[...]
        o_ref[:, AK:AK + HKV * HD] = dcak[...]
        o_ref[:, AV:AV + HKV * HD] = dcav[...].astype(BF16)
        o_ref[:, KV_COLS:] = jnp.zeros((tm, IN_COLS - KV_COLS), BF16)

    args = (dck16, dcv16, dcak16, dvc)
    return pl.pallas_call(
        body, name="assemble_ctx", grid=(c_rows // tm,),
        in_specs=[pl.BlockSpec(memory_space=pl.ANY)]
        + [pl.BlockSpec((tm, a.shape[1]), lambda i: (i, 0)) for a in args],
        out_specs=pl.BlockSpec((tm, IN_COLS), lambda i: (rb + i, 0)), out_shape=_sds(dp_all.shape, BF16),
        input_output_aliases={0: 0},
        compiler_params=_params(("parallel",)),
    )(dp_all, *args)


def _norm_bwd(dh, x2, mod3, norm_w, dxn, row_off, rows_per_group, group0, tm, name):
    with_dx = dxn is not None
    rows = x2.shape[0]
    rb0 = row_off // tm
    bpg = rows_per_group // tm
    ngroups = rows // rows_per_group

    def body(*refs):
        if with_dx:
            dh_ref, x_ref, sc_ref, nw_ref, dxn_ref, dx_ref, dsh_ref, dsc_ref, dnw_ref = refs
        else:
            dh_ref, x_ref, sc_ref, nw_ref, dsh_ref, dsc_ref, dnw_ref = refs
        i = pl.program_id(0)
        dhv = dh_ref[...]
        xv = x_ref[...]
        nw = nw_ref[...]
        r = lax.rsqrt(jnp.mean(xv * xv, axis=-1, keepdims=True) + EPS)
        xh = xv * r
        dm = dhv * (1.0 + sc_ref[...])
        dsh = jnp.sum(dhv, axis=0, keepdims=True)
        dsc = jnp.sum(dhv * (xh * nw), axis=0, keepdims=True)
        dnw = jnp.sum(dm * xh, axis=0, keepdims=True)
        if with_dx:
            dxh = dm * nw
            dx_ref[...] = dxn_ref[...] + r * (dxh - xh * jnp.mean(dxh * xh, axis=-1, keepdims=True))

        @pl.when(i % bpg == 0)
        def _():
            dsh_ref[...] = dsh
            dsc_ref[...] = dsc

        @pl.when(i % bpg != 0)
        def _():
            dsh_ref[...] += dsh
            dsc_ref[...] += dsc

        @pl.when(i == 0)
        def _():
            dnw_ref[...] = dnw

        @pl.when(i > 0)
        def _():
            dnw_ref[...] += dnw

    grp = pl.BlockSpec((None, 1, D), lambda i: (i // bpg, 0, 0))
    in_specs = [pl.BlockSpec((tm, D), lambda i: (rb0 + i, 0)), pl.BlockSpec((tm, D), lambda i: (i, 0)),
                pl.BlockSpec((None, 1, D), lambda i: (group0 + i // bpg, 0, 1)),
                pl.BlockSpec((1, D), lambda i: (0, 0))]
    args = [dh, x2, mod3, norm_w]
    out_specs = [grp, grp, pl.BlockSpec((1, D), lambda i: (0, 0))]
    out_shape = [_sds((ngroups, 1, D), F32), _sds((ngroups, 1, D), F32), _sds((1, D), F32)]
    if with_dx:
        in_specs.append(pl.BlockSpec((tm, D), lambda i: (i, 0)))
        args.append(dxn)
        out_specs.insert(0, pl.BlockSpec((tm, D), lambda i: (i, 0)))
        out_shape.insert(0, _sds((rows, D), F32))
    return pl.pallas_call(
        body, name=name, grid=(rows // tm,), in_specs=in_specs, out_specs=out_specs, out_shape=out_shape,
        compiler_params=_params(("arbitrary",)),
    )(*args)


def _small_final(dmod_all, dmodc_parts, c_rows, dm_loc_rows, nw_parts, misc_parts, c_ctx, r_pad, w_ada16):
    loc = dm_loc_rows.shape[1]

    def body(dm_ref, dmc_ref, c_ref, dml_ref, nwp_ref, mp_ref, cc_ref, r_ref, w_ref,
             gb_ref, gc_ref, gnw_ref, misc_ref, gwa_ref):
        dmc = jnp.sum(dmc_ref[...], axis=0, keepdims=True)
        gb_ref[...] = jnp.sum(dm_ref[...], axis=0, keepdims=True) + dmc
        dsc = _dot(jnp.broadcast_to(dmc, (8, 3 * D)).astype(BF16), w_ref[...], NT)[0:1, :]
        gc_ref[...] = dsc * _dsilu(cc_ref[...])
        gnw_ref[...] = jnp.sum(nwp_ref[...], axis=0, keepdims=True)
        misc = jnp.sum(mp_ref[...], axis=0, keepdims=True)
        y = jnp.exp2(r_ref[...])
        lane = lax.broadcasted_iota(jnp.int32, (1, D), 1)
        is_decay = jnp.logical_and(lane >= 2 * HD, lane < 2 * HD + 2 * RH)
        misc_ref[...] = misc * jnp.where(is_decay, -(y * np.float32(np.log(2.0))) / (1.0 - y), 1.0)
        gwa_ref[...] = _dot(_silu(c_ref[...]).astype(BF16), dml_ref[...].astype(BF16), TN)

    return pl.pallas_call(
        body, name="small_final",
        out_shape=[_sds((1, 3 * D), F32), _sds((1, D), F32), _sds((1, D), F32), _sds((1, D), F32), _sds((D, loc), F32)],
        compiler_params=pltpu.CompilerParams(vmem_limit_bytes=VMEM_LIMIT),
    )(dmod_all, dmodc_parts, c_rows, dm_loc_rows, nw_parts, misc_parts, c_ctx, r_pad, w_ada16)


def _adamw(w, g, m, v, name):
    rows, cols = w.shape
    tm = _pick(rows, 448, 8)
    bc1 = 1.0 - B1 ** STEP
    bc2 = 1.0 - B2 ** STEP

    def body(w_ref, g_ref, m_ref, v_ref, d_ref, nm_ref, nv_ref):
        g_ = g_ref[...]
        nm = B1 * m_ref[...] + (1.0 - B1) * g_
        nv = B2 * v_ref[...] + (1.0 - B2) * (g_ * g_)
        nm_ref[...] = nm
        nv_ref[...] = nv
        d_ref[...] = -LR * ((nm / bc1) / (jnp.sqrt(nv / bc2) + ADAM_EPS) + WD * w_ref[...])

    blk = pl.BlockSpec((tm, cols), lambda i: (i, 0))
    return pl.pallas_call(
        body, name=name, grid=(rows // tm,), in_specs=[blk] * 4, out_specs=[blk] * 3,
        out_shape=[_sds((rows, cols), F32)] * 3, compiler_params=_params(("parallel",)),
    )(w, g, m, v)


def _mesh_pos():
    return lax.axis_index("x"), lax.axis_index("y"), lax.axis_index("c")


def _all_gather(arrs, name):
    n = len(arrs)

    def body(*refs):
        ins, outs = refs[:n], refs[n:2 * n]
        send_sems, recv_sems, local_sems = refs[2 * n:]
        x, y, c = _mesh_pos()
        me, sib = (x, y, c), (x, y, 1 - c)
        chips = [(1 - x, y), (x, 1 - y), (1 - x, 1 - y)]

        def slot(p):
            return 4 * p[0] + 2 * p[1] + p[2]

        def copy(a, k, block, to, own):
            dst = outs[a].at[slot(block)]
            return pltpu.make_async_remote_copy(
                src_ref=ins[a] if own else dst, dst_ref=dst, send_sem=send_sems.at[a, k], recv_sem=recv_sems.at[a, k],
                device_id=to, device_id_type=MESH_T)

        mine = [pltpu.make_async_copy(ins[a], outs[a].at[slot(me)], local_sems.at[a]) for a in range(n)]
        for cp in mine:
            cp.start()
        first = []
        for a in range(n):
            first.append(copy(a, 0, me, sib, True))
            first += [copy(a, 1 + j, me, (*chip, c), True) for j, chip in enumerate(chips)]
        for cp in first:
            cp.start()
        passed = []
        for j, chip in enumerate(chips):
            for a in range(n):
                copy(a, 1 + j, (*chip, c), me, False).wait_recv()
                fwd = copy(a, 4 + j, (*chip, c), sib, False)
                fwd.start()
                passed.append(fwd)
        for a in range(n):
            copy(a, 0, sib, me, False).wait_recv()
            for j, chip in enumerate(chips):
                copy(a, 4 + j, (*chip, 1 - c), me, False).wait_recv()
        for cp in first + passed:
            cp.wait_send()
        for cp in mine:
            cp.wait()

    hbm = pl.BlockSpec(memory_space=pl.ANY)
    return pl.pallas_call(
        body, name=name, in_specs=[hbm] * n, out_specs=[hbm] * n,
        out_shape=[_sds((N_DEV,) + a.shape, a.dtype) for a in arrs],
        scratch_shapes=[pltpu.SemaphoreType.DMA((n, 7)), pltpu.SemaphoreType.DMA((n, 7)), pltpu.SemaphoreType.DMA((n,))],
    )(*arrs)


def _pair_exchange(arrs, name):
    n = len(arrs)

    def body(*refs):
        ins, outs = refs[:n], refs[n:2 * n]
        send_sems, recv_sems = refs[2 * n:]
        x, y, c = _mesh_pos()
        sib = (x, y, 1 - c)
        sends = []
        for a in range(n):
            for k in range(4):
                sends.append(pltpu.make_async_remote_copy(
                    src_ref=ins[a].at[2 * k + 1 - c], dst_ref=outs[a].at[k], send_sem=send_sems.at[a, k],
                    recv_sem=recv_sems.at[a, k], device_id=sib, device_id_type=MESH_T))
        for cp in sends:
            cp.start()
        for cp in sends:
            cp.wait_recv()
        for cp in sends:
            cp.wait_send()

    hbm = pl.BlockSpec(memory_space=pl.ANY)
    return pl.pallas_call(
        body, name=name, in_specs=[hbm] * n, out_specs=[hbm] * n,
        out_shape=[_sds((4,) + a.shape[1:], a.dtype) for a in arrs],
        scratch_shapes=[pltpu.SemaphoreType.DMA((n, 4)), pltpu.SemaphoreType.DMA((n, 4))],
    )(*arrs)


def _pair_add(part, got, core, name):
    _, rows, cols = part.shape
    tm = _pick(rows, 672, 16)
    p4 = part.reshape(4, 2, rows, cols)

    def body(core_ref, p_ref, g_ref, o_ref):
        o_ref[...] = (p_ref[...].astype(F32) + g_ref[...].astype(F32)).astype(BF16)

    blk = pl.BlockSpec((None, tm, cols), lambda k, i, cr: (k, i, 0))
    return pl.pallas_call(
        body, name=name,
        grid_spec=pltpu.PrefetchScalarGridSpec(
            num_scalar_prefetch=1, grid=(4, rows // tm),
            in_specs=[pl.BlockSpec((None, None, tm, cols), lambda k, i, cr: (k, cr[0], i, 0)), blk], out_specs=blk),
        out_shape=_sds((4, rows, cols), BF16), compiler_params=_params(("parallel", "parallel")),
    )(core, p4, got)


def _chip_sum(pair_sums, landed, chip, name):
    _, rows, cols = pair_sums.shape
    tm = _pick(rows, 672, 16)

    def body(chip_ref, s_ref, l_ref, o_ref):
        acc = s_ref[...].astype(F32)
        for j in range(3):
            acc = acc + l_ref[j].astype(F32)
        o_ref[...] = acc

    return pl.pallas_call(
        body, name=name,
        grid_spec=pltpu.PrefetchScalarGridSpec(
            num_scalar_prefetch=1, grid=(rows // tm,),
            in_specs=[pl.BlockSpec((None, tm, cols), lambda i, ch: (ch[0], i, 0)),
                      pl.BlockSpec((3, tm, cols), lambda i, ch: (0, i, 0))],
            out_specs=pl.BlockSpec((tm, cols), lambda i, ch: (i, 0))),
        out_shape=_sds((rows, cols), F32), compiler_params=_params(("parallel",)),
    )(chip, pair_sums, landed)


_HBM = pl.BlockSpec(memory_space=pltpu.HBM)
_SEM = pl.BlockSpec(memory_space=pltpu.SEMAPHORE)
_EFFECT = pltpu.SideEffectType.DATAFLOW_SIDE_EFFECTING


def _chip_routes(n):
    def plan(x, y, c):
        routes = []
        for a in range(n):
            for j in range(1, 4):
                px, py = x ^ (j >> 1), y ^ (j & 1)
                routes.append((a, 2 * px + py, (px, py, c), j - 1))
        return routes
    return plan, 3 * n


def _bcast_routes(n):
    def plan(x, y, c):
        routes = []
        for a in range(n):
            for k in range(1, N_DEV):
                peer = (x ^ ((k >> 2) & 1), y ^ ((k >> 1) & 1), c ^ (k & 1))
                routes.append((a, 0, peer, 4 * x + 2 * y + c))
        return routes
    return plan, 7 * n


def _route_copies(srcs, lands, send_sems, recv_sems, routes):
    return [pltpu.make_async_remote_copy(
        src_ref=srcs[a].at[sb], dst_ref=lands[a].at[lb], send_sem=send_sems.at[r], recv_sem=recv_sems.at[r],
        device_id=peer, device_id_type=MESH_T) for r, (a, sb, peer, lb) in enumerate(routes)]


def _exchange_start(srcs, lands, routes, name, after=()):
    plan, count = routes
    n = len(srcs)
    n_in = 2 * n + len(after)

    def body(*refs):
        send_sems, recv_sems = refs[n_in], refs[n_in + 1]
        token = refs[-1]
        for cp in _route_copies(refs[:n], refs[n:2 * n], send_sems, recv_sems, plan(*_mesh_pos())):
            cp.start()
        token[...] = jnp.zeros_like(token)

    args = [pltpu.with_memory_space_constraint(a, pltpu.HBM) for a in list(srcs) + list(lands)]
    out = pl.pallas_call(
        body, name=name,
        out_shape=(pltpu.SemaphoreType.DMA((count,)), pltpu.SemaphoreType.DMA((count,)),
                   *[pltpu.HBM(a.shape, a.dtype) for a in args], _sds((8, 128), F32)),
        in_specs=[_HBM] * (2 * n) + [pl.BlockSpec(memory_space=pl.ANY)] * len(after),
        out_specs=(_SEM, _SEM, *([_HBM] * (2 * n)), pl.BlockSpec(memory_space=pltpu.VMEM)),
        input_output_aliases={i: 2 + i for i in range(2 * n)},
        compiler_params=pltpu.CompilerParams(has_side_effects=_EFFECT),
    )(*args, *after)
    return (out[0], out[1], list(out[2:2 + 2 * n]), routes), out[-1]


def _exchange_wait_some(state, after, only, name):
    send_sems, recv_sems, bufs, (plan, count) = state
    n = len(bufs) // 2

    def body(*refs):
        send_s, recv_s = refs[2 * n], refs[2 * n + 1]
        for r, cp in enumerate(_route_copies(refs[:n], refs[n:2 * n], send_s, recv_s, plan(*_mesh_pos()))):
            if only is None or r in only:
                cp.wait_send()
                cp.wait_recv()

    out = pl.pallas_call(
        body, name=name, out_shape=tuple(pltpu.HBM(a.shape, a.dtype) for a in bufs),
        in_specs=[_HBM] * (2 * n) + [_SEM, _SEM, pl.BlockSpec(memory_space=pl.ANY)], out_specs=tuple([_HBM] * (2 * n)),
        input_output_aliases={i: i for i in range(2 * n)},
        compiler_params=pltpu.CompilerParams(has_side_effects=_EFFECT),
    )(*bufs, send_sems, recv_sems, after)
    return (send_sems, recv_sems, list(out), (plan, count)), list(out[:n]), list(out[n:])


def _exchange_wait(state, after, name):
    _, srcs, lands = _exchange_wait_some(state, after, None, name)
    return srcs, lands


def _group_routes(js):
    def plan(x, y, c):
        return [(0, 0, (x ^ (j >> 1), y ^ (j & 1), c), 2 * j + c) for j in js]
    return plan, len(js)


def _pair_fill(groups, j, name, after=()):
    def body(*refs):
        g_ref, send_sem, recv_sem = refs[-3:]
        x, y, c = _mesh_pos()
        mine = g_ref.at[2 * j + c]
        to_sib = pltpu.make_async_remote_copy(src_ref=mine, dst_ref=mine, send_sem=send_sem, recv_sem=recv_sem,
                                              device_id=(x, y, 1 - c), device_id_type=MESH_T)
        to_sib.start()
        pltpu.make_async_remote_copy(src_ref=mine, dst_ref=g_ref.at[2 * j + 1 - c], send_sem=send_sem, recv_sem=recv_sem,
                                     device_id=(x, y, 1 - c), device_id_type=MESH_T).wait_recv()
        to_sib.wait_send()

    hbm = pl.BlockSpec(memory_space=pl.ANY)
    return pl.pallas_call(
        body, name=name, in_specs=[hbm] * (1 + len(after)), out_specs=hbm, out_shape=_sds(groups.shape, groups.dtype),
        input_output_aliases={0: 0},
        scratch_shapes=[pltpu.SemaphoreType.DMA, pltpu.SemaphoreType.DMA],
    )(groups, *after)


def _in_proj_group(h_all, groups, j, chip, px_prev, after, name):
    rows_all = h_all.shape[0]
    gcols = IN_COLS // 4
    tm = _pick(rows_all, 1536, 128)
    g4 = groups.reshape(4, gcols, D)

    n_lead = (1 if px_prev is not None else 0) + len(after)
    lead = ([px_prev] if px_prev is not None else []) + list(after)

    def body(chip_ref, *refs):
        h_ref, w_ref, o_ref = refs[n_lead:]
        o_ref[...] = _dot(h_ref[...], w_ref[...], NT).astype(BF16)
    return pl.pallas_call(
        body, name=name,
        grid_spec=pltpu.PrefetchScalarGridSpec(
            num_scalar_prefetch=1, grid=(rows_all // tm,),
            in_specs=[pl.BlockSpec(memory_space=pl.ANY)] * n_lead
            + [pl.BlockSpec((tm, D), lambda i, ch: (i, 0)), pl.BlockSpec((None, gcols, D), lambda i, ch: (j, 0, 0))],
            out_specs=pl.BlockSpec((tm, gcols), lambda i, ch: (i, ch[0] ^ j))),
        out_shape=_sds((rows_all, IN_COLS), BF16),
        input_output_aliases={1: 0} if px_prev is not None else {},
        compiler_params=_params(("parallel",)),
    )(chip, *lead, h_all, g4)


def _d_h_groups(dp_all, groups, chip, after):
    rows_all = dp_all.shape[0]
    gcols = IN_COLS // 4
    tm = _pick(rows_all, 1536, 128)
    g4 = groups.reshape(4, gcols, D)
    n_lead = len(after)

    def body(chip_ref, *refs):
        a_ref, w_ref, o_ref = refs[n_lead:]
        j = pl.program_id(1)
        part = _dot(a_ref[...], w_ref[...])

        @pl.when(j == 0)
        def _():
            o_ref[...] = part

        @pl.when(j > 0)
        def _():
            o_ref[...] += part

    return pl.pallas_call(
        body, name="d_h",
        grid_spec=pltpu.PrefetchScalarGridSpec(
            num_scalar_prefetch=1, grid=(rows_all // tm, 4),
            in_specs=[pl.BlockSpec(memory_space=pl.ANY)] * n_lead
            + [pl.BlockSpec((tm, gcols), lambda i, j, ch: (i, ch[0] ^ j)),
               pl.BlockSpec((None, gcols, D), lambda i, j, ch: (j, 0, 0))],
            out_specs=pl.BlockSpec((tm, D), lambda i, j, ch: (i, 0))),
        out_shape=_sds((rows_all, D), F32),
        compiler_params=_params(("parallel", "arbitrary")),
    )(chip, *after, dp_all, g4)


def _reduce_scatter_start(parts, core, name):
    got = _pair_exchange(parts, name + "_pair")
    sums = [_pair_add(p, g, core, "%s_add_%d" % (name, i)) for i, (p, g) in enumerate(zip(parts, got))]
    lands = [lax.empty((3,) + s_.shape[1:], BF16) for s_ in sums]
    return _exchange_start(sums, lands, _chip_routes(len(sums)), name + "_start")


def _reduce_scatter_finish(rs_state, after, chip, name):
    sums, landed = _exchange_wait(rs_state, after, name + "_wait")
    return [_chip_sum(s_, l_, chip, "%s_sum_%d" % (name, i)) for i, (s_, l_) in enumerate(zip(sums, landed))]


def _local_step(x, c, ctx, norm_w, ret_log2_decay, q_norm_w, k_norm_w, loss_target,
                mod, proj_in, proj_back, get_w_o, on_out_grads, on_in_grad):
    nb, seq, _ = x.shape
    cx = ctx.shape[1]
    t_rows, c_rows = nb * seq, nb * cx
    rows_all = t_rows + c_rows
    nc = seq // CH
    tm = _pick(seq, 256, 128)
    te = _pick(seq, 512, 128)
    assert cx % tm == 0 and t_rows % cx == 0 and seq % GRID_W == 0

    x2 = x.reshape(t_rows, D)
    ctx2 = ctx.reshape(c_rows, D)
    tgt = loss_target.reshape(t_rows, D)
    lg = _log_gamma(ret_log2_decay)
    cos, sin = _rope_tables(seq)

    mod3 = mod[:, None, :]
    h_all = _norm_fwd(x2, mod3, norm_w, rows_all, 0, seq, 0, None, te, "norm_fwd")
    h_all = _norm_fwd(ctx2, mod3, norm_w, rows_all, t_rows, c_rows, nb, h_all, tm, "norm_fwd_ctx")
    px = proj_in(h_all)
    s0f, s0b = _ctx_state(px, lg, nb, t_rows, cx)
    o_f, o_b, hist_f, hist_b = _ret_fwd(px, lg, s0f, s0b, nb, nc)
    yret16 = _ret_post(o_f, o_b, px, te)
    q16 = _qk_prep(px, q_norm_w, cos, sin, t_rows, 0, AQ, HQ, 4, seq, te, "q_prep")
    kx16 = _qk_prep(px, k_norm_w, cos, sin, t_rows, 0, AK, HKV, HKV, seq, te, "k_prep")
    kc16 = _qk_prep(px, k_norm_w, None, None, c_rows, t_rows, AK, HKV, HKV, seq, tm, "kc_prep")
    o_att, yatt16, lse = _att_fwd(q16, kx16, kc16, px, nb, seq, cx, tm)
    w_o_ret16, w_o_att16, w_out16 = get_w_o(lse)
    a_ret, a_att, y16 = _merge(yret16, yatt16, px, w_o_ret16, w_o_att16, te)
    dxn, dout16, dgate, loss_b = _outproj(y16, w_out16, x2, tgt, mod3, nb, seq, te)

    gw_out = _matmul(y16, dout16, ta=True, tm=D, tn=D, tk=D, out_dtype=BF16, name="gw_out")
    da_ret16, da_att16, dmr16, dma16 = _bwd_merge(dout16, w_out16, px, a_ret, a_att, te)
    gw_o_ret = _matmul(yret16, da_ret16, ta=True, tm=D, tn=D, tk=D, out_dtype=BF16, name="gw_o_ret")
    gw_o_att = _matmul(yatt16, da_att16, ta=True, tm=D, tn=D, tk=D, out_dtype=BF16, name="gw_o_att")
    out_state, out_started = on_out_grads([gw_o_ret, gw_o_att, gw_out])
    do16, drg16 = _bwd_branch_ret(da_ret16, w_o_ret16, px, o_f, o_b, te, after=out_started)
    dao, dag16 = _bwd_branch_att(da_att16, w_o_att16, px, o_att, te)
    dq_rot, dkx, dvx, dkc, dvc = _att_bwd(q16, kx16, kc16, px, dao, o_att, lse, nb, seq, cx, tm)
    daq16, gq = _qk_prep_bwd(dq_rot, px, q_norm_w, cos, sin, t_rows, 0, AQ, HQ, 4, seq, te, "q_prep_bwd")
    dak16, gk_lat = _qk_prep_bwd(dkx.reshape(t_rows, HKV * HD), px, k_norm_w, cos, sin, t_rows, 0, AK, HKV, HKV, seq, te,
                                 "k_prep_bwd")
    dcak16, gk_ctx = _qk_prep_bwd(dkc.reshape(c_rows, HKV * HD), px, k_norm_w, None, None, c_rows, t_rows, AK, HKV, HKV,
                                  seq, tm, "kc_prep_bwd")
    dq_f, dk_f, dv_f, dq_b, dk_b, dv_b, ds_f, ds_b, dlg_scan = _ret_bwd(px, lg, do16, hist_f, hist_b, nb, nc)
    dck16, dcv16, dlg_ctx = _ctx_state_bwd(px, lg, ds_f, ds_b, nb, t_rows, cx)
    dp_all = _assemble_lat(rows_all, dk_f, dk_b, dv_f, dv_b, dak16, dvx.reshape(t_rows, HKV * HD), dq_f, dq_b, drg16,
                           daq16, dag16, dmr16, dma16, tm)
    dp_all = _assemble_ctx(dp_all, dck16, dcv16, dcak16, dvc.reshape(c_rows, HKV * HD), t_rows, tm)
    gw_in_t = _matmul(dp_all, h_all, ta=True, tm=1536, tn=D, tk=1536, out_dtype=BF16, name="gw_in")
    in_state, in_started = on_in_grad(gw_in_t)
    dh = proj_back(dp_all, in_started)
    grad_x, dsh, dsc, gnw_lat = _norm_bwd(dh, x2, mod3, norm_w, dxn, 0, seq, 0, te, "norm_bwd")
    dsh_c, dsc_c, gnw_ctx = _norm_bwd(dh, ctx2, mod3, norm_w, None, t_rows, c_rows, nb, tm, "norm_bwd_ctx")

    dlg = (jnp.sum(dlg_scan[:, :, 0], axis=0) + jnp.sum(dlg_ctx[:, :, :2, 0], axis=0).T.reshape(2 * RH)).reshape(1, 2 * RH)
    misc = jnp.concatenate([gq, gk_lat + gk_ctx, dlg, jnp.sum(loss_b[:, 0, 0]).reshape(1, 1),
                            jnp.zeros((1, D - 2 * HD - 2 * RH - 1), F32)], axis=1)
    rows = []
    for b in range(nb):
        rows += [dsh[b], dsc[b], dgate[b]]
    rows += [dsh_c[0], dsc_c[0]] + [c[b:b + 1] for b in range(nb)] + [gnw_lat + gnw_ctx, misc]
    payload = jnp.concatenate(rows + [jnp.zeros((PAY_ROWS - len(rows), D), F32)], axis=0)
    return grad_x.reshape(nb, seq, D), out_state, in_state, payload


def _finish_small(gathered, nb, c_ctx, ret_log2_decay, w_ada16, dev):
    n_dev = gathered.shape[0]
    loc = 3 * D // n_dev
    dmod_all = gathered[:, :3 * nb].reshape(n_dev * nb, 3 * D)
    dmodc_parts = jnp.concatenate([gathered[:, 3 * nb:3 * nb + 2].reshape(n_dev, 2 * D), jnp.zeros((n_dev, D), F32)], axis=1)
    c_all = gathered[:, 3 * nb + 2:4 * nb + 2].reshape(n_dev * nb, D)
    nw_parts = gathered[:, 4 * nb + 2]
    misc_parts = gathered[:, 4 * nb + 3]
    n_rows = n_dev * nb + n_dev
    pad = (-n_rows) % 16
    c_rows = jnp.concatenate([c_all, jnp.broadcast_to(c_ctx.reshape(1, D), (n_dev, D)), jnp.zeros((pad, D), F32)], axis=0)
    dm_rows = jnp.concatenate([dmod_all, dmodc_parts, jnp.zeros((pad, 3 * D), F32)], axis=0)
    dm_loc_rows = lax.dynamic_slice_in_dim(dm_rows, dev * loc, loc, axis=1)
    r_pad = jnp.full((1, D), -1.0, F32).at[:, 2 * HD:2 * HD + 2 * RH].set(ret_log2_decay.reshape(1, 2 * RH))
    gb, gc, gnw, misc, gwa = _small_final(dmod_all, dmodc_parts, c_rows, dm_loc_rows, nw_parts, misc_parts,
                                          c_ctx.reshape(1, D), r_pad, w_ada16)
    return (gb, gc, gnw, misc[:, :HD], misc[:, HD:2 * HD], misc[:, 2 * HD:2 * HD + 2 * RH], gwa,
            misc[0, 2 * HD + 2 * RH])


def kernel(x, c, ctx, c_ctx, norm_w, w_ada, b_ada, w_in, ret_log2_decay, q_norm_w, k_norm_w, w_o_ret, w_o_att, w_out, loss_target, m_c_ctx, m_norm_w, m_w_ada, m_b_ada, m_w_in, m_ret_log2_decay, m_q_norm_w, m_k_norm_w, m_w_o_ret, m_w_o_att, m_w_out, v_c_ctx, v_norm_w, v_w_ada, v_b_ada, v_w_in, v_ret_log2_decay, v_q_norm_w, v_k_norm_w, v_w_o_ret, v_w_o_att, v_w_out):
    nb = x.shape[0]
    mx, my, mc = _mesh_pos()
    dev = 4 * mx + 2 * my + mc
    core = jnp.reshape(mc, (1,)).astype(jnp.int32)
    chip = jnp.reshape(2 * mx + my, (1,)).astype(jnp.int32)

    n_loc = 3 * D // N_DEV
    c8 = jnp.zeros((8, D), F32).at[:nb].set(c).at[nb].set(c_ctx)
    (c_all,) = _all_gather([c8], "gather_c")
    ada_shard = w_ada[0].astype(BF16)
    b_loc = lax.dynamic_slice(b_ada, (0, dev * n_loc), (1, n_loc))
    mod_cols = _mod_part(c_all.reshape(N_DEV * 8, D), ada_shard, b_loc)
    (mod_all,) = _all_gather([mod_cols], "gather_mod")
    mod = jnp.transpose(lax.dynamic_slice(mod_all, (0, dev * 8, 0), (N_DEV, 8, n_loc)), (1, 0, 2)).reshape(8, 3 * D)
    ada_land = lax.dynamic_update_slice(lax.empty((N_DEV,) + ada_shard.shape, BF16), ada_shard[None], (dev, 0, 0))

    w_in_t = jnp.transpose(w_in[0])
    in_shard = w_in_t.astype(BF16)
    groups = lax.dynamic_update_slice(lax.empty((N_DEV,) + in_shard.shape, BF16), in_shard[None], (mc, 0, 0))
    groups = _pair_fill(groups, 0, "gather_in_pair", after=(mod_all,))
    (near_send, near_recv, near_bufs, near_routes), gin_token = _exchange_start(
        [in_shard[None]], [groups], _group_routes((1, 2)), "gather_in_start")
    w_in_groups, wo_states, ada_states = [], [], []
    wo_shards = [w_[0].astype(BF16) for w_ in (w_o_ret, w_o_att, w_out)]
    wo_lands = [lax.dynamic_update_slice(lax.empty((N_DEV,) + s_.shape, BF16), s_[None], (dev, 0, 0)) for s_ in wo_shards]

    def proj_in(h_all):
        src, groups = near_bufs
        px = _in_proj_group(h_all, groups, 0, chip, None, (gin_token,), "in_proj_0")
        (far_send, far_recv, (src, groups), far_routes), far_token = _exchange_start(
            [src], [groups], _group_routes((3,)), "gather_in_start_far", after=(px,))
        wo_state, wo_token = _exchange_start([s_[None] for s_ in wo_shards], wo_lands, _bcast_routes(3),
                                             "gather_wo_start", after=(far_token,))
        wo_states.append(wo_state)
        ada_state, ada_token = _exchange_start([ada_shard[None]], [ada_land], _bcast_routes(1), "gather_ada_start",
                                               after=(wo_token,))
        ada_states.append(ada_state)
        for j in (1, 2, 3):
            state = ((near_send, near_recv, [src, groups], near_routes) if j < 3 else
                     (far_send, far_recv, [src, groups], far_routes))
            _, (src,), (groups,) = _exchange_wait_some(state, ada_token if j == 1 else px, (j - 1,) if j < 3 else None,
                                                       "gather_in_wait_%d" % j)
            groups = _pair_fill(groups, j, "gather_in_fill_%d" % j)
            px = _in_proj_group(h_all, groups, j, chip, px, (), "in_proj_%d" % j)
        w_in_groups.append(groups)
        return px

    def proj_back(dp_all, after):
        return _d_h_groups(dp_all, w_in_groups[0], chip, after)

    def get_w_o(after):
        _, (l_ret, l_att, l_out) = _exchange_wait(wo_states[0], after, "gather_wo_wait")
        return l_ret.reshape(RH * DV, D), l_att.reshape(D, D), l_out.reshape(D, D)

    def on_out_grads(grads):
        parts = [g_.reshape(N_DEV, g_.shape[0] // N_DEV, D) for g_ in grads]
        state, token = _reduce_scatter_start(parts, core, "rs_out")
        return state, (token,)

    def on_in_grad(grad):
        state, token = _reduce_scatter_start([grad.reshape(N_DEV, IN_COLS // N_DEV, D)], core, "rs_in")
        return state, (token,)

    grad_x, out_state, in_state, payload = _local_step(
        x, c, ctx, norm_w, ret_log2_decay, q_norm_w, k_norm_w, loss_target,
        mod, proj_in, proj_back, get_w_o, on_out_grads, on_in_grad)

    (gathered,) = _all_gather([payload], "gather_small")
    _, (l_ada,) = _exchange_wait(ada_states[0], gathered, "gather_ada_wait")
    w_ada16 = jnp.transpose(l_ada, (1, 0, 2)).reshape(D, 3 * D)
    gb, gc, gnw, gq, gk, gr, gwa, loss = _finish_small(gathered, nb, c_ctx, ret_log2_decay, w_ada16, dev)

    g_w_o_ret, g_w_o_att, g_w_out = _reduce_scatter_finish(out_state, gathered, chip, "rs_out")
    (g_w_in_t,) = _reduce_scatter_finish(in_state, gathered, chip, "rs_in")

    grads = [gc.reshape(c_ctx.shape), gnw, gwa[None], gb, g_w_in_t, gr.reshape(ret_log2_decay.shape), gq, gk,
             g_w_o_ret[None], g_w_o_att[None], g_w_out[None]]
    weights = [c_ctx, norm_w, w_ada, b_ada, w_in_t, ret_log2_decay, q_norm_w, k_norm_w, w_o_ret, w_o_att, w_out]
    ms = [m_c_ctx, m_norm_w, m_w_ada, m_b_ada, jnp.transpose(m_w_in[0]), m_ret_log2_decay, m_q_norm_w, m_k_norm_w,
          m_w_o_ret, m_w_o_att, m_w_out]
    vs = [v_c_ctx, v_norm_w, v_w_ada, v_b_ada, jnp.transpose(v_w_in[0]), v_ret_log2_decay, v_q_norm_w, v_k_norm_w,
          v_w_o_ret, v_w_o_att, v_w_out]
    deltas, new_ms, new_vs = [], [], []
    for i, (w, g, m, v) in enumerate(zip(weights, grads, ms, vs)):
        shape2 = (-1, w.shape[-1])
        res = _adamw(w.reshape(shape2), g.reshape(shape2), m.reshape(shape2), v.reshape(shape2), "adamw_%d" % i)
        for lst, r in zip((deltas, new_ms, new_vs), res):
            lst.append(jnp.transpose(r)[None] if i == 4 else r.reshape(w.shape))
    grads[4] = jnp.transpose(g_w_in_t)[None]
    return (loss, grad_x, *grads, *deltas, *new_ms, *new_vs)
```

```python
import numpy as np
import jax
import jax.numpy as jnp
from jax import lax
from jax.experimental import pallas as pl
from jax.experimental.pallas import tpu as pltpu

F32 = jnp.float32
BF16 = jnp.bfloat16

D = 1024
RH, DK, DV, CH = 4, 256, 512, 256
HQ, HKV, HD = 8, 2, 128
GRID_W = 64
ROPE_THETA = 10000.0
EPS = 1e-6
RK, RV, AK, AV, RQ, RG, AQ, AG, MR, MA = 0, 1024, 3072, 3328, 3584, 4608, 6656, 7680, 8704, 9728
IN_COLS = 10752
KV_COLS = 3584
N_DEV = 8
LR, B1, B2, ADAM_EPS, WD, STEP = 0.001, 0.9, 0.999, 1e-08, 0.01, 10
PAY_ROWS = 16
VMEM_LIMIT = 56 * 1024 * 1024
MESH_T = pl.DeviceIdType.MESH

NT = (((1,), (1,)), ((), ()))
TN = (((0,), (0,)), ((), ()))
SM_C = (HD ** -0.5) * float(np.log2(np.e))


def _params(sem):
    return pltpu.CompilerParams(dimension_semantics=sem, vmem_limit_bytes=VMEM_LIMIT)


def _pick(n, target, mult=8):
    best = None
    for t in range(mult, min(n, target) + 1, mult):
        if n % t == 0:
            best = t
    return best or n


def _dot(a, b, dn=None):
    if dn is None:
        return jnp.dot(a, b, preferred_element_type=F32)
    return lax.dot_general(a, b, dn, preferred_element_type=F32)


def _sig(v):
    return jax.nn.sigmoid(v)


def _silu(v):
    return v * _sig(v)


def _dsilu(v):
    s = _sig(v)
    return s * (1.0 + v * (1.0 - s))


def _sds(shape, dtype):
    return jax.ShapeDtypeStruct(shape, dtype)


def _matmul(a, b, *, ta=False, tb=False, tm, tn, tk, out_dtype, name, after=()):
    m = a.shape[1] if ta else a.shape[0]
    kdim = a.shape[0] if ta else a.shape[1]
    n = b.shape[0] if tb else b.shape[1]
    tm, tn, tk = _pick(m, tm, 128), _pick(n, tn, 128), _pick(kdim, tk, 128)
    nk = kdim // tk
    dn = (((0 if ta else 1,), (1 if tb else 0,)), ((), ()))

    def body(a_ref, b_ref, *rest):
        o_ref, acc_ref = rest[-2:]
        k = pl.program_id(2)
        part = _dot(a_ref[...].astype(BF16), b_ref[...].astype(BF16), dn)
        if nk == 1:
            o_ref[...] = part.astype(o_ref.dtype)
        else:
            @pl.when(k == 0)
            def _():
                acc_ref[...] = part

            @pl.when(k > 0)
            def _():
                acc_ref[...] += part

            @pl.when(k == nk - 1)
            def _():
                o_ref[...] = acc_ref[...].astype(o_ref.dtype)

    a_spec = pl.BlockSpec((tk, tm), lambda i, j, k: (k, i)) if ta else pl.BlockSpec((tm, tk), lambda i, j, k: (i, k))
    b_spec = pl.BlockSpec((tn, tk), lambda i, j, k: (j, k)) if tb else pl.BlockSpec((tk, tn), lambda i, j, k: (k, j))
    return pl.pallas_call(
        body, name=name, grid=(m // tm, n // tn, nk),
        in_specs=[a_spec, b_spec] + [pl.BlockSpec(memory_space=pl.ANY)] * len(after),
        out_specs=pl.BlockSpec((tm, tn), lambda i, j, k: (i, j)), out_shape=_sds((m, n), out_dtype),
        scratch_shapes=[pltpu.VMEM((tm, tn) if nk > 1 else (8, 128), F32)],
        compiler_params=_params(("parallel", "parallel", "arbitrary")),
    )(a, b, *after)


def _log_gamma(r):
    rp = jnp.full((8, 128), -1.0, F32).at[:2, :RH].set(r.reshape(2, RH))

    def body(r_ref, o_ref):
        o_ref[...] = jnp.log1p(-jnp.exp2(r_ref[...]))

    out = pl.pallas_call(body, name="log_gamma", out_shape=_sds((8, 128), F32))(rp)
    return out[:2, :RH]


def _mod_part(c_rows, w_ada_loc16, b_loc):
    def body(c_ref, w_ref, b_ref, o_ref):
        o_ref[...] = _dot(_silu(c_ref[...]).astype(BF16), w_ref[...]) + b_ref[...]

    return pl.pallas_call(
        body, name="mod_part", out_shape=_sds((c_rows.shape[0], w_ada_loc16.shape[1]), F32),
    )(c_rows, w_ada_loc16, b_loc)


def _norm_fwd(x2, mod3, norm_w, rows_all, row_off, rows_per_group, group0, h_prev, tm, name, after=()):
    rows = x2.shape[0]
    rb0 = row_off // tm
    bpg = rows_per_group // tm

    def body(*refs):
        x_ref, sh_ref, sc_ref, nw_ref, o_ref = refs[-5:]
        xv = x_ref[...]
        r = lax.rsqrt(jnp.mean(xv * xv, axis=-1, keepdims=True) + EPS)
        o_ref[...] = ((xv * r) * nw_ref[...] * (1.0 + sc_ref[...]) + sh_ref[...]).astype(BF16)

    in_specs = [pl.BlockSpec((tm, D), lambda i: (i, 0)),
                pl.BlockSpec((None, 1, D), lambda i: (group0 + i // bpg, 0, 0)),
                pl.BlockSpec((None, 1, D), lambda i: (group0 + i // bpg, 0, 1)),
                pl.BlockSpec((1, D), lambda i: (0, 0))]
    in_specs = [pl.BlockSpec(memory_space=pl.ANY)] * len(after) + in_specs
    args = list(after) + [x2, mod3, mod3, norm_w]
    alias = {}
    if h_prev is not None:
        in_specs.insert(0, pl.BlockSpec(memory_space=pl.ANY))
        args.insert(0, h_prev)
        alias = {0: 0}
    return pl.pallas_call(
        body, name=name, grid=(rows // tm,), in_specs=in_specs,
        out_specs=pl.BlockSpec((tm, D), lambda i: (rb0 + i, 0)), out_shape=_sds((rows_all, D), BF16),
        input_output_aliases=alias, compiler_params=_params(("parallel",)),
    )(*args)


def _decays(lg, fwd):
    ii = lax.broadcasted_iota(jnp.int32, (CH, CH), 0)
    jj = lax.broadcasted_iota(jnp.int32, (CH, CH), 1)
    ri = lax.broadcasted_iota(jnp.int32, (CH, 1), 0).astype(F32)
    rel = (ii - jj) if fwd else (jj - ii)
    relf = jnp.maximum(rel, 0).astype(F32)
    mask = jnp.where(rel >= 0, jnp.exp(lg * relf), 0.0)
    qe = (ri + 1.0) if fwd else (CH - ri)
    ke = (CH - 1.0 - ri) if fwd else ri
    return mask, relf, jnp.exp(lg * qe), qe, jnp.exp(lg * ke), ke


def _wide_specs(rowf):
    return [pl.BlockSpec((CH, 2 * DK), lambda b, c: (rowf(b, c), RQ // (2 * DK))),
            pl.BlockSpec((CH, 2 * DK), lambda b, c: (rowf(b, c), RQ // (2 * DK) + 1)),
            pl.BlockSpec((CH, RH * DK), lambda b, c: (rowf(b, c), RK // (RH * DK))),
            pl.BlockSpec((CH, 2 * DV), lambda b, c: (rowf(b, c), RV // (2 * DV))),
            pl.BlockSpec((CH, 2 * DV), lambda b, c: (rowf(b, c), RV // (2 * DV) + 1))]


def _head_qkv(refs, h):
    q0, q1, k, v0, v1 = refs
    lo = h % 2
    q = (q0, q1)[h // 2][:, lo * DK:(lo + 1) * DK].astype(F32)
    kk = k[:, h * DK:(h + 1) * DK].astype(F32) * (DK ** -0.5)
    v16 = (v0, v1)[h // 2][:, lo * DV:(lo + 1) * DV].astype(BF16)
    return q, kk, v16


def _ctx_state(px, lg, nb, t_rows, cx):
    rb = t_rows // cx

    def body(lg_ref, k_ref, v_ref, sf_ref, sb_ref):
        h = pl.program_id(1)
        pos = lax.broadcasted_iota(jnp.int32, (cx, 1), 0).astype(F32)
        k = k_ref[...].astype(F32) * (DK ** -0.5)
        v16 = v_ref[...].astype(BF16)
        wf = jnp.exp(lg_ref[0, h] * (cx - 1.0 - pos))
        wb = jnp.exp(lg_ref[1, h] * pos)
        sf_ref[...] = _dot((k * wf).astype(BF16), v16, TN)
        sb_ref[...] = _dot((k * wb).astype(BF16), v16, TN)

    st = pl.BlockSpec((None, None, DK, DV), lambda b, h: (b, h, 0, 0))
    return pl.pallas_call(
        body, name="ctx_state", grid=(nb, RH),
        in_specs=[pl.BlockSpec(memory_space=pltpu.SMEM),
                  pl.BlockSpec((cx, DK), lambda b, h: (rb + b, RK // DK + h)),
                  pl.BlockSpec((cx, DV), lambda b, h: (rb + b, RV // DV + h))],
        out_specs=[st, st], out_shape=[_sds((nb, RH, DK, DV), F32)] * 2,
        compiler_params=_params(("parallel", "parallel")),
    )(lg, px, px)


def _ret_fwd(px, lg, s0f, s0b, nb, nc):
    t_rows = nb * nc * CH

    def body(lg_ref, *refs):
        ins = (refs[0:5], refs[5:10])
        s0f_ref, s0b_ref, of_ref, ob_ref, hf_ref, hb_ref, sf, sb = refs[10:]
        c = pl.program_id(1)

        @pl.when(c == 0)
        def _():
            sf[...] = s0f_ref[...]
            sb[...] = s0b_ref[...]

        for d, (o_ref, h_ref, s) in enumerate(((of_ref, hf_ref, sf), (ob_ref, hb_ref, sb))):
            for h in range(RH):
                lg_d = lg_ref[d, h]
                mask, _, qd, _, kd, _ = _decays(lg_d, d == 0)
                q, k, v16 = _head_qkv(ins[d], h)
                a = _dot(q.astype(BF16), k.astype(BF16), NT)
                st = s[h]
                st16 = st.astype(BF16)
                h_ref[h] = st16
                o = _dot((a * mask).astype(BF16), v16) + _dot((q * qd).astype(BF16), st16)
                o_ref[:, h * DV:(h + 1) * DV] = o.astype(BF16)
                s[h] = st * jnp.exp(lg_d * CH) + _dot((k * kd).astype(BF16), v16, TN)

    def fw(b, c):
        return b * nc + c

    def bw(b, c):
        return b * nc + nc - 1 - c

    st = pl.BlockSpec((None, RH, DK, DV), lambda b, c: (b, 0, 0, 0))
    in_specs = [pl.BlockSpec(memory_space=pltpu.SMEM)] + _wide_specs(fw) + _wide_specs(bw) + [st, st]
    out_specs = [pl.BlockSpec((CH, RH * DV), lambda b, c: (fw(b, c), 0)),
                 pl.BlockSpec((CH, RH * DV), lambda b, c: (bw(b, c), 0)),
                 pl.BlockSpec((None, None, RH, DK, DV), lambda b, c: (b, c, 0, 0, 0)),
                 pl.BlockSpec((None, None, RH, DK, DV), lambda b, c: (b, nc - 1 - c, 0, 0, 0))]
    return pl.pallas_call(
        body, name="ret_fwd", grid=(nb, nc), in_specs=in_specs, out_specs=out_specs,
        out_shape=[_sds((t_rows, RH * DV), BF16)] * 2 + [_sds((nb, nc, RH, DK, DV), BF16)] * 2,
        scratch_shapes=[pltpu.VMEM((RH, DK, DV), F32), pltpu.VMEM((RH, DK, DV), F32)],
        compiler_params=_params(("parallel", "arbitrary")),
    )(lg, *([px] * 10), s0f, s0b)


def _ret_post(o_f, o_b, px, tm):
    t_rows = o_f.shape[0]

    def body(of_ref, ob_ref, g0, g1, g2, g3, y_ref):
        for h, g_ref in enumerate((g0, g1, g2, g3)):
            sl = slice(h * DV, (h + 1) * DV)
            o = of_ref[:, sl].astype(F32) + ob_ref[:, sl].astype(F32)
            r = lax.rsqrt(jnp.mean(o * o, axis=-1, keepdims=True) + EPS)
            y_ref[:, sl] = ((o * r) * _silu(g_ref[...].astype(F32))).astype(BF16)

    def gate(h):
        return pl.BlockSpec((tm, DV), lambda i: (i, RG // DV + h))

    wide = pl.BlockSpec((tm, RH * DV), lambda i: (i, 0))
    return pl.pallas_call(
        body, name="ret_post", grid=(t_rows // tm,),
        in_specs=[wide, wide] + [gate(h) for h in range(RH)],
        out_specs=wide, out_shape=_sds((t_rows, RH * DV), BF16),
        compiler_params=_params(("parallel",)),
    )(o_f, o_b, *([px] * RH))


def _rope_tables(seq):
    rows = seq // GRID_W
    row = np.repeat(np.arange(rows, dtype=np.float32), GRID_W)
    col = np.tile(np.arange(GRID_W, dtype=np.float32), rows)
    half = HD // 2
    freqs = (ROPE_THETA ** (-np.arange(0, half, 2, dtype=np.float32) / half)).astype(np.float32)
    ang = np.concatenate([row[:, None] * freqs, col[:, None] * freqs], axis=-1).astype(np.float32)
    cos = np.repeat(np.cos(ang), 2, axis=-1).astype(np.float32)
    sin = np.repeat(np.sin(ang), 2, axis=-1).astype(np.float32)
    sign = np.tile(np.array([-1.0, 1.0], np.float32), HD // 2)
    return jnp.asarray(cos), jnp.asarray(sin * sign)


def _swap_pairs(v):
    lane = lax.broadcasted_iota(jnp.int32, v.shape, 1)
    return jnp.where((lane & 1) == 0, pltpu.roll(v, HD - 1, 1), pltpu.roll(v, 1, 1))


def _qk_prep(px, nw, cos, sin, rows, row_off, col_off, heads, hb, seq, tm, name):
    rope = cos is not None
    rb0 = row_off // tm
    pb = seq // tm if rope else 1
    bw = hb * HD

    def body(*refs):
        if rope:
            x_ref, w_ref, c_ref, s_ref, o_ref = refs
        else:
            x_ref, w_ref, o_ref = refs
        for h in range(hb):
            sl = slice(h * HD, (h + 1) * HD)
            xv = x_ref[:, sl].astype(F32)
            r = lax.rsqrt(jnp.mean(xv * xv, axis=-1, keepdims=True) + EPS)
            t = (xv * r) * w_ref[...]
            if rope:
                t = t * c_ref[...] + _swap_pairs(t) * s_ref[...]
            o_ref[:, sl] = t.astype(BF16)

    in_specs = [pl.BlockSpec((tm, bw), lambda i, j: (rb0 + i, col_off // bw + j)),
                pl.BlockSpec((1, HD), lambda i, j: (0, 0))]
    args = [px, nw]
    if rope:
        in_specs += [pl.BlockSpec((tm, HD), lambda i, j: (i % pb, 0))] * 2
        args += [cos, sin]
    return pl.pallas_call(
        body, name=name, grid=(rows // tm, heads // hb), in_specs=in_specs,
        out_specs=pl.BlockSpec((tm, bw), lambda i, j: (i, j)), out_shape=_sds((rows, heads * HD), BF16),
        compiler_params=_params(("parallel", "parallel")),
    )(*args)


def _att_fwd(q16, kx16, kc16, px, nb, seq, cx, tq):
    t_rows = nb * seq
    nq = seq // tq
    rep = HQ // HKV
    gw = rep * HD

    def body(q_ref, kx_ref, kc_ref, vx_ref, vc_ref, g_ref, o_ref, y_ref, l_ref):
        kx = kx_ref[...]
        kc = kc_ref[...]
        vx = vx_ref[...].astype(BF16)
        vc = vc_ref[...].astype(BF16)
        l_ref[...] = jnp.zeros_like(l_ref)
        for r in range(rep):
            sl = slice(r * HD, (r + 1) * HD)
            q = q_ref[:, sl]
            s1 = _dot(q, kx, NT)
            s2 = _dot(q, kc, NT)
            m = jnp.maximum(jnp.max(s1, axis=-1, keepdims=True), jnp.max(s2, axis=-1, keepdims=True))
            e1 = jnp.exp2((s1 - m) * SM_C)
            e2 = jnp.exp2((s2 - m) * SM_C)
            tot = jnp.sum(e1, axis=-1, keepdims=True) + jnp.sum(e2, axis=-1, keepdims=True)
            o = (_dot(e1.astype(BF16), vx) + _dot(e2.astype(BF16), vc)) * (1.0 / tot)
            o_ref[:, sl] = o
            y_ref[:, sl] = (o * _silu(g_ref[:, sl].astype(F32))).astype(BF16)
            l_ref[:, r:r + 1] = m * SM_C + jnp.log(tot) * float(np.log2(np.e))

    qblk = pl.BlockSpec((tq, gw), lambda b, g, i: (b * nq + i, g))
    return pl.pallas_call(
        body, name="att_fwd", grid=(nb, HKV, nq),
        in_specs=[qblk,
                  pl.BlockSpec((seq, HD), lambda b, g, i: (b, g)),
                  pl.BlockSpec((cx, HD), lambda b, g, i: (b, g)),
                  pl.BlockSpec((seq, HD), lambda b, g, i: (b, AV // HD + g)),
                  pl.BlockSpec((cx, HD), lambda b, g, i: (t_rows // cx + b, AV // HD + g)),
                  pl.BlockSpec((tq, gw), lambda b, g, i: (b * nq + i, AG // gw + g))],
        out_specs=[qblk, qblk, pl.BlockSpec((tq, 128), lambda b, g, i: (b * nq + i, g))],
        out_shape=[_sds((t_rows, D), F32), _sds((t_rows, D), BF16), _sds((t_rows, HKV * 128), F32)],
        compiler_params=_params(("parallel", "parallel", "parallel")),
    )(q16, kx16, kc16, px, px, px)


def _gate_specs(tm, col0):
    hw = D // 2
    return [pl.BlockSpec((tm, hw), lambda i: (i, col0 // hw)), pl.BlockSpec((tm, hw), lambda i: (i, col0 // hw + 1))]


def _merge(yret16, yatt16, px, w_o_ret16, w_o_att16, tm):
    t_rows = yret16.shape[0]
    hw = D // 2

    def body(yr_ref, wr_ref, ya_ref, wa_ref, mr0, mr1, ma0, ma1, ar_ref, aa_ref, y_ref):
        ar = _dot(yr_ref[...], wr_ref[...])
        aa = _dot(ya_ref[...], wa_ref[...])
        ar_ref[...] = ar.astype(BF16)
        aa_ref[...] = aa.astype(BF16)
        for j, (mr_ref, ma_ref) in enumerate(((mr0, ma0), (mr1, ma1))):
            sl = slice(j * hw, (j + 1) * hw)
            y_ref[:, sl] = (_sig(mr_ref[...].astype(F32)) * ar[:, sl]
                            + _sig(ma_ref[...].astype(F32)) * aa[:, sl]).astype(BF16)

    row = pl.BlockSpec((tm, D), lambda i: (i, 0))
    return pl.pallas_call(
        body, name="merge", grid=(t_rows // tm,),
        in_specs=[pl.BlockSpec((tm, RH * DV), lambda i: (i, 0)), pl.BlockSpec((RH * DV, D), lambda i: (0, 0)),
                  row, pl.BlockSpec((D, D), lambda i: (0, 0))] + _gate_specs(tm, MR) + _gate_specs(tm, MA),
        out_specs=[row, row, row], out_shape=[_sds((t_rows, D), BF16)] * 3,
        compiler_params=_params(("parallel",)),
    )(yret16, w_o_ret16, yatt16, w_o_att16, px, px, px, px)


def _outproj(y16, w_out16, x2, tgt, mod3, nb, seq, tm):
    t_rows = nb * seq
    bpb = seq // tm

    def body(y_ref, w_ref, x_ref, t_ref, g_ref, dxn_ref, dout_ref, dg_ref, loss_ref):
        i = pl.program_id(1)
        out = _dot(y_ref[...], w_ref[...])
        gate = g_ref[...]
        diff = x_ref[...] + gate * out - t_ref[...]
        dxn = diff * (1.0 / D)
        dxn_ref[...] = dxn
        dout_ref[...] = (gate * dxn).astype(BF16)
        dg = jnp.sum(dxn * out, axis=0, keepdims=True)
        ls = jnp.broadcast_to(jnp.sum(diff * diff) * (0.5 / D), (1, 128))

        @pl.when(i == 0)
        def _():
            dg_ref[...] = dg
            loss_ref[...] = ls

        @pl.when(i > 0)
        def _():
            dg_ref[...] += dg
            loss_ref[...] += ls

    row = pl.BlockSpec((tm, D), lambda b, i: (b * bpb + i, 0))
    return pl.pallas_call(
        body, name="outproj", grid=(nb, bpb),
        in_specs=[row, pl.BlockSpec((D, D), lambda b, i: (0, 0)), row, row,
                  pl.BlockSpec((None, 1, D), lambda b, i: (b, 0, 2))],
        out_specs=[row, row, pl.BlockSpec((None, 1, D), lambda b, i: (b, 0, 0)),
                   pl.BlockSpec((None, 1, 128), lambda b, i: (b, 0, 0))],
        out_shape=[_sds((t_rows, D), F32), _sds((t_rows, D), BF16), _sds((nb, 1, D), F32), _sds((nb, 1, 128), F32)],
        compiler_params=_params(("parallel", "arbitrary")),
    )(y16, w_out16, x2, tgt, mod3)


def _bwd_merge(dout16, w_out16, px, a_ret, a_att, tm):
    t_rows = dout16.shape[0]
    hw = D // 2

    def body(do_ref, w_ref, mr0, mr1, ma0, ma1, ar_ref, aa_ref, dar_ref, daa_ref, dmr_ref, dma_ref):
        dy_all = _dot(do_ref[...], w_ref[...], NT)
        for j, (mr_ref, ma_ref) in enumerate(((mr0, ma0), (mr1, ma1))):
            sl = slice(j * hw, (j + 1) * hw)
            dy = dy_all[:, sl]
            sr = _sig(mr_ref[...].astype(F32))
            sa = _sig(ma_ref[...].astype(F32))
            dar_ref[:, sl] = (dy * sr).astype(BF16)
            daa_ref[:, sl] = (dy * sa).astype(BF16)
            dmr_ref[:, sl] = (dy * ar_ref[:, sl].astype(F32) * sr * (1.0 - sr)).astype(BF16)
            dma_ref[:, sl] = (dy * aa_ref[:, sl].astype(F32) * sa * (1.0 - sa)).astype(BF16)

    row = pl.BlockSpec((tm, D), lambda i: (i, 0))
    return pl.pallas_call(
        body, name="bwd_merge", grid=(t_rows // tm,),
        in_specs=[row, pl.BlockSpec((D, D), lambda i: (0, 0))] + _gate_specs(tm, MR) + _gate_specs(tm, MA) + [row, row],
        out_specs=[row] * 4, out_shape=[_sds((t_rows, D), BF16)] * 4,
        compiler_params=_params(("parallel",)),
    )(dout16, w_out16, px, px, px, px, a_ret, a_att)


def _bwd_branch_ret(da_ret16, w_o_ret16, px, o_f, o_b, tm, after=()):
    t_rows = da_ret16.shape[0]

    def body(da_ref, w_ref, g0, g1, g2, g3, of_ref, ob_ref, *rest):
        do_ref, dg_ref = rest[-2:]
        da = da_ref[...]
        for h, g_ref in enumerate((g0, g1, g2, g3)):
            sl = slice(h * DV, (h + 1) * DV)
            dy = _dot(da, w_ref[sl, :], NT)
            g = g_ref[...].astype(F32)
            o = of_ref[:, sl].astype(F32) + ob_ref[:, sl].astype(F32)
            r = lax.rsqrt(jnp.mean(o * o, axis=-1, keepdims=True) + EPS)
            on = o * r
            don = dy * _silu(g)
            dg_ref[:, sl] = (dy * on * _dsilu(g)).astype(BF16)
            do_ref[:, sl] = (r * (don - on * jnp.mean(on * don, axis=-1, keepdims=True))).astype(BF16)

    def gate(h):
        return pl.BlockSpec((tm, DV), lambda i: (i, RG // DV + h))

    wide = pl.BlockSpec((tm, RH * DV), lambda i: (i, 0))
    return pl.pallas_call(
        body, name="bwd_branch_ret", grid=(t_rows // tm,),
        in_specs=[pl.BlockSpec((tm, D), lambda i: (i, 0)), pl.BlockSpec((RH * DV, D), lambda i: (0, 0))]
        + [gate(h) for h in range(RH)] + [wide, wide] + [pl.BlockSpec(memory_space=pl.ANY)] * len(after),
        out_specs=[wide, wide], out_shape=[_sds((t_rows, RH * DV), BF16)] * 2,
        compiler_params=_params(("parallel",)),
    )(da_ret16, w_o_ret16, *([px] * RH), o_f, o_b, *after)


def _bwd_branch_att(da_att16, w_o_att16, px, o_att, tm):
    t_rows = da_att16.shape[0]
    hw = D // 2

    def body(da_ref, w_ref, g0, g1, o_ref, dao_ref, dg_ref):
        dy_all = _dot(da_ref[...], w_ref[...], NT)
        for j, g_ref in enumerate((g0, g1)):
            sl = slice(j * hw, (j + 1) * hw)
            dy = dy_all[:, sl]
            g = g_ref[...].astype(F32)
            dao_ref[:, sl] = dy * _silu(g)
            dg_ref[:, sl] = (dy * o_ref[:, sl] * _dsilu(g)).astype(BF16)

    row = pl.BlockSpec((tm, D), lambda i: (i, 0))
    return pl.pallas_call(
        body, name="bwd_branch_att", grid=(t_rows // tm,),
        in_specs=[row, pl.BlockSpec((D, D), lambda i: (0, 0))] + _gate_specs(tm, AG) + [row],
        out_specs=[row, row], out_shape=[_sds((t_rows, D), F32), _sds((t_rows, D), BF16)],
        compiler_params=_params(("parallel",)),
    )(da_att16, w_o_att16, px, px, o_att)


def _att_bwd(q16, kx16, kc16, px, dao, o_att, lse, nb, seq, cx, tq):
    t_rows = nb * seq
    nq = seq // tq
    rep = HQ // HKV
    gw = rep * HD
    scale = HD ** -0.5

    def body(q_ref, kx_ref, kc_ref, vx_ref, vc_ref, dao_ref, o_ref, l_ref, dq_ref, dkx_ref, dvx_ref, dkc_ref, dvc_ref):
        i = pl.program_id(2)
        kx = kx_ref[...]
        kc = kc_ref[...]
        vx = vx_ref[...].astype(BF16)
        vc = vc_ref[...].astype(BF16)
        dkx = jnp.zeros((seq, HD), F32)
        dvx = jnp.zeros((seq, HD), F32)
        dkc = jnp.zeros((cx, HD), F32)
        dvc = jnp.zeros((cx, HD), F32)
        for r in range(rep):
            sl = slice(r * HD, (r + 1) * HD)
            q = q_ref[:, sl]
            lr = l_ref[:, r:r + 1]
            p1 = jnp.exp2(_dot(q, kx, NT) * SM_C - lr)
            p2 = jnp.exp2(_dot(q, kc, NT) * SM_C - lr)
            da = dao_ref[:, sl]
            da16 = da.astype(BF16)
            delta = jnp.sum(da * o_ref[:, sl], axis=-1, keepdims=True)
            ds1 = (p1 * (_dot(da16, vx, NT) - delta)).astype(BF16)
            ds2 = (p2 * (_dot(da16, vc, NT) - delta)).astype(BF16)
            dq_ref[:, sl] = (_dot(ds1, kx) + _dot(ds2, kc)) * scale
            dkx += _dot(ds1, q, TN)
            dkc += _dot(ds2, q, TN)
            dvx += _dot(p1.astype(BF16), da16, TN)
            dvc += _dot(p2.astype(BF16), da16, TN)
        dkx = dkx * scale
        dkc = dkc * scale

        @pl.when(i == 0)
        def _():
            dkx_ref[...] = dkx
            dvx_ref[...] = dvx
            dkc_ref[...] = dkc
            dvc_ref[...] = dvc

        @pl.when(i > 0)
        def _():
            dkx_ref[...] += dkx
            dvx_ref[...] += dvx
            dkc_ref[...] += dkc
            dvc_ref[...] += dvc

    qblk = pl.BlockSpec((tq, gw), lambda b, g, i: (b * nq + i, g))
    kxb = pl.BlockSpec((None, seq, HD), lambda b, g, i: (b, 0, g))
    kcb = pl.BlockSpec((None, cx, HD), lambda b, g, i: (b, 0, g))
    return pl.pallas_call(
        body, name="att_bwd", grid=(nb, HKV, nq),
        in_specs=[qblk,
                  pl.BlockSpec((seq, HD), lambda b, g, i: (b, g)),
                  pl.BlockSpec((cx, HD), lambda b, g, i: (b, g)),
                  pl.BlockSpec((seq, HD), lambda b, g, i: (b, AV // HD + g)),
                  pl.BlockSpec((cx, HD), lambda b, g, i: (t_rows // cx + b, AV // HD + g)),
                  qblk, qblk, pl.BlockSpec((tq, 128), lambda b, g, i: (b * nq + i, g))],
        out_specs=[qblk, kxb, kxb, kcb, kcb],
        out_shape=[_sds((t_rows, D), F32), _sds((nb, seq, HKV * HD), F32), _sds((nb, seq, HKV * HD), F32),
                   _sds((nb, cx, HKV * HD), F32), _sds((nb, cx, HKV * HD), F32)],
        compiler_params=_params(("parallel", "parallel", "arbitrary")),
    )(q16, kx16, kc16, px, px, dao, o_att, lse)


def _qk_prep_bwd(dt, px, nw, cos, sin, rows, row_off, col_off, heads, hb, seq, tm, name):
    rope = cos is not None
    rb0 = row_off // tm
    pb = seq // tm if rope else 1
    bw = hb * HD

    def body(*refs):
        if rope:
            d_ref, x_ref, w_ref, c_ref, s_ref, dx_ref, dw_ref = refs
        else:
            d_ref, x_ref, w_ref, dx_ref, dw_ref = refs
        first = jnp.logical_and(pl.program_id(0) == 0, pl.program_id(1) == 0)
        dw = jnp.zeros((1, HD), F32)
        for h in range(hb):
            sl = slice(h * HD, (h + 1) * HD)
            dtv = d_ref[:, sl]
            if rope:
                dtv = dtv * c_ref[...] + _swap_pairs(dtv * s_ref[...])
            xv = x_ref[:, sl].astype(F32)
            r = lax.rsqrt(jnp.mean(xv * xv, axis=-1, keepdims=True) + EPS)
            xh = xv * r
            dxh = dtv * w_ref[...]
            dx_ref[:, sl] = (r * (dxh - xh * jnp.mean(dxh * xh, axis=-1, keepdims=True))).astype(BF16)
            dw += jnp.sum(dtv * xh, axis=0, keepdims=True)

        @pl.when(first)
        def _():
            dw_ref[...] = dw

        @pl.when(jnp.logical_not(first))
        def _():
            dw_ref[...] += dw

    blk = pl.BlockSpec((tm, bw), lambda i, j: (i, j))
    in_specs = [blk, pl.BlockSpec((tm, bw), lambda i, j: (rb0 + i, col_off // bw + j)),
                pl.BlockSpec((1, HD), lambda i, j: (0, 0))]
    args = [dt, px, nw]
    if rope:
        in_specs += [pl.BlockSpec((tm, HD), lambda i, j: (i % pb, 0))] * 2
        args += [cos, sin]
    return pl.pallas_call(
        body, name=name, grid=(rows // tm, heads // hb), in_specs=in_specs,
        out_specs=[blk, pl.BlockSpec((1, HD), lambda i, j: (0, 0))],
        out_shape=[_sds((rows, heads * HD), BF16), _sds((1, HD), F32)],
        compiler_params=_params(("arbitrary", "arbitrary")),
    )(*args)


def _ret_bwd(px, lg, do16, hist_f, hist_b, nb, nc):
    t_rows = nb * nc * CH

    def body(lg_ref, *refs):
        ins = (refs[0:5], refs[7:12])
        do_refs = (refs[5], refs[12])
        h_refs = (refs[6], refs[13])
        outs = (refs[14:17], refs[17:20])
        ds_outs = (refs[20], refs[21])
        dlg_ref = refs[22]
        dss = (refs[23], refs[24])
        c = pl.program_id(1)

        @pl.when(c == 0)
        def _():
            dss[0][...] = jnp.zeros_like(dss[0])
            dss[1][...] = jnp.zeros_like(dss[1])
            dlg_ref[...] = jnp.zeros_like(dlg_ref)

        for d in range(2):
            dq_ref, dk_ref, dv_ref = outs[d]
            for h in range(RH):
                lg_d = lg_ref[d, h]
                mask, relf, qd, qe, kd, ke = _decays(lg_d, d == 0)
                g_ch = jnp.exp(lg_d * CH)
                q, k, v16 = _head_qkv(ins[d], h)
                q16 = q.astype(BF16)
                k16 = k.astype(BF16)
                do16v = do_refs[d][:, h * DV:(h + 1) * DV]
                st16 = h_refs[d][h]
                dst = dss[d][h]
                dst16 = dst.astype(BF16)
                a = _dot(q16, k16, NT) * mask
                dp = _dot(do16v, v16, NT)
                da16 = (dp * mask).astype(BF16)
                dq_cross = _dot(do16v, st16, NT) * qd
                dq_ref[:, h * DK:(h + 1) * DK] = (_dot(da16, k16) + dq_cross).astype(BF16)
                dk_state = _dot(v16, dst16, NT) * kd
                dk_ref[:, h * DK:(h + 1) * DK] = ((_dot(da16, q16, TN) + dk_state) * (DK ** -0.5)).astype(BF16)
                dv = _dot(a.astype(BF16), do16v, TN) + _dot((k * kd).astype(BF16), dst16)
                dv_ref[:, h * DV:(h + 1) * DV] = dv.astype(BF16)
                dlg = (jnp.sum(relf * a * dp)
                       + jnp.sum(qe * jnp.sum(q * dq_cross, axis=-1, keepdims=True))
                       + jnp.sum(ke * jnp.sum(k * dk_state, axis=-1, keepdims=True))
                       + CH * g_ch * jnp.sum(dst * st16.astype(F32)))
                row = d * RH + h
                dlg_ref[row:row + 1, :] += jnp.broadcast_to(dlg, (1, 128))
                ds_new = g_ch * dst + _dot((q * qd).astype(BF16), do16v, TN)
                dss[d][h] = ds_new

                @pl.when(c == nc - 1)
                def _():
                    ds_outs[d][h] = ds_new

    def fw(b, c):
        return b * nc + nc - 1 - c

    def bw(b, c):
        return b * nc + c

    def rows(rowf, width):
        return pl.BlockSpec((CH, width), lambda b, c: (rowf(b, c), 0))

    def hist(rowf):
        return pl.BlockSpec((None, None, RH, DK, DV), lambda b, c: (b, rowf(0, c), 0, 0, 0))

    st = pl.BlockSpec((None, RH, DK, DV), lambda b, c: (b, 0, 0, 0))
    in_specs = [pl.BlockSpec(memory_space=pltpu.SMEM)]
    out_specs = []
    for rowf in (fw, bw):
        in_specs += _wide_specs(rowf) + [rows(rowf, RH * DV), hist(rowf)]
        out_specs += [rows(rowf, RH * DK), rows(rowf, RH * DK), rows(rowf, RH * DV)]
    out_specs += [st, st, pl.BlockSpec((None, 8, 128), lambda b, c: (b, 0, 0))]
    qk = _sds((t_rows, RH * DK), BF16)
    vv = _sds((t_rows, RH * DV), BF16)
    return pl.pallas_call(
        body, name="ret_bwd", grid=(nb, nc), in_specs=in_specs, out_specs=out_specs,
        out_shape=[qk, qk, vv, qk, qk, vv, _sds((nb, RH, DK, DV), F32), _sds((nb, RH, DK, DV), F32),
                   _sds((nb, 8, 128), F32)],
        scratch_shapes=[pltpu.VMEM((RH, DK, DV), F32), pltpu.VMEM((RH, DK, DV), F32)],
        compiler_params=_params(("parallel", "arbitrary")),
    )(lg, *([px] * 5), do16, hist_f, *([px] * 5), do16, hist_b)


def _ctx_state_bwd(px, lg, ds_f, ds_b, nb, t_rows, cx):
    rb = t_rows // cx

    def body(lg_ref, k_ref, v_ref, dsf_ref, dsb_ref, dk_ref, dv_ref, dlg_ref):
        h = pl.program_id(1)
        pos = lax.broadcasted_iota(jnp.int32, (cx, 1), 0).astype(F32)
        k = k_ref[...].astype(F32) * (DK ** -0.5)
        v16 = v_ref[...].astype(BF16)
        dk = jnp.zeros((cx, DK), F32)
        dv = jnp.zeros((cx, DV), F32)
        dlg_ref[...] = jnp.zeros_like(dlg_ref)
        for d, (ds_ref, e) in enumerate(((dsf_ref, cx - 1.0 - pos), (dsb_ref, pos))):
            w = jnp.exp(lg_ref[d, h] * e)
            ds16 = ds_ref[...].astype(BF16)
            t = _dot(v16, ds16, NT)
            dk += t * w
            dv += _dot((k * w).astype(BF16), ds16)
            dlg = jnp.sum(e * w * jnp.sum(k * t, axis=-1, keepdims=True))
            dlg_ref[d:d + 1, :] = jnp.broadcast_to(dlg, (1, 128))
        dk_ref[...] = (dk * (DK ** -0.5)).astype(BF16)
        dv_ref[...] = dv.astype(BF16)

    st = pl.BlockSpec((None, None, DK, DV), lambda b, h: (b, h, 0, 0))
    return pl.pallas_call(
        body, name="ctx_state_bwd", grid=(nb, RH),
        in_specs=[pl.BlockSpec(memory_space=pltpu.SMEM),
                  pl.BlockSpec((cx, DK), lambda b, h: (rb + b, RK // DK + h)),
                  pl.BlockSpec((cx, DV), lambda b, h: (rb + b, RV // DV + h)), st, st],
        out_specs=[pl.BlockSpec((cx, DK), lambda b, h: (b, h)), pl.BlockSpec((cx, DV), lambda b, h: (b, h)),
                   pl.BlockSpec((None, None, 8, 128), lambda b, h: (b, h, 0, 0))],
        out_shape=[_sds((nb * cx, RH * DK), BF16), _sds((nb * cx, RH * DV), BF16), _sds((nb, RH, 8, 128), F32)],
        compiler_params=_params(("parallel", "parallel")),
    )(lg, px, px, ds_f, ds_b)


def _assemble_lat(rows_all, dk_f, dk_b, dv_f, dv_b, dak16, dvx, dq_f, dq_b, drg16, daq16, dag16, dmr16, dma16, tm):
    t_rows = dk_f.shape[0]

    def body(dkf, dkb, dvf, dvb, dak, dav, dqf, dqb, drg, daq, dag, dmr, dma, o_ref):
        o_ref[:, RK:RK + RH * DK] = (dkf[...].astype(F32) + dkb[...].astype(F32)).astype(BF16)
        o_ref[:, RV:RV + RH * DV] = (dvf[...].astype(F32) + dvb[...].astype(F32)).astype(BF16)
        o_ref[:, AK:AK + HKV * HD] = dak[...]
        o_ref[:, AV:AV + HKV * HD] = dav[...].astype(BF16)
        o_ref[:, RQ:RQ + RH * DK] = (dqf[...].astype(F32) + dqb[...].astype(F32)).astype(BF16)
        o_ref[:, RG:RG + RH * DV] = drg[...]
        o_ref[:, AQ:AQ + D] = daq[...]
        o_ref[:, AG:AG + D] = dag[...]
        o_ref[:, MR:MR + D] = dmr[...]
        o_ref[:, MA:MA + D] = dma[...]

    args = (dk_f, dk_b, dv_f, dv_b, dak16, dvx, dq_f, dq_b, drg16, daq16, dag16, dmr16, dma16)
    return pl.pallas_call(
        body, name="assemble_lat", grid=(t_rows // tm,),
        in_specs=[pl.BlockSpec((tm, a.shape[1]), lambda i: (i, 0)) for a in args],
        out_specs=pl.BlockSpec((tm, IN_COLS), lambda i: (i, 0)), out_shape=_sds((rows_all, IN_COLS), BF16),
        compiler_params=_params(("parallel",)),
    )(*args)


def _assemble_ctx(dp_all, dck16, dcv16, dcak16, dvc, t_rows, tm):
    c_rows = dck16.shape[0]
    rb = t_rows // tm

    def body(_, dck, dcv, dcak, dcav, o_ref):
        o_ref[:, RK:RK + RH * DK] = dck[...]
        o_ref[:, RV:RV + RH * DV] = dcv[...]
        o_ref[:, AK:AK + HKV * HD] = dcak[...]
        o_ref[:, AV:AV + HKV * HD] = dcav[...].astype(BF16)
        o_ref[:, KV_COLS:] = jnp.zeros((tm, IN_COLS - KV_COLS), BF16)

    args = (dck16, dcv16, dcak16, dvc)
    return pl.pallas_call(
        body, name="assemble_ctx", grid=(c_rows // tm,),
        in_specs=[pl.BlockSpec(memory_space=pl.ANY)]
        + [pl.BlockSpec((tm, a.shape[1]), lambda i: (i, 0)) for a in args],
        out_specs=pl.BlockSpec((tm, IN_COLS), lambda i: (rb + i, 0)), out_shape=_sds(dp_all.shape, BF16),
        input_output_aliases={0: 0},
        compiler_params=_params(("parallel",)),
    )(dp_all, *args)


def _norm_bwd(dh, x2, mod3, norm_w, dxn, row_off, rows_per_group, group0, tm, name):
    with_dx = dxn is not None
    rows = x2.shape[0]
    rb0 = row_off // tm
    bpg = rows_per_group // tm
    ngroups = rows // rows_per_group

    def body(*refs):
        if with_dx:
            dh_ref, x_ref, sc_ref, nw_ref, dxn_ref, dx_ref, dsh_ref, dsc_ref, dnw_ref = refs
        else:
            dh_ref, x_ref, sc_ref, nw_ref, dsh_ref, dsc_ref, dnw_ref = refs
        i = pl.program_id(0)
        dhv = dh_ref[...]
        xv = x_ref[...]
        nw = nw_ref[...]
        r = lax.rsqrt(jnp.mean(xv * xv, axis=-1, keepdims=True) + EPS)
        xh = xv * r
        dm = dhv * (1.0 + sc_ref[...])
        dsh = jnp.sum(dhv, axis=0, keepdims=True)
        dsc = jnp.sum(dhv * (xh * nw), axis=0, keepdims=True)
        dnw = jnp.sum(dm * xh, axis=0, keepdims=True)
        if with_dx:
            dxh = dm * nw
            dx_ref[...] = dxn_ref[...] + r * (dxh - xh * jnp.mean(dxh * xh, axis=-1, keepdims=True))

        @pl.when(i % bpg == 0)
        def _():
            dsh_ref[...] = dsh
            dsc_ref[...] = dsc

        @pl.when(i % bpg != 0)
        def _():
            dsh_ref[...] += dsh
            dsc_ref[...] += dsc

        @pl.when(i == 0)
        def _():
            dnw_ref[...] = dnw

        @pl.when(i > 0)
        def _():
            dnw_ref[...] += dnw

    grp = pl.BlockSpec((None, 1, D), lambda i: (i // bpg, 0, 0))
    in_specs = [pl.BlockSpec((tm, D), lambda i: (rb0 + i, 0)), pl.BlockSpec((tm, D), lambda i: (i, 0)),
                pl.BlockSpec((None, 1, D), lambda i: (group0 + i // bpg, 0, 1)),
                pl.BlockSpec((1, D), lambda i: (0, 0))]
    args = [dh, x2, mod3, norm_w]
    out_specs = [grp, grp, pl.BlockSpec((1, D), lambda i: (0, 0))]
    out_shape = [_sds((ngroups, 1, D), F32), _sds((ngroups, 1, D), F32), _sds((1, D), F32)]
    if with_dx:
        in_specs.append(pl.BlockSpec((tm, D), lambda i: (i, 0)))
        args.append(dxn)
        out_specs.insert(0, pl.BlockSpec((tm, D), lambda i: (i, 0)))
        out_shape.insert(0, _sds((rows, D), F32))
    return pl.pallas_call(
        body, name=name, grid=(rows // tm,), in_specs=in_specs, out_specs=out_specs, out_shape=out_shape,
        compiler_params=_params(("arbitrary",)),
    )(*args)


def _small_final(dmod_all, dmodc_parts, c_rows, dm_loc_rows, nw_parts, misc_parts, c_ctx, r_pad, w_ada16):
    loc = dm_loc_rows.shape[1]

    def body(dm_ref, dmc_ref, c_ref, dml_ref, nwp_ref, mp_ref, cc_ref, r_ref, w_ref,
             gb_ref, gc_ref, gnw_ref, misc_ref, gwa_ref):
        dmc = jnp.sum(dmc_ref[...], axis=0, keepdims=True)
        gb_ref[...] = jnp.sum(dm_ref[...], axis=0, keepdims=True) + dmc
        dsc = _dot(jnp.broadcast_to(dmc, (8, 3 * D)).astype(BF16), w_ref[...], NT)[0:1, :]
        gc_ref[...] = dsc * _dsilu(cc_ref[...])
        gnw_ref[...] = jnp.sum(nwp_ref[...], axis=0, keepdims=True)
        misc = jnp.sum(mp_ref[...], axis=0, keepdims=True)
        y = jnp.exp2(r_ref[...])
        lane = lax.broadcasted_iota(jnp.int32, (1, D), 1)
        is_decay = jnp.logical_and(lane >= 2 * HD, lane < 2 * HD + 2 * RH)
        misc_ref[...] = misc * jnp.where(is_decay, -(y * np.float32(np.log(2.0))) / (1.0 - y), 1.0)
        gwa_ref[...] = _dot(_silu(c_ref[...]).astype(BF16), dml_ref[...].astype(BF16), TN)

    return pl.pallas_call(
        body, name="small_final",
        out_shape=[_sds((1, 3 * D), F32), _sds((1, D), F32), _sds((1, D), F32), _sds((1, D), F32), _sds((D, loc), F32)],
        compiler_params=pltpu.CompilerParams(vmem_limit_bytes=VMEM_LIMIT),
    )(dmod_all, dmodc_parts, c_rows, dm_loc_rows, nw_parts, misc_parts, c_ctx, r_pad, w_ada16)


def _adamw(w, g, m, v, name):
    rows, cols = w.shape
    tm = _pick(rows, 448, 8)
    bc1 = 1.0 - B1 ** STEP
    bc2 = 1.0 - B2 ** STEP

    def body(w_ref, g_ref, m_ref, v_ref, d_ref, nm_ref, nv_ref):
        g_ = g_ref[...]
        nm = B1 * m_ref[...] + (1.0 - B1) * g_
        nv = B2 * v_ref[...] + (1.0 - B2) * (g_ * g_)
        nm_ref[...] = nm
        nv_ref[...] = nv
        d_ref[...] = -LR * ((nm / bc1) / (jnp.sqrt(nv / bc2) + ADAM_EPS) + WD * w_ref[...])

    blk = pl.BlockSpec((tm, cols), lambda i: (i, 0))
    return pl.pallas_call(
        body, name=name, grid=(rows // tm,), in_specs=[blk] * 4, out_specs=[blk] * 3,
        out_shape=[_sds((rows, cols), F32)] * 3, compiler_params=_params(("parallel",)),
    )(w, g, m, v)


def _mesh_pos():
    return lax.axis_index("x"), lax.axis_index("y"), lax.axis_index("c")


def _all_gather(arrs, name):
    n = len(arrs)

    def body(*refs):
        ins, outs = refs[:n], refs[n:2 * n]
        send_sems, recv_sems, local_sems = refs[2 * n:]
        x, y, c = _mesh_pos()
        me, sib = (x, y, c), (x, y, 1 - c)
        chips = [(1 - x, y), (x, 1 - y), (1 - x, 1 - y)]

        def slot(p):
            return 4 * p[0] + 2 * p[1] + p[2]

        def copy(a, k, block, to, own):
            dst = outs[a].at[slot(block)]
            return pltpu.make_async_remote_copy(
                src_ref=ins[a] if own else dst, dst_ref=dst, send_sem=send_sems.at[a, k], recv_sem=recv_sems.at[a, k],
                device_id=to, device_id_type=MESH_T)

        mine = [pltpu.make_async_copy(ins[a], outs[a].at[slot(me)], local_sems.at[a]) for a in range(n)]
        for cp in mine:
            cp.start()
        first = []
        for a in range(n):
            first.append(copy(a, 0, me, sib, True))
            first += [copy(a, 1 + j, me, (*chip, c), True) for j, chip in enumerate(chips)]
        for cp in first:
            cp.start()
        passed = []
        for j, chip in enumerate(chips):
            for a in range(n):
                copy(a, 1 + j, (*chip, c), me, False).wait_recv()
                fwd = copy(a, 4 + j, (*chip, c), sib, False)
                fwd.start()
                passed.append(fwd)
        for a in range(n):
            copy(a, 0, sib, me, False).wait_recv()
            for j, chip in enumerate(chips):
                copy(a, 4 + j, (*chip, 1 - c), me, False).wait_recv()
        for cp in first + passed:
            cp.wait_send()
        for cp in mine:
            cp.wait()

    hbm = pl.BlockSpec(memory_space=pl.ANY)
    return pl.pallas_call(
        body, name=name, in_specs=[hbm] * n, out_specs=[hbm] * n,
        out_shape=[_sds((N_DEV,) + a.shape, a.dtype) for a in arrs],
        scratch_shapes=[pltpu.SemaphoreType.DMA((n, 7)), pltpu.SemaphoreType.DMA((n, 7)), pltpu.SemaphoreType.DMA((n,))],
    )(*arrs)


def _pair_exchange(arrs, name):
    n = len(arrs)

    def body(*refs):
        ins, outs = refs[:n], refs[n:2 * n]
        send_sems, recv_sems = refs[2 * n:]
        x, y, c = _mesh_pos()
        sib = (x, y, 1 - c)
        sends = []
        for a in range(n):
            for k in range(4):
                sends.append(pltpu.make_async_remote_copy(
                    src_ref=ins[a].at[2 * k + 1 - c], dst_ref=outs[a].at[k], send_sem=send_sems.at[a, k],
                    recv_sem=recv_sems.at[a, k], device_id=sib, device_id_type=MESH_T))
        for cp in sends:
            cp.start()
        for cp in sends:
            cp.wait_recv()
        for cp in sends:
            cp.wait_send()

    hbm = pl.BlockSpec(memory_space=pl.ANY)
    return pl.pallas_call(
        body, name=name, in_specs=[hbm] * n, out_specs=[hbm] * n,
        out_shape=[_sds((4,) + a.shape[1:], a.dtype) for a in arrs],
        scratch_shapes=[pltpu.SemaphoreType.DMA((n, 4)), pltpu.SemaphoreType.DMA((n, 4))],
    )(*arrs)


def _pair_add(part, got, core, name):
    _, rows, cols = part.shape
    tm = _pick(rows, 672, 16)
    p4 = part.reshape(4, 2, rows, cols)

    def body(core_ref, p_ref, g_ref, o_ref):
        o_ref[...] = (p_ref[...].astype(F32) + g_ref[...].astype(F32)).astype(BF16)

    blk = pl.BlockSpec((None, tm, cols), lambda k, i, cr: (k, i, 0))
    return pl.pallas_call(
        body, name=name,
        grid_spec=pltpu.PrefetchScalarGridSpec(
            num_scalar_prefetch=1, grid=(4, rows // tm),
            in_specs=[pl.BlockSpec((None, None, tm, cols), lambda k, i, cr: (k, cr[0], i, 0)), blk], out_specs=blk),
        out_shape=_sds((4, rows, cols), BF16), compiler_params=_params(("parallel", "parallel")),
    )(core, p4, got)


def _chip_sum(pair_sums, landed, chip, name):
    _, rows, cols = pair_sums.shape
    tm = _pick(rows, 672, 16)

    def body(chip_ref, s_ref, l_ref, o_ref):
        acc = s_ref[...].astype(F32)
        for j in range(3):
            acc = acc + l_ref[j].astype(F32)
        o_ref[...] = acc

    return pl.pallas_call(
        body, name=name,
        grid_spec=pltpu.PrefetchScalarGridSpec(
            num_scalar_prefetch=1, grid=(rows // tm,),
            in_specs=[pl.BlockSpec((None, tm, cols), lambda i, ch: (ch[0], i, 0)),
                      pl.BlockSpec((3, tm, cols), lambda i, ch: (0, i, 0))],
            out_specs=pl.BlockSpec((tm, cols), lambda i, ch: (i, 0))),
        out_shape=_sds((rows, cols), F32), compiler_params=_params(("parallel",)),
    )(chip, pair_sums, landed)


_HBM = pl.BlockSpec(memory_space=pltpu.HBM)
_SEM = pl.BlockSpec(memory_space=pltpu.SEMAPHORE)
_EFFECT = pltpu.SideEffectType.DATAFLOW_SIDE_EFFECTING


def _chip_routes(n):
    def plan(x, y, c):
        routes = []
        for a in range(n):
            for j in range(1, 4):
                px, py = x ^ (j >> 1), y ^ (j & 1)
                routes.append((a, 2 * px + py, (px, py, c), j - 1))
        return routes
    return plan, 3 * n


def _bcast_routes(n):
    def plan(x, y, c):
        routes = []
        for a in range(n):
            for k in range(1, N_DEV):
                peer = (x ^ ((k >> 2) & 1), y ^ ((k >> 1) & 1), c ^ (k & 1))
                routes.append((a, 0, peer, 4 * x + 2 * y + c))
        return routes
    return plan, 7 * n


def _route_copies(srcs, lands, send_sems, recv_sems, routes):
    return [pltpu.make_async_remote_copy(
        src_ref=srcs[a].at[sb], dst_ref=lands[a].at[lb], send_sem=send_sems.at[r], recv_sem=recv_sems.at[r],
        device_id=peer, device_id_type=MESH_T) for r, (a, sb, peer, lb) in enumerate(routes)]


def _exchange_start(srcs, lands, routes, name, after=()):
    plan, count = routes
    n = len(srcs)
    n_in = 2 * n + len(after)

    def body(*refs):
        send_sems, recv_sems = refs[n_in], refs[n_in + 1]
        token = refs[-1]
        for cp in _route_copies(refs[:n], refs[n:2 * n], send_sems, recv_sems, plan(*_mesh_pos())):
            cp.start()
        token[...] = jnp.zeros_like(token)

    args = [pltpu.with_memory_space_constraint(a, pltpu.HBM) for a in list(srcs) + list(lands)]
    out = pl.pallas_call(
        body, name=name,
        out_shape=(pltpu.SemaphoreType.DMA((count,)), pltpu.SemaphoreType.DMA((count,)),
                   *[pltpu.HBM(a.shape, a.dtype) for a in args], _sds((8, 128), F32)),
        in_specs=[_HBM] * (2 * n) + [pl.BlockSpec(memory_space=pl.ANY)] * len(after),
        out_specs=(_SEM, _SEM, *([_HBM] * (2 * n)), pl.BlockSpec(memory_space=pltpu.VMEM)),
        input_output_aliases={i: 2 + i for i in range(2 * n)},
        compiler_params=pltpu.CompilerParams(has_side_effects=_EFFECT),
    )(*args, *after)
    return (out[0], out[1], list(out[2:2 + 2 * n]), routes), out[-1]


def _exchange_wait_some(state, after, only, name):
    send_sems, recv_sems, bufs, (plan, count) = state
    n = len(bufs) // 2

    def body(*refs):
        send_s, recv_s = refs[2 * n], refs[2 * n + 1]
        for r, cp in enumerate(_route_copies(refs[:n], refs[n:2 * n], send_s, recv_s, plan(*_mesh_pos()))):
            if only is None or r in only:
                cp.wait_send()
                cp.wait_recv()

    out = pl.pallas_call(
        body, name=name, out_shape=tuple(pltpu.HBM(a.shape, a.dtype) for a in bufs),
        in_specs=[_HBM] * (2 * n) + [_SEM, _SEM, pl.BlockSpec(memory_space=pl.ANY)], out_specs=tuple([_HBM] * (2 * n)),
        input_output_aliases={i: i for i in range(2 * n)},
        compiler_params=pltpu.CompilerParams(has_side_effects=_EFFECT),
    )(*bufs, send_sems, recv_sems, after)
    return (send_sems, recv_sems, list(out), (plan, count)), list(out[:n]), list(out[n:])


def _exchange_wait(state, after, name):
    _, srcs, lands = _exchange_wait_some(state, after, None, name)
    return srcs, lands


def _group_routes(js):
    def plan(x, y, c):
        return [(0, 0, (x ^ (j >> 1), y ^ (j & 1), c), 2 * j + c) for j in js]
    return plan, len(js)


def _pair_fill(groups, j, name, after=()):
    def body(*refs):
        g_ref, send_sem, recv_sem = refs[-3:]
        x, y, c = _mesh_pos()
        mine = g_ref.at[2 * j + c]
        to_sib = pltpu.make_async_remote_copy(src_ref=mine, dst_ref=mine, send_sem=send_sem, recv_sem=recv_sem,
                                              device_id=(x, y, 1 - c), device_id_type=MESH_T)
        to_sib.start()
        pltpu.make_async_remote_copy(src_ref=mine, dst_ref=g_ref.at[2 * j + 1 - c], send_sem=send_sem, recv_sem=recv_sem,
                                     device_id=(x, y, 1 - c), device_id_type=MESH_T).wait_recv()
        to_sib.wait_send()

    hbm = pl.BlockSpec(memory_space=pl.ANY)
    return pl.pallas_call(
        body, name=name, in_specs=[hbm] * (1 + len(after)), out_specs=hbm, out_shape=_sds(groups.shape, groups.dtype),
        input_output_aliases={0: 0},
        scratch_shapes=[pltpu.SemaphoreType.DMA, pltpu.SemaphoreType.DMA],
    )(groups, *after)


def _in_proj_group(h_all, groups, j, chip, px_prev, after, name):
    rows_all = h_all.shape[0]
    gcols = IN_COLS // 4
    tm = _pick(rows_all, 1536, 128)
    g4 = groups.reshape(4, gcols, D)

    n_lead = (1 if px_prev is not None else 0) + len(after)
    lead = ([px_prev] if px_prev is not None else []) + list(after)

    def body(chip_ref, *refs):
        h_ref, w_ref, o_ref = refs[n_lead:]
        o_ref[...] = _dot(h_ref[...], w_ref[...], NT).astype(BF16)
    return pl.pallas_call(
        body, name=name,
        grid_spec=pltpu.PrefetchScalarGridSpec(
            num_scalar_prefetch=1, grid=(rows_all // tm,),
            in_specs=[pl.BlockSpec(memory_space=pl.ANY)] * n_lead
            + [pl.BlockSpec((tm, D), lambda i, ch: (i, 0)), pl.BlockSpec((None, gcols, D), lambda i, ch: (j, 0, 0))],
            out_specs=pl.BlockSpec((tm, gcols), lambda i, ch: (i, ch[0] ^ j))),
        out_shape=_sds((rows_all, IN_COLS), BF16),
        input_output_aliases={1: 0} if px_prev is not None else {},
        compiler_params=_params(("parallel",)),
    )(chip, *lead, h_all, g4)


def _d_h_groups(dp_all, groups, chip, after):
    rows_all = dp_all.shape[0]
    gcols = IN_COLS // 4
    tm = _pick(rows_all, 1536, 128)
    g4 = groups.reshape(4, gcols, D)
    n_lead = len(after)

    def body(chip_ref, *refs):
        a_ref, w_ref, o_ref = refs[n_lead:]
        j = pl.program_id(1)
        part = _dot(a_ref[...], w_ref[...])

        @pl.when(j == 0)
        def _():
            o_ref[...] = part

        @pl.when(j > 0)
        def _():
            o_ref[...] += part

    return pl.pallas_call(
        body, name="d_h",
        grid_spec=pltpu.PrefetchScalarGridSpec(
            num_scalar_prefetch=1, grid=(rows_all // tm, 4),
            in_specs=[pl.BlockSpec(memory_space=pl.ANY)] * n_lead
            + [pl.BlockSpec((tm, gcols), lambda i, j, ch: (i, ch[0] ^ j)),
               pl.BlockSpec((None, gcols, D), lambda i, j, ch: (j, 0, 0))],
            out_specs=pl.BlockSpec((tm, D), lambda i, j, ch: (i, 0))),
        out_shape=_sds((rows_all, D), F32),
        compiler_params=_params(("parallel", "arbitrary")),
    )(chip, *after, dp_all, g4)


def _reduce_scatter_start(parts, core, name):
    got = _pair_exchange(parts, name + "_pair")
    sums = [_pair_add(p, g, core, "%s_add_%d" % (name, i)) for i, (p, g) in enumerate(zip(parts, got))]
    lands = [lax.empty((3,) + s_.shape[1:], BF16) for s_ in sums]
    return _exchange_start(sums, lands, _chip_routes(len(sums)), name + "_start")


def _reduce_scatter_finish(rs_state, after, chip, name):
    sums, landed = _exchange_wait(rs_state, after, name + "_wait")
    return [_chip_sum(s_, l_, chip, "%s_sum_%d" % (name, i)) for i, (s_, l_) in enumerate(zip(sums, landed))]


def _local_step(x, c, ctx, norm_w, ret_log2_decay, q_norm_w, k_norm_w, loss_target,
                mod, proj_in, proj_back, get_w_o, on_out_grads, on_in_grad, started=()):
    nb, seq, _ = x.shape
    cx = ctx.shape[1]
    t_rows, c_rows = nb * seq, nb * cx
    rows_all = t_rows + c_rows
    nc = seq // CH
    tm = _pick(seq, 256, 128)
    te = _pick(seq, 512, 128)
    assert cx % tm == 0 and t_rows % cx == 0 and seq % GRID_W == 0

    x2 = x.reshape(t_rows, D)
    ctx2 = ctx.reshape(c_rows, D)
    tgt = loss_target.reshape(t_rows, D)
    lg = _log_gamma(ret_log2_decay)
    cos, sin = _rope_tables(seq)

    mod3 = mod[:, None, :]
    h_all = _norm_fwd(x2, mod3, norm_w, rows_all, 0, seq, 0, None, te, "norm_fwd", after=started)
    h_all = _norm_fwd(ctx2, mod3, norm_w, rows_all, t_rows, c_rows, nb, h_all, tm, "norm_fwd_ctx")
    px = proj_in(h_all)
    s0f, s0b = _ctx_state(px, lg, nb, t_rows, cx)
    o_f, o_b, hist_f, hist_b = _ret_fwd(px, lg, s0f, s0b, nb, nc)
    yret16 = _ret_post(o_f, o_b, px, te)
    q16 = _qk_prep(px, q_norm_w, cos, sin, t_rows, 0, AQ, HQ, 4, seq, te, "q_prep")
    kx16 = _qk_prep(px, k_norm_w, cos, sin, t_rows, 0, AK, HKV, HKV, seq, te, "k_prep")
    kc16 = _qk_prep(px, k_norm_w, None, None, c_rows, t_rows, AK, HKV, HKV, seq, tm, "kc_prep")
    o_att, yatt16, lse = _att_fwd(q16, kx16, kc16, px, nb, seq, cx, tm)
    w_o_ret16, w_o_att16, w_out16 = get_w_o(lse)
    a_ret, a_att, y16 = _merge(yret16, yatt16, px, w_o_ret16, w_o_att16, te)
    dxn, dout16, dgate, loss_b = _outproj(y16, w_out16, x2, tgt, mod3, nb, seq, te)

    gw_out = _matmul(y16, dout16, ta=True, tm=D, tn=D, tk=D, out_dtype=BF16, name="gw_out")
    da_ret16, da_att16, dmr16, dma16 = _bwd_merge(dout16, w_out16, px, a_ret, a_att, te)
    gw_o_ret = _matmul(yret16, da_ret16, ta=True, tm=D, tn=D, tk=D, out_dtype=BF16, name="gw_o_ret")
    gw_o_att = _matmul(yatt16, da_att16, ta=True, tm=D, tn=D, tk=D, out_dtype=BF16, name="gw_o_att")
    out_state, out_started = on_out_grads([gw_o_ret, gw_o_att, gw_out])
    do16, drg16 = _bwd_branch_ret(da_ret16, w_o_ret16, px, o_f, o_b, te, after=out_started)
    dao, dag16 = _bwd_branch_att(da_att16, w_o_att16, px, o_att, te)
    dq_rot, dkx, dvx, dkc, dvc = _att_bwd(q16, kx16, kc16, px, dao, o_att, lse, nb, seq, cx, tm)
    daq16, gq = _qk_prep_bwd(dq_rot, px, q_norm_w, cos, sin, t_rows, 0, AQ, HQ, 4, seq, te, "q_prep_bwd")
    dak16, gk_lat = _qk_prep_bwd(dkx.reshape(t_rows, HKV * HD), px, k_norm_w, cos, sin, t_rows, 0, AK, HKV, HKV, seq, te,
                                 "k_prep_bwd")
    dcak16, gk_ctx = _qk_prep_bwd(dkc.reshape(c_rows, HKV * HD), px, k_norm_w, None, None, c_rows, t_rows, AK, HKV, HKV,
                                  seq, tm, "kc_prep_bwd")
    dq_f, dk_f, dv_f, dq_b, dk_b, dv_b, ds_f, ds_b, dlg_scan = _ret_bwd(px, lg, do16, hist_f, hist_b, nb, nc)
    dck16, dcv16, dlg_ctx = _ctx_state_bwd(px, lg, ds_f, ds_b, nb, t_rows, cx)
    dp_all = _assemble_lat(rows_all, dk_f, dk_b, dv_f, dv_b, dak16, dvx.reshape(t_rows, HKV * HD), dq_f, dq_b, drg16,
                           daq16, dag16, dmr16, dma16, tm)
    dp_all = _assemble_ctx(dp_all, dck16, dcv16, dcak16, dvc.reshape(c_rows, HKV * HD), t_rows, tm)
    gw_in_t = _matmul(dp_all, h_all, ta=True, tm=1536, tn=D, tk=1536, out_dtype=BF16, name="gw_in")
    in_state, in_started = on_in_grad(gw_in_t)
    dh = proj_back(dp_all, in_started)
    grad_x, dsh, dsc, gnw_lat = _norm_bwd(dh, x2, mod3, norm_w, dxn, 0, seq, 0, te, "norm_bwd")
    dsh_c, dsc_c, gnw_ctx = _norm_bwd(dh, ctx2, mod3, norm_w, None, t_rows, c_rows, nb, tm, "norm_bwd_ctx")

    dlg = (jnp.sum(dlg_scan[:, :, 0], axis=0) + jnp.sum(dlg_ctx[:, :, :2, 0], axis=0).T.reshape(2 * RH)).reshape(1, 2 * RH)
    misc = jnp.concatenate([gq, gk_lat + gk_ctx, dlg, jnp.sum(loss_b[:, 0, 0]).reshape(1, 1),
                            jnp.zeros((1, D - 2 * HD - 2 * RH - 1), F32)], axis=1)
    rows = []
    for b in range(nb):
        rows += [dsh[b], dsc[b], dgate[b]]
    rows += [dsh_c[0], dsc_c[0]] + [c[b:b + 1] for b in range(nb)] + [gnw_lat + gnw_ctx, misc]
    payload = jnp.concatenate(rows + [jnp.zeros((PAY_ROWS - len(rows), D), F32)], axis=0)
    return grad_x.reshape(nb, seq, D), out_state, in_state, payload


def _finish_small(gathered, nb, c_ctx, ret_log2_decay, w_ada16, dev):
    n_dev = gathered.shape[0]
    loc = 3 * D // n_dev
    dmod_all = gathered[:, :3 * nb].reshape(n_dev * nb, 3 * D)
    dmodc_parts = jnp.concatenate([gathered[:, 3 * nb:3 * nb + 2].reshape(n_dev, 2 * D), jnp.zeros((n_dev, D), F32)], axis=1)
    c_all = gathered[:, 3 * nb + 2:4 * nb + 2].reshape(n_dev * nb, D)
    nw_parts = gathered[:, 4 * nb + 2]
    misc_parts = gathered[:, 4 * nb + 3]
    n_rows = n_dev * nb + n_dev
    pad = (-n_rows) % 16
    c_rows = jnp.concatenate([c_all, jnp.broadcast_to(c_ctx.reshape(1, D), (n_dev, D)), jnp.zeros((pad, D), F32)], axis=0)
    dm_rows = jnp.concatenate([dmod_all, dmodc_parts, jnp.zeros((pad, 3 * D), F32)], axis=0)
    dm_loc_rows = lax.dynamic_slice_in_dim(dm_rows, dev * loc, loc, axis=1)
    r_pad = jnp.full((1, D), -1.0, F32).at[:, 2 * HD:2 * HD + 2 * RH].set(ret_log2_decay.reshape(1, 2 * RH))
    gb, gc, gnw, misc, gwa = _small_final(dmod_all, dmodc_parts, c_rows, dm_loc_rows, nw_parts, misc_parts,
                                          c_ctx.reshape(1, D), r_pad, w_ada16)
    return (gb, gc, gnw, misc[:, :HD], misc[:, HD:2 * HD], misc[:, 2 * HD:2 * HD + 2 * RH], gwa,
            misc[0, 2 * HD + 2 * RH])


def kernel(x, c, ctx, c_ctx, norm_w, w_ada, b_ada, w_in, ret_log2_decay, q_norm_w, k_norm_w, w_o_ret, w_o_att, w_out, loss_target, m_c_ctx, m_norm_w, m_w_ada, m_b_ada, m_w_in, m_ret_log2_decay, m_q_norm_w, m_k_norm_w, m_w_o_ret, m_w_o_att, m_w_out, v_c_ctx, v_norm_w, v_w_ada, v_b_ada, v_w_in, v_ret_log2_decay, v_q_norm_w, v_k_norm_w, v_w_o_ret, v_w_o_att, v_w_out):
    nb = x.shape[0]
    mx, my, mc = _mesh_pos()
    dev = 4 * mx + 2 * my + mc
    core = jnp.reshape(mc, (1,)).astype(jnp.int32)
    chip = jnp.reshape(2 * mx + my, (1,)).astype(jnp.int32)

    n_loc = 3 * D // N_DEV
    c8 = jnp.zeros((8, D), F32).at[:nb].set(c).at[nb].set(c_ctx)
    (c_all,) = _all_gather([c8], "gather_c")
    ada_shard = w_ada[0].astype(BF16)
    b_loc = lax.dynamic_slice(b_ada, (0, dev * n_loc), (1, n_loc))
    mod_cols = _mod_part(c_all.reshape(N_DEV * 8, D), ada_shard, b_loc)
    (mod_all,) = _all_gather([mod_cols], "gather_mod")
    mod = jnp.transpose(lax.dynamic_slice(mod_all, (0, dev * 8, 0), (N_DEV, 8, n_loc)), (1, 0, 2)).reshape(8, 3 * D)
    ada_land = lax.dynamic_update_slice(lax.empty((N_DEV,) + ada_shard.shape, BF16), ada_shard[None], (dev, 0, 0))

    w_in_t = jnp.transpose(w_in[0])
    in_shard = w_in_t.astype(BF16)
    groups = lax.dynamic_update_slice(lax.empty((N_DEV,) + in_shard.shape, BF16), in_shard[None], (mc, 0, 0))
    groups = _pair_fill(groups, 0, "gather_in_pair", after=(mod_all,))
    (near_send, near_recv, near_bufs, near_routes), gin_token = _exchange_start(
        [in_shard[None]], [groups], _group_routes((1, 2)), "gather_in_start")
    w_in_groups, wo_states, ada_states = [], [], []
    wo_shards = [w_[0].astype(BF16) for w_ in (w_o_ret, w_o_att, w_out)]
    wo_lands = [lax.dynamic_update_slice(lax.empty((N_DEV,) + s_.shape, BF16), s_[None], (dev, 0, 0)) for s_ in wo_shards]

    def proj_in(h_all):
        src, groups = near_bufs
        px = _in_proj_group(h_all, groups, 0, chip, None, (gin_token,), "in_proj_0")
        _, (src,), (groups,) = _exchange_wait_some((near_send, near_recv, [src, groups], near_routes), px, (0,),
                                                   "gather_in_wait_1")
        groups = _pair_fill(groups, 1, "gather_in_fill_1")
        (far_send, far_recv, (src, groups), far_routes), far_token = _exchange_start(
            [src], [groups], _group_routes((3,)), "gather_in_start_far")
        wo_state, wo_token = _exchange_start([s_[None] for s_ in wo_shards], wo_lands, _bcast_routes(3),
                                             "gather_wo_start", after=(far_token,))
        wo_states.append(wo_state)
        ada_state, ada_token = _exchange_start([ada_shard[None]], [ada_land], _bcast_routes(1), "gather_ada_start",
                                               after=(wo_token,))
        ada_states.append(ada_state)
        px = _in_proj_group(h_all, groups, 1, chip, px, (ada_token,), "in_proj_1")
        for j in (2, 3):
            state = ((near_send, near_recv, [src, groups], near_routes) if j < 3 else
                     (far_send, far_recv, [src, groups], far_routes))
            _, (src,), (groups,) = _exchange_wait_some(state, px, (1,) if j < 3 else None, "gather_in_wait_%d" % j)
            groups = _pair_fill(groups, j, "gather_in_fill_%d" % j)
            px = _in_proj_group(h_all, groups, j, chip, px, (), "in_proj_%d" % j)
        w_in_groups.append(groups)
        return px

    def proj_back(dp_all, after):
        return _d_h_groups(dp_all, w_in_groups[0], chip, after)

    def get_w_o(after):
        _, (l_ret, l_att, l_out) = _exchange_wait(wo_states[0], after, "gather_wo_wait")
        return l_ret.reshape(RH * DV, D), l_att.reshape(D, D), l_out.reshape(D, D)

    def on_out_grads(grads):
        parts = [g_.reshape(N_DEV, g_.shape[0] // N_DEV, D) for g_ in grads]
        state, token = _reduce_scatter_start(parts, core, "rs_out")
        return state, (token,)

    def on_in_grad(grad):
        state, token = _reduce_scatter_start([grad.reshape(N_DEV, IN_COLS // N_DEV, D)], core, "rs_in")
        return state, (token,)

    grad_x, out_state, in_state, payload = _local_step(
        x, c, ctx, norm_w, ret_log2_decay, q_norm_w, k_norm_w, loss_target,
        mod, proj_in, proj_back, get_w_o, on_out_grads, on_in_grad, started=(gin_token,))

    (gathered,) = _all_gather([payload], "gather_small")
    _, (l_ada,) = _exchange_wait(ada_states[0], gathered, "gather_ada_wait")
    w_ada16 = jnp.transpose(l_ada, (1, 0, 2)).reshape(D, 3 * D)
    gb, gc, gnw, gq, gk, gr, gwa, loss = _finish_small(gathered, nb, c_ctx, ret_log2_decay, w_ada16, dev)

    g_w_o_ret, g_w_o_att, g_w_out = _reduce_scatter_finish(out_state, gathered, chip, "rs_out")
    (g_w_in_t,) = _reduce_scatter_finish(in_state, gathered, chip, "rs_in")

    grads = [gc.reshape(c_ctx.shape), gnw, gwa[None], gb, g_w_in_t, gr.reshape(ret_log2_decay.shape), gq, gk,
             g_w_o_ret[None], g_w_o_att[None], g_w_out[None]]
    weights = [c_ctx, norm_w, w_ada, b_ada, w_in_t, ret_log2_decay, q_norm_w, k_norm_w, w_o_ret, w_o_att, w_out]
    ms = [m_c_ctx, m_norm_w, m_w_ada, m_b_ada, jnp.transpose(m_w_in[0]), m_ret_log2_decay, m_q_norm_w, m_k_norm_w,
          m_w_o_ret, m_w_o_att, m_w_out]
    vs = [v_c_ctx, v_norm_w, v_w_ada, v_b_ada, jnp.transpose(v_w_in[0]), v_ret_log2_decay, v_q_norm_w, v_k_norm_w,
          v_w_o_ret, v_w_o_att, v_w_out]
    deltas, new_ms, new_vs = [], [], []
    for i, (w, g, m, v) in enumerate(zip(weights, grads, ms, vs)):
        shape2 = (-1, w.shape[-1])
        res = _adamw(w.reshape(shape2), g.reshape(shape2), m.reshape(shape2), v.reshape(shape2), "adamw_%d" % i)
        for lst, r in zip((deltas, new_ms, new_vs), res):
            lst.append(jnp.transpose(r)[None] if i == 4 else r.reshape(w.shape))
    grads[4] = jnp.transpose(g_w_in_t)[None]
    return (loss, grad_x, *grads, *deltas, *new_ms, *new_vs)
```

```python
import numpy as np
import jax
import jax.numpy as jnp
from jax import lax
from jax.experimental import pallas as pl
from jax.experimental.pallas import tpu as pltpu

F32 = jnp.float32
BF16 = jnp.bfloat16

D = 1024
RH, DK, DV, CH = 4, 256, 512, 256
HQ, HKV, HD = 8, 2, 128
GRID_W = 64
ROPE_THETA = 10000.0
EPS = 1e-6
RK, RV, AK, AV, RQ, RG, AQ, AG, MR, MA = 0, 1024, 3072, 3328, 3584, 4608, 6656, 7680, 8704, 9728
IN_COLS = 10752
KV_COLS = 3584
N_DEV = 8
LR, B1, B2, ADAM_EPS, WD, STEP = 0.001, 0.9, 0.999, 1e-08, 0.01, 10
PAY_ROWS = 16
VMEM_LIMIT = 56 * 1024 * 1024
MESH_T = pl.DeviceIdType.MESH

NT = (((1,), (1,)), ((), ()))
TN = (((0,), (0,)), ((), ()))
SM_C = (HD ** -0.5) * float(np.log2(np.e))


def _params(sem):
    return pltpu.CompilerParams(dimension_semantics=sem, vmem_limit_bytes=VMEM_LIMIT)


def _pick(n, target, mult=8):
    best = None
    for t in range(mult, min(n, target) + 1, mult):
        if n % t == 0:
            best = t
    return best or n


def _dot(a, b, dn=None):
    if dn is None:
        return jnp.dot(a, b, preferred_element_type=F32)
    return lax.dot_general(a, b, dn, preferred_element_type=F32)


def _sig(v):
    return jax.nn.sigmoid(v)


def _silu(v):
    return v * _sig(v)


def _dsilu(v):
    s = _sig(v)
    return s * (1.0 + v * (1.0 - s))


def _sds(shape, dtype):
    return jax.ShapeDtypeStruct(shape, dtype)


def _matmul(a, b, *, ta=False, tb=False, tm, tn, tk, out_dtype, name, after=()):
    m = a.shape[1] if ta else a.shape[0]
    kdim = a.shape[0] if ta else a.shape[1]
    n = b.shape[0] if tb else b.shape[1]
    tm, tn, tk = _pick(m, tm, 128), _pick(n, tn, 128), _pick(kdim, tk, 128)
    nk = kdim // tk
    dn = (((0 if ta else 1,), (1 if tb else 0,)), ((), ()))

    def body(a_ref, b_ref, *rest):
        o_ref, acc_ref = rest[-2:]
        k = pl.program_id(2)
        part = _dot(a_ref[...].astype(BF16), b_ref[...].astype(BF16), dn)
        if nk == 1:
            o_ref[...] = part.astype(o_ref.dtype)
        else:
            @pl.when(k == 0)
            def _():
                acc_ref[...] = part

            @pl.when(k > 0)
            def _():
                acc_ref[...] += part

            @pl.when(k == nk - 1)
            def _():
                o_ref[...] = acc_ref[...].astype(o_ref.dtype)

    a_spec = pl.BlockSpec((tk, tm), lambda i, j, k: (k, i)) if ta else pl.BlockSpec((tm, tk), lambda i, j, k: (i, k))
    b_spec = pl.BlockSpec((tn, tk), lambda i, j, k: (j, k)) if tb else pl.BlockSpec((tk, tn), lambda i, j, k: (k, j))
    return pl.pallas_call(
        body, name=name, grid=(m // tm, n // tn, nk),
        in_specs=[a_spec, b_spec] + [pl.BlockSpec(memory_space=pl.ANY)] * len(after),
        out_specs=pl.BlockSpec((tm, tn), lambda i, j, k: (i, j)), out_shape=_sds((m, n), out_dtype),
        scratch_shapes=[pltpu.VMEM((tm, tn) if nk > 1 else (8, 128), F32)],
        compiler_params=_params(("parallel", "parallel", "arbitrary")),
    )(a, b, *after)


def _log_gamma(r):
    rp = jnp.full((8, 128), -1.0, F32).at[:2, :RH].set(r.reshape(2, RH))

    def body(r_ref, o_ref):
        o_ref[...] = jnp.log1p(-jnp.exp2(r_ref[...]))

    out = pl.pallas_call(body, name="log_gamma", out_shape=_sds((8, 128), F32))(rp)
    return out[:2, :RH]


def _mod_part(c_rows, w_ada_loc16, b_loc):
    def body(c_ref, w_ref, b_ref, o_ref):
        o_ref[...] = _dot(_silu(c_ref[...]).astype(BF16), w_ref[...]) + b_ref[...]

    return pl.pallas_call(
        body, name="mod_part", out_shape=_sds((c_rows.shape[0], w_ada_loc16.shape[1]), F32),
    )(c_rows, w_ada_loc16, b_loc)


def _norm_fwd(x2, mod3, norm_w, rows_all, row_off, rows_per_group, group0, h_prev, tm, name, after=()):
    rows = x2.shape[0]
    rb0 = row_off // tm
    bpg = rows_per_group // tm

    def body(*refs):
        x_ref, sh_ref, sc_ref, nw_ref, o_ref = refs[-5:]
        xv = x_ref[...]
        r = lax.rsqrt(jnp.mean(xv * xv, axis=-1, keepdims=True) + EPS)
        o_ref[...] = ((xv * r) * nw_ref[...] * (1.0 + sc_ref[...]) + sh_ref[...]).astype(BF16)

    in_specs = [pl.BlockSpec((tm, D), lambda i: (i, 0)),
                pl.BlockSpec((None, 1, D), lambda i: (group0 + i // bpg, 0, 0)),
                pl.BlockSpec((None, 1, D), lambda i: (group0 + i // bpg, 0, 1)),
                pl.BlockSpec((1, D), lambda i: (0, 0))]
    in_specs = [pl.BlockSpec(memory_space=pl.ANY)] * len(after) + in_specs
    args = list(after) + [x2, mod3, mod3, norm_w]
    alias = {}
    if h_prev is not None:
        in_specs.insert(0, pl.BlockSpec(memory_space=pl.ANY))
        args.insert(0, h_prev)
        alias = {0: 0}
    return pl.pallas_call(
        body, name=name, grid=(rows // tm,), in_specs=in_specs,
        out_specs=pl.BlockSpec((tm, D), lambda i: (rb0 + i, 0)), out_shape=_sds((rows_all, D), BF16),
        input_output_aliases=alias, compiler_params=_params(("parallel",)),
    )(*args)


def _decays(lg, fwd):
    ii = lax.broadcasted_iota(jnp.int32, (CH, CH), 0)
    jj = lax.broadcasted_iota(jnp.int32, (CH, CH), 1)
    ri = lax.broadcasted_iota(jnp.int32, (CH, 1), 0).astype(F32)
    rel = (ii - jj) if fwd else (jj - ii)
    relf = jnp.maximum(rel, 0).astype(F32)
    mask = jnp.where(rel >= 0, jnp.exp(lg * relf), 0.0)
    qe = (ri + 1.0) if fwd else (CH - ri)
    ke = (CH - 1.0 - ri) if fwd else ri
    return mask, relf, jnp.exp(lg * qe), qe, jnp.exp(lg * ke), ke


def _wide_specs(rowf):
    return [pl.BlockSpec((CH, 2 * DK), lambda b, c: (rowf(b, c), RQ // (2 * DK))),
            pl.BlockSpec((CH, 2 * DK), lambda b, c: (rowf(b, c), RQ // (2 * DK) + 1)),
            pl.BlockSpec((CH, RH * DK), lambda b, c: (rowf(b, c), RK // (RH * DK))),
            pl.BlockSpec((CH, 2 * DV), lambda b, c: (rowf(b, c), RV // (2 * DV))),
            pl.BlockSpec((CH, 2 * DV), lambda b, c: (rowf(b, c), RV // (2 * DV) + 1))]


def _head_qkv(refs, h):
    q0, q1, k, v0, v1 = refs
    lo = h % 2
    q = (q0, q1)[h // 2][:, lo * DK:(lo + 1) * DK].astype(F32)
    kk = k[:, h * DK:(h + 1) * DK].astype(F32) * (DK ** -0.5)
    v16 = (v0, v1)[h // 2][:, lo * DV:(lo + 1) * DV].astype(BF16)
    return q, kk, v16


def _ctx_state(px, lg, nb, t_rows, cx):
    rb = t_rows // cx

    def body(lg_ref, k_ref, v_ref, sf_ref, sb_ref):
        h = pl.program_id(1)
        pos = lax.broadcasted_iota(jnp.int32, (cx, 1), 0).astype(F32)
        k = k_ref[...].astype(F32) * (DK ** -0.5)
        v16 = v_ref[...].astype(BF16)
        wf = jnp.exp(lg_ref[0, h] * (cx - 1.0 - pos))
        wb = jnp.exp(lg_ref[1, h] * pos)
        sf_ref[...] = _dot((k * wf).astype(BF16), v16, TN)
        sb_ref[...] = _dot((k * wb).astype(BF16), v16, TN)

    st = pl.BlockSpec((None, None, DK, DV), lambda b, h: (b, h, 0, 0))
    return pl.pallas_call(
        body, name="ctx_state", grid=(nb, RH),
        in_specs=[pl.BlockSpec(memory_space=pltpu.SMEM),
                  pl.BlockSpec((cx, DK), lambda b, h: (rb + b, RK // DK + h)),
                  pl.BlockSpec((cx, DV), lambda b, h: (rb + b, RV // DV + h))],
        out_specs=[st, st], out_shape=[_sds((nb, RH, DK, DV), F32)] * 2,
        compiler_params=_params(("parallel", "parallel")),
    )(lg, px, px)


def _ret_fwd(px, lg, s0f, s0b, nb, nc):
    t_rows = nb * nc * CH

    def body(lg_ref, *refs):
        ins = (refs[0:5], refs[5:10])
        s0f_ref, s0b_ref, of_ref, ob_ref, hf_ref, hb_ref, sf, sb = refs[10:]
        c = pl.program_id(1)

        @pl.when(c == 0)
        def _():
            sf[...] = s0f_ref[...]
            sb[...] = s0b_ref[...]

        for d, (o_ref, h_ref, s) in enumerate(((of_ref, hf_ref, sf), (ob_ref, hb_ref, sb))):
            for h in range(RH):
                lg_d = lg_ref[d, h]
                mask, _, qd, _, kd, _ = _decays(lg_d, d == 0)
                q, k, v16 = _head_qkv(ins[d], h)
                a = _dot(q.astype(BF16), k.astype(BF16), NT)
                st = s[h]
                st16 = st.astype(BF16)
                h_ref[h] = st16
                o = _dot((a * mask).astype(BF16), v16) + _dot((q * qd).astype(BF16), st16)
                o_ref[:, h * DV:(h + 1) * DV] = o.astype(BF16)
                s[h] = st * jnp.exp(lg_d * CH) + _dot((k * kd).astype(BF16), v16, TN)

    def fw(b, c):
        return b * nc + c

    def bw(b, c):
        return b * nc + nc - 1 - c

    st = pl.BlockSpec((None, RH, DK, DV), lambda b, c: (b, 0, 0, 0))
    in_specs = [pl.BlockSpec(memory_space=pltpu.SMEM)] + _wide_specs(fw) + _wide_specs(bw) + [st, st]
    out_specs = [pl.BlockSpec((CH, RH * DV), lambda b, c: (fw(b, c), 0)),
                 pl.BlockSpec((CH, RH * DV), lambda b, c: (bw(b, c), 0)),
                 pl.BlockSpec((None, None, RH, DK, DV), lambda b, c: (b, c, 0, 0, 0)),
                 pl.BlockSpec((None, None, RH, DK, DV), lambda b, c: (b, nc - 1 - c, 0, 0, 0))]
    return pl.pallas_call(
        body, name="ret_fwd", grid=(nb, nc), in_specs=in_specs, out_specs=out_specs,
        out_shape=[_sds((t_rows, RH * DV), BF16)] * 2 + [_sds((nb, nc, RH, DK, DV), BF16)] * 2,
        scratch_shapes=[pltpu.VMEM((RH, DK, DV), F32), pltpu.VMEM((RH, DK, DV), F32)],
        compiler_params=_params(("parallel", "arbitrary")),
    )(lg, *([px] * 10), s0f, s0b)


def _ret_post(o_f, o_b, px, tm):
    t_rows = o_f.shape[0]

    def body(of_ref, ob_ref, g0, g1, g2, g3, y_ref):
        for h, g_ref in enumerate((g0, g1, g2, g3)):
            sl = slice(h * DV, (h + 1) * DV)
            o = of_ref[:, sl].astype(F32) + ob_ref[:, sl].astype(F32)
            r = lax.rsqrt(jnp.mean(o * o, axis=-1, keepdims=True) + EPS)
            y_ref[:, sl] = ((o * r) * _silu(g_ref[...].astype(F32))).astype(BF16)

    def gate(h):
        return pl.BlockSpec((tm, DV), lambda i: (i, RG // DV + h))

    wide = pl.BlockSpec((tm, RH * DV), lambda i: (i, 0))
    return pl.pallas_call(
        body, name="ret_post", grid=(t_rows // tm,),
        in_specs=[wide, wide] + [gate(h) for h in range(RH)],
        out_specs=wide, out_shape=_sds((t_rows, RH * DV), BF16),
        compiler_params=_params(("parallel",)),
    )(o_f, o_b, *([px] * RH))


def _rope_tables(seq):
    rows = seq // GRID_W
    row = np.repeat(np.arange(rows, dtype=np.float32), GRID_W)
    col = np.tile(np.arange(GRID_W, dtype=np.float32), rows)
    half = HD // 2
    freqs = (ROPE_THETA ** (-np.arange(0, half, 2, dtype=np.float32) / half)).astype(np.float32)
    ang = np.concatenate([row[:, None] * freqs, col[:, None] * freqs], axis=-1).astype(np.float32)
    cos = np.repeat(np.cos(ang), 2, axis=-1).astype(np.float32)
    sin = np.repeat(np.sin(ang), 2, axis=-1).astype(np.float32)
    sign = np.tile(np.array([-1.0, 1.0], np.float32), HD // 2)
    return jnp.asarray(cos), jnp.asarray(sin * sign)


def _swap_pairs(v):
    lane = lax.broadcasted_iota(jnp.int32, v.shape, 1)
    return jnp.where((lane & 1) == 0, pltpu.roll(v, HD - 1, 1), pltpu.roll(v, 1, 1))


def _qk_prep(px, nw, cos, sin, rows, row_off, col_off, heads, hb, seq, tm, name):
    rope = cos is not None
    rb0 = row_off // tm
    pb = seq // tm if rope else 1
    bw = hb * HD

    def body(*refs):
        if rope:
            x_ref, w_ref, c_ref, s_ref, o_ref = refs
        else:
            x_ref, w_ref, o_ref = refs
        for h in range(hb):
            sl = slice(h * HD, (h + 1) * HD)
            xv = x_ref[:, sl].astype(F32)
            r = lax.rsqrt(jnp.mean(xv * xv, axis=-1, keepdims=True) + EPS)
            t = (xv * r) * w_ref[...]
            if rope:
                t = t * c_ref[...] + _swap_pairs(t) * s_ref[...]
            o_ref[:, sl] = t.astype(BF16)

    in_specs = [pl.BlockSpec((tm, bw), lambda i, j: (rb0 + i, col_off // bw + j)),
                pl.BlockSpec((1, HD), lambda i, j: (0, 0))]
    args = [px, nw]
    if rope:
        in_specs += [pl.BlockSpec((tm, HD), lambda i, j: (i % pb, 0))] * 2
        args += [cos, sin]
    return pl.pallas_call(
        body, name=name, grid=(rows // tm, heads // hb), in_specs=in_specs,
        out_specs=pl.BlockSpec((tm, bw), lambda i, j: (i, j)), out_shape=_sds((rows, heads * HD), BF16),
        compiler_params=_params(("parallel", "parallel")),
    )(*args)


def _att_fwd(q16, kx16, kc16, px, nb, seq, cx, tq):
    t_rows = nb * seq
    nq = seq // tq
    rep = HQ // HKV
    gw = rep * HD

    def body(q_ref, kx_ref, kc_ref, vx_ref, vc_ref, g_ref, o_ref, y_ref, l_ref):
        kx = kx_ref[...]
        kc = kc_ref[...]
        vx = vx_ref[...].astype(BF16)
        vc = vc_ref[...].astype(BF16)
        l_ref[...] = jnp.zeros_like(l_ref)
        for r in range(rep):
            sl = slice(r * HD, (r + 1) * HD)
            q = q_ref[:, sl]
            s1 = _dot(q, kx, NT)
            s2 = _dot(q, kc, NT)
            m = jnp.maximum(jnp.max(s1, axis=-1, keepdims=True), jnp.max(s2, axis=-1, keepdims=True))
            e1 = jnp.exp2((s1 - m) * SM_C)
            e2 = jnp.exp2((s2 - m) * SM_C)
            tot = jnp.sum(e1, axis=-1, keepdims=True) + jnp.sum(e2, axis=-1, keepdims=True)
            o = (_dot(e1.astype(BF16), vx) + _dot(e2.astype(BF16), vc)) * (1.0 / tot)
            o_ref[:, sl] = o
            y_ref[:, sl] = (o * _silu(g_ref[:, sl].astype(F32))).astype(BF16)
            l_ref[:, r:r + 1] = m * SM_C + jnp.log(tot) * float(np.log2(np.e))

    qblk = pl.BlockSpec((tq, gw), lambda b, g, i: (b * nq + i, g))
    return pl.pallas_call(
        body, name="att_fwd", grid=(nb, HKV, nq),
        in_specs=[qblk,
                  pl.BlockSpec((seq, HD), lambda b, g, i: (b, g)),
                  pl.BlockSpec((cx, HD), lambda b, g, i: (b, g)),
                  pl.BlockSpec((seq, HD), lambda b, g, i: (b, AV // HD + g)),
                  pl.BlockSpec((cx, HD), lambda b, g, i: (t_rows // cx + b, AV // HD + g)),
                  pl.BlockSpec((tq, gw), lambda b, g, i: (b * nq + i, AG // gw + g))],
        out_specs=[qblk, qblk, pl.BlockSpec((tq, 128), lambda b, g, i: (b * nq + i, g))],
        out_shape=[_sds((t_rows, D), F32), _sds((t_rows, D), BF16), _sds((t_rows, HKV * 128), F32)],
        compiler_params=_params(("parallel", "parallel", "parallel")),
    )(q16, kx16, kc16, px, px, px)


def _gate_specs(tm, col0):
    hw = D // 2
    return [pl.BlockSpec((tm, hw), lambda i: (i, col0 // hw)), pl.BlockSpec((tm, hw), lambda i: (i, col0 // hw + 1))]


def _merge(yret16, yatt16, px, w_o_ret16, w_o_att16, tm):
    t_rows = yret16.shape[0]
    hw = D // 2

    def body(yr_ref, wr_ref, ya_ref, wa_ref, mr0, mr1, ma0, ma1, ar_ref, aa_ref, y_ref):
        ar = _dot(yr_ref[...], wr_ref[...])
        aa = _dot(ya_ref[...], wa_ref[...])
        ar_ref[...] = ar.astype(BF16)
        aa_ref[...] = aa.astype(BF16)
        for j, (mr_ref, ma_ref) in enumerate(((mr0, ma0), (mr1, ma1))):
            sl = slice(j * hw, (j + 1) * hw)
            y_ref[:, sl] = (_sig(mr_ref[...].astype(F32)) * ar[:, sl]
                            + _sig(ma_ref[...].astype(F32)) * aa[:, sl]).astype(BF16)

    row = pl.BlockSpec((tm, D), lambda i: (i, 0))
    return pl.pallas_call(
        body, name="merge", grid=(t_rows // tm,),
        in_specs=[pl.BlockSpec((tm, RH * DV), lambda i: (i, 0)), pl.BlockSpec((RH * DV, D), lambda i: (0, 0)),
                  row, pl.BlockSpec((D, D), lambda i: (0, 0))] + _gate_specs(tm, MR) + _gate_specs(tm, MA),
        out_specs=[row, row, row], out_shape=[_sds((t_rows, D), BF16)] * 3,
        compiler_params=_params(("parallel",)),
    )(yret16, w_o_ret16, yatt16, w_o_att16, px, px, px, px)


def _outproj(y16, w_out16, x2, tgt, mod3, nb, seq, tm):
    t_rows = nb * seq
    bpb = seq // tm

    def body(y_ref, w_ref, x_ref, t_ref, g_ref, dxn_ref, dout_ref, dg_ref, loss_ref):
        i = pl.program_id(1)
        out = _dot(y_ref[...], w_ref[...])
        gate = g_ref[...]
        diff = x_ref[...] + gate * out - t_ref[...]
        dxn = diff * (1.0 / D)
        dxn_ref[...] = dxn
        dout_ref[...] = (gate * dxn).astype(BF16)
        dg = jnp.sum(dxn * out, axis=0, keepdims=True)
        ls = jnp.broadcast_to(jnp.sum(diff * diff) * (0.5 / D), (1, 128))

        @pl.when(i == 0)
        def _():
            dg_ref[...] = dg
            loss_ref[...] = ls

        @pl.when(i > 0)
        def _():
            dg_ref[...] += dg
            loss_ref[...] += ls

    row = pl.BlockSpec((tm, D), lambda b, i: (b * bpb + i, 0))
    return pl.pallas_call(
        body, name="outproj", grid=(nb, bpb),
        in_specs=[row, pl.BlockSpec((D, D), lambda b, i: (0, 0)), row, row,
                  pl.BlockSpec((None, 1, D), lambda b, i: (b, 0, 2))],
        out_specs=[row, row, pl.BlockSpec((None, 1, D), lambda b, i: (b, 0, 0)),
                   pl.BlockSpec((None, 1, 128), lambda b, i: (b, 0, 0))],
        out_shape=[_sds((t_rows, D), F32), _sds((t_rows, D), BF16), _sds((nb, 1, D), F32), _sds((nb, 1, 128), F32)],
        compiler_params=_params(("parallel", "arbitrary")),
    )(y16, w_out16, x2, tgt, mod3)


def _bwd_merge(dout16, w_out16, px, a_ret, a_att, tm):
    t_rows = dout16.shape[0]
    hw = D // 2

    def body(do_ref, w_ref, mr0, mr1, ma0, ma1, ar_ref, aa_ref, dar_ref, daa_ref, dmr_ref, dma_ref):
        dy_all = _dot(do_ref[...], w_ref[...], NT)
        for j, (mr_ref, ma_ref) in enumerate(((mr0, ma0), (mr1, ma1))):
            sl = slice(j * hw, (j + 1) * hw)
            dy = dy_all[:, sl]
            sr = _sig(mr_ref[...].astype(F32))
            sa = _sig(ma_ref[...].astype(F32))
            dar_ref[:, sl] = (dy * sr).astype(BF16)
            daa_ref[:, sl] = (dy * sa).astype(BF16)
            dmr_ref[:, sl] = (dy * ar_ref[:, sl].astype(F32) * sr * (1.0 - sr)).astype(BF16)
            dma_ref[:, sl] = (dy * aa_ref[:, sl].astype(F32) * sa * (1.0 - sa)).astype(BF16)

    row = pl.BlockSpec((tm, D), lambda i: (i, 0))
    return pl.pallas_call(
        body, name="bwd_merge", grid=(t_rows // tm,),
        in_specs=[row, pl.BlockSpec((D, D), lambda i: (0, 0))] + _gate_specs(tm, MR) + _gate_specs(tm, MA) + [row, row],
        out_specs=[row] * 4, out_shape=[_sds((t_rows, D), BF16)] * 4,
        compiler_params=_params(("parallel",)),
    )(dout16, w_out16, px, px, px, px, a_ret, a_att)


def _bwd_branch_ret(da_ret16, w_o_ret16, px, o_f, o_b, tm, after=()):
    t_rows = da_ret16.shape[0]

    def body(da_ref, w_ref, g0, g1, g2, g3, of_ref, ob_ref, *rest):
        do_ref, dg_ref = rest[-2:]
        da = da_ref[...]
        for h, g_ref in enumerate((g0, g1, g2, g3)):
            sl = slice(h * DV, (h + 1) * DV)
            dy = _dot(da, w_ref[sl, :], NT)
            g = g_ref[...].astype(F32)
            o = of_ref[:, sl].astype(F32) + ob_ref[:, sl].astype(F32)
            r = lax.rsqrt(jnp.mean(o * o, axis=-1, keepdims=True) + EPS)
            on = o * r
            don = dy * _silu(g)
            dg_ref[:, sl] = (dy * on * _dsilu(g)).astype(BF16)
            do_ref[:, sl] = (r * (don - on * jnp.mean(on * don, axis=-1, keepdims=True))).astype(BF16)

    def gate(h):
        return pl.BlockSpec((tm, DV), lambda i: (i, RG // DV + h))

    wide = pl.BlockSpec((tm, RH * DV), lambda i: (i, 0))
    return pl.pallas_call(
        body, name="bwd_branch_ret", grid=(t_rows // tm,),
        in_specs=[pl.BlockSpec((tm, D), lambda i: (i, 0)), pl.BlockSpec((RH * DV, D), lambda i: (0, 0))]
        + [gate(h) for h in range(RH)] + [wide, wide] + [pl.BlockSpec(memory_space=pl.ANY)] * len(after),
        out_specs=[wide, wide], out_shape=[_sds((t_rows, RH * DV), BF16)] * 2,
        compiler_params=_params(("parallel",)),
    )(da_ret16, w_o_ret16, *([px] * RH), o_f, o_b, *after)


def _bwd_branch_att(da_att16, w_o_att16, px, o_att, tm):
    t_rows = da_att16.shape[0]
    hw = D // 2

    def body(da_ref, w_ref, g0, g1, o_ref, dao_ref, dg_ref):
        dy_all = _dot(da_ref[...], w_ref[...], NT)
        for j, g_ref in enumerate((g0, g1)):
            sl = slice(j * hw, (j + 1) * hw)
            dy = dy_all[:, sl]
            g = g_ref[...].astype(F32)
            dao_ref[:, sl] = dy * _silu(g)
            dg_ref[:, sl] = (dy * o_ref[:, sl] * _dsilu(g)).astype(BF16)

    row = pl.BlockSpec((tm, D), lambda i: (i, 0))
    return pl.pallas_call(
        body, name="bwd_branch_att", grid=(t_rows // tm,),
        in_specs=[row, pl.BlockSpec((D, D), lambda i: (0, 0))] + _gate_specs(tm, AG) + [row],
        out_specs=[row, row], out_shape=[_sds((t_rows, D), F32), _sds((t_rows, D), BF16)],
        compiler_params=_params(("parallel",)),
    )(da_att16, w_o_att16, px, px, o_att)


def _att_bwd(q16, kx16, kc16, px, dao, o_att, lse, nb, seq, cx, tq):
    t_rows = nb * seq
    nq = seq // tq
    rep = HQ // HKV
    gw = rep * HD
    scale = HD ** -0.5

    def body(q_ref, kx_ref, kc_ref, vx_ref, vc_ref, dao_ref, o_ref, l_ref, dq_ref, dkx_ref, dvx_ref, dkc_ref, dvc_ref):
        i = pl.program_id(2)
        kx = kx_ref[...]
        kc = kc_ref[...]
        vx = vx_ref[...].astype(BF16)
        vc = vc_ref[...].astype(BF16)
        dkx = jnp.zeros((seq, HD), F32)
        dvx = jnp.zeros((seq, HD), F32)
        dkc = jnp.zeros((cx, HD), F32)
        dvc = jnp.zeros((cx, HD), F32)
        for r in range(rep):
            sl = slice(r * HD, (r + 1) * HD)
            q = q_ref[:, sl]
            lr = l_ref[:, r:r + 1]
            p1 = jnp.exp2(_dot(q, kx, NT) * SM_C - lr)
            p2 = jnp.exp2(_dot(q, kc, NT) * SM_C - lr)
            da = dao_ref[:, sl]
            da16 = da.astype(BF16)
            delta = jnp.sum(da * o_ref[:, sl], axis=-1, keepdims=True)
            ds1 = (p1 * (_dot(da16, vx, NT) - delta)).astype(BF16)
            ds2 = (p2 * (_dot(da16, vc, NT) - delta)).astype(BF16)
            dq_ref[:, sl] = (_dot(ds1, kx) + _dot(ds2, kc)) * scale
            dkx += _dot(ds1, q, TN)
            dkc += _dot(ds2, q, TN)
            dvx += _dot(p1.astype(BF16), da16, TN)
            dvc += _dot(p2.astype(BF16), da16, TN)
        dkx = dkx * scale
        dkc = dkc * scale

        @pl.when(i == 0)
        def _():
            dkx_ref[...] = dkx
            dvx_ref[...] = dvx
            dkc_ref[...] = dkc
            dvc_ref[...] = dvc

        @pl.when(i > 0)
        def _():
            dkx_ref[...] += dkx
            dvx_ref[...] += dvx
            dkc_ref[...] += dkc
            dvc_ref[...] += dvc

    qblk = pl.BlockSpec((tq, gw), lambda b, g, i: (b * nq + i, g))
    kxb = pl.BlockSpec((None, seq, HD), lambda b, g, i: (b, 0, g))
    kcb = pl.BlockSpec((None, cx, HD), lambda b, g, i: (b, 0, g))
    return pl.pallas_call(
        body, name="att_bwd", grid=(nb, HKV, nq),
        in_specs=[qblk,
                  pl.BlockSpec((seq, HD), lambda b, g, i: (b, g)),
                  pl.BlockSpec((cx, HD), lambda b, g, i: (b, g)),
                  pl.BlockSpec((seq, HD), lambda b, g, i: (b, AV // HD + g)),
                  pl.BlockSpec((cx, HD), lambda b, g, i: (t_rows // cx + b, AV // HD + g)),
                  qblk, qblk, pl.BlockSpec((tq, 128), lambda b, g, i: (b * nq + i, g))],
        out_specs=[qblk, kxb, kxb, kcb, kcb],
        out_shape=[_sds((t_rows, D), F32), _sds((nb, seq, HKV * HD), F32), _sds((nb, seq, HKV * HD), F32),
                   _sds((nb, cx, HKV * HD), F32), _sds((nb, cx, HKV * HD), F32)],
        compiler_params=_params(("parallel", "parallel", "arbitrary")),
    )(q16, kx16, kc16, px, px, dao, o_att, lse)


def _qk_prep_bwd(dt, px, nw, cos, sin, rows, row_off, col_off, heads, hb, seq, tm, name):
    rope = cos is not None
    rb0 = row_off // tm
    pb = seq // tm if rope else 1
    bw = hb * HD

    def body(*refs):
        if rope:
            d_ref, x_ref, w_ref, c_ref, s_ref, dx_ref, dw_ref = refs
        else:
            d_ref, x_ref, w_ref, dx_ref, dw_ref = refs
        first = jnp.logical_and(pl.program_id(0) == 0, pl.program_id(1) == 0)
        dw = jnp.zeros((1, HD), F32)
        for h in range(hb):
            sl = slice(h * HD, (h + 1) * HD)
            dtv = d_ref[:, sl]
            if rope:
                dtv = dtv * c_ref[...] + _swap_pairs(dtv * s_ref[...])
            xv = x_ref[:, sl].astype(F32)
            r = lax.rsqrt(jnp.mean(xv * xv, axis=-1, keepdims=True) + EPS)
            xh = xv * r
            dxh = dtv * w_ref[...]
            dx_ref[:, sl] = (r * (dxh - xh * jnp.mean(dxh * xh, axis=-1, keepdims=True))).astype(BF16)
            dw += jnp.sum(dtv * xh, axis=0, keepdims=True)

        @pl.when(first)
        def _():
            dw_ref[...] = dw

        @pl.when(jnp.logical_not(first))
        def _():
            dw_ref[...] += dw

    blk = pl.BlockSpec((tm, bw), lambda i, j: (i, j))
    in_specs = [blk, pl.BlockSpec((tm, bw), lambda i, j: (rb0 + i, col_off // bw + j)),
                pl.BlockSpec((1, HD), lambda i, j: (0, 0))]
    args = [dt, px, nw]
    if rope:
        in_specs += [pl.BlockSpec((tm, HD), lambda i, j: (i % pb, 0))] * 2
        args += [cos, sin]
    return pl.pallas_call(
        body, name=name, grid=(rows // tm, heads // hb), in_specs=in_specs,
        out_specs=[blk, pl.BlockSpec((1, HD), lambda i, j: (0, 0))],
        out_shape=[_sds((rows, heads * HD), BF16), _sds((1, HD), F32)],
        compiler_params=_params(("arbitrary", "arbitrary")),
    )(*args)


def _ret_bwd(px, lg, do16, hist_f, hist_b, nb, nc):
    t_rows = nb * nc * CH

    def body(lg_ref, *refs):
        ins = (refs[0:5], refs[7:12])
        do_refs = (refs[5], refs[12])
        h_refs = (refs[6], refs[13])
        outs = (refs[14:17], refs[17:20])
        ds_outs = (refs[20], refs[21])
        dlg_ref = refs[22]
        dss = (refs[23], refs[24])
        c = pl.program_id(1)

        @pl.when(c == 0)
        def _():
            dss[0][...] = jnp.zeros_like(dss[0])
            dss[1][...] = jnp.zeros_like(dss[1])
            dlg_ref[...] = jnp.zeros_like(dlg_ref)

        for d in range(2):
            dq_ref, dk_ref, dv_ref = outs[d]
            for h in range(RH):
                lg_d = lg_ref[d, h]
                mask, relf, qd, qe, kd, ke = _decays(lg_d, d == 0)
                g_ch = jnp.exp(lg_d * CH)
                q, k, v16 = _head_qkv(ins[d], h)
                q16 = q.astype(BF16)
                k16 = k.astype(BF16)
                do16v = do_refs[d][:, h * DV:(h + 1) * DV]
                st16 = h_refs[d][h]
                dst = dss[d][h]
                dst16 = dst.astype(BF16)
                a = _dot(q16, k16, NT) * mask
                dp = _dot(do16v, v16, NT)
                da16 = (dp * mask).astype(BF16)
                dq_cross = _dot(do16v, st16, NT) * qd
                dq_ref[:, h * DK:(h + 1) * DK] = (_dot(da16, k16) + dq_cross).astype(BF16)
                dk_state = _dot(v16, dst16, NT) * kd
                dk_ref[:, h * DK:(h + 1) * DK] = ((_dot(da16, q16, TN) + dk_state) * (DK ** -0.5)).astype(BF16)
                dv = _dot(a.astype(BF16), do16v, TN) + _dot((k * kd).astype(BF16), dst16)
                dv_ref[:, h * DV:(h + 1) * DV] = dv.astype(BF16)
                dlg = (jnp.sum(relf * a * dp)
                       + jnp.sum(qe * jnp.sum(q * dq_cross, axis=-1, keepdims=True))
                       + jnp.sum(ke * jnp.sum(k * dk_state, axis=-1, keepdims=True))
                       + CH * g_ch * jnp.sum(dst * st16.astype(F32)))
                row = d * RH + h
                dlg_ref[row:row + 1, :] += jnp.broadcast_to(dlg, (1, 128))
                ds_new = g_ch * dst + _dot((q * qd).astype(BF16), do16v, TN)
                dss[d][h] = ds_new

                @pl.when(c == nc - 1)
                def _():
                    ds_outs[d][h] = ds_new

    def fw(b, c):
        return b * nc + nc - 1 - c

    def bw(b, c):
        return b * nc + c

    def rows(rowf, width):
        return pl.BlockSpec((CH, width), lambda b, c: (rowf(b, c), 0))

    def hist(rowf):
        return pl.BlockSpec((None, None, RH, DK, DV), lambda b, c: (b, rowf(0, c), 0, 0, 0))

    st = pl.BlockSpec((None, RH, DK, DV), lambda b, c: (b, 0, 0, 0))
    in_specs = [pl.BlockSpec(memory_space=pltpu.SMEM)]
    out_specs = []
    for rowf in (fw, bw):
        in_specs += _wide_specs(rowf) + [rows(rowf, RH * DV), hist(rowf)]
        out_specs += [rows(rowf, RH * DK), rows(rowf, RH * DK), rows(rowf, RH * DV)]
    out_specs += [st, st, pl.BlockSpec((None, 8, 128), lambda b, c: (b, 0, 0))]
    qk = _sds((t_rows, RH * DK), BF16)
    vv = _sds((t_rows, RH * DV), BF16)
    return pl.pallas_call(
        body, name="ret_bwd", grid=(nb, nc), in_specs=in_specs, out_specs=out_specs,
        out_shape=[qk, qk, vv, qk, qk, vv, _sds((nb, RH, DK, DV), F32), _sds((nb, RH, DK, DV), F32),
                   _sds((nb, 8, 128), F32)],
        scratch_shapes=[pltpu.VMEM((RH, DK, DV), F32), pltpu.VMEM((RH, DK, DV), F32)],
        compiler_params=_params(("parallel", "arbitrary")),
    )(lg, *([px] * 5), do16, hist_f, *([px] * 5), do16, hist_b)


def _ctx_state_bwd(px, lg, ds_f, ds_b, nb, t_rows, cx):
    rb = t_rows // cx

    def body(lg_ref, k_ref, v_ref, dsf_ref, dsb_ref, dk_ref, dv_ref, dlg_ref):
        h = pl.program_id(1)
        pos = lax.broadcasted_iota(jnp.int32, (cx, 1), 0).astype(F32)
        k = k_ref[...].astype(F32) * (DK ** -0.5)
        v16 = v_ref[...].astype(BF16)
        dk = jnp.zeros((cx, DK), F32)
        dv = jnp.zeros((cx, DV), F32)
        dlg_ref[...] = jnp.zeros_like(dlg_ref)
        for d, (ds_ref, e) in enumerate(((dsf_ref, cx - 1.0 - pos), (dsb_ref, pos))):
            w = jnp.exp(lg_ref[d, h] * e)
            ds16 = ds_ref[...].astype(BF16)
            t = _dot(v16, ds16, NT)
            dk += t * w
            dv += _dot((k * w).astype(BF16), ds16)
            dlg = jnp.sum(e * w * jnp.sum(k * t, axis=-1, keepdims=True))
            dlg_ref[d:d + 1, :] = jnp.broadcast_to(dlg, (1, 128))
        dk_ref[...] = (dk * (DK ** -0.5)).astype(BF16)
        dv_ref[...] = dv.astype(BF16)

    st = pl.BlockSpec((None, None, DK, DV), lambda b, h: (b, h, 0, 0))
    return pl.pallas_call(
        body, name="ctx_state_bwd", grid=(nb, RH),
        in_specs=[pl.BlockSpec(memory_space=pltpu.SMEM),
                  pl.BlockSpec((cx, DK), lambda b, h: (rb + b, RK // DK + h)),
                  pl.BlockSpec((cx, DV), lambda b, h: (rb + b, RV // DV + h)), st, st],
        out_specs=[pl.BlockSpec((cx, DK), lambda b, h: (b, h)), pl.BlockSpec((cx, DV), lambda b, h: (b, h)),
                   pl.BlockSpec((None, None, 8, 128), lambda b, h: (b, h, 0, 0))],
        out_shape=[_sds((nb * cx, RH * DK), BF16), _sds((nb * cx, RH * DV), BF16), _sds((nb, RH, 8, 128), F32)],
        compiler_params=_params(("parallel", "parallel")),
    )(lg, px, px, ds_f, ds_b)


def _assemble_lat(rows_all, dk_f, dk_b, dv_f, dv_b, dak16, dvx, dq_f, dq_b, drg16, daq16, dag16, dmr16, dma16, tm):
    t_rows = dk_f.shape[0]

    def body(dkf, dkb, dvf, dvb, dak, dav, dqf, dqb, drg, daq, dag, dmr, dma, o_ref):
        o_ref[:, RK:RK + RH * DK] = (dkf[...].astype(F32) + dkb[...].astype(F32)).astype(BF16)
        o_ref[:, RV:RV + RH * DV] = (dvf[...].astype(F32) + dvb[...].astype(F32)).astype(BF16)
        o_ref[:, AK:AK + HKV * HD] = dak[...]
        o_ref[:, AV:AV + HKV * HD] = dav[...].astype(BF16)
        o_ref[:, RQ:RQ + RH * DK] = (dqf[...].astype(F32) + dqb[...].astype(F32)).astype(BF16)
        o_ref[:, RG:RG + RH * DV] = drg[...]
        o_ref[:, AQ:AQ + D] = daq[...]
        o_ref[:, AG:AG + D] = dag[...]
        o_ref[:, MR:MR + D] = dmr[...]
        o_ref[:, MA:MA + D] = dma[...]

    args = (dk_f, dk_b, dv_f, dv_b, dak16, dvx, dq_f, dq_b, drg16, daq16, dag16, dmr16, dma16)
    return pl.pallas_call(
        body, name="assemble_lat", grid=(t_rows // tm,),
        in_specs=[pl.BlockSpec((tm, a.shape[1]), lambda i: (i, 0)) for a in args],
        out_specs=pl.BlockSpec((tm, IN_COLS), lambda i: (i, 0)), out_shape=_sds((rows_all, IN_COLS), BF16),
        compiler_params=_params(("parallel",)),
    )(*args)


def _assemble_ctx(dp_all, dck16, dcv16, dcak16, dvc, t_rows, tm):
    c_rows = dck16.shape[0]
    rb = t_rows // tm

    def body(_, dck, dcv, dcak, dcav, o_ref):
        o_ref[:, RK:RK + RH * DK] = dck[...]
        o_ref[:, RV:RV + RH * DV] = dcv[...]
        o_ref[:, AK:AK + HKV * HD] = dcak[...]
        o_ref[:, AV:AV + HKV * HD] = dcav[...].astype(BF16)
        o_ref[:, KV_COLS:] = jnp.zeros((tm, IN_COLS - KV_COLS), BF16)

    args = (dck16, dcv16, dcak16, dvc)
    return pl.pallas_call(
        body, name="assemble_ctx", grid=(c_rows // tm,),
        in_specs=[pl.BlockSpec(memory_space=pl.ANY)]
        + [pl.BlockSpec((tm, a.shape[1]), lambda i: (i, 0)) for a in args],
        out_specs=pl.BlockSpec((tm, IN_COLS), lambda i: (rb + i, 0)), out_shape=_sds(dp_all.shape, BF16),
        input_output_aliases={0: 0},
        compiler_params=_params(("parallel",)),
    )(dp_all, *args)


def _norm_bwd(dh, x2, mod3, norm_w, dxn, row_off, rows_per_group, group0, tm, name):
    with_dx = dxn is not None
    rows = x2.shape[0]
    rb0 = row_off // tm
    bpg = rows_per_group // tm
    ngroups = rows // rows_per_group

    def body(*refs):
        if with_dx:
            dh_ref, x_ref, sc_ref, nw_ref, dxn_ref, dx_ref, dsh_ref, dsc_ref, dnw_ref = refs
        else:
            dh_ref, x_ref, sc_ref, nw_ref, dsh_ref, dsc_ref, dnw_ref = refs
        i = pl.program_id(0)
        dhv = dh_ref[...]
        xv = x_ref[...]
        nw = nw_ref[...]
        r = lax.rsqrt(jnp.mean(xv * xv, axis=-1, keepdims=True) + EPS)
        xh = xv * r
        dm = dhv * (1.0 + sc_ref[...])
        dsh = jnp.sum(dhv, axis=0, keepdims=True)
        dsc = jnp.sum(dhv * (xh * nw), axis=0, keepdims=True)
        dnw = jnp.sum(dm * xh, axis=0, keepdims=True)
        if with_dx:
            dxh = dm * nw
            dx_ref[...] = dxn_ref[...] + r * (dxh - xh * jnp.mean(dxh * xh, axis=-1, keepdims=True))

        @pl.when(i % bpg == 0)
        def _():
            dsh_ref[...] = dsh
            dsc_ref[...] = dsc

        @pl.when(i % bpg != 0)
        def _():
            dsh_ref[...] += dsh
            dsc_ref[...] += dsc

        @pl.when(i == 0)
        def _():
            dnw_ref[...] = dnw

        @pl.when(i > 0)
        def _():
            dnw_ref[...] += dnw

    grp = pl.BlockSpec((None, 1, D), lambda i: (i // bpg, 0, 0))
    in_specs = [pl.BlockSpec((tm, D), lambda i: (rb0 + i, 0)), pl.BlockSpec((tm, D), lambda i: (i, 0)),
                pl.BlockSpec((None, 1, D), lambda i: (group0 + i // bpg, 0, 1)),
                pl.BlockSpec((1, D), lambda i: (0, 0))]
    args = [dh, x2, mod3, norm_w]
    out_specs = [grp, grp, pl.BlockSpec((1, D), lambda i: (0, 0))]
    out_shape = [_sds((ngroups, 1, D), F32), _sds((ngroups, 1, D), F32), _sds((1, D), F32)]
    if with_dx:
        in_specs.append(pl.BlockSpec((tm, D), lambda i: (i, 0)))
        args.append(dxn)
        out_specs.insert(0, pl.BlockSpec((tm, D), lambda i: (i, 0)))
        out_shape.insert(0, _sds((rows, D), F32))
    return pl.pallas_call(
        body, name=name, grid=(rows // tm,), in_specs=in_specs, out_specs=out_specs, out_shape=out_shape,
        compiler_params=_params(("arbitrary",)),
    )(*args)


def _small_final(dmod_all, dmodc_parts, c_rows, dm_loc_rows, nw_parts, misc_parts, c_ctx, r_pad, w_ada16):
    loc = dm_loc_rows.shape[1]

    def body(dm_ref, dmc_ref, c_ref, dml_ref, nwp_ref, mp_ref, cc_ref, r_ref, w_ref,
             gb_ref, gc_ref, gnw_ref, misc_ref, gwa_ref):
        dmc = jnp.sum(dmc_ref[...], axis=0, keepdims=True)
        gb_ref[...] = jnp.sum(dm_ref[...], axis=0, keepdims=True) + dmc
        dsc = _dot(jnp.broadcast_to(dmc, (8, 3 * D)).astype(BF16), w_ref[...], NT)[0:1, :]
        gc_ref[...] = dsc * _dsilu(cc_ref[...])
        gnw_ref[...] = jnp.sum(nwp_ref[...], axis=0, keepdims=True)
        misc = jnp.sum(mp_ref[...], axis=0, keepdims=True)
        y = jnp.exp2(r_ref[...])
        lane = lax.broadcasted_iota(jnp.int32, (1, D), 1)
        is_decay = jnp.logical_and(lane >= 2 * HD, lane < 2 * HD + 2 * RH)
        misc_ref[...] = misc * jnp.where(is_decay, -(y * np.float32(np.log(2.0))) / (1.0 - y), 1.0)
        gwa_ref[...] = _dot(_silu(c_ref[...]).astype(BF16), dml_ref[...].astype(BF16), TN)

    return pl.pallas_call(
        body, name="small_final",
        out_shape=[_sds((1, 3 * D), F32), _sds((1, D), F32), _sds((1, D), F32), _sds((1, D), F32), _sds((D, loc), F32)],
        compiler_params=pltpu.CompilerParams(vmem_limit_bytes=VMEM_LIMIT),
    )(dmod_all, dmodc_parts, c_rows, dm_loc_rows, nw_parts, misc_parts, c_ctx, r_pad, w_ada16)


def _adamw(w, g, m, v, name):
    rows, cols = w.shape
    tm = _pick(rows, 448, 8)
    bc1 = 1.0 - B1 ** STEP
    bc2 = 1.0 - B2 ** STEP

    def body(w_ref, g_ref, m_ref, v_ref, d_ref, nm_ref, nv_ref):
        g_ = g_ref[...]
        nm = B1 * m_ref[...] + (1.0 - B1) * g_
        nv = B2 * v_ref[...] + (1.0 - B2) * (g_ * g_)
        nm_ref[...] = nm
        nv_ref[...] = nv
        d_ref[...] = -LR * ((nm / bc1) / (jnp.sqrt(nv / bc2) + ADAM_EPS) + WD * w_ref[...])

    blk = pl.BlockSpec((tm, cols), lambda i: (i, 0))
    return pl.pallas_call(
        body, name=name, grid=(rows // tm,), in_specs=[blk] * 4, out_specs=[blk] * 3,
        out_shape=[_sds((rows, cols), F32)] * 3, compiler_params=_params(("parallel",)),
    )(w, g, m, v)


def _mesh_pos():
    return lax.axis_index("x"), lax.axis_index("y"), lax.axis_index("c")


def _all_gather(arrs, name):
    n = len(arrs)

    def body(*refs):
        ins, outs = refs[:n], refs[n:2 * n]
        send_sems, recv_sems, local_sems = refs[2 * n:]
        x, y, c = _mesh_pos()
        me, sib = (x, y, c), (x, y, 1 - c)
        chips = [(1 - x, y), (x, 1 - y), (1 - x, 1 - y)]

        def slot(p):
            return 4 * p[0] + 2 * p[1] + p[2]

        def copy(a, k, block, to, own):
            dst = outs[a].at[slot(block)]
            return pltpu.make_async_remote_copy(
                src_ref=ins[a] if own else dst, dst_ref=dst, send_sem=send_sems.at[a, k], recv_sem=recv_sems.at[a, k],
                device_id=to, device_id_type=MESH_T)

        mine = [pltpu.make_async_copy(ins[a], outs[a].at[slot(me)], local_sems.at[a]) for a in range(n)]
        for cp in mine:
            cp.start()
        first = []
        for a in range(n):
            first.append(copy(a, 0, me, sib, True))
            first += [copy(a, 1 + j, me, (*chip, c), True) for j, chip in enumerate(chips)]
        for cp in first:
            cp.start()
        passed = []
        for j, chip in enumerate(chips):
            for a in range(n):
                copy(a, 1 + j, (*chip, c), me, False).wait_recv()
                fwd = copy(a, 4 + j, (*chip, c), sib, False)
                fwd.start()
                passed.append(fwd)
        for a in range(n):
            copy(a, 0, sib, me, False).wait_recv()
            for j, chip in enumerate(chips):
                copy(a, 4 + j, (*chip, 1 - c), me, False).wait_recv()
        for cp in first + passed:
            cp.wait_send()
        for cp in mine:
            cp.wait()

    hbm = pl.BlockSpec(memory_space=pl.ANY)
    return pl.pallas_call(
        body, name=name, in_specs=[hbm] * n, out_specs=[hbm] * n,
        out_shape=[_sds((N_DEV,) + a.shape, a.dtype) for a in arrs],
        scratch_shapes=[pltpu.SemaphoreType.DMA((n, 7)), pltpu.SemaphoreType.DMA((n, 7)), pltpu.SemaphoreType.DMA((n,))],
    )(*arrs)


def _pair_exchange(arrs, name):
    n = len(arrs)

    def body(*refs):
        ins, outs = refs[:n], refs[n:2 * n]
        send_sems, recv_sems = refs[2 * n:]
        x, y, c = _mesh_pos()
        sib = (x, y, 1 - c)
        sends = []
        for a in range(n):
            for k in range(4):
                sends.append(pltpu.make_async_remote_copy(
                    src_ref=ins[a].at[2 * k + 1 - c], dst_ref=outs[a].at[k], send_sem=send_sems.at[a, k],
                    recv_sem=recv_sems.at[a, k], device_id=sib, device_id_type=MESH_T))
        for cp in sends:
            cp.start()
        for cp in sends:
            cp.wait_recv()
        for cp in sends:
            cp.wait_send()

    hbm = pl.BlockSpec(memory_space=pl.ANY)
    return pl.pallas_call(
        body, name=name, in_specs=[hbm] * n, out_specs=[hbm] * n,
        out_shape=[_sds((4,) + a.shape[1:], a.dtype) for a in arrs],
        scratch_shapes=[pltpu.SemaphoreType.DMA((n, 4)), pltpu.SemaphoreType.DMA((n, 4))],
    )(*arrs)


def _pair_add(part, got, core, name):
    _, rows, cols = part.shape
    tm = _pick(rows, 672, 16)
    p4 = part.reshape(4, 2, rows, cols)

    def body(core_ref, p_ref, g_ref, o_ref):
        o_ref[...] = (p_ref[...].astype(F32) + g_ref[...].astype(F32)).astype(BF16)

    blk = pl.BlockSpec((None, tm, cols), lambda k, i, cr: (k, i, 0))
    return pl.pallas_call(
        body, name=name,
        grid_spec=pltpu.PrefetchScalarGridSpec(
            num_scalar_prefetch=1, grid=(4, rows // tm),
            in_specs=[pl.BlockSpec((None, None, tm, cols), lambda k, i, cr: (k, cr[0], i, 0)), blk], out_specs=blk),
        out_shape=_sds((4, rows, cols), BF16), compiler_params=_params(("parallel", "parallel")),
    )(core, p4, got)


def _chip_sum(pair_sums, landed, chip, name):
    _, rows, cols = pair_sums.shape
    tm = _pick(rows, 672, 16)

    def body(chip_ref, s_ref, l_ref, o_ref):
        acc = s_ref[...].astype(F32)
        for j in range(3):
            acc = acc + l_ref[j].astype(F32)
        o_ref[...] = acc

    return pl.pallas_call(
        body, name=name,
        grid_spec=pltpu.PrefetchScalarGridSpec(
            num_scalar_prefetch=1, grid=(rows // tm,),
            in_specs=[pl.BlockSpec((None, tm, cols), lambda i, ch: (ch[0], i, 0)),
                      pl.BlockSpec((3, tm, cols), lambda i, ch: (0, i, 0))],
            out_specs=pl.BlockSpec((tm, cols), lambda i, ch: (i, 0))),
        out_shape=_sds((rows, cols), F32), compiler_params=_params(("parallel",)),
    )(chip, pair_sums, landed)


_HBM = pl.BlockSpec(memory_space=pltpu.HBM)
_SEM = pl.BlockSpec(memory_space=pltpu.SEMAPHORE)
_EFFECT = pltpu.SideEffectType.DATAFLOW_SIDE_EFFECTING


def _chip_routes(n):
    def plan(x, y, c):
        routes = []
        for a in range(n):
            for j in range(1, 4):
                px, py = x ^ (j >> 1), y ^ (j & 1)
                routes.append((a, 2 * px + py, (px, py, c), j - 1))
        return routes
    return plan, 3 * n


def _bcast_routes(n):
    def plan(x, y, c):
        routes = []
        for a in range(n):
            for k in range(1, N_DEV):
                peer = (x ^ ((k >> 2) & 1), y ^ ((k >> 1) & 1), c ^ (k & 1))
                routes.append((a, 0, peer, 4 * x + 2 * y + c))
        return routes
    return plan, 7 * n


def _route_copies(srcs, lands, send_sems, recv_sems, routes):
    return [pltpu.make_async_remote_copy(
        src_ref=srcs[a].at[sb], dst_ref=lands[a].at[lb], send_sem=send_sems.at[r], recv_sem=recv_sems.at[r],
        device_id=peer, device_id_type=MESH_T) for r, (a, sb, peer, lb) in enumerate(routes)]


def _exchange_start(srcs, lands, routes, name, after=()):
    plan, count = routes
    n = len(srcs)
    n_in = 2 * n + len(after)

    def body(*refs):
        send_sems, recv_sems = refs[n_in], refs[n_in + 1]
        token = refs[-1]
        for cp in _route_copies(refs[:n], refs[n:2 * n], send_sems, recv_sems, plan(*_mesh_pos())):
            cp.start()
        token[...] = jnp.zeros_like(token)

    args = [pltpu.with_memory_space_constraint(a, pltpu.HBM) for a in list(srcs) + list(lands)]
    out = pl.pallas_call(
        body, name=name,
        out_shape=(pltpu.SemaphoreType.DMA((count,)), pltpu.SemaphoreType.DMA((count,)),
                   *[pltpu.HBM(a.shape, a.dtype) for a in args], _sds((8, 128), F32)),
        in_specs=[_HBM] * (2 * n) + [pl.BlockSpec(memory_space=pl.ANY)] * len(after),
        out_specs=(_SEM, _SEM, *([_HBM] * (2 * n)), pl.BlockSpec(memory_space=pltpu.VMEM)),
        input_output_aliases={i: 2 + i for i in range(2 * n)},
        compiler_params=pltpu.CompilerParams(has_side_effects=_EFFECT),
    )(*args, *after)
    return (out[0], out[1], list(out[2:2 + 2 * n]), routes), out[-1]


def _exchange_wait_some(state, after, only, name):
    send_sems, recv_sems, bufs, (plan, count) = state
    n = len(bufs) // 2

    def body(*refs):
        send_s, recv_s = refs[2 * n], refs[2 * n + 1]
        for r, cp in enumerate(_route_copies(refs[:n], refs[n:2 * n], send_s, recv_s, plan(*_mesh_pos()))):
            if only is None or r in only:
                cp.wait_send()
                cp.wait_recv()

    out = pl.pallas_call(
        body, name=name, out_shape=tuple(pltpu.HBM(a.shape, a.dtype) for a in bufs),
        in_specs=[_HBM] * (2 * n) + [_SEM, _SEM, pl.BlockSpec(memory_space=pl.ANY)], out_specs=tuple([_HBM] * (2 * n)),
        input_output_aliases={i: i for i in range(2 * n)},
        compiler_params=pltpu.CompilerParams(has_side_effects=_EFFECT),
    )(*bufs, send_sems, recv_sems, after)
    return (send_sems, recv_sems, list(out), (plan, count)), list(out[:n]), list(out[n:])


def _exchange_wait(state, after, name):
    _, srcs, lands = _exchange_wait_some(state, after, None, name)
    return srcs, lands


def _group_routes(js):
    def plan(x, y, c):
        return [(0, 0, (x ^ (j >> 1), y ^ (j & 1), c), 2 * j + c) for j in js]
    return plan, len(js)


def _pair_fill(groups, j, name, after=()):
    def body(*refs):
        g_ref, send_sem, recv_sem = refs[-3:]
        x, y, c = _mesh_pos()
        mine = g_ref.at[2 * j + c]
        to_sib = pltpu.make_async_remote_copy(src_ref=mine, dst_ref=mine, send_sem=send_sem, recv_sem=recv_sem,
                                              device_id=(x, y, 1 - c), device_id_type=MESH_T)
        to_sib.start()
        pltpu.make_async_remote_copy(src_ref=mine, dst_ref=g_ref.at[2 * j + 1 - c], send_sem=send_sem, recv_sem=recv_sem,
                                     device_id=(x, y, 1 - c), device_id_type=MESH_T).wait_recv()
        to_sib.wait_send()

    hbm = pl.BlockSpec(memory_space=pl.ANY)
    return pl.pallas_call(
        body, name=name, in_specs=[hbm] * (1 + len(after)), out_specs=hbm, out_shape=_sds(groups.shape, groups.dtype),
        input_output_aliases={0: 0},
        scratch_shapes=[pltpu.SemaphoreType.DMA, pltpu.SemaphoreType.DMA],
    )(groups, *after)


def _in_proj_group(h_all, groups, j, chip, px_prev, after, name):
    rows_all = h_all.shape[0]
    gcols = IN_COLS // 4
    tm = _pick(rows_all, 1536, 128)
    g4 = groups.reshape(4, gcols, D)

    n_lead = (1 if px_prev is not None else 0) + len(after)
    lead = ([px_prev] if px_prev is not None else []) + list(after)

    def body(chip_ref, *refs):
        h_ref, w_ref, o_ref = refs[n_lead:]
        o_ref[...] = _dot(h_ref[...], w_ref[...], NT).astype(BF16)
    return pl.pallas_call(
        body, name=name,
        grid_spec=pltpu.PrefetchScalarGridSpec(
            num_scalar_prefetch=1, grid=(rows_all // tm,),
            in_specs=[pl.BlockSpec(memory_space=pl.ANY)] * n_lead
            + [pl.BlockSpec((tm, D), lambda i, ch: (i, 0)), pl.BlockSpec((None, gcols, D), lambda i, ch: (j, 0, 0))],
            out_specs=pl.BlockSpec((tm, gcols), lambda i, ch: (i, ch[0] ^ j))),
        out_shape=_sds((rows_all, IN_COLS), BF16),
        input_output_aliases={1: 0} if px_prev is not None else {},
        compiler_params=_params(("parallel",)),
    )(chip, *lead, h_all, g4)


def _d_h_groups(dp_all, groups, chip, after):
    rows_all = dp_all.shape[0]
    gcols = IN_COLS // 4
    tm = _pick(rows_all, 1536, 128)
    g4 = groups.reshape(4, gcols, D)
    n_lead = len(after)

    def body(chip_ref, *refs):
        a_ref, w_ref, o_ref = refs[n_lead:]
        j = pl.program_id(1)
        part = _dot(a_ref[...], w_ref[...])

        @pl.when(j == 0)
        def _():
            o_ref[...] = part

        @pl.when(j > 0)
        def _():
            o_ref[...] += part

    return pl.pallas_call(
        body, name="d_h",
        grid_spec=pltpu.PrefetchScalarGridSpec(
            num_scalar_prefetch=1, grid=(rows_all // tm, 4),
            in_specs=[pl.BlockSpec(memory_space=pl.ANY)] * n_lead
            + [pl.BlockSpec((tm, gcols), lambda i, j, ch: (i, ch[0] ^ j)),
               pl.BlockSpec((None, gcols, D), lambda i, j, ch: (j, 0, 0))],
            out_specs=pl.BlockSpec((tm, D), lambda i, j, ch: (i, 0))),
        out_shape=_sds((rows_all, D), F32),
        compiler_params=_params(("parallel", "arbitrary")),
    )(chip, *after, dp_all, g4)


def _reduce_scatter_start(parts, core, name):
    got = _pair_exchange(parts, name + "_pair")
    sums = [_pair_add(p, g, core, "%s_add_%d" % (name, i)) for i, (p, g) in enumerate(zip(parts, got))]
    lands = [lax.empty((3,) + s_.shape[1:], BF16) for s_ in sums]
    return _exchange_start(sums, lands, _chip_routes(len(sums)), name + "_start")


def _reduce_scatter_finish(rs_state, after, chip, name):
    sums, landed = _exchange_wait(rs_state, after, name + "_wait")
    return [_chip_sum(s_, l_, chip, "%s_sum_%d" % (name, i)) for i, (s_, l_) in enumerate(zip(sums, landed))]


def _local_step(x, c, ctx, norm_w, ret_log2_decay, q_norm_w, k_norm_w, loss_target,
                mod, proj_in, proj_back, get_w_o, on_out_grads, on_in_grad, started=()):
    nb, seq, _ = x.shape
    cx = ctx.shape[1]
    t_rows, c_rows = nb * seq, nb * cx
    rows_all = t_rows + c_rows
    nc = seq // CH
    tm = _pick(seq, 256, 128)
    te = _pick(seq, 512, 128)
    assert cx % tm == 0 and t_rows % cx == 0 and seq % GRID_W == 0

    x2 = x.reshape(t_rows, D)
    ctx2 = ctx.reshape(c_rows, D)
    tgt = loss_target.reshape(t_rows, D)
    lg = _log_gamma(ret_log2_decay)
    cos, sin = _rope_tables(seq)

    mod3 = mod[:, None, :]
    h_all = _norm_fwd(x2, mod3, norm_w, rows_all, 0, seq, 0, None, te, "norm_fwd", after=started)
    h_all = _norm_fwd(ctx2, mod3, norm_w, rows_all, t_rows, c_rows, nb, h_all, tm, "norm_fwd_ctx")
    px = proj_in(h_all)
    s0f, s0b = _ctx_state(px, lg, nb, t_rows, cx)
    o_f, o_b, hist_f, hist_b = _ret_fwd(px, lg, s0f, s0b, nb, nc)
    yret16 = _ret_post(o_f, o_b, px, te)
    q16 = _qk_prep(px, q_norm_w, cos, sin, t_rows, 0, AQ, HQ, 4, seq, te, "q_prep")
    kx16 = _qk_prep(px, k_norm_w, cos, sin, t_rows, 0, AK, HKV, HKV, seq, te, "k_prep")
    kc16 = _qk_prep(px, k_norm_w, None, None, c_rows, t_rows, AK, HKV, HKV, seq, tm, "kc_prep")
    o_att, yatt16, lse = _att_fwd(q16, kx16, kc16, px, nb, seq, cx, te)
    w_o_ret16, w_o_att16, w_out16 = get_w_o(lse)
    a_ret, a_att, y16 = _merge(yret16, yatt16, px, w_o_ret16, w_o_att16, te)
    dxn, dout16, dgate, loss_b = _outproj(y16, w_out16, x2, tgt, mod3, nb, seq, te)

    gw_out = _matmul(y16, dout16, ta=True, tm=D, tn=D, tk=D, out_dtype=BF16, name="gw_out")
    da_ret16, da_att16, dmr16, dma16 = _bwd_merge(dout16, w_out16, px, a_ret, a_att, te)
    gw_o_ret = _matmul(yret16, da_ret16, ta=True, tm=D, tn=D, tk=D, out_dtype=BF16, name="gw_o_ret")
    gw_o_att = _matmul(yatt16, da_att16, ta=True, tm=D, tn=D, tk=D, out_dtype=BF16, name="gw_o_att")
    out_state, out_started = on_out_grads([gw_o_ret, gw_o_att, gw_out])
    do16, drg16 = _bwd_branch_ret(da_ret16, w_o_ret16, px, o_f, o_b, te, after=out_started)
    dao, dag16 = _bwd_branch_att(da_att16, w_o_att16, px, o_att, te)
    dq_rot, dkx, dvx, dkc, dvc = _att_bwd(q16, kx16, kc16, px, dao, o_att, lse, nb, seq, cx, te)
    daq16, gq = _qk_prep_bwd(dq_rot, px, q_norm_w, cos, sin, t_rows, 0, AQ, HQ, 4, seq, te, "q_prep_bwd")
    dak16, gk_lat = _qk_prep_bwd(dkx.reshape(t_rows, HKV * HD), px, k_norm_w, cos, sin, t_rows, 0, AK, HKV, HKV, seq, te,
                                 "k_prep_bwd")
    dcak16, gk_ctx = _qk_prep_bwd(dkc.reshape(c_rows, HKV * HD), px, k_norm_w, None, None, c_rows, t_rows, AK, HKV, HKV,
                                  seq, tm, "kc_prep_bwd")
    dq_f, dk_f, dv_f, dq_b, dk_b, dv_b, ds_f, ds_b, dlg_scan = _ret_bwd(px, lg, do16, hist_f, hist_b, nb, nc)
    dck16, dcv16, dlg_ctx = _ctx_state_bwd(px, lg, ds_f, ds_b, nb, t_rows, cx)
    dp_all = _assemble_lat(rows_all, dk_f, dk_b, dv_f, dv_b, dak16, dvx.reshape(t_rows, HKV * HD), dq_f, dq_b, drg16,
                           daq16, dag16, dmr16, dma16, tm)
    dp_all = _assemble_ctx(dp_all, dck16, dcv16, dcak16, dvc.reshape(c_rows, HKV * HD), t_rows, tm)
    gw_in_t = _matmul(dp_all, h_all, ta=True, tm=1536, tn=D, tk=2304, out_dtype=BF16, name="gw_in")
    in_state, in_started = on_in_grad(gw_in_t)
    dh = proj_back(dp_all, in_started)
    grad_x, dsh, dsc, gnw_lat = _norm_bwd(dh, x2, mod3, norm_w, dxn, 0, seq, 0, te, "norm_bwd")
    dsh_c, dsc_c, gnw_ctx = _norm_bwd(dh, ctx2, mod3, norm_w, None, t_rows, c_rows, nb, tm, "norm_bwd_ctx")

    dlg = (jnp.sum(dlg_scan[:, :, 0], axis=0) + jnp.sum(dlg_ctx[:, :, :2, 0], axis=0).T.reshape(2 * RH)).reshape(1, 2 * RH)
    misc = jnp.concatenate([gq, gk_lat + gk_ctx, dlg, jnp.sum(loss_b[:, 0, 0]).reshape(1, 1),
                            jnp.zeros((1, D - 2 * HD - 2 * RH - 1), F32)], axis=1)
    rows = []
    for b in range(nb):
        rows += [dsh[b], dsc[b], dgate[b]]
    rows += [dsh_c[0], dsc_c[0]] + [c[b:b + 1] for b in range(nb)] + [gnw_lat + gnw_ctx, misc]
    payload = jnp.concatenate(rows + [jnp.zeros((PAY_ROWS - len(rows), D), F32)], axis=0)
    return grad_x.reshape(nb, seq, D), out_state, in_state, payload


def _finish_small(gathered, nb, c_ctx, ret_log2_decay, w_ada16, dev):
    n_dev = gathered.shape[0]
    loc = 3 * D // n_dev
    dmod_all = gathered[:, :3 * nb].reshape(n_dev * nb, 3 * D)
    dmodc_parts = jnp.concatenate([gathered[:, 3 * nb:3 * nb + 2].reshape(n_dev, 2 * D), jnp.zeros((n_dev, D), F32)], axis=1)
    c_all = gathered[:, 3 * nb + 2:4 * nb + 2].reshape(n_dev * nb, D)
    nw_parts = gathered[:, 4 * nb + 2]
    misc_parts = gathered[:, 4 * nb + 3]
    n_rows = n_dev * nb + n_dev
    pad = (-n_rows) % 16
    c_rows = jnp.concatenate([c_all, jnp.broadcast_to(c_ctx.reshape(1, D), (n_dev, D)), jnp.zeros((pad, D), F32)], axis=0)
    dm_rows = jnp.concatenate([dmod_all, dmodc_parts, jnp.zeros((pad, 3 * D), F32)], axis=0)
    dm_loc_rows = lax.dynamic_slice_in_dim(dm_rows, dev * loc, loc, axis=1)
    r_pad = jnp.full((1, D), -1.0, F32).at[:, 2 * HD:2 * HD + 2 * RH].set(ret_log2_decay.reshape(1, 2 * RH))
    gb, gc, gnw, misc, gwa = _small_final(dmod_all, dmodc_parts, c_rows, dm_loc_rows, nw_parts, misc_parts,
                                          c_ctx.reshape(1, D), r_pad, w_ada16)
    return (gb, gc, gnw, misc[:, :HD], misc[:, HD:2 * HD], misc[:, 2 * HD:2 * HD + 2 * RH], gwa,
            misc[0, 2 * HD + 2 * RH])


def kernel(x, c, ctx, c_ctx, norm_w, w_ada, b_ada, w_in, ret_log2_decay, q_norm_w, k_norm_w, w_o_ret, w_o_att, w_out, loss_target, m_c_ctx, m_norm_w, m_w_ada, m_b_ada, m_w_in, m_ret_log2_decay, m_q_norm_w, m_k_norm_w, m_w_o_ret, m_w_o_att, m_w_out, v_c_ctx, v_norm_w, v_w_ada, v_b_ada, v_w_in, v_ret_log2_decay, v_q_norm_w, v_k_norm_w, v_w_o_ret, v_w_o_att, v_w_out):
    nb = x.shape[0]
    mx, my, mc = _mesh_pos()
    dev = 4 * mx + 2 * my + mc
    core = jnp.reshape(mc, (1,)).astype(jnp.int32)
    chip = jnp.reshape(2 * mx + my, (1,)).astype(jnp.int32)

    n_loc = 3 * D // N_DEV
    c8 = jnp.zeros((8, D), F32).at[:nb].set(c).at[nb].set(c_ctx)
    (c_all,) = _all_gather([c8], "gather_c")
    ada_shard = w_ada[0].astype(BF16)
    b_loc = lax.dynamic_slice(b_ada, (0, dev * n_loc), (1, n_loc))
    mod_cols = _mod_part(c_all.reshape(N_DEV * 8, D), ada_shard, b_loc)
    (mod_all,) = _all_gather([mod_cols], "gather_mod")
    mod = jnp.transpose(lax.dynamic_slice(mod_all, (0, dev * 8, 0), (N_DEV, 8, n_loc)), (1, 0, 2)).reshape(8, 3 * D)
    ada_land = lax.dynamic_update_slice(lax.empty((N_DEV,) + ada_shard.shape, BF16), ada_shard[None], (dev, 0, 0))

    w_in_t = jnp.transpose(w_in[0])
    in_shard = w_in_t.astype(BF16)
    groups = lax.dynamic_update_slice(lax.empty((N_DEV,) + in_shard.shape, BF16), in_shard[None], (mc, 0, 0))
    groups = _pair_fill(groups, 0, "gather_in_pair", after=(mod_all,))
    (near_send, near_recv, near_bufs, near_routes), gin_token = _exchange_start(
        [in_shard[None]], [groups], _group_routes((1, 2)), "gather_in_start")
    w_in_groups, wo_states, ada_states = [], [], []
    wo_shards = [w_[0].astype(BF16) for w_ in (w_o_ret, w_o_att, w_out)]
    wo_lands = [lax.dynamic_update_slice(lax.empty((N_DEV,) + s_.shape, BF16), s_[None], (dev, 0, 0)) for s_ in wo_shards]

    def proj_in(h_all):
        src, groups = near_bufs
        px = _in_proj_group(h_all, groups, 0, chip, None, (gin_token,), "in_proj_0")
        _, (src,), (groups,) = _exchange_wait_some((near_send, near_recv, [src, groups], near_routes), px, (0,),
                                                   "gather_in_wait_1")
        groups = _pair_fill(groups, 1, "gather_in_fill_1")
        (far_send, far_recv, (src, groups), far_routes), far_token = _exchange_start(
            [src], [groups], _group_routes((3,)), "gather_in_start_far")
        wo_state, wo_token = _exchange_start([s_[None] for s_ in wo_shards], wo_lands, _bcast_routes(3),
                                             "gather_wo_start", after=(far_token,))
        wo_states.append(wo_state)
        ada_state, ada_token = _exchange_start([ada_shard[None]], [ada_land], _bcast_routes(1), "gather_ada_start",
                                               after=(wo_token,))
        ada_states.append(ada_state)
        px = _in_proj_group(h_all, groups, 1, chip, px, (ada_token,), "in_proj_1")
        for j in (2, 3):
            state = ((near_send, near_recv, [src, groups], near_routes) if j < 3 else
                     (far_send, far_recv, [src, groups], far_routes))
            _, (src,), (groups,) = _exchange_wait_some(state, px, (1,) if j < 3 else None, "gather_in_wait_%d" % j)
            groups = _pair_fill(groups, j, "gather_in_fill_%d" % j)
            px = _in_proj_group(h_all, groups, j, chip, px, (), "in_proj_%d" % j)
        w_in_groups.append(groups)
        return px

    def proj_back(dp_all, after):
        return _d_h_groups(dp_all, w_in_groups[0], chip, after)

    def get_w_o(after):
        _, (l_ret, l_att, l_out) = _exchange_wait(wo_states[0], after, "gather_wo_wait")
        return l_ret.reshape(RH * DV, D), l_att.reshape(D, D), l_out.reshape(D, D)

    def on_out_grads(grads):
        parts = [g_.reshape(N_DEV, g_.shape[0] // N_DEV, D) for g_ in grads]
        state, token = _reduce_scatter_start(parts, core, "rs_out")
        return state, (token,)

    def on_in_grad(grad):
        state, token = _reduce_scatter_start([grad.reshape(N_DEV, IN_COLS // N_DEV, D)], core, "rs_in")
        return state, (token,)

    grad_x, out_state, in_state, payload = _local_step(
        x, c, ctx, norm_w, ret_log2_decay, q_norm_w, k_norm_w, loss_target,
        mod, proj_in, proj_back, get_w_o, on_out_grads, on_in_grad, started=(gin_token,))

    (gathered,) = _all_gather([payload], "gather_small")
    _, (l_ada,) = _exchange_wait(ada_states[0], gathered, "gather_ada_wait")
    w_ada16 = jnp.transpose(l_ada, (1, 0, 2)).reshape(D, 3 * D)
    gb, gc, gnw, gq, gk, gr, gwa, loss = _finish_small(gathered, nb, c_ctx, ret_log2_decay, w_ada16, dev)

    g_w_o_ret, g_w_o_att, g_w_out = _reduce_scatter_finish(out_state, gathered, chip, "rs_out")
    (g_w_in_t,) = _reduce_scatter_finish(in_state, gathered, chip, "rs_in")

    grads = [gc.reshape(c_ctx.shape), gnw, gwa[None], gb, g_w_in_t, gr.reshape(ret_log2_decay.shape), gq, gk,
             g_w_o_ret[None], g_w_o_att[None], g_w_out[None]]
    weights = [c_ctx, norm_w, w_ada, b_ada, w_in_t, ret_log2_decay, q_norm_w, k_norm_w, w_o_ret, w_o_att, w_out]
    ms = [m_c_ctx, m_norm_w, m_w_ada, m_b_ada, jnp.transpose(m_w_in[0]), m_ret_log2_decay, m_q_norm_w, m_k_norm_w,
          m_w_o_ret, m_w_o_att, m_w_out]
    vs = [v_c_ctx, v_norm_w, v_w_ada, v_b_ada, jnp.transpose(v_w_in[0]), v_ret_log2_decay, v_q_norm_w, v_k_norm_w,
          v_w_o_ret, v_w_o_att, v_w_out]
    deltas, new_ms, new_vs = [], [], []
    for i, (w, g, m, v) in enumerate(zip(weights, grads, ms, vs)):
        shape2 = (-1, w.shape[-1])
        res = _adamw(w.reshape(shape2), g.reshape(shape2), m.reshape(shape2), v.reshape(shape2), "adamw_%d" % i)
        for lst, r in zip((deltas, new_ms, new_vs), res):
            lst.append(jnp.transpose(r)[None] if i == 4 else r.reshape(w.shape))
    grads[4] = jnp.transpose(g_w_in_t)[None]
    return (loss, grad_x, *grads, *deltas, *new_ms, *new_vs)
```

```python
import numpy as np
import jax
import jax.numpy as jnp
from jax import lax
from jax.experimental import pallas as pl
from jax.experimental.pallas import tpu as pltpu

F32 = jnp.float32
BF16 = jnp.bfloat16

D = 1024
RH, DK, DV, CH = 4, 256, 512, 256
HQ, HKV, HD = 8, 2, 128
GRID_W = 64
ROPE_THETA = 10000.0
EPS = 1e-6
RK, RV, AK, AV, RQ, RG, AQ, AG, MR, MA = 0, 1024, 3072, 3328, 3584, 4608, 6656, 7680, 8704, 9728
IN_COLS = 10752
KV_COLS = 3584
N_DEV = 8
LR, B1, B2, ADAM_EPS, WD, STEP = 0.001, 0.9, 0.999, 1e-08, 0.01, 10
PAY_ROWS = 16
VMEM_LIMIT = 56 * 1024 * 1024
MESH_T = pl.DeviceIdType.MESH

NT = (((1,), (1,)), ((), ()))
TN = (((0,), (0,)), ((), ()))
SM_C = (HD ** -0.5) * float(np.log2(np.e))


def _params(sem):
    return pltpu.CompilerParams(dimension_semantics=sem, vmem_limit_bytes=VMEM_LIMIT)


def _pick(n, target, mult=8):
    best = None
    for t in range(mult, min(n, target) + 1, mult):
        if n % t == 0:
            best = t
    return best or n


def _dot(a, b, dn=None):
    if dn is None:
        return jnp.dot(a, b, preferred_element_type=F32)
    return lax.dot_general(a, b, dn, preferred_element_type=F32)


def _sig(v):
    return jax.nn.sigmoid(v)


def _silu(v):
    return v * _sig(v)


def _dsilu(v):
    s = _sig(v)
    return s * (1.0 + v * (1.0 - s))


def _sds(shape, dtype):
    return jax.ShapeDtypeStruct(shape, dtype)


def _matmul(a, b, *, ta=False, tb=False, tm, tn, tk, out_dtype, name, after=()):
    m = a.shape[1] if ta else a.shape[0]
    kdim = a.shape[0] if ta else a.shape[1]
    n = b.shape[0] if tb else b.shape[1]
    tm, tn, tk = _pick(m, tm, 128), _pick(n, tn, 128), _pick(kdim, tk, 128)
    nk = kdim // tk
    dn = (((0 if ta else 1,), (1 if tb else 0,)), ((), ()))

    def body(a_ref, b_ref, *rest):
        o_ref, acc_ref = rest[-2:]
        k = pl.program_id(2)
        part = _dot(a_ref[...].astype(BF16), b_ref[...].astype(BF16), dn)
        if nk == 1:
            o_ref[...] = part.astype(o_ref.dtype)
        else:
            @pl.when(k == 0)
            def _():
                acc_ref[...] = part

            @pl.when(k > 0)
            def _():
                acc_ref[...] += part

            @pl.when(k == nk - 1)
            def _():
                o_ref[...] = acc_ref[...].astype(o_ref.dtype)

    a_spec = pl.BlockSpec((tk, tm), lambda i, j, k: (k, i)) if ta else pl.BlockSpec((tm, tk), lambda i, j, k: (i, k))
    b_spec = pl.BlockSpec((tn, tk), lambda i, j, k: (j, k)) if tb else pl.BlockSpec((tk, tn), lambda i, j, k: (k, j))
    return pl.pallas_call(
        body, name=name, grid=(m // tm, n // tn, nk),
        in_specs=[a_spec, b_spec] + [pl.BlockSpec(memory_space=pl.ANY)] * len(after),
        out_specs=pl.BlockSpec((tm, tn), lambda i, j, k: (i, j)), out_shape=_sds((m, n), out_dtype),
        scratch_shapes=[pltpu.VMEM((tm, tn) if nk > 1 else (8, 128), F32)],
        compiler_params=_params(("parallel", "parallel", "arbitrary")),
    )(a, b, *after)


def _log_gamma(r):
    rp = jnp.full((8, 128), -1.0, F32).at[:2, :RH].set(r.reshape(2, RH))

    def body(r_ref, o_ref):
        o_ref[...] = jnp.log1p(-jnp.exp2(r_ref[...]))

    out = pl.pallas_call(body, name="log_gamma", out_shape=_sds((8, 128), F32))(rp)
    return out[:2, :RH]


def _mod_part(c_rows, w_ada_loc16, b_loc):
    def body(c_ref, w_ref, b_ref, o_ref):
        o_ref[...] = _dot(_silu(c_ref[...]).astype(BF16), w_ref[...]) + b_ref[...]

    return pl.pallas_call(
        body, name="mod_part", out_shape=_sds((c_rows.shape[0], w_ada_loc16.shape[1]), F32),
    )(c_rows, w_ada_loc16, b_loc)


def _norm_fwd(x2, mod3, norm_w, rows_all, row_off, rows_per_group, group0, h_prev, tm, name, after=()):
    rows = x2.shape[0]
    rb0 = row_off // tm
    bpg = rows_per_group // tm

    def body(*refs):
        x_ref, sh_ref, sc_ref, nw_ref, o_ref = refs[-5:]
        xv = x_ref[...]
        r = lax.rsqrt(jnp.mean(xv * xv, axis=-1, keepdims=True) + EPS)
        o_ref[...] = ((xv * r) * nw_ref[...] * (1.0 + sc_ref[...]) + sh_ref[...]).astype(BF16)

    in_specs = [pl.BlockSpec((tm, D), lambda i: (i, 0)),
                pl.BlockSpec((None, 1, D), lambda i: (group0 + i // bpg, 0, 0)),
                pl.BlockSpec((None, 1, D), lambda i: (group0 + i // bpg, 0, 1)),
                pl.BlockSpec((1, D), lambda i: (0, 0))]
    in_specs = [pl.BlockSpec(memory_space=pl.ANY)] * len(after) + in_specs
    args = list(after) + [x2, mod3, mod3, norm_w]
    alias = {}
    if h_prev is not None:
        in_specs.insert(0, pl.BlockSpec(memory_space=pl.ANY))
        args.insert(0, h_prev)
        alias = {0: 0}
    return pl.pallas_call(
        body, name=name, grid=(rows // tm,), in_specs=in_specs,
        out_specs=pl.BlockSpec((tm, D), lambda i: (rb0 + i, 0)), out_shape=_sds((rows_all, D), BF16),
        input_output_aliases=alias, compiler_params=_params(("parallel",)),
    )(*args)


def _decays(lg, fwd):
    ii = lax.broadcasted_iota(jnp.int32, (CH, CH), 0)
    jj = lax.broadcasted_iota(jnp.int32, (CH, CH), 1)
    ri = lax.broadcasted_iota(jnp.int32, (CH, 1), 0).astype(F32)
    rel = (ii - jj) if fwd else (jj - ii)
    relf = jnp.maximum(rel, 0).astype(F32)
    mask = jnp.where(rel >= 0, jnp.exp(lg * relf), 0.0)
    qe = (ri + 1.0) if fwd else (CH - ri)
    ke = (CH - 1.0 - ri) if fwd else ri
    return mask, relf, jnp.exp(lg * qe), qe, jnp.exp(lg * ke), ke


def _wide_specs(rowf):
    return [pl.BlockSpec((CH, 2 * DK), lambda b, c: (rowf(b, c), RQ // (2 * DK))),
            pl.BlockSpec((CH, 2 * DK), lambda b, c: (rowf(b, c), RQ // (2 * DK) + 1)),
            pl.BlockSpec((CH, RH * DK), lambda b, c: (rowf(b, c), RK // (RH * DK))),
            pl.BlockSpec((CH, 2 * DV), lambda b, c: (rowf(b, c), RV // (2 * DV))),
            pl.BlockSpec((CH, 2 * DV), lambda b, c: (rowf(b, c), RV // (2 * DV) + 1))]


def _head_qkv(refs, h):
    q0, q1, k, v0, v1 = refs
    lo = h % 2
    q = (q0, q1)[h // 2][:, lo * DK:(lo + 1) * DK].astype(F32)
    kk = k[:, h * DK:(h + 1) * DK].astype(F32) * (DK ** -0.5)
    v16 = (v0, v1)[h // 2][:, lo * DV:(lo + 1) * DV].astype(BF16)
    return q, kk, v16


def _ctx_state(px, lg, nb, t_rows, cx):
    rb = t_rows // cx

    def body(lg_ref, k_ref, v_ref, sf_ref, sb_ref):
        h = pl.program_id(1)
        pos = lax.broadcasted_iota(jnp.int32, (cx, 1), 0).astype(F32)
        k = k_ref[...].astype(F32) * (DK ** -0.5)
        v16 = v_ref[...].astype(BF16)
        wf = jnp.exp(lg_ref[0, h] * (cx - 1.0 - pos))
        wb = jnp.exp(lg_ref[1, h] * pos)
        sf_ref[...] = _dot((k * wf).astype(BF16), v16, TN)
        sb_ref[...] = _dot((k * wb).astype(BF16), v16, TN)

    st = pl.BlockSpec((None, None, DK, DV), lambda b, h: (b, h, 0, 0))
    return pl.pallas_call(
        body, name="ctx_state", grid=(nb, RH),
        in_specs=[pl.BlockSpec(memory_space=pltpu.SMEM),
                  pl.BlockSpec((cx, DK), lambda b, h: (rb + b, RK // DK + h)),
                  pl.BlockSpec((cx, DV), lambda b, h: (rb + b, RV // DV + h))],
        out_specs=[st, st], out_shape=[_sds((nb, RH, DK, DV), F32)] * 2,
        compiler_params=_params(("parallel", "parallel")),
    )(lg, px, px)


def _ret_fwd(px, lg, s0f, s0b, nb, nc):
    t_rows = nb * nc * CH

    def body(lg_ref, *refs):
        ins = (refs[0:5], refs[5:10])
        s0f_ref, s0b_ref, of_ref, ob_ref, hf_ref, hb_ref, sf, sb = refs[10:]
        c = pl.program_id(1)

        @pl.when(c == 0)
        def _():
            sf[...] = s0f_ref[...]
            sb[...] = s0b_ref[...]

        for d, (o_ref, h_ref, s) in enumerate(((of_ref, hf_ref, sf), (ob_ref, hb_ref, sb))):
            for h in range(RH):
                lg_d = lg_ref[d, h]
                mask, _, qd, _, kd, _ = _decays(lg_d, d == 0)
                q, k, v16 = _head_qkv(ins[d], h)
                a = _dot(q.astype(BF16), k.astype(BF16), NT)
                st = s[h]
                st16 = st.astype(BF16)
                h_ref[h] = st16
                o = _dot((a * mask).astype(BF16), v16) + _dot((q * qd).astype(BF16), st16)
                o_ref[:, h * DV:(h + 1) * DV] = o.astype(BF16)
                s[h] = st * jnp.exp(lg_d * CH) + _dot((k * kd).astype(BF16), v16, TN)

    def fw(b, c):
        return b * nc + c

    def bw(b, c):
        return b * nc + nc - 1 - c

    st = pl.BlockSpec((None, RH, DK, DV), lambda b, c: (b, 0, 0, 0))
    in_specs = [pl.BlockSpec(memory_space=pltpu.SMEM)] + _wide_specs(fw) + _wide_specs(bw) + [st, st]
    out_specs = [pl.BlockSpec((CH, RH * DV), lambda b, c: (fw(b, c), 0)),
                 pl.BlockSpec((CH, RH * DV), lambda b, c: (bw(b, c), 0)),
                 pl.BlockSpec((None, None, RH, DK, DV), lambda b, c: (b, c, 0, 0, 0)),
                 pl.BlockSpec((None, None, RH, DK, DV), lambda b, c: (b, nc - 1 - c, 0, 0, 0))]
    return pl.pallas_call(
        body, name="ret_fwd", grid=(nb, nc), in_specs=in_specs, out_specs=out_specs,
        out_shape=[_sds((t_rows, RH * DV), BF16)] * 2 + [_sds((nb, nc, RH, DK, DV), BF16)] * 2,
        scratch_shapes=[pltpu.VMEM((RH, DK, DV), F32), pltpu.VMEM((RH, DK, DV), F32)],
        compiler_params=_params(("parallel", "arbitrary")),
    )(lg, *([px] * 10), s0f, s0b)


def _ret_post(o_f, o_b, px, tm):
    t_rows = o_f.shape[0]

    def body(of_ref, ob_ref, g0, g1, g2, g3, y_ref):
        for h, g_ref in enumerate((g0, g1, g2, g3)):
            sl = slice(h * DV, (h + 1) * DV)
            o = of_ref[:, sl].astype(F32) + ob_ref[:, sl].astype(F32)
            r = lax.rsqrt(jnp.mean(o * o, axis=-1, keepdims=True) + EPS)
            y_ref[:, sl] = ((o * r) * _silu(g_ref[...].astype(F32))).astype(BF16)

    def gate(h):
        return pl.BlockSpec((tm, DV), lambda i: (i, RG // DV + h))

    wide = pl.BlockSpec((tm, RH * DV), lambda i: (i, 0))
    return pl.pallas_call(
        body, name="ret_post", grid=(t_rows // tm,),
        in_specs=[wide, wide] + [gate(h) for h in range(RH)],
        out_specs=wide, out_shape=_sds((t_rows, RH * DV), BF16),
        compiler_params=_params(("parallel",)),
    )(o_f, o_b, *([px] * RH))


def _rope_tables(seq):
    rows = seq // GRID_W
    row = np.repeat(np.arange(rows, dtype=np.float32), GRID_W)
    col = np.tile(np.arange(GRID_W, dtype=np.float32), rows)
    half = HD // 2
    freqs = (ROPE_THETA ** (-np.arange(0, half, 2, dtype=np.float32) / half)).astype(np.float32)
    ang = np.concatenate([row[:, None] * freqs, col[:, None] * freqs], axis=-1).astype(np.float32)
    cos = np.repeat(np.cos(ang), 2, axis=-1).astype(np.float32)
    sin = np.repeat(np.sin(ang), 2, axis=-1).astype(np.float32)
    sign = np.tile(np.array([-1.0, 1.0], np.float32), HD // 2)
    return jnp.asarray(cos), jnp.asarray(sin * sign)


def _swap_pairs(v):
    lane = lax.broadcasted_iota(jnp.int32, v.shape, 1)
    return jnp.where((lane & 1) == 0, pltpu.roll(v, HD - 1, 1), pltpu.roll(v, 1, 1))


def _qk_prep(px, nw, cos, sin, rows, row_off, col_off, heads, hb, seq, tm, name):
    rope = cos is not None
    rb0 = row_off // tm
    pb = seq // tm if rope else 1
    bw = hb * HD

    def body(*refs):
        if rope:
            x_ref, w_ref, c_ref, s_ref, o_ref = refs
        else:
            x_ref, w_ref, o_ref = refs
        for h in range(hb):
            sl = slice(h * HD, (h + 1) * HD)
            xv = x_ref[:, sl].astype(F32)
            r = lax.rsqrt(jnp.mean(xv * xv, axis=-1, keepdims=True) + EPS)
            t = (xv * r) * w_ref[...]
            if rope:
                t = t * c_ref[...] + _swap_pairs(t) * s_ref[...]
            o_ref[:, sl] = t.astype(BF16)

    in_specs = [pl.BlockSpec((tm, bw), lambda i, j: (rb0 + i, col_off // bw + j)),
                pl.BlockSpec((1, HD), lambda i, j: (0, 0))]
    args = [px, nw]
    if rope:
        in_specs += [pl.BlockSpec((tm, HD), lambda i, j: (i % pb, 0))] * 2
        args += [cos, sin]
    return pl.pallas_call(
        body, name=name, grid=(rows // tm, heads // hb), in_specs=in_specs,
        out_specs=pl.BlockSpec((tm, bw), lambda i, j: (i, j)), out_shape=_sds((rows, heads * HD), BF16),
        compiler_params=_params(("parallel", "parallel")),
    )(*args)


def _att_fwd(q16, k_all, v_all, px, nb, seq, tq):
    t_rows = nb * seq
    keys = k_all.shape[1]
    nq = seq // tq
    rep = HQ // HKV
    gw = rep * HD

    def body(q_ref, k_ref, v_ref, g_ref, o_ref, y_ref, l_ref):
        k = k_ref[...]
        v = v_ref[...]
        l_ref[...] = jnp.zeros_like(l_ref)
        for r in range(rep):
            sl = slice(r * HD, (r + 1) * HD)
            s = _dot(q_ref[:, sl], k, NT)
            m = jnp.max(s, axis=-1, keepdims=True)
            e = jnp.exp2((s - m) * SM_C)
            tot = jnp.sum(e, axis=-1, keepdims=True)
            o = _dot(e.astype(BF16), v) * (1.0 / tot)
            o_ref[:, sl] = o
            y_ref[:, sl] = (o * _silu(g_ref[:, sl].astype(F32))).astype(BF16)
            l_ref[:, r:r + 1] = m * SM_C + jnp.log(tot) * float(np.log2(np.e))

    qblk = pl.BlockSpec((tq, gw), lambda b, g, i: (b * nq + i, g))
    kvb = pl.BlockSpec((None, keys, HD), lambda b, g, i: (b, 0, g))
    return pl.pallas_call(
        body, name="att_fwd", grid=(nb, HKV, nq),
        in_specs=[qblk, kvb, kvb, pl.BlockSpec((tq, gw), lambda b, g, i: (b * nq + i, AG // gw + g))],
        out_specs=[qblk, qblk, pl.BlockSpec((tq, 128), lambda b, g, i: (b * nq + i, g))],
        out_shape=[_sds((t_rows, D), F32), _sds((t_rows, D), BF16), _sds((t_rows, HKV * 128), F32)],
        compiler_params=_params(("parallel", "parallel", "parallel")),
    )(q16, k_all, v_all, px)


def _gate_specs(tm, col0):
    hw = D // 2
    return [pl.BlockSpec((tm, hw), lambda i: (i, col0 // hw)), pl.BlockSpec((tm, hw), lambda i: (i, col0 // hw + 1))]


def _merge(yret16, yatt16, px, w_o_ret16, w_o_att16, tm):
    t_rows = yret16.shape[0]
    hw = D // 2

    def body(yr_ref, wr_ref, ya_ref, wa_ref, mr0, mr1, ma0, ma1, ar_ref, aa_ref, y_ref):
        ar = _dot(yr_ref[...], wr_ref[...])
        aa = _dot(ya_ref[...], wa_ref[...])
        ar_ref[...] = ar.astype(BF16)
        aa_ref[...] = aa.astype(BF16)
        for j, (mr_ref, ma_ref) in enumerate(((mr0, ma0), (mr1, ma1))):
            sl = slice(j * hw, (j + 1) * hw)
            y_ref[:, sl] = (_sig(mr_ref[...].astype(F32)) * ar[:, sl]
                            + _sig(ma_ref[...].astype(F32)) * aa[:, sl]).astype(BF16)

    row = pl.BlockSpec((tm, D), lambda i: (i, 0))
    return pl.pallas_call(
        body, name="merge", grid=(t_rows // tm,),
        in_specs=[pl.BlockSpec((tm, RH * DV), lambda i: (i, 0)), pl.BlockSpec((RH * DV, D), lambda i: (0, 0)),
                  row, pl.BlockSpec((D, D), lambda i: (0, 0))] + _gate_specs(tm, MR) + _gate_specs(tm, MA),
        out_specs=[row, row, row], out_shape=[_sds((t_rows, D), BF16)] * 3,
        compiler_params=_params(("parallel",)),
    )(yret16, w_o_ret16, yatt16, w_o_att16, px, px, px, px)


def _outproj(y16, w_out16, x2, tgt, mod3, nb, seq, tm):
    t_rows = nb * seq
    bpb = seq // tm

    def body(y_ref, w_ref, x_ref, t_ref, g_ref, dxn_ref, dout_ref, dg_ref, loss_ref):
        i = pl.program_id(1)
        out = _dot(y_ref[...], w_ref[...])
        gate = g_ref[...]
        diff = x_ref[...] + gate * out - t_ref[...]
        dxn = diff * (1.0 / D)
        dxn_ref[...] = dxn
        dout_ref[...] = (gate * dxn).astype(BF16)
        dg = jnp.sum(dxn * out, axis=0, keepdims=True)
        ls = jnp.broadcast_to(jnp.sum(diff * diff) * (0.5 / D), (1, 128))

        @pl.when(i == 0)
        def _():
            dg_ref[...] = dg
            loss_ref[...] = ls

        @pl.when(i > 0)
        def _():
            dg_ref[...] += dg
            loss_ref[...] += ls

    row = pl.BlockSpec((tm, D), lambda b, i: (b * bpb + i, 0))
    return pl.pallas_call(
        body, name="outproj", grid=(nb, bpb),
        in_specs=[row, pl.BlockSpec((D, D), lambda b, i: (0, 0)), row, row,
                  pl.BlockSpec((None, 1, D), lambda b, i: (b, 0, 2))],
        out_specs=[row, row, pl.BlockSpec((None, 1, D), lambda b, i: (b, 0, 0)),
                   pl.BlockSpec((None, 1, 128), lambda b, i: (b, 0, 0))],
        out_shape=[_sds((t_rows, D), F32), _sds((t_rows, D), BF16), _sds((nb, 1, D), F32), _sds((nb, 1, 128), F32)],
        compiler_params=_params(("parallel", "arbitrary")),
    )(y16, w_out16, x2, tgt, mod3)


def _bwd_merge(dout16, w_out16, px, a_ret, a_att, tm):
    t_rows = dout16.shape[0]
    hw = D // 2

    def body(do_ref, w_ref, mr0, mr1, ma0, ma1, ar_ref, aa_ref, dar_ref, daa_ref, dmr_ref, dma_ref):
        dy_all = _dot(do_ref[...], w_ref[...], NT)
        for j, (mr_ref, ma_ref) in enumerate(((mr0, ma0), (mr1, ma1))):
            sl = slice(j * hw, (j + 1) * hw)
            dy = dy_all[:, sl]
            sr = _sig(mr_ref[...].astype(F32))
            sa = _sig(ma_ref[...].astype(F32))
            dar_ref[:, sl] = (dy * sr).astype(BF16)
            daa_ref[:, sl] = (dy * sa).astype(BF16)
            dmr_ref[:, sl] = (dy * ar_ref[:, sl].astype(F32) * sr * (1.0 - sr)).astype(BF16)
            dma_ref[:, sl] = (dy * aa_ref[:, sl].astype(F32) * sa * (1.0 - sa)).astype(BF16)

    row = pl.BlockSpec((tm, D), lambda i: (i, 0))
    return pl.pallas_call(
        body, name="bwd_merge", grid=(t_rows // tm,),
        in_specs=[row, pl.BlockSpec((D, D), lambda i: (0, 0))] + _gate_specs(tm, MR) + _gate_specs(tm, MA) + [row, row],
        out_specs=[row] * 4, out_shape=[_sds((t_rows, D), BF16)] * 4,
        compiler_params=_params(("parallel",)),
    )(dout16, w_out16, px, px, px, px, a_ret, a_att)


def _bwd_branch_ret(da_ret16, w_o_ret16, px, o_f, o_b, tm, after=()):
    t_rows = da_ret16.shape[0]

    def body(da_ref, w_ref, g0, g1, g2, g3, of_ref, ob_ref, *rest):
        do_ref, dg_ref = rest[-2:]
        da = da_ref[...]
        for h, g_ref in enumerate((g0, g1, g2, g3)):
            sl = slice(h * DV, (h + 1) * DV)
            dy = _dot(da, w_ref[sl, :], NT)
            g = g_ref[...].astype(F32)
            o = of_ref[:, sl].astype(F32) + ob_ref[:, sl].astype(F32)
            r = lax.rsqrt(jnp.mean(o * o, axis=-1, keepdims=True) + EPS)
            on = o * r
            sg = _sig(g)
            don = dy * (g * sg)
            dg_ref[:, sl] = (dy * on * (sg * (1.0 + g * (1.0 - sg)))).astype(BF16)
            do_ref[:, sl] = (r * (don - on * jnp.mean(on * don, axis=-1, keepdims=True))).astype(BF16)

    def gate(h):
        return pl.BlockSpec((tm, DV), lambda i: (i, RG // DV + h))

    wide = pl.BlockSpec((tm, RH * DV), lambda i: (i, 0))
    return pl.pallas_call(
        body, name="bwd_branch_ret", grid=(t_rows // tm,),
        in_specs=[pl.BlockSpec((tm, D), lambda i: (i, 0)), pl.BlockSpec((RH * DV, D), lambda i: (0, 0))]
        + [gate(h) for h in range(RH)] + [wide, wide] + [pl.BlockSpec(memory_space=pl.ANY)] * len(after),
        out_specs=[wide, wide], out_shape=[_sds((t_rows, RH * DV), BF16)] * 2,
        compiler_params=_params(("parallel",)),
    )(da_ret16, w_o_ret16, *([px] * RH), o_f, o_b, *after)


def _bwd_branch_att(da_att16, w_o_att16, px, o_att, tm):
    t_rows = da_att16.shape[0]
    hw = D // 2

    def body(da_ref, w_ref, g0, g1, o_ref, dao_ref, dg_ref):
        dy_all = _dot(da_ref[...], w_ref[...], NT)
        for j, g_ref in enumerate((g0, g1)):
            sl = slice(j * hw, (j + 1) * hw)
            dy = dy_all[:, sl]
            g = g_ref[...].astype(F32)
            sg = _sig(g)
            dao_ref[:, sl] = dy * (g * sg)
            dg_ref[:, sl] = (dy * o_ref[:, sl] * (sg * (1.0 + g * (1.0 - sg)))).astype(BF16)

    row = pl.BlockSpec((tm, D), lambda i: (i, 0))
    return pl.pallas_call(
        body, name="bwd_branch_att", grid=(t_rows // tm,),
        in_specs=[row, pl.BlockSpec((D, D), lambda i: (0, 0))] + _gate_specs(tm, AG) + [row],
        out_specs=[row, row], out_shape=[_sds((t_rows, D), F32), _sds((t_rows, D), BF16)],
        compiler_params=_params(("parallel",)),
    )(da_att16, w_o_att16, px, px, o_att)


def _att_bwd(q16, k_all, v_all, dao, o_att, lse, nb, seq, tq):
    t_rows = nb * seq
    keys = k_all.shape[1]
    nq = seq // tq
    rep = HQ // HKV
    gw = rep * HD
    scale = HD ** -0.5

    def body(q_ref, k_ref, v_ref, dao_ref, o_ref, l_ref, dq_ref, dk_ref, dv_ref):
        i = pl.program_id(2)
        k = k_ref[...]
        v = v_ref[...]
        dk = jnp.zeros((keys, HD), F32)
        dv = jnp.zeros((keys, HD), F32)
        for r in range(rep):
            sl = slice(r * HD, (r + 1) * HD)
            q = q_ref[:, sl]
            p = jnp.exp2(_dot(q, k, NT) * SM_C - l_ref[:, r:r + 1])
            da = dao_ref[:, sl]
            da16 = da.astype(BF16)
            delta = jnp.sum(da * o_ref[:, sl], axis=-1, keepdims=True)
            ds = (p * (_dot(da16, v, NT) - delta)).astype(BF16)
            dq_ref[:, sl] = _dot(ds, k) * scale
            dk += _dot(ds, q, TN)
            dv += _dot(p.astype(BF16), da16, TN)
        dk = dk * scale

        @pl.when(i == 0)
        def _():
            dk_ref[...] = dk
            dv_ref[...] = dv

        @pl.when(i > 0)
        def _():
            dk_ref[...] += dk
            dv_ref[...] += dv

    qblk = pl.BlockSpec((tq, gw), lambda b, g, i: (b * nq + i, g))
    kvb = pl.BlockSpec((None, keys, HD), lambda b, g, i: (b, 0, g))
    return pl.pallas_call(
        body, name="att_bwd", grid=(nb, HKV, nq),
        in_specs=[qblk, kvb, kvb, qblk, qblk, pl.BlockSpec((tq, 128), lambda b, g, i: (b * nq + i, g))],
        out_specs=[qblk, kvb, kvb],
        out_shape=[_sds((t_rows, D), F32), _sds((nb, keys, HKV * HD), F32), _sds((nb, keys, HKV * HD), F32)],
        compiler_params=_params(("parallel", "parallel", "arbitrary")),
    )(q16, k_all, v_all, dao, o_att, lse)


def _qk_prep_bwd(dt, px, nw, cos, sin, rows, row_off, col_off, heads, hb, seq, tm, name):
    rope = cos is not None
    rb0 = row_off // tm
    pb = seq // tm if rope else 1
    bw = hb * HD

    def body(*refs):
        if rope:
            d_ref, x_ref, w_ref, c_ref, s_ref, dx_ref, dw_ref = refs
        else:
            d_ref, x_ref, w_ref, dx_ref, dw_ref = refs
        first = jnp.logical_and(pl.program_id(0) == 0, pl.program_id(1) == 0)
        dw = jnp.zeros((1, HD), F32)
        for h in range(hb):
            sl = slice(h * HD, (h + 1) * HD)
            dtv = d_ref[:, sl]
            if rope:
                dtv = dtv * c_ref[...] + _swap_pairs(dtv * s_ref[...])
            xv = x_ref[:, sl].astype(F32)
            r = lax.rsqrt(jnp.mean(xv * xv, axis=-1, keepdims=True) + EPS)
            xh = xv * r
            dxh = dtv * w_ref[...]
            dx_ref[:, sl] = (r * (dxh - xh * jnp.mean(dxh * xh, axis=-1, keepdims=True))).astype(BF16)
            dw += jnp.sum(dtv * xh, axis=0, keepdims=True)

        @pl.when(first)
        def _():
            dw_ref[...] = dw

        @pl.when(jnp.logical_not(first))
        def _():
            dw_ref[...] += dw

    blk = pl.BlockSpec((tm, bw), lambda i, j: (i, j))
    in_specs = [blk, pl.BlockSpec((tm, bw), lambda i, j: (rb0 + i, col_off // bw + j)),
                pl.BlockSpec((1, HD), lambda i, j: (0, 0))]
    args = [dt, px, nw]
    if rope:
        in_specs += [pl.BlockSpec((tm, HD), lambda i, j: (i % pb, 0))] * 2
        args += [cos, sin]
    return pl.pallas_call(
        body, name=name, grid=(rows // tm, heads // hb), in_specs=in_specs,
        out_specs=[blk, pl.BlockSpec((1, HD), lambda i, j: (0, 0))],
        out_shape=[_sds((rows, heads * HD), BF16), _sds((1, HD), F32)],
        compiler_params=_params(("arbitrary", "arbitrary")),
    )(*args)


def _ret_bwd(px, lg, do16, hist_f, hist_b, nb, nc):
    t_rows = nb * nc * CH

    def body(lg_ref, *refs):
        ins = (refs[0:5], refs[7:12])
        do_refs = (refs[5], refs[12])
        h_refs = (refs[6], refs[13])
        outs = (refs[14:17], refs[17:20])
        ds_outs = (refs[20], refs[21])
        dlg_ref = refs[22]
        dss = (refs[23], refs[24])
        c = pl.program_id(1)

        @pl.when(c == 0)
        def _():
            dss[0][...] = jnp.zeros_like(dss[0])
            dss[1][...] = jnp.zeros_like(dss[1])
            dlg_ref[...] = jnp.zeros_like(dlg_ref)

        for d in range(2):
            dq_ref, dk_ref, dv_ref = outs[d]
            for h in range(RH):
                lg_d = lg_ref[d, h]
                mask, relf, qd, qe, kd, ke = _decays(lg_d, d == 0)
                g_ch = jnp.exp(lg_d * CH)
                q, k, v16 = _head_qkv(ins[d], h)
                q16 = q.astype(BF16)
                k16 = k.astype(BF16)
                do16v = do_refs[d][:, h * DV:(h + 1) * DV]
                st16 = h_refs[d][h]
                dst = dss[d][h]
                dst16 = dst.astype(BF16)
                a = _dot(q16, k16, NT) * mask
                dp = _dot(do16v, v16, NT)
                da16 = (dp * mask).astype(BF16)
                dq_cross = _dot(do16v, st16, NT) * qd
                dq_ref[:, h * DK:(h + 1) * DK] = (_dot(da16, k16) + dq_cross).astype(BF16)
                dk_state = _dot(v16, dst16, NT) * kd
                dk_ref[:, h * DK:(h + 1) * DK] = ((_dot(da16, q16, TN) + dk_state) * (DK ** -0.5)).astype(BF16)
                dv = _dot(a.astype(BF16), do16v, TN) + _dot((k * kd).astype(BF16), dst16)
                dv_ref[:, h * DV:(h + 1) * DV] = dv.astype(BF16)
                dlg = (jnp.sum(relf * a * dp)
                       + jnp.sum(qe * jnp.sum(q * dq_cross, axis=-1, keepdims=True))
                       + jnp.sum(ke * jnp.sum(k * dk_state, axis=-1, keepdims=True))
                       + CH * g_ch * jnp.sum(dst * st16.astype(F32)))
                row = d * RH + h
                dlg_ref[row:row + 1, :] += jnp.broadcast_to(dlg, (1, 128))
                ds_new = g_ch * dst + _dot((q * qd).astype(BF16), do16v, TN)
                dss[d][h] = ds_new

                @pl.when(c == nc - 1)
                def _():
                    ds_outs[d][h] = ds_new

    def fw(b, c):
        return b * nc + nc - 1 - c

    def bw(b, c):
        return b * nc + c

    def rows(rowf, width):
        return pl.BlockSpec((CH, width), lambda b, c: (rowf(b, c), 0))

    def hist(rowf):
        return pl.BlockSpec((None, None, RH, DK, DV), lambda b, c: (b, rowf(0, c), 0, 0, 0))

    st = pl.BlockSpec((None, RH, DK, DV), lambda b, c: (b, 0, 0, 0))
    in_specs = [pl.BlockSpec(memory_space=pltpu.SMEM)]
    out_specs = []
    for rowf in (fw, bw):
        in_specs += _wide_specs(rowf) + [rows(rowf, RH * DV), hist(rowf)]
        out_specs += [rows(rowf, RH * DK), rows(rowf, RH * DK), rows(rowf, RH * DV)]
    out_specs += [st, st, pl.BlockSpec((None, 8, 128), lambda b, c: (b, 0, 0))]
    qk = _sds((t_rows, RH * DK), BF16)
    vv = _sds((t_rows, RH * DV), BF16)
    return pl.pallas_call(
        body, name="ret_bwd", grid=(nb, nc), in_specs=in_specs, out_specs=out_specs,
        out_shape=[qk, qk, vv, qk, qk, vv, _sds((nb, RH, DK, DV), F32), _sds((nb, RH, DK, DV), F32),
                   _sds((nb, 8, 128), F32)],
        scratch_shapes=[pltpu.VMEM((RH, DK, DV), F32), pltpu.VMEM((RH, DK, DV), F32)],
        compiler_params=_params(("parallel", "arbitrary")),
    )(lg, *([px] * 5), do16, hist_f, *([px] * 5), do16, hist_b)


def _ctx_state_bwd(px, lg, ds_f, ds_b, nb, t_rows, cx):
    rb = t_rows // cx

    def body(lg_ref, k_ref, v_ref, dsf_ref, dsb_ref, dk_ref, dv_ref, dlg_ref):
        h = pl.program_id(1)
        pos = lax.broadcasted_iota(jnp.int32, (cx, 1), 0).astype(F32)
        k = k_ref[...].astype(F32) * (DK ** -0.5)
        v16 = v_ref[...].astype(BF16)
        dk = jnp.zeros((cx, DK), F32)
        dv = jnp.zeros((cx, DV), F32)
        dlg_ref[...] = jnp.zeros_like(dlg_ref)
        for d, (ds_ref, e) in enumerate(((dsf_ref, cx - 1.0 - pos), (dsb_ref, pos))):
            w = jnp.exp(lg_ref[d, h] * e)
            ds16 = ds_ref[...].astype(BF16)
            t = _dot(v16, ds16, NT)
            dk += t * w
            dv += _dot((k * w).astype(BF16), ds16)
            dlg = jnp.sum(e * w * jnp.sum(k * t, axis=-1, keepdims=True))
            dlg_ref[d:d + 1, :] = jnp.broadcast_to(dlg, (1, 128))
        dk_ref[...] = (dk * (DK ** -0.5)).astype(BF16)
        dv_ref[...] = dv.astype(BF16)

    st = pl.BlockSpec((None, None, DK, DV), lambda b, h: (b, h, 0, 0))
    return pl.pallas_call(
        body, name="ctx_state_bwd", grid=(nb, RH),
        in_specs=[pl.BlockSpec(memory_space=pltpu.SMEM),
                  pl.BlockSpec((cx, DK), lambda b, h: (rb + b, RK // DK + h)),
                  pl.BlockSpec((cx, DV), lambda b, h: (rb + b, RV // DV + h)), st, st],
        out_specs=[pl.BlockSpec((cx, DK), lambda b, h: (b, h)), pl.BlockSpec((cx, DV), lambda b, h: (b, h)),
                   pl.BlockSpec((None, None, 8, 128), lambda b, h: (b, h, 0, 0))],
        out_shape=[_sds((nb * cx, RH * DK), BF16), _sds((nb * cx, RH * DV), BF16), _sds((nb, RH, 8, 128), F32)],
        compiler_params=_params(("parallel", "parallel")),
    )(lg, px, px, ds_f, ds_b)


def _assemble_lat(rows_all, dk_f, dk_b, dv_f, dv_b, dak16, dvx, dq_f, dq_b, drg16, daq16, dag16, dmr16, dma16, tm):
    t_rows = dk_f.shape[0]

    def body(dkf, dkb, dvf, dvb, dak, dav, dqf, dqb, drg, daq, dag, dmr, dma, o_ref):
        o_ref[:, RK:RK + RH * DK] = (dkf[...].astype(F32) + dkb[...].astype(F32)).astype(BF16)
        o_ref[:, RV:RV + RH * DV] = (dvf[...].astype(F32) + dvb[...].astype(F32)).astype(BF16)
        o_ref[:, AK:AK + HKV * HD] = dak[...]
        o_ref[:, AV:AV + HKV * HD] = dav[...].astype(BF16)
        o_ref[:, RQ:RQ + RH * DK] = (dqf[...].astype(F32) + dqb[...].astype(F32)).astype(BF16)
        o_ref[:, RG:RG + RH * DV] = drg[...]
        o_ref[:, AQ:AQ + D] = daq[...]
        o_ref[:, AG:AG + D] = dag[...]
        o_ref[:, MR:MR + D] = dmr[...]
        o_ref[:, MA:MA + D] = dma[...]

    args = (dk_f, dk_b, dv_f, dv_b, dak16, dvx, dq_f, dq_b, drg16, daq16, dag16, dmr16, dma16)
    return pl.pallas_call(
        body, name="assemble_lat", grid=(t_rows // tm,),
        in_specs=[pl.BlockSpec((tm, a.shape[1]), lambda i: (i, 0)) for a in args],
        out_specs=pl.BlockSpec((tm, IN_COLS), lambda i: (i, 0)), out_shape=_sds((rows_all, IN_COLS), BF16),
        compiler_params=_params(("parallel",)),
    )(*args)


def _assemble_ctx(dp_all, dck16, dcv16, dcak16, dvc, t_rows, tm):
    c_rows = dck16.shape[0]
    rb = t_rows // tm

    def body(_, dck, dcv, dcak, dcav, o_ref):
        o_ref[:, RK:RK + RH * DK] = dck[...]
        o_ref[:, RV:RV + RH * DV] = dcv[...]
        o_ref[:, AK:AK + HKV * HD] = dcak[...]
        o_ref[:, AV:AV + HKV * HD] = dcav[...].astype(BF16)
        o_ref[:, KV_COLS:] = jnp.zeros((tm, IN_COLS - KV_COLS), BF16)

    args = (dck16, dcv16, dcak16, dvc)
    return pl.pallas_call(
        body, name="assemble_ctx", grid=(c_rows // tm,),
        in_specs=[pl.BlockSpec(memory_space=pl.ANY)]
        + [pl.BlockSpec((tm, a.shape[1]), lambda i: (i, 0)) for a in args],
        out_specs=pl.BlockSpec((tm, IN_COLS), lambda i: (rb + i, 0)), out_shape=_sds(dp_all.shape, BF16),
        input_output_aliases={0: 0},
        compiler_params=_params(("parallel",)),
    )(dp_all, *args)


def _norm_bwd(dh, x2, mod3, norm_w, dxn, row_off, rows_per_group, group0, tm, name):
    with_dx = dxn is not None
    rows = x2.shape[0]
    rb0 = row_off // tm
    bpg = rows_per_group // tm
    ngroups = rows // rows_per_group

    def body(*refs):
        if with_dx:
            dh_ref, x_ref, sc_ref, nw_ref, dxn_ref, dx_ref, dsh_ref, dsc_ref, dnw_ref = refs
        else:
            dh_ref, x_ref, sc_ref, nw_ref, dsh_ref, dsc_ref, dnw_ref = refs
        i = pl.program_id(0)
        dhv = dh_ref[...]
        xv = x_ref[...]
        nw = nw_ref[...]
        r = lax.rsqrt(jnp.mean(xv * xv, axis=-1, keepdims=True) + EPS)
        xh = xv * r
        dm = dhv * (1.0 + sc_ref[...])
        dsh = jnp.sum(dhv, axis=0, keepdims=True)
        dsc = jnp.sum(dhv * (xh * nw), axis=0, keepdims=True)
        dnw = jnp.sum(dm * xh, axis=0, keepdims=True)
        if with_dx:
            dxh = dm * nw
            dx_ref[...] = dxn_ref[...] + r * (dxh - xh * jnp.mean(dxh * xh, axis=-1, keepdims=True))

        @pl.when(i % bpg == 0)
        def _():
            dsh_ref[...] = dsh
            dsc_ref[...] = dsc

        @pl.when(i % bpg != 0)
        def _():
            dsh_ref[...] += dsh
            dsc_ref[...] += dsc

        @pl.when(i == 0)
        def _():
            dnw_ref[...] = dnw

        @pl.when(i > 0)
        def _():
            dnw_ref[...] += dnw

    grp = pl.BlockSpec((None, 1, D), lambda i: (i // bpg, 0, 0))
    in_specs = [pl.BlockSpec((tm, D), lambda i: (rb0 + i, 0)), pl.BlockSpec((tm, D), lambda i: (i, 0)),
                pl.BlockSpec((None, 1, D), lambda i: (group0 + i // bpg, 0, 1)),
                pl.BlockSpec((1, D), lambda i: (0, 0))]
    args = [dh, x2, mod3, norm_w]
    out_specs = [grp, grp, pl.BlockSpec((1, D), lambda i: (0, 0))]
    out_shape = [_sds((ngroups, 1, D), F32), _sds((ngroups, 1, D), F32), _sds((1, D), F32)]
    if with_dx:
        in_specs.append(pl.BlockSpec((tm, D), lambda i: (i, 0)))
        args.append(dxn)
        out_specs.insert(0, pl.BlockSpec((tm, D), lambda i: (i, 0)))
        out_shape.insert(0, _sds((rows, D), F32))
    return pl.pallas_call(
        body, name=name, grid=(rows // tm,), in_specs=in_specs, out_specs=out_specs, out_shape=out_shape,
        compiler_params=_params(("arbitrary",)),
    )(*args)


def _small_final(dmod_all, dmodc_parts, c_rows, dm_loc_rows, nw_parts, misc_parts, c_ctx, r_pad, w_ada16):
    loc = dm_loc_rows.shape[1]

    def body(dm_ref, dmc_ref, c_ref, dml_ref, nwp_ref, mp_ref, cc_ref, r_ref, w_ref,
             gb_ref, gc_ref, gnw_ref, misc_ref, gwa_ref):
        dmc = jnp.sum(dmc_ref[...], axis=0, keepdims=True)
        gb_ref[...] = jnp.sum(dm_ref[...], axis=0, keepdims=True) + dmc
        dsc = _dot(jnp.broadcast_to(dmc, (8, 3 * D)).astype(BF16), w_ref[...], NT)[0:1, :]
        gc_ref[...] = dsc * _dsilu(cc_ref[...])
        gnw_ref[...] = jnp.sum(nwp_ref[...], axis=0, keepdims=True)
        misc = jnp.sum(mp_ref[...], axis=0, keepdims=True)
        y = jnp.exp2(r_ref[...])
        lane = lax.broadcasted_iota(jnp.int32, (1, D), 1)
        is_decay = jnp.logical_and(lane >= 2 * HD, lane < 2 * HD + 2 * RH)
        misc_ref[...] = misc * jnp.where(is_decay, -(y * np.float32(np.log(2.0))) / (1.0 - y), 1.0)
        gwa_ref[...] = _dot(_silu(c_ref[...]).astype(BF16), dml_ref[...].astype(BF16), TN)

    return pl.pallas_call(
        body, name="small_final",
        out_shape=[_sds((1, 3 * D), F32), _sds((1, D), F32), _sds((1, D), F32), _sds((1, D), F32), _sds((D, loc), F32)],
        compiler_params=pltpu.CompilerParams(vmem_limit_bytes=VMEM_LIMIT),
    )(dmod_all, dmodc_parts, c_rows, dm_loc_rows, nw_parts, misc_parts, c_ctx, r_pad, w_ada16)


def _adamw(w, g, m, v, name):
    rows, cols = w.shape
    tm = _pick(rows, 448, 8)
    bc1 = 1.0 - B1 ** STEP
    bc2 = 1.0 - B2 ** STEP

    def body(w_ref, g_ref, m_ref, v_ref, d_ref, nm_ref, nv_ref):
        g_ = g_ref[...]
        nm = B1 * m_ref[...] + (1.0 - B1) * g_
        nv = B2 * v_ref[...] + (1.0 - B2) * (g_ * g_)
        nm_ref[...] = nm
        nv_ref[...] = nv
        d_ref[...] = -LR * ((nm / bc1) / (jnp.sqrt(nv / bc2) + ADAM_EPS) + WD * w_ref[...])

    blk = pl.BlockSpec((tm, cols), lambda i: (i, 0))
    return pl.pallas_call(
        body, name=name, grid=(rows // tm,), in_specs=[blk] * 4, out_specs=[blk] * 3,
        out_shape=[_sds((rows, cols), F32)] * 3, compiler_params=_params(("parallel",)),
    )(w, g, m, v)


def _mesh_pos():
    return lax.axis_index("x"), lax.axis_index("y"), lax.axis_index("c")


def _all_gather(arrs, name):
    n = len(arrs)

    def body(*refs):
        ins, outs = refs[:n], refs[n:2 * n]
        send_sems, recv_sems, local_sems = refs[2 * n:]
        x, y, c = _mesh_pos()
        me, sib = (x, y, c), (x, y, 1 - c)
        chips = [(1 - x, y), (x, 1 - y), (1 - x, 1 - y)]

        def slot(p):
            return 4 * p[0] + 2 * p[1] + p[2]

        def copy(a, k, block, to, own):
            dst = outs[a].at[slot(block)]
            return pltpu.make_async_remote_copy(
                src_ref=ins[a] if own else dst, dst_ref=dst, send_sem=send_sems.at[a, k], recv_sem=recv_sems.at[a, k],
                device_id=to, device_id_type=MESH_T)

        mine = [pltpu.make_async_copy(ins[a], outs[a].at[slot(me)], local_sems.at[a]) for a in range(n)]
        for cp in mine:
            cp.start()
        first = []
        for a in range(n):
            first.append(copy(a, 0, me, sib, True))
            first += [copy(a, 1 + j, me, (*chip, c), True) for j, chip in enumerate(chips)]
        for cp in first:
            cp.start()
        passed = []
        for j, chip in enumerate(chips):
            for a in range(n):
                copy(a, 1 + j, (*chip, c), me, False).wait_recv()
                fwd = copy(a, 4 + j, (*chip, c), sib, False)
                fwd.start()
                passed.append(fwd)
        for a in range(n):
            copy(a, 0, sib, me, False).wait_recv()
            for j, chip in enumerate(chips):
                copy(a, 4 + j, (*chip, 1 - c), me, False).wait_recv()
        for cp in first + passed:
            cp.wait_send()
        for cp in mine:
            cp.wait()

    hbm = pl.BlockSpec(memory_space=pl.ANY)
    return pl.pallas_call(
        body, name=name, in_specs=[hbm] * n, out_specs=[hbm] * n,
        out_shape=[_sds((N_DEV,) + a.shape, a.dtype) for a in arrs],
        scratch_shapes=[pltpu.SemaphoreType.DMA((n, 7)), pltpu.SemaphoreType.DMA((n, 7)), pltpu.SemaphoreType.DMA((n,))],
    )(*arrs)


def _pair_exchange(arrs, name):
    n = len(arrs)

    def body(*refs):
        ins, outs = refs[:n], refs[n:2 * n]
        send_sems, recv_sems = refs[2 * n:]
        x, y, c = _mesh_pos()
        sib = (x, y, 1 - c)
        sends = []
        for a in range(n):
            for k in range(4):
                sends.append(pltpu.make_async_remote_copy(
                    src_ref=ins[a].at[2 * k + 1 - c], dst_ref=outs[a].at[k], send_sem=send_sems.at[a, k],
                    recv_sem=recv_sems.at[a, k], device_id=sib, device_id_type=MESH_T))
        for cp in sends:
            cp.start()
        for cp in sends:
            cp.wait_recv()
        for cp in sends:
            cp.wait_send()

    hbm = pl.BlockSpec(memory_space=pl.ANY)
    return pl.pallas_call(
        body, name=name, in_specs=[hbm] * n, out_specs=[hbm] * n,
        out_shape=[_sds((4,) + a.shape[1:], a.dtype) for a in arrs],
        scratch_shapes=[pltpu.SemaphoreType.DMA((n, 4)), pltpu.SemaphoreType.DMA((n, 4))],
    )(*arrs)


def _pair_add(part, got, core, name):
    _, rows, cols = part.shape
    tm = _pick(rows, 672, 16)
    p4 = part.reshape(4, 2, rows, cols)

    def body(core_ref, p_ref, g_ref, o_ref):
        o_ref[...] = (p_ref[...].astype(F32) + g_ref[...].astype(F32)).astype(BF16)

    blk = pl.BlockSpec((None, tm, cols), lambda k, i, cr: (k, i, 0))
    return pl.pallas_call(
        body, name=name,
        grid_spec=pltpu.PrefetchScalarGridSpec(
            num_scalar_prefetch=1, grid=(4, rows // tm),
            in_specs=[pl.BlockSpec((None, None, tm, cols), lambda k, i, cr: (k, cr[0], i, 0)), blk], out_specs=blk),
        out_shape=_sds((4, rows, cols), BF16), compiler_params=_params(("parallel", "parallel")),
    )(core, p4, got)


def _chip_sum(pair_sums, landed, chip, name):
    _, rows, cols = pair_sums.shape
    tm = _pick(rows, 672, 16)

    def body(chip_ref, s_ref, l_ref, o_ref):
        acc = s_ref[...].astype(F32)
        for j in range(3):
            acc = acc + l_ref[j].astype(F32)
        o_ref[...] = acc

    return pl.pallas_call(
        body, name=name,
        grid_spec=pltpu.PrefetchScalarGridSpec(
            num_scalar_prefetch=1, grid=(rows // tm,),
            in_specs=[pl.BlockSpec((None, tm, cols), lambda i, ch: (ch[0], i, 0)),
                      pl.BlockSpec((3, tm, cols), lambda i, ch: (0, i, 0))],
            out_specs=pl.BlockSpec((tm, cols), lambda i, ch: (i, 0))),
        out_shape=_sds((rows, cols), F32), compiler_params=_params(("parallel",)),
    )(chip, pair_sums, landed)


_HBM = pl.BlockSpec(memory_space=pltpu.HBM)
_SEM = pl.BlockSpec(memory_space=pltpu.SEMAPHORE)
_EFFECT = pltpu.SideEffectType.DATAFLOW_SIDE_EFFECTING


def _chip_routes(n):
    def plan(x, y, c):
        routes = []
        for a in range(n):
            for j in range(1, 4):
                px, py = x ^ (j >> 1), y ^ (j & 1)
                routes.append((a, 2 * px + py, (px, py, c), j - 1))
        return routes
    return plan, 3 * n


def _bcast_routes(n):
    def plan(x, y, c):
        routes = []
        for a in range(n):
            for k in range(1, N_DEV):
                peer = (x ^ ((k >> 2) & 1), y ^ ((k >> 1) & 1), c ^ (k & 1))
                routes.append((a, 0, peer, 4 * x + 2 * y + c))
        return routes
    return plan, 7 * n


def _route_copies(srcs, lands, send_sems, recv_sems, routes):
    return [pltpu.make_async_remote_copy(
        src_ref=srcs[a].at[sb], dst_ref=lands[a].at[lb], send_sem=send_sems.at[r], recv_sem=recv_sems.at[r],
        device_id=peer, device_id_type=MESH_T) for r, (a, sb, peer, lb) in enumerate(routes)]


def _exchange_start(srcs, lands, routes, name, after=()):
    plan, count = routes
    n = len(srcs)
    n_in = 2 * n + len(after)

    def body(*refs):
        send_sems, recv_sems = refs[n_in], refs[n_in + 1]
        token = refs[-1]
        for cp in _route_copies(refs[:n], refs[n:2 * n], send_sems, recv_sems, plan(*_mesh_pos())):
            cp.start()
        token[...] = jnp.zeros_like(token)

    args = [pltpu.with_memory_space_constraint(a, pltpu.HBM) for a in list(srcs) + list(lands)]
    out = pl.pallas_call(
        body, name=name,
        out_shape=(pltpu.SemaphoreType.DMA((count,)), pltpu.SemaphoreType.DMA((count,)),
                   *[pltpu.HBM(a.shape, a.dtype) for a in args], _sds((8, 128), F32)),
        in_specs=[_HBM] * (2 * n) + [pl.BlockSpec(memory_space=pl.ANY)] * len(after),
        out_specs=(_SEM, _SEM, *([_HBM] * (2 * n)), pl.BlockSpec(memory_space=pltpu.VMEM)),
        input_output_aliases={i: 2 + i for i in range(2 * n)},
        compiler_params=pltpu.CompilerParams(has_side_effects=_EFFECT),
    )(*args, *after)
    return (out[0], out[1], list(out[2:2 + 2 * n]), routes), out[-1]


def _exchange_wait_some(state, after, only, name):
    send_sems, recv_sems, bufs, (plan, count) = state
    n = len(bufs) // 2

    def body(*refs):
        send_s, recv_s = refs[2 * n], refs[2 * n + 1]
        for r, cp in enumerate(_route_copies(refs[:n], refs[n:2 * n], send_s, recv_s, plan(*_mesh_pos()))):
            if only is None or r in only:
                cp.wait_send()
                cp.wait_recv()

    out = pl.pallas_call(
        body, name=name, out_shape=tuple(pltpu.HBM(a.shape, a.dtype) for a in bufs),
        in_specs=[_HBM] * (2 * n) + [_SEM, _SEM, pl.BlockSpec(memory_space=pl.ANY)], out_specs=tuple([_HBM] * (2 * n)),
        input_output_aliases={i: i for i in range(2 * n)},
        compiler_params=pltpu.CompilerParams(has_side_effects=_EFFECT),
    )(*bufs, send_sems, recv_sems, after)
    return (send_sems, recv_sems, list(out), (plan, count)), list(out[:n]), list(out[n:])


def _exchange_wait(state, after, name):
    _, srcs, lands = _exchange_wait_some(state, after, None, name)
    return srcs, lands


def _group_routes(js):
    def plan(x, y, c):
        return [(0, 0, (x ^ (j >> 1), y ^ (j & 1), c), 2 * j + c) for j in js]
    return plan, len(js)


def _pair_fill(groups, j, name, after=()):
    def body(*refs):
        g_ref, send_sem, recv_sem = refs[-3:]
        x, y, c = _mesh_pos()
        mine = g_ref.at[2 * j + c]
        to_sib = pltpu.make_async_remote_copy(src_ref=mine, dst_ref=mine, send_sem=send_sem, recv_sem=recv_sem,
                                              device_id=(x, y, 1 - c), device_id_type=MESH_T)
        to_sib.start()
        pltpu.make_async_remote_copy(src_ref=mine, dst_ref=g_ref.at[2 * j + 1 - c], send_sem=send_sem, recv_sem=recv_sem,
                                     device_id=(x, y, 1 - c), device_id_type=MESH_T).wait_recv()
        to_sib.wait_send()

    hbm = pl.BlockSpec(memory_space=pl.ANY)
    return pl.pallas_call(
        body, name=name, in_specs=[hbm] * (1 + len(after)), out_specs=hbm, out_shape=_sds(groups.shape, groups.dtype),
        input_output_aliases={0: 0},
        scratch_shapes=[pltpu.SemaphoreType.DMA, pltpu.SemaphoreType.DMA],
    )(groups, *after)


def _in_proj_group(h_all, groups, j, chip, px_prev, after, name):
    rows_all = h_all.shape[0]
    gcols = IN_COLS // 4
    tm = _pick(rows_all, 1536, 128)
    g4 = groups.reshape(4, gcols, D)

    n_lead = (1 if px_prev is not None else 0) + len(after)
    lead = ([px_prev] if px_prev is not None else []) + list(after)

    def body(chip_ref, *refs):
        h_ref, w_ref, o_ref = refs[n_lead:]
        o_ref[...] = _dot(h_ref[...], w_ref[...], NT).astype(BF16)
    return pl.pallas_call(
        body, name=name,
        grid_spec=pltpu.PrefetchScalarGridSpec(
            num_scalar_prefetch=1, grid=(rows_all // tm,),
            in_specs=[pl.BlockSpec(memory_space=pl.ANY)] * n_lead
            + [pl.BlockSpec((tm, D), lambda i, ch: (i, 0)), pl.BlockSpec((None, gcols, D), lambda i, ch: (j, 0, 0))],
            out_specs=pl.BlockSpec((tm, gcols), lambda i, ch: (i, ch[0] ^ j))),
        out_shape=_sds((rows_all, IN_COLS), BF16),
        input_output_aliases={1: 0} if px_prev is not None else {},
        compiler_params=_params(("parallel",)),
    )(chip, *lead, h_all, g4)


def _d_h_groups(dp_all, groups, chip, after):
    rows_all = dp_all.shape[0]
    gcols = IN_COLS // 4
    tm = _pick(rows_all, 1536, 128)
    g4 = groups.reshape(4, gcols, D)
    n_lead = len(after)

    def body(chip_ref, *refs):
        a_ref, w_ref, o_ref = refs[n_lead:]
        j = pl.program_id(1)
        part = _dot(a_ref[...], w_ref[...])

        @pl.when(j == 0)
        def _():
            o_ref[...] = part

        @pl.when(j > 0)
        def _():
            o_ref[...] += part

    return pl.pallas_call(
        body, name="d_h",
        grid_spec=pltpu.PrefetchScalarGridSpec(
            num_scalar_prefetch=1, grid=(rows_all // tm, 4),
            in_specs=[pl.BlockSpec(memory_space=pl.ANY)] * n_lead
            + [pl.BlockSpec((tm, gcols), lambda i, j, ch: (i, ch[0] ^ j)),
               pl.BlockSpec((None, gcols, D), lambda i, j, ch: (j, 0, 0))],
            out_specs=pl.BlockSpec((tm, D), lambda i, j, ch: (i, 0))),
        out_shape=_sds((rows_all, D), F32),
        compiler_params=_params(("parallel", "arbitrary")),
    )(chip, *after, dp_all, g4)


def _reduce_scatter_start(parts, core, name):
    got = _pair_exchange(parts, name + "_pair")
    sums = [_pair_add(p, g, core, "%s_add_%d" % (name, i)) for i, (p, g) in enumerate(zip(parts, got))]
    lands = [lax.empty((3,) + s_.shape[1:], BF16) for s_ in sums]
    return _exchange_start(sums, lands, _chip_routes(len(sums)), name + "_start")


def _reduce_scatter_finish(rs_state, after, chip, name):
    sums, landed = _exchange_wait(rs_state, after, name + "_wait")
    return [_chip_sum(s_, l_, chip, "%s_sum_%d" % (name, i)) for i, (s_, l_) in enumerate(zip(sums, landed))]


def _local_step(x, c, ctx, norm_w, ret_log2_decay, q_norm_w, k_norm_w, loss_target,
                mod, proj_in, proj_back, get_w_o, on_out_grads, on_in_grad, started=()):
    nb, seq, _ = x.shape
    cx = ctx.shape[1]
    t_rows, c_rows = nb * seq, nb * cx
    rows_all = t_rows + c_rows
    nc = seq // CH
    tm = _pick(seq, 256, 128)
    te = _pick(seq, 512, 128)
    assert cx % tm == 0 and t_rows % cx == 0 and seq % GRID_W == 0

    x2 = x.reshape(t_rows, D)
    ctx2 = ctx.reshape(c_rows, D)
    tgt = loss_target.reshape(t_rows, D)
    lg = _log_gamma(ret_log2_decay)
    cos, sin = _rope_tables(seq)

    mod3 = mod[:, None, :]
    h_all = _norm_fwd(x2, mod3, norm_w, rows_all, 0, seq, 0, None, te, "norm_fwd", after=started)
    h_all = _norm_fwd(ctx2, mod3, norm_w, rows_all, t_rows, c_rows, nb, h_all, tm, "norm_fwd_ctx")
    px = proj_in(h_all)
    s0f, s0b = _ctx_state(px, lg, nb, t_rows, cx)
    o_f, o_b, hist_f, hist_b = _ret_fwd(px, lg, s0f, s0b, nb, nc)
    yret16 = _ret_post(o_f, o_b, px, te)
    q16 = _qk_prep(px, q_norm_w, cos, sin, t_rows, 0, AQ, HQ, 4, seq, te, "q_prep")
    kx16 = _qk_prep(px, k_norm_w, cos, sin, t_rows, 0, AK, HKV, HKV, seq, te, "k_prep")
    kc16 = _qk_prep(px, k_norm_w, None, None, c_rows, t_rows, AK, HKV, HKV, seq, tm, "kc_prep")
    kv = HKV * HD
    k_all = jnp.concatenate([kx16.reshape(nb, seq, kv), kc16.reshape(nb, cx, kv)], axis=1)
    v_all = jnp.concatenate([px[:t_rows, AV:AV + kv].reshape(nb, seq, kv), px[t_rows:, AV:AV + kv].reshape(nb, cx, kv)],
                            axis=1)
    o_att, yatt16, lse = _att_fwd(q16, k_all, v_all, px, nb, seq, te)
    w_o_ret16, w_o_att16, w_out16 = get_w_o(lse)
    a_ret, a_att, y16 = _merge(yret16, yatt16, px, w_o_ret16, w_o_att16, te)
    dxn, dout16, dgate, loss_b = _outproj(y16, w_out16, x2, tgt, mod3, nb, seq, te)

    gw_out = _matmul(y16, dout16, ta=True, tm=D, tn=D, tk=D, out_dtype=BF16, name="gw_out")
    da_ret16, da_att16, dmr16, dma16 = _bwd_merge(dout16, w_out16, px, a_ret, a_att, te)
    gw_o_ret = _matmul(yret16, da_ret16, ta=True, tm=D, tn=D, tk=D, out_dtype=BF16, name="gw_o_ret")
    gw_o_att = _matmul(yatt16, da_att16, ta=True, tm=D, tn=D, tk=D, out_dtype=BF16, name="gw_o_att")
    out_state, out_started = on_out_grads([gw_o_ret, gw_o_att, gw_out])
    do16, drg16 = _bwd_branch_ret(da_ret16, w_o_ret16, px, o_f, o_b, te, after=out_started)
    dao, dag16 = _bwd_branch_att(da_att16, w_o_att16, px, o_att, te)
    dq_rot, dk_all, dv_all = _att_bwd(q16, k_all, v_all, dao, o_att, lse, nb, seq, te)
    dkx, dkc, dvx, dvc = dk_all[:, :seq], dk_all[:, seq:], dv_all[:, :seq], dv_all[:, seq:]
    daq16, gq = _qk_prep_bwd(dq_rot, px, q_norm_w, cos, sin, t_rows, 0, AQ, HQ, 4, seq, te, "q_prep_bwd")
    dak16, gk_lat = _qk_prep_bwd(dkx.reshape(t_rows, HKV * HD), px, k_norm_w, cos, sin, t_rows, 0, AK, HKV, HKV, seq, te,
                                 "k_prep_bwd")
    dcak16, gk_ctx = _qk_prep_bwd(dkc.reshape(c_rows, HKV * HD), px, k_norm_w, None, None, c_rows, t_rows, AK, HKV, HKV,
                                  seq, tm, "kc_prep_bwd")
    dq_f, dk_f, dv_f, dq_b, dk_b, dv_b, ds_f, ds_b, dlg_scan = _ret_bwd(px, lg, do16, hist_f, hist_b, nb, nc)
    dck16, dcv16, dlg_ctx = _ctx_state_bwd(px, lg, ds_f, ds_b, nb, t_rows, cx)
    dp_all = _assemble_lat(rows_all, dk_f, dk_b, dv_f, dv_b, dak16, dvx.reshape(t_rows, HKV * HD), dq_f, dq_b, drg16,
                           daq16, dag16, dmr16, dma16, tm)
    dp_all = _assemble_ctx(dp_all, dck16, dcv16, dcak16, dvc.reshape(c_rows, HKV * HD), t_rows, tm)
    gw_in_t = _matmul(dp_all, h_all, ta=True, tm=1536, tn=D, tk=2304, out_dtype=BF16, name="gw_in")
    in_state, in_started = on_in_grad(gw_in_t)
    dh = proj_back(dp_all, in_started)
    grad_x, dsh, dsc, gnw_lat = _norm_bwd(dh, x2, mod3, norm_w, dxn, 0, seq, 0, te, "norm_bwd")
    dsh_c, dsc_c, gnw_ctx = _norm_bwd(dh, ctx2, mod3, norm_w, None, t_rows, c_rows, nb, tm, "norm_bwd_ctx")

    dlg = (jnp.sum(dlg_scan[:, :, 0], axis=0) + jnp.sum(dlg_ctx[:, :, :2, 0], axis=0).T.reshape(2 * RH)).reshape(1, 2 * RH)
    misc = jnp.concatenate([gq, gk_lat + gk_ctx, dlg, jnp.sum(loss_b[:, 0, 0]).reshape(1, 1),
                            jnp.zeros((1, D - 2 * HD - 2 * RH - 1), F32)], axis=1)
    rows = []
    for b in range(nb):
        rows += [dsh[b], dsc[b], dgate[b]]
    rows += [dsh_c[0], dsc_c[0]] + [c[b:b + 1] for b in range(nb)] + [gnw_lat + gnw_ctx, misc]
    payload = jnp.concatenate(rows + [jnp.zeros((PAY_ROWS - len(rows), D), F32)], axis=0)
    return grad_x.reshape(nb, seq, D), out_state, in_state, payload


def _finish_small(gathered, nb, c_ctx, ret_log2_decay, w_ada16, dev):
    n_dev = gathered.shape[0]
    loc = 3 * D // n_dev
    dmod_all = gathered[:, :3 * nb].reshape(n_dev * nb, 3 * D)
    dmodc_parts = jnp.concatenate([gathered[:, 3 * nb:3 * nb + 2].reshape(n_dev, 2 * D), jnp.zeros((n_dev, D), F32)], axis=1)
    c_all = gathered[:, 3 * nb + 2:4 * nb + 2].reshape(n_dev * nb, D)
    nw_parts = gathered[:, 4 * nb + 2]
    misc_parts = gathered[:, 4 * nb + 3]
    n_rows = n_dev * nb + n_dev
    pad = (-n_rows) % 16
    c_rows = jnp.concatenate([c_all, jnp.broadcast_to(c_ctx.reshape(1, D), (n_dev, D)), jnp.zeros((pad, D), F32)], axis=0)
    dm_rows = jnp.concatenate([dmod_all, dmodc_parts, jnp.zeros((pad, 3 * D), F32)], axis=0)
    dm_loc_rows = lax.dynamic_slice_in_dim(dm_rows, dev * loc, loc, axis=1)
    r_pad = jnp.full((1, D), -1.0, F32).at[:, 2 * HD:2 * HD + 2 * RH].set(ret_log2_decay.reshape(1, 2 * RH))
    gb, gc, gnw, misc, gwa = _small_final(dmod_all, dmodc_parts, c_rows, dm_loc_rows, nw_parts, misc_parts,
                                          c_ctx.reshape(1, D), r_pad, w_ada16)
    return (gb, gc, gnw, misc[:, :HD], misc[:, HD:2 * HD], misc[:, 2 * HD:2 * HD + 2 * RH], gwa,
            misc[0, 2 * HD + 2 * RH])


def kernel(x, c, ctx, c_ctx, norm_w, w_ada, b_ada, w_in, ret_log2_decay, q_norm_w, k_norm_w, w_o_ret, w_o_att, w_out, loss_target, m_c_ctx, m_norm_w, m_w_ada, m_b_ada, m_w_in, m_ret_log2_decay, m_q_norm_w, m_k_norm_w, m_w_o_ret, m_w_o_att, m_w_out, v_c_ctx, v_norm_w, v_w_ada, v_b_ada, v_w_in, v_ret_log2_decay, v_q_norm_w, v_k_norm_w, v_w_o_ret, v_w_o_att, v_w_out):
    nb = x.shape[0]
    mx, my, mc = _mesh_pos()
    dev = 4 * mx + 2 * my + mc
    core = jnp.reshape(mc, (1,)).astype(jnp.int32)
    chip = jnp.reshape(2 * mx + my, (1,)).astype(jnp.int32)

    n_loc = 3 * D // N_DEV
    c8 = jnp.zeros((8, D), F32).at[:nb].set(c).at[nb].set(c_ctx)
    (c_all,) = _all_gather([c8], "gather_c")
    ada_shard = w_ada[0].astype(BF16)
    b_loc = lax.dynamic_slice(b_ada, (0, dev * n_loc), (1, n_loc))
    mod_cols = _mod_part(c_all.reshape(N_DEV * 8, D), ada_shard, b_loc)
    (mod_all,) = _all_gather([mod_cols], "gather_mod")
    mod = jnp.transpose(lax.dynamic_slice(mod_all, (0, dev * 8, 0), (N_DEV, 8, n_loc)), (1, 0, 2)).reshape(8, 3 * D)
    ada_land = lax.dynamic_update_slice(lax.empty((N_DEV,) + ada_shard.shape, BF16), ada_shard[None], (dev, 0, 0))

    w_in_t = jnp.transpose(w_in[0])
    in_shard = w_in_t.astype(BF16)
    groups = lax.dynamic_update_slice(lax.empty((N_DEV,) + in_shard.shape, BF16), in_shard[None], (mc, 0, 0))
    groups = _pair_fill(groups, 0, "gather_in_pair", after=(mod_all,))
    (near_send, near_recv, near_bufs, near_routes), gin_token = _exchange_start(
        [in_shard[None]], [groups], _group_routes((1, 2)), "gather_in_start")
    w_in_groups, wo_states, ada_states = [], [], []
    wo_shards = [w_[0].astype(BF16) for w_ in (w_o_ret, w_o_att, w_out)]
    wo_lands = [lax.dynamic_update_slice(lax.empty((N_DEV,) + s_.shape, BF16), s_[None], (dev, 0, 0)) for s_ in wo_shards]

    def proj_in(h_all):
        src, groups = near_bufs
        px = _in_proj_group(h_all, groups, 0, chip, None, (gin_token,), "in_proj_0")
        _, (src,), (groups,) = _exchange_wait_some((near_send, near_recv, [src, groups], near_routes), px, (0,),
                                                   "gather_in_wait_1")
        groups = _pair_fill(groups, 1, "gather_in_fill_1")
        (far_send, far_recv, (src, groups), far_routes), far_token = _exchange_start(
            [src], [groups], _group_routes((3,)), "gather_in_start_far")
        wo_state, wo_token = _exchange_start([s_[None] for s_ in wo_shards], wo_lands, _bcast_routes(3),
                                             "gather_wo_start", after=(far_token,))
        wo_states.append(wo_state)
        ada_state, ada_token = _exchange_start([ada_shard[None]], [ada_land], _bcast_routes(1), "gather_ada_start",
                                               after=(wo_token,))
        ada_states.append(ada_state)
        px = _in_proj_group(h_all, groups, 1, chip, px, (ada_token,), "in_proj_1")
        for j in (2, 3):
            state = ((near_send, near_recv, [src, groups], near_routes) if j < 3 else
                     (far_send, far_recv, [src, groups], far_routes))
            _, (src,), (groups,) = _exchange_wait_some(state, px, (1,) if j < 3 else None, "gather_in_wait_%d" % j)
            groups = _pair_fill(groups, j, "gather_in_fill_%d" % j)
            px = _in_proj_group(h_all, groups, j, chip, px, (), "in_proj_%d" % j)
        w_in_groups.append(groups)
        return px

    def proj_back(dp_all, after):
        return _d_h_groups(dp_all, w_in_groups[0], chip, after)

    def get_w_o(after):
        _, (l_ret, l_att, l_out) = _exchange_wait(wo_states[0], after, "gather_wo_wait")
        return l_ret.reshape(RH * DV, D), l_att.reshape(D, D), l_out.reshape(D, D)

    def on_out_grads(grads):
        parts = [g_.reshape(N_DEV, g_.shape[0] // N_DEV, D) for g_ in grads]
        state, token = _reduce_scatter_start(parts, core, "rs_out")
        return state, (token,)

    def on_in_grad(grad):
        state, token = _reduce_scatter_start([grad.reshape(N_DEV, IN_COLS // N_DEV, D)], core, "rs_in")
        return state, (token,)

    grad_x, out_state, in_state, payload = _local_step(
        x, c, ctx, norm_w, ret_log2_decay, q_norm_w, k_norm_w, loss_target,
        mod, proj_in, proj_back, get_w_o, on_out_grads, on_in_grad, started=(gin_token,))

    (gathered,) = _all_gather([payload], "gather_small")
    _, (l_ada,) = _exchange_wait(ada_states[0], gathered, "gather_ada_wait")
    w_ada16 = jnp.transpose(l_ada, (1, 0, 2)).reshape(D, 3 * D)
    gb, gc, gnw, gq, gk, gr, gwa, loss = _finish_small(gathered, nb, c_ctx, ret_log2_decay, w_ada16, dev)

    g_w_o_ret, g_w_o_att, g_w_out = _reduce_scatter_finish(out_state, gathered, chip, "rs_out")
    (g_w_in_t,) = _reduce_scatter_finish(in_state, gathered, chip, "rs_in")

    grads = [gc.reshape(c_ctx.shape), gnw, gwa[None], gb, g_w_in_t, gr.reshape(ret_log2_decay.shape), gq, gk,
             g_w_o_ret[None], g_w_o_att[None], g_w_out[None]]
    weights = [c_ctx, norm_w, w_ada, b_ada, w_in_t, ret_log2_decay, q_norm_w, k_norm_w, w_o_ret, w_o_att, w_out]
    ms = [m_c_ctx, m_norm_w, m_w_ada, m_b_ada, jnp.transpose(m_w_in[0]), m_ret_log2_decay, m_q_norm_w, m_k_norm_w,
          m_w_o_ret, m_w_o_att, m_w_out]
    vs = [v_c_ctx, v_norm_w, v_w_ada, v_b_ada, jnp.transpose(v_w_in[0]), v_ret_log2_decay, v_q_norm_w, v_k_norm_w,
          v_w_o_ret, v_w_o_att, v_w_out]
    deltas, new_ms, new_vs = [], [], []
    for i, (w, g, m, v) in enumerate(zip(weights, grads, ms, vs)):
        shape2 = (-1, w.shape[-1])
        res = _adamw(w.reshape(shape2), g.reshape(shape2), m.reshape(shape2), v.reshape(shape2), "adamw_%d" % i)
        for lst, r in zip((deltas, new_ms, new_vs), res):
            lst.append(jnp.transpose(r)[None] if i == 4 else r.reshape(w.shape))
    grads[4] = jnp.transpose(g_w_in_t)[None]
    return (loss, grad_x, *grads, *deltas, *new_ms, *new_vs)
```

```python
import numpy as np
import jax
import jax.numpy as jnp
from jax import lax
from jax.experimental import pallas as pl
from jax.experimental.pallas import tpu as pltpu

F32 = jnp.float32
BF16 = jnp.bfloat16

D = 1024
RH, DK, DV, CH = 4, 256, 512, 256
HQ, HKV, HD = 8, 2, 128
GRID_W = 64
ROPE_THETA = 10000.0
EPS = 1e-6
RK, RV, AK, AV, RQ, RG, AQ, AG, MR, MA = 0, 1024, 3072, 3328, 3584, 4608, 6656, 7680, 8704, 9728
IN_COLS = 10752
KV_COLS = 3584
N_DEV = 8
LR, B1, B2, ADAM_EPS, WD, STEP = 0.001, 0.9, 0.999, 1e-08, 0.01, 10
PAY_ROWS = 16
VMEM_LIMIT = 56 * 1024 * 1024
MESH_T = pl.DeviceIdType.MESH

NT = (((1,), (1,)), ((), ()))
TN = (((0,), (0,)), ((), ()))
SM_C = (HD ** -0.5) * float(np.log2(np.e))


def _params(sem):
    return pltpu.CompilerParams(dimension_semantics=sem, vmem_limit_bytes=VMEM_LIMIT)


def _pick(n, target, mult=8):
    best = None
    for t in range(mult, min(n, target) + 1, mult):
        if n % t == 0:
            best = t
    return best or n


def _dot(a, b, dn=None):
    if dn is None:
        return jnp.dot(a, b, preferred_element_type=F32)
    return lax.dot_general(a, b, dn, preferred_element_type=F32)


def _sig(v):
    return jax.nn.sigmoid(v)


def _silu(v):
    return v * _sig(v)


def _dsilu(v):
    s = _sig(v)
    return s * (1.0 + v * (1.0 - s))


def _sds(shape, dtype):
    return jax.ShapeDtypeStruct(shape, dtype)


def _matmul(a, b, *, ta=False, tb=False, tm, tn, tk, out_dtype, name, after=()):
    m = a.shape[1] if ta else a.shape[0]
    kdim = a.shape[0] if ta else a.shape[1]
    n = b.shape[0] if tb else b.shape[1]
    tm, tn, tk = _pick(m, tm, 128), _pick(n, tn, 128), _pick(kdim, tk, 128)
    nk = kdim // tk
    dn = (((0 if ta else 1,), (1 if tb else 0,)), ((), ()))

    def body(a_ref, b_ref, *rest):
        o_ref, acc_ref = rest[-2:]
        k = pl.program_id(2)
        part = _dot(a_ref[...].astype(BF16), b_ref[...].astype(BF16), dn)
        if nk == 1:
            o_ref[...] = part.astype(o_ref.dtype)
        else:
            @pl.when(k == 0)
            def _():
                acc_ref[...] = part

            @pl.when(k > 0)
            def _():
                acc_ref[...] += part

            @pl.when(k == nk - 1)
            def _():
                o_ref[...] = acc_ref[...].astype(o_ref.dtype)

    a_spec = pl.BlockSpec((tk, tm), lambda i, j, k: (k, i)) if ta else pl.BlockSpec((tm, tk), lambda i, j, k: (i, k))
    b_spec = pl.BlockSpec((tn, tk), lambda i, j, k: (j, k)) if tb else pl.BlockSpec((tk, tn), lambda i, j, k: (k, j))
    return pl.pallas_call(
        body, name=name, grid=(m // tm, n // tn, nk),
        in_specs=[a_spec, b_spec] + [pl.BlockSpec(memory_space=pl.ANY)] * len(after),
        out_specs=pl.BlockSpec((tm, tn), lambda i, j, k: (i, j)), out_shape=_sds((m, n), out_dtype),
        scratch_shapes=[pltpu.VMEM((tm, tn) if nk > 1 else (8, 128), F32)],
        compiler_params=_params(("parallel", "parallel", "arbitrary")),
    )(a, b, *after)


def _log_gamma(r):
    rp = jnp.full((8, 128), -1.0, F32).at[:2, :RH].set(r.reshape(2, RH))

    def body(r_ref, o_ref):
        o_ref[...] = jnp.log1p(-jnp.exp2(r_ref[...]))

    out = pl.pallas_call(body, name="log_gamma", out_shape=_sds((8, 128), F32))(rp)
    return out[:2, :RH]


def _mod_part(c_rows, w_ada_loc16, b_loc):
    def body(c_ref, w_ref, b_ref, o_ref):
        o_ref[...] = _dot(_silu(c_ref[...]).astype(BF16), w_ref[...]) + b_ref[...]

    return pl.pallas_call(
        body, name="mod_part", out_shape=_sds((c_rows.shape[0], w_ada_loc16.shape[1]), F32),
    )(c_rows, w_ada_loc16, b_loc)


def _norm_fwd(x2, mod3, norm_w, rows_all, row_off, rows_per_group, group0, h_prev, tm, name, after=()):
    rows = x2.shape[0]
    rb0 = row_off // tm
    bpg = rows_per_group // tm

    def body(*refs):
        x_ref, sh_ref, sc_ref, nw_ref, o_ref = refs[-5:]
        xv = x_ref[...]
        r = lax.rsqrt(jnp.mean(xv * xv, axis=-1, keepdims=True) + EPS)
        o_ref[...] = ((xv * r) * nw_ref[...] * (1.0 + sc_ref[...]) + sh_ref[...]).astype(BF16)

    in_specs = [pl.BlockSpec((tm, D), lambda i: (i, 0)),
                pl.BlockSpec((None, 1, D), lambda i: (group0 + i // bpg, 0, 0)),
                pl.BlockSpec((None, 1, D), lambda i: (group0 + i // bpg, 0, 1)),
                pl.BlockSpec((1, D), lambda i: (0, 0))]
    in_specs = [pl.BlockSpec(memory_space=pl.ANY)] * len(after) + in_specs
    args = list(after) + [x2, mod3, mod3, norm_w]
    alias = {}
    if h_prev is not None:
        in_specs.insert(0, pl.BlockSpec(memory_space=pl.ANY))
        args.insert(0, h_prev)
        alias = {0: 0}
    return pl.pallas_call(
        body, name=name, grid=(rows // tm,), in_specs=in_specs,
        out_specs=pl.BlockSpec((tm, D), lambda i: (rb0 + i, 0)), out_shape=_sds((rows_all, D), BF16),
        input_output_aliases=alias, compiler_params=_params(("parallel",)),
    )(*args)


def _decays(lg, fwd):
    ii = lax.broadcasted_iota(jnp.int32, (CH, CH), 0)
    jj = lax.broadcasted_iota(jnp.int32, (CH, CH), 1)
    ri = lax.broadcasted_iota(jnp.int32, (CH, 1), 0).astype(F32)
    rel = (ii - jj) if fwd else (jj - ii)
    relf = jnp.maximum(rel, 0).astype(F32)
    mask = jnp.where(rel >= 0, jnp.exp(lg * relf), 0.0)
    qe = (ri + 1.0) if fwd else (CH - ri)
    ke = (CH - 1.0 - ri) if fwd else ri
    return mask, relf, jnp.exp(lg * qe), qe, jnp.exp(lg * ke), ke


def _wide_specs(rowf):
    return [pl.BlockSpec((CH, 2 * DK), lambda b, c: (rowf(b, c), RQ // (2 * DK))),
            pl.BlockSpec((CH, 2 * DK), lambda b, c: (rowf(b, c), RQ // (2 * DK) + 1)),
            pl.BlockSpec((CH, RH * DK), lambda b, c: (rowf(b, c), RK // (RH * DK))),
            pl.BlockSpec((CH, 2 * DV), lambda b, c: (rowf(b, c), RV // (2 * DV))),
            pl.BlockSpec((CH, 2 * DV), lambda b, c: (rowf(b, c), RV // (2 * DV) + 1))]


def _head_qkv(refs, h):
    q0, q1, k, v0, v1 = refs
    lo = h % 2
    q = (q0, q1)[h // 2][:, lo * DK:(lo + 1) * DK].astype(F32)
    kk = k[:, h * DK:(h + 1) * DK].astype(F32) * (DK ** -0.5)
    v16 = (v0, v1)[h // 2][:, lo * DV:(lo + 1) * DV].astype(BF16)
    return q, kk, v16


def _ctx_state(px, lg, nb, t_rows, cx):
    rb = t_rows // cx

    def body(lg_ref, k_ref, v_ref, sf_ref, sb_ref):
        h = pl.program_id(1)
        pos = lax.broadcasted_iota(jnp.int32, (cx, 1), 0).astype(F32)
        k = k_ref[...].astype(F32) * (DK ** -0.5)
        v16 = v_ref[...].astype(BF16)
        wf = jnp.exp(lg_ref[0, h] * (cx - 1.0 - pos))
        wb = jnp.exp(lg_ref[1, h] * pos)
        sf_ref[...] = _dot((k * wf).astype(BF16), v16, TN)
        sb_ref[...] = _dot((k * wb).astype(BF16), v16, TN)

    st = pl.BlockSpec((None, None, DK, DV), lambda b, h: (b, h, 0, 0))
    return pl.pallas_call(
        body, name="ctx_state", grid=(nb, RH),
        in_specs=[pl.BlockSpec(memory_space=pltpu.SMEM),
                  pl.BlockSpec((cx, DK), lambda b, h: (rb + b, RK // DK + h)),
                  pl.BlockSpec((cx, DV), lambda b, h: (rb + b, RV // DV + h))],
        out_specs=[st, st], out_shape=[_sds((nb, RH, DK, DV), F32)] * 2,
        compiler_params=_params(("parallel", "parallel")),
    )(lg, px, px)


def _ret_fwd(px, lg, s0f, s0b, nb, nc):
    t_rows = nb * nc * CH

    def body(lg_ref, *refs):
        ins = (refs[0:5], refs[5:10])
        s0f_ref, s0b_ref, of_ref, ob_ref, hf_ref, hb_ref, sf, sb = refs[10:]
        c = pl.program_id(1)

        @pl.when(c == 0)
        def _():
            sf[...] = s0f_ref[...]
            sb[...] = s0b_ref[...]

        for d, (o_ref, h_ref, s) in enumerate(((of_ref, hf_ref, sf), (ob_ref, hb_ref, sb))):
            for h in range(RH):
                lg_d = lg_ref[d, h]
                mask, _, qd, _, kd, _ = _decays(lg_d, d == 0)
                q, k, v16 = _head_qkv(ins[d], h)
                a = _dot(q.astype(BF16), k.astype(BF16), NT)
                st = s[h]
                st16 = st.astype(BF16)
                h_ref[h] = st16
                o = _dot((a * mask).astype(BF16), v16) + _dot((q * qd).astype(BF16), st16)
                o_ref[:, h * DV:(h + 1) * DV] = o.astype(BF16)
                s[h] = st * jnp.exp(lg_d * CH) + _dot((k * kd).astype(BF16), v16, TN)

    def fw(b, c):
        return b * nc + c

    def bw(b, c):
        return b * nc + nc - 1 - c

    st = pl.BlockSpec((None, RH, DK, DV), lambda b, c: (b, 0, 0, 0))
    in_specs = [pl.BlockSpec(memory_space=pltpu.SMEM)] + _wide_specs(fw) + _wide_specs(bw) + [st, st]
    out_specs = [pl.BlockSpec((CH, RH * DV), lambda b, c: (fw(b, c), 0)),
                 pl.BlockSpec((CH, RH * DV), lambda b, c: (bw(b, c), 0)),
                 pl.BlockSpec((None, None, RH, DK, DV), lambda b, c: (b, c, 0, 0, 0)),
                 pl.BlockSpec((None, None, RH, DK, DV), lambda b, c: (b, nc - 1 - c, 0, 0, 0))]
    return pl.pallas_call(
        body, name="ret_fwd", grid=(nb, nc), in_specs=in_specs, out_specs=out_specs,
        out_shape=[_sds((t_rows, RH * DV), BF16)] * 2 + [_sds((nb, nc, RH, DK, DV), BF16)] * 2,
        scratch_shapes=[pltpu.VMEM((RH, DK, DV), F32), pltpu.VMEM((RH, DK, DV), F32)],
        compiler_params=_params(("parallel", "arbitrary")),
    )(lg, *([px] * 10), s0f, s0b)


def _ret_post(o_f, o_b, px, tm):
    t_rows = o_f.shape[0]

    def body(of_ref, ob_ref, g0, g1, g2, g3, y_ref):
        for h, g_ref in enumerate((g0, g1, g2, g3)):
            sl = slice(h * DV, (h + 1) * DV)
            o = of_ref[:, sl].astype(F32) + ob_ref[:, sl].astype(F32)
            r = lax.rsqrt(jnp.mean(o * o, axis=-1, keepdims=True) + EPS)
            y_ref[:, sl] = ((o * r) * _silu(g_ref[...].astype(F32))).astype(BF16)

    def gate(h):
        return pl.BlockSpec((tm, DV), lambda i: (i, RG // DV + h))

    wide = pl.BlockSpec((tm, RH * DV), lambda i: (i, 0))
    return pl.pallas_call(
        body, name="ret_post", grid=(t_rows // tm,),
        in_specs=[wide, wide] + [gate(h) for h in range(RH)],
        out_specs=wide, out_shape=_sds((t_rows, RH * DV), BF16),
        compiler_params=_params(("parallel",)),
    )(o_f, o_b, *([px] * RH))


def _rope_tables(seq):
    rows = seq // GRID_W
    row = np.repeat(np.arange(rows, dtype=np.float32), GRID_W)
    col = np.tile(np.arange(GRID_W, dtype=np.float32), rows)
    half = HD // 2
    freqs = (ROPE_THETA ** (-np.arange(0, half, 2, dtype=np.float32) / half)).astype(np.float32)
    ang = np.concatenate([row[:, None] * freqs, col[:, None] * freqs], axis=-1).astype(np.float32)
    cos = np.repeat(np.cos(ang), 2, axis=-1).astype(np.float32)
    sin = np.repeat(np.sin(ang), 2, axis=-1).astype(np.float32)
    sign = np.tile(np.array([-1.0, 1.0], np.float32), HD // 2)
    return jnp.asarray(cos), jnp.asarray(sin * sign)


def _swap_pairs(v):
    lane = lax.broadcasted_iota(jnp.int32, v.shape, 1)
    return jnp.where((lane & 1) == 0, pltpu.roll(v, HD - 1, 1), pltpu.roll(v, 1, 1))


def _qk_prep(px, nw, cos, sin, rows, row_off, col_off, heads, hb, seq, tm, name):
    rope = cos is not None
    rb0 = row_off // tm
    pb = seq // tm if rope else 1
    bw = hb * HD

    def body(*refs):
        if rope:
            x_ref, w_ref, c_ref, s_ref, o_ref = refs
        else:
            x_ref, w_ref, o_ref = refs
        for h in range(hb):
            sl = slice(h * HD, (h + 1) * HD)
            xv = x_ref[:, sl].astype(F32)
            r = lax.rsqrt(jnp.mean(xv * xv, axis=-1, keepdims=True) + EPS)
            t = (xv * r) * w_ref[...]
            if rope:
                t = t * c_ref[...] + _swap_pairs(t) * s_ref[...]
            o_ref[:, sl] = t.astype(BF16)

    in_specs = [pl.BlockSpec((tm, bw), lambda i, j: (rb0 + i, col_off // bw + j)),
                pl.BlockSpec((1, HD), lambda i, j: (0, 0))]
    args = [px, nw]
    if rope:
        in_specs += [pl.BlockSpec((tm, HD), lambda i, j: (i % pb, 0))] * 2
        args += [cos, sin]
    return pl.pallas_call(
        body, name=name, grid=(rows // tm, heads // hb), in_specs=in_specs,
        out_specs=pl.BlockSpec((tm, bw), lambda i, j: (i, j)), out_shape=_sds((rows, heads * HD), BF16),
        compiler_params=_params(("parallel", "parallel")),
    )(*args)


def _att_fwd(q16, kx16, kc16, px, nb, seq, cx, tq):
    t_rows = nb * seq
    nq = seq // tq
    rep = HQ // HKV
    gw = rep * HD

    def body(q_ref, kx_ref, kc_ref, vx_ref, vc_ref, g_ref, o_ref, y_ref, l_ref):
        kx = kx_ref[...]
        kc = kc_ref[...]
        vx = vx_ref[...].astype(BF16)
        vc = vc_ref[...].astype(BF16)
        l_ref[...] = jnp.zeros_like(l_ref)
        for r in range(rep):
            sl = slice(r * HD, (r + 1) * HD)
            q = q_ref[:, sl]
            s1 = _dot(q, kx, NT)
            s2 = _dot(q, kc, NT)
            m = jnp.maximum(jnp.max(s1, axis=-1, keepdims=True), jnp.max(s2, axis=-1, keepdims=True))
            e1 = jnp.exp2((s1 - m) * SM_C)
            e2 = jnp.exp2((s2 - m) * SM_C)
            tot = jnp.sum(e1, axis=-1, keepdims=True) + jnp.sum(e2, axis=-1, keepdims=True)
            o = (_dot(e1.astype(BF16), vx) + _dot(e2.astype(BF16), vc)) * (1.0 / tot)
            o_ref[:, sl] = o
            y_ref[:, sl] = (o * _silu(g_ref[:, sl].astype(F32))).astype(BF16)
            l_ref[:, r:r + 1] = m * SM_C + jnp.log(tot) * float(np.log2(np.e))

    qblk = pl.BlockSpec((tq, gw), lambda b, g, i: (b * nq + i, g))
    return pl.pallas_call(
        body, name="att_fwd", grid=(nb, HKV, nq),
        in_specs=[qblk,
                  pl.BlockSpec((seq, HD), lambda b, g, i: (b, g)),
                  pl.BlockSpec((cx, HD), lambda b, g, i: (b, g)),
                  pl.BlockSpec((seq, HD), lambda b, g, i: (b, AV // HD + g)),
                  pl.BlockSpec((cx, HD), lambda b, g, i: (t_rows // cx + b, AV // HD + g)),
                  pl.BlockSpec((tq, gw), lambda b, g, i: (b * nq + i, AG // gw + g))],
        out_specs=[qblk, qblk, pl.BlockSpec((tq, 128), lambda b, g, i: (b * nq + i, g))],
        out_shape=[_sds((t_rows, D), F32), _sds((t_rows, D), BF16), _sds((t_rows, HKV * 128), F32)],
        compiler_params=_params(("parallel", "parallel", "parallel")),
    )(q16, kx16, kc16, px, px, px)


def _gate_specs(tm, col0):
    hw = D // 2
    return [pl.BlockSpec((tm, hw), lambda i: (i, col0 // hw)), pl.BlockSpec((tm, hw), lambda i: (i, col0 // hw + 1))]


def _merge(yret16, yatt16, px, w_o_ret16, w_o_att16, tm):
    t_rows = yret16.shape[0]
    hw = D // 2

    def body(yr_ref, wr_ref, ya_ref, wa_ref, mr0, mr1, ma0, ma1, ar_ref, aa_ref, y_ref):
        ar = _dot(yr_ref[...], wr_ref[...])
        aa = _dot(ya_ref[...], wa_ref[...])
        ar_ref[...] = ar.astype(BF16)
        aa_ref[...] = aa.astype(BF16)
        for j, (mr_ref, ma_ref) in enumerate(((mr0, ma0), (mr1, ma1))):
            sl = slice(j * hw, (j + 1) * hw)
            y_ref[:, sl] = (_sig(mr_ref[...].astype(F32)) * ar[:, sl]
                            + _sig(ma_ref[...].astype(F32)) * aa[:, sl]).astype(BF16)

    row = pl.BlockSpec((tm, D), lambda i: (i, 0))
    return pl.pallas_call(
        body, name="merge", grid=(t_rows // tm,),
        in_specs=[pl.BlockSpec((tm, RH * DV), lambda i: (i, 0)), pl.BlockSpec((RH * DV, D), lambda i: (0, 0)),
                  row, pl.BlockSpec((D, D), lambda i: (0, 0))] + _gate_specs(tm, MR) + _gate_specs(tm, MA),
        out_specs=[row, row, row], out_shape=[_sds((t_rows, D), BF16)] * 3,
        compiler_params=_params(("parallel",)),
    )(yret16, w_o_ret16, yatt16, w_o_att16, px, px, px, px)


def _outproj(y16, w_out16, x2, tgt, mod3, nb, seq, tm):
    t_rows = nb * seq
    bpb = seq // tm

    def body(y_ref, w_ref, x_ref, t_ref, g_ref, dxn_ref, dout_ref, dg_ref, loss_ref):
        i = pl.program_id(1)
        out = _dot(y_ref[...], w_ref[...])
        gate = g_ref[...]
        diff = x_ref[...] + gate * out - t_ref[...]
        dxn = diff * (1.0 / D)
        dxn_ref[...] = dxn
        dout_ref[...] = (gate * dxn).astype(BF16)
        dg = jnp.sum(dxn * out, axis=0, keepdims=True)
        ls = jnp.broadcast_to(jnp.sum(diff * diff) * (0.5 / D), (1, 128))

        @pl.when(i == 0)
        def _():
            dg_ref[...] = dg
            loss_ref[...] = ls

        @pl.when(i > 0)
        def _():
            dg_ref[...] += dg
            loss_ref[...] += ls

    row = pl.BlockSpec((tm, D), lambda b, i: (b * bpb + i, 0))
    return pl.pallas_call(
        body, name="outproj", grid=(nb, bpb),
        in_specs=[row, pl.BlockSpec((D, D), lambda b, i: (0, 0)), row, row,
                  pl.BlockSpec((None, 1, D), lambda b, i: (b, 0, 2))],
        out_specs=[row, row, pl.BlockSpec((None, 1, D), lambda b, i: (b, 0, 0)),
                   pl.BlockSpec((None, 1, 128), lambda b, i: (b, 0, 0))],
        out_shape=[_sds((t_rows, D), F32), _sds((t_rows, D), BF16), _sds((nb, 1, D), F32), _sds((nb, 1, 128), F32)],
        compiler_params=_params(("parallel", "arbitrary")),
    )(y16, w_out16, x2, tgt, mod3)


def _bwd_merge(dout16, w_out16, px, a_ret, a_att, tm):
    t_rows = dout16.shape[0]
    hw = D // 2

    def body(do_ref, w_ref, mr0, mr1, ma0, ma1, ar_ref, aa_ref, dar_ref, daa_ref, dmr_ref, dma_ref):
        dy_all = _dot(do_ref[...], w_ref[...], NT)
        for j, (mr_ref, ma_ref) in enumerate(((mr0, ma0), (mr1, ma1))):
            sl = slice(j * hw, (j + 1) * hw)
            dy = dy_all[:, sl]
            sr = _sig(mr_ref[...].astype(F32))
            sa = _sig(ma_ref[...].astype(F32))
            dar_ref[:, sl] = (dy * sr).astype(BF16)
            daa_ref[:, sl] = (dy * sa).astype(BF16)
            dmr_ref[:, sl] = (dy * ar_ref[:, sl].astype(F32) * sr * (1.0 - sr)).astype(BF16)
            dma_ref[:, sl] = (dy * aa_ref[:, sl].astype(F32) * sa * (1.0 - sa)).astype(BF16)

    row = pl.BlockSpec((tm, D), lambda i: (i, 0))
    return pl.pallas_call(
        body, name="bwd_merge", grid=(t_rows // tm,),
        in_specs=[row, pl.BlockSpec((D, D), lambda i: (0, 0))] + _gate_specs(tm, MR) + _gate_specs(tm, MA) + [row, row],
        out_specs=[row] * 4, out_shape=[_sds((t_rows, D), BF16)] * 4,
        compiler_params=_params(("parallel",)),
    )(dout16, w_out16, px, px, px, px, a_ret, a_att)


def _bwd_branch_ret(da_ret16, w_o_ret16, px, o_f, o_b, tm, after=()):
    t_rows = da_ret16.shape[0]

    def body(da_ref, w_ref, g0, g1, g2, g3, of_ref, ob_ref, *rest):
        do_ref, dg_ref = rest[-2:]
        da = da_ref[...]
        for h, g_ref in enumerate((g0, g1, g2, g3)):
            sl = slice(h * DV, (h + 1) * DV)
            dy = _dot(da, w_ref[sl, :], NT)
            g = g_ref[...].astype(F32)
            o = of_ref[:, sl].astype(F32) + ob_ref[:, sl].astype(F32)
            r = lax.rsqrt(jnp.mean(o * o, axis=-1, keepdims=True) + EPS)
            on = o * r
            sg = _sig(g)
            don = dy * (g * sg)
            dg_ref[:, sl] = (dy * on * (sg * (1.0 + g * (1.0 - sg)))).astype(BF16)
            do_ref[:, sl] = (r * (don - on * jnp.mean(on * don, axis=-1, keepdims=True))).astype(BF16)

    def gate(h):
        return pl.BlockSpec((tm, DV), lambda i: (i, RG // DV + h))

    wide = pl.BlockSpec((tm, RH * DV), lambda i: (i, 0))
    return pl.pallas_call(
        body, name="bwd_branch_ret", grid=(t_rows // tm,),
        in_specs=[pl.BlockSpec((tm, D), lambda i: (i, 0)), pl.BlockSpec((RH * DV, D), lambda i: (0, 0))]
        + [gate(h) for h in range(RH)] + [wide, wide] + [pl.BlockSpec(memory_space=pl.ANY)] * len(after),
        out_specs=[wide, wide], out_shape=[_sds((t_rows, RH * DV), BF16)] * 2,
        compiler_params=_params(("parallel",)),
    )(da_ret16, w_o_ret16, *([px] * RH), o_f, o_b, *after)


def _bwd_branch_att(da_att16, w_o_att16, px, o_att, tm):
    t_rows = da_att16.shape[0]
    hw = D // 2

    def body(da_ref, w_ref, g0, g1, o_ref, dao_ref, dg_ref):
        dy_all = _dot(da_ref[...], w_ref[...], NT)
        for j, g_ref in enumerate((g0, g1)):
            sl = slice(j * hw, (j + 1) * hw)
            dy = dy_all[:, sl]
            g = g_ref[...].astype(F32)
            sg = _sig(g)
            dao_ref[:, sl] = dy * (g * sg)
            dg_ref[:, sl] = (dy * o_ref[:, sl] * (sg * (1.0 + g * (1.0 - sg)))).astype(BF16)

    row = pl.BlockSpec((tm, D), lambda i: (i, 0))
    return pl.pallas_call(
        body, name="bwd_branch_att", grid=(t_rows // tm,),
        in_specs=[row, pl.BlockSpec((D, D), lambda i: (0, 0))] + _gate_specs(tm, AG) + [row],
        out_specs=[row, row], out_shape=[_sds((t_rows, D), F32), _sds((t_rows, D), BF16)],
        compiler_params=_params(("parallel",)),
    )(da_att16, w_o_att16, px, px, o_att)


def _att_bwd(q16, kx16, kc16, px, dao, o_att, lse, nb, seq, cx, tq):
    t_rows = nb * seq
    nq = seq // tq
    rep = HQ // HKV
    gw = rep * HD
    scale = HD ** -0.5

    def body(q_ref, kx_ref, kc_ref, vx_ref, vc_ref, dao_ref, o_ref, l_ref, dq_ref, dkx_ref, dvx_ref, dkc_ref, dvc_ref):
        i = pl.program_id(2)
        kx = kx_ref[...]
        kc = kc_ref[...]
        vx = vx_ref[...].astype(BF16)
        vc = vc_ref[...].astype(BF16)
        dkx = jnp.zeros((seq, HD), F32)
        dvx = jnp.zeros((seq, HD), F32)
        dkc = jnp.zeros((cx, HD), F32)
        dvc = jnp.zeros((cx, HD), F32)
        for r in range(rep):
            sl = slice(r * HD, (r + 1) * HD)
            q = q_ref[:, sl]
            lr = l_ref[:, r:r + 1]
            p1 = jnp.exp2(_dot(q, kx, NT) * SM_C - lr)
            p2 = jnp.exp2(_dot(q, kc, NT) * SM_C - lr)
            da = dao_ref[:, sl]
            da16 = da.astype(BF16)
            delta = jnp.sum(da * o_ref[:, sl], axis=-1, keepdims=True)
            ds1 = (p1 * (_dot(da16, vx, NT) - delta)).astype(BF16)
            ds2 = (p2 * (_dot(da16, vc, NT) - delta)).astype(BF16)
            dq_ref[:, sl] = (_dot(ds1, kx) + _dot(ds2, kc)) * scale
            dkx += _dot(ds1, q, TN)
            dkc += _dot(ds2, q, TN)
            dvx += _dot(p1.astype(BF16), da16, TN)
            dvc += _dot(p2.astype(BF16), da16, TN)
        dkx = dkx * scale
        dkc = dkc * scale

        @pl.when(i == 0)
        def _():
            dkx_ref[...] = dkx
            dvx_ref[...] = dvx
            dkc_ref[...] = dkc
            dvc_ref[...] = dvc

        @pl.when(i > 0)
        def _():
            dkx_ref[...] += dkx
            dvx_ref[...] += dvx
            dkc_ref[...] += dkc
            dvc_ref[...] += dvc

    qblk = pl.BlockSpec((tq, gw), lambda b, g, i: (b * nq + i, g))
    kxb = pl.BlockSpec((None, seq, HD), lambda b, g, i: (b, 0, g))
    kcb = pl.BlockSpec((None, cx, HD), lambda b, g, i: (b, 0, g))
    return pl.pallas_call(
        body, name="att_bwd", grid=(nb, HKV, nq),
        in_specs=[qblk,
                  pl.BlockSpec((seq, HD), lambda b, g, i: (b, g)),
                  pl.BlockSpec((cx, HD), lambda b, g, i: (b, g)),
                  pl.BlockSpec((seq, HD), lambda b, g, i: (b, AV // HD + g)),
                  pl.BlockSpec((cx, HD), lambda b, g, i: (t_rows // cx + b, AV // HD + g)),
                  qblk, qblk, pl.BlockSpec((tq, 128), lambda b, g, i: (b * nq + i, g))],
        out_specs=[qblk, kxb, kxb, kcb, kcb],
        out_shape=[_sds((t_rows, D), F32), _sds((nb, seq, HKV * HD), F32), _sds((nb, seq, HKV * HD), F32),
                   _sds((nb, cx, HKV * HD), F32), _sds((nb, cx, HKV * HD), F32)],
        compiler_params=_params(("parallel", "parallel", "arbitrary")),
    )(q16, kx16, kc16, px, px, dao, o_att, lse)


def _qk_prep_bwd(dt, px, nw, cos, sin, rows, row_off, col_off, heads, hb, seq, tm, name):
    rope = cos is not None
    rb0 = row_off // tm
    pb = seq // tm if rope else 1
    bw = hb * HD

    def body(*refs):
        if rope:
            d_ref, x_ref, w_ref, c_ref, s_ref, dx_ref, dw_ref = refs
        else:
            d_ref, x_ref, w_ref, dx_ref, dw_ref = refs
        first = jnp.logical_and(pl.program_id(0) == 0, pl.program_id(1) == 0)
        dw = jnp.zeros((1, HD), F32)
        for h in range(hb):
            sl = slice(h * HD, (h + 1) * HD)
            dtv = d_ref[:, sl]
            if rope:
                dtv = dtv * c_ref[...] + _swap_pairs(dtv * s_ref[...])
            xv = x_ref[:, sl].astype(F32)
            r = lax.rsqrt(jnp.mean(xv * xv, axis=-1, keepdims=True) + EPS)
            xh = xv * r
            dxh = dtv * w_ref[...]
            dx_ref[:, sl] = (r * (dxh - xh * jnp.mean(dxh * xh, axis=-1, keepdims=True))).astype(BF16)
            dw += jnp.sum(dtv * xh, axis=0, keepdims=True)

        @pl.when(first)
        def _():
            dw_ref[...] = dw

        @pl.when(jnp.logical_not(first))
        def _():
            dw_ref[...] += dw

    blk = pl.BlockSpec((tm, bw), lambda i, j: (i, j))
    in_specs = [blk, pl.BlockSpec((tm, bw), lambda i, j: (rb0 + i, col_off // bw + j)),
                pl.BlockSpec((1, HD), lambda i, j: (0, 0))]
    args = [dt, px, nw]
    if rope:
        in_specs += [pl.BlockSpec((tm, HD), lambda i, j: (i % pb, 0))] * 2
        args += [cos, sin]
    return pl.pallas_call(
        body, name=name, grid=(rows // tm, heads // hb), in_specs=in_specs,
        out_specs=[blk, pl.BlockSpec((1, HD), lambda i, j: (0, 0))],
        out_shape=[_sds((rows, heads * HD), BF16), _sds((1, HD), F32)],
        compiler_params=_params(("arbitrary", "arbitrary")),
    )(*args)


def _ret_bwd(px, lg, do16, hist_f, hist_b, nb, nc):
    t_rows = nb * nc * CH

    def body(lg_ref, *refs):
        ins = (refs[0:5], refs[7:12])
        do_refs = (refs[5], refs[12])
        h_refs = (refs[6], refs[13])
        outs = (refs[14:17], refs[17:20])
        ds_outs = (refs[20], refs[21])
        dlg_ref = refs[22]
        dss = (refs[23], refs[24])
        c = pl.program_id(1)

        @pl.when(c == 0)
        def _():
            dss[0][...] = jnp.zeros_like(dss[0])
            dss[1][...] = jnp.zeros_like(dss[1])
            dlg_ref[...] = jnp.zeros_like(dlg_ref)

        for d in range(2):
            dq_ref, dk_ref, dv_ref = outs[d]
            for h in range(RH):
                lg_d = lg_ref[d, h]
                mask, relf, qd, qe, kd, ke = _decays(lg_d, d == 0)
                g_ch = jnp.exp(lg_d * CH)
                q, k, v16 = _head_qkv(ins[d], h)
                q16 = q.astype(BF16)
                k16 = k.astype(BF16)
                do16v = do_refs[d][:, h * DV:(h + 1) * DV]
                st16 = h_refs[d][h]
                dst = dss[d][h]
                dst16 = dst.astype(BF16)
                a = _dot(q16, k16, NT) * mask
                dp = _dot(do16v, v16, NT)
                da16 = (dp * mask).astype(BF16)
                dq_cross = _dot(do16v, st16, NT) * qd
                dq_ref[:, h * DK:(h + 1) * DK] = (_dot(da16, k16) + dq_cross).astype(BF16)
                dk_state = _dot(v16, dst16, NT) * kd
                dk_ref[:, h * DK:(h + 1) * DK] = ((_dot(da16, q16, TN) + dk_state) * (DK ** -0.5)).astype(BF16)
                dv = _dot(a.astype(BF16), do16v, TN) + _dot((k * kd).astype(BF16), dst16)
                dv_ref[:, h * DV:(h + 1) * DV] = dv.astype(BF16)
                dlg = (jnp.sum(relf * a * dp)
                       + jnp.sum(qe * jnp.sum(q * dq_cross, axis=-1, keepdims=True))
                       + jnp.sum(ke * jnp.sum(k * dk_state, axis=-1, keepdims=True))
                       + CH * g_ch * jnp.sum(dst * st16.astype(F32)))
                row = d * RH + h
                dlg_ref[row:row + 1, :] += jnp.broadcast_to(dlg, (1, 128))
                ds_new = g_ch * dst + _dot((q * qd).astype(BF16), do16v, TN)
                dss[d][h] = ds_new

                @pl.when(c == nc - 1)
                def _():
                    ds_outs[d][h] = ds_new

    def fw(b, c):
        return b * nc + nc - 1 - c

    def bw(b, c):
        return b * nc + c

    def rows(rowf, width):
        return pl.BlockSpec((CH, width), lambda b, c: (rowf(b, c), 0))

    def hist(rowf):
        return pl.BlockSpec((None, None, RH, DK, DV), lambda b, c: (b, rowf(0, c), 0, 0, 0))

    st = pl.BlockSpec((None, RH, DK, DV), lambda b, c: (b, 0, 0, 0))
    in_specs = [pl.BlockSpec(memory_space=pltpu.SMEM)]
    out_specs = []
    for rowf in (fw, bw):
        in_specs += _wide_specs(rowf) + [rows(rowf, RH * DV), hist(rowf)]
        out_specs += [rows(rowf, RH * DK), rows(rowf, RH * DK), rows(rowf, RH * DV)]
    out_specs += [st, st, pl.BlockSpec((None, 8, 128), lambda b, c: (b, 0, 0))]
    qk = _sds((t_rows, RH * DK), BF16)
    vv = _sds((t_rows, RH * DV), BF16)
    return pl.pallas_call(
        body, name="ret_bwd", grid=(nb, nc), in_specs=in_specs, out_specs=out_specs,
        out_shape=[qk, qk, vv, qk, qk, vv, _sds((nb, RH, DK, DV), F32), _sds((nb, RH, DK, DV), F32),
                   _sds((nb, 8, 128), F32)],
        scratch_shapes=[pltpu.VMEM((RH, DK, DV), F32), pltpu.VMEM((RH, DK, DV), F32)],
        compiler_params=_params(("parallel", "arbitrary")),
    )(lg, *([px] * 5), do16, hist_f, *([px] * 5), do16, hist_b)


def _ctx_state_bwd(px, lg, ds_f, ds_b, nb, t_rows, cx):
    rb = t_rows // cx

    def body(lg_ref, k_ref, v_ref, dsf_ref, dsb_ref, dk_ref, dv_ref, dlg_ref):
        h = pl.program_id(1)
        pos = lax.broadcasted_iota(jnp.int32, (cx, 1), 0).astype(F32)
        k = k_ref[...].astype(F32) * (DK ** -0.5)
        v16 = v_ref[...].astype(BF16)
        dk = jnp.zeros((cx, DK), F32)
        dv = jnp.zeros((cx, DV), F32)
        dlg_ref[...] = jnp.zeros_like(dlg_ref)
        for d, (ds_ref, e) in enumerate(((dsf_ref, cx - 1.0 - pos), (dsb_ref, pos))):
            w = jnp.exp(lg_ref[d, h] * e)
            ds16 = ds_ref[...].astype(BF16)
            t = _dot(v16, ds16, NT)
            dk += t * w
            dv += _dot((k * w).astype(BF16), ds16)
            dlg = jnp.sum(e * w * jnp.sum(k * t, axis=-1, keepdims=True))
            dlg_ref[d:d + 1, :] = jnp.broadcast_to(dlg, (1, 128))
        dk_ref[...] = (dk * (DK ** -0.5)).astype(BF16)
        dv_ref[...] = dv.astype(BF16)

    st = pl.BlockSpec((None, None, DK, DV), lambda b, h: (b, h, 0, 0))
    return pl.pallas_call(
        body, name="ctx_state_bwd", grid=(nb, RH),
        in_specs=[pl.BlockSpec(memory_space=pltpu.SMEM),
                  pl.BlockSpec((cx, DK), lambda b, h: (rb + b, RK // DK + h)),
                  pl.BlockSpec((cx, DV), lambda b, h: (rb + b, RV // DV + h)), st, st],
        out_specs=[pl.BlockSpec((cx, DK), lambda b, h: (b, h)), pl.BlockSpec((cx, DV), lambda b, h: (b, h)),
                   pl.BlockSpec((None, None, 8, 128), lambda b, h: (b, h, 0, 0))],
        out_shape=[_sds((nb * cx, RH * DK), BF16), _sds((nb * cx, RH * DV), BF16), _sds((nb, RH, 8, 128), F32)],
        compiler_params=_params(("parallel", "parallel")),
    )(lg, px, px, ds_f, ds_b)


def _assemble_lat(rows_all, dk_f, dk_b, dv_f, dv_b, dak16, dvx, dq_f, dq_b, drg16, daq16, dag16, dmr16, dma16, tm):
    t_rows = dk_f.shape[0]

    def body(dkf, dkb, dvf, dvb, dak, dav, dqf, dqb, drg, daq, dag, dmr, dma, o_ref):
        o_ref[:, RK:RK + RH * DK] = (dkf[...].astype(F32) + dkb[...].astype(F32)).astype(BF16)
        o_ref[:, RV:RV + RH * DV] = (dvf[...].astype(F32) + dvb[...].astype(F32)).astype(BF16)
        o_ref[:, AK:AK + HKV * HD] = dak[...]
        o_ref[:, AV:AV + HKV * HD] = dav[...].astype(BF16)
        o_ref[:, RQ:RQ + RH * DK] = (dqf[...].astype(F32) + dqb[...].astype(F32)).astype(BF16)
        o_ref[:, RG:RG + RH * DV] = drg[...]
        o_ref[:, AQ:AQ + D] = daq[...]
        o_ref[:, AG:AG + D] = dag[...]
        o_ref[:, MR:MR + D] = dmr[...]
        o_ref[:, MA:MA + D] = dma[...]

    args = (dk_f, dk_b, dv_f, dv_b, dak16, dvx, dq_f, dq_b, drg16, daq16, dag16, dmr16, dma16)
    return pl.pallas_call(
        body, name="assemble_lat", grid=(t_rows // tm,),
        in_specs=[pl.BlockSpec((tm, a.shape[1]), lambda i: (i, 0)) for a in args],
        out_specs=pl.BlockSpec((tm, IN_COLS), lambda i: (i, 0)), out_shape=_sds((rows_all, IN_COLS), BF16),
        compiler_params=_params(("parallel",)),
    )(*args)


def _assemble_ctx(dp_all, dck16, dcv16, dcak16, dvc, t_rows, tm):
    c_rows = dck16.shape[0]
    rb = t_rows // tm

    def body(_, dck, dcv, dcak, dcav, o_ref):
        o_ref[:, RK:RK + RH * DK] = dck[...]
        o_ref[:, RV:RV + RH * DV] = dcv[...]
        o_ref[:, AK:AK + HKV * HD] = dcak[...]
        o_ref[:, AV:AV + HKV * HD] = dcav[...].astype(BF16)
        o_ref[:, KV_COLS:] = jnp.zeros((tm, IN_COLS - KV_COLS), BF16)

    args = (dck16, dcv16, dcak16, dvc)
    return pl.pallas_call(
        body, name="assemble_ctx", grid=(c_rows // tm,),
        in_specs=[pl.BlockSpec(memory_space=pl.ANY)]
        + [pl.BlockSpec((tm, a.shape[1]), lambda i: (i, 0)) for a in args],
        out_specs=pl.BlockSpec((tm, IN_COLS), lambda i: (rb + i, 0)), out_shape=_sds(dp_all.shape, BF16),
        input_output_aliases={0: 0},
        compiler_params=_params(("parallel",)),
    )(dp_all, *args)


def _norm_bwd(dh, x2, mod3, norm_w, dxn, row_off, rows_per_group, group0, tm, name):
    with_dx = dxn is not None
    rows = x2.shape[0]
    rb0 = row_off // tm
    bpg = rows_per_group // tm
    ngroups = rows // rows_per_group

    def body(*refs):
        if with_dx:
            dh_ref, x_ref, sc_ref, nw_ref, dxn_ref, dx_ref, dsh_ref, dsc_ref, dnw_ref = refs
        else:
            dh_ref, x_ref, sc_ref, nw_ref, dsh_ref, dsc_ref, dnw_ref = refs
        i = pl.program_id(0)
        dhv = dh_ref[...]
        xv = x_ref[...]
        nw = nw_ref[...]
        r = lax.rsqrt(jnp.mean(xv * xv, axis=-1, keepdims=True) + EPS)
        xh = xv * r
        dm = dhv * (1.0 + sc_ref[...])
        dsh = jnp.sum(dhv, axis=0, keepdims=True)
        dsc = jnp.sum(dhv * (xh * nw), axis=0, keepdims=True)
        dnw = jnp.sum(dm * xh, axis=0, keepdims=True)
        if with_dx:
            dxh = dm * nw
            dx_ref[...] = dxn_ref[...] + r * (dxh - xh * jnp.mean(dxh * xh, axis=-1, keepdims=True))

        @pl.when(i % bpg == 0)
        def _():
            dsh_ref[...] = dsh
            dsc_ref[...] = dsc

        @pl.when(i % bpg != 0)
        def _():
            dsh_ref[...] += dsh
            dsc_ref[...] += dsc

        @pl.when(i == 0)
        def _():
            dnw_ref[...] = dnw

        @pl.when(i > 0)
        def _():
            dnw_ref[...] += dnw

    grp = pl.BlockSpec((None, 1, D), lambda i: (i // bpg, 0, 0))
    in_specs = [pl.BlockSpec((tm, D), lambda i: (rb0 + i, 0)), pl.BlockSpec((tm, D), lambda i: (i, 0)),
                pl.BlockSpec((None, 1, D), lambda i: (group0 + i // bpg, 0, 1)),
                pl.BlockSpec((1, D), lambda i: (0, 0))]
    args = [dh, x2, mod3, norm_w]
    out_specs = [grp, grp, pl.BlockSpec((1, D), lambda i: (0, 0))]
    out_shape = [_sds((ngroups, 1, D), F32), _sds((ngroups, 1, D), F32), _sds((1, D), F32)]
    if with_dx:
        in_specs.append(pl.BlockSpec((tm, D), lambda i: (i, 0)))
        args.append(dxn)
        out_specs.insert(0, pl.BlockSpec((tm, D), lambda i: (i, 0)))
        out_shape.insert(0, _sds((rows, D), F32))
    return pl.pallas_call(
        body, name=name, grid=(rows // tm,), in_specs=in_specs, out_specs=out_specs, out_shape=out_shape,
        compiler_params=_params(("arbitrary",)),
    )(*args)


def _small_final(dmod_all, dmodc_parts, c_rows, dm_loc_rows, nw_parts, misc_parts, c_ctx, r_pad, w_ada16):
    loc = dm_loc_rows.shape[1]

    def body(dm_ref, dmc_ref, c_ref, dml_ref, nwp_ref, mp_ref, cc_ref, r_ref, w_ref,
             gb_ref, gc_ref, gnw_ref, misc_ref, gwa_ref):
        dmc = jnp.sum(dmc_ref[...], axis=0, keepdims=True)
        gb_ref[...] = jnp.sum(dm_ref[...], axis=0, keepdims=True) + dmc
        dsc = _dot(jnp.broadcast_to(dmc, (8, 3 * D)).astype(BF16), w_ref[...], NT)[0:1, :]
        gc_ref[...] = dsc * _dsilu(cc_ref[...])
        gnw_ref[...] = jnp.sum(nwp_ref[...], axis=0, keepdims=True)
        misc = jnp.sum(mp_ref[...], axis=0, keepdims=True)
        y = jnp.exp2(r_ref[...])
        lane = lax.broadcasted_iota(jnp.int32, (1, D), 1)
        is_decay = jnp.logical_and(lane >= 2 * HD, lane < 2 * HD + 2 * RH)
        misc_ref[...] = misc * jnp.where(is_decay, -(y * np.float32(np.log(2.0))) / (1.0 - y), 1.0)
        gwa_ref[...] = _dot(_silu(c_ref[...]).astype(BF16), dml_ref[...].astype(BF16), TN)

    return pl.pallas_call(
        body, name="small_final",
        out_shape=[_sds((1, 3 * D), F32), _sds((1, D), F32), _sds((1, D), F32), _sds((1, D), F32), _sds((D, loc), F32)],
        compiler_params=pltpu.CompilerParams(vmem_limit_bytes=VMEM_LIMIT),
    )(dmod_all, dmodc_parts, c_rows, dm_loc_rows, nw_parts, misc_parts, c_ctx, r_pad, w_ada16)


def _adamw(w, g, m, v, name):
    rows, cols = w.shape
    tm = _pick(rows, 448, 8)
    bc1 = 1.0 - B1 ** STEP
    bc2 = 1.0 - B2 ** STEP

    def body(w_ref, g_ref, m_ref, v_ref, d_ref, nm_ref, nv_ref):
        g_ = g_ref[...]
        nm = B1 * m_ref[...] + (1.0 - B1) * g_
        nv = B2 * v_ref[...] + (1.0 - B2) * (g_ * g_)
        nm_ref[...] = nm
        nv_ref[...] = nv
        d_ref[...] = -LR * ((nm / bc1) / (jnp.sqrt(nv / bc2) + ADAM_EPS) + WD * w_ref[...])

    blk = pl.BlockSpec((tm, cols), lambda i: (i, 0))
    return pl.pallas_call(
        body, name=name, grid=(rows // tm,), in_specs=[blk] * 4, out_specs=[blk] * 3,
        out_shape=[_sds((rows, cols), F32)] * 3, compiler_params=_params(("parallel",)),
    )(w, g, m, v)


def _mesh_pos():
    return lax.axis_index("x"), lax.axis_index("y"), lax.axis_index("c")


def _all_gather(arrs, name):
    n = len(arrs)

    def body(*refs):
        ins, outs = refs[:n], refs[n:2 * n]
        send_sems, recv_sems, local_sems = refs[2 * n:]
        x, y, c = _mesh_pos()
        me, sib = (x, y, c), (x, y, 1 - c)
        chips = [(1 - x, y), (x, 1 - y), (1 - x, 1 - y)]

        def slot(p):
            return 4 * p[0] + 2 * p[1] + p[2]

        def copy(a, k, block, to, own):
            dst = outs[a].at[slot(block)]
            return pltpu.make_async_remote_copy(
                src_ref=ins[a] if own else dst, dst_ref=dst, send_sem=send_sems.at[a, k], recv_sem=recv_sems.at[a, k],
                device_id=to, device_id_type=MESH_T)

        mine = [pltpu.make_async_copy(ins[a], outs[a].at[slot(me)], local_sems.at[a]) for a in range(n)]
        for cp in mine:
            cp.start()
        first = []
        for a in range(n):
            first.append(copy(a, 0, me, sib, True))
            first += [copy(a, 1 + j, me, (*chip, c), True) for j, chip in enumerate(chips)]
        for cp in first:
            cp.start()
        passed = []
        for j, chip in enumerate(chips):
            for a in range(n):
                copy(a, 1 + j, (*chip, c), me, False).wait_recv()
                fwd = copy(a, 4 + j, (*chip, c), sib, False)
                fwd.start()
                passed.append(fwd)
        for a in range(n):
            copy(a, 0, sib, me, False).wait_recv()
            for j, chip in enumerate(chips):
                copy(a, 4 + j, (*chip, 1 - c), me, False).wait_recv()
        for cp in first + passed:
            cp.wait_send()
        for cp in mine:
            cp.wait()

    hbm = pl.BlockSpec(memory_space=pl.ANY)
    return pl.pallas_call(
        body, name=name, in_specs=[hbm] * n, out_specs=[hbm] * n,
        out_shape=[_sds((N_DEV,) + a.shape, a.dtype) for a in arrs],
        scratch_shapes=[pltpu.SemaphoreType.DMA((n, 7)), pltpu.SemaphoreType.DMA((n, 7)), pltpu.SemaphoreType.DMA((n,))],
    )(*arrs)


def _pair_exchange(arrs, name):
    n = len(arrs)

    def body(*refs):
        ins, outs = refs[:n], refs[n:2 * n]
        send_sems, recv_sems = refs[2 * n:]
        x, y, c = _mesh_pos()
        sib = (x, y, 1 - c)
        sends = []
        for a in range(n):
            for k in range(4):
                sends.append(pltpu.make_async_remote_copy(
                    src_ref=ins[a].at[2 * k + 1 - c], dst_ref=outs[a].at[k], send_sem=send_sems.at[a, k],
                    recv_sem=recv_sems.at[a, k], device_id=sib, device_id_type=MESH_T))
        for cp in sends:
            cp.start()
        for cp in sends:
            cp.wait_recv()
        for cp in sends:
            cp.wait_send()

    hbm = pl.BlockSpec(memory_space=pl.ANY)
    return pl.pallas_call(
        body, name=name, in_specs=[hbm] * n, out_specs=[hbm] * n,
        out_shape=[_sds((4,) + a.shape[1:], a.dtype) for a in arrs],
        scratch_shapes=[pltpu.SemaphoreType.DMA((n, 4)), pltpu.SemaphoreType.DMA((n, 4))],
    )(*arrs)


def _pair_add(part, got, core, name):
    _, rows, cols = part.shape
    tm = _pick(rows, 672, 16)
    p4 = part.reshape(4, 2, rows, cols)

    def body(core_ref, p_ref, g_ref, o_ref):
        o_ref[...] = (p_ref[...].astype(F32) + g_ref[...].astype(F32)).astype(BF16)

    blk = pl.BlockSpec((None, tm, cols), lambda k, i, cr: (k, i, 0))
    return pl.pallas_call(
        body, name=name,
        grid_spec=pltpu.PrefetchScalarGridSpec(
            num_scalar_prefetch=1, grid=(4, rows // tm),
            in_specs=[pl.BlockSpec((None, None, tm, cols), lambda k, i, cr: (k, cr[0], i, 0)), blk], out_specs=blk),
        out_shape=_sds((4, rows, cols), BF16), compiler_params=_params(("parallel", "parallel")),
    )(core, p4, got)


def _chip_sum(pair_sums, landed, chip, name):
    _, rows, cols = pair_sums.shape
    tm = _pick(rows, 672, 16)

    def body(chip_ref, s_ref, l_ref, o_ref):
        acc = s_ref[...].astype(F32)
        for j in range(3):
            acc = acc + l_ref[j].astype(F32)
        o_ref[...] = acc

    return pl.pallas_call(
        body, name=name,
        grid_spec=pltpu.PrefetchScalarGridSpec(
            num_scalar_prefetch=1, grid=(rows // tm,),
            in_specs=[pl.BlockSpec((None, tm, cols), lambda i, ch: (ch[0], i, 0)),
                      pl.BlockSpec((3, tm, cols), lambda i, ch: (0, i, 0))],
            out_specs=pl.BlockSpec((tm, cols), lambda i, ch: (i, 0))),
        out_shape=_sds((rows, cols), F32), compiler_params=_params(("parallel",)),
    )(chip, pair_sums, landed)


_HBM = pl.BlockSpec(memory_space=pltpu.HBM)
_SEM = pl.BlockSpec(memory_space=pltpu.SEMAPHORE)
_EFFECT = pltpu.SideEffectType.DATAFLOW_SIDE_EFFECTING


def _chip_routes(n):
    def plan(x, y, c):
        routes = []
        for a in range(n):
            for j in range(1, 4):
                px, py = x ^ (j >> 1), y ^ (j & 1)
                routes.append((a, 2 * px + py, (px, py, c), j - 1))
        return routes
    return plan, 3 * n


def _bcast_routes(n):
    def plan(x, y, c):
        routes = []
        for a in range(n):
            for k in range(1, N_DEV):
                peer = (x ^ ((k >> 2) & 1), y ^ ((k >> 1) & 1), c ^ (k & 1))
                routes.append((a, 0, peer, 4 * x + 2 * y + c))
        return routes
    return plan, 7 * n


def _route_copies(srcs, lands, send_sems, recv_sems, routes):
    return [pltpu.make_async_remote_copy(
        src_ref=srcs[a].at[sb], dst_ref=lands[a].at[lb], send_sem=send_sems.at[r], recv_sem=recv_sems.at[r],
        device_id=peer, device_id_type=MESH_T) for r, (a, sb, peer, lb) in enumerate(routes)]


def _exchange_start(srcs, lands, routes, name, after=()):
    plan, count = routes
    n = len(srcs)
    n_in = 2 * n + len(after)

    def body(*refs):
        send_sems, recv_sems = refs[n_in], refs[n_in + 1]
        token = refs[-1]
        for cp in _route_copies(refs[:n], refs[n:2 * n], send_sems, recv_sems, plan(*_mesh_pos())):
            cp.start()
        token[...] = jnp.zeros_like(token)

    args = [pltpu.with_memory_space_constraint(a, pltpu.HBM) for a in list(srcs) + list(lands)]
    out = pl.pallas_call(
        body, name=name,
        out_shape=(pltpu.SemaphoreType.DMA((count,)), pltpu.SemaphoreType.DMA((count,)),
                   *[pltpu.HBM(a.shape, a.dtype) for a in args], _sds((8, 128), F32)),
        in_specs=[_HBM] * (2 * n) + [pl.BlockSpec(memory_space=pl.ANY)] * len(after),
        out_specs=(_SEM, _SEM, *([_HBM] * (2 * n)), pl.BlockSpec(memory_space=pltpu.VMEM)),
        input_output_aliases={i: 2 + i for i in range(2 * n)},
        compiler_params=pltpu.CompilerParams(has_side_effects=_EFFECT),
    )(*args, *after)
    return (out[0], out[1], list(out[2:2 + 2 * n]), routes), out[-1]


def _exchange_wait_some(state, after, only, name):
    send_sems, recv_sems, bufs, (plan, count) = state
    n = len(bufs) // 2

    def body(*refs):
        send_s, recv_s = refs[2 * n], refs[2 * n + 1]
        for r, cp in enumerate(_route_copies(refs[:n], refs[n:2 * n], send_s, recv_s, plan(*_mesh_pos()))):
            if only is None or r in only:
                cp.wait_send()
                cp.wait_recv()

    out = pl.pallas_call(
        body, name=name, out_shape=tuple(pltpu.HBM(a.shape, a.dtype) for a in bufs),
        in_specs=[_HBM] * (2 * n) + [_SEM, _SEM, pl.BlockSpec(memory_space=pl.ANY)], out_specs=tuple([_HBM] * (2 * n)),
        input_output_aliases={i: i for i in range(2 * n)},
        compiler_params=pltpu.CompilerParams(has_side_effects=_EFFECT),
    )(*bufs, send_sems, recv_sems, after)
    return (send_sems, recv_sems, list(out), (plan, count)), list(out[:n]), list(out[n:])


def _exchange_wait(state, after, name):
    _, srcs, lands = _exchange_wait_some(state, after, None, name)
    return srcs, lands


def _group_routes(js):
    def plan(x, y, c):
        return [(0, 0, (x ^ (j >> 1), y ^ (j & 1), c), 2 * j + c) for j in js]
    return plan, len(js)


def _pair_fill(groups, j, name, after=()):
    def body(*refs):
        g_ref, send_sem, recv_sem = refs[-3:]
        x, y, c = _mesh_pos()
        mine = g_ref.at[2 * j + c]
        to_sib = pltpu.make_async_remote_copy(src_ref=mine, dst_ref=mine, send_sem=send_sem, recv_sem=recv_sem,
                                              device_id=(x, y, 1 - c), device_id_type=MESH_T)
        to_sib.start()
        pltpu.make_async_remote_copy(src_ref=mine, dst_ref=g_ref.at[2 * j + 1 - c], send_sem=send_sem, recv_sem=recv_sem,
                                     device_id=(x, y, 1 - c), device_id_type=MESH_T).wait_recv()
        to_sib.wait_send()

    hbm = pl.BlockSpec(memory_space=pl.ANY)
    return pl.pallas_call(
        body, name=name, in_specs=[hbm] * (1 + len(after)), out_specs=hbm, out_shape=_sds(groups.shape, groups.dtype),
        input_output_aliases={0: 0},
        scratch_shapes=[pltpu.SemaphoreType.DMA, pltpu.SemaphoreType.DMA],
    )(groups, *after)


def _in_proj_group(h_all, groups, j, chip, px_prev, after, name):
    rows_all = h_all.shape[0]
    gcols = IN_COLS // 4
    tm = _pick(rows_all, 1536, 128)
    g4 = groups.reshape(4, gcols, D)

    n_lead = (1 if px_prev is not None else 0) + len(after)
    lead = ([px_prev] if px_prev is not None else []) + list(after)

    def body(chip_ref, *refs):
        h_ref, w_ref, o_ref = refs[n_lead:]
        o_ref[...] = _dot(h_ref[...], w_ref[...], NT).astype(BF16)
    return pl.pallas_call(
        body, name=name,
        grid_spec=pltpu.PrefetchScalarGridSpec(
            num_scalar_prefetch=1, grid=(rows_all // tm,),
            in_specs=[pl.BlockSpec(memory_space=pl.ANY)] * n_lead
            + [pl.BlockSpec((tm, D), lambda i, ch: (i, 0)), pl.BlockSpec((None, gcols, D), lambda i, ch: (j, 0, 0))],
            out_specs=pl.BlockSpec((tm, gcols), lambda i, ch: (i, ch[0] ^ j))),
        out_shape=_sds((rows_all, IN_COLS), BF16),
        input_output_aliases={1: 0} if px_prev is not None else {},
        compiler_params=_params(("parallel",)),
    )(chip, *lead, h_all, g4)


def _d_h_groups(dp_all, groups, chip, after):
    rows_all = dp_all.shape[0]
    gcols = IN_COLS // 4
    tm = _pick(rows_all, 1536, 128)
    g4 = groups.reshape(4, gcols, D)
    n_lead = len(after)

    def body(chip_ref, *refs):
        a_ref, w_ref, o_ref = refs[n_lead:]
        j = pl.program_id(1)
        part = _dot(a_ref[...], w_ref[...])

        @pl.when(j == 0)
        def _():
            o_ref[...] = part

        @pl.when(j > 0)
        def _():
            o_ref[...] += part

    return pl.pallas_call(
        body, name="d_h",
        grid_spec=pltpu.PrefetchScalarGridSpec(
            num_scalar_prefetch=1, grid=(rows_all // tm, 4),
            in_specs=[pl.BlockSpec(memory_space=pl.ANY)] * n_lead
            + [pl.BlockSpec((tm, gcols), lambda i, j, ch: (i, ch[0] ^ j)),
               pl.BlockSpec((None, gcols, D), lambda i, j, ch: (j, 0, 0))],
            out_specs=pl.BlockSpec((tm, D), lambda i, j, ch: (i, 0))),
        out_shape=_sds((rows_all, D), F32),
        compiler_params=_params(("parallel", "arbitrary")),
    )(chip, *after, dp_all, g4)


def _reduce_scatter_start(parts, core, name):
    got = _pair_exchange(parts, name + "_pair")
    sums = [_pair_add(p, g, core, "%s_add_%d" % (name, i)) for i, (p, g) in enumerate(zip(parts, got))]
    lands = [lax.empty((3,) + s_.shape[1:], BF16) for s_ in sums]
    return _exchange_start(sums, lands, _chip_routes(len(sums)), name + "_start")


def _reduce_scatter_finish(rs_state, after, chip, name):
    sums, landed = _exchange_wait(rs_state, after, name + "_wait")
    return [_chip_sum(s_, l_, chip, "%s_sum_%d" % (name, i)) for i, (s_, l_) in enumerate(zip(sums, landed))]


def _local_step(x, c, ctx, norm_w, ret_log2_decay, q_norm_w, k_norm_w, loss_target,
                mod, proj_in, proj_back, get_w_o, on_out_grads, on_in_grad, started=()):
    nb, seq, _ = x.shape
    cx = ctx.shape[1]
    t_rows, c_rows = nb * seq, nb * cx
    rows_all = t_rows + c_rows
    nc = seq // CH
    tm = _pick(seq, 256, 128)
    te = _pick(seq, 512, 128)
    assert cx % tm == 0 and t_rows % cx == 0 and seq % GRID_W == 0

    x2 = x.reshape(t_rows, D)
    ctx2 = ctx.reshape(c_rows, D)
    tgt = loss_target.reshape(t_rows, D)
    lg = _log_gamma(ret_log2_decay)
    cos, sin = _rope_tables(seq)

    mod3 = mod[:, None, :]
    h_all = _norm_fwd(x2, mod3, norm_w, rows_all, 0, seq, 0, None, te, "norm_fwd", after=started)
    h_all = _norm_fwd(ctx2, mod3, norm_w, rows_all, t_rows, c_rows, nb, h_all, tm, "norm_fwd_ctx")
    px = proj_in(h_all)
    s0f, s0b = _ctx_state(px, lg, nb, t_rows, cx)
    o_f, o_b, hist_f, hist_b = _ret_fwd(px, lg, s0f, s0b, nb, nc)
    yret16 = _ret_post(o_f, o_b, px, te)
    q16 = _qk_prep(px, q_norm_w, cos, sin, t_rows, 0, AQ, HQ, 4, seq, te, "q_prep")
    kx16 = _qk_prep(px, k_norm_w, cos, sin, t_rows, 0, AK, HKV, HKV, seq, te, "k_prep")
    kc16 = _qk_prep(px, k_norm_w, None, None, c_rows, t_rows, AK, HKV, HKV, seq, tm, "kc_prep")
    o_att, yatt16, lse = _att_fwd(q16, kx16, kc16, px, nb, seq, cx, te)
    w_o_ret16, w_o_att16, w_out16 = get_w_o(lse)
    a_ret, a_att, y16 = _merge(yret16, yatt16, px, w_o_ret16, w_o_att16, te)
    dxn, dout16, dgate, loss_b = _outproj(y16, w_out16, x2, tgt, mod3, nb, seq, te)

    gw_out = _matmul(y16, dout16, ta=True, tm=D, tn=D, tk=D, out_dtype=BF16, name="gw_out")
    da_ret16, da_att16, dmr16, dma16 = _bwd_merge(dout16, w_out16, px, a_ret, a_att, te)
    gw_o_ret = _matmul(yret16, da_ret16, ta=True, tm=D, tn=D, tk=D, out_dtype=BF16, name="gw_o_ret")
    gw_o_att = _matmul(yatt16, da_att16, ta=True, tm=D, tn=D, tk=D, out_dtype=BF16, name="gw_o_att")
    out_state, out_started = on_out_grads([gw_o_ret, gw_o_att, gw_out])
    do16, drg16 = _bwd_branch_ret(da_ret16, w_o_ret16, px, o_f, o_b, te, after=out_started)
    dao, dag16 = _bwd_branch_att(da_att16, w_o_att16, px, o_att, te)
    dq_rot, dkx, dvx, dkc, dvc = _att_bwd(q16, kx16, kc16, px, dao, o_att, lse, nb, seq, cx, te)
    daq16, gq = _qk_prep_bwd(dq_rot, px, q_norm_w, cos, sin, t_rows, 0, AQ, HQ, 4, seq, te, "q_prep_bwd")
    dak16, gk_lat = _qk_prep_bwd(dkx.reshape(t_rows, HKV * HD), px, k_norm_w, cos, sin, t_rows, 0, AK, HKV, HKV, seq, te,
                                 "k_prep_bwd")
    dcak16, gk_ctx = _qk_prep_bwd(dkc.reshape(c_rows, HKV * HD), px, k_norm_w, None, None, c_rows, t_rows, AK, HKV, HKV,
                                  seq, tm, "kc_prep_bwd")
    dq_f, dk_f, dv_f, dq_b, dk_b, dv_b, ds_f, ds_b, dlg_scan = _ret_bwd(px, lg, do16, hist_f, hist_b, nb, nc)
    dck16, dcv16, dlg_ctx = _ctx_state_bwd(px, lg, ds_f, ds_b, nb, t_rows, cx)
    dp_all = _assemble_lat(rows_all, dk_f, dk_b, dv_f, dv_b, dak16, dvx.reshape(t_rows, HKV * HD), dq_f, dq_b, drg16,
                           daq16, dag16, dmr16, dma16, tm)
    dp_all = _assemble_ctx(dp_all, dck16, dcv16, dcak16, dvc.reshape(c_rows, HKV * HD), t_rows, tm)
    gw_in_t = _matmul(dp_all, h_all, ta=True, tm=1536, tn=D, tk=2304, out_dtype=BF16, name="gw_in")
    in_state, in_started = on_in_grad(gw_in_t)
    dh = proj_back(dp_all, in_started)
    grad_x, dsh, dsc, gnw_lat = _norm_bwd(dh, x2, mod3, norm_w, dxn, 0, seq, 0, te, "norm_bwd")
    dsh_c, dsc_c, gnw_ctx = _norm_bwd(dh, ctx2, mod3, norm_w, None, t_rows, c_rows, nb, tm, "norm_bwd_ctx")

    dlg = (jnp.sum(dlg_scan[:, :, 0], axis=0) + jnp.sum(dlg_ctx[:, :, :2, 0], axis=0).T.reshape(2 * RH)).reshape(1, 2 * RH)
    misc = jnp.concatenate([gq, gk_lat + gk_ctx, dlg, jnp.sum(loss_b[:, 0, 0]).reshape(1, 1),
                            jnp.zeros((1, D - 2 * HD - 2 * RH - 1), F32)], axis=1)
    rows = []
    for b in range(nb):
        rows += [dsh[b], dsc[b], dgate[b]]
    rows += [dsh_c[0], dsc_c[0]] + [c[b:b + 1] for b in range(nb)] + [gnw_lat + gnw_ctx, misc]
    payload = jnp.concatenate(rows + [jnp.zeros((PAY_ROWS - len(rows), D), F32)], axis=0)
    return grad_x.reshape(nb, seq, D), out_state, in_state, payload


def _finish_small(gathered, nb, c_ctx, ret_log2_decay, w_ada16, dev):
    n_dev = gathered.shape[0]
    loc = 3 * D // n_dev
    dmod_all = gathered[:, :3 * nb].reshape(n_dev * nb, 3 * D)
    dmodc_parts = jnp.concatenate([gathered[:, 3 * nb:3 * nb + 2].reshape(n_dev, 2 * D), jnp.zeros((n_dev, D), F32)], axis=1)
    c_all = gathered[:, 3 * nb + 2:4 * nb + 2].reshape(n_dev * nb, D)
    nw_parts = gathered[:, 4 * nb + 2]
    misc_parts = gathered[:, 4 * nb + 3]
    n_rows = n_dev * nb + n_dev
    pad = (-n_rows) % 16
    c_rows = jnp.concatenate([c_all, jnp.broadcast_to(c_ctx.reshape(1, D), (n_dev, D)), jnp.zeros((pad, D), F32)], axis=0)
    dm_rows = jnp.concatenate([dmod_all, dmodc_parts, jnp.zeros((pad, 3 * D), F32)], axis=0)
    dm_loc_rows = lax.dynamic_slice_in_dim(dm_rows, dev * loc, loc, axis=1)
    r_pad = jnp.full((1, D), -1.0, F32).at[:, 2 * HD:2 * HD + 2 * RH].set(ret_log2_decay.reshape(1, 2 * RH))
    gb, gc, gnw, misc, gwa = _small_final(dmod_all, dmodc_parts, c_rows, dm_loc_rows, nw_parts, misc_parts,
                                          c_ctx.reshape(1, D), r_pad, w_ada16)
    return (gb, gc, gnw, misc[:, :HD], misc[:, HD:2 * HD], misc[:, 2 * HD:2 * HD + 2 * RH], gwa,
            misc[0, 2 * HD + 2 * RH])


def kernel(x, c, ctx, c_ctx, norm_w, w_ada, b_ada, w_in, ret_log2_decay, q_norm_w, k_norm_w, w_o_ret, w_o_att, w_out, loss_target, m_c_ctx, m_norm_w, m_w_ada, m_b_ada, m_w_in, m_ret_log2_decay, m_q_norm_w, m_k_norm_w, m_w_o_ret, m_w_o_att, m_w_out, v_c_ctx, v_norm_w, v_w_ada, v_b_ada, v_w_in, v_ret_log2_decay, v_q_norm_w, v_k_norm_w, v_w_o_ret, v_w_o_att, v_w_out):
    nb = x.shape[0]
    mx, my, mc = _mesh_pos()
    dev = 4 * mx + 2 * my + mc
    core = jnp.reshape(mc, (1,)).astype(jnp.int32)
    chip = jnp.reshape(2 * mx + my, (1,)).astype(jnp.int32)

    n_loc = 3 * D // N_DEV
    c8 = jnp.zeros((8, D), F32).at[:nb].set(c).at[nb].set(c_ctx)
    (c_all,) = _all_gather([c8], "gather_c")
    ada_shard = w_ada[0].astype(BF16)
    b_loc = lax.dynamic_slice(b_ada, (0, dev * n_loc), (1, n_loc))
    mod_cols = _mod_part(c_all.reshape(N_DEV * 8, D), ada_shard, b_loc)
    (mod_all,) = _all_gather([mod_cols], "gather_mod")
    mod = jnp.transpose(lax.dynamic_slice(mod_all, (0, dev * 8, 0), (N_DEV, 8, n_loc)), (1, 0, 2)).reshape(8, 3 * D)
    ada_land = lax.dynamic_update_slice(lax.empty((N_DEV,) + ada_shard.shape, BF16), ada_shard[None], (dev, 0, 0))

    w_in_t = jnp.transpose(w_in[0])
    in_shard = w_in_t.astype(BF16)
    groups = lax.dynamic_update_slice(lax.empty((N_DEV,) + in_shard.shape, BF16), in_shard[None], (mc, 0, 0))
    groups = _pair_fill(groups, 0, "gather_in_pair", after=(mod_all,))
    (near_send, near_recv, near_bufs, near_routes), gin_token = _exchange_start(
        [in_shard[None]], [groups], _group_routes((1, 2)), "gather_in_start")
    w_in_groups, wo_states, ada_states = [], [], []
    wo_shards = [w_[0].astype(BF16) for w_ in (w_o_ret, w_o_att, w_out)]
    wo_lands = [lax.dynamic_update_slice(lax.empty((N_DEV,) + s_.shape, BF16), s_[None], (dev, 0, 0)) for s_ in wo_shards]

    def proj_in(h_all):
        src, groups = near_bufs
        px = _in_proj_group(h_all, groups, 0, chip, None, (gin_token,), "in_proj_0")
        _, (src,), (groups,) = _exchange_wait_some((near_send, near_recv, [src, groups], near_routes), px, (0,),
                                                   "gather_in_wait_1")
        groups = _pair_fill(groups, 1, "gather_in_fill_1")
        (far_send, far_recv, (src, groups), far_routes), far_token = _exchange_start(
            [src], [groups], _group_routes((3,)), "gather_in_start_far")
        wo_state, wo_token = _exchange_start([s_[None] for s_ in wo_shards], wo_lands, _bcast_routes(3),
                                             "gather_wo_start", after=(far_token,))
        wo_states.append(wo_state)
        ada_state, ada_token = _exchange_start([ada_shard[None]], [ada_land], _bcast_routes(1), "gather_ada_start",
                                               after=(wo_token,))
        ada_states.append(ada_state)
        px = _in_proj_group(h_all, groups, 1, chip, px, (ada_token,), "in_proj_1")
        for j in (2, 3):
            state = ((near_send, near_recv, [src, groups], near_routes) if j < 3 else
                     (far_send, far_recv, [src, groups], far_routes))
            _, (src,), (groups,) = _exchange_wait_some(state, px, (1,) if j < 3 else None, "gather_in_wait_%d" % j)
            groups = _pair_fill(groups, j, "gather_in_fill_%d" % j)
            px = _in_proj_group(h_all, groups, j, chip, px, (), "in_proj_%d" % j)
        w_in_groups.append(groups)
        return px

    def proj_back(dp_all, after):
        return _d_h_groups(dp_all, w_in_groups[0], chip, after)

    def get_w_o(after):
        _, (l_ret, l_att, l_out) = _exchange_wait(wo_states[0], after, "gather_wo_wait")
        return l_ret.reshape(RH * DV, D), l_att.reshape(D, D), l_out.reshape(D, D)

    def on_out_grads(grads):
        parts = [g_.reshape(N_DEV, g_.shape[0] // N_DEV, D) for g_ in grads]
        state, token = _reduce_scatter_start(parts, core, "rs_out")
        return state, (token,)

    def on_in_grad(grad):
        state, token = _reduce_scatter_start([grad.reshape(N_DEV, IN_COLS // N_DEV, D)], core, "rs_in")
        return state, (token,)

    grad_x, out_state, in_state, payload = _local_step(
        x, c, ctx, norm_w, ret_log2_decay, q_norm_w, k_norm_w, loss_target,
        mod, proj_in, proj_back, get_w_o, on_out_grads, on_in_grad, started=(gin_token,))

    (gathered,) = _all_gather([payload], "gather_small")
    _, (l_ada,) = _exchange_wait(ada_states[0], gathered, "gather_ada_wait")
    w_ada16 = jnp.transpose(l_ada, (1, 0, 2)).reshape(D, 3 * D)
    gb, gc, gnw, gq, gk, gr, gwa, loss = _finish_small(gathered, nb, c_ctx, ret_log2_decay, w_ada16, dev)

    g_w_o_ret, g_w_o_att, g_w_out = _reduce_scatter_finish(out_state, gathered, chip, "rs_out")
    (g_w_in_t,) = _reduce_scatter_finish(in_state, gathered, chip, "rs_in")

    grads = [gc.reshape(c_ctx.shape), gnw, gwa[None], gb, g_w_in_t, gr.reshape(ret_log2_decay.shape), gq, gk,
             g_w_o_ret[None], g_w_o_att[None], g_w_out[None]]
    weights = [c_ctx, norm_w, w_ada, b_ada, w_in_t, ret_log2_decay, q_norm_w, k_norm_w, w_o_ret, w_o_att, w_out]
    ms = [m_c_ctx, m_norm_w, m_w_ada, m_b_ada, jnp.transpose(m_w_in[0]), m_ret_log2_decay, m_q_norm_w, m_k_norm_w,
          m_w_o_ret, m_w_o_att, m_w_out]
    vs = [v_c_ctx, v_norm_w, v_w_ada, v_b_ada, jnp.transpose(v_w_in[0]), v_ret_log2_decay, v_q_norm_w, v_k_norm_w,
          v_w_o_ret, v_w_o_att, v_w_out]
    deltas, new_ms, new_vs = [], [], []
    for i, (w, g, m, v) in enumerate(zip(weights, grads, ms, vs)):
        shape2 = (-1, w.shape[-1])
        res = _adamw(w.reshape(shape2), g.reshape(shape2), m.reshape(shape2), v.reshape(shape2), "adamw_%d" % i)
        for lst, r in zip((deltas, new_ms, new_vs), res):
            lst.append(jnp.transpose(r)[None] if i == 4 else r.reshape(w.shape))
    grads[4] = jnp.transpose(g_w_in_t)[None]
    return (loss, grad_x, *grads, *deltas, *new_ms, *new_vs)
```

```python
import numpy as np
import jax
import jax.numpy as jnp
from jax import lax
from jax.experimental import pallas as pl
from jax.experimental.pallas import tpu as pltpu

F32 = jnp.float32
BF16 = jnp.bfloat16

D = 1024
RH, DK, DV, CH = 4, 256, 512, 256
HQ, HKV, HD = 8, 2, 128
GRID_W = 64
ROPE_THETA = 10000.0
EPS = 1e-6
RK, RV, AK, AV, RQ, RG, AQ, AG, MR, MA = 0, 1024, 3072, 3328, 3584, 4608, 6656, 7680, 8704, 9728
IN_COLS = 10752
KV_COLS = 3584
N_DEV = 8
LR, B1, B2, ADAM_EPS, WD, STEP = 0.001, 0.9, 0.999, 1e-08, 0.01, 10
PAY_ROWS = 16
VMEM_LIMIT = 56 * 1024 * 1024
MESH_T = pl.DeviceIdType.MESH

NT = (((1,), (1,)), ((), ()))
TN = (((0,), (0,)), ((), ()))
SM_C = (HD ** -0.5) * float(np.log2(np.e))


def _params(sem):
    return pltpu.CompilerParams(dimension_semantics=sem, vmem_limit_bytes=VMEM_LIMIT)


def _pick(n, target, mult=8):
    best = None
    for t in range(mult, min(n, target) + 1, mult):
        if n % t == 0:
            best = t
    return best or n


def _dot(a, b, dn=None):
    if dn is None:
        return jnp.dot(a, b, preferred_element_type=F32)
    return lax.dot_general(a, b, dn, preferred_element_type=F32)


def _sig(v):
    return jax.nn.sigmoid(v)


def _silu(v):
    return v * _sig(v)


def _dsilu(v):
    s = _sig(v)
    return s * (1.0 + v * (1.0 - s))


def _sds(shape, dtype):
    return jax.ShapeDtypeStruct(shape, dtype)


def _matmul(a, b, *, ta=False, tb=False, tm, tn, tk, out_dtype, name, after=()):
    m = a.shape[1] if ta else a.shape[0]
    kdim = a.shape[0] if ta else a.shape[1]
    n = b.shape[0] if tb else b.shape[1]
    tm, tn, tk = _pick(m, tm, 128), _pick(n, tn, 128), _pick(kdim, tk, 128)
    nk = kdim // tk
    dn = (((0 if ta else 1,), (1 if tb else 0,)), ((), ()))

    def body(a_ref, b_ref, *rest):
        o_ref, acc_ref = rest[-2:]
        k = pl.program_id(2)
        part = _dot(a_ref[...].astype(BF16), b_ref[...].astype(BF16), dn)
        if nk == 1:
            o_ref[...] = part.astype(o_ref.dtype)
        else:
            @pl.when(k == 0)
            def _():
                acc_ref[...] = part

            @pl.when(k > 0)
            def _():
                acc_ref[...] += part

            @pl.when(k == nk - 1)
            def _():
                o_ref[...] = acc_ref[...].astype(o_ref.dtype)

    a_spec = pl.BlockSpec((tk, tm), lambda i, j, k: (k, i)) if ta else pl.BlockSpec((tm, tk), lambda i, j, k: (i, k))
    b_spec = pl.BlockSpec((tn, tk), lambda i, j, k: (j, k)) if tb else pl.BlockSpec((tk, tn), lambda i, j, k: (k, j))
    return pl.pallas_call(
        body, name=name, grid=(m // tm, n // tn, nk),
        in_specs=[a_spec, b_spec] + [pl.BlockSpec(memory_space=pl.ANY)] * len(after),
        out_specs=pl.BlockSpec((tm, tn), lambda i, j, k: (i, j)), out_shape=_sds((m, n), out_dtype),
        scratch_shapes=[pltpu.VMEM((tm, tn) if nk > 1 else (8, 128), F32)],
        compiler_params=_params(("parallel", "parallel", "arbitrary")),
    )(a, b, *after)


def _log_gamma(r):
    rp = jnp.full((8, 128), -1.0, F32).at[:2, :RH].set(r.reshape(2, RH))

    def body(r_ref, o_ref):
        o_ref[...] = jnp.log1p(-jnp.exp2(r_ref[...]))

    out = pl.pallas_call(body, name="log_gamma", out_shape=_sds((8, 128), F32))(rp)
    return out[:2, :RH]


def _mod_part(c_rows, w_ada_loc16, b_loc):
    def body(c_ref, w_ref, b_ref, o_ref):
        o_ref[...] = _dot(_silu(c_ref[...]).astype(BF16), w_ref[...]) + b_ref[...]

    return pl.pallas_call(
        body, name="mod_part", out_shape=_sds((c_rows.shape[0], w_ada_loc16.shape[1]), F32),
    )(c_rows, w_ada_loc16, b_loc)


def _norm_fwd(x2, mod3, norm_w, rows_all, row_off, rows_per_group, group0, h_prev, tm, name, after=()):
    rows = x2.shape[0]
    rb0 = row_off // tm
    bpg = rows_per_group // tm

    def body(*refs):
        x_ref, sh_ref, sc_ref, nw_ref, o_ref = refs[-5:]
        xv = x_ref[...]
        r = lax.rsqrt(jnp.mean(xv * xv, axis=-1, keepdims=True) + EPS)
        o_ref[...] = ((xv * r) * nw_ref[...] * (1.0 + sc_ref[...]) + sh_ref[...]).astype(BF16)

    in_specs = [pl.BlockSpec((tm, D), lambda i: (i, 0)),
                pl.BlockSpec((None, 1, D), lambda i: (group0 + i // bpg, 0, 0)),
                pl.BlockSpec((None, 1, D), lambda i: (group0 + i // bpg, 0, 1)),
                pl.BlockSpec((1, D), lambda i: (0, 0))]
    in_specs = [pl.BlockSpec(memory_space=pl.ANY)] * len(after) + in_specs
    args = list(after) + [x2, mod3, mod3, norm_w]
    alias = {}
    if h_prev is not None:
        in_specs.insert(0, pl.BlockSpec(memory_space=pl.ANY))
        args.insert(0, h_prev)
        alias = {0: 0}
    return pl.pallas_call(
        body, name=name, grid=(rows // tm,), in_specs=in_specs,
        out_specs=pl.BlockSpec((tm, D), lambda i: (rb0 + i, 0)), out_shape=_sds((rows_all, D), BF16),
        input_output_aliases=alias, compiler_params=_params(("parallel",)),
    )(*args)


def _decays(lg, fwd):
    ii = lax.broadcasted_iota(jnp.int32, (CH, CH), 0)
    jj = lax.broadcasted_iota(jnp.int32, (CH, CH), 1)
    ri = lax.broadcasted_iota(jnp.int32, (CH, 1), 0).astype(F32)
    rel = (ii - jj) if fwd else (jj - ii)
    relf = jnp.maximum(rel, 0).astype(F32)
    mask = jnp.where(rel >= 0, jnp.exp(lg * relf), 0.0)
    qe = (ri + 1.0) if fwd else (CH - ri)
    ke = (CH - 1.0 - ri) if fwd else ri
    return mask, relf, jnp.exp(lg * qe), qe, jnp.exp(lg * ke), ke


def _wide_specs(rowf):
    return [pl.BlockSpec((CH, 2 * DK), lambda b, c: (rowf(b, c), RQ // (2 * DK))),
            pl.BlockSpec((CH, 2 * DK), lambda b, c: (rowf(b, c), RQ // (2 * DK) + 1)),
            pl.BlockSpec((CH, RH * DK), lambda b, c: (rowf(b, c), RK // (RH * DK))),
            pl.BlockSpec((CH, 2 * DV), lambda b, c: (rowf(b, c), RV // (2 * DV))),
            pl.BlockSpec((CH, 2 * DV), lambda b, c: (rowf(b, c), RV // (2 * DV) + 1))]


def _head_qkv(refs, h):
    q0, q1, k, v0, v1 = refs
    lo = h % 2
    q = (q0, q1)[h // 2][:, lo * DK:(lo + 1) * DK].astype(F32)
    kk = k[:, h * DK:(h + 1) * DK].astype(F32) * (DK ** -0.5)
    v16 = (v0, v1)[h // 2][:, lo * DV:(lo + 1) * DV].astype(BF16)
    return q, kk, v16


def _ctx_state(px, lg, nb, t_rows, cx):
    rb = t_rows // cx

    def body(lg_ref, k_ref, v_ref, sf_ref, sb_ref):
        h = pl.program_id(1)
        pos = lax.broadcasted_iota(jnp.int32, (cx, 1), 0).astype(F32)
        k = k_ref[...].astype(F32) * (DK ** -0.5)
        v16 = v_ref[...].astype(BF16)
        wf = jnp.exp(lg_ref[0, h] * (cx - 1.0 - pos))
        wb = jnp.exp(lg_ref[1, h] * pos)
        sf_ref[...] = _dot((k * wf).astype(BF16), v16, TN)
        sb_ref[...] = _dot((k * wb).astype(BF16), v16, TN)

    st = pl.BlockSpec((None, None, DK, DV), lambda b, h: (b, h, 0, 0))
    return pl.pallas_call(
        body, name="ctx_state", grid=(nb, RH),
        in_specs=[pl.BlockSpec(memory_space=pltpu.SMEM),
                  pl.BlockSpec((cx, DK), lambda b, h: (rb + b, RK // DK + h)),
                  pl.BlockSpec((cx, DV), lambda b, h: (rb + b, RV // DV + h))],
        out_specs=[st, st], out_shape=[_sds((nb, RH, DK, DV), F32)] * 2,
        compiler_params=_params(("parallel", "parallel")),
    )(lg, px, px)


def _ret_fwd(px, lg, s0f, s0b, nb, nc):
    t_rows = nb * nc * CH

    def body(lg_ref, *refs):
        ins = (refs[0:5], refs[5:10])
        s0f_ref, s0b_ref, of_ref, ob_ref, hf_ref, hb_ref, sf, sb = refs[10:]
        c = pl.program_id(1)

        @pl.when(c == 0)
        def _():
            sf[...] = s0f_ref[...]
            sb[...] = s0b_ref[...]

        for d, (o_ref, h_ref, s) in enumerate(((of_ref, hf_ref, sf), (ob_ref, hb_ref, sb))):
            for h in range(RH):
                lg_d = lg_ref[d, h]
                mask, _, qd, _, kd, _ = _decays(lg_d, d == 0)
                q, k, v16 = _head_qkv(ins[d], h)
                a = _dot(q.astype(BF16), k.astype(BF16), NT)
                st = s[h]
                st16 = st.astype(BF16)
                h_ref[h] = st16
                o = _dot((a * mask).astype(BF16), v16) + _dot((q * qd).astype(BF16), st16)
                o_ref[:, h * DV:(h + 1) * DV] = o.astype(BF16)
                s[h] = st * jnp.exp(lg_d * CH) + _dot((k * kd).astype(BF16), v16, TN)

    def fw(b, c):
        return b * nc + c

    def bw(b, c):
        return b * nc + nc - 1 - c

    st = pl.BlockSpec((None, RH, DK, DV), lambda b, c: (b, 0, 0, 0))
    in_specs = [pl.BlockSpec(memory_space=pltpu.SMEM)] + _wide_specs(fw) + _wide_specs(bw) + [st, st]
    out_specs = [pl.BlockSpec((CH, RH * DV), lambda b, c: (fw(b, c), 0)),
                 pl.BlockSpec((CH, RH * DV), lambda b, c: (bw(b, c), 0)),
                 pl.BlockSpec((None, None, RH, DK, DV), lambda b, c: (b, c, 0, 0, 0)),
                 pl.BlockSpec((None, None, RH, DK, DV), lambda b, c: (b, nc - 1 - c, 0, 0, 0))]
    return pl.pallas_call(
        body, name="ret_fwd", grid=(nb, nc), in_specs=in_specs, out_specs=out_specs,
        out_shape=[_sds((t_rows, RH * DV), BF16)] * 2 + [_sds((nb, nc, RH, DK, DV), BF16)] * 2,
        scratch_shapes=[pltpu.VMEM((RH, DK, DV), F32), pltpu.VMEM((RH, DK, DV), F32)],
        compiler_params=_params(("parallel", "arbitrary")),
    )(lg, *([px] * 10), s0f, s0b)


def _ret_post(o_f, o_b, px, tm):
    t_rows = o_f.shape[0]

    def body(of_ref, ob_ref, g0, g1, g2, g3, y_ref):
        for h, g_ref in enumerate((g0, g1, g2, g3)):
            sl = slice(h * DV, (h + 1) * DV)
            o = of_ref[:, sl].astype(F32) + ob_ref[:, sl].astype(F32)
            r = lax.rsqrt(jnp.mean(o * o, axis=-1, keepdims=True) + EPS)
            y_ref[:, sl] = ((o * r) * _silu(g_ref[...].astype(F32))).astype(BF16)

    def gate(h):
        return pl.BlockSpec((tm, DV), lambda i: (i, RG // DV + h))

    wide = pl.BlockSpec((tm, RH * DV), lambda i: (i, 0))
    return pl.pallas_call(
        body, name="ret_post", grid=(t_rows // tm,),
        in_specs=[wide, wide] + [gate(h) for h in range(RH)],
        out_specs=wide, out_shape=_sds((t_rows, RH * DV), BF16),
        compiler_params=_params(("parallel",)),
    )(o_f, o_b, *([px] * RH))


def _rope_tables(seq):
    rows = seq // GRID_W
    row = np.repeat(np.arange(rows, dtype=np.float32), GRID_W)
    col = np.tile(np.arange(GRID_W, dtype=np.float32), rows)
    half = HD // 2
    freqs = (ROPE_THETA ** (-np.arange(0, half, 2, dtype=np.float32) / half)).astype(np.float32)
    ang = np.concatenate([row[:, None] * freqs, col[:, None] * freqs], axis=-1).astype(np.float32)
    cos = np.repeat(np.cos(ang), 2, axis=-1).astype(np.float32)
    sin = np.repeat(np.sin(ang), 2, axis=-1).astype(np.float32)
    sign = np.tile(np.array([-1.0, 1.0], np.float32), HD // 2)
    return jnp.asarray(cos), jnp.asarray(sin * sign)


def _swap_pairs(v):
    lane = lax.broadcasted_iota(jnp.int32, v.shape, 1)
    return jnp.where((lane & 1) == 0, pltpu.roll(v, HD - 1, 1), pltpu.roll(v, 1, 1))


def _qk_prep(px, nw, cos, sin, rows, row_off, col_off, heads, hb, seq, tm, name):
    rope = cos is not None
    rb0 = row_off // tm
    pb = seq // tm if rope else 1
    bw = hb * HD

    def body(*refs):
        if rope:
            x_ref, w_ref, c_ref, s_ref, o_ref = refs
        else:
            x_ref, w_ref, o_ref = refs
        for h in range(hb):
            sl = slice(h * HD, (h + 1) * HD)
            xv = x_ref[:, sl].astype(F32)
            r = lax.rsqrt(jnp.mean(xv * xv, axis=-1, keepdims=True) + EPS)
            t = (xv * r) * w_ref[...]
            if rope:
                t = t * c_ref[...] + _swap_pairs(t) * s_ref[...]
            o_ref[:, sl] = t.astype(BF16)

    in_specs = [pl.BlockSpec((tm, bw), lambda i, j: (rb0 + i, col_off // bw + j)),
                pl.BlockSpec((1, HD), lambda i, j: (0, 0))]
    args = [px, nw]
    if rope:
        in_specs += [pl.BlockSpec((tm, HD), lambda i, j: (i % pb, 0))] * 2
        args += [cos, sin]
    return pl.pallas_call(
        body, name=name, grid=(rows // tm, heads // hb), in_specs=in_specs,
        out_specs=pl.BlockSpec((tm, bw), lambda i, j: (i, j)), out_shape=_sds((rows, heads * HD), BF16),
        compiler_params=_params(("parallel", "parallel")),
    )(*args)


def _att_fwd(q16, kx16, kc16, px, nb, seq, cx, tq):
    t_rows = nb * seq
    nq = seq // tq
    rep = HQ // HKV
    gw = rep * HD

    def body(q_ref, kx_ref, kc_ref, vx_ref, vc_ref, g_ref, o_ref, y_ref, l_ref):
        kx = kx_ref[...]
        kc = kc_ref[...]
        vx = vx_ref[...].astype(BF16)
        vc = vc_ref[...].astype(BF16)
        l_ref[...] = jnp.zeros_like(l_ref)
        for r in range(rep):
            sl = slice(r * HD, (r + 1) * HD)
            q = q_ref[:, sl]
            s1 = _dot(q, kx, NT)
            s2 = _dot(q, kc, NT)
            m = jnp.maximum(jnp.max(s1, axis=-1, keepdims=True), jnp.max(s2, axis=-1, keepdims=True))
            e1 = jnp.exp2((s1 - m) * SM_C)
            e2 = jnp.exp2((s2 - m) * SM_C)
            tot = jnp.sum(e1, axis=-1, keepdims=True) + jnp.sum(e2, axis=-1, keepdims=True)
            o = (_dot(e1.astype(BF16), vx) + _dot(e2.astype(BF16), vc)) * (1.0 / tot)
            o_ref[:, sl] = o
            y_ref[:, sl] = (o * _silu(g_ref[:, sl].astype(F32))).astype(BF16)
            l_ref[:, r:r + 1] = m * SM_C + jnp.log(tot) * float(np.log2(np.e))

    qblk = pl.BlockSpec((tq, gw), lambda b, g, i: (b * nq + i, g))
    return pl.pallas_call(
        body, name="att_fwd", grid=(nb, HKV, nq),
        in_specs=[qblk,
                  pl.BlockSpec((seq, HD), lambda b, g, i: (b, g)),
                  pl.BlockSpec((cx, HD), lambda b, g, i: (b, g)),
                  pl.BlockSpec((seq, HD), lambda b, g, i: (b, AV // HD + g)),
                  pl.BlockSpec((cx, HD), lambda b, g, i: (t_rows // cx + b, AV // HD + g)),
                  pl.BlockSpec((tq, gw), lambda b, g, i: (b * nq + i, AG // gw + g))],
        out_specs=[qblk, qblk, pl.BlockSpec((tq, 128), lambda b, g, i: (b * nq + i, g))],
        out_shape=[_sds((t_rows, D), F32), _sds((t_rows, D), BF16), _sds((t_rows, HKV * 128), F32)],
        compiler_params=_params(("parallel", "parallel", "parallel")),
    )(q16, kx16, kc16, px, px, px)


def _gate_specs(tm, col0):
    hw = D // 2
    return [pl.BlockSpec((tm, hw), lambda i: (i, col0 // hw)), pl.BlockSpec((tm, hw), lambda i: (i, col0 // hw + 1))]


def _merge(yret16, yatt16, px, w_o_ret16, w_o_att16, tm):
    t_rows = yret16.shape[0]
    hw = D // 2

    def body(yr_ref, wr_ref, ya_ref, wa_ref, mr0, mr1, ma0, ma1, ar_ref, aa_ref, y_ref):
        ar = _dot(yr_ref[...], wr_ref[...])
        aa = _dot(ya_ref[...], wa_ref[...])
        ar_ref[...] = ar.astype(BF16)
        aa_ref[...] = aa.astype(BF16)
        for j, (mr_ref, ma_ref) in enumerate(((mr0, ma0), (mr1, ma1))):
            sl = slice(j * hw, (j + 1) * hw)
            y_ref[:, sl] = (_sig(mr_ref[...].astype(F32)) * ar[:, sl]
                            + _sig(ma_ref[...].astype(F32)) * aa[:, sl]).astype(BF16)

    row = pl.BlockSpec((tm, D), lambda i: (i, 0))
    return pl.pallas_call(
        body, name="merge", grid=(t_rows // tm,),
        in_specs=[pl.BlockSpec((tm, RH * DV), lambda i: (i, 0)), pl.BlockSpec((RH * DV, D), lambda i: (0, 0)),
                  row, pl.BlockSpec((D, D), lambda i: (0, 0))] + _gate_specs(tm, MR) + _gate_specs(tm, MA),
        out_specs=[row, row, row], out_shape=[_sds((t_rows, D), BF16)] * 3,
        compiler_params=_params(("parallel",)),
    )(yret16, w_o_ret16, yatt16, w_o_att16, px, px, px, px)


def _outproj(y16, w_out16, x2, tgt, mod3, nb, seq, tm):
    t_rows = nb * seq
    bpb = seq // tm

    def body(y_ref, w_ref, x_ref, t_ref, g_ref, dxn_ref, dout_ref, dg_ref, loss_ref):
        i = pl.program_id(1)
        out = _dot(y_ref[...], w_ref[...])
        gate = g_ref[...]
        diff = x_ref[...] + gate * out - t_ref[...]
        dxn = diff * (1.0 / D)
        dxn_ref[...] = dxn
        dout_ref[...] = (gate * dxn).astype(BF16)
        dg = jnp.sum(dxn * out, axis=0, keepdims=True)
        ls = jnp.broadcast_to(jnp.sum(diff * diff) * (0.5 / D), (1, 128))

        @pl.when(i == 0)
        def _():
            dg_ref[...] = dg
            loss_ref[...] = ls

        @pl.when(i > 0)
        def _():
            dg_ref[...] += dg
            loss_ref[...] += ls

    row = pl.BlockSpec((tm, D), lambda b, i: (b * bpb + i, 0))
    return pl.pallas_call(
        body, name="outproj", grid=(nb, bpb),
        in_specs=[row, pl.BlockSpec((D, D), lambda b, i: (0, 0)), row, row,
                  pl.BlockSpec((None, 1, D), lambda b, i: (b, 0, 2))],
        out_specs=[row, row, pl.BlockSpec((None, 1, D), lambda b, i: (b, 0, 0)),
                   pl.BlockSpec((None, 1, 128), lambda b, i: (b, 0, 0))],
        out_shape=[_sds((t_rows, D), F32), _sds((t_rows, D), BF16), _sds((nb, 1, D), F32), _sds((nb, 1, 128), F32)],
        compiler_params=_params(("parallel", "arbitrary")),
    )(y16, w_out16, x2, tgt, mod3)


def _bwd_merge(dout16, w_out16, px, a_ret, a_att, tm):
    t_rows = dout16.shape[0]
    hw = D // 2

    def body(do_ref, w_ref, mr0, mr1, ma0, ma1, ar_ref, aa_ref, dar_ref, daa_ref, dmr_ref, dma_ref):
        dy_all = _dot(do_ref[...], w_ref[...], NT)
        for j, (mr_ref, ma_ref) in enumerate(((mr0, ma0), (mr1, ma1))):
            sl = slice(j * hw, (j + 1) * hw)
            dy = dy_all[:, sl]
            sr = _sig(mr_ref[...].astype(F32))
            sa = _sig(ma_ref[...].astype(F32))
            dar_ref[:, sl] = (dy * sr).astype(BF16)
            daa_ref[:, sl] = (dy * sa).astype(BF16)
            dmr_ref[:, sl] = (dy * ar_ref[:, sl].astype(F32) * sr * (1.0 - sr)).astype(BF16)
            dma_ref[:, sl] = (dy * aa_ref[:, sl].astype(F32) * sa * (1.0 - sa)).astype(BF16)

    row = pl.BlockSpec((tm, D), lambda i: (i, 0))
    return pl.pallas_call(
        body, name="bwd_merge", grid=(t_rows // tm,),
        in_specs=[row, pl.BlockSpec((D, D), lambda i: (0, 0))] + _gate_specs(tm, MR) + _gate_specs(tm, MA) + [row, row],
        out_specs=[row] * 4, out_shape=[_sds((t_rows, D), BF16)] * 4,
        compiler_params=_params(("parallel",)),
    )(dout16, w_out16, px, px, px, px, a_ret, a_att)


def _bwd_branch_ret(da_ret16, w_o_ret16, px, o_f, o_b, tm, after=()):
    t_rows = da_ret16.shape[0]

    def body(da_ref, w_ref, g0, g1, g2, g3, of_ref, ob_ref, *rest):
        do_ref, dg_ref = rest[-2:]
        da = da_ref[...]
        for h, g_ref in enumerate((g0, g1, g2, g3)):
            sl = slice(h * DV, (h + 1) * DV)
            dy = _dot(da, w_ref[sl, :], NT)
            g = g_ref[...].astype(F32)
            o = of_ref[:, sl].astype(F32) + ob_ref[:, sl].astype(F32)
            r = lax.rsqrt(jnp.mean(o * o, axis=-1, keepdims=True) + EPS)
            on = o * r
            sg = _sig(g)
            don = dy * (g * sg)
            dg_ref[:, sl] = (dy * on * (sg * (1.0 + g * (1.0 - sg)))).astype(BF16)
            do_ref[:, sl] = (r * (don - on * jnp.mean(on * don, axis=-1, keepdims=True))).astype(BF16)

    def gate(h):
        return pl.BlockSpec((tm, DV), lambda i: (i, RG // DV + h))

    wide = pl.BlockSpec((tm, RH * DV), lambda i: (i, 0))
    return pl.pallas_call(
        body, name="bwd_branch_ret", grid=(t_rows // tm,),
        in_specs=[pl.BlockSpec((tm, D), lambda i: (i, 0)), pl.BlockSpec((RH * DV, D), lambda i: (0, 0))]
        + [gate(h) for h in range(RH)] + [wide, wide] + [pl.BlockSpec(memory_space=pl.ANY)] * len(after),
        out_specs=[wide, wide], out_shape=[_sds((t_rows, RH * DV), BF16)] * 2,
        compiler_params=_params(("parallel",)),
    )(da_ret16, w_o_ret16, *([px] * RH), o_f, o_b, *after)


def _bwd_branch_att(da_att16, w_o_att16, px, o_att, tm):
    t_rows = da_att16.shape[0]
    hw = D // 2

    def body(da_ref, w_ref, g0, g1, o_ref, dao_ref, dg_ref):
        dy_all = _dot(da_ref[...], w_ref[...], NT)
        for j, g_ref in enumerate((g0, g1)):
            sl = slice(j * hw, (j + 1) * hw)
            dy = dy_all[:, sl]
            g = g_ref[...].astype(F32)
            sg = _sig(g)
            dao_ref[:, sl] = dy * (g * sg)
            dg_ref[:, sl] = (dy * o_ref[:, sl] * (sg * (1.0 + g * (1.0 - sg)))).astype(BF16)

    row = pl.BlockSpec((tm, D), lambda i: (i, 0))
    return pl.pallas_call(
        body, name="bwd_branch_att", grid=(t_rows // tm,),
        in_specs=[row, pl.BlockSpec((D, D), lambda i: (0, 0))] + _gate_specs(tm, AG) + [row],
        out_specs=[row, row], out_shape=[_sds((t_rows, D), F32), _sds((t_rows, D), BF16)],
        compiler_params=_params(("parallel",)),
    )(da_att16, w_o_att16, px, px, o_att)


def _att_bwd(q16, kx16, kc16, px, dao, o_att, lse, nb, seq, cx, tq):
    t_rows = nb * seq
    nq = seq // tq
    rep = HQ // HKV
    gw = rep * HD
    scale = HD ** -0.5

    def body(q_ref, kx_ref, kc_ref, vx_ref, vc_ref, dao_ref, o_ref, l_ref, dq_ref, dkx_ref, dvx_ref, dkc_ref, dvc_ref):
        i = pl.program_id(2)
        kx = kx_ref[...]
        kc = kc_ref[...]
        vx = vx_ref[...].astype(BF16)
        vc = vc_ref[...].astype(BF16)
        dkx = jnp.zeros((seq, HD), F32)
        dvx = jnp.zeros((seq, HD), F32)
        dkc = jnp.zeros((cx, HD), F32)
        dvc = jnp.zeros((cx, HD), F32)
        for r in range(rep):
            sl = slice(r * HD, (r + 1) * HD)
            q = q_ref[:, sl]
            lr = l_ref[:, r:r + 1]
            p1 = jnp.exp2(_dot(q, kx, NT) * SM_C - lr)
            p2 = jnp.exp2(_dot(q, kc, NT) * SM_C - lr)
            da = dao_ref[:, sl]
            da16 = da.astype(BF16)
            delta = jnp.sum(da * o_ref[:, sl], axis=-1, keepdims=True)
            ds1 = (p1 * (_dot(da16, vx, NT) - delta)).astype(BF16)
            ds2 = (p2 * (_dot(da16, vc, NT) - delta)).astype(BF16)
            dq_ref[:, sl] = (_dot(ds1, kx) + _dot(ds2, kc)) * scale
            dkx += _dot(ds1, q, TN)
            dkc += _dot(ds2, q, TN)
            dvx += _dot(p1.astype(BF16), da16, TN)
            dvc += _dot(p2.astype(BF16), da16, TN)
        dkx = dkx * scale
        dkc = dkc * scale

        @pl.when(i == 0)
        def _():
            dkx_ref[...] = dkx
            dvx_ref[...] = dvx
            dkc_ref[...] = dkc
            dvc_ref[...] = dvc

        @pl.when(i > 0)
        def _():
            dkx_ref[...] += dkx
            dvx_ref[...] += dvx
            dkc_ref[...] += dkc
            dvc_ref[...] += dvc

    qblk = pl.BlockSpec((tq, gw), lambda b, g, i: (b * nq + i, g))
    kxb = pl.BlockSpec((None, seq, HD), lambda b, g, i: (b, 0, g))
    kcb = pl.BlockSpec((None, cx, HD), lambda b, g, i: (b, 0, g))
    return pl.pallas_call(
        body, name="att_bwd", grid=(nb, HKV, nq),
        in_specs=[qblk,
                  pl.BlockSpec((seq, HD), lambda b, g, i: (b, g)),
                  pl.BlockSpec((cx, HD), lambda b, g, i: (b, g)),
                  pl.BlockSpec((seq, HD), lambda b, g, i: (b, AV // HD + g)),
                  pl.BlockSpec((cx, HD), lambda b, g, i: (t_rows // cx + b, AV // HD + g)),
                  qblk, qblk, pl.BlockSpec((tq, 128), lambda b, g, i: (b * nq + i, g))],
        out_specs=[qblk, kxb, kxb, kcb, kcb],
        out_shape=[_sds((t_rows, D), F32), _sds((nb, seq, HKV * HD), F32), _sds((nb, seq, HKV * HD), F32),
                   _sds((nb, cx, HKV * HD), F32), _sds((nb, cx, HKV * HD), F32)],
        compiler_params=_params(("parallel", "parallel", "arbitrary")),
    )(q16, kx16, kc16, px, px, dao, o_att, lse)


def _qk_prep_bwd(dt, px, nw, cos, sin, rows, row_off, col_off, heads, hb, seq, tm, name):
    rope = cos is not None
    rb0 = row_off // tm
    pb = seq // tm if rope else 1
    bw = hb * HD

    def body(*refs):
        if rope:
            d_ref, x_ref, w_ref, c_ref, s_ref, dx_ref, dw_ref = refs
        else:
            d_ref, x_ref, w_ref, dx_ref, dw_ref = refs
        first = jnp.logical_and(pl.program_id(0) == 0, pl.program_id(1) == 0)
        dw = jnp.zeros((1, HD), F32)
        for h in range(hb):
            sl = slice(h * HD, (h + 1) * HD)
            dtv = d_ref[:, sl]
            if rope:
                dtv = dtv * c_ref[...] + _swap_pairs(dtv * s_ref[...])
            xv = x_ref[:, sl].astype(F32)
            r = lax.rsqrt(jnp.mean(xv * xv, axis=-1, keepdims=True) + EPS)
            xh = xv * r
            dxh = dtv * w_ref[...]
            dx_ref[:, sl] = (r * (dxh - xh * jnp.mean(dxh * xh, axis=-1, keepdims=True))).astype(BF16)
            dw += jnp.sum(dtv * xh, axis=0, keepdims=True)

        @pl.when(first)
        def _():
            dw_ref[...] = dw

        @pl.when(jnp.logical_not(first))
        def _():
            dw_ref[...] += dw

    blk = pl.BlockSpec((tm, bw), lambda i, j: (i, j))
    in_specs = [blk, pl.BlockSpec((tm, bw), lambda i, j: (rb0 + i, col_off // bw + j)),
                pl.BlockSpec((1, HD), lambda i, j: (0, 0))]
    args = [dt, px, nw]
    if rope:
        in_specs += [pl.BlockSpec((tm, HD), lambda i, j: (i % pb, 0))] * 2
        args += [cos, sin]
    return pl.pallas_call(
        body, name=name, grid=(rows // tm, heads // hb), in_specs=in_specs,
        out_specs=[blk, pl.BlockSpec((1, HD), lambda i, j: (0, 0))],
        out_shape=[_sds((rows, heads * HD), BF16), _sds((1, HD), F32)],
        compiler_params=_params(("arbitrary", "arbitrary")),
    )(*args)


def _ret_bwd(px, lg, do16, hist_f, hist_b, nb, nc):
    t_rows = nb * nc * CH

    def body(lg_ref, *refs):
        ins = (refs[0:5], refs[7:12])
        do_refs = (refs[5], refs[12])
        h_refs = (refs[6], refs[13])
        outs = (refs[14:17], refs[17:20])
        ds_outs = (refs[20], refs[21])
        dlg_ref = refs[22]
        dss = (refs[23], refs[24])
        c = pl.program_id(1)

        @pl.when(c == 0)
        def _():
            dss[0][...] = jnp.zeros_like(dss[0])
            dss[1][...] = jnp.zeros_like(dss[1])
            dlg_ref[...] = jnp.zeros_like(dlg_ref)

        for d in range(2):
            dq_ref, dk_ref, dv_ref = outs[d]
            for h in range(RH):
                lg_d = lg_ref[d, h]
                mask, relf, qd, qe, kd, ke = _decays(lg_d, d == 0)
                g_ch = jnp.exp(lg_d * CH)
                q, k, v16 = _head_qkv(ins[d], h)
                q16 = q.astype(BF16)
                k16 = k.astype(BF16)
                do16v = do_refs[d][:, h * DV:(h + 1) * DV]
                st16 = h_refs[d][h]
                dst = dss[d][h]
                dst16 = dst.astype(BF16)
                a = _dot(q16, k16, NT) * mask
                dp = _dot(do16v, v16, NT)
                da16 = (dp * mask).astype(BF16)
                dq_cross = _dot(do16v, st16, NT) * qd
                dq_ref[:, h * DK:(h + 1) * DK] = (_dot(da16, k16) + dq_cross).astype(BF16)
                dk_state = _dot(v16, dst16, NT) * kd
                dk_ref[:, h * DK:(h + 1) * DK] = ((_dot(da16, q16, TN) + dk_state) * (DK ** -0.5)).astype(BF16)
                dv = _dot(a.astype(BF16), do16v, TN) + _dot((k * kd).astype(BF16), dst16)
                dv_ref[:, h * DV:(h + 1) * DV] = dv.astype(BF16)
                dlg = (jnp.sum(relf * a * dp)
                       + jnp.sum(qe * jnp.sum(q * dq_cross, axis=-1, keepdims=True))
                       + jnp.sum(ke * jnp.sum(k * dk_state, axis=-1, keepdims=True))
                       + CH * g_ch * jnp.sum(dst * st16.astype(F32)))
                row = d * RH + h
                dlg_ref[row:row + 1, :] += jnp.broadcast_to(dlg, (1, 128))
                ds_new = g_ch * dst + _dot((q * qd).astype(BF16), do16v, TN)
                dss[d][h] = ds_new

                @pl.when(c == nc - 1)
                def _():
                    ds_outs[d][h] = ds_new

    def fw(b, c):
        return b * nc + nc - 1 - c

    def bw(b, c):
        return b * nc + c

    def rows(rowf, width):
        return pl.BlockSpec((CH, width), lambda b, c: (rowf(b, c), 0))

    def hist(rowf):
        return pl.BlockSpec((None, None, RH, DK, DV), lambda b, c: (b, rowf(0, c), 0, 0, 0))

    st = pl.BlockSpec((None, RH, DK, DV), lambda b, c: (b, 0, 0, 0))
    in_specs = [pl.BlockSpec(memory_space=pltpu.SMEM)]
    out_specs = []
    for rowf in (fw, bw):
        in_specs += _wide_specs(rowf) + [rows(rowf, RH * DV), hist(rowf)]
        out_specs += [rows(rowf, RH * DK), rows(rowf, RH * DK), rows(rowf, RH * DV)]
    out_specs += [st, st, pl.BlockSpec((None, 8, 128), lambda b, c: (b, 0, 0))]
    qk = _sds((t_rows, RH * DK), BF16)
    vv = _sds((t_rows, RH * DV), BF16)
    return pl.pallas_call(
        body, name="ret_bwd", grid=(nb, nc), in_specs=in_specs, out_specs=out_specs,
        out_shape=[qk, qk, vv, qk, qk, vv, _sds((nb, RH, DK, DV), F32), _sds((nb, RH, DK, DV), F32),
                   _sds((nb, 8, 128), F32)],
        scratch_shapes=[pltpu.VMEM((RH, DK, DV), F32), pltpu.VMEM((RH, DK, DV), F32)],
        compiler_params=_params(("parallel", "arbitrary")),
    )(lg, *([px] * 5), do16, hist_f, *([px] * 5), do16, hist_b)


def _ctx_state_bwd(px, lg, ds_f, ds_b, nb, t_rows, cx):
    rb = t_rows // cx

    def body(lg_ref, k_ref, v_ref, dsf_ref, dsb_ref, dk_ref, dv_ref, dlg_ref):
        h = pl.program_id(1)
        pos = lax.broadcasted_iota(jnp.int32, (cx, 1), 0).astype(F32)
        k = k_ref[...].astype(F32) * (DK ** -0.5)
        v16 = v_ref[...].astype(BF16)
        dk = jnp.zeros((cx, DK), F32)
        dv = jnp.zeros((cx, DV), F32)
        dlg_ref[...] = jnp.zeros_like(dlg_ref)
        for d, (ds_ref, e) in enumerate(((dsf_ref, cx - 1.0 - pos), (dsb_ref, pos))):
            w = jnp.exp(lg_ref[d, h] * e)
            ds16 = ds_ref[...].astype(BF16)
            t = _dot(v16, ds16, NT)
            dk += t * w
            dv += _dot((k * w).astype(BF16), ds16)
            dlg = jnp.sum(e * w * jnp.sum(k * t, axis=-1, keepdims=True))
            dlg_ref[d:d + 1, :] = jnp.broadcast_to(dlg, (1, 128))
        dk_ref[...] = (dk * (DK ** -0.5)).astype(BF16)
        dv_ref[...] = dv.astype(BF16)

    st = pl.BlockSpec((None, None, DK, DV), lambda b, h: (b, h, 0, 0))
    return pl.pallas_call(
        body, name="ctx_state_bwd", grid=(nb, RH),
        in_specs=[pl.BlockSpec(memory_space=pltpu.SMEM),
                  pl.BlockSpec((cx, DK), lambda b, h: (rb + b, RK // DK + h)),
                  pl.BlockSpec((cx, DV), lambda b, h: (rb + b, RV // DV + h)), st, st],
        out_specs=[pl.BlockSpec((cx, DK), lambda b, h: (b, h)), pl.BlockSpec((cx, DV), lambda b, h: (b, h)),
                   pl.BlockSpec((None, None, 8, 128), lambda b, h: (b, h, 0, 0))],
        out_shape=[_sds((nb * cx, RH * DK), BF16), _sds((nb * cx, RH * DV), BF16), _sds((nb, RH, 8, 128), F32)],
        compiler_params=_params(("parallel", "parallel")),
    )(lg, px, px, ds_f, ds_b)


def _assemble_lat(rows_all, dk_f, dk_b, dv_f, dv_b, dak16, dvx, dq_f, dq_b, drg16, daq16, dag16, dmr16, dma16, tm):
    t_rows = dk_f.shape[0]

    def body(dkf, dkb, dvf, dvb, dak, dav, dqf, dqb, drg, daq, dag, dmr, dma, o_ref):
        o_ref[:, RK:RK + RH * DK] = (dkf[...].astype(F32) + dkb[...].astype(F32)).astype(BF16)
        o_ref[:, RV:RV + RH * DV] = (dvf[...].astype(F32) + dvb[...].astype(F32)).astype(BF16)
        o_ref[:, AK:AK + HKV * HD] = dak[...]
        o_ref[:, AV:AV + HKV * HD] = dav[...].astype(BF16)
        o_ref[:, RQ:RQ + RH * DK] = (dqf[...].astype(F32) + dqb[...].astype(F32)).astype(BF16)
        o_ref[:, RG:RG + RH * DV] = drg[...]
        o_ref[:, AQ:AQ + D] = daq[...]
        o_ref[:, AG:AG + D] = dag[...]
        o_ref[:, MR:MR + D] = dmr[...]
        o_ref[:, MA:MA + D] = dma[...]

    args = (dk_f, dk_b, dv_f, dv_b, dak16, dvx, dq_f, dq_b, drg16, daq16, dag16, dmr16, dma16)
    return pl.pallas_call(
        body, name="assemble_lat", grid=(t_rows // tm,),
        in_specs=[pl.BlockSpec((tm, a.shape[1]), lambda i: (i, 0)) for a in args],
        out_specs=pl.BlockSpec((tm, IN_COLS), lambda i: (i, 0)), out_shape=_sds((rows_all, IN_COLS), BF16),
        compiler_params=_params(("parallel",)),
    )(*args)


def _assemble_ctx(dp_all, dck16, dcv16, dcak16, dvc, t_rows, tm):
    c_rows = dck16.shape[0]
    rb = t_rows // tm

    def body(_, dck, dcv, dcak, dcav, o_ref):
        o_ref[:, RK:RK + RH * DK] = dck[...]
        o_ref[:, RV:RV + RH * DV] = dcv[...]
        o_ref[:, AK:AK + HKV * HD] = dcak[...]
        o_ref[:, AV:AV + HKV * HD] = dcav[...].astype(BF16)
        o_ref[:, KV_COLS:] = jnp.zeros((tm, IN_COLS - KV_COLS), BF16)

    args = (dck16, dcv16, dcak16, dvc)
    return pl.pallas_call(
        body, name="assemble_ctx", grid=(c_rows // tm,),
        in_specs=[pl.BlockSpec(memory_space=pl.ANY)]
        + [pl.BlockSpec((tm, a.shape[1]), lambda i: (i, 0)) for a in args],
        out_specs=pl.BlockSpec((tm, IN_COLS), lambda i: (rb + i, 0)), out_shape=_sds(dp_all.shape, BF16),
        input_output_aliases={0: 0},
        compiler_params=_params(("parallel",)),
    )(dp_all, *args)


def _norm_bwd(dh, x2, mod3, norm_w, dxn, row_off, rows_per_group, group0, tm, name):
    with_dx = dxn is not None
    rows = x2.shape[0]
    rb0 = row_off // tm
    bpg = rows_per_group // tm
    ngroups = rows // rows_per_group

    def body(*refs):
        if with_dx:
            dh_ref, x_ref, sc_ref, nw_ref, dxn_ref, dx_ref, dsh_ref, dsc_ref, dnw_ref = refs
        else:
            dh_ref, x_ref, sc_ref, nw_ref, dsh_ref, dsc_ref, dnw_ref = refs
        i = pl.program_id(0)
        dhv = dh_ref[...]
        xv = x_ref[...]
        nw = nw_ref[...]
        r = lax.rsqrt(jnp.mean(xv * xv, axis=-1, keepdims=True) + EPS)
        xh = xv * r
        dm = dhv * (1.0 + sc_ref[...])
        dsh = jnp.sum(dhv, axis=0, keepdims=True)
        dsc = jnp.sum(dhv * (xh * nw), axis=0, keepdims=True)
        dnw = jnp.sum(dm * xh, axis=0, keepdims=True)
        if with_dx:
            dxh = dm * nw
            dx_ref[...] = dxn_ref[...] + r * (dxh - xh * jnp.mean(dxh * xh, axis=-1, keepdims=True))

        @pl.when(i % bpg == 0)
        def _():
            dsh_ref[...] = dsh
            dsc_ref[...] = dsc

        @pl.when(i % bpg != 0)
        def _():
            dsh_ref[...] += dsh
            dsc_ref[...] += dsc

        @pl.when(i == 0)
        def _():
            dnw_ref[...] = dnw

        @pl.when(i > 0)
        def _():
            dnw_ref[...] += dnw

    grp = pl.BlockSpec((None, 1, D), lambda i: (i // bpg, 0, 0))
    in_specs = [pl.BlockSpec((tm, D), lambda i: (rb0 + i, 0)), pl.BlockSpec((tm, D), lambda i: (i, 0)),
                pl.BlockSpec((None, 1, D), lambda i: (group0 + i // bpg, 0, 1)),
                pl.BlockSpec((1, D), lambda i: (0, 0))]
    args = [dh, x2, mod3, norm_w]
    out_specs = [grp, grp, pl.BlockSpec((1, D), lambda i: (0, 0))]
    out_shape = [_sds((ngroups, 1, D), F32), _sds((ngroups, 1, D), F32), _sds((1, D), F32)]
    if with_dx:
        in_specs.append(pl.BlockSpec((tm, D), lambda i: (i, 0)))
        args.append(dxn)
        out_specs.insert(0, pl.BlockSpec((tm, D), lambda i: (i, 0)))
        out_shape.insert(0, _sds((rows, D), F32))
    return pl.pallas_call(
        body, name=name, grid=(rows // tm,), in_specs=in_specs, out_specs=out_specs, out_shape=out_shape,
        compiler_params=_params(("arbitrary",)),
    )(*args)


def _small_final(dmod_all, dmodc_parts, c_rows, dm_loc_rows, nw_parts, misc_parts, c_ctx, r_pad, w_ada16):
    loc = dm_loc_rows.shape[1]

    def body(dm_ref, dmc_ref, c_ref, dml_ref, nwp_ref, mp_ref, cc_ref, r_ref, w_ref,
             gb_ref, gc_ref, gnw_ref, misc_ref, gwa_ref):
        dmc = jnp.sum(dmc_ref[...], axis=0, keepdims=True)
        gb_ref[...] = jnp.sum(dm_ref[...], axis=0, keepdims=True) + dmc
        dsc = _dot(jnp.broadcast_to(dmc, (8, 3 * D)).astype(BF16), w_ref[...], NT)[0:1, :]
        gc_ref[...] = dsc * _dsilu(cc_ref[...])
        gnw_ref[...] = jnp.sum(nwp_ref[...], axis=0, keepdims=True)
        misc = jnp.sum(mp_ref[...], axis=0, keepdims=True)
        y = jnp.exp2(r_ref[...])
        lane = lax.broadcasted_iota(jnp.int32, (1, D), 1)
        is_decay = jnp.logical_and(lane >= 2 * HD, lane < 2 * HD + 2 * RH)
        misc_ref[...] = misc * jnp.where(is_decay, -(y * np.float32(np.log(2.0))) / (1.0 - y), 1.0)
        gwa_ref[...] = _dot(_silu(c_ref[...]).astype(BF16), dml_ref[...].astype(BF16), TN)

    return pl.pallas_call(
        body, name="small_final",
        out_shape=[_sds((1, 3 * D), F32), _sds((1, D), F32), _sds((1, D), F32), _sds((1, D), F32), _sds((D, loc), F32)],
        compiler_params=pltpu.CompilerParams(vmem_limit_bytes=VMEM_LIMIT),
    )(dmod_all, dmodc_parts, c_rows, dm_loc_rows, nw_parts, misc_parts, c_ctx, r_pad, w_ada16)


def _adamw(w, g, m, v, name):
    rows, cols = w.shape
    tm = _pick(rows, 448, 8)
    bc1 = 1.0 - B1 ** STEP
    bc2 = 1.0 - B2 ** STEP

    def body(w_ref, g_ref, m_ref, v_ref, d_ref, nm_ref, nv_ref):
        g_ = g_ref[...]
        nm = B1 * m_ref[...] + (1.0 - B1) * g_
        nv = B2 * v_ref[...] + (1.0 - B2) * (g_ * g_)
        nm_ref[...] = nm
        nv_ref[...] = nv
        d_ref[...] = -LR * ((nm / bc1) / (jnp.sqrt(nv / bc2) + ADAM_EPS) + WD * w_ref[...])

    blk = pl.BlockSpec((tm, cols), lambda i: (i, 0))
    return pl.pallas_call(
        body, name=name, grid=(rows // tm,), in_specs=[blk] * 4, out_specs=[blk] * 3,
        out_shape=[_sds((rows, cols), F32)] * 3, compiler_params=_params(("parallel",)),
    )(w, g, m, v)


def _mesh_pos():
    return lax.axis_index("x"), lax.axis_index("y"), lax.axis_index("c")


def _all_gather(arrs, name):
    n = len(arrs)

    def body(*refs):
        ins, outs = refs[:n], refs[n:2 * n]
        send_sems, recv_sems, local_sems = refs[2 * n:]
        x, y, c = _mesh_pos()
        me, sib = (x, y, c), (x, y, 1 - c)
        chips = [(1 - x, y), (x, 1 - y), (1 - x, 1 - y)]

        def slot(p):
            return 4 * p[0] + 2 * p[1] + p[2]

        def copy(a, k, block, to, own):
            dst = outs[a].at[slot(block)]
            return pltpu.make_async_remote_copy(
                src_ref=ins[a] if own else dst, dst_ref=dst, send_sem=send_sems.at[a, k], recv_sem=recv_sems.at[a, k],
                device_id=to, device_id_type=MESH_T)

        mine = [pltpu.make_async_copy(ins[a], outs[a].at[slot(me)], local_sems.at[a]) for a in range(n)]
        for cp in mine:
            cp.start()
        first = []
        for a in range(n):
            first.append(copy(a, 0, me, sib, True))
            first += [copy(a, 1 + j, me, (*chip, c), True) for j, chip in enumerate(chips)]
        for cp in first:
            cp.start()
        passed = []
        for j, chip in enumerate(chips):
            for a in range(n):
                copy(a, 1 + j, (*chip, c), me, False).wait_recv()
                fwd = copy(a, 4 + j, (*chip, c), sib, False)
                fwd.start()
                passed.append(fwd)
        for a in range(n):
            copy(a, 0, sib, me, False).wait_recv()
            for j, chip in enumerate(chips):
                copy(a, 4 + j, (*chip, 1 - c), me, False).wait_recv()
        for cp in first + passed:
            cp.wait_send()
        for cp in mine:
            cp.wait()

    hbm = pl.BlockSpec(memory_space=pl.ANY)
    return pl.pallas_call(
        body, name=name, in_specs=[hbm] * n, out_specs=[hbm] * n,
        out_shape=[_sds((N_DEV,) + a.shape, a.dtype) for a in arrs],
        scratch_shapes=[pltpu.SemaphoreType.DMA((n, 7)), pltpu.SemaphoreType.DMA((n, 7)), pltpu.SemaphoreType.DMA((n,))],
    )(*arrs)


def _pair_exchange(arrs, name):
    n = len(arrs)

    def body(*refs):
        ins, outs = refs[:n], refs[n:2 * n]
        send_sems, recv_sems = refs[2 * n:]
        x, y, c = _mesh_pos()
        sib = (x, y, 1 - c)
        sends = []
        for a in range(n):
            for k in range(4):
                sends.append(pltpu.make_async_remote_copy(
                    src_ref=ins[a].at[2 * k + 1 - c], dst_ref=outs[a].at[k], send_sem=send_sems.at[a, k],
                    recv_sem=recv_sems.at[a, k], device_id=sib, device_id_type=MESH_T))
        for cp in sends:
            cp.start()
        for cp in sends:
            cp.wait_recv()
        for cp in sends:
            cp.wait_send()

    hbm = pl.BlockSpec(memory_space=pl.ANY)
    return pl.pallas_call(
        body, name=name, in_specs=[hbm] * n, out_specs=[hbm] * n,
        out_shape=[_sds((4,) + a.shape[1:], a.dtype) for a in arrs],
        scratch_shapes=[pltpu.SemaphoreType.DMA((n, 4)), pltpu.SemaphoreType.DMA((n, 4))],
    )(*arrs)


def _pair_add(part, got, core, name):
    _, rows, cols = part.shape
    tm = _pick(rows, 672, 16)
    p4 = part.reshape(4, 2, rows, cols)

    def body(core_ref, p_ref, g_ref, o_ref):
        o_ref[...] = (p_ref[...].astype(F32) + g_ref[...].astype(F32)).astype(BF16)

    blk = pl.BlockSpec((None, tm, cols), lambda k, i, cr: (k, i, 0))
    return pl.pallas_call(
        body, name=name,
        grid_spec=pltpu.PrefetchScalarGridSpec(
            num_scalar_prefetch=1, grid=(4, rows // tm),
            in_specs=[pl.BlockSpec((None, None, tm, cols), lambda k, i, cr: (k, cr[0], i, 0)), blk], out_specs=blk),
        out_shape=_sds((4, rows, cols), BF16), compiler_params=_params(("parallel", "parallel")),
    )(core, p4, got)


def _chip_sum(pair_sums, landed, chip, name):
    _, rows, cols = pair_sums.shape
    tm = _pick(rows, 672, 16)

    def body(chip_ref, s_ref, l_ref, o_ref):
        acc = s_ref[...].astype(F32)
        for j in range(3):
            acc = acc + l_ref[j].astype(F32)
        o_ref[...] = acc

    return pl.pallas_call(
        body, name=name,
        grid_spec=pltpu.PrefetchScalarGridSpec(
            num_scalar_prefetch=1, grid=(rows // tm,),
            in_specs=[pl.BlockSpec((None, tm, cols), lambda i, ch: (ch[0], i, 0)),
                      pl.BlockSpec((3, tm, cols), lambda i, ch: (0, i, 0))],
            out_specs=pl.BlockSpec((tm, cols), lambda i, ch: (i, 0))),
        out_shape=_sds((rows, cols), F32), compiler_params=_params(("parallel",)),
    )(chip, pair_sums, landed)


_HBM = pl.BlockSpec(memory_space=pltpu.HBM)
_SEM = pl.BlockSpec(memory_space=pltpu.SEMAPHORE)
_EFFECT = pltpu.SideEffectType.DATAFLOW_SIDE_EFFECTING


def _chip_routes(n):
    def plan(x, y, c):
        routes = []
        for a in range(n):
            for j in range(1, 4):
                px, py = x ^ (j >> 1), y ^ (j & 1)
                routes.append((a, 2 * px + py, (px, py, c), j - 1))
        return routes
    return plan, 3 * n


def _bcast_routes(n):
    def plan(x, y, c):
        routes = []
        for a in range(n):
            for k in range(1, N_DEV):
                peer = (x ^ ((k >> 2) & 1), y ^ ((k >> 1) & 1), c ^ (k & 1))
                routes.append((a, 0, peer, 4 * x + 2 * y + c))
        return routes
    return plan, 7 * n


def _route_copies(srcs, lands, send_sems, recv_sems, routes):
    return [pltpu.make_async_remote_copy(
        src_ref=srcs[a].at[sb], dst_ref=lands[a].at[lb], send_sem=send_sems.at[r], recv_sem=recv_sems.at[r],
        device_id=peer, device_id_type=MESH_T) for r, (a, sb, peer, lb) in enumerate(routes)]


def _exchange_start(srcs, lands, routes, name, after=()):
    plan, count = routes
    n = len(srcs)
    n_in = 2 * n + len(after)

    def body(*refs):
        send_sems, recv_sems = refs[n_in], refs[n_in + 1]
        token = refs[-1]
        for cp in _route_copies(refs[:n], refs[n:2 * n], send_sems, recv_sems, plan(*_mesh_pos())):
            cp.start()
        token[...] = jnp.zeros_like(token)

    args = [pltpu.with_memory_space_constraint(a, pltpu.HBM) for a in list(srcs) + list(lands)]
    out = pl.pallas_call(
        body, name=name,
        out_shape=(pltpu.SemaphoreType.DMA((count,)), pltpu.SemaphoreType.DMA((count,)),
                   *[pltpu.HBM(a.shape, a.dtype) for a in args], _sds((8, 128), F32)),
        in_specs=[_HBM] * (2 * n) + [pl.BlockSpec(memory_space=pl.ANY)] * len(after),
        out_specs=(_SEM, _SEM, *([_HBM] * (2 * n)), pl.BlockSpec(memory_space=pltpu.VMEM)),
        input_output_aliases={i: 2 + i for i in range(2 * n)},
        compiler_params=pltpu.CompilerParams(has_side_effects=_EFFECT),
    )(*args, *after)
    return (out[0], out[1], list(out[2:2 + 2 * n]), routes), out[-1]


def _exchange_wait(state, after, name):
    send_sems, recv_sems, bufs, (plan, count) = state
    n = len(bufs) // 2

    def body(*refs):
        send_s, recv_s = refs[2 * n], refs[2 * n + 1]
        for cp in _route_copies(refs[:n], refs[n:2 * n], send_s, recv_s, plan(*_mesh_pos())):
            cp.wait_send()
            cp.wait_recv()

    out = pl.pallas_call(
        body, name=name, out_shape=tuple(pltpu.HBM(a.shape, a.dtype) for a in bufs),
        in_specs=[_HBM] * (2 * n) + [_SEM, _SEM, pl.BlockSpec(memory_space=pl.ANY)], out_specs=tuple([_HBM] * (2 * n)),
        input_output_aliases={i: i for i in range(2 * n)},
        compiler_params=pltpu.CompilerParams(has_side_effects=_EFFECT),
    )(*bufs, send_sems, recv_sems, after)
    return list(out[:n]), list(out[n:])


def _group_routes(js):
    def plan(x, y, c):
        return [(0, 0, (x ^ (j >> 1), y ^ (j & 1), c), 2 * j + c) for j in js]
    return plan, len(js)


def _pair_fill(groups, js, name, after=()):
    def body(*refs):
        g_ref, send_sems, recv_sems = refs[-3:]
        x, y, c = _mesh_pos()
        sends = []
        for n, j in enumerate(js):
            mine = g_ref.at[2 * j + c]
            sends.append(pltpu.make_async_remote_copy(
                src_ref=mine, dst_ref=mine, send_sem=send_sems.at[n], recv_sem=recv_sems.at[n],
                device_id=(x, y, 1 - c), device_id_type=MESH_T))
        for cp in sends:
            cp.start()
        for n, j in enumerate(js):
            pltpu.make_async_remote_copy(
                src_ref=g_ref.at[2 * j + c], dst_ref=g_ref.at[2 * j + 1 - c], send_sem=send_sems.at[n],
                recv_sem=recv_sems.at[n], device_id=(x, y, 1 - c), device_id_type=MESH_T).wait_recv()
        for cp in sends:
            cp.wait_send()

    hbm = pl.BlockSpec(memory_space=pl.ANY)
    return pl.pallas_call(
        body, name=name, in_specs=[hbm] * (1 + len(after)), out_specs=hbm, out_shape=_sds(groups.shape, groups.dtype),
        input_output_aliases={0: 0},
        scratch_shapes=[pltpu.SemaphoreType.DMA((len(js),)), pltpu.SemaphoreType.DMA((len(js),))],
    )(groups, *after)


def _in_proj_group(h_all, groups, j0, ng, chip, px_prev, after, name):
    rows_all = h_all.shape[0]
    gcols = IN_COLS // 4
    tm = _pick(rows_all, 1536, 128)
    g4 = groups.reshape(4, gcols, D)

    n_lead = (1 if px_prev is not None else 0) + len(after)
    lead = ([px_prev] if px_prev is not None else []) + list(after)

    def body(chip_ref, *refs):
        h_ref, w_ref, o_ref = refs[n_lead:]
        o_ref[...] = _dot(h_ref[...], w_ref[...], NT).astype(BF16)

    return pl.pallas_call(
        body, name=name,
        grid_spec=pltpu.PrefetchScalarGridSpec(
            num_scalar_prefetch=1, grid=(ng, rows_all // tm),
            in_specs=[pl.BlockSpec(memory_space=pl.ANY)] * n_lead
            + [pl.BlockSpec((tm, D), lambda n, i, ch: (i, 0)),
               pl.BlockSpec((None, gcols, D), lambda n, i, ch: (j0 + n, 0, 0))],
            out_specs=pl.BlockSpec((tm, gcols), lambda n, i, ch: (i, ch[0] ^ (j0 + n)))),
        out_shape=_sds((rows_all, IN_COLS), BF16),
        input_output_aliases={1: 0} if px_prev is not None else {},
        compiler_params=_params(("parallel", "parallel")),
    )(chip, *lead, h_all, g4)


def _d_h_groups(dp_all, groups, chip, after):
    rows_all = dp_all.shape[0]
    gcols = IN_COLS // 4
    tm = _pick(rows_all, 1536, 128)
    g4 = groups.reshape(4, gcols, D)
    n_lead = len(after)

    def body(chip_ref, *refs):
        a_ref, w_ref, o_ref = refs[n_lead:]
        j = pl.program_id(1)
        part = _dot(a_ref[...], w_ref[...])

        @pl.when(j == 0)
        def _():
            o_ref[...] = part

        @pl.when(j > 0)
        def _():
            o_ref[...] += part

    return pl.pallas_call(
        body, name="d_h",
        grid_spec=pltpu.PrefetchScalarGridSpec(
            num_scalar_prefetch=1, grid=(rows_all // tm, 4),
            in_specs=[pl.BlockSpec(memory_space=pl.ANY)] * n_lead
            + [pl.BlockSpec((tm, gcols), lambda i, j, ch: (i, ch[0] ^ j)),
               pl.BlockSpec((None, gcols, D), lambda i, j, ch: (j, 0, 0))],
            out_specs=pl.BlockSpec((tm, D), lambda i, j, ch: (i, 0))),
        out_shape=_sds((rows_all, D), F32),
        compiler_params=_params(("parallel", "arbitrary")),
    )(chip, *after, dp_all, g4)


def _reduce_scatter_start(parts, core, name):
    got = _pair_exchange(parts, name + "_pair")
    sums = [_pair_add(p, g, core, "%s_add_%d" % (name, i)) for i, (p, g) in enumerate(zip(parts, got))]
    lands = [lax.empty((3,) + s_.shape[1:], BF16) for s_ in sums]
    return _exchange_start(sums, lands, _chip_routes(len(sums)), name + "_start")


def _reduce_scatter_finish(rs_state, after, chip, name):
    sums, landed = _exchange_wait(rs_state, after, name + "_wait")
    return [_chip_sum(s_, l_, chip, "%s_sum_%d" % (name, i)) for i, (s_, l_) in enumerate(zip(sums, landed))]


def _local_step(x, c, ctx, norm_w, ret_log2_decay, q_norm_w, k_norm_w, loss_target,
                mod, proj_in, proj_back, get_w_o, on_out_grads, on_in_grad, started=()):
    nb, seq, _ = x.shape
    cx = ctx.shape[1]
    t_rows, c_rows = nb * seq, nb * cx
    rows_all = t_rows + c_rows
    nc = seq // CH
    tm = _pick(seq, 256, 128)
    te = _pick(seq, 512, 128)
    assert cx % tm == 0 and t_rows % cx == 0 and seq % GRID_W == 0

    x2 = x.reshape(t_rows, D)
    ctx2 = ctx.reshape(c_rows, D)
    tgt = loss_target.reshape(t_rows, D)
    lg = _log_gamma(ret_log2_decay)
    cos, sin = _rope_tables(seq)

    mod3 = mod[:, None, :]
    h_all = _norm_fwd(x2, mod3, norm_w, rows_all, 0, seq, 0, None, te, "norm_fwd", after=started)
    h_all = _norm_fwd(ctx2, mod3, norm_w, rows_all, t_rows, c_rows, nb, h_all, tm, "norm_fwd_ctx")
    px = proj_in(h_all)
    s0f, s0b = _ctx_state(px, lg, nb, t_rows, cx)
    o_f, o_b, hist_f, hist_b = _ret_fwd(px, lg, s0f, s0b, nb, nc)
    yret16 = _ret_post(o_f, o_b, px, te)
    q16 = _qk_prep(px, q_norm_w, cos, sin, t_rows, 0, AQ, HQ, 4, seq, te, "q_prep")
    kx16 = _qk_prep(px, k_norm_w, cos, sin, t_rows, 0, AK, HKV, HKV, seq, te, "k_prep")
    kc16 = _qk_prep(px, k_norm_w, None, None, c_rows, t_rows, AK, HKV, HKV, seq, tm, "kc_prep")
    o_att, yatt16, lse = _att_fwd(q16, kx16, kc16, px, nb, seq, cx, te)
    w_o_ret16, w_o_att16, w_out16 = get_w_o(lse)
    a_ret, a_att, y16 = _merge(yret16, yatt16, px, w_o_ret16, w_o_att16, te)
    dxn, dout16, dgate, loss_b = _outproj(y16, w_out16, x2, tgt, mod3, nb, seq, te)

    gw_out = _matmul(y16, dout16, ta=True, tm=D, tn=D, tk=D, out_dtype=BF16, name="gw_out")
    da_ret16, da_att16, dmr16, dma16 = _bwd_merge(dout16, w_out16, px, a_ret, a_att, te)
    gw_o_ret = _matmul(yret16, da_ret16, ta=True, tm=D, tn=D, tk=D, out_dtype=BF16, name="gw_o_ret")
    gw_o_att = _matmul(yatt16, da_att16, ta=True, tm=D, tn=D, tk=D, out_dtype=BF16, name="gw_o_att")
    out_state, out_started = on_out_grads([gw_o_ret, gw_o_att, gw_out])
    do16, drg16 = _bwd_branch_ret(da_ret16, w_o_ret16, px, o_f, o_b, te, after=out_started)
    dao, dag16 = _bwd_branch_att(da_att16, w_o_att16, px, o_att, te)
    dq_rot, dkx, dvx, dkc, dvc = _att_bwd(q16, kx16, kc16, px, dao, o_att, lse, nb, seq, cx, te)
    daq16, gq = _qk_prep_bwd(dq_rot, px, q_norm_w, cos, sin, t_rows, 0, AQ, HQ, 4, seq, te, "q_prep_bwd")
    dak16, gk_lat = _qk_prep_bwd(dkx.reshape(t_rows, HKV * HD), px, k_norm_w, cos, sin, t_rows, 0, AK, HKV, HKV, seq, te,
                                 "k_prep_bwd")
    dcak16, gk_ctx = _qk_prep_bwd(dkc.reshape(c_rows, HKV * HD), px, k_norm_w, None, None, c_rows, t_rows, AK, HKV, HKV,
                                  seq, tm, "kc_prep_bwd")
    dq_f, dk_f, dv_f, dq_b, dk_b, dv_b, ds_f, ds_b, dlg_scan = _ret_bwd(px, lg, do16, hist_f, hist_b, nb, nc)
    dck16, dcv16, dlg_ctx = _ctx_state_bwd(px, lg, ds_f, ds_b, nb, t_rows, cx)
    dp_all = _assemble_lat(rows_all, dk_f, dk_b, dv_f, dv_b, dak16, dvx.reshape(t_rows, HKV * HD), dq_f, dq_b, drg16,
                           daq16, dag16, dmr16, dma16, tm)
    dp_all = _assemble_ctx(dp_all, dck16, dcv16, dcak16, dvc.reshape(c_rows, HKV * HD), t_rows, tm)
    gw_in_t = _matmul(dp_all, h_all, ta=True, tm=1536, tn=D, tk=2304, out_dtype=BF16, name="gw_in")
    in_state, in_started = on_in_grad(gw_in_t)
    dh = proj_back(dp_all, in_started)
    grad_x, dsh, dsc, gnw_lat = _norm_bwd(dh, x2, mod3, norm_w, dxn, 0, seq, 0, te, "norm_bwd")
    dsh_c, dsc_c, gnw_ctx = _norm_bwd(dh, ctx2, mod3, norm_w, None, t_rows, c_rows, nb, tm, "norm_bwd_ctx")

    dlg = (jnp.sum(dlg_scan[:, :, 0], axis=0) + jnp.sum(dlg_ctx[:, :, :2, 0], axis=0).T.reshape(2 * RH)).reshape(1, 2 * RH)
    misc = jnp.concatenate([gq, gk_lat + gk_ctx, dlg, jnp.sum(loss_b[:, 0, 0]).reshape(1, 1),
                            jnp.zeros((1, D - 2 * HD - 2 * RH - 1), F32)], axis=1)
    rows = []
    for b in range(nb):
        rows += [dsh[b], dsc[b], dgate[b]]
    rows += [dsh_c[0], dsc_c[0]] + [c[b:b + 1] for b in range(nb)] + [gnw_lat + gnw_ctx, misc]
    payload = jnp.concatenate(rows + [jnp.zeros((PAY_ROWS - len(rows), D), F32)], axis=0)
    return grad_x.reshape(nb, seq, D), out_state, in_state, payload


def _finish_small(gathered, nb, c_ctx, ret_log2_decay, w_ada16, dev):
    n_dev = gathered.shape[0]
    loc = 3 * D // n_dev
    dmod_all = gathered[:, :3 * nb].reshape(n_dev * nb, 3 * D)
    dmodc_parts = jnp.concatenate([gathered[:, 3 * nb:3 * nb + 2].reshape(n_dev, 2 * D), jnp.zeros((n_dev, D), F32)], axis=1)
    c_all = gathered[:, 3 * nb + 2:4 * nb + 2].reshape(n_dev * nb, D)
    nw_parts = gathered[:, 4 * nb + 2]
    misc_parts = gathered[:, 4 * nb + 3]
    n_rows = n_dev * nb + n_dev
    pad = (-n_rows) % 16
    c_rows = jnp.concatenate([c_all, jnp.broadcast_to(c_ctx.reshape(1, D), (n_dev, D)), jnp.zeros((pad, D), F32)], axis=0)
    dm_rows = jnp.concatenate([dmod_all, dmodc_parts, jnp.zeros((pad, 3 * D), F32)], axis=0)
    dm_loc_rows = lax.dynamic_slice_in_dim(dm_rows, dev * loc, loc, axis=1)
    r_pad = jnp.full((1, D), -1.0, F32).at[:, 2 * HD:2 * HD + 2 * RH].set(ret_log2_decay.reshape(1, 2 * RH))
    gb, gc, gnw, misc, gwa = _small_final(dmod_all, dmodc_parts, c_rows, dm_loc_rows, nw_parts, misc_parts,
                                          c_ctx.reshape(1, D), r_pad, w_ada16)
    return (gb, gc, gnw, misc[:, :HD], misc[:, HD:2 * HD], misc[:, 2 * HD:2 * HD + 2 * RH], gwa,
            misc[0, 2 * HD + 2 * RH])


def kernel(x, c, ctx, c_ctx, norm_w, w_ada, b_ada, w_in, ret_log2_decay, q_norm_w, k_norm_w, w_o_ret, w_o_att, w_out, loss_target, m_c_ctx, m_norm_w, m_w_ada, m_b_ada, m_w_in, m_ret_log2_decay, m_q_norm_w, m_k_norm_w, m_w_o_ret, m_w_o_att, m_w_out, v_c_ctx, v_norm_w, v_w_ada, v_b_ada, v_w_in, v_ret_log2_decay, v_q_norm_w, v_k_norm_w, v_w_o_ret, v_w_o_att, v_w_out):
    nb = x.shape[0]
    mx, my, mc = _mesh_pos()
    dev = 4 * mx + 2 * my + mc
    core = jnp.reshape(mc, (1,)).astype(jnp.int32)
    chip = jnp.reshape(2 * mx + my, (1,)).astype(jnp.int32)

    n_loc = 3 * D // N_DEV
    c8 = jnp.zeros((8, D), F32).at[:nb].set(c).at[nb].set(c_ctx)
    (c_all,) = _all_gather([c8], "gather_c")
    ada_shard = w_ada[0].astype(BF16)
    b_loc = lax.dynamic_slice(b_ada, (0, dev * n_loc), (1, n_loc))
    mod_cols = _mod_part(c_all.reshape(N_DEV * 8, D), ada_shard, b_loc)
    (mod_all,) = _all_gather([mod_cols], "gather_mod")
    mod = jnp.transpose(lax.dynamic_slice(mod_all, (0, dev * 8, 0), (N_DEV, 8, n_loc)), (1, 0, 2)).reshape(8, 3 * D)
    ada_land = lax.dynamic_update_slice(lax.empty((N_DEV,) + ada_shard.shape, BF16), ada_shard[None], (dev, 0, 0))

    w_in_t = jnp.transpose(w_in[0])
    in_shard = w_in_t.astype(BF16)
    groups = lax.dynamic_update_slice(lax.empty((N_DEV,) + in_shard.shape, BF16), in_shard[None], (mc, 0, 0))
    groups = _pair_fill(groups, (0,), "gather_in_pair", after=(mod_all,))
    (near_send, near_recv, near_bufs, near_routes), gin_token = _exchange_start(
        [in_shard[None]], [groups], _group_routes((1, 2)), "gather_in_start")
    w_in_groups, wo_states, ada_states = [], [], []
    wo_shards = [w_[0].astype(BF16) for w_ in (w_o_ret, w_o_att, w_out)]
    wo_lands = [lax.dynamic_update_slice(lax.empty((N_DEV,) + s_.shape, BF16), s_[None], (dev, 0, 0)) for s_ in wo_shards]

    def _state(send, recv, src, groups, routes):
        return send, recv, [src, groups], routes

    def proj_in(h_all):
        src, groups = near_bufs
        px = _in_proj_group(h_all, groups, 0, 1, chip, None, (gin_token,), "in_proj_0")
        (src,), (groups,) = _exchange_wait(_state(near_send, near_recv, src, groups, near_routes), px,
                                           "gather_in_wait_near")
        groups = _pair_fill(groups, (1, 2), "gather_in_fill_near")
        (far_send, far_recv, (src, groups), far_routes), far_token = _exchange_start(
            [src], [groups], _group_routes((3,)), "gather_in_start_far")
        wo_state, wo_token = _exchange_start([s_[None] for s_ in wo_shards], wo_lands, _bcast_routes(3),
                                             "gather_wo_start", after=(far_token,))
        wo_states.append(wo_state)
        ada_state, ada_token = _exchange_start([ada_shard[None]], [ada_land], _bcast_routes(1), "gather_ada_start",
                                               after=(wo_token,))
        ada_states.append(ada_state)
        px = _in_proj_group(h_all, groups, 1, 2, chip, px, (ada_token,), "in_proj_near")
        (src,), (groups,) = _exchange_wait(_state(far_send, far_recv, src, groups, far_routes), px,
                                           "gather_in_wait_far")
        groups = _pair_fill(groups, (3,), "gather_in_fill_far")
        px = _in_proj_group(h_all, groups, 3, 1, chip, px, (), "in_proj_far")
        w_in_groups.append(groups)
        return px

    def proj_back(dp_all, after):
        return _d_h_groups(dp_all, w_in_groups[0], chip, after)

    def get_w_o(after):
        _, (l_ret, l_att, l_out) = _exchange_wait(wo_states[0], after, "gather_wo_wait")
        return l_ret.reshape(RH * DV, D), l_att.reshape(D, D), l_out.reshape(D, D)

    def on_out_grads(grads):
        parts = [g_.reshape(N_DEV, g_.shape[0] // N_DEV, D) for g_ in grads]
        state, token = _reduce_scatter_start(parts, core, "rs_out")
        return state, (token,)

    def on_in_grad(grad):
        state, token = _reduce_scatter_start([grad.reshape(N_DEV, IN_COLS // N_DEV, D)], core, "rs_in")
        return state, (token,)

    grad_x, out_state, in_state, payload = _local_step(
        x, c, ctx, norm_w, ret_log2_decay, q_norm_w, k_norm_w, loss_target,
        mod, proj_in, proj_back, get_w_o, on_out_grads, on_in_grad, started=(gin_token,))

    (gathered,) = _all_gather([payload], "gather_small")
    _, (l_ada,) = _exchange_wait(ada_states[0], gathered, "gather_ada_wait")
    w_ada16 = jnp.transpose(l_ada, (1, 0, 2)).reshape(D, 3 * D)
    gb, gc, gnw, gq, gk, gr, gwa, loss = _finish_small(gathered, nb, c_ctx, ret_log2_decay, w_ada16, dev)

    g_w_o_ret, g_w_o_att, g_w_out = _reduce_scatter_finish(out_state, gathered, chip, "rs_out")
    (g_w_in_t,) = _reduce_scatter_finish(in_state, gathered, chip, "rs_in")

    grads = [gc.reshape(c_ctx.shape), gnw, gwa[None], gb, g_w_in_t, gr.reshape(ret_log2_decay.shape), gq, gk,
             g_w_o_ret[None], g_w_o_att[None], g_w_out[None]]
    weights = [c_ctx, norm_w, w_ada, b_ada, w_in_t, ret_log2_decay, q_norm_w, k_norm_w, w_o_ret, w_o_att, w_out]
    ms = [m_c_ctx, m_norm_w, m_w_ada, m_b_ada, jnp.transpose(m_w_in[0]), m_ret_log2_decay, m_q_norm_w, m_k_norm_w,
          m_w_o_ret, m_w_o_att, m_w_out]
    vs = [v_c_ctx, v_norm_w, v_w_ada, v_b_ada, jnp.transpose(v_w_in[0]), v_ret_log2_decay, v_q_norm_w, v_k_norm_w,
          v_w_o_ret, v_w_o_att, v_w_out]
    deltas, new_ms, new_vs = [], [], []
    for i, (w, g, m, v) in enumerate(zip(weights, grads, ms, vs)):
        shape2 = (-1, w.shape[-1])
        res = _adamw(w.reshape(shape2), g.reshape(shape2), m.reshape(shape2), v.reshape(shape2), "adamw_%d" % i)
        for lst, r in zip((deltas, new_ms, new_vs), res):
            lst.append(jnp.transpose(r)[None] if i == 4 else r.reshape(w.shape))
    grads[4] = jnp.transpose(g_w_in_t)[None]
    return (loss, grad_x, *grads, *deltas, *new_ms, *new_vs)
```

```python
import numpy as np
import jax
import jax.numpy as jnp
from jax import lax
from jax.experimental import pallas as pl
from jax.experimental.pallas import tpu as pltpu

F32 = jnp.float32
BF16 = jnp.bfloat16

D = 1024
RH, DK, DV, CH = 4, 256, 512, 256
HQ, HKV, HD = 8, 2, 128
GRID_W = 64
ROPE_THETA = 10000.0
EPS = 1e-6
RK, RV, AK, AV, RQ, RG, AQ, AG, MR, MA = 0, 1024, 3072, 3328, 3584, 4608, 6656, 7680, 8704, 9728
IN_COLS = 10752
KV_COLS = 3584
N_DEV = 8
LR, B1, B2, ADAM_EPS, WD, STEP = 0.001, 0.9, 0.999, 1e-08, 0.01, 10
PAY_ROWS = 16
VMEM_LIMIT = 56 * 1024 * 1024
MESH_T = pl.DeviceIdType.MESH

NT = (((1,), (1,)), ((), ()))
TN = (((0,), (0,)), ((), ()))
SM_C = (HD ** -0.5) * float(np.log2(np.e))


def _params(sem):
    return pltpu.CompilerParams(dimension_semantics=sem, vmem_limit_bytes=VMEM_LIMIT)


def _pick(n, target, mult=8):
    best = None
    for t in range(mult, min(n, target) + 1, mult):
        if n % t == 0:
            best = t
    return best or n


def _dot(a, b, dn=None):
    if dn is None:
        return jnp.dot(a, b, preferred_element_type=F32)
    return lax.dot_general(a, b, dn, preferred_element_type=F32)


def _sig(v):
    return jax.nn.sigmoid(v)


def _silu(v):
    return v * _sig(v)


def _dsilu(v):
    s = _sig(v)
    return s * (1.0 + v * (1.0 - s))


def _sds(shape, dtype):
    return jax.ShapeDtypeStruct(shape, dtype)


def _matmul(a, b, *, ta=False, tb=False, tm, tn, tk, out_dtype, name, after=()):
    m = a.shape[1] if ta else a.shape[0]
    kdim = a.shape[0] if ta else a.shape[1]
    n = b.shape[0] if tb else b.shape[1]
    tm, tn, tk = _pick(m, tm, 128), _pick(n, tn, 128), _pick(kdim, tk, 128)
    nk = kdim // tk
    dn = (((0 if ta else 1,), (1 if tb else 0,)), ((), ()))

    def body(a_ref, b_ref, *rest):
        o_ref, acc_ref = rest[-2:]
        k = pl.program_id(2)
        part = _dot(a_ref[...].astype(BF16), b_ref[...].astype(BF16), dn)
        if nk == 1:
            o_ref[...] = part.astype(o_ref.dtype)
        else:
            @pl.when(k == 0)
            def _():
                acc_ref[...] = part

            @pl.when(k > 0)
            def _():
                acc_ref[...] += part

            @pl.when(k == nk - 1)
            def _():
                o_ref[...] = acc_ref[...].astype(o_ref.dtype)

    a_spec = pl.BlockSpec((tk, tm), lambda i, j, k: (k, i)) if ta else pl.BlockSpec((tm, tk), lambda i, j, k: (i, k))
    b_spec = pl.BlockSpec((tn, tk), lambda i, j, k: (j, k)) if tb else pl.BlockSpec((tk, tn), lambda i, j, k: (k, j))
    return pl.pallas_call(
        body, name=name, grid=(m // tm, n // tn, nk),
        in_specs=[a_spec, b_spec] + [pl.BlockSpec(memory_space=pl.ANY)] * len(after),
        out_specs=pl.BlockSpec((tm, tn), lambda i, j, k: (i, j)), out_shape=_sds((m, n), out_dtype),
        scratch_shapes=[pltpu.VMEM((tm, tn) if nk > 1 else (8, 128), F32)],
        compiler_params=_params(("parallel", "parallel", "arbitrary")),
    )(a, b, *after)


def _log_gamma(r):
    rp = jnp.full((8, 128), -1.0, F32).at[:2, :RH].set(r.reshape(2, RH))

    def body(r_ref, o_ref):
        o_ref[...] = jnp.log1p(-jnp.exp2(r_ref[...]))

    out = pl.pallas_call(body, name="log_gamma", out_shape=_sds((8, 128), F32))(rp)
    return out[:2, :RH]


def _mod_part(c_rows, w_ada_loc16, b_loc):
    def body(c_ref, w_ref, b_ref, o_ref):
        o_ref[...] = _dot(_silu(c_ref[...]).astype(BF16), w_ref[...]) + b_ref[...]

    return pl.pallas_call(
        body, name="mod_part", out_shape=_sds((c_rows.shape[0], w_ada_loc16.shape[1]), F32),
    )(c_rows, w_ada_loc16, b_loc)


def _norm_fwd(x2, mod3, norm_w, rows_all, row_off, rows_per_group, group0, h_prev, tm, name, after=()):
    rows = x2.shape[0]
    rb0 = row_off // tm
    bpg = rows_per_group // tm

    def body(*refs):
        x_ref, sh_ref, sc_ref, nw_ref, o_ref = refs[-5:]
        xv = x_ref[...]
        r = lax.rsqrt(jnp.mean(xv * xv, axis=-1, keepdims=True) + EPS)
        o_ref[...] = ((xv * r) * nw_ref[...] * (1.0 + sc_ref[...]) + sh_ref[...]).astype(BF16)

    in_specs = [pl.BlockSpec((tm, D), lambda i: (i, 0)),
                pl.BlockSpec((None, 1, D), lambda i: (group0 + i // bpg, 0, 0)),
                pl.BlockSpec((None, 1, D), lambda i: (group0 + i // bpg, 0, 1)),
                pl.BlockSpec((1, D), lambda i: (0, 0))]
    in_specs = [pl.BlockSpec(memory_space=pl.ANY)] * len(after) + in_specs
    args = list(after) + [x2, mod3, mod3, norm_w]
    alias = {}
    if h_prev is not None:
        in_specs.insert(0, pl.BlockSpec(memory_space=pl.ANY))
        args.insert(0, h_prev)
        alias = {0: 0}
    return pl.pallas_call(
        body, name=name, grid=(rows // tm,), in_specs=in_specs,
        out_specs=pl.BlockSpec((tm, D), lambda i: (rb0 + i, 0)), out_shape=_sds((rows_all, D), BF16),
        input_output_aliases=alias, compiler_params=_params(("parallel",)),
    )(*args)


def _decays(lg, fwd):
    ii = lax.broadcasted_iota(jnp.int32, (CH, CH), 0)
    jj = lax.broadcasted_iota(jnp.int32, (CH, CH), 1)
    ri = lax.broadcasted_iota(jnp.int32, (CH, 1), 0).astype(F32)
    rel = (ii - jj) if fwd else (jj - ii)
    relf = jnp.maximum(rel, 0).astype(F32)
    mask = jnp.where(rel >= 0, jnp.exp(lg * relf), 0.0)
    qe = (ri + 1.0) if fwd else (CH - ri)
    ke = (CH - 1.0 - ri) if fwd else ri
    return mask, relf, jnp.exp(lg * qe), qe, jnp.exp(lg * ke), ke


def _wide_specs(rowf):
    return [pl.BlockSpec((CH, 2 * DK), lambda b, c: (rowf(b, c), RQ // (2 * DK))),
            pl.BlockSpec((CH, 2 * DK), lambda b, c: (rowf(b, c), RQ // (2 * DK) + 1)),
            pl.BlockSpec((CH, RH * DK), lambda b, c: (rowf(b, c), RK // (RH * DK))),
            pl.BlockSpec((CH, 2 * DV), lambda b, c: (rowf(b, c), RV // (2 * DV))),
            pl.BlockSpec((CH, 2 * DV), lambda b, c: (rowf(b, c), RV // (2 * DV) + 1))]


def _head_qkv(refs, h):
    q0, q1, k, v0, v1 = refs
    lo = h % 2
    q = (q0, q1)[h // 2][:, lo * DK:(lo + 1) * DK].astype(F32)
    kk = k[:, h * DK:(h + 1) * DK].astype(F32) * (DK ** -0.5)
    v16 = (v0, v1)[h // 2][:, lo * DV:(lo + 1) * DV].astype(BF16)
    return q, kk, v16


def _ctx_state(px, lg, nb, t_rows, cx):
    rb = t_rows // cx

    def body(lg_ref, k_ref, v_ref, sf_ref, sb_ref):
        h = pl.program_id(1)
        pos = lax.broadcasted_iota(jnp.int32, (cx, 1), 0).astype(F32)
        k = k_ref[...].astype(F32) * (DK ** -0.5)
        v16 = v_ref[...].astype(BF16)
        wf = jnp.exp(lg_ref[0, h] * (cx - 1.0 - pos))
        wb = jnp.exp(lg_ref[1, h] * pos)
        sf_ref[...] = _dot((k * wf).astype(BF16), v16, TN)
        sb_ref[...] = _dot((k * wb).astype(BF16), v16, TN)

    st = pl.BlockSpec((None, None, DK, DV), lambda b, h: (b, h, 0, 0))
    return pl.pallas_call(
        body, name="ctx_state", grid=(nb, RH),
        in_specs=[pl.BlockSpec(memory_space=pltpu.SMEM),
                  pl.BlockSpec((cx, DK), lambda b, h: (rb + b, RK // DK + h)),
                  pl.BlockSpec((cx, DV), lambda b, h: (rb + b, RV // DV + h))],
        out_specs=[st, st], out_shape=[_sds((nb, RH, DK, DV), F32)] * 2,
        compiler_params=_params(("parallel", "parallel")),
    )(lg, px, px)


def _ret_fwd(px, lg, s0f, s0b, nb, nc):
    t_rows = nb * nc * CH

    def body(lg_ref, *refs):
        ins = (refs[0:5], refs[5:10])
        s0f_ref, s0b_ref, of_ref, ob_ref, hf_ref, hb_ref, sf, sb = refs[10:]
        c = pl.program_id(1)

        @pl.when(c == 0)
        def _():
            sf[...] = s0f_ref[...]
            sb[...] = s0b_ref[...]

        for d, (o_ref, h_ref, s) in enumerate(((of_ref, hf_ref, sf), (ob_ref, hb_ref, sb))):
            for h in range(RH):
                lg_d = lg_ref[d, h]
                mask, _, qd, _, kd, _ = _decays(lg_d, d == 0)
                q, k, v16 = _head_qkv(ins[d], h)
                a = _dot(q.astype(BF16), k.astype(BF16), NT)
                st = s[h]
                st16 = st.astype(BF16)
                h_ref[h] = st16
                o = _dot((a * mask).astype(BF16), v16) + _dot((q * qd).astype(BF16), st16)
                o_ref[:, h * DV:(h + 1) * DV] = o.astype(BF16)
                s[h] = st * jnp.exp(lg_d * CH) + _dot((k * kd).astype(BF16), v16, TN)

    def fw(b, c):
        return b * nc + c

    def bw(b, c):
        return b * nc + nc - 1 - c

    st = pl.BlockSpec((None, RH, DK, DV), lambda b, c: (b, 0, 0, 0))
    in_specs = [pl.BlockSpec(memory_space=pltpu.SMEM)] + _wide_specs(fw) + _wide_specs(bw) + [st, st]
    out_specs = [pl.BlockSpec((CH, RH * DV), lambda b, c: (fw(b, c), 0)),
                 pl.BlockSpec((CH, RH * DV), lambda b, c: (bw(b, c), 0)),
                 pl.BlockSpec((None, None, RH, DK, DV), lambda b, c: (b, c, 0, 0, 0)),
                 pl.BlockSpec((None, None, RH, DK, DV), lambda b, c: (b, nc - 1 - c, 0, 0, 0))]
    return pl.pallas_call(
        body, name="ret_fwd", grid=(nb, nc), in_specs=in_specs, out_specs=out_specs,
        out_shape=[_sds((t_rows, RH * DV), BF16)] * 2 + [_sds((nb, nc, RH, DK, DV), BF16)] * 2,
        scratch_shapes=[pltpu.VMEM((RH, DK, DV), F32), pltpu.VMEM((RH, DK, DV), F32)],
        compiler_params=_params(("parallel", "arbitrary")),
    )(lg, *([px] * 10), s0f, s0b)


def _ret_post(o_f, o_b, px, tm):
    t_rows = o_f.shape[0]

    def body(of_ref, ob_ref, g0, g1, g2, g3, y_ref):
        for h, g_ref in enumerate((g0, g1, g2, g3)):
            sl = slice(h * DV, (h + 1) * DV)
            o = of_ref[:, sl].astype(F32) + ob_ref[:, sl].astype(F32)
            r = lax.rsqrt(jnp.mean(o * o, axis=-1, keepdims=True) + EPS)
            y_ref[:, sl] = ((o * r) * _silu(g_ref[...].astype(F32))).astype(BF16)

    def gate(h):
        return pl.BlockSpec((tm, DV), lambda i: (i, RG // DV + h))

    wide = pl.BlockSpec((tm, RH * DV), lambda i: (i, 0))
    return pl.pallas_call(
        body, name="ret_post", grid=(t_rows // tm,),
        in_specs=[wide, wide] + [gate(h) for h in range(RH)],
        out_specs=wide, out_shape=_sds((t_rows, RH * DV), BF16),
        compiler_params=_params(("parallel",)),
    )(o_f, o_b, *([px] * RH))


def _rope_tables(seq):
    rows = seq // GRID_W
    row = np.repeat(np.arange(rows, dtype=np.float32), GRID_W)
    col = np.tile(np.arange(GRID_W, dtype=np.float32), rows)
    half = HD // 2
    freqs = (ROPE_THETA ** (-np.arange(0, half, 2, dtype=np.float32) / half)).astype(np.float32)
    ang = np.concatenate([row[:, None] * freqs, col[:, None] * freqs], axis=-1).astype(np.float32)
    cos = np.repeat(np.cos(ang), 2, axis=-1).astype(np.float32)
    sin = np.repeat(np.sin(ang), 2, axis=-1).astype(np.float32)
    sign = np.tile(np.array([-1.0, 1.0], np.float32), HD // 2)
    return jnp.asarray(cos), jnp.asarray(sin * sign)


def _swap_pairs(v):
    lane = lax.broadcasted_iota(jnp.int32, v.shape, 1)
    return jnp.where((lane & 1) == 0, pltpu.roll(v, HD - 1, 1), pltpu.roll(v, 1, 1))


def _qk_prep(px, nw, cos, sin, rows, row_off, col_off, heads, hb, seq, tm, name):
    rope = cos is not None
    rb0 = row_off // tm
    pb = seq // tm if rope else 1
    bw = hb * HD

    def body(*refs):
        if rope:
            x_ref, w_ref, c_ref, s_ref, o_ref = refs
        else:
            x_ref, w_ref, o_ref = refs
        for h in range(hb):
            sl = slice(h * HD, (h + 1) * HD)
            xv = x_ref[:, sl].astype(F32)
            r = lax.rsqrt(jnp.mean(xv * xv, axis=-1, keepdims=True) + EPS)
            t = (xv * r) * w_ref[...]
            if rope:
                t = t * c_ref[...] + _swap_pairs(t) * s_ref[...]
            o_ref[:, sl] = t.astype(BF16)

    in_specs = [pl.BlockSpec((tm, bw), lambda i, j: (rb0 + i, col_off // bw + j)),
                pl.BlockSpec((1, HD), lambda i, j: (0, 0))]
    args = [px, nw]
    if rope:
        in_specs += [pl.BlockSpec((tm, HD), lambda i, j: (i % pb, 0))] * 2
        args += [cos, sin]
    return pl.pallas_call(
        body, name=name, grid=(rows // tm, heads // hb), in_specs=in_specs,
        out_specs=pl.BlockSpec((tm, bw), lambda i, j: (i, j)), out_shape=_sds((rows, heads * HD), BF16),
        compiler_params=_params(("parallel", "parallel")),
    )(*args)


def _att_fwd(q16, kx16, kc16, px, nb, seq, cx, tq):
    t_rows = nb * seq
    nq = seq // tq
    rep = HQ // HKV
    gw = rep * HD

    def body(q_ref, kx_ref, kc_ref, vx_ref, vc_ref, g_ref, o_ref, y_ref, l_ref):
        kx = kx_ref[...]
        kc = kc_ref[...]
        vx = vx_ref[...].astype(BF16)
        vc = vc_ref[...].astype(BF16)
        l_ref[...] = jnp.zeros_like(l_ref)
        for r in range(rep):
            sl = slice(r * HD, (r + 1) * HD)
            q = q_ref[:, sl]
            s1 = _dot(q, kx, NT)
            s2 = _dot(q, kc, NT)
            m = jnp.maximum(jnp.max(s1, axis=-1, keepdims=True), jnp.max(s2, axis=-1, keepdims=True))
            e1 = jnp.exp2((s1 - m) * SM_C)
            e2 = jnp.exp2((s2 - m) * SM_C)
            tot = jnp.sum(e1, axis=-1, keepdims=True) + jnp.sum(e2, axis=-1, keepdims=True)
            o = (_dot(e1.astype(BF16), vx) + _dot(e2.astype(BF16), vc)) * (1.0 / tot)
            o_ref[:, sl] = o
            y_ref[:, sl] = (o * _silu(g_ref[:, sl].astype(F32))).astype(BF16)
            l_ref[:, r:r + 1] = m * SM_C + jnp.log(tot) * float(np.log2(np.e))

    qblk = pl.BlockSpec((tq, gw), lambda b, g, i: (b * nq + i, g))
    return pl.pallas_call(
        body, name="att_fwd", grid=(nb, HKV, nq),
        in_specs=[qblk,
                  pl.BlockSpec((seq, HD), lambda b, g, i: (b, g)),
                  pl.BlockSpec((cx, HD), lambda b, g, i: (b, g)),
                  pl.BlockSpec((seq, HD), lambda b, g, i: (b, AV // HD + g)),
                  pl.BlockSpec((cx, HD), lambda b, g, i: (t_rows // cx + b, AV // HD + g)),
                  pl.BlockSpec((tq, gw), lambda b, g, i: (b * nq + i, AG // gw + g))],
        out_specs=[qblk, qblk, pl.BlockSpec((tq, 128), lambda b, g, i: (b * nq + i, g))],
        out_shape=[_sds((t_rows, D), F32), _sds((t_rows, D), BF16), _sds((t_rows, HKV * 128), F32)],
        compiler_params=_params(("parallel", "parallel", "parallel")),
    )(q16, kx16, kc16, px, px, px)


def _gate_specs(tm, col0):
    hw = D // 2
    return [pl.BlockSpec((tm, hw), lambda i: (i, col0 // hw)), pl.BlockSpec((tm, hw), lambda i: (i, col0 // hw + 1))]


def _merge(yret16, yatt16, px, w_o_ret16, w_o_att16, tm):
    t_rows = yret16.shape[0]
    hw = D // 2

    def body(yr_ref, wr_ref, ya_ref, wa_ref, mr0, mr1, ma0, ma1, ar_ref, aa_ref, y_ref):
        ar = _dot(yr_ref[...], wr_ref[...])
        aa = _dot(ya_ref[...], wa_ref[...])
        ar_ref[...] = ar.astype(BF16)
        aa_ref[...] = aa.astype(BF16)
        for j, (mr_ref, ma_ref) in enumerate(((mr0, ma0), (mr1, ma1))):
            sl = slice(j * hw, (j + 1) * hw)
            y_ref[:, sl] = (_sig(mr_ref[...].astype(F32)) * ar[:, sl]
                            + _sig(ma_ref[...].astype(F32)) * aa[:, sl]).astype(BF16)

    row = pl.BlockSpec((tm, D), lambda i: (i, 0))
    return pl.pallas_call(
        body, name="merge", grid=(t_rows // tm,),
        in_specs=[pl.BlockSpec((tm, RH * DV), lambda i: (i, 0)), pl.BlockSpec((RH * DV, D), lambda i: (0, 0)),
                  row, pl.BlockSpec((D, D), lambda i: (0, 0))] + _gate_specs(tm, MR) + _gate_specs(tm, MA),
        out_specs=[row, row, row], out_shape=[_sds((t_rows, D), BF16)] * 3,
        compiler_params=_params(("parallel",)),
    )(yret16, w_o_ret16, yatt16, w_o_att16, px, px, px, px)


def _outproj(y16, w_out16, x2, tgt, mod3, nb, seq, tm):
    t_rows = nb * seq
    bpb = seq // tm

    def body(y_ref, w_ref, x_ref, t_ref, g_ref, dxn_ref, dout_ref, dg_ref, loss_ref):
        i = pl.program_id(1)
        out = _dot(y_ref[...], w_ref[...])
        gate = g_ref[...]
        diff = x_ref[...] + gate * out - t_ref[...]
        dxn = diff * (1.0 / D)
        dxn_ref[...] = dxn
        dout_ref[...] = (gate * dxn).astype(BF16)
        dg = jnp.sum(dxn * out, axis=0, keepdims=True)
        ls = jnp.broadcast_to(jnp.sum(diff * diff) * (0.5 / D), (1, 128))

        @pl.when(i == 0)
        def _():
            dg_ref[...] = dg
            loss_ref[...] = ls

        @pl.when(i > 0)
        def _():
            dg_ref[...] += dg
            loss_ref[...] += ls

    row = pl.BlockSpec((tm, D), lambda b, i: (b * bpb + i, 0))
    return pl.pallas_call(
        body, name="outproj", grid=(nb, bpb),
        in_specs=[row, pl.BlockSpec((D, D), lambda b, i: (0, 0)), row, row,
                  pl.BlockSpec((None, 1, D), lambda b, i: (b, 0, 2))],
        out_specs=[row, row, pl.BlockSpec((None, 1, D), lambda b, i: (b, 0, 0)),
                   pl.BlockSpec((None, 1, 128), lambda b, i: (b, 0, 0))],
        out_shape=[_sds((t_rows, D), F32), _sds((t_rows, D), BF16), _sds((nb, 1, D), F32), _sds((nb, 1, 128), F32)],
        compiler_params=_params(("parallel", "arbitrary")),
    )(y16, w_out16, x2, tgt, mod3)


def _bwd_merge(dout16, w_out16, px, a_ret, a_att, tm):
    t_rows = dout16.shape[0]
    hw = D // 2

    def body(do_ref, w_ref, mr0, mr1, ma0, ma1, ar_ref, aa_ref, dar_ref, daa_ref, dmr_ref, dma_ref):
        dy_all = _dot(do_ref[...], w_ref[...], NT)
        for j, (mr_ref, ma_ref) in enumerate(((mr0, ma0), (mr1, ma1))):
            sl = slice(j * hw, (j + 1) * hw)
            dy = dy_all[:, sl]
            sr = _sig(mr_ref[...].astype(F32))
            sa = _sig(ma_ref[...].astype(F32))
            dar_ref[:, sl] = (dy * sr).astype(BF16)
            daa_ref[:, sl] = (dy * sa).astype(BF16)
            dmr_ref[:, sl] = (dy * ar_ref[:, sl].astype(F32) * sr * (1.0 - sr)).astype(BF16)
            dma_ref[:, sl] = (dy * aa_ref[:, sl].astype(F32) * sa * (1.0 - sa)).astype(BF16)

    row = pl.BlockSpec((tm, D), lambda i: (i, 0))
    return pl.pallas_call(
        body, name="bwd_merge", grid=(t_rows // tm,),
        in_specs=[row, pl.BlockSpec((D, D), lambda i: (0, 0))] + _gate_specs(tm, MR) + _gate_specs(tm, MA) + [row, row],
        out_specs=[row] * 4, out_shape=[_sds((t_rows, D), BF16)] * 4,
        compiler_params=_params(("parallel",)),
    )(dout16, w_out16, px, px, px, px, a_ret, a_att)


def _bwd_branch_ret(da_ret16, w_o_ret16, px, o_f, o_b, tm, after=()):
    t_rows = da_ret16.shape[0]

    def body(da_ref, w_ref, g0, g1, g2, g3, of_ref, ob_ref, *rest):
        do_ref, dg_ref = rest[-2:]
        da = da_ref[...]
        for h, g_ref in enumerate((g0, g1, g2, g3)):
            sl = slice(h * DV, (h + 1) * DV)
            dy = _dot(da, w_ref[sl, :], NT)
            g = g_ref[...].astype(F32)
            o = of_ref[:, sl].astype(F32) + ob_ref[:, sl].astype(F32)
            r = lax.rsqrt(jnp.mean(o * o, axis=-1, keepdims=True) + EPS)
            on = o * r
            sg = _sig(g)
            don = dy * (g * sg)
            dg_ref[:, sl] = (dy * on * (sg * (1.0 + g * (1.0 - sg)))).astype(BF16)
            do_ref[:, sl] = (r * (don - on * jnp.mean(on * don, axis=-1, keepdims=True))).astype(BF16)

    def gate(h):
        return pl.BlockSpec((tm, DV), lambda i: (i, RG // DV + h))

    wide = pl.BlockSpec((tm, RH * DV), lambda i: (i, 0))
    return pl.pallas_call(
        body, name="bwd_branch_ret", grid=(t_rows // tm,),
        in_specs=[pl.BlockSpec((tm, D), lambda i: (i, 0)), pl.BlockSpec((RH * DV, D), lambda i: (0, 0))]
        + [gate(h) for h in range(RH)] + [wide, wide] + [pl.BlockSpec(memory_space=pl.ANY)] * len(after),
        out_specs=[wide, wide], out_shape=[_sds((t_rows, RH * DV), BF16)] * 2,
        compiler_params=_params(("parallel",)),
    )(da_ret16, w_o_ret16, *([px] * RH), o_f, o_b, *after)


def _bwd_branch_att(da_att16, w_o_att16, px, o_att, tm):
    t_rows = da_att16.shape[0]
    hw = D // 2

    def body(da_ref, w_ref, g0, g1, o_ref, dao_ref, dg_ref):
        dy_all = _dot(da_ref[...], w_ref[...], NT)
        for j, g_ref in enumerate((g0, g1)):
            sl = slice(j * hw, (j + 1) * hw)
            dy = dy_all[:, sl]
            g = g_ref[...].astype(F32)
            sg = _sig(g)
            dao_ref[:, sl] = dy * (g * sg)
            dg_ref[:, sl] = (dy * o_ref[:, sl] * (sg * (1.0 + g * (1.0 - sg)))).astype(BF16)

    row = pl.BlockSpec((tm, D), lambda i: (i, 0))
    return pl.pallas_call(
        body, name="bwd_branch_att", grid=(t_rows // tm,),
        in_specs=[row, pl.BlockSpec((D, D), lambda i: (0, 0))] + _gate_specs(tm, AG) + [row],
        out_specs=[row, row], out_shape=[_sds((t_rows, D), F32), _sds((t_rows, D), BF16)],
        compiler_params=_params(("parallel",)),
    )(da_att16, w_o_att16, px, px, o_att)


def _att_bwd(q16, kx16, kc16, px, dao, o_att, lse, nb, seq, cx, tq):
    t_rows = nb * seq
    nq = seq // tq
    rep = HQ // HKV
    gw = rep * HD
    scale = HD ** -0.5

    def body(q_ref, kx_ref, kc_ref, vx_ref, vc_ref, dao_ref, o_ref, l_ref, dq_ref, dkx_ref, dvx_ref, dkc_ref, dvc_ref):
        i = pl.program_id(2)
        kx = kx_ref[...]
        kc = kc_ref[...]
        vx = vx_ref[...].astype(BF16)
        vc = vc_ref[...].astype(BF16)
        dkx = jnp.zeros((seq, HD), F32)
        dvx = jnp.zeros((seq, HD), F32)
        dkc = jnp.zeros((cx, HD), F32)
        dvc = jnp.zeros((cx, HD), F32)
        for r in range(rep):
            sl = slice(r * HD, (r + 1) * HD)
            q = q_ref[:, sl]
            lr = l_ref[:, r:r + 1]
            p1 = jnp.exp2(_dot(q, kx, NT) * SM_C - lr)
            p2 = jnp.exp2(_dot(q, kc, NT) * SM_C - lr)
            da = dao_ref[:, sl]
            da16 = da.astype(BF16)
            delta = jnp.sum(da * o_ref[:, sl], axis=-1, keepdims=True)
            ds1 = (p1 * (_dot(da16, vx, NT) - delta)).astype(BF16)
            ds2 = (p2 * (_dot(da16, vc, NT) - delta)).astype(BF16)
            dq_ref[:, sl] = (_dot(ds1, kx) + _dot(ds2, kc)) * scale
            dkx += _dot(ds1, q, TN)
            dkc += _dot(ds2, q, TN)
            dvx += _dot(p1.astype(BF16), da16, TN)
            dvc += _dot(p2.astype(BF16), da16, TN)
        dkx = dkx * scale
        dkc = dkc * scale

        @pl.when(i == 0)
        def _():
            dkx_ref[...] = dkx
            dvx_ref[...] = dvx
            dkc_ref[...] = dkc
            dvc_ref[...] = dvc

        @pl.when(i > 0)
        def _():
            dkx_ref[...] += dkx
            dvx_ref[...] += dvx
            dkc_ref[...] += dkc
            dvc_ref[...] += dvc

    qblk = pl.BlockSpec((tq, gw), lambda b, g, i: (b * nq + i, g))
    kxb = pl.BlockSpec((None, seq, HD), lambda b, g, i: (b, 0, g))
    kcb = pl.BlockSpec((None, cx, HD), lambda b, g, i: (b, 0, g))
    return pl.pallas_call(
        body, name="att_bwd", grid=(nb, HKV, nq),
        in_specs=[qblk,
                  pl.BlockSpec((seq, HD), lambda b, g, i: (b, g)),
                  pl.BlockSpec((cx, HD), lambda b, g, i: (b, g)),
                  pl.BlockSpec((seq, HD), lambda b, g, i: (b, AV // HD + g)),
                  pl.BlockSpec((cx, HD), lambda b, g, i: (t_rows // cx + b, AV // HD + g)),
                  qblk, qblk, pl.BlockSpec((tq, 128), lambda b, g, i: (b * nq + i, g))],
        out_specs=[qblk, kxb, kxb, kcb, kcb],
        out_shape=[_sds((t_rows, D), F32), _sds((nb, seq, HKV * HD), F32), _sds((nb, seq, HKV * HD), F32),
                   _sds((nb, cx, HKV * HD), F32), _sds((nb, cx, HKV * HD), F32)],
        compiler_params=_params(("parallel", "parallel", "arbitrary")),
    )(q16, kx16, kc16, px, px, dao, o_att, lse)


def _qk_prep_bwd(dt, px, nw, cos, sin, rows, row_off, col_off, heads, hb, seq, tm, name):
    rope = cos is not None
    rb0 = row_off // tm
    pb = seq // tm if rope else 1
    bw = hb * HD

    def body(*refs):
        if rope:
            d_ref, x_ref, w_ref, c_ref, s_ref, dx_ref, dw_ref = refs
        else:
            d_ref, x_ref, w_ref, dx_ref, dw_ref = refs
        first = jnp.logical_and(pl.program_id(0) == 0, pl.program_id(1) == 0)
        dw = jnp.zeros((1, HD), F32)
        for h in range(hb):
            sl = slice(h * HD, (h + 1) * HD)
            dtv = d_ref[:, sl]
            if rope:
                dtv = dtv * c_ref[...] + _swap_pairs(dtv * s_ref[...])
            xv = x_ref[:, sl].astype(F32)
            r = lax.rsqrt(jnp.mean(xv * xv, axis=-1, keepdims=True) + EPS)
            xh = xv * r
            dxh = dtv * w_ref[...]
            dx_ref[:, sl] = (r * (dxh - xh * jnp.mean(dxh * xh, axis=-1, keepdims=True))).astype(BF16)
            dw += jnp.sum(dtv * xh, axis=0, keepdims=True)

        @pl.when(first)
        def _():
            dw_ref[...] = dw

        @pl.when(jnp.logical_not(first))
        def _():
            dw_ref[...] += dw

    blk = pl.BlockSpec((tm, bw), lambda i, j: (i, j))
    in_specs = [blk, pl.BlockSpec((tm, bw), lambda i, j: (rb0 + i, col_off // bw + j)),
                pl.BlockSpec((1, HD), lambda i, j: (0, 0))]
    args = [dt, px, nw]
    if rope:
        in_specs += [pl.BlockSpec((tm, HD), lambda i, j: (i % pb, 0))] * 2
        args += [cos, sin]
    return pl.pallas_call(
        body, name=name, grid=(rows // tm, heads // hb), in_specs=in_specs,
        out_specs=[blk, pl.BlockSpec((1, HD), lambda i, j: (0, 0))],
        out_shape=[_sds((rows, heads * HD), BF16), _sds((1, HD), F32)],
        compiler_params=_params(("arbitrary", "arbitrary")),
    )(*args)


def _ret_bwd(px, lg, do16, hist_f, hist_b, nb, nc):
    t_rows = nb * nc * CH

    def body(lg_ref, *refs):
        ins = (refs[0:5], refs[7:12])
        do_refs = (refs[5], refs[12])
        h_refs = (refs[6], refs[13])
        outs = (refs[14:17], refs[17:20])
        ds_outs = (refs[20], refs[21])
        dlg_ref = refs[22]
        dss = (refs[23], refs[24])
        c = pl.program_id(1)

        @pl.when(c == 0)
        def _():
            dss[0][...] = jnp.zeros_like(dss[0])
            dss[1][...] = jnp.zeros_like(dss[1])
            dlg_ref[...] = jnp.zeros_like(dlg_ref)

        for d in range(2):
            dq_ref, dk_ref, dv_ref = outs[d]
            for h in range(RH):
                lg_d = lg_ref[d, h]
                mask, relf, qd, qe, kd, ke = _decays(lg_d, d == 0)
                g_ch = jnp.exp(lg_d * CH)
                q, k, v16 = _head_qkv(ins[d], h)
                q16 = q.astype(BF16)
                k16 = k.astype(BF16)
                do16v = do_refs[d][:, h * DV:(h + 1) * DV]
                st16 = h_refs[d][h]
                dst = dss[d][h]
                dst16 = dst.astype(BF16)
                a = _dot(q16, k16, NT) * mask
                dp = _dot(do16v, v16, NT)
                da16 = (dp * mask).astype(BF16)
                dq_cross = _dot(do16v, st16, NT) * qd
                dq_ref[:, h * DK:(h + 1) * DK] = (_dot(da16, k16) + dq_cross).astype(BF16)
                dk_state = _dot(v16, dst16, NT) * kd
                dk_ref[:, h * DK:(h + 1) * DK] = ((_dot(da16, q16, TN) + dk_state) * (DK ** -0.5)).astype(BF16)
                dv = _dot(a.astype(BF16), do16v, TN) + _dot((k * kd).astype(BF16), dst16)
                dv_ref[:, h * DV:(h + 1) * DV] = dv.astype(BF16)
                dlg = (jnp.sum(relf * a * dp)
                       + jnp.sum(qe * jnp.sum(q * dq_cross, axis=-1, keepdims=True))
                       + jnp.sum(ke * jnp.sum(k * dk_state, axis=-1, keepdims=True))
                       + CH * g_ch * jnp.sum(dst * st16.astype(F32)))
                row = d * RH + h
                dlg_ref[row:row + 1, :] += jnp.broadcast_to(dlg, (1, 128))
                ds_new = g_ch * dst + _dot((q * qd).astype(BF16), do16v, TN)
                dss[d][h] = ds_new

                @pl.when(c == nc - 1)
                def _():
                    ds_outs[d][h] = ds_new

    def fw(b, c):
        return b * nc + nc - 1 - c

    def bw(b, c):
        return b * nc + c

    def rows(rowf, width):
        return pl.BlockSpec((CH, width), lambda b, c: (rowf(b, c), 0))

    def hist(rowf):
        return pl.BlockSpec((None, None, RH, DK, DV), lambda b, c: (b, rowf(0, c), 0, 0, 0))

    st = pl.BlockSpec((None, RH, DK, DV), lambda b, c: (b, 0, 0, 0))
    in_specs = [pl.BlockSpec(memory_space=pltpu.SMEM)]
    out_specs = []
    for rowf in (fw, bw):
        in_specs += _wide_specs(rowf) + [rows(rowf, RH * DV), hist(rowf)]
        out_specs += [rows(rowf, RH * DK), rows(rowf, RH * DK), rows(rowf, RH * DV)]
    out_specs += [st, st, pl.BlockSpec((None, 8, 128), lambda b, c: (b, 0, 0))]
    qk = _sds((t_rows, RH * DK), BF16)
    vv = _sds((t_rows, RH * DV), BF16)
    return pl.pallas_call(
        body, name="ret_bwd", grid=(nb, nc), in_specs=in_specs, out_specs=out_specs,
        out_shape=[qk, qk, vv, qk, qk, vv, _sds((nb, RH, DK, DV), F32), _sds((nb, RH, DK, DV), F32),
                   _sds((nb, 8, 128), F32)],
        scratch_shapes=[pltpu.VMEM((RH, DK, DV), F32), pltpu.VMEM((RH, DK, DV), F32)],
        compiler_params=_params(("parallel", "arbitrary")),
    )(lg, *([px] * 5), do16, hist_f, *([px] * 5), do16, hist_b)


def _ctx_state_bwd(px, lg, ds_f, ds_b, nb, t_rows, cx):
    rb = t_rows // cx

    def body(lg_ref, k_ref, v_ref, dsf_ref, dsb_ref, dk_ref, dv_ref, dlg_ref):
        h = pl.program_id(1)
        pos = lax.broadcasted_iota(jnp.int32, (cx, 1), 0).astype(F32)
        k = k_ref[...].astype(F32) * (DK ** -0.5)
        v16 = v_ref[...].astype(BF16)
        dk = jnp.zeros((cx, DK), F32)
        dv = jnp.zeros((cx, DV), F32)
        dlg_ref[...] = jnp.zeros_like(dlg_ref)
        for d, (ds_ref, e) in enumerate(((dsf_ref, cx - 1.0 - pos), (dsb_ref, pos))):
            w = jnp.exp(lg_ref[d, h] * e)
            ds16 = ds_ref[...].astype(BF16)
            t = _dot(v16, ds16, NT)
            dk += t * w
            dv += _dot((k * w).astype(BF16), ds16)
            dlg = jnp.sum(e * w * jnp.sum(k * t, axis=-1, keepdims=True))
            dlg_ref[d:d + 1, :] = jnp.broadcast_to(dlg, (1, 128))
        dk_ref[...] = (dk * (DK ** -0.5)).astype(BF16)
        dv_ref[...] = dv.astype(BF16)

    st = pl.BlockSpec((None, None, DK, DV), lambda b, h: (b, h, 0, 0))
    return pl.pallas_call(
        body, name="ctx_state_bwd", grid=(nb, RH),
        in_specs=[pl.BlockSpec(memory_space=pltpu.SMEM),
                  pl.BlockSpec((cx, DK), lambda b, h: (rb + b, RK // DK + h)),
                  pl.BlockSpec((cx, DV), lambda b, h: (rb + b, RV // DV + h)), st, st],
        out_specs=[pl.BlockSpec((cx, DK), lambda b, h: (b, h)), pl.BlockSpec((cx, DV), lambda b, h: (b, h)),
                   pl.BlockSpec((None, None, 8, 128), lambda b, h: (b, h, 0, 0))],
        out_shape=[_sds((nb * cx, RH * DK), BF16), _sds((nb * cx, RH * DV), BF16), _sds((nb, RH, 8, 128), F32)],
        compiler_params=_params(("parallel", "parallel")),
    )(lg, px, px, ds_f, ds_b)


def _assemble_lat(rows_all, dk_f, dk_b, dv_f, dv_b, dak16, dvx, dq_f, dq_b, drg16, daq16, dag16, dmr16, dma16, tm):
    t_rows = dk_f.shape[0]

    def body(dkf, dkb, dvf, dvb, dak, dav, dqf, dqb, drg, daq, dag, dmr, dma, o_ref):
        o_ref[:, RK:RK + RH * DK] = (dkf[...].astype(F32) + dkb[...].astype(F32)).astype(BF16)
        o_ref[:, RV:RV + RH * DV] = (dvf[...].astype(F32) + dvb[...].astype(F32)).astype(BF16)
        o_ref[:, AK:AK + HKV * HD] = dak[...]
        o_ref[:, AV:AV + HKV * HD] = dav[...].astype(BF16)
        o_ref[:, RQ:RQ + RH * DK] = (dqf[...].astype(F32) + dqb[...].astype(F32)).astype(BF16)
        o_ref[:, RG:RG + RH * DV] = drg[...]
        o_ref[:, AQ:AQ + D] = daq[...]
        o_ref[:, AG:AG + D] = dag[...]
        o_ref[:, MR:MR + D] = dmr[...]
        o_ref[:, MA:MA + D] = dma[...]

    args = (dk_f, dk_b, dv_f, dv_b, dak16, dvx, dq_f, dq_b, drg16, daq16, dag16, dmr16, dma16)
    return pl.pallas_call(
        body, name="assemble_lat", grid=(t_rows // tm,),
        in_specs=[pl.BlockSpec((tm, a.shape[1]), lambda i: (i, 0)) for a in args],
        out_specs=pl.BlockSpec((tm, IN_COLS), lambda i: (i, 0)), out_shape=_sds((rows_all, IN_COLS), BF16),
        compiler_params=_params(("parallel",)),
    )(*args)


def _assemble_ctx(dp_all, dck16, dcv16, dcak16, dvc, t_rows, tm):
    c_rows = dck16.shape[0]
    rb = t_rows // tm

    def body(_, dck, dcv, dcak, dcav, o_ref):
        o_ref[:, RK:RK + RH * DK] = dck[...]
        o_ref[:, RV:RV + RH * DV] = dcv[...]
        o_ref[:, AK:AK + HKV * HD] = dcak[...]
        o_ref[:, AV:AV + HKV * HD] = dcav[...].astype(BF16)
        o_ref[:, KV_COLS:] = jnp.zeros((tm, IN_COLS - KV_COLS), BF16)

    args = (dck16, dcv16, dcak16, dvc)
    return pl.pallas_call(
        body, name="assemble_ctx", grid=(c_rows // tm,),
        in_specs=[pl.BlockSpec(memory_space=pl.ANY)]
        + [pl.BlockSpec((tm, a.shape[1]), lambda i: (i, 0)) for a in args],
        out_specs=pl.BlockSpec((tm, IN_COLS), lambda i: (rb + i, 0)), out_shape=_sds(dp_all.shape, BF16),
        input_output_aliases={0: 0},
        compiler_params=_params(("parallel",)),
    )(dp_all, *args)


def _norm_bwd(dh, x2, mod3, norm_w, dxn, row_off, rows_per_group, group0, tm, name):
    with_dx = dxn is not None
    rows = x2.shape[0]
    rb0 = row_off // tm
    bpg = rows_per_group // tm
    ngroups = rows // rows_per_group

    def body(*refs):
        if with_dx:
            dh_ref, x_ref, sc_ref, nw_ref, dxn_ref, dx_ref, dsh_ref, dsc_ref, dnw_ref = refs
        else:
            dh_ref, x_ref, sc_ref, nw_ref, dsh_ref, dsc_ref, dnw_ref = refs
        i = pl.program_id(0)
        dhv = dh_ref[...]
        xv = x_ref[...]
        nw = nw_ref[...]
        r = lax.rsqrt(jnp.mean(xv * xv, axis=-1, keepdims=True) + EPS)
        xh = xv * r
        dm = dhv * (1.0 + sc_ref[...])
        dsh = jnp.sum(dhv, axis=0, keepdims=True)
        dsc = jnp.sum(dhv * (xh * nw), axis=0, keepdims=True)
        dnw = jnp.sum(dm * xh, axis=0, keepdims=True)
        if with_dx:
            dxh = dm * nw
            dx_ref[...] = dxn_ref[...] + r * (dxh - xh * jnp.mean(dxh * xh, axis=-1, keepdims=True))

        @pl.when(i % bpg == 0)
        def _():
            dsh_ref[...] = dsh
            dsc_ref[...] = dsc

        @pl.when(i % bpg != 0)
        def _():
            dsh_ref[...] += dsh
            dsc_ref[...] += dsc

        @pl.when(i == 0)
        def _():
            dnw_ref[...] = dnw

        @pl.when(i > 0)
        def _():
            dnw_ref[...] += dnw

    grp = pl.BlockSpec((None, 1, D), lambda i: (i // bpg, 0, 0))
    in_specs = [pl.BlockSpec((tm, D), lambda i: (rb0 + i, 0)), pl.BlockSpec((tm, D), lambda i: (i, 0)),
                pl.BlockSpec((None, 1, D), lambda i: (group0 + i // bpg, 0, 1)),
                pl.BlockSpec((1, D), lambda i: (0, 0))]
    args = [dh, x2, mod3, norm_w]
    out_specs = [grp, grp, pl.BlockSpec((1, D), lambda i: (0, 0))]
    out_shape = [_sds((ngroups, 1, D), F32), _sds((ngroups, 1, D), F32), _sds((1, D), F32)]
    if with_dx:
        in_specs.append(pl.BlockSpec((tm, D), lambda i: (i, 0)))
        args.append(dxn)
        out_specs.insert(0, pl.BlockSpec((tm, D), lambda i: (i, 0)))
        out_shape.insert(0, _sds((rows, D), F32))
    return pl.pallas_call(
        body, name=name, grid=(rows // tm,), in_specs=in_specs, out_specs=out_specs, out_shape=out_shape,
        compiler_params=_params(("arbitrary",)),
    )(*args)


def _small_final(dmod_all, dmodc_parts, c_rows, dm_loc_rows, nw_parts, misc_parts, c_ctx, r_pad, w_ada16):
    loc = dm_loc_rows.shape[1]

    def body(dm_ref, dmc_ref, c_ref, dml_ref, nwp_ref, mp_ref, cc_ref, r_ref, w_ref,
             gb_ref, gc_ref, gnw_ref, misc_ref, gwa_ref):
        dmc = jnp.sum(dmc_ref[...], axis=0, keepdims=True)
        gb_ref[...] = jnp.sum(dm_ref[...], axis=0, keepdims=True) + dmc
        dsc = _dot(jnp.broadcast_to(dmc, (8, 3 * D)).astype(BF16), w_ref[...], NT)[0:1, :]
        gc_ref[...] = dsc * _dsilu(cc_ref[...])
        gnw_ref[...] = jnp.sum(nwp_ref[...], axis=0, keepdims=True)
        misc = jnp.sum(mp_ref[...], axis=0, keepdims=True)
        y = jnp.exp2(r_ref[...])
        lane = lax.broadcasted_iota(jnp.int32, (1, D), 1)
        is_decay = jnp.logical_and(lane >= 2 * HD, lane < 2 * HD + 2 * RH)
        misc_ref[...] = misc * jnp.where(is_decay, -(y * np.float32(np.log(2.0))) / (1.0 - y), 1.0)
        gwa_ref[...] = _dot(_silu(c_ref[...]).astype(BF16), dml_ref[...].astype(BF16), TN)

    return pl.pallas_call(
        body, name="small_final",
        out_shape=[_sds((1, 3 * D), F32), _sds((1, D), F32), _sds((1, D), F32), _sds((1, D), F32), _sds((D, loc), F32)],
        compiler_params=pltpu.CompilerParams(vmem_limit_bytes=VMEM_LIMIT),
    )(dmod_all, dmodc_parts, c_rows, dm_loc_rows, nw_parts, misc_parts, c_ctx, r_pad, w_ada16)


def _adamw_math(w, g, m, v):
    nm = B1 * m + (1.0 - B1) * g
    nv = B2 * v + (1.0 - B2) * (g * g)
    return -LR * ((nm / (1.0 - B1 ** STEP)) / (jnp.sqrt(nv / (1.0 - B2 ** STEP)) + ADAM_EPS) + WD * w), nm, nv


def _adamw(w, g, m, v, name):
    rows, cols = w.shape
    tm = _pick(rows, 448, 8)

    def body(w_ref, g_ref, m_ref, v_ref, d_ref, nm_ref, nv_ref):
        d_ref[...], nm_ref[...], nv_ref[...] = _adamw_math(w_ref[...], g_ref[...], m_ref[...], v_ref[...])

    blk = pl.BlockSpec((tm, cols), lambda i: (i, 0))
    return pl.pallas_call(
        body, name=name, grid=(rows // tm,), in_specs=[blk] * 4, out_specs=[blk] * 3,
        out_shape=[_sds((rows, cols), F32)] * 3, compiler_params=_params(("parallel",)),
    )(w, g, m, v)


def _mesh_pos():
    return lax.axis_index("x"), lax.axis_index("y"), lax.axis_index("c")


def _all_gather(arrs, name):
    n = len(arrs)

    def body(*refs):
        ins, outs = refs[:n], refs[n:2 * n]
        send_sems, recv_sems, local_sems = refs[2 * n:]
        x, y, c = _mesh_pos()
        me, sib = (x, y, c), (x, y, 1 - c)
        chips = [(1 - x, y), (x, 1 - y), (1 - x, 1 - y)]

        def slot(p):
            return 4 * p[0] + 2 * p[1] + p[2]

        def copy(a, k, block, to, own):
            dst = outs[a].at[slot(block)]
            return pltpu.make_async_remote_copy(
                src_ref=ins[a] if own else dst, dst_ref=dst, send_sem=send_sems.at[a, k], recv_sem=recv_sems.at[a, k],
                device_id=to, device_id_type=MESH_T)

        mine = [pltpu.make_async_copy(ins[a], outs[a].at[slot(me)], local_sems.at[a]) for a in range(n)]
        for cp in mine:
            cp.start()
        first = []
        for a in range(n):
            first.append(copy(a, 0, me, sib, True))
            first += [copy(a, 1 + j, me, (*chip, c), True) for j, chip in enumerate(chips)]
        for cp in first:
            cp.start()
        passed = []
        for j, chip in enumerate(chips):
            for a in range(n):
                copy(a, 1 + j, (*chip, c), me, False).wait_recv()
                fwd = copy(a, 4 + j, (*chip, c), sib, False)
                fwd.start()
                passed.append(fwd)
        for a in range(n):
            copy(a, 0, sib, me, False).wait_recv()
            for j, chip in enumerate(chips):
                copy(a, 4 + j, (*chip, 1 - c), me, False).wait_recv()
        for cp in first + passed:
            cp.wait_send()
        for cp in mine:
            cp.wait()

    hbm = pl.BlockSpec(memory_space=pl.ANY)
    return pl.pallas_call(
        body, name=name, in_specs=[hbm] * n, out_specs=[hbm] * n,
        out_shape=[_sds((N_DEV,) + a.shape, a.dtype) for a in arrs],
        scratch_shapes=[pltpu.SemaphoreType.DMA((n, 7)), pltpu.SemaphoreType.DMA((n, 7)), pltpu.SemaphoreType.DMA((n,))],
    )(*arrs)


def _pair_exchange(arrs, name):
    n = len(arrs)

    def body(*refs):
        ins, outs = refs[:n], refs[n:2 * n]
        send_sems, recv_sems = refs[2 * n:]
        x, y, c = _mesh_pos()
        sib = (x, y, 1 - c)
        sends = []
        for a in range(n):
            for k in range(4):
                sends.append(pltpu.make_async_remote_copy(
                    src_ref=ins[a].at[2 * k + 1 - c], dst_ref=outs[a].at[k], send_sem=send_sems.at[a, k],
                    recv_sem=recv_sems.at[a, k], device_id=sib, device_id_type=MESH_T))
        for cp in sends:
            cp.start()
        for cp in sends:
            cp.wait_recv()
        for cp in sends:
            cp.wait_send()

    hbm = pl.BlockSpec(memory_space=pl.ANY)
    return pl.pallas_call(
        body, name=name, in_specs=[hbm] * n, out_specs=[hbm] * n,
        out_shape=[_sds((4,) + a.shape[1:], a.dtype) for a in arrs],
        scratch_shapes=[pltpu.SemaphoreType.DMA((n, 4)), pltpu.SemaphoreType.DMA((n, 4))],
    )(*arrs)


def _pair_add(part, got, core, name):
    _, rows, cols = part.shape
    tm = _pick(rows, 672, 16)
    p4 = part.reshape(4, 2, rows, cols)

    def body(core_ref, p_ref, g_ref, o_ref):
        o_ref[...] = (p_ref[...].astype(F32) + g_ref[...].astype(F32)).astype(BF16)

    blk = pl.BlockSpec((None, tm, cols), lambda k, i, cr: (k, i, 0))
    return pl.pallas_call(
        body, name=name,
        grid_spec=pltpu.PrefetchScalarGridSpec(
            num_scalar_prefetch=1, grid=(4, rows // tm),
            in_specs=[pl.BlockSpec((None, None, tm, cols), lambda k, i, cr: (k, cr[0], i, 0)), blk], out_specs=blk),
        out_shape=_sds((4, rows, cols), BF16), compiler_params=_params(("parallel", "parallel")),
    )(core, p4, got)


def _chip_sum_adamw(pair_sums, landed, chip, w, m, v, name):
    _, rows, cols = pair_sums.shape
    tm = _pick(rows, 448, 16)

    def body(chip_ref, s_ref, l_ref, w_ref, m_ref, v_ref, g_ref, d_ref, nm_ref, nv_ref):
        acc = s_ref[...].astype(F32)
        for j in range(3):
            acc = acc + l_ref[j].astype(F32)
        g_ref[...] = acc
        d_ref[...], nm_ref[...], nv_ref[...] = _adamw_math(w_ref[...], acc, m_ref[...], v_ref[...])

    blk = pl.BlockSpec((tm, cols), lambda i, ch: (i, 0))
    return pl.pallas_call(
        body, name=name,
        grid_spec=pltpu.PrefetchScalarGridSpec(
            num_scalar_prefetch=1, grid=(rows // tm,),
            in_specs=[pl.BlockSpec((None, tm, cols), lambda i, ch: (ch[0], i, 0)),
                      pl.BlockSpec((3, tm, cols), lambda i, ch: (0, i, 0)), blk, blk, blk],
            out_specs=[blk] * 4),
        out_shape=[_sds((rows, cols), F32)] * 4, compiler_params=_params(("parallel",)),
    )(chip, pair_sums, landed, w, m, v)


_HBM = pl.BlockSpec(memory_space=pltpu.HBM)
_SEM = pl.BlockSpec(memory_space=pltpu.SEMAPHORE)
_EFFECT = pltpu.SideEffectType.DATAFLOW_SIDE_EFFECTING


def _chip_routes(n):
    def plan(x, y, c):
        routes = []
        for a in range(n):
            for j in range(1, 4):
                px, py = x ^ (j >> 1), y ^ (j & 1)
                routes.append((a, 2 * px + py, (px, py, c), j - 1))
        return routes
    return plan, 3 * n


def _bcast_routes(n):
    def plan(x, y, c):
        routes = []
        for a in range(n):
            for k in range(1, N_DEV):
                peer = (x ^ ((k >> 2) & 1), y ^ ((k >> 1) & 1), c ^ (k & 1))
                routes.append((a, 0, peer, 4 * x + 2 * y + c))
        return routes
    return plan, 7 * n


def _route_copies(srcs, lands, send_sems, recv_sems, routes):
    return [pltpu.make_async_remote_copy(
        src_ref=srcs[a].at[sb], dst_ref=lands[a].at[lb], send_sem=send_sems.at[r], recv_sem=recv_sems.at[r],
        device_id=peer, device_id_type=MESH_T) for r, (a, sb, peer, lb) in enumerate(routes)]


def _exchange_start(srcs, lands, routes, name, after=()):
    plan, count = routes
    n = len(srcs)
    n_in = 2 * n + len(after)

    def body(*refs):
        send_sems, recv_sems = refs[n_in], refs[n_in + 1]
        token = refs[-1]
        for cp in _route_copies(refs[:n], refs[n:2 * n], send_sems, recv_sems, plan(*_mesh_pos())):
            cp.start()
        token[...] = jnp.zeros_like(token)

    args = [pltpu.with_memory_space_constraint(a, pltpu.HBM) for a in list(srcs) + list(lands)]
    out = pl.pallas_call(
        body, name=name,
        out_shape=(pltpu.SemaphoreType.DMA((count,)), pltpu.SemaphoreType.DMA((count,)),
                   *[pltpu.HBM(a.shape, a.dtype) for a in args], _sds((8, 128), F32)),
        in_specs=[_HBM] * (2 * n) + [pl.BlockSpec(memory_space=pl.ANY)] * len(after),
        out_specs=(_SEM, _SEM, *([_HBM] * (2 * n)), pl.BlockSpec(memory_space=pltpu.VMEM)),
        input_output_aliases={i: 2 + i for i in range(2 * n)},
        compiler_params=pltpu.CompilerParams(has_side_effects=_EFFECT),
    )(*args, *after)
    return (out[0], out[1], list(out[2:2 + 2 * n]), routes), out[-1]


def _exchange_wait(state, after, name):
    send_sems, recv_sems, bufs, (plan, count) = state
    n = len(bufs) // 2

    def body(*refs):
        send_s, recv_s = refs[2 * n], refs[2 * n + 1]
        for cp in _route_copies(refs[:n], refs[n:2 * n], send_s, recv_s, plan(*_mesh_pos())):
            cp.wait_send()
            cp.wait_recv()

    out = pl.pallas_call(
        body, name=name, out_shape=tuple(pltpu.HBM(a.shape, a.dtype) for a in bufs),
        in_specs=[_HBM] * (2 * n) + [_SEM, _SEM, pl.BlockSpec(memory_space=pl.ANY)], out_specs=tuple([_HBM] * (2 * n)),
        input_output_aliases={i: i for i in range(2 * n)},
        compiler_params=pltpu.CompilerParams(has_side_effects=_EFFECT),
    )(*bufs, send_sems, recv_sems, after)
    return list(out[:n]), list(out[n:])


def _group_routes(js):
    def plan(x, y, c):
        return [(0, 0, (x ^ (j >> 1), y ^ (j & 1), c), 2 * j + c) for j in js]
    return plan, len(js)


def _pair_fill(groups, js, name, after=()):
    def body(*refs):
        g_ref, send_sems, recv_sems = refs[-3:]
        x, y, c = _mesh_pos()
        sends = []
        for n, j in enumerate(js):
            mine = g_ref.at[2 * j + c]
            sends.append(pltpu.make_async_remote_copy(
                src_ref=mine, dst_ref=mine, send_sem=send_sems.at[n], recv_sem=recv_sems.at[n],
                device_id=(x, y, 1 - c), device_id_type=MESH_T))
        for cp in sends:
            cp.start()
        for n, j in enumerate(js):
            pltpu.make_async_remote_copy(
                src_ref=g_ref.at[2 * j + c], dst_ref=g_ref.at[2 * j + 1 - c], send_sem=send_sems.at[n],
                recv_sem=recv_sems.at[n], device_id=(x, y, 1 - c), device_id_type=MESH_T).wait_recv()
        for cp in sends:
            cp.wait_send()

    hbm = pl.BlockSpec(memory_space=pl.ANY)
    return pl.pallas_call(
        body, name=name, in_specs=[hbm] * (1 + len(after)), out_specs=hbm, out_shape=_sds(groups.shape, groups.dtype),
        input_output_aliases={0: 0},
        scratch_shapes=[pltpu.SemaphoreType.DMA((len(js),)), pltpu.SemaphoreType.DMA((len(js),))],
    )(groups, *after)


def _in_proj_group(h_all, groups, j0, ng, chip, px_prev, after, name):
    rows_all = h_all.shape[0]
    gcols = IN_COLS // 4
    tm = _pick(rows_all, 1536, 128)
    g4 = groups.reshape(4, gcols, D)

    n_lead = (1 if px_prev is not None else 0) + len(after)
    lead = ([px_prev] if px_prev is not None else []) + list(after)

    def body(chip_ref, *refs):
        h_ref, w_ref, o_ref = refs[n_lead:]
        o_ref[...] = _dot(h_ref[...], w_ref[...], NT).astype(BF16)

    return pl.pallas_call(
        body, name=name,
        grid_spec=pltpu.PrefetchScalarGridSpec(
            num_scalar_prefetch=1, grid=(ng, rows_all // tm),
            in_specs=[pl.BlockSpec(memory_space=pl.ANY)] * n_lead
            + [pl.BlockSpec((tm, D), lambda n, i, ch: (i, 0)),
               pl.BlockSpec((None, gcols, D), lambda n, i, ch: (j0 + n, 0, 0))],
            out_specs=pl.BlockSpec((tm, gcols), lambda n, i, ch: (i, ch[0] ^ (j0 + n)))),
        out_shape=_sds((rows_all, IN_COLS), BF16),
        input_output_aliases={1: 0} if px_prev is not None else {},
        compiler_params=_params(("parallel", "parallel")),
    )(chip, *lead, h_all, g4)


def _d_h_groups(dp_all, groups, chip, after):
    rows_all = dp_all.shape[0]
    gcols = IN_COLS // 4
    tm = _pick(rows_all, 1536, 128)
    g4 = groups.reshape(4, gcols, D)
    n_lead = len(after)

    def body(chip_ref, *refs):
        a_ref, w_ref, o_ref = refs[n_lead:]
        j = pl.program_id(1)
        part = _dot(a_ref[...], w_ref[...])

        @pl.when(j == 0)
        def _():
            o_ref[...] = part

        @pl.when(j > 0)
        def _():
            o_ref[...] += part

    return pl.pallas_call(
        body, name="d_h",
        grid_spec=pltpu.PrefetchScalarGridSpec(
            num_scalar_prefetch=1, grid=(rows_all // tm, 4),
            in_specs=[pl.BlockSpec(memory_space=pl.ANY)] * n_lead
            + [pl.BlockSpec((tm, gcols), lambda i, j, ch: (i, ch[0] ^ j)),
               pl.BlockSpec((None, gcols, D), lambda i, j, ch: (j, 0, 0))],
            out_specs=pl.BlockSpec((tm, D), lambda i, j, ch: (i, 0))),
        out_shape=_sds((rows_all, D), F32),
        compiler_params=_params(("parallel", "arbitrary")),
    )(chip, *after, dp_all, g4)


def _reduce_scatter_start(parts, core, name):
    got = _pair_exchange(parts, name + "_pair")
    sums = [_pair_add(p, g, core, "%s_add_%d" % (name, i)) for i, (p, g) in enumerate(zip(parts, got))]
    lands = [lax.empty((3,) + s_.shape[1:], BF16) for s_ in sums]
    return _exchange_start(sums, lands, _chip_routes(len(sums)), name + "_start")


def _reduce_scatter_finish(rs_state, after, chip, wmv, name):
    sums, landed = _exchange_wait(rs_state, after, name + "_wait")
    return [_chip_sum_adamw(s_, l_, chip, *t, "%s_adamw_%d" % (name, i))
            for i, (s_, l_, t) in enumerate(zip(sums, landed, wmv))]


def _local_step(x, c, ctx, norm_w, ret_log2_decay, q_norm_w, k_norm_w, loss_target,
                mod, proj_in, proj_back, get_w_o, on_out_grads, on_in_grad, started=()):
    nb, seq, _ = x.shape
    cx = ctx.shape[1]
    t_rows, c_rows = nb * seq, nb * cx
    rows_all = t_rows + c_rows
    nc = seq // CH
    tm = _pick(seq, 256, 128)
    te = _pick(seq, 512, 128)
    assert cx % tm == 0 and t_rows % cx == 0 and seq % GRID_W == 0

    x2 = x.reshape(t_rows, D)
    ctx2 = ctx.reshape(c_rows, D)
    tgt = loss_target.reshape(t_rows, D)
    lg = _log_gamma(ret_log2_decay)
    cos, sin = _rope_tables(seq)

    mod3 = mod[:, None, :]
    h_all = _norm_fwd(x2, mod3, norm_w, rows_all, 0, seq, 0, None, te, "norm_fwd", after=started)
    h_all = _norm_fwd(ctx2, mod3, norm_w, rows_all, t_rows, c_rows, nb, h_all, tm, "norm_fwd_ctx")
    px = proj_in(h_all)
    s0f, s0b = _ctx_state(px, lg, nb, t_rows, cx)
    o_f, o_b, hist_f, hist_b = _ret_fwd(px, lg, s0f, s0b, nb, nc)
    yret16 = _ret_post(o_f, o_b, px, te)
    q16 = _qk_prep(px, q_norm_w, cos, sin, t_rows, 0, AQ, HQ, 4, seq, te, "q_prep")
    kx16 = _qk_prep(px, k_norm_w, cos, sin, t_rows, 0, AK, HKV, HKV, seq, te, "k_prep")
    kc16 = _qk_prep(px, k_norm_w, None, None, c_rows, t_rows, AK, HKV, HKV, seq, tm, "kc_prep")
    o_att, yatt16, lse = _att_fwd(q16, kx16, kc16, px, nb, seq, cx, te)
    w_o_ret16, w_o_att16, w_out16 = get_w_o(lse)
    a_ret, a_att, y16 = _merge(yret16, yatt16, px, w_o_ret16, w_o_att16, te)
    dxn, dout16, dgate, loss_b = _outproj(y16, w_out16, x2, tgt, mod3, nb, seq, te)

    gw_out = _matmul(y16, dout16, ta=True, tm=D, tn=D, tk=D, out_dtype=BF16, name="gw_out")
    da_ret16, da_att16, dmr16, dma16 = _bwd_merge(dout16, w_out16, px, a_ret, a_att, te)
    gw_o_ret = _matmul(yret16, da_ret16, ta=True, tm=D, tn=D, tk=D, out_dtype=BF16, name="gw_o_ret")
    gw_o_att = _matmul(yatt16, da_att16, ta=True, tm=D, tn=D, tk=D, out_dtype=BF16, name="gw_o_att")
    out_state, out_started = on_out_grads([gw_o_ret, gw_o_att, gw_out])
    do16, drg16 = _bwd_branch_ret(da_ret16, w_o_ret16, px, o_f, o_b, te, after=out_started)
    dao, dag16 = _bwd_branch_att(da_att16, w_o_att16, px, o_att, te)
    dq_rot, dkx, dvx, dkc, dvc = _att_bwd(q16, kx16, kc16, px, dao, o_att, lse, nb, seq, cx, te)
    daq16, gq = _qk_prep_bwd(dq_rot, px, q_norm_w, cos, sin, t_rows, 0, AQ, HQ, 4, seq, te, "q_prep_bwd")
    dak16, gk_lat = _qk_prep_bwd(dkx.reshape(t_rows, HKV * HD), px, k_norm_w, cos, sin, t_rows, 0, AK, HKV, HKV, seq, te,
                                 "k_prep_bwd")
    dcak16, gk_ctx = _qk_prep_bwd(dkc.reshape(c_rows, HKV * HD), px, k_norm_w, None, None, c_rows, t_rows, AK, HKV, HKV,
                                  seq, tm, "kc_prep_bwd")
    dq_f, dk_f, dv_f, dq_b, dk_b, dv_b, ds_f, ds_b, dlg_scan = _ret_bwd(px, lg, do16, hist_f, hist_b, nb, nc)
    dck16, dcv16, dlg_ctx = _ctx_state_bwd(px, lg, ds_f, ds_b, nb, t_rows, cx)
    dp_all = _assemble_lat(rows_all, dk_f, dk_b, dv_f, dv_b, dak16, dvx.reshape(t_rows, HKV * HD), dq_f, dq_b, drg16,
                           daq16, dag16, dmr16, dma16, tm)
    dp_all = _assemble_ctx(dp_all, dck16, dcv16, dcak16, dvc.reshape(c_rows, HKV * HD), t_rows, tm)
    gw_in_t = _matmul(dp_all, h_all, ta=True, tm=1536, tn=D, tk=2304, out_dtype=BF16, name="gw_in")
    in_state, in_started = on_in_grad(gw_in_t)
    dh = proj_back(dp_all, in_started)
    grad_x, dsh, dsc, gnw_lat = _norm_bwd(dh, x2, mod3, norm_w, dxn, 0, seq, 0, te, "norm_bwd")
    dsh_c, dsc_c, gnw_ctx = _norm_bwd(dh, ctx2, mod3, norm_w, None, t_rows, c_rows, nb, tm, "norm_bwd_ctx")

    dlg = (jnp.sum(dlg_scan[:, :, 0], axis=0) + jnp.sum(dlg_ctx[:, :, :2, 0], axis=0).T.reshape(2 * RH)).reshape(1, 2 * RH)
    misc = jnp.concatenate([gq, gk_lat + gk_ctx, dlg, jnp.sum(loss_b[:, 0, 0]).reshape(1, 1),
                            jnp.zeros((1, D - 2 * HD - 2 * RH - 1), F32)], axis=1)
    rows = []
    for b in range(nb):
        rows += [dsh[b], dsc[b], dgate[b]]
    rows += [dsh_c[0], dsc_c[0]] + [c[b:b + 1] for b in range(nb)] + [gnw_lat + gnw_ctx, misc]
    payload = jnp.concatenate(rows + [jnp.zeros((PAY_ROWS - len(rows), D), F32)], axis=0)
    return grad_x.reshape(nb, seq, D), out_state, in_state, payload


def _finish_small(gathered, nb, c_ctx, ret_log2_decay, w_ada16, dev):
    n_dev = gathered.shape[0]
    loc = 3 * D // n_dev
    dmod_all = gathered[:, :3 * nb].reshape(n_dev * nb, 3 * D)
    dmodc_parts = jnp.concatenate([gathered[:, 3 * nb:3 * nb + 2].reshape(n_dev, 2 * D), jnp.zeros((n_dev, D), F32)], axis=1)
    c_all = gathered[:, 3 * nb + 2:4 * nb + 2].reshape(n_dev * nb, D)
    nw_parts = gathered[:, 4 * nb + 2]
    misc_parts = gathered[:, 4 * nb + 3]
    n_rows = n_dev * nb + n_dev
    pad = (-n_rows) % 16
    c_rows = jnp.concatenate([c_all, jnp.broadcast_to(c_ctx.reshape(1, D), (n_dev, D)), jnp.zeros((pad, D), F32)], axis=0)
    dm_rows = jnp.concatenate([dmod_all, dmodc_parts, jnp.zeros((pad, 3 * D), F32)], axis=0)
    dm_loc_rows = lax.dynamic_slice_in_dim(dm_rows, dev * loc, loc, axis=1)
    r_pad = jnp.full((1, D), -1.0, F32).at[:, 2 * HD:2 * HD + 2 * RH].set(ret_log2_decay.reshape(1, 2 * RH))
    gb, gc, gnw, misc, gwa = _small_final(dmod_all, dmodc_parts, c_rows, dm_loc_rows, nw_parts, misc_parts,
                                          c_ctx.reshape(1, D), r_pad, w_ada16)
    return (gb, gc, gnw, misc[:, :HD], misc[:, HD:2 * HD], misc[:, 2 * HD:2 * HD + 2 * RH], gwa,
            misc[0, 2 * HD + 2 * RH])


def kernel(x, c, ctx, c_ctx, norm_w, w_ada, b_ada, w_in, ret_log2_decay, q_norm_w, k_norm_w, w_o_ret, w_o_att, w_out, loss_target, m_c_ctx, m_norm_w, m_w_ada, m_b_ada, m_w_in, m_ret_log2_decay, m_q_norm_w, m_k_norm_w, m_w_o_ret, m_w_o_att, m_w_out, v_c_ctx, v_norm_w, v_w_ada, v_b_ada, v_w_in, v_ret_log2_decay, v_q_norm_w, v_k_norm_w, v_w_o_ret, v_w_o_att, v_w_out):
    nb = x.shape[0]
    mx, my, mc = _mesh_pos()
    dev = 4 * mx + 2 * my + mc
    core = jnp.reshape(mc, (1,)).astype(jnp.int32)
    chip = jnp.reshape(2 * mx + my, (1,)).astype(jnp.int32)

    n_loc = 3 * D // N_DEV
    c8 = jnp.zeros((8, D), F32).at[:nb].set(c).at[nb].set(c_ctx)
    (c_all,) = _all_gather([c8], "gather_c")
    ada_shard = w_ada[0].astype(BF16)
    b_loc = lax.dynamic_slice(b_ada, (0, dev * n_loc), (1, n_loc))
    mod_cols = _mod_part(c_all.reshape(N_DEV * 8, D), ada_shard, b_loc)
    (mod_all,) = _all_gather([mod_cols], "gather_mod")
    mod = jnp.transpose(lax.dynamic_slice(mod_all, (0, dev * 8, 0), (N_DEV, 8, n_loc)), (1, 0, 2)).reshape(8, 3 * D)
    ada_land = lax.dynamic_update_slice(lax.empty((N_DEV,) + ada_shard.shape, BF16), ada_shard[None], (dev, 0, 0))

    w_in_t = jnp.transpose(w_in[0])
    in_shard = w_in_t.astype(BF16)
    groups = lax.dynamic_update_slice(lax.empty((N_DEV,) + in_shard.shape, BF16), in_shard[None], (mc, 0, 0))
    groups = _pair_fill(groups, (0,), "gather_in_pair", after=(mod_all,))
    (near_send, near_recv, near_bufs, near_routes), gin_token = _exchange_start(
        [in_shard[None]], [groups], _group_routes((1, 2)), "gather_in_start")
    w_in_groups, wo_states, ada_states = [], [], []
    wo_shards = [w_[0].astype(BF16) for w_ in (w_o_ret, w_o_att, w_out)]
    wo_lands = [lax.dynamic_update_slice(lax.empty((N_DEV,) + s_.shape, BF16), s_[None], (dev, 0, 0)) for s_ in wo_shards]

    def _state(send, recv, src, groups, routes):
        return send, recv, [src, groups], routes

    def proj_in(h_all):
        src, groups = near_bufs
        px = _in_proj_group(h_all, groups, 0, 1, chip, None, (gin_token,), "in_proj_0")
        (src,), (groups,) = _exchange_wait(_state(near_send, near_recv, src, groups, near_routes), px,
                                           "gather_in_wait_near")
        groups = _pair_fill(groups, (1, 2), "gather_in_fill_near")
        (far_send, far_recv, (src, groups), far_routes), far_token = _exchange_start(
            [src], [groups], _group_routes((3,)), "gather_in_start_far")
        wo_state, wo_token = _exchange_start([s_[None] for s_ in wo_shards], wo_lands, _bcast_routes(3),
                                             "gather_wo_start", after=(far_token,))
        wo_states.append(wo_state)
        ada_state, ada_token = _exchange_start([ada_shard[None]], [ada_land], _bcast_routes(1), "gather_ada_start",
                                               after=(wo_token,))
        ada_states.append(ada_state)
        px = _in_proj_group(h_all, groups, 1, 2, chip, px, (ada_token,), "in_proj_near")
        (src,), (groups,) = _exchange_wait(_state(far_send, far_recv, src, groups, far_routes), px,
                                           "gather_in_wait_far")
        groups = _pair_fill(groups, (3,), "gather_in_fill_far")
        px = _in_proj_group(h_all, groups, 3, 1, chip, px, (), "in_proj_far")
        w_in_groups.append(groups)
        return px

    def proj_back(dp_all, after):
        return _d_h_groups(dp_all, w_in_groups[0], chip, after)

    def get_w_o(after):
        _, (l_ret, l_att, l_out) = _exchange_wait(wo_states[0], after, "gather_wo_wait")
        return l_ret.reshape(RH * DV, D), l_att.reshape(D, D), l_out.reshape(D, D)

    def on_out_grads(grads):
        parts = [g_.reshape(N_DEV, g_.shape[0] // N_DEV, D) for g_ in grads]
        state, token = _reduce_scatter_start(parts, core, "rs_out")
        return state, (token,)

    def on_in_grad(grad):
        state, token = _reduce_scatter_start([grad.reshape(N_DEV, IN_COLS // N_DEV, D)], core, "rs_in")
        return state, (token,)

    grad_x, out_state, in_state, payload = _local_step(
        x, c, ctx, norm_w, ret_log2_decay, q_norm_w, k_norm_w, loss_target,
        mod, proj_in, proj_back, get_w_o, on_out_grads, on_in_grad, started=(gin_token,))

    (gathered,) = _all_gather([payload], "gather_small")
    _, (l_ada,) = _exchange_wait(ada_states[0], gathered, "gather_ada_wait")
    w_ada16 = jnp.transpose(l_ada, (1, 0, 2)).reshape(D, 3 * D)
    gb, gc, gnw, gq, gk, gr, gwa, loss = _finish_small(gathered, nb, c_ctx, ret_log2_decay, w_ada16, dev)

    out_res = _reduce_scatter_finish(out_state, gathered, chip,
                                     [(w_[0], m_[0], v_[0]) for w_, m_, v_ in ((w_o_ret, m_w_o_ret, v_w_o_ret),
                                                                                (w_o_att, m_w_o_att, v_w_o_att),
                                                                                (w_out, m_w_out, v_w_out))], "rs_out")
    (in_res,) = _reduce_scatter_finish(in_state, gathered, chip,
                                       [(w_in_t, jnp.transpose(m_w_in[0]), jnp.transpose(v_w_in[0]))], "rs_in")
    big = {4: [jnp.transpose(r)[None] for r in in_res]}
    for i, res in zip((8, 9, 10), out_res):
        big[i] = [r[None] for r in res]
    small_g = {0: gc.reshape(c_ctx.shape), 1: gnw, 2: gwa[None], 3: gb, 5: gr.reshape(ret_log2_decay.shape), 6: gq, 7: gk}
    weights = [c_ctx, norm_w, w_ada, b_ada, w_in, ret_log2_decay, q_norm_w, k_norm_w, w_o_ret, w_o_att, w_out]
    ms = [m_c_ctx, m_norm_w, m_w_ada, m_b_ada, m_w_in, m_ret_log2_decay, m_q_norm_w, m_k_norm_w, m_w_o_ret, m_w_o_att, m_w_out]
    vs = [v_c_ctx, v_norm_w, v_w_ada, v_b_ada, v_w_in, v_ret_log2_decay, v_q_norm_w, v_k_norm_w, v_w_o_ret, v_w_o_att, v_w_out]
    grads, deltas, new_ms, new_vs = [], [], [], []
    for i, (w, m, v) in enumerate(zip(weights, ms, vs)):
        if i in big:
            res = big[i]
        else:
            shape2 = (-1, w.shape[-1])
            g = small_g[i]
            res = [g] + [r.reshape(w.shape) for r in _adamw(w.reshape(shape2), g.reshape(shape2), m.reshape(shape2),
                                                             v.reshape(shape2), "adamw_%d" % i)]
        for lst, r in zip((grads, deltas, new_ms, new_vs), res):
            lst.append(r)
    return (loss, grad_x, *grads, *deltas, *new_ms, *new_vs)
```

```python
import numpy as np
import jax
import jax.numpy as jnp
from jax import lax
from jax.experimental import pallas as pl
from jax.experimental.pallas import tpu as pltpu

F32 = jnp.float32
BF16 = jnp.bfloat16

D = 1024
RH, DK, DV, CH = 4, 256, 512, 256
HQ, HKV, HD = 8, 2, 128
GRID_W = 64
ROPE_THETA = 10000.0
EPS = 1e-6
RK, RV, AK, AV, RQ, RG, AQ, AG, MR, MA = 0, 1024, 3072, 3328, 3584, 4608, 6656, 7680, 8704, 9728
IN_COLS = 10752
KV_COLS = 3584
N_DEV = 8
LR, B1, B2, ADAM_EPS, WD, STEP = 0.001, 0.9, 0.999, 1e-08, 0.01, 10
PAY_ROWS = 16
VMEM_LIMIT = 56 * 1024 * 1024
MESH_T = pl.DeviceIdType.MESH

NT = (((1,), (1,)), ((), ()))
TN = (((0,), (0,)), ((), ()))
SM_C = (HD ** -0.5) * float(np.log2(np.e))


def _params(sem):
    return pltpu.CompilerParams(dimension_semantics=sem, vmem_limit_bytes=VMEM_LIMIT)


def _pick(n, target, mult=8):
    best = None
    for t in range(mult, min(n, target) + 1, mult):
        if n % t == 0:
            best = t
    return best or n


def _dot(a, b, dn=None):
    if dn is None:
        return jnp.dot(a, b, preferred_element_type=F32)
    return lax.dot_general(a, b, dn, preferred_element_type=F32)


def _sig(v):
    return jax.nn.sigmoid(v)


def _silu(v):
    return v * _sig(v)


def _dsilu(v):
    s = _sig(v)
    return s * (1.0 + v * (1.0 - s))


def _sds(shape, dtype):
    return jax.ShapeDtypeStruct(shape, dtype)


def _matmul(a, b, *, ta=False, tb=False, tm, tn, tk, out_dtype, name, after=()):
    m = a.shape[1] if ta else a.shape[0]
    kdim = a.shape[0] if ta else a.shape[1]
    n = b.shape[0] if tb else b.shape[1]
    tm, tn, tk = _pick(m, tm, 128), _pick(n, tn, 128), _pick(kdim, tk, 128)
    nk = kdim // tk
    dn = (((0 if ta else 1,), (1 if tb else 0,)), ((), ()))

    def body(a_ref, b_ref, *rest):
        o_ref, acc_ref = rest[-2:]
        k = pl.program_id(2)
        part = _dot(a_ref[...].astype(BF16), b_ref[...].astype(BF16), dn)
        if nk == 1:
            o_ref[...] = part.astype(o_ref.dtype)
        else:
            @pl.when(k == 0)
            def _():
                acc_ref[...] = part

            @pl.when(k > 0)
            def _():
                acc_ref[...] += part

            @pl.when(k == nk - 1)
            def _():
                o_ref[...] = acc_ref[...].astype(o_ref.dtype)

    a_spec = pl.BlockSpec((tk, tm), lambda i, j, k: (k, i)) if ta else pl.BlockSpec((tm, tk), lambda i, j, k: (i, k))
    b_spec = pl.BlockSpec((tn, tk), lambda i, j, k: (j, k)) if tb else pl.BlockSpec((tk, tn), lambda i, j, k: (k, j))
    return pl.pallas_call(
        body, name=name, grid=(m // tm, n // tn, nk),
        in_specs=[a_spec, b_spec] + [pl.BlockSpec(memory_space=pl.ANY)] * len(after),
        out_specs=pl.BlockSpec((tm, tn), lambda i, j, k: (i, j)), out_shape=_sds((m, n), out_dtype),
        scratch_shapes=[pltpu.VMEM((tm, tn) if nk > 1 else (8, 128), F32)],
        compiler_params=_params(("parallel", "parallel", "arbitrary")),
    )(a, b, *after)


def _log_gamma(r):
    rp = jnp.full((8, 128), -1.0, F32).at[:2, :RH].set(r.reshape(2, RH))

    def body(r_ref, o_ref):
        o_ref[...] = jnp.log1p(-jnp.exp2(r_ref[...]))

    out = pl.pallas_call(body, name="log_gamma", out_shape=_sds((8, 128), F32))(rp)
    return out[:2, :RH]


def _mod_part(c_rows, w_ada_loc16, b_loc):
    def body(c_ref, w_ref, b_ref, o_ref):
        o_ref[...] = _dot(_silu(c_ref[...]).astype(BF16), w_ref[...]) + b_ref[...]

    return pl.pallas_call(
        body, name="mod_part", out_shape=_sds((c_rows.shape[0], w_ada_loc16.shape[1]), F32),
    )(c_rows, w_ada_loc16, b_loc)


def _norm_fwd(x2, mod3, norm_w, rows_all, row_off, rows_per_group, group0, h_prev, tm, name, after=()):
    rows = x2.shape[0]
    rb0 = row_off // tm
    bpg = rows_per_group // tm

    def body(*refs):
        x_ref, sh_ref, sc_ref, nw_ref, o_ref = refs[-5:]
        xv = x_ref[...]
        r = lax.rsqrt(jnp.mean(xv * xv, axis=-1, keepdims=True) + EPS)
        o_ref[...] = ((xv * r) * nw_ref[...] * (1.0 + sc_ref[...]) + sh_ref[...]).astype(BF16)

    in_specs = [pl.BlockSpec((tm, D), lambda i: (i, 0)),
                pl.BlockSpec((None, 1, D), lambda i: (group0 + i // bpg, 0, 0)),
                pl.BlockSpec((None, 1, D), lambda i: (group0 + i // bpg, 0, 1)),
                pl.BlockSpec((1, D), lambda i: (0, 0))]
    in_specs = [pl.BlockSpec(memory_space=pl.ANY)] * len(after) + in_specs
    args = list(after) + [x2, mod3, mod3, norm_w]
    alias = {}
    if h_prev is not None:
        in_specs.insert(0, pl.BlockSpec(memory_space=pl.ANY))
        args.insert(0, h_prev)
        alias = {0: 0}
    return pl.pallas_call(
        body, name=name, grid=(rows // tm,), in_specs=in_specs,
        out_specs=pl.BlockSpec((tm, D), lambda i: (rb0 + i, 0)), out_shape=_sds((rows_all, D), BF16),
        input_output_aliases=alias, compiler_params=_params(("parallel",)),
    )(*args)


def _decays(lg, fwd):
    ii = lax.broadcasted_iota(jnp.int32, (CH, CH), 0)
    jj = lax.broadcasted_iota(jnp.int32, (CH, CH), 1)
    ri = lax.broadcasted_iota(jnp.int32, (CH, 1), 0).astype(F32)
    rel = (ii - jj) if fwd else (jj - ii)
    relf = jnp.maximum(rel, 0).astype(F32)
    mask = jnp.where(rel >= 0, jnp.exp(lg * relf), 0.0)
    qe = (ri + 1.0) if fwd else (CH - ri)
    ke = (CH - 1.0 - ri) if fwd else ri
    return mask, relf, jnp.exp(lg * qe), qe, jnp.exp(lg * ke), ke


def _wide_specs(rowf):
    return [pl.BlockSpec((CH, 2 * DK), lambda b, c: (rowf(b, c), RQ // (2 * DK))),
            pl.BlockSpec((CH, 2 * DK), lambda b, c: (rowf(b, c), RQ // (2 * DK) + 1)),
            pl.BlockSpec((CH, RH * DK), lambda b, c: (rowf(b, c), RK // (RH * DK))),
            pl.BlockSpec((CH, 2 * DV), lambda b, c: (rowf(b, c), RV // (2 * DV))),
            pl.BlockSpec((CH, 2 * DV), lambda b, c: (rowf(b, c), RV // (2 * DV) + 1))]


def _head_qkv(refs, h):
    q0, q1, k, v0, v1 = refs
    lo = h % 2
    q = (q0, q1)[h // 2][:, lo * DK:(lo + 1) * DK].astype(F32)
    kk = k[:, h * DK:(h + 1) * DK].astype(F32) * (DK ** -0.5)
    v16 = (v0, v1)[h // 2][:, lo * DV:(lo + 1) * DV].astype(BF16)
    return q, kk, v16


def _ctx_state(px, lg, nb, t_rows, cx):
    rb = t_rows // cx

    def body(lg_ref, k_ref, v_ref, sf_ref, sb_ref):
        h = pl.program_id(1)
        pos = lax.broadcasted_iota(jnp.int32, (cx, 1), 0).astype(F32)
        k = k_ref[...].astype(F32) * (DK ** -0.5)
        v16 = v_ref[...].astype(BF16)
        wf = jnp.exp(lg_ref[0, h] * (cx - 1.0 - pos))
        wb = jnp.exp(lg_ref[1, h] * pos)
        sf_ref[...] = _dot((k * wf).astype(BF16), v16, TN)
        sb_ref[...] = _dot((k * wb).astype(BF16), v16, TN)

    st = pl.BlockSpec((None, None, DK, DV), lambda b, h: (b, h, 0, 0))
    return pl.pallas_call(
        body, name="ctx_state", grid=(nb, RH),
        in_specs=[pl.BlockSpec(memory_space=pltpu.SMEM),
                  pl.BlockSpec((cx, DK), lambda b, h: (rb + b, RK // DK + h)),
                  pl.BlockSpec((cx, DV), lambda b, h: (rb + b, RV // DV + h))],
        out_specs=[st, st], out_shape=[_sds((nb, RH, DK, DV), F32)] * 2,
        compiler_params=_params(("parallel", "parallel")),
    )(lg, px, px)


def _ret_fwd(px, lg, s0f, s0b, nb, nc):
    t_rows = nb * nc * CH

    def body(lg_ref, *refs):
        ins = (refs[0:5], refs[5:10])
        s0f_ref, s0b_ref, of_ref, ob_ref, hf_ref, hb_ref, sf, sb = refs[10:]
        c = pl.program_id(1)

        @pl.when(c == 0)
        def _():
            sf[...] = s0f_ref[...]
            sb[...] = s0b_ref[...]

        for d, (o_ref, h_ref, s) in enumerate(((of_ref, hf_ref, sf), (ob_ref, hb_ref, sb))):
            for h in range(RH):
                lg_d = lg_ref[d, h]
                mask, _, qd, _, kd, _ = _decays(lg_d, d == 0)
                q, k, v16 = _head_qkv(ins[d], h)
                a = _dot(q.astype(BF16), k.astype(BF16), NT)
                st = s[h]
                st16 = st.astype(BF16)
                h_ref[h] = st16
                o = _dot((a * mask).astype(BF16), v16) + _dot((q * qd).astype(BF16), st16)
                o_ref[:, h * DV:(h + 1) * DV] = o.astype(BF16)
                s[h] = st * jnp.exp(lg_d * CH) + _dot((k * kd).astype(BF16), v16, TN)

    def fw(b, c):
        return b * nc + c

    def bw(b, c):
        return b * nc + nc - 1 - c

    st = pl.BlockSpec((None, RH, DK, DV), lambda b, c: (b, 0, 0, 0))
    in_specs = [pl.BlockSpec(memory_space=pltpu.SMEM)] + _wide_specs(fw) + _wide_specs(bw) + [st, st]
    out_specs = [pl.BlockSpec((CH, RH * DV), lambda b, c: (fw(b, c), 0)),
                 pl.BlockSpec((CH, RH * DV), lambda b, c: (bw(b, c), 0)),
                 pl.BlockSpec((None, None, RH, DK, DV), lambda b, c: (b, c, 0, 0, 0)),
                 pl.BlockSpec((None, None, RH, DK, DV), lambda b, c: (b, nc - 1 - c, 0, 0, 0))]
    return pl.pallas_call(
        body, name="ret_fwd", grid=(nb, nc), in_specs=in_specs, out_specs=out_specs,
        out_shape=[_sds((t_rows, RH * DV), BF16)] * 2 + [_sds((nb, nc, RH, DK, DV), BF16)] * 2,
        scratch_shapes=[pltpu.VMEM((RH, DK, DV), F32), pltpu.VMEM((RH, DK, DV), F32)],
        compiler_params=_params(("parallel", "arbitrary")),
    )(lg, *([px] * 10), s0f, s0b)


def _ret_post(o_f, o_b, px, tm):
    t_rows = o_f.shape[0]

    def body(of_ref, ob_ref, g0, g1, g2, g3, y_ref):
        for h, g_ref in enumerate((g0, g1, g2, g3)):
            sl = slice(h * DV, (h + 1) * DV)
            o = of_ref[:, sl].astype(F32) + ob_ref[:, sl].astype(F32)
            r = lax.rsqrt(jnp.mean(o * o, axis=-1, keepdims=True) + EPS)
            y_ref[:, sl] = ((o * r) * _silu(g_ref[...].astype(F32))).astype(BF16)

    def gate(h):
        return pl.BlockSpec((tm, DV), lambda i: (i, RG // DV + h))

    wide = pl.BlockSpec((tm, RH * DV), lambda i: (i, 0))
    return pl.pallas_call(
        body, name="ret_post", grid=(t_rows // tm,),
        in_specs=[wide, wide] + [gate(h) for h in range(RH)],
        out_specs=wide, out_shape=_sds((t_rows, RH * DV), BF16),
        compiler_params=_params(("parallel",)),
    )(o_f, o_b, *([px] * RH))


def _rope_tables(seq):
    rows = seq // GRID_W
    row = np.repeat(np.arange(rows, dtype=np.float32), GRID_W)
    col = np.tile(np.arange(GRID_W, dtype=np.float32), rows)
    half = HD // 2
    freqs = (ROPE_THETA ** (-np.arange(0, half, 2, dtype=np.float32) / half)).astype(np.float32)
    ang = np.concatenate([row[:, None] * freqs, col[:, None] * freqs], axis=-1).astype(np.float32)
    cos = np.repeat(np.cos(ang), 2, axis=-1).astype(np.float32)
    sin = np.repeat(np.sin(ang), 2, axis=-1).astype(np.float32)
    sign = np.tile(np.array([-1.0, 1.0], np.float32), HD // 2)
    return jnp.asarray(cos), jnp.asarray(sin * sign)


def _swap_pairs(v):
    lane = lax.broadcasted_iota(jnp.int32, v.shape, 1)
    return jnp.where((lane & 1) == 0, pltpu.roll(v, HD - 1, 1), pltpu.roll(v, 1, 1))


def _qk_prep(px, nw, cos, sin, rows, row_off, col_off, heads, hb, seq, tm, name):
    rope = cos is not None
    rb0 = row_off // tm
    pb = seq // tm if rope else 1
    bw = hb * HD

    def body(*refs):
        if rope:
            x_ref, w_ref, c_ref, s_ref, o_ref = refs
        else:
            x_ref, w_ref, o_ref = refs
        for h in range(hb):
            sl = slice(h * HD, (h + 1) * HD)
            xv = x_ref[:, sl].astype(F32)
            r = lax.rsqrt(jnp.mean(xv * xv, axis=-1, keepdims=True) + EPS)
            t = (xv * r) * w_ref[...]
            if rope:
                t = t * c_ref[...] + _swap_pairs(t) * s_ref[...]
            o_ref[:, sl] = t.astype(BF16)

    in_specs = [pl.BlockSpec((tm, bw), lambda i, j: (rb0 + i, col_off // bw + j)),
                pl.BlockSpec((1, HD), lambda i, j: (0, 0))]
    args = [px, nw]
    if rope:
        in_specs += [pl.BlockSpec((tm, HD), lambda i, j: (i % pb, 0))] * 2
        args += [cos, sin]
    return pl.pallas_call(
        body, name=name, grid=(rows // tm, heads // hb), in_specs=in_specs,
        out_specs=pl.BlockSpec((tm, bw), lambda i, j: (i, j)), out_shape=_sds((rows, heads * HD), BF16),
        compiler_params=_params(("parallel", "parallel")),
    )(*args)


def _att_fwd(q16, kx16, kc16, px, nb, seq, cx, tq):
    t_rows = nb * seq
    nq = seq // tq
    rep = HQ // HKV
    gw = rep * HD

    def body(q_ref, kx_ref, kc_ref, vx_ref, vc_ref, g_ref, o_ref, y_ref, l_ref):
        kx = kx_ref[...]
        kc = kc_ref[...]
        vx = vx_ref[...].astype(BF16)
        vc = vc_ref[...].astype(BF16)
        l_ref[...] = jnp.zeros_like(l_ref)
        for r in range(rep):
            sl = slice(r * HD, (r + 1) * HD)
            q = q_ref[:, sl]
            s1 = _dot(q, kx, NT)
            s2 = _dot(q, kc, NT)
            m = jnp.maximum(jnp.max(s1, axis=-1, keepdims=True), jnp.max(s2, axis=-1, keepdims=True))
            e1 = jnp.exp2((s1 - m) * SM_C)
            e2 = jnp.exp2((s2 - m) * SM_C)
            tot = jnp.sum(e1, axis=-1, keepdims=True) + jnp.sum(e2, axis=-1, keepdims=True)
            o = (_dot(e1.astype(BF16), vx) + _dot(e2.astype(BF16), vc)) * (1.0 / tot)
            o_ref[:, sl] = o
            y_ref[:, sl] = (o * _silu(g_ref[:, sl].astype(F32))).astype(BF16)
            l_ref[:, r:r + 1] = m * SM_C + jnp.log(tot) * float(np.log2(np.e))

    qblk = pl.BlockSpec((tq, gw), lambda b, g, i: (b * nq + i, g))
    return pl.pallas_call(
        body, name="att_fwd", grid=(nb, HKV, nq),
        in_specs=[qblk,
                  pl.BlockSpec((seq, HD), lambda b, g, i: (b, g)),
                  pl.BlockSpec((cx, HD), lambda b, g, i: (b, g)),
                  pl.BlockSpec((seq, HD), lambda b, g, i: (b, AV // HD + g)),
                  pl.BlockSpec((cx, HD), lambda b, g, i: (t_rows // cx + b, AV // HD + g)),
                  pl.BlockSpec((tq, gw), lambda b, g, i: (b * nq + i, AG // gw + g))],
        out_specs=[qblk, qblk, pl.BlockSpec((tq, 128), lambda b, g, i: (b * nq + i, g))],
        out_shape=[_sds((t_rows, D), F32), _sds((t_rows, D), BF16), _sds((t_rows, HKV * 128), F32)],
        compiler_params=_params(("parallel", "parallel", "parallel")),
    )(q16, kx16, kc16, px, px, px)


def _gate_specs(tm, col0):
    hw = D // 2
    return [pl.BlockSpec((tm, hw), lambda i: (i, col0 // hw)), pl.BlockSpec((tm, hw), lambda i: (i, col0 // hw + 1))]


def _merge(yret16, yatt16, px, w_o_ret16, w_o_att16, tm):
    t_rows = yret16.shape[0]
    hw = D // 2

    def body(yr_ref, wr_ref, ya_ref, wa_ref, mr0, mr1, ma0, ma1, ar_ref, aa_ref, y_ref):
        ar = _dot(yr_ref[...], wr_ref[...])
        aa = _dot(ya_ref[...], wa_ref[...])
        ar_ref[...] = ar.astype(BF16)
        aa_ref[...] = aa.astype(BF16)
        for j, (mr_ref, ma_ref) in enumerate(((mr0, ma0), (mr1, ma1))):
            sl = slice(j * hw, (j + 1) * hw)
            y_ref[:, sl] = (_sig(mr_ref[...].astype(F32)) * ar[:, sl]
                            + _sig(ma_ref[...].astype(F32)) * aa[:, sl]).astype(BF16)

    row = pl.BlockSpec((tm, D), lambda i: (i, 0))
    return pl.pallas_call(
        body, name="merge", grid=(t_rows // tm,),
        in_specs=[pl.BlockSpec((tm, RH * DV), lambda i: (i, 0)), pl.BlockSpec((RH * DV, D), lambda i: (0, 0)),
                  row, pl.BlockSpec((D, D), lambda i: (0, 0))] + _gate_specs(tm, MR) + _gate_specs(tm, MA),
        out_specs=[row, row, row], out_shape=[_sds((t_rows, D), BF16)] * 3,
        compiler_params=_params(("parallel",)),
    )(yret16, w_o_ret16, yatt16, w_o_att16, px, px, px, px)


def _outproj(y16, w_out16, x2, tgt, mod3, nb, seq, tm):
    t_rows = nb * seq
    bpb = seq // tm

    def body(y_ref, w_ref, x_ref, t_ref, g_ref, dxn_ref, dout_ref, dg_ref, loss_ref):
        i = pl.program_id(1)
        out = _dot(y_ref[...], w_ref[...])
        gate = g_ref[...]
        diff = x_ref[...] + gate * out - t_ref[...]
        dxn = diff * (1.0 / D)
        dxn_ref[...] = dxn
        dout_ref[...] = (gate * dxn).astype(BF16)
        dg = jnp.sum(dxn * out, axis=0, keepdims=True)
        ls = jnp.broadcast_to(jnp.sum(diff * diff) * (0.5 / D), (1, 128))

        @pl.when(i == 0)
        def _():
            dg_ref[...] = dg
            loss_ref[...] = ls

        @pl.when(i > 0)
        def _():
            dg_ref[...] += dg
            loss_ref[...] += ls

    row = pl.BlockSpec((tm, D), lambda b, i: (b * bpb + i, 0))
    return pl.pallas_call(
        body, name="outproj", grid=(nb, bpb),
        in_specs=[row, pl.BlockSpec((D, D), lambda b, i: (0, 0)), row, row,
                  pl.BlockSpec((None, 1, D), lambda b, i: (b, 0, 2))],
        out_specs=[row, row, pl.BlockSpec((None, 1, D), lambda b, i: (b, 0, 0)),
                   pl.BlockSpec((None, 1, 128), lambda b, i: (b, 0, 0))],
        out_shape=[_sds((t_rows, D), F32), _sds((t_rows, D), BF16), _sds((nb, 1, D), F32), _sds((nb, 1, 128), F32)],
        compiler_params=_params(("parallel", "arbitrary")),
    )(y16, w_out16, x2, tgt, mod3)


def _bwd_merge(dout16, w_out16, px, a_ret, a_att, tm):
    t_rows = dout16.shape[0]
    hw = D // 2

    def body(do_ref, w_ref, mr0, mr1, ma0, ma1, ar_ref, aa_ref, dar_ref, daa_ref, dmr_ref, dma_ref):
        dy_all = _dot(do_ref[...], w_ref[...], NT)
        for j, (mr_ref, ma_ref) in enumerate(((mr0, ma0), (mr1, ma1))):
            sl = slice(j * hw, (j + 1) * hw)
            dy = dy_all[:, sl]
            sr = _sig(mr_ref[...].astype(F32))
            sa = _sig(ma_ref[...].astype(F32))
            dar_ref[:, sl] = (dy * sr).astype(BF16)
            daa_ref[:, sl] = (dy * sa).astype(BF16)
            dmr_ref[:, sl] = (dy * ar_ref[:, sl].astype(F32) * sr * (1.0 - sr)).astype(BF16)
            dma_ref[:, sl] = (dy * aa_ref[:, sl].astype(F32) * sa * (1.0 - sa)).astype(BF16)

    row = pl.BlockSpec((tm, D), lambda i: (i, 0))
    return pl.pallas_call(
        body, name="bwd_merge", grid=(t_rows // tm,),
        in_specs=[row, pl.BlockSpec((D, D), lambda i: (0, 0))] + _gate_specs(tm, MR) + _gate_specs(tm, MA) + [row, row],
        out_specs=[row] * 4, out_shape=[_sds((t_rows, D), BF16)] * 4,
        compiler_params=_params(("parallel",)),
    )(dout16, w_out16, px, px, px, px, a_ret, a_att)


def _bwd_branch_ret(da_ret16, w_o_ret16, px, o_f, o_b, tm, after=()):
    t_rows = da_ret16.shape[0]

    def body(da_ref, w_ref, g0, g1, g2, g3, of_ref, ob_ref, *rest):
        do_ref, dg_ref = rest[-2:]
        da = da_ref[...]
        for h, g_ref in enumerate((g0, g1, g2, g3)):
            sl = slice(h * DV, (h + 1) * DV)
            dy = _dot(da, w_ref[sl, :], NT)
            g = g_ref[...].astype(F32)
            o = of_ref[:, sl].astype(F32) + ob_ref[:, sl].astype(F32)
            r = lax.rsqrt(jnp.mean(o * o, axis=-1, keepdims=True) + EPS)
            on = o * r
            sg = _sig(g)
            don = dy * (g * sg)
            dg_ref[:, sl] = (dy * on * (sg * (1.0 + g * (1.0 - sg)))).astype(BF16)
            do_ref[:, sl] = (r * (don - on * jnp.mean(on * don, axis=-1, keepdims=True))).astype(BF16)

    def gate(h):
        return pl.BlockSpec((tm, DV), lambda i: (i, RG // DV + h))

    wide = pl.BlockSpec((tm, RH * DV), lambda i: (i, 0))
    return pl.pallas_call(
        body, name="bwd_branch_ret", grid=(t_rows // tm,),
        in_specs=[pl.BlockSpec((tm, D), lambda i: (i, 0)), pl.BlockSpec((RH * DV, D), lambda i: (0, 0))]
        + [gate(h) for h in range(RH)] + [wide, wide] + [pl.BlockSpec(memory_space=pl.ANY)] * len(after),
        out_specs=[wide, wide], out_shape=[_sds((t_rows, RH * DV), BF16)] * 2,
        compiler_params=_params(("parallel",)),
    )(da_ret16, w_o_ret16, *([px] * RH), o_f, o_b, *after)


def _bwd_branch_att(da_att16, w_o_att16, px, o_att, tm):
    t_rows = da_att16.shape[0]
    hw = D // 2

    def body(da_ref, w_ref, g0, g1, o_ref, dao_ref, dg_ref):
        dy_all = _dot(da_ref[...], w_ref[...], NT)
        for j, g_ref in enumerate((g0, g1)):
            sl = slice(j * hw, (j + 1) * hw)
            dy = dy_all[:, sl]
            g = g_ref[...].astype(F32)
            sg = _sig(g)
            dao_ref[:, sl] = dy * (g * sg)
            dg_ref[:, sl] = (dy * o_ref[:, sl] * (sg * (1.0 + g * (1.0 - sg)))).astype(BF16)

    row = pl.BlockSpec((tm, D), lambda i: (i, 0))
    return pl.pallas_call(
        body, name="bwd_branch_att", grid=(t_rows // tm,),
        in_specs=[row, pl.BlockSpec((D, D), lambda i: (0, 0))] + _gate_specs(tm, AG) + [row],
        out_specs=[row, row], out_shape=[_sds((t_rows, D), F32), _sds((t_rows, D), BF16)],
        compiler_params=_params(("parallel",)),
    )(da_att16, w_o_att16, px, px, o_att)


def _att_bwd(q16, kx16, kc16, px, dao, o_att, lse, nb, seq, cx, tq):
    t_rows = nb * seq
    nq = seq // tq
    rep = HQ // HKV
    gw = rep * HD
    scale = HD ** -0.5

    def body(q_ref, kx_ref, kc_ref, vx_ref, vc_ref, dao_ref, o_ref, l_ref, dq_ref, dkx_ref, dvx_ref, dkc_ref, dvc_ref):
        i = pl.program_id(2)
        kx = kx_ref[...]
        kc = kc_ref[...]
        vx = vx_ref[...].astype(BF16)
        vc = vc_ref[...].astype(BF16)
        dkx = jnp.zeros((seq, HD), F32)
        dvx = jnp.zeros((seq, HD), F32)
        dkc = jnp.zeros((cx, HD), F32)
        dvc = jnp.zeros((cx, HD), F32)
        for r in range(rep):
            sl = slice(r * HD, (r + 1) * HD)
            q = q_ref[:, sl]
            lr = l_ref[:, r:r + 1]
            p1 = jnp.exp2(_dot(q, kx, NT) * SM_C - lr)
            p2 = jnp.exp2(_dot(q, kc, NT) * SM_C - lr)
            da = dao_ref[:, sl]
            da16 = da.astype(BF16)
            delta = jnp.sum(da * o_ref[:, sl], axis=-1, keepdims=True)
            ds1 = (p1 * (_dot(da16, vx, NT) - delta)).astype(BF16)
            ds2 = (p2 * (_dot(da16, vc, NT) - delta)).astype(BF16)
            dq_ref[:, sl] = (_dot(ds1, kx) + _dot(ds2, kc)) * scale
            dkx += _dot(ds1, q, TN)
            dkc += _dot(ds2, q, TN)
            dvx += _dot(p1.astype(BF16), da16, TN)
            dvc += _dot(p2.astype(BF16), da16, TN)
        dkx = dkx * scale
        dkc = dkc * scale

        @pl.when(i == 0)
        def _():
            dkx_ref[...] = dkx
            dvx_ref[...] = dvx
            dkc_ref[...] = dkc
            dvc_ref[...] = dvc

        @pl.when(i > 0)
        def _():
            dkx_ref[...] += dkx
            dvx_ref[...] += dvx
            dkc_ref[...] += dkc
            dvc_ref[...] += dvc

    qblk = pl.BlockSpec((tq, gw), lambda b, g, i: (b * nq + i, g))
    kxb = pl.BlockSpec((None, seq, HD), lambda b, g, i: (b, 0, g))
    kcb = pl.BlockSpec((None, cx, HD), lambda b, g, i: (b, 0, g))
    return pl.pallas_call(
        body, name="att_bwd", grid=(nb, HKV, nq),
        in_specs=[qblk,
                  pl.BlockSpec((seq, HD), lambda b, g, i: (b, g)),
                  pl.BlockSpec((cx, HD), lambda b, g, i: (b, g)),
                  pl.BlockSpec((seq, HD), lambda b, g, i: (b, AV // HD + g)),
                  pl.BlockSpec((cx, HD), lambda b, g, i: (t_rows // cx + b, AV // HD + g)),
                  qblk, qblk, pl.BlockSpec((tq, 128), lambda b, g, i: (b * nq + i, g))],
        out_specs=[qblk, kxb, kxb, kcb, kcb],
        out_shape=[_sds((t_rows, D), F32), _sds((nb, seq, HKV * HD), F32), _sds((nb, seq, HKV * HD), F32),
                   _sds((nb, cx, HKV * HD), F32), _sds((nb, cx, HKV * HD), F32)],
        compiler_params=_params(("parallel", "parallel", "arbitrary")),
    )(q16, kx16, kc16, px, px, dao, o_att, lse)


def _qk_prep_bwd(dt, px, nw, cos, sin, rows, row_off, col_off, heads, hb, seq, tm, name):
    rope = cos is not None
    rb0 = row_off // tm
    pb = seq // tm if rope else 1
    bw = hb * HD

    def body(*refs):
        if rope:
            d_ref, x_ref, w_ref, c_ref, s_ref, dx_ref, dw_ref = refs
        else:
            d_ref, x_ref, w_ref, dx_ref, dw_ref = refs
        first = jnp.logical_and(pl.program_id(0) == 0, pl.program_id(1) == 0)
        dw = jnp.zeros((1, HD), F32)
        for h in range(hb):
            sl = slice(h * HD, (h + 1) * HD)
            dtv = d_ref[:, sl]
            if rope:
                dtv = dtv * c_ref[...] + _swap_pairs(dtv * s_ref[...])
            xv = x_ref[:, sl].astype(F32)
            r = lax.rsqrt(jnp.mean(xv * xv, axis=-1, keepdims=True) + EPS)
            xh = xv * r
            dxh = dtv * w_ref[...]
            dx_ref[:, sl] = (r * (dxh - xh * jnp.mean(dxh * xh, axis=-1, keepdims=True))).astype(BF16)
            dw += jnp.sum(dtv * xh, axis=0, keepdims=True)

        @pl.when(first)
        def _():
            dw_ref[...] = dw

        @pl.when(jnp.logical_not(first))
        def _():
            dw_ref[...] += dw

    blk = pl.BlockSpec((tm, bw), lambda i, j: (i, j))
    in_specs = [blk, pl.BlockSpec((tm, bw), lambda i, j: (rb0 + i, col_off // bw + j)),
                pl.BlockSpec((1, HD), lambda i, j: (0, 0))]
    args = [dt, px, nw]
    if rope:
        in_specs += [pl.BlockSpec((tm, HD), lambda i, j: (i % pb, 0))] * 2
        args += [cos, sin]
    return pl.pallas_call(
        body, name=name, grid=(rows // tm, heads // hb), in_specs=in_specs,
        out_specs=[blk, pl.BlockSpec((1, HD), lambda i, j: (0, 0))],
        out_shape=[_sds((rows, heads * HD), BF16), _sds((1, HD), F32)],
        compiler_params=_params(("arbitrary", "arbitrary")),
    )(*args)


def _ret_bwd(px, lg, do16, hist_f, hist_b, nb, nc):
    t_rows = nb * nc * CH

    def body(lg_ref, *refs):
        ins = (refs[0:5], refs[7:12])
        do_refs = (refs[5], refs[12])
        h_refs = (refs[6], refs[13])
        outs = (refs[14:17], refs[17:20])
        ds_outs = (refs[20], refs[21])
        dlg_ref = refs[22]
        dss = (refs[23], refs[24])
        c = pl.program_id(1)

        @pl.when(c == 0)
        def _():
            dss[0][...] = jnp.zeros_like(dss[0])
            dss[1][...] = jnp.zeros_like(dss[1])
            dlg_ref[...] = jnp.zeros_like(dlg_ref)

        for d in range(2):
            dq_ref, dk_ref, dv_ref = outs[d]
            for h in range(RH):
                lg_d = lg_ref[d, h]
                mask, relf, qd, qe, kd, ke = _decays(lg_d, d == 0)
                g_ch = jnp.exp(lg_d * CH)
                q, k, v16 = _head_qkv(ins[d], h)
                q16 = q.astype(BF16)
                k16 = k.astype(BF16)
                do16v = do_refs[d][:, h * DV:(h + 1) * DV]
                st16 = h_refs[d][h]
                dst = dss[d][h]
                dst16 = dst.astype(BF16)
                a = _dot(q16, k16, NT) * mask
                dp = _dot(do16v, v16, NT)
                da16 = (dp * mask).astype(BF16)
                dq_cross = _dot(do16v, st16, NT) * qd
                dq_ref[:, h * DK:(h + 1) * DK] = (_dot(da16, k16) + dq_cross).astype(BF16)
                dk_state = _dot(v16, dst16, NT) * kd
                dk_ref[:, h * DK:(h + 1) * DK] = ((_dot(da16, q16, TN) + dk_state) * (DK ** -0.5)).astype(BF16)
                dv = _dot(a.astype(BF16), do16v, TN) + _dot((k * kd).astype(BF16), dst16)
                dv_ref[:, h * DV:(h + 1) * DV] = dv.astype(BF16)
                dlg = (jnp.sum(relf * a * dp)
                       + jnp.sum(qe * jnp.sum(q * dq_cross, axis=-1, keepdims=True))
                       + jnp.sum(ke * jnp.sum(k * dk_state, axis=-1, keepdims=True))
                       + CH * g_ch * jnp.sum(dst * st16.astype(F32)))
                row = d * RH + h
                dlg_ref[row:row + 1, :] += jnp.broadcast_to(dlg, (1, 128))
                ds_new = g_ch * dst + _dot((q * qd).astype(BF16), do16v, TN)
                dss[d][h] = ds_new

                @pl.when(c == nc - 1)
                def _():
                    ds_outs[d][h] = ds_new

    def fw(b, c):
        return b * nc + nc - 1 - c

    def bw(b, c):
        return b * nc + c

    def rows(rowf, width):
        return pl.BlockSpec((CH, width), lambda b, c: (rowf(b, c), 0))

    def hist(rowf):
        return pl.BlockSpec((None, None, RH, DK, DV), lambda b, c: (b, rowf(0, c), 0, 0, 0))

    st = pl.BlockSpec((None, RH, DK, DV), lambda b, c: (b, 0, 0, 0))
    in_specs = [pl.BlockSpec(memory_space=pltpu.SMEM)]
    out_specs = []
    for rowf in (fw, bw):
        in_specs += _wide_specs(rowf) + [rows(rowf, RH * DV), hist(rowf)]
        out_specs += [rows(rowf, RH * DK), rows(rowf, RH * DK), rows(rowf, RH * DV)]
    out_specs += [st, st, pl.BlockSpec((None, 8, 128), lambda b, c: (b, 0, 0))]
    qk = _sds((t_rows, RH * DK), BF16)
    vv = _sds((t_rows, RH * DV), BF16)
    return pl.pallas_call(
        body, name="ret_bwd", grid=(nb, nc), in_specs=in_specs, out_specs=out_specs,
        out_shape=[qk, qk, vv, qk, qk, vv, _sds((nb, RH, DK, DV), F32), _sds((nb, RH, DK, DV), F32),
                   _sds((nb, 8, 128), F32)],
        scratch_shapes=[pltpu.VMEM((RH, DK, DV), F32), pltpu.VMEM((RH, DK, DV), F32)],
        compiler_params=_params(("parallel", "arbitrary")),
    )(lg, *([px] * 5), do16, hist_f, *([px] * 5), do16, hist_b)


def _ctx_state_bwd(px, lg, ds_f, ds_b, nb, t_rows, cx):
    rb = t_rows // cx

    def body(lg_ref, k_ref, v_ref, dsf_ref, dsb_ref, dk_ref, dv_ref, dlg_ref):
        h = pl.program_id(1)
        pos = lax.broadcasted_iota(jnp.int32, (cx, 1), 0).astype(F32)
        k = k_ref[...].astype(F32) * (DK ** -0.5)
        v16 = v_ref[...].astype(BF16)
        dk = jnp.zeros((cx, DK), F32)
        dv = jnp.zeros((cx, DV), F32)
        dlg_ref[...] = jnp.zeros_like(dlg_ref)
        for d, (ds_ref, e) in enumerate(((dsf_ref, cx - 1.0 - pos), (dsb_ref, pos))):
            w = jnp.exp(lg_ref[d, h] * e)
            ds16 = ds_ref[...].astype(BF16)
            t = _dot(v16, ds16, NT)
            dk += t * w
            dv += _dot((k * w).astype(BF16), ds16)
            dlg = jnp.sum(e * w * jnp.sum(k * t, axis=-1, keepdims=True))
            dlg_ref[d:d + 1, :] = jnp.broadcast_to(dlg, (1, 128))
        dk_ref[...] = (dk * (DK ** -0.5)).astype(BF16)
        dv_ref[...] = dv.astype(BF16)

    st = pl.BlockSpec((None, None, DK, DV), lambda b, h: (b, h, 0, 0))
    return pl.pallas_call(
        body, name="ctx_state_bwd", grid=(nb, RH),
        in_specs=[pl.BlockSpec(memory_space=pltpu.SMEM),
                  pl.BlockSpec((cx, DK), lambda b, h: (rb + b, RK // DK + h)),
                  pl.BlockSpec((cx, DV), lambda b, h: (rb + b, RV // DV + h)), st, st],
        out_specs=[pl.BlockSpec((cx, DK), lambda b, h: (b, h)), pl.BlockSpec((cx, DV), lambda b, h: (b, h)),
                   pl.BlockSpec((None, None, 8, 128), lambda b, h: (b, h, 0, 0))],
        out_shape=[_sds((nb * cx, RH * DK), BF16), _sds((nb * cx, RH * DV), BF16), _sds((nb, RH, 8, 128), F32)],
        compiler_params=_params(("parallel", "parallel")),
    )(lg, px, px, ds_f, ds_b)


def _assemble_lat(rows_all, dk_f, dk_b, dv_f, dv_b, dak16, dvx, dq_f, dq_b, drg16, daq16, dag16, dmr16, dma16, tm):
    t_rows = dk_f.shape[0]

    def body(dkf, dkb, dvf, dvb, dak, dav, dqf, dqb, drg, daq, dag, dmr, dma, o_ref):
        o_ref[:, RK:RK + RH * DK] = (dkf[...].astype(F32) + dkb[...].astype(F32)).astype(BF16)
        o_ref[:, RV:RV + RH * DV] = (dvf[...].astype(F32) + dvb[...].astype(F32)).astype(BF16)
        o_ref[:, AK:AK + HKV * HD] = dak[...]
        o_ref[:, AV:AV + HKV * HD] = dav[...].astype(BF16)
        o_ref[:, RQ:RQ + RH * DK] = (dqf[...].astype(F32) + dqb[...].astype(F32)).astype(BF16)
        o_ref[:, RG:RG + RH * DV] = drg[...]
        o_ref[:, AQ:AQ + D] = daq[...]
        o_ref[:, AG:AG + D] = dag[...]
        o_ref[:, MR:MR + D] = dmr[...]
        o_ref[:, MA:MA + D] = dma[...]

    args = (dk_f, dk_b, dv_f, dv_b, dak16, dvx, dq_f, dq_b, drg16, daq16, dag16, dmr16, dma16)
    return pl.pallas_call(
        body, name="assemble_lat", grid=(t_rows // tm,),
        in_specs=[pl.BlockSpec((tm, a.shape[1]), lambda i: (i, 0)) for a in args],
        out_specs=pl.BlockSpec((tm, IN_COLS), lambda i: (i, 0)), out_shape=_sds((rows_all, IN_COLS), BF16),
        compiler_params=_params(("parallel",)),
    )(*args)


def _assemble_ctx(dp_all, dck16, dcv16, dcak16, dvc, t_rows, tm):
    c_rows = dck16.shape[0]
    rb = t_rows // tm

    def body(_, dck, dcv, dcak, dcav, o_ref):
        o_ref[:, RK:RK + RH * DK] = dck[...]
        o_ref[:, RV:RV + RH * DV] = dcv[...]
        o_ref[:, AK:AK + HKV * HD] = dcak[...]
        o_ref[:, AV:AV + HKV * HD] = dcav[...].astype(BF16)
        o_ref[:, KV_COLS:] = jnp.zeros((tm, IN_COLS - KV_COLS), BF16)

    args = (dck16, dcv16, dcak16, dvc)
    return pl.pallas_call(
        body, name="assemble_ctx", grid=(c_rows // tm,),
        in_specs=[pl.BlockSpec(memory_space=pl.ANY)]
        + [pl.BlockSpec((tm, a.shape[1]), lambda i: (i, 0)) for a in args],
        out_specs=pl.BlockSpec((tm, IN_COLS), lambda i: (rb + i, 0)), out_shape=_sds(dp_all.shape, BF16),
        input_output_aliases={0: 0},
        compiler_params=_params(("parallel",)),
    )(dp_all, *args)


def _norm_bwd(dh, x2, mod3, norm_w, dxn, row_off, rows_per_group, group0, tm, name):
    with_dx = dxn is not None
    rows = x2.shape[0]
    rb0 = row_off // tm
    bpg = rows_per_group // tm
    ngroups = rows // rows_per_group

    def body(*refs):
        if with_dx:
            dh_ref, x_ref, sc_ref, nw_ref, dxn_ref, dx_ref, dsh_ref, dsc_ref, dnw_ref = refs
        else:
            dh_ref, x_ref, sc_ref, nw_ref, dsh_ref, dsc_ref, dnw_ref = refs
        i = pl.program_id(0)
        dhv = dh_ref[...]
        xv = x_ref[...]
        nw = nw_ref[...]
        r = lax.rsqrt(jnp.mean(xv * xv, axis=-1, keepdims=True) + EPS)
        xh = xv * r
        dm = dhv * (1.0 + sc_ref[...])
        dsh = jnp.sum(dhv, axis=0, keepdims=True)
        dsc = jnp.sum(dhv * (xh * nw), axis=0, keepdims=True)
        dnw = jnp.sum(dm * xh, axis=0, keepdims=True)
        if with_dx:
            dxh = dm * nw
            dx_ref[...] = dxn_ref[...] + r * (dxh - xh * jnp.mean(dxh * xh, axis=-1, keepdims=True))

        @pl.when(i % bpg == 0)
        def _():
            dsh_ref[...] = dsh
            dsc_ref[...] = dsc

        @pl.when(i % bpg != 0)
        def _():
            dsh_ref[...] += dsh
            dsc_ref[...] += dsc

        @pl.when(i == 0)
        def _():
            dnw_ref[...] = dnw

        @pl.when(i > 0)
        def _():
            dnw_ref[...] += dnw

    grp = pl.BlockSpec((None, 1, D), lambda i: (i // bpg, 0, 0))
    in_specs = [pl.BlockSpec((tm, D), lambda i: (rb0 + i, 0)), pl.BlockSpec((tm, D), lambda i: (i, 0)),
                pl.BlockSpec((None, 1, D), lambda i: (group0 + i // bpg, 0, 1)),
                pl.BlockSpec((1, D), lambda i: (0, 0))]
    args = [dh, x2, mod3, norm_w]
    out_specs = [grp, grp, pl.BlockSpec((1, D), lambda i: (0, 0))]
    out_shape = [_sds((ngroups, 1, D), F32), _sds((ngroups, 1, D), F32), _sds((1, D), F32)]
    if with_dx:
        in_specs.append(pl.BlockSpec((tm, D), lambda i: (i, 0)))
        args.append(dxn)
        out_specs.insert(0, pl.BlockSpec((tm, D), lambda i: (i, 0)))
        out_shape.insert(0, _sds((rows, D), F32))
    return pl.pallas_call(
        body, name=name, grid=(rows // tm,), in_specs=in_specs, out_specs=out_specs, out_shape=out_shape,
        compiler_params=_params(("arbitrary",)),
    )(*args)


def _small_final(dmod_all, dmodc_parts, c_rows, dm_loc_rows, nw_parts, misc_parts, c_ctx, r_pad, w_ada16):
    loc = dm_loc_rows.shape[1]

    def body(dm_ref, dmc_ref, c_ref, dml_ref, nwp_ref, mp_ref, cc_ref, r_ref, w_ref,
             gb_ref, gc_ref, gnw_ref, misc_ref, gwa_ref):
        dmc = jnp.sum(dmc_ref[...], axis=0, keepdims=True)
        gb_ref[...] = jnp.sum(dm_ref[...], axis=0, keepdims=True) + dmc
        dsc = _dot(jnp.broadcast_to(dmc, (8, 3 * D)).astype(BF16), w_ref[...], NT)[0:1, :]
        gc_ref[...] = dsc * _dsilu(cc_ref[...])
        gnw_ref[...] = jnp.sum(nwp_ref[...], axis=0, keepdims=True)
        misc = jnp.sum(mp_ref[...], axis=0, keepdims=True)
        y = jnp.exp2(r_ref[...])
        lane = lax.broadcasted_iota(jnp.int32, (1, D), 1)
        is_decay = jnp.logical_and(lane >= 2 * HD, lane < 2 * HD + 2 * RH)
        misc_ref[...] = misc * jnp.where(is_decay, -(y * np.float32(np.log(2.0))) / (1.0 - y), 1.0)
        gwa_ref[...] = _dot(_silu(c_ref[...]).astype(BF16), dml_ref[...].astype(BF16), TN)

    return pl.pallas_call(
        body, name="small_final",
        out_shape=[_sds((1, 3 * D), F32), _sds((1, D), F32), _sds((1, D), F32), _sds((1, D), F32), _sds((D, loc), F32)],
        compiler_params=pltpu.CompilerParams(vmem_limit_bytes=VMEM_LIMIT),
    )(dmod_all, dmodc_parts, c_rows, dm_loc_rows, nw_parts, misc_parts, c_ctx, r_pad, w_ada16)


def _adamw_math(w, g, m, v):
    nm = B1 * m + (1.0 - B1) * g
    nv = B2 * v + (1.0 - B2) * (g * g)
    return -LR * ((nm / (1.0 - B1 ** STEP)) / (jnp.sqrt(nv / (1.0 - B2 ** STEP)) + ADAM_EPS) + WD * w), nm, nv


def _adamw(w, g, m, v, name):
    rows, cols = w.shape
    tm = _pick(rows, 448, 8)

    def body(w_ref, g_ref, m_ref, v_ref, d_ref, nm_ref, nv_ref):
        d_ref[...], nm_ref[...], nv_ref[...] = _adamw_math(w_ref[...], g_ref[...], m_ref[...], v_ref[...])

    blk = pl.BlockSpec((tm, cols), lambda i: (i, 0))
    return pl.pallas_call(
        body, name=name, grid=(rows // tm,), in_specs=[blk] * 4, out_specs=[blk] * 3,
        out_shape=[_sds((rows, cols), F32)] * 3, compiler_params=_params(("parallel",)),
    )(w, g, m, v)


def _mesh_pos():
    return lax.axis_index("x"), lax.axis_index("y"), lax.axis_index("c")


def _all_gather(arrs, name):
    n = len(arrs)

    def body(*refs):
        ins, outs = refs[:n], refs[n:2 * n]
        send_sems, recv_sems, local_sems = refs[2 * n:]
        x, y, c = _mesh_pos()
        me, sib = (x, y, c), (x, y, 1 - c)
        chips = [(1 - x, y), (x, 1 - y), (1 - x, 1 - y)]

        def slot(p):
            return 4 * p[0] + 2 * p[1] + p[2]

        def copy(a, k, block, to, own):
            dst = outs[a].at[slot(block)]
            return pltpu.make_async_remote_copy(
                src_ref=ins[a] if own else dst, dst_ref=dst, send_sem=send_sems.at[a, k], recv_sem=recv_sems.at[a, k],
                device_id=to, device_id_type=MESH_T)

        mine = [pltpu.make_async_copy(ins[a], outs[a].at[slot(me)], local_sems.at[a]) for a in range(n)]
        for cp in mine:
            cp.start()
        first = []
        for a in range(n):
            first.append(copy(a, 0, me, sib, True))
            first += [copy(a, 1 + j, me, (*chip, c), True) for j, chip in enumerate(chips)]
        for cp in first:
            cp.start()
        passed = []
        for j, chip in enumerate(chips):
            for a in range(n):
                copy(a, 1 + j, (*chip, c), me, False).wait_recv()
                fwd = copy(a, 4 + j, (*chip, c), sib, False)
                fwd.start()
                passed.append(fwd)
        for a in range(n):
            copy(a, 0, sib, me, False).wait_recv()
            for j, chip in enumerate(chips):
                copy(a, 4 + j, (*chip, 1 - c), me, False).wait_recv()
        for cp in first + passed:
            cp.wait_send()
        for cp in mine:
            cp.wait()

    hbm = pl.BlockSpec(memory_space=pl.ANY)
    return pl.pallas_call(
        body, name=name, in_specs=[hbm] * n, out_specs=[hbm] * n,
        out_shape=[_sds((N_DEV,) + a.shape, a.dtype) for a in arrs],
        scratch_shapes=[pltpu.SemaphoreType.DMA((n, 7)), pltpu.SemaphoreType.DMA((n, 7)), pltpu.SemaphoreType.DMA((n,))],
    )(*arrs)


def _pair_exchange(arrs, name):
    n = len(arrs)

    def body(*refs):
        ins, outs = refs[:n], refs[n:2 * n]
        send_sems, recv_sems = refs[2 * n:]
        x, y, c = _mesh_pos()
        sib = (x, y, 1 - c)
        sends = []
        for a in range(n):
            for k in range(4):
                sends.append(pltpu.make_async_remote_copy(
                    src_ref=ins[a].at[2 * k + 1 - c], dst_ref=outs[a].at[k], send_sem=send_sems.at[a, k],
                    recv_sem=recv_sems.at[a, k], device_id=sib, device_id_type=MESH_T))
        for cp in sends:
            cp.start()
        for cp in sends:
            cp.wait_recv()
        for cp in sends:
            cp.wait_send()

    hbm = pl.BlockSpec(memory_space=pl.ANY)
    return pl.pallas_call(
        body, name=name, in_specs=[hbm] * n, out_specs=[hbm] * n,
        out_shape=[_sds((4,) + a.shape[1:], a.dtype) for a in arrs],
        scratch_shapes=[pltpu.SemaphoreType.DMA((n, 4)), pltpu.SemaphoreType.DMA((n, 4))],
    )(*arrs)


def _pair_add(part, got, core, name):
    _, rows, cols = part.shape
    tm = _pick(rows, 672, 16)
    p4 = part.reshape(4, 2, rows, cols)

    def body(core_ref, p_ref, g_ref, o_ref):
        o_ref[...] = (p_ref[...].astype(F32) + g_ref[...].astype(F32)).astype(BF16)

    blk = pl.BlockSpec((None, tm, cols), lambda k, i, cr: (k, i, 0))
    return pl.pallas_call(
        body, name=name,
        grid_spec=pltpu.PrefetchScalarGridSpec(
            num_scalar_prefetch=1, grid=(4, rows // tm),
            in_specs=[pl.BlockSpec((None, None, tm, cols), lambda k, i, cr: (k, cr[0], i, 0)), blk], out_specs=blk),
        out_shape=_sds((4, rows, cols), BF16), compiler_params=_params(("parallel", "parallel")),
    )(core, p4, got)


def _chip_sum_adamw(pair_sums, landed, chip, w, m, v, name):
    _, rows, cols = pair_sums.shape
    tm = _pick(rows, 448, 16)

    def body(chip_ref, s_ref, l_ref, w_ref, m_ref, v_ref, g_ref, d_ref, nm_ref, nv_ref):
        acc = s_ref[...].astype(F32)
        for j in range(3):
            acc = acc + l_ref[j].astype(F32)
        g_ref[...] = acc
        d_ref[...], nm_ref[...], nv_ref[...] = _adamw_math(w_ref[...], acc, m_ref[...], v_ref[...])

    blk = pl.BlockSpec((tm, cols), lambda i, ch: (i, 0))
    return pl.pallas_call(
        body, name=name,
        grid_spec=pltpu.PrefetchScalarGridSpec(
            num_scalar_prefetch=1, grid=(rows // tm,),
            in_specs=[pl.BlockSpec((None, tm, cols), lambda i, ch: (ch[0], i, 0)),
                      pl.BlockSpec((3, tm, cols), lambda i, ch: (0, i, 0)), blk, blk, blk],
            out_specs=[blk] * 4),
        out_shape=[_sds((rows, cols), F32)] * 4, compiler_params=_params(("parallel",)),
    )(chip, pair_sums, landed, w, m, v)


_HBM = pl.BlockSpec(memory_space=pltpu.HBM)
_SEM = pl.BlockSpec(memory_space=pltpu.SEMAPHORE)
_EFFECT = pltpu.SideEffectType.DATAFLOW_SIDE_EFFECTING


def _chip_routes(n):
    def plan(x, y, c):
        routes = []
        for a in range(n):
            for j in range(1, 4):
                px, py = x ^ (j >> 1), y ^ (j & 1)
                routes.append((a, 2 * px + py, (px, py, c), j - 1))
        return routes
    return plan, 3 * n


def _bcast_routes(n):
    def plan(x, y, c):
        routes = []
        for a in range(n):
            for k in range(1, N_DEV):
                peer = (x ^ ((k >> 2) & 1), y ^ ((k >> 1) & 1), c ^ (k & 1))
                routes.append((a, 0, peer, 4 * x + 2 * y + c))
        return routes
    return plan, 7 * n


def _route_copies(srcs, lands, send_sems, recv_sems, routes):
    return [pltpu.make_async_remote_copy(
        src_ref=srcs[a].at[sb], dst_ref=lands[a].at[lb], send_sem=send_sems.at[r], recv_sem=recv_sems.at[r],
        device_id=peer, device_id_type=MESH_T) for r, (a, sb, peer, lb) in enumerate(routes)]


def _exchange_start(srcs, lands, routes, name, after=()):
    plan, count = routes
    n = len(srcs)
    n_in = 2 * n + len(after)

    def body(*refs):
        send_sems, recv_sems = refs[n_in], refs[n_in + 1]
        token = refs[-1]
        for cp in _route_copies(refs[:n], refs[n:2 * n], send_sems, recv_sems, plan(*_mesh_pos())):
            cp.start()
        token[...] = jnp.zeros_like(token)

    args = [pltpu.with_memory_space_constraint(a, pltpu.HBM) for a in list(srcs) + list(lands)]
    out = pl.pallas_call(
        body, name=name,
        out_shape=(pltpu.SemaphoreType.DMA((count,)), pltpu.SemaphoreType.DMA((count,)),
                   *[pltpu.HBM(a.shape, a.dtype) for a in args], _sds((8, 128), F32)),
        in_specs=[_HBM] * (2 * n) + [pl.BlockSpec(memory_space=pl.ANY)] * len(after),
        out_specs=(_SEM, _SEM, *([_HBM] * (2 * n)), pl.BlockSpec(memory_space=pltpu.VMEM)),
        input_output_aliases={i: 2 + i for i in range(2 * n)},
        compiler_params=pltpu.CompilerParams(has_side_effects=_EFFECT),
    )(*args, *after)
    return (out[0], out[1], list(out[2:2 + 2 * n]), routes), out[-1]


def _exchange_wait(state, after, name):
    send_sems, recv_sems, bufs, (plan, count) = state
    n = len(bufs) // 2

    def body(*refs):
        send_s, recv_s = refs[2 * n], refs[2 * n + 1]
        for cp in _route_copies(refs[:n], refs[n:2 * n], send_s, recv_s, plan(*_mesh_pos())):
            cp.wait_send()
            cp.wait_recv()

    out = pl.pallas_call(
        body, name=name, out_shape=tuple(pltpu.HBM(a.shape, a.dtype) for a in bufs),
        in_specs=[_HBM] * (2 * n) + [_SEM, _SEM, pl.BlockSpec(memory_space=pl.ANY)], out_specs=tuple([_HBM] * (2 * n)),
        input_output_aliases={i: i for i in range(2 * n)},
        compiler_params=pltpu.CompilerParams(has_side_effects=_EFFECT),
    )(*bufs, send_sems, recv_sems, after)
    return list(out[:n]), list(out[n:])


def _group_routes(js):
    def plan(x, y, c):
        return [(0, 0, (x ^ (j >> 1), y ^ (j & 1), c), 2 * j + c) for j in js]
    return plan, len(js)


def _pair_fill(groups, js, name, after=()):
    def body(*refs):
        g_ref, send_sems, recv_sems = refs[-3:]
        x, y, c = _mesh_pos()
        sends = []
        for n, j in enumerate(js):
            mine = g_ref.at[2 * j + c]
            sends.append(pltpu.make_async_remote_copy(
                src_ref=mine, dst_ref=mine, send_sem=send_sems.at[n], recv_sem=recv_sems.at[n],
                device_id=(x, y, 1 - c), device_id_type=MESH_T))
        for cp in sends:
            cp.start()
        for n, j in enumerate(js):
            pltpu.make_async_remote_copy(
                src_ref=g_ref.at[2 * j + c], dst_ref=g_ref.at[2 * j + 1 - c], send_sem=send_sems.at[n],
                recv_sem=recv_sems.at[n], device_id=(x, y, 1 - c), device_id_type=MESH_T).wait_recv()
        for cp in sends:
            cp.wait_send()

    hbm = pl.BlockSpec(memory_space=pl.ANY)
    return pl.pallas_call(
        body, name=name, in_specs=[hbm] * (1 + len(after)), out_specs=hbm, out_shape=_sds(groups.shape, groups.dtype),
        input_output_aliases={0: 0},
        scratch_shapes=[pltpu.SemaphoreType.DMA((len(js),)), pltpu.SemaphoreType.DMA((len(js),))],
    )(groups, *after)


def _in_proj_group(h_all, groups, j0, ng, chip, px_prev, after, name):
    rows_all = h_all.shape[0]
    gcols = IN_COLS // 4
    tm = _pick(rows_all, 1536, 128)
    g4 = groups.reshape(4, gcols, D)

    n_lead = (1 if px_prev is not None else 0) + len(after)
    lead = ([px_prev] if px_prev is not None else []) + list(after)

    def body(chip_ref, *refs):
        h_ref, w_ref, o_ref = refs[n_lead:]
        o_ref[...] = _dot(h_ref[...], w_ref[...], NT).astype(BF16)

    return pl.pallas_call(
        body, name=name,
        grid_spec=pltpu.PrefetchScalarGridSpec(
            num_scalar_prefetch=1, grid=(ng, rows_all // tm),
            in_specs=[pl.BlockSpec(memory_space=pl.ANY)] * n_lead
            + [pl.BlockSpec((tm, D), lambda n, i, ch: (i, 0)),
               pl.BlockSpec((None, gcols, D), lambda n, i, ch: (j0 + n, 0, 0))],
            out_specs=pl.BlockSpec((tm, gcols), lambda n, i, ch: (i, ch[0] ^ (j0 + n)))),
        out_shape=_sds((rows_all, IN_COLS), BF16),
        input_output_aliases={1: 0} if px_prev is not None else {},
        compiler_params=_params(("parallel", "parallel")),
    )(chip, *lead, h_all, g4)


def _d_h_groups(dp_all, groups, chip, after):
    rows_all = dp_all.shape[0]
    gcols = IN_COLS // 4
    tm = _pick(rows_all, 1536, 128)
    g4 = groups.reshape(4, gcols, D)
    n_lead = len(after)

    def body(chip_ref, *refs):
        a_ref, w_ref, o_ref = refs[n_lead:]
        j = pl.program_id(1)
        part = _dot(a_ref[...], w_ref[...])

        @pl.when(j == 0)
        def _():
            o_ref[...] = part

        @pl.when(j > 0)
        def _():
            o_ref[...] += part

    return pl.pallas_call(
        body, name="d_h",
        grid_spec=pltpu.PrefetchScalarGridSpec(
            num_scalar_prefetch=1, grid=(rows_all // tm, 4),
            in_specs=[pl.BlockSpec(memory_space=pl.ANY)] * n_lead
            + [pl.BlockSpec((tm, gcols), lambda i, j, ch: (i, ch[0] ^ j)),
               pl.BlockSpec((None, gcols, D), lambda i, j, ch: (j, 0, 0))],
            out_specs=pl.BlockSpec((tm, D), lambda i, j, ch: (i, 0))),
        out_shape=_sds((rows_all, D), F32),
        compiler_params=_params(("parallel", "arbitrary")),
    )(chip, *after, dp_all, g4)


def _reduce_scatter_start(parts, core, name):
    got = _pair_exchange(parts, name + "_pair")
    sums = [_pair_add(p, g, core, "%s_add_%d" % (name, i)) for i, (p, g) in enumerate(zip(parts, got))]
    lands = [lax.empty((3,) + s_.shape[1:], BF16) for s_ in sums]
    return _exchange_start(sums, lands, _chip_routes(len(sums)), name + "_start")


def _reduce_scatter_finish(rs_state, after, chip, wmv, name):
    sums, landed = _exchange_wait(rs_state, after, name + "_wait")
    return [_chip_sum_adamw(s_, l_, chip, *t, "%s_adamw_%d" % (name, i))
            for i, (s_, l_, t) in enumerate(zip(sums, landed, wmv))]


def _local_step(x, c, ctx, norm_w, ret_log2_decay, q_norm_w, k_norm_w, loss_target,
                mod, proj_in, proj_back, get_w_o, on_out_grads, on_in_grad, started=()):
    nb, seq, _ = x.shape
    cx = ctx.shape[1]
    t_rows, c_rows = nb * seq, nb * cx
    rows_all = t_rows + c_rows
    nc = seq // CH
    tm = _pick(seq, 256, 128)
    te = _pick(seq, 512, 128)
    assert cx % tm == 0 and t_rows % cx == 0 and seq % GRID_W == 0

    x2 = x.reshape(t_rows, D)
    ctx2 = ctx.reshape(c_rows, D)
    tgt = loss_target.reshape(t_rows, D)
    lg = _log_gamma(ret_log2_decay)
    cos, sin = _rope_tables(seq)

    mod3 = mod[:, None, :]
    h_all = _norm_fwd(x2, mod3, norm_w, rows_all, 0, seq, 0, None, te, "norm_fwd", after=started)
    h_all = _norm_fwd(ctx2, mod3, norm_w, rows_all, t_rows, c_rows, nb, h_all, tm, "norm_fwd_ctx")
    px = proj_in(h_all)
    s0f, s0b = _ctx_state(px, lg, nb, t_rows, cx)
    o_f, o_b, hist_f, hist_b = _ret_fwd(px, lg, s0f, s0b, nb, nc)
    yret16 = _ret_post(o_f, o_b, px, te)
    q16 = _qk_prep(px, q_norm_w, cos, sin, t_rows, 0, AQ, HQ, 4, seq, te, "q_prep")
    kx16 = _qk_prep(px, k_norm_w, cos, sin, t_rows, 0, AK, HKV, HKV, seq, te, "k_prep")
    kc16 = _qk_prep(px, k_norm_w, None, None, c_rows, t_rows, AK, HKV, HKV, seq, tm, "kc_prep")
    o_att, yatt16, lse = _att_fwd(q16, kx16, kc16, px, nb, seq, cx, te)
    w_o_ret16, w_o_att16, w_out16 = get_w_o(lse)
    a_ret, a_att, y16 = _merge(yret16, yatt16, px, w_o_ret16, w_o_att16, te)
    dxn, dout16, dgate, loss_b = _outproj(y16, w_out16, x2, tgt, mod3, nb, seq, te)

    gw_out = _matmul(y16, dout16, ta=True, tm=D, tn=D, tk=D, out_dtype=BF16, name="gw_out")
    da_ret16, da_att16, dmr16, dma16 = _bwd_merge(dout16, w_out16, px, a_ret, a_att, te)
    gw_o_ret = _matmul(yret16, da_ret16, ta=True, tm=D, tn=D, tk=D, out_dtype=BF16, name="gw_o_ret")
    gw_o_att = _matmul(yatt16, da_att16, ta=True, tm=D, tn=D, tk=D, out_dtype=BF16, name="gw_o_att")
    out_state, out_started = on_out_grads([gw_o_ret, gw_o_att, gw_out])
    do16, drg16 = _bwd_branch_ret(da_ret16, w_o_ret16, px, o_f, o_b, te, after=out_started)
    dao, dag16 = _bwd_branch_att(da_att16, w_o_att16, px, o_att, te)
    dq_rot, dkx, dvx, dkc, dvc = _att_bwd(q16, kx16, kc16, px, dao, o_att, lse, nb, seq, cx, te)
    daq16, gq = _qk_prep_bwd(dq_rot, px, q_norm_w, cos, sin, t_rows, 0, AQ, HQ, 4, seq, te, "q_prep_bwd")
    dak16, gk_lat = _qk_prep_bwd(dkx.reshape(t_rows, HKV * HD), px, k_norm_w, cos, sin, t_rows, 0, AK, HKV, HKV, seq, te,
                                 "k_prep_bwd")
    dcak16, gk_ctx = _qk_prep_bwd(dkc.reshape(c_rows, HKV * HD), px, k_norm_w, None, None, c_rows, t_rows, AK, HKV, HKV,
                                  seq, tm, "kc_prep_bwd")
    dq_f, dk_f, dv_f, dq_b, dk_b, dv_b, ds_f, ds_b, dlg_scan = _ret_bwd(px, lg, do16, hist_f, hist_b, nb, nc)
    dck16, dcv16, dlg_ctx = _ctx_state_bwd(px, lg, ds_f, ds_b, nb, t_rows, cx)
    dp_all = _assemble_lat(rows_all, dk_f, dk_b, dv_f, dv_b, dak16, dvx.reshape(t_rows, HKV * HD), dq_f, dq_b, drg16,
                           daq16, dag16, dmr16, dma16, tm)
    dp_all = _assemble_ctx(dp_all, dck16, dcv16, dcak16, dvc.reshape(c_rows, HKV * HD), t_rows, tm)
    gw_in_t = _matmul(dp_all, h_all, ta=True, tm=1536, tn=D, tk=2304, out_dtype=BF16, name="gw_in")
    in_state, in_started = on_in_grad(gw_in_t)
    dh = proj_back(dp_all, in_started)
    grad_x, dsh, dsc, gnw_lat = _norm_bwd(dh, x2, mod3, norm_w, dxn, 0, seq, 0, te, "norm_bwd")
    dsh_c, dsc_c, gnw_ctx = _norm_bwd(dh, ctx2, mod3, norm_w, None, t_rows, c_rows, nb, tm, "norm_bwd_ctx")

    dlg = (jnp.sum(dlg_scan[:, :, 0], axis=0) + jnp.sum(dlg_ctx[:, :, :2, 0], axis=0).T.reshape(2 * RH)).reshape(1, 2 * RH)
    misc = jnp.concatenate([gq, gk_lat + gk_ctx, dlg, jnp.sum(loss_b[:, 0, 0]).reshape(1, 1),
                            jnp.zeros((1, D - 2 * HD - 2 * RH - 1), F32)], axis=1)
    rows = []
    for b in range(nb):
        rows += [dsh[b], dsc[b], dgate[b]]
    rows += [dsh_c[0], dsc_c[0]] + [c[b:b + 1] for b in range(nb)] + [gnw_lat + gnw_ctx, misc]
    payload = jnp.concatenate(rows + [jnp.zeros((PAY_ROWS - len(rows), D), F32)], axis=0)
    return grad_x.reshape(nb, seq, D), out_state, in_state, payload


def _finish_small(gathered, nb, c_ctx, ret_log2_decay, w_ada16, dev):
    n_dev = gathered.shape[0]
    loc = 3 * D // n_dev
    dmod_all = gathered[:, :3 * nb].reshape(n_dev * nb, 3 * D)
    dmodc_parts = jnp.concatenate([gathered[:, 3 * nb:3 * nb + 2].reshape(n_dev, 2 * D), jnp.zeros((n_dev, D), F32)], axis=1)
    c_all = gathered[:, 3 * nb + 2:4 * nb + 2].reshape(n_dev * nb, D)
    nw_parts = gathered[:, 4 * nb + 2]
    misc_parts = gathered[:, 4 * nb + 3]
    n_rows = n_dev * nb + n_dev
    pad = (-n_rows) % 16
    c_rows = jnp.concatenate([c_all, jnp.broadcast_to(c_ctx.reshape(1, D), (n_dev, D)), jnp.zeros((pad, D), F32)], axis=0)
    dm_rows = jnp.concatenate([dmod_all, dmodc_parts, jnp.zeros((pad, 3 * D), F32)], axis=0)
    dm_loc_rows = lax.dynamic_slice_in_dim(dm_rows, dev * loc, loc, axis=1)
    r_pad = jnp.full((1, D), -1.0, F32).at[:, 2 * HD:2 * HD + 2 * RH].set(ret_log2_decay.reshape(1, 2 * RH))
    gb, gc, gnw, misc, gwa = _small_final(dmod_all, dmodc_parts, c_rows, dm_loc_rows, nw_parts, misc_parts,
                                          c_ctx.reshape(1, D), r_pad, w_ada16)
    return (gb, gc, gnw, misc[:, :HD], misc[:, HD:2 * HD], misc[:, 2 * HD:2 * HD + 2 * RH], gwa,
            misc[0, 2 * HD + 2 * RH])


def kernel(x, c, ctx, c_ctx, norm_w, w_ada, b_ada, w_in, ret_log2_decay, q_norm_w, k_norm_w, w_o_ret, w_o_att, w_out, loss_target, m_c_ctx, m_norm_w, m_w_ada, m_b_ada, m_w_in, m_ret_log2_decay, m_q_norm_w, m_k_norm_w, m_w_o_ret, m_w_o_att, m_w_out, v_c_ctx, v_norm_w, v_w_ada, v_b_ada, v_w_in, v_ret_log2_decay, v_q_norm_w, v_k_norm_w, v_w_o_ret, v_w_o_att, v_w_out):
    nb = x.shape[0]
    mx, my, mc = _mesh_pos()
    dev = 4 * mx + 2 * my + mc
    core = jnp.reshape(mc, (1,)).astype(jnp.int32)
    chip = jnp.reshape(2 * mx + my, (1,)).astype(jnp.int32)

    n_loc = 3 * D // N_DEV
    c8 = jnp.zeros((8, D), F32).at[:nb].set(c).at[nb].set(c_ctx)
    c_land = lax.dynamic_update_slice(lax.empty((N_DEV, 8, D), F32), c8[None], (dev, 0, 0))
    c_state, c_token = _exchange_start([c8[None]], [c_land], _bcast_routes(1), "gather_c_start")
    w_in_t = jnp.transpose(w_in[0])
    in_shard = w_in_t.astype(BF16)
    groups = lax.dynamic_update_slice(lax.empty((N_DEV,) + in_shard.shape, BF16), in_shard[None], (mc, 0, 0))
    groups = _pair_fill(groups, (0,), "gather_in_pair", after=(c_token,))
    _, (c_all,) = _exchange_wait(c_state, groups, "gather_c_wait")
    ada_shard = w_ada[0].astype(BF16)
    b_loc = lax.dynamic_slice(b_ada, (0, dev * n_loc), (1, n_loc))
    mod_cols = _mod_part(c_all.reshape(N_DEV * 8, D), ada_shard, b_loc)
    (mod_all,) = _all_gather([mod_cols], "gather_mod")
    mod = jnp.transpose(lax.dynamic_slice(mod_all, (0, dev * 8, 0), (N_DEV, 8, n_loc)), (1, 0, 2)).reshape(8, 3 * D)
    ada_land = lax.dynamic_update_slice(lax.empty((N_DEV,) + ada_shard.shape, BF16), ada_shard[None], (dev, 0, 0))

    (near_send, near_recv, near_bufs, near_routes), gin_token = _exchange_start(
        [in_shard[None]], [groups], _group_routes((1, 2)), "gather_in_start", after=(mod_all,))
    w_in_groups, wo_states, ada_states = [], [], []
    wo_shards = [w_[0].astype(BF16) for w_ in (w_o_ret, w_o_att, w_out)]
    wo_lands = [lax.dynamic_update_slice(lax.empty((N_DEV,) + s_.shape, BF16), s_[None], (dev, 0, 0)) for s_ in wo_shards]

    def _state(send, recv, src, groups, routes):
        return send, recv, [src, groups], routes

    def proj_in(h_all):
        src, groups = near_bufs
        px = _in_proj_group(h_all, groups, 0, 1, chip, None, (gin_token,), "in_proj_0")
        (src,), (groups,) = _exchange_wait(_state(near_send, near_recv, src, groups, near_routes), px,
                                           "gather_in_wait_near")
        groups = _pair_fill(groups, (1, 2), "gather_in_fill_near")
        (far_send, far_recv, (src, groups), far_routes), far_token = _exchange_start(
            [src], [groups], _group_routes((3,)), "gather_in_start_far")
        wo_state, wo_token = _exchange_start([s_[None] for s_ in wo_shards], wo_lands, _bcast_routes(3),
                                             "gather_wo_start", after=(far_token,))
        wo_states.append(wo_state)
        ada_state, ada_token = _exchange_start([ada_shard[None]], [ada_land], _bcast_routes(1), "gather_ada_start",
                                               after=(wo_token,))
        ada_states.append(ada_state)
        px = _in_proj_group(h_all, groups, 1, 2, chip, px, (ada_token,), "in_proj_near")
        (src,), (groups,) = _exchange_wait(_state(far_send, far_recv, src, groups, far_routes), px,
                                           "gather_in_wait_far")
        groups = _pair_fill(groups, (3,), "gather_in_fill_far")
        px = _in_proj_group(h_all, groups, 3, 1, chip, px, (), "in_proj_far")
        w_in_groups.append(groups)
        return px

    def proj_back(dp_all, after):
        return _d_h_groups(dp_all, w_in_groups[0], chip, after)

    def get_w_o(after):
        _, (l_ret, l_att, l_out) = _exchange_wait(wo_states[0], after, "gather_wo_wait")
        return l_ret.reshape(RH * DV, D), l_att.reshape(D, D), l_out.reshape(D, D)

    def on_out_grads(grads):
        parts = [g_.reshape(N_DEV, g_.shape[0] // N_DEV, D) for g_ in grads]
        state, token = _reduce_scatter_start(parts, core, "rs_out")
        return state, (token,)

    def on_in_grad(grad):
        state, token = _reduce_scatter_start([grad.reshape(N_DEV, IN_COLS // N_DEV, D)], core, "rs_in")
        return state, (token,)

    grad_x, out_state, in_state, payload = _local_step(
        x, c, ctx, norm_w, ret_log2_decay, q_norm_w, k_norm_w, loss_target,
        mod, proj_in, proj_back, get_w_o, on_out_grads, on_in_grad, started=(gin_token,))

    pay_land = lax.dynamic_update_slice(lax.empty((N_DEV,) + payload.shape, F32), payload[None], (dev, 0, 0))
    pay_state, pay_token = _exchange_start([payload[None]], [pay_land], _bcast_routes(1), "gather_small_start")

    out_res = _reduce_scatter_finish(out_state, pay_token, chip,
                                     [(w_[0], m_[0], v_[0]) for w_, m_, v_ in ((w_o_ret, m_w_o_ret, v_w_o_ret),
                                                                                (w_o_att, m_w_o_att, v_w_o_att),
                                                                                (w_out, m_w_out, v_w_out))], "rs_out")
    (in_res,) = _reduce_scatter_finish(in_state, out_res[0][0], chip,
                                       [(w_in_t, jnp.transpose(m_w_in[0]), jnp.transpose(v_w_in[0]))], "rs_in")

    _, (gathered,) = _exchange_wait(pay_state, in_res[0], "gather_small_wait")
    _, (l_ada,) = _exchange_wait(ada_states[0], gathered, "gather_ada_wait")
    w_ada16 = jnp.transpose(l_ada, (1, 0, 2)).reshape(D, 3 * D)
    gb, gc, gnw, gq, gk, gr, gwa, loss = _finish_small(gathered, nb, c_ctx, ret_log2_decay, w_ada16, dev)
    big = {4: [jnp.transpose(r)[None] for r in in_res]}
    for i, res in zip((8, 9, 10), out_res):
        big[i] = [r[None] for r in res]
    small_g = {0: gc.reshape(c_ctx.shape), 1: gnw, 2: gwa[None], 3: gb, 5: gr.reshape(ret_log2_decay.shape), 6: gq, 7: gk}
    weights = [c_ctx, norm_w, w_ada, b_ada, w_in, ret_log2_decay, q_norm_w, k_norm_w, w_o_ret, w_o_att, w_out]
    ms = [m_c_ctx, m_norm_w, m_w_ada, m_b_ada, m_w_in, m_ret_log2_decay, m_q_norm_w, m_k_norm_w, m_w_o_ret, m_w_o_att, m_w_out]
    vs = [v_c_ctx, v_norm_w, v_w_ada, v_b_ada, v_w_in, v_ret_log2_decay, v_q_norm_w, v_k_norm_w, v_w_o_ret, v_w_o_att, v_w_out]
    grads, deltas, new_ms, new_vs = [], [], [], []
    for i, (w, m, v) in enumerate(zip(weights, ms, vs)):
        if i in big:
            res = big[i]
        else:
            shape2 = (-1, w.shape[-1])
            g = small_g[i]
            res = [g] + [r.reshape(w.shape) for r in _adamw(w.reshape(shape2), g.reshape(shape2), m.reshape(shape2),
                                                             v.reshape(shape2), "adamw_%d" % i)]
        for lst, r in zip((grads, deltas, new_ms, new_vs), res):
            lst.append(r)
    return (loss, grad_x, *grads, *deltas, *new_ms, *new_vs)
```

```python
import numpy as np
import jax
import jax.numpy as jnp
from jax import lax
from jax.experimental import pallas as pl
from jax.experimental.pallas import tpu as pltpu

F32 = jnp.float32
BF16 = jnp.bfloat16

D = 1024
RH, DK, DV, CH = 4, 256, 512, 256
HQ, HKV, HD = 8, 2, 128
GRID_W = 64
ROPE_THETA = 10000.0
EPS = 1e-6
RK, RV, AK, AV, RQ, RG, AQ, AG, MR, MA = 0, 1024, 3072, 3328, 3584, 4608, 6656, 7680, 8704, 9728
IN_COLS = 10752
KV_COLS = 3584
N_DEV = 8
LR, B1, B2, ADAM_EPS, WD, STEP = 0.001, 0.9, 0.999, 1e-08, 0.01, 10
PAY_ROWS = 16
VMEM_LIMIT = 56 * 1024 * 1024
MESH_T = pl.DeviceIdType.MESH

NT = (((1,), (1,)), ((), ()))
TN = (((0,), (0,)), ((), ()))
SM_C = (HD ** -0.5) * float(np.log2(np.e))


def _params(sem):
    return pltpu.CompilerParams(dimension_semantics=sem, vmem_limit_bytes=VMEM_LIMIT)


def _pick(n, target, mult=8):
    best = None
    for t in range(mult, min(n, target) + 1, mult):
        if n % t == 0:
            best = t
    return best or n


def _dot(a, b, dn=None):
    if dn is None:
        return jnp.dot(a, b, preferred_element_type=F32)
    return lax.dot_general(a, b, dn, preferred_element_type=F32)


def _sig(v):
    return jax.nn.sigmoid(v)


def _silu(v):
    return v * _sig(v)


def _dsilu(v):
    s = _sig(v)
    return s * (1.0 + v * (1.0 - s))


def _sds(shape, dtype):
    return jax.ShapeDtypeStruct(shape, dtype)


def _matmul(a, b, *, ta=False, tb=False, tm, tn, tk, out_dtype, name, after=()):
    m = a.shape[1] if ta else a.shape[0]
    kdim = a.shape[0] if ta else a.shape[1]
    n = b.shape[0] if tb else b.shape[1]
    tm, tn, tk = _pick(m, tm, 128), _pick(n, tn, 128), _pick(kdim, tk, 128)
    nk = kdim // tk
    dn = (((0 if ta else 1,), (1 if tb else 0,)), ((), ()))

    def body(a_ref, b_ref, *rest):
        o_ref, acc_ref = rest[-2:]
        k = pl.program_id(2)
        part = _dot(a_ref[...].astype(BF16), b_ref[...].astype(BF16), dn)
        if nk == 1:
            o_ref[...] = part.astype(o_ref.dtype)
        else:
            @pl.when(k == 0)
            def _():
                acc_ref[...] = part

            @pl.when(k > 0)
            def _():
                acc_ref[...] += part

            @pl.when(k == nk - 1)
            def _():
                o_ref[...] = acc_ref[...].astype(o_ref.dtype)

    a_spec = pl.BlockSpec((tk, tm), lambda i, j, k: (k, i)) if ta else pl.BlockSpec((tm, tk), lambda i, j, k: (i, k))
    b_spec = pl.BlockSpec((tn, tk), lambda i, j, k: (j, k)) if tb else pl.BlockSpec((tk, tn), lambda i, j, k: (k, j))
    return pl.pallas_call(
        body, name=name, grid=(m // tm, n // tn, nk),
        in_specs=[a_spec, b_spec] + [pl.BlockSpec(memory_space=pl.ANY)] * len(after),
        out_specs=pl.BlockSpec((tm, tn), lambda i, j, k: (i, j)), out_shape=_sds((m, n), out_dtype),
        scratch_shapes=[pltpu.VMEM((tm, tn) if nk > 1 else (8, 128), F32)],
        compiler_params=_params(("parallel", "parallel", "arbitrary")),
    )(a, b, *after)


def _log_gamma(r):
    rp = jnp.full((8, 128), -1.0, F32).at[:2, :RH].set(r.reshape(2, RH))

    def body(r_ref, o_ref):
        o_ref[...] = jnp.log1p(-jnp.exp2(r_ref[...]))

    out = pl.pallas_call(body, name="log_gamma", out_shape=_sds((8, 128), F32))(rp)
    return out[:2, :RH]


def _mod_part(c_rows, w_ada_loc16, b_loc):
    def body(c_ref, w_ref, b_ref, o_ref):
        o_ref[...] = _dot(_silu(c_ref[...]).astype(BF16), w_ref[...]) + b_ref[...]

    return pl.pallas_call(
        body, name="mod_part", out_shape=_sds((c_rows.shape[0], w_ada_loc16.shape[1]), F32),
    )(c_rows, w_ada_loc16, b_loc)


def _norm_fwd(x2, mod3, norm_w, rows_all, row_off, rows_per_group, group0, h_prev, tm, name, after=()):
    rows = x2.shape[0]
    rb0 = row_off // tm
    bpg = rows_per_group // tm

    def body(*refs):
        x_ref, sh_ref, sc_ref, nw_ref, o_ref = refs[-5:]
        xv = x_ref[...]
        r = lax.rsqrt(jnp.mean(xv * xv, axis=-1, keepdims=True) + EPS)
        o_ref[...] = ((xv * r) * nw_ref[...] * (1.0 + sc_ref[...]) + sh_ref[...]).astype(BF16)

    in_specs = [pl.BlockSpec((tm, D), lambda i: (i, 0)),
                pl.BlockSpec((None, 1, D), lambda i: (group0 + i // bpg, 0, 0)),
                pl.BlockSpec((None, 1, D), lambda i: (group0 + i // bpg, 0, 1)),
                pl.BlockSpec((1, D), lambda i: (0, 0))]
    in_specs = [pl.BlockSpec(memory_space=pl.ANY)] * len(after) + in_specs
    args = list(after) + [x2, mod3, mod3, norm_w]
    alias = {}
    if h_prev is not None:
        in_specs.insert(0, pl.BlockSpec(memory_space=pl.ANY))
        args.insert(0, h_prev)
        alias = {0: 0}
    return pl.pallas_call(
        body, name=name, grid=(rows // tm,), in_specs=in_specs,
        out_specs=pl.BlockSpec((tm, D), lambda i: (rb0 + i, 0)), out_shape=_sds((rows_all, D), BF16),
        input_output_aliases=alias, compiler_params=_params(("parallel",)),
    )(*args)


def _decays(lg, fwd):
    ii = lax.broadcasted_iota(jnp.int32, (CH, CH), 0)
    jj = lax.broadcasted_iota(jnp.int32, (CH, CH), 1)
    ri = lax.broadcasted_iota(jnp.int32, (CH, 1), 0).astype(F32)
    rel = (ii - jj) if fwd else (jj - ii)
    relf = jnp.maximum(rel, 0).astype(F32)
    mask = jnp.where(rel >= 0, jnp.exp(lg * relf), 0.0)
    qe = (ri + 1.0) if fwd else (CH - ri)
    ke = (CH - 1.0 - ri) if fwd else ri
    return mask, relf, jnp.exp(lg * qe), qe, jnp.exp(lg * ke), ke


def _wide_specs(rowf):
    return [pl.BlockSpec((CH, 2 * DK), lambda b, c: (rowf(b, c), RQ // (2 * DK))),
            pl.BlockSpec((CH, 2 * DK), lambda b, c: (rowf(b, c), RQ // (2 * DK) + 1)),
            pl.BlockSpec((CH, RH * DK), lambda b, c: (rowf(b, c), RK // (RH * DK))),
            pl.BlockSpec((CH, 2 * DV), lambda b, c: (rowf(b, c), RV // (2 * DV))),
            pl.BlockSpec((CH, 2 * DV), lambda b, c: (rowf(b, c), RV // (2 * DV) + 1))]


def _head_qkv(refs, h):
    q0, q1, k, v0, v1 = refs
    lo = h % 2
    q = (q0, q1)[h // 2][:, lo * DK:(lo + 1) * DK].astype(F32)
    kk = k[:, h * DK:(h + 1) * DK].astype(F32) * (DK ** -0.5)
    v16 = (v0, v1)[h // 2][:, lo * DV:(lo + 1) * DV].astype(BF16)
    return q, kk, v16


def _ctx_state(px, lg, nb, t_rows, cx):
    rb = t_rows // cx

    def body(lg_ref, k_ref, v_ref, sf_ref, sb_ref):
        h = pl.program_id(1)
        pos = lax.broadcasted_iota(jnp.int32, (cx, 1), 0).astype(F32)
        k = k_ref[...].astype(F32) * (DK ** -0.5)
        v16 = v_ref[...].astype(BF16)
        wf = jnp.exp(lg_ref[0, h] * (cx - 1.0 - pos))
        wb = jnp.exp(lg_ref[1, h] * pos)
        sf_ref[...] = _dot((k * wf).astype(BF16), v16, TN)
        sb_ref[...] = _dot((k * wb).astype(BF16), v16, TN)

    st = pl.BlockSpec((None, None, DK, DV), lambda b, h: (b, h, 0, 0))
    return pl.pallas_call(
        body, name="ctx_state", grid=(nb, RH),
        in_specs=[pl.BlockSpec(memory_space=pltpu.SMEM),
                  pl.BlockSpec((cx, DK), lambda b, h: (rb + b, RK // DK + h)),
                  pl.BlockSpec((cx, DV), lambda b, h: (rb + b, RV // DV + h))],
        out_specs=[st, st], out_shape=[_sds((nb, RH, DK, DV), F32)] * 2,
        compiler_params=_params(("parallel", "parallel")),
    )(lg, px, px)


def _ret_fwd(px, lg, s0f, s0b, nb, nc):
    t_rows = nb * nc * CH

    def body(lg_ref, *refs):
        ins = (refs[0:5], refs[5:10])
        s0f_ref, s0b_ref, of_ref, ob_ref, hf_ref, hb_ref, sf, sb = refs[10:]
        c = pl.program_id(1)

        @pl.when(c == 0)
        def _():
            sf[...] = s0f_ref[...]
            sb[...] = s0b_ref[...]

        for d, (o_ref, h_ref, s) in enumerate(((of_ref, hf_ref, sf), (ob_ref, hb_ref, sb))):
            for h in range(RH):
                lg_d = lg_ref[d, h]
                mask, _, qd, _, kd, _ = _decays(lg_d, d == 0)
                q, k, v16 = _head_qkv(ins[d], h)
                a = _dot(q.astype(BF16), k.astype(BF16), NT)
                st = s[h]
                st16 = st.astype(BF16)
                h_ref[h] = st16
                o = _dot((a * mask).astype(BF16), v16) + _dot((q * qd).astype(BF16), st16)
                o_ref[:, h * DV:(h + 1) * DV] = o.astype(BF16)
                s[h] = st * jnp.exp(lg_d * CH) + _dot((k * kd).astype(BF16), v16, TN)

    def fw(b, c):
        return b * nc + c

    def bw(b, c):
        return b * nc + nc - 1 - c

    st = pl.BlockSpec((None, RH, DK, DV), lambda b, c: (b, 0, 0, 0))
    in_specs = [pl.BlockSpec(memory_space=pltpu.SMEM)] + _wide_specs(fw) + _wide_specs(bw) + [st, st]
    out_specs = [pl.BlockSpec((CH, RH * DV), lambda b, c: (fw(b, c), 0)),
                 pl.BlockSpec((CH, RH * DV), lambda b, c: (bw(b, c), 0)),
                 pl.BlockSpec((None, None, RH, DK, DV), lambda b, c: (b, c, 0, 0, 0)),
                 pl.BlockSpec((None, None, RH, DK, DV), lambda b, c: (b, nc - 1 - c, 0, 0, 0))]
    return pl.pallas_call(
        body, name="ret_fwd", grid=(nb, nc), in_specs=in_specs, out_specs=out_specs,
        out_shape=[_sds((t_rows, RH * DV), BF16)] * 2 + [_sds((nb, nc, RH, DK, DV), BF16)] * 2,
        scratch_shapes=[pltpu.VMEM((RH, DK, DV), F32), pltpu.VMEM((RH, DK, DV), F32)],
        compiler_params=_params(("parallel", "arbitrary")),
    )(lg, *([px] * 10), s0f, s0b)


def _rope_tables(seq):
    rows = seq // GRID_W
    row = np.repeat(np.arange(rows, dtype=np.float32), GRID_W)
    col = np.tile(np.arange(GRID_W, dtype=np.float32), rows)
    half = HD // 2
    freqs = (ROPE_THETA ** (-np.arange(0, half, 2, dtype=np.float32) / half)).astype(np.float32)
    ang = np.concatenate([row[:, None] * freqs, col[:, None] * freqs], axis=-1).astype(np.float32)
    cos = np.repeat(np.cos(ang), 2, axis=-1).astype(np.float32)
    sin = np.repeat(np.sin(ang), 2, axis=-1).astype(np.float32)
    sign = np.tile(np.array([-1.0, 1.0], np.float32), HD // 2)
    return jnp.asarray(cos), jnp.asarray(sin * sign)


def _swap_pairs(v):
    lane = lax.broadcasted_iota(jnp.int32, v.shape, 1)
    return jnp.where((lane & 1) == 0, pltpu.roll(v, HD - 1, 1), pltpu.roll(v, 1, 1))


def _qk_prep(px, nw, cos, sin, rows, row_off, col_off, heads, hb, seq, tm, name):
    rope = cos is not None
    rb0 = row_off // tm
    pb = seq // tm if rope else 1
    bw = hb * HD

    def body(*refs):
        if rope:
            x_ref, w_ref, c_ref, s_ref, o_ref = refs
        else:
            x_ref, w_ref, o_ref = refs
        for h in range(hb):
            sl = slice(h * HD, (h + 1) * HD)
            xv = x_ref[:, sl].astype(F32)
            r = lax.rsqrt(jnp.mean(xv * xv, axis=-1, keepdims=True) + EPS)
            t = (xv * r) * w_ref[...]
            if rope:
                t = t * c_ref[...] + _swap_pairs(t) * s_ref[...]
            o_ref[:, sl] = t.astype(BF16)

    in_specs = [pl.BlockSpec((tm, bw), lambda i, j: (rb0 + i, col_off // bw + j)),
                pl.BlockSpec((1, HD), lambda i, j: (0, 0))]
    args = [px, nw]
    if rope:
        in_specs += [pl.BlockSpec((tm, HD), lambda i, j: (i % pb, 0))] * 2
        args += [cos, sin]
    return pl.pallas_call(
        body, name=name, grid=(rows // tm, heads // hb), in_specs=in_specs,
        out_specs=pl.BlockSpec((tm, bw), lambda i, j: (i, j)), out_shape=_sds((rows, heads * HD), BF16),
        compiler_params=_params(("parallel", "parallel")),
    )(*args)


def _att_fwd(q16, kx16, kc16, px, nb, seq, cx, tq):
    t_rows = nb * seq
    nq = seq // tq
    rep = HQ // HKV
    gw = rep * HD

    def body(q_ref, kx_ref, kc_ref, vx_ref, vc_ref, g_ref, o_ref, y_ref, l_ref):
        kx = kx_ref[...]
        kc = kc_ref[...]
        vx = vx_ref[...].astype(BF16)
        vc = vc_ref[...].astype(BF16)
        l_ref[...] = jnp.zeros_like(l_ref)
        for r in range(rep):
            sl = slice(r * HD, (r + 1) * HD)
            q = q_ref[:, sl]
            s1 = _dot(q, kx, NT)
            s2 = _dot(q, kc, NT)
            m = jnp.maximum(jnp.max(s1, axis=-1, keepdims=True), jnp.max(s2, axis=-1, keepdims=True))
            e1 = jnp.exp2((s1 - m) * SM_C)
            e2 = jnp.exp2((s2 - m) * SM_C)
            tot = jnp.sum(e1, axis=-1, keepdims=True) + jnp.sum(e2, axis=-1, keepdims=True)
            o = (_dot(e1.astype(BF16), vx) + _dot(e2.astype(BF16), vc)) * (1.0 / tot)
            o_ref[:, sl] = o
            y_ref[:, sl] = (o * _silu(g_ref[:, sl].astype(F32))).astype(BF16)
            l_ref[:, r:r + 1] = m * SM_C + jnp.log(tot) * float(np.log2(np.e))

    qblk = pl.BlockSpec((tq, gw), lambda b, g, i: (b * nq + i, g))
    return pl.pallas_call(
        body, name="att_fwd", grid=(nb, HKV, nq),
        in_specs=[qblk,
                  pl.BlockSpec((seq, HD), lambda b, g, i: (b, g)),
                  pl.BlockSpec((cx, HD), lambda b, g, i: (b, g)),
                  pl.BlockSpec((seq, HD), lambda b, g, i: (b, AV // HD + g)),
                  pl.BlockSpec((cx, HD), lambda b, g, i: (t_rows // cx + b, AV // HD + g)),
                  pl.BlockSpec((tq, gw), lambda b, g, i: (b * nq + i, AG // gw + g))],
        out_specs=[qblk, qblk, pl.BlockSpec((tq, 128), lambda b, g, i: (b * nq + i, g))],
        out_shape=[_sds((t_rows, D), F32), _sds((t_rows, D), BF16), _sds((t_rows, HKV * 128), F32)],
        compiler_params=_params(("parallel", "parallel", "parallel")),
    )(q16, kx16, kc16, px, px, px)


def _gate_specs(tm, col0):
    hw = D // 2
    return [pl.BlockSpec((tm, hw), lambda i: (i, col0 // hw)), pl.BlockSpec((tm, hw), lambda i: (i, col0 // hw + 1))]


def _merge(o_f, o_b, yatt16, px, w_o_ret16, w_o_att16, tm):
    t_rows = o_f.shape[0]
    hw = D // 2

    def body(of_ref, ob_ref, g0, g1, g2, g3, wr_ref, ya_ref, wa_ref, mr0, mr1, ma0, ma1, yr_ref, ar_ref, aa_ref, y_ref):
        for h, g_ref in enumerate((g0, g1, g2, g3)):
            sl = slice(h * DV, (h + 1) * DV)
            o = of_ref[:, sl].astype(F32) + ob_ref[:, sl].astype(F32)
            r = lax.rsqrt(jnp.mean(o * o, axis=-1, keepdims=True) + EPS)
            yr_ref[:, sl] = ((o * r) * _silu(g_ref[...].astype(F32))).astype(BF16)
        ar = _dot(yr_ref[...], wr_ref[...])
        aa = _dot(ya_ref[...], wa_ref[...])
        ar_ref[...] = ar.astype(BF16)
        aa_ref[...] = aa.astype(BF16)
        for j, (mr_ref, ma_ref) in enumerate(((mr0, ma0), (mr1, ma1))):
            sl = slice(j * hw, (j + 1) * hw)
            y_ref[:, sl] = (_sig(mr_ref[...].astype(F32)) * ar[:, sl]
                            + _sig(ma_ref[...].astype(F32)) * aa[:, sl]).astype(BF16)

    def gate(h):
        return pl.BlockSpec((tm, DV), lambda i: (i, RG // DV + h))

    row = pl.BlockSpec((tm, D), lambda i: (i, 0))
    wide = pl.BlockSpec((tm, RH * DV), lambda i: (i, 0))
    return pl.pallas_call(
        body, name="merge", grid=(t_rows // tm,),
        in_specs=[wide, wide] + [gate(h) for h in range(RH)]
        + [pl.BlockSpec((RH * DV, D), lambda i: (0, 0)), row, pl.BlockSpec((D, D), lambda i: (0, 0))]
        + _gate_specs(tm, MR) + _gate_specs(tm, MA),
        out_specs=[wide, row, row, row],
        out_shape=[_sds((t_rows, RH * DV), BF16)] + [_sds((t_rows, D), BF16)] * 3,
        compiler_params=_params(("parallel",)),
    )(o_f, o_b, *([px] * RH), w_o_ret16, yatt16, w_o_att16, px, px, px, px)


def _outproj(y16, w_out16, x2, tgt, mod3, nb, seq, tm):
    t_rows = nb * seq
    bpb = seq // tm

    def body(y_ref, w_ref, x_ref, t_ref, g_ref, dxn_ref, dout_ref, dg_ref, loss_ref):
        i = pl.program_id(1)
        out = _dot(y_ref[...], w_ref[...])
        gate = g_ref[...]
        diff = x_ref[...] + gate * out - t_ref[...]
        dxn = diff * (1.0 / D)
        dxn_ref[...] = dxn
        dout_ref[...] = (gate * dxn).astype(BF16)
        dg = jnp.sum(dxn * out, axis=0, keepdims=True)
        ls = jnp.broadcast_to(jnp.sum(diff * diff) * (0.5 / D), (1, 128))

        @pl.when(i == 0)
        def _():
            dg_ref[...] = dg
            loss_ref[...] = ls

        @pl.when(i > 0)
        def _():
            dg_ref[...] += dg
            loss_ref[...] += ls

    row = pl.BlockSpec((tm, D), lambda b, i: (b * bpb + i, 0))
    return pl.pallas_call(
        body, name="outproj", grid=(nb, bpb),
        in_specs=[row, pl.BlockSpec((D, D), lambda b, i: (0, 0)), row, row,
                  pl.BlockSpec((None, 1, D), lambda b, i: (b, 0, 2))],
        out_specs=[row, row, pl.BlockSpec((None, 1, D), lambda b, i: (b, 0, 0)),
                   pl.BlockSpec((None, 1, 128), lambda b, i: (b, 0, 0))],
        out_shape=[_sds((t_rows, D), F32), _sds((t_rows, D), BF16), _sds((nb, 1, D), F32), _sds((nb, 1, 128), F32)],
        compiler_params=_params(("parallel", "arbitrary")),
    )(y16, w_out16, x2, tgt, mod3)


def _bwd_merge(dout16, w_out16, px, a_ret, a_att, tm):
    t_rows = dout16.shape[0]
    hw = D // 2

    def body(do_ref, w_ref, mr0, mr1, ma0, ma1, ar_ref, aa_ref, dar_ref, daa_ref, dmr_ref, dma_ref):
        dy_all = _dot(do_ref[...], w_ref[...], NT)
        for j, (mr_ref, ma_ref) in enumerate(((mr0, ma0), (mr1, ma1))):
            sl = slice(j * hw, (j + 1) * hw)
            dy = dy_all[:, sl]
            sr = _sig(mr_ref[...].astype(F32))
            sa = _sig(ma_ref[...].astype(F32))
            dar_ref[:, sl] = (dy * sr).astype(BF16)
            daa_ref[:, sl] = (dy * sa).astype(BF16)
            dmr_ref[:, sl] = (dy * ar_ref[:, sl].astype(F32) * sr * (1.0 - sr)).astype(BF16)
            dma_ref[:, sl] = (dy * aa_ref[:, sl].astype(F32) * sa * (1.0 - sa)).astype(BF16)

    row = pl.BlockSpec((tm, D), lambda i: (i, 0))
    return pl.pallas_call(
        body, name="bwd_merge", grid=(t_rows // tm,),
        in_specs=[row, pl.BlockSpec((D, D), lambda i: (0, 0))] + _gate_specs(tm, MR) + _gate_specs(tm, MA) + [row, row],
        out_specs=[row] * 4, out_shape=[_sds((t_rows, D), BF16)] * 4,
        compiler_params=_params(("parallel",)),
    )(dout16, w_out16, px, px, px, px, a_ret, a_att)


def _bwd_branch_ret(da_ret16, w_o_ret16, px, o_f, o_b, tm, after=()):
    t_rows = da_ret16.shape[0]

    def body(da_ref, w_ref, g0, g1, g2, g3, of_ref, ob_ref, *rest):
        do_ref, dg_ref = rest[-2:]
        da = da_ref[...]
        for h, g_ref in enumerate((g0, g1, g2, g3)):
            sl = slice(h * DV, (h + 1) * DV)
            dy = _dot(da, w_ref[sl, :], NT)
            g = g_ref[...].astype(F32)
            o = of_ref[:, sl].astype(F32) + ob_ref[:, sl].astype(F32)
            r = lax.rsqrt(jnp.mean(o * o, axis=-1, keepdims=True) + EPS)
            on = o * r
            sg = _sig(g)
            don = dy * (g * sg)
            dg_ref[:, sl] = (dy * on * (sg * (1.0 + g * (1.0 - sg)))).astype(BF16)
            do_ref[:, sl] = (r * (don - on * jnp.mean(on * don, axis=-1, keepdims=True))).astype(BF16)

    def gate(h):
        return pl.BlockSpec((tm, DV), lambda i: (i, RG // DV + h))

    wide = pl.BlockSpec((tm, RH * DV), lambda i: (i, 0))
    return pl.pallas_call(
        body, name="bwd_branch_ret", grid=(t_rows // tm,),
        in_specs=[pl.BlockSpec((tm, D), lambda i: (i, 0)), pl.BlockSpec((RH * DV, D), lambda i: (0, 0))]
        + [gate(h) for h in range(RH)] + [wide, wide] + [pl.BlockSpec(memory_space=pl.ANY)] * len(after),
        out_specs=[wide, wide], out_shape=[_sds((t_rows, RH * DV), BF16)] * 2,
        compiler_params=_params(("parallel",)),
    )(da_ret16, w_o_ret16, *([px] * RH), o_f, o_b, *after)


def _bwd_branch_att(da_att16, w_o_att16, px, o_att, tm):
    t_rows = da_att16.shape[0]
    hw = D // 2

    def body(da_ref, w_ref, g0, g1, o_ref, dao_ref, dg_ref):
        dy_all = _dot(da_ref[...], w_ref[...], NT)
        for j, g_ref in enumerate((g0, g1)):
            sl = slice(j * hw, (j + 1) * hw)
            dy = dy_all[:, sl]
            g = g_ref[...].astype(F32)
            sg = _sig(g)
            dao_ref[:, sl] = dy * (g * sg)
            dg_ref[:, sl] = (dy * o_ref[:, sl] * (sg * (1.0 + g * (1.0 - sg)))).astype(BF16)

    row = pl.BlockSpec((tm, D), lambda i: (i, 0))
    return pl.pallas_call(
        body, name="bwd_branch_att", grid=(t_rows // tm,),
        in_specs=[row, pl.BlockSpec((D, D), lambda i: (0, 0))] + _gate_specs(tm, AG) + [row],
        out_specs=[row, row], out_shape=[_sds((t_rows, D), F32), _sds((t_rows, D), BF16)],
        compiler_params=_params(("parallel",)),
    )(da_att16, w_o_att16, px, px, o_att)


def _att_bwd(q16, kx16, kc16, px, dao, o_att, lse, nb, seq, cx, tq):
    t_rows = nb * seq
    nq = seq // tq
    rep = HQ // HKV
    gw = rep * HD
    scale = HD ** -0.5

    def body(q_ref, kx_ref, kc_ref, vx_ref, vc_ref, dao_ref, o_ref, l_ref, dq_ref, dkx_ref, dvx_ref, dkc_ref, dvc_ref):
        i = pl.program_id(2)
        kx = kx_ref[...]
        kc = kc_ref[...]
        vx = vx_ref[...].astype(BF16)
        vc = vc_ref[...].astype(BF16)
        dkx = jnp.zeros((seq, HD), F32)
        dvx = jnp.zeros((seq, HD), F32)
        dkc = jnp.zeros((cx, HD), F32)
        dvc = jnp.zeros((cx, HD), F32)
        for r in range(rep):
            sl = slice(r * HD, (r + 1) * HD)
            q = q_ref[:, sl]
            lr = l_ref[:, r:r + 1]
            p1 = jnp.exp2(_dot(q, kx, NT) * SM_C - lr)
            p2 = jnp.exp2(_dot(q, kc, NT) * SM_C - lr)
            da = dao_ref[:, sl]
            da16 = da.astype(BF16)
            delta = jnp.sum(da * o_ref[:, sl], axis=-1, keepdims=True)
            ds1 = (p1 * (_dot(da16, vx, NT) - delta)).astype(BF16)
            ds2 = (p2 * (_dot(da16, vc, NT) - delta)).astype(BF16)
            dq_ref[:, sl] = (_dot(ds1, kx) + _dot(ds2, kc)) * scale
            dkx += _dot(ds1, q, TN)
            dkc += _dot(ds2, q, TN)
            dvx += _dot(p1.astype(BF16), da16, TN)
            dvc += _dot(p2.astype(BF16), da16, TN)
        dkx = dkx * scale
        dkc = dkc * scale

        @pl.when(i == 0)
        def _():
            dkx_ref[...] = dkx
            dvx_ref[...] = dvx
            dkc_ref[...] = dkc
            dvc_ref[...] = dvc

        @pl.when(i > 0)
        def _():
            dkx_ref[...] += dkx
            dvx_ref[...] += dvx
            dkc_ref[...] += dkc
            dvc_ref[...] += dvc

    qblk = pl.BlockSpec((tq, gw), lambda b, g, i: (b * nq + i, g))
    kxb = pl.BlockSpec((None, seq, HD), lambda b, g, i: (b, 0, g))
    kcb = pl.BlockSpec((None, cx, HD), lambda b, g, i: (b, 0, g))
    return pl.pallas_call(
        body, name="att_bwd", grid=(nb, HKV, nq),
        in_specs=[qblk,
                  pl.BlockSpec((seq, HD), lambda b, g, i: (b, g)),
                  pl.BlockSpec((cx, HD), lambda b, g, i: (b, g)),
                  pl.BlockSpec((seq, HD), lambda b, g, i: (b, AV // HD + g)),
                  pl.BlockSpec((cx, HD), lambda b, g, i: (t_rows // cx + b, AV // HD + g)),
                  qblk, qblk, pl.BlockSpec((tq, 128), lambda b, g, i: (b * nq + i, g))],
        out_specs=[qblk, kxb, kxb, kcb, kcb],
        out_shape=[_sds((t_rows, D), F32), _sds((nb, seq, HKV * HD), F32), _sds((nb, seq, HKV * HD), F32),
                   _sds((nb, cx, HKV * HD), F32), _sds((nb, cx, HKV * HD), F32)],
        compiler_params=_params(("parallel", "parallel", "arbitrary")),
    )(q16, kx16, kc16, px, px, dao, o_att, lse)


def _qk_prep_bwd(dt, px, nw, cos, sin, rows, row_off, col_off, heads, hb, seq, tm, name):
    rope = cos is not None
    rb0 = row_off // tm
    pb = seq // tm if rope else 1
    bw = hb * HD

    def body(*refs):
        if rope:
            d_ref, x_ref, w_ref, c_ref, s_ref, dx_ref, dw_ref = refs
        else:
            d_ref, x_ref, w_ref, dx_ref, dw_ref = refs
        first = jnp.logical_and(pl.program_id(0) == 0, pl.program_id(1) == 0)
        dw = jnp.zeros((1, HD), F32)
        for h in range(hb):
            sl = slice(h * HD, (h + 1) * HD)
            dtv = d_ref[:, sl]
            if rope:
                dtv = dtv * c_ref[...] + _swap_pairs(dtv * s_ref[...])
            xv = x_ref[:, sl].astype(F32)
            r = lax.rsqrt(jnp.mean(xv * xv, axis=-1, keepdims=True) + EPS)
            xh = xv * r
            dxh = dtv * w_ref[...]
            dx_ref[:, sl] = (r * (dxh - xh * jnp.mean(dxh * xh, axis=-1, keepdims=True))).astype(BF16)
            dw += jnp.sum(dtv * xh, axis=0, keepdims=True)

        @pl.when(first)
        def _():
            dw_ref[...] = dw

        @pl.when(jnp.logical_not(first))
        def _():
            dw_ref[...] += dw

    blk = pl.BlockSpec((tm, bw), lambda i, j: (i, j))
    in_specs = [blk, pl.BlockSpec((tm, bw), lambda i, j: (rb0 + i, col_off // bw + j)),
                pl.BlockSpec((1, HD), lambda i, j: (0, 0))]
    args = [dt, px, nw]
    if rope:
        in_specs += [pl.BlockSpec((tm, HD), lambda i, j: (i % pb, 0))] * 2
        args += [cos, sin]
    return pl.pallas_call(
        body, name=name, grid=(rows // tm, heads // hb), in_specs=in_specs,
        out_specs=[blk, pl.BlockSpec((1, HD), lambda i, j: (0, 0))],
        out_shape=[_sds((rows, heads * HD), BF16), _sds((1, HD), F32)],
        compiler_params=_params(("arbitrary", "arbitrary")),
    )(*args)


def _ret_bwd(px, lg, do16, hist_f, hist_b, nb, nc):
    t_rows = nb * nc * CH

    def body(lg_ref, *refs):
        ins = (refs[0:5], refs[7:12])
        do_refs = (refs[5], refs[12])
        h_refs = (refs[6], refs[13])
        outs = (refs[14:17], refs[17:20])
        ds_outs = (refs[20], refs[21])
        dlg_ref = refs[22]
        dss = (refs[23], refs[24])
        c = pl.program_id(1)

        @pl.when(c == 0)
        def _():
            dss[0][...] = jnp.zeros_like(dss[0])
            dss[1][...] = jnp.zeros_like(dss[1])
            dlg_ref[...] = jnp.zeros_like(dlg_ref)

        for d in range(2):
            dq_ref, dk_ref, dv_ref = outs[d]
            for h in range(RH):
                lg_d = lg_ref[d, h]
                mask, relf, qd, qe, kd, ke = _decays(lg_d, d == 0)
                g_ch = jnp.exp(lg_d * CH)
                q, k, v16 = _head_qkv(ins[d], h)
                q16 = q.astype(BF16)
                k16 = k.astype(BF16)
                do16v = do_refs[d][:, h * DV:(h + 1) * DV]
                st16 = h_refs[d][h]
                dst = dss[d][h]
                dst16 = dst.astype(BF16)
                a = _dot(q16, k16, NT) * mask
                dp = _dot(do16v, v16, NT)
                da16 = (dp * mask).astype(BF16)
                dq_cross = _dot(do16v, st16, NT) * qd
                dq_ref[:, h * DK:(h + 1) * DK] = (_dot(da16, k16) + dq_cross).astype(BF16)
                dk_state = _dot(v16, dst16, NT) * kd
                dk_ref[:, h * DK:(h + 1) * DK] = ((_dot(da16, q16, TN) + dk_state) * (DK ** -0.5)).astype(BF16)
                dv = _dot(a.astype(BF16), do16v, TN) + _dot((k * kd).astype(BF16), dst16)
                dv_ref[:, h * DV:(h + 1) * DV] = dv.astype(BF16)
                dlg = (jnp.sum(relf * a * dp)
                       + jnp.sum(qe * jnp.sum(q * dq_cross, axis=-1, keepdims=True))
                       + jnp.sum(ke * jnp.sum(k * dk_state, axis=-1, keepdims=True))
                       + CH * g_ch * jnp.sum(dst * st16.astype(F32)))
                row = d * RH + h
                dlg_ref[row:row + 1, :] += jnp.broadcast_to(dlg, (1, 128))
                ds_new = g_ch * dst + _dot((q * qd).astype(BF16), do16v, TN)
                dss[d][h] = ds_new

                @pl.when(c == nc - 1)
                def _():
                    ds_outs[d][h] = ds_new

    def fw(b, c):
        return b * nc + nc - 1 - c

    def bw(b, c):
        return b * nc + c

    def rows(rowf, width):
        return pl.BlockSpec((CH, width), lambda b, c: (rowf(b, c), 0))

    def hist(rowf):
        return pl.BlockSpec((None, None, RH, DK, DV), lambda b, c: (b, rowf(0, c), 0, 0, 0))

    st = pl.BlockSpec((None, RH, DK, DV), lambda b, c: (b, 0, 0, 0))
    in_specs = [pl.BlockSpec(memory_space=pltpu.SMEM)]
    out_specs = []
    for rowf in (fw, bw):
        in_specs += _wide_specs(rowf) + [rows(rowf, RH * DV), hist(rowf)]
        out_specs += [rows(rowf, RH * DK), rows(rowf, RH * DK), rows(rowf, RH * DV)]
    out_specs += [st, st, pl.BlockSpec((None, 8, 128), lambda b, c: (b, 0, 0))]
    qk = _sds((t_rows, RH * DK), BF16)
    vv = _sds((t_rows, RH * DV), BF16)
    return pl.pallas_call(
        body, name="ret_bwd", grid=(nb, nc), in_specs=in_specs, out_specs=out_specs,
        out_shape=[qk, qk, vv, qk, qk, vv, _sds((nb, RH, DK, DV), F32), _sds((nb, RH, DK, DV), F32),
                   _sds((nb, 8, 128), F32)],
        scratch_shapes=[pltpu.VMEM((RH, DK, DV), F32), pltpu.VMEM((RH, DK, DV), F32)],
        compiler_params=_params(("parallel", "arbitrary")),
    )(lg, *([px] * 5), do16, hist_f, *([px] * 5), do16, hist_b)


def _ctx_state_bwd(px, lg, ds_f, ds_b, nb, t_rows, cx):
    rb = t_rows // cx

    def body(lg_ref, k_ref, v_ref, dsf_ref, dsb_ref, dk_ref, dv_ref, dlg_ref):
        h = pl.program_id(1)
        pos = lax.broadcasted_iota(jnp.int32, (cx, 1), 0).astype(F32)
        k = k_ref[...].astype(F32) * (DK ** -0.5)
        v16 = v_ref[...].astype(BF16)
        dk = jnp.zeros((cx, DK), F32)
        dv = jnp.zeros((cx, DV), F32)
        dlg_ref[...] = jnp.zeros_like(dlg_ref)
        for d, (ds_ref, e) in enumerate(((dsf_ref, cx - 1.0 - pos), (dsb_ref, pos))):
            w = jnp.exp(lg_ref[d, h] * e)
            ds16 = ds_ref[...].astype(BF16)
            t = _dot(v16, ds16, NT)
            dk += t * w
            dv += _dot((k * w).astype(BF16), ds16)
            dlg = jnp.sum(e * w * jnp.sum(k * t, axis=-1, keepdims=True))
            dlg_ref[d:d + 1, :] = jnp.broadcast_to(dlg, (1, 128))
        dk_ref[...] = (dk * (DK ** -0.5)).astype(BF16)
        dv_ref[...] = dv.astype(BF16)

    st = pl.BlockSpec((None, None, DK, DV), lambda b, h: (b, h, 0, 0))
    return pl.pallas_call(
        body, name="ctx_state_bwd", grid=(nb, RH),
        in_specs=[pl.BlockSpec(memory_space=pltpu.SMEM),
                  pl.BlockSpec((cx, DK), lambda b, h: (rb + b, RK // DK + h)),
                  pl.BlockSpec((cx, DV), lambda b, h: (rb + b, RV // DV + h)), st, st],
        out_specs=[pl.BlockSpec((cx, DK), lambda b, h: (b, h)), pl.BlockSpec((cx, DV), lambda b, h: (b, h)),
                   pl.BlockSpec((None, None, 8, 128), lambda b, h: (b, h, 0, 0))],
        out_shape=[_sds((nb * cx, RH * DK), BF16), _sds((nb * cx, RH * DV), BF16), _sds((nb, RH, 8, 128), F32)],
        compiler_params=_params(("parallel", "parallel")),
    )(lg, px, px, ds_f, ds_b)


def _assemble_lat(rows_all, dk_f, dk_b, dv_f, dv_b, dak16, dvx, dq_f, dq_b, drg16, daq16, dag16, dmr16, dma16, tm):
    t_rows = dk_f.shape[0]

    def body(dkf, dkb, dvf, dvb, dak, dav, dqf, dqb, drg, daq, dag, dmr, dma, o_ref):
        o_ref[:, RK:RK + RH * DK] = (dkf[...].astype(F32) + dkb[...].astype(F32)).astype(BF16)
        o_ref[:, RV:RV + RH * DV] = (dvf[...].astype(F32) + dvb[...].astype(F32)).astype(BF16)
        o_ref[:, AK:AK + HKV * HD] = dak[...]
        o_ref[:, AV:AV + HKV * HD] = dav[...].astype(BF16)
        o_ref[:, RQ:RQ + RH * DK] = (dqf[...].astype(F32) + dqb[...].astype(F32)).astype(BF16)
        o_ref[:, RG:RG + RH * DV] = drg[...]
        o_ref[:, AQ:AQ + D] = daq[...]
        o_ref[:, AG:AG + D] = dag[...]
        o_ref[:, MR:MR + D] = dmr[...]
        o_ref[:, MA:MA + D] = dma[...]

    args = (dk_f, dk_b, dv_f, dv_b, dak16, dvx, dq_f, dq_b, drg16, daq16, dag16, dmr16, dma16)
    return pl.pallas_call(
        body, name="assemble_lat", grid=(t_rows // tm,),
        in_specs=[pl.BlockSpec((tm, a.shape[1]), lambda i: (i, 0)) for a in args],
        out_specs=pl.BlockSpec((tm, IN_COLS), lambda i: (i, 0)), out_shape=_sds((rows_all, IN_COLS), BF16),
        compiler_params=_params(("parallel",)),
    )(*args)


def _assemble_ctx(dp_all, dck16, dcv16, dcak16, dvc, t_rows, tm):
    c_rows = dck16.shape[0]
    rb = t_rows // tm

    def body(_, dck, dcv, dcak, dcav, o_ref):
        o_ref[:, RK:RK + RH * DK] = dck[...]
        o_ref[:, RV:RV + RH * DV] = dcv[...]
        o_ref[:, AK:AK + HKV * HD] = dcak[...]
        o_ref[:, AV:AV + HKV * HD] = dcav[...].astype(BF16)
        o_ref[:, KV_COLS:] = jnp.zeros((tm, IN_COLS - KV_COLS), BF16)

    args = (dck16, dcv16, dcak16, dvc)
    return pl.pallas_call(
        body, name="assemble_ctx", grid=(c_rows // tm,),
        in_specs=[pl.BlockSpec(memory_space=pl.ANY)]
        + [pl.BlockSpec((tm, a.shape[1]), lambda i: (i, 0)) for a in args],
        out_specs=pl.BlockSpec((tm, IN_COLS), lambda i: (rb + i, 0)), out_shape=_sds(dp_all.shape, BF16),
        input_output_aliases={0: 0},
        compiler_params=_params(("parallel",)),
    )(dp_all, *args)


def _norm_bwd(dh, x2, mod3, norm_w, dxn, row_off, rows_per_group, group0, tm, name):
    with_dx = dxn is not None
    rows = x2.shape[0]
    rb0 = row_off // tm
    bpg = rows_per_group // tm
    ngroups = rows // rows_per_group

    def body(*refs):
        if with_dx:
            dh_ref, x_ref, sc_ref, nw_ref, dxn_ref, dx_ref, dsh_ref, dsc_ref, dnw_ref = refs
        else:
            dh_ref, x_ref, sc_ref, nw_ref, dsh_ref, dsc_ref, dnw_ref = refs
        i = pl.program_id(0)
        dhv = dh_ref[...]
        xv = x_ref[...]
        nw = nw_ref[...]
        r = lax.rsqrt(jnp.mean(xv * xv, axis=-1, keepdims=True) + EPS)
        xh = xv * r
        dm = dhv * (1.0 + sc_ref[...])
        dsh = jnp.sum(dhv, axis=0, keepdims=True)
        dsc = jnp.sum(dhv * (xh * nw), axis=0, keepdims=True)
        dnw = jnp.sum(dm * xh, axis=0, keepdims=True)
        if with_dx:
            dxh = dm * nw
            dx_ref[...] = dxn_ref[...] + r * (dxh - xh * jnp.mean(dxh * xh, axis=-1, keepdims=True))

        @pl.when(i % bpg == 0)
        def _():
            dsh_ref[...] = dsh
            dsc_ref[...] = dsc

        @pl.when(i % bpg != 0)
        def _():
            dsh_ref[...] += dsh
            dsc_ref[...] += dsc

        @pl.when(i == 0)
        def _():
            dnw_ref[...] = dnw

        @pl.when(i > 0)
        def _():
            dnw_ref[...] += dnw

    grp = pl.BlockSpec((None, 1, D), lambda i: (i // bpg, 0, 0))
    in_specs = [pl.BlockSpec((tm, D), lambda i: (rb0 + i, 0)), pl.BlockSpec((tm, D), lambda i: (i, 0)),
                pl.BlockSpec((None, 1, D), lambda i: (group0 + i // bpg, 0, 1)),
                pl.BlockSpec((1, D), lambda i: (0, 0))]
    args = [dh, x2, mod3, norm_w]
    out_specs = [grp, grp, pl.BlockSpec((1, D), lambda i: (0, 0))]
    out_shape = [_sds((ngroups, 1, D), F32), _sds((ngroups, 1, D), F32), _sds((1, D), F32)]
    if with_dx:
        in_specs.append(pl.BlockSpec((tm, D), lambda i: (i, 0)))
        args.append(dxn)
        out_specs.insert(0, pl.BlockSpec((tm, D), lambda i: (i, 0)))
        out_shape.insert(0, _sds((rows, D), F32))
    return pl.pallas_call(
        body, name=name, grid=(rows // tm,), in_specs=in_specs, out_specs=out_specs, out_shape=out_shape,
        compiler_params=_params(("arbitrary",)),
    )(*args)


def _small_final(dmod_all, dmodc_parts, c_rows, dm_loc_rows, nw_parts, misc_parts, c_ctx, r_pad, w_ada16):
    loc = dm_loc_rows.shape[1]

    def body(dm_ref, dmc_ref, c_ref, dml_ref, nwp_ref, mp_ref, cc_ref, r_ref, w_ref,
             gb_ref, gc_ref, gnw_ref, misc_ref, gwa_ref):
        dmc = jnp.sum(dmc_ref[...], axis=0, keepdims=True)
        gb_ref[...] = jnp.sum(dm_ref[...], axis=0, keepdims=True) + dmc
        dsc = _dot(jnp.broadcast_to(dmc, (8, 3 * D)).astype(BF16), w_ref[...], NT)[0:1, :]
        gc_ref[...] = dsc * _dsilu(cc_ref[...])
        gnw_ref[...] = jnp.sum(nwp_ref[...], axis=0, keepdims=True)
        misc = jnp.sum(mp_ref[...], axis=0, keepdims=True)
        y = jnp.exp2(r_ref[...])
        lane = lax.broadcasted_iota(jnp.int32, (1, D), 1)
        is_decay = jnp.logical_and(lane >= 2 * HD, lane < 2 * HD + 2 * RH)
        misc_ref[...] = misc * jnp.where(is_decay, -(y * np.float32(np.log(2.0))) / (1.0 - y), 1.0)
        gwa_ref[...] = _dot(_silu(c_ref[...]).astype(BF16), dml_ref[...].astype(BF16), TN)

    return pl.pallas_call(
        body, name="small_final",
        out_shape=[_sds((1, 3 * D), F32), _sds((1, D), F32), _sds((1, D), F32), _sds((1, D), F32), _sds((D, loc), F32)],
        compiler_params=pltpu.CompilerParams(vmem_limit_bytes=VMEM_LIMIT),
    )(dmod_all, dmodc_parts, c_rows, dm_loc_rows, nw_parts, misc_parts, c_ctx, r_pad, w_ada16)


def _adamw_math(w, g, m, v):
    nm = B1 * m + (1.0 - B1) * g
    nv = B2 * v + (1.0 - B2) * (g * g)
    return -LR * ((nm / (1.0 - B1 ** STEP)) / (jnp.sqrt(nv / (1.0 - B2 ** STEP)) + ADAM_EPS) + WD * w), nm, nv


def _adamw(w, g, m, v, name):
    rows, cols = w.shape
    tm = _pick(rows, 448, 8)

    def body(w_ref, g_ref, m_ref, v_ref, d_ref, nm_ref, nv_ref):
        d_ref[...], nm_ref[...], nv_ref[...] = _adamw_math(w_ref[...], g_ref[...], m_ref[...], v_ref[...])

    blk = pl.BlockSpec((tm, cols), lambda i: (i, 0))
    return pl.pallas_call(
        body, name=name, grid=(rows // tm,), in_specs=[blk] * 4, out_specs=[blk] * 3,
        out_shape=[_sds((rows, cols), F32)] * 3, compiler_params=_params(("parallel",)),
    )(w, g, m, v)


def _mesh_pos():
    return lax.axis_index("x"), lax.axis_index("y"), lax.axis_index("c")


def _all_gather(arrs, name):
    n = len(arrs)

    def body(*refs):
        ins, outs = refs[:n], refs[n:2 * n]
        send_sems, recv_sems, local_sems = refs[2 * n:]
        x, y, c = _mesh_pos()
        me, sib = (x, y, c), (x, y, 1 - c)
        chips = [(1 - x, y), (x, 1 - y), (1 - x, 1 - y)]

        def slot(p):
            return 4 * p[0] + 2 * p[1] + p[2]

        def copy(a, k, block, to, own):
            dst = outs[a].at[slot(block)]
            return pltpu.make_async_remote_copy(
                src_ref=ins[a] if own else dst, dst_ref=dst, send_sem=send_sems.at[a, k], recv_sem=recv_sems.at[a, k],
                device_id=to, device_id_type=MESH_T)

        mine = [pltpu.make_async_copy(ins[a], outs[a].at[slot(me)], local_sems.at[a]) for a in range(n)]
        for cp in mine:
            cp.start()
        first = []
        for a in range(n):
            first.append(copy(a, 0, me, sib, True))
            first += [copy(a, 1 + j, me, (*chip, c), True) for j, chip in enumerate(chips)]
        for cp in first:
            cp.start()
        passed = []
        for j, chip in enumerate(chips):
            for a in range(n):
                copy(a, 1 + j, (*chip, c), me, False).wait_recv()
                fwd = copy(a, 4 + j, (*chip, c), sib, False)
                fwd.start()
                passed.append(fwd)
        for a in range(n):
            copy(a, 0, sib, me, False).wait_recv()
            for j, chip in enumerate(chips):
                copy(a, 4 + j, (*chip, 1 - c), me, False).wait_recv()
        for cp in first + passed:
            cp.wait_send()
        for cp in mine:
            cp.wait()

    hbm = pl.BlockSpec(memory_space=pl.ANY)
    return pl.pallas_call(
        body, name=name, in_specs=[hbm] * n, out_specs=[hbm] * n,
        out_shape=[_sds((N_DEV,) + a.shape, a.dtype) for a in arrs],
        scratch_shapes=[pltpu.SemaphoreType.DMA((n, 7)), pltpu.SemaphoreType.DMA((n, 7)), pltpu.SemaphoreType.DMA((n,))],
    )(*arrs)


def _pair_exchange(arrs, name):
    n = len(arrs)

    def body(*refs):
        ins, outs = refs[:n], refs[n:2 * n]
        send_sems, recv_sems = refs[2 * n:]
        x, y, c = _mesh_pos()
        sib = (x, y, 1 - c)
        sends = []
        for a in range(n):
            for k in range(4):
                sends.append(pltpu.make_async_remote_copy(
                    src_ref=ins[a].at[2 * k + 1 - c], dst_ref=outs[a].at[k], send_sem=send_sems.at[a, k],
                    recv_sem=recv_sems.at[a, k], device_id=sib, device_id_type=MESH_T))
        for cp in sends:
            cp.start()
        for cp in sends:
            cp.wait_recv()
        for cp in sends:
            cp.wait_send()

    hbm = pl.BlockSpec(memory_space=pl.ANY)
    return pl.pallas_call(
        body, name=name, in_specs=[hbm] * n, out_specs=[hbm] * n,
        out_shape=[_sds((4,) + a.shape[1:], a.dtype) for a in arrs],
        scratch_shapes=[pltpu.SemaphoreType.DMA((n, 4)), pltpu.SemaphoreType.DMA((n, 4))],
    )(*arrs)


def _pair_add(part, got, core, name):
    _, rows, cols = part.shape
    tm = _pick(rows, 672, 16)
    p4 = part.reshape(4, 2, rows, cols)

    def body(core_ref, p_ref, g_ref, o_ref):
        o_ref[...] = (p_ref[...].astype(F32) + g_ref[...].astype(F32)).astype(BF16)

    blk = pl.BlockSpec((None, tm, cols), lambda k, i, cr: (k, i, 0))
    return pl.pallas_call(
        body, name=name,
        grid_spec=pltpu.PrefetchScalarGridSpec(
            num_scalar_prefetch=1, grid=(4, rows // tm),
            in_specs=[pl.BlockSpec((None, None, tm, cols), lambda k, i, cr: (k, cr[0], i, 0)), blk], out_specs=blk),
        out_shape=_sds((4, rows, cols), BF16), compiler_params=_params(("parallel", "parallel")),
    )(core, p4, got)


def _chip_sum_adamw(pair_sums, landed, chip, w, m, v, name):
    _, rows, cols = pair_sums.shape
    tm = _pick(rows, 448, 16)

    def body(chip_ref, s_ref, l_ref, w_ref, m_ref, v_ref, g_ref, d_ref, nm_ref, nv_ref):
        acc = s_ref[...].astype(F32)
        for j in range(3):
            acc = acc + l_ref[j].astype(F32)
        g_ref[...] = acc
        d_ref[...], nm_ref[...], nv_ref[...] = _adamw_math(w_ref[...], acc, m_ref[...], v_ref[...])

    blk = pl.BlockSpec((tm, cols), lambda i, ch: (i, 0))
    return pl.pallas_call(
        body, name=name,
        grid_spec=pltpu.PrefetchScalarGridSpec(
            num_scalar_prefetch=1, grid=(rows // tm,),
            in_specs=[pl.BlockSpec((None, tm, cols), lambda i, ch: (ch[0], i, 0)),
                      pl.BlockSpec((3, tm, cols), lambda i, ch: (0, i, 0)), blk, blk, blk],
            out_specs=[blk] * 4),
        out_shape=[_sds((rows, cols), F32)] * 4, compiler_params=_params(("parallel",)),
    )(chip, pair_sums, landed, w, m, v)


_HBM = pl.BlockSpec(memory_space=pltpu.HBM)
_SEM = pl.BlockSpec(memory_space=pltpu.SEMAPHORE)
_EFFECT = pltpu.SideEffectType.DATAFLOW_SIDE_EFFECTING


def _chip_routes(n):
    def plan(x, y, c):
        routes = []
        for a in range(n):
            for j in range(1, 4):
                px, py = x ^ (j >> 1), y ^ (j & 1)
                routes.append((a, 2 * px + py, (px, py, c), j - 1))
        return routes
    return plan, 3 * n


def _bcast_routes(n):
    def plan(x, y, c):
        routes = []
        for a in range(n):
            for k in range(1, N_DEV):
                peer = (x ^ ((k >> 2) & 1), y ^ ((k >> 1) & 1), c ^ (k & 1))
                routes.append((a, 0, peer, 4 * x + 2 * y + c))
        return routes
    return plan, 7 * n


def _route_copies(srcs, lands, send_sems, recv_sems, routes):
    return [pltpu.make_async_remote_copy(
        src_ref=srcs[a].at[sb], dst_ref=lands[a].at[lb], send_sem=send_sems.at[r], recv_sem=recv_sems.at[r],
        device_id=peer, device_id_type=MESH_T) for r, (a, sb, peer, lb) in enumerate(routes)]


def _exchange_start(srcs, lands, routes, name, after=()):
    plan, count = routes
    n = len(srcs)
    n_in = 2 * n + len(after)

    def body(*refs):
        send_sems, recv_sems = refs[n_in], refs[n_in + 1]
        token = refs[-1]
        for cp in _route_copies(refs[:n], refs[n:2 * n], send_sems, recv_sems, plan(*_mesh_pos())):
            cp.start()
        token[...] = jnp.zeros_like(token)

    args = [pltpu.with_memory_space_constraint(a, pltpu.HBM) for a in list(srcs) + list(lands)]
    out = pl.pallas_call(
        body, name=name,
        out_shape=(pltpu.SemaphoreType.DMA((count,)), pltpu.SemaphoreType.DMA((count,)),
                   *[pltpu.HBM(a.shape, a.dtype) for a in args], _sds((8, 128), F32)),
        in_specs=[_HBM] * (2 * n) + [pl.BlockSpec(memory_space=pl.ANY)] * len(after),
        out_specs=(_SEM, _SEM, *([_HBM] * (2 * n)), pl.BlockSpec(memory_space=pltpu.VMEM)),
        input_output_aliases={i: 2 + i for i in range(2 * n)},
        compiler_params=pltpu.CompilerParams(has_side_effects=_EFFECT),
    )(*args, *after)
    return (out[0], out[1], list(out[2:2 + 2 * n]), routes), out[-1]


def _exchange_wait(state, after, name):
    send_sems, recv_sems, bufs, (plan, count) = state
    n = len(bufs) // 2

    def body(*refs):
        send_s, recv_s = refs[2 * n], refs[2 * n + 1]
        for cp in _route_copies(refs[:n], refs[n:2 * n], send_s, recv_s, plan(*_mesh_pos())):
            cp.wait_send()
            cp.wait_recv()

    out = pl.pallas_call(
        body, name=name, out_shape=tuple(pltpu.HBM(a.shape, a.dtype) for a in bufs),
        in_specs=[_HBM] * (2 * n) + [_SEM, _SEM, pl.BlockSpec(memory_space=pl.ANY)], out_specs=tuple([_HBM] * (2 * n)),
        input_output_aliases={i: i for i in range(2 * n)},
        compiler_params=pltpu.CompilerParams(has_side_effects=_EFFECT),
    )(*bufs, send_sems, recv_sems, after)
    return list(out[:n]), list(out[n:])


def _group_routes(js):
    def plan(x, y, c):
        return [(0, 0, (x ^ (j >> 1), y ^ (j & 1), c), 2 * j + c) for j in js]
    return plan, len(js)


def _pair_fill(groups, js, name, after=()):
    def body(*refs):
        g_ref, send_sems, recv_sems = refs[-3:]
        x, y, c = _mesh_pos()
        sends = []
        for n, j in enumerate(js):
            mine = g_ref.at[2 * j + c]
            sends.append(pltpu.make_async_remote_copy(
                src_ref=mine, dst_ref=mine, send_sem=send_sems.at[n], recv_sem=recv_sems.at[n],
                device_id=(x, y, 1 - c), device_id_type=MESH_T))
        for cp in sends:
            cp.start()
        for n, j in enumerate(js):
            pltpu.make_async_remote_copy(
                src_ref=g_ref.at[2 * j + c], dst_ref=g_ref.at[2 * j + 1 - c], send_sem=send_sems.at[n],
                recv_sem=recv_sems.at[n], device_id=(x, y, 1 - c), device_id_type=MESH_T).wait_recv()
        for cp in sends:
            cp.wait_send()

    hbm = pl.BlockSpec(memory_space=pl.ANY)
    return pl.pallas_call(
        body, name=name, in_specs=[hbm] * (1 + len(after)), out_specs=hbm, out_shape=_sds(groups.shape, groups.dtype),
        input_output_aliases={0: 0},
        scratch_shapes=[pltpu.SemaphoreType.DMA((len(js),)), pltpu.SemaphoreType.DMA((len(js),))],
    )(groups, *after)


def _in_proj_group(h_all, groups, j0, ng, chip, px_prev, after, name):
    rows_all = h_all.shape[0]
    gcols = IN_COLS // 4
    tm = _pick(rows_all, 1536, 128)
    g4 = groups.reshape(4, gcols, D)

    n_lead = (1 if px_prev is not None else 0) + len(after)
    lead = ([px_prev] if px_prev is not None else []) + list(after)

    def body(chip_ref, *refs):
        h_ref, w_ref, o_ref = refs[n_lead:]
        o_ref[...] = _dot(h_ref[...], w_ref[...], NT).astype(BF16)

    return pl.pallas_call(
        body, name=name,
        grid_spec=pltpu.PrefetchScalarGridSpec(
            num_scalar_prefetch=1, grid=(ng, rows_all // tm),
            in_specs=[pl.BlockSpec(memory_space=pl.ANY)] * n_lead
            + [pl.BlockSpec((tm, D), lambda n, i, ch: (i, 0)),
               pl.BlockSpec((None, gcols, D), lambda n, i, ch: (j0 + n, 0, 0))],
            out_specs=pl.BlockSpec((tm, gcols), lambda n, i, ch: (i, ch[0] ^ (j0 + n)))),
        out_shape=_sds((rows_all, IN_COLS), BF16),
        input_output_aliases={1: 0} if px_prev is not None else {},
        compiler_params=_params(("parallel", "parallel")),
    )(chip, *lead, h_all, g4)


def _d_h_groups(dp_all, groups, chip, after):
    rows_all = dp_all.shape[0]
    gcols = IN_COLS // 4
    tm = _pick(rows_all, 1536, 128)
    g4 = groups.reshape(4, gcols, D)
    n_lead = len(after)

    def body(chip_ref, *refs):
        a_ref, w_ref, o_ref = refs[n_lead:]
        j = pl.program_id(1)
        part = _dot(a_ref[...], w_ref[...])

        @pl.when(j == 0)
        def _():
            o_ref[...] = part

        @pl.when(j > 0)
        def _():
            o_ref[...] += part

    return pl.pallas_call(
        body, name="d_h",
        grid_spec=pltpu.PrefetchScalarGridSpec(
            num_scalar_prefetch=1, grid=(rows_all // tm, 4),
            in_specs=[pl.BlockSpec(memory_space=pl.ANY)] * n_lead
            + [pl.BlockSpec((tm, gcols), lambda i, j, ch: (i, ch[0] ^ j)),
               pl.BlockSpec((None, gcols, D), lambda i, j, ch: (j, 0, 0))],
            out_specs=pl.BlockSpec((tm, D), lambda i, j, ch: (i, 0))),
        out_shape=_sds((rows_all, D), F32),
        compiler_params=_params(("parallel", "arbitrary")),
    )(chip, *after, dp_all, g4)


def _reduce_scatter_start(parts, core, name):
    got = _pair_exchange(parts, name + "_pair")
    sums = [_pair_add(p, g, core, "%s_add_%d" % (name, i)) for i, (p, g) in enumerate(zip(parts, got))]
    lands = [lax.empty((3,) + s_.shape[1:], BF16) for s_ in sums]
    return _exchange_start(sums, lands, _chip_routes(len(sums)), name + "_start")


def _reduce_scatter_finish(rs_state, after, chip, wmv, name):
    sums, landed = _exchange_wait(rs_state, after, name + "_wait")
    return [_chip_sum_adamw(s_, l_, chip, *t, "%s_adamw_%d" % (name, i))
            for i, (s_, l_, t) in enumerate(zip(sums, landed, wmv))]


def _local_step(x, c, ctx, norm_w, ret_log2_decay, q_norm_w, k_norm_w, loss_target,
                mod, proj_in, proj_back, get_w_o, on_out_grads, on_in_grad, started=()):
    nb, seq, _ = x.shape
    cx = ctx.shape[1]
    t_rows, c_rows = nb * seq, nb * cx
    rows_all = t_rows + c_rows
    nc = seq // CH
    tm = _pick(seq, 256, 128)
    te = _pick(seq, 512, 128)
    assert cx % tm == 0 and t_rows % cx == 0 and seq % GRID_W == 0

    x2 = x.reshape(t_rows, D)
    ctx2 = ctx.reshape(c_rows, D)
    tgt = loss_target.reshape(t_rows, D)
    lg = _log_gamma(ret_log2_decay)
    cos, sin = _rope_tables(seq)

    mod3 = mod[:, None, :]
    h_all = _norm_fwd(x2, mod3, norm_w, rows_all, 0, seq, 0, None, te, "norm_fwd", after=started)
    h_all = _norm_fwd(ctx2, mod3, norm_w, rows_all, t_rows, c_rows, nb, h_all, tm, "norm_fwd_ctx")
    px = proj_in(h_all)
    s0f, s0b = _ctx_state(px, lg, nb, t_rows, cx)
    o_f, o_b, hist_f, hist_b = _ret_fwd(px, lg, s0f, s0b, nb, nc)
    q16 = _qk_prep(px, q_norm_w, cos, sin, t_rows, 0, AQ, HQ, 4, seq, te, "q_prep")
    kx16 = _qk_prep(px, k_norm_w, cos, sin, t_rows, 0, AK, HKV, HKV, seq, te, "k_prep")
    kc16 = _qk_prep(px, k_norm_w, None, None, c_rows, t_rows, AK, HKV, HKV, seq, tm, "kc_prep")
    o_att, yatt16, lse = _att_fwd(q16, kx16, kc16, px, nb, seq, cx, te)
    w_o_ret16, w_o_att16, w_out16 = get_w_o(lse)
    yret16, a_ret, a_att, y16 = _merge(o_f, o_b, yatt16, px, w_o_ret16, w_o_att16, te)
    dxn, dout16, dgate, loss_b = _outproj(y16, w_out16, x2, tgt, mod3, nb, seq, te)

    gw_out = _matmul(y16, dout16, ta=True, tm=D, tn=D, tk=D, out_dtype=BF16, name="gw_out")
    da_ret16, da_att16, dmr16, dma16 = _bwd_merge(dout16, w_out16, px, a_ret, a_att, te)
    gw_o_ret = _matmul(yret16, da_ret16, ta=True, tm=D, tn=D, tk=D, out_dtype=BF16, name="gw_o_ret")
    gw_o_att = _matmul(yatt16, da_att16, ta=True, tm=D, tn=D, tk=D, out_dtype=BF16, name="gw_o_att")
    out_state, out_started = on_out_grads([gw_o_ret, gw_o_att, gw_out])
    do16, drg16 = _bwd_branch_ret(da_ret16, w_o_ret16, px, o_f, o_b, te, after=out_started)
    dao, dag16 = _bwd_branch_att(da_att16, w_o_att16, px, o_att, te)
    dq_rot, dkx, dvx, dkc, dvc = _att_bwd(q16, kx16, kc16, px, dao, o_att, lse, nb, seq, cx, te)
    daq16, gq = _qk_prep_bwd(dq_rot, px, q_norm_w, cos, sin, t_rows, 0, AQ, HQ, 4, seq, te, "q_prep_bwd")
    dak16, gk_lat = _qk_prep_bwd(dkx.reshape(t_rows, HKV * HD), px, k_norm_w, cos, sin, t_rows, 0, AK, HKV, HKV, seq, te,
                                 "k_prep_bwd")
    dcak16, gk_ctx = _qk_prep_bwd(dkc.reshape(c_rows, HKV * HD), px, k_norm_w, None, None, c_rows, t_rows, AK, HKV, HKV,
                                  seq, tm, "kc_prep_bwd")
    dq_f, dk_f, dv_f, dq_b, dk_b, dv_b, ds_f, ds_b, dlg_scan = _ret_bwd(px, lg, do16, hist_f, hist_b, nb, nc)
    dck16, dcv16, dlg_ctx = _ctx_state_bwd(px, lg, ds_f, ds_b, nb, t_rows, cx)
    dp_all = _assemble_lat(rows_all, dk_f, dk_b, dv_f, dv_b, dak16, dvx.reshape(t_rows, HKV * HD), dq_f, dq_b, drg16,
                           daq16, dag16, dmr16, dma16, tm)
    dp_all = _assemble_ctx(dp_all, dck16, dcv16, dcak16, dvc.reshape(c_rows, HKV * HD), t_rows, tm)
    gw_in_t = _matmul(dp_all, h_all, ta=True, tm=1536, tn=D, tk=2304, out_dtype=BF16, name="gw_in")
    in_state, in_started = on_in_grad(gw_in_t)
    dh = proj_back(dp_all, in_started)
    grad_x, dsh, dsc, gnw_lat = _norm_bwd(dh, x2, mod3, norm_w, dxn, 0, seq, 0, te, "norm_bwd")
    dsh_c, dsc_c, gnw_ctx = _norm_bwd(dh, ctx2, mod3, norm_w, None, t_rows, c_rows, nb, tm, "norm_bwd_ctx")

    dlg = (jnp.sum(dlg_scan[:, :, 0], axis=0) + jnp.sum(dlg_ctx[:, :, :2, 0], axis=0).T.reshape(2 * RH)).reshape(1, 2 * RH)
    misc = jnp.concatenate([gq, gk_lat + gk_ctx, dlg, jnp.sum(loss_b[:, 0, 0]).reshape(1, 1),
                            jnp.zeros((1, D - 2 * HD - 2 * RH - 1), F32)], axis=1)
    rows = []
    for b in range(nb):
        rows += [dsh[b], dsc[b], dgate[b]]
    rows += [dsh_c[0], dsc_c[0]] + [c[b:b + 1] for b in range(nb)] + [gnw_lat + gnw_ctx, misc]
    payload = jnp.concatenate(rows + [jnp.zeros((PAY_ROWS - len(rows), D), F32)], axis=0)
    return grad_x.reshape(nb, seq, D), out_state, in_state, payload


def _finish_small(gathered, nb, c_ctx, ret_log2_decay, w_ada16, dev):
    n_dev = gathered.shape[0]
    loc = 3 * D // n_dev
    dmod_all = gathered[:, :3 * nb].reshape(n_dev * nb, 3 * D)
    dmodc_parts = jnp.concatenate([gathered[:, 3 * nb:3 * nb + 2].reshape(n_dev, 2 * D), jnp.zeros((n_dev, D), F32)], axis=1)
    c_all = gathered[:, 3 * nb + 2:4 * nb + 2].reshape(n_dev * nb, D)
    nw_parts = gathered[:, 4 * nb + 2]
    misc_parts = gathered[:, 4 * nb + 3]
    n_rows = n_dev * nb + n_dev
    pad = (-n_rows) % 16
    c_rows = jnp.concatenate([c_all, jnp.broadcast_to(c_ctx.reshape(1, D), (n_dev, D)), jnp.zeros((pad, D), F32)], axis=0)
    dm_rows = jnp.concatenate([dmod_all, dmodc_parts, jnp.zeros((pad, 3 * D), F32)], axis=0)
    dm_loc_rows = lax.dynamic_slice_in_dim(dm_rows, dev * loc, loc, axis=1)
    r_pad = jnp.full((1, D), -1.0, F32).at[:, 2 * HD:2 * HD + 2 * RH].set(ret_log2_decay.reshape(1, 2 * RH))
    gb, gc, gnw, misc, gwa = _small_final(dmod_all, dmodc_parts, c_rows, dm_loc_rows, nw_parts, misc_parts,
                                          c_ctx.reshape(1, D), r_pad, w_ada16)
    return (gb, gc, gnw, misc[:, :HD], misc[:, HD:2 * HD], misc[:, 2 * HD:2 * HD + 2 * RH], gwa,
            misc[0, 2 * HD + 2 * RH])


def kernel(x, c, ctx, c_ctx, norm_w, w_ada, b_ada, w_in, ret_log2_decay, q_norm_w, k_norm_w, w_o_ret, w_o_att, w_out, loss_target, m_c_ctx, m_norm_w, m_w_ada, m_b_ada, m_w_in, m_ret_log2_decay, m_q_norm_w, m_k_norm_w, m_w_o_ret, m_w_o_att, m_w_out, v_c_ctx, v_norm_w, v_w_ada, v_b_ada, v_w_in, v_ret_log2_decay, v_q_norm_w, v_k_norm_w, v_w_o_ret, v_w_o_att, v_w_out):
    nb = x.shape[0]
    mx, my, mc = _mesh_pos()
    dev = 4 * mx + 2 * my + mc
    core = jnp.reshape(mc, (1,)).astype(jnp.int32)
    chip = jnp.reshape(2 * mx + my, (1,)).astype(jnp.int32)

    n_loc = 3 * D // N_DEV
    c8 = jnp.zeros((8, D), F32).at[:nb].set(c).at[nb].set(c_ctx)
    c_land = lax.dynamic_update_slice(lax.empty((N_DEV, 8, D), F32), c8[None], (dev, 0, 0))
    c_state, c_token = _exchange_start([c8[None]], [c_land], _bcast_routes(1), "gather_c_start")
    w_in_t = jnp.transpose(w_in[0])
    in_shard = w_in_t.astype(BF16)
    groups = lax.dynamic_update_slice(lax.empty((N_DEV,) + in_shard.shape, BF16), in_shard[None], (mc, 0, 0))
    groups = _pair_fill(groups, (0,), "gather_in_pair", after=(c_token,))
    _, (c_all,) = _exchange_wait(c_state, groups, "gather_c_wait")
    ada_shard = w_ada[0].astype(BF16)
    b_loc = lax.dynamic_slice(b_ada, (0, dev * n_loc), (1, n_loc))
    mod_cols = _mod_part(c_all.reshape(N_DEV * 8, D), ada_shard, b_loc)
    (mod_all,) = _all_gather([mod_cols], "gather_mod")
    mod = jnp.transpose(lax.dynamic_slice(mod_all, (0, dev * 8, 0), (N_DEV, 8, n_loc)), (1, 0, 2)).reshape(8, 3 * D)
    ada_land = lax.dynamic_update_slice(lax.empty((N_DEV,) + ada_shard.shape, BF16), ada_shard[None], (dev, 0, 0))

    (near_send, near_recv, near_bufs, near_routes), gin_token = _exchange_start(
        [in_shard[None]], [groups], _group_routes((1, 2)), "gather_in_start", after=(mod_all,))
    w_in_groups, wo_states, ada_states = [], [], []
    wo_shards = [w_[0].astype(BF16) for w_ in (w_o_ret, w_o_att, w_out)]
    wo_lands = [lax.dynamic_update_slice(lax.empty((N_DEV,) + s_.shape, BF16), s_[None], (dev, 0, 0)) for s_ in wo_shards]

    def _state(send, recv, src, groups, routes):
        return send, recv, [src, groups], routes

    def proj_in(h_all):
        src, groups = near_bufs
        px = _in_proj_group(h_all, groups, 0, 1, chip, None, (gin_token,), "in_proj_0")
        (src,), (groups,) = _exchange_wait(_state(near_send, near_recv, src, groups, near_routes), px,
                                           "gather_in_wait_near")
        groups = _pair_fill(groups, (1, 2), "gather_in_fill_near")
        (far_send, far_recv, (src, groups), far_routes), far_token = _exchange_start(
            [src], [groups], _group_routes((3,)), "gather_in_start_far")
        wo_state, wo_token = _exchange_start([s_[None] for s_ in wo_shards], wo_lands, _bcast_routes(3),
                                             "gather_wo_start", after=(far_token,))
        wo_states.append(wo_state)
        ada_state, ada_token = _exchange_start([ada_shard[None]], [ada_land], _bcast_routes(1), "gather_ada_start",
                                               after=(wo_token,))
        ada_states.append(ada_state)
        px = _in_proj_group(h_all, groups, 1, 2, chip, px, (ada_token,), "in_proj_near")
        (src,), (groups,) = _exchange_wait(_state(far_send, far_recv, src, groups, far_routes), px,
                                           "gather_in_wait_far")
        groups = _pair_fill(groups, (3,), "gather_in_fill_far")
        px = _in_proj_group(h_all, groups, 3, 1, chip, px, (), "in_proj_far")
        w_in_groups.append(groups)
        return px

    def proj_back(dp_all, after):
        return _d_h_groups(dp_all, w_in_groups[0], chip, after)

    def get_w_o(after):
        _, (l_ret, l_att, l_out) = _exchange_wait(wo_states[0], after, "gather_wo_wait")
        return l_ret.reshape(RH * DV, D), l_att.reshape(D, D), l_out.reshape(D, D)

    def on_out_grads(grads):
        parts = [g_.reshape(N_DEV, g_.shape[0] // N_DEV, D) for g_ in grads]
        state, token = _reduce_scatter_start(parts, core, "rs_out")
        return state, (token,)

    def on_in_grad(grad):
        state, token = _reduce_scatter_start([grad.reshape(N_DEV, IN_COLS // N_DEV, D)], core, "rs_in")
        return state, (token,)

    grad_x, out_state, in_state, payload = _local_step(
        x, c, ctx, norm_w, ret_log2_decay, q_norm_w, k_norm_w, loss_target,
        mod, proj_in, proj_back, get_w_o, on_out_grads, on_in_grad, started=(gin_token,))

    pay_land = lax.dynamic_update_slice(lax.empty((N_DEV,) + payload.shape, F32), payload[None], (dev, 0, 0))
    pay_state, pay_token = _exchange_start([payload[None]], [pay_land], _bcast_routes(1), "gather_small_start")

    out_res = _reduce_scatter_finish(out_state, pay_token, chip,
                                     [(w_[0], m_[0], v_[0]) for w_, m_, v_ in ((w_o_ret, m_w_o_ret, v_w_o_ret),
                                                                                (w_o_att, m_w_o_att, v_w_o_att),
                                                                                (w_out, m_w_out, v_w_out))], "rs_out")
    (in_res,) = _reduce_scatter_finish(in_state, out_res[0][0], chip,
                                       [(w_in_t, jnp.transpose(m_w_in[0]), jnp.transpose(v_w_in[0]))], "rs_in")

    _, (gathered,) = _exchange_wait(pay_state, in_res[0], "gather_small_wait")
    _, (l_ada,) = _exchange_wait(ada_states[0], gathered, "gather_ada_wait")
    w_ada16 = jnp.transpose(l_ada, (1, 0, 2)).reshape(D, 3 * D)
    gb, gc, gnw, gq, gk, gr, gwa, loss = _finish_small(gathered, nb, c_ctx, ret_log2_decay, w_ada16, dev)
    big = {4: [jnp.transpose(r)[None] for r in in_res]}
    for i, res in zip((8, 9, 10), out_res):
        big[i] = [r[None] for r in res]
    small_g = {0: gc.reshape(c_ctx.shape), 1: gnw, 2: gwa[None], 3: gb, 5: gr.reshape(ret_log2_decay.shape), 6: gq, 7: gk}
    weights = [c_ctx, norm_w, w_ada, b_ada, w_in, ret_log2_decay, q_norm_w, k_norm_w, w_o_ret, w_o_att, w_out]
    ms = [m_c_ctx, m_norm_w, m_w_ada, m_b_ada, m_w_in, m_ret_log2_decay, m_q_norm_w, m_k_norm_w, m_w_o_ret, m_w_o_att, m_w_out]
    vs = [v_c_ctx, v_norm_w, v_w_ada, v_b_ada, v_w_in, v_ret_log2_decay, v_q_norm_w, v_k_norm_w, v_w_o_ret, v_w_o_att, v_w_out]
    grads, deltas, new_ms, new_vs = [], [], [], []
    for i, (w, m, v) in enumerate(zip(weights, ms, vs)):
        if i in big:
            res = big[i]
        else:
            shape2 = (-1, w.shape[-1])
            g = small_g[i]
            res = [g] + [r.reshape(w.shape) for r in _adamw(w.reshape(shape2), g.reshape(shape2), m.reshape(shape2),
                                                             v.reshape(shape2), "adamw_%d" % i)]
        for lst, r in zip((grads, deltas, new_ms, new_vs), res):
            lst.append(r)
    return (loss, grad_x, *grads, *deltas, *new_ms, *new_vs)
```

```python
import numpy as np
import jax
import jax.numpy as jnp
from jax import lax
from jax.experimental import pallas as pl
from jax.experimental.pallas import tpu as pltpu

F32 = jnp.float32
BF16 = jnp.bfloat16

D = 1024
RH, DK, DV, CH = 4, 256, 512, 256
HQ, HKV, HD = 8, 2, 128
GRID_W = 64
ROPE_THETA = 10000.0
EPS = 1e-6
RK, RV, AK, AV, RQ, RG, AQ, AG, MR, MA = 0, 1024, 3072, 3328, 3584, 4608, 6656, 7680, 8704, 9728
IN_COLS = 10752
KV_COLS = 3584
N_DEV = 8
LR, B1, B2, ADAM_EPS, WD, STEP = 0.001, 0.9, 0.999, 1e-08, 0.01, 10
PAY_ROWS = 16
VMEM_LIMIT = 56 * 1024 * 1024
MESH_T = pl.DeviceIdType.MESH

NT = (((1,), (1,)), ((), ()))
TN = (((0,), (0,)), ((), ()))
SM_C = (HD ** -0.5) * float(np.log2(np.e))


def _params(sem):
    return pltpu.CompilerParams(dimension_semantics=sem, vmem_limit_bytes=VMEM_LIMIT)


def _pick(n, target, mult=8):
    best = None
    for t in range(mult, min(n, target) + 1, mult):
        if n % t == 0:
            best = t
    return best or n


def _dot(a, b, dn=None):
    if dn is None:
        return jnp.dot(a, b, preferred_element_type=F32)
    return lax.dot_general(a, b, dn, preferred_element_type=F32)


def _sig(v):
    return jax.nn.sigmoid(v)


def _silu(v):
    return v * _sig(v)


def _dsilu(v):
    s = _sig(v)
    return s * (1.0 + v * (1.0 - s))


def _sds(shape, dtype):
    return jax.ShapeDtypeStruct(shape, dtype)


def _matmul(a, b, *, ta=False, tb=False, tm, tn, tk, out_dtype, name, after=()):
    m = a.shape[1] if ta else a.shape[0]
    kdim = a.shape[0] if ta else a.shape[1]
    n = b.shape[0] if tb else b.shape[1]
    tm, tn, tk = _pick(m, tm, 128), _pick(n, tn, 128), _pick(kdim, tk, 128)
    nk = kdim // tk
    dn = (((0 if ta else 1,), (1 if tb else 0,)), ((), ()))

    def body(a_ref, b_ref, *rest):
        o_ref, acc_ref = rest[-2:]
        k = pl.program_id(2)
        part = _dot(a_ref[...].astype(BF16), b_ref[...].astype(BF16), dn)
        if nk == 1:
            o_ref[...] = part.astype(o_ref.dtype)
        else:
            @pl.when(k == 0)
            def _():
                acc_ref[...] = part

            @pl.when(k > 0)
            def _():
                acc_ref[...] += part

            @pl.when(k == nk - 1)
            def _():
                o_ref[...] = acc_ref[...].astype(o_ref.dtype)

    a_spec = pl.BlockSpec((tk, tm), lambda i, j, k: (k, i)) if ta else pl.BlockSpec((tm, tk), lambda i, j, k: (i, k))
    b_spec = pl.BlockSpec((tn, tk), lambda i, j, k: (j, k)) if tb else pl.BlockSpec((tk, tn), lambda i, j, k: (k, j))
    return pl.pallas_call(
        body, name=name, grid=(m // tm, n // tn, nk),
        in_specs=[a_spec, b_spec] + [pl.BlockSpec(memory_space=pl.ANY)] * len(after),
        out_specs=pl.BlockSpec((tm, tn), lambda i, j, k: (i, j)), out_shape=_sds((m, n), out_dtype),
        scratch_shapes=[pltpu.VMEM((tm, tn) if nk > 1 else (8, 128), F32)],
        compiler_params=_params(("parallel", "parallel", "arbitrary")),
    )(a, b, *after)


def _log_gamma(r):
    rp = jnp.full((8, 128), -1.0, F32).at[:2, :RH].set(r.reshape(2, RH))

    def body(r_ref, o_ref):
        o_ref[...] = jnp.log1p(-jnp.exp2(r_ref[...]))

    out = pl.pallas_call(body, name="log_gamma", out_shape=_sds((8, 128), F32))(rp)
    return out[:2, :RH]


def _mod_part(c_rows, w_ada_loc16, b_loc):
    def body(c_ref, w_ref, b_ref, o_ref):
        o_ref[...] = _dot(_silu(c_ref[...]).astype(BF16), w_ref[...]) + b_ref[...]

    return pl.pallas_call(
        body, name="mod_part", out_shape=_sds((c_rows.shape[0], w_ada_loc16.shape[1]), F32),
    )(c_rows, w_ada_loc16, b_loc)


def _norm_fwd(x2, mod3, norm_w, rows_all, row_off, rows_per_group, group0, h_prev, tm, name, after=()):
    rows = x2.shape[0]
    rb0 = row_off // tm
    bpg = rows_per_group // tm

    def body(*refs):
        x_ref, sh_ref, sc_ref, nw_ref, o_ref = refs[-5:]
        xv = x_ref[...]
        r = lax.rsqrt(jnp.mean(xv * xv, axis=-1, keepdims=True) + EPS)
        o_ref[...] = ((xv * r) * nw_ref[...] * (1.0 + sc_ref[...]) + sh_ref[...]).astype(BF16)

    in_specs = [pl.BlockSpec((tm, D), lambda i: (i, 0)),
                pl.BlockSpec((None, 1, D), lambda i: (group0 + i // bpg, 0, 0)),
                pl.BlockSpec((None, 1, D), lambda i: (group0 + i // bpg, 0, 1)),
                pl.BlockSpec((1, D), lambda i: (0, 0))]
    in_specs = [pl.BlockSpec(memory_space=pl.ANY)] * len(after) + in_specs
    args = list(after) + [x2, mod3, mod3, norm_w]
    alias = {}
    if h_prev is not None:
        in_specs.insert(0, pl.BlockSpec(memory_space=pl.ANY))
        args.insert(0, h_prev)
        alias = {0: 0}
    return pl.pallas_call(
        body, name=name, grid=(rows // tm,), in_specs=in_specs,
        out_specs=pl.BlockSpec((tm, D), lambda i: (rb0 + i, 0)), out_shape=_sds((rows_all, D), BF16),
        input_output_aliases=alias, compiler_params=_params(("parallel",)),
    )(*args)


def _decays(lg, fwd):
    ii = lax.broadcasted_iota(jnp.int32, (CH, CH), 0)
    jj = lax.broadcasted_iota(jnp.int32, (CH, CH), 1)
    ri = lax.broadcasted_iota(jnp.int32, (CH, 1), 0).astype(F32)
    rel = (ii - jj) if fwd else (jj - ii)
    relf = jnp.maximum(rel, 0).astype(F32)
    mask = jnp.where(rel >= 0, jnp.exp(lg * relf), 0.0)
    qe = (ri + 1.0) if fwd else (CH - ri)
    ke = (CH - 1.0 - ri) if fwd else ri
    return mask, relf, jnp.exp(lg * qe), qe, jnp.exp(lg * ke), ke


def _wide_specs(rowf):
    return [pl.BlockSpec((CH, 2 * DK), lambda b, c: (rowf(b, c), RQ // (2 * DK))),
            pl.BlockSpec((CH, 2 * DK), lambda b, c: (rowf(b, c), RQ // (2 * DK) + 1)),
            pl.BlockSpec((CH, RH * DK), lambda b, c: (rowf(b, c), RK // (RH * DK))),
            pl.BlockSpec((CH, 2 * DV), lambda b, c: (rowf(b, c), RV // (2 * DV))),
            pl.BlockSpec((CH, 2 * DV), lambda b, c: (rowf(b, c), RV // (2 * DV) + 1))]


def _head_qkv(refs, h):
    q0, q1, k, v0, v1 = refs
    lo = h % 2
    q = (q0, q1)[h // 2][:, lo * DK:(lo + 1) * DK].astype(F32)
    kk = k[:, h * DK:(h + 1) * DK].astype(F32) * (DK ** -0.5)
    v16 = (v0, v1)[h // 2][:, lo * DV:(lo + 1) * DV].astype(BF16)
    return q, kk, v16


def _ctx_state(px, lg, nb, t_rows, cx):
    rb = t_rows // cx

    def body(lg_ref, k_ref, v_ref, sf_ref, sb_ref):
        h = pl.program_id(1)
        pos = lax.broadcasted_iota(jnp.int32, (cx, 1), 0).astype(F32)
        k = k_ref[...].astype(F32) * (DK ** -0.5)
        v16 = v_ref[...].astype(BF16)
        wf = jnp.exp(lg_ref[0, h] * (cx - 1.0 - pos))
        wb = jnp.exp(lg_ref[1, h] * pos)
        sf_ref[...] = _dot((k * wf).astype(BF16), v16, TN)
        sb_ref[...] = _dot((k * wb).astype(BF16), v16, TN)

    st = pl.BlockSpec((None, None, DK, DV), lambda b, h: (b, h, 0, 0))
    return pl.pallas_call(
        body, name="ctx_state", grid=(nb, RH),
        in_specs=[pl.BlockSpec(memory_space=pltpu.SMEM),
                  pl.BlockSpec((cx, DK), lambda b, h: (rb + b, RK // DK + h)),
                  pl.BlockSpec((cx, DV), lambda b, h: (rb + b, RV // DV + h))],
        out_specs=[st, st], out_shape=[_sds((nb, RH, DK, DV), F32)] * 2,
        compiler_params=_params(("parallel", "parallel")),
    )(lg, px, px)


def _ret_fwd(px, lg, s0f, s0b, nb, nc):
    t_rows = nb * nc * CH

    def body(lg_ref, *refs):
        ins = (refs[0:5], refs[5:10])
        s0f_ref, s0b_ref, of_ref, ob_ref, hf_ref, hb_ref, sf, sb = refs[10:]
        c = pl.program_id(1)

        @pl.when(c == 0)
        def _():
            sf[...] = s0f_ref[...]
            sb[...] = s0b_ref[...]

        for d, (o_ref, h_ref, s) in enumerate(((of_ref, hf_ref, sf), (ob_ref, hb_ref, sb))):
            for h in range(RH):
                lg_d = lg_ref[d, h]
                mask, _, qd, _, kd, _ = _decays(lg_d, d == 0)
                q, k, v16 = _head_qkv(ins[d], h)
                a = _dot(q.astype(BF16), k.astype(BF16), NT)
                st = s[h]
                st16 = st.astype(BF16)
                h_ref[h] = st16
                o = _dot((a * mask).astype(BF16), v16) + _dot((q * qd).astype(BF16), st16)
                o_ref[:, h * DV:(h + 1) * DV] = o.astype(BF16)
                s[h] = st * jnp.exp(lg_d * CH) + _dot((k * kd).astype(BF16), v16, TN)

    def fw(b, c):
        return b * nc + c

    def bw(b, c):
        return b * nc + nc - 1 - c

    st = pl.BlockSpec((None, RH, DK, DV), lambda b, c: (b, 0, 0, 0))
    in_specs = [pl.BlockSpec(memory_space=pltpu.SMEM)] + _wide_specs(fw) + _wide_specs(bw) + [st, st]
    out_specs = [pl.BlockSpec((CH, RH * DV), lambda b, c: (fw(b, c), 0)),
                 pl.BlockSpec((CH, RH * DV), lambda b, c: (bw(b, c), 0)),
                 pl.BlockSpec((None, None, RH, DK, DV), lambda b, c: (b, c, 0, 0, 0)),
                 pl.BlockSpec((None, None, RH, DK, DV), lambda b, c: (b, nc - 1 - c, 0, 0, 0))]
    return pl.pallas_call(
        body, name="ret_fwd", grid=(nb, nc), in_specs=in_specs, out_specs=out_specs,
        out_shape=[_sds((t_rows, RH * DV), BF16)] * 2 + [_sds((nb, nc, RH, DK, DV), BF16)] * 2,
        scratch_shapes=[pltpu.VMEM((RH, DK, DV), F32), pltpu.VMEM((RH, DK, DV), F32)],
        compiler_params=_params(("parallel", "arbitrary")),
    )(lg, *([px] * 10), s0f, s0b)


def _rope_tables(seq):
    rows = seq // GRID_W
    row = np.repeat(np.arange(rows, dtype=np.float32), GRID_W)
    col = np.tile(np.arange(GRID_W, dtype=np.float32), rows)
    half = HD // 2
    freqs = (ROPE_THETA ** (-np.arange(0, half, 2, dtype=np.float32) / half)).astype(np.float32)
    ang = np.concatenate([row[:, None] * freqs, col[:, None] * freqs], axis=-1).astype(np.float32)
    cos = np.repeat(np.cos(ang), 2, axis=-1).astype(np.float32)
    sin = np.repeat(np.sin(ang), 2, axis=-1).astype(np.float32)
    sign = np.tile(np.array([-1.0, 1.0], np.float32), HD // 2)
    return jnp.asarray(cos), jnp.asarray(sin * sign)


def _swap_pairs(v):
    lane = lax.broadcasted_iota(jnp.int32, v.shape, 1)
    return jnp.where((lane & 1) == 0, pltpu.roll(v, HD - 1, 1), pltpu.roll(v, 1, 1))


def _qk_prep(px, nw, cos, sin, rows, row_off, col_off, heads, hb, seq, tm, name):
    rope = cos is not None
    rb0 = row_off // tm
    pb = seq // tm if rope else 1
    bw = hb * HD

    def body(*refs):
        if rope:
            x_ref, w_ref, c_ref, s_ref, o_ref = refs
        else:
            x_ref, w_ref, o_ref = refs
        for h in range(hb):
            sl = slice(h * HD, (h + 1) * HD)
            xv = x_ref[:, sl].astype(F32)
            r = lax.rsqrt(jnp.mean(xv * xv, axis=-1, keepdims=True) + EPS)
            t = (xv * r) * w_ref[...]
            if rope:
                t = t * c_ref[...] + _swap_pairs(t) * s_ref[...]
            o_ref[:, sl] = t.astype(BF16)

    in_specs = [pl.BlockSpec((tm, bw), lambda i, j: (rb0 + i, col_off // bw + j)),
                pl.BlockSpec((1, HD), lambda i, j: (0, 0))]
    args = [px, nw]
    if rope:
        in_specs += [pl.BlockSpec((tm, HD), lambda i, j: (i % pb, 0))] * 2
        args += [cos, sin]
    return pl.pallas_call(
        body, name=name, grid=(rows // tm, heads // hb), in_specs=in_specs,
        out_specs=pl.BlockSpec((tm, bw), lambda i, j: (i, j)), out_shape=_sds((rows, heads * HD), BF16),
        compiler_params=_params(("parallel", "parallel")),
    )(*args)


def _att_fwd(q16, kx16, kc16, px, nb, seq, cx, tq):
    t_rows = nb * seq
    nq = seq // tq
    rep = HQ // HKV
    gw = rep * HD

    def body(q_ref, kx_ref, kc_ref, vx_ref, vc_ref, g_ref, o_ref, y_ref, l_ref):
        kx = kx_ref[...]
        kc = kc_ref[...]
        vx = vx_ref[...].astype(BF16)
        vc = vc_ref[...].astype(BF16)
        l_ref[...] = jnp.zeros_like(l_ref)
        for r in range(rep):
            sl = slice(r * HD, (r + 1) * HD)
            q = q_ref[:, sl]
            s1 = _dot(q, kx, NT)
            s2 = _dot(q, kc, NT)
            m = jnp.maximum(jnp.max(s1, axis=-1, keepdims=True), jnp.max(s2, axis=-1, keepdims=True))
            e1 = jnp.exp2((s1 - m) * SM_C)
            e2 = jnp.exp2((s2 - m) * SM_C)
            tot = jnp.sum(e1, axis=-1, keepdims=True) + jnp.sum(e2, axis=-1, keepdims=True)
            o = (_dot(e1.astype(BF16), vx) + _dot(e2.astype(BF16), vc)) * (1.0 / tot)
            o_ref[:, sl] = o
            y_ref[:, sl] = (o * _silu(g_ref[:, sl].astype(F32))).astype(BF16)
            l_ref[:, r:r + 1] = m * SM_C + jnp.log(tot) * float(np.log2(np.e))

    qblk = pl.BlockSpec((tq, gw), lambda b, g, i: (b * nq + i, g))
    return pl.pallas_call(
        body, name="att_fwd", grid=(nb, HKV, nq),
        in_specs=[qblk,
                  pl.BlockSpec((seq, HD), lambda b, g, i: (b, g)),
                  pl.BlockSpec((cx, HD), lambda b, g, i: (b, g)),
                  pl.BlockSpec((seq, HD), lambda b, g, i: (b, AV // HD + g)),
                  pl.BlockSpec((cx, HD), lambda b, g, i: (t_rows // cx + b, AV // HD + g)),
                  pl.BlockSpec((tq, gw), lambda b, g, i: (b * nq + i, AG // gw + g))],
        out_specs=[qblk, qblk, pl.BlockSpec((tq, 128), lambda b, g, i: (b * nq + i, g))],
        out_shape=[_sds((t_rows, D), F32), _sds((t_rows, D), BF16), _sds((t_rows, HKV * 128), F32)],
        compiler_params=_params(("parallel", "parallel", "parallel")),
    )(q16, kx16, kc16, px, px, px)


def _gate_specs(tm, col0):
    hw = D // 2
    return [pl.BlockSpec((tm, hw), lambda i: (i, col0 // hw)), pl.BlockSpec((tm, hw), lambda i: (i, col0 // hw + 1))]


def _merge(o_f, o_b, yatt16, px, w_o_ret16, w_o_att16, tm):
    t_rows = o_f.shape[0]
    hw = D // 2

    def body(of_ref, ob_ref, g0, g1, g2, g3, wr_ref, ya_ref, wa_ref, mr0, mr1, ma0, ma1, yr_ref, ar_ref, aa_ref, y_ref):
        for h, g_ref in enumerate((g0, g1, g2, g3)):
            sl = slice(h * DV, (h + 1) * DV)
            o = of_ref[:, sl].astype(F32) + ob_ref[:, sl].astype(F32)
            r = lax.rsqrt(jnp.mean(o * o, axis=-1, keepdims=True) + EPS)
            yr_ref[:, sl] = ((o * r) * _silu(g_ref[...].astype(F32))).astype(BF16)
        ar = _dot(yr_ref[...], wr_ref[...])
        aa = _dot(ya_ref[...], wa_ref[...])
        ar_ref[...] = ar.astype(BF16)
        aa_ref[...] = aa.astype(BF16)
        for j, (mr_ref, ma_ref) in enumerate(((mr0, ma0), (mr1, ma1))):
            sl = slice(j * hw, (j + 1) * hw)
            y_ref[:, sl] = (_sig(mr_ref[...].astype(F32)) * ar[:, sl]
                            + _sig(ma_ref[...].astype(F32)) * aa[:, sl]).astype(BF16)

    def gate(h):
        return pl.BlockSpec((tm, DV), lambda i: (i, RG // DV + h))

    row = pl.BlockSpec((tm, D), lambda i: (i, 0))
    wide = pl.BlockSpec((tm, RH * DV), lambda i: (i, 0))
    return pl.pallas_call(
        body, name="merge", grid=(t_rows // tm,),
        in_specs=[wide, wide] + [gate(h) for h in range(RH)]
        + [pl.BlockSpec((RH * DV, D), lambda i: (0, 0)), row, pl.BlockSpec((D, D), lambda i: (0, 0))]
        + _gate_specs(tm, MR) + _gate_specs(tm, MA),
        out_specs=[wide, row, row, row],
        out_shape=[_sds((t_rows, RH * DV), BF16)] + [_sds((t_rows, D), BF16)] * 3,
        compiler_params=_params(("parallel",)),
    )(o_f, o_b, *([px] * RH), w_o_ret16, yatt16, w_o_att16, px, px, px, px)


def _outproj(y16, w_out16, x2, tgt, mod3, nb, seq, tm):
    t_rows = nb * seq
    bpb = seq // tm

    def body(y_ref, w_ref, x_ref, t_ref, g_ref, dxn_ref, dout_ref, dg_ref, loss_ref):
        i = pl.program_id(1)
        out = _dot(y_ref[...], w_ref[...])
        gate = g_ref[...]
        diff = x_ref[...] + gate * out - t_ref[...]
        dxn = diff * (1.0 / D)
        dxn_ref[...] = dxn
        dout_ref[...] = (gate * dxn).astype(BF16)
        dg = jnp.sum(dxn * out, axis=0, keepdims=True)
        ls = jnp.broadcast_to(jnp.sum(diff * diff) * (0.5 / D), (1, 128))

        @pl.when(i == 0)
        def _():
            dg_ref[...] = dg
            loss_ref[...] = ls

        @pl.when(i > 0)
        def _():
            dg_ref[...] += dg
            loss_ref[...] += ls

    row = pl.BlockSpec((tm, D), lambda b, i: (b * bpb + i, 0))
    return pl.pallas_call(
        body, name="outproj", grid=(nb, bpb),
        in_specs=[row, pl.BlockSpec((D, D), lambda b, i: (0, 0)), row, row,
                  pl.BlockSpec((None, 1, D), lambda b, i: (b, 0, 2))],
        out_specs=[row, row, pl.BlockSpec((None, 1, D), lambda b, i: (b, 0, 0)),
                   pl.BlockSpec((None, 1, 128), lambda b, i: (b, 0, 0))],
        out_shape=[_sds((t_rows, D), F32), _sds((t_rows, D), BF16), _sds((nb, 1, D), F32), _sds((nb, 1, 128), F32)],
        compiler_params=_params(("parallel", "arbitrary")),
    )(y16, w_out16, x2, tgt, mod3)


def _bwd_branches(dout16, w_out16, w_o_ret16, w_o_att16, px, a_ret, a_att, o_f, o_b, o_att, tm):
    t_rows = dout16.shape[0]
    hw = D // 2

    def body(do_ref, wo_ref, wr_ref, wa_ref, mr0, mr1, ma0, ma1, ar_ref, aa_ref, rg0, rg1, rg2, rg3, of_ref, ob_ref,
             ag0, ag1, oa_ref, dar_ref, daa_ref, dmr_ref, dma_ref, dor_ref, drg_ref, dao_ref, dag_ref):
        dy_all = _dot(do_ref[...], wo_ref[...], NT)
        for j, (mr_ref, ma_ref) in enumerate(((mr0, ma0), (mr1, ma1))):
            sl = slice(j * hw, (j + 1) * hw)
            dy = dy_all[:, sl]
            sr = _sig(mr_ref[...].astype(F32))
            sa = _sig(ma_ref[...].astype(F32))
            dar_ref[:, sl] = (dy * sr).astype(BF16)
            daa_ref[:, sl] = (dy * sa).astype(BF16)
            dmr_ref[:, sl] = (dy * ar_ref[:, sl].astype(F32) * sr * (1.0 - sr)).astype(BF16)
            dma_ref[:, sl] = (dy * aa_ref[:, sl].astype(F32) * sa * (1.0 - sa)).astype(BF16)
        da_ret = dar_ref[...]
        for h, g_ref in enumerate((rg0, rg1, rg2, rg3)):
            sl = slice(h * DV, (h + 1) * DV)
            dy = _dot(da_ret, wr_ref[sl, :], NT)
            g = g_ref[...].astype(F32)
            o = of_ref[:, sl].astype(F32) + ob_ref[:, sl].astype(F32)
            r = lax.rsqrt(jnp.mean(o * o, axis=-1, keepdims=True) + EPS)
            on = o * r
            sg = _sig(g)
            don = dy * (g * sg)
            drg_ref[:, sl] = (dy * on * (sg * (1.0 + g * (1.0 - sg)))).astype(BF16)
            dor_ref[:, sl] = (r * (don - on * jnp.mean(on * don, axis=-1, keepdims=True))).astype(BF16)
        dy_all = _dot(daa_ref[...], wa_ref[...], NT)
        for j, g_ref in enumerate((ag0, ag1)):
            sl = slice(j * hw, (j + 1) * hw)
            dy = dy_all[:, sl]
            g = g_ref[...].astype(F32)
            sg = _sig(g)
            dao_ref[:, sl] = dy * (g * sg)
            dag_ref[:, sl] = (dy * oa_ref[:, sl] * (sg * (1.0 + g * (1.0 - sg)))).astype(BF16)

    def gate(h):
        return pl.BlockSpec((tm, DV), lambda i: (i, RG // DV + h))

    def whole(rows):
        return pl.BlockSpec((rows, D), lambda i: (0, 0))

    row = pl.BlockSpec((tm, D), lambda i: (i, 0))
    wide = pl.BlockSpec((tm, RH * DV), lambda i: (i, 0))
    return pl.pallas_call(
        body, name="bwd_branches", grid=(t_rows // tm,),
        in_specs=[row, whole(D), whole(RH * DV), whole(D)] + _gate_specs(tm, MR) + _gate_specs(tm, MA) + [row, row]
        + [gate(h) for h in range(RH)] + [wide, wide] + _gate_specs(tm, AG) + [row],
        out_specs=[row] * 4 + [wide, wide, row, row],
        out_shape=[_sds((t_rows, D), BF16)] * 4 + [_sds((t_rows, RH * DV), BF16)] * 2
        + [_sds((t_rows, D), F32), _sds((t_rows, D), BF16)],
        compiler_params=_params(("parallel",)),
    )(dout16, w_out16, w_o_ret16, w_o_att16, px, px, px, px, a_ret, a_att, *([px] * RH), o_f, o_b, px, px, o_att)


def _att_bwd(q16, kx16, kc16, px, dao, o_att, lse, nb, seq, cx, tq, after=()):
    t_rows = nb * seq
    nq = seq // tq
    rep = HQ // HKV
    gw = rep * HD
    scale = HD ** -0.5

    def body(q_ref, kx_ref, kc_ref, vx_ref, vc_ref, dao_ref, o_ref, l_ref, *rest):
        dq_ref, dkx_ref, dvx_ref, dkc_ref, dvc_ref = rest[-5:]
        i = pl.program_id(2)
        kx = kx_ref[...]
        kc = kc_ref[...]
        vx = vx_ref[...].astype(BF16)
        vc = vc_ref[...].astype(BF16)
        dkx = jnp.zeros((seq, HD), F32)
        dvx = jnp.zeros((seq, HD), F32)
        dkc = jnp.zeros((cx, HD), F32)
        dvc = jnp.zeros((cx, HD), F32)
        for r in range(rep):
            sl = slice(r * HD, (r + 1) * HD)
            q = q_ref[:, sl]
            lr = l_ref[:, r:r + 1]
            p1 = jnp.exp2(_dot(q, kx, NT) * SM_C - lr)
            p2 = jnp.exp2(_dot(q, kc, NT) * SM_C - lr)
            da = dao_ref[:, sl]
            da16 = da.astype(BF16)
            delta = jnp.sum(da * o_ref[:, sl], axis=-1, keepdims=True)
            ds1 = (p1 * (_dot(da16, vx, NT) - delta)).astype(BF16)
            ds2 = (p2 * (_dot(da16, vc, NT) - delta)).astype(BF16)
            dq_ref[:, sl] = (_dot(ds1, kx) + _dot(ds2, kc)) * scale
            dkx += _dot(ds1, q, TN)
            dkc += _dot(ds2, q, TN)
            dvx += _dot(p1.astype(BF16), da16, TN)
            dvc += _dot(p2.astype(BF16), da16, TN)
        dkx = dkx * scale
        dkc = dkc * scale

        @pl.when(i == 0)
        def _():
            dkx_ref[...] = dkx
            dvx_ref[...] = dvx
            dkc_ref[...] = dkc
            dvc_ref[...] = dvc

        @pl.when(i > 0)
        def _():
            dkx_ref[...] += dkx
            dvx_ref[...] += dvx
            dkc_ref[...] += dkc
            dvc_ref[...] += dvc

    qblk = pl.BlockSpec((tq, gw), lambda b, g, i: (b * nq + i, g))
    kxb = pl.BlockSpec((None, seq, HD), lambda b, g, i: (b, 0, g))
    kcb = pl.BlockSpec((None, cx, HD), lambda b, g, i: (b, 0, g))
    return pl.pallas_call(
        body, name="att_bwd", grid=(nb, HKV, nq),
        in_specs=[qblk,
                  pl.BlockSpec((seq, HD), lambda b, g, i: (b, g)),
                  pl.BlockSpec((cx, HD), lambda b, g, i: (b, g)),
                  pl.BlockSpec((seq, HD), lambda b, g, i: (b, AV // HD + g)),
                  pl.BlockSpec((cx, HD), lambda b, g, i: (t_rows // cx + b, AV // HD + g)),
                  qblk, qblk, pl.BlockSpec((tq, 128), lambda b, g, i: (b * nq + i, g))]
        + [pl.BlockSpec(memory_space=pl.ANY)] * len(after),
        out_specs=[qblk, kxb, kxb, kcb, kcb],
        out_shape=[_sds((t_rows, D), F32), _sds((nb, seq, HKV * HD), F32), _sds((nb, seq, HKV * HD), F32),
                   _sds((nb, cx, HKV * HD), F32), _sds((nb, cx, HKV * HD), F32)],
        compiler_params=_params(("parallel", "parallel", "arbitrary")),
    )(q16, kx16, kc16, px, px, dao, o_att, lse, *after)


def _qk_prep_bwd(dt, px, nw, cos, sin, rows, row_off, col_off, heads, hb, seq, tm, name):
    rope = cos is not None
    rb0 = row_off // tm
    pb = seq // tm if rope else 1
    bw = hb * HD

    def body(*refs):
        if rope:
            d_ref, x_ref, w_ref, c_ref, s_ref, dx_ref, dw_ref = refs
        else:
            d_ref, x_ref, w_ref, dx_ref, dw_ref = refs
        first = jnp.logical_and(pl.program_id(0) == 0, pl.program_id(1) == 0)
        dw = jnp.zeros((1, HD), F32)
        for h in range(hb):
            sl = slice(h * HD, (h + 1) * HD)
            dtv = d_ref[:, sl]
            if rope:
                dtv = dtv * c_ref[...] + _swap_pairs(dtv * s_ref[...])
            xv = x_ref[:, sl].astype(F32)
            r = lax.rsqrt(jnp.mean(xv * xv, axis=-1, keepdims=True) + EPS)
            xh = xv * r
            dxh = dtv * w_ref[...]
            dx_ref[:, sl] = (r * (dxh - xh * jnp.mean(dxh * xh, axis=-1, keepdims=True))).astype(BF16)
            dw += jnp.sum(dtv * xh, axis=0, keepdims=True)

        @pl.when(first)
        def _():
            dw_ref[...] = dw

        @pl.when(jnp.logical_not(first))
        def _():
            dw_ref[...] += dw

    blk = pl.BlockSpec((tm, bw), lambda i, j: (i, j))
    in_specs = [blk, pl.BlockSpec((tm, bw), lambda i, j: (rb0 + i, col_off // bw + j)),
                pl.BlockSpec((1, HD), lambda i, j: (0, 0))]
    args = [dt, px, nw]
    if rope:
        in_specs += [pl.BlockSpec((tm, HD), lambda i, j: (i % pb, 0))] * 2
        args += [cos, sin]
    return pl.pallas_call(
        body, name=name, grid=(rows // tm, heads // hb), in_specs=in_specs,
        out_specs=[blk, pl.BlockSpec((1, HD), lambda i, j: (0, 0))],
        out_shape=[_sds((rows, heads * HD), BF16), _sds((1, HD), F32)],
        compiler_params=_params(("arbitrary", "arbitrary")),
    )(*args)


def _ret_bwd(px, lg, do16, hist_f, hist_b, nb, nc):
    t_rows = nb * nc * CH

    def body(lg_ref, *refs):
        ins = (refs[0:5], refs[7:12])
        do_refs = (refs[5], refs[12])
        h_refs = (refs[6], refs[13])
        outs = (refs[14:17], refs[17:20])
        ds_outs = (refs[20], refs[21])
        dlg_ref = refs[22]
        dss = (refs[23], refs[24])
        c = pl.program_id(1)

        @pl.when(c == 0)
        def _():
            dss[0][...] = jnp.zeros_like(dss[0])
            dss[1][...] = jnp.zeros_like(dss[1])
            dlg_ref[...] = jnp.zeros_like(dlg_ref)

        for d in range(2):
            dq_ref, dk_ref, dv_ref = outs[d]
            for h in range(RH):
                lg_d = lg_ref[d, h]
                mask, relf, qd, qe, kd, ke = _decays(lg_d, d == 0)
                g_ch = jnp.exp(lg_d * CH)
                q, k, v16 = _head_qkv(ins[d], h)
                q16 = q.astype(BF16)
                k16 = k.astype(BF16)
                do16v = do_refs[d][:, h * DV:(h + 1) * DV]
                st16 = h_refs[d][h]
                dst = dss[d][h]
                dst16 = dst.astype(BF16)
                a = _dot(q16, k16, NT) * mask
                dp = _dot(do16v, v16, NT)
                da16 = (dp * mask).astype(BF16)
                dq_cross = _dot(do16v, st16, NT) * qd
                dq_ref[:, h * DK:(h + 1) * DK] = (_dot(da16, k16) + dq_cross).astype(BF16)
                dk_state = _dot(v16, dst16, NT) * kd
                dk_ref[:, h * DK:(h + 1) * DK] = ((_dot(da16, q16, TN) + dk_state) * (DK ** -0.5)).astype(BF16)
                dv = _dot(a.astype(BF16), do16v, TN) + _dot((k * kd).astype(BF16), dst16)
                dv_ref[:, h * DV:(h + 1) * DV] = dv.astype(BF16)
                dlg = (jnp.sum(relf * a * dp)
                       + jnp.sum(qe * jnp.sum(q * dq_cross, axis=-1, keepdims=True))
                       + jnp.sum(ke * jnp.sum(k * dk_state, axis=-1, keepdims=True))
                       + CH * g_ch * jnp.sum(dst * st16.astype(F32)))
                row = d * RH + h
                dlg_ref[row:row + 1, :] += jnp.broadcast_to(dlg, (1, 128))
                ds_new = g_ch * dst + _dot((q * qd).astype(BF16), do16v, TN)
                dss[d][h] = ds_new

                @pl.when(c == nc - 1)
                def _():
                    ds_outs[d][h] = ds_new

    def fw(b, c):
        return b * nc + nc - 1 - c

    def bw(b, c):
        return b * nc + c

    def rows(rowf, width):
        return pl.BlockSpec((CH, width), lambda b, c: (rowf(b, c), 0))

    def hist(rowf):
        return pl.BlockSpec((None, None, RH, DK, DV), lambda b, c: (b, rowf(0, c), 0, 0, 0))

    st = pl.BlockSpec((None, RH, DK, DV), lambda b, c: (b, 0, 0, 0))
    in_specs = [pl.BlockSpec(memory_space=pltpu.SMEM)]
    out_specs = []
    for rowf in (fw, bw):
        in_specs += _wide_specs(rowf) + [rows(rowf, RH * DV), hist(rowf)]
        out_specs += [rows(rowf, RH * DK), rows(rowf, RH * DK), rows(rowf, RH * DV)]
    out_specs += [st, st, pl.BlockSpec((None, 8, 128), lambda b, c: (b, 0, 0))]
    qk = _sds((t_rows, RH * DK), BF16)
    vv = _sds((t_rows, RH * DV), BF16)
    return pl.pallas_call(
        body, name="ret_bwd", grid=(nb, nc), in_specs=in_specs, out_specs=out_specs,
        out_shape=[qk, qk, vv, qk, qk, vv, _sds((nb, RH, DK, DV), F32), _sds((nb, RH, DK, DV), F32),
                   _sds((nb, 8, 128), F32)],
        scratch_shapes=[pltpu.VMEM((RH, DK, DV), F32), pltpu.VMEM((RH, DK, DV), F32)],
        compiler_params=_params(("parallel", "arbitrary")),
    )(lg, *([px] * 5), do16, hist_f, *([px] * 5), do16, hist_b)


def _ctx_state_bwd(px, lg, ds_f, ds_b, nb, t_rows, cx):
    rb = t_rows // cx

    def body(lg_ref, k_ref, v_ref, dsf_ref, dsb_ref, dk_ref, dv_ref, dlg_ref):
        h = pl.program_id(1)
        pos = lax.broadcasted_iota(jnp.int32, (cx, 1), 0).astype(F32)
        k = k_ref[...].astype(F32) * (DK ** -0.5)
        v16 = v_ref[...].astype(BF16)
        dk = jnp.zeros((cx, DK), F32)
        dv = jnp.zeros((cx, DV), F32)
        dlg_ref[...] = jnp.zeros_like(dlg_ref)
        for d, (ds_ref, e) in enumerate(((dsf_ref, cx - 1.0 - pos), (dsb_ref, pos))):
            w = jnp.exp(lg_ref[d, h] * e)
            ds16 = ds_ref[...].astype(BF16)
            t = _dot(v16, ds16, NT)
            dk += t * w
            dv += _dot((k * w).astype(BF16), ds16)
            dlg = jnp.sum(e * w * jnp.sum(k * t, axis=-1, keepdims=True))
            dlg_ref[d:d + 1, :] = jnp.broadcast_to(dlg, (1, 128))
        dk_ref[...] = (dk * (DK ** -0.5)).astype(BF16)
        dv_ref[...] = dv.astype(BF16)

    st = pl.BlockSpec((None, None, DK, DV), lambda b, h: (b, h, 0, 0))
    return pl.pallas_call(
        body, name="ctx_state_bwd", grid=(nb, RH),
        in_specs=[pl.BlockSpec(memory_space=pltpu.SMEM),
                  pl.BlockSpec((cx, DK), lambda b, h: (rb + b, RK // DK + h)),
                  pl.BlockSpec((cx, DV), lambda b, h: (rb + b, RV // DV + h)), st, st],
        out_specs=[pl.BlockSpec((cx, DK), lambda b, h: (b, h)), pl.BlockSpec((cx, DV), lambda b, h: (b, h)),
                   pl.BlockSpec((None, None, 8, 128), lambda b, h: (b, h, 0, 0))],
        out_shape=[_sds((nb * cx, RH * DK), BF16), _sds((nb * cx, RH * DV), BF16), _sds((nb, RH, 8, 128), F32)],
        compiler_params=_params(("parallel", "parallel")),
    )(lg, px, px, ds_f, ds_b)


def _assemble_lat(rows_all, dk_f, dk_b, dv_f, dv_b, dak16, dvx, dq_f, dq_b, drg16, daq16, dag16, dmr16, dma16, tm):
    t_rows = dk_f.shape[0]

    def body(dkf, dkb, dvf, dvb, dak, dav, dqf, dqb, drg, daq, dag, dmr, dma, o_ref):
        o_ref[:, RK:RK + RH * DK] = (dkf[...].astype(F32) + dkb[...].astype(F32)).astype(BF16)
        o_ref[:, RV:RV + RH * DV] = (dvf[...].astype(F32) + dvb[...].astype(F32)).astype(BF16)
        o_ref[:, AK:AK + HKV * HD] = dak[...]
        o_ref[:, AV:AV + HKV * HD] = dav[...].astype(BF16)
        o_ref[:, RQ:RQ + RH * DK] = (dqf[...].astype(F32) + dqb[...].astype(F32)).astype(BF16)
        o_ref[:, RG:RG + RH * DV] = drg[...]
        o_ref[:, AQ:AQ + D] = daq[...]
        o_ref[:, AG:AG + D] = dag[...]
        o_ref[:, MR:MR + D] = dmr[...]
        o_ref[:, MA:MA + D] = dma[...]

    args = (dk_f, dk_b, dv_f, dv_b, dak16, dvx, dq_f, dq_b, drg16, daq16, dag16, dmr16, dma16)
    return pl.pallas_call(
        body, name="assemble_lat", grid=(t_rows // tm,),
        in_specs=[pl.BlockSpec((tm, a.shape[1]), lambda i: (i, 0)) for a in args],
        out_specs=pl.BlockSpec((tm, IN_COLS), lambda i: (i, 0)), out_shape=_sds((rows_all, IN_COLS), BF16),
        compiler_params=_params(("parallel",)),
    )(*args)


def _assemble_ctx(dp_all, dck16, dcv16, dcak16, dvc, t_rows, tm):
    c_rows = dck16.shape[0]
    rb = t_rows // tm

    def body(_, dck, dcv, dcak, dcav, o_ref):
        o_ref[:, RK:RK + RH * DK] = dck[...]
        o_ref[:, RV:RV + RH * DV] = dcv[...]
        o_ref[:, AK:AK + HKV * HD] = dcak[...]
        o_ref[:, AV:AV + HKV * HD] = dcav[...].astype(BF16)
        o_ref[:, KV_COLS:] = jnp.zeros((tm, IN_COLS - KV_COLS), BF16)

    args = (dck16, dcv16, dcak16, dvc)
    return pl.pallas_call(
        body, name="assemble_ctx", grid=(c_rows // tm,),
        in_specs=[pl.BlockSpec(memory_space=pl.ANY)]
        + [pl.BlockSpec((tm, a.shape[1]), lambda i: (i, 0)) for a in args],
        out_specs=pl.BlockSpec((tm, IN_COLS), lambda i: (rb + i, 0)), out_shape=_sds(dp_all.shape, BF16),
        input_output_aliases={0: 0},
        compiler_params=_params(("parallel",)),
    )(dp_all, *args)


def _norm_bwd(dh, x2, mod3, norm_w, dxn, row_off, rows_per_group, group0, tm, name):
    with_dx = dxn is not None
    rows = x2.shape[0]
    rb0 = row_off // tm
    bpg = rows_per_group // tm
    ngroups = rows // rows_per_group

    def body(*refs):
        if with_dx:
            dh_ref, x_ref, sc_ref, nw_ref, dxn_ref, dx_ref, dsh_ref, dsc_ref, dnw_ref = refs
        else:
            dh_ref, x_ref, sc_ref, nw_ref, dsh_ref, dsc_ref, dnw_ref = refs
        i = pl.program_id(0)
        dhv = dh_ref[...]
        xv = x_ref[...]
        nw = nw_ref[...]
        r = lax.rsqrt(jnp.mean(xv * xv, axis=-1, keepdims=True) + EPS)
        xh = xv * r
        dm = dhv * (1.0 + sc_ref[...])
        dsh = jnp.sum(dhv, axis=0, keepdims=True)
        dsc = jnp.sum(dhv * (xh * nw), axis=0, keepdims=True)
        dnw = jnp.sum(dm * xh, axis=0, keepdims=True)
        if with_dx:
            dxh = dm * nw
            dx_ref[...] = dxn_ref[...] + r * (dxh - xh * jnp.mean(dxh * xh, axis=-1, keepdims=True))

        @pl.when(i % bpg == 0)
        def _():
            dsh_ref[...] = dsh
            dsc_ref[...] = dsc

        @pl.when(i % bpg != 0)
        def _():
            dsh_ref[...] += dsh
            dsc_ref[...] += dsc

        @pl.when(i == 0)
        def _():
            dnw_ref[...] = dnw

        @pl.when(i > 0)
        def _():
            dnw_ref[...] += dnw

    grp = pl.BlockSpec((None, 1, D), lambda i: (i // bpg, 0, 0))
    in_specs = [pl.BlockSpec((tm, D), lambda i: (rb0 + i, 0)), pl.BlockSpec((tm, D), lambda i: (i, 0)),
                pl.BlockSpec((None, 1, D), lambda i: (group0 + i // bpg, 0, 1)),
                pl.BlockSpec((1, D), lambda i: (0, 0))]
    args = [dh, x2, mod3, norm_w]
    out_specs = [grp, grp, pl.BlockSpec((1, D), lambda i: (0, 0))]
    out_shape = [_sds((ngroups, 1, D), F32), _sds((ngroups, 1, D), F32), _sds((1, D), F32)]
    if with_dx:
        in_specs.append(pl.BlockSpec((tm, D), lambda i: (i, 0)))
        args.append(dxn)
        out_specs.insert(0, pl.BlockSpec((tm, D), lambda i: (i, 0)))
        out_shape.insert(0, _sds((rows, D), F32))
    return pl.pallas_call(
        body, name=name, grid=(rows // tm,), in_specs=in_specs, out_specs=out_specs, out_shape=out_shape,
        compiler_params=_params(("arbitrary",)),
    )(*args)


def _small_final(dmod_all, dmodc_parts, c_rows, dm_loc_rows, nw_parts, misc_parts, c_ctx, r_pad, w_ada16):
    loc = dm_loc_rows.shape[1]

    def body(dm_ref, dmc_ref, c_ref, dml_ref, nwp_ref, mp_ref, cc_ref, r_ref, w_ref,
             gb_ref, gc_ref, gnw_ref, misc_ref, gwa_ref):
        dmc = jnp.sum(dmc_ref[...], axis=0, keepdims=True)
        gb_ref[...] = jnp.sum(dm_ref[...], axis=0, keepdims=True) + dmc
        dsc = _dot(jnp.broadcast_to(dmc, (8, 3 * D)).astype(BF16), w_ref[...], NT)[0:1, :]
        gc_ref[...] = dsc * _dsilu(cc_ref[...])
        gnw_ref[...] = jnp.sum(nwp_ref[...], axis=0, keepdims=True)
        misc = jnp.sum(mp_ref[...], axis=0, keepdims=True)
        y = jnp.exp2(r_ref[...])
        lane = lax.broadcasted_iota(jnp.int32, (1, D), 1)
        is_decay = jnp.logical_and(lane >= 2 * HD, lane < 2 * HD + 2 * RH)
        misc_ref[...] = misc * jnp.where(is_decay, -(y * np.float32(np.log(2.0))) / (1.0 - y), 1.0)
        gwa_ref[...] = _dot(_silu(c_ref[...]).astype(BF16), dml_ref[...].astype(BF16), TN)

    return pl.pallas_call(
        body, name="small_final",
        out_shape=[_sds((1, 3 * D), F32), _sds((1, D), F32), _sds((1, D), F32), _sds((1, D), F32), _sds((D, loc), F32)],
        compiler_params=pltpu.CompilerParams(vmem_limit_bytes=VMEM_LIMIT),
    )(dmod_all, dmodc_parts, c_rows, dm_loc_rows, nw_parts, misc_parts, c_ctx, r_pad, w_ada16)


def _adamw_math(w, g, m, v):
    nm = B1 * m + (1.0 - B1) * g
    nv = B2 * v + (1.0 - B2) * (g * g)
    return -LR * ((nm / (1.0 - B1 ** STEP)) / (jnp.sqrt(nv / (1.0 - B2 ** STEP)) + ADAM_EPS) + WD * w), nm, nv


def _adamw(w, g, m, v, name):
    rows, cols = w.shape
    tm = _pick(rows, 448, 8)

    def body(w_ref, g_ref, m_ref, v_ref, d_ref, nm_ref, nv_ref):
        d_ref[...], nm_ref[...], nv_ref[...] = _adamw_math(w_ref[...], g_ref[...], m_ref[...], v_ref[...])

    blk = pl.BlockSpec((tm, cols), lambda i: (i, 0))
    return pl.pallas_call(
        body, name=name, grid=(rows // tm,), in_specs=[blk] * 4, out_specs=[blk] * 3,
        out_shape=[_sds((rows, cols), F32)] * 3, compiler_params=_params(("parallel",)),
    )(w, g, m, v)


def _mesh_pos():
    return lax.axis_index("x"), lax.axis_index("y"), lax.axis_index("c")


def _all_gather(arrs, name):
    n = len(arrs)

    def body(*refs):
        ins, outs = refs[:n], refs[n:2 * n]
        send_sems, recv_sems, local_sems = refs[2 * n:]
        x, y, c = _mesh_pos()
        me, sib = (x, y, c), (x, y, 1 - c)
        chips = [(1 - x, y), (x, 1 - y), (1 - x, 1 - y)]

        def slot(p):
            return 4 * p[0] + 2 * p[1] + p[2]

        def copy(a, k, block, to, own):
            dst = outs[a].at[slot(block)]
            return pltpu.make_async_remote_copy(
                src_ref=ins[a] if own else dst, dst_ref=dst, send_sem=send_sems.at[a, k], recv_sem=recv_sems.at[a, k],
                device_id=to, device_id_type=MESH_T)

        mine = [pltpu.make_async_copy(ins[a], outs[a].at[slot(me)], local_sems.at[a]) for a in range(n)]
        for cp in mine:
            cp.start()
        first = []
        for a in range(n):
            first.append(copy(a, 0, me, sib, True))
            first += [copy(a, 1 + j, me, (*chip, c), True) for j, chip in enumerate(chips)]
        for cp in first:
            cp.start()
        passed = []
        for j, chip in enumerate(chips):
            for a in range(n):
                copy(a, 1 + j, (*chip, c), me, False).wait_recv()
                fwd = copy(a, 4 + j, (*chip, c), sib, False)
                fwd.start()
                passed.append(fwd)
        for a in range(n):
            copy(a, 0, sib, me, False).wait_recv()
            for j, chip in enumerate(chips):
                copy(a, 4 + j, (*chip, 1 - c), me, False).wait_recv()
        for cp in first + passed:
            cp.wait_send()
        for cp in mine:
            cp.wait()

    hbm = pl.BlockSpec(memory_space=pl.ANY)
    return pl.pallas_call(
        body, name=name, in_specs=[hbm] * n, out_specs=[hbm] * n,
        out_shape=[_sds((N_DEV,) + a.shape, a.dtype) for a in arrs],
        scratch_shapes=[pltpu.SemaphoreType.DMA((n, 7)), pltpu.SemaphoreType.DMA((n, 7)), pltpu.SemaphoreType.DMA((n,))],
    )(*arrs)


def _pair_exchange(arrs, name):
    n = len(arrs)

    def body(*refs):
        ins, outs = refs[:n], refs[n:2 * n]
        send_sems, recv_sems = refs[2 * n:]
        x, y, c = _mesh_pos()
        sib = (x, y, 1 - c)
        sends = []
        for a in range(n):
            for k in range(4):
                sends.append(pltpu.make_async_remote_copy(
                    src_ref=ins[a].at[2 * k + 1 - c], dst_ref=outs[a].at[k], send_sem=send_sems.at[a, k],
                    recv_sem=recv_sems.at[a, k], device_id=sib, device_id_type=MESH_T))
        for cp in sends:
            cp.start()
        for cp in sends:
            cp.wait_recv()
        for cp in sends:
            cp.wait_send()

    hbm = pl.BlockSpec(memory_space=pl.ANY)
    return pl.pallas_call(
        body, name=name, in_specs=[hbm] * n, out_specs=[hbm] * n,
        out_shape=[_sds((4,) + a.shape[1:], a.dtype) for a in arrs],
        scratch_shapes=[pltpu.SemaphoreType.DMA((n, 4)), pltpu.SemaphoreType.DMA((n, 4))],
    )(*arrs)


def _pair_add(part, got, core, name):
    _, rows, cols = part.shape
    tm = _pick(rows, 672, 16)
    p4 = part.reshape(4, 2, rows, cols)

    def body(core_ref, p_ref, g_ref, o_ref):
        o_ref[...] = (p_ref[...].astype(F32) + g_ref[...].astype(F32)).astype(BF16)

    blk = pl.BlockSpec((None, tm, cols), lambda k, i, cr: (k, i, 0))
    return pl.pallas_call(
        body, name=name,
        grid_spec=pltpu.PrefetchScalarGridSpec(
            num_scalar_prefetch=1, grid=(4, rows // tm),
            in_specs=[pl.BlockSpec((None, None, tm, cols), lambda k, i, cr: (k, cr[0], i, 0)), blk], out_specs=blk),
        out_shape=_sds((4, rows, cols), BF16), compiler_params=_params(("parallel", "parallel")),
    )(core, p4, got)


def _chip_sum_adamw(pair_sums, landed, chip, w, m, v, name):
    _, rows, cols = pair_sums.shape
    tm = _pick(rows, 448, 16)

    def body(chip_ref, s_ref, l_ref, w_ref, m_ref, v_ref, g_ref, d_ref, nm_ref, nv_ref):
        acc = s_ref[...].astype(F32)
        for j in range(3):
            acc = acc + l_ref[j].astype(F32)
        g_ref[...] = acc
        d_ref[...], nm_ref[...], nv_ref[...] = _adamw_math(w_ref[...], acc, m_ref[...], v_ref[...])

    blk = pl.BlockSpec((tm, cols), lambda i, ch: (i, 0))
    return pl.pallas_call(
        body, name=name,
        grid_spec=pltpu.PrefetchScalarGridSpec(
            num_scalar_prefetch=1, grid=(rows // tm,),
            in_specs=[pl.BlockSpec((None, tm, cols), lambda i, ch: (ch[0], i, 0)),
                      pl.BlockSpec((3, tm, cols), lambda i, ch: (0, i, 0)), blk, blk, blk],
            out_specs=[blk] * 4),
        out_shape=[_sds((rows, cols), F32)] * 4, compiler_params=_params(("parallel",)),
    )(chip, pair_sums, landed, w, m, v)


_HBM = pl.BlockSpec(memory_space=pltpu.HBM)
_SEM = pl.BlockSpec(memory_space=pltpu.SEMAPHORE)
_EFFECT = pltpu.SideEffectType.DATAFLOW_SIDE_EFFECTING


def _chip_routes(n):
    def plan(x, y, c):
        routes = []
        for a in range(n):
            for j in range(1, 4):
                px, py = x ^ (j >> 1), y ^ (j & 1)
                routes.append((a, 2 * px + py, (px, py, c), j - 1))
        return routes
    return plan, 3 * n


def _bcast_routes(n):
    def plan(x, y, c):
        routes = []
        for a in range(n):
            for k in range(1, N_DEV):
                peer = (x ^ ((k >> 2) & 1), y ^ ((k >> 1) & 1), c ^ (k & 1))
                routes.append((a, 0, peer, 4 * x + 2 * y + c))
        return routes
    return plan, 7 * n


def _route_copies(srcs, lands, send_sems, recv_sems, routes):
    return [pltpu.make_async_remote_copy(
        src_ref=srcs[a].at[sb], dst_ref=lands[a].at[lb], send_sem=send_sems.at[r], recv_sem=recv_sems.at[r],
        device_id=peer, device_id_type=MESH_T) for r, (a, sb, peer, lb) in enumerate(routes)]


def _exchange_start(srcs, lands, routes, name, after=()):
    plan, count = routes
    n = len(srcs)
    n_in = 2 * n + len(after)

    def body(*refs):
        send_sems, recv_sems = refs[n_in], refs[n_in + 1]
        token = refs[-1]
        for cp in _route_copies(refs[:n], refs[n:2 * n], send_sems, recv_sems, plan(*_mesh_pos())):
            cp.start()
        token[...] = jnp.zeros_like(token)

    args = [pltpu.with_memory_space_constraint(a, pltpu.HBM) for a in list(srcs) + list(lands)]
    out = pl.pallas_call(
        body, name=name,
        out_shape=(pltpu.SemaphoreType.DMA((count,)), pltpu.SemaphoreType.DMA((count,)),
                   *[pltpu.HBM(a.shape, a.dtype) for a in args], _sds((8, 128), F32)),
        in_specs=[_HBM] * (2 * n) + [pl.BlockSpec(memory_space=pl.ANY)] * len(after),
        out_specs=(_SEM, _SEM, *([_HBM] * (2 * n)), pl.BlockSpec(memory_space=pltpu.VMEM)),
        input_output_aliases={i: 2 + i for i in range(2 * n)},
        compiler_params=pltpu.CompilerParams(has_side_effects=_EFFECT),
    )(*args, *after)
    return (out[0], out[1], list(out[2:2 + 2 * n]), routes), out[-1]


def _exchange_wait(state, after, name):
    send_sems, recv_sems, bufs, (plan, count) = state
    n = len(bufs) // 2

    def body(*refs):
        send_s, recv_s = refs[2 * n], refs[2 * n + 1]
        for cp in _route_copies(refs[:n], refs[n:2 * n], send_s, recv_s, plan(*_mesh_pos())):
            cp.wait_send()
            cp.wait_recv()

    out = pl.pallas_call(
        body, name=name, out_shape=tuple(pltpu.HBM(a.shape, a.dtype) for a in bufs),
        in_specs=[_HBM] * (2 * n) + [_SEM, _SEM, pl.BlockSpec(memory_space=pl.ANY)], out_specs=tuple([_HBM] * (2 * n)),
        input_output_aliases={i: i for i in range(2 * n)},
        compiler_params=pltpu.CompilerParams(has_side_effects=_EFFECT),
    )(*bufs, send_sems, recv_sems, after)
    return list(out[:n]), list(out[n:])


def _group_routes(js):
    def plan(x, y, c):
        return [(0, 0, (x ^ (j >> 1), y ^ (j & 1), c), 2 * j + c) for j in js]
    return plan, len(js)


def _pair_fill(groups, js, name, after=()):
    def body(*refs):
        g_ref, send_sems, recv_sems = refs[-3:]
        x, y, c = _mesh_pos()
        sends = []
        for n, j in enumerate(js):
            mine = g_ref.at[2 * j + c]
            sends.append(pltpu.make_async_remote_copy(
                src_ref=mine, dst_ref=mine, send_sem=send_sems.at[n], recv_sem=recv_sems.at[n],
                device_id=(x, y, 1 - c), device_id_type=MESH_T))
        for cp in sends:
            cp.start()
        for n, j in enumerate(js):
            pltpu.make_async_remote_copy(
                src_ref=g_ref.at[2 * j + c], dst_ref=g_ref.at[2 * j + 1 - c], send_sem=send_sems.at[n],
                recv_sem=recv_sems.at[n], device_id=(x, y, 1 - c), device_id_type=MESH_T).wait_recv()
        for cp in sends:
            cp.wait_send()

    hbm = pl.BlockSpec(memory_space=pl.ANY)
    return pl.pallas_call(
        body, name=name, in_specs=[hbm] * (1 + len(after)), out_specs=hbm, out_shape=_sds(groups.shape, groups.dtype),
        input_output_aliases={0: 0},
        scratch_shapes=[pltpu.SemaphoreType.DMA((len(js),)), pltpu.SemaphoreType.DMA((len(js),))],
    )(groups, *after)


def _in_proj_group(h_all, groups, j0, ng, chip, px_prev, after, name):
    rows_all = h_all.shape[0]
    gcols = IN_COLS // 4
    tm = _pick(rows_all, 1536, 128)
    g4 = groups.reshape(4, gcols, D)

    n_lead = (1 if px_prev is not None else 0) + len(after)
    lead = ([px_prev] if px_prev is not None else []) + list(after)

    def body(chip_ref, *refs):
        h_ref, w_ref, o_ref = refs[n_lead:]
        o_ref[...] = _dot(h_ref[...], w_ref[...], NT).astype(BF16)

    return pl.pallas_call(
        body, name=name,
        grid_spec=pltpu.PrefetchScalarGridSpec(
            num_scalar_prefetch=1, grid=(ng, rows_all // tm),
            in_specs=[pl.BlockSpec(memory_space=pl.ANY)] * n_lead
            + [pl.BlockSpec((tm, D), lambda n, i, ch: (i, 0)),
               pl.BlockSpec((None, gcols, D), lambda n, i, ch: (j0 + n, 0, 0))],
            out_specs=pl.BlockSpec((tm, gcols), lambda n, i, ch: (i, ch[0] ^ (j0 + n)))),
        out_shape=_sds((rows_all, IN_COLS), BF16),
        input_output_aliases={1: 0} if px_prev is not None else {},
        compiler_params=_params(("parallel", "parallel")),
    )(chip, *lead, h_all, g4)


def _d_h_groups(dp_all, groups, chip, after):
    rows_all = dp_all.shape[0]
    gcols = IN_COLS // 4
    tm = _pick(rows_all, 1536, 128)
    g4 = groups.reshape(4, gcols, D)
    n_lead = len(after)

    def body(chip_ref, *refs):
        a_ref, w_ref, o_ref = refs[n_lead:]
        j = pl.program_id(1)
        part = _dot(a_ref[...], w_ref[...])

        @pl.when(j == 0)
        def _():
            o_ref[...] = part

        @pl.when(j > 0)
        def _():
            o_ref[...] += part

    return pl.pallas_call(
        body, name="d_h",
        grid_spec=pltpu.PrefetchScalarGridSpec(
            num_scalar_prefetch=1, grid=(rows_all // tm, 4),
            in_specs=[pl.BlockSpec(memory_space=pl.ANY)] * n_lead
            + [pl.BlockSpec((tm, gcols), lambda i, j, ch: (i, ch[0] ^ j)),
               pl.BlockSpec((None, gcols, D), lambda i, j, ch: (j, 0, 0))],
            out_specs=pl.BlockSpec((tm, D), lambda i, j, ch: (i, 0))),
        out_shape=_sds((rows_all, D), F32),
        compiler_params=_params(("parallel", "arbitrary")),
    )(chip, *after, dp_all, g4)


def _reduce_scatter_start(parts, core, name):
    got = _pair_exchange(parts, name + "_pair")
    sums = [_pair_add(p, g, core, "%s_add_%d" % (name, i)) for i, (p, g) in enumerate(zip(parts, got))]
    lands = [lax.empty((3,) + s_.shape[1:], BF16) for s_ in sums]
    return _exchange_start(sums, lands, _chip_routes(len(sums)), name + "_start")


def _reduce_scatter_finish(rs_state, after, chip, wmv, name):
    sums, landed = _exchange_wait(rs_state, after, name + "_wait")
    return [_chip_sum_adamw(s_, l_, chip, *t, "%s_adamw_%d" % (name, i))
            for i, (s_, l_, t) in enumerate(zip(sums, landed, wmv))]


def _local_step(x, c, ctx, norm_w, ret_log2_decay, q_norm_w, k_norm_w, loss_target,
                mod, proj_in, proj_back, get_w_o, on_out_grads, on_in_grad, started=()):
    nb, seq, _ = x.shape
    cx = ctx.shape[1]
    t_rows, c_rows = nb * seq, nb * cx
    rows_all = t_rows + c_rows
    nc = seq // CH
    tm = _pick(seq, 256, 128)
    te = _pick(seq, 512, 128)
    assert cx % tm == 0 and t_rows % cx == 0 and seq % GRID_W == 0

    x2 = x.reshape(t_rows, D)
    ctx2 = ctx.reshape(c_rows, D)
    tgt = loss_target.reshape(t_rows, D)
    lg = _log_gamma(ret_log2_decay)
    cos, sin = _rope_tables(seq)

    mod3 = mod[:, None, :]
    h_all = _norm_fwd(x2, mod3, norm_w, rows_all, 0, seq, 0, None, te, "norm_fwd", after=started)
    h_all = _norm_fwd(ctx2, mod3, norm_w, rows_all, t_rows, c_rows, nb, h_all, tm, "norm_fwd_ctx")
    px = proj_in(h_all)
    s0f, s0b = _ctx_state(px, lg, nb, t_rows, cx)
    o_f, o_b, hist_f, hist_b = _ret_fwd(px, lg, s0f, s0b, nb, nc)
    q16 = _qk_prep(px, q_norm_w, cos, sin, t_rows, 0, AQ, HQ, 4, seq, te, "q_prep")
    kx16 = _qk_prep(px, k_norm_w, cos, sin, t_rows, 0, AK, HKV, HKV, seq, te, "k_prep")
    kc16 = _qk_prep(px, k_norm_w, None, None, c_rows, t_rows, AK, HKV, HKV, seq, tm, "kc_prep")
    o_att, yatt16, lse = _att_fwd(q16, kx16, kc16, px, nb, seq, cx, te)
    w_o_ret16, w_o_att16, w_out16 = get_w_o(lse)
    yret16, a_ret, a_att, y16 = _merge(o_f, o_b, yatt16, px, w_o_ret16, w_o_att16, te)
    dxn, dout16, dgate, loss_b = _outproj(y16, w_out16, x2, tgt, mod3, nb, seq, te)

    gw_out = _matmul(y16, dout16, ta=True, tm=D, tn=D, tk=D, out_dtype=BF16, name="gw_out")
    da_ret16, da_att16, dmr16, dma16, do16, drg16, dao, dag16 = _bwd_branches(
        dout16, w_out16, w_o_ret16, w_o_att16, px, a_ret, a_att, o_f, o_b, o_att, tm)
    gw_o_ret = _matmul(yret16, da_ret16, ta=True, tm=D, tn=D, tk=D, out_dtype=BF16, name="gw_o_ret")
    gw_o_att = _matmul(yatt16, da_att16, ta=True, tm=D, tn=D, tk=D, out_dtype=BF16, name="gw_o_att")
    out_state, out_started = on_out_grads([gw_o_ret, gw_o_att, gw_out])
    dq_rot, dkx, dvx, dkc, dvc = _att_bwd(q16, kx16, kc16, px, dao, o_att, lse, nb, seq, cx, te, after=out_started)
    daq16, gq = _qk_prep_bwd(dq_rot, px, q_norm_w, cos, sin, t_rows, 0, AQ, HQ, 4, seq, te, "q_prep_bwd")
    dak16, gk_lat = _qk_prep_bwd(dkx.reshape(t_rows, HKV * HD), px, k_norm_w, cos, sin, t_rows, 0, AK, HKV, HKV, seq, te,
                                 "k_prep_bwd")
    dcak16, gk_ctx = _qk_prep_bwd(dkc.reshape(c_rows, HKV * HD), px, k_norm_w, None, None, c_rows, t_rows, AK, HKV, HKV,
                                  seq, tm, "kc_prep_bwd")
    dq_f, dk_f, dv_f, dq_b, dk_b, dv_b, ds_f, ds_b, dlg_scan = _ret_bwd(px, lg, do16, hist_f, hist_b, nb, nc)
    dck16, dcv16, dlg_ctx = _ctx_state_bwd(px, lg, ds_f, ds_b, nb, t_rows, cx)
    dp_all = _assemble_lat(rows_all, dk_f, dk_b, dv_f, dv_b, dak16, dvx.reshape(t_rows, HKV * HD), dq_f, dq_b, drg16,
                           daq16, dag16, dmr16, dma16, tm)
    dp_all = _assemble_ctx(dp_all, dck16, dcv16, dcak16, dvc.reshape(c_rows, HKV * HD), t_rows, tm)
    gw_in_t = _matmul(dp_all, h_all, ta=True, tm=1536, tn=D, tk=2304, out_dtype=BF16, name="gw_in")
    in_state, in_started = on_in_grad(gw_in_t)
    dh = proj_back(dp_all, in_started)
    grad_x, dsh, dsc, gnw_lat = _norm_bwd(dh, x2, mod3, norm_w, dxn, 0, seq, 0, te, "norm_bwd")
    dsh_c, dsc_c, gnw_ctx = _norm_bwd(dh, ctx2, mod3, norm_w, None, t_rows, c_rows, nb, tm, "norm_bwd_ctx")

    dlg = (jnp.sum(dlg_scan[:, :, 0], axis=0) + jnp.sum(dlg_ctx[:, :, :2, 0], axis=0).T.reshape(2 * RH)).reshape(1, 2 * RH)
    misc = jnp.concatenate([gq, gk_lat + gk_ctx, dlg, jnp.sum(loss_b[:, 0, 0]).reshape(1, 1),
                            jnp.zeros((1, D - 2 * HD - 2 * RH - 1), F32)], axis=1)
    rows = []
    for b in range(nb):
        rows += [dsh[b], dsc[b], dgate[b]]
    rows += [dsh_c[0], dsc_c[0]] + [c[b:b + 1] for b in range(nb)] + [gnw_lat + gnw_ctx, misc]
    payload = jnp.concatenate(rows + [jnp.zeros((PAY_ROWS - len(rows), D), F32)], axis=0)
    return grad_x.reshape(nb, seq, D), out_state, in_state, payload


def _finish_small(gathered, nb, c_ctx, ret_log2_decay, w_ada16, dev):
    n_dev = gathered.shape[0]
    loc = 3 * D // n_dev
    dmod_all = gathered[:, :3 * nb].reshape(n_dev * nb, 3 * D)
    dmodc_parts = jnp.concatenate([gathered[:, 3 * nb:3 * nb + 2].reshape(n_dev, 2 * D), jnp.zeros((n_dev, D), F32)], axis=1)
    c_all = gathered[:, 3 * nb + 2:4 * nb + 2].reshape(n_dev * nb, D)
    nw_parts = gathered[:, 4 * nb + 2]
    misc_parts = gathered[:, 4 * nb + 3]
    n_rows = n_dev * nb + n_dev
    pad = (-n_rows) % 16
    c_rows = jnp.concatenate([c_all, jnp.broadcast_to(c_ctx.reshape(1, D), (n_dev, D)), jnp.zeros((pad, D), F32)], axis=0)
    dm_rows = jnp.concatenate([dmod_all, dmodc_parts, jnp.zeros((pad, 3 * D), F32)], axis=0)
    dm_loc_rows = lax.dynamic_slice_in_dim(dm_rows, dev * loc, loc, axis=1)
    r_pad = jnp.full((1, D), -1.0, F32).at[:, 2 * HD:2 * HD + 2 * RH].set(ret_log2_decay.reshape(1, 2 * RH))
    gb, gc, gnw, misc, gwa = _small_final(dmod_all, dmodc_parts, c_rows, dm_loc_rows, nw_parts, misc_parts,
                                          c_ctx.reshape(1, D), r_pad, w_ada16)
    return (gb, gc, gnw, misc[:, :HD], misc[:, HD:2 * HD], misc[:, 2 * HD:2 * HD + 2 * RH], gwa,
            misc[0, 2 * HD + 2 * RH])


def kernel(x, c, ctx, c_ctx, norm_w, w_ada, b_ada, w_in, ret_log2_decay, q_norm_w, k_norm_w, w_o_ret, w_o_att, w_out, loss_target, m_c_ctx, m_norm_w, m_w_ada, m_b_ada, m_w_in, m_ret_log2_decay, m_q_norm_w, m_k_norm_w, m_w_o_ret, m_w_o_att, m_w_out, v_c_ctx, v_norm_w, v_w_ada, v_b_ada, v_w_in, v_ret_log2_decay, v_q_norm_w, v_k_norm_w, v_w_o_ret, v_w_o_att, v_w_out):
    nb = x.shape[0]
    mx, my, mc = _mesh_pos()
    dev = 4 * mx + 2 * my + mc
    core = jnp.reshape(mc, (1,)).astype(jnp.int32)
    chip = jnp.reshape(2 * mx + my, (1,)).astype(jnp.int32)

    n_loc = 3 * D // N_DEV
    c8 = jnp.zeros((8, D), F32).at[:nb].set(c).at[nb].set(c_ctx)
    c_land = lax.dynamic_update_slice(lax.empty((N_DEV, 8, D), F32), c8[None], (dev, 0, 0))
    c_state, c_token = _exchange_start([c8[None]], [c_land], _bcast_routes(1), "gather_c_start")
    w_in_t = jnp.transpose(w_in[0])
    in_shard = w_in_t.astype(BF16)
    groups = lax.dynamic_update_slice(lax.empty((N_DEV,) + in_shard.shape, BF16), in_shard[None], (mc, 0, 0))
    groups = _pair_fill(groups, (0,), "gather_in_pair", after=(c_token,))
    _, (c_all,) = _exchange_wait(c_state, groups, "gather_c_wait")
    ada_shard = w_ada[0].astype(BF16)
    b_loc = lax.dynamic_slice(b_ada, (0, dev * n_loc), (1, n_loc))
    mod_cols = _mod_part(c_all.reshape(N_DEV * 8, D), ada_shard, b_loc)
    (mod_all,) = _all_gather([mod_cols], "gather_mod")
    mod = jnp.transpose(lax.dynamic_slice(mod_all, (0, dev * 8, 0), (N_DEV, 8, n_loc)), (1, 0, 2)).reshape(8, 3 * D)
    ada_land = lax.dynamic_update_slice(lax.empty((N_DEV,) + ada_shard.shape, BF16), ada_shard[None], (dev, 0, 0))

    (near_send, near_recv, near_bufs, near_routes), gin_token = _exchange_start(
        [in_shard[None]], [groups], _group_routes((1, 2)), "gather_in_start", after=(mod_all,))
    w_in_groups, wo_states, ada_states = [], [], []
    wo_shards = [w_[0].astype(BF16) for w_ in (w_o_ret, w_o_att, w_out)]
    wo_lands = [lax.dynamic_update_slice(lax.empty((N_DEV,) + s_.shape, BF16), s_[None], (dev, 0, 0)) for s_ in wo_shards]

    def _state(send, recv, src, groups, routes):
        return send, recv, [src, groups], routes

    def proj_in(h_all):
        src, groups = near_bufs
        px = _in_proj_group(h_all, groups, 0, 1, chip, None, (gin_token,), "in_proj_0")
        (src,), (groups,) = _exchange_wait(_state(near_send, near_recv, src, groups, near_routes), px,
                                           "gather_in_wait_near")
        groups = _pair_fill(groups, (1, 2), "gather_in_fill_near")
        (far_send, far_recv, (src, groups), far_routes), far_token = _exchange_start(
            [src], [groups], _group_routes((3,)), "gather_in_start_far")
        wo_state, wo_token = _exchange_start([s_[None] for s_ in wo_shards], wo_lands, _bcast_routes(3),
                                             "gather_wo_start", after=(far_token,))
        wo_states.append(wo_state)
        ada_state, ada_token = _exchange_start([ada_shard[None]], [ada_land], _bcast_routes(1), "gather_ada_start",
                                               after=(wo_token,))
        ada_states.append(ada_state)
        px = _in_proj_group(h_all, groups, 1, 2, chip, px, (ada_token,), "in_proj_near")
        (src,), (groups,) = _exchange_wait(_state(far_send, far_recv, src, groups, far_routes), px,
                                           "gather_in_wait_far")
        groups = _pair_fill(groups, (3,), "gather_in_fill_far")
        px = _in_proj_group(h_all, groups, 3, 1, chip, px, (), "in_proj_far")
        w_in_groups.append(groups)
        return px

    def proj_back(dp_all, after):
        return _d_h_groups(dp_all, w_in_groups[0], chip, after)

    def get_w_o(after):
        _, (l_ret, l_att, l_out) = _exchange_wait(wo_states[0], after, "gather_wo_wait")
        return l_ret.reshape(RH * DV, D), l_att.reshape(D, D), l_out.reshape(D, D)

    def on_out_grads(grads):
        parts = [g_.reshape(N_DEV, g_.shape[0] // N_DEV, D) for g_ in grads]
        state, token = _reduce_scatter_start(parts, core, "rs_out")
        return state, (token,)

    def on_in_grad(grad):
        state, token = _reduce_scatter_start([grad.reshape(N_DEV, IN_COLS // N_DEV, D)], core, "rs_in")
        return state, (token,)

    grad_x, out_state, in_state, payload = _local_step(
        x, c, ctx, norm_w, ret_log2_decay, q_norm_w, k_norm_w, loss_target,
        mod, proj_in, proj_back, get_w_o, on_out_grads, on_in_grad, started=(gin_token,))

    pay_land = lax.dynamic_update_slice(lax.empty((N_DEV,) + payload.shape, F32), payload[None], (dev, 0, 0))
    pay_state, pay_token = _exchange_start([payload[None]], [pay_land], _bcast_routes(1), "gather_small_start")

    out_res = _reduce_scatter_finish(out_state, pay_token, chip,
                                     [(w_[0], m_[0], v_[0]) for w_, m_, v_ in ((w_o_ret, m_w_o_ret, v_w_o_ret),
                                                                                (w_o_att, m_w_o_att, v_w_o_att),
                                                                                (w_out, m_w_out, v_w_out))], "rs_out")
    (in_res,) = _reduce_scatter_finish(in_state, out_res[0][0], chip,
                                       [(w_in_t, jnp.transpose(m_w_in[0]), jnp.transpose(v_w_in[0]))], "rs_in")

    _, (gathered,) = _exchange_wait(pay_state, in_res[0], "gather_small_wait")
    _, (l_ada,) = _exchange_wait(ada_states[0], gathered, "gather_ada_wait")
    w_ada16 = jnp.transpose(l_ada, (1, 0, 2)).reshape(D, 3 * D)
    gb, gc, gnw, gq, gk, gr, gwa, loss = _finish_small(gathered, nb, c_ctx, ret_log2_decay, w_ada16, dev)
    big = {4: [jnp.transpose(r)[None] for r in in_res]}
    for i, res in zip((8, 9, 10), out_res):
        big[i] = [r[None] for r in res]
    small_g = {0: gc.reshape(c_ctx.shape), 1: gnw, 2: gwa[None], 3: gb, 5: gr.reshape(ret_log2_decay.shape), 6: gq, 7: gk}
    weights = [c_ctx, norm_w, w_ada, b_ada, w_in, ret_log2_decay, q_norm_w, k_norm_w, w_o_ret, w_o_att, w_out]
    ms = [m_c_ctx, m_norm_w, m_w_ada, m_b_ada, m_w_in, m_ret_log2_decay, m_q_norm_w, m_k_norm_w, m_w_o_ret, m_w_o_att, m_w_out]
    vs = [v_c_ctx, v_norm_w, v_w_ada, v_b_ada, v_w_in, v_ret_log2_decay, v_q_norm_w, v_k_norm_w, v_w_o_ret, v_w_o_att, v_w_out]
    grads, deltas, new_ms, new_vs = [], [], [], []
    for i, (w, m, v) in enumerate(zip(weights, ms, vs)):
        if i in big:
            res = big[i]
        else:
            shape2 = (-1, w.shape[-1])
            g = small_g[i]
            res = [g] + [r.reshape(w.shape) for r in _adamw(w.reshape(shape2), g.reshape(shape2), m.reshape(shape2),
                                                             v.reshape(shape2), "adamw_%d" % i)]
        for lst, r in zip((grads, deltas, new_ms, new_vs), res):
            lst.append(r)
    return (loss, grad_x, *grads, *deltas, *new_ms, *new_vs)
```

```python
import numpy as np
import jax
import jax.numpy as jnp
from jax import lax
from jax.experimental import pallas as pl
from jax.experimental.pallas import tpu as pltpu

F32 = jnp.float32
BF16 = jnp.bfloat16

D = 1024
RH, DK, DV, CH = 4, 256, 512, 256
HQ, HKV, HD = 8, 2, 128
GRID_W = 64
ROPE_THETA = 10000.0
EPS = 1e-6
RK, RV, AK, AV, RQ, RG, AQ, AG, MR, MA = 0, 1024, 3072, 3328, 3584, 4608, 6656, 7680, 8704, 9728
IN_COLS = 10752
KV_COLS = 3584
N_DEV = 8
LR, B1, B2, ADAM_EPS, WD, STEP = 0.001, 0.9, 0.999, 1e-08, 0.01, 10
PAY_ROWS = 16
VMEM_LIMIT = 56 * 1024 * 1024
MESH_T = pl.DeviceIdType.MESH

NT = (((1,), (1,)), ((), ()))
TN = (((0,), (0,)), ((), ()))
SM_C = (HD ** -0.5) * float(np.log2(np.e))


def _params(sem):
    return pltpu.CompilerParams(dimension_semantics=sem, vmem_limit_bytes=VMEM_LIMIT)


def _pick(n, target, mult=8):
    best = None
    for t in range(mult, min(n, target) + 1, mult):
        if n % t == 0:
            best = t
    return best or n


def _dot(a, b, dn=None):
    if dn is None:
        return jnp.dot(a, b, preferred_element_type=F32)
    return lax.dot_general(a, b, dn, preferred_element_type=F32)


def _sig(v):
    return jax.nn.sigmoid(v)


def _silu(v):
    return v * _sig(v)


def _dsilu(v):
    s = _sig(v)
    return s * (1.0 + v * (1.0 - s))


def _sds(shape, dtype):
    return jax.ShapeDtypeStruct(shape, dtype)


def _matmul(a, b, *, ta=False, tb=False, tm, tn, tk, out_dtype, name, after=()):
    m = a.shape[1] if ta else a.shape[0]
    kdim = a.shape[0] if ta else a.shape[1]
    n = b.shape[0] if tb else b.shape[1]
    tm, tn, tk = _pick(m, tm, 128), _pick(n, tn, 128), _pick(kdim, tk, 128)
    nk = kdim // tk
    dn = (((0 if ta else 1,), (1 if tb else 0,)), ((), ()))

    def body(a_ref, b_ref, *rest):
        o_ref, acc_ref = rest[-2:]
        k = pl.program_id(2)
        part = _dot(a_ref[...].astype(BF16), b_ref[...].astype(BF16), dn)
        if nk == 1:
            o_ref[...] = part.astype(o_ref.dtype)
        else:
            @pl.when(k == 0)
            def _():
                acc_ref[...] = part

            @pl.when(k > 0)
            def _():
                acc_ref[...] += part

            @pl.when(k == nk - 1)
            def _():
                o_ref[...] = acc_ref[...].astype(o_ref.dtype)

    a_spec = pl.BlockSpec((tk, tm), lambda i, j, k: (k, i)) if ta else pl.BlockSpec((tm, tk), lambda i, j, k: (i, k))
    b_spec = pl.BlockSpec((tn, tk), lambda i, j, k: (j, k)) if tb else pl.BlockSpec((tk, tn), lambda i, j, k: (k, j))
    return pl.pallas_call(
        body, name=name, grid=(m // tm, n // tn, nk),
        in_specs=[a_spec, b_spec] + [pl.BlockSpec(memory_space=pl.ANY)] * len(after),
        out_specs=pl.BlockSpec((tm, tn), lambda i, j, k: (i, j)), out_shape=_sds((m, n), out_dtype),
        scratch_shapes=[pltpu.VMEM((tm, tn) if nk > 1 else (8, 128), F32)],
        compiler_params=_params(("parallel", "parallel", "arbitrary")),
    )(a, b, *after)


def _log_gamma(r):
    rp = jnp.full((8, 128), -1.0, F32).at[:2, :RH].set(r.reshape(2, RH))

    def body(r_ref, o_ref):
        o_ref[...] = jnp.log1p(-jnp.exp2(r_ref[...]))

    out = pl.pallas_call(body, name="log_gamma", out_shape=_sds((8, 128), F32))(rp)
    return out[:2, :RH]


def _mod_part(c_rows, w_ada_loc16, b_loc):
    def body(c_ref, w_ref, b_ref, o_ref):
        o_ref[...] = _dot(_silu(c_ref[...]).astype(BF16), w_ref[...]) + b_ref[...]

    return pl.pallas_call(
        body, name="mod_part", out_shape=_sds((c_rows.shape[0], w_ada_loc16.shape[1]), F32),
    )(c_rows, w_ada_loc16, b_loc)


def _norm_fwd(x2, mod3, norm_w, rows_all, row_off, rows_per_group, group0, h_prev, tm, name, after=()):
    rows = x2.shape[0]
    rb0 = row_off // tm
    bpg = rows_per_group // tm

    def body(*refs):
        x_ref, sh_ref, sc_ref, nw_ref, o_ref = refs[-5:]
        xv = x_ref[...]
        r = lax.rsqrt(jnp.mean(xv * xv, axis=-1, keepdims=True) + EPS)
        o_ref[...] = ((xv * r) * nw_ref[...] * (1.0 + sc_ref[...]) + sh_ref[...]).astype(BF16)

    in_specs = [pl.BlockSpec((tm, D), lambda i: (i, 0)),
                pl.BlockSpec((None, 1, D), lambda i: (group0 + i // bpg, 0, 0)),
                pl.BlockSpec((None, 1, D), lambda i: (group0 + i // bpg, 0, 1)),
                pl.BlockSpec((1, D), lambda i: (0, 0))]
    in_specs = [pl.BlockSpec(memory_space=pl.ANY)] * len(after) + in_specs
    args = list(after) + [x2, mod3, mod3, norm_w]
    alias = {}
    if h_prev is not None:
        in_specs.insert(0, pl.BlockSpec(memory_space=pl.ANY))
        args.insert(0, h_prev)
        alias = {0: 0}
    return pl.pallas_call(
        body, name=name, grid=(rows // tm,), in_specs=in_specs,
        out_specs=pl.BlockSpec((tm, D), lambda i: (rb0 + i, 0)), out_shape=_sds((rows_all, D), BF16),
        input_output_aliases=alias, compiler_params=_params(("parallel",)),
    )(*args)


def _decays(lg, fwd):
    ii = lax.broadcasted_iota(jnp.int32, (CH, CH), 0)
    jj = lax.broadcasted_iota(jnp.int32, (CH, CH), 1)
    ri = lax.broadcasted_iota(jnp.int32, (CH, 1), 0).astype(F32)
    rel = (ii - jj) if fwd else (jj - ii)
    relf = jnp.maximum(rel, 0).astype(F32)
    mask = jnp.where(rel >= 0, jnp.exp(lg * relf), 0.0)
    qe = (ri + 1.0) if fwd else (CH - ri)
    ke = (CH - 1.0 - ri) if fwd else ri
    return mask, relf, jnp.exp(lg * qe), qe, jnp.exp(lg * ke), ke


def _wide_specs(rowf):
    return [pl.BlockSpec((CH, 2 * DK), lambda b, c: (rowf(b, c), RQ // (2 * DK))),
            pl.BlockSpec((CH, 2 * DK), lambda b, c: (rowf(b, c), RQ // (2 * DK) + 1)),
            pl.BlockSpec((CH, RH * DK), lambda b, c: (rowf(b, c), RK // (RH * DK))),
            pl.BlockSpec((CH, 2 * DV), lambda b, c: (rowf(b, c), RV // (2 * DV))),
            pl.BlockSpec((CH, 2 * DV), lambda b, c: (rowf(b, c), RV // (2 * DV) + 1))]


def _head_qkv(refs, h):
    q0, q1, k, v0, v1 = refs
    lo = h % 2
    q = (q0, q1)[h // 2][:, lo * DK:(lo + 1) * DK].astype(F32)
    kk = k[:, h * DK:(h + 1) * DK].astype(F32) * (DK ** -0.5)
    v16 = (v0, v1)[h // 2][:, lo * DV:(lo + 1) * DV].astype(BF16)
    return q, kk, v16


def _ctx_state(px, lg, nb, t_rows, cx):
    rb = t_rows // cx

    def body(lg_ref, k_ref, v_ref, sf_ref, sb_ref):
        h = pl.program_id(1)
        pos = lax.broadcasted_iota(jnp.int32, (cx, 1), 0).astype(F32)
        k = k_ref[...].astype(F32) * (DK ** -0.5)
        v16 = v_ref[...].astype(BF16)
        wf = jnp.exp(lg_ref[0, h] * (cx - 1.0 - pos))
        wb = jnp.exp(lg_ref[1, h] * pos)
        sf_ref[...] = _dot((k * wf).astype(BF16), v16, TN)
        sb_ref[...] = _dot((k * wb).astype(BF16), v16, TN)

    st = pl.BlockSpec((None, None, DK, DV), lambda b, h: (b, h, 0, 0))
    return pl.pallas_call(
        body, name="ctx_state", grid=(nb, RH),
        in_specs=[pl.BlockSpec(memory_space=pltpu.SMEM),
                  pl.BlockSpec((cx, DK), lambda b, h: (rb + b, RK // DK + h)),
                  pl.BlockSpec((cx, DV), lambda b, h: (rb + b, RV // DV + h))],
        out_specs=[st, st], out_shape=[_sds((nb, RH, DK, DV), F32)] * 2,
        compiler_params=_params(("parallel", "parallel")),
    )(lg, px, px)


def _ret_fwd(px, lg, s0f, s0b, nb, nc):
    t_rows = nb * nc * CH

    def body(lg_ref, *refs):
        ins = (refs[0:5], refs[5:10])
        s0f_ref, s0b_ref, of_ref, ob_ref, hf_ref, hb_ref, sf, sb = refs[10:]
        c = pl.program_id(1)

        @pl.when(c == 0)
        def _():
            sf[...] = s0f_ref[...]
            sb[...] = s0b_ref[...]

        for d, (o_ref, h_ref, s) in enumerate(((of_ref, hf_ref, sf), (ob_ref, hb_ref, sb))):
            for h in range(RH):
                lg_d = lg_ref[d, h]
                mask, _, qd, _, kd, _ = _decays(lg_d, d == 0)
                q, k, v16 = _head_qkv(ins[d], h)
                a = _dot(q.astype(BF16), k.astype(BF16), NT)
                st = s[h]
                st16 = st.astype(BF16)
                h_ref[h] = st16
                o = _dot((a * mask).astype(BF16), v16) + _dot((q * qd).astype(BF16), st16)
                o_ref[:, h * DV:(h + 1) * DV] = o.astype(BF16)
                s[h] = st * jnp.exp(lg_d * CH) + _dot((k * kd).astype(BF16), v16, TN)

    def fw(b, c):
        return b * nc + c

    def bw(b, c):
        return b * nc + nc - 1 - c

    st = pl.BlockSpec((None, RH, DK, DV), lambda b, c: (b, 0, 0, 0))
    in_specs = [pl.BlockSpec(memory_space=pltpu.SMEM)] + _wide_specs(fw) + _wide_specs(bw) + [st, st]
    out_specs = [pl.BlockSpec((CH, RH * DV), lambda b, c: (fw(b, c), 0)),
                 pl.BlockSpec((CH, RH * DV), lambda b, c: (bw(b, c), 0)),
                 pl.BlockSpec((None, None, RH, DK, DV), lambda b, c: (b, c, 0, 0, 0)),
                 pl.BlockSpec((None, None, RH, DK, DV), lambda b, c: (b, nc - 1 - c, 0, 0, 0))]
    return pl.pallas_call(
        body, name="ret_fwd", grid=(nb, nc), in_specs=in_specs, out_specs=out_specs,
        out_shape=[_sds((t_rows, RH * DV), BF16)] * 2 + [_sds((nb, nc, RH, DK, DV), BF16)] * 2,
        scratch_shapes=[pltpu.VMEM((RH, DK, DV), F32), pltpu.VMEM((RH, DK, DV), F32)],
        compiler_params=_params(("parallel", "arbitrary")),
    )(lg, *([px] * 10), s0f, s0b)


def _rope_tables(seq):
    rows = seq // GRID_W
    row = np.repeat(np.arange(rows, dtype=np.float32), GRID_W)
    col = np.tile(np.arange(GRID_W, dtype=np.float32), rows)
    half = HD // 2
    freqs = (ROPE_THETA ** (-np.arange(0, half, 2, dtype=np.float32) / half)).astype(np.float32)
    ang = np.concatenate([row[:, None] * freqs, col[:, None] * freqs], axis=-1).astype(np.float32)
    cos = np.repeat(np.cos(ang), 2, axis=-1).astype(np.float32)
    sin = np.repeat(np.sin(ang), 2, axis=-1).astype(np.float32)
    sign = np.tile(np.array([-1.0, 1.0], np.float32), HD // 2)
    return jnp.asarray(cos), jnp.asarray(sin * sign)


def _swap_pairs(v):
    lane = lax.broadcasted_iota(jnp.int32, v.shape, 1)
    return jnp.where((lane & 1) == 0, pltpu.roll(v, HD - 1, 1), pltpu.roll(v, 1, 1))


def _qk_prep(px, nw, cos, sin, rows, row_off, col_off, heads, hb, seq, tm, name):
    rope = cos is not None
    rb0 = row_off // tm
    pb = seq // tm if rope else 1
    bw = hb * HD

    def body(*refs):
        if rope:
            x_ref, w_ref, c_ref, s_ref, o_ref = refs
        else:
            x_ref, w_ref, o_ref = refs
        for h in range(hb):
            sl = slice(h * HD, (h + 1) * HD)
            xv = x_ref[:, sl].astype(F32)
            r = lax.rsqrt(jnp.mean(xv * xv, axis=-1, keepdims=True) + EPS)
            t = (xv * r) * w_ref[...]
            if rope:
                t = t * c_ref[...] + _swap_pairs(t) * s_ref[...]
            o_ref[:, sl] = t.astype(BF16)

    in_specs = [pl.BlockSpec((tm, bw), lambda i, j: (rb0 + i, col_off // bw + j)),
                pl.BlockSpec((1, HD), lambda i, j: (0, 0))]
    args = [px, nw]
    if rope:
        in_specs += [pl.BlockSpec((tm, HD), lambda i, j: (i % pb, 0))] * 2
        args += [cos, sin]
    return pl.pallas_call(
        body, name=name, grid=(rows // tm, heads // hb), in_specs=in_specs,
        out_specs=pl.BlockSpec((tm, bw), lambda i, j: (i, j)), out_shape=_sds((rows, heads * HD), BF16),
        compiler_params=_params(("parallel", "parallel")),
    )(*args)


def _att_fwd(q16, kx16, kc16, px, nb, seq, cx, tq):
    t_rows = nb * seq
    nq = seq // tq
    rep = HQ // HKV
    gw = rep * HD

    def body(q_ref, kx_ref, kc_ref, vx_ref, vc_ref, g_ref, o_ref, y_ref, l_ref):
        kx = kx_ref[...]
        kc = kc_ref[...]
        vx = vx_ref[...].astype(BF16)
        vc = vc_ref[...].astype(BF16)
        l_ref[...] = jnp.zeros_like(l_ref)
        for r in range(rep):
            sl = slice(r * HD, (r + 1) * HD)
            q = q_ref[:, sl]
            s1 = _dot(q, kx, NT)
            s2 = _dot(q, kc, NT)
            m = jnp.maximum(jnp.max(s1, axis=-1, keepdims=True), jnp.max(s2, axis=-1, keepdims=True))
            e1 = jnp.exp2((s1 - m) * SM_C)
            e2 = jnp.exp2((s2 - m) * SM_C)
            tot = jnp.sum(e1, axis=-1, keepdims=True) + jnp.sum(e2, axis=-1, keepdims=True)
            o = (_dot(e1.astype(BF16), vx) + _dot(e2.astype(BF16), vc)) * (1.0 / tot)
            o_ref[:, sl] = o
            y_ref[:, sl] = (o * _silu(g_ref[:, sl].astype(F32))).astype(BF16)
            l_ref[:, r:r + 1] = m * SM_C + jnp.log(tot) * float(np.log2(np.e))

    qblk = pl.BlockSpec((tq, gw), lambda b, g, i: (b * nq + i, g))
    return pl.pallas_call(
        body, name="att_fwd", grid=(nb, HKV, nq),
        in_specs=[qblk,
                  pl.BlockSpec((seq, HD), lambda b, g, i: (b, g)),
                  pl.BlockSpec((cx, HD), lambda b, g, i: (b, g)),
                  pl.BlockSpec((seq, HD), lambda b, g, i: (b, AV // HD + g)),
                  pl.BlockSpec((cx, HD), lambda b, g, i: (t_rows // cx + b, AV // HD + g)),
                  pl.BlockSpec((tq, gw), lambda b, g, i: (b * nq + i, AG // gw + g))],
        out_specs=[qblk, qblk, pl.BlockSpec((tq, 128), lambda b, g, i: (b * nq + i, g))],
        out_shape=[_sds((t_rows, D), F32), _sds((t_rows, D), BF16), _sds((t_rows, HKV * 128), F32)],
        compiler_params=_params(("parallel", "parallel", "parallel")),
    )(q16, kx16, kc16, px, px, px)


def _gate_specs(tm, col0):
    hw = D // 2
    return [pl.BlockSpec((tm, hw), lambda i: (i, col0 // hw)), pl.BlockSpec((tm, hw), lambda i: (i, col0 // hw + 1))]


def _merge_out(o_f, o_b, yatt16, px, w_o_ret16, w_o_att16, w_out16, x2, tgt, mod3, nb, seq, tm):
    t_rows = nb * seq
    bpb = seq // tm
    hw = D // 2

    def body(of_ref, ob_ref, g0, g1, g2, g3, wr_ref, ya_ref, wa_ref, mr0, mr1, ma0, ma1, wo_ref, x_ref, t_ref, gt_ref,
             yr_ref, ar_ref, aa_ref, y_ref, dxn_ref, dout_ref, dg_ref, loss_ref):
        i = pl.program_id(1)
        for h, g_ref in enumerate((g0, g1, g2, g3)):
            sl = slice(h * DV, (h + 1) * DV)
            o = of_ref[:, sl].astype(F32) + ob_ref[:, sl].astype(F32)
            r = lax.rsqrt(jnp.mean(o * o, axis=-1, keepdims=True) + EPS)
            yr_ref[:, sl] = ((o * r) * _silu(g_ref[...].astype(F32))).astype(BF16)
        ar = _dot(yr_ref[...], wr_ref[...])
        aa = _dot(ya_ref[...], wa_ref[...])
        ar_ref[...] = ar.astype(BF16)
        aa_ref[...] = aa.astype(BF16)
        for j, (mr_ref, ma_ref) in enumerate(((mr0, ma0), (mr1, ma1))):
            sl = slice(j * hw, (j + 1) * hw)
            y_ref[:, sl] = (_sig(mr_ref[...].astype(F32)) * ar[:, sl]
                            + _sig(ma_ref[...].astype(F32)) * aa[:, sl]).astype(BF16)
        out = _dot(y_ref[...], wo_ref[...])
        gate = gt_ref[...]
        diff = x_ref[...] + gate * out - t_ref[...]
        dxn = diff * (1.0 / D)
        dxn_ref[...] = dxn
        dout_ref[...] = (gate * dxn).astype(BF16)
        dg = jnp.sum(dxn * out, axis=0, keepdims=True)
        ls = jnp.broadcast_to(jnp.sum(diff * diff) * (0.5 / D), (1, 128))

        @pl.when(i == 0)
        def _():
            dg_ref[...] = dg
            loss_ref[...] = ls

        @pl.when(i > 0)
        def _():
            dg_ref[...] += dg
            loss_ref[...] += ls

    def cols(width, col0):
        return pl.BlockSpec((tm, width), lambda b, i: (b * bpb + i, col0 // width))

    def whole(rows):
        return pl.BlockSpec((rows, D), lambda b, i: (0, 0))

    row, wide = cols(D, 0), cols(RH * DV, 0)
    gates = [cols(DV, RG + h * DV) for h in range(RH)]
    merge_gates = [cols(hw, MR), cols(hw, MR + hw), cols(hw, MA), cols(hw, MA + hw)]
    return pl.pallas_call(
        body, name="merge_out", grid=(nb, bpb),
        in_specs=[wide, wide] + gates + [whole(RH * DV), row, whole(D)] + merge_gates
        + [whole(D), row, row, pl.BlockSpec((None, 1, D), lambda b, i: (b, 0, 2))],
        out_specs=[wide, row, row, row, row, row, pl.BlockSpec((None, 1, D), lambda b, i: (b, 0, 0)),
                   pl.BlockSpec((None, 1, 128), lambda b, i: (b, 0, 0))],
        out_shape=[_sds((t_rows, RH * DV), BF16)] + [_sds((t_rows, D), BF16)] * 3
        + [_sds((t_rows, D), F32), _sds((t_rows, D), BF16), _sds((nb, 1, D), F32), _sds((nb, 1, 128), F32)],
        compiler_params=_params(("parallel", "arbitrary")),
    )(o_f, o_b, *([px] * RH), w_o_ret16, yatt16, w_o_att16, px, px, px, px, w_out16, x2, tgt, mod3)


def _bwd_branches(dout16, w_out16, w_o_ret16, w_o_att16, px, a_ret, a_att, o_f, o_b, o_att, tm):
    t_rows = dout16.shape[0]
    hw = D // 2

    def body(do_ref, wo_ref, wr_ref, wa_ref, mr0, mr1, ma0, ma1, ar_ref, aa_ref, rg0, rg1, rg2, rg3, of_ref, ob_ref,
             ag0, ag1, oa_ref, dar_ref, daa_ref, dmr_ref, dma_ref, dor_ref, drg_ref, dao_ref, dag_ref):
        dy_all = _dot(do_ref[...], wo_ref[...], NT)
        for j, (mr_ref, ma_ref) in enumerate(((mr0, ma0), (mr1, ma1))):
            sl = slice(j * hw, (j + 1) * hw)
            dy = dy_all[:, sl]
            sr = _sig(mr_ref[...].astype(F32))
            sa = _sig(ma_ref[...].astype(F32))
            dar_ref[:, sl] = (dy * sr).astype(BF16)
            daa_ref[:, sl] = (dy * sa).astype(BF16)
            dmr_ref[:, sl] = (dy * ar_ref[:, sl].astype(F32) * sr * (1.0 - sr)).astype(BF16)
            dma_ref[:, sl] = (dy * aa_ref[:, sl].astype(F32) * sa * (1.0 - sa)).astype(BF16)
        da_ret = dar_ref[...]
        for h, g_ref in enumerate((rg0, rg1, rg2, rg3)):
            sl = slice(h * DV, (h + 1) * DV)
            dy = _dot(da_ret, wr_ref[sl, :], NT)
            g = g_ref[...].astype(F32)
            o = of_ref[:, sl].astype(F32) + ob_ref[:, sl].astype(F32)
            r = lax.rsqrt(jnp.mean(o * o, axis=-1, keepdims=True) + EPS)
            on = o * r
            sg = _sig(g)
            don = dy * (g * sg)
            drg_ref[:, sl] = (dy * on * (sg * (1.0 + g * (1.0 - sg)))).astype(BF16)
            dor_ref[:, sl] = (r * (don - on * jnp.mean(on * don, axis=-1, keepdims=True))).astype(BF16)
        dy_all = _dot(daa_ref[...], wa_ref[...], NT)
        for j, g_ref in enumerate((ag0, ag1)):
            sl = slice(j * hw, (j + 1) * hw)
            dy = dy_all[:, sl]
            g = g_ref[...].astype(F32)
            sg = _sig(g)
            dao_ref[:, sl] = dy * (g * sg)
            dag_ref[:, sl] = (dy * oa_ref[:, sl] * (sg * (1.0 + g * (1.0 - sg)))).astype(BF16)

    def gate(h):
        return pl.BlockSpec((tm, DV), lambda i: (i, RG // DV + h))

    def whole(rows):
        return pl.BlockSpec((rows, D), lambda i: (0, 0))

    row = pl.BlockSpec((tm, D), lambda i: (i, 0))
    wide = pl.BlockSpec((tm, RH * DV), lambda i: (i, 0))
    return pl.pallas_call(
        body, name="bwd_branches", grid=(t_rows // tm,),
        in_specs=[row, whole(D), whole(RH * DV), whole(D)] + _gate_specs(tm, MR) + _gate_specs(tm, MA) + [row, row]
        + [gate(h) for h in range(RH)] + [wide, wide] + _gate_specs(tm, AG) + [row],
        out_specs=[row] * 4 + [wide, wide, row, row],
        out_shape=[_sds((t_rows, D), BF16)] * 4 + [_sds((t_rows, RH * DV), BF16)] * 2
        + [_sds((t_rows, D), F32), _sds((t_rows, D), BF16)],
        compiler_params=_params(("parallel",)),
    )(dout16, w_out16, w_o_ret16, w_o_att16, px, px, px, px, a_ret, a_att, *([px] * RH), o_f, o_b, px, px, o_att)


def _att_bwd(q16, kx16, kc16, px, dao, o_att, lse, nb, seq, cx, tq, after=()):
    t_rows = nb * seq
    nq = seq // tq
    rep = HQ // HKV
    gw = rep * HD
    scale = HD ** -0.5

    def body(q_ref, kx_ref, kc_ref, vx_ref, vc_ref, dao_ref, o_ref, l_ref, *rest):
        dq_ref, dkx_ref, dvx_ref, dkc_ref, dvc_ref = rest[-5:]
        i = pl.program_id(2)
        kx = kx_ref[...]
        kc = kc_ref[...]
        vx = vx_ref[...].astype(BF16)
        vc = vc_ref[...].astype(BF16)
        dkx = jnp.zeros((seq, HD), F32)
        dvx = jnp.zeros((seq, HD), F32)
        dkc = jnp.zeros((cx, HD), F32)
        dvc = jnp.zeros((cx, HD), F32)
        for r in range(rep):
            sl = slice(r * HD, (r + 1) * HD)
            q = q_ref[:, sl]
            lr = l_ref[:, r:r + 1]
            p1 = jnp.exp2(_dot(q, kx, NT) * SM_C - lr)
            p2 = jnp.exp2(_dot(q, kc, NT) * SM_C - lr)
            da = dao_ref[:, sl]
            da16 = da.astype(BF16)
            delta = jnp.sum(da * o_ref[:, sl], axis=-1, keepdims=True)
            ds1 = (p1 * (_dot(da16, vx, NT) - delta)).astype(BF16)
            ds2 = (p2 * (_dot(da16, vc, NT) - delta)).astype(BF16)
            dq_ref[:, sl] = (_dot(ds1, kx) + _dot(ds2, kc)) * scale
            dkx += _dot(ds1, q, TN)
            dkc += _dot(ds2, q, TN)
            dvx += _dot(p1.astype(BF16), da16, TN)
            dvc += _dot(p2.astype(BF16), da16, TN)
        dkx = dkx * scale
        dkc = dkc * scale

        @pl.when(i == 0)
        def _():
            dkx_ref[...] = dkx
            dvx_ref[...] = dvx
            dkc_ref[...] = dkc
            dvc_ref[...] = dvc

        @pl.when(i > 0)
        def _():
            dkx_ref[...] += dkx
            dvx_ref[...] += dvx
            dkc_ref[...] += dkc
            dvc_ref[...] += dvc

    qblk = pl.BlockSpec((tq, gw), lambda b, g, i: (b * nq + i, g))
    kxb = pl.BlockSpec((None, seq, HD), lambda b, g, i: (b, 0, g))
    kcb = pl.BlockSpec((None, cx, HD), lambda b, g, i: (b, 0, g))
    return pl.pallas_call(
        body, name="att_bwd", grid=(nb, HKV, nq),
        in_specs=[qblk,
                  pl.BlockSpec((seq, HD), lambda b, g, i: (b, g)),
                  pl.BlockSpec((cx, HD), lambda b, g, i: (b, g)),
                  pl.BlockSpec((seq, HD), lambda b, g, i: (b, AV // HD + g)),
                  pl.BlockSpec((cx, HD), lambda b, g, i: (t_rows // cx + b, AV // HD + g)),
                  qblk, qblk, pl.BlockSpec((tq, 128), lambda b, g, i: (b * nq + i, g))]
        + [pl.BlockSpec(memory_space=pl.ANY)] * len(after),
        out_specs=[qblk, kxb, kxb, kcb, kcb],
        out_shape=[_sds((t_rows, D), F32), _sds((nb, seq, HKV * HD), F32), _sds((nb, seq, HKV * HD), F32),
                   _sds((nb, cx, HKV * HD), F32), _sds((nb, cx, HKV * HD), F32)],
        compiler_params=_params(("parallel", "parallel", "arbitrary")),
    )(q16, kx16, kc16, px, px, dao, o_att, lse, *after)


def _qk_prep_bwd(dt, px, nw, cos, sin, rows, row_off, col_off, heads, hb, seq, tm, name):
    rope = cos is not None
    rb0 = row_off // tm
    pb = seq // tm if rope else 1
    bw = hb * HD

    def body(*refs):
        if rope:
            d_ref, x_ref, w_ref, c_ref, s_ref, dx_ref, dw_ref = refs
        else:
            d_ref, x_ref, w_ref, dx_ref, dw_ref = refs
        first = jnp.logical_and(pl.program_id(0) == 0, pl.program_id(1) == 0)
        dw = jnp.zeros((1, HD), F32)
        for h in range(hb):
            sl = slice(h * HD, (h + 1) * HD)
            dtv = d_ref[:, sl]
            if rope:
                dtv = dtv * c_ref[...] + _swap_pairs(dtv * s_ref[...])
            xv = x_ref[:, sl].astype(F32)
            r = lax.rsqrt(jnp.mean(xv * xv, axis=-1, keepdims=True) + EPS)
            xh = xv * r
            dxh = dtv * w_ref[...]
            dx_ref[:, sl] = (r * (dxh - xh * jnp.mean(dxh * xh, axis=-1, keepdims=True))).astype(BF16)
            dw += jnp.sum(dtv * xh, axis=0, keepdims=True)

        @pl.when(first)
        def _():
            dw_ref[...] = dw

        @pl.when(jnp.logical_not(first))
        def _():
            dw_ref[...] += dw

    blk = pl.BlockSpec((tm, bw), lambda i, j: (i, j))
    in_specs = [blk, pl.BlockSpec((tm, bw), lambda i, j: (rb0 + i, col_off // bw + j)),
                pl.BlockSpec((1, HD), lambda i, j: (0, 0))]
    args = [dt, px, nw]
    if rope:
        in_specs += [pl.BlockSpec((tm, HD), lambda i, j: (i % pb, 0))] * 2
        args += [cos, sin]
    return pl.pallas_call(
        body, name=name, grid=(rows // tm, heads // hb), in_specs=in_specs,
        out_specs=[blk, pl.BlockSpec((1, HD), lambda i, j: (0, 0))],
        out_shape=[_sds((rows, heads * HD), BF16), _sds((1, HD), F32)],
        compiler_params=_params(("arbitrary", "arbitrary")),
    )(*args)


def _ret_bwd(px, lg, do16, hist_f, hist_b, nb, nc):
    t_rows = nb * nc * CH

    def body(lg_ref, *refs):
        ins = (refs[0:5], refs[7:12])
        do_refs = (refs[5], refs[12])
        h_refs = (refs[6], refs[13])
        outs = (refs[14:17], refs[17:20])
        ds_outs = (refs[20], refs[21])
        dlg_ref = refs[22]
        dss = (refs[23], refs[24])
        c = pl.program_id(1)

        @pl.when(c == 0)
        def _():
            dss[0][...] = jnp.zeros_like(dss[0])
            dss[1][...] = jnp.zeros_like(dss[1])
            dlg_ref[...] = jnp.zeros_like(dlg_ref)

        for d in range(2):
            dq_ref, dk_ref, dv_ref = outs[d]
            for h in range(RH):
                lg_d = lg_ref[d, h]
                mask, relf, qd, qe, kd, ke = _decays(lg_d, d == 0)
                g_ch = jnp.exp(lg_d * CH)
                q, k, v16 = _head_qkv(ins[d], h)
                q16 = q.astype(BF16)
                k16 = k.astype(BF16)
                do16v = do_refs[d][:, h * DV:(h + 1) * DV]
                st16 = h_refs[d][h]
                dst = dss[d][h]
                dst16 = dst.astype(BF16)
                a = _dot(q16, k16, NT) * mask
                dp = _dot(do16v, v16, NT)
                da16 = (dp * mask).astype(BF16)
                dq_cross = _dot(do16v, st16, NT) * qd
                dq_ref[:, h * DK:(h + 1) * DK] = (_dot(da16, k16) + dq_cross).astype(BF16)
                dk_state = _dot(v16, dst16, NT) * kd
                dk_ref[:, h * DK:(h + 1) * DK] = ((_dot(da16, q16, TN) + dk_state) * (DK ** -0.5)).astype(BF16)
                dv = _dot(a.astype(BF16), do16v, TN) + _dot((k * kd).astype(BF16), dst16)
                dv_ref[:, h * DV:(h + 1) * DV] = dv.astype(BF16)
                dlg = (jnp.sum(relf * a * dp)
                       + jnp.sum(qe * jnp.sum(q * dq_cross, axis=-1, keepdims=True))
                       + jnp.sum(ke * jnp.sum(k * dk_state, axis=-1, keepdims=True))
                       + CH * g_ch * jnp.sum(dst * st16.astype(F32)))
                row = d * RH + h
                dlg_ref[row:row + 1, :] += jnp.broadcast_to(dlg, (1, 128))
                ds_new = g_ch * dst + _dot((q * qd).astype(BF16), do16v, TN)
                dss[d][h] = ds_new

                @pl.when(c == nc - 1)
                def _():
                    ds_outs[d][h] = ds_new

    def fw(b, c):
        return b * nc + nc - 1 - c

    def bw(b, c):
        return b * nc + c

    def rows(rowf, width):
        return pl.BlockSpec((CH, width), lambda b, c: (rowf(b, c), 0))

    def hist(rowf):
        return pl.BlockSpec((None, None, RH, DK, DV), lambda b, c: (b, rowf(0, c), 0, 0, 0))

    st = pl.BlockSpec((None, RH, DK, DV), lambda b, c: (b, 0, 0, 0))
    in_specs = [pl.BlockSpec(memory_space=pltpu.SMEM)]
    out_specs = []
    for rowf in (fw, bw):
        in_specs += _wide_specs(rowf) + [rows(rowf, RH * DV), hist(rowf)]
        out_specs += [rows(rowf, RH * DK), rows(rowf, RH * DK), rows(rowf, RH * DV)]
    out_specs += [st, st, pl.BlockSpec((None, 8, 128), lambda b, c: (b, 0, 0))]
    qk = _sds((t_rows, RH * DK), BF16)
    vv = _sds((t_rows, RH * DV), BF16)
    return pl.pallas_call(
        body, name="ret_bwd", grid=(nb, nc), in_specs=in_specs, out_specs=out_specs,
        out_shape=[qk, qk, vv, qk, qk, vv, _sds((nb, RH, DK, DV), F32), _sds((nb, RH, DK, DV), F32),
                   _sds((nb, 8, 128), F32)],
        scratch_shapes=[pltpu.VMEM((RH, DK, DV), F32), pltpu.VMEM((RH, DK, DV), F32)],
        compiler_params=_params(("parallel", "arbitrary")),
    )(lg, *([px] * 5), do16, hist_f, *([px] * 5), do16, hist_b)


def _ctx_state_bwd(px, lg, ds_f, ds_b, nb, t_rows, cx):
    rb = t_rows // cx

    def body(lg_ref, k_ref, v_ref, dsf_ref, dsb_ref, dk_ref, dv_ref, dlg_ref):
        h = pl.program_id(1)
        pos = lax.broadcasted_iota(jnp.int32, (cx, 1), 0).astype(F32)
        k = k_ref[...].astype(F32) * (DK ** -0.5)
        v16 = v_ref[...].astype(BF16)
        dk = jnp.zeros((cx, DK), F32)
        dv = jnp.zeros((cx, DV), F32)
        dlg_ref[...] = jnp.zeros_like(dlg_ref)
        for d, (ds_ref, e) in enumerate(((dsf_ref, cx - 1.0 - pos), (dsb_ref, pos))):
            w = jnp.exp(lg_ref[d, h] * e)
            ds16 = ds_ref[...].astype(BF16)
            t = _dot(v16, ds16, NT)
            dk += t * w
            dv += _dot((k * w).astype(BF16), ds16)
            dlg = jnp.sum(e * w * jnp.sum(k * t, axis=-1, keepdims=True))
            dlg_ref[d:d + 1, :] = jnp.broadcast_to(dlg, (1, 128))
        dk_ref[...] = (dk * (DK ** -0.5)).astype(BF16)
        dv_ref[...] = dv.astype(BF16)

    st = pl.BlockSpec((None, None, DK, DV), lambda b, h: (b, h, 0, 0))
    return pl.pallas_call(
        body, name="ctx_state_bwd", grid=(nb, RH),
        in_specs=[pl.BlockSpec(memory_space=pltpu.SMEM),
                  pl.BlockSpec((cx, DK), lambda b, h: (rb + b, RK // DK + h)),
                  pl.BlockSpec((cx, DV), lambda b, h: (rb + b, RV // DV + h)), st, st],
        out_specs=[pl.BlockSpec((cx, DK), lambda b, h: (b, h)), pl.BlockSpec((cx, DV), lambda b, h: (b, h)),
                   pl.BlockSpec((None, None, 8, 128), lambda b, h: (b, h, 0, 0))],
        out_shape=[_sds((nb * cx, RH * DK), BF16), _sds((nb * cx, RH * DV), BF16), _sds((nb, RH, 8, 128), F32)],
        compiler_params=_params(("parallel", "parallel")),
    )(lg, px, px, ds_f, ds_b)


def _assemble_lat(rows_all, dk_f, dk_b, dv_f, dv_b, dak16, dvx, dq_f, dq_b, drg16, daq16, dag16, dmr16, dma16, tm):
    t_rows = dk_f.shape[0]

    def body(dkf, dkb, dvf, dvb, dak, dav, dqf, dqb, drg, daq, dag, dmr, dma, o_ref):
        o_ref[:, RK:RK + RH * DK] = (dkf[...].astype(F32) + dkb[...].astype(F32)).astype(BF16)
        o_ref[:, RV:RV + RH * DV] = (dvf[...].astype(F32) + dvb[...].astype(F32)).astype(BF16)
        o_ref[:, AK:AK + HKV * HD] = dak[...]
        o_ref[:, AV:AV + HKV * HD] = dav[...].astype(BF16)
        o_ref[:, RQ:RQ + RH * DK] = (dqf[...].astype(F32) + dqb[...].astype(F32)).astype(BF16)
        o_ref[:, RG:RG + RH * DV] = drg[...]
        o_ref[:, AQ:AQ + D] = daq[...]
        o_ref[:, AG:AG + D] = dag[...]
        o_ref[:, MR:MR + D] = dmr[...]
        o_ref[:, MA:MA + D] = dma[...]

    args = (dk_f, dk_b, dv_f, dv_b, dak16, dvx, dq_f, dq_b, drg16, daq16, dag16, dmr16, dma16)
    return pl.pallas_call(
        body, name="assemble_lat", grid=(t_rows // tm,),
        in_specs=[pl.BlockSpec((tm, a.shape[1]), lambda i: (i, 0)) for a in args],
        out_specs=pl.BlockSpec((tm, IN_COLS), lambda i: (i, 0)), out_shape=_sds((rows_all, IN_COLS), BF16),
        compiler_params=_params(("parallel",)),
    )(*args)


def _assemble_ctx(dp_all, dck16, dcv16, dcak16, dvc, t_rows, tm):
    c_rows = dck16.shape[0]
    rb = t_rows // tm

    def body(_, dck, dcv, dcak, dcav, o_ref):
        o_ref[:, RK:RK + RH * DK] = dck[...]
        o_ref[:, RV:RV + RH * DV] = dcv[...]
        o_ref[:, AK:AK + HKV * HD] = dcak[...]
        o_ref[:, AV:AV + HKV * HD] = dcav[...].astype(BF16)
        o_ref[:, KV_COLS:] = jnp.zeros((tm, IN_COLS - KV_COLS), BF16)

    args = (dck16, dcv16, dcak16, dvc)
    return pl.pallas_call(
        body, name="assemble_ctx", grid=(c_rows // tm,),
        in_specs=[pl.BlockSpec(memory_space=pl.ANY)]
        + [pl.BlockSpec((tm, a.shape[1]), lambda i: (i, 0)) for a in args],
        out_specs=pl.BlockSpec((tm, IN_COLS), lambda i: (rb + i, 0)), out_shape=_sds(dp_all.shape, BF16),
        input_output_aliases={0: 0},
        compiler_params=_params(("parallel",)),
    )(dp_all, *args)


def _norm_bwd(dh, x2, mod3, norm_w, dxn, row_off, rows_per_group, group0, tm, name):
    with_dx = dxn is not None
    rows = x2.shape[0]
    rb0 = row_off // tm
    bpg = rows_per_group // tm
    ngroups = rows // rows_per_group

    def body(*refs):
        if with_dx:
            dh_ref, x_ref, sc_ref, nw_ref, dxn_ref, dx_ref, dsh_ref, dsc_ref, dnw_ref = refs
        else:
            dh_ref, x_ref, sc_ref, nw_ref, dsh_ref, dsc_ref, dnw_ref = refs
        i = pl.program_id(0)
        dhv = dh_ref[...]
        xv = x_ref[...]
        nw = nw_ref[...]
        r = lax.rsqrt(jnp.mean(xv * xv, axis=-1, keepdims=True) + EPS)
        xh = xv * r
        dm = dhv * (1.0 + sc_ref[...])
        dsh = jnp.sum(dhv, axis=0, keepdims=True)
        dsc = jnp.sum(dhv * (xh * nw), axis=0, keepdims=True)
        dnw = jnp.sum(dm * xh, axis=0, keepdims=True)
        if with_dx:
            dxh = dm * nw
            dx_ref[...] = dxn_ref[...] + r * (dxh - xh * jnp.mean(dxh * xh, axis=-1, keepdims=True))

        @pl.when(i % bpg == 0)
        def _():
            dsh_ref[...] = dsh
            dsc_ref[...] = dsc

        @pl.when(i % bpg != 0)
        def _():
            dsh_ref[...] += dsh
            dsc_ref[...] += dsc

        @pl.when(i == 0)
        def _():
            dnw_ref[...] = dnw

        @pl.when(i > 0)
        def _():
            dnw_ref[...] += dnw

    grp = pl.BlockSpec((None, 1, D), lambda i: (i // bpg, 0, 0))
    in_specs = [pl.BlockSpec((tm, D), lambda i: (rb0 + i, 0)), pl.BlockSpec((tm, D), lambda i: (i, 0)),
                pl.BlockSpec((None, 1, D), lambda i: (group0 + i // bpg, 0, 1)),
                pl.BlockSpec((1, D), lambda i: (0, 0))]
    args = [dh, x2, mod3, norm_w]
    out_specs = [grp, grp, pl.BlockSpec((1, D), lambda i: (0, 0))]
    out_shape = [_sds((ngroups, 1, D), F32), _sds((ngroups, 1, D), F32), _sds((1, D), F32)]
    if with_dx:
        in_specs.append(pl.BlockSpec((tm, D), lambda i: (i, 0)))
        args.append(dxn)
        out_specs.insert(0, pl.BlockSpec((tm, D), lambda i: (i, 0)))
        out_shape.insert(0, _sds((rows, D), F32))
    return pl.pallas_call(
        body, name=name, grid=(rows // tm,), in_specs=in_specs, out_specs=out_specs, out_shape=out_shape,
        compiler_params=_params(("arbitrary",)),
    )(*args)


def _small_final(dmod_all, dmodc_parts, c_rows, dm_loc_rows, nw_parts, misc_parts, c_ctx, r_pad, w_ada16):
    loc = dm_loc_rows.shape[1]

    def body(dm_ref, dmc_ref, c_ref, dml_ref, nwp_ref, mp_ref, cc_ref, r_ref, w_ref,
             gb_ref, gc_ref, gnw_ref, misc_ref, gwa_ref):
        dmc = jnp.sum(dmc_ref[...], axis=0, keepdims=True)
        gb_ref[...] = jnp.sum(dm_ref[...], axis=0, keepdims=True) + dmc
        dsc = _dot(jnp.broadcast_to(dmc, (8, 3 * D)).astype(BF16), w_ref[...], NT)[0:1, :]
        gc_ref[...] = dsc * _dsilu(cc_ref[...])
        gnw_ref[...] = jnp.sum(nwp_ref[...], axis=0, keepdims=True)
        misc = jnp.sum(mp_ref[...], axis=0, keepdims=True)
        y = jnp.exp2(r_ref[...])
        lane = lax.broadcasted_iota(jnp.int32, (1, D), 1)
        is_decay = jnp.logical_and(lane >= 2 * HD, lane < 2 * HD + 2 * RH)
        misc_ref[...] = misc * jnp.where(is_decay, -(y * np.float32(np.log(2.0))) / (1.0 - y), 1.0)
        gwa_ref[...] = _dot(_silu(c_ref[...]).astype(BF16), dml_ref[...].astype(BF16), TN)

    return pl.pallas_call(
        body, name="small_final",
        out_shape=[_sds((1, 3 * D), F32), _sds((1, D), F32), _sds((1, D), F32), _sds((1, D), F32), _sds((D, loc), F32)],
        compiler_params=pltpu.CompilerParams(vmem_limit_bytes=VMEM_LIMIT),
    )(dmod_all, dmodc_parts, c_rows, dm_loc_rows, nw_parts, misc_parts, c_ctx, r_pad, w_ada16)


def _adamw_math(w, g, m, v):
    nm = B1 * m + (1.0 - B1) * g
    nv = B2 * v + (1.0 - B2) * (g * g)
    return -LR * ((nm / (1.0 - B1 ** STEP)) / (jnp.sqrt(nv / (1.0 - B2 ** STEP)) + ADAM_EPS) + WD * w), nm, nv


def _adamw(w, g, m, v, name):
    rows, cols = w.shape
    tm = _pick(rows, 448, 8)

    def body(w_ref, g_ref, m_ref, v_ref, d_ref, nm_ref, nv_ref):
        d_ref[...], nm_ref[...], nv_ref[...] = _adamw_math(w_ref[...], g_ref[...], m_ref[...], v_ref[...])

    blk = pl.BlockSpec((tm, cols), lambda i: (i, 0))
    return pl.pallas_call(
        body, name=name, grid=(rows // tm,), in_specs=[blk] * 4, out_specs=[blk] * 3,
        out_shape=[_sds((rows, cols), F32)] * 3, compiler_params=_params(("parallel",)),
    )(w, g, m, v)


def _mesh_pos():
    return lax.axis_index("x"), lax.axis_index("y"), lax.axis_index("c")


def _all_gather(arrs, name):
    n = len(arrs)

    def body(*refs):
        ins, outs = refs[:n], refs[n:2 * n]
        send_sems, recv_sems, local_sems = refs[2 * n:]
        x, y, c = _mesh_pos()
        me, sib = (x, y, c), (x, y, 1 - c)
        chips = [(1 - x, y), (x, 1 - y), (1 - x, 1 - y)]

        def slot(p):
            return 4 * p[0] + 2 * p[1] + p[2]

        def copy(a, k, block, to, own):
            dst = outs[a].at[slot(block)]
            return pltpu.make_async_remote_copy(
                src_ref=ins[a] if own else dst, dst_ref=dst, send_sem=send_sems.at[a, k], recv_sem=recv_sems.at[a, k],
                device_id=to, device_id_type=MESH_T)

        mine = [pltpu.make_async_copy(ins[a], outs[a].at[slot(me)], local_sems.at[a]) for a in range(n)]
        for cp in mine:
            cp.start()
        first = []
        for a in range(n):
            first.append(copy(a, 0, me, sib, True))
            first += [copy(a, 1 + j, me, (*chip, c), True) for j, chip in enumerate(chips)]
        for cp in first:
            cp.start()
        passed = []
        for j, chip in enumerate(chips):
            for a in range(n):
                copy(a, 1 + j, (*chip, c), me, False).wait_recv()
                fwd = copy(a, 4 + j, (*chip, c), sib, False)
                fwd.start()
                passed.append(fwd)
        for a in range(n):
            copy(a, 0, sib, me, False).wait_recv()
            for j, chip in enumerate(chips):
                copy(a, 4 + j, (*chip, 1 - c), me, False).wait_recv()
        for cp in first + passed:
            cp.wait_send()
        for cp in mine:
            cp.wait()

    hbm = pl.BlockSpec(memory_space=pl.ANY)
    return pl.pallas_call(
        body, name=name, in_specs=[hbm] * n, out_specs=[hbm] * n,
        out_shape=[_sds((N_DEV,) + a.shape, a.dtype) for a in arrs],
        scratch_shapes=[pltpu.SemaphoreType.DMA((n, 7)), pltpu.SemaphoreType.DMA((n, 7)), pltpu.SemaphoreType.DMA((n,))],
    )(*arrs)


def _pair_exchange(arrs, name):
    n = len(arrs)

    def body(*refs):
        ins, outs = refs[:n], refs[n:2 * n]
        send_sems, recv_sems = refs[2 * n:]
        x, y, c = _mesh_pos()
        sib = (x, y, 1 - c)
        sends = []
        for a in range(n):
            for k in range(4):
                sends.append(pltpu.make_async_remote_copy(
                    src_ref=ins[a].at[2 * k + 1 - c], dst_ref=outs[a].at[k], send_sem=send_sems.at[a, k],
                    recv_sem=recv_sems.at[a, k], device_id=sib, device_id_type=MESH_T))
        for cp in sends:
            cp.start()
        for cp in sends:
            cp.wait_recv()
        for cp in sends:
            cp.wait_send()

    hbm = pl.BlockSpec(memory_space=pl.ANY)
    return pl.pallas_call(
        body, name=name, in_specs=[hbm] * n, out_specs=[hbm] * n,
        out_shape=[_sds((4,) + a.shape[1:], a.dtype) for a in arrs],
        scratch_shapes=[pltpu.SemaphoreType.DMA((n, 4)), pltpu.SemaphoreType.DMA((n, 4))],
    )(*arrs)


def _pair_add(part, got, core, name):
    _, rows, cols = part.shape
    tm = _pick(rows, 672, 16)
    p4 = part.reshape(4, 2, rows, cols)

    def body(core_ref, p_ref, g_ref, o_ref):
        o_ref[...] = (p_ref[...].astype(F32) + g_ref[...].astype(F32)).astype(BF16)

    blk = pl.BlockSpec((None, tm, cols), lambda k, i, cr: (k, i, 0))
    return pl.pallas_call(
        body, name=name,
        grid_spec=pltpu.PrefetchScalarGridSpec(
            num_scalar_prefetch=1, grid=(4, rows // tm),
            in_specs=[pl.BlockSpec((None, None, tm, cols), lambda k, i, cr: (k, cr[0], i, 0)), blk], out_specs=blk),
        out_shape=_sds((4, rows, cols), BF16), compiler_params=_params(("parallel", "parallel")),
    )(core, p4, got)


def _chip_sum_adamw(pair_sums, landed, chip, w, m, v, name):
    _, rows, cols = pair_sums.shape
    tm = _pick(rows, 448, 16)

    def body(chip_ref, s_ref, l_ref, w_ref, m_ref, v_ref, g_ref, d_ref, nm_ref, nv_ref):
        acc = s_ref[...].astype(F32)
        for j in range(3):
            acc = acc + l_ref[j].astype(F32)
        g_ref[...] = acc
        d_ref[...], nm_ref[...], nv_ref[...] = _adamw_math(w_ref[...], acc, m_ref[...], v_ref[...])

    blk = pl.BlockSpec((tm, cols), lambda i, ch: (i, 0))
    return pl.pallas_call(
        body, name=name,
        grid_spec=pltpu.PrefetchScalarGridSpec(
            num_scalar_prefetch=1, grid=(rows // tm,),
            in_specs=[pl.BlockSpec((None, tm, cols), lambda i, ch: (ch[0], i, 0)),
                      pl.BlockSpec((3, tm, cols), lambda i, ch: (0, i, 0)), blk, blk, blk],
            out_specs=[blk] * 4),
        out_shape=[_sds((rows, cols), F32)] * 4, compiler_params=_params(("parallel",)),
    )(chip, pair_sums, landed, w, m, v)


_HBM = pl.BlockSpec(memory_space=pltpu.HBM)
_SEM = pl.BlockSpec(memory_space=pltpu.SEMAPHORE)
_EFFECT = pltpu.SideEffectType.DATAFLOW_SIDE_EFFECTING


def _chip_routes(n):
    def plan(x, y, c):
        routes = []
        for a in range(n):
            for j in range(1, 4):
                px, py = x ^ (j >> 1), y ^ (j & 1)
                routes.append((a, 2 * px + py, (px, py, c), j - 1))
        return routes
    return plan, 3 * n


def _bcast_routes(n):
    def plan(x, y, c):
        routes = []
        for a in range(n):
            for k in range(1, N_DEV):
                peer = (x ^ ((k >> 2) & 1), y ^ ((k >> 1) & 1), c ^ (k & 1))
                routes.append((a, 0, peer, 4 * x + 2 * y + c))
        return routes
    return plan, 7 * n


def _route_copies(srcs, lands, send_sems, recv_sems, routes):
    return [pltpu.make_async_remote_copy(
        src_ref=srcs[a].at[sb], dst_ref=lands[a].at[lb], send_sem=send_sems.at[r], recv_sem=recv_sems.at[r],
        device_id=peer, device_id_type=MESH_T) for r, (a, sb, peer, lb) in enumerate(routes)]


def _exchange_start(srcs, lands, routes, name, after=()):
    plan, count = routes
    n = len(srcs)
    n_in = 2 * n + len(after)

    def body(*refs):
        send_sems, recv_sems = refs[n_in], refs[n_in + 1]
        token = refs[-1]
        for cp in _route_copies(refs[:n], refs[n:2 * n], send_sems, recv_sems, plan(*_mesh_pos())):
            cp.start()
        token[...] = jnp.zeros_like(token)

    args = [pltpu.with_memory_space_constraint(a, pltpu.HBM) for a in list(srcs) + list(lands)]
    out = pl.pallas_call(
        body, name=name,
        out_shape=(pltpu.SemaphoreType.DMA((count,)), pltpu.SemaphoreType.DMA((count,)),
                   *[pltpu.HBM(a.shape, a.dtype) for a in args], _sds((8, 128), F32)),
        in_specs=[_HBM] * (2 * n) + [pl.BlockSpec(memory_space=pl.ANY)] * len(after),
        out_specs=(_SEM, _SEM, *([_HBM] * (2 * n)), pl.BlockSpec(memory_space=pltpu.VMEM)),
        input_output_aliases={i: 2 + i for i in range(2 * n)},
        compiler_params=pltpu.CompilerParams(has_side_effects=_EFFECT),
    )(*args, *after)
    return (out[0], out[1], list(out[2:2 + 2 * n]), routes), out[-1]


def _exchange_wait(state, after, name):
    send_sems, recv_sems, bufs, (plan, count) = state
    n = len(bufs) // 2

    def body(*refs):
        send_s, recv_s = refs[2 * n], refs[2 * n + 1]
        for cp in _route_copies(refs[:n], refs[n:2 * n], send_s, recv_s, plan(*_mesh_pos())):
            cp.wait_send()
            cp.wait_recv()

    out = pl.pallas_call(
        body, name=name, out_shape=tuple(pltpu.HBM(a.shape, a.dtype) for a in bufs),
        in_specs=[_HBM] * (2 * n) + [_SEM, _SEM, pl.BlockSpec(memory_space=pl.ANY)], out_specs=tuple([_HBM] * (2 * n)),
        input_output_aliases={i: i for i in range(2 * n)},
        compiler_params=pltpu.CompilerParams(has_side_effects=_EFFECT),
    )(*bufs, send_sems, recv_sems, after)
    return list(out[:n]), list(out[n:])


def _group_routes(js):
    def plan(x, y, c):
        return [(0, 0, (x ^ (j >> 1), y ^ (j & 1), c), 2 * j + c) for j in js]
    return plan, len(js)


def _pair_fill(groups, js, name, after=()):
    def body(*refs):
        g_ref, send_sems, recv_sems = refs[-3:]
        x, y, c = _mesh_pos()
        sends = []
        for n, j in enumerate(js):
            mine = g_ref.at[2 * j + c]
            sends.append(pltpu.make_async_remote_copy(
                src_ref=mine, dst_ref=mine, send_sem=send_sems.at[n], recv_sem=recv_sems.at[n],
                device_id=(x, y, 1 - c), device_id_type=MESH_T))
        for cp in sends:
            cp.start()
        for n, j in enumerate(js):
            pltpu.make_async_remote_copy(
                src_ref=g_ref.at[2 * j + c], dst_ref=g_ref.at[2 * j + 1 - c], send_sem=send_sems.at[n],
                recv_sem=recv_sems.at[n], device_id=(x, y, 1 - c), device_id_type=MESH_T).wait_recv()
        for cp in sends:
            cp.wait_send()

    hbm = pl.BlockSpec(memory_space=pl.ANY)
    return pl.pallas_call(
        body, name=name, in_specs=[hbm] * (1 + len(after)), out_specs=hbm, out_shape=_sds(groups.shape, groups.dtype),
        input_output_aliases={0: 0},
        scratch_shapes=[pltpu.SemaphoreType.DMA((len(js),)), pltpu.SemaphoreType.DMA((len(js),))],
    )(groups, *after)


def _in_proj_group(h_all, groups, j0, ng, chip, px_prev, after, name):
    rows_all = h_all.shape[0]
    gcols = IN_COLS // 4
    tm = _pick(rows_all, 1536, 128)
    g4 = groups.reshape(4, gcols, D)

    n_lead = (1 if px_prev is not None else 0) + len(after)
    lead = ([px_prev] if px_prev is not None else []) + list(after)

    def body(chip_ref, *refs):
        h_ref, w_ref, o_ref = refs[n_lead:]
        o_ref[...] = _dot(h_ref[...], w_ref[...], NT).astype(BF16)

    return pl.pallas_call(
        body, name=name,
        grid_spec=pltpu.PrefetchScalarGridSpec(
            num_scalar_prefetch=1, grid=(ng, rows_all // tm),
            in_specs=[pl.BlockSpec(memory_space=pl.ANY)] * n_lead
            + [pl.BlockSpec((tm, D), lambda n, i, ch: (i, 0)),
               pl.BlockSpec((None, gcols, D), lambda n, i, ch: (j0 + n, 0, 0))],
            out_specs=pl.BlockSpec((tm, gcols), lambda n, i, ch: (i, ch[0] ^ (j0 + n)))),
        out_shape=_sds((rows_all, IN_COLS), BF16),
        input_output_aliases={1: 0} if px_prev is not None else {},
        compiler_params=_params(("parallel", "parallel")),
    )(chip, *lead, h_all, g4)


def _d_h_groups(dp_all, groups, chip, after):
    rows_all = dp_all.shape[0]
    gcols = IN_COLS // 4
    tm = _pick(rows_all, 1536, 128)
    g4 = groups.reshape(4, gcols, D)
    n_lead = len(after)

    def body(chip_ref, *refs):
        a_ref, w_ref, o_ref = refs[n_lead:]
        j = pl.program_id(1)
        part = _dot(a_ref[...], w_ref[...])

        @pl.when(j == 0)
        def _():
            o_ref[...] = part

        @pl.when(j > 0)
        def _():
            o_ref[...] += part

    return pl.pallas_call(
        body, name="d_h",
        grid_spec=pltpu.PrefetchScalarGridSpec(
            num_scalar_prefetch=1, grid=(rows_all // tm, 4),
            in_specs=[pl.BlockSpec(memory_space=pl.ANY)] * n_lead
            + [pl.BlockSpec((tm, gcols), lambda i, j, ch: (i, ch[0] ^ j)),
               pl.BlockSpec((None, gcols, D), lambda i, j, ch: (j, 0, 0))],
            out_specs=pl.BlockSpec((tm, D), lambda i, j, ch: (i, 0))),
        out_shape=_sds((rows_all, D), F32),
        compiler_params=_params(("parallel", "arbitrary")),
    )(chip, *after, dp_all, g4)


def _reduce_scatter_start(parts, core, name):
    got = _pair_exchange(parts, name + "_pair")
    sums = [_pair_add(p, g, core, "%s_add_%d" % (name, i)) for i, (p, g) in enumerate(zip(parts, got))]
    lands = [lax.empty((3,) + s_.shape[1:], BF16) for s_ in sums]
    return _exchange_start(sums, lands, _chip_routes(len(sums)), name + "_start")


def _reduce_scatter_finish(rs_state, after, chip, wmv, name):
    sums, landed = _exchange_wait(rs_state, after, name + "_wait")
    return [_chip_sum_adamw(s_, l_, chip, *t, "%s_adamw_%d" % (name, i))
            for i, (s_, l_, t) in enumerate(zip(sums, landed, wmv))]


def _local_step(x, c, ctx, norm_w, ret_log2_decay, q_norm_w, k_norm_w, loss_target,
                mod, proj_in, proj_back, get_w_o, on_out_grads, on_in_grad, started=()):
    nb, seq, _ = x.shape
    cx = ctx.shape[1]
    t_rows, c_rows = nb * seq, nb * cx
    rows_all = t_rows + c_rows
    nc = seq // CH
    tm = _pick(seq, 256, 128)
    te = _pick(seq, 512, 128)
    assert cx % tm == 0 and t_rows % cx == 0 and seq % GRID_W == 0

    x2 = x.reshape(t_rows, D)
    ctx2 = ctx.reshape(c_rows, D)
    tgt = loss_target.reshape(t_rows, D)
    lg = _log_gamma(ret_log2_decay)
    cos, sin = _rope_tables(seq)

    mod3 = mod[:, None, :]
    h_all = _norm_fwd(x2, mod3, norm_w, rows_all, 0, seq, 0, None, te, "norm_fwd", after=started)
    h_all = _norm_fwd(ctx2, mod3, norm_w, rows_all, t_rows, c_rows, nb, h_all, tm, "norm_fwd_ctx")
    px = proj_in(h_all)
    s0f, s0b = _ctx_state(px, lg, nb, t_rows, cx)
    o_f, o_b, hist_f, hist_b = _ret_fwd(px, lg, s0f, s0b, nb, nc)
    q16 = _qk_prep(px, q_norm_w, cos, sin, t_rows, 0, AQ, HQ, 4, seq, te, "q_prep")
    kx16 = _qk_prep(px, k_norm_w, cos, sin, t_rows, 0, AK, HKV, HKV, seq, te, "k_prep")
    kc16 = _qk_prep(px, k_norm_w, None, None, c_rows, t_rows, AK, HKV, HKV, seq, tm, "kc_prep")
    o_att, yatt16, lse = _att_fwd(q16, kx16, kc16, px, nb, seq, cx, te)
    w_o_ret16, w_o_att16, w_out16 = get_w_o(lse)
    yret16, a_ret, a_att, y16, dxn, dout16, dgate, loss_b = _merge_out(
        o_f, o_b, yatt16, px, w_o_ret16, w_o_att16, w_out16, x2, tgt, mod3, nb, seq, tm)

    gw_out = _matmul(y16, dout16, ta=True, tm=D, tn=D, tk=D, out_dtype=BF16, name="gw_out")
    da_ret16, da_att16, dmr16, dma16, do16, drg16, dao, dag16 = _bwd_branches(
        dout16, w_out16, w_o_ret16, w_o_att16, px, a_ret, a_att, o_f, o_b, o_att, tm)
    gw_o_ret = _matmul(yret16, da_ret16, ta=True, tm=D, tn=D, tk=D, out_dtype=BF16, name="gw_o_ret")
    gw_o_att = _matmul(yatt16, da_att16, ta=True, tm=D, tn=D, tk=D, out_dtype=BF16, name="gw_o_att")
    out_state, out_started = on_out_grads([gw_o_ret, gw_o_att, gw_out])
    dq_rot, dkx, dvx, dkc, dvc = _att_bwd(q16, kx16, kc16, px, dao, o_att, lse, nb, seq, cx, te, after=out_started)
    daq16, gq = _qk_prep_bwd(dq_rot, px, q_norm_w, cos, sin, t_rows, 0, AQ, HQ, 4, seq, te, "q_prep_bwd")
    dak16, gk_lat = _qk_prep_bwd(dkx.reshape(t_rows, HKV * HD), px, k_norm_w, cos, sin, t_rows, 0, AK, HKV, HKV, seq, te,
                                 "k_prep_bwd")
    dcak16, gk_ctx = _qk_prep_bwd(dkc.reshape(c_rows, HKV * HD), px, k_norm_w, None, None, c_rows, t_rows, AK, HKV, HKV,
                                  seq, tm, "kc_prep_bwd")
    dq_f, dk_f, dv_f, dq_b, dk_b, dv_b, ds_f, ds_b, dlg_scan = _ret_bwd(px, lg, do16, hist_f, hist_b, nb, nc)
    dck16, dcv16, dlg_ctx = _ctx_state_bwd(px, lg, ds_f, ds_b, nb, t_rows, cx)
    dp_all = _assemble_lat(rows_all, dk_f, dk_b, dv_f, dv_b, dak16, dvx.reshape(t_rows, HKV * HD), dq_f, dq_b, drg16,
                           daq16, dag16, dmr16, dma16, tm)
    dp_all = _assemble_ctx(dp_all, dck16, dcv16, dcak16, dvc.reshape(c_rows, HKV * HD), t_rows, tm)
    gw_in_t = _matmul(dp_all, h_all, ta=True, tm=1536, tn=D, tk=2304, out_dtype=BF16, name="gw_in")
    in_state, in_started = on_in_grad(gw_in_t)
    dh = proj_back(dp_all, in_started)
    grad_x, dsh, dsc, gnw_lat = _norm_bwd(dh, x2, mod3, norm_w, dxn, 0, seq, 0, te, "norm_bwd")
    dsh_c, dsc_c, gnw_ctx = _norm_bwd(dh, ctx2, mod3, norm_w, None, t_rows, c_rows, nb, tm, "norm_bwd_ctx")

    dlg = (jnp.sum(dlg_scan[:, :, 0], axis=0) + jnp.sum(dlg_ctx[:, :, :2, 0], axis=0).T.reshape(2 * RH)).reshape(1, 2 * RH)
    misc = jnp.concatenate([gq, gk_lat + gk_ctx, dlg, jnp.sum(loss_b[:, 0, 0]).reshape(1, 1),
                            jnp.zeros((1, D - 2 * HD - 2 * RH - 1), F32)], axis=1)
    rows = []
    for b in range(nb):
        rows += [dsh[b], dsc[b], dgate[b]]
    rows += [dsh_c[0], dsc_c[0]] + [c[b:b + 1] for b in range(nb)] + [gnw_lat + gnw_ctx, misc]
    payload = jnp.concatenate(rows + [jnp.zeros((PAY_ROWS - len(rows), D), F32)], axis=0)
    return grad_x.reshape(nb, seq, D), out_state, in_state, payload


def _finish_small(gathered, nb, c_ctx, ret_log2_decay, w_ada16, dev):
    n_dev = gathered.shape[0]
    loc = 3 * D // n_dev
    dmod_all = gathered[:, :3 * nb].reshape(n_dev * nb, 3 * D)
    dmodc_parts = jnp.concatenate([gathered[:, 3 * nb:3 * nb + 2].reshape(n_dev, 2 * D), jnp.zeros((n_dev, D), F32)], axis=1)
    c_all = gathered[:, 3 * nb + 2:4 * nb + 2].reshape(n_dev * nb, D)
    nw_parts = gathered[:, 4 * nb + 2]
    misc_parts = gathered[:, 4 * nb + 3]
    n_rows = n_dev * nb + n_dev
    pad = (-n_rows) % 16
    c_rows = jnp.concatenate([c_all, jnp.broadcast_to(c_ctx.reshape(1, D), (n_dev, D)), jnp.zeros((pad, D), F32)], axis=0)
    dm_rows = jnp.concatenate([dmod_all, dmodc_parts, jnp.zeros((pad, 3 * D), F32)], axis=0)
    dm_loc_rows = lax.dynamic_slice_in_dim(dm_rows, dev * loc, loc, axis=1)
    r_pad = jnp.full((1, D), -1.0, F32).at[:, 2 * HD:2 * HD + 2 * RH].set(ret_log2_decay.reshape(1, 2 * RH))
    gb, gc, gnw, misc, gwa = _small_final(dmod_all, dmodc_parts, c_rows, dm_loc_rows, nw_parts, misc_parts,
                                          c_ctx.reshape(1, D), r_pad, w_ada16)
    return (gb, gc, gnw, misc[:, :HD], misc[:, HD:2 * HD], misc[:, 2 * HD:2 * HD + 2 * RH], gwa,
            misc[0, 2 * HD + 2 * RH])


def kernel(x, c, ctx, c_ctx, norm_w, w_ada, b_ada, w_in, ret_log2_decay, q_norm_w, k_norm_w, w_o_ret, w_o_att, w_out, loss_target, m_c_ctx, m_norm_w, m_w_ada, m_b_ada, m_w_in, m_ret_log2_decay, m_q_norm_w, m_k_norm_w, m_w_o_ret, m_w_o_att, m_w_out, v_c_ctx, v_norm_w, v_w_ada, v_b_ada, v_w_in, v_ret_log2_decay, v_q_norm_w, v_k_norm_w, v_w_o_ret, v_w_o_att, v_w_out):
    nb = x.shape[0]
    mx, my, mc = _mesh_pos()
    dev = 4 * mx + 2 * my + mc
    core = jnp.reshape(mc, (1,)).astype(jnp.int32)
    chip = jnp.reshape(2 * mx + my, (1,)).astype(jnp.int32)

    n_loc = 3 * D // N_DEV
    c8 = jnp.zeros((8, D), F32).at[:nb].set(c).at[nb].set(c_ctx)
    c_land = lax.dynamic_update_slice(lax.empty((N_DEV, 8, D), F32), c8[None], (dev, 0, 0))
    c_state, c_token = _exchange_start([c8[None]], [c_land], _bcast_routes(1), "gather_c_start")
    w_in_t = jnp.transpose(w_in[0])
    in_shard = w_in_t.astype(BF16)
    groups = lax.dynamic_update_slice(lax.empty((N_DEV,) + in_shard.shape, BF16), in_shard[None], (mc, 0, 0))
    groups = _pair_fill(groups, (0,), "gather_in_pair", after=(c_token,))
    _, (c_all,) = _exchange_wait(c_state, groups, "gather_c_wait")
    ada_shard = w_ada[0].astype(BF16)
    b_loc = lax.dynamic_slice(b_ada, (0, dev * n_loc), (1, n_loc))
    mod_cols = _mod_part(c_all.reshape(N_DEV * 8, D), ada_shard, b_loc)
    (mod_all,) = _all_gather([mod_cols], "gather_mod")
    mod = jnp.transpose(lax.dynamic_slice(mod_all, (0, dev * 8, 0), (N_DEV, 8, n_loc)), (1, 0, 2)).reshape(8, 3 * D)
    ada_land = lax.dynamic_update_slice(lax.empty((N_DEV,) + ada_shard.shape, BF16), ada_shard[None], (dev, 0, 0))

    (near_send, near_recv, near_bufs, near_routes), gin_token = _exchange_start(
        [in_shard[None]], [groups], _group_routes((1, 2)), "gather_in_start", after=(mod_all,))
    w_in_groups, wo_states, ada_states = [], [], []
    wo_shards = [w_[0].astype(BF16) for w_ in (w_o_ret, w_o_att, w_out)]
    wo_lands = [lax.dynamic_update_slice(lax.empty((N_DEV,) + s_.shape, BF16), s_[None], (dev, 0, 0)) for s_ in wo_shards]

    def _state(send, recv, src, groups, routes):
        return send, recv, [src, groups], routes

    def proj_in(h_all):
        src, groups = near_bufs
        px = _in_proj_group(h_all, groups, 0, 1, chip, None, (gin_token,), "in_proj_0")
        (src,), (groups,) = _exchange_wait(_state(near_send, near_recv, src, groups, near_routes), px,
                                           "gather_in_wait_near")
        groups = _pair_fill(groups, (1, 2), "gather_in_fill_near")
        (far_send, far_recv, (src, groups), far_routes), far_token = _exchange_start(
            [src], [groups], _group_routes((3,)), "gather_in_start_far")
        wo_state, wo_token = _exchange_start([s_[None] for s_ in wo_shards], wo_lands, _bcast_routes(3),
                                             "gather_wo_start", after=(far_token,))
        wo_states.append(wo_state)
        ada_state, ada_token = _exchange_start([ada_shard[None]], [ada_land], _bcast_routes(1), "gather_ada_start",
                                               after=(wo_token,))
        ada_states.append(ada_state)
        px = _in_proj_group(h_all, groups, 1, 2, chip, px, (ada_token,), "in_proj_near")
        (src,), (groups,) = _exchange_wait(_state(far_send, far_recv, src, groups, far_routes), px,
                                           "gather_in_wait_far")
        groups = _pair_fill(groups, (3,), "gather_in_fill_far")
        px = _in_proj_group(h_all, groups, 3, 1, chip, px, (), "in_proj_far")
        w_in_groups.append(groups)
        return px

    def proj_back(dp_all, after):
        return _d_h_groups(dp_all, w_in_groups[0], chip, after)

    def get_w_o(after):
        _, (l_ret, l_att, l_out) = _exchange_wait(wo_states[0], after, "gather_wo_wait")
        return l_ret.reshape(RH * DV, D), l_att.reshape(D, D), l_out.reshape(D, D)

    def on_out_grads(grads):
        parts = [g_.reshape(N_DEV, g_.shape[0] // N_DEV, D) for g_ in grads]
        state, token = _reduce_scatter_start(parts, core, "rs_out")
        return state, (token,)

    def on_in_grad(grad):
        state, token = _reduce_scatter_start([grad.reshape(N_DEV, IN_COLS // N_DEV, D)], core, "rs_in")
        return state, (token,)

    grad_x, out_state, in_state, payload = _local_step(
        x, c, ctx, norm_w, ret_log2_decay, q_norm_w, k_norm_w, loss_target,
        mod, proj_in, proj_back, get_w_o, on_out_grads, on_in_grad, started=(gin_token,))

    pay_land = lax.dynamic_update_slice(lax.empty((N_DEV,) + payload.shape, F32), payload[None], (dev, 0, 0))
    pay_state, pay_token = _exchange_start([payload[None]], [pay_land], _bcast_routes(1), "gather_small_start")

    out_res = _reduce_scatter_finish(out_state, pay_token, chip,
                                     [(w_[0], m_[0], v_[0]) for w_, m_, v_ in ((w_o_ret, m_w_o_ret, v_w_o_ret),
                                                                                (w_o_att, m_w_o_att, v_w_o_att),
                                                                                (w_out, m_w_out, v_w_out))], "rs_out")
    (in_res,) = _reduce_scatter_finish(in_state, out_res[0][0], chip,
                                       [(w_in_t, jnp.transpose(m_w_in[0]), jnp.transpose(v_w_in[0]))], "rs_in")

    _, (gathered,) = _exchange_wait(pay_state, in_res[0], "gather_small_wait")
    _, (l_ada,) = _exchange_wait(ada_states[0], gathered, "gather_ada_wait")
    w_ada16 = jnp.transpose(l_ada, (1, 0, 2)).reshape(D, 3 * D)
    gb, gc, gnw, gq, gk, gr, gwa, loss = _finish_small(gathered, nb, c_ctx, ret_log2_decay, w_ada16, dev)
    big = {4: [jnp.transpose(r)[None] for r in in_res]}
    for i, res in zip((8, 9, 10), out_res):
        big[i] = [r[None] for r in res]
    small_g = {0: gc.reshape(c_ctx.shape), 1: gnw, 2: gwa[None], 3: gb, 5: gr.reshape(ret_log2_decay.shape), 6: gq, 7: gk}
    weights = [c_ctx, norm_w, w_ada, b_ada, w_in, ret_log2_decay, q_norm_w, k_norm_w, w_o_ret, w_o_att, w_out]
    ms = [m_c_ctx, m_norm_w, m_w_ada, m_b_ada, m_w_in, m_ret_log2_decay, m_q_norm_w, m_k_norm_w, m_w_o_ret, m_w_o_att, m_w_out]
    vs = [v_c_ctx, v_norm_w, v_w_ada, v_b_ada, v_w_in, v_ret_log2_decay, v_q_norm_w, v_k_norm_w, v_w_o_ret, v_w_o_att, v_w_out]
    grads, deltas, new_ms, new_vs = [], [], [], []
    for i, (w, m, v) in enumerate(zip(weights, ms, vs)):
        if i in big:
            res = big[i]
        else:
            shape2 = (-1, w.shape[-1])
            g = small_g[i]
            res = [g] + [r.reshape(w.shape) for r in _adamw(w.reshape(shape2), g.reshape(shape2), m.reshape(shape2),
                                                             v.reshape(shape2), "adamw_%d" % i)]
        for lst, r in zip((grads, deltas, new_ms, new_vs), res):
            lst.append(r)
    return (loss, grad_x, *grads, *deltas, *new_ms, *new_vs)
```

```python
import numpy as np
import jax
import jax.numpy as jnp
from jax import lax
from jax.experimental import pallas as pl
from jax.experimental.pallas import tpu as pltpu

F32 = jnp.float32
BF16 = jnp.bfloat16

D = 1024
RH, DK, DV, CH = 4, 256, 512, 256
HQ, HKV, HD = 8, 2, 128
GRID_W = 64
ROPE_THETA = 10000.0
EPS = 1e-6
RK, RV, AK, AV, RQ, RG, AQ, AG, MR, MA = 0, 1024, 3072, 3328, 3584, 4608, 6656, 7680, 8704, 9728
IN_COLS = 10752
KV_COLS = 3584
N_DEV = 8
LR, B1, B2, ADAM_EPS, WD, STEP = 0.001, 0.9, 0.999, 1e-08, 0.01, 10
PAY_ROWS = 16
VMEM_LIMIT = 56 * 1024 * 1024
MESH_T = pl.DeviceIdType.MESH

NT = (((1,), (1,)), ((), ()))
TN = (((0,), (0,)), ((), ()))
SM_C = (HD ** -0.5) * float(np.log2(np.e))


def _params(sem):
    return pltpu.CompilerParams(dimension_semantics=sem, vmem_limit_bytes=VMEM_LIMIT)


def _pick(n, target, mult=8):
    best = None
    for t in range(mult, min(n, target) + 1, mult):
        if n % t == 0:
            best = t
    return best or n


def _dot(a, b, dn=None):
    if dn is None:
        return jnp.dot(a, b, preferred_element_type=F32)
    return lax.dot_general(a, b, dn, preferred_element_type=F32)


def _sig(v):
    return jax.nn.sigmoid(v)


def _silu(v):
    return v * _sig(v)


def _dsilu(v):
    s = _sig(v)
    return s * (1.0 + v * (1.0 - s))


def _sds(shape, dtype):
    return jax.ShapeDtypeStruct(shape, dtype)


def _matmul(a, b, *, ta=False, tb=False, tm, tn, tk, out_dtype, name, after=()):
    m = a.shape[1] if ta else a.shape[0]
    kdim = a.shape[0] if ta else a.shape[1]
    n = b.shape[0] if tb else b.shape[1]
    tm, tn, tk = _pick(m, tm, 128), _pick(n, tn, 128), _pick(kdim, tk, 128)
    nk = kdim // tk
    dn = (((0 if ta else 1,), (1 if tb else 0,)), ((), ()))

    def body(a_ref, b_ref, *rest):
        o_ref, acc_ref = rest[-2:]
        k = pl.program_id(2)
        part = _dot(a_ref[...].astype(BF16), b_ref[...].astype(BF16), dn)
        if nk == 1:
            o_ref[...] = part.astype(o_ref.dtype)
        else:
            @pl.when(k == 0)
            def _():
                acc_ref[...] = part

            @pl.when(k > 0)
            def _():
                acc_ref[...] += part

            @pl.when(k == nk - 1)
            def _():
                o_ref[...] = acc_ref[...].astype(o_ref.dtype)

    a_spec = pl.BlockSpec((tk, tm), lambda i, j, k: (k, i)) if ta else pl.BlockSpec((tm, tk), lambda i, j, k: (i, k))
    b_spec = pl.BlockSpec((tn, tk), lambda i, j, k: (j, k)) if tb else pl.BlockSpec((tk, tn), lambda i, j, k: (k, j))
    return pl.pallas_call(
        body, name=name, grid=(m // tm, n // tn, nk),
        in_specs=[a_spec, b_spec] + [pl.BlockSpec(memory_space=pl.ANY)] * len(after),
        out_specs=pl.BlockSpec((tm, tn), lambda i, j, k: (i, j)), out_shape=_sds((m, n), out_dtype),
        scratch_shapes=[pltpu.VMEM((tm, tn) if nk > 1 else (8, 128), F32)],
        compiler_params=_params(("parallel", "parallel", "arbitrary")),
    )(a, b, *after)


def _log_gamma(r):
    rp = jnp.full((8, 128), -1.0, F32).at[:2, :RH].set(r.reshape(2, RH))

    def body(r_ref, o_ref):
        o_ref[...] = jnp.log1p(-jnp.exp2(r_ref[...]))

    out = pl.pallas_call(body, name="log_gamma", out_shape=_sds((8, 128), F32))(rp)
    return out[:2, :RH]


def _mod_part(c_rows, w_ada_loc16, b_loc):
    def body(c_ref, w_ref, b_ref, o_ref):
        o_ref[...] = _dot(_silu(c_ref[...]).astype(BF16), w_ref[...]) + b_ref[...]

    return pl.pallas_call(
        body, name="mod_part", out_shape=_sds((c_rows.shape[0], w_ada_loc16.shape[1]), F32),
    )(c_rows, w_ada_loc16, b_loc)


def _norm_fwd(x2, mod3, norm_w, rows_all, row_off, rows_per_group, group0, h_prev, tm, name, after=()):
    rows = x2.shape[0]
    rb0 = row_off // tm
    bpg = rows_per_group // tm

    def body(*refs):
        x_ref, sh_ref, sc_ref, nw_ref, o_ref = refs[-5:]
        xv = x_ref[...]
        r = lax.rsqrt(jnp.mean(xv * xv, axis=-1, keepdims=True) + EPS)
        o_ref[...] = ((xv * r) * nw_ref[...] * (1.0 + sc_ref[...]) + sh_ref[...]).astype(BF16)

    in_specs = [pl.BlockSpec((tm, D), lambda i: (i, 0)),
                pl.BlockSpec((None, 1, D), lambda i: (group0 + i // bpg, 0, 0)),
                pl.BlockSpec((None, 1, D), lambda i: (group0 + i // bpg, 0, 1)),
                pl.BlockSpec((1, D), lambda i: (0, 0))]
    in_specs = [pl.BlockSpec(memory_space=pl.ANY)] * len(after) + in_specs
    args = list(after) + [x2, mod3, mod3, norm_w]
    alias = {}
    if h_prev is not None:
        in_specs.insert(0, pl.BlockSpec(memory_space=pl.ANY))
        args.insert(0, h_prev)
        alias = {0: 0}
    return pl.pallas_call(
        body, name=name, grid=(rows // tm,), in_specs=in_specs,
        out_specs=pl.BlockSpec((tm, D), lambda i: (rb0 + i, 0)), out_shape=_sds((rows_all, D), BF16),
        input_output_aliases=alias, compiler_params=_params(("parallel",)),
    )(*args)


def _decays(lg, fwd):
    ii = lax.broadcasted_iota(jnp.int32, (CH, CH), 0)
    jj = lax.broadcasted_iota(jnp.int32, (CH, CH), 1)
    ri = lax.broadcasted_iota(jnp.int32, (CH, 1), 0).astype(F32)
    rel = (ii - jj) if fwd else (jj - ii)
    relf = jnp.maximum(rel, 0).astype(F32)
    mask = jnp.where(rel >= 0, jnp.exp(lg * relf), 0.0)
    qe = (ri + 1.0) if fwd else (CH - ri)
    ke = (CH - 1.0 - ri) if fwd else ri
    return mask, relf, jnp.exp(lg * qe), qe, jnp.exp(lg * ke), ke


def _wide_specs(rowf):
    return [pl.BlockSpec((CH, 2 * DK), lambda b, c: (rowf(b, c), RQ // (2 * DK))),
            pl.BlockSpec((CH, 2 * DK), lambda b, c: (rowf(b, c), RQ // (2 * DK) + 1)),
            pl.BlockSpec((CH, RH * DK), lambda b, c: (rowf(b, c), RK // (RH * DK))),
            pl.BlockSpec((CH, 2 * DV), lambda b, c: (rowf(b, c), RV // (2 * DV))),
            pl.BlockSpec((CH, 2 * DV), lambda b, c: (rowf(b, c), RV // (2 * DV) + 1))]


def _head_qkv(refs, h):
    q0, q1, k, v0, v1 = refs
    lo = h % 2
    q = (q0, q1)[h // 2][:, lo * DK:(lo + 1) * DK].astype(F32)
    kk = k[:, h * DK:(h + 1) * DK].astype(F32) * (DK ** -0.5)
    v16 = (v0, v1)[h // 2][:, lo * DV:(lo + 1) * DV].astype(BF16)
    return q, kk, v16


def _ctx_state(px, lg, nb, t_rows, cx):
    rb = t_rows // cx

    def body(lg_ref, k_ref, v_ref, sf_ref, sb_ref):
        h = pl.program_id(1)
        pos = lax.broadcasted_iota(jnp.int32, (cx, 1), 0).astype(F32)
        k = k_ref[...].astype(F32) * (DK ** -0.5)
        v16 = v_ref[...].astype(BF16)
        wf = jnp.exp(lg_ref[0, h] * (cx - 1.0 - pos))
        wb = jnp.exp(lg_ref[1, h] * pos)
        sf_ref[...] = _dot((k * wf).astype(BF16), v16, TN)
        sb_ref[...] = _dot((k * wb).astype(BF16), v16, TN)

    st = pl.BlockSpec((None, None, DK, DV), lambda b, h: (b, h, 0, 0))
    return pl.pallas_call(
        body, name="ctx_state", grid=(nb, RH),
        in_specs=[pl.BlockSpec(memory_space=pltpu.SMEM),
                  pl.BlockSpec((cx, DK), lambda b, h: (rb + b, RK // DK + h)),
                  pl.BlockSpec((cx, DV), lambda b, h: (rb + b, RV // DV + h))],
        out_specs=[st, st], out_shape=[_sds((nb, RH, DK, DV), F32)] * 2,
        compiler_params=_params(("parallel", "parallel")),
    )(lg, px, px)


def _ret_fwd(px, lg, s0f, s0b, nb, nc):
    t_rows = nb * nc * CH

    def body(lg_ref, *refs):
        ins = (refs[0:5], refs[5:10])
        s0f_ref, s0b_ref, of_ref, ob_ref, hf_ref, hb_ref, sf, sb = refs[10:]
        c = pl.program_id(1)

        @pl.when(c == 0)
        def _():
            sf[...] = s0f_ref[...]
            sb[...] = s0b_ref[...]

        for d, (o_ref, h_ref, s) in enumerate(((of_ref, hf_ref, sf), (ob_ref, hb_ref, sb))):
            for h in range(RH):
                lg_d = lg_ref[d, h]
                mask, _, qd, _, kd, _ = _decays(lg_d, d == 0)
                q, k, v16 = _head_qkv(ins[d], h)
                a = _dot(q.astype(BF16), k.astype(BF16), NT)
                st = s[h]
                st16 = st.astype(BF16)
                h_ref[h] = st16
                o = _dot((a * mask).astype(BF16), v16) + _dot((q * qd).astype(BF16), st16)
                o_ref[:, h * DV:(h + 1) * DV] = o.astype(BF16)
                s[h] = st * jnp.exp(lg_d * CH) + _dot((k * kd).astype(BF16), v16, TN)

    def fw(b, c):
        return b * nc + c

    def bw(b, c):
        return b * nc + nc - 1 - c

    st = pl.BlockSpec((None, RH, DK, DV), lambda b, c: (b, 0, 0, 0))
    in_specs = [pl.BlockSpec(memory_space=pltpu.SMEM)] + _wide_specs(fw) + _wide_specs(bw) + [st, st]
    out_specs = [pl.BlockSpec((CH, RH * DV), lambda b, c: (fw(b, c), 0)),
                 pl.BlockSpec((CH, RH * DV), lambda b, c: (bw(b, c), 0)),
                 pl.BlockSpec((None, None, RH, DK, DV), lambda b, c: (b, c, 0, 0, 0)),
                 pl.BlockSpec((None, None, RH, DK, DV), lambda b, c: (b, nc - 1 - c, 0, 0, 0))]
    return pl.pallas_call(
        body, name="ret_fwd", grid=(nb, nc), in_specs=in_specs, out_specs=out_specs,
        out_shape=[_sds((t_rows, RH * DV), BF16)] * 2 + [_sds((nb, nc, RH, DK, DV), BF16)] * 2,
        scratch_shapes=[pltpu.VMEM((RH, DK, DV), F32), pltpu.VMEM((RH, DK, DV), F32)],
        compiler_params=_params(("parallel", "arbitrary")),
    )(lg, *([px] * 10), s0f, s0b)


def _rope_tables(seq):
    rows = seq // GRID_W
    row = np.repeat(np.arange(rows, dtype=np.float32), GRID_W)
    col = np.tile(np.arange(GRID_W, dtype=np.float32), rows)
    half = HD // 2
    freqs = (ROPE_THETA ** (-np.arange(0, half, 2, dtype=np.float32) / half)).astype(np.float32)
    ang = np.concatenate([row[:, None] * freqs, col[:, None] * freqs], axis=-1).astype(np.float32)
    cos = np.repeat(np.cos(ang), 2, axis=-1).astype(np.float32)
    sin = np.repeat(np.sin(ang), 2, axis=-1).astype(np.float32)
    sign = np.tile(np.array([-1.0, 1.0], np.float32), HD // 2)
    return jnp.asarray(cos), jnp.asarray(sin * sign)


def _swap_pairs(v):
    lane = lax.broadcasted_iota(jnp.int32, v.shape, 1)
    return jnp.where((lane & 1) == 0, pltpu.roll(v, HD - 1, 1), pltpu.roll(v, 1, 1))


def _qk_prep(px, nw, cos, sin, rows, row_off, col_off, heads, hb, seq, tm, name):
    rope = cos is not None
    rb0 = row_off // tm
    pb = seq // tm if rope else 1
    bw = hb * HD

    def body(*refs):
        if rope:
            x_ref, w_ref, c_ref, s_ref, o_ref = refs
        else:
            x_ref, w_ref, o_ref = refs
        for h in range(hb):
            sl = slice(h * HD, (h + 1) * HD)
            xv = x_ref[:, sl].astype(F32)
            r = lax.rsqrt(jnp.mean(xv * xv, axis=-1, keepdims=True) + EPS)
            t = (xv * r) * w_ref[...]
            if rope:
                t = t * c_ref[...] + _swap_pairs(t) * s_ref[...]
            o_ref[:, sl] = t.astype(BF16)

    in_specs = [pl.BlockSpec((tm, bw), lambda i, j: (rb0 + i, col_off // bw + j)),
                pl.BlockSpec((1, HD), lambda i, j: (0, 0))]
    args = [px, nw]
    if rope:
        in_specs += [pl.BlockSpec((tm, HD), lambda i, j: (i % pb, 0))] * 2
        args += [cos, sin]
    return pl.pallas_call(
        body, name=name, grid=(rows // tm, heads // hb), in_specs=in_specs,
        out_specs=pl.BlockSpec((tm, bw), lambda i, j: (i, j)), out_shape=_sds((rows, heads * HD), BF16),
        compiler_params=_params(("parallel", "parallel")),
    )(*args)


def _att_fwd(q16, kx16, kc16, px, nb, seq, cx, tq):
    t_rows = nb * seq
    nq = seq // tq
    rep = HQ // HKV
    gw = rep * HD

    def body(q_ref, kx_ref, kc_ref, vx_ref, vc_ref, g_ref, o_ref, y_ref, l_ref):
        kx = kx_ref[...]
        kc = kc_ref[...]
        vx = vx_ref[...].astype(BF16)
        vc = vc_ref[...].astype(BF16)
        l_ref[...] = jnp.zeros_like(l_ref)
        for r in range(rep):
            sl = slice(r * HD, (r + 1) * HD)
            q = q_ref[:, sl]
            s1 = _dot(q, kx, NT)
            s2 = _dot(q, kc, NT)
            m = jnp.maximum(jnp.max(s1, axis=-1, keepdims=True), jnp.max(s2, axis=-1, keepdims=True))
            e1 = jnp.exp2((s1 - m) * SM_C)
            e2 = jnp.exp2((s2 - m) * SM_C)
            tot = jnp.sum(e1, axis=-1, keepdims=True) + jnp.sum(e2, axis=-1, keepdims=True)
            o = (_dot(e1.astype(BF16), vx) + _dot(e2.astype(BF16), vc)) * (1.0 / tot)
            o_ref[:, sl] = o
            y_ref[:, sl] = (o * _silu(g_ref[:, sl].astype(F32))).astype(BF16)
            l_ref[:, r:r + 1] = m * SM_C + jnp.log(tot) * float(np.log2(np.e))

    qblk = pl.BlockSpec((tq, gw), lambda b, g, i: (b * nq + i, g))
    return pl.pallas_call(
        body, name="att_fwd", grid=(nb, HKV, nq),
        in_specs=[qblk,
                  pl.BlockSpec((seq, HD), lambda b, g, i: (b, g)),
                  pl.BlockSpec((cx, HD), lambda b, g, i: (b, g)),
                  pl.BlockSpec((seq, HD), lambda b, g, i: (b, AV // HD + g)),
                  pl.BlockSpec((cx, HD), lambda b, g, i: (t_rows // cx + b, AV // HD + g)),
                  pl.BlockSpec((tq, gw), lambda b, g, i: (b * nq + i, AG // gw + g))],
        out_specs=[qblk, qblk, pl.BlockSpec((tq, 128), lambda b, g, i: (b * nq + i, g))],
        out_shape=[_sds((t_rows, D), F32), _sds((t_rows, D), BF16), _sds((t_rows, HKV * 128), F32)],
        compiler_params=_params(("parallel", "parallel", "parallel")),
    )(q16, kx16, kc16, px, px, px)


def _gate_specs(tm, col0):
    hw = D // 2
    return [pl.BlockSpec((tm, hw), lambda i: (i, col0 // hw)), pl.BlockSpec((tm, hw), lambda i: (i, col0 // hw + 1))]


def _merge_out(o_f, o_b, yatt16, px, w_o_ret16, w_o_att16, w_out16, x2, tgt, mod3, nb, seq, tm):
    t_rows = nb * seq
    bpb = seq // tm
    hw = D // 2

    def body(of_ref, ob_ref, g0, g1, g2, g3, wr_ref, ya_ref, wa_ref, mr0, mr1, ma0, ma1, wo_ref, x_ref, t_ref, gt_ref,
             yr_ref, ar_ref, aa_ref, y_ref, dxn_ref, dout_ref, dg_ref, loss_ref):
        i = pl.program_id(1)
        for h, g_ref in enumerate((g0, g1, g2, g3)):
            sl = slice(h * DV, (h + 1) * DV)
            o = of_ref[:, sl].astype(F32) + ob_ref[:, sl].astype(F32)
            r = lax.rsqrt(jnp.mean(o * o, axis=-1, keepdims=True) + EPS)
            yr_ref[:, sl] = ((o * r) * _silu(g_ref[...].astype(F32))).astype(BF16)
        ar = _dot(yr_ref[...], wr_ref[...])
        aa = _dot(ya_ref[...], wa_ref[...])
        ar_ref[...] = ar.astype(BF16)
        aa_ref[...] = aa.astype(BF16)
        for j, (mr_ref, ma_ref) in enumerate(((mr0, ma0), (mr1, ma1))):
            sl = slice(j * hw, (j + 1) * hw)
            y_ref[:, sl] = (_sig(mr_ref[...].astype(F32)) * ar[:, sl]
                            + _sig(ma_ref[...].astype(F32)) * aa[:, sl]).astype(BF16)
        out = _dot(y_ref[...], wo_ref[...])
        gate = gt_ref[...]
        diff = x_ref[...] + gate * out - t_ref[...]
        dxn = diff * (1.0 / D)
        dxn_ref[...] = dxn
        dout_ref[...] = (gate * dxn).astype(BF16)
        dg = jnp.sum(dxn * out, axis=0, keepdims=True)
        ls = jnp.broadcast_to(jnp.sum(diff * diff) * (0.5 / D), (1, 128))

        @pl.when(i == 0)
        def _():
            dg_ref[...] = dg
            loss_ref[...] = ls

        @pl.when(i > 0)
        def _():
            dg_ref[...] += dg
            loss_ref[...] += ls

    def cols(width, col0):
        return pl.BlockSpec((tm, width), lambda b, i: (b * bpb + i, col0 // width))

    def whole(rows):
        return pl.BlockSpec((rows, D), lambda b, i: (0, 0))

    row, wide = cols(D, 0), cols(RH * DV, 0)
    gates = [cols(DV, RG + h * DV) for h in range(RH)]
    merge_gates = [cols(hw, MR), cols(hw, MR + hw), cols(hw, MA), cols(hw, MA + hw)]
    return pl.pallas_call(
        body, name="merge_out", grid=(nb, bpb),
        in_specs=[wide, wide] + gates + [whole(RH * DV), row, whole(D)] + merge_gates
        + [whole(D), row, row, pl.BlockSpec((None, 1, D), lambda b, i: (b, 0, 2))],
        out_specs=[wide, row, row, row, row, row, pl.BlockSpec((None, 1, D), lambda b, i: (b, 0, 0)),
                   pl.BlockSpec((None, 1, 128), lambda b, i: (b, 0, 0))],
        out_shape=[_sds((t_rows, RH * DV), BF16)] + [_sds((t_rows, D), BF16)] * 3
        + [_sds((t_rows, D), F32), _sds((t_rows, D), BF16), _sds((nb, 1, D), F32), _sds((nb, 1, 128), F32)],
        compiler_params=_params(("parallel", "arbitrary")),
    )(o_f, o_b, *([px] * RH), w_o_ret16, yatt16, w_o_att16, px, px, px, px, w_out16, x2, tgt, mod3)


def _bwd_branches(dout16, w_out16, w_o_ret16, w_o_att16, px, a_ret, a_att, o_f, o_b, o_att, tm):
    t_rows = dout16.shape[0]
    hw = D // 2

    def body(do_ref, wo_ref, wr_ref, wa_ref, mr0, mr1, ma0, ma1, ar_ref, aa_ref, rg0, rg1, rg2, rg3, of_ref, ob_ref,
             ag0, ag1, oa_ref, dar_ref, daa_ref, dmr_ref, dma_ref, dor_ref, drg_ref, dao_ref, dag_ref):
        dy_all = _dot(do_ref[...], wo_ref[...], NT)
        for j, (mr_ref, ma_ref) in enumerate(((mr0, ma0), (mr1, ma1))):
            sl = slice(j * hw, (j + 1) * hw)
            dy = dy_all[:, sl]
            sr = _sig(mr_ref[...].astype(F32))
            sa = _sig(ma_ref[...].astype(F32))
            dar_ref[:, sl] = (dy * sr).astype(BF16)
            daa_ref[:, sl] = (dy * sa).astype(BF16)
            dmr_ref[:, sl] = (dy * ar_ref[:, sl].astype(F32) * sr * (1.0 - sr)).astype(BF16)
            dma_ref[:, sl] = (dy * aa_ref[:, sl].astype(F32) * sa * (1.0 - sa)).astype(BF16)
        da_ret = dar_ref[...]
        for h, g_ref in enumerate((rg0, rg1, rg2, rg3)):
            sl = slice(h * DV, (h + 1) * DV)
            dy = _dot(da_ret, wr_ref[sl, :], NT)
            g = g_ref[...].astype(F32)
            o = of_ref[:, sl].astype(F32) + ob_ref[:, sl].astype(F32)
            r = lax.rsqrt(jnp.mean(o * o, axis=-1, keepdims=True) + EPS)
            on = o * r
            sg = _sig(g)
            don = dy * (g * sg)
            drg_ref[:, sl] = (dy * on * (sg * (1.0 + g * (1.0 - sg)))).astype(BF16)
            dor_ref[:, sl] = (r * (don - on * jnp.mean(on * don, axis=-1, keepdims=True))).astype(BF16)
        dy_all = _dot(daa_ref[...], wa_ref[...], NT)
        for j, g_ref in enumerate((ag0, ag1)):
            sl = slice(j * hw, (j + 1) * hw)
            dy = dy_all[:, sl]
            g = g_ref[...].astype(F32)
            sg = _sig(g)
            dao_ref[:, sl] = dy * (g * sg)
            dag_ref[:, sl] = (dy * oa_ref[:, sl] * (sg * (1.0 + g * (1.0 - sg)))).astype(BF16)

    def gate(h):
        return pl.BlockSpec((tm, DV), lambda i: (i, RG // DV + h))

    def whole(rows):
        return pl.BlockSpec((rows, D), lambda i: (0, 0))

    row = pl.BlockSpec((tm, D), lambda i: (i, 0))
    wide = pl.BlockSpec((tm, RH * DV), lambda i: (i, 0))
    return pl.pallas_call(
        body, name="bwd_branches", grid=(t_rows // tm,),
        in_specs=[row, whole(D), whole(RH * DV), whole(D)] + _gate_specs(tm, MR) + _gate_specs(tm, MA) + [row, row]
        + [gate(h) for h in range(RH)] + [wide, wide] + _gate_specs(tm, AG) + [row],
        out_specs=[row] * 4 + [wide, wide, row, row],
        out_shape=[_sds((t_rows, D), BF16)] * 4 + [_sds((t_rows, RH * DV), BF16)] * 2
        + [_sds((t_rows, D), F32), _sds((t_rows, D), BF16)],
        compiler_params=_params(("parallel",)),
    )(dout16, w_out16, w_o_ret16, w_o_att16, px, px, px, px, a_ret, a_att, *([px] * RH), o_f, o_b, px, px, o_att)


def _att_bwd(q16, kx16, kc16, px, dao, o_att, lse, q_norm_w, cos, sin, nb, seq, cx, tq, after=()):
    t_rows = nb * seq
    nq = seq // tq
    rep = HQ // HKV
    gw = rep * HD
    scale = HD ** -0.5

    def body(q_ref, kx_ref, kc_ref, vx_ref, vc_ref, dao_ref, o_ref, l_ref, xq_ref, w_ref, c_ref, s_ref, *rest):
        daq_ref, gq_ref, dkx_ref, dvx_ref, dkc_ref, dvc_ref = rest[-6:]
        i = pl.program_id(2)
        first = jnp.logical_and(jnp.logical_and(pl.program_id(0) == 0, pl.program_id(1) == 0), i == 0)
        gq = jnp.zeros((1, HD), F32)
        kx = kx_ref[...]
        kc = kc_ref[...]
        vx = vx_ref[...].astype(BF16)
        vc = vc_ref[...].astype(BF16)
        dkx = jnp.zeros((seq, HD), F32)
        dvx = jnp.zeros((seq, HD), F32)
        dkc = jnp.zeros((cx, HD), F32)
        dvc = jnp.zeros((cx, HD), F32)
        for r in range(rep):
            sl = slice(r * HD, (r + 1) * HD)
            q = q_ref[:, sl]
            lr = l_ref[:, r:r + 1]
            p1 = jnp.exp2(_dot(q, kx, NT) * SM_C - lr)
            p2 = jnp.exp2(_dot(q, kc, NT) * SM_C - lr)
            da = dao_ref[:, sl]
            da16 = da.astype(BF16)
            delta = jnp.sum(da * o_ref[:, sl], axis=-1, keepdims=True)
            ds1 = (p1 * (_dot(da16, vx, NT) - delta)).astype(BF16)
            ds2 = (p2 * (_dot(da16, vc, NT) - delta)).astype(BF16)
            dq = (_dot(ds1, kx) + _dot(ds2, kc)) * scale
            dkx += _dot(ds1, q, TN)
            dkc += _dot(ds2, q, TN)
            dvx += _dot(p1.astype(BF16), da16, TN)
            dvc += _dot(p2.astype(BF16), da16, TN)
            dt = dq * c_ref[...] + _swap_pairs(dq * s_ref[...])
            xv = xq_ref[:, sl].astype(F32)
            rn = lax.rsqrt(jnp.mean(xv * xv, axis=-1, keepdims=True) + EPS)
            xh = xv * rn
            dxh = dt * w_ref[...]
            daq_ref[:, sl] = (rn * (dxh - xh * jnp.mean(dxh * xh, axis=-1, keepdims=True))).astype(BF16)
            gq += jnp.sum(dt * xh, axis=0, keepdims=True)
        dkx = dkx * scale
        dkc = dkc * scale

        @pl.when(first)
        def _():
            gq_ref[...] = gq

        @pl.when(jnp.logical_not(first))
        def _():
            gq_ref[...] += gq

        @pl.when(i == 0)
        def _():
            dkx_ref[...] = dkx
            dvx_ref[...] = dvx
            dkc_ref[...] = dkc
            dvc_ref[...] = dvc

        @pl.when(i > 0)
        def _():
            dkx_ref[...] += dkx
            dvx_ref[...] += dvx
            dkc_ref[...] += dkc
            dvc_ref[...] += dvc

    qblk = pl.BlockSpec((tq, gw), lambda b, g, i: (b * nq + i, g))
    kxb = pl.BlockSpec((None, seq, HD), lambda b, g, i: (b, 0, g))
    kcb = pl.BlockSpec((None, cx, HD), lambda b, g, i: (b, 0, g))
    table = pl.BlockSpec((tq, HD), lambda b, g, i: (i, 0))
    one = pl.BlockSpec((1, HD), lambda b, g, i: (0, 0))
    return pl.pallas_call(
        body, name="att_bwd", grid=(nb, HKV, nq),
        in_specs=[qblk,
                  pl.BlockSpec((seq, HD), lambda b, g, i: (b, g)),
                  pl.BlockSpec((cx, HD), lambda b, g, i: (b, g)),
                  pl.BlockSpec((seq, HD), lambda b, g, i: (b, AV // HD + g)),
                  pl.BlockSpec((cx, HD), lambda b, g, i: (t_rows // cx + b, AV // HD + g)),
                  qblk, qblk, pl.BlockSpec((tq, 128), lambda b, g, i: (b * nq + i, g)),
                  pl.BlockSpec((tq, gw), lambda b, g, i: (b * nq + i, AQ // gw + g)), one, table, table]
        + [pl.BlockSpec(memory_space=pl.ANY)] * len(after),
        out_specs=[qblk, one, kxb, kxb, kcb, kcb],
        out_shape=[_sds((t_rows, D), BF16), _sds((1, HD), F32), _sds((nb, seq, HKV * HD), F32),
                   _sds((nb, seq, HKV * HD), F32), _sds((nb, cx, HKV * HD), F32), _sds((nb, cx, HKV * HD), F32)],
        compiler_params=_params(("arbitrary", "arbitrary", "arbitrary")),
    )(q16, kx16, kc16, px, px, dao, o_att, lse, px, q_norm_w, cos, sin, *after)


def _qk_prep_bwd(dt, px, nw, cos, sin, rows, row_off, col_off, heads, hb, seq, tm, name):
    rope = cos is not None
    rb0 = row_off // tm
    pb = seq // tm if rope else 1
    bw = hb * HD

    def body(*refs):
        if rope:
            d_ref, x_ref, w_ref, c_ref, s_ref, dx_ref, dw_ref = refs
        else:
            d_ref, x_ref, w_ref, dx_ref, dw_ref = refs
        first = jnp.logical_and(pl.program_id(0) == 0, pl.program_id(1) == 0)
        dw = jnp.zeros((1, HD), F32)
        for h in range(hb):
            sl = slice(h * HD, (h + 1) * HD)
            dtv = d_ref[:, sl]
            if rope:
                dtv = dtv * c_ref[...] + _swap_pairs(dtv * s_ref[...])
            xv = x_ref[:, sl].astype(F32)
            r = lax.rsqrt(jnp.mean(xv * xv, axis=-1, keepdims=True) + EPS)
            xh = xv * r
            dxh = dtv * w_ref[...]
            dx_ref[:, sl] = (r * (dxh - xh * jnp.mean(dxh * xh, axis=-1, keepdims=True))).astype(BF16)
            dw += jnp.sum(dtv * xh, axis=0, keepdims=True)

        @pl.when(first)
        def _():
            dw_ref[...] = dw

        @pl.when(jnp.logical_not(first))
        def _():
            dw_ref[...] += dw

    blk = pl.BlockSpec((tm, bw), lambda i, j: (i, j))
    in_specs = [blk, pl.BlockSpec((tm, bw), lambda i, j: (rb0 + i, col_off // bw + j)),
                pl.BlockSpec((1, HD), lambda i, j: (0, 0))]
    args = [dt, px, nw]
    if rope:
        in_specs += [pl.BlockSpec((tm, HD), lambda i, j: (i % pb, 0))] * 2
        args += [cos, sin]
    return pl.pallas_call(
        body, name=name, grid=(rows // tm, heads // hb), in_specs=in_specs,
        out_specs=[blk, pl.BlockSpec((1, HD), lambda i, j: (0, 0))],
        out_shape=[_sds((rows, heads * HD), BF16), _sds((1, HD), F32)],
        compiler_params=_params(("arbitrary", "arbitrary")),
    )(*args)


def _ret_bwd(px, lg, do16, hist_f, hist_b, nb, nc):
    t_rows = nb * nc * CH

    def body(lg_ref, *refs):
        ins = (refs[0:5], refs[7:12])
        do_refs = (refs[5], refs[12])
        h_refs = (refs[6], refs[13])
        outs = (refs[14:17], refs[17:20])
        ds_outs = (refs[20], refs[21])
        dlg_ref = refs[22]
        dss = (refs[23], refs[24])
        c = pl.program_id(1)

        @pl.when(c == 0)
        def _():
            dss[0][...] = jnp.zeros_like(dss[0])
            dss[1][...] = jnp.zeros_like(dss[1])
            dlg_ref[...] = jnp.zeros_like(dlg_ref)

        for d in range(2):
            dq_ref, dk_ref, dv_ref = outs[d]
            for h in range(RH):
                lg_d = lg_ref[d, h]
                mask, relf, qd, qe, kd, ke = _decays(lg_d, d == 0)
                g_ch = jnp.exp(lg_d * CH)
                q, k, v16 = _head_qkv(ins[d], h)
                q16 = q.astype(BF16)
                k16 = k.astype(BF16)
                do16v = do_refs[d][:, h * DV:(h + 1) * DV]
                st16 = h_refs[d][h]
                dst = dss[d][h]
                dst16 = dst.astype(BF16)
                a = _dot(q16, k16, NT) * mask
                dp = _dot(do16v, v16, NT)
                da16 = (dp * mask).astype(BF16)
                dq_cross = _dot(do16v, st16, NT) * qd
                dq_ref[:, h * DK:(h + 1) * DK] = (_dot(da16, k16) + dq_cross).astype(BF16)
                dk_state = _dot(v16, dst16, NT) * kd
                dk_ref[:, h * DK:(h + 1) * DK] = ((_dot(da16, q16, TN) + dk_state) * (DK ** -0.5)).astype(BF16)
                dv = _dot(a.astype(BF16), do16v, TN) + _dot((k * kd).astype(BF16), dst16)
                dv_ref[:, h * DV:(h + 1) * DV] = dv.astype(BF16)
                dlg = (jnp.sum(relf * a * dp)
                       + jnp.sum(qe * jnp.sum(q * dq_cross, axis=-1, keepdims=True))
                       + jnp.sum(ke * jnp.sum(k * dk_state, axis=-1, keepdims=True))
                       + CH * g_ch * jnp.sum(dst * st16.astype(F32)))
                row = d * RH + h
                dlg_ref[row:row + 1, :] += jnp.broadcast_to(dlg, (1, 128))
                ds_new = g_ch * dst + _dot((q * qd).astype(BF16), do16v, TN)
                dss[d][h] = ds_new

                @pl.when(c == nc - 1)
                def _():
                    ds_outs[d][h] = ds_new

    def fw(b, c):
        return b * nc + nc - 1 - c

    def bw(b, c):
        return b * nc + c

    def rows(rowf, width):
        return pl.BlockSpec((CH, width), lambda b, c: (rowf(b, c), 0))

    def hist(rowf):
        return pl.BlockSpec((None, None, RH, DK, DV), lambda b, c: (b, rowf(0, c), 0, 0, 0))

    st = pl.BlockSpec((None, RH, DK, DV), lambda b, c: (b, 0, 0, 0))
    in_specs = [pl.BlockSpec(memory_space=pltpu.SMEM)]
    out_specs = []
    for rowf in (fw, bw):
        in_specs += _wide_specs(rowf) + [rows(rowf, RH * DV), hist(rowf)]
        out_specs += [rows(rowf, RH * DK), rows(rowf, RH * DK), rows(rowf, RH * DV)]
    out_specs += [st, st, pl.BlockSpec((None, 8, 128), lambda b, c: (b, 0, 0))]
    qk = _sds((t_rows, RH * DK), BF16)
    vv = _sds((t_rows, RH * DV), BF16)
    return pl.pallas_call(
        body, name="ret_bwd", grid=(nb, nc), in_specs=in_specs, out_specs=out_specs,
        out_shape=[qk, qk, vv, qk, qk, vv, _sds((nb, RH, DK, DV), F32), _sds((nb, RH, DK, DV), F32),
                   _sds((nb, 8, 128), F32)],
        scratch_shapes=[pltpu.VMEM((RH, DK, DV), F32), pltpu.VMEM((RH, DK, DV), F32)],
        compiler_params=_params(("parallel", "arbitrary")),
    )(lg, *([px] * 5), do16, hist_f, *([px] * 5), do16, hist_b)


def _ctx_state_bwd(px, lg, ds_f, ds_b, nb, t_rows, cx):
    rb = t_rows // cx

    def body(lg_ref, k_ref, v_ref, dsf_ref, dsb_ref, dk_ref, dv_ref, dlg_ref):
        h = pl.program_id(1)
        pos = lax.broadcasted_iota(jnp.int32, (cx, 1), 0).astype(F32)
        k = k_ref[...].astype(F32) * (DK ** -0.5)
        v16 = v_ref[...].astype(BF16)
        dk = jnp.zeros((cx, DK), F32)
        dv = jnp.zeros((cx, DV), F32)
        dlg_ref[...] = jnp.zeros_like(dlg_ref)
        for d, (ds_ref, e) in enumerate(((dsf_ref, cx - 1.0 - pos), (dsb_ref, pos))):
            w = jnp.exp(lg_ref[d, h] * e)
            ds16 = ds_ref[...].astype(BF16)
            t = _dot(v16, ds16, NT)
            dk += t * w
            dv += _dot((k * w).astype(BF16), ds16)
            dlg = jnp.sum(e * w * jnp.sum(k * t, axis=-1, keepdims=True))
            dlg_ref[d:d + 1, :] = jnp.broadcast_to(dlg, (1, 128))
        dk_ref[...] = (dk * (DK ** -0.5)).astype(BF16)
        dv_ref[...] = dv.astype(BF16)

    st = pl.BlockSpec((None, None, DK, DV), lambda b, h: (b, h, 0, 0))
    return pl.pallas_call(
        body, name="ctx_state_bwd", grid=(nb, RH),
        in_specs=[pl.BlockSpec(memory_space=pltpu.SMEM),
                  pl.BlockSpec((cx, DK), lambda b, h: (rb + b, RK // DK + h)),
                  pl.BlockSpec((cx, DV), lambda b, h: (rb + b, RV // DV + h)), st, st],
        out_specs=[pl.BlockSpec((cx, DK), lambda b, h: (b, h)), pl.BlockSpec((cx, DV), lambda b, h: (b, h)),
                   pl.BlockSpec((None, None, 8, 128), lambda b, h: (b, h, 0, 0))],
        out_shape=[_sds((nb * cx, RH * DK), BF16), _sds((nb * cx, RH * DV), BF16), _sds((nb, RH, 8, 128), F32)],
        compiler_params=_params(("parallel", "parallel")),
    )(lg, px, px, ds_f, ds_b)


def _assemble_lat(rows_all, dk_f, dk_b, dv_f, dv_b, dak16, dvx, dq_f, dq_b, drg16, daq16, dag16, dmr16, dma16, tm):
    t_rows = dk_f.shape[0]

    def body(dkf, dkb, dvf, dvb, dak, dav, dqf, dqb, drg, daq, dag, dmr, dma, o_ref):
        o_ref[:, RK:RK + RH * DK] = (dkf[...].astype(F32) + dkb[...].astype(F32)).astype(BF16)
        o_ref[:, RV:RV + RH * DV] = (dvf[...].astype(F32) + dvb[...].astype(F32)).astype(BF16)
        o_ref[:, AK:AK + HKV * HD] = dak[...]
        o_ref[:, AV:AV + HKV * HD] = dav[...].astype(BF16)
        o_ref[:, RQ:RQ + RH * DK] = (dqf[...].astype(F32) + dqb[...].astype(F32)).astype(BF16)
        o_ref[:, RG:RG + RH * DV] = drg[...]
        o_ref[:, AQ:AQ + D] = daq[...]
        o_ref[:, AG:AG + D] = dag[...]
        o_ref[:, MR:MR + D] = dmr[...]
        o_ref[:, MA:MA + D] = dma[...]

    args = (dk_f, dk_b, dv_f, dv_b, dak16, dvx, dq_f, dq_b, drg16, daq16, dag16, dmr16, dma16)
    return pl.pallas_call(
        body, name="assemble_lat", grid=(t_rows // tm,),
        in_specs=[pl.BlockSpec((tm, a.shape[1]), lambda i: (i, 0)) for a in args],
        out_specs=pl.BlockSpec((tm, IN_COLS), lambda i: (i, 0)), out_shape=_sds((rows_all, IN_COLS), BF16),
        compiler_params=_params(("parallel",)),
    )(*args)


def _assemble_ctx(dp_all, dck16, dcv16, dcak16, dvc, t_rows, tm):
    c_rows = dck16.shape[0]
    rb = t_rows // tm

    def body(_, dck, dcv, dcak, dcav, o_ref):
        o_ref[:, RK:RK + RH * DK] = dck[...]
        o_ref[:, RV:RV + RH * DV] = dcv[...]
        o_ref[:, AK:AK + HKV * HD] = dcak[...]
        o_ref[:, AV:AV + HKV * HD] = dcav[...].astype(BF16)
        o_ref[:, KV_COLS:] = jnp.zeros((tm, IN_COLS - KV_COLS), BF16)

    args = (dck16, dcv16, dcak16, dvc)
    return pl.pallas_call(
        body, name="assemble_ctx", grid=(c_rows // tm,),
        in_specs=[pl.BlockSpec(memory_space=pl.ANY)]
        + [pl.BlockSpec((tm, a.shape[1]), lambda i: (i, 0)) for a in args],
        out_specs=pl.BlockSpec((tm, IN_COLS), lambda i: (rb + i, 0)), out_shape=_sds(dp_all.shape, BF16),
        input_output_aliases={0: 0},
        compiler_params=_params(("parallel",)),
    )(dp_all, *args)


def _norm_bwd(dh, x2, mod3, norm_w, dxn, row_off, rows_per_group, group0, tm, name):
    with_dx = dxn is not None
    rows = x2.shape[0]
    rb0 = row_off // tm
    bpg = rows_per_group // tm
    ngroups = rows // rows_per_group

    def body(*refs):
        if with_dx:
            dh_ref, x_ref, sc_ref, nw_ref, dxn_ref, dx_ref, dsh_ref, dsc_ref, dnw_ref = refs
        else:
            dh_ref, x_ref, sc_ref, nw_ref, dsh_ref, dsc_ref, dnw_ref = refs
        i = pl.program_id(0)
        dhv = dh_ref[...]
        xv = x_ref[...]
        nw = nw_ref[...]
        r = lax.rsqrt(jnp.mean(xv * xv, axis=-1, keepdims=True) + EPS)
        xh = xv * r
        dm = dhv * (1.0 + sc_ref[...])
        dsh = jnp.sum(dhv, axis=0, keepdims=True)
        dsc = jnp.sum(dhv * (xh * nw), axis=0, keepdims=True)
        dnw = jnp.sum(dm * xh, axis=0, keepdims=True)
        if with_dx:
            dxh = dm * nw
            dx_ref[...] = dxn_ref[...] + r * (dxh - xh * jnp.mean(dxh * xh, axis=-1, keepdims=True))

        @pl.when(i % bpg == 0)
        def _():
            dsh_ref[...] = dsh
            dsc_ref[...] = dsc

        @pl.when(i % bpg != 0)
        def _():
            dsh_ref[...] += dsh
            dsc_ref[...] += dsc

        @pl.when(i == 0)
        def _():
            dnw_ref[...] = dnw

        @pl.when(i > 0)
        def _():
            dnw_ref[...] += dnw

    grp = pl.BlockSpec((None, 1, D), lambda i: (i // bpg, 0, 0))
    in_specs = [pl.BlockSpec((tm, D), lambda i: (rb0 + i, 0)), pl.BlockSpec((tm, D), lambda i: (i, 0)),
                pl.BlockSpec((None, 1, D), lambda i: (group0 + i // bpg, 0, 1)),
                pl.BlockSpec((1, D), lambda i: (0, 0))]
    args = [dh, x2, mod3, norm_w]
    out_specs = [grp, grp, pl.BlockSpec((1, D), lambda i: (0, 0))]
    out_shape = [_sds((ngroups, 1, D), F32), _sds((ngroups, 1, D), F32), _sds((1, D), F32)]
    if with_dx:
        in_specs.append(pl.BlockSpec((tm, D), lambda i: (i, 0)))
        args.append(dxn)
        out_specs.insert(0, pl.BlockSpec((tm, D), lambda i: (i, 0)))
        out_shape.insert(0, _sds((rows, D), F32))
    return pl.pallas_call(
        body, name=name, grid=(rows // tm,), in_specs=in_specs, out_specs=out_specs, out_shape=out_shape,
        compiler_params=_params(("arbitrary",)),
    )(*args)


def _small_final(dmod_all, dmodc_parts, c_rows, dm_loc_rows, nw_parts, misc_parts, c_ctx, r_pad, w_ada16):
    loc = dm_loc_rows.shape[1]

    def body(dm_ref, dmc_ref, c_ref, dml_ref, nwp_ref, mp_ref, cc_ref, r_ref, w_ref,
             gb_ref, gc_ref, gnw_ref, misc_ref, gwa_ref):
        dmc = jnp.sum(dmc_ref[...], axis=0, keepdims=True)
        gb_ref[...] = jnp.sum(dm_ref[...], axis=0, keepdims=True) + dmc
        dsc = _dot(jnp.broadcast_to(dmc, (8, 3 * D)).astype(BF16), w_ref[...], NT)[0:1, :]
        gc_ref[...] = dsc * _dsilu(cc_ref[...])
        gnw_ref[...] = jnp.sum(nwp_ref[...], axis=0, keepdims=True)
        misc = jnp.sum(mp_ref[...], axis=0, keepdims=True)
        y = jnp.exp2(r_ref[...])
        lane = lax.broadcasted_iota(jnp.int32, (1, D), 1)
        is_decay = jnp.logical_and(lane >= 2 * HD, lane < 2 * HD + 2 * RH)
        misc_ref[...] = misc * jnp.where(is_decay, -(y * np.float32(np.log(2.0))) / (1.0 - y), 1.0)
        gwa_ref[...] = _dot(_silu(c_ref[...]).astype(BF16), dml_ref[...].astype(BF16), TN)

    return pl.pallas_call(
        body, name="small_final",
        out_shape=[_sds((1, 3 * D), F32), _sds((1, D), F32), _sds((1, D), F32), _sds((1, D), F32), _sds((D, loc), F32)],
        compiler_params=pltpu.CompilerParams(vmem_limit_bytes=VMEM_LIMIT),
    )(dmod_all, dmodc_parts, c_rows, dm_loc_rows, nw_parts, misc_parts, c_ctx, r_pad, w_ada16)


def _adamw_math(w, g, m, v):
    nm = B1 * m + (1.0 - B1) * g
    nv = B2 * v + (1.0 - B2) * (g * g)
    return -LR * ((nm / (1.0 - B1 ** STEP)) / (jnp.sqrt(nv / (1.0 - B2 ** STEP)) + ADAM_EPS) + WD * w), nm, nv


def _adamw(w, g, m, v, name):
    rows, cols = w.shape
    tm = _pick(rows, 448, 8)

    def body(w_ref, g_ref, m_ref, v_ref, d_ref, nm_ref, nv_ref):
        d_ref[...], nm_ref[...], nv_ref[...] = _adamw_math(w_ref[...], g_ref[...], m_ref[...], v_ref[...])

    blk = pl.BlockSpec((tm, cols), lambda i: (i, 0))
    return pl.pallas_call(
        body, name=name, grid=(rows // tm,), in_specs=[blk] * 4, out_specs=[blk] * 3,
        out_shape=[_sds((rows, cols), F32)] * 3, compiler_params=_params(("parallel",)),
    )(w, g, m, v)


def _mesh_pos():
    return lax.axis_index("x"), lax.axis_index("y"), lax.axis_index("c")


def _all_gather(arrs, name):
    n = len(arrs)

    def body(*refs):
        ins, outs = refs[:n], refs[n:2 * n]
        send_sems, recv_sems, local_sems = refs[2 * n:]
        x, y, c = _mesh_pos()
        me, sib = (x, y, c), (x, y, 1 - c)
        chips = [(1 - x, y), (x, 1 - y), (1 - x, 1 - y)]

        def slot(p):
            return 4 * p[0] + 2 * p[1] + p[2]

        def copy(a, k, block, to, own):
            dst = outs[a].at[slot(block)]
            return pltpu.make_async_remote_copy(
                src_ref=ins[a] if own else dst, dst_ref=dst, send_sem=send_sems.at[a, k], recv_sem=recv_sems.at[a, k],
                device_id=to, device_id_type=MESH_T)

        mine = [pltpu.make_async_copy(ins[a], outs[a].at[slot(me)], local_sems.at[a]) for a in range(n)]
        for cp in mine:
            cp.start()
        first = []
        for a in range(n):
            first.append(copy(a, 0, me, sib, True))
            first += [copy(a, 1 + j, me, (*chip, c), True) for j, chip in enumerate(chips)]
        for cp in first:
            cp.start()
        passed = []
        for j, chip in enumerate(chips):
            for a in range(n):
                copy(a, 1 + j, (*chip, c), me, False).wait_recv()
                fwd = copy(a, 4 + j, (*chip, c), sib, False)
                fwd.start()
                passed.append(fwd)
        for a in range(n):
            copy(a, 0, sib, me, False).wait_recv()
            for j, chip in enumerate(chips):
                copy(a, 4 + j, (*chip, 1 - c), me, False).wait_recv()
        for cp in first + passed:
            cp.wait_send()
        for cp in mine:
            cp.wait()

    hbm = pl.BlockSpec(memory_space=pl.ANY)
    return pl.pallas_call(
        body, name=name, in_specs=[hbm] * n, out_specs=[hbm] * n,
        out_shape=[_sds((N_DEV,) + a.shape, a.dtype) for a in arrs],
        scratch_shapes=[pltpu.SemaphoreType.DMA((n, 7)), pltpu.SemaphoreType.DMA((n, 7)), pltpu.SemaphoreType.DMA((n,))],
    )(*arrs)


def _pair_exchange(arrs, name):
    n = len(arrs)

    def body(*refs):
        ins, outs = refs[:n], refs[n:2 * n]
        send_sems, recv_sems = refs[2 * n:]
        x, y, c = _mesh_pos()
        sib = (x, y, 1 - c)
        sends = []
        for a in range(n):
            for k in range(4):
                sends.append(pltpu.make_async_remote_copy(
                    src_ref=ins[a].at[2 * k + 1 - c], dst_ref=outs[a].at[k], send_sem=send_sems.at[a, k],
                    recv_sem=recv_sems.at[a, k], device_id=sib, device_id_type=MESH_T))
        for cp in sends:
            cp.start()
        for cp in sends:
            cp.wait_recv()
        for cp in sends:
            cp.wait_send()

    hbm = pl.BlockSpec(memory_space=pl.ANY)
    return pl.pallas_call(
        body, name=name, in_specs=[hbm] * n, out_specs=[hbm] * n,
        out_shape=[_sds((4,) + a.shape[1:], a.dtype) for a in arrs],
        scratch_shapes=[pltpu.SemaphoreType.DMA((n, 4)), pltpu.SemaphoreType.DMA((n, 4))],
    )(*arrs)


def _pair_add(part, got, core, name):
    _, rows, cols = part.shape
    tm = _pick(rows, 672, 16)
    p4 = part.reshape(4, 2, rows, cols)

    def body(core_ref, p_ref, g_ref, o_ref):
        o_ref[...] = (p_ref[...].astype(F32) + g_ref[...].astype(F32)).astype(BF16)

    blk = pl.BlockSpec((None, tm, cols), lambda k, i, cr: (k, i, 0))
    return pl.pallas_call(
        body, name=name,
        grid_spec=pltpu.PrefetchScalarGridSpec(
            num_scalar_prefetch=1, grid=(4, rows // tm),
            in_specs=[pl.BlockSpec((None, None, tm, cols), lambda k, i, cr: (k, cr[0], i, 0)), blk], out_specs=blk),
        out_shape=_sds((4, rows, cols), BF16), compiler_params=_params(("parallel", "parallel")),
    )(core, p4, got)


def _chip_sum_adamw(pair_sums, landed, chip, w, m, v, name):
    _, rows, cols = pair_sums.shape
    tm = _pick(rows, 448, 16)

    def body(chip_ref, s_ref, l_ref, w_ref, m_ref, v_ref, g_ref, d_ref, nm_ref, nv_ref):
        acc = s_ref[...].astype(F32)
        for j in range(3):
            acc = acc + l_ref[j].astype(F32)
        g_ref[...] = acc
        d_ref[...], nm_ref[...], nv_ref[...] = _adamw_math(w_ref[...], acc, m_ref[...], v_ref[...])

    blk = pl.BlockSpec((tm, cols), lambda i, ch: (i, 0))
    return pl.pallas_call(
        body, name=name,
        grid_spec=pltpu.PrefetchScalarGridSpec(
            num_scalar_prefetch=1, grid=(rows // tm,),
            in_specs=[pl.BlockSpec((None, tm, cols), lambda i, ch: (ch[0], i, 0)),
                      pl.BlockSpec((3, tm, cols), lambda i, ch: (0, i, 0)), blk, blk, blk],
            out_specs=[blk] * 4),
        out_shape=[_sds((rows, cols), F32)] * 4, compiler_params=_params(("parallel",)),
    )(chip, pair_sums, landed, w, m, v)


_HBM = pl.BlockSpec(memory_space=pltpu.HBM)
_SEM = pl.BlockSpec(memory_space=pltpu.SEMAPHORE)
_EFFECT = pltpu.SideEffectType.DATAFLOW_SIDE_EFFECTING


def _chip_routes(n):
    def plan(x, y, c):
        routes = []
        for a in range(n):
            for j in range(1, 4):
                px, py = x ^ (j >> 1), y ^ (j & 1)
                routes.append((a, 2 * px + py, (px, py, c), j - 1))
        return routes
    return plan, 3 * n


def _bcast_routes(n):
    def plan(x, y, c):
        routes = []
        for a in range(n):
            for k in range(1, N_DEV):
                peer = (x ^ ((k >> 2) & 1), y ^ ((k >> 1) & 1), c ^ (k & 1))
                routes.append((a, 0, peer, 4 * x + 2 * y + c))
        return routes
    return plan, 7 * n


def _route_copies(srcs, lands, send_sems, recv_sems, routes):
    return [pltpu.make_async_remote_copy(
        src_ref=srcs[a].at[sb], dst_ref=lands[a].at[lb], send_sem=send_sems.at[r], recv_sem=recv_sems.at[r],
        device_id=peer, device_id_type=MESH_T) for r, (a, sb, peer, lb) in enumerate(routes)]


def _exchange_start(srcs, lands, routes, name, after=()):
    plan, count = routes
    n = len(srcs)
    n_in = 2 * n + len(after)

    def body(*refs):
        send_sems, recv_sems = refs[n_in], refs[n_in + 1]
        token = refs[-1]
        for cp in _route_copies(refs[:n], refs[n:2 * n], send_sems, recv_sems, plan(*_mesh_pos())):
            cp.start()
        token[...] = jnp.zeros_like(token)

    args = [pltpu.with_memory_space_constraint(a, pltpu.HBM) for a in list(srcs) + list(lands)]
    out = pl.pallas_call(
        body, name=name,
        out_shape=(pltpu.SemaphoreType.DMA((count,)), pltpu.SemaphoreType.DMA((count,)),
                   *[pltpu.HBM(a.shape, a.dtype) for a in args], _sds((8, 128), F32)),
        in_specs=[_HBM] * (2 * n) + [pl.BlockSpec(memory_space=pl.ANY)] * len(after),
        out_specs=(_SEM, _SEM, *([_HBM] * (2 * n)), pl.BlockSpec(memory_space=pltpu.VMEM)),
        input_output_aliases={i: 2 + i for i in range(2 * n)},
        compiler_params=pltpu.CompilerParams(has_side_effects=_EFFECT),
    )(*args, *after)
    return (out[0], out[1], list(out[2:2 + 2 * n]), routes), out[-1]


def _exchange_wait(state, after, name):
    send_sems, recv_sems, bufs, (plan, count) = state
    n = len(bufs) // 2

    def body(*refs):
        send_s, recv_s = refs[2 * n], refs[2 * n + 1]
        for cp in _route_copies(refs[:n], refs[n:2 * n], send_s, recv_s, plan(*_mesh_pos())):
            cp.wait_send()
            cp.wait_recv()

    out = pl.pallas_call(
        body, name=name, out_shape=tuple(pltpu.HBM(a.shape, a.dtype) for a in bufs),
        in_specs=[_HBM] * (2 * n) + [_SEM, _SEM, pl.BlockSpec(memory_space=pl.ANY)], out_specs=tuple([_HBM] * (2 * n)),
        input_output_aliases={i: i for i in range(2 * n)},
        compiler_params=pltpu.CompilerParams(has_side_effects=_EFFECT),
    )(*bufs, send_sems, recv_sems, after)
    return list(out[:n]), list(out[n:])


def _group_routes(js):
    def plan(x, y, c):
        return [(0, 0, (x ^ (j >> 1), y ^ (j & 1), c), 2 * j + c) for j in js]
    return plan, len(js)


def _pair_fill(groups, js, name, after=()):
    def body(*refs):
        g_ref, send_sems, recv_sems = refs[-3:]
        x, y, c = _mesh_pos()
        sends = []
        for n, j in enumerate(js):
            mine = g_ref.at[2 * j + c]
            sends.append(pltpu.make_async_remote_copy(
                src_ref=mine, dst_ref=mine, send_sem=send_sems.at[n], recv_sem=recv_sems.at[n],
                device_id=(x, y, 1 - c), device_id_type=MESH_T))
        for cp in sends:
            cp.start()
        for n, j in enumerate(js):
            pltpu.make_async_remote_copy(
                src_ref=g_ref.at[2 * j + c], dst_ref=g_ref.at[2 * j + 1 - c], send_sem=send_sems.at[n],
                recv_sem=recv_sems.at[n], device_id=(x, y, 1 - c), device_id_type=MESH_T).wait_recv()
        for cp in sends:
            cp.wait_send()

    hbm = pl.BlockSpec(memory_space=pl.ANY)
    return pl.pallas_call(
        body, name=name, in_specs=[hbm] * (1 + len(after)), out_specs=hbm, out_shape=_sds(groups.shape, groups.dtype),
        input_output_aliases={0: 0},
        scratch_shapes=[pltpu.SemaphoreType.DMA((len(js),)), pltpu.SemaphoreType.DMA((len(js),))],
    )(groups, *after)


def _in_proj_group(h_all, groups, j0, ng, chip, px_prev, after, name):
    rows_all = h_all.shape[0]
    gcols = IN_COLS // 4
    tm = _pick(rows_all, 1536, 128)
    g4 = groups.reshape(4, gcols, D)

    n_lead = (1 if px_prev is not None else 0) + len(after)
    lead = ([px_prev] if px_prev is not None else []) + list(after)

    def body(chip_ref, *refs):
        h_ref, w_ref, o_ref = refs[n_lead:]
        o_ref[...] = _dot(h_ref[...], w_ref[...], NT).astype(BF16)

    return pl.pallas_call(
        body, name=name,
        grid_spec=pltpu.PrefetchScalarGridSpec(
            num_scalar_prefetch=1, grid=(ng, rows_all // tm),
            in_specs=[pl.BlockSpec(memory_space=pl.ANY)] * n_lead
            + [pl.BlockSpec((tm, D), lambda n, i, ch: (i, 0)),
               pl.BlockSpec((None, gcols, D), lambda n, i, ch: (j0 + n, 0, 0))],
            out_specs=pl.BlockSpec((tm, gcols), lambda n, i, ch: (i, ch[0] ^ (j0 + n)))),
        out_shape=_sds((rows_all, IN_COLS), BF16),
        input_output_aliases={1: 0} if px_prev is not None else {},
        compiler_params=_params(("parallel", "parallel")),
    )(chip, *lead, h_all, g4)


def _d_h_groups(dp_all, groups, chip, after):
    rows_all = dp_all.shape[0]
    gcols = IN_COLS // 4
    tm = _pick(rows_all, 1536, 128)
    g4 = groups.reshape(4, gcols, D)
    n_lead = len(after)

    def body(chip_ref, *refs):
        a_ref, w_ref, o_ref = refs[n_lead:]
        j = pl.program_id(1)
        part = _dot(a_ref[...], w_ref[...])

        @pl.when(j == 0)
        def _():
            o_ref[...] = part

        @pl.when(j > 0)
        def _():
            o_ref[...] += part

    return pl.pallas_call(
        body, name="d_h",
        grid_spec=pltpu.PrefetchScalarGridSpec(
            num_scalar_prefetch=1, grid=(rows_all // tm, 4),
            in_specs=[pl.BlockSpec(memory_space=pl.ANY)] * n_lead
            + [pl.BlockSpec((tm, gcols), lambda i, j, ch: (i, ch[0] ^ j)),
               pl.BlockSpec((None, gcols, D), lambda i, j, ch: (j, 0, 0))],
            out_specs=pl.BlockSpec((tm, D), lambda i, j, ch: (i, 0))),
        out_shape=_sds((rows_all, D), F32),
        compiler_params=_params(("parallel", "arbitrary")),
    )(chip, *after, dp_all, g4)


def _reduce_scatter_start(parts, core, name):
    got = _pair_exchange(parts, name + "_pair")
    sums = [_pair_add(p, g, core, "%s_add_%d" % (name, i)) for i, (p, g) in enumerate(zip(parts, got))]
    lands = [lax.empty((3,) + s_.shape[1:], BF16) for s_ in sums]
    return _exchange_start(sums, lands, _chip_routes(len(sums)), name + "_start")


def _reduce_scatter_finish(rs_state, after, chip, wmv, name):
    sums, landed = _exchange_wait(rs_state, after, name + "_wait")
    return [_chip_sum_adamw(s_, l_, chip, *t, "%s_adamw_%d" % (name, i))
            for i, (s_, l_, t) in enumerate(zip(sums, landed, wmv))]


def _local_step(x, c, ctx, norm_w, ret_log2_decay, q_norm_w, k_norm_w, loss_target,
                mod, proj_in, proj_back, get_w_o, on_out_grads, on_in_grad, started=()):
    nb, seq, _ = x.shape
    cx = ctx.shape[1]
    t_rows, c_rows = nb * seq, nb * cx
    rows_all = t_rows + c_rows
    nc = seq // CH
    tm = _pick(seq, 256, 128)
    te = _pick(seq, 512, 128)
    assert cx % tm == 0 and t_rows % cx == 0 and seq % GRID_W == 0

    x2 = x.reshape(t_rows, D)
    ctx2 = ctx.reshape(c_rows, D)
    tgt = loss_target.reshape(t_rows, D)
    lg = _log_gamma(ret_log2_decay)
    cos, sin = _rope_tables(seq)

    mod3 = mod[:, None, :]
    h_all = _norm_fwd(x2, mod3, norm_w, rows_all, 0, seq, 0, None, te, "norm_fwd", after=started)
    h_all = _norm_fwd(ctx2, mod3, norm_w, rows_all, t_rows, c_rows, nb, h_all, tm, "norm_fwd_ctx")
    px = proj_in(h_all)
    s0f, s0b = _ctx_state(px, lg, nb, t_rows, cx)
    o_f, o_b, hist_f, hist_b = _ret_fwd(px, lg, s0f, s0b, nb, nc)
    q16 = _qk_prep(px, q_norm_w, cos, sin, t_rows, 0, AQ, HQ, 4, seq, te, "q_prep")
    kx16 = _qk_prep(px, k_norm_w, cos, sin, t_rows, 0, AK, HKV, HKV, seq, te, "k_prep")
    kc16 = _qk_prep(px, k_norm_w, None, None, c_rows, t_rows, AK, HKV, HKV, seq, tm, "kc_prep")
    o_att, yatt16, lse = _att_fwd(q16, kx16, kc16, px, nb, seq, cx, te)
    w_o_ret16, w_o_att16, w_out16 = get_w_o(lse)
    yret16, a_ret, a_att, y16, dxn, dout16, dgate, loss_b = _merge_out(
        o_f, o_b, yatt16, px, w_o_ret16, w_o_att16, w_out16, x2, tgt, mod3, nb, seq, tm)

    gw_out = _matmul(y16, dout16, ta=True, tm=D, tn=D, tk=D, out_dtype=BF16, name="gw_out")
    da_ret16, da_att16, dmr16, dma16, do16, drg16, dao, dag16 = _bwd_branches(
        dout16, w_out16, w_o_ret16, w_o_att16, px, a_ret, a_att, o_f, o_b, o_att, tm)
    gw_o_ret = _matmul(yret16, da_ret16, ta=True, tm=D, tn=D, tk=D, out_dtype=BF16, name="gw_o_ret")
    gw_o_att = _matmul(yatt16, da_att16, ta=True, tm=D, tn=D, tk=D, out_dtype=BF16, name="gw_o_att")
    out_state, out_started = on_out_grads([gw_o_ret, gw_o_att, gw_out])
    daq16, gq, dkx, dvx, dkc, dvc = _att_bwd(q16, kx16, kc16, px, dao, o_att, lse, q_norm_w, cos, sin, nb, seq, cx, te,
                                             after=out_started)
    dak16, gk_lat = _qk_prep_bwd(dkx.reshape(t_rows, HKV * HD), px, k_norm_w, cos, sin, t_rows, 0, AK, HKV, HKV, seq, te,
                                 "k_prep_bwd")
    dcak16, gk_ctx = _qk_prep_bwd(dkc.reshape(c_rows, HKV * HD), px, k_norm_w, None, None, c_rows, t_rows, AK, HKV, HKV,
                                  seq, tm, "kc_prep_bwd")
    dq_f, dk_f, dv_f, dq_b, dk_b, dv_b, ds_f, ds_b, dlg_scan = _ret_bwd(px, lg, do16, hist_f, hist_b, nb, nc)
    dck16, dcv16, dlg_ctx = _ctx_state_bwd(px, lg, ds_f, ds_b, nb, t_rows, cx)
    dp_all = _assemble_lat(rows_all, dk_f, dk_b, dv_f, dv_b, dak16, dvx.reshape(t_rows, HKV * HD), dq_f, dq_b, drg16,
                           daq16, dag16, dmr16, dma16, tm)
    dp_all = _assemble_ctx(dp_all, dck16, dcv16, dcak16, dvc.reshape(c_rows, HKV * HD), t_rows, tm)
    gw_in_t = _matmul(dp_all, h_all, ta=True, tm=1536, tn=D, tk=2304, out_dtype=BF16, name="gw_in")
    in_state, in_started = on_in_grad(gw_in_t)
    dh = proj_back(dp_all, in_started)
    grad_x, dsh, dsc, gnw_lat = _norm_bwd(dh, x2, mod3, norm_w, dxn, 0, seq, 0, te, "norm_bwd")
    dsh_c, dsc_c, gnw_ctx = _norm_bwd(dh, ctx2, mod3, norm_w, None, t_rows, c_rows, nb, tm, "norm_bwd_ctx")

    dlg = (jnp.sum(dlg_scan[:, :, 0], axis=0) + jnp.sum(dlg_ctx[:, :, :2, 0], axis=0).T.reshape(2 * RH)).reshape(1, 2 * RH)
    misc = jnp.concatenate([gq, gk_lat + gk_ctx, dlg, jnp.sum(loss_b[:, 0, 0]).reshape(1, 1),
                            jnp.zeros((1, D - 2 * HD - 2 * RH - 1), F32)], axis=1)
    rows = []
    for b in range(nb):
        rows += [dsh[b], dsc[b], dgate[b]]
    rows += [dsh_c[0], dsc_c[0]] + [c[b:b + 1] for b in range(nb)] + [gnw_lat + gnw_ctx, misc]
    payload = jnp.concatenate(rows + [jnp.zeros((PAY_ROWS - len(rows), D), F32)], axis=0)
    return grad_x.reshape(nb, seq, D), out_state, in_state, payload


def _finish_small(gathered, nb, c_ctx, ret_log2_decay, w_ada16, dev):
    n_dev = gathered.shape[0]
    loc = 3 * D // n_dev
    dmod_all = gathered[:, :3 * nb].reshape(n_dev * nb, 3 * D)
    dmodc_parts = jnp.concatenate([gathered[:, 3 * nb:3 * nb + 2].reshape(n_dev, 2 * D), jnp.zeros((n_dev, D), F32)], axis=1)
    c_all = gathered[:, 3 * nb + 2:4 * nb + 2].reshape(n_dev * nb, D)
    nw_parts = gathered[:, 4 * nb + 2]
    misc_parts = gathered[:, 4 * nb + 3]
    n_rows = n_dev * nb + n_dev
    pad = (-n_rows) % 16
    c_rows = jnp.concatenate([c_all, jnp.broadcast_to(c_ctx.reshape(1, D), (n_dev, D)), jnp.zeros((pad, D), F32)], axis=0)
    dm_rows = jnp.concatenate([dmod_all, dmodc_parts, jnp.zeros((pad, 3 * D), F32)], axis=0)
    dm_loc_rows = lax.dynamic_slice_in_dim(dm_rows, dev * loc, loc, axis=1)
    r_pad = jnp.full((1, D), -1.0, F32).at[:, 2 * HD:2 * HD + 2 * RH].set(ret_log2_decay.reshape(1, 2 * RH))
    gb, gc, gnw, misc, gwa = _small_final(dmod_all, dmodc_parts, c_rows, dm_loc_rows, nw_parts, misc_parts,
                                          c_ctx.reshape(1, D), r_pad, w_ada16)
    return (gb, gc, gnw, misc[:, :HD], misc[:, HD:2 * HD], misc[:, 2 * HD:2 * HD + 2 * RH], gwa,
            misc[0, 2 * HD + 2 * RH])


def kernel(x, c, ctx, c_ctx, norm_w, w_ada, b_ada, w_in, ret_log2_decay, q_norm_w, k_norm_w, w_o_ret, w_o_att, w_out, loss_target, m_c_ctx, m_norm_w, m_w_ada, m_b_ada, m_w_in, m_ret_log2_decay, m_q_norm_w, m_k_norm_w, m_w_o_ret, m_w_o_att, m_w_out, v_c_ctx, v_norm_w, v_w_ada, v_b_ada, v_w_in, v_ret_log2_decay, v_q_norm_w, v_k_norm_w, v_w_o_ret, v_w_o_att, v_w_out):
    nb = x.shape[0]
    mx, my, mc = _mesh_pos()
    dev = 4 * mx + 2 * my + mc
    core = jnp.reshape(mc, (1,)).astype(jnp.int32)
    chip = jnp.reshape(2 * mx + my, (1,)).astype(jnp.int32)

    n_loc = 3 * D // N_DEV
    c8 = jnp.zeros((8, D), F32).at[:nb].set(c).at[nb].set(c_ctx)
    c_land = lax.dynamic_update_slice(lax.empty((N_DEV, 8, D), F32), c8[None], (dev, 0, 0))
    c_state, c_token = _exchange_start([c8[None]], [c_land], _bcast_routes(1), "gather_c_start")
    w_in_t = jnp.transpose(w_in[0])
    in_shard = w_in_t.astype(BF16)
    groups = lax.dynamic_update_slice(lax.empty((N_DEV,) + in_shard.shape, BF16), in_shard[None], (mc, 0, 0))
    groups = _pair_fill(groups, (0,), "gather_in_pair", after=(c_token,))
    _, (c_all,) = _exchange_wait(c_state, groups, "gather_c_wait")
    ada_shard = w_ada[0].astype(BF16)
    b_loc = lax.dynamic_slice(b_ada, (0, dev * n_loc), (1, n_loc))
    mod_cols = _mod_part(c_all.reshape(N_DEV * 8, D), ada_shard, b_loc)
    (mod_all,) = _all_gather([mod_cols], "gather_mod")
    mod = jnp.transpose(lax.dynamic_slice(mod_all, (0, dev * 8, 0), (N_DEV, 8, n_loc)), (1, 0, 2)).reshape(8, 3 * D)
    ada_land = lax.dynamic_update_slice(lax.empty((N_DEV,) + ada_shard.shape, BF16), ada_shard[None], (dev, 0, 0))

    (near_send, near_recv, near_bufs, near_routes), gin_token = _exchange_start(
        [in_shard[None]], [groups], _group_routes((1, 2)), "gather_in_start", after=(mod_all,))
    w_in_groups, wo_states, ada_states = [], [], []
    wo_shards = [w_[0].astype(BF16) for w_ in (w_o_ret, w_o_att, w_out)]
    wo_lands = [lax.dynamic_update_slice(lax.empty((N_DEV,) + s_.shape, BF16), s_[None], (dev, 0, 0)) for s_ in wo_shards]

    def _state(send, recv, src, groups, routes):
        return send, recv, [src, groups], routes

    def proj_in(h_all):
        src, groups = near_bufs
        px = _in_proj_group(h_all, groups, 0, 1, chip, None, (gin_token,), "in_proj_0")
        (src,), (groups,) = _exchange_wait(_state(near_send, near_recv, src, groups, near_routes), px,
                                           "gather_in_wait_near")
        groups = _pair_fill(groups, (1, 2), "gather_in_fill_near")
        (far_send, far_recv, (src, groups), far_routes), far_token = _exchange_start(
            [src], [groups], _group_routes((3,)), "gather_in_start_far")
        wo_state, wo_token = _exchange_start([s_[None] for s_ in wo_shards], wo_lands, _bcast_routes(3),
                                             "gather_wo_start", after=(far_token,))
        wo_states.append(wo_state)
        ada_state, ada_token = _exchange_start([ada_shard[None]], [ada_land], _bcast_routes(1), "gather_ada_start",
                                               after=(wo_token,))
        ada_states.append(ada_state)
        px = _in_proj_group(h_all, groups, 1, 2, chip, px, (ada_token,), "in_proj_near")
        (src,), (groups,) = _exchange_wait(_state(far_send, far_recv, src, groups, far_routes), px,
                                           "gather_in_wait_far")
        groups = _pair_fill(groups, (3,), "gather_in_fill_far")
        px = _in_proj_group(h_all, groups, 3, 1, chip, px, (), "in_proj_far")
        w_in_groups.append(groups)
        return px

    def proj_back(dp_all, after):
        return _d_h_groups(dp_all, w_in_groups[0], chip, after)

    def get_w_o(after):
        _, (l_ret, l_att, l_out) = _exchange_wait(wo_states[0], after, "gather_wo_wait")
        return l_ret.reshape(RH * DV, D), l_att.reshape(D, D), l_out.reshape(D, D)

    def on_out_grads(grads):
        parts = [g_.reshape(N_DEV, g_.shape[0] // N_DEV, D) for g_ in grads]
        state, token = _reduce_scatter_start(parts, core, "rs_out")
        return state, (token,)

    def on_in_grad(grad):
        state, token = _reduce_scatter_start([grad.reshape(N_DEV, IN_COLS // N_DEV, D)], core, "rs_in")
        return state, (token,)

    grad_x, out_state, in_state, payload = _local_step(
        x, c, ctx, norm_w, ret_log2_decay, q_norm_w, k_norm_w, loss_target,
        mod, proj_in, proj_back, get_w_o, on_out_grads, on_in_grad, started=(gin_token,))

    pay_land = lax.dynamic_update_slice(lax.empty((N_DEV,) + payload.shape, F32), payload[None], (dev, 0, 0))
    pay_state, pay_token = _exchange_start([payload[None]], [pay_land], _bcast_routes(1), "gather_small_start")

    out_res = _reduce_scatter_finish(out_state, pay_token, chip,
                                     [(w_[0], m_[0], v_[0]) for w_, m_, v_ in ((w_o_ret, m_w_o_ret, v_w_o_ret),
                                                                                (w_o_att, m_w_o_att, v_w_o_att),
                                                                                (w_out, m_w_out, v_w_out))], "rs_out")
    (in_res,) = _reduce_scatter_finish(in_state, out_res[0][0], chip,
                                       [(w_in_t, jnp.transpose(m_w_in[0]), jnp.transpose(v_w_in[0]))], "rs_in")

    _, (gathered,) = _exchange_wait(pay_state, in_res[0], "gather_small_wait")
    _, (l_ada,) = _exchange_wait(ada_states[0], gathered, "gather_ada_wait")
    w_ada16 = jnp.transpose(l_ada, (1, 0, 2)).reshape(D, 3 * D)
    gb, gc, gnw, gq, gk, gr, gwa, loss = _finish_small(gathered, nb, c_ctx, ret_log2_decay, w_ada16, dev)
    big = {4: [jnp.transpose(r)[None] for r in in_res]}
    for i, res in zip((8, 9, 10), out_res):
        big[i] = [r[None] for r in res]
    small_g = {0: gc.reshape(c_ctx.shape), 1: gnw, 2: gwa[None], 3: gb, 5: gr.reshape(ret_log2_decay.shape), 6: gq, 7: gk}
    weights = [c_ctx, norm_w, w_ada, b_ada, w_in, ret_log2_decay, q_norm_w, k_norm_w, w_o_ret, w_o_att, w_out]
    ms = [m_c_ctx, m_norm_w, m_w_ada, m_b_ada, m_w_in, m_ret_log2_decay, m_q_norm_w, m_k_norm_w, m_w_o_ret, m_w_o_att, m_w_out]
    vs = [v_c_ctx, v_norm_w, v_w_ada, v_b_ada, v_w_in, v_ret_log2_decay, v_q_norm_w, v_k_norm_w, v_w_o_ret, v_w_o_att, v_w_out]
    grads, deltas, new_ms, new_vs = [], [], [], []
    for i, (w, m, v) in enumerate(zip(weights, ms, vs)):
        if i in big:
            res = big[i]
        else:
            shape2 = (-1, w.shape[-1])
            g = small_g[i]
            res = [g] + [r.reshape(w.shape) for r in _adamw(w.reshape(shape2), g.reshape(shape2), m.reshape(shape2),
                                                             v.reshape(shape2), "adamw_%d" % i)]
        for lst, r in zip((grads, deltas, new_ms, new_vs), res):
            lst.append(r)
    return (loss, grad_x, *grads, *deltas, *new_ms, *new_vs)
```

```python
import numpy as np
import jax
import jax.numpy as jnp
from jax import lax
from jax.experimental import pallas as pl
from jax.experimental.pallas import tpu as pltpu

F32 = jnp.float32
BF16 = jnp.bfloat16

D = 1024
RH, DK, DV, CH = 4, 256, 512, 256
HQ, HKV, HD = 8, 2, 128
GRID_W = 64
ROPE_THETA = 10000.0
EPS = 1e-6
RK, RV, AK, AV, RQ, RG, AQ, AG, MR, MA = 0, 1024, 3072, 3328, 3584, 4608, 6656, 7680, 8704, 9728
IN_COLS = 10752
KV_COLS = 3584
N_DEV = 8
LR, B1, B2, ADAM_EPS, WD, STEP = 0.001, 0.9, 0.999, 1e-08, 0.01, 10
PAY_ROWS = 16
VMEM_LIMIT = 56 * 1024 * 1024
MESH_T = pl.DeviceIdType.MESH

NT = (((1,), (1,)), ((), ()))
TN = (((0,), (0,)), ((), ()))
SM_C = (HD ** -0.5) * float(np.log2(np.e))


def _params(sem):
    return pltpu.CompilerParams(dimension_semantics=sem, vmem_limit_bytes=VMEM_LIMIT)


def _pick(n, target, mult=8):
    best = None
    for t in range(mult, min(n, target) + 1, mult):
        if n % t == 0:
            best = t
    return best or n


def _dot(a, b, dn=None):
    if dn is None:
        return jnp.dot(a, b, preferred_element_type=F32)
    return lax.dot_general(a, b, dn, preferred_element_type=F32)


def _sig(v):
    return jax.nn.sigmoid(v)


def _silu(v):
    return v * _sig(v)


def _dsilu(v):
    s = _sig(v)
    return s * (1.0 + v * (1.0 - s))


def _sds(shape, dtype):
    return jax.ShapeDtypeStruct(shape, dtype)


def _matmul(a, b, *, ta=False, tb=False, tm, tn, tk, out_dtype, name, after=()):
    m = a.shape[1] if ta else a.shape[0]
    kdim = a.shape[0] if ta else a.shape[1]
    n = b.shape[0] if tb else b.shape[1]
    tm, tn, tk = _pick(m, tm, 128), _pick(n, tn, 128), _pick(kdim, tk, 128)
    nk = kdim // tk
    dn = (((0 if ta else 1,), (1 if tb else 0,)), ((), ()))

    def body(a_ref, b_ref, *rest):
        o_ref, acc_ref = rest[-2:]
        k = pl.program_id(2)
        part = _dot(a_ref[...].astype(BF16), b_ref[...].astype(BF16), dn)
        if nk == 1:
            o_ref[...] = part.astype(o_ref.dtype)
        else:
            @pl.when(k == 0)
            def _():
                acc_ref[...] = part

            @pl.when(k > 0)
            def _():
                acc_ref[...] += part

            @pl.when(k == nk - 1)
            def _():
                o_ref[...] = acc_ref[...].astype(o_ref.dtype)

    a_spec = pl.BlockSpec((tk, tm), lambda i, j, k: (k, i)) if ta else pl.BlockSpec((tm, tk), lambda i, j, k: (i, k))
    b_spec = pl.BlockSpec((tn, tk), lambda i, j, k: (j, k)) if tb else pl.BlockSpec((tk, tn), lambda i, j, k: (k, j))
    return pl.pallas_call(
        body, name=name, grid=(m // tm, n // tn, nk),
        in_specs=[a_spec, b_spec] + [pl.BlockSpec(memory_space=pl.ANY)] * len(after),
        out_specs=pl.BlockSpec((tm, tn), lambda i, j, k: (i, j)), out_shape=_sds((m, n), out_dtype),
        scratch_shapes=[pltpu.VMEM((tm, tn) if nk > 1 else (8, 128), F32)],
        compiler_params=_params(("parallel", "parallel", "arbitrary")),
    )(a, b, *after)


def _log_gamma(r):
    rp = jnp.full((8, 128), -1.0, F32).at[:2, :RH].set(r.reshape(2, RH))

    def body(r_ref, o_ref):
        o_ref[...] = jnp.log1p(-jnp.exp2(r_ref[...]))

    out = pl.pallas_call(body, name="log_gamma", out_shape=_sds((8, 128), F32))(rp)
    return out[:2, :RH]


def _mod_part(c_rows, w_ada_loc16, b_loc):
    def body(c_ref, w_ref, b_ref, o_ref):
        o_ref[...] = _dot(_silu(c_ref[...]).astype(BF16), w_ref[...]) + b_ref[...]

    return pl.pallas_call(
        body, name="mod_part", out_shape=_sds((c_rows.shape[0], w_ada_loc16.shape[1]), F32),
    )(c_rows, w_ada_loc16, b_loc)


def _norm_fwd(x2, mod3, norm_w, rows_all, row_off, rows_per_group, group0, h_prev, tm, name, after=()):
    rows = x2.shape[0]
    rb0 = row_off // tm
    bpg = rows_per_group // tm

    def body(*refs):
        x_ref, sh_ref, sc_ref, nw_ref, o_ref = refs[-5:]
        xv = x_ref[...]
        r = lax.rsqrt(jnp.mean(xv * xv, axis=-1, keepdims=True) + EPS)
        o_ref[...] = ((xv * r) * nw_ref[...] * (1.0 + sc_ref[...]) + sh_ref[...]).astype(BF16)

    in_specs = [pl.BlockSpec((tm, D), lambda i: (i, 0)),
                pl.BlockSpec((None, 1, D), lambda i: (group0 + i // bpg, 0, 0)),
                pl.BlockSpec((None, 1, D), lambda i: (group0 + i // bpg, 0, 1)),
                pl.BlockSpec((1, D), lambda i: (0, 0))]
    in_specs = [pl.BlockSpec(memory_space=pl.ANY)] * len(after) + in_specs
    args = list(after) + [x2, mod3, mod3, norm_w]
    alias = {}
    if h_prev is not None:
        in_specs.insert(0, pl.BlockSpec(memory_space=pl.ANY))
        args.insert(0, h_prev)
        alias = {0: 0}
    return pl.pallas_call(
        body, name=name, grid=(rows // tm,), in_specs=in_specs,
        out_specs=pl.BlockSpec((tm, D), lambda i: (rb0 + i, 0)), out_shape=_sds((rows_all, D), BF16),
        input_output_aliases=alias, compiler_params=_params(("parallel",)),
    )(*args)


def _decays(lg, fwd):
    ii = lax.broadcasted_iota(jnp.int32, (CH, CH), 0)
    jj = lax.broadcasted_iota(jnp.int32, (CH, CH), 1)
    ri = lax.broadcasted_iota(jnp.int32, (CH, 1), 0).astype(F32)
    rel = (ii - jj) if fwd else (jj - ii)
    relf = jnp.maximum(rel, 0).astype(F32)
    mask = jnp.where(rel >= 0, jnp.exp(lg * relf), 0.0)
    qe = (ri + 1.0) if fwd else (CH - ri)
    ke = (CH - 1.0 - ri) if fwd else ri
    return mask, relf, jnp.exp(lg * qe), qe, jnp.exp(lg * ke), ke


def _wide_specs(rowf):
    return [pl.BlockSpec((CH, 2 * DK), lambda b, c: (rowf(b, c), RQ // (2 * DK))),
            pl.BlockSpec((CH, 2 * DK), lambda b, c: (rowf(b, c), RQ // (2 * DK) + 1)),
            pl.BlockSpec((CH, RH * DK), lambda b, c: (rowf(b, c), RK // (RH * DK))),
            pl.BlockSpec((CH, 2 * DV), lambda b, c: (rowf(b, c), RV // (2 * DV))),
            pl.BlockSpec((CH, 2 * DV), lambda b, c: (rowf(b, c), RV // (2 * DV) + 1))]


def _head_qkv(refs, h):
    q0, q1, k, v0, v1 = refs
    lo = h % 2
    q = (q0, q1)[h // 2][:, lo * DK:(lo + 1) * DK].astype(F32)
    kk = k[:, h * DK:(h + 1) * DK].astype(F32) * (DK ** -0.5)
    v16 = (v0, v1)[h // 2][:, lo * DV:(lo + 1) * DV].astype(BF16)
    return q, kk, v16


def _ctx_specs(t_rows, cx):
    rb = t_rows // cx
    return [pl.BlockSpec((cx, RH * DK), lambda b, c: (rb + b, RK // (RH * DK))),
            pl.BlockSpec((cx, 2 * DV), lambda b, c: (rb + b, RV // (2 * DV))),
            pl.BlockSpec((cx, 2 * DV), lambda b, c: (rb + b, RV // (2 * DV) + 1))]


def _ctx_kv(refs, h):
    k, v0, v1 = refs
    kk = k[:, h * DK:(h + 1) * DK].astype(F32) * (DK ** -0.5)
    lo = h % 2
    return kk, (v0, v1)[h // 2][:, lo * DV:(lo + 1) * DV].astype(BF16)


def _ret_fwd(px, lg, nb, nc, cx):
    t_rows = nb * nc * CH

    def body(lg_ref, *refs):
        ins = (refs[0:5], refs[5:10])
        ctx_refs = refs[10:13]
        of_ref, ob_ref, hf_ref, hb_ref, sf, sb = refs[13:]
        c = pl.program_id(1)

        @pl.when(c == 0)
        def _():
            pos = lax.broadcasted_iota(jnp.int32, (cx, 1), 0).astype(F32)
            for h in range(RH):
                k, v16 = _ctx_kv(ctx_refs, h)
                sf[h] = _dot((k * jnp.exp(lg_ref[0, h] * (cx - 1.0 - pos))).astype(BF16), v16, TN)
                sb[h] = _dot((k * jnp.exp(lg_ref[1, h] * pos)).astype(BF16), v16, TN)

        for d, (o_ref, h_ref, s) in enumerate(((of_ref, hf_ref, sf), (ob_ref, hb_ref, sb))):
            for h in range(RH):
                lg_d = lg_ref[d, h]
                mask, _, qd, _, kd, _ = _decays(lg_d, d == 0)
                q, k, v16 = _head_qkv(ins[d], h)
                a = _dot(q.astype(BF16), k.astype(BF16), NT)
                st = s[h]
                st16 = st.astype(BF16)
                h_ref[h] = st16
                o = _dot((a * mask).astype(BF16), v16) + _dot((q * qd).astype(BF16), st16)
                o_ref[:, h * DV:(h + 1) * DV] = o.astype(BF16)
                s[h] = st * jnp.exp(lg_d * CH) + _dot((k * kd).astype(BF16), v16, TN)

    def fw(b, c):
        return b * nc + c

    def bw(b, c):
        return b * nc + nc - 1 - c

    in_specs = [pl.BlockSpec(memory_space=pltpu.SMEM)] + _wide_specs(fw) + _wide_specs(bw) + _ctx_specs(t_rows, cx)
    out_specs = [pl.BlockSpec((CH, RH * DV), lambda b, c: (fw(b, c), 0)),
                 pl.BlockSpec((CH, RH * DV), lambda b, c: (bw(b, c), 0)),
                 pl.BlockSpec((None, None, RH, DK, DV), lambda b, c: (b, c, 0, 0, 0)),
                 pl.BlockSpec((None, None, RH, DK, DV), lambda b, c: (b, nc - 1 - c, 0, 0, 0))]
    return pl.pallas_call(
        body, name="ret_fwd", grid=(nb, nc), in_specs=in_specs, out_specs=out_specs,
        out_shape=[_sds((t_rows, RH * DV), BF16)] * 2 + [_sds((nb, nc, RH, DK, DV), BF16)] * 2,
        scratch_shapes=[pltpu.VMEM((RH, DK, DV), F32), pltpu.VMEM((RH, DK, DV), F32)],
        compiler_params=_params(("parallel", "arbitrary")),
    )(lg, *([px] * 13))


def _rope_tables(seq):
    rows = seq // GRID_W
    row = np.repeat(np.arange(rows, dtype=np.float32), GRID_W)
    col = np.tile(np.arange(GRID_W, dtype=np.float32), rows)
    half = HD // 2
    freqs = (ROPE_THETA ** (-np.arange(0, half, 2, dtype=np.float32) / half)).astype(np.float32)
    ang = np.concatenate([row[:, None] * freqs, col[:, None] * freqs], axis=-1).astype(np.float32)
    cos = np.repeat(np.cos(ang), 2, axis=-1).astype(np.float32)
    sin = np.repeat(np.sin(ang), 2, axis=-1).astype(np.float32)
    sign = np.tile(np.array([-1.0, 1.0], np.float32), HD // 2)
    return jnp.asarray(cos), jnp.asarray(sin * sign)


def _swap_pairs(v):
    lane = lax.broadcasted_iota(jnp.int32, v.shape, 1)
    return jnp.where((lane & 1) == 0, pltpu.roll(v, HD - 1, 1), pltpu.roll(v, 1, 1))


def _qk_prep(px, nw, cos, sin, rows, row_off, col_off, heads, hb, seq, tm, name):
    rope = cos is not None
    rb0 = row_off // tm
    pb = seq // tm if rope else 1
    bw = hb * HD

    def body(*refs):
        if rope:
            x_ref, w_ref, c_ref, s_ref, o_ref = refs
        else:
            x_ref, w_ref, o_ref = refs
        for h in range(hb):
            sl = slice(h * HD, (h + 1) * HD)
            xv = x_ref[:, sl].astype(F32)
            r = lax.rsqrt(jnp.mean(xv * xv, axis=-1, keepdims=True) + EPS)
            t = (xv * r) * w_ref[...]
            if rope:
                t = t * c_ref[...] + _swap_pairs(t) * s_ref[...]
            o_ref[:, sl] = t.astype(BF16)

    in_specs = [pl.BlockSpec((tm, bw), lambda i, j: (rb0 + i, col_off // bw + j)),
                pl.BlockSpec((1, HD), lambda i, j: (0, 0))]
    args = [px, nw]
    if rope:
        in_specs += [pl.BlockSpec((tm, HD), lambda i, j: (i % pb, 0))] * 2
        args += [cos, sin]
    return pl.pallas_call(
        body, name=name, grid=(rows // tm, heads // hb), in_specs=in_specs,
        out_specs=pl.BlockSpec((tm, bw), lambda i, j: (i, j)), out_shape=_sds((rows, heads * HD), BF16),
        compiler_params=_params(("parallel", "parallel")),
    )(*args)


def _att_fwd(q16, kx16, kc16, px, nb, seq, cx, tq):
    t_rows = nb * seq
    nq = seq // tq
    rep = HQ // HKV
    gw = rep * HD

    def body(q_ref, kx_ref, kc_ref, vx_ref, vc_ref, g_ref, o_ref, y_ref, l_ref):
        kx = kx_ref[...]
        kc = kc_ref[...]
        vx = vx_ref[...].astype(BF16)
        vc = vc_ref[...].astype(BF16)
        l_ref[...] = jnp.zeros_like(l_ref)
        for r in range(rep):
            sl = slice(r * HD, (r + 1) * HD)
            q = q_ref[:, sl]
            s1 = _dot(q, kx, NT)
            s2 = _dot(q, kc, NT)
            m = jnp.maximum(jnp.max(s1, axis=-1, keepdims=True), jnp.max(s2, axis=-1, keepdims=True))
            e1 = jnp.exp2((s1 - m) * SM_C)
            e2 = jnp.exp2((s2 - m) * SM_C)
            tot = jnp.sum(e1, axis=-1, keepdims=True) + jnp.sum(e2, axis=-1, keepdims=True)
            o = (_dot(e1.astype(BF16), vx) + _dot(e2.astype(BF16), vc)) * (1.0 / tot)
            o_ref[:, sl] = o
            y_ref[:, sl] = (o * _silu(g_ref[:, sl].astype(F32))).astype(BF16)
            l_ref[:, r:r + 1] = m * SM_C + jnp.log(tot) * float(np.log2(np.e))

    qblk = pl.BlockSpec((tq, gw), lambda b, g, i: (b * nq + i, g))
    return pl.pallas_call(
        body, name="att_fwd", grid=(nb, HKV, nq),
        in_specs=[qblk,
                  pl.BlockSpec((seq, HD), lambda b, g, i: (b, g)),
                  pl.BlockSpec((cx, HD), lambda b, g, i: (b, g)),
                  pl.BlockSpec((seq, HD), lambda b, g, i: (b, AV // HD + g)),
                  pl.BlockSpec((cx, HD), lambda b, g, i: (t_rows // cx + b, AV // HD + g)),
                  pl.BlockSpec((tq, gw), lambda b, g, i: (b * nq + i, AG // gw + g))],
        out_specs=[qblk, qblk, pl.BlockSpec((tq, 128), lambda b, g, i: (b * nq + i, g))],
        out_shape=[_sds((t_rows, D), F32), _sds((t_rows, D), BF16), _sds((t_rows, HKV * 128), F32)],
        compiler_params=_params(("parallel", "parallel", "parallel")),
    )(q16, kx16, kc16, px, px, px)


def _gate_specs(tm, col0):
    hw = D // 2
    return [pl.BlockSpec((tm, hw), lambda i: (i, col0 // hw)), pl.BlockSpec((tm, hw), lambda i: (i, col0 // hw + 1))]


def _merge_out(o_f, o_b, yatt16, px, w_o_ret16, w_o_att16, w_out16, x2, tgt, mod3, nb, seq, tm):
    t_rows = nb * seq
    bpb = seq // tm
    hw = D // 2

    def body(of_ref, ob_ref, g0, g1, g2, g3, wr_ref, ya_ref, wa_ref, mr0, mr1, ma0, ma1, wo_ref, x_ref, t_ref, gt_ref,
             yr_ref, ar_ref, aa_ref, y_ref, dxn_ref, dout_ref, dg_ref, loss_ref):
        i = pl.program_id(1)
        for h, g_ref in enumerate((g0, g1, g2, g3)):
            sl = slice(h * DV, (h + 1) * DV)
            o = of_ref[:, sl].astype(F32) + ob_ref[:, sl].astype(F32)
            r = lax.rsqrt(jnp.mean(o * o, axis=-1, keepdims=True) + EPS)
            yr_ref[:, sl] = ((o * r) * _silu(g_ref[...].astype(F32))).astype(BF16)
        ar = _dot(yr_ref[...], wr_ref[...])
        aa = _dot(ya_ref[...], wa_ref[...])
        ar_ref[...] = ar.astype(BF16)
        aa_ref[...] = aa.astype(BF16)
        for j, (mr_ref, ma_ref) in enumerate(((mr0, ma0), (mr1, ma1))):
            sl = slice(j * hw, (j + 1) * hw)
            y_ref[:, sl] = (_sig(mr_ref[...].astype(F32)) * ar[:, sl]
                            + _sig(ma_ref[...].astype(F32)) * aa[:, sl]).astype(BF16)
        out = _dot(y_ref[...], wo_ref[...])
        gate = gt_ref[...]
        diff = x_ref[...] + gate * out - t_ref[...]
        dxn = diff * (1.0 / D)
        dxn_ref[...] = dxn
        dout_ref[...] = (gate * dxn).astype(BF16)
        dg = jnp.sum(dxn * out, axis=0, keepdims=True)
        ls = jnp.broadcast_to(jnp.sum(diff * diff) * (0.5 / D), (1, 128))

        @pl.when(i == 0)
        def _():
            dg_ref[...] = dg
            loss_ref[...] = ls

        @pl.when(i > 0)
        def _():
            dg_ref[...] += dg
            loss_ref[...] += ls

    def cols(width, col0):
        return pl.BlockSpec((tm, width), lambda b, i: (b * bpb + i, col0 // width))

    def whole(rows):
        return pl.BlockSpec((rows, D), lambda b, i: (0, 0))

    row, wide = cols(D, 0), cols(RH * DV, 0)
    gates = [cols(DV, RG + h * DV) for h in range(RH)]
    merge_gates = [cols(hw, MR), cols(hw, MR + hw), cols(hw, MA), cols(hw, MA + hw)]
    return pl.pallas_call(
        body, name="merge_out", grid=(nb, bpb),
        in_specs=[wide, wide] + gates + [whole(RH * DV), row, whole(D)] + merge_gates
        + [whole(D), row, row, pl.BlockSpec((None, 1, D), lambda b, i: (b, 0, 2))],
        out_specs=[wide, row, row, row, row, row, pl.BlockSpec((None, 1, D), lambda b, i: (b, 0, 0)),
                   pl.BlockSpec((None, 1, 128), lambda b, i: (b, 0, 0))],
        out_shape=[_sds((t_rows, RH * DV), BF16)] + [_sds((t_rows, D), BF16)] * 3
        + [_sds((t_rows, D), F32), _sds((t_rows, D), BF16), _sds((nb, 1, D), F32), _sds((nb, 1, 128), F32)],
        compiler_params=_params(("parallel", "arbitrary")),
    )(o_f, o_b, *([px] * RH), w_o_ret16, yatt16, w_o_att16, px, px, px, px, w_out16, x2, tgt, mod3)


def _bwd_branches(dout16, w_out16, w_o_ret16, w_o_att16, px, a_ret, a_att, o_f, o_b, o_att, tm):
    t_rows = dout16.shape[0]
    hw = D // 2

    def body(do_ref, wo_ref, wr_ref, wa_ref, mr0, mr1, ma0, ma1, ar_ref, aa_ref, rg0, rg1, rg2, rg3, of_ref, ob_ref,
             ag0, ag1, oa_ref, dar_ref, daa_ref, dmr_ref, dma_ref, dor_ref, drg_ref, dao_ref, dag_ref):
        dy_all = _dot(do_ref[...], wo_ref[...], NT)
        for j, (mr_ref, ma_ref) in enumerate(((mr0, ma0), (mr1, ma1))):
            sl = slice(j * hw, (j + 1) * hw)
            dy = dy_all[:, sl]
            sr = _sig(mr_ref[...].astype(F32))
            sa = _sig(ma_ref[...].astype(F32))
            dar_ref[:, sl] = (dy * sr).astype(BF16)
            daa_ref[:, sl] = (dy * sa).astype(BF16)
            dmr_ref[:, sl] = (dy * ar_ref[:, sl].astype(F32) * sr * (1.0 - sr)).astype(BF16)
            dma_ref[:, sl] = (dy * aa_ref[:, sl].astype(F32) * sa * (1.0 - sa)).astype(BF16)
        da_ret = dar_ref[...]
        for h, g_ref in enumerate((rg0, rg1, rg2, rg3)):
            sl = slice(h * DV, (h + 1) * DV)
            dy = _dot(da_ret, wr_ref[sl, :], NT)
            g = g_ref[...].astype(F32)
            o = of_ref[:, sl].astype(F32) + ob_ref[:, sl].astype(F32)
            r = lax.rsqrt(jnp.mean(o * o, axis=-1, keepdims=True) + EPS)
            on = o * r
            sg = _sig(g)
            don = dy * (g * sg)
            drg_ref[:, sl] = (dy * on * (sg * (1.0 + g * (1.0 - sg)))).astype(BF16)
            dor_ref[:, sl] = (r * (don - on * jnp.mean(on * don, axis=-1, keepdims=True))).astype(BF16)
        dy_all = _dot(daa_ref[...], wa_ref[...], NT)
        for j, g_ref in enumerate((ag0, ag1)):
            sl = slice(j * hw, (j + 1) * hw)
            dy = dy_all[:, sl]
            g = g_ref[...].astype(F32)
            sg = _sig(g)
            dao_ref[:, sl] = dy * (g * sg)
            dag_ref[:, sl] = (dy * oa_ref[:, sl] * (sg * (1.0 + g * (1.0 - sg)))).astype(BF16)

    def gate(h):
        return pl.BlockSpec((tm, DV), lambda i: (i, RG // DV + h))

    def whole(rows):
        return pl.BlockSpec((rows, D), lambda i: (0, 0))

    row = pl.BlockSpec((tm, D), lambda i: (i, 0))
    wide = pl.BlockSpec((tm, RH * DV), lambda i: (i, 0))
    return pl.pallas_call(
        body, name="bwd_branches", grid=(t_rows // tm,),
        in_specs=[row, whole(D), whole(RH * DV), whole(D)] + _gate_specs(tm, MR) + _gate_specs(tm, MA) + [row, row]
        + [gate(h) for h in range(RH)] + [wide, wide] + _gate_specs(tm, AG) + [row],
        out_specs=[row] * 4 + [wide, wide, row, row],
        out_shape=[_sds((t_rows, D), BF16)] * 4 + [_sds((t_rows, RH * DV), BF16)] * 2
        + [_sds((t_rows, D), F32), _sds((t_rows, D), BF16)],
        compiler_params=_params(("parallel",)),
    )(dout16, w_out16, w_o_ret16, w_o_att16, px, px, px, px, a_ret, a_att, *([px] * RH), o_f, o_b, px, px, o_att)


def _att_bwd(q16, kx16, kc16, px, dao, o_att, lse, q_norm_w, cos, sin, nb, seq, cx, tq, after=()):
    t_rows = nb * seq
    nq = seq // tq
    rep = HQ // HKV
    gw = rep * HD
    scale = HD ** -0.5

    def body(q_ref, kx_ref, kc_ref, vx_ref, vc_ref, dao_ref, o_ref, l_ref, xq_ref, w_ref, c_ref, s_ref, *rest):
        daq_ref, gq_ref, dkx_ref, dvx_ref, dkc_ref, dvc_ref = rest[-6:]
        i = pl.program_id(2)
        first = jnp.logical_and(jnp.logical_and(pl.program_id(0) == 0, pl.program_id(1) == 0), i == 0)
        gq = jnp.zeros((1, HD), F32)
        kx = kx_ref[...]
        kc = kc_ref[...]
        vx = vx_ref[...].astype(BF16)
        vc = vc_ref[...].astype(BF16)
        dkx = jnp.zeros((seq, HD), F32)
        dvx = jnp.zeros((seq, HD), F32)
        dkc = jnp.zeros((cx, HD), F32)
        dvc = jnp.zeros((cx, HD), F32)
        for r in range(rep):
            sl = slice(r * HD, (r + 1) * HD)
            q = q_ref[:, sl]
            lr = l_ref[:, r:r + 1]
            p1 = jnp.exp2(_dot(q, kx, NT) * SM_C - lr)
            p2 = jnp.exp2(_dot(q, kc, NT) * SM_C - lr)
            da = dao_ref[:, sl]
            da16 = da.astype(BF16)
            delta = jnp.sum(da * o_ref[:, sl], axis=-1, keepdims=True)
            ds1 = (p1 * (_dot(da16, vx, NT) - delta)).astype(BF16)
            ds2 = (p2 * (_dot(da16, vc, NT) - delta)).astype(BF16)
            dq = (_dot(ds1, kx) + _dot(ds2, kc)) * scale
            dkx += _dot(ds1, q, TN)
            dkc += _dot(ds2, q, TN)
            dvx += _dot(p1.astype(BF16), da16, TN)
            dvc += _dot(p2.astype(BF16), da16, TN)
            dt = dq * c_ref[...] + _swap_pairs(dq * s_ref[...])
            xv = xq_ref[:, sl].astype(F32)
            rn = lax.rsqrt(jnp.mean(xv * xv, axis=-1, keepdims=True) + EPS)
            xh = xv * rn
            dxh = dt * w_ref[...]
            daq_ref[:, sl] = (rn * (dxh - xh * jnp.mean(dxh * xh, axis=-1, keepdims=True))).astype(BF16)
            gq += jnp.sum(dt * xh, axis=0, keepdims=True)
        dkx = dkx * scale
        dkc = dkc * scale

        @pl.when(first)
        def _():
            gq_ref[...] = gq

        @pl.when(jnp.logical_not(first))
        def _():
            gq_ref[...] += gq

        @pl.when(i == 0)
        def _():
            dkx_ref[...] = dkx
            dvx_ref[...] = dvx
            dkc_ref[...] = dkc
            dvc_ref[...] = dvc

        @pl.when(i > 0)
        def _():
            dkx_ref[...] += dkx
            dvx_ref[...] += dvx
            dkc_ref[...] += dkc
            dvc_ref[...] += dvc

    qblk = pl.BlockSpec((tq, gw), lambda b, g, i: (b * nq + i, g))
    kxb = pl.BlockSpec((None, seq, HD), lambda b, g, i: (b, 0, g))
    kcb = pl.BlockSpec((None, cx, HD), lambda b, g, i: (b, 0, g))
    table = pl.BlockSpec((tq, HD), lambda b, g, i: (i, 0))
    one = pl.BlockSpec((1, HD), lambda b, g, i: (0, 0))
    return pl.pallas_call(
        body, name="att_bwd", grid=(nb, HKV, nq),
        in_specs=[qblk,
                  pl.BlockSpec((seq, HD), lambda b, g, i: (b, g)),
                  pl.BlockSpec((cx, HD), lambda b, g, i: (b, g)),
                  pl.BlockSpec((seq, HD), lambda b, g, i: (b, AV // HD + g)),
                  pl.BlockSpec((cx, HD), lambda b, g, i: (t_rows // cx + b, AV // HD + g)),
                  qblk, qblk, pl.BlockSpec((tq, 128), lambda b, g, i: (b * nq + i, g)),
                  pl.BlockSpec((tq, gw), lambda b, g, i: (b * nq + i, AQ // gw + g)), one, table, table]
        + [pl.BlockSpec(memory_space=pl.ANY)] * len(after),
        out_specs=[qblk, one, kxb, kxb, kcb, kcb],
        out_shape=[_sds((t_rows, D), BF16), _sds((1, HD), F32), _sds((nb, seq, HKV * HD), F32),
                   _sds((nb, seq, HKV * HD), F32), _sds((nb, cx, HKV * HD), F32), _sds((nb, cx, HKV * HD), F32)],
        compiler_params=_params(("arbitrary", "arbitrary", "arbitrary")),
    )(q16, kx16, kc16, px, px, dao, o_att, lse, px, q_norm_w, cos, sin, *after)


def _qk_prep_bwd(dt, px, nw, cos, sin, rows, row_off, col_off, heads, hb, seq, tm, name):
    rope = cos is not None
    rb0 = row_off // tm
    pb = seq // tm if rope else 1
    bw = hb * HD

    def body(*refs):
        if rope:
            d_ref, x_ref, w_ref, c_ref, s_ref, dx_ref, dw_ref = refs
        else:
            d_ref, x_ref, w_ref, dx_ref, dw_ref = refs
        first = jnp.logical_and(pl.program_id(0) == 0, pl.program_id(1) == 0)
        dw = jnp.zeros((1, HD), F32)
        for h in range(hb):
            sl = slice(h * HD, (h + 1) * HD)
            dtv = d_ref[:, sl]
            if rope:
                dtv = dtv * c_ref[...] + _swap_pairs(dtv * s_ref[...])
            xv = x_ref[:, sl].astype(F32)
            r = lax.rsqrt(jnp.mean(xv * xv, axis=-1, keepdims=True) + EPS)
            xh = xv * r
            dxh = dtv * w_ref[...]
            dx_ref[:, sl] = (r * (dxh - xh * jnp.mean(dxh * xh, axis=-1, keepdims=True))).astype(BF16)
            dw += jnp.sum(dtv * xh, axis=0, keepdims=True)

        @pl.when(first)
        def _():
            dw_ref[...] = dw

        @pl.when(jnp.logical_not(first))
        def _():
            dw_ref[...] += dw

    blk = pl.BlockSpec((tm, bw), lambda i, j: (i, j))
    in_specs = [blk, pl.BlockSpec((tm, bw), lambda i, j: (rb0 + i, col_off // bw + j)),
                pl.BlockSpec((1, HD), lambda i, j: (0, 0))]
    args = [dt, px, nw]
    if rope:
        in_specs += [pl.BlockSpec((tm, HD), lambda i, j: (i % pb, 0))] * 2
        args += [cos, sin]
    return pl.pallas_call(
        body, name=name, grid=(rows // tm, heads // hb), in_specs=in_specs,
        out_specs=[blk, pl.BlockSpec((1, HD), lambda i, j: (0, 0))],
        out_shape=[_sds((rows, heads * HD), BF16), _sds((1, HD), F32)],
        compiler_params=_params(("arbitrary", "arbitrary")),
    )(*args)


def _ret_bwd(px, lg, do16, hist_f, hist_b, nb, nc, cx):
    t_rows = nb * nc * CH

    def body(lg_ref, *refs):
        ins = (refs[0:5], refs[7:12])
        do_refs = (refs[5], refs[12])
        h_refs = (refs[6], refs[13])
        ctx_refs = refs[14:17]
        outs = (refs[17:20], refs[20:23])
        dck_ref, dcv_ref, dlg_ref = refs[23:26]
        dss = (refs[26], refs[27])
        c = pl.program_id(1)

        @pl.when(c == 0)
        def _():
            dss[0][...] = jnp.zeros_like(dss[0])
            dss[1][...] = jnp.zeros_like(dss[1])
            dlg_ref[...] = jnp.zeros_like(dlg_ref)

        for d in range(2):
            dq_ref, dk_ref, dv_ref = outs[d]
            for h in range(RH):
                lg_d = lg_ref[d, h]
                mask, relf, qd, qe, kd, ke = _decays(lg_d, d == 0)
                g_ch = jnp.exp(lg_d * CH)
                q, k, v16 = _head_qkv(ins[d], h)
                q16 = q.astype(BF16)
                k16 = k.astype(BF16)
                do16v = do_refs[d][:, h * DV:(h + 1) * DV]
                st16 = h_refs[d][h]
                dst = dss[d][h]
                dst16 = dst.astype(BF16)
                a = _dot(q16, k16, NT) * mask
                dp = _dot(do16v, v16, NT)
                da16 = (dp * mask).astype(BF16)
                dq_cross = _dot(do16v, st16, NT) * qd
                dq_ref[:, h * DK:(h + 1) * DK] = (_dot(da16, k16) + dq_cross).astype(BF16)
                dk_state = _dot(v16, dst16, NT) * kd
                dk_ref[:, h * DK:(h + 1) * DK] = ((_dot(da16, q16, TN) + dk_state) * (DK ** -0.5)).astype(BF16)
                dv = _dot(a.astype(BF16), do16v, TN) + _dot((k * kd).astype(BF16), dst16)
                dv_ref[:, h * DV:(h + 1) * DV] = dv.astype(BF16)
                dlg = (jnp.sum(relf * a * dp)
                       + jnp.sum(qe * jnp.sum(q * dq_cross, axis=-1, keepdims=True))
                       + jnp.sum(ke * jnp.sum(k * dk_state, axis=-1, keepdims=True))
                       + CH * g_ch * jnp.sum(dst * st16.astype(F32)))
                row = d * RH + h
                dlg_ref[row:row + 1, :] += jnp.broadcast_to(dlg, (1, 128))
                dss[d][h] = g_ch * dst + _dot((q * qd).astype(BF16), do16v, TN)

        @pl.when(c == nc - 1)
        def _():
            pos = lax.broadcasted_iota(jnp.int32, (cx, 1), 0).astype(F32)
            for h in range(RH):
                k, v16 = _ctx_kv(ctx_refs, h)
                dk = jnp.zeros((cx, DK), F32)
                dv = jnp.zeros((cx, DV), F32)
                for d, e in enumerate((cx - 1.0 - pos, pos)):
                    w = jnp.exp(lg_ref[d, h] * e)
                    ds16 = dss[d][h].astype(BF16)
                    t = _dot(v16, ds16, NT)
                    dk += t * w
                    dv += _dot((k * w).astype(BF16), ds16)
                    dlg = jnp.sum(e * w * jnp.sum(k * t, axis=-1, keepdims=True))
                    row = d * RH + h
                    dlg_ref[row:row + 1, :] += jnp.broadcast_to(dlg, (1, 128))
                dck_ref[:, h * DK:(h + 1) * DK] = (dk * (DK ** -0.5)).astype(BF16)
                dcv_ref[:, h * DV:(h + 1) * DV] = dv.astype(BF16)

    def fw(b, c):
        return b * nc + nc - 1 - c

    def bw(b, c):
        return b * nc + c

    def rows(rowf, width):
        return pl.BlockSpec((CH, width), lambda b, c: (rowf(b, c), 0))

    def hist(rowf):
        return pl.BlockSpec((None, None, RH, DK, DV), lambda b, c: (b, rowf(0, c), 0, 0, 0))

    in_specs = [pl.BlockSpec(memory_space=pltpu.SMEM)]
    out_specs = []
    for rowf in (fw, bw):
        in_specs += _wide_specs(rowf) + [rows(rowf, RH * DV), hist(rowf)]
        out_specs += [rows(rowf, RH * DK), rows(rowf, RH * DK), rows(rowf, RH * DV)]
    in_specs += _ctx_specs(t_rows, cx)
    out_specs += [pl.BlockSpec((cx, RH * DK), lambda b, c: (b, 0)), pl.BlockSpec((cx, RH * DV), lambda b, c: (b, 0)),
                  pl.BlockSpec((None, 8, 128), lambda b, c: (b, 0, 0))]
    qk = _sds((t_rows, RH * DK), BF16)
    vv = _sds((t_rows, RH * DV), BF16)
    return pl.pallas_call(
        body, name="ret_bwd", grid=(nb, nc), in_specs=in_specs, out_specs=out_specs,
        out_shape=[qk, qk, vv, qk, qk, vv, _sds((nb * cx, RH * DK), BF16), _sds((nb * cx, RH * DV), BF16),
                   _sds((nb, 8, 128), F32)],
        scratch_shapes=[pltpu.VMEM((RH, DK, DV), F32), pltpu.VMEM((RH, DK, DV), F32)],
        compiler_params=_params(("parallel", "arbitrary")),
    )(lg, *([px] * 5), do16, hist_f, *([px] * 5), do16, hist_b, *([px] * 3))


def _assemble_lat(rows_all, dk_f, dk_b, dv_f, dv_b, dak16, dvx, dq_f, dq_b, drg16, daq16, dag16, dmr16, dma16, tm):
    t_rows = dk_f.shape[0]

    def body(dkf, dkb, dvf, dvb, dak, dav, dqf, dqb, drg, daq, dag, dmr, dma, o_ref):
        o_ref[:, RK:RK + RH * DK] = (dkf[...].astype(F32) + dkb[...].astype(F32)).astype(BF16)
        o_ref[:, RV:RV + RH * DV] = (dvf[...].astype(F32) + dvb[...].astype(F32)).astype(BF16)
        o_ref[:, AK:AK + HKV * HD] = dak[...]
        o_ref[:, AV:AV + HKV * HD] = dav[...].astype(BF16)
        o_ref[:, RQ:RQ + RH * DK] = (dqf[...].astype(F32) + dqb[...].astype(F32)).astype(BF16)
        o_ref[:, RG:RG + RH * DV] = drg[...]
        o_ref[:, AQ:AQ + D] = daq[...]
        o_ref[:, AG:AG + D] = dag[...]
        o_ref[:, MR:MR + D] = dmr[...]
        o_ref[:, MA:MA + D] = dma[...]

    args = (dk_f, dk_b, dv_f, dv_b, dak16, dvx, dq_f, dq_b, drg16, daq16, dag16, dmr16, dma16)
    return pl.pallas_call(
        body, name="assemble_lat", grid=(t_rows // tm,),
        in_specs=[pl.BlockSpec((tm, a.shape[1]), lambda i: (i, 0)) for a in args],
        out_specs=pl.BlockSpec((tm, IN_COLS), lambda i: (i, 0)), out_shape=_sds((rows_all, IN_COLS), BF16),
        compiler_params=_params(("parallel",)),
    )(*args)


def _assemble_ctx(dp_all, dck16, dcv16, dcak16, dvc, t_rows, tm):
    c_rows = dck16.shape[0]
    rb = t_rows // tm

    def body(_, dck, dcv, dcak, dcav, o_ref):
        o_ref[:, RK:RK + RH * DK] = dck[...]
        o_ref[:, RV:RV + RH * DV] = dcv[...]
        o_ref[:, AK:AK + HKV * HD] = dcak[...]
        o_ref[:, AV:AV + HKV * HD] = dcav[...].astype(BF16)
        o_ref[:, KV_COLS:] = jnp.zeros((tm, IN_COLS - KV_COLS), BF16)

    args = (dck16, dcv16, dcak16, dvc)
    return pl.pallas_call(
        body, name="assemble_ctx", grid=(c_rows // tm,),
        in_specs=[pl.BlockSpec(memory_space=pl.ANY)]
        + [pl.BlockSpec((tm, a.shape[1]), lambda i: (i, 0)) for a in args],
        out_specs=pl.BlockSpec((tm, IN_COLS), lambda i: (rb + i, 0)), out_shape=_sds(dp_all.shape, BF16),
        input_output_aliases={0: 0},
        compiler_params=_params(("parallel",)),
    )(dp_all, *args)


def _norm_bwd(dh, x2, mod3, norm_w, dxn, row_off, rows_per_group, group0, tm, name):
    with_dx = dxn is not None
    rows = x2.shape[0]
    rb0 = row_off // tm
    bpg = rows_per_group // tm
    ngroups = rows // rows_per_group

    def body(*refs):
        if with_dx:
            dh_ref, x_ref, sc_ref, nw_ref, dxn_ref, dx_ref, dsh_ref, dsc_ref, dnw_ref = refs
        else:
            dh_ref, x_ref, sc_ref, nw_ref, dsh_ref, dsc_ref, dnw_ref = refs
        i = pl.program_id(0)
        dhv = dh_ref[...]
        xv = x_ref[...]
        nw = nw_ref[...]
        r = lax.rsqrt(jnp.mean(xv * xv, axis=-1, keepdims=True) + EPS)
        xh = xv * r
        dm = dhv * (1.0 + sc_ref[...])
        dsh = jnp.sum(dhv, axis=0, keepdims=True)
        dsc = jnp.sum(dhv * (xh * nw), axis=0, keepdims=True)
        dnw = jnp.sum(dm * xh, axis=0, keepdims=True)
        if with_dx:
            dxh = dm * nw
            dx_ref[...] = dxn_ref[...] + r * (dxh - xh * jnp.mean(dxh * xh, axis=-1, keepdims=True))

        @pl.when(i % bpg == 0)
        def _():
            dsh_ref[...] = dsh
            dsc_ref[...] = dsc

        @pl.when(i % bpg != 0)
        def _():
            dsh_ref[...] += dsh
            dsc_ref[...] += dsc

        @pl.when(i == 0)
        def _():
            dnw_ref[...] = dnw

        @pl.when(i > 0)
        def _():
            dnw_ref[...] += dnw

    grp = pl.BlockSpec((None, 1, D), lambda i: (i // bpg, 0, 0))
    in_specs = [pl.BlockSpec((tm, D), lambda i: (rb0 + i, 0)), pl.BlockSpec((tm, D), lambda i: (i, 0)),
                pl.BlockSpec((None, 1, D), lambda i: (group0 + i // bpg, 0, 1)),
                pl.BlockSpec((1, D), lambda i: (0, 0))]
    args = [dh, x2, mod3, norm_w]
    out_specs = [grp, grp, pl.BlockSpec((1, D), lambda i: (0, 0))]
    out_shape = [_sds((ngroups, 1, D), F32), _sds((ngroups, 1, D), F32), _sds((1, D), F32)]
    if with_dx:
        in_specs.append(pl.BlockSpec((tm, D), lambda i: (i, 0)))
        args.append(dxn)
        out_specs.insert(0, pl.BlockSpec((tm, D), lambda i: (i, 0)))
        out_shape.insert(0, _sds((rows, D), F32))
    return pl.pallas_call(
        body, name=name, grid=(rows // tm,), in_specs=in_specs, out_specs=out_specs, out_shape=out_shape,
        compiler_params=_params(("arbitrary",)),
    )(*args)


def _small_final(dmod_all, dmodc_parts, c_rows, dm_loc_rows, nw_parts, misc_parts, c_ctx, r_pad, w_ada16):
    loc = dm_loc_rows.shape[1]

    def body(dm_ref, dmc_ref, c_ref, dml_ref, nwp_ref, mp_ref, cc_ref, r_ref, w_ref,
             gb_ref, gc_ref, gnw_ref, misc_ref, gwa_ref):
        dmc = jnp.sum(dmc_ref[...], axis=0, keepdims=True)
        gb_ref[...] = jnp.sum(dm_ref[...], axis=0, keepdims=True) + dmc
        dsc = _dot(jnp.broadcast_to(dmc, (8, 3 * D)).astype(BF16), w_ref[...], NT)[0:1, :]
        gc_ref[...] = dsc * _dsilu(cc_ref[...])
        gnw_ref[...] = jnp.sum(nwp_ref[...], axis=0, keepdims=True)
        misc = jnp.sum(mp_ref[...], axis=0, keepdims=True)
        y = jnp.exp2(r_ref[...])
        lane = lax.broadcasted_iota(jnp.int32, (1, D), 1)
        is_decay = jnp.logical_and(lane >= 2 * HD, lane < 2 * HD + 2 * RH)
        misc_ref[...] = misc * jnp.where(is_decay, -(y * np.float32(np.log(2.0))) / (1.0 - y), 1.0)
        gwa_ref[...] = _dot(_silu(c_ref[...]).astype(BF16), dml_ref[...].astype(BF16), TN)

    return pl.pallas_call(
        body, name="small_final",
        out_shape=[_sds((1, 3 * D), F32), _sds((1, D), F32), _sds((1, D), F32), _sds((1, D), F32), _sds((D, loc), F32)],
        compiler_params=pltpu.CompilerParams(vmem_limit_bytes=VMEM_LIMIT),
    )(dmod_all, dmodc_parts, c_rows, dm_loc_rows, nw_parts, misc_parts, c_ctx, r_pad, w_ada16)


def _adamw_math(w, g, m, v):
    nm = B1 * m + (1.0 - B1) * g
    nv = B2 * v + (1.0 - B2) * (g * g)
    return -LR * ((nm / (1.0 - B1 ** STEP)) / (jnp.sqrt(nv / (1.0 - B2 ** STEP)) + ADAM_EPS) + WD * w), nm, nv


def _adamw(w, g, m, v, name):
    rows, cols = w.shape
    tm = _pick(rows, 448, 8)

    def body(w_ref, g_ref, m_ref, v_ref, d_ref, nm_ref, nv_ref):
        d_ref[...], nm_ref[...], nv_ref[...] = _adamw_math(w_ref[...], g_ref[...], m_ref[...], v_ref[...])

    blk = pl.BlockSpec((tm, cols), lambda i: (i, 0))
    return pl.pallas_call(
        body, name=name, grid=(rows // tm,), in_specs=[blk] * 4, out_specs=[blk] * 3,
        out_shape=[_sds((rows, cols), F32)] * 3, compiler_params=_params(("parallel",)),
    )(w, g, m, v)


def _mesh_pos():
    return lax.axis_index("x"), lax.axis_index("y"), lax.axis_index("c")


def _all_gather(arrs, name):
    n = len(arrs)

    def body(*refs):
        ins, outs = refs[:n], refs[n:2 * n]
        send_sems, recv_sems, local_sems = refs[2 * n:]
        x, y, c = _mesh_pos()
        me, sib = (x, y, c), (x, y, 1 - c)
        chips = [(1 - x, y), (x, 1 - y), (1 - x, 1 - y)]

        def slot(p):
            return 4 * p[0] + 2 * p[1] + p[2]

        def copy(a, k, block, to, own):
            dst = outs[a].at[slot(block)]
            return pltpu.make_async_remote_copy(
                src_ref=ins[a] if own else dst, dst_ref=dst, send_sem=send_sems.at[a, k], recv_sem=recv_sems.at[a, k],
                device_id=to, device_id_type=MESH_T)

        mine = [pltpu.make_async_copy(ins[a], outs[a].at[slot(me)], local_sems.at[a]) for a in range(n)]
        for cp in mine:
            cp.start()
        first = []
        for a in range(n):
            first.append(copy(a, 0, me, sib, True))
            first += [copy(a, 1 + j, me, (*chip, c), True) for j, chip in enumerate(chips)]
        for cp in first:
            cp.start()
        passed = []
        for j, chip in enumerate(chips):
            for a in range(n):
                copy(a, 1 + j, (*chip, c), me, False).wait_recv()
                fwd = copy(a, 4 + j, (*chip, c), sib, False)
                fwd.start()
                passed.append(fwd)
        for a in range(n):
            copy(a, 0, sib, me, False).wait_recv()
            for j, chip in enumerate(chips):
                copy(a, 4 + j, (*chip, 1 - c), me, False).wait_recv()
        for cp in first + passed:
            cp.wait_send()
        for cp in mine:
            cp.wait()

    hbm = pl.BlockSpec(memory_space=pl.ANY)
    return pl.pallas_call(
        body, name=name, in_specs=[hbm] * n, out_specs=[hbm] * n,
        out_shape=[_sds((N_DEV,) + a.shape, a.dtype) for a in arrs],
        scratch_shapes=[pltpu.SemaphoreType.DMA((n, 7)), pltpu.SemaphoreType.DMA((n, 7)), pltpu.SemaphoreType.DMA((n,))],
    )(*arrs)


def _pair_exchange(arrs, name):
    n = len(arrs)

    def body(*refs):
        ins, outs = refs[:n], refs[n:2 * n]
        send_sems, recv_sems = refs[2 * n:]
        x, y, c = _mesh_pos()
        sib = (x, y, 1 - c)
        sends = []
        for a in range(n):
            for k in range(4):
                sends.append(pltpu.make_async_remote_copy(
                    src_ref=ins[a].at[2 * k + 1 - c], dst_ref=outs[a].at[k], send_sem=send_sems.at[a, k],
                    recv_sem=recv_sems.at[a, k], device_id=sib, device_id_type=MESH_T))
        for cp in sends:
            cp.start()
        for cp in sends:
            cp.wait_recv()
        for cp in sends:
            cp.wait_send()

    hbm = pl.BlockSpec(memory_space=pl.ANY)
    return pl.pallas_call(
        body, name=name, in_specs=[hbm] * n, out_specs=[hbm] * n,
        out_shape=[_sds((4,) + a.shape[1:], a.dtype) for a in arrs],
        scratch_shapes=[pltpu.SemaphoreType.DMA((n, 4)), pltpu.SemaphoreType.DMA((n, 4))],
    )(*arrs)


def _pair_add(part, got, core, name):
    _, rows, cols = part.shape
    tm = _pick(rows, 672, 16)
    p4 = part.reshape(4, 2, rows, cols)

    def body(core_ref, p_ref, g_ref, o_ref):
        o_ref[...] = (p_ref[...].astype(F32) + g_ref[...].astype(F32)).astype(BF16)

    blk = pl.BlockSpec((None, tm, cols), lambda k, i, cr: (k, i, 0))
    return pl.pallas_call(
        body, name=name,
        grid_spec=pltpu.PrefetchScalarGridSpec(
            num_scalar_prefetch=1, grid=(4, rows // tm),
            in_specs=[pl.BlockSpec((None, None, tm, cols), lambda k, i, cr: (k, cr[0], i, 0)), blk], out_specs=blk),
        out_shape=_sds((4, rows, cols), BF16), compiler_params=_params(("parallel", "parallel")),
    )(core, p4, got)


def _chip_sum_adamw(pair_sums, landed, chip, w, m, v, name):
    _, rows, cols = pair_sums.shape
    tm = _pick(rows, 448, 16)

    def body(chip_ref, s_ref, l_ref, w_ref, m_ref, v_ref, g_ref, d_ref, nm_ref, nv_ref):
        acc = s_ref[...].astype(F32)
        for j in range(3):
            acc = acc + l_ref[j].astype(F32)
        g_ref[...] = acc
        d_ref[...], nm_ref[...], nv_ref[...] = _adamw_math(w_ref[...], acc, m_ref[...], v_ref[...])

    blk = pl.BlockSpec((tm, cols), lambda i, ch: (i, 0))
    return pl.pallas_call(
        body, name=name,
        grid_spec=pltpu.PrefetchScalarGridSpec(
            num_scalar_prefetch=1, grid=(rows // tm,),
            in_specs=[pl.BlockSpec((None, tm, cols), lambda i, ch: (ch[0], i, 0)),
                      pl.BlockSpec((3, tm, cols), lambda i, ch: (0, i, 0)), blk, blk, blk],
            out_specs=[blk] * 4),
        out_shape=[_sds((rows, cols), F32)] * 4, compiler_params=_params(("parallel",)),
    )(chip, pair_sums, landed, w, m, v)


_HBM = pl.BlockSpec(memory_space=pltpu.HBM)
_SEM = pl.BlockSpec(memory_space=pltpu.SEMAPHORE)
_EFFECT = pltpu.SideEffectType.DATAFLOW_SIDE_EFFECTING


def _chip_routes(n):
    def plan(x, y, c):
        routes = []
        for a in range(n):
            for j in range(1, 4):
                px, py = x ^ (j >> 1), y ^ (j & 1)
                routes.append((a, 2 * px + py, (px, py, c), j - 1))
        return routes
    return plan, 3 * n


def _bcast_routes(n):
    def plan(x, y, c):
        routes = []
        for a in range(n):
            for k in range(1, N_DEV):
                peer = (x ^ ((k >> 2) & 1), y ^ ((k >> 1) & 1), c ^ (k & 1))
                routes.append((a, 0, peer, 4 * x + 2 * y + c))
        return routes
    return plan, 7 * n


def _route_copies(srcs, lands, send_sems, recv_sems, routes):
    return [pltpu.make_async_remote_copy(
        src_ref=srcs[a].at[sb], dst_ref=lands[a].at[lb], send_sem=send_sems.at[r], recv_sem=recv_sems.at[r],
        device_id=peer, device_id_type=MESH_T) for r, (a, sb, peer, lb) in enumerate(routes)]


def _exchange_start(srcs, lands, routes, name, after=()):
    plan, count = routes
    n = len(srcs)
    n_in = 2 * n + len(after)

    def body(*refs):
        send_sems, recv_sems = refs[n_in], refs[n_in + 1]
        token = refs[-1]
        for cp in _route_copies(refs[:n], refs[n:2 * n], send_sems, recv_sems, plan(*_mesh_pos())):
            cp.start()
        token[...] = jnp.zeros_like(token)

    args = [pltpu.with_memory_space_constraint(a, pltpu.HBM) for a in list(srcs) + list(lands)]
    out = pl.pallas_call(
        body, name=name,
        out_shape=(pltpu.SemaphoreType.DMA((count,)), pltpu.SemaphoreType.DMA((count,)),
                   *[pltpu.HBM(a.shape, a.dtype) for a in args], _sds((8, 128), F32)),
        in_specs=[_HBM] * (2 * n) + [pl.BlockSpec(memory_space=pl.ANY)] * len(after),
        out_specs=(_SEM, _SEM, *([_HBM] * (2 * n)), pl.BlockSpec(memory_space=pltpu.VMEM)),
        input_output_aliases={i: 2 + i for i in range(2 * n)},
        compiler_params=pltpu.CompilerParams(has_side_effects=_EFFECT),
    )(*args, *after)
    return (out[0], out[1], list(out[2:2 + 2 * n]), routes), out[-1]


def _exchange_wait(state, after, name):
    send_sems, recv_sems, bufs, (plan, count) = state
    n = len(bufs) // 2

    def body(*refs):
        send_s, recv_s = refs[2 * n], refs[2 * n + 1]
        for cp in _route_copies(refs[:n], refs[n:2 * n], send_s, recv_s, plan(*_mesh_pos())):
            cp.wait_send()
            cp.wait_recv()

    out = pl.pallas_call(
        body, name=name, out_shape=tuple(pltpu.HBM(a.shape, a.dtype) for a in bufs),
        in_specs=[_HBM] * (2 * n) + [_SEM, _SEM, pl.BlockSpec(memory_space=pl.ANY)], out_specs=tuple([_HBM] * (2 * n)),
        input_output_aliases={i: i for i in range(2 * n)},
        compiler_params=pltpu.CompilerParams(has_side_effects=_EFFECT),
    )(*bufs, send_sems, recv_sems, after)
    return list(out[:n]), list(out[n:])


def _group_routes(js):
    def plan(x, y, c):
        return [(0, 0, (x ^ (j >> 1), y ^ (j & 1), c), 2 * j + c) for j in js]
    return plan, len(js)


def _pair_fill(groups, js, name, after=()):
    def body(*refs):
        g_ref, send_sems, recv_sems = refs[-3:]
        x, y, c = _mesh_pos()
        sends = []
        for n, j in enumerate(js):
            mine = g_ref.at[2 * j + c]
            sends.append(pltpu.make_async_remote_copy(
                src_ref=mine, dst_ref=mine, send_sem=send_sems.at[n], recv_sem=recv_sems.at[n],
                device_id=(x, y, 1 - c), device_id_type=MESH_T))
        for cp in sends:
            cp.start()
        for n, j in enumerate(js):
            pltpu.make_async_remote_copy(
                src_ref=g_ref.at[2 * j + c], dst_ref=g_ref.at[2 * j + 1 - c], send_sem=send_sems.at[n],
                recv_sem=recv_sems.at[n], device_id=(x, y, 1 - c), device_id_type=MESH_T).wait_recv()
        for cp in sends:
            cp.wait_send()

    hbm = pl.BlockSpec(memory_space=pl.ANY)
    return pl.pallas_call(
        body, name=name, in_specs=[hbm] * (1 + len(after)), out_specs=hbm, out_shape=_sds(groups.shape, groups.dtype),
        input_output_aliases={0: 0},
        scratch_shapes=[pltpu.SemaphoreType.DMA((len(js),)), pltpu.SemaphoreType.DMA((len(js),))],
    )(groups, *after)


def _in_proj_group(h_all, groups, j0, ng, chip, px_prev, after, name):
    rows_all = h_all.shape[0]
    gcols = IN_COLS // 4
    tm = _pick(rows_all, 1536, 128)
    g4 = groups.reshape(4, gcols, D)

    n_lead = (1 if px_prev is not None else 0) + len(after)
    lead = ([px_prev] if px_prev is not None else []) + list(after)

    def body(chip_ref, *refs):
        h_ref, w_ref, o_ref = refs[n_lead:]
        o_ref[...] = _dot(h_ref[...], w_ref[...], NT).astype(BF16)

    return pl.pallas_call(
        body, name=name,
        grid_spec=pltpu.PrefetchScalarGridSpec(
            num_scalar_prefetch=1, grid=(ng, rows_all // tm),
            in_specs=[pl.BlockSpec(memory_space=pl.ANY)] * n_lead
            + [pl.BlockSpec((tm, D), lambda n, i, ch: (i, 0)),
               pl.BlockSpec((None, gcols, D), lambda n, i, ch: (j0 + n, 0, 0))],
            out_specs=pl.BlockSpec((tm, gcols), lambda n, i, ch: (i, ch[0] ^ (j0 + n)))),
        out_shape=_sds((rows_all, IN_COLS), BF16),
        input_output_aliases={1: 0} if px_prev is not None else {},
        compiler_params=_params(("parallel", "parallel")),
    )(chip, *lead, h_all, g4)


def _d_h_groups(dp_all, groups, chip, after):
    rows_all = dp_all.shape[0]
    gcols = IN_COLS // 4
    tm = _pick(rows_all, 1536, 128)
    g4 = groups.reshape(4, gcols, D)
    n_lead = len(after)

    def body(chip_ref, *refs):
        a_ref, w_ref, o_ref = refs[n_lead:]
        j = pl.program_id(1)
        part = _dot(a_ref[...], w_ref[...])

        @pl.when(j == 0)
        def _():
            o_ref[...] = part

        @pl.when(j > 0)
        def _():
            o_ref[...] += part

    return pl.pallas_call(
        body, name="d_h",
        grid_spec=pltpu.PrefetchScalarGridSpec(
            num_scalar_prefetch=1, grid=(rows_all // tm, 4),
            in_specs=[pl.BlockSpec(memory_space=pl.ANY)] * n_lead
            + [pl.BlockSpec((tm, gcols), lambda i, j, ch: (i, ch[0] ^ j)),
               pl.BlockSpec((None, gcols, D), lambda i, j, ch: (j, 0, 0))],
            out_specs=pl.BlockSpec((tm, D), lambda i, j, ch: (i, 0))),
        out_shape=_sds((rows_all, D), F32),
        compiler_params=_params(("parallel", "arbitrary")),
    )(chip, *after, dp_all, g4)


def _reduce_scatter_start(parts, core, name):
    got = _pair_exchange(parts, name + "_pair")
    sums = [_pair_add(p, g, core, "%s_add_%d" % (name, i)) for i, (p, g) in enumerate(zip(parts, got))]
    lands = [lax.empty((3,) + s_.shape[1:], BF16) for s_ in sums]
    return _exchange_start(sums, lands, _chip_routes(len(sums)), name + "_start")


def _reduce_scatter_finish(rs_state, after, chip, wmv, name):
    sums, landed = _exchange_wait(rs_state, after, name + "_wait")
    return [_chip_sum_adamw(s_, l_, chip, *t, "%s_adamw_%d" % (name, i))
            for i, (s_, l_, t) in enumerate(zip(sums, landed, wmv))]


def _local_step(x, c, ctx, norm_w, ret_log2_decay, q_norm_w, k_norm_w, loss_target,
                mod, proj_in, proj_back, get_w_o, on_out_grads, on_in_grad, started=()):
    nb, seq, _ = x.shape
    cx = ctx.shape[1]
    t_rows, c_rows = nb * seq, nb * cx
    rows_all = t_rows + c_rows
    nc = seq // CH
    tm = _pick(seq, 256, 128)
    te = _pick(seq, 512, 128)
    assert cx % tm == 0 and t_rows % cx == 0 and seq % GRID_W == 0

    x2 = x.reshape(t_rows, D)
    ctx2 = ctx.reshape(c_rows, D)
    tgt = loss_target.reshape(t_rows, D)
    lg = _log_gamma(ret_log2_decay)
    cos, sin = _rope_tables(seq)

    mod3 = mod[:, None, :]
    h_all = _norm_fwd(x2, mod3, norm_w, rows_all, 0, seq, 0, None, te, "norm_fwd", after=started)
    h_all = _norm_fwd(ctx2, mod3, norm_w, rows_all, t_rows, c_rows, nb, h_all, tm, "norm_fwd_ctx")
    px = proj_in(h_all)
    o_f, o_b, hist_f, hist_b = _ret_fwd(px, lg, nb, nc, cx)
    q16 = _qk_prep(px, q_norm_w, cos, sin, t_rows, 0, AQ, HQ, 4, seq, te, "q_prep")
    kx16 = _qk_prep(px, k_norm_w, cos, sin, t_rows, 0, AK, HKV, HKV, seq, te, "k_prep")
    kc16 = _qk_prep(px, k_norm_w, None, None, c_rows, t_rows, AK, HKV, HKV, seq, tm, "kc_prep")
    o_att, yatt16, lse = _att_fwd(q16, kx16, kc16, px, nb, seq, cx, te)
    w_o_ret16, w_o_att16, w_out16 = get_w_o(lse)
    yret16, a_ret, a_att, y16, dxn, dout16, dgate, loss_b = _merge_out(
        o_f, o_b, yatt16, px, w_o_ret16, w_o_att16, w_out16, x2, tgt, mod3, nb, seq, tm)

    gw_out = _matmul(y16, dout16, ta=True, tm=D, tn=D, tk=D, out_dtype=BF16, name="gw_out")
    da_ret16, da_att16, dmr16, dma16, do16, drg16, dao, dag16 = _bwd_branches(
        dout16, w_out16, w_o_ret16, w_o_att16, px, a_ret, a_att, o_f, o_b, o_att, tm)
    gw_o_ret = _matmul(yret16, da_ret16, ta=True, tm=D, tn=D, tk=D, out_dtype=BF16, name="gw_o_ret")
    gw_o_att = _matmul(yatt16, da_att16, ta=True, tm=D, tn=D, tk=D, out_dtype=BF16, name="gw_o_att")
    out_state, out_started = on_out_grads([gw_o_ret, gw_o_att, gw_out])
    daq16, gq, dkx, dvx, dkc, dvc = _att_bwd(q16, kx16, kc16, px, dao, o_att, lse, q_norm_w, cos, sin, nb, seq, cx, te,
                                             after=out_started)
    dak16, gk_lat = _qk_prep_bwd(dkx.reshape(t_rows, HKV * HD), px, k_norm_w, cos, sin, t_rows, 0, AK, HKV, HKV, seq, te,
                                 "k_prep_bwd")
    dcak16, gk_ctx = _qk_prep_bwd(dkc.reshape(c_rows, HKV * HD), px, k_norm_w, None, None, c_rows, t_rows, AK, HKV, HKV,
                                  seq, tm, "kc_prep_bwd")
    dq_f, dk_f, dv_f, dq_b, dk_b, dv_b, dck16, dcv16, dlg_scan = _ret_bwd(px, lg, do16, hist_f, hist_b, nb, nc, cx)
    dp_all = _assemble_lat(rows_all, dk_f, dk_b, dv_f, dv_b, dak16, dvx.reshape(t_rows, HKV * HD), dq_f, dq_b, drg16,
                           daq16, dag16, dmr16, dma16, tm)
    dp_all = _assemble_ctx(dp_all, dck16, dcv16, dcak16, dvc.reshape(c_rows, HKV * HD), t_rows, tm)
    gw_in_t = _matmul(dp_all, h_all, ta=True, tm=1536, tn=D, tk=2304, out_dtype=BF16, name="gw_in")
    in_state, in_started = on_in_grad(gw_in_t)
    dh = proj_back(dp_all, in_started)
    grad_x, dsh, dsc, gnw_lat = _norm_bwd(dh, x2, mod3, norm_w, dxn, 0, seq, 0, te, "norm_bwd")
    dsh_c, dsc_c, gnw_ctx = _norm_bwd(dh, ctx2, mod3, norm_w, None, t_rows, c_rows, nb, tm, "norm_bwd_ctx")

    dlg = jnp.sum(dlg_scan[:, :, 0], axis=0).reshape(1, 2 * RH)
    misc = jnp.concatenate([gq, gk_lat + gk_ctx, dlg, jnp.sum(loss_b[:, 0, 0]).reshape(1, 1),
                            jnp.zeros((1, D - 2 * HD - 2 * RH - 1), F32)], axis=1)
    rows = []
    for b in range(nb):
        rows += [dsh[b], dsc[b], dgate[b]]
    rows += [dsh_c[0], dsc_c[0]] + [c[b:b + 1] for b in range(nb)] + [gnw_lat + gnw_ctx, misc]
    payload = jnp.concatenate(rows + [jnp.zeros((PAY_ROWS - len(rows), D), F32)], axis=0)
    return grad_x.reshape(nb, seq, D), out_state, in_state, payload


def _finish_small(gathered, nb, c_ctx, ret_log2_decay, w_ada16, dev):
    n_dev = gathered.shape[0]
    loc = 3 * D // n_dev
    dmod_all = gathered[:, :3 * nb].reshape(n_dev * nb, 3 * D)
    dmodc_parts = jnp.concatenate([gathered[:, 3 * nb:3 * nb + 2].reshape(n_dev, 2 * D), jnp.zeros((n_dev, D), F32)], axis=1)
    c_all = gathered[:, 3 * nb + 2:4 * nb + 2].reshape(n_dev * nb, D)
    nw_parts = gathered[:, 4 * nb + 2]
    misc_parts = gathered[:, 4 * nb + 3]
    n_rows = n_dev * nb + n_dev
    pad = (-n_rows) % 16
    c_rows = jnp.concatenate([c_all, jnp.broadcast_to(c_ctx.reshape(1, D), (n_dev, D)), jnp.zeros((pad, D), F32)], axis=0)
    dm_rows = jnp.concatenate([dmod_all, dmodc_parts, jnp.zeros((pad, 3 * D), F32)], axis=0)
    dm_loc_rows = lax.dynamic_slice_in_dim(dm_rows, dev * loc, loc, axis=1)
    r_pad = jnp.full((1, D), -1.0, F32).at[:, 2 * HD:2 * HD + 2 * RH].set(ret_log2_decay.reshape(1, 2 * RH))
    gb, gc, gnw, misc, gwa = _small_final(dmod_all, dmodc_parts, c_rows, dm_loc_rows, nw_parts, misc_parts,
                                          c_ctx.reshape(1, D), r_pad, w_ada16)
    return (gb, gc, gnw, misc[:, :HD], misc[:, HD:2 * HD], misc[:, 2 * HD:2 * HD + 2 * RH], gwa,
            misc[0, 2 * HD + 2 * RH])


def kernel(x, c, ctx, c_ctx, norm_w, w_ada, b_ada, w_in, ret_log2_decay, q_norm_w, k_norm_w, w_o_ret, w_o_att, w_out, loss_target, m_c_ctx, m_norm_w, m_w_ada, m_b_ada, m_w_in, m_ret_log2_decay, m_q_norm_w, m_k_norm_w, m_w_o_ret, m_w_o_att, m_w_out, v_c_ctx, v_norm_w, v_w_ada, v_b_ada, v_w_in, v_ret_log2_decay, v_q_norm_w, v_k_norm_w, v_w_o_ret, v_w_o_att, v_w_out):
    nb = x.shape[0]
    mx, my, mc = _mesh_pos()
    dev = 4 * mx + 2 * my + mc
    core = jnp.reshape(mc, (1,)).astype(jnp.int32)
    chip = jnp.reshape(2 * mx + my, (1,)).astype(jnp.int32)

    n_loc = 3 * D // N_DEV
    c8 = jnp.zeros((8, D), F32).at[:nb].set(c).at[nb].set(c_ctx)
    c_land = lax.dynamic_update_slice(lax.empty((N_DEV, 8, D), F32), c8[None], (dev, 0, 0))
    c_state, c_token = _exchange_start([c8[None]], [c_land], _bcast_routes(1), "gather_c_start")
    w_in_t = jnp.transpose(w_in[0])
    in_shard = w_in_t.astype(BF16)
    groups = lax.dynamic_update_slice(lax.empty((N_DEV,) + in_shard.shape, BF16), in_shard[None], (mc, 0, 0))
    groups = _pair_fill(groups, (0,), "gather_in_pair", after=(c_token,))
    _, (c_all,) = _exchange_wait(c_state, groups, "gather_c_wait")
    ada_shard = w_ada[0].astype(BF16)
    b_loc = lax.dynamic_slice(b_ada, (0, dev * n_loc), (1, n_loc))
    mod_cols = _mod_part(c_all.reshape(N_DEV * 8, D), ada_shard, b_loc)
    (mod_all,) = _all_gather([mod_cols], "gather_mod")
    mod = jnp.transpose(lax.dynamic_slice(mod_all, (0, dev * 8, 0), (N_DEV, 8, n_loc)), (1, 0, 2)).reshape(8, 3 * D)
    ada_land = lax.dynamic_update_slice(lax.empty((N_DEV,) + ada_shard.shape, BF16), ada_shard[None], (dev, 0, 0))

    (near_send, near_recv, near_bufs, near_routes), gin_token = _exchange_start(
        [in_shard[None]], [groups], _group_routes((1, 2)), "gather_in_start", after=(mod_all,))
    w_in_groups, wo_states, ada_states = [], [], []
    wo_shards = [w_[0].astype(BF16) for w_ in (w_o_ret, w_o_att, w_out)]
    wo_lands = [lax.dynamic_update_slice(lax.empty((N_DEV,) + s_.shape, BF16), s_[None], (dev, 0, 0)) for s_ in wo_shards]

    def _state(send, recv, src, groups, routes):
        return send, recv, [src, groups], routes

    def proj_in(h_all):
        src, groups = near_bufs
        px = _in_proj_group(h_all, groups, 0, 1, chip, None, (gin_token,), "in_proj_0")
        (src,), (groups,) = _exchange_wait(_state(near_send, near_recv, src, groups, near_routes), px,
                                           "gather_in_wait_near")
        groups = _pair_fill(groups, (1, 2), "gather_in_fill_near")
        (far_send, far_recv, (src, groups), far_routes), far_token = _exchange_start(
            [src], [groups], _group_routes((3,)), "gather_in_start_far")
        wo_state, wo_token = _exchange_start([s_[None] for s_ in wo_shards], wo_lands, _bcast_routes(3),
                                             "gather_wo_start", after=(far_token,))
        wo_states.append(wo_state)
        ada_state, ada_token = _exchange_start([ada_shard[None]], [ada_land], _bcast_routes(1), "gather_ada_start",
                                               after=(wo_token,))
        ada_states.append(ada_state)
        px = _in_proj_group(h_all, groups, 1, 2, chip, px, (ada_token,), "in_proj_near")
        (src,), (groups,) = _exchange_wait(_state(far_send, far_recv, src, groups, far_routes), px,
                                           "gather_in_wait_far")
        groups = _pair_fill(groups, (3,), "gather_in_fill_far")
        px = _in_proj_group(h_all, groups, 3, 1, chip, px, (), "in_proj_far")
        w_in_groups.append(groups)
        return px

    def proj_back(dp_all, after):
        return _d_h_groups(dp_all, w_in_groups[0], chip, after)

    def get_w_o(after):
        _, (l_ret, l_att, l_out) = _exchange_wait(wo_states[0], after, "gather_wo_wait")
        return l_ret.reshape(RH * DV, D), l_att.reshape(D, D), l_out.reshape(D, D)

    def on_out_grads(grads):
        parts = [g_.reshape(N_DEV, g_.shape[0] // N_DEV, D) for g_ in grads]
        state, token = _reduce_scatter_start(parts, core, "rs_out")
        return state, (token,)

    def on_in_grad(grad):
        state, token = _reduce_scatter_start([grad.reshape(N_DEV, IN_COLS // N_DEV, D)], core, "rs_in")
        return state, (token,)

    grad_x, out_state, in_state, payload = _local_step(
        x, c, ctx, norm_w, ret_log2_decay, q_norm_w, k_norm_w, loss_target,
        mod, proj_in, proj_back, get_w_o, on_out_grads, on_in_grad, started=(gin_token,))

    pay_land = lax.dynamic_update_slice(lax.empty((N_DEV,) + payload.shape, F32), payload[None], (dev, 0, 0))
    pay_state, pay_token = _exchange_start([payload[None]], [pay_land], _bcast_routes(1), "gather_small_start")

    out_res = _reduce_scatter_finish(out_state, pay_token, chip,
                                     [(w_[0], m_[0], v_[0]) for w_, m_, v_ in ((w_o_ret, m_w_o_ret, v_w_o_ret),
                                                                                (w_o_att, m_w_o_att, v_w_o_att),
                                                                                (w_out, m_w_out, v_w_out))], "rs_out")
    (in_res,) = _reduce_scatter_finish(in_state, out_res[0][0], chip,
                                       [(w_in_t, jnp.transpose(m_w_in[0]), jnp.transpose(v_w_in[0]))], "rs_in")

    _, (gathered,) = _exchange_wait(pay_state, in_res[0], "gather_small_wait")
    _, (l_ada,) = _exchange_wait(ada_states[0], gathered, "gather_ada_wait")
    w_ada16 = jnp.transpose(l_ada, (1, 0, 2)).reshape(D, 3 * D)
    gb, gc, gnw, gq, gk, gr, gwa, loss = _finish_small(gathered, nb, c_ctx, ret_log2_decay, w_ada16, dev)
    big = {4: [jnp.transpose(r)[None] for r in in_res]}
    for i, res in zip((8, 9, 10), out_res):
        big[i] = [r[None] for r in res]
    small_g = {0: gc.reshape(c_ctx.shape), 1: gnw, 2: gwa[None], 3: gb, 5: gr.reshape(ret_log2_decay.shape), 6: gq, 7: gk}
    weights = [c_ctx, norm_w, w_ada, b_ada, w_in, ret_log2_decay, q_norm_w, k_norm_w, w_o_ret, w_o_att, w_out]
    ms = [m_c_ctx, m_norm_w, m_w_ada, m_b_ada, m_w_in, m_ret_log2_decay, m_q_norm_w, m_k_norm_w, m_w_o_ret, m_w_o_att, m_w_out]
    vs = [v_c_ctx, v_norm_w, v_w_ada, v_b_ada, v_w_in, v_ret_log2_decay, v_q_norm_w, v_k_norm_w, v_w_o_ret, v_w_o_att, v_w_out]
    grads, deltas, new_ms, new_vs = [], [], [], []
    for i, (w, m, v) in enumerate(zip(weights, ms, vs)):
        if i in big:
            res = big[i]
        else:
            shape2 = (-1, w.shape[-1])
            g = small_g[i]
            res = [g] + [r.reshape(w.shape) for r in _adamw(w.reshape(shape2), g.reshape(shape2), m.reshape(shape2),
                                                             v.reshape(shape2), "adamw_%d" % i)]
        for lst, r in zip((grads, deltas, new_ms, new_vs), res):
            lst.append(r)
    return (loss, grad_x, *grads, *deltas, *new_ms, *new_vs)
```

```python
import numpy as np
import jax
import jax.numpy as jnp
from jax import lax
from jax.experimental import pallas as pl
from jax.experimental.pallas import tpu as pltpu

F32 = jnp.float32
BF16 = jnp.bfloat16

D = 1024
RH, DK, DV, CH = 4, 256, 512, 256
HQ, HKV, HD = 8, 2, 128
GRID_W = 64
ROPE_THETA = 10000.0
EPS = 1e-6
RK, RV, AK, AV, RQ, RG, AQ, AG, MR, MA = 0, 1024, 3072, 3328, 3584, 4608, 6656, 7680, 8704, 9728
IN_COLS = 10752
KV_COLS = 3584
N_DEV = 8
LR, B1, B2, ADAM_EPS, WD, STEP = 0.001, 0.9, 0.999, 1e-08, 0.01, 10
PAY_ROWS = 16
VMEM_LIMIT = 56 * 1024 * 1024
MESH_T = pl.DeviceIdType.MESH

NT = (((1,), (1,)), ((), ()))
TN = (((0,), (0,)), ((), ()))
SM_C = (HD ** -0.5) * float(np.log2(np.e))


def _params(sem):
    return pltpu.CompilerParams(dimension_semantics=sem, vmem_limit_bytes=VMEM_LIMIT)


def _pick(n, target, mult=8):
    best = None
    for t in range(mult, min(n, target) + 1, mult):
        if n % t == 0:
            best = t
    return best or n


def _dot(a, b, dn=None):
    if dn is None:
        return jnp.dot(a, b, preferred_element_type=F32)
    return lax.dot_general(a, b, dn, preferred_element_type=F32)


def _sig(v):
    return jax.nn.sigmoid(v)


def _silu(v):
    return v * _sig(v)


def _dsilu(v):
    s = _sig(v)
    return s * (1.0 + v * (1.0 - s))


def _sds(shape, dtype):
    return jax.ShapeDtypeStruct(shape, dtype)


def _matmul(a, b, *, ta=False, tb=False, tm, tn, tk, out_dtype, name, after=()):
    m = a.shape[1] if ta else a.shape[0]
    kdim = a.shape[0] if ta else a.shape[1]
    n = b.shape[0] if tb else b.shape[1]
    tm, tn, tk = _pick(m, tm, 128), _pick(n, tn, 128), _pick(kdim, tk, 128)
    nk = kdim // tk
    dn = (((0 if ta else 1,), (1 if tb else 0,)), ((), ()))

    def body(a_ref, b_ref, *rest):
        o_ref, acc_ref = rest[-2:]
        k = pl.program_id(2)
        part = _dot(a_ref[...].astype(BF16), b_ref[...].astype(BF16), dn)
        if nk == 1:
            o_ref[...] = part.astype(o_ref.dtype)
        else:
            @pl.when(k == 0)
            def _():
                acc_ref[...] = part

            @pl.when(k > 0)
            def _():
                acc_ref[...] += part

            @pl.when(k == nk - 1)
            def _():
                o_ref[...] = acc_ref[...].astype(o_ref.dtype)

    a_spec = pl.BlockSpec((tk, tm), lambda i, j, k: (k, i)) if ta else pl.BlockSpec((tm, tk), lambda i, j, k: (i, k))
    b_spec = pl.BlockSpec((tn, tk), lambda i, j, k: (j, k)) if tb else pl.BlockSpec((tk, tn), lambda i, j, k: (k, j))
    return pl.pallas_call(
        body, name=name, grid=(m // tm, n // tn, nk),
        in_specs=[a_spec, b_spec] + [pl.BlockSpec(memory_space=pl.ANY)] * len(after),
        out_specs=pl.BlockSpec((tm, tn), lambda i, j, k: (i, j)), out_shape=_sds((m, n), out_dtype),
        scratch_shapes=[pltpu.VMEM((tm, tn) if nk > 1 else (8, 128), F32)],
        compiler_params=_params(("parallel", "parallel", "arbitrary")),
    )(a, b, *after)


def _log_gamma(r):
    rp = jnp.full((8, 128), -1.0, F32).at[:2, :RH].set(r.reshape(2, RH))

    def body(r_ref, o_ref):
        o_ref[...] = jnp.log1p(-jnp.exp2(r_ref[...]))

    out = pl.pallas_call(body, name="log_gamma", out_shape=_sds((8, 128), F32))(rp)
    return out[:2, :RH]


def _mod_part(c_rows, w_ada_loc16, b_loc):
    def body(c_ref, w_ref, b_ref, o_ref):
        o_ref[...] = _dot(_silu(c_ref[...]).astype(BF16), w_ref[...]) + b_ref[...]

    return pl.pallas_call(
        body, name="mod_part", out_shape=_sds((c_rows.shape[0], w_ada_loc16.shape[1]), F32),
    )(c_rows, w_ada_loc16, b_loc)


def _norm_fwd(x2, mod3, norm_w, rows_all, row_off, rows_per_group, group0, h_prev, tm, name, after=()):
    rows = x2.shape[0]
    rb0 = row_off // tm
    bpg = rows_per_group // tm

    def body(*refs):
        x_ref, sh_ref, sc_ref, nw_ref, o_ref = refs[-5:]
        xv = x_ref[...]
        r = lax.rsqrt(jnp.mean(xv * xv, axis=-1, keepdims=True) + EPS)
        o_ref[...] = ((xv * r) * nw_ref[...] * (1.0 + sc_ref[...]) + sh_ref[...]).astype(BF16)

    in_specs = [pl.BlockSpec((tm, D), lambda i: (i, 0)),
                pl.BlockSpec((None, 1, D), lambda i: (group0 + i // bpg, 0, 0)),
                pl.BlockSpec((None, 1, D), lambda i: (group0 + i // bpg, 0, 1)),
                pl.BlockSpec((1, D), lambda i: (0, 0))]
    in_specs = [pl.BlockSpec(memory_space=pl.ANY)] * len(after) + in_specs
    args = list(after) + [x2, mod3, mod3, norm_w]
    alias = {}
    if h_prev is not None:
        in_specs.insert(0, pl.BlockSpec(memory_space=pl.ANY))
        args.insert(0, h_prev)
        alias = {0: 0}
    return pl.pallas_call(
        body, name=name, grid=(rows // tm,), in_specs=in_specs,
        out_specs=pl.BlockSpec((tm, D), lambda i: (rb0 + i, 0)), out_shape=_sds((rows_all, D), BF16),
        input_output_aliases=alias, compiler_params=_params(("parallel",)),
    )(*args)


def _decays(lg, fwd):
    ii = lax.broadcasted_iota(jnp.int32, (CH, CH), 0)
    jj = lax.broadcasted_iota(jnp.int32, (CH, CH), 1)
    ri = lax.broadcasted_iota(jnp.int32, (CH, 1), 0).astype(F32)
    rel = (ii - jj) if fwd else (jj - ii)
    relf = jnp.maximum(rel, 0).astype(F32)
    mask = jnp.where(rel >= 0, jnp.exp(lg * relf), 0.0)
    qe = (ri + 1.0) if fwd else (CH - ri)
    ke = (CH - 1.0 - ri) if fwd else ri
    return mask, relf, jnp.exp(lg * qe), qe, jnp.exp(lg * ke), ke


def _wide_specs(rowf):
    return [pl.BlockSpec((CH, 2 * DK), lambda b, c: (rowf(b, c), RQ // (2 * DK))),
            pl.BlockSpec((CH, 2 * DK), lambda b, c: (rowf(b, c), RQ // (2 * DK) + 1)),
            pl.BlockSpec((CH, RH * DK), lambda b, c: (rowf(b, c), RK // (RH * DK))),
            pl.BlockSpec((CH, 2 * DV), lambda b, c: (rowf(b, c), RV // (2 * DV))),
            pl.BlockSpec((CH, 2 * DV), lambda b, c: (rowf(b, c), RV // (2 * DV) + 1))]


def _head_qkv(refs, h):
    q0, q1, k, v0, v1 = refs
    lo = h % 2
    q = (q0, q1)[h // 2][:, lo * DK:(lo + 1) * DK].astype(F32)
    kk = k[:, h * DK:(h + 1) * DK].astype(F32) * (DK ** -0.5)
    v16 = (v0, v1)[h // 2][:, lo * DV:(lo + 1) * DV].astype(BF16)
    return q, kk, v16


def _ctx_specs(t_rows, cx):
    rb = t_rows // cx
    return [pl.BlockSpec((cx, RH * DK), lambda b, c: (rb + b, RK // (RH * DK))),
            pl.BlockSpec((cx, 2 * DV), lambda b, c: (rb + b, RV // (2 * DV))),
            pl.BlockSpec((cx, 2 * DV), lambda b, c: (rb + b, RV // (2 * DV) + 1))]


def _ctx_kv(refs, h):
    k, v0, v1 = refs
    kk = k[:, h * DK:(h + 1) * DK].astype(F32) * (DK ** -0.5)
    lo = h % 2
    return kk, (v0, v1)[h // 2][:, lo * DV:(lo + 1) * DV].astype(BF16)


def _ret_fwd(px, lg, nb, nc, cx):
    t_rows = nb * nc * CH

    def body(lg_ref, *refs):
        ins = (refs[0:5], refs[5:10])
        ctx_refs = refs[10:13]
        of_ref, ob_ref, hf_ref, hb_ref, sf, sb = refs[13:]
        c = pl.program_id(1)

        @pl.when(c == 0)
        def _():
            pos = lax.broadcasted_iota(jnp.int32, (cx, 1), 0).astype(F32)
            for h in range(RH):
                k, v16 = _ctx_kv(ctx_refs, h)
                sf[h] = _dot((k * jnp.exp(lg_ref[0, h] * (cx - 1.0 - pos))).astype(BF16), v16, TN)
                sb[h] = _dot((k * jnp.exp(lg_ref[1, h] * pos)).astype(BF16), v16, TN)

        for d, (o_ref, h_ref, s) in enumerate(((of_ref, hf_ref, sf), (ob_ref, hb_ref, sb))):
            for h in range(RH):
                lg_d = lg_ref[d, h]
                mask, _, qd, _, kd, _ = _decays(lg_d, d == 0)
                q, k, v16 = _head_qkv(ins[d], h)
                a = _dot(q.astype(BF16), k.astype(BF16), NT)
                st = s[h]
                st16 = st.astype(BF16)
                h_ref[h] = st16
                o = _dot((a * mask).astype(BF16), v16) + _dot((q * qd).astype(BF16), st16)
                o_ref[:, h * DV:(h + 1) * DV] = o.astype(BF16)
                s[h] = st * jnp.exp(lg_d * CH) + _dot((k * kd).astype(BF16), v16, TN)

    def fw(b, c):
        return b * nc + c

    def bw(b, c):
        return b * nc + nc - 1 - c

    in_specs = [pl.BlockSpec(memory_space=pltpu.SMEM)] + _wide_specs(fw) + _wide_specs(bw) + _ctx_specs(t_rows, cx)
    out_specs = [pl.BlockSpec((CH, RH * DV), lambda b, c: (fw(b, c), 0)),
                 pl.BlockSpec((CH, RH * DV), lambda b, c: (bw(b, c), 0)),
                 pl.BlockSpec((None, None, RH, DK, DV), lambda b, c: (b, c, 0, 0, 0)),
                 pl.BlockSpec((None, None, RH, DK, DV), lambda b, c: (b, nc - 1 - c, 0, 0, 0))]
    return pl.pallas_call(
        body, name="ret_fwd", grid=(nb, nc), in_specs=in_specs, out_specs=out_specs,
        out_shape=[_sds((t_rows, RH * DV), BF16)] * 2 + [_sds((nb, nc, RH, DK, DV), BF16)] * 2,
        scratch_shapes=[pltpu.VMEM((RH, DK, DV), F32), pltpu.VMEM((RH, DK, DV), F32)],
        compiler_params=_params(("parallel", "arbitrary")),
    )(lg, *([px] * 13))


def _rope_tables(seq):
    rows = seq // GRID_W
    row = np.repeat(np.arange(rows, dtype=np.float32), GRID_W)
    col = np.tile(np.arange(GRID_W, dtype=np.float32), rows)
    half = HD // 2
    freqs = (ROPE_THETA ** (-np.arange(0, half, 2, dtype=np.float32) / half)).astype(np.float32)
    ang = np.concatenate([row[:, None] * freqs, col[:, None] * freqs], axis=-1).astype(np.float32)
    cos = np.repeat(np.cos(ang), 2, axis=-1).astype(np.float32)
    sin = np.repeat(np.sin(ang), 2, axis=-1).astype(np.float32)
    sign = np.tile(np.array([-1.0, 1.0], np.float32), HD // 2)
    return jnp.asarray(cos), jnp.asarray(sin * sign)


def _swap_pairs(v):
    lane = lax.broadcasted_iota(jnp.int32, v.shape, 1)
    return jnp.where((lane & 1) == 0, pltpu.roll(v, HD - 1, 1), pltpu.roll(v, 1, 1))


def _qk_prep(px, nw, cos, sin, rows, row_off, col_off, heads, hb, seq, tm, name):
    rope = cos is not None
    rb0 = row_off // tm
    pb = seq // tm if rope else 1
    bw = hb * HD

    def body(*refs):
        if rope:
            x_ref, w_ref, c_ref, s_ref, o_ref = refs
        else:
            x_ref, w_ref, o_ref = refs
        for h in range(hb):
            sl = slice(h * HD, (h + 1) * HD)
            xv = x_ref[:, sl].astype(F32)
            r = lax.rsqrt(jnp.mean(xv * xv, axis=-1, keepdims=True) + EPS)
            t = (xv * r) * w_ref[...]
            if rope:
                t = t * c_ref[...] + _swap_pairs(t) * s_ref[...]
            o_ref[:, sl] = t.astype(BF16)

    in_specs = [pl.BlockSpec((tm, bw), lambda i, j: (rb0 + i, col_off // bw + j)),
                pl.BlockSpec((1, HD), lambda i, j: (0, 0))]
    args = [px, nw]
    if rope:
        in_specs += [pl.BlockSpec((tm, HD), lambda i, j: (i % pb, 0))] * 2
        args += [cos, sin]
    return pl.pallas_call(
        body, name=name, grid=(rows // tm, heads // hb), in_specs=in_specs,
        out_specs=pl.BlockSpec((tm, bw), lambda i, j: (i, j)), out_shape=_sds((rows, heads * HD), BF16),
        compiler_params=_params(("parallel", "parallel")),
    )(*args)


def _att_fwd(q16, kx16, kc16, px, nb, seq, cx, tq):
    t_rows = nb * seq
    nq = seq // tq
    rep = HQ // HKV
    gw = rep * HD

    def body(q_ref, kx_ref, kc_ref, vx_ref, vc_ref, g_ref, o_ref, y_ref, l_ref):
        kx = kx_ref[...]
        kc = kc_ref[...]
        vx = vx_ref[...].astype(BF16)
        vc = vc_ref[...].astype(BF16)
        l_ref[...] = jnp.zeros_like(l_ref)
        for r in range(rep):
            sl = slice(r * HD, (r + 1) * HD)
            q = q_ref[:, sl]
            s1 = _dot(q, kx, NT)
            s2 = _dot(q, kc, NT)
            m = jnp.maximum(jnp.max(s1, axis=-1, keepdims=True), jnp.max(s2, axis=-1, keepdims=True))
            e1 = jnp.exp2((s1 - m) * SM_C)
            e2 = jnp.exp2((s2 - m) * SM_C)
            tot = jnp.sum(e1, axis=-1, keepdims=True) + jnp.sum(e2, axis=-1, keepdims=True)
            o = (_dot(e1.astype(BF16), vx) + _dot(e2.astype(BF16), vc)) * (1.0 / tot)
            o_ref[:, sl] = o
            y_ref[:, sl] = (o * _silu(g_ref[:, sl].astype(F32))).astype(BF16)
            l_ref[:, r:r + 1] = m * SM_C + jnp.log(tot) * float(np.log2(np.e))

    qblk = pl.BlockSpec((tq, gw), lambda b, g, i: (b * nq + i, g))
    return pl.pallas_call(
        body, name="att_fwd", grid=(nb, HKV, nq),
        in_specs=[qblk,
                  pl.BlockSpec((seq, HD), lambda b, g, i: (b, g)),
                  pl.BlockSpec((cx, HD), lambda b, g, i: (b, g)),
                  pl.BlockSpec((seq, HD), lambda b, g, i: (b, AV // HD + g)),
                  pl.BlockSpec((cx, HD), lambda b, g, i: (t_rows // cx + b, AV // HD + g)),
                  pl.BlockSpec((tq, gw), lambda b, g, i: (b * nq + i, AG // gw + g))],
        out_specs=[qblk, qblk, pl.BlockSpec((tq, 128), lambda b, g, i: (b * nq + i, g))],
        out_shape=[_sds((t_rows, D), F32), _sds((t_rows, D), BF16), _sds((t_rows, HKV * 128), F32)],
        compiler_params=_params(("parallel", "parallel", "parallel")),
    )(q16, kx16, kc16, px, px, px)


def _gate_specs(tm, col0):
    hw = D // 2
    return [pl.BlockSpec((tm, hw), lambda i: (i, col0 // hw)), pl.BlockSpec((tm, hw), lambda i: (i, col0 // hw + 1))]


def _merge_out(o_f, o_b, yatt16, px, w_o_ret16, w_o_att16, w_out16, x2, tgt, mod3, nb, seq, tm):
    t_rows = nb * seq
    bpb = seq // tm
    hw = D // 2

    def body(of_ref, ob_ref, g0, g1, g2, g3, wr_ref, ya_ref, wa_ref, mr0, mr1, ma0, ma1, wo_ref, x_ref, t_ref, gt_ref,
             yr_ref, ar_ref, aa_ref, y_ref, dxn_ref, dout_ref, dg_ref, loss_ref):
        i = pl.program_id(1)
        for h, g_ref in enumerate((g0, g1, g2, g3)):
            sl = slice(h * DV, (h + 1) * DV)
            o = of_ref[:, sl].astype(F32) + ob_ref[:, sl].astype(F32)
            r = lax.rsqrt(jnp.mean(o * o, axis=-1, keepdims=True) + EPS)
            yr_ref[:, sl] = ((o * r) * _silu(g_ref[...].astype(F32))).astype(BF16)
        ar = _dot(yr_ref[...], wr_ref[...])
        aa = _dot(ya_ref[...], wa_ref[...])
        ar_ref[...] = ar.astype(BF16)
        aa_ref[...] = aa.astype(BF16)
        for j, (mr_ref, ma_ref) in enumerate(((mr0, ma0), (mr1, ma1))):
            sl = slice(j * hw, (j + 1) * hw)
            y_ref[:, sl] = (_sig(mr_ref[...].astype(F32)) * ar[:, sl]
                            + _sig(ma_ref[...].astype(F32)) * aa[:, sl]).astype(BF16)
        out = _dot(y_ref[...], wo_ref[...])
        gate = gt_ref[...]
        diff = x_ref[...] + gate * out - t_ref[...]
        dxn = diff * (1.0 / D)
        dxn_ref[...] = dxn
        dout_ref[...] = (gate * dxn).astype(BF16)
        dg = jnp.sum(dxn * out, axis=0, keepdims=True)
        ls = jnp.broadcast_to(jnp.sum(diff * diff) * (0.5 / D), (1, 128))

        @pl.when(i == 0)
        def _():
            dg_ref[...] = dg
            loss_ref[...] = ls

        @pl.when(i > 0)
        def _():
            dg_ref[...] += dg
            loss_ref[...] += ls

    def cols(width, col0):
        return pl.BlockSpec((tm, width), lambda b, i: (b * bpb + i, col0 // width))

    def whole(rows):
        return pl.BlockSpec((rows, D), lambda b, i: (0, 0))

    row, wide = cols(D, 0), cols(RH * DV, 0)
    gates = [cols(DV, RG + h * DV) for h in range(RH)]
    merge_gates = [cols(hw, MR), cols(hw, MR + hw), cols(hw, MA), cols(hw, MA + hw)]
    return pl.pallas_call(
        body, name="merge_out", grid=(nb, bpb),
        in_specs=[wide, wide] + gates + [whole(RH * DV), row, whole(D)] + merge_gates
        + [whole(D), row, row, pl.BlockSpec((None, 1, D), lambda b, i: (b, 0, 2))],
        out_specs=[wide, row, row, row, row, row, pl.BlockSpec((None, 1, D), lambda b, i: (b, 0, 0)),
                   pl.BlockSpec((None, 1, 128), lambda b, i: (b, 0, 0))],
        out_shape=[_sds((t_rows, RH * DV), BF16)] + [_sds((t_rows, D), BF16)] * 3
        + [_sds((t_rows, D), F32), _sds((t_rows, D), BF16), _sds((nb, 1, D), F32), _sds((nb, 1, 128), F32)],
        compiler_params=_params(("parallel", "arbitrary")),
    )(o_f, o_b, *([px] * RH), w_o_ret16, yatt16, w_o_att16, px, px, px, px, w_out16, x2, tgt, mod3)


def _bwd_branches(dout16, w_out16, w_o_ret16, w_o_att16, px, a_ret, a_att, o_f, o_b, o_att, tm):
    t_rows = dout16.shape[0]
    hw = D // 2

    def body(do_ref, wo_ref, wr_ref, wa_ref, mr0, mr1, ma0, ma1, ar_ref, aa_ref, rg0, rg1, rg2, rg3, of_ref, ob_ref,
             ag0, ag1, oa_ref, dar_ref, daa_ref, dmr_ref, dma_ref, dor_ref, drg_ref, dao_ref, dag_ref):
        dy_all = _dot(do_ref[...], wo_ref[...], NT)
        for j, (mr_ref, ma_ref) in enumerate(((mr0, ma0), (mr1, ma1))):
            sl = slice(j * hw, (j + 1) * hw)
            dy = dy_all[:, sl]
            sr = _sig(mr_ref[...].astype(F32))
            sa = _sig(ma_ref[...].astype(F32))
            dar_ref[:, sl] = (dy * sr).astype(BF16)
            daa_ref[:, sl] = (dy * sa).astype(BF16)
            dmr_ref[:, sl] = (dy * ar_ref[:, sl].astype(F32) * sr * (1.0 - sr)).astype(BF16)
            dma_ref[:, sl] = (dy * aa_ref[:, sl].astype(F32) * sa * (1.0 - sa)).astype(BF16)
        da_ret = dar_ref[...]
        for h, g_ref in enumerate((rg0, rg1, rg2, rg3)):
            sl = slice(h * DV, (h + 1) * DV)
            dy = _dot(da_ret, wr_ref[sl, :], NT)
            g = g_ref[...].astype(F32)
            o = of_ref[:, sl].astype(F32) + ob_ref[:, sl].astype(F32)
            r = lax.rsqrt(jnp.mean(o * o, axis=-1, keepdims=True) + EPS)
            on = o * r
            sg = _sig(g)
            don = dy * (g * sg)
            drg_ref[:, sl] = (dy * on * (sg * (1.0 + g * (1.0 - sg)))).astype(BF16)
            dor_ref[:, sl] = (r * (don - on * jnp.mean(on * don, axis=-1, keepdims=True))).astype(BF16)
        dy_all = _dot(daa_ref[...], wa_ref[...], NT)
        for j, g_ref in enumerate((ag0, ag1)):
            sl = slice(j * hw, (j + 1) * hw)
            dy = dy_all[:, sl]
            g = g_ref[...].astype(F32)
            sg = _sig(g)
            dao_ref[:, sl] = dy * (g * sg)
            dag_ref[:, sl] = (dy * oa_ref[:, sl] * (sg * (1.0 + g * (1.0 - sg)))).astype(BF16)

    def gate(h):
        return pl.BlockSpec((tm, DV), lambda i: (i, RG // DV + h))

    def whole(rows):
        return pl.BlockSpec((rows, D), lambda i: (0, 0))

    row = pl.BlockSpec((tm, D), lambda i: (i, 0))
    wide = pl.BlockSpec((tm, RH * DV), lambda i: (i, 0))
    return pl.pallas_call(
        body, name="bwd_branches", grid=(t_rows // tm,),
        in_specs=[row, whole(D), whole(RH * DV), whole(D)] + _gate_specs(tm, MR) + _gate_specs(tm, MA) + [row, row]
        + [gate(h) for h in range(RH)] + [wide, wide] + _gate_specs(tm, AG) + [row],
        out_specs=[row] * 4 + [wide, wide, row, row],
        out_shape=[_sds((t_rows, D), BF16)] * 4 + [_sds((t_rows, RH * DV), BF16)] * 2
        + [_sds((t_rows, D), F32), _sds((t_rows, D), BF16)],
        compiler_params=_params(("parallel",)),
    )(dout16, w_out16, w_o_ret16, w_o_att16, px, px, px, px, a_ret, a_att, *([px] * RH), o_f, o_b, px, px, o_att)


def _att_bwd(q16, kx16, kc16, px, dao, o_att, lse, q_norm_w, cos, sin, nb, seq, cx, tq, after=()):
    t_rows = nb * seq
    nq = seq // tq
    rep = HQ // HKV
    gw = rep * HD
    scale = HD ** -0.5

    def body(q_ref, kx_ref, kc_ref, vx_ref, vc_ref, dao_ref, o_ref, l_ref, xq_ref, w_ref, c_ref, s_ref, *rest):
        daq_ref, gq_ref, dkx_ref, dvx_ref, dkc_ref, dvc_ref = rest[len(after):len(after) + 6]
        accs = rest[len(after) + 6:]
        i = pl.program_id(2)
        first = jnp.logical_and(jnp.logical_and(pl.program_id(0) == 0, pl.program_id(1) == 0), i == 0)
        gq = jnp.zeros((1, HD), F32)
        kx = kx_ref[...]
        kc = kc_ref[...]
        vx = vx_ref[...].astype(BF16)
        vc = vc_ref[...].astype(BF16)
        @pl.when(i == 0)
        def _():
            for acc in accs:
                acc[...] = jnp.zeros_like(acc)

        dkx, dvx, dkc, dvc = [acc[...] for acc in accs]
        for r in range(rep):
            sl = slice(r * HD, (r + 1) * HD)
            q = q_ref[:, sl]
            lr = l_ref[:, r:r + 1]
            p1 = jnp.exp2(_dot(q, kx, NT) * SM_C - lr)
            p2 = jnp.exp2(_dot(q, kc, NT) * SM_C - lr)
            da = dao_ref[:, sl]
            da16 = da.astype(BF16)
            delta = jnp.sum(da * o_ref[:, sl], axis=-1, keepdims=True)
            ds1 = (p1 * (_dot(da16, vx, NT) - delta)).astype(BF16)
            ds2 = (p2 * (_dot(da16, vc, NT) - delta)).astype(BF16)
            dq = (_dot(ds1, kx) + _dot(ds2, kc)) * scale
            dkx += _dot(q, ds1, TN)
            dkc += _dot(q, ds2, TN)
            dvx += _dot(da16, p1.astype(BF16), TN)
            dvc += _dot(da16, p2.astype(BF16), TN)
            dt = dq * c_ref[...] + _swap_pairs(dq * s_ref[...])
            xv = xq_ref[:, sl].astype(F32)
            rn = lax.rsqrt(jnp.mean(xv * xv, axis=-1, keepdims=True) + EPS)
            xh = xv * rn
            dxh = dt * w_ref[...]
            daq_ref[:, sl] = (rn * (dxh - xh * jnp.mean(dxh * xh, axis=-1, keepdims=True))).astype(BF16)
            gq += jnp.sum(dt * xh, axis=0, keepdims=True)
        for acc, val in zip(accs, (dkx, dvx, dkc, dvc)):
            acc[...] = val

        @pl.when(first)
        def _():
            gq_ref[...] = gq

        @pl.when(jnp.logical_not(first))
        def _():
            gq_ref[...] += gq

        @pl.when(i == nq - 1)
        def _():
            dkx_ref[...] = dkx.T * scale
            dvx_ref[...] = dvx.T
            dkc_ref[...] = dkc.T * scale
            dvc_ref[...] = dvc.T

    qblk = pl.BlockSpec((tq, gw), lambda b, g, i: (b * nq + i, g))
    kxb = pl.BlockSpec((None, seq, HD), lambda b, g, i: (b, 0, g))
    kcb = pl.BlockSpec((None, cx, HD), lambda b, g, i: (b, 0, g))
    table = pl.BlockSpec((tq, HD), lambda b, g, i: (i, 0))
    one = pl.BlockSpec((1, HD), lambda b, g, i: (0, 0))
    return pl.pallas_call(
        body, name="att_bwd", grid=(nb, HKV, nq),
        in_specs=[qblk,
                  pl.BlockSpec((seq, HD), lambda b, g, i: (b, g)),
                  pl.BlockSpec((cx, HD), lambda b, g, i: (b, g)),
                  pl.BlockSpec((seq, HD), lambda b, g, i: (b, AV // HD + g)),
                  pl.BlockSpec((cx, HD), lambda b, g, i: (t_rows // cx + b, AV // HD + g)),
                  qblk, qblk, pl.BlockSpec((tq, 128), lambda b, g, i: (b * nq + i, g)),
                  pl.BlockSpec((tq, gw), lambda b, g, i: (b * nq + i, AQ // gw + g)), one, table, table]
        + [pl.BlockSpec(memory_space=pl.ANY)] * len(after),
        out_specs=[qblk, one, kxb, kxb, kcb, kcb],
        out_shape=[_sds((t_rows, D), BF16), _sds((1, HD), F32), _sds((nb, seq, HKV * HD), F32),
                   _sds((nb, seq, HKV * HD), F32), _sds((nb, cx, HKV * HD), F32), _sds((nb, cx, HKV * HD), F32)],
        scratch_shapes=[pltpu.VMEM((HD, seq), F32), pltpu.VMEM((HD, seq), F32), pltpu.VMEM((HD, cx), F32),
                        pltpu.VMEM((HD, cx), F32)],
        compiler_params=_params(("arbitrary", "arbitrary", "arbitrary")),
    )(q16, kx16, kc16, px, px, dao, o_att, lse, px, q_norm_w, cos, sin, *after)


def _qk_prep_bwd(dt, px, nw, cos, sin, rows, row_off, col_off, heads, hb, seq, tm, name):
    rope = cos is not None
    rb0 = row_off // tm
    pb = seq // tm if rope else 1
    bw = hb * HD

    def body(*refs):
        if rope:
            d_ref, x_ref, w_ref, c_ref, s_ref, dx_ref, dw_ref = refs
        else:
            d_ref, x_ref, w_ref, dx_ref, dw_ref = refs
        first = jnp.logical_and(pl.program_id(0) == 0, pl.program_id(1) == 0)
        dw = jnp.zeros((1, HD), F32)
        for h in range(hb):
            sl = slice(h * HD, (h + 1) * HD)
            dtv = d_ref[:, sl]
            if rope:
                dtv = dtv * c_ref[...] + _swap_pairs(dtv * s_ref[...])
            xv = x_ref[:, sl].astype(F32)
            r = lax.rsqrt(jnp.mean(xv * xv, axis=-1, keepdims=True) + EPS)
            xh = xv * r
            dxh = dtv * w_ref[...]
            dx_ref[:, sl] = (r * (dxh - xh * jnp.mean(dxh * xh, axis=-1, keepdims=True))).astype(BF16)
            dw += jnp.sum(dtv * xh, axis=0, keepdims=True)

        @pl.when(first)
        def _():
            dw_ref[...] = dw

        @pl.when(jnp.logical_not(first))
        def _():
            dw_ref[...] += dw

    blk = pl.BlockSpec((tm, bw), lambda i, j: (i, j))
    in_specs = [blk, pl.BlockSpec((tm, bw), lambda i, j: (rb0 + i, col_off // bw + j)),
                pl.BlockSpec((1, HD), lambda i, j: (0, 0))]
    args = [dt, px, nw]
    if rope:
        in_specs += [pl.BlockSpec((tm, HD), lambda i, j: (i % pb, 0))] * 2
        args += [cos, sin]
    return pl.pallas_call(
        body, name=name, grid=(rows // tm, heads // hb), in_specs=in_specs,
        out_specs=[blk, pl.BlockSpec((1, HD), lambda i, j: (0, 0))],
        out_shape=[_sds((rows, heads * HD), BF16), _sds((1, HD), F32)],
        compiler_params=_params(("arbitrary", "arbitrary")),
    )(*args)


def _ret_bwd(px, lg, do16, hist_f, hist_b, nb, nc, cx):
    t_rows = nb * nc * CH

    def body(lg_ref, *refs):
        ins = (refs[0:5], refs[7:12])
        do_refs = (refs[5], refs[12])
        h_refs = (refs[6], refs[13])
        ctx_refs = refs[14:17]
        outs = (refs[17:20], refs[20:23])
        dck_ref, dcv_ref, dlg_ref = refs[23:26]
        dss = (refs[26], refs[27])
        c = pl.program_id(1)

        @pl.when(c == 0)
        def _():
            dss[0][...] = jnp.zeros_like(dss[0])
            dss[1][...] = jnp.zeros_like(dss[1])
            dlg_ref[...] = jnp.zeros_like(dlg_ref)

        for d in range(2):
            dq_ref, dk_ref, dv_ref = outs[d]
            for h in range(RH):
                lg_d = lg_ref[d, h]
                mask, relf, qd, qe, kd, ke = _decays(lg_d, d == 0)
                g_ch = jnp.exp(lg_d * CH)
                q, k, v16 = _head_qkv(ins[d], h)
                q16 = q.astype(BF16)
                k16 = k.astype(BF16)
                do16v = do_refs[d][:, h * DV:(h + 1) * DV]
                st16 = h_refs[d][h]
                dst = dss[d][h]
                dst16 = dst.astype(BF16)
                a = _dot(q16, k16, NT) * mask
                dp = _dot(do16v, v16, NT)
                da16 = (dp * mask).astype(BF16)
                dq_cross = _dot(do16v, st16, NT) * qd
                dq_ref[:, h * DK:(h + 1) * DK] = (_dot(da16, k16) + dq_cross).astype(BF16)
                dk_state = _dot(v16, dst16, NT) * kd
                dk_ref[:, h * DK:(h + 1) * DK] = ((_dot(da16, q16, TN) + dk_state) * (DK ** -0.5)).astype(BF16)
                dv = _dot(a.astype(BF16), do16v, TN) + _dot((k * kd).astype(BF16), dst16)
                dv_ref[:, h * DV:(h + 1) * DV] = dv.astype(BF16)
                dlg = (jnp.sum(relf * a * dp)
                       + jnp.sum(qe * jnp.sum(q * dq_cross, axis=-1, keepdims=True))
                       + jnp.sum(ke * jnp.sum(k * dk_state, axis=-1, keepdims=True))
                       + CH * g_ch * jnp.sum(dst * st16.astype(F32)))
                row = d * RH + h
                dlg_ref[row:row + 1, :] += jnp.broadcast_to(dlg, (1, 128))
                dss[d][h] = g_ch * dst + _dot((q * qd).astype(BF16), do16v, TN)

        @pl.when(c == nc - 1)
        def _():
            pos = lax.broadcasted_iota(jnp.int32, (cx, 1), 0).astype(F32)
            for h in range(RH):
                k, v16 = _ctx_kv(ctx_refs, h)
                dk = jnp.zeros((cx, DK), F32)
                dv = jnp.zeros((cx, DV), F32)
                for d, e in enumerate((cx - 1.0 - pos, pos)):
                    w = jnp.exp(lg_ref[d, h] * e)
                    ds16 = dss[d][h].astype(BF16)
                    t = _dot(v16, ds16, NT)
                    dk += t * w
                    dv += _dot((k * w).astype(BF16), ds16)
                    dlg = jnp.sum(e * w * jnp.sum(k * t, axis=-1, keepdims=True))
                    row = d * RH + h
                    dlg_ref[row:row + 1, :] += jnp.broadcast_to(dlg, (1, 128))
                dck_ref[:, h * DK:(h + 1) * DK] = (dk * (DK ** -0.5)).astype(BF16)
                dcv_ref[:, h * DV:(h + 1) * DV] = dv.astype(BF16)

    def fw(b, c):
        return b * nc + nc - 1 - c

    def bw(b, c):
        return b * nc + c

    def rows(rowf, width):
        return pl.BlockSpec((CH, width), lambda b, c: (rowf(b, c), 0))

    def hist(rowf):
        return pl.BlockSpec((None, None, RH, DK, DV), lambda b, c: (b, rowf(0, c), 0, 0, 0))

    in_specs = [pl.BlockSpec(memory_space=pltpu.SMEM)]
    out_specs = []
    for rowf in (fw, bw):
        in_specs += _wide_specs(rowf) + [rows(rowf, RH * DV), hist(rowf)]
        out_specs += [rows(rowf, RH * DK), rows(rowf, RH * DK), rows(rowf, RH * DV)]
    in_specs += _ctx_specs(t_rows, cx)
    out_specs += [pl.BlockSpec((cx, RH * DK), lambda b, c: (b, 0)), pl.BlockSpec((cx, RH * DV), lambda b, c: (b, 0)),
                  pl.BlockSpec((None, 8, 128), lambda b, c: (b, 0, 0))]
    qk = _sds((t_rows, RH * DK), BF16)
    vv = _sds((t_rows, RH * DV), BF16)
    return pl.pallas_call(
        body, name="ret_bwd", grid=(nb, nc), in_specs=in_specs, out_specs=out_specs,
        out_shape=[qk, qk, vv, qk, qk, vv, _sds((nb * cx, RH * DK), BF16), _sds((nb * cx, RH * DV), BF16),
                   _sds((nb, 8, 128), F32)],
        scratch_shapes=[pltpu.VMEM((RH, DK, DV), F32), pltpu.VMEM((RH, DK, DV), F32)],
        compiler_params=_params(("parallel", "arbitrary")),
    )(lg, *([px] * 5), do16, hist_f, *([px] * 5), do16, hist_b, *([px] * 3))


def _assemble_lat(rows_all, dk_f, dk_b, dv_f, dv_b, dak16, dvx, dq_f, dq_b, drg16, daq16, dag16, dmr16, dma16, tm):
    t_rows = dk_f.shape[0]

    def body(dkf, dkb, dvf, dvb, dak, dav, dqf, dqb, drg, daq, dag, dmr, dma, o_ref):
        o_ref[:, RK:RK + RH * DK] = (dkf[...].astype(F32) + dkb[...].astype(F32)).astype(BF16)
        o_ref[:, RV:RV + RH * DV] = (dvf[...].astype(F32) + dvb[...].astype(F32)).astype(BF16)
        o_ref[:, AK:AK + HKV * HD] = dak[...]
        o_ref[:, AV:AV + HKV * HD] = dav[...].astype(BF16)
        o_ref[:, RQ:RQ + RH * DK] = (dqf[...].astype(F32) + dqb[...].astype(F32)).astype(BF16)
        o_ref[:, RG:RG + RH * DV] = drg[...]
        o_ref[:, AQ:AQ + D] = daq[...]
        o_ref[:, AG:AG + D] = dag[...]
        o_ref[:, MR:MR + D] = dmr[...]
        o_ref[:, MA:MA + D] = dma[...]

    args = (dk_f, dk_b, dv_f, dv_b, dak16, dvx, dq_f, dq_b, drg16, daq16, dag16, dmr16, dma16)
    return pl.pallas_call(
        body, name="assemble_lat", grid=(t_rows // tm,),
        in_specs=[pl.BlockSpec((tm, a.shape[1]), lambda i: (i, 0)) for a in args],
        out_specs=pl.BlockSpec((tm, IN_COLS), lambda i: (i, 0)), out_shape=_sds((rows_all, IN_COLS), BF16),
        compiler_params=_params(("parallel",)),
    )(*args)


def _assemble_ctx(dp_all, dck16, dcv16, dcak16, dvc, t_rows, tm):
    c_rows = dck16.shape[0]
    rb = t_rows // tm

    def body(_, dck, dcv, dcak, dcav, o_ref):
        o_ref[:, RK:RK + RH * DK] = dck[...]
        o_ref[:, RV:RV + RH * DV] = dcv[...]
        o_ref[:, AK:AK + HKV * HD] = dcak[...]
        o_ref[:, AV:AV + HKV * HD] = dcav[...].astype(BF16)
        o_ref[:, KV_COLS:] = jnp.zeros((tm, IN_COLS - KV_COLS), BF16)

    args = (dck16, dcv16, dcak16, dvc)
    return pl.pallas_call(
        body, name="assemble_ctx", grid=(c_rows // tm,),
        in_specs=[pl.BlockSpec(memory_space=pl.ANY)]
        + [pl.BlockSpec((tm, a.shape[1]), lambda i: (i, 0)) for a in args],
        out_specs=pl.BlockSpec((tm, IN_COLS), lambda i: (rb + i, 0)), out_shape=_sds(dp_all.shape, BF16),
        input_output_aliases={0: 0},
        compiler_params=_params(("parallel",)),
    )(dp_all, *args)


def _norm_bwd(dh, x2, mod3, norm_w, dxn, row_off, rows_per_group, group0, tm, name):
    with_dx = dxn is not None
    rows = x2.shape[0]
    rb0 = row_off // tm
    bpg = rows_per_group // tm
    ngroups = rows // rows_per_group

    def body(*refs):
        if with_dx:
            dh_ref, x_ref, sc_ref, nw_ref, dxn_ref, dx_ref, dsh_ref, dsc_ref, dnw_ref = refs
        else:
            dh_ref, x_ref, sc_ref, nw_ref, dsh_ref, dsc_ref, dnw_ref = refs
        i = pl.program_id(0)
        dhv = dh_ref[...]
        xv = x_ref[...]
        nw = nw_ref[...]
        r = lax.rsqrt(jnp.mean(xv * xv, axis=-1, keepdims=True) + EPS)
        xh = xv * r
        dm = dhv * (1.0 + sc_ref[...])
        dsh = jnp.sum(dhv, axis=0, keepdims=True)
        dsc = jnp.sum(dhv * (xh * nw), axis=0, keepdims=True)
        dnw = jnp.sum(dm * xh, axis=0, keepdims=True)
        if with_dx:
            dxh = dm * nw
            dx_ref[...] = dxn_ref[...] + r * (dxh - xh * jnp.mean(dxh * xh, axis=-1, keepdims=True))

        @pl.when(i % bpg == 0)
        def _():
            dsh_ref[...] = dsh
            dsc_ref[...] = dsc

        @pl.when(i % bpg != 0)
        def _():
            dsh_ref[...] += dsh
            dsc_ref[...] += dsc

        @pl.when(i == 0)
        def _():
            dnw_ref[...] = dnw

        @pl.when(i > 0)
        def _():
            dnw_ref[...] += dnw

    grp = pl.BlockSpec((None, 1, D), lambda i: (i // bpg, 0, 0))
    in_specs = [pl.BlockSpec((tm, D), lambda i: (rb0 + i, 0)), pl.BlockSpec((tm, D), lambda i: (i, 0)),
                pl.BlockSpec((None, 1, D), lambda i: (group0 + i // bpg, 0, 1)),
                pl.BlockSpec((1, D), lambda i: (0, 0))]
    args = [dh, x2, mod3, norm_w]
    out_specs = [grp, grp, pl.BlockSpec((1, D), lambda i: (0, 0))]
    out_shape = [_sds((ngroups, 1, D), F32), _sds((ngroups, 1, D), F32), _sds((1, D), F32)]
    if with_dx:
        in_specs.append(pl.BlockSpec((tm, D), lambda i: (i, 0)))
        args.append(dxn)
        out_specs.insert(0, pl.BlockSpec((tm, D), lambda i: (i, 0)))
        out_shape.insert(0, _sds((rows, D), F32))
    return pl.pallas_call(
        body, name=name, grid=(rows // tm,), in_specs=in_specs, out_specs=out_specs, out_shape=out_shape,
        compiler_params=_params(("arbitrary",)),
    )(*args)


def _small_final(dmod_all, dmodc_parts, c_rows, dm_loc_rows, nw_parts, misc_parts, c_ctx, r_pad, w_ada16):
    loc = dm_loc_rows.shape[1]

    def body(dm_ref, dmc_ref, c_ref, dml_ref, nwp_ref, mp_ref, cc_ref, r_ref, w_ref,
             gb_ref, gc_ref, gnw_ref, misc_ref, gwa_ref):
        dmc = jnp.sum(dmc_ref[...], axis=0, keepdims=True)
        gb_ref[...] = jnp.sum(dm_ref[...], axis=0, keepdims=True) + dmc
        dsc = _dot(jnp.broadcast_to(dmc, (8, 3 * D)).astype(BF16), w_ref[...], NT)[0:1, :]
        gc_ref[...] = dsc * _dsilu(cc_ref[...])
        gnw_ref[...] = jnp.sum(nwp_ref[...], axis=0, keepdims=True)
        misc = jnp.sum(mp_ref[...], axis=0, keepdims=True)
        y = jnp.exp2(r_ref[...])
        lane = lax.broadcasted_iota(jnp.int32, (1, D), 1)
        is_decay = jnp.logical_and(lane >= 2 * HD, lane < 2 * HD + 2 * RH)
        misc_ref[...] = misc * jnp.where(is_decay, -(y * np.float32(np.log(2.0))) / (1.0 - y), 1.0)
        gwa_ref[...] = _dot(_silu(c_ref[...]).astype(BF16), dml_ref[...].astype(BF16), TN)

    return pl.pallas_call(
        body, name="small_final",
        out_shape=[_sds((1, 3 * D), F32), _sds((1, D), F32), _sds((1, D), F32), _sds((1, D), F32), _sds((D, loc), F32)],
        compiler_params=pltpu.CompilerParams(vmem_limit_bytes=VMEM_LIMIT),
    )(dmod_all, dmodc_parts, c_rows, dm_loc_rows, nw_parts, misc_parts, c_ctx, r_pad, w_ada16)


def _adamw_math(w, g, m, v):
    nm = B1 * m + (1.0 - B1) * g
    nv = B2 * v + (1.0 - B2) * (g * g)
    return -LR * ((nm / (1.0 - B1 ** STEP)) / (jnp.sqrt(nv / (1.0 - B2 ** STEP)) + ADAM_EPS) + WD * w), nm, nv


def _adamw(w, g, m, v, name):
    rows, cols = w.shape
    tm = _pick(rows, 448, 8)

    def body(w_ref, g_ref, m_ref, v_ref, d_ref, nm_ref, nv_ref):
        d_ref[...], nm_ref[...], nv_ref[...] = _adamw_math(w_ref[...], g_ref[...], m_ref[...], v_ref[...])

    blk = pl.BlockSpec((tm, cols), lambda i: (i, 0))
    return pl.pallas_call(
        body, name=name, grid=(rows // tm,), in_specs=[blk] * 4, out_specs=[blk] * 3,
        out_shape=[_sds((rows, cols), F32)] * 3, compiler_params=_params(("parallel",)),
    )(w, g, m, v)


def _mesh_pos():
    return lax.axis_index("x"), lax.axis_index("y"), lax.axis_index("c")


def _all_gather(arrs, name):
    n = len(arrs)

    def body(*refs):
        ins, outs = refs[:n], refs[n:2 * n]
        send_sems, recv_sems, local_sems = refs[2 * n:]
        x, y, c = _mesh_pos()
        me, sib = (x, y, c), (x, y, 1 - c)
        chips = [(1 - x, y), (x, 1 - y), (1 - x, 1 - y)]

        def slot(p):
            return 4 * p[0] + 2 * p[1] + p[2]

        def copy(a, k, block, to, own):
            dst = outs[a].at[slot(block)]
            return pltpu.make_async_remote_copy(
                src_ref=ins[a] if own else dst, dst_ref=dst, send_sem=send_sems.at[a, k], recv_sem=recv_sems.at[a, k],
                device_id=to, device_id_type=MESH_T)

        mine = [pltpu.make_async_copy(ins[a], outs[a].at[slot(me)], local_sems.at[a]) for a in range(n)]
        for cp in mine:
            cp.start()
        first = []
        for a in range(n):
            first.append(copy(a, 0, me, sib, True))
            first += [copy(a, 1 + j, me, (*chip, c), True) for j, chip in enumerate(chips)]
        for cp in first:
            cp.start()
        passed = []
        for j, chip in enumerate(chips):
            for a in range(n):
                copy(a, 1 + j, (*chip, c), me, False).wait_recv()
                fwd = copy(a, 4 + j, (*chip, c), sib, False)
                fwd.start()
                passed.append(fwd)
        for a in range(n):
            copy(a, 0, sib, me, False).wait_recv()
            for j, chip in enumerate(chips):
                copy(a, 4 + j, (*chip, 1 - c), me, False).wait_recv()
        for cp in first + passed:
            cp.wait_send()
        for cp in mine:
            cp.wait()

    hbm = pl.BlockSpec(memory_space=pl.ANY)
    return pl.pallas_call(
        body, name=name, in_specs=[hbm] * n, out_specs=[hbm] * n,
        out_shape=[_sds((N_DEV,) + a.shape, a.dtype) for a in arrs],
        scratch_shapes=[pltpu.SemaphoreType.DMA((n, 7)), pltpu.SemaphoreType.DMA((n, 7)), pltpu.SemaphoreType.DMA((n,))],
    )(*arrs)


def _pair_exchange(arrs, name):
    n = len(arrs)

    def body(*refs):
        ins, outs = refs[:n], refs[n:2 * n]
        send_sems, recv_sems = refs[2 * n:]
        x, y, c = _mesh_pos()
        sib = (x, y, 1 - c)
        sends = []
        for a in range(n):
            for k in range(4):
                sends.append(pltpu.make_async_remote_copy(
                    src_ref=ins[a].at[2 * k + 1 - c], dst_ref=outs[a].at[k], send_sem=send_sems.at[a, k],
                    recv_sem=recv_sems.at[a, k], device_id=sib, device_id_type=MESH_T))
        for cp in sends:
            cp.start()
        for cp in sends:
            cp.wait_recv()
        for cp in sends:
            cp.wait_send()

    hbm = pl.BlockSpec(memory_space=pl.ANY)
    return pl.pallas_call(
        body, name=name, in_specs=[hbm] * n, out_specs=[hbm] * n,
        out_shape=[_sds((4,) + a.shape[1:], a.dtype) for a in arrs],
        scratch_shapes=[pltpu.SemaphoreType.DMA((n, 4)), pltpu.SemaphoreType.DMA((n, 4))],
    )(*arrs)


def _pair_add(part, got, core, name):
    _, rows, cols = part.shape
    tm = _pick(rows, 672, 16)
    p4 = part.reshape(4, 2, rows, cols)

    def body(core_ref, p_ref, g_ref, o_ref):
        o_ref[...] = (p_ref[...].astype(F32) + g_ref[...].astype(F32)).astype(BF16)

    blk = pl.BlockSpec((None, tm, cols), lambda k, i, cr: (k, i, 0))
    return pl.pallas_call(
        body, name=name,
        grid_spec=pltpu.PrefetchScalarGridSpec(
            num_scalar_prefetch=1, grid=(4, rows // tm),
            in_specs=[pl.BlockSpec((None, None, tm, cols), lambda k, i, cr: (k, cr[0], i, 0)), blk], out_specs=blk),
        out_shape=_sds((4, rows, cols), BF16), compiler_params=_params(("parallel", "parallel")),
    )(core, p4, got)


def _chip_sum_adamw(pair_sums, landed, chip, w, m, v, name):
    _, rows, cols = pair_sums.shape
    tm = _pick(rows, 448, 16)

    def body(chip_ref, s_ref, l_ref, w_ref, m_ref, v_ref, g_ref, d_ref, nm_ref, nv_ref):
        acc = s_ref[...].astype(F32)
        for j in range(3):
            acc = acc + l_ref[j].astype(F32)
        g_ref[...] = acc
        d_ref[...], nm_ref[...], nv_ref[...] = _adamw_math(w_ref[...], acc, m_ref[...], v_ref[...])

    blk = pl.BlockSpec((tm, cols), lambda i, ch: (i, 0))
    return pl.pallas_call(
        body, name=name,
        grid_spec=pltpu.PrefetchScalarGridSpec(
            num_scalar_prefetch=1, grid=(rows // tm,),
            in_specs=[pl.BlockSpec((None, tm, cols), lambda i, ch: (ch[0], i, 0)),
                      pl.BlockSpec((3, tm, cols), lambda i, ch: (0, i, 0)), blk, blk, blk],
            out_specs=[blk] * 4),
        out_shape=[_sds((rows, cols), F32)] * 4, compiler_params=_params(("parallel",)),
    )(chip, pair_sums, landed, w, m, v)


_HBM = pl.BlockSpec(memory_space=pltpu.HBM)
_SEM = pl.BlockSpec(memory_space=pltpu.SEMAPHORE)
_EFFECT = pltpu.SideEffectType.DATAFLOW_SIDE_EFFECTING


def _chip_routes(n):
    def plan(x, y, c):
        routes = []
        for a in range(n):
            for j in range(1, 4):
                px, py = x ^ (j >> 1), y ^ (j & 1)
                routes.append((a, 2 * px + py, (px, py, c), j - 1))
        return routes
    return plan, 3 * n


def _bcast_routes(n):
    def plan(x, y, c):
        routes = []
        for a in range(n):
            for k in range(1, N_DEV):
                peer = (x ^ ((k >> 2) & 1), y ^ ((k >> 1) & 1), c ^ (k & 1))
                routes.append((a, 0, peer, 4 * x + 2 * y + c))
        return routes
    return plan, 7 * n


def _route_copies(srcs, lands, send_sems, recv_sems, routes):
    return [pltpu.make_async_remote_copy(
        src_ref=srcs[a].at[sb], dst_ref=lands[a].at[lb], send_sem=send_sems.at[r], recv_sem=recv_sems.at[r],
        device_id=peer, device_id_type=MESH_T) for r, (a, sb, peer, lb) in enumerate(routes)]


def _exchange_start(srcs, lands, routes, name, after=()):
    plan, count = routes
    n = len(srcs)
    n_in = 2 * n + len(after)

    def body(*refs):
        send_sems, recv_sems = refs[n_in], refs[n_in + 1]
        token = refs[-1]
        for cp in _route_copies(refs[:n], refs[n:2 * n], send_sems, recv_sems, plan(*_mesh_pos())):
            cp.start()
        token[...] = jnp.zeros_like(token)

    args = [pltpu.with_memory_space_constraint(a, pltpu.HBM) for a in list(srcs) + list(lands)]
    out = pl.pallas_call(
        body, name=name,
        out_shape=(pltpu.SemaphoreType.DMA((count,)), pltpu.SemaphoreType.DMA((count,)),
                   *[pltpu.HBM(a.shape, a.dtype) for a in args], _sds((8, 128), F32)),
        in_specs=[_HBM] * (2 * n) + [pl.BlockSpec(memory_space=pl.ANY)] * len(after),
        out_specs=(_SEM, _SEM, *([_HBM] * (2 * n)), pl.BlockSpec(memory_space=pltpu.VMEM)),
        input_output_aliases={i: 2 + i for i in range(2 * n)},
        compiler_params=pltpu.CompilerParams(has_side_effects=_EFFECT),
    )(*args, *after)
    return (out[0], out[1], list(out[2:2 + 2 * n]), routes), out[-1]


def _exchange_wait(state, after, name):
    send_sems, recv_sems, bufs, (plan, count) = state
    n = len(bufs) // 2

    def body(*refs):
        send_s, recv_s = refs[2 * n], refs[2 * n + 1]
        for cp in _route_copies(refs[:n], refs[n:2 * n], send_s, recv_s, plan(*_mesh_pos())):
            cp.wait_send()
            cp.wait_recv()

    out = pl.pallas_call(
        body, name=name, out_shape=tuple(pltpu.HBM(a.shape, a.dtype) for a in bufs),
        in_specs=[_HBM] * (2 * n) + [_SEM, _SEM, pl.BlockSpec(memory_space=pl.ANY)], out_specs=tuple([_HBM] * (2 * n)),
        input_output_aliases={i: i for i in range(2 * n)},
        compiler_params=pltpu.CompilerParams(has_side_effects=_EFFECT),
    )(*bufs, send_sems, recv_sems, after)
    return list(out[:n]), list(out[n:])


def _group_routes(js):
    def plan(x, y, c):
        return [(0, 0, (x ^ (j >> 1), y ^ (j & 1), c), 2 * j + c) for j in js]
    return plan, len(js)


def _pair_fill(groups, js, name, after=()):
    def body(*refs):
        g_ref, send_sems, recv_sems = refs[-3:]
        x, y, c = _mesh_pos()
        sends = []
        for n, j in enumerate(js):
            mine = g_ref.at[2 * j + c]
            sends.append(pltpu.make_async_remote_copy(
                src_ref=mine, dst_ref=mine, send_sem=send_sems.at[n], recv_sem=recv_sems.at[n],
                device_id=(x, y, 1 - c), device_id_type=MESH_T))
        for cp in sends:
            cp.start()
        for n, j in enumerate(js):
            pltpu.make_async_remote_copy(
                src_ref=g_ref.at[2 * j + c], dst_ref=g_ref.at[2 * j + 1 - c], send_sem=send_sems.at[n],
                recv_sem=recv_sems.at[n], device_id=(x, y, 1 - c), device_id_type=MESH_T).wait_recv()
        for cp in sends:
            cp.wait_send()

    hbm = pl.BlockSpec(memory_space=pl.ANY)
    return pl.pallas_call(
        body, name=name, in_specs=[hbm] * (1 + len(after)), out_specs=hbm, out_shape=_sds(groups.shape, groups.dtype),
        input_output_aliases={0: 0},
        scratch_shapes=[pltpu.SemaphoreType.DMA((len(js),)), pltpu.SemaphoreType.DMA((len(js),))],
    )(groups, *after)


def _in_proj_group(h_all, groups, j0, ng, chip, px_prev, after, name):
    rows_all = h_all.shape[0]
    gcols = IN_COLS // 4
    tm = _pick(rows_all, 1536, 128)
    g4 = groups.reshape(4, gcols, D)

    n_lead = (1 if px_prev is not None else 0) + len(after)
    lead = ([px_prev] if px_prev is not None else []) + list(after)

    def body(chip_ref, *refs):
        h_ref, w_ref, o_ref = refs[n_lead:]
        o_ref[...] = _dot(h_ref[...], w_ref[...], NT).astype(BF16)

    return pl.pallas_call(
        body, name=name,
        grid_spec=pltpu.PrefetchScalarGridSpec(
            num_scalar_prefetch=1, grid=(ng, rows_all // tm),
            in_specs=[pl.BlockSpec(memory_space=pl.ANY)] * n_lead
            + [pl.BlockSpec((tm, D), lambda n, i, ch: (i, 0)),
               pl.BlockSpec((None, gcols, D), lambda n, i, ch: (j0 + n, 0, 0))],
            out_specs=pl.BlockSpec((tm, gcols), lambda n, i, ch: (i, ch[0] ^ (j0 + n)))),
        out_shape=_sds((rows_all, IN_COLS), BF16),
        input_output_aliases={1: 0} if px_prev is not None else {},
        compiler_params=_params(("parallel", "parallel")),
    )(chip, *lead, h_all, g4)


def _d_h_groups(dp_all, groups, chip, after):
    rows_all = dp_all.shape[0]
    gcols = IN_COLS // 4
    tm = _pick(rows_all, 1536, 128)
    g4 = groups.reshape(4, gcols, D)
    n_lead = len(after)

    def body(chip_ref, *refs):
        a_ref, w_ref, o_ref = refs[n_lead:]
        j = pl.program_id(1)
        part = _dot(a_ref[...], w_ref[...])

        @pl.when(j == 0)
        def _():
            o_ref[...] = part

        @pl.when(j > 0)
        def _():
            o_ref[...] += part

    return pl.pallas_call(
        body, name="d_h",
        grid_spec=pltpu.PrefetchScalarGridSpec(
            num_scalar_prefetch=1, grid=(rows_all // tm, 4),
            in_specs=[pl.BlockSpec(memory_space=pl.ANY)] * n_lead
            + [pl.BlockSpec((tm, gcols), lambda i, j, ch: (i, ch[0] ^ j)),
               pl.BlockSpec((None, gcols, D), lambda i, j, ch: (j, 0, 0))],
            out_specs=pl.BlockSpec((tm, D), lambda i, j, ch: (i, 0))),
        out_shape=_sds((rows_all, D), F32),
        compiler_params=_params(("parallel", "arbitrary")),
    )(chip, *after, dp_all, g4)


def _reduce_scatter_start(parts, core, name):
    got = _pair_exchange(parts, name + "_pair")
    sums = [_pair_add(p, g, core, "%s_add_%d" % (name, i)) for i, (p, g) in enumerate(zip(parts, got))]
    lands = [lax.empty((3,) + s_.shape[1:], BF16) for s_ in sums]
    return _exchange_start(sums, lands, _chip_routes(len(sums)), name + "_start")


def _reduce_scatter_finish(rs_state, after, chip, wmv, name):
    sums, landed = _exchange_wait(rs_state, after, name + "_wait")
    return [_chip_sum_adamw(s_, l_, chip, *t, "%s_adamw_%d" % (name, i))
            for i, (s_, l_, t) in enumerate(zip(sums, landed, wmv))]


def _local_step(x, c, ctx, norm_w, ret_log2_decay, q_norm_w, k_norm_w, loss_target,
                mod, proj_in, proj_back, get_w_o, on_out_grads, on_in_grad, started=()):
    nb, seq, _ = x.shape
    cx = ctx.shape[1]
    t_rows, c_rows = nb * seq, nb * cx
    rows_all = t_rows + c_rows
    nc = seq // CH
    tm = _pick(seq, 256, 128)
    te = _pick(seq, 512, 128)
    assert cx % tm == 0 and t_rows % cx == 0 and seq % GRID_W == 0

    x2 = x.reshape(t_rows, D)
    ctx2 = ctx.reshape(c_rows, D)
    tgt = loss_target.reshape(t_rows, D)
    lg = _log_gamma(ret_log2_decay)
    cos, sin = _rope_tables(seq)

    mod3 = mod[:, None, :]
    h_all = _norm_fwd(x2, mod3, norm_w, rows_all, 0, seq, 0, None, te, "norm_fwd", after=started)
    h_all = _norm_fwd(ctx2, mod3, norm_w, rows_all, t_rows, c_rows, nb, h_all, tm, "norm_fwd_ctx")
    px = proj_in(h_all)
    o_f, o_b, hist_f, hist_b = _ret_fwd(px, lg, nb, nc, cx)
    q16 = _qk_prep(px, q_norm_w, cos, sin, t_rows, 0, AQ, HQ, 4, seq, te, "q_prep")
    kx16 = _qk_prep(px, k_norm_w, cos, sin, t_rows, 0, AK, HKV, HKV, seq, te, "k_prep")
    kc16 = _qk_prep(px, k_norm_w, None, None, c_rows, t_rows, AK, HKV, HKV, seq, tm, "kc_prep")
    o_att, yatt16, lse = _att_fwd(q16, kx16, kc16, px, nb, seq, cx, te)
    w_o_ret16, w_o_att16, w_out16 = get_w_o(lse)
    yret16, a_ret, a_att, y16, dxn, dout16, dgate, loss_b = _merge_out(
        o_f, o_b, yatt16, px, w_o_ret16, w_o_att16, w_out16, x2, tgt, mod3, nb, seq, tm)

    gw_out = _matmul(y16, dout16, ta=True, tm=D, tn=D, tk=D, out_dtype=BF16, name="gw_out")
    da_ret16, da_att16, dmr16, dma16, do16, drg16, dao, dag16 = _bwd_branches(
        dout16, w_out16, w_o_ret16, w_o_att16, px, a_ret, a_att, o_f, o_b, o_att, tm)
    gw_o_ret = _matmul(yret16, da_ret16, ta=True, tm=D, tn=D, tk=D, out_dtype=BF16, name="gw_o_ret")
    gw_o_att = _matmul(yatt16, da_att16, ta=True, tm=D, tn=D, tk=D, out_dtype=BF16, name="gw_o_att")
    out_state, out_started = on_out_grads([gw_o_ret, gw_o_att, gw_out])
    daq16, gq, dkx, dvx, dkc, dvc = _att_bwd(q16, kx16, kc16, px, dao, o_att, lse, q_norm_w, cos, sin, nb, seq, cx, te,
                                             after=out_started)
    dak16, gk_lat = _qk_prep_bwd(dkx.reshape(t_rows, HKV * HD), px, k_norm_w, cos, sin, t_rows, 0, AK, HKV, HKV, seq, te,
                                 "k_prep_bwd")
    dcak16, gk_ctx = _qk_prep_bwd(dkc.reshape(c_rows, HKV * HD), px, k_norm_w, None, None, c_rows, t_rows, AK, HKV, HKV,
                                  seq, tm, "kc_prep_bwd")
    dq_f, dk_f, dv_f, dq_b, dk_b, dv_b, dck16, dcv16, dlg_scan = _ret_bwd(px, lg, do16, hist_f, hist_b, nb, nc, cx)
    dp_all = _assemble_lat(rows_all, dk_f, dk_b, dv_f, dv_b, dak16, dvx.reshape(t_rows, HKV * HD), dq_f, dq_b, drg16,
                           daq16, dag16, dmr16, dma16, tm)
    dp_all = _assemble_ctx(dp_all, dck16, dcv16, dcak16, dvc.reshape(c_rows, HKV * HD), t_rows, tm)
    gw_in_t = _matmul(dp_all, h_all, ta=True, tm=1536, tn=D, tk=2304, out_dtype=BF16, name="gw_in")
    in_state, in_started = on_in_grad(gw_in_t)
    dh = proj_back(dp_all, in_started)
    grad_x, dsh, dsc, gnw_lat = _norm_bwd(dh, x2, mod3, norm_w, dxn, 0, seq, 0, te, "norm_bwd")
    dsh_c, dsc_c, gnw_ctx = _norm_bwd(dh, ctx2, mod3, norm_w, None, t_rows, c_rows, nb, tm, "norm_bwd_ctx")

    dlg = jnp.sum(dlg_scan[:, :, 0], axis=0).reshape(1, 2 * RH)
    misc = jnp.concatenate([gq, gk_lat + gk_ctx, dlg, jnp.sum(loss_b[:, 0, 0]).reshape(1, 1),
                            jnp.zeros((1, D - 2 * HD - 2 * RH - 1), F32)], axis=1)
    rows = []
    for b in range(nb):
        rows += [dsh[b], dsc[b], dgate[b]]
    rows += [dsh_c[0], dsc_c[0]] + [c[b:b + 1] for b in range(nb)] + [gnw_lat + gnw_ctx, misc]
    payload = jnp.concatenate(rows + [jnp.zeros((PAY_ROWS - len(rows), D), F32)], axis=0)
    return grad_x.reshape(nb, seq, D), out_state, in_state, payload


def _finish_small(gathered, nb, c_ctx, ret_log2_decay, w_ada16, dev):
    n_dev = gathered.shape[0]
    loc = 3 * D // n_dev
    dmod_all = gathered[:, :3 * nb].reshape(n_dev * nb, 3 * D)
    dmodc_parts = jnp.concatenate([gathered[:, 3 * nb:3 * nb + 2].reshape(n_dev, 2 * D), jnp.zeros((n_dev, D), F32)], axis=1)
    c_all = gathered[:, 3 * nb + 2:4 * nb + 2].reshape(n_dev * nb, D)
    nw_parts = gathered[:, 4 * nb + 2]
    misc_parts = gathered[:, 4 * nb + 3]
    n_rows = n_dev * nb + n_dev
    pad = (-n_rows) % 16
    c_rows = jnp.concatenate([c_all, jnp.broadcast_to(c_ctx.reshape(1, D), (n_dev, D)), jnp.zeros((pad, D), F32)], axis=0)
    dm_rows = jnp.concatenate([dmod_all, dmodc_parts, jnp.zeros((pad, 3 * D), F32)], axis=0)
    dm_loc_rows = lax.dynamic_slice_in_dim(dm_rows, dev * loc, loc, axis=1)
    r_pad = jnp.full((1, D), -1.0, F32).at[:, 2 * HD:2 * HD + 2 * RH].set(ret_log2_decay.reshape(1, 2 * RH))
    gb, gc, gnw, misc, gwa = _small_final(dmod_all, dmodc_parts, c_rows, dm_loc_rows, nw_parts, misc_parts,
                                          c_ctx.reshape(1, D), r_pad, w_ada16)
    return (gb, gc, gnw, misc[:, :HD], misc[:, HD:2 * HD], misc[:, 2 * HD:2 * HD + 2 * RH], gwa,
            misc[0, 2 * HD + 2 * RH])


def kernel(x, c, ctx, c_ctx, norm_w, w_ada, b_ada, w_in, ret_log2_decay, q_norm_w, k_norm_w, w_o_ret, w_o_att, w_out, loss_target, m_c_ctx, m_norm_w, m_w_ada, m_b_ada, m_w_in, m_ret_log2_decay, m_q_norm_w, m_k_norm_w, m_w_o_ret, m_w_o_att, m_w_out, v_c_ctx, v_norm_w, v_w_ada, v_b_ada, v_w_in, v_ret_log2_decay, v_q_norm_w, v_k_norm_w, v_w_o_ret, v_w_o_att, v_w_out):
    nb = x.shape[0]
    mx, my, mc = _mesh_pos()
    dev = 4 * mx + 2 * my + mc
    core = jnp.reshape(mc, (1,)).astype(jnp.int32)
    chip = jnp.reshape(2 * mx + my, (1,)).astype(jnp.int32)

    n_loc = 3 * D // N_DEV
    c8 = jnp.zeros((8, D), F32).at[:nb].set(c).at[nb].set(c_ctx)
    c_land = lax.dynamic_update_slice(lax.empty((N_DEV, 8, D), F32), c8[None], (dev, 0, 0))
    c_state, c_token = _exchange_start([c8[None]], [c_land], _bcast_routes(1), "gather_c_start")
    w_in_t = jnp.transpose(w_in[0])
    in_shard = w_in_t.astype(BF16)
    groups = lax.dynamic_update_slice(lax.empty((N_DEV,) + in_shard.shape, BF16), in_shard[None], (mc, 0, 0))
    groups = _pair_fill(groups, (0,), "gather_in_pair", after=(c_token,))
    _, (c_all,) = _exchange_wait(c_state, groups, "gather_c_wait")
    ada_shard = w_ada[0].astype(BF16)
    b_loc = lax.dynamic_slice(b_ada, (0, dev * n_loc), (1, n_loc))
    mod_cols = _mod_part(c_all.reshape(N_DEV * 8, D), ada_shard, b_loc)
    (mod_all,) = _all_gather([mod_cols], "gather_mod")
    mod = jnp.transpose(lax.dynamic_slice(mod_all, (0, dev * 8, 0), (N_DEV, 8, n_loc)), (1, 0, 2)).reshape(8, 3 * D)
    ada_land = lax.dynamic_update_slice(lax.empty((N_DEV,) + ada_shard.shape, BF16), ada_shard[None], (dev, 0, 0))

    (near_send, near_recv, near_bufs, near_routes), gin_token = _exchange_start(
        [in_shard[None]], [groups], _group_routes((1, 2)), "gather_in_start", after=(mod_all,))
    w_in_groups, wo_states, ada_states = [], [], []
    wo_shards = [w_[0].astype(BF16) for w_ in (w_o_ret, w_o_att, w_out)]
    wo_lands = [lax.dynamic_update_slice(lax.empty((N_DEV,) + s_.shape, BF16), s_[None], (dev, 0, 0)) for s_ in wo_shards]

    def _state(send, recv, src, groups, routes):
        return send, recv, [src, groups], routes

    def proj_in(h_all):
        src, groups = near_bufs
        px = _in_proj_group(h_all, groups, 0, 1, chip, None, (gin_token,), "in_proj_0")
        (src,), (groups,) = _exchange_wait(_state(near_send, near_recv, src, groups, near_routes), px,
                                           "gather_in_wait_near")
        groups = _pair_fill(groups, (1, 2), "gather_in_fill_near")
        (far_send, far_recv, (src, groups), far_routes), far_token = _exchange_start(
            [src], [groups], _group_routes((3,)), "gather_in_start_far")
        wo_state, wo_token = _exchange_start([s_[None] for s_ in wo_shards], wo_lands, _bcast_routes(3),
                                             "gather_wo_start", after=(far_token,))
        wo_states.append(wo_state)
        ada_state, ada_token = _exchange_start([ada_shard[None]], [ada_land], _bcast_routes(1), "gather_ada_start",
                                               after=(wo_token,))
        ada_states.append(ada_state)
        px = _in_proj_group(h_all, groups, 1, 2, chip, px, (ada_token,), "in_proj_near")
        (src,), (groups,) = _exchange_wait(_state(far_send, far_recv, src, groups, far_routes), px,
                                           "gather_in_wait_far")
        groups = _pair_fill(groups, (3,), "gather_in_fill_far")
        px = _in_proj_group(h_all, groups, 3, 1, chip, px, (), "in_proj_far")
        w_in_groups.append(groups)
        return px

    def proj_back(dp_all, after):
        return _d_h_groups(dp_all, w_in_groups[0], chip, after)

    def get_w_o(after):
        _, (l_ret, l_att, l_out) = _exchange_wait(wo_states[0], after, "gather_wo_wait")
        return l_ret.reshape(RH * DV, D), l_att.reshape(D, D), l_out.reshape(D, D)

    def on_out_grads(grads):
        parts = [g_.reshape(N_DEV, g_.shape[0] // N_DEV, D) for g_ in grads]
        state, token = _reduce_scatter_start(parts, core, "rs_out")
        return state, (token,)

    def on_in_grad(grad):
        state, token = _reduce_scatter_start([grad.reshape(N_DEV, IN_COLS // N_DEV, D)], core, "rs_in")
        return state, (token,)

    grad_x, out_state, in_state, payload = _local_step(
        x, c, ctx, norm_w, ret_log2_decay, q_norm_w, k_norm_w, loss_target,
        mod, proj_in, proj_back, get_w_o, on_out_grads, on_in_grad, started=(gin_token,))

    pay_land = lax.dynamic_update_slice(lax.empty((N_DEV,) + payload.shape, F32), payload[None], (dev, 0, 0))
    pay_state, pay_token = _exchange_start([payload[None]], [pay_land], _bcast_routes(1), "gather_small_start")

    out_res = _reduce_scatter_finish(out_state, pay_token, chip,
                                     [(w_[0], m_[0], v_[0]) for w_, m_, v_ in ((w_o_ret, m_w_o_ret, v_w_o_ret),
                                                                                (w_o_att, m_w_o_att, v_w_o_att),
                                                                                (w_out, m_w_out, v_w_out))], "rs_out")
    (in_res,) = _reduce_scatter_finish(in_state, out_res[0][0], chip,
                                       [(w_in_t, jnp.transpose(m_w_in[0]), jnp.transpose(v_w_in[0]))], "rs_in")

    _, (gathered,) = _exchange_wait(pay_state, in_res[0], "gather_small_wait")
    _, (l_ada,) = _exchange_wait(ada_states[0], gathered, "gather_ada_wait")
    w_ada16 = jnp.transpose(l_ada, (1, 0, 2)).reshape(D, 3 * D)
    gb, gc, gnw, gq, gk, gr, gwa, loss = _finish_small(gathered, nb, c_ctx, ret_log2_decay, w_ada16, dev)
    big = {4: [jnp.transpose(r)[None] for r in in_res]}
    for i, res in zip((8, 9, 10), out_res):
        big[i] = [r[None] for r in res]
    small_g = {0: gc.reshape(c_ctx.shape), 1: gnw, 2: gwa[None], 3: gb, 5: gr.reshape(ret_log2_decay.shape), 6: gq, 7: gk}
    weights = [c_ctx, norm_w, w_ada, b_ada, w_in, ret_log2_decay, q_norm_w, k_norm_w, w_o_ret, w_o_att, w_out]
    ms = [m_c_ctx, m_norm_w, m_w_ada, m_b_ada, m_w_in, m_ret_log2_decay, m_q_norm_w, m_k_norm_w, m_w_o_ret, m_w_o_att, m_w_out]
    vs = [v_c_ctx, v_norm_w, v_w_ada, v_b_ada, v_w_in, v_ret_log2_decay, v_q_norm_w, v_k_norm_w, v_w_o_ret, v_w_o_att, v_w_out]
    grads, deltas, new_ms, new_vs = [], [], [], []
    for i, (w, m, v) in enumerate(zip(weights, ms, vs)):
        if i in big:
            res = big[i]
        else:
            shape2 = (-1, w.shape[-1])
            g = small_g[i]
            res = [g] + [r.reshape(w.shape) for r in _adamw(w.reshape(shape2), g.reshape(shape2), m.reshape(shape2),
                                                             v.reshape(shape2), "adamw_%d" % i)]
        for lst, r in zip((grads, deltas, new_ms, new_vs), res):
            lst.append(r)
    return (loss, grad_x, *grads, *deltas, *new_ms, *new_vs)
```

```python
import numpy as np
import jax
import jax.numpy as jnp
from jax import lax
from jax.experimental import pallas as pl
from jax.experimental.pallas import tpu as pltpu

F32 = jnp.float32
BF16 = jnp.bfloat16

D = 1024
RH, DK, DV, CH = 4, 256, 512, 256
HQ, HKV, HD = 8, 2, 128
GRID_W = 64
ROPE_THETA = 10000.0
EPS = 1e-6
RK, RV, AK, AV, RQ, RG, AQ, AG, MR, MA = 0, 1024, 3072, 3328, 3584, 4608, 6656, 7680, 8704, 9728
IN_COLS = 10752
KV_COLS = 3584
N_DEV = 8
LR, B1, B2, ADAM_EPS, WD, STEP = 0.001, 0.9, 0.999, 1e-08, 0.01, 10
PAY_ROWS = 16
VMEM_LIMIT = 56 * 1024 * 1024
MESH_T = pl.DeviceIdType.MESH

NT = (((1,), (1,)), ((), ()))
TN = (((0,), (0,)), ((), ()))
SM_C = (HD ** -0.5) * float(np.log2(np.e))


def _params(sem):
    return pltpu.CompilerParams(dimension_semantics=sem, vmem_limit_bytes=VMEM_LIMIT)


def _pick(n, target, mult=8):
    best = None
    for t in range(mult, min(n, target) + 1, mult):
        if n % t == 0:
            best = t
    return best or n


def _dot(a, b, dn=None):
    if dn is None:
        return jnp.dot(a, b, preferred_element_type=F32)
    return lax.dot_general(a, b, dn, preferred_element_type=F32)


def _sig(v):
    return jax.nn.sigmoid(v)


def _silu(v):
    return v * _sig(v)


def _dsilu(v):
    s = _sig(v)
    return s * (1.0 + v * (1.0 - s))


def _sds(shape, dtype):
    return jax.ShapeDtypeStruct(shape, dtype)


def _matmul(a, b, *, ta=False, tb=False, tm, tn, tk, out_dtype, name, after=()):
    m = a.shape[1] if ta else a.shape[0]
    kdim = a.shape[0] if ta else a.shape[1]
    n = b.shape[0] if tb else b.shape[1]
    tm, tn, tk = _pick(m, tm, 128), _pick(n, tn, 128), _pick(kdim, tk, 128)
    nk = kdim // tk
    dn = (((0 if ta else 1,), (1 if tb else 0,)), ((), ()))

    def body(a_ref, b_ref, *rest):
        o_ref, acc_ref = rest[-2:]
        k = pl.program_id(2)
        part = _dot(a_ref[...].astype(BF16), b_ref[...].astype(BF16), dn)
        if nk == 1:
            o_ref[...] = part.astype(o_ref.dtype)
        else:
            @pl.when(k == 0)
            def _():
                acc_ref[...] = part

            @pl.when(k > 0)
            def _():
                acc_ref[...] += part

            @pl.when(k == nk - 1)
            def _():
                o_ref[...] = acc_ref[...].astype(o_ref.dtype)

    a_spec = pl.BlockSpec((tk, tm), lambda i, j, k: (k, i)) if ta else pl.BlockSpec((tm, tk), lambda i, j, k: (i, k))
    b_spec = pl.BlockSpec((tn, tk), lambda i, j, k: (j, k)) if tb else pl.BlockSpec((tk, tn), lambda i, j, k: (k, j))
    return pl.pallas_call(
        body, name=name, grid=(m // tm, n // tn, nk),
        in_specs=[a_spec, b_spec] + [pl.BlockSpec(memory_space=pl.ANY)] * len(after),
        out_specs=pl.BlockSpec((tm, tn), lambda i, j, k: (i, j)), out_shape=_sds((m, n), out_dtype),
        scratch_shapes=[pltpu.VMEM((tm, tn) if nk > 1 else (8, 128), F32)],
        compiler_params=_params(("parallel", "parallel", "arbitrary")),
    )(a, b, *after)


def _log_gamma(r):
    rp = jnp.full((8, 128), -1.0, F32).at[:2, :RH].set(r.reshape(2, RH))

    def body(r_ref, o_ref):
        o_ref[...] = jnp.log1p(-jnp.exp2(r_ref[...]))

    out = pl.pallas_call(body, name="log_gamma", out_shape=_sds((8, 128), F32))(rp)
    return out[:2, :RH]


def _mod_part(c_rows, w_ada_loc16, b_loc):
    def body(c_ref, w_ref, b_ref, o_ref):
        o_ref[...] = _dot(_silu(c_ref[...]).astype(BF16), w_ref[...]) + b_ref[...]

    return pl.pallas_call(
        body, name="mod_part", out_shape=_sds((c_rows.shape[0], w_ada_loc16.shape[1]), F32),
    )(c_rows, w_ada_loc16, b_loc)


def _norm_fwd(x2, mod3, norm_w, rows_all, row_off, rows_per_group, group0, h_prev, tm, name, after=()):
    rows = x2.shape[0]
    rb0 = row_off // tm
    bpg = rows_per_group // tm

    def body(*refs):
        x_ref, sh_ref, sc_ref, nw_ref, o_ref = refs[-5:]
        xv = x_ref[...]
        r = lax.rsqrt(jnp.mean(xv * xv, axis=-1, keepdims=True) + EPS)
        o_ref[...] = ((xv * r) * nw_ref[...] * (1.0 + sc_ref[...]) + sh_ref[...]).astype(BF16)

    in_specs = [pl.BlockSpec((tm, D), lambda i: (i, 0)),
                pl.BlockSpec((None, 1, D), lambda i: (group0 + i // bpg, 0, 0)),
                pl.BlockSpec((None, 1, D), lambda i: (group0 + i // bpg, 0, 1)),
                pl.BlockSpec((1, D), lambda i: (0, 0))]
    in_specs = [pl.BlockSpec(memory_space=pl.ANY)] * len(after) + in_specs
    args = list(after) + [x2, mod3, mod3, norm_w]
    alias = {}
    if h_prev is not None:
        in_specs.insert(0, pl.BlockSpec(memory_space=pl.ANY))
        args.insert(0, h_prev)
        alias = {0: 0}
    return pl.pallas_call(
        body, name=name, grid=(rows // tm,), in_specs=in_specs,
        out_specs=pl.BlockSpec((tm, D), lambda i: (rb0 + i, 0)), out_shape=_sds((rows_all, D), BF16),
        input_output_aliases=alias, compiler_params=_params(("parallel",)),
    )(*args)


def _decays(lg, fwd):
    ii = lax.broadcasted_iota(jnp.int32, (CH, CH), 0)
    jj = lax.broadcasted_iota(jnp.int32, (CH, CH), 1)
    ri = lax.broadcasted_iota(jnp.int32, (CH, 1), 0).astype(F32)
    rel = (ii - jj) if fwd else (jj - ii)
    relf = jnp.maximum(rel, 0).astype(F32)
    mask = jnp.where(rel >= 0, jnp.exp(lg * relf), 0.0)
    qe = (ri + 1.0) if fwd else (CH - ri)
    ke = (CH - 1.0 - ri) if fwd else ri
    return mask, relf, jnp.exp(lg * qe), qe, jnp.exp(lg * ke), ke


def _wide_specs(rowf):
    return [pl.BlockSpec((CH, 2 * DK), lambda b, c: (rowf(b, c), RQ // (2 * DK))),
            pl.BlockSpec((CH, 2 * DK), lambda b, c: (rowf(b, c), RQ // (2 * DK) + 1)),
            pl.BlockSpec((CH, RH * DK), lambda b, c: (rowf(b, c), RK // (RH * DK))),
            pl.BlockSpec((CH, 2 * DV), lambda b, c: (rowf(b, c), RV // (2 * DV))),
            pl.BlockSpec((CH, 2 * DV), lambda b, c: (rowf(b, c), RV // (2 * DV) + 1))]


def _head_qkv(refs, h):
    q0, q1, k, v0, v1 = refs
    lo = h % 2
    q = (q0, q1)[h // 2][:, lo * DK:(lo + 1) * DK].astype(F32)
    kk = k[:, h * DK:(h + 1) * DK].astype(F32) * (DK ** -0.5)
    v16 = (v0, v1)[h // 2][:, lo * DV:(lo + 1) * DV].astype(BF16)
    return q, kk, v16


def _ctx_specs(t_rows, cx):
    rb = t_rows // cx
    return [pl.BlockSpec((cx, RH * DK), lambda b, c: (rb + b, RK // (RH * DK))),
            pl.BlockSpec((cx, 2 * DV), lambda b, c: (rb + b, RV // (2 * DV))),
            pl.BlockSpec((cx, 2 * DV), lambda b, c: (rb + b, RV // (2 * DV) + 1))]


def _ctx_kv(refs, h):
    k, v0, v1 = refs
    kk = k[:, h * DK:(h + 1) * DK].astype(F32) * (DK ** -0.5)
    lo = h % 2
    return kk, (v0, v1)[h // 2][:, lo * DV:(lo + 1) * DV].astype(BF16)


def _ret_fwd(px, lg, nb, nc, cx):
    t_rows = nb * nc * CH

    def body(lg_ref, *refs):
        ins = (refs[0:5], refs[5:10])
        ctx_refs = refs[10:13]
        of_ref, ob_ref, hf_ref, hb_ref, sf, sb = refs[13:]
        c = pl.program_id(1)

        @pl.when(c == 0)
        def _():
            pos = lax.broadcasted_iota(jnp.int32, (cx, 1), 0).astype(F32)
            for h in range(RH):
                k, v16 = _ctx_kv(ctx_refs, h)
                sf[h] = _dot((k * jnp.exp(lg_ref[0, h] * (cx - 1.0 - pos))).astype(BF16), v16, TN)
                sb[h] = _dot((k * jnp.exp(lg_ref[1, h] * pos)).astype(BF16), v16, TN)

        for d, (o_ref, h_ref, s) in enumerate(((of_ref, hf_ref, sf), (ob_ref, hb_ref, sb))):
            for h in range(RH):
                lg_d = lg_ref[d, h]
                mask, _, qd, _, kd, _ = _decays(lg_d, d == 0)
                q, k, v16 = _head_qkv(ins[d], h)
                a = _dot(q.astype(BF16), k.astype(BF16), NT)
                st = s[h]
                st16 = st.astype(BF16)
                h_ref[h] = st16
                o = _dot((a * mask).astype(BF16), v16) + _dot((q * qd).astype(BF16), st16)
                o_ref[:, h * DV:(h + 1) * DV] = o.astype(BF16)
                s[h] = st * jnp.exp(lg_d * CH) + _dot((k * kd).astype(BF16), v16, TN)

    def fw(b, c):
        return b * nc + c

    def bw(b, c):
        return b * nc + nc - 1 - c

    in_specs = [pl.BlockSpec(memory_space=pltpu.SMEM)] + _wide_specs(fw) + _wide_specs(bw) + _ctx_specs(t_rows, cx)
    out_specs = [pl.BlockSpec((CH, RH * DV), lambda b, c: (fw(b, c), 0)),
                 pl.BlockSpec((CH, RH * DV), lambda b, c: (bw(b, c), 0)),
                 pl.BlockSpec((None, None, RH, DK, DV), lambda b, c: (b, c, 0, 0, 0)),
                 pl.BlockSpec((None, None, RH, DK, DV), lambda b, c: (b, nc - 1 - c, 0, 0, 0))]
    return pl.pallas_call(
        body, name="ret_fwd", grid=(nb, nc), in_specs=in_specs, out_specs=out_specs,
        out_shape=[_sds((t_rows, RH * DV), BF16)] * 2 + [_sds((nb, nc, RH, DK, DV), BF16)] * 2,
        scratch_shapes=[pltpu.VMEM((RH, DK, DV), F32), pltpu.VMEM((RH, DK, DV), F32)],
        compiler_params=_params(("parallel", "arbitrary")),
    )(lg, *([px] * 13))


def _rope_tables(seq):
    rows = seq // GRID_W
    row = np.repeat(np.arange(rows, dtype=np.float32), GRID_W)
    col = np.tile(np.arange(GRID_W, dtype=np.float32), rows)
    half = HD // 2
    freqs = (ROPE_THETA ** (-np.arange(0, half, 2, dtype=np.float32) / half)).astype(np.float32)
    ang = np.concatenate([row[:, None] * freqs, col[:, None] * freqs], axis=-1).astype(np.float32)
    cos = np.repeat(np.cos(ang), 2, axis=-1).astype(np.float32)
    sin = np.repeat(np.sin(ang), 2, axis=-1).astype(np.float32)
    sign = np.tile(np.array([-1.0, 1.0], np.float32), HD // 2)
    return jnp.asarray(cos), jnp.asarray(sin * sign)


def _swap_pairs(v):
    lane = lax.broadcasted_iota(jnp.int32, v.shape, 1)
    return jnp.where((lane & 1) == 0, pltpu.roll(v, HD - 1, 1), pltpu.roll(v, 1, 1))


def _qk_prep(px, nw, cos, sin, rows, row_off, col_off, heads, hb, seq, tm, name):
    rope = cos is not None
    rb0 = row_off // tm
    pb = seq // tm if rope else 1
    bw = hb * HD

    def body(*refs):
        if rope:
            x_ref, w_ref, c_ref, s_ref, o_ref = refs
        else:
            x_ref, w_ref, o_ref = refs
        for h in range(hb):
            sl = slice(h * HD, (h + 1) * HD)
            xv = x_ref[:, sl].astype(F32)
            r = lax.rsqrt(jnp.mean(xv * xv, axis=-1, keepdims=True) + EPS)
            t = (xv * r) * w_ref[...]
            if rope:
                t = t * c_ref[...] + _swap_pairs(t) * s_ref[...]
            o_ref[:, sl] = t.astype(BF16)

    in_specs = [pl.BlockSpec((tm, bw), lambda i, j: (rb0 + i, col_off // bw + j)),
                pl.BlockSpec((1, HD), lambda i, j: (0, 0))]
    args = [px, nw]
    if rope:
        in_specs += [pl.BlockSpec((tm, HD), lambda i, j: (i % pb, 0))] * 2
        args += [cos, sin]
    return pl.pallas_call(
        body, name=name, grid=(rows // tm, heads // hb), in_specs=in_specs,
        out_specs=pl.BlockSpec((tm, bw), lambda i, j: (i, j)), out_shape=_sds((rows, heads * HD), BF16),
        compiler_params=_params(("parallel", "parallel")),
    )(*args)


def _att_fwd(q16, kx16, kc16, px, nb, seq, cx, tq):
    t_rows = nb * seq
    nq = seq // tq
    rep = HQ // HKV
    gw = rep * HD

    def body(q_ref, kx_ref, kc_ref, vx_ref, vc_ref, g_ref, o_ref, y_ref, l_ref):
        kx = kx_ref[...]
        kc = kc_ref[...]
        vx = vx_ref[...].astype(BF16)
        vc = vc_ref[...].astype(BF16)
        l_ref[...] = jnp.zeros_like(l_ref)
        for r in range(rep):
            sl = slice(r * HD, (r + 1) * HD)
            q = q_ref[:, sl]
            s1 = _dot(q, kx, NT)
            s2 = _dot(q, kc, NT)
            m = jnp.maximum(jnp.max(s1, axis=-1, keepdims=True), jnp.max(s2, axis=-1, keepdims=True))
            e1 = jnp.exp2((s1 - m) * SM_C)
            e2 = jnp.exp2((s2 - m) * SM_C)
            tot = jnp.sum(e1, axis=-1, keepdims=True) + jnp.sum(e2, axis=-1, keepdims=True)
            o = (_dot(e1.astype(BF16), vx) + _dot(e2.astype(BF16), vc)) * (1.0 / tot)
            o_ref[:, sl] = o
            y_ref[:, sl] = (o * _silu(g_ref[:, sl].astype(F32))).astype(BF16)
            l_ref[:, r:r + 1] = m * SM_C + jnp.log(tot) * float(np.log2(np.e))

    qblk = pl.BlockSpec((tq, gw), lambda b, g, i: (b * nq + i, g))
    return pl.pallas_call(
        body, name="att_fwd", grid=(nb, HKV, nq),
        in_specs=[qblk,
                  pl.BlockSpec((seq, HD), lambda b, g, i: (b, g)),
                  pl.BlockSpec((cx, HD), lambda b, g, i: (b, g)),
                  pl.BlockSpec((seq, HD), lambda b, g, i: (b, AV // HD + g)),
                  pl.BlockSpec((cx, HD), lambda b, g, i: (t_rows // cx + b, AV // HD + g)),
                  pl.BlockSpec((tq, gw), lambda b, g, i: (b * nq + i, AG // gw + g))],
        out_specs=[qblk, qblk, pl.BlockSpec((tq, 128), lambda b, g, i: (b * nq + i, g))],
        out_shape=[_sds((t_rows, D), F32), _sds((t_rows, D), BF16), _sds((t_rows, HKV * 128), F32)],
        compiler_params=_params(("parallel", "parallel", "parallel")),
    )(q16, kx16, kc16, px, px, px)


def _gate_specs(tm, col0):
    hw = D // 2
    return [pl.BlockSpec((tm, hw), lambda i: (i, col0 // hw)), pl.BlockSpec((tm, hw), lambda i: (i, col0 // hw + 1))]


def _merge_out(o_f, o_b, yatt16, px, w_o_ret16, w_o_att16, w_out16, x2, tgt, mod3, nb, seq, tm):
    t_rows = nb * seq
    bpb = seq // tm
    hw = D // 2

    def body(of_ref, ob_ref, g0, g1, g2, g3, wr_ref, ya_ref, wa_ref, mr0, mr1, ma0, ma1, wo_ref, x_ref, t_ref, gt_ref,
             yr_ref, ar_ref, aa_ref, y_ref, dxn_ref, dout_ref, dg_ref, loss_ref):
        i = pl.program_id(1)
        for h, g_ref in enumerate((g0, g1, g2, g3)):
            sl = slice(h * DV, (h + 1) * DV)
            o = of_ref[:, sl].astype(F32) + ob_ref[:, sl].astype(F32)
            r = lax.rsqrt(jnp.mean(o * o, axis=-1, keepdims=True) + EPS)
            yr_ref[:, sl] = ((o * r) * _silu(g_ref[...].astype(F32))).astype(BF16)
        ar = _dot(yr_ref[...], wr_ref[...])
        aa = _dot(ya_ref[...], wa_ref[...])
        ar_ref[...] = ar.astype(BF16)
        aa_ref[...] = aa.astype(BF16)
        for j, (mr_ref, ma_ref) in enumerate(((mr0, ma0), (mr1, ma1))):
            sl = slice(j * hw, (j + 1) * hw)
            y_ref[:, sl] = (_sig(mr_ref[...].astype(F32)) * ar[:, sl]
                            + _sig(ma_ref[...].astype(F32)) * aa[:, sl]).astype(BF16)
        out = _dot(y_ref[...], wo_ref[...])
        gate = gt_ref[...]
        diff = x_ref[...] + gate * out - t_ref[...]
        dxn = diff * (1.0 / D)
        dxn_ref[...] = dxn
        dout_ref[...] = (gate * dxn).astype(BF16)
        dg = jnp.sum(dxn * out, axis=0, keepdims=True)
        ls = jnp.broadcast_to(jnp.sum(diff * diff) * (0.5 / D), (1, 128))

        @pl.when(i == 0)
        def _():
            dg_ref[...] = dg
            loss_ref[...] = ls

        @pl.when(i > 0)
        def _():
            dg_ref[...] += dg
            loss_ref[...] += ls

    def cols(width, col0):
        return pl.BlockSpec((tm, width), lambda b, i: (b * bpb + i, col0 // width))

    def whole(rows):
        return pl.BlockSpec((rows, D), lambda b, i: (0, 0))

    row, wide = cols(D, 0), cols(RH * DV, 0)
    gates = [cols(DV, RG + h * DV) for h in range(RH)]
    merge_gates = [cols(hw, MR), cols(hw, MR + hw), cols(hw, MA), cols(hw, MA + hw)]
    return pl.pallas_call(
        body, name="merge_out", grid=(nb, bpb),
        in_specs=[wide, wide] + gates + [whole(RH * DV), row, whole(D)] + merge_gates
        + [whole(D), row, row, pl.BlockSpec((None, 1, D), lambda b, i: (b, 0, 2))],
        out_specs=[wide, row, row, row, row, row, pl.BlockSpec((None, 1, D), lambda b, i: (b, 0, 0)),
                   pl.BlockSpec((None, 1, 128), lambda b, i: (b, 0, 0))],
        out_shape=[_sds((t_rows, RH * DV), BF16)] + [_sds((t_rows, D), BF16)] * 3
        + [_sds((t_rows, D), F32), _sds((t_rows, D), BF16), _sds((nb, 1, D), F32), _sds((nb, 1, 128), F32)],
        compiler_params=_params(("parallel", "arbitrary")),
    )(o_f, o_b, *([px] * RH), w_o_ret16, yatt16, w_o_att16, px, px, px, px, w_out16, x2, tgt, mod3)


def _bwd_branches(dout16, w_out16, w_o_ret16, w_o_att16, px, a_ret, a_att, o_f, o_b, o_att, rows_all, tm):
    t_rows = dout16.shape[0]
    hw = D // 2

    def body(do_ref, wo_ref, wr_ref, wa_ref, mr0, mr1, ma0, ma1, ar_ref, aa_ref, rg0, rg1, rg2, rg3, of_ref, ob_ref,
             ag0, ag1, oa_ref, dar_ref, daa_ref, dor_ref, dao_ref, dp_ref):
        dy_all = _dot(do_ref[...], wo_ref[...], NT)
        for j, (mr_ref, ma_ref) in enumerate(((mr0, ma0), (mr1, ma1))):
            sl = slice(j * hw, (j + 1) * hw)
            dy = dy_all[:, sl]
            sr = _sig(mr_ref[...].astype(F32))
            sa = _sig(ma_ref[...].astype(F32))
            dar_ref[:, sl] = (dy * sr).astype(BF16)
            daa_ref[:, sl] = (dy * sa).astype(BF16)
            dp_ref[:, MR - RG + j * hw:MR - RG + (j + 1) * hw] = (
                dy * ar_ref[:, sl].astype(F32) * sr * (1.0 - sr)).astype(BF16)
            dp_ref[:, MA - RG + j * hw:MA - RG + (j + 1) * hw] = (
                dy * aa_ref[:, sl].astype(F32) * sa * (1.0 - sa)).astype(BF16)
        da_ret = dar_ref[...]
        for h, g_ref in enumerate((rg0, rg1, rg2, rg3)):
            sl = slice(h * DV, (h + 1) * DV)
            dy = _dot(da_ret, wr_ref[sl, :], NT)
            g = g_ref[...].astype(F32)
            o = of_ref[:, sl].astype(F32) + ob_ref[:, sl].astype(F32)
            r = lax.rsqrt(jnp.mean(o * o, axis=-1, keepdims=True) + EPS)
            on = o * r
            sg = _sig(g)
            don = dy * (g * sg)
            dp_ref[:, sl] = (dy * on * (sg * (1.0 + g * (1.0 - sg)))).astype(BF16)
            dor_ref[:, sl] = (r * (don - on * jnp.mean(on * don, axis=-1, keepdims=True))).astype(BF16)
        dy_all = _dot(daa_ref[...], wa_ref[...], NT)
        for j, g_ref in enumerate((ag0, ag1)):
            sl = slice(j * hw, (j + 1) * hw)
            dy = dy_all[:, sl]
            g = g_ref[...].astype(F32)
            sg = _sig(g)
            dao_ref[:, sl] = dy * (g * sg)
            dp_ref[:, AG - RG + j * hw:AG - RG + (j + 1) * hw] = (
                dy * oa_ref[:, sl] * (sg * (1.0 + g * (1.0 - sg)))).astype(BF16)

    def gate(h):
        return pl.BlockSpec((tm, DV), lambda i: (i, RG // DV + h))

    def whole(rows):
        return pl.BlockSpec((rows, D), lambda i: (0, 0))

    row = pl.BlockSpec((tm, D), lambda i: (i, 0))
    wide = pl.BlockSpec((tm, RH * DV), lambda i: (i, 0))
    return pl.pallas_call(
        body, name="bwd_branches", grid=(t_rows // tm,),
        in_specs=[row, whole(D), whole(RH * DV), whole(D)] + _gate_specs(tm, MR) + _gate_specs(tm, MA) + [row, row]
        + [gate(h) for h in range(RH)] + [wide, wide] + _gate_specs(tm, AG) + [row],
        out_specs=[row, row, wide, row,
                   pl.BlockSpec((pl.Element(tm), pl.Element(IN_COLS - RG)), lambda i: (i * tm, RG))],
        out_shape=[_sds((t_rows, D), BF16)] * 2 + [_sds((t_rows, RH * DV), BF16), _sds((t_rows, D), F32),
                                                  _sds((rows_all, IN_COLS), BF16)],
        compiler_params=_params(("parallel",)),
    )(dout16, w_out16, w_o_ret16, w_o_att16, px, px, px, px, a_ret, a_att, *([px] * RH), o_f, o_b, px, px, o_att)


def _att_bwd(q16, kx16, kc16, px, dao, o_att, lse, q_norm_w, cos, sin, dp_all, nb, seq, cx, tq, after=()):
    t_rows = nb * seq
    nq = seq // tq
    rep = HQ // HKV
    gw = rep * HD
    scale = HD ** -0.5

    def body(q_ref, kx_ref, kc_ref, vx_ref, vc_ref, dao_ref, o_ref, l_ref, xq_ref, w_ref, c_ref, s_ref, *rest):
        daq_ref, gq_ref, dkx_ref, dvx_ref, dkc_ref, dvc_ref = rest[1 + len(after):7 + len(after)]
        accs = rest[7 + len(after):]
        i = pl.program_id(2)
        first = jnp.logical_and(jnp.logical_and(pl.program_id(0) == 0, pl.program_id(1) == 0), i == 0)
        gq = jnp.zeros((1, HD), F32)
        kx = kx_ref[...]
        kc = kc_ref[...]
        vx = vx_ref[...].astype(BF16)
        vc = vc_ref[...].astype(BF16)
        @pl.when(i == 0)
        def _():
            for acc in accs:
                acc[...] = jnp.zeros_like(acc)

        dkx, dvx, dkc, dvc = [acc[...] for acc in accs]
        for r in range(rep):
            sl = slice(r * HD, (r + 1) * HD)
            q = q_ref[:, sl]
            lr = l_ref[:, r:r + 1]
            p1 = jnp.exp2(_dot(q, kx, NT) * SM_C - lr)
            p2 = jnp.exp2(_dot(q, kc, NT) * SM_C - lr)
            da = dao_ref[:, sl]
            da16 = da.astype(BF16)
            delta = jnp.sum(da * o_ref[:, sl], axis=-1, keepdims=True)
            ds1 = (p1 * (_dot(da16, vx, NT) - delta)).astype(BF16)
            ds2 = (p2 * (_dot(da16, vc, NT) - delta)).astype(BF16)
            dq = (_dot(ds1, kx) + _dot(ds2, kc)) * scale
            dkx += _dot(q, ds1, TN)
            dkc += _dot(q, ds2, TN)
            dvx += _dot(da16, p1.astype(BF16), TN)
            dvc += _dot(da16, p2.astype(BF16), TN)
            dt = dq * c_ref[...] + _swap_pairs(dq * s_ref[...])
            xv = xq_ref[:, sl].astype(F32)
            rn = lax.rsqrt(jnp.mean(xv * xv, axis=-1, keepdims=True) + EPS)
            xh = xv * rn
            dxh = dt * w_ref[...]
            daq_ref[:, sl] = (rn * (dxh - xh * jnp.mean(dxh * xh, axis=-1, keepdims=True))).astype(BF16)
            gq += jnp.sum(dt * xh, axis=0, keepdims=True)
        for acc, val in zip(accs, (dkx, dvx, dkc, dvc)):
            acc[...] = val

        @pl.when(first)
        def _():
            gq_ref[...] = gq

        @pl.when(jnp.logical_not(first))
        def _():
            gq_ref[...] += gq

        @pl.when(i == nq - 1)
        def _():
            dkx_ref[...] = dkx.T * scale
            dvx_ref[...] = dvx.T
            dkc_ref[...] = dkc.T * scale
            dvc_ref[...] = dvc.T

    qblk = pl.BlockSpec((tq, gw), lambda b, g, i: (b * nq + i, g))
    kxb = pl.BlockSpec((None, seq, HD), lambda b, g, i: (b, 0, g))
    kcb = pl.BlockSpec((None, cx, HD), lambda b, g, i: (b, 0, g))
    table = pl.BlockSpec((tq, HD), lambda b, g, i: (i, 0))
    one = pl.BlockSpec((1, HD), lambda b, g, i: (0, 0))
    return pl.pallas_call(
        body, name="att_bwd", grid=(nb, HKV, nq),
        in_specs=[qblk,
                  pl.BlockSpec((seq, HD), lambda b, g, i: (b, g)),
                  pl.BlockSpec((cx, HD), lambda b, g, i: (b, g)),
                  pl.BlockSpec((seq, HD), lambda b, g, i: (b, AV // HD + g)),
                  pl.BlockSpec((cx, HD), lambda b, g, i: (t_rows // cx + b, AV // HD + g)),
                  qblk, qblk, pl.BlockSpec((tq, 128), lambda b, g, i: (b * nq + i, g)),
                  pl.BlockSpec((tq, gw), lambda b, g, i: (b * nq + i, AQ // gw + g)), one, table, table]
        + [pl.BlockSpec(memory_space=pl.ANY)] * (1 + len(after)),
        out_specs=[pl.BlockSpec((tq, gw), lambda b, g, i: (b * nq + i, AQ // gw + g)), one, kxb, kxb, kcb, kcb],
        out_shape=[_sds(dp_all.shape, BF16), _sds((1, HD), F32), _sds((nb, seq, HKV * HD), F32),
                   _sds((nb, seq, HKV * HD), F32), _sds((nb, cx, HKV * HD), F32), _sds((nb, cx, HKV * HD), F32)],
        scratch_shapes=[pltpu.VMEM((HD, seq), F32), pltpu.VMEM((HD, seq), F32), pltpu.VMEM((HD, cx), F32),
                        pltpu.VMEM((HD, cx), F32)],
        input_output_aliases={12: 0},
        compiler_params=_params(("arbitrary", "arbitrary", "arbitrary")),
    )(q16, kx16, kc16, px, px, dao, o_att, lse, px, q_norm_w, cos, sin, dp_all, *after)


def _qk_prep_bwd(dt, px, nw, cos, sin, rows, row_off, col_off, heads, hb, seq, tm, name):
    rope = cos is not None
    rb0 = row_off // tm
    pb = seq // tm if rope else 1
    bw = hb * HD

    def body(*refs):
        if rope:
            d_ref, x_ref, w_ref, c_ref, s_ref, dx_ref, dw_ref = refs
        else:
            d_ref, x_ref, w_ref, dx_ref, dw_ref = refs
        first = jnp.logical_and(pl.program_id(0) == 0, pl.program_id(1) == 0)
        dw = jnp.zeros((1, HD), F32)
        for h in range(hb):
            sl = slice(h * HD, (h + 1) * HD)
            dtv = d_ref[:, sl]
            if rope:
                dtv = dtv * c_ref[...] + _swap_pairs(dtv * s_ref[...])
            xv = x_ref[:, sl].astype(F32)
            r = lax.rsqrt(jnp.mean(xv * xv, axis=-1, keepdims=True) + EPS)
            xh = xv * r
            dxh = dtv * w_ref[...]
            dx_ref[:, sl] = (r * (dxh - xh * jnp.mean(dxh * xh, axis=-1, keepdims=True))).astype(BF16)
            dw += jnp.sum(dtv * xh, axis=0, keepdims=True)

        @pl.when(first)
        def _():
            dw_ref[...] = dw

        @pl.when(jnp.logical_not(first))
        def _():
            dw_ref[...] += dw

    blk = pl.BlockSpec((tm, bw), lambda i, j: (i, j))
    in_specs = [blk, pl.BlockSpec((tm, bw), lambda i, j: (rb0 + i, col_off // bw + j)),
                pl.BlockSpec((1, HD), lambda i, j: (0, 0))]
    args = [dt, px, nw]
    if rope:
        in_specs += [pl.BlockSpec((tm, HD), lambda i, j: (i % pb, 0))] * 2
        args += [cos, sin]
    return pl.pallas_call(
        body, name=name, grid=(rows // tm, heads // hb), in_specs=in_specs,
        out_specs=[blk, pl.BlockSpec((1, HD), lambda i, j: (0, 0))],
        out_shape=[_sds((rows, heads * HD), BF16), _sds((1, HD), F32)],
        compiler_params=_params(("arbitrary", "arbitrary")),
    )(*args)


def _ret_bwd(px, lg, do16, hist_f, hist_b, nb, nc, cx):
    t_rows = nb * nc * CH

    def body(lg_ref, *refs):
        ins = (refs[0:5], refs[7:12])
        do_refs = (refs[5], refs[12])
        h_refs = (refs[6], refs[13])
        ctx_refs = refs[14:17]
        outs = (refs[17:20], refs[20:23])
        dck_ref, dcv_ref, dlg_ref = refs[23:26]
        dss = (refs[26], refs[27])
        c = pl.program_id(1)

        @pl.when(c == 0)
        def _():
            dss[0][...] = jnp.zeros_like(dss[0])
            dss[1][...] = jnp.zeros_like(dss[1])
            dlg_ref[...] = jnp.zeros_like(dlg_ref)

        for d in range(2):
            dq_ref, dk_ref, dv_ref = outs[d]
            for h in range(RH):
                lg_d = lg_ref[d, h]
                mask, relf, qd, qe, kd, ke = _decays(lg_d, d == 0)
                g_ch = jnp.exp(lg_d * CH)
                q, k, v16 = _head_qkv(ins[d], h)
                q16 = q.astype(BF16)
                k16 = k.astype(BF16)
                do16v = do_refs[d][:, h * DV:(h + 1) * DV]
                st16 = h_refs[d][h]
                dst = dss[d][h]
                dst16 = dst.astype(BF16)
                a = _dot(q16, k16, NT) * mask
                dp = _dot(do16v, v16, NT)
                da16 = (dp * mask).astype(BF16)
                dq_cross = _dot(do16v, st16, NT) * qd
                dq_ref[:, h * DK:(h + 1) * DK] = (_dot(da16, k16) + dq_cross).astype(BF16)
                dk_state = _dot(v16, dst16, NT) * kd
                dk_ref[:, h * DK:(h + 1) * DK] = ((_dot(da16, q16, TN) + dk_state) * (DK ** -0.5)).astype(BF16)
                dv = _dot(a.astype(BF16), do16v, TN) + _dot((k * kd).astype(BF16), dst16)
                dv_ref[:, h * DV:(h + 1) * DV] = dv.astype(BF16)
                dlg = (jnp.sum(relf * a * dp)
                       + jnp.sum(qe * jnp.sum(q * dq_cross, axis=-1, keepdims=True))
                       + jnp.sum(ke * jnp.sum(k * dk_state, axis=-1, keepdims=True))
                       + CH * g_ch * jnp.sum(dst * st16.astype(F32)))
                row = d * RH + h
                dlg_ref[row:row + 1, :] += jnp.broadcast_to(dlg, (1, 128))
                dss[d][h] = g_ch * dst + _dot((q * qd).astype(BF16), do16v, TN)

        @pl.when(c == nc - 1)
        def _():
            pos = lax.broadcasted_iota(jnp.int32, (cx, 1), 0).astype(F32)
            for h in range(RH):
                k, v16 = _ctx_kv(ctx_refs, h)
                dk = jnp.zeros((cx, DK), F32)
                dv = jnp.zeros((cx, DV), F32)
                for d, e in enumerate((cx - 1.0 - pos, pos)):
                    w = jnp.exp(lg_ref[d, h] * e)
                    ds16 = dss[d][h].astype(BF16)
                    t = _dot(v16, ds16, NT)
                    dk += t * w
                    dv += _dot((k * w).astype(BF16), ds16)
                    dlg = jnp.sum(e * w * jnp.sum(k * t, axis=-1, keepdims=True))
                    row = d * RH + h
                    dlg_ref[row:row + 1, :] += jnp.broadcast_to(dlg, (1, 128))
                dck_ref[:, h * DK:(h + 1) * DK] = (dk * (DK ** -0.5)).astype(BF16)
                dcv_ref[:, h * DV:(h + 1) * DV] = dv.astype(BF16)

    def fw(b, c):
        return b * nc + nc - 1 - c

    def bw(b, c):
        return b * nc + c

    def rows(rowf, width):
        return pl.BlockSpec((CH, width), lambda b, c: (rowf(b, c), 0))

    def hist(rowf):
        return pl.BlockSpec((None, None, RH, DK, DV), lambda b, c: (b, rowf(0, c), 0, 0, 0))

    in_specs = [pl.BlockSpec(memory_space=pltpu.SMEM)]
    out_specs = []
    for rowf in (fw, bw):
        in_specs += _wide_specs(rowf) + [rows(rowf, RH * DV), hist(rowf)]
        out_specs += [rows(rowf, RH * DK), rows(rowf, RH * DK), rows(rowf, RH * DV)]
    in_specs += _ctx_specs(t_rows, cx)
    out_specs += [pl.BlockSpec((cx, RH * DK), lambda b, c: (b, 0)), pl.BlockSpec((cx, RH * DV), lambda b, c: (b, 0)),
                  pl.BlockSpec((None, 8, 128), lambda b, c: (b, 0, 0))]
    qk = _sds((t_rows, RH * DK), BF16)
    vv = _sds((t_rows, RH * DV), BF16)
    return pl.pallas_call(
        body, name="ret_bwd", grid=(nb, nc), in_specs=in_specs, out_specs=out_specs,
        out_shape=[qk, qk, vv, qk, qk, vv, _sds((nb * cx, RH * DK), BF16), _sds((nb * cx, RH * DV), BF16),
                   _sds((nb, 8, 128), F32)],
        scratch_shapes=[pltpu.VMEM((RH, DK, DV), F32), pltpu.VMEM((RH, DK, DV), F32)],
        compiler_params=_params(("parallel", "arbitrary")),
    )(lg, *([px] * 5), do16, hist_f, *([px] * 5), do16, hist_b, *([px] * 3))


def _assemble_lat(dp_all, dk_f, dk_b, dv_f, dv_b, dak16, dvx, dq_f, dq_b, tm):
    t_rows = dk_f.shape[0]

    def body(_, dkf, dkb, dvf, dvb, dak, dav, dqf, dqb, o_ref):
        o_ref[:, RK:RK + RH * DK] = (dkf[...].astype(F32) + dkb[...].astype(F32)).astype(BF16)
        o_ref[:, RV:RV + RH * DV] = (dvf[...].astype(F32) + dvb[...].astype(F32)).astype(BF16)
        o_ref[:, AK:AK + HKV * HD] = dak[...]
        o_ref[:, AV:AV + HKV * HD] = dav[...].astype(BF16)
        o_ref[:, RQ:RQ + RH * DK] = (dqf[...].astype(F32) + dqb[...].astype(F32)).astype(BF16)

    args = (dk_f, dk_b, dv_f, dv_b, dak16, dvx, dq_f, dq_b)
    return pl.pallas_call(
        body, name="assemble_lat", grid=(t_rows // tm,),
        in_specs=[pl.BlockSpec(memory_space=pl.ANY)]
        + [pl.BlockSpec((tm, a.shape[1]), lambda i: (i, 0)) for a in args],
        out_specs=pl.BlockSpec((tm, RG), lambda i: (i, 0)), out_shape=_sds(dp_all.shape, BF16),
        input_output_aliases={0: 0},
        compiler_params=_params(("parallel",)),
    )(dp_all, *args)


def _assemble_ctx(dp_all, dck16, dcv16, dcak16, dvc, t_rows, tm):
    c_rows = dck16.shape[0]
    rb = t_rows // tm

    def body(_, dck, dcv, dcak, dcav, o_ref):
        o_ref[:, RK:RK + RH * DK] = dck[...]
        o_ref[:, RV:RV + RH * DV] = dcv[...]
        o_ref[:, AK:AK + HKV * HD] = dcak[...]
        o_ref[:, AV:AV + HKV * HD] = dcav[...].astype(BF16)
        o_ref[:, KV_COLS:] = jnp.zeros((tm, IN_COLS - KV_COLS), BF16)

    args = (dck16, dcv16, dcak16, dvc)
    return pl.pallas_call(
        body, name="assemble_ctx", grid=(c_rows // tm,),
        in_specs=[pl.BlockSpec(memory_space=pl.ANY)]
        + [pl.BlockSpec((tm, a.shape[1]), lambda i: (i, 0)) for a in args],
        out_specs=pl.BlockSpec((tm, IN_COLS), lambda i: (rb + i, 0)), out_shape=_sds(dp_all.shape, BF16),
        input_output_aliases={0: 0},
        compiler_params=_params(("parallel",)),
    )(dp_all, *args)


def _norm_bwd(dh, x2, mod3, norm_w, dxn, row_off, rows_per_group, group0, tm, name):
    with_dx = dxn is not None
    rows = x2.shape[0]
    rb0 = row_off // tm
    bpg = rows_per_group // tm
    ngroups = rows // rows_per_group

    def body(*refs):
        if with_dx:
            dh_ref, x_ref, sc_ref, nw_ref, dxn_ref, dx_ref, dsh_ref, dsc_ref, dnw_ref = refs
        else:
            dh_ref, x_ref, sc_ref, nw_ref, dsh_ref, dsc_ref, dnw_ref = refs
        i = pl.program_id(0)
        dhv = dh_ref[...]
        xv = x_ref[...]
        nw = nw_ref[...]
        r = lax.rsqrt(jnp.mean(xv * xv, axis=-1, keepdims=True) + EPS)
        xh = xv * r
        dm = dhv * (1.0 + sc_ref[...])
        dsh = jnp.sum(dhv, axis=0, keepdims=True)
        dsc = jnp.sum(dhv * (xh * nw), axis=0, keepdims=True)
        dnw = jnp.sum(dm * xh, axis=0, keepdims=True)
        if with_dx:
            dxh = dm * nw
            dx_ref[...] = dxn_ref[...] + r * (dxh - xh * jnp.mean(dxh * xh, axis=-1, keepdims=True))

        @pl.when(i % bpg == 0)
        def _():
            dsh_ref[...] = dsh
            dsc_ref[...] = dsc

        @pl.when(i % bpg != 0)
        def _():
            dsh_ref[...] += dsh
            dsc_ref[...] += dsc

        @pl.when(i == 0)
        def _():
            dnw_ref[...] = dnw

        @pl.when(i > 0)
        def _():
            dnw_ref[...] += dnw

    grp = pl.BlockSpec((None, 1, D), lambda i: (i // bpg, 0, 0))
    in_specs = [pl.BlockSpec((tm, D), lambda i: (rb0 + i, 0)), pl.BlockSpec((tm, D), lambda i: (i, 0)),
                pl.BlockSpec((None, 1, D), lambda i: (group0 + i // bpg, 0, 1)),
                pl.BlockSpec((1, D), lambda i: (0, 0))]
    args = [dh, x2, mod3, norm_w]
    out_specs = [grp, grp, pl.BlockSpec((1, D), lambda i: (0, 0))]
    out_shape = [_sds((ngroups, 1, D), F32), _sds((ngroups, 1, D), F32), _sds((1, D), F32)]
    if with_dx:
        in_specs.append(pl.BlockSpec((tm, D), lambda i: (i, 0)))
        args.append(dxn)
        out_specs.insert(0, pl.BlockSpec((tm, D), lambda i: (i, 0)))
        out_shape.insert(0, _sds((rows, D), F32))
    return pl.pallas_call(
        body, name=name, grid=(rows // tm,), in_specs=in_specs, out_specs=out_specs, out_shape=out_shape,
        compiler_params=_params(("arbitrary",)),
    )(*args)


def _small_final(dmod_all, dmodc_parts, c_rows, dm_loc_rows, nw_parts, misc_parts, c_ctx, r_pad, w_ada16):
    loc = dm_loc_rows.shape[1]

    def body(dm_ref, dmc_ref, c_ref, dml_ref, nwp_ref, mp_ref, cc_ref, r_ref, w_ref,
             gb_ref, gc_ref, gnw_ref, misc_ref, gwa_ref):
        dmc = jnp.sum(dmc_ref[...], axis=0, keepdims=True)
        gb_ref[...] = jnp.sum(dm_ref[...], axis=0, keepdims=True) + dmc
        dsc = _dot(jnp.broadcast_to(dmc, (8, 3 * D)).astype(BF16), w_ref[...], NT)[0:1, :]
        gc_ref[...] = dsc * _dsilu(cc_ref[...])
        gnw_ref[...] = jnp.sum(nwp_ref[...], axis=0, keepdims=True)
        misc = jnp.sum(mp_ref[...], axis=0, keepdims=True)
        y = jnp.exp2(r_ref[...])
        lane = lax.broadcasted_iota(jnp.int32, (1, D), 1)
        is_decay = jnp.logical_and(lane >= 2 * HD, lane < 2 * HD + 2 * RH)
        misc_ref[...] = misc * jnp.where(is_decay, -(y * np.float32(np.log(2.0))) / (1.0 - y), 1.0)
        gwa_ref[...] = _dot(_silu(c_ref[...]).astype(BF16), dml_ref[...].astype(BF16), TN)

    return pl.pallas_call(
        body, name="small_final",
        out_shape=[_sds((1, 3 * D), F32), _sds((1, D), F32), _sds((1, D), F32), _sds((1, D), F32), _sds((D, loc), F32)],
        compiler_params=pltpu.CompilerParams(vmem_limit_bytes=VMEM_LIMIT),
    )(dmod_all, dmodc_parts, c_rows, dm_loc_rows, nw_parts, misc_parts, c_ctx, r_pad, w_ada16)


def _adamw_math(w, g, m, v):
    nm = B1 * m + (1.0 - B1) * g
    nv = B2 * v + (1.0 - B2) * (g * g)
    return -LR * ((nm / (1.0 - B1 ** STEP)) / (jnp.sqrt(nv / (1.0 - B2 ** STEP)) + ADAM_EPS) + WD * w), nm, nv


def _adamw(w, g, m, v, name):
    rows, cols = w.shape
    tm = _pick(rows, 448, 8)

    def body(w_ref, g_ref, m_ref, v_ref, d_ref, nm_ref, nv_ref):
        d_ref[...], nm_ref[...], nv_ref[...] = _adamw_math(w_ref[...], g_ref[...], m_ref[...], v_ref[...])

    blk = pl.BlockSpec((tm, cols), lambda i: (i, 0))
    return pl.pallas_call(
        body, name=name, grid=(rows // tm,), in_specs=[blk] * 4, out_specs=[blk] * 3,
        out_shape=[_sds((rows, cols), F32)] * 3, compiler_params=_params(("parallel",)),
    )(w, g, m, v)


def _mesh_pos():
    return lax.axis_index("x"), lax.axis_index("y"), lax.axis_index("c")


def _all_gather(arrs, name):
    n = len(arrs)

    def body(*refs):
        ins, outs = refs[:n], refs[n:2 * n]
        send_sems, recv_sems, local_sems = refs[2 * n:]
        x, y, c = _mesh_pos()
        me, sib = (x, y, c), (x, y, 1 - c)
        chips = [(1 - x, y), (x, 1 - y), (1 - x, 1 - y)]

        def slot(p):
            return 4 * p[0] + 2 * p[1] + p[2]

        def copy(a, k, block, to, own):
            dst = outs[a].at[slot(block)]
            return pltpu.make_async_remote_copy(
                src_ref=ins[a] if own else dst, dst_ref=dst, send_sem=send_sems.at[a, k], recv_sem=recv_sems.at[a, k],
                device_id=to, device_id_type=MESH_T)

        mine = [pltpu.make_async_copy(ins[a], outs[a].at[slot(me)], local_sems.at[a]) for a in range(n)]
        for cp in mine:
            cp.start()
        first = []
        for a in range(n):
            first.append(copy(a, 0, me, sib, True))
            first += [copy(a, 1 + j, me, (*chip, c), True) for j, chip in enumerate(chips)]
        for cp in first:
            cp.start()
        passed = []
        for j, chip in enumerate(chips):
            for a in range(n):
                copy(a, 1 + j, (*chip, c), me, False).wait_recv()
                fwd = copy(a, 4 + j, (*chip, c), sib, False)
                fwd.start()
                passed.append(fwd)
        for a in range(n):
            copy(a, 0, sib, me, False).wait_recv()
            for j, chip in enumerate(chips):
                copy(a, 4 + j, (*chip, 1 - c), me, False).wait_recv()
        for cp in first + passed:
            cp.wait_send()
        for cp in mine:
            cp.wait()

    hbm = pl.BlockSpec(memory_space=pl.ANY)
    return pl.pallas_call(
        body, name=name, in_specs=[hbm] * n, out_specs=[hbm] * n,
        out_shape=[_sds((N_DEV,) + a.shape, a.dtype) for a in arrs],
        scratch_shapes=[pltpu.SemaphoreType.DMA((n, 7)), pltpu.SemaphoreType.DMA((n, 7)), pltpu.SemaphoreType.DMA((n,))],
    )(*arrs)


def _pair_exchange(arrs, name):
    n = len(arrs)

    def body(*refs):
        ins, outs = refs[:n], refs[n:2 * n]
        send_sems, recv_sems = refs[2 * n:]
        x, y, c = _mesh_pos()
        sib = (x, y, 1 - c)
        sends = []
        for a in range(n):
            for k in range(4):
                sends.append(pltpu.make_async_remote_copy(
                    src_ref=ins[a].at[2 * k + 1 - c], dst_ref=outs[a].at[k], send_sem=send_sems.at[a, k],
                    recv_sem=recv_sems.at[a, k], device_id=sib, device_id_type=MESH_T))
        for cp in sends:
            cp.start()
        for cp in sends:
            cp.wait_recv()
        for cp in sends:
            cp.wait_send()

    hbm = pl.BlockSpec(memory_space=pl.ANY)
    return pl.pallas_call(
        body, name=name, in_specs=[hbm] * n, out_specs=[hbm] * n,
        out_shape=[_sds((4,) + a.shape[1:], a.dtype) for a in arrs],
        scratch_shapes=[pltpu.SemaphoreType.DMA((n, 4)), pltpu.SemaphoreType.DMA((n, 4))],
    )(*arrs)


def _pair_add(part, got, core, name):
    _, rows, cols = part.shape
    tm = _pick(rows, 672, 16)
    p4 = part.reshape(4, 2, rows, cols)

    def body(core_ref, p_ref, g_ref, o_ref):
        o_ref[...] = (p_ref[...].astype(F32) + g_ref[...].astype(F32)).astype(BF16)

    blk = pl.BlockSpec((None, tm, cols), lambda k, i, cr: (k, i, 0))
    return pl.pallas_call(
        body, name=name,
        grid_spec=pltpu.PrefetchScalarGridSpec(
            num_scalar_prefetch=1, grid=(4, rows // tm),
            in_specs=[pl.BlockSpec((None, None, tm, cols), lambda k, i, cr: (k, cr[0], i, 0)), blk], out_specs=blk),
        out_shape=_sds((4, rows, cols), BF16), compiler_params=_params(("parallel", "parallel")),
    )(core, p4, got)


def _chip_sum_adamw(pair_sums, landed, chip, w, m, v, name):
    _, rows, cols = pair_sums.shape
    tm = _pick(rows, 448, 16)

    def body(chip_ref, s_ref, l_ref, w_ref, m_ref, v_ref, g_ref, d_ref, nm_ref, nv_ref):
        acc = s_ref[...].astype(F32)
        for j in range(3):
            acc = acc + l_ref[j].astype(F32)
        g_ref[...] = acc
        d_ref[...], nm_ref[...], nv_ref[...] = _adamw_math(w_ref[...], acc, m_ref[...], v_ref[...])

    blk = pl.BlockSpec((tm, cols), lambda i, ch: (i, 0))
    return pl.pallas_call(
        body, name=name,
        grid_spec=pltpu.PrefetchScalarGridSpec(
            num_scalar_prefetch=1, grid=(rows // tm,),
            in_specs=[pl.BlockSpec((None, tm, cols), lambda i, ch: (ch[0], i, 0)),
                      pl.BlockSpec((3, tm, cols), lambda i, ch: (0, i, 0)), blk, blk, blk],
            out_specs=[blk] * 4),
        out_shape=[_sds((rows, cols), F32)] * 4, compiler_params=_params(("parallel",)),
    )(chip, pair_sums, landed, w, m, v)


_HBM = pl.BlockSpec(memory_space=pltpu.HBM)
_SEM = pl.BlockSpec(memory_space=pltpu.SEMAPHORE)
_EFFECT = pltpu.SideEffectType.DATAFLOW_SIDE_EFFECTING


def _chip_routes(n):
    def plan(x, y, c):
        routes = []
        for a in range(n):
            for j in range(1, 4):
                px, py = x ^ (j >> 1), y ^ (j & 1)
                routes.append((a, 2 * px + py, (px, py, c), j - 1))
        return routes
    return plan, 3 * n


def _bcast_routes(n):
    def plan(x, y, c):
        routes = []
        for a in range(n):
            for k in range(1, N_DEV):
                peer = (x ^ ((k >> 2) & 1), y ^ ((k >> 1) & 1), c ^ (k & 1))
                routes.append((a, 0, peer, 4 * x + 2 * y + c))
        return routes
    return plan, 7 * n


def _route_copies(srcs, lands, send_sems, recv_sems, routes):
    return [pltpu.make_async_remote_copy(
        src_ref=srcs[a].at[sb], dst_ref=lands[a].at[lb], send_sem=send_sems.at[r], recv_sem=recv_sems.at[r],
        device_id=peer, device_id_type=MESH_T) for r, (a, sb, peer, lb) in enumerate(routes)]


def _exchange_start(srcs, lands, routes, name, after=()):
    plan, count = routes
    n = len(srcs)
    n_in = 2 * n + len(after)

    def body(*refs):
        send_sems, recv_sems = refs[n_in], refs[n_in + 1]
        token = refs[-1]
        for cp in _route_copies(refs[:n], refs[n:2 * n], send_sems, recv_sems, plan(*_mesh_pos())):
            cp.start()
        token[...] = jnp.zeros_like(token)

    args = [pltpu.with_memory_space_constraint(a, pltpu.HBM) for a in list(srcs) + list(lands)]
    out = pl.pallas_call(
        body, name=name,
        out_shape=(pltpu.SemaphoreType.DMA((count,)), pltpu.SemaphoreType.DMA((count,)),
                   *[pltpu.HBM(a.shape, a.dtype) for a in args], _sds((8, 128), F32)),
        in_specs=[_HBM] * (2 * n) + [pl.BlockSpec(memory_space=pl.ANY)] * len(after),
        out_specs=(_SEM, _SEM, *([_HBM] * (2 * n)), pl.BlockSpec(memory_space=pltpu.VMEM)),
        input_output_aliases={i: 2 + i for i in range(2 * n)},
        compiler_params=pltpu.CompilerParams(has_side_effects=_EFFECT),
    )(*args, *after)
    return (out[0], out[1], list(out[2:2 + 2 * n]), routes), out[-1]


def _exchange_wait(state, after, name):
    send_sems, recv_sems, bufs, (plan, count) = state
    n = len(bufs) // 2

    def body(*refs):
        send_s, recv_s = refs[2 * n], refs[2 * n + 1]
        for cp in _route_copies(refs[:n], refs[n:2 * n], send_s, recv_s, plan(*_mesh_pos())):
            cp.wait_send()
            cp.wait_recv()

    out = pl.pallas_call(
        body, name=name, out_shape=tuple(pltpu.HBM(a.shape, a.dtype) for a in bufs),
        in_specs=[_HBM] * (2 * n) + [_SEM, _SEM, pl.BlockSpec(memory_space=pl.ANY)], out_specs=tuple([_HBM] * (2 * n)),
        input_output_aliases={i: i for i in range(2 * n)},
        compiler_params=pltpu.CompilerParams(has_side_effects=_EFFECT),
    )(*bufs, send_sems, recv_sems, after)
    return list(out[:n]), list(out[n:])


def _group_routes(js):
    def plan(x, y, c):
        return [(0, 0, (x ^ (j >> 1), y ^ (j & 1), c), 2 * j + c) for j in js]
    return plan, len(js)


def _pair_fill(groups, js, name, after=()):
    def body(*refs):
        g_ref, send_sems, recv_sems = refs[-3:]
        x, y, c = _mesh_pos()
        sends = []
        for n, j in enumerate(js):
            mine = g_ref.at[2 * j + c]
            sends.append(pltpu.make_async_remote_copy(
                src_ref=mine, dst_ref=mine, send_sem=send_sems.at[n], recv_sem=recv_sems.at[n],
                device_id=(x, y, 1 - c), device_id_type=MESH_T))
        for cp in sends:
            cp.start()
        for n, j in enumerate(js):
            pltpu.make_async_remote_copy(
                src_ref=g_ref.at[2 * j + c], dst_ref=g_ref.at[2 * j + 1 - c], send_sem=send_sems.at[n],
                recv_sem=recv_sems.at[n], device_id=(x, y, 1 - c), device_id_type=MESH_T).wait_recv()
        for cp in sends:
            cp.wait_send()

    hbm = pl.BlockSpec(memory_space=pl.ANY)
    return pl.pallas_call(
        body, name=name, in_specs=[hbm] * (1 + len(after)), out_specs=hbm, out_shape=_sds(groups.shape, groups.dtype),
        input_output_aliases={0: 0},
        scratch_shapes=[pltpu.SemaphoreType.DMA((len(js),)), pltpu.SemaphoreType.DMA((len(js),))],
    )(groups, *after)


def _in_proj_group(h_all, groups, j0, ng, chip, px_prev, after, name):
    rows_all = h_all.shape[0]
    gcols = IN_COLS // 4
    tm = _pick(rows_all, 1536, 128)
    g4 = groups.reshape(4, gcols, D)

    n_lead = (1 if px_prev is not None else 0) + len(after)
    lead = ([px_prev] if px_prev is not None else []) + list(after)

    def body(chip_ref, *refs):
        h_ref, w_ref, o_ref = refs[n_lead:]
        o_ref[...] = _dot(h_ref[...], w_ref[...], NT).astype(BF16)

    return pl.pallas_call(
        body, name=name,
        grid_spec=pltpu.PrefetchScalarGridSpec(
            num_scalar_prefetch=1, grid=(ng, rows_all // tm),
            in_specs=[pl.BlockSpec(memory_space=pl.ANY)] * n_lead
            + [pl.BlockSpec((tm, D), lambda n, i, ch: (i, 0)),
               pl.BlockSpec((None, gcols, D), lambda n, i, ch: (j0 + n, 0, 0))],
            out_specs=pl.BlockSpec((tm, gcols), lambda n, i, ch: (i, ch[0] ^ (j0 + n)))),
        out_shape=_sds((rows_all, IN_COLS), BF16),
        input_output_aliases={1: 0} if px_prev is not None else {},
        compiler_params=_params(("parallel", "parallel")),
    )(chip, *lead, h_all, g4)


def _d_h_groups(dp_all, groups, chip, after):
    rows_all = dp_all.shape[0]
    gcols = IN_COLS // 4
    tm = _pick(rows_all, 1536, 128)
    g4 = groups.reshape(4, gcols, D)
    n_lead = len(after)

    def body(chip_ref, *refs):
        a_ref, w_ref, o_ref = refs[n_lead:]
        j = pl.program_id(1)
        part = _dot(a_ref[...], w_ref[...])

        @pl.when(j == 0)
        def _():
            o_ref[...] = part

        @pl.when(j > 0)
        def _():
            o_ref[...] += part

    return pl.pallas_call(
        body, name="d_h",
        grid_spec=pltpu.PrefetchScalarGridSpec(
            num_scalar_prefetch=1, grid=(rows_all // tm, 4),
            in_specs=[pl.BlockSpec(memory_space=pl.ANY)] * n_lead
            + [pl.BlockSpec((tm, gcols), lambda i, j, ch: (i, ch[0] ^ j)),
               pl.BlockSpec((None, gcols, D), lambda i, j, ch: (j, 0, 0))],
            out_specs=pl.BlockSpec((tm, D), lambda i, j, ch: (i, 0))),
        out_shape=_sds((rows_all, D), F32),
        compiler_params=_params(("parallel", "arbitrary")),
    )(chip, *after, dp_all, g4)


def _reduce_scatter_start(parts, core, name):
    got = _pair_exchange(parts, name + "_pair")
    sums = [_pair_add(p, g, core, "%s_add_%d" % (name, i)) for i, (p, g) in enumerate(zip(parts, got))]
    lands = [lax.empty((3,) + s_.shape[1:], BF16) for s_ in sums]
    return _exchange_start(sums, lands, _chip_routes(len(sums)), name + "_start")


def _reduce_scatter_finish(rs_state, after, chip, wmv, name):
    sums, landed = _exchange_wait(rs_state, after, name + "_wait")
    return [_chip_sum_adamw(s_, l_, chip, *t, "%s_adamw_%d" % (name, i))
            for i, (s_, l_, t) in enumerate(zip(sums, landed, wmv))]


def _local_step(x, c, ctx, norm_w, ret_log2_decay, q_norm_w, k_norm_w, loss_target,
                mod, proj_in, proj_back, get_w_o, on_out_grads, on_in_grad, started=()):
    nb, seq, _ = x.shape
    cx = ctx.shape[1]
    t_rows, c_rows = nb * seq, nb * cx
    rows_all = t_rows + c_rows
    nc = seq // CH
    tm = _pick(seq, 256, 128)
    te = _pick(seq, 512, 128)
    assert cx % tm == 0 and t_rows % cx == 0 and seq % GRID_W == 0

    x2 = x.reshape(t_rows, D)
    ctx2 = ctx.reshape(c_rows, D)
    tgt = loss_target.reshape(t_rows, D)
    lg = _log_gamma(ret_log2_decay)
    cos, sin = _rope_tables(seq)

    mod3 = mod[:, None, :]
    h_all = _norm_fwd(x2, mod3, norm_w, rows_all, 0, seq, 0, None, te, "norm_fwd", after=started)
    h_all = _norm_fwd(ctx2, mod3, norm_w, rows_all, t_rows, c_rows, nb, h_all, tm, "norm_fwd_ctx")
    px = proj_in(h_all)
    o_f, o_b, hist_f, hist_b = _ret_fwd(px, lg, nb, nc, cx)
    q16 = _qk_prep(px, q_norm_w, cos, sin, t_rows, 0, AQ, HQ, 4, seq, te, "q_prep")
    kx16 = _qk_prep(px, k_norm_w, cos, sin, t_rows, 0, AK, HKV, HKV, seq, te, "k_prep")
    kc16 = _qk_prep(px, k_norm_w, None, None, c_rows, t_rows, AK, HKV, HKV, seq, tm, "kc_prep")
    o_att, yatt16, lse = _att_fwd(q16, kx16, kc16, px, nb, seq, cx, te)
    w_o_ret16, w_o_att16, w_out16 = get_w_o(lse)
    yret16, a_ret, a_att, y16, dxn, dout16, dgate, loss_b = _merge_out(
        o_f, o_b, yatt16, px, w_o_ret16, w_o_att16, w_out16, x2, tgt, mod3, nb, seq, tm)

    gw_out = _matmul(y16, dout16, ta=True, tm=D, tn=D, tk=D, out_dtype=BF16, name="gw_out")
    da_ret16, da_att16, do16, dao, dp_all = _bwd_branches(
        dout16, w_out16, w_o_ret16, w_o_att16, px, a_ret, a_att, o_f, o_b, o_att, rows_all, tm)
    gw_o_ret = _matmul(yret16, da_ret16, ta=True, tm=D, tn=D, tk=D, out_dtype=BF16, name="gw_o_ret")
    gw_o_att = _matmul(yatt16, da_att16, ta=True, tm=D, tn=D, tk=D, out_dtype=BF16, name="gw_o_att")
    out_state, out_started = on_out_grads([gw_o_ret, gw_o_att, gw_out])
    dp_all, gq, dkx, dvx, dkc, dvc = _att_bwd(q16, kx16, kc16, px, dao, o_att, lse, q_norm_w, cos, sin, dp_all, nb, seq,
                                              cx, te, after=out_started)
    dak16, gk_lat = _qk_prep_bwd(dkx.reshape(t_rows, HKV * HD), px, k_norm_w, cos, sin, t_rows, 0, AK, HKV, HKV, seq, te,
                                 "k_prep_bwd")
    dcak16, gk_ctx = _qk_prep_bwd(dkc.reshape(c_rows, HKV * HD), px, k_norm_w, None, None, c_rows, t_rows, AK, HKV, HKV,
                                  seq, tm, "kc_prep_bwd")
    dq_f, dk_f, dv_f, dq_b, dk_b, dv_b, dck16, dcv16, dlg_scan = _ret_bwd(px, lg, do16, hist_f, hist_b, nb, nc, cx)
    dp_all = _assemble_lat(dp_all, dk_f, dk_b, dv_f, dv_b, dak16, dvx.reshape(t_rows, HKV * HD), dq_f, dq_b, tm)
    dp_all = _assemble_ctx(dp_all, dck16, dcv16, dcak16, dvc.reshape(c_rows, HKV * HD), t_rows, tm)
    gw_in_t = _matmul(dp_all, h_all, ta=True, tm=1536, tn=D, tk=2304, out_dtype=BF16, name="gw_in")
    in_state, in_started = on_in_grad(gw_in_t)
    dh = proj_back(dp_all, in_started)
    grad_x, dsh, dsc, gnw_lat = _norm_bwd(dh, x2, mod3, norm_w, dxn, 0, seq, 0, te, "norm_bwd")
    dsh_c, dsc_c, gnw_ctx = _norm_bwd(dh, ctx2, mod3, norm_w, None, t_rows, c_rows, nb, tm, "norm_bwd_ctx")

    dlg = jnp.sum(dlg_scan[:, :, 0], axis=0).reshape(1, 2 * RH)
    misc = jnp.concatenate([gq, gk_lat + gk_ctx, dlg, jnp.sum(loss_b[:, 0, 0]).reshape(1, 1),
                            jnp.zeros((1, D - 2 * HD - 2 * RH - 1), F32)], axis=1)
    rows = []
    for b in range(nb):
        rows += [dsh[b], dsc[b], dgate[b]]
    rows += [dsh_c[0], dsc_c[0]] + [c[b:b + 1] for b in range(nb)] + [gnw_lat + gnw_ctx, misc]
    payload = jnp.concatenate(rows + [jnp.zeros((PAY_ROWS - len(rows), D), F32)], axis=0)
    return grad_x.reshape(nb, seq, D), out_state, in_state, payload


def _finish_small(gathered, nb, c_ctx, ret_log2_decay, w_ada16, dev):
    n_dev = gathered.shape[0]
    loc = 3 * D // n_dev
    dmod_all = gathered[:, :3 * nb].reshape(n_dev * nb, 3 * D)
    dmodc_parts = jnp.concatenate([gathered[:, 3 * nb:3 * nb + 2].reshape(n_dev, 2 * D), jnp.zeros((n_dev, D), F32)], axis=1)
    c_all = gathered[:, 3 * nb + 2:4 * nb + 2].reshape(n_dev * nb, D)
    nw_parts = gathered[:, 4 * nb + 2]
    misc_parts = gathered[:, 4 * nb + 3]
    n_rows = n_dev * nb + n_dev
    pad = (-n_rows) % 16
    c_rows = jnp.concatenate([c_all, jnp.broadcast_to(c_ctx.reshape(1, D), (n_dev, D)), jnp.zeros((pad, D), F32)], axis=0)
    dm_rows = jnp.concatenate([dmod_all, dmodc_parts, jnp.zeros((pad, 3 * D), F32)], axis=0)
    dm_loc_rows = lax.dynamic_slice_in_dim(dm_rows, dev * loc, loc, axis=1)
    r_pad = jnp.full((1, D), -1.0, F32).at[:, 2 * HD:2 * HD + 2 * RH].set(ret_log2_decay.reshape(1, 2 * RH))
    gb, gc, gnw, misc, gwa = _small_final(dmod_all, dmodc_parts, c_rows, dm_loc_rows, nw_parts, misc_parts,
                                          c_ctx.reshape(1, D), r_pad, w_ada16)
    return (gb, gc, gnw, misc[:, :HD], misc[:, HD:2 * HD], misc[:, 2 * HD:2 * HD + 2 * RH], gwa,
            misc[0, 2 * HD + 2 * RH])


def kernel(x, c, ctx, c_ctx, norm_w, w_ada, b_ada, w_in, ret_log2_decay, q_norm_w, k_norm_w, w_o_ret, w_o_att, w_out, loss_target, m_c_ctx, m_norm_w, m_w_ada, m_b_ada, m_w_in, m_ret_log2_decay, m_q_norm_w, m_k_norm_w, m_w_o_ret, m_w_o_att, m_w_out, v_c_ctx, v_norm_w, v_w_ada, v_b_ada, v_w_in, v_ret_log2_decay, v_q_norm_w, v_k_norm_w, v_w_o_ret, v_w_o_att, v_w_out):
    nb = x.shape[0]
    mx, my, mc = _mesh_pos()
    dev = 4 * mx + 2 * my + mc
    core = jnp.reshape(mc, (1,)).astype(jnp.int32)
    chip = jnp.reshape(2 * mx + my, (1,)).astype(jnp.int32)

    n_loc = 3 * D // N_DEV
    c8 = jnp.zeros((8, D), F32).at[:nb].set(c).at[nb].set(c_ctx)
    c_land = lax.dynamic_update_slice(lax.empty((N_DEV, 8, D), F32), c8[None], (dev, 0, 0))
    c_state, c_token = _exchange_start([c8[None]], [c_land], _bcast_routes(1), "gather_c_start")
    w_in_t = jnp.transpose(w_in[0])
    in_shard = w_in_t.astype(BF16)
    groups = lax.dynamic_update_slice(lax.empty((N_DEV,) + in_shard.shape, BF16), in_shard[None], (mc, 0, 0))
    groups = _pair_fill(groups, (0,), "gather_in_pair", after=(c_token,))
    _, (c_all,) = _exchange_wait(c_state, groups, "gather_c_wait")
    ada_shard = w_ada[0].astype(BF16)
    b_loc = lax.dynamic_slice(b_ada, (0, dev * n_loc), (1, n_loc))
    mod_cols = _mod_part(c_all.reshape(N_DEV * 8, D), ada_shard, b_loc)
    (mod_all,) = _all_gather([mod_cols], "gather_mod")
    mod = jnp.transpose(lax.dynamic_slice(mod_all, (0, dev * 8, 0), (N_DEV, 8, n_loc)), (1, 0, 2)).reshape(8, 3 * D)
    ada_land = lax.dynamic_update_slice(lax.empty((N_DEV,) + ada_shard.shape, BF16), ada_shard[None], (dev, 0, 0))

    (near_send, near_recv, near_bufs, near_routes), gin_token = _exchange_start(
        [in_shard[None]], [groups], _group_routes((1, 2)), "gather_in_start", after=(mod_all,))
    w_in_groups, wo_states, ada_states = [], [], []
    wo_shards = [w_[0].astype(BF16) for w_ in (w_o_ret, w_o_att, w_out)]
    wo_lands = [lax.dynamic_update_slice(lax.empty((N_DEV,) + s_.shape, BF16), s_[None], (dev, 0, 0)) for s_ in wo_shards]

    def _state(send, recv, src, groups, routes):
        return send, recv, [src, groups], routes

    def proj_in(h_all):
        src, groups = near_bufs
        px = _in_proj_group(h_all, groups, 0, 1, chip, None, (gin_token,), "in_proj_0")
        (src,), (groups,) = _exchange_wait(_state(near_send, near_recv, src, groups, near_routes), px,
                                           "gather_in_wait_near")
        groups = _pair_fill(groups, (1, 2), "gather_in_fill_near")
        (far_send, far_recv, (src, groups), far_routes), far_token = _exchange_start(
            [src], [groups], _group_routes((3,)), "gather_in_start_far")
        wo_state, wo_token = _exchange_start([s_[None] for s_ in wo_shards], wo_lands, _bcast_routes(3),
                                             "gather_wo_start", after=(far_token,))
        wo_states.append(wo_state)
        ada_state, ada_token = _exchange_start([ada_shard[None]], [ada_land], _bcast_routes(1), "gather_ada_start",
                                               after=(wo_token,))
        ada_states.append(ada_state)
        px = _in_proj_group(h_all, groups, 1, 2, chip, px, (ada_token,), "in_proj_near")
        (src,), (groups,) = _exchange_wait(_state(far_send, far_recv, src, groups, far_routes), px,
                                           "gather_in_wait_far")
        groups = _pair_fill(groups, (3,), "gather_in_fill_far")
        px = _in_proj_group(h_all, groups, 3, 1, chip, px, (), "in_proj_far")
        w_in_groups.append(groups)
        return px

    def proj_back(dp_all, after):
        return _d_h_groups(dp_all, w_in_groups[0], chip, after)

    def get_w_o(after):
        _, (l_ret, l_att, l_out) = _exchange_wait(wo_states[0], after, "gather_wo_wait")
        return l_ret.reshape(RH * DV, D), l_att.reshape(D, D), l_out.reshape(D, D)

    def on_out_grads(grads):
        parts = [g_.reshape(N_DEV, g_.shape[0] // N_DEV, D) for g_ in grads]
        state, token = _reduce_scatter_start(parts, core, "rs_out")
        return state, (token,)

    def on_in_grad(grad):
        state, token = _reduce_scatter_start([grad.reshape(N_DEV, IN_COLS // N_DEV, D)], core, "rs_in")
        return state, (token,)

    grad_x, out_state, in_state, payload = _local_step(
        x, c, ctx, norm_w, ret_log2_decay, q_norm_w, k_norm_w, loss_target,
        mod, proj_in, proj_back, get_w_o, on_out_grads, on_in_grad, started=(gin_token,))

    pay_land = lax.dynamic_update_slice(lax.empty((N_DEV,) + payload.shape, F32), payload[None], (dev, 0, 0))
    pay_state, pay_token = _exchange_start([payload[None]], [pay_land], _bcast_routes(1), "gather_small_start")

    out_res = _reduce_scatter_finish(out_state, pay_token, chip,
                                     [(w_[0], m_[0], v_[0]) for w_, m_, v_ in ((w_o_ret, m_w_o_ret, v_w_o_ret),
                                                                                (w_o_att, m_w_o_att, v_w_o_att),
                                                                                (w_out, m_w_out, v_w_out))], "rs_out")
    (in_res,) = _reduce_scatter_finish(in_state, out_res[0][0], chip,
                                       [(w_in_t, jnp.transpose(m_w_in[0]), jnp.transpose(v_w_in[0]))], "rs_in")

    _, (gathered,) = _exchange_wait(pay_state, in_res[0], "gather_small_wait")
    _, (l_ada,) = _exchange_wait(ada_states[0], gathered, "gather_ada_wait")
    w_ada16 = jnp.transpose(l_ada, (1, 0, 2)).reshape(D, 3 * D)
    gb, gc, gnw, gq, gk, gr, gwa, loss = _finish_small(gathered, nb, c_ctx, ret_log2_decay, w_ada16, dev)
    big = {4: [jnp.transpose(r)[None] for r in in_res]}
    for i, res in zip((8, 9, 10), out_res):
        big[i] = [r[None] for r in res]
    small_g = {0: gc.reshape(c_ctx.shape), 1: gnw, 2: gwa[None], 3: gb, 5: gr.reshape(ret_log2_decay.shape), 6: gq, 7: gk}
    weights = [c_ctx, norm_w, w_ada, b_ada, w_in, ret_log2_decay, q_norm_w, k_norm_w, w_o_ret, w_o_att, w_out]
    ms = [m_c_ctx, m_norm_w, m_w_ada, m_b_ada, m_w_in, m_ret_log2_decay, m_q_norm_w, m_k_norm_w, m_w_o_ret, m_w_o_att, m_w_out]
    vs = [v_c_ctx, v_norm_w, v_w_ada, v_b_ada, v_w_in, v_ret_log2_decay, v_q_norm_w, v_k_norm_w, v_w_o_ret, v_w_o_att, v_w_out]
    grads, deltas, new_ms, new_vs = [], [], [], []
    for i, (w, m, v) in enumerate(zip(weights, ms, vs)):
        if i in big:
            res = big[i]
        else:
            shape2 = (-1, w.shape[-1])
            g = small_g[i]
            res = [g] + [r.reshape(w.shape) for r in _adamw(w.reshape(shape2), g.reshape(shape2), m.reshape(shape2),
                                                             v.reshape(shape2), "adamw_%d" % i)]
        for lst, r in zip((grads, deltas, new_ms, new_vs), res):
            lst.append(r)
    return (loss, grad_x, *grads, *deltas, *new_ms, *new_vs)
```

```python
import numpy as np
import jax
import jax.numpy as jnp
from jax import lax
from jax.experimental import pallas as pl
from jax.experimental.pallas import tpu as pltpu

F32 = jnp.float32
BF16 = jnp.bfloat16

D = 1024
RH, DK, DV, CH = 4, 256, 512, 256
HQ, HKV, HD = 8, 2, 128
GRID_W = 64
ROPE_THETA = 10000.0
EPS = 1e-6
RK, RV, AK, AV, RQ, RG, AQ, AG, MR, MA = 0, 1024, 3072, 3328, 3584, 4608, 6656, 7680, 8704, 9728
IN_COLS = 10752
KV_COLS = 3584
N_DEV = 8
LR, B1, B2, ADAM_EPS, WD, STEP = 0.001, 0.9, 0.999, 1e-08, 0.01, 10
PAY_ROWS = 16
VMEM_LIMIT = 56 * 1024 * 1024
MESH_T = pl.DeviceIdType.MESH

NT = (((1,), (1,)), ((), ()))
TN = (((0,), (0,)), ((), ()))
SM_C = (HD ** -0.5) * float(np.log2(np.e))


def _params(sem):
    return pltpu.CompilerParams(dimension_semantics=sem, vmem_limit_bytes=VMEM_LIMIT)


def _pick(n, target, mult=8):
    best = None
    for t in range(mult, min(n, target) + 1, mult):
        if n % t == 0:
            best = t
    return best or n


def _dot(a, b, dn=None):
    if dn is None:
        return jnp.dot(a, b, preferred_element_type=F32)
    return lax.dot_general(a, b, dn, preferred_element_type=F32)


def _sig(v):
    return jax.nn.sigmoid(v)


def _silu(v):
    return v * _sig(v)


def _dsilu(v):
    s = _sig(v)
    return s * (1.0 + v * (1.0 - s))


def _sds(shape, dtype):
    return jax.ShapeDtypeStruct(shape, dtype)


def _matmul(a, b, *, ta=False, tb=False, tm, tn, tk, out_dtype, name, after=()):
    m = a.shape[1] if ta else a.shape[0]
    kdim = a.shape[0] if ta else a.shape[1]
    n = b.shape[0] if tb else b.shape[1]
    tm, tn, tk = _pick(m, tm, 128), _pick(n, tn, 128), _pick(kdim, tk, 128)
    nk = kdim // tk
    dn = (((0 if ta else 1,), (1 if tb else 0,)), ((), ()))

    def body(a_ref, b_ref, *rest):
        o_ref, acc_ref = rest[-2:]
        k = pl.program_id(2)
        part = _dot(a_ref[...].astype(BF16), b_ref[...].astype(BF16), dn)
        if nk == 1:
            o_ref[...] = part.astype(o_ref.dtype)
        else:
            @pl.when(k == 0)
            def _():
                acc_ref[...] = part

            @pl.when(k > 0)
            def _():
                acc_ref[...] += part

            @pl.when(k == nk - 1)
            def _():
                o_ref[...] = acc_ref[...].astype(o_ref.dtype)

    a_spec = pl.BlockSpec((tk, tm), lambda i, j, k: (k, i)) if ta else pl.BlockSpec((tm, tk), lambda i, j, k: (i, k))
    b_spec = pl.BlockSpec((tn, tk), lambda i, j, k: (j, k)) if tb else pl.BlockSpec((tk, tn), lambda i, j, k: (k, j))
    return pl.pallas_call(
        body, name=name, grid=(m // tm, n // tn, nk),
        in_specs=[a_spec, b_spec] + [pl.BlockSpec(memory_space=pl.ANY)] * len(after),
        out_specs=pl.BlockSpec((tm, tn), lambda i, j, k: (i, j)), out_shape=_sds((m, n), out_dtype),
        scratch_shapes=[pltpu.VMEM((tm, tn) if nk > 1 else (8, 128), F32)],
        compiler_params=_params(("parallel", "parallel", "arbitrary")),
    )(a, b, *after)


def _log_gamma(r):
    rp = jnp.full((8, 128), -1.0, F32).at[:2, :RH].set(r.reshape(2, RH))

    def body(r_ref, o_ref):
        o_ref[...] = jnp.log1p(-jnp.exp2(r_ref[...]))

    out = pl.pallas_call(body, name="log_gamma", out_shape=_sds((8, 128), F32))(rp)
    return out[:2, :RH]


def _mod_part(c_rows, w_ada_loc16, b_loc):
    def body(c_ref, w_ref, b_ref, o_ref):
        o_ref[...] = _dot(_silu(c_ref[...]).astype(BF16), w_ref[...]) + b_ref[...]

    return pl.pallas_call(
        body, name="mod_part", out_shape=_sds((c_rows.shape[0], w_ada_loc16.shape[1]), F32),
    )(c_rows, w_ada_loc16, b_loc)


def _norm_fwd(x2, mod3, norm_w, rows_all, row_off, rows_per_group, group0, h_prev, tm, name, after=()):
    rows = x2.shape[0]
    rb0 = row_off // tm
    bpg = rows_per_group // tm

    def body(*refs):
        x_ref, sh_ref, sc_ref, nw_ref, o_ref = refs[-5:]
        xv = x_ref[...]
        r = lax.rsqrt(jnp.mean(xv * xv, axis=-1, keepdims=True) + EPS)
        o_ref[...] = ((xv * r) * nw_ref[...] * (1.0 + sc_ref[...]) + sh_ref[...]).astype(BF16)

    in_specs = [pl.BlockSpec((tm, D), lambda i: (i, 0)),
                pl.BlockSpec((None, 1, D), lambda i: (group0 + i // bpg, 0, 0)),
                pl.BlockSpec((None, 1, D), lambda i: (group0 + i // bpg, 0, 1)),
                pl.BlockSpec((1, D), lambda i: (0, 0))]
    in_specs = [pl.BlockSpec(memory_space=pl.ANY)] * len(after) + in_specs
    args = list(after) + [x2, mod3, mod3, norm_w]
    alias = {}
    if h_prev is not None:
        in_specs.insert(0, pl.BlockSpec(memory_space=pl.ANY))
        args.insert(0, h_prev)
        alias = {0: 0}
    return pl.pallas_call(
        body, name=name, grid=(rows // tm,), in_specs=in_specs,
        out_specs=pl.BlockSpec((tm, D), lambda i: (rb0 + i, 0)), out_shape=_sds((rows_all, D), BF16),
        input_output_aliases=alias, compiler_params=_params(("parallel",)),
    )(*args)


def _decays(lg, fwd):
    ii = lax.broadcasted_iota(jnp.int32, (CH, CH), 0)
    jj = lax.broadcasted_iota(jnp.int32, (CH, CH), 1)
    ri = lax.broadcasted_iota(jnp.int32, (CH, 1), 0).astype(F32)
    rel = (ii - jj) if fwd else (jj - ii)
    relf = jnp.maximum(rel, 0).astype(F32)
    mask = jnp.where(rel >= 0, jnp.exp(lg * relf), 0.0)
    qe = (ri + 1.0) if fwd else (CH - ri)
    ke = (CH - 1.0 - ri) if fwd else ri
    return mask, relf, jnp.exp(lg * qe), qe, jnp.exp(lg * ke), ke


def _wide_specs(rowf):
    return [pl.BlockSpec((CH, 2 * DK), lambda b, c: (rowf(b, c), RQ // (2 * DK))),
            pl.BlockSpec((CH, 2 * DK), lambda b, c: (rowf(b, c), RQ // (2 * DK) + 1)),
            pl.BlockSpec((CH, RH * DK), lambda b, c: (rowf(b, c), RK // (RH * DK))),
            pl.BlockSpec((CH, 2 * DV), lambda b, c: (rowf(b, c), RV // (2 * DV))),
            pl.BlockSpec((CH, 2 * DV), lambda b, c: (rowf(b, c), RV // (2 * DV) + 1))]


def _head_qkv(refs, h):
    q0, q1, k, v0, v1 = refs
    lo = h % 2
    q = (q0, q1)[h // 2][:, lo * DK:(lo + 1) * DK].astype(F32)
    kk = k[:, h * DK:(h + 1) * DK].astype(F32) * (DK ** -0.5)
    v16 = (v0, v1)[h // 2][:, lo * DV:(lo + 1) * DV].astype(BF16)
    return q, kk, v16


def _ctx_specs(t_rows, cx):
    rb = t_rows // cx
    return [pl.BlockSpec((cx, RH * DK), lambda b, c: (rb + b, RK // (RH * DK))),
            pl.BlockSpec((cx, 2 * DV), lambda b, c: (rb + b, RV // (2 * DV))),
            pl.BlockSpec((cx, 2 * DV), lambda b, c: (rb + b, RV // (2 * DV) + 1))]


def _ctx_kv(refs, h):
    k, v0, v1 = refs
    kk = k[:, h * DK:(h + 1) * DK].astype(F32) * (DK ** -0.5)
    lo = h % 2
    return kk, (v0, v1)[h // 2][:, lo * DV:(lo + 1) * DV].astype(BF16)


def _ret_fwd(px, lg, nb, nc, cx):
    t_rows = nb * nc * CH

    def body(lg_ref, *refs):
        ins = (refs[0:5], refs[5:10])
        ctx_refs = refs[10:13]
        of_ref, ob_ref, hf_ref, hb_ref, sf, sb = refs[13:]
        c = pl.program_id(1)

        @pl.when(c == 0)
        def _():
            pos = lax.broadcasted_iota(jnp.int32, (cx, 1), 0).astype(F32)
            for h in range(RH):
                k, v16 = _ctx_kv(ctx_refs, h)
                sf[h] = _dot((k * jnp.exp(lg_ref[0, h] * (cx - 1.0 - pos))).astype(BF16), v16, TN)
                sb[h] = _dot((k * jnp.exp(lg_ref[1, h] * pos)).astype(BF16), v16, TN)

        for d, (o_ref, h_ref, s) in enumerate(((of_ref, hf_ref, sf), (ob_ref, hb_ref, sb))):
            for h in range(RH):
                lg_d = lg_ref[d, h]
                mask, _, qd, _, kd, _ = _decays(lg_d, d == 0)
                q, k, v16 = _head_qkv(ins[d], h)
                a = _dot(q.astype(BF16), k.astype(BF16), NT)
                st = s[h]
                st16 = st.astype(BF16)
                h_ref[h] = st16
                o = _dot((a * mask).astype(BF16), v16) + _dot((q * qd).astype(BF16), st16)
                o_ref[:, h * DV:(h + 1) * DV] = o.astype(BF16)
                s[h] = st * jnp.exp(lg_d * CH) + _dot((k * kd).astype(BF16), v16, TN)

    def fw(b, c):
        return b * nc + c

    def bw(b, c):
        return b * nc + nc - 1 - c

    in_specs = [pl.BlockSpec(memory_space=pltpu.SMEM)] + _wide_specs(fw) + _wide_specs(bw) + _ctx_specs(t_rows, cx)
    out_specs = [pl.BlockSpec((CH, RH * DV), lambda b, c: (fw(b, c), 0)),
                 pl.BlockSpec((CH, RH * DV), lambda b, c: (bw(b, c), 0)),
                 pl.BlockSpec((None, None, RH, DK, DV), lambda b, c: (b, c, 0, 0, 0)),
                 pl.BlockSpec((None, None, RH, DK, DV), lambda b, c: (b, nc - 1 - c, 0, 0, 0))]
    return pl.pallas_call(
        body, name="ret_fwd", grid=(nb, nc), in_specs=in_specs, out_specs=out_specs,
        out_shape=[_sds((t_rows, RH * DV), BF16)] * 2 + [_sds((nb, nc, RH, DK, DV), BF16)] * 2,
        scratch_shapes=[pltpu.VMEM((RH, DK, DV), F32), pltpu.VMEM((RH, DK, DV), F32)],
        compiler_params=_params(("parallel", "arbitrary")),
    )(lg, *([px] * 13))


def _rope_tables(seq):
    rows = seq // GRID_W
    row = np.repeat(np.arange(rows, dtype=np.float32), GRID_W)
    col = np.tile(np.arange(GRID_W, dtype=np.float32), rows)
    half = HD // 2
    freqs = (ROPE_THETA ** (-np.arange(0, half, 2, dtype=np.float32) / half)).astype(np.float32)
    ang = np.concatenate([row[:, None] * freqs, col[:, None] * freqs], axis=-1).astype(np.float32)
    cos = np.repeat(np.cos(ang), 2, axis=-1).astype(np.float32)
    sin = np.repeat(np.sin(ang), 2, axis=-1).astype(np.float32)
    sign = np.tile(np.array([-1.0, 1.0], np.float32), HD // 2)
    return jnp.asarray(cos), jnp.asarray(sin * sign)


def _swap_pairs(v):
    lane = lax.broadcasted_iota(jnp.int32, v.shape, 1)
    return jnp.where((lane & 1) == 0, pltpu.roll(v, HD - 1, 1), pltpu.roll(v, 1, 1))


def _qk_prep(px, nw, cos, sin, rows, row_off, col_off, heads, hb, seq, tm, name):
    rope = cos is not None
    rb0 = row_off // tm
    pb = seq // tm if rope else 1
    bw = hb * HD

    def body(*refs):
        if rope:
            x_ref, w_ref, c_ref, s_ref, o_ref = refs
        else:
            x_ref, w_ref, o_ref = refs
        for h in range(hb):
            sl = slice(h * HD, (h + 1) * HD)
            xv = x_ref[:, sl].astype(F32)
            r = lax.rsqrt(jnp.mean(xv * xv, axis=-1, keepdims=True) + EPS)
            t = (xv * r) * w_ref[...]
            if rope:
                t = t * c_ref[...] + _swap_pairs(t) * s_ref[...]
            o_ref[:, sl] = t.astype(BF16)

    in_specs = [pl.BlockSpec((tm, bw), lambda i, j: (rb0 + i, col_off // bw + j)),
                pl.BlockSpec((1, HD), lambda i, j: (0, 0))]
    args = [px, nw]
    if rope:
        in_specs += [pl.BlockSpec((tm, HD), lambda i, j: (i % pb, 0))] * 2
        args += [cos, sin]
    return pl.pallas_call(
        body, name=name, grid=(rows // tm, heads // hb), in_specs=in_specs,
        out_specs=pl.BlockSpec((tm, bw), lambda i, j: (i, j)), out_shape=_sds((rows, heads * HD), BF16),
        compiler_params=_params(("parallel", "parallel")),
    )(*args)


def _att_fwd(q16, kx16, kc16, px, nb, seq, cx, tq):
    t_rows = nb * seq
    nq = seq // tq
    rep = HQ // HKV
    gw = rep * HD

    def body(q_ref, kx_ref, kc_ref, vx_ref, vc_ref, g_ref, o_ref, y_ref, l_ref):
        kx = kx_ref[...]
        kc = kc_ref[...]
        vx = vx_ref[...].astype(BF16)
        vc = vc_ref[...].astype(BF16)
        l_ref[...] = jnp.zeros_like(l_ref)
        for r in range(rep):
            sl = slice(r * HD, (r + 1) * HD)
            q = q_ref[:, sl]
            s1 = _dot(q, kx, NT)
            s2 = _dot(q, kc, NT)
            m = jnp.maximum(jnp.max(s1, axis=-1, keepdims=True), jnp.max(s2, axis=-1, keepdims=True))
            e1 = jnp.exp2((s1 - m) * SM_C)
            e2 = jnp.exp2((s2 - m) * SM_C)
            tot = jnp.sum(e1, axis=-1, keepdims=True) + jnp.sum(e2, axis=-1, keepdims=True)
            o = (_dot(e1.astype(BF16), vx) + _dot(e2.astype(BF16), vc)) * (1.0 / tot)
            o_ref[:, sl] = o
            y_ref[:, sl] = (o * _silu(g_ref[:, sl].astype(F32))).astype(BF16)
            l_ref[:, r:r + 1] = m * SM_C + jnp.log(tot) * float(np.log2(np.e))

    qblk = pl.BlockSpec((tq, gw), lambda b, g, i: (b * nq + i, g))
    return pl.pallas_call(
        body, name="att_fwd", grid=(nb, HKV, nq),
        in_specs=[qblk,
                  pl.BlockSpec((seq, HD), lambda b, g, i: (b, g)),
                  pl.BlockSpec((cx, HD), lambda b, g, i: (b, g)),
                  pl.BlockSpec((seq, HD), lambda b, g, i: (b, AV // HD + g)),
                  pl.BlockSpec((cx, HD), lambda b, g, i: (t_rows // cx + b, AV // HD + g)),
                  pl.BlockSpec((tq, gw), lambda b, g, i: (b * nq + i, AG // gw + g))],
        out_specs=[qblk, qblk, pl.BlockSpec((tq, 128), lambda b, g, i: (b * nq + i, g))],
        out_shape=[_sds((t_rows, D), F32), _sds((t_rows, D), BF16), _sds((t_rows, HKV * 128), F32)],
        compiler_params=_params(("parallel", "parallel", "parallel")),
    )(q16, kx16, kc16, px, px, px)


def _gate_specs(tm, col0):
    hw = D // 2
    return [pl.BlockSpec((tm, hw), lambda i: (i, col0 // hw)), pl.BlockSpec((tm, hw), lambda i: (i, col0 // hw + 1))]


def _merge_out(o_f, o_b, yatt16, px, w_o_ret16, w_o_att16, w_out16, x2, tgt, mod3, nb, seq, tm):
    t_rows = nb * seq
    bpb = seq // tm
    hw = D // 2

    def body(of_ref, ob_ref, g0, g1, g2, g3, wr_ref, ya_ref, wa_ref, mr0, mr1, ma0, ma1, wo_ref, x_ref, t_ref, gt_ref,
             yr_ref, ar_ref, aa_ref, y_ref, dxn_ref, dout_ref, dg_ref, loss_ref):
        i = pl.program_id(1)
        for h, g_ref in enumerate((g0, g1, g2, g3)):
            sl = slice(h * DV, (h + 1) * DV)
            o = of_ref[:, sl].astype(F32) + ob_ref[:, sl].astype(F32)
            r = lax.rsqrt(jnp.mean(o * o, axis=-1, keepdims=True) + EPS)
            yr_ref[:, sl] = ((o * r) * _silu(g_ref[...].astype(F32))).astype(BF16)
        ar = _dot(yr_ref[...], wr_ref[...])
        aa = _dot(ya_ref[...], wa_ref[...])
        ar_ref[...] = ar.astype(BF16)
        aa_ref[...] = aa.astype(BF16)
        for j, (mr_ref, ma_ref) in enumerate(((mr0, ma0), (mr1, ma1))):
            sl = slice(j * hw, (j + 1) * hw)
            y_ref[:, sl] = (_sig(mr_ref[...].astype(F32)) * ar[:, sl]
                            + _sig(ma_ref[...].astype(F32)) * aa[:, sl]).astype(BF16)
        out = _dot(y_ref[...], wo_ref[...])
        gate = gt_ref[...]
        diff = x_ref[...] + gate * out - t_ref[...]
        dxn = diff * (1.0 / D)
        dxn_ref[...] = dxn
        dout_ref[...] = (gate * dxn).astype(BF16)
        dg = jnp.sum(dxn * out, axis=0, keepdims=True)
        ls = jnp.broadcast_to(jnp.sum(diff * diff) * (0.5 / D), (1, 128))

        @pl.when(i == 0)
        def _():
            dg_ref[...] = dg
            loss_ref[...] = ls

        @pl.when(i > 0)
        def _():
            dg_ref[...] += dg
            loss_ref[...] += ls

    def cols(width, col0):
        return pl.BlockSpec((tm, width), lambda b, i: (b * bpb + i, col0 // width))

    def whole(rows):
        return pl.BlockSpec((rows, D), lambda b, i: (0, 0))

    row, wide = cols(D, 0), cols(RH * DV, 0)
    gates = [cols(DV, RG + h * DV) for h in range(RH)]
    merge_gates = [cols(hw, MR), cols(hw, MR + hw), cols(hw, MA), cols(hw, MA + hw)]
    return pl.pallas_call(
        body, name="merge_out", grid=(nb, bpb),
        in_specs=[wide, wide] + gates + [whole(RH * DV), row, whole(D)] + merge_gates
        + [whole(D), row, row, pl.BlockSpec((None, 1, D), lambda b, i: (b, 0, 2))],
        out_specs=[wide, row, row, row, row, row, pl.BlockSpec((None, 1, D), lambda b, i: (b, 0, 0)),
                   pl.BlockSpec((None, 1, 128), lambda b, i: (b, 0, 0))],
        out_shape=[_sds((t_rows, RH * DV), BF16)] + [_sds((t_rows, D), BF16)] * 3
        + [_sds((t_rows, D), F32), _sds((t_rows, D), BF16), _sds((nb, 1, D), F32), _sds((nb, 1, 128), F32)],
        compiler_params=_params(("parallel", "arbitrary")),
    )(o_f, o_b, *([px] * RH), w_o_ret16, yatt16, w_o_att16, px, px, px, px, w_out16, x2, tgt, mod3)


def _bwd_branches(dout16, w_out16, w_o_ret16, w_o_att16, px, a_ret, a_att, o_f, o_b, o_att, rows_all, tm):
    t_rows = dout16.shape[0]
    hw = D // 2

    def body(do_ref, wo_ref, wr_ref, wa_ref, mr0, mr1, ma0, ma1, ar_ref, aa_ref, rg0, rg1, rg2, rg3, of_ref, ob_ref,
             ag0, ag1, oa_ref, dar_ref, daa_ref, dor_ref, dao_ref, dl_ref, dp_ref):
        dy_all = _dot(do_ref[...], wo_ref[...], NT)
        for j, (mr_ref, ma_ref) in enumerate(((mr0, ma0), (mr1, ma1))):
            sl = slice(j * hw, (j + 1) * hw)
            dy = dy_all[:, sl]
            sr = _sig(mr_ref[...].astype(F32))
            sa = _sig(ma_ref[...].astype(F32))
            dar_ref[:, sl] = (dy * sr).astype(BF16)
            daa_ref[:, sl] = (dy * sa).astype(BF16)
            dp_ref[:, MR - RG + j * hw:MR - RG + (j + 1) * hw] = (
                dy * ar_ref[:, sl].astype(F32) * sr * (1.0 - sr)).astype(BF16)
            dp_ref[:, MA - RG + j * hw:MA - RG + (j + 1) * hw] = (
                dy * aa_ref[:, sl].astype(F32) * sa * (1.0 - sa)).astype(BF16)
        da_ret = dar_ref[...]
        for h, g_ref in enumerate((rg0, rg1, rg2, rg3)):
            sl = slice(h * DV, (h + 1) * DV)
            dy = _dot(da_ret, wr_ref[sl, :], NT)
            g = g_ref[...].astype(F32)
            o = of_ref[:, sl].astype(F32) + ob_ref[:, sl].astype(F32)
            r = lax.rsqrt(jnp.mean(o * o, axis=-1, keepdims=True) + EPS)
            on = o * r
            sg = _sig(g)
            don = dy * (g * sg)
            dp_ref[:, sl] = (dy * on * (sg * (1.0 + g * (1.0 - sg)))).astype(BF16)
            dor_ref[:, sl] = (r * (don - on * jnp.mean(on * don, axis=-1, keepdims=True))).astype(BF16)
        dy_all = _dot(daa_ref[...], wa_ref[...], NT)
        dl_ref[...] = jnp.zeros_like(dl_ref)
        for j, g_ref in enumerate((ag0, ag1)):
            sl = slice(j * hw, (j + 1) * hw)
            dy = dy_all[:, sl]
            g = g_ref[...].astype(F32)
            sg = _sig(g)
            dao = dy * (g * sg)
            dao_ref[:, sl] = dao.astype(BF16)
            prod = dao * oa_ref[:, sl]
            for r in range(hw // HD):
                dl_ref[:, j * 128 + r:j * 128 + r + 1] = jnp.sum(prod[:, r * HD:(r + 1) * HD], axis=-1, keepdims=True)
            dp_ref[:, AG - RG + j * hw:AG - RG + (j + 1) * hw] = (
                dy * oa_ref[:, sl] * (sg * (1.0 + g * (1.0 - sg)))).astype(BF16)

    def gate(h):
        return pl.BlockSpec((tm, DV), lambda i: (i, RG // DV + h))

    def whole(rows):
        return pl.BlockSpec((rows, D), lambda i: (0, 0))

    row = pl.BlockSpec((tm, D), lambda i: (i, 0))
    wide = pl.BlockSpec((tm, RH * DV), lambda i: (i, 0))
    return pl.pallas_call(
        body, name="bwd_branches", grid=(t_rows // tm,),
        in_specs=[row, whole(D), whole(RH * DV), whole(D)] + _gate_specs(tm, MR) + _gate_specs(tm, MA) + [row, row]
        + [gate(h) for h in range(RH)] + [wide, wide] + _gate_specs(tm, AG) + [row],
        out_specs=[row, row, wide, row, pl.BlockSpec((tm, HKV * 128), lambda i: (i, 0)),
                   pl.BlockSpec((pl.Element(tm), pl.Element(IN_COLS - RG)), lambda i: (i * tm, RG))],
        out_shape=[_sds((t_rows, D), BF16)] * 2 + [_sds((t_rows, RH * DV), BF16), _sds((t_rows, D), BF16),
                                                  _sds((t_rows, HKV * 128), F32), _sds((rows_all, IN_COLS), BF16)],
        compiler_params=_params(("parallel",)),
    )(dout16, w_out16, w_o_ret16, w_o_att16, px, px, px, px, a_ret, a_att, *([px] * RH), o_f, o_b, px, px, o_att)


def _att_bwd(q16, kx16, kc16, px, dao16, delta, lse, q_norm_w, cos, sin, dp_all, nb, seq, cx, tq, after=()):
    t_rows = nb * seq
    nq = seq // tq
    rep = HQ // HKV
    gw = rep * HD
    scale = HD ** -0.5

    def body(q_ref, kx_ref, kc_ref, vx_ref, vc_ref, dao_ref, dl_ref, l_ref, xq_ref, w_ref, c_ref, s_ref, *rest):
        daq_ref, gq_ref, dkx_ref, dvx_ref, dkc_ref, dvc_ref = rest[1 + len(after):7 + len(after)]
        accs = rest[7 + len(after):]
        i = pl.program_id(2)
        first = jnp.logical_and(jnp.logical_and(pl.program_id(0) == 0, pl.program_id(1) == 0), i == 0)
        gq = jnp.zeros((1, HD), F32)
        kx = kx_ref[...]
        kc = kc_ref[...]
        vx = vx_ref[...].astype(BF16)
        vc = vc_ref[...].astype(BF16)
        @pl.when(i == 0)
        def _():
            for acc in accs:
                acc[...] = jnp.zeros_like(acc)

        dkx, dvx, dkc, dvc = [acc[...] for acc in accs]
        for r in range(rep):
            sl = slice(r * HD, (r + 1) * HD)
            q = q_ref[:, sl]
            lr = l_ref[:, r:r + 1]
            p1 = jnp.exp2(_dot(q, kx, NT) * SM_C - lr)
            p2 = jnp.exp2(_dot(q, kc, NT) * SM_C - lr)
            da16 = dao_ref[:, sl]
            delta = dl_ref[:, r:r + 1]
            ds1 = (p1 * (_dot(da16, vx, NT) - delta)).astype(BF16)
            ds2 = (p2 * (_dot(da16, vc, NT) - delta)).astype(BF16)
            dq = (_dot(ds1, kx) + _dot(ds2, kc)) * scale
            dkx += _dot(q, ds1, TN)
            dkc += _dot(q, ds2, TN)
            dvx += _dot(da16, p1.astype(BF16), TN)
            dvc += _dot(da16, p2.astype(BF16), TN)
            dt = dq * c_ref[...] + _swap_pairs(dq * s_ref[...])
            xv = xq_ref[:, sl].astype(F32)
            rn = lax.rsqrt(jnp.mean(xv * xv, axis=-1, keepdims=True) + EPS)
            xh = xv * rn
            dxh = dt * w_ref[...]
            daq_ref[:, sl] = (rn * (dxh - xh * jnp.mean(dxh * xh, axis=-1, keepdims=True))).astype(BF16)
            gq += jnp.sum(dt * xh, axis=0, keepdims=True)
        for acc, val in zip(accs, (dkx, dvx, dkc, dvc)):
            acc[...] = val

        @pl.when(first)
        def _():
            gq_ref[...] = gq

        @pl.when(jnp.logical_not(first))
        def _():
            gq_ref[...] += gq

        @pl.when(i == nq - 1)
        def _():
            dkx_ref[...] = dkx.T * scale
            dvx_ref[...] = dvx.T
            dkc_ref[...] = dkc.T * scale
            dvc_ref[...] = dvc.T

    qblk = pl.BlockSpec((tq, gw), lambda b, g, i: (b * nq + i, g))
    kxb = pl.BlockSpec((None, seq, HD), lambda b, g, i: (b, 0, g))
    kcb = pl.BlockSpec((None, cx, HD), lambda b, g, i: (b, 0, g))
    table = pl.BlockSpec((tq, HD), lambda b, g, i: (i, 0))
    one = pl.BlockSpec((1, HD), lambda b, g, i: (0, 0))
    lane = pl.BlockSpec((tq, 128), lambda b, g, i: (b * nq + i, g))
    return pl.pallas_call(
        body, name="att_bwd", grid=(nb, HKV, nq),
        in_specs=[qblk,
                  pl.BlockSpec((seq, HD), lambda b, g, i: (b, g)),
                  pl.BlockSpec((cx, HD), lambda b, g, i: (b, g)),
                  pl.BlockSpec((seq, HD), lambda b, g, i: (b, AV // HD + g)),
                  pl.BlockSpec((cx, HD), lambda b, g, i: (t_rows // cx + b, AV // HD + g)),
                  qblk, lane, lane,
                  pl.BlockSpec((tq, gw), lambda b, g, i: (b * nq + i, AQ // gw + g)), one, table, table]
        + [pl.BlockSpec(memory_space=pl.ANY)] * (1 + len(after)),
        out_specs=[pl.BlockSpec((tq, gw), lambda b, g, i: (b * nq + i, AQ // gw + g)), one, kxb, kxb, kcb, kcb],
        out_shape=[_sds(dp_all.shape, BF16), _sds((1, HD), F32), _sds((nb, seq, HKV * HD), F32),
                   _sds((nb, seq, HKV * HD), F32), _sds((nb, cx, HKV * HD), F32), _sds((nb, cx, HKV * HD), F32)],
        scratch_shapes=[pltpu.VMEM((HD, seq), F32), pltpu.VMEM((HD, seq), F32), pltpu.VMEM((HD, cx), F32),
                        pltpu.VMEM((HD, cx), F32)],
        input_output_aliases={12: 0},
        compiler_params=_params(("arbitrary", "arbitrary", "arbitrary")),
    )(q16, kx16, kc16, px, px, dao16, delta, lse, px, q_norm_w, cos, sin, dp_all, *after)


def _qk_prep_bwd(dt, px, nw, cos, sin, rows, row_off, col_off, heads, hb, seq, tm, name):
    rope = cos is not None
    rb0 = row_off // tm
    pb = seq // tm if rope else 1
    bw = hb * HD

    def body(*refs):
        if rope:
            d_ref, x_ref, w_ref, c_ref, s_ref, dx_ref, dw_ref = refs
        else:
            d_ref, x_ref, w_ref, dx_ref, dw_ref = refs
        first = jnp.logical_and(pl.program_id(0) == 0, pl.program_id(1) == 0)
        dw = jnp.zeros((1, HD), F32)
        for h in range(hb):
            sl = slice(h * HD, (h + 1) * HD)
            dtv = d_ref[:, sl]
            if rope:
                dtv = dtv * c_ref[...] + _swap_pairs(dtv * s_ref[...])
            xv = x_ref[:, sl].astype(F32)
            r = lax.rsqrt(jnp.mean(xv * xv, axis=-1, keepdims=True) + EPS)
            xh = xv * r
            dxh = dtv * w_ref[...]
            dx_ref[:, sl] = (r * (dxh - xh * jnp.mean(dxh * xh, axis=-1, keepdims=True))).astype(BF16)
            dw += jnp.sum(dtv * xh, axis=0, keepdims=True)

        @pl.when(first)
        def _():
            dw_ref[...] = dw

        @pl.when(jnp.logical_not(first))
        def _():
            dw_ref[...] += dw

    blk = pl.BlockSpec((tm, bw), lambda i, j: (i, j))
    in_specs = [blk, pl.BlockSpec((tm, bw), lambda i, j: (rb0 + i, col_off // bw + j)),
                pl.BlockSpec((1, HD), lambda i, j: (0, 0))]
    args = [dt, px, nw]
    if rope:
        in_specs += [pl.BlockSpec((tm, HD), lambda i, j: (i % pb, 0))] * 2
        args += [cos, sin]
    return pl.pallas_call(
        body, name=name, grid=(rows // tm, heads // hb), in_specs=in_specs,
        out_specs=[blk, pl.BlockSpec((1, HD), lambda i, j: (0, 0))],
        out_shape=[_sds((rows, heads * HD), BF16), _sds((1, HD), F32)],
        compiler_params=_params(("arbitrary", "arbitrary")),
    )(*args)


def _ret_bwd(px, lg, do16, hist_f, hist_b, nb, nc, cx):
    t_rows = nb * nc * CH

    def body(lg_ref, *refs):
        ins = (refs[0:5], refs[7:12])
        do_refs = (refs[5], refs[12])
        h_refs = (refs[6], refs[13])
        ctx_refs = refs[14:17]
        outs = (refs[17:20], refs[20:23])
        dck_ref, dcv_ref, dlg_ref = refs[23:26]
        dss = (refs[26], refs[27])
        c = pl.program_id(1)

        @pl.when(c == 0)
        def _():
            dss[0][...] = jnp.zeros_like(dss[0])
            dss[1][...] = jnp.zeros_like(dss[1])
            dlg_ref[...] = jnp.zeros_like(dlg_ref)

        for d in range(2):
            dq_ref, dk_ref, dv_ref = outs[d]
            for h in range(RH):
                lg_d = lg_ref[d, h]
                mask, relf, qd, qe, kd, ke = _decays(lg_d, d == 0)
                g_ch = jnp.exp(lg_d * CH)
                q, k, v16 = _head_qkv(ins[d], h)
                q16 = q.astype(BF16)
                k16 = k.astype(BF16)
                do16v = do_refs[d][:, h * DV:(h + 1) * DV]
                st16 = h_refs[d][h]
                dst = dss[d][h]
                dst16 = dst.astype(BF16)
                a = _dot(q16, k16, NT) * mask
                dp = _dot(do16v, v16, NT)
                da16 = (dp * mask).astype(BF16)
                dq_cross = _dot(do16v, st16, NT) * qd
                dq_ref[:, h * DK:(h + 1) * DK] = (_dot(da16, k16) + dq_cross).astype(BF16)
                dk_state = _dot(v16, dst16, NT) * kd
                dk_ref[:, h * DK:(h + 1) * DK] = ((_dot(da16, q16, TN) + dk_state) * (DK ** -0.5)).astype(BF16)
                dv = _dot(a.astype(BF16), do16v, TN) + _dot((k * kd).astype(BF16), dst16)
                dv_ref[:, h * DV:(h + 1) * DV] = dv.astype(BF16)
                dlg = (jnp.sum(relf * a * dp)
                       + jnp.sum(qe * jnp.sum(q * dq_cross, axis=-1, keepdims=True))
                       + jnp.sum(ke * jnp.sum(k * dk_state, axis=-1, keepdims=True))
                       + CH * g_ch * jnp.sum(dst * st16.astype(F32)))
                row = d * RH + h
                dlg_ref[row:row + 1, :] += jnp.broadcast_to(dlg, (1, 128))
                dss[d][h] = g_ch * dst + _dot((q * qd).astype(BF16), do16v, TN)

        @pl.when(c == nc - 1)
        def _():
            pos = lax.broadcasted_iota(jnp.int32, (cx, 1), 0).astype(F32)
            for h in range(RH):
                k, v16 = _ctx_kv(ctx_refs, h)
                dk = jnp.zeros((cx, DK), F32)
                dv = jnp.zeros((cx, DV), F32)
                for d, e in enumerate((cx - 1.0 - pos, pos)):
                    w = jnp.exp(lg_ref[d, h] * e)
                    ds16 = dss[d][h].astype(BF16)
                    t = _dot(v16, ds16, NT)
                    dk += t * w
                    dv += _dot((k * w).astype(BF16), ds16)
                    dlg = jnp.sum(e * w * jnp.sum(k * t, axis=-1, keepdims=True))
                    row = d * RH + h
                    dlg_ref[row:row + 1, :] += jnp.broadcast_to(dlg, (1, 128))
                dck_ref[:, h * DK:(h + 1) * DK] = (dk * (DK ** -0.5)).astype(BF16)
                dcv_ref[:, h * DV:(h + 1) * DV] = dv.astype(BF16)

    def fw(b, c):
        return b * nc + nc - 1 - c

    def bw(b, c):
        return b * nc + c

    def rows(rowf, width):
        return pl.BlockSpec((CH, width), lambda b, c: (rowf(b, c), 0))

    def hist(rowf):
        return pl.BlockSpec((None, None, RH, DK, DV), lambda b, c: (b, rowf(0, c), 0, 0, 0))

    in_specs = [pl.BlockSpec(memory_space=pltpu.SMEM)]
    out_specs = []
    for rowf in (fw, bw):
        in_specs += _wide_specs(rowf) + [rows(rowf, RH * DV), hist(rowf)]
        out_specs += [rows(rowf, RH * DK), rows(rowf, RH * DK), rows(rowf, RH * DV)]
    in_specs += _ctx_specs(t_rows, cx)
    out_specs += [pl.BlockSpec((cx, RH * DK), lambda b, c: (b, 0)), pl.BlockSpec((cx, RH * DV), lambda b, c: (b, 0)),
                  pl.BlockSpec((None, 8, 128), lambda b, c: (b, 0, 0))]
    qk = _sds((t_rows, RH * DK), BF16)
    vv = _sds((t_rows, RH * DV), BF16)
    return pl.pallas_call(
        body, name="ret_bwd", grid=(nb, nc), in_specs=in_specs, out_specs=out_specs,
        out_shape=[qk, qk, vv, qk, qk, vv, _sds((nb * cx, RH * DK), BF16), _sds((nb * cx, RH * DV), BF16),
                   _sds((nb, 8, 128), F32)],
        scratch_shapes=[pltpu.VMEM((RH, DK, DV), F32), pltpu.VMEM((RH, DK, DV), F32)],
        compiler_params=_params(("parallel", "arbitrary")),
    )(lg, *([px] * 5), do16, hist_f, *([px] * 5), do16, hist_b, *([px] * 3))


def _assemble_lat(dp_all, dk_f, dk_b, dv_f, dv_b, dak16, dvx, dq_f, dq_b, tm):
    t_rows = dk_f.shape[0]

    def body(_, dkf, dkb, dvf, dvb, dak, dav, dqf, dqb, o_ref):
        o_ref[:, RK:RK + RH * DK] = (dkf[...].astype(F32) + dkb[...].astype(F32)).astype(BF16)
        o_ref[:, RV:RV + RH * DV] = (dvf[...].astype(F32) + dvb[...].astype(F32)).astype(BF16)
        o_ref[:, AK:AK + HKV * HD] = dak[...]
        o_ref[:, AV:AV + HKV * HD] = dav[...].astype(BF16)
        o_ref[:, RQ:RQ + RH * DK] = (dqf[...].astype(F32) + dqb[...].astype(F32)).astype(BF16)

    args = (dk_f, dk_b, dv_f, dv_b, dak16, dvx, dq_f, dq_b)
    return pl.pallas_call(
        body, name="assemble_lat", grid=(t_rows // tm,),
        in_specs=[pl.BlockSpec(memory_space=pl.ANY)]
        + [pl.BlockSpec((tm, a.shape[1]), lambda i: (i, 0)) for a in args],
        out_specs=pl.BlockSpec((tm, RG), lambda i: (i, 0)), out_shape=_sds(dp_all.shape, BF16),
        input_output_aliases={0: 0},
        compiler_params=_params(("parallel",)),
    )(dp_all, *args)


def _assemble_ctx(dp_all, dck16, dcv16, dcak16, dvc, t_rows, tm):
    c_rows = dck16.shape[0]
    rb = t_rows // tm

    def body(_, dck, dcv, dcak, dcav, o_ref):
        o_ref[:, RK:RK + RH * DK] = dck[...]
        o_ref[:, RV:RV + RH * DV] = dcv[...]
        o_ref[:, AK:AK + HKV * HD] = dcak[...]
        o_ref[:, AV:AV + HKV * HD] = dcav[...].astype(BF16)
        o_ref[:, KV_COLS:] = jnp.zeros((tm, IN_COLS - KV_COLS), BF16)

    args = (dck16, dcv16, dcak16, dvc)
    return pl.pallas_call(
        body, name="assemble_ctx", grid=(c_rows // tm,),
        in_specs=[pl.BlockSpec(memory_space=pl.ANY)]
        + [pl.BlockSpec((tm, a.shape[1]), lambda i: (i, 0)) for a in args],
        out_specs=pl.BlockSpec((tm, IN_COLS), lambda i: (rb + i, 0)), out_shape=_sds(dp_all.shape, BF16),
        input_output_aliases={0: 0},
        compiler_params=_params(("parallel",)),
    )(dp_all, *args)


def _norm_bwd(dh, x2, mod3, norm_w, dxn, row_off, rows_per_group, group0, tm, name):
    with_dx = dxn is not None
    rows = x2.shape[0]
    rb0 = row_off // tm
    bpg = rows_per_group // tm
    ngroups = rows // rows_per_group

    def body(*refs):
        if with_dx:
            dh_ref, x_ref, sc_ref, nw_ref, dxn_ref, dx_ref, dsh_ref, dsc_ref, dnw_ref = refs
        else:
            dh_ref, x_ref, sc_ref, nw_ref, dsh_ref, dsc_ref, dnw_ref = refs
        i = pl.program_id(0)
        dhv = dh_ref[...]
        xv = x_ref[...]
        nw = nw_ref[...]
        r = lax.rsqrt(jnp.mean(xv * xv, axis=-1, keepdims=True) + EPS)
        xh = xv * r
        dm = dhv * (1.0 + sc_ref[...])
        dsh = jnp.sum(dhv, axis=0, keepdims=True)
        dsc = jnp.sum(dhv * (xh * nw), axis=0, keepdims=True)
        dnw = jnp.sum(dm * xh, axis=0, keepdims=True)
        if with_dx:
            dxh = dm * nw
            dx_ref[...] = dxn_ref[...] + r * (dxh - xh * jnp.mean(dxh * xh, axis=-1, keepdims=True))

        @pl.when(i % bpg == 0)
        def _():
            dsh_ref[...] = dsh
            dsc_ref[...] = dsc

        @pl.when(i % bpg != 0)
        def _():
            dsh_ref[...] += dsh
            dsc_ref[...] += dsc

        @pl.when(i == 0)
        def _():
            dnw_ref[...] = dnw

        @pl.when(i > 0)
        def _():
            dnw_ref[...] += dnw

    grp = pl.BlockSpec((None, 1, D), lambda i: (i // bpg, 0, 0))
    in_specs = [pl.BlockSpec((tm, D), lambda i: (rb0 + i, 0)), pl.BlockSpec((tm, D), lambda i: (i, 0)),
                pl.BlockSpec((None, 1, D), lambda i: (group0 + i // bpg, 0, 1)),
                pl.BlockSpec((1, D), lambda i: (0, 0))]
    args = [dh, x2, mod3, norm_w]
    out_specs = [grp, grp, pl.BlockSpec((1, D), lambda i: (0, 0))]
    out_shape = [_sds((ngroups, 1, D), F32), _sds((ngroups, 1, D), F32), _sds((1, D), F32)]
    if with_dx:
        in_specs.append(pl.BlockSpec((tm, D), lambda i: (i, 0)))
        args.append(dxn)
        out_specs.insert(0, pl.BlockSpec((tm, D), lambda i: (i, 0)))
        out_shape.insert(0, _sds((rows, D), F32))
    return pl.pallas_call(
        body, name=name, grid=(rows // tm,), in_specs=in_specs, out_specs=out_specs, out_shape=out_shape,
        compiler_params=_params(("arbitrary",)),
    )(*args)


def _small_final(dmod_all, dmodc_parts, c_rows, dm_loc_rows, nw_parts, misc_parts, c_ctx, r_pad, w_ada16):
    loc = dm_loc_rows.shape[1]

    def body(dm_ref, dmc_ref, c_ref, dml_ref, nwp_ref, mp_ref, cc_ref, r_ref, w_ref,
             gb_ref, gc_ref, gnw_ref, misc_ref, gwa_ref):
        dmc = jnp.sum(dmc_ref[...], axis=0, keepdims=True)
        gb_ref[...] = jnp.sum(dm_ref[...], axis=0, keepdims=True) + dmc
        dsc = _dot(jnp.broadcast_to(dmc, (8, 3 * D)).astype(BF16), w_ref[...], NT)[0:1, :]
        gc_ref[...] = dsc * _dsilu(cc_ref[...])
        gnw_ref[...] = jnp.sum(nwp_ref[...], axis=0, keepdims=True)
        misc = jnp.sum(mp_ref[...], axis=0, keepdims=True)
        y = jnp.exp2(r_ref[...])
        lane = lax.broadcasted_iota(jnp.int32, (1, D), 1)
        is_decay = jnp.logical_and(lane >= 2 * HD, lane < 2 * HD + 2 * RH)
        misc_ref[...] = misc * jnp.where(is_decay, -(y * np.float32(np.log(2.0))) / (1.0 - y), 1.0)
        gwa_ref[...] = _dot(_silu(c_ref[...]).astype(BF16), dml_ref[...].astype(BF16), TN)

    return pl.pallas_call(
        body, name="small_final",
        out_shape=[_sds((1, 3 * D), F32), _sds((1, D), F32), _sds((1, D), F32), _sds((1, D), F32), _sds((D, loc), F32)],
        compiler_params=pltpu.CompilerParams(vmem_limit_bytes=VMEM_LIMIT),
    )(dmod_all, dmodc_parts, c_rows, dm_loc_rows, nw_parts, misc_parts, c_ctx, r_pad, w_ada16)


def _adamw_math(w, g, m, v):
    nm = B1 * m + (1.0 - B1) * g
    nv = B2 * v + (1.0 - B2) * (g * g)
    return -LR * ((nm / (1.0 - B1 ** STEP)) / (jnp.sqrt(nv / (1.0 - B2 ** STEP)) + ADAM_EPS) + WD * w), nm, nv


def _adamw(w, g, m, v, name):
    rows, cols = w.shape
    tm = _pick(rows, 448, 8)

    def body(w_ref, g_ref, m_ref, v_ref, d_ref, nm_ref, nv_ref):
        d_ref[...], nm_ref[...], nv_ref[...] = _adamw_math(w_ref[...], g_ref[...], m_ref[...], v_ref[...])

    blk = pl.BlockSpec((tm, cols), lambda i: (i, 0))
    return pl.pallas_call(
        body, name=name, grid=(rows // tm,), in_specs=[blk] * 4, out_specs=[blk] * 3,
        out_shape=[_sds((rows, cols), F32)] * 3, compiler_params=_params(("parallel",)),
    )(w, g, m, v)


def _mesh_pos():
    return lax.axis_index("x"), lax.axis_index("y"), lax.axis_index("c")


def _all_gather(arrs, name):
    n = len(arrs)

    def body(*refs):
        ins, outs = refs[:n], refs[n:2 * n]
        send_sems, recv_sems, local_sems = refs[2 * n:]
        x, y, c = _mesh_pos()
        me, sib = (x, y, c), (x, y, 1 - c)
        chips = [(1 - x, y), (x, 1 - y), (1 - x, 1 - y)]

        def slot(p):
            return 4 * p[0] + 2 * p[1] + p[2]

        def copy(a, k, block, to, own):
            dst = outs[a].at[slot(block)]
            return pltpu.make_async_remote_copy(
                src_ref=ins[a] if own else dst, dst_ref=dst, send_sem=send_sems.at[a, k], recv_sem=recv_sems.at[a, k],
                device_id=to, device_id_type=MESH_T)

        mine = [pltpu.make_async_copy(ins[a], outs[a].at[slot(me)], local_sems.at[a]) for a in range(n)]
        for cp in mine:
            cp.start()
        first = []
        for a in range(n):
            first.append(copy(a, 0, me, sib, True))
            first += [copy(a, 1 + j, me, (*chip, c), True) for j, chip in enumerate(chips)]
        for cp in first:
            cp.start()
        passed = []
        for j, chip in enumerate(chips):
            for a in range(n):
                copy(a, 1 + j, (*chip, c), me, False).wait_recv()
                fwd = copy(a, 4 + j, (*chip, c), sib, False)
                fwd.start()
                passed.append(fwd)
        for a in range(n):
            copy(a, 0, sib, me, False).wait_recv()
            for j, chip in enumerate(chips):
                copy(a, 4 + j, (*chip, 1 - c), me, False).wait_recv()
        for cp in first + passed:
            cp.wait_send()
        for cp in mine:
            cp.wait()

    hbm = pl.BlockSpec(memory_space=pl.ANY)
    return pl.pallas_call(
        body, name=name, in_specs=[hbm] * n, out_specs=[hbm] * n,
        out_shape=[_sds((N_DEV,) + a.shape, a.dtype) for a in arrs],
        scratch_shapes=[pltpu.SemaphoreType.DMA((n, 7)), pltpu.SemaphoreType.DMA((n, 7)), pltpu.SemaphoreType.DMA((n,))],
    )(*arrs)


def _pair_exchange(arrs, name):
    n = len(arrs)

    def body(*refs):
        ins, outs = refs[:n], refs[n:2 * n]
        send_sems, recv_sems = refs[2 * n:]
        x, y, c = _mesh_pos()
        sib = (x, y, 1 - c)
        sends = []
        for a in range(n):
            for k in range(4):
                sends.append(pltpu.make_async_remote_copy(
                    src_ref=ins[a].at[2 * k + 1 - c], dst_ref=outs[a].at[k], send_sem=send_sems.at[a, k],
                    recv_sem=recv_sems.at[a, k], device_id=sib, device_id_type=MESH_T))
        for cp in sends:
            cp.start()
        for cp in sends:
            cp.wait_recv()
        for cp in sends:
            cp.wait_send()

    hbm = pl.BlockSpec(memory_space=pl.ANY)
    return pl.pallas_call(
        body, name=name, in_specs=[hbm] * n, out_specs=[hbm] * n,
        out_shape=[_sds((4,) + a.shape[1:], a.dtype) for a in arrs],
        scratch_shapes=[pltpu.SemaphoreType.DMA((n, 4)), pltpu.SemaphoreType.DMA((n, 4))],
    )(*arrs)


def _pair_add(part, got, core, name):
    _, rows, cols = part.shape
    tm = _pick(rows, 672, 16)
    p4 = part.reshape(4, 2, rows, cols)

    def body(core_ref, p_ref, g_ref, o_ref):
        o_ref[...] = (p_ref[...].astype(F32) + g_ref[...].astype(F32)).astype(BF16)

    blk = pl.BlockSpec((None, tm, cols), lambda k, i, cr: (k, i, 0))
    return pl.pallas_call(
        body, name=name,
        grid_spec=pltpu.PrefetchScalarGridSpec(
            num_scalar_prefetch=1, grid=(4, rows // tm),
            in_specs=[pl.BlockSpec((None, None, tm, cols), lambda k, i, cr: (k, cr[0], i, 0)), blk], out_specs=blk),
        out_shape=_sds((4, rows, cols), BF16), compiler_params=_params(("parallel", "parallel")),
    )(core, p4, got)


def _chip_sum_adamw(pair_sums, landed, chip, w, m, v, name):
    _, rows, cols = pair_sums.shape
    tm = _pick(rows, 448, 16)

    def body(chip_ref, s_ref, l_ref, w_ref, m_ref, v_ref, g_ref, d_ref, nm_ref, nv_ref):
        acc = s_ref[...].astype(F32)
        for j in range(3):
            acc = acc + l_ref[j].astype(F32)
        g_ref[...] = acc
        d_ref[...], nm_ref[...], nv_ref[...] = _adamw_math(w_ref[...], acc, m_ref[...], v_ref[...])

    blk = pl.BlockSpec((tm, cols), lambda i, ch: (i, 0))
    return pl.pallas_call(
        body, name=name,
        grid_spec=pltpu.PrefetchScalarGridSpec(
            num_scalar_prefetch=1, grid=(rows // tm,),
            in_specs=[pl.BlockSpec((None, tm, cols), lambda i, ch: (ch[0], i, 0)),
                      pl.BlockSpec((3, tm, cols), lambda i, ch: (0, i, 0)), blk, blk, blk],
            out_specs=[blk] * 4),
        out_shape=[_sds((rows, cols), F32)] * 4, compiler_params=_params(("parallel",)),
    )(chip, pair_sums, landed, w, m, v)


_HBM = pl.BlockSpec(memory_space=pltpu.HBM)
_SEM = pl.BlockSpec(memory_space=pltpu.SEMAPHORE)
_EFFECT = pltpu.SideEffectType.DATAFLOW_SIDE_EFFECTING


def _chip_routes(n):
    def plan(x, y, c):
        routes = []
        for a in range(n):
            for j in range(1, 4):
                px, py = x ^ (j >> 1), y ^ (j & 1)
                routes.append((a, 2 * px + py, (px, py, c), j - 1))
        return routes
    return plan, 3 * n


def _bcast_routes(n):
    def plan(x, y, c):
        routes = []
        for a in range(n):
            for k in range(1, N_DEV):
                peer = (x ^ ((k >> 2) & 1), y ^ ((k >> 1) & 1), c ^ (k & 1))
                routes.append((a, 0, peer, 4 * x + 2 * y + c))
        return routes
    return plan, 7 * n


def _route_copies(srcs, lands, send_sems, recv_sems, routes):
    return [pltpu.make_async_remote_copy(
        src_ref=srcs[a].at[sb], dst_ref=lands[a].at[lb], send_sem=send_sems.at[r], recv_sem=recv_sems.at[r],
        device_id=peer, device_id_type=MESH_T) for r, (a, sb, peer, lb) in enumerate(routes)]


def _exchange_start(srcs, lands, routes, name, after=()):
    plan, count = routes
    n = len(srcs)
    n_in = 2 * n + len(after)

    def body(*refs):
        send_sems, recv_sems = refs[n_in], refs[n_in + 1]
        token = refs[-1]
        for cp in _route_copies(refs[:n], refs[n:2 * n], send_sems, recv_sems, plan(*_mesh_pos())):
            cp.start()
        token[...] = jnp.zeros_like(token)

    args = [pltpu.with_memory_space_constraint(a, pltpu.HBM) for a in list(srcs) + list(lands)]
    out = pl.pallas_call(
        body, name=name,
        out_shape=(pltpu.SemaphoreType.DMA((count,)), pltpu.SemaphoreType.DMA((count,)),
                   *[pltpu.HBM(a.shape, a.dtype) for a in args], _sds((8, 128), F32)),
        in_specs=[_HBM] * (2 * n) + [pl.BlockSpec(memory_space=pl.ANY)] * len(after),
        out_specs=(_SEM, _SEM, *([_HBM] * (2 * n)), pl.BlockSpec(memory_space=pltpu.VMEM)),
        input_output_aliases={i: 2 + i for i in range(2 * n)},
        compiler_params=pltpu.CompilerParams(has_side_effects=_EFFECT),
    )(*args, *after)
    return (out[0], out[1], list(out[2:2 + 2 * n]), routes), out[-1]


def _exchange_wait(state, after, name):
    send_sems, recv_sems, bufs, (plan, count) = state
    n = len(bufs) // 2

    def body(*refs):
        send_s, recv_s = refs[2 * n], refs[2 * n + 1]
        for cp in _route_copies(refs[:n], refs[n:2 * n], send_s, recv_s, plan(*_mesh_pos())):
            cp.wait_send()
            cp.wait_recv()

    out = pl.pallas_call(
        body, name=name, out_shape=tuple(pltpu.HBM(a.shape, a.dtype) for a in bufs),
        in_specs=[_HBM] * (2 * n) + [_SEM, _SEM, pl.BlockSpec(memory_space=pl.ANY)], out_specs=tuple([_HBM] * (2 * n)),
        input_output_aliases={i: i for i in range(2 * n)},
        compiler_params=pltpu.CompilerParams(has_side_effects=_EFFECT),
    )(*bufs, send_sems, recv_sems, after)
    return list(out[:n]), list(out[n:])


def _group_routes(js):
    def plan(x, y, c):
        return [(0, 0, (x ^ (j >> 1), y ^ (j & 1), c), 2 * j + c) for j in js]
    return plan, len(js)


def _pair_fill(groups, js, name, after=()):
    def body(*refs):
        g_ref, send_sems, recv_sems = refs[-3:]
        x, y, c = _mesh_pos()
        sends = []
        for n, j in enumerate(js):
            mine = g_ref.at[2 * j + c]
            sends.append(pltpu.make_async_remote_copy(
                src_ref=mine, dst_ref=mine, send_sem=send_sems.at[n], recv_sem=recv_sems.at[n],
                device_id=(x, y, 1 - c), device_id_type=MESH_T))
        for cp in sends:
            cp.start()
        for n, j in enumerate(js):
            pltpu.make_async_remote_copy(
                src_ref=g_ref.at[2 * j + c], dst_ref=g_ref.at[2 * j + 1 - c], send_sem=send_sems.at[n],
                recv_sem=recv_sems.at[n], device_id=(x, y, 1 - c), device_id_type=MESH_T).wait_recv()
        for cp in sends:
            cp.wait_send()

    hbm = pl.BlockSpec(memory_space=pl.ANY)
    return pl.pallas_call(
        body, name=name, in_specs=[hbm] * (1 + len(after)), out_specs=hbm, out_shape=_sds(groups.shape, groups.dtype),
        input_output_aliases={0: 0},
        scratch_shapes=[pltpu.SemaphoreType.DMA((len(js),)), pltpu.SemaphoreType.DMA((len(js),))],
    )(groups, *after)


def _in_proj_group(h_all, groups, j0, ng, chip, px_prev, after, name):
    rows_all = h_all.shape[0]
    gcols = IN_COLS // 4
    tm = _pick(rows_all, 1536, 128)
    g4 = groups.reshape(4, gcols, D)

    n_lead = (1 if px_prev is not None else 0) + len(after)
    lead = ([px_prev] if px_prev is not None else []) + list(after)

    def body(chip_ref, *refs):
        h_ref, w_ref, o_ref = refs[n_lead:]
        o_ref[...] = _dot(h_ref[...], w_ref[...], NT).astype(BF16)

    return pl.pallas_call(
        body, name=name,
        grid_spec=pltpu.PrefetchScalarGridSpec(
            num_scalar_prefetch=1, grid=(ng, rows_all // tm),
            in_specs=[pl.BlockSpec(memory_space=pl.ANY)] * n_lead
            + [pl.BlockSpec((tm, D), lambda n, i, ch: (i, 0)),
               pl.BlockSpec((None, gcols, D), lambda n, i, ch: (j0 + n, 0, 0))],
            out_specs=pl.BlockSpec((tm, gcols), lambda n, i, ch: (i, ch[0] ^ (j0 + n)))),
        out_shape=_sds((rows_all, IN_COLS), BF16),
        input_output_aliases={1: 0} if px_prev is not None else {},
        compiler_params=_params(("parallel", "parallel")),
    )(chip, *lead, h_all, g4)


def _d_h_groups(dp_all, groups, chip, after):
    rows_all = dp_all.shape[0]
    gcols = IN_COLS // 4
    tm = _pick(rows_all, 1536, 128)
    g4 = groups.reshape(4, gcols, D)
    n_lead = len(after)

    def body(chip_ref, *refs):
        a_ref, w_ref, o_ref = refs[n_lead:]
        j = pl.program_id(1)
        part = _dot(a_ref[...], w_ref[...])

        @pl.when(j == 0)
        def _():
            o_ref[...] = part

        @pl.when(j > 0)
        def _():
            o_ref[...] += part

    return pl.pallas_call(
        body, name="d_h",
        grid_spec=pltpu.PrefetchScalarGridSpec(
            num_scalar_prefetch=1, grid=(rows_all // tm, 4),
            in_specs=[pl.BlockSpec(memory_space=pl.ANY)] * n_lead
            + [pl.BlockSpec((tm, gcols), lambda i, j, ch: (i, ch[0] ^ j)),
               pl.BlockSpec((None, gcols, D), lambda i, j, ch: (j, 0, 0))],
            out_specs=pl.BlockSpec((tm, D), lambda i, j, ch: (i, 0))),
        out_shape=_sds((rows_all, D), F32),
        compiler_params=_params(("parallel", "arbitrary")),
    )(chip, *after, dp_all, g4)


def _reduce_scatter_start(parts, core, name):
    got = _pair_exchange(parts, name + "_pair")
    sums = [_pair_add(p, g, core, "%s_add_%d" % (name, i)) for i, (p, g) in enumerate(zip(parts, got))]
    lands = [lax.empty((3,) + s_.shape[1:], BF16) for s_ in sums]
    return _exchange_start(sums, lands, _chip_routes(len(sums)), name + "_start")


def _reduce_scatter_finish(rs_state, after, chip, wmv, name):
    sums, landed = _exchange_wait(rs_state, after, name + "_wait")
    return [_chip_sum_adamw(s_, l_, chip, *t, "%s_adamw_%d" % (name, i))
            for i, (s_, l_, t) in enumerate(zip(sums, landed, wmv))]


def _local_step(x, c, ctx, norm_w, ret_log2_decay, q_norm_w, k_norm_w, loss_target,
                mod, proj_in, proj_back, get_w_o, on_out_grads, on_in_grad, started=()):
    nb, seq, _ = x.shape
    cx = ctx.shape[1]
    t_rows, c_rows = nb * seq, nb * cx
    rows_all = t_rows + c_rows
    nc = seq // CH
    tm = _pick(seq, 256, 128)
    te = _pick(seq, 512, 128)
    assert cx % tm == 0 and t_rows % cx == 0 and seq % GRID_W == 0

    x2 = x.reshape(t_rows, D)
    ctx2 = ctx.reshape(c_rows, D)
    tgt = loss_target.reshape(t_rows, D)
    lg = _log_gamma(ret_log2_decay)
    cos, sin = _rope_tables(seq)

    mod3 = mod[:, None, :]
    h_all = _norm_fwd(x2, mod3, norm_w, rows_all, 0, seq, 0, None, te, "norm_fwd", after=started)
    h_all = _norm_fwd(ctx2, mod3, norm_w, rows_all, t_rows, c_rows, nb, h_all, tm, "norm_fwd_ctx")
    px = proj_in(h_all)
    o_f, o_b, hist_f, hist_b = _ret_fwd(px, lg, nb, nc, cx)
    q16 = _qk_prep(px, q_norm_w, cos, sin, t_rows, 0, AQ, HQ, 4, seq, te, "q_prep")
    kx16 = _qk_prep(px, k_norm_w, cos, sin, t_rows, 0, AK, HKV, HKV, seq, te, "k_prep")
    kc16 = _qk_prep(px, k_norm_w, None, None, c_rows, t_rows, AK, HKV, HKV, seq, tm, "kc_prep")
    o_att, yatt16, lse = _att_fwd(q16, kx16, kc16, px, nb, seq, cx, te)
    w_o_ret16, w_o_att16, w_out16 = get_w_o(lse)
    yret16, a_ret, a_att, y16, dxn, dout16, dgate, loss_b = _merge_out(
        o_f, o_b, yatt16, px, w_o_ret16, w_o_att16, w_out16, x2, tgt, mod3, nb, seq, tm)

    gw_out = _matmul(y16, dout16, ta=True, tm=D, tn=D, tk=D, out_dtype=BF16, name="gw_out")
    da_ret16, da_att16, do16, dao16, delta, dp_all = _bwd_branches(
        dout16, w_out16, w_o_ret16, w_o_att16, px, a_ret, a_att, o_f, o_b, o_att, rows_all, tm)
    gw_o_ret = _matmul(yret16, da_ret16, ta=True, tm=D, tn=D, tk=D, out_dtype=BF16, name="gw_o_ret")
    gw_o_att = _matmul(yatt16, da_att16, ta=True, tm=D, tn=D, tk=D, out_dtype=BF16, name="gw_o_att")
    out_state, out_started = on_out_grads([gw_o_ret, gw_o_att, gw_out])
    dp_all, gq, dkx, dvx, dkc, dvc = _att_bwd(q16, kx16, kc16, px, dao16, delta, lse, q_norm_w, cos, sin, dp_all, nb,
                                              seq, cx, te, after=out_started)
    dak16, gk_lat = _qk_prep_bwd(dkx.reshape(t_rows, HKV * HD), px, k_norm_w, cos, sin, t_rows, 0, AK, HKV, HKV, seq, te,
                                 "k_prep_bwd")
    dcak16, gk_ctx = _qk_prep_bwd(dkc.reshape(c_rows, HKV * HD), px, k_norm_w, None, None, c_rows, t_rows, AK, HKV, HKV,
                                  seq, tm, "kc_prep_bwd")
    dq_f, dk_f, dv_f, dq_b, dk_b, dv_b, dck16, dcv16, dlg_scan = _ret_bwd(px, lg, do16, hist_f, hist_b, nb, nc, cx)
    dp_all = _assemble_lat(dp_all, dk_f, dk_b, dv_f, dv_b, dak16, dvx.reshape(t_rows, HKV * HD), dq_f, dq_b, tm)
    dp_all = _assemble_ctx(dp_all, dck16, dcv16, dcak16, dvc.reshape(c_rows, HKV * HD), t_rows, tm)
    gw_in_t = _matmul(dp_all, h_all, ta=True, tm=1536, tn=D, tk=2304, out_dtype=BF16, name="gw_in")
    in_state, in_started = on_in_grad(gw_in_t)
    dh = proj_back(dp_all, in_started)
    grad_x, dsh, dsc, gnw_lat = _norm_bwd(dh, x2, mod3, norm_w, dxn, 0, seq, 0, te, "norm_bwd")
    dsh_c, dsc_c, gnw_ctx = _norm_bwd(dh, ctx2, mod3, norm_w, None, t_rows, c_rows, nb, tm, "norm_bwd_ctx")

    dlg = jnp.sum(dlg_scan[:, :, 0], axis=0).reshape(1, 2 * RH)
    misc = jnp.concatenate([gq, gk_lat + gk_ctx, dlg, jnp.sum(loss_b[:, 0, 0]).reshape(1, 1),
                            jnp.zeros((1, D - 2 * HD - 2 * RH - 1), F32)], axis=1)
    rows = []
    for b in range(nb):
        rows += [dsh[b], dsc[b], dgate[b]]
    rows += [dsh_c[0], dsc_c[0]] + [c[b:b + 1] for b in range(nb)] + [gnw_lat + gnw_ctx, misc]
    payload = jnp.concatenate(rows + [jnp.zeros((PAY_ROWS - len(rows), D), F32)], axis=0)
    return grad_x.reshape(nb, seq, D), out_state, in_state, payload


def _finish_small(gathered, nb, c_ctx, ret_log2_decay, w_ada16, dev):
    n_dev = gathered.shape[0]
    loc = 3 * D // n_dev
    dmod_all = gathered[:, :3 * nb].reshape(n_dev * nb, 3 * D)
    dmodc_parts = jnp.concatenate([gathered[:, 3 * nb:3 * nb + 2].reshape(n_dev, 2 * D), jnp.zeros((n_dev, D), F32)], axis=1)
    c_all = gathered[:, 3 * nb + 2:4 * nb + 2].reshape(n_dev * nb, D)
    nw_parts = gathered[:, 4 * nb + 2]
    misc_parts = gathered[:, 4 * nb + 3]
    n_rows = n_dev * nb + n_dev
    pad = (-n_rows) % 16
    c_rows = jnp.concatenate([c_all, jnp.broadcast_to(c_ctx.reshape(1, D), (n_dev, D)), jnp.zeros((pad, D), F32)], axis=0)
    dm_rows = jnp.concatenate([dmod_all, dmodc_parts, jnp.zeros((pad, 3 * D), F32)], axis=0)
    dm_loc_rows = lax.dynamic_slice_in_dim(dm_rows, dev * loc, loc, axis=1)
    r_pad = jnp.full((1, D), -1.0, F32).at[:, 2 * HD:2 * HD + 2 * RH].set(ret_log2_decay.reshape(1, 2 * RH))
    gb, gc, gnw, misc, gwa = _small_final(dmod_all, dmodc_parts, c_rows, dm_loc_rows, nw_parts, misc_parts,
                                          c_ctx.reshape(1, D), r_pad, w_ada16)
    return (gb, gc, gnw, misc[:, :HD], misc[:, HD:2 * HD], misc[:, 2 * HD:2 * HD + 2 * RH], gwa,
            misc[0, 2 * HD + 2 * RH])


def kernel(x, c, ctx, c_ctx, norm_w, w_ada, b_ada, w_in, ret_log2_decay, q_norm_w, k_norm_w, w_o_ret, w_o_att, w_out, loss_target, m_c_ctx, m_norm_w, m_w_ada, m_b_ada, m_w_in, m_ret_log2_decay, m_q_norm_w, m_k_norm_w, m_w_o_ret, m_w_o_att, m_w_out, v_c_ctx, v_norm_w, v_w_ada, v_b_ada, v_w_in, v_ret_log2_decay, v_q_norm_w, v_k_norm_w, v_w_o_ret, v_w_o_att, v_w_out):
    nb = x.shape[0]
    mx, my, mc = _mesh_pos()
    dev = 4 * mx + 2 * my + mc
    core = jnp.reshape(mc, (1,)).astype(jnp.int32)
    chip = jnp.reshape(2 * mx + my, (1,)).astype(jnp.int32)

    n_loc = 3 * D // N_DEV
    c8 = jnp.zeros((8, D), F32).at[:nb].set(c).at[nb].set(c_ctx)
    c_land = lax.dynamic_update_slice(lax.empty((N_DEV, 8, D), F32), c8[None], (dev, 0, 0))
    c_state, c_token = _exchange_start([c8[None]], [c_land], _bcast_routes(1), "gather_c_start")
    w_in_t = jnp.transpose(w_in[0])
    in_shard = w_in_t.astype(BF16)
    groups = lax.dynamic_update_slice(lax.empty((N_DEV,) + in_shard.shape, BF16), in_shard[None], (mc, 0, 0))
    groups = _pair_fill(groups, (0,), "gather_in_pair", after=(c_token,))
    _, (c_all,) = _exchange_wait(c_state, groups, "gather_c_wait")
    ada_shard = w_ada[0].astype(BF16)
    b_loc = lax.dynamic_slice(b_ada, (0, dev * n_loc), (1, n_loc))
    mod_cols = _mod_part(c_all.reshape(N_DEV * 8, D), ada_shard, b_loc)
    (mod_all,) = _all_gather([mod_cols], "gather_mod")
    mod = jnp.transpose(lax.dynamic_slice(mod_all, (0, dev * 8, 0), (N_DEV, 8, n_loc)), (1, 0, 2)).reshape(8, 3 * D)
    ada_land = lax.dynamic_update_slice(lax.empty((N_DEV,) + ada_shard.shape, BF16), ada_shard[None], (dev, 0, 0))

    (near_send, near_recv, near_bufs, near_routes), gin_token = _exchange_start(
        [in_shard[None]], [groups], _group_routes((1, 2)), "gather_in_start", after=(mod_all,))
    w_in_groups, wo_states, ada_states = [], [], []
    wo_shards = [w_[0].astype(BF16) for w_ in (w_o_ret, w_o_att, w_out)]
    wo_lands = [lax.dynamic_update_slice(lax.empty((N_DEV,) + s_.shape, BF16), s_[None], (dev, 0, 0)) for s_ in wo_shards]

    def _state(send, recv, src, groups, routes):
        return send, recv, [src, groups], routes

    def proj_in(h_all):
        src, groups = near_bufs
        px = _in_proj_group(h_all, groups, 0, 1, chip, None, (gin_token,), "in_proj_0")
        (src,), (groups,) = _exchange_wait(_state(near_send, near_recv, src, groups, near_routes), px,
                                           "gather_in_wait_near")
        groups = _pair_fill(groups, (1, 2), "gather_in_fill_near")
        (far_send, far_recv, (src, groups), far_routes), far_token = _exchange_start(
            [src], [groups], _group_routes((3,)), "gather_in_start_far")
        wo_state, wo_token = _exchange_start([s_[None] for s_ in wo_shards], wo_lands, _bcast_routes(3),
                                             "gather_wo_start", after=(far_token,))
        wo_states.append(wo_state)
        ada_state, ada_token = _exchange_start([ada_shard[None]], [ada_land], _bcast_routes(1), "gather_ada_start",
                                               after=(wo_token,))
        ada_states.append(ada_state)
        px = _in_proj_group(h_all, groups, 1, 2, chip, px, (ada_token,), "in_proj_near")
        (src,), (groups,) = _exchange_wait(_state(far_send, far_recv, src, groups, far_routes), px,
                                           "gather_in_wait_far")
        groups = _pair_fill(groups, (3,), "gather_in_fill_far")
        px = _in_proj_group(h_all, groups, 3, 1, chip, px, (), "in_proj_far")
        w_in_groups.append(groups)
        return px

    def proj_back(dp_all, after):
        return _d_h_groups(dp_all, w_in_groups[0], chip, after)

    def get_w_o(after):
        _, (l_ret, l_att, l_out) = _exchange_wait(wo_states[0], after, "gather_wo_wait")
        return l_ret.reshape(RH * DV, D), l_att.reshape(D, D), l_out.reshape(D, D)

    def on_out_grads(grads):
        parts = [g_.reshape(N_DEV, g_.shape[0] // N_DEV, D) for g_ in grads]
        state, token = _reduce_scatter_start(parts, core, "rs_out")
        return state, (token,)

    def on_in_grad(grad):
        state, token = _reduce_scatter_start([grad.reshape(N_DEV, IN_COLS // N_DEV, D)], core, "rs_in")
        return state, (token,)

    grad_x, out_state, in_state, payload = _local_step(
        x, c, ctx, norm_w, ret_log2_decay, q_norm_w, k_norm_w, loss_target,
        mod, proj_in, proj_back, get_w_o, on_out_grads, on_in_grad, started=(gin_token,))

    pay_land = lax.dynamic_update_slice(lax.empty((N_DEV,) + payload.shape, F32), payload[None], (dev, 0, 0))
    pay_state, pay_token = _exchange_start([payload[None]], [pay_land], _bcast_routes(1), "gather_small_start")

    out_res = _reduce_scatter_finish(out_state, pay_token, chip,
                                     [(w_[0], m_[0], v_[0]) for w_, m_, v_ in ((w_o_ret, m_w_o_ret, v_w_o_ret),
                                                                                (w_o_att, m_w_o_att, v_w_o_att),
                                                                                (w_out, m_w_out, v_w_out))], "rs_out")
    (in_res,) = _reduce_scatter_finish(in_state, out_res[0][0], chip,
                                       [(w_in_t, jnp.transpose(m_w_in[0]), jnp.transpose(v_w_in[0]))], "rs_in")

    _, (gathered,) = _exchange_wait(pay_state, in_res[0], "gather_small_wait")
    _, (l_ada,) = _exchange_wait(ada_states[0], gathered, "gather_ada_wait")
    w_ada16 = jnp.transpose(l_ada, (1, 0, 2)).reshape(D, 3 * D)
    gb, gc, gnw, gq, gk, gr, gwa, loss = _finish_small(gathered, nb, c_ctx, ret_log2_decay, w_ada16, dev)
    big = {4: [jnp.transpose(r)[None] for r in in_res]}
    for i, res in zip((8, 9, 10), out_res):
        big[i] = [r[None] for r in res]
    small_g = {0: gc.reshape(c_ctx.shape), 1: gnw, 2: gwa[None], 3: gb, 5: gr.reshape(ret_log2_decay.shape), 6: gq, 7: gk}
    weights = [c_ctx, norm_w, w_ada, b_ada, w_in, ret_log2_decay, q_norm_w, k_norm_w, w_o_ret, w_o_att, w_out]
    ms = [m_c_ctx, m_norm_w, m_w_ada, m_b_ada, m_w_in, m_ret_log2_decay, m_q_norm_w, m_k_norm_w, m_w_o_ret, m_w_o_att, m_w_out]
    vs = [v_c_ctx, v_norm_w, v_w_ada, v_b_ada, v_w_in, v_ret_log2_decay, v_q_norm_w, v_k_norm_w, v_w_o_ret, v_w_o_att, v_w_out]
    grads, deltas, new_ms, new_vs = [], [], [], []
    for i, (w, m, v) in enumerate(zip(weights, ms, vs)):
        if i in big:
            res = big[i]
        else:
            shape2 = (-1, w.shape[-1])
            g = small_g[i]
            res = [g] + [r.reshape(w.shape) for r in _adamw(w.reshape(shape2), g.reshape(shape2), m.reshape(shape2),
                                                             v.reshape(shape2), "adamw_%d" % i)]
        for lst, r in zip((grads, deltas, new_ms, new_vs), res):
            lst.append(r)
    return (loss, grad_x, *grads, *deltas, *new_ms, *new_vs)
```

```python
import numpy as np
import jax
import jax.numpy as jnp
from jax import lax
from jax.experimental import pallas as pl
from jax.experimental.pallas import tpu as pltpu

F32 = jnp.float32
BF16 = jnp.bfloat16

D = 1024
RH, DK, DV, CH = 4, 256, 512, 256
HQ, HKV, HD = 8, 2, 128
GRID_W = 64
ROPE_THETA = 10000.0
EPS = 1e-6
RK, RV, AK, AV, RQ, RG, AQ, AG, MR, MA = 0, 1024, 3072, 3328, 3584, 4608, 6656, 7680, 8704, 9728
IN_COLS = 10752
KV_COLS = 3584
N_DEV = 8
LR, B1, B2, ADAM_EPS, WD, STEP = 0.001, 0.9, 0.999, 1e-08, 0.01, 10
PAY_ROWS = 16
VMEM_LIMIT = 56 * 1024 * 1024
_D_H_ROWS = 1536
MESH_T = pl.DeviceIdType.MESH

NT = (((1,), (1,)), ((), ()))
TN = (((0,), (0,)), ((), ()))
SM_C = (HD ** -0.5) * float(np.log2(np.e))


def _params(sem):
    return pltpu.CompilerParams(dimension_semantics=sem, vmem_limit_bytes=VMEM_LIMIT)


def _pick(n, target, mult=8):
    best = None
    for t in range(mult, min(n, target) + 1, mult):
        if n % t == 0:
            best = t
    return best or n


def _dot(a, b, dn=None):
    if dn is None:
        return jnp.dot(a, b, preferred_element_type=F32)
    return lax.dot_general(a, b, dn, preferred_element_type=F32)


def _sig(v):
    return jax.nn.sigmoid(v)


def _silu(v):
    return v * _sig(v)


def _dsilu(v):
    s = _sig(v)
    return s * (1.0 + v * (1.0 - s))


def _sds(shape, dtype):
    return jax.ShapeDtypeStruct(shape, dtype)


def _matmul(a, b, *, ta=False, tb=False, tm, tn, tk, out_dtype, name, after=()):
    m = a.shape[1] if ta else a.shape[0]
    kdim = a.shape[0] if ta else a.shape[1]
    n = b.shape[0] if tb else b.shape[1]
    tm, tn, tk = _pick(m, tm, 128), _pick(n, tn, 128), _pick(kdim, tk, 128)
    nk = kdim // tk
    dn = (((0 if ta else 1,), (1 if tb else 0,)), ((), ()))

    def body(a_ref, b_ref, *rest):
        o_ref, acc_ref = rest[-2:]
        k = pl.program_id(2)
        part = _dot(a_ref[...].astype(BF16), b_ref[...].astype(BF16), dn)
        if nk == 1:
            o_ref[...] = part.astype(o_ref.dtype)
        else:
            @pl.when(k == 0)
            def _():
                acc_ref[...] = part

            @pl.when(k > 0)
            def _():
                acc_ref[...] += part

            @pl.when(k == nk - 1)
            def _():
                o_ref[...] = acc_ref[...].astype(o_ref.dtype)

    a_spec = pl.BlockSpec((tk, tm), lambda i, j, k: (k, i)) if ta else pl.BlockSpec((tm, tk), lambda i, j, k: (i, k))
    b_spec = pl.BlockSpec((tn, tk), lambda i, j, k: (j, k)) if tb else pl.BlockSpec((tk, tn), lambda i, j, k: (k, j))
    return pl.pallas_call(
        body, name=name, grid=(m // tm, n // tn, nk),
        in_specs=[a_spec, b_spec] + [pl.BlockSpec(memory_space=pl.ANY)] * len(after),
        out_specs=pl.BlockSpec((tm, tn), lambda i, j, k: (i, j)), out_shape=_sds((m, n), out_dtype),
        scratch_shapes=[pltpu.VMEM((tm, tn) if nk > 1 else (8, 128), F32)],
        compiler_params=_params(("parallel", "parallel", "arbitrary")),
    )(a, b, *after)


def _log_gamma(r):
    rp = jnp.full((8, 128), -1.0, F32).at[:2, :RH].set(r.reshape(2, RH))

    def body(r_ref, o_ref):
        o_ref[...] = jnp.log1p(-jnp.exp2(r_ref[...]))

    out = pl.pallas_call(body, name="log_gamma", out_shape=_sds((8, 128), F32))(rp)
    return out[:2, :RH]


def _mod_part(c_rows, w_ada_loc16, b_loc):
    def body(c_ref, w_ref, b_ref, o_ref):
        o_ref[...] = _dot(_silu(c_ref[...]).astype(BF16), w_ref[...]) + b_ref[...]

    return pl.pallas_call(
        body, name="mod_part", out_shape=_sds((c_rows.shape[0], w_ada_loc16.shape[1]), F32),
    )(c_rows, w_ada_loc16, b_loc)


def _norm_fwd(x2, mod3, norm_w, rows_all, row_off, rows_per_group, group0, h_prev, tm, name, after=()):
    rows = x2.shape[0]
    rb0 = row_off // tm
    bpg = rows_per_group // tm

    def body(*refs):
        x_ref, sh_ref, sc_ref, nw_ref, o_ref = refs[-5:]
        xv = x_ref[...]
        r = lax.rsqrt(jnp.mean(xv * xv, axis=-1, keepdims=True) + EPS)
        o_ref[...] = ((xv * r) * nw_ref[...] * (1.0 + sc_ref[...]) + sh_ref[...]).astype(BF16)

    in_specs = [pl.BlockSpec((tm, D), lambda i: (i, 0)),
                pl.BlockSpec((None, 1, D), lambda i: (group0 + i // bpg, 0, 0)),
                pl.BlockSpec((None, 1, D), lambda i: (group0 + i // bpg, 0, 1)),
                pl.BlockSpec((1, D), lambda i: (0, 0))]
    in_specs = [pl.BlockSpec(memory_space=pl.ANY)] * len(after) + in_specs
    args = list(after) + [x2, mod3, mod3, norm_w]
    alias = {}
    if h_prev is not None:
        in_specs.insert(0, pl.BlockSpec(memory_space=pl.ANY))
        args.insert(0, h_prev)
        alias = {0: 0}
    return pl.pallas_call(
        body, name=name, grid=(rows // tm,), in_specs=in_specs,
        out_specs=pl.BlockSpec((tm, D), lambda i: (rb0 + i, 0)), out_shape=_sds((rows_all, D), BF16),
        input_output_aliases=alias, compiler_params=_params(("parallel",)),
    )(*args)


def _decays(lg, fwd):
    ii = lax.broadcasted_iota(jnp.int32, (CH, CH), 0)
    jj = lax.broadcasted_iota(jnp.int32, (CH, CH), 1)
    ri = lax.broadcasted_iota(jnp.int32, (CH, 1), 0).astype(F32)
    rel = (ii - jj) if fwd else (jj - ii)
    relf = jnp.maximum(rel, 0).astype(F32)
    mask = jnp.where(rel >= 0, jnp.exp(lg * relf), 0.0)
    qe = (ri + 1.0) if fwd else (CH - ri)
    ke = (CH - 1.0 - ri) if fwd else ri
    return mask, relf, jnp.exp(lg * qe), qe, jnp.exp(lg * ke), ke


def _wide_specs(rowf):
    return [pl.BlockSpec((CH, 2 * DK), lambda b, c: (rowf(b, c), RQ // (2 * DK))),
            pl.BlockSpec((CH, 2 * DK), lambda b, c: (rowf(b, c), RQ // (2 * DK) + 1)),
            pl.BlockSpec((CH, RH * DK), lambda b, c: (rowf(b, c), RK // (RH * DK))),
            pl.BlockSpec((CH, 2 * DV), lambda b, c: (rowf(b, c), RV // (2 * DV))),
            pl.BlockSpec((CH, 2 * DV), lambda b, c: (rowf(b, c), RV // (2 * DV) + 1))]


def _head_qkv(refs, h):
    q0, q1, k, v0, v1 = refs
    lo = h % 2
    q = (q0, q1)[h // 2][:, lo * DK:(lo + 1) * DK].astype(F32)
    kk = k[:, h * DK:(h + 1) * DK].astype(F32) * (DK ** -0.5)
    v16 = (v0, v1)[h // 2][:, lo * DV:(lo + 1) * DV].astype(BF16)
    return q, kk, v16


def _ctx_specs(t_rows, cx):
    rb = t_rows // cx
    return [pl.BlockSpec((cx, RH * DK), lambda b, c: (rb + b, RK // (RH * DK))),
            pl.BlockSpec((cx, 2 * DV), lambda b, c: (rb + b, RV // (2 * DV))),
            pl.BlockSpec((cx, 2 * DV), lambda b, c: (rb + b, RV // (2 * DV) + 1))]


def _ctx_kv(refs, h):
    k, v0, v1 = refs
    kk = k[:, h * DK:(h + 1) * DK].astype(F32) * (DK ** -0.5)
    lo = h % 2
    return kk, (v0, v1)[h // 2][:, lo * DV:(lo + 1) * DV].astype(BF16)


def _ret_fwd(px, lg, nb, nc, cx):
    t_rows = nb * nc * CH

    def body(lg_ref, *refs):
        ins = (refs[0:5], refs[5:10])
        ctx_refs = refs[10:13]
        of_ref, ob_ref, hf_ref, hb_ref, sf, sb = refs[13:]
        c = pl.program_id(1)

        @pl.when(c == 0)
        def _():
            pos = lax.broadcasted_iota(jnp.int32, (cx, 1), 0).astype(F32)
            for h in range(RH):
                k, v16 = _ctx_kv(ctx_refs, h)
                sf[h] = _dot((k * jnp.exp(lg_ref[0, h] * (cx - 1.0 - pos))).astype(BF16), v16, TN)
                sb[h] = _dot((k * jnp.exp(lg_ref[1, h] * pos)).astype(BF16), v16, TN)

        for d, (o_ref, h_ref, s) in enumerate(((of_ref, hf_ref, sf), (ob_ref, hb_ref, sb))):
            for h in range(RH):
                lg_d = lg_ref[d, h]
                mask, _, qd, _, kd, _ = _decays(lg_d, d == 0)
                q, k, v16 = _head_qkv(ins[d], h)
                a = _dot(q.astype(BF16), k.astype(BF16), NT)
                st = s[h]
                st16 = st.astype(BF16)
                h_ref[h] = st16
                o = _dot((a * mask).astype(BF16), v16) + _dot((q * qd).astype(BF16), st16)
                o_ref[:, h * DV:(h + 1) * DV] = o.astype(BF16)
                s[h] = st * jnp.exp(lg_d * CH) + _dot((k * kd).astype(BF16), v16, TN)

    def fw(b, c):
        return b * nc + c

    def bw(b, c):
        return b * nc + nc - 1 - c

    in_specs = [pl.BlockSpec(memory_space=pltpu.SMEM)] + _wide_specs(fw) + _wide_specs(bw) + _ctx_specs(t_rows, cx)
    out_specs = [pl.BlockSpec((CH, RH * DV), lambda b, c: (fw(b, c), 0)),
                 pl.BlockSpec((CH, RH * DV), lambda b, c: (bw(b, c), 0)),
                 pl.BlockSpec((None, None, RH, DK, DV), lambda b, c: (b, c, 0, 0, 0)),
                 pl.BlockSpec((None, None, RH, DK, DV), lambda b, c: (b, nc - 1 - c, 0, 0, 0))]
    return pl.pallas_call(
        body, name="ret_fwd", grid=(nb, nc), in_specs=in_specs, out_specs=out_specs,
        out_shape=[_sds((t_rows, RH * DV), BF16)] * 2 + [_sds((nb, nc, RH, DK, DV), BF16)] * 2,
        scratch_shapes=[pltpu.VMEM((RH, DK, DV), F32), pltpu.VMEM((RH, DK, DV), F32)],
        compiler_params=_params(("parallel", "arbitrary")),
    )(lg, *([px] * 13))


def _rope_tables(seq):
    rows = seq // GRID_W
    row = np.repeat(np.arange(rows, dtype=np.float32), GRID_W)
    col = np.tile(np.arange(GRID_W, dtype=np.float32), rows)
    half = HD // 2
    freqs = (ROPE_THETA ** (-np.arange(0, half, 2, dtype=np.float32) / half)).astype(np.float32)
    ang = np.concatenate([row[:, None] * freqs, col[:, None] * freqs], axis=-1).astype(np.float32)
    cos = np.repeat(np.cos(ang), 2, axis=-1).astype(np.float32)
    sin = np.repeat(np.sin(ang), 2, axis=-1).astype(np.float32)
    sign = np.tile(np.array([-1.0, 1.0], np.float32), HD // 2)
    return jnp.asarray(cos), jnp.asarray(sin * sign)


def _swap_pairs(v):
    lane = lax.broadcasted_iota(jnp.int32, v.shape, 1)
    return jnp.where((lane & 1) == 0, pltpu.roll(v, HD - 1, 1), pltpu.roll(v, 1, 1))


def _qk_prep(px, nw, cos, sin, rows, row_off, col_off, heads, hb, seq, tm, name):
    rope = cos is not None
    rb0 = row_off // tm
    pb = seq // tm if rope else 1
    bw = hb * HD

    def body(*refs):
        if rope:
            x_ref, w_ref, c_ref, s_ref, o_ref = refs
        else:
            x_ref, w_ref, o_ref = refs
        for h in range(hb):
            sl = slice(h * HD, (h + 1) * HD)
            xv = x_ref[:, sl].astype(F32)
            r = lax.rsqrt(jnp.mean(xv * xv, axis=-1, keepdims=True) + EPS)
            t = (xv * r) * w_ref[...]
            if rope:
                t = t * c_ref[...] + _swap_pairs(t) * s_ref[...]
            o_ref[:, sl] = t.astype(BF16)

    in_specs = [pl.BlockSpec((tm, bw), lambda i, j: (rb0 + i, col_off // bw + j)),
                pl.BlockSpec((1, HD), lambda i, j: (0, 0))]
    args = [px, nw]
    if rope:
        in_specs += [pl.BlockSpec((tm, HD), lambda i, j: (i % pb, 0))] * 2
        args += [cos, sin]
    return pl.pallas_call(
        body, name=name, grid=(rows // tm, heads // hb), in_specs=in_specs,
        out_specs=pl.BlockSpec((tm, bw), lambda i, j: (i, j)), out_shape=_sds((rows, heads * HD), BF16),
        compiler_params=_params(("parallel", "parallel")),
    )(*args)


def _att_fwd(q16, kx16, kc16, px, nb, seq, cx, tq):
    t_rows = nb * seq
    nq = seq // tq
    rep = HQ // HKV
    gw = rep * HD

    def body(q_ref, kx_ref, kc_ref, vx_ref, vc_ref, g_ref, o_ref, y_ref, l_ref):
        kx = kx_ref[...]
        kc = kc_ref[...]
        vx = vx_ref[...].astype(BF16)
        vc = vc_ref[...].astype(BF16)
        l_ref[...] = jnp.zeros_like(l_ref)
        for r in range(rep):
            sl = slice(r * HD, (r + 1) * HD)
            q = q_ref[:, sl]
            s1 = _dot(q, kx, NT)
            s2 = _dot(q, kc, NT)
            m = jnp.maximum(jnp.max(s1, axis=-1, keepdims=True), jnp.max(s2, axis=-1, keepdims=True))
            e1 = jnp.exp2((s1 - m) * SM_C)
            e2 = jnp.exp2((s2 - m) * SM_C)
            tot = jnp.sum(e1, axis=-1, keepdims=True) + jnp.sum(e2, axis=-1, keepdims=True)
            o = (_dot(e1.astype(BF16), vx) + _dot(e2.astype(BF16), vc)) * (1.0 / tot)
            o_ref[:, sl] = o
            y_ref[:, sl] = (o * _silu(g_ref[:, sl].astype(F32))).astype(BF16)
            l_ref[:, r:r + 1] = m * SM_C + jnp.log(tot) * float(np.log2(np.e))

    qblk = pl.BlockSpec((tq, gw), lambda b, g, i: (b * nq + i, g))
    return pl.pallas_call(
        body, name="att_fwd", grid=(nb, HKV, nq),
        in_specs=[qblk,
                  pl.BlockSpec((seq, HD), lambda b, g, i: (b, g)),
                  pl.BlockSpec((cx, HD), lambda b, g, i: (b, g)),
                  pl.BlockSpec((seq, HD), lambda b, g, i: (b, AV // HD + g)),
                  pl.BlockSpec((cx, HD), lambda b, g, i: (t_rows // cx + b, AV // HD + g)),
                  pl.BlockSpec((tq, gw), lambda b, g, i: (b * nq + i, AG // gw + g))],
        out_specs=[qblk, qblk, pl.BlockSpec((tq, 128), lambda b, g, i: (b * nq + i, g))],
        out_shape=[_sds((t_rows, D), F32), _sds((t_rows, D), BF16), _sds((t_rows, HKV * 128), F32)],
        compiler_params=_params(("parallel", "parallel", "parallel")),
    )(q16, kx16, kc16, px, px, px)


def _gate_specs(tm, col0):
    hw = D // 2
    return [pl.BlockSpec((tm, hw), lambda i: (i, col0 // hw)), pl.BlockSpec((tm, hw), lambda i: (i, col0 // hw + 1))]


def _merge_out(o_f, o_b, yatt16, px, w_o_ret16, w_o_att16, w_out16, x2, tgt, mod3, nb, seq, tm):
    t_rows = nb * seq
    bpb = seq // tm
    hw = D // 2

    def body(of_ref, ob_ref, g0, g1, g2, g3, wr_ref, ya_ref, wa_ref, mr0, mr1, ma0, ma1, wo_ref, x_ref, t_ref, gt_ref,
             yr_ref, ar_ref, aa_ref, y_ref, dxn_ref, dout_ref, dg_ref, loss_ref):
        i = pl.program_id(1)
        for h, g_ref in enumerate((g0, g1, g2, g3)):
            sl = slice(h * DV, (h + 1) * DV)
            o = of_ref[:, sl].astype(F32) + ob_ref[:, sl].astype(F32)
            r = lax.rsqrt(jnp.mean(o * o, axis=-1, keepdims=True) + EPS)
            yr_ref[:, sl] = ((o * r) * _silu(g_ref[...].astype(F32))).astype(BF16)
        ar = _dot(yr_ref[...], wr_ref[...])
        aa = _dot(ya_ref[...], wa_ref[...])
        ar_ref[...] = ar.astype(BF16)
        aa_ref[...] = aa.astype(BF16)
        for j, (mr_ref, ma_ref) in enumerate(((mr0, ma0), (mr1, ma1))):
            sl = slice(j * hw, (j + 1) * hw)
            y_ref[:, sl] = (_sig(mr_ref[...].astype(F32)) * ar[:, sl]
                            + _sig(ma_ref[...].astype(F32)) * aa[:, sl]).astype(BF16)
        out = _dot(y_ref[...], wo_ref[...])
        gate = gt_ref[...]
        diff = x_ref[...] + gate * out - t_ref[...]
        dxn = diff * (1.0 / D)
        dxn_ref[...] = dxn
        dout_ref[...] = (gate * dxn).astype(BF16)
        dg = jnp.sum(dxn * out, axis=0, keepdims=True)
        ls = jnp.broadcast_to(jnp.sum(diff * diff) * (0.5 / D), (1, 128))

        @pl.when(i == 0)
        def _():
            dg_ref[...] = dg
            loss_ref[...] = ls

        @pl.when(i > 0)
        def _():
            dg_ref[...] += dg
            loss_ref[...] += ls

    def cols(width, col0):
        return pl.BlockSpec((tm, width), lambda b, i: (b * bpb + i, col0 // width))

    def whole(rows):
        return pl.BlockSpec((rows, D), lambda b, i: (0, 0))

    row, wide = cols(D, 0), cols(RH * DV, 0)
    gates = [cols(DV, RG + h * DV) for h in range(RH)]
    merge_gates = [cols(hw, MR), cols(hw, MR + hw), cols(hw, MA), cols(hw, MA + hw)]
    return pl.pallas_call(
        body, name="merge_out", grid=(nb, bpb),
        in_specs=[wide, wide] + gates + [whole(RH * DV), row, whole(D)] + merge_gates
        + [whole(D), row, row, pl.BlockSpec((None, 1, D), lambda b, i: (b, 0, 2))],
        out_specs=[wide, row, row, row, row, row, pl.BlockSpec((None, 1, D), lambda b, i: (b, 0, 0)),
                   pl.BlockSpec((None, 1, 128), lambda b, i: (b, 0, 0))],
        out_shape=[_sds((t_rows, RH * DV), BF16)] + [_sds((t_rows, D), BF16)] * 3
        + [_sds((t_rows, D), F32), _sds((t_rows, D), BF16), _sds((nb, 1, D), F32), _sds((nb, 1, 128), F32)],
        compiler_params=_params(("parallel", "arbitrary")),
    )(o_f, o_b, *([px] * RH), w_o_ret16, yatt16, w_o_att16, px, px, px, px, w_out16, x2, tgt, mod3)


def _bwd_branches(dout16, w_out16, w_o_ret16, w_o_att16, px, a_ret, a_att, o_f, o_b, o_att, rows_all, tm):
    t_rows = dout16.shape[0]
    hw = D // 2

    def body(do_ref, wo_ref, wr_ref, wa_ref, mr0, mr1, ma0, ma1, ar_ref, aa_ref, rg0, rg1, rg2, rg3, of_ref, ob_ref,
             ag0, ag1, oa_ref, dar_ref, daa_ref, dor_ref, dao_ref, dl_ref, dp_ref):
        dy_all = _dot(do_ref[...], wo_ref[...], NT)
        for j, (mr_ref, ma_ref) in enumerate(((mr0, ma0), (mr1, ma1))):
            sl = slice(j * hw, (j + 1) * hw)
            dy = dy_all[:, sl]
            sr = _sig(mr_ref[...].astype(F32))
            sa = _sig(ma_ref[...].astype(F32))
            dar_ref[:, sl] = (dy * sr).astype(BF16)
            daa_ref[:, sl] = (dy * sa).astype(BF16)
            dp_ref[:, MR - RG + j * hw:MR - RG + (j + 1) * hw] = (
                dy * ar_ref[:, sl].astype(F32) * sr * (1.0 - sr)).astype(BF16)
            dp_ref[:, MA - RG + j * hw:MA - RG + (j + 1) * hw] = (
                dy * aa_ref[:, sl].astype(F32) * sa * (1.0 - sa)).astype(BF16)
        da_ret = dar_ref[...]
        for h, g_ref in enumerate((rg0, rg1, rg2, rg3)):
            sl = slice(h * DV, (h + 1) * DV)
            dy = _dot(da_ret, wr_ref[sl, :], NT)
            g = g_ref[...].astype(F32)
            o = of_ref[:, sl].astype(F32) + ob_ref[:, sl].astype(F32)
            r = lax.rsqrt(jnp.mean(o * o, axis=-1, keepdims=True) + EPS)
            on = o * r
            sg = _sig(g)
            don = dy * (g * sg)
            dp_ref[:, sl] = (dy * on * (sg * (1.0 + g * (1.0 - sg)))).astype(BF16)
            dor_ref[:, sl] = (r * (don - on * jnp.mean(on * don, axis=-1, keepdims=True))).astype(BF16)
        dy_all = _dot(daa_ref[...], wa_ref[...], NT)
        dl_ref[...] = jnp.zeros_like(dl_ref)
        for j, g_ref in enumerate((ag0, ag1)):
            sl = slice(j * hw, (j + 1) * hw)
            dy = dy_all[:, sl]
            g = g_ref[...].astype(F32)
            sg = _sig(g)
            dao = dy * (g * sg)
            dao_ref[:, sl] = dao.astype(BF16)
            prod = dao * oa_ref[:, sl]
            for r in range(hw // HD):
                dl_ref[:, j * 128 + r:j * 128 + r + 1] = jnp.sum(prod[:, r * HD:(r + 1) * HD], axis=-1, keepdims=True)
            dp_ref[:, AG - RG + j * hw:AG - RG + (j + 1) * hw] = (
                dy * oa_ref[:, sl] * (sg * (1.0 + g * (1.0 - sg)))).astype(BF16)

    def gate(h):
        return pl.BlockSpec((tm, DV), lambda i: (i, RG // DV + h))

    def whole(rows):
        return pl.BlockSpec((rows, D), lambda i: (0, 0))

    row = pl.BlockSpec((tm, D), lambda i: (i, 0))
    wide = pl.BlockSpec((tm, RH * DV), lambda i: (i, 0))
    return pl.pallas_call(
        body, name="bwd_branches", grid=(t_rows // tm,),
        in_specs=[row, whole(D), whole(RH * DV), whole(D)] + _gate_specs(tm, MR) + _gate_specs(tm, MA) + [row, row]
        + [gate(h) for h in range(RH)] + [wide, wide] + _gate_specs(tm, AG) + [row],
        out_specs=[row, row, wide, row, pl.BlockSpec((tm, HKV * 128), lambda i: (i, 0)),
                   pl.BlockSpec((pl.Element(tm), pl.Element(IN_COLS - RG)), lambda i: (i * tm, RG))],
        out_shape=[_sds((t_rows, D), BF16)] * 2 + [_sds((t_rows, RH * DV), BF16), _sds((t_rows, D), BF16),
                                                  _sds((t_rows, HKV * 128), F32), _sds((rows_all, IN_COLS), BF16)],
        compiler_params=_params(("parallel",)),
    )(dout16, w_out16, w_o_ret16, w_o_att16, px, px, px, px, a_ret, a_att, *([px] * RH), o_f, o_b, px, px, o_att)


def _att_bwd(q16, kx16, kc16, px, dao16, delta, lse, q_norm_w, cos, sin, dp_all, nb, seq, cx, tq, after=()):
    t_rows = nb * seq
    nq = seq // tq
    rep = HQ // HKV
    gw = rep * HD
    scale = HD ** -0.5

    def body(q_ref, kx_ref, kc_ref, vx_ref, vc_ref, dao_ref, dl_ref, l_ref, xq_ref, w_ref, c_ref, s_ref, *rest):
        daq_ref, gq_ref, dkx_ref, dvx_ref, dkc_ref, dvc_ref = rest[1 + len(after):7 + len(after)]
        accs = rest[7 + len(after):]
        i = pl.program_id(2)
        first = jnp.logical_and(jnp.logical_and(pl.program_id(0) == 0, pl.program_id(1) == 0), i == 0)
        gq = jnp.zeros((1, HD), F32)
        kx = kx_ref[...]
        kc = kc_ref[...]
        vx = vx_ref[...].astype(BF16)
        vc = vc_ref[...].astype(BF16)
        @pl.when(i == 0)
        def _():
            for acc in accs:
                acc[...] = jnp.zeros_like(acc)

        dkx, dvx, dkc, dvc = [acc[...] for acc in accs]
        for r in range(rep):
            sl = slice(r * HD, (r + 1) * HD)
            q = q_ref[:, sl]
            lr = l_ref[:, r:r + 1]
            p1 = jnp.exp2(_dot(q, kx, NT) * SM_C - lr)
            p2 = jnp.exp2(_dot(q, kc, NT) * SM_C - lr)
            da16 = dao_ref[:, sl]
            delta = dl_ref[:, r:r + 1]
            ds1 = (p1 * (_dot(da16, vx, NT) - delta)).astype(BF16)
            ds2 = (p2 * (_dot(da16, vc, NT) - delta)).astype(BF16)
            dq = (_dot(ds1, kx) + _dot(ds2, kc)) * scale
            dkx += _dot(q, ds1, TN)
            dkc += _dot(q, ds2, TN)
            dvx += _dot(da16, p1.astype(BF16), TN)
            dvc += _dot(da16, p2.astype(BF16), TN)
            dt = dq * c_ref[...] + _swap_pairs(dq * s_ref[...])
            xv = xq_ref[:, sl].astype(F32)
            rn = lax.rsqrt(jnp.mean(xv * xv, axis=-1, keepdims=True) + EPS)
            xh = xv * rn
            dxh = dt * w_ref[...]
            daq_ref[:, sl] = (rn * (dxh - xh * jnp.mean(dxh * xh, axis=-1, keepdims=True))).astype(BF16)
            gq += jnp.sum(dt * xh, axis=0, keepdims=True)
        for acc, val in zip(accs, (dkx, dvx, dkc, dvc)):
            acc[...] = val

        @pl.when(first)
        def _():
            gq_ref[...] = gq

        @pl.when(jnp.logical_not(first))
        def _():
            gq_ref[...] += gq

        @pl.when(i == nq - 1)
        def _():
            dkx_ref[...] = dkx.T * scale
            dvx_ref[...] = dvx.T
            dkc_ref[...] = dkc.T * scale
            dvc_ref[...] = dvc.T

    qblk = pl.BlockSpec((tq, gw), lambda b, g, i: (b * nq + i, g))
    kxb = pl.BlockSpec((None, seq, HD), lambda b, g, i: (b, 0, g))
    kcb = pl.BlockSpec((None, cx, HD), lambda b, g, i: (b, 0, g))
    table = pl.BlockSpec((tq, HD), lambda b, g, i: (i, 0))
    one = pl.BlockSpec((1, HD), lambda b, g, i: (0, 0))
    lane = pl.BlockSpec((tq, 128), lambda b, g, i: (b * nq + i, g))
    return pl.pallas_call(
        body, name="att_bwd", grid=(nb, HKV, nq),
        in_specs=[qblk,
                  pl.BlockSpec((seq, HD), lambda b, g, i: (b, g)),
                  pl.BlockSpec((cx, HD), lambda b, g, i: (b, g)),
                  pl.BlockSpec((seq, HD), lambda b, g, i: (b, AV // HD + g)),
                  pl.BlockSpec((cx, HD), lambda b, g, i: (t_rows // cx + b, AV // HD + g)),
                  qblk, lane, lane,
                  pl.BlockSpec((tq, gw), lambda b, g, i: (b * nq + i, AQ // gw + g)), one, table, table]
        + [pl.BlockSpec(memory_space=pl.ANY)] * (1 + len(after)),
        out_specs=[pl.BlockSpec((tq, gw), lambda b, g, i: (b * nq + i, AQ // gw + g)), one, kxb, kxb, kcb, kcb],
        out_shape=[_sds(dp_all.shape, BF16), _sds((1, HD), F32), _sds((nb, seq, HKV * HD), F32),
                   _sds((nb, seq, HKV * HD), F32), _sds((nb, cx, HKV * HD), F32), _sds((nb, cx, HKV * HD), F32)],
        scratch_shapes=[pltpu.VMEM((HD, seq), F32), pltpu.VMEM((HD, seq), F32), pltpu.VMEM((HD, cx), F32),
                        pltpu.VMEM((HD, cx), F32)],
        input_output_aliases={12: 0},
        compiler_params=_params(("arbitrary", "arbitrary", "arbitrary")),
    )(q16, kx16, kc16, px, px, dao16, delta, lse, px, q_norm_w, cos, sin, dp_all, *after)


def _qk_prep_bwd(dt, px, nw, cos, sin, rows, row_off, col_off, heads, hb, seq, tm, name):
    rope = cos is not None
    rb0 = row_off // tm
    pb = seq // tm if rope else 1
    bw = hb * HD

    def body(*refs):
        if rope:
            d_ref, x_ref, w_ref, c_ref, s_ref, dx_ref, dw_ref = refs
        else:
            d_ref, x_ref, w_ref, dx_ref, dw_ref = refs
        first = jnp.logical_and(pl.program_id(0) == 0, pl.program_id(1) == 0)
        dw = jnp.zeros((1, HD), F32)
        for h in range(hb):
            sl = slice(h * HD, (h + 1) * HD)
            dtv = d_ref[:, sl]
            if rope:
                dtv = dtv * c_ref[...] + _swap_pairs(dtv * s_ref[...])
            xv = x_ref[:, sl].astype(F32)
            r = lax.rsqrt(jnp.mean(xv * xv, axis=-1, keepdims=True) + EPS)
            xh = xv * r
            dxh = dtv * w_ref[...]
            dx_ref[:, sl] = (r * (dxh - xh * jnp.mean(dxh * xh, axis=-1, keepdims=True))).astype(BF16)
            dw += jnp.sum(dtv * xh, axis=0, keepdims=True)

        @pl.when(first)
        def _():
            dw_ref[...] = dw

        @pl.when(jnp.logical_not(first))
        def _():
            dw_ref[...] += dw

    blk = pl.BlockSpec((tm, bw), lambda i, j: (i, j))
    in_specs = [blk, pl.BlockSpec((tm, bw), lambda i, j: (rb0 + i, col_off // bw + j)),
                pl.BlockSpec((1, HD), lambda i, j: (0, 0))]
    args = [dt, px, nw]
    if rope:
        in_specs += [pl.BlockSpec((tm, HD), lambda i, j: (i % pb, 0))] * 2
        args += [cos, sin]
    return pl.pallas_call(
        body, name=name, grid=(rows // tm, heads // hb), in_specs=in_specs,
        out_specs=[blk, pl.BlockSpec((1, HD), lambda i, j: (0, 0))],
        out_shape=[_sds((rows, heads * HD), BF16), _sds((1, HD), F32)],
        compiler_params=_params(("arbitrary", "arbitrary")),
    )(*args)


def _ret_bwd(px, lg, do16, hist_f, hist_b, nb, nc, cx):
    t_rows = nb * nc * CH

    def body(lg_ref, *refs):
        ins = (refs[0:5], refs[7:12])
        do_refs = (refs[5], refs[12])
        h_refs = (refs[6], refs[13])
        ctx_refs = refs[14:17]
        outs = (refs[17:20], refs[20:23])
        dck_ref, dcv_ref, dlg_ref = refs[23:26]
        dss = (refs[26], refs[27])
        c = pl.program_id(1)

        @pl.when(c == 0)
        def _():
            dss[0][...] = jnp.zeros_like(dss[0])
            dss[1][...] = jnp.zeros_like(dss[1])
            dlg_ref[...] = jnp.zeros_like(dlg_ref)

        for d in range(2):
            dq_ref, dk_ref, dv_ref = outs[d]
            for h in range(RH):
                lg_d = lg_ref[d, h]
                mask, relf, qd, qe, kd, ke = _decays(lg_d, d == 0)
                g_ch = jnp.exp(lg_d * CH)
                q, k, v16 = _head_qkv(ins[d], h)
                q16 = q.astype(BF16)
                k16 = k.astype(BF16)
                do16v = do_refs[d][:, h * DV:(h + 1) * DV]
                st16 = h_refs[d][h]
                dst = dss[d][h]
                dst16 = dst.astype(BF16)
                a = _dot(q16, k16, NT) * mask
                dp = _dot(do16v, v16, NT)
                da16 = (dp * mask).astype(BF16)
                dq_cross = _dot(do16v, st16, NT) * qd
                dq_ref[:, h * DK:(h + 1) * DK] = (_dot(da16, k16) + dq_cross).astype(BF16)
                dk_state = _dot(v16, dst16, NT) * kd
                dk_ref[:, h * DK:(h + 1) * DK] = ((_dot(da16, q16, TN) + dk_state) * (DK ** -0.5)).astype(BF16)
                dv = _dot(a.astype(BF16), do16v, TN) + _dot((k * kd).astype(BF16), dst16)
                dv_ref[:, h * DV:(h + 1) * DV] = dv.astype(BF16)
                dlg = (jnp.sum(relf * a * dp)
                       + jnp.sum(qe * jnp.sum(q * dq_cross, axis=-1, keepdims=True))
                       + jnp.sum(ke * jnp.sum(k * dk_state, axis=-1, keepdims=True))
                       + CH * g_ch * jnp.sum(dst * st16.astype(F32)))
                row = d * RH + h
                dlg_ref[row:row + 1, :] += jnp.broadcast_to(dlg, (1, 128))
                dss[d][h] = g_ch * dst + _dot((q * qd).astype(BF16), do16v, TN)

        @pl.when(c == nc - 1)
        def _():
            pos = lax.broadcasted_iota(jnp.int32, (cx, 1), 0).astype(F32)
            for h in range(RH):
                k, v16 = _ctx_kv(ctx_refs, h)
                dk = jnp.zeros((cx, DK), F32)
                dv = jnp.zeros((cx, DV), F32)
                for d, e in enumerate((cx - 1.0 - pos, pos)):
                    w = jnp.exp(lg_ref[d, h] * e)
                    ds16 = dss[d][h].astype(BF16)
                    t = _dot(v16, ds16, NT)
                    dk += t * w
                    dv += _dot((k * w).astype(BF16), ds16)
                    dlg = jnp.sum(e * w * jnp.sum(k * t, axis=-1, keepdims=True))
                    row = d * RH + h
                    dlg_ref[row:row + 1, :] += jnp.broadcast_to(dlg, (1, 128))
                dck_ref[:, h * DK:(h + 1) * DK] = (dk * (DK ** -0.5)).astype(BF16)
                dcv_ref[:, h * DV:(h + 1) * DV] = dv.astype(BF16)

    def fw(b, c):
        return b * nc + nc - 1 - c

    def bw(b, c):
        return b * nc + c

    def rows(rowf, width):
        return pl.BlockSpec((CH, width), lambda b, c: (rowf(b, c), 0))

    def hist(rowf):
        return pl.BlockSpec((None, None, RH, DK, DV), lambda b, c: (b, rowf(0, c), 0, 0, 0))

    in_specs = [pl.BlockSpec(memory_space=pltpu.SMEM)]
    out_specs = []
    for rowf in (fw, bw):
        in_specs += _wide_specs(rowf) + [rows(rowf, RH * DV), hist(rowf)]
        out_specs += [rows(rowf, RH * DK), rows(rowf, RH * DK), rows(rowf, RH * DV)]
    in_specs += _ctx_specs(t_rows, cx)
    out_specs += [pl.BlockSpec((cx, RH * DK), lambda b, c: (b, 0)), pl.BlockSpec((cx, RH * DV), lambda b, c: (b, 0)),
                  pl.BlockSpec((None, 8, 128), lambda b, c: (b, 0, 0))]
    qk = _sds((t_rows, RH * DK), BF16)
    vv = _sds((t_rows, RH * DV), BF16)
    return pl.pallas_call(
        body, name="ret_bwd", grid=(nb, nc), in_specs=in_specs, out_specs=out_specs,
        out_shape=[qk, qk, vv, qk, qk, vv, _sds((nb * cx, RH * DK), BF16), _sds((nb * cx, RH * DV), BF16),
                   _sds((nb, 8, 128), F32)],
        scratch_shapes=[pltpu.VMEM((RH, DK, DV), F32), pltpu.VMEM((RH, DK, DV), F32)],
        compiler_params=_params(("parallel", "arbitrary")),
    )(lg, *([px] * 5), do16, hist_f, *([px] * 5), do16, hist_b, *([px] * 3))


def _assemble_lat(dp_all, dk_f, dk_b, dv_f, dv_b, dak16, dvx, dq_f, dq_b, tm):
    t_rows = dk_f.shape[0]

    def body(_, dkf, dkb, dvf, dvb, dak, dav, dqf, dqb, o_ref):
        o_ref[:, RK:RK + RH * DK] = (dkf[...].astype(F32) + dkb[...].astype(F32)).astype(BF16)
        o_ref[:, RV:RV + RH * DV] = (dvf[...].astype(F32) + dvb[...].astype(F32)).astype(BF16)
        o_ref[:, AK:AK + HKV * HD] = dak[...]
        o_ref[:, AV:AV + HKV * HD] = dav[...].astype(BF16)
        o_ref[:, RQ:RQ + RH * DK] = (dqf[...].astype(F32) + dqb[...].astype(F32)).astype(BF16)

    args = (dk_f, dk_b, dv_f, dv_b, dak16, dvx, dq_f, dq_b)
    return pl.pallas_call(
        body, name="assemble_lat", grid=(t_rows // tm,),
        in_specs=[pl.BlockSpec(memory_space=pl.ANY)]
        + [pl.BlockSpec((tm, a.shape[1]), lambda i: (i, 0)) for a in args],
        out_specs=pl.BlockSpec((tm, RG), lambda i: (i, 0)), out_shape=_sds(dp_all.shape, BF16),
        input_output_aliases={0: 0},
        compiler_params=_params(("parallel",)),
    )(dp_all, *args)


def _assemble_ctx(dp_all, dck16, dcv16, dcak16, dvc, t_rows, tm):
    c_rows = dck16.shape[0]
    rb = t_rows // tm

    def body(_, dck, dcv, dcak, dcav, o_ref):
        o_ref[:, RK:RK + RH * DK] = dck[...]
        o_ref[:, RV:RV + RH * DV] = dcv[...]
        o_ref[:, AK:AK + HKV * HD] = dcak[...]
        o_ref[:, AV:AV + HKV * HD] = dcav[...].astype(BF16)
        o_ref[:, KV_COLS:] = jnp.zeros((tm, IN_COLS - KV_COLS), BF16)

    args = (dck16, dcv16, dcak16, dvc)
    return pl.pallas_call(
        body, name="assemble_ctx", grid=(c_rows // tm,),
        in_specs=[pl.BlockSpec(memory_space=pl.ANY)]
        + [pl.BlockSpec((tm, a.shape[1]), lambda i: (i, 0)) for a in args],
        out_specs=pl.BlockSpec((tm, IN_COLS), lambda i: (rb + i, 0)), out_shape=_sds(dp_all.shape, BF16),
        input_output_aliases={0: 0},
        compiler_params=_params(("parallel",)),
    )(dp_all, *args)


def _norm_bwd(dh, x2, mod3, norm_w, dxn, row_off, rows_per_group, group0, tm, name):
    with_dx = dxn is not None
    rows = x2.shape[0]
    rb0 = row_off // tm
    bpg = rows_per_group // tm
    ngroups = rows // rows_per_group

    def body(*refs):
        if with_dx:
            dh_ref, x_ref, sc_ref, nw_ref, dxn_ref, dx_ref, dsh_ref, dsc_ref, dnw_ref = refs
        else:
            dh_ref, x_ref, sc_ref, nw_ref, dsh_ref, dsc_ref, dnw_ref = refs
        i = pl.program_id(0)
        dhv = dh_ref[...]
        xv = x_ref[...]
        nw = nw_ref[...]
        r = lax.rsqrt(jnp.mean(xv * xv, axis=-1, keepdims=True) + EPS)
        xh = xv * r
        dm = dhv * (1.0 + sc_ref[...])
        dsh = jnp.sum(dhv, axis=0, keepdims=True)
        dsc = jnp.sum(dhv * (xh * nw), axis=0, keepdims=True)
        dnw = jnp.sum(dm * xh, axis=0, keepdims=True)
        if with_dx:
            dxh = dm * nw
            dx_ref[...] = dxn_ref[...] + r * (dxh - xh * jnp.mean(dxh * xh, axis=-1, keepdims=True))

        @pl.when(i % bpg == 0)
        def _():
            dsh_ref[...] = dsh
            dsc_ref[...] = dsc

        @pl.when(i % bpg != 0)
        def _():
            dsh_ref[...] += dsh
            dsc_ref[...] += dsc

        @pl.when(i == 0)
        def _():
            dnw_ref[...] = dnw

        @pl.when(i > 0)
        def _():
            dnw_ref[...] += dnw

    grp = pl.BlockSpec((None, 1, D), lambda i: (i // bpg, 0, 0))
    in_specs = [pl.BlockSpec((tm, D), lambda i: (rb0 + i, 0)), pl.BlockSpec((tm, D), lambda i: (i, 0)),
                pl.BlockSpec((None, 1, D), lambda i: (group0 + i // bpg, 0, 1)),
                pl.BlockSpec((1, D), lambda i: (0, 0))]
    args = [dh, x2, mod3, norm_w]
    out_specs = [grp, grp, pl.BlockSpec((1, D), lambda i: (0, 0))]
    out_shape = [_sds((ngroups, 1, D), F32), _sds((ngroups, 1, D), F32), _sds((1, D), F32)]
    if with_dx:
        in_specs.append(pl.BlockSpec((tm, D), lambda i: (i, 0)))
        args.append(dxn)
        out_specs.insert(0, pl.BlockSpec((tm, D), lambda i: (i, 0)))
        out_shape.insert(0, _sds((rows, D), F32))
    return pl.pallas_call(
        body, name=name, grid=(rows // tm,), in_specs=in_specs, out_specs=out_specs, out_shape=out_shape,
        compiler_params=_params(("arbitrary",)),
    )(*args)


def _small_final(dmod_all, dmodc_parts, c_rows, dm_loc_rows, nw_parts, misc_parts, c_ctx, r_pad, w_ada16):
    loc = dm_loc_rows.shape[1]

    def body(dm_ref, dmc_ref, c_ref, dml_ref, nwp_ref, mp_ref, cc_ref, r_ref, w_ref,
             gb_ref, gc_ref, gnw_ref, misc_ref, gwa_ref):
        dmc = jnp.sum(dmc_ref[...], axis=0, keepdims=True)
        gb_ref[...] = jnp.sum(dm_ref[...], axis=0, keepdims=True) + dmc
        dsc = _dot(jnp.broadcast_to(dmc, (8, 3 * D)).astype(BF16), w_ref[...], NT)[0:1, :]
        gc_ref[...] = dsc * _dsilu(cc_ref[...])
        gnw_ref[...] = jnp.sum(nwp_ref[...], axis=0, keepdims=True)
        misc = jnp.sum(mp_ref[...], axis=0, keepdims=True)
        y = jnp.exp2(r_ref[...])
        lane = lax.broadcasted_iota(jnp.int32, (1, D), 1)
        is_decay = jnp.logical_and(lane >= 2 * HD, lane < 2 * HD + 2 * RH)
        misc_ref[...] = misc * jnp.where(is_decay, -(y * np.float32(np.log(2.0))) / (1.0 - y), 1.0)
        gwa_ref[...] = _dot(_silu(c_ref[...]).astype(BF16), dml_ref[...].astype(BF16), TN)

    return pl.pallas_call(
        body, name="small_final",
        out_shape=[_sds((1, 3 * D), F32), _sds((1, D), F32), _sds((1, D), F32), _sds((1, D), F32), _sds((D, loc), F32)],
        compiler_params=pltpu.CompilerParams(vmem_limit_bytes=VMEM_LIMIT),
    )(dmod_all, dmodc_parts, c_rows, dm_loc_rows, nw_parts, misc_parts, c_ctx, r_pad, w_ada16)


def _adamw_math(w, g, m, v):
    nm = B1 * m + (1.0 - B1) * g
    nv = B2 * v + (1.0 - B2) * (g * g)
    return -LR * ((nm / (1.0 - B1 ** STEP)) / (jnp.sqrt(nv / (1.0 - B2 ** STEP)) + ADAM_EPS) + WD * w), nm, nv


def _adamw(w, g, m, v, name):
    rows, cols = w.shape
    tm = _pick(rows, 448, 8)

    def body(w_ref, g_ref, m_ref, v_ref, d_ref, nm_ref, nv_ref):
        d_ref[...], nm_ref[...], nv_ref[...] = _adamw_math(w_ref[...], g_ref[...], m_ref[...], v_ref[...])

    blk = pl.BlockSpec((tm, cols), lambda i: (i, 0))
    return pl.pallas_call(
        body, name=name, grid=(rows // tm,), in_specs=[blk] * 4, out_specs=[blk] * 3,
        out_shape=[_sds((rows, cols), F32)] * 3, compiler_params=_params(("parallel",)),
    )(w, g, m, v)


def _mesh_pos():
    return lax.axis_index("x"), lax.axis_index("y"), lax.axis_index("c")


def _all_gather(arrs, name):
    n = len(arrs)

    def body(*refs):
        ins, outs = refs[:n], refs[n:2 * n]
        send_sems, recv_sems, local_sems = refs[2 * n:]
        x, y, c = _mesh_pos()
        me, sib = (x, y, c), (x, y, 1 - c)
        chips = [(1 - x, y), (x, 1 - y), (1 - x, 1 - y)]

        def slot(p):
            return 4 * p[0] + 2 * p[1] + p[2]

        def copy(a, k, block, to, own):
            dst = outs[a].at[slot(block)]
            return pltpu.make_async_remote_copy(
                src_ref=ins[a] if own else dst, dst_ref=dst, send_sem=send_sems.at[a, k], recv_sem=recv_sems.at[a, k],
                device_id=to, device_id_type=MESH_T)

        mine = [pltpu.make_async_copy(ins[a], outs[a].at[slot(me)], local_sems.at[a]) for a in range(n)]
        for cp in mine:
            cp.start()
        first = []
        for a in range(n):
            first.append(copy(a, 0, me, sib, True))
            first += [copy(a, 1 + j, me, (*chip, c), True) for j, chip in enumerate(chips)]
        for cp in first:
            cp.start()
        passed = []
        for j, chip in enumerate(chips):
            for a in range(n):
                copy(a, 1 + j, (*chip, c), me, False).wait_recv()
                fwd = copy(a, 4 + j, (*chip, c), sib, False)
                fwd.start()
                passed.append(fwd)
        for a in range(n):
            copy(a, 0, sib, me, False).wait_recv()
            for j, chip in enumerate(chips):
                copy(a, 4 + j, (*chip, 1 - c), me, False).wait_recv()
        for cp in first + passed:
            cp.wait_send()
        for cp in mine:
            cp.wait()

    hbm = pl.BlockSpec(memory_space=pl.ANY)
    return pl.pallas_call(
        body, name=name, in_specs=[hbm] * n, out_specs=[hbm] * n,
        out_shape=[_sds((N_DEV,) + a.shape, a.dtype) for a in arrs],
        scratch_shapes=[pltpu.SemaphoreType.DMA((n, 7)), pltpu.SemaphoreType.DMA((n, 7)), pltpu.SemaphoreType.DMA((n,))],
    )(*arrs)


def _pair_exchange(arrs, name):
    n = len(arrs)

    def body(*refs):
        ins, outs = refs[:n], refs[n:2 * n]
        send_sems, recv_sems = refs[2 * n:]
        x, y, c = _mesh_pos()
        sib = (x, y, 1 - c)
        sends = []
        for a in range(n):
            for k in range(4):
                sends.append(pltpu.make_async_remote_copy(
                    src_ref=ins[a].at[2 * k + 1 - c], dst_ref=outs[a].at[k], send_sem=send_sems.at[a, k],
                    recv_sem=recv_sems.at[a, k], device_id=sib, device_id_type=MESH_T))
        for cp in sends:
            cp.start()
        for cp in sends:
            cp.wait_recv()
        for cp in sends:
            cp.wait_send()

    hbm = pl.BlockSpec(memory_space=pl.ANY)
    return pl.pallas_call(
        body, name=name, in_specs=[hbm] * n, out_specs=[hbm] * n,
        out_shape=[_sds((4,) + a.shape[1:], a.dtype) for a in arrs],
        scratch_shapes=[pltpu.SemaphoreType.DMA((n, 4)), pltpu.SemaphoreType.DMA((n, 4))],
    )(*arrs)


def _pair_add(part, got, core, name):
    _, rows, cols = part.shape
    tm = _pick(rows, 672, 16)
    p4 = part.reshape(4, 2, rows, cols)

    def body(core_ref, p_ref, g_ref, o_ref):
        o_ref[...] = (p_ref[...].astype(F32) + g_ref[...].astype(F32)).astype(BF16)

    blk = pl.BlockSpec((None, tm, cols), lambda k, i, cr: (k, i, 0))
    return pl.pallas_call(
        body, name=name,
        grid_spec=pltpu.PrefetchScalarGridSpec(
            num_scalar_prefetch=1, grid=(4, rows // tm),
            in_specs=[pl.BlockSpec((None, None, tm, cols), lambda k, i, cr: (k, cr[0], i, 0)), blk], out_specs=blk),
        out_shape=_sds((4, rows, cols), BF16), compiler_params=_params(("parallel", "parallel")),
    )(core, p4, got)


def _chip_sum_adamw(pair_sums, landed, chip, w, m, v, name):
    _, rows, cols = pair_sums.shape
    tm = _pick(rows, 448, 16)

    def body(chip_ref, s_ref, l_ref, w_ref, m_ref, v_ref, g_ref, d_ref, nm_ref, nv_ref):
        acc = s_ref[...].astype(F32)
        for j in range(3):
            acc = acc + l_ref[j].astype(F32)
        g_ref[...] = acc
        d_ref[...], nm_ref[...], nv_ref[...] = _adamw_math(w_ref[...], acc, m_ref[...], v_ref[...])

    blk = pl.BlockSpec((tm, cols), lambda i, ch: (i, 0))
    return pl.pallas_call(
        body, name=name,
        grid_spec=pltpu.PrefetchScalarGridSpec(
            num_scalar_prefetch=1, grid=(rows // tm,),
            in_specs=[pl.BlockSpec((None, tm, cols), lambda i, ch: (ch[0], i, 0)),
                      pl.BlockSpec((3, tm, cols), lambda i, ch: (0, i, 0)), blk, blk, blk],
            out_specs=[blk] * 4),
        out_shape=[_sds((rows, cols), F32)] * 4, compiler_params=_params(("parallel",)),
    )(chip, pair_sums, landed, w, m, v)


_HBM = pl.BlockSpec(memory_space=pltpu.HBM)
_SEM = pl.BlockSpec(memory_space=pltpu.SEMAPHORE)
_EFFECT = pltpu.SideEffectType.DATAFLOW_SIDE_EFFECTING


def _chip_routes(n):
    def plan(x, y, c):
        routes = []
        for a in range(n):
            for j in range(1, 4):
                px, py = x ^ (j >> 1), y ^ (j & 1)
                routes.append((a, 2 * px + py, (px, py, c), j - 1))
        return routes
    return plan, 3 * n


def _pair_routes(n):
    def plan(x, y, c):
        return [(a, 2 * k + 1 - c, (x, y, 1 - c), k) for a in range(n) for k in range(4)]
    return plan, 4 * n


def _bcast_routes(n):
    def plan(x, y, c):
        routes = []
        for a in range(n):
            for k in range(1, N_DEV):
                peer = (x ^ ((k >> 2) & 1), y ^ ((k >> 1) & 1), c ^ (k & 1))
                routes.append((a, 0, peer, 4 * x + 2 * y + c))
        return routes
    return plan, 7 * n


def _route_copies(srcs, lands, send_sems, recv_sems, routes):
    return [pltpu.make_async_remote_copy(
        src_ref=srcs[a].at[sb], dst_ref=lands[a].at[lb], send_sem=send_sems.at[r], recv_sem=recv_sems.at[r],
        device_id=peer, device_id_type=MESH_T) for r, (a, sb, peer, lb) in enumerate(routes)]


def _exchange_start(srcs, lands, routes, name, after=()):
    plan, count = routes
    n = len(srcs)
    n_in = 2 * n + len(after)

    def body(*refs):
        send_sems, recv_sems = refs[n_in], refs[n_in + 1]
        token = refs[-1]
        for cp in _route_copies(refs[:n], refs[n:2 * n], send_sems, recv_sems, plan(*_mesh_pos())):
            cp.start()
        token[...] = jnp.zeros_like(token)

    args = [pltpu.with_memory_space_constraint(a, pltpu.HBM) for a in list(srcs) + list(lands)]
    out = pl.pallas_call(
        body, name=name,
        out_shape=(pltpu.SemaphoreType.DMA((count,)), pltpu.SemaphoreType.DMA((count,)),
                   *[pltpu.HBM(a.shape, a.dtype) for a in args], _sds((8, 128), F32)),
        in_specs=[_HBM] * (2 * n) + [pl.BlockSpec(memory_space=pl.ANY)] * len(after),
        out_specs=(_SEM, _SEM, *([_HBM] * (2 * n)), pl.BlockSpec(memory_space=pltpu.VMEM)),
        input_output_aliases={i: 2 + i for i in range(2 * n)},
        compiler_params=pltpu.CompilerParams(has_side_effects=_EFFECT),
    )(*args, *after)
    return (out[0], out[1], list(out[2:2 + 2 * n]), routes), out[-1]


def _exchange_wait(state, after, name):
    send_sems, recv_sems, bufs, (plan, count) = state
    n = len(bufs) // 2

    def body(*refs):
        send_s, recv_s = refs[2 * n], refs[2 * n + 1]
        for cp in _route_copies(refs[:n], refs[n:2 * n], send_s, recv_s, plan(*_mesh_pos())):
            cp.wait_send()
            cp.wait_recv()

    out = pl.pallas_call(
        body, name=name, out_shape=tuple(pltpu.HBM(a.shape, a.dtype) for a in bufs),
        in_specs=[_HBM] * (2 * n) + [_SEM, _SEM, pl.BlockSpec(memory_space=pl.ANY)], out_specs=tuple([_HBM] * (2 * n)),
        input_output_aliases={i: i for i in range(2 * n)},
        compiler_params=pltpu.CompilerParams(has_side_effects=_EFFECT),
    )(*bufs, send_sems, recv_sems, after)
    return list(out[:n]), list(out[n:])


def _group_routes(js):
    def plan(x, y, c):
        return [(0, 0, (x ^ (j >> 1), y ^ (j & 1), c), 2 * j + c) for j in js]
    return plan, len(js)


def _pair_fill(groups, js, name, after=()):
    def body(*refs):
        g_ref, send_sems, recv_sems = refs[-3:]
        x, y, c = _mesh_pos()
        sends = []
        for n, j in enumerate(js):
            mine = g_ref.at[2 * j + c]
            sends.append(pltpu.make_async_remote_copy(
                src_ref=mine, dst_ref=mine, send_sem=send_sems.at[n], recv_sem=recv_sems.at[n],
                device_id=(x, y, 1 - c), device_id_type=MESH_T))
        for cp in sends:
            cp.start()
        for n, j in enumerate(js):
            pltpu.make_async_remote_copy(
                src_ref=g_ref.at[2 * j + c], dst_ref=g_ref.at[2 * j + 1 - c], send_sem=send_sems.at[n],
                recv_sem=recv_sems.at[n], device_id=(x, y, 1 - c), device_id_type=MESH_T).wait_recv()
        for cp in sends:
            cp.wait_send()

    hbm = pl.BlockSpec(memory_space=pl.ANY)
    return pl.pallas_call(
        body, name=name, in_specs=[hbm] * (1 + len(after)), out_specs=hbm, out_shape=_sds(groups.shape, groups.dtype),
        input_output_aliases={0: 0},
        scratch_shapes=[pltpu.SemaphoreType.DMA((len(js),)), pltpu.SemaphoreType.DMA((len(js),))],
    )(groups, *after)


def _in_proj_group(h_all, groups, j0, ng, chip, px_prev, after, name):
    rows_all = h_all.shape[0]
    gcols = IN_COLS // 4
    tm = _pick(rows_all, 1536, 128)
    g4 = groups.reshape(4, gcols, D)

    n_lead = (1 if px_prev is not None else 0) + len(after)
    lead = ([px_prev] if px_prev is not None else []) + list(after)

    def body(chip_ref, *refs):
        h_ref, w_ref, o_ref = refs[n_lead:]
        o_ref[...] = _dot(h_ref[...], w_ref[...], NT).astype(BF16)

    return pl.pallas_call(
        body, name=name,
        grid_spec=pltpu.PrefetchScalarGridSpec(
            num_scalar_prefetch=1, grid=(ng, rows_all // tm),
            in_specs=[pl.BlockSpec(memory_space=pl.ANY)] * n_lead
            + [pl.BlockSpec((tm, D), lambda n, i, ch: (i, 0)),
               pl.BlockSpec((None, gcols, D), lambda n, i, ch: (j0 + n, 0, 0))],
            out_specs=pl.BlockSpec((tm, gcols), lambda n, i, ch: (i, ch[0] ^ (j0 + n)))),
        out_shape=_sds((rows_all, IN_COLS), BF16),
        input_output_aliases={1: 0} if px_prev is not None else {},
        compiler_params=_params(("parallel", "parallel")),
    )(chip, *lead, h_all, g4)


def _d_h_groups(dp_all, groups, chip, i0, ni, dh_prev, after):
    rows_all = dp_all.shape[0]
    gcols = IN_COLS // 4
    tm = _D_H_ROWS
    g4 = groups.reshape(4, gcols, D)
    lead = ([dh_prev] if dh_prev is not None else []) + list(after)
    n_lead = len(lead)

    def body(chip_ref, *refs):
        a_ref, w_ref, o_ref = refs[n_lead:]
        j = pl.program_id(1)
        part = _dot(a_ref[...], w_ref[...])

        @pl.when(j == 0)
        def _():
            o_ref[...] = part

        @pl.when(j > 0)
        def _():
            o_ref[...] += part

    return pl.pallas_call(
        body, name="d_h_%d" % i0,
        grid_spec=pltpu.PrefetchScalarGridSpec(
            num_scalar_prefetch=1, grid=(ni, 4),
            in_specs=[pl.BlockSpec(memory_space=pl.ANY)] * n_lead
            + [pl.BlockSpec((tm, gcols), lambda i, j, ch: (i0 + i, ch[0] ^ j)),
               pl.BlockSpec((None, gcols, D), lambda i, j, ch: (j, 0, 0))],
            out_specs=pl.BlockSpec((tm, D), lambda i, j, ch: (i0 + i, 0))),
        out_shape=_sds((rows_all, D), F32),
        input_output_aliases={1: 0} if dh_prev is not None else {},
        compiler_params=_params(("parallel", "arbitrary")),
    )(chip, *lead, dp_all, g4)


def _reduce_scatter_start(parts, core, name):
    got = _pair_exchange(parts, name + "_pair")
    return _reduce_scatter_send(parts, got, core, name)


def _reduce_scatter_send(parts, got, core, name):
    sums = [_pair_add(p, g, core, "%s_add_%d" % (name, i)) for i, (p, g) in enumerate(zip(parts, got))]
    lands = [lax.empty((3,) + s_.shape[1:], BF16) for s_ in sums]
    return _exchange_start(sums, lands, _chip_routes(len(sums)), name + "_start")


def _reduce_scatter_finish(rs_state, after, chip, wmv, name):
    sums, landed = _exchange_wait(rs_state, after, name + "_wait")
    return [_chip_sum_adamw(s_, l_, chip, *t, "%s_adamw_%d" % (name, i))
            for i, (s_, l_, t) in enumerate(zip(sums, landed, wmv))]


def _local_step(x, c, ctx, norm_w, ret_log2_decay, q_norm_w, k_norm_w, loss_target,
                mod, proj_in, get_w_o, on_out_grads, on_in_grad, started=()):
    nb, seq, _ = x.shape
    cx = ctx.shape[1]
    t_rows, c_rows = nb * seq, nb * cx
    rows_all = t_rows + c_rows
    nc = seq // CH
    tm = _pick(seq, 256, 128)
    te = _pick(seq, 512, 128)
    assert cx % tm == 0 and t_rows % cx == 0 and seq % GRID_W == 0

    x2 = x.reshape(t_rows, D)
    ctx2 = ctx.reshape(c_rows, D)
    tgt = loss_target.reshape(t_rows, D)
    lg = _log_gamma(ret_log2_decay)
    cos, sin = _rope_tables(seq)

    mod3 = mod[:, None, :]
    h_all = _norm_fwd(x2, mod3, norm_w, rows_all, 0, seq, 0, None, te, "norm_fwd", after=started)
    h_all = _norm_fwd(ctx2, mod3, norm_w, rows_all, t_rows, c_rows, nb, h_all, tm, "norm_fwd_ctx")
    px = proj_in(h_all)
    o_f, o_b, hist_f, hist_b = _ret_fwd(px, lg, nb, nc, cx)
    q16 = _qk_prep(px, q_norm_w, cos, sin, t_rows, 0, AQ, HQ, 4, seq, te, "q_prep")
    kx16 = _qk_prep(px, k_norm_w, cos, sin, t_rows, 0, AK, HKV, HKV, seq, te, "k_prep")
    kc16 = _qk_prep(px, k_norm_w, None, None, c_rows, t_rows, AK, HKV, HKV, seq, tm, "kc_prep")
    o_att, yatt16, lse = _att_fwd(q16, kx16, kc16, px, nb, seq, cx, te)
    w_o_ret16, w_o_att16, w_out16 = get_w_o(lse)
    yret16, a_ret, a_att, y16, dxn, dout16, dgate, loss_b = _merge_out(
        o_f, o_b, yatt16, px, w_o_ret16, w_o_att16, w_out16, x2, tgt, mod3, nb, seq, tm)

    gw_out = _matmul(y16, dout16, ta=True, tm=D, tn=D, tk=D, out_dtype=BF16, name="gw_out")
    da_ret16, da_att16, do16, dao16, delta, dp_all = _bwd_branches(
        dout16, w_out16, w_o_ret16, w_o_att16, px, a_ret, a_att, o_f, o_b, o_att, rows_all, tm)
    gw_o_ret = _matmul(yret16, da_ret16, ta=True, tm=D, tn=D, tk=D, out_dtype=BF16, name="gw_o_ret")
    gw_o_att = _matmul(yatt16, da_att16, ta=True, tm=D, tn=D, tk=D, out_dtype=BF16, name="gw_o_att")
    out_state, out_started = on_out_grads([gw_o_ret, gw_o_att, gw_out])
    dp_all, gq, dkx, dvx, dkc, dvc = _att_bwd(q16, kx16, kc16, px, dao16, delta, lse, q_norm_w, cos, sin, dp_all, nb,
                                              seq, cx, te, after=out_started)
    dak16, gk_lat = _qk_prep_bwd(dkx.reshape(t_rows, HKV * HD), px, k_norm_w, cos, sin, t_rows, 0, AK, HKV, HKV, seq, te,
                                 "k_prep_bwd")
    dcak16, gk_ctx = _qk_prep_bwd(dkc.reshape(c_rows, HKV * HD), px, k_norm_w, None, None, c_rows, t_rows, AK, HKV, HKV,
                                  seq, tm, "kc_prep_bwd")
    dq_f, dk_f, dv_f, dq_b, dk_b, dv_b, dck16, dcv16, dlg_scan = _ret_bwd(px, lg, do16, hist_f, hist_b, nb, nc, cx)
    dp_all = _assemble_lat(dp_all, dk_f, dk_b, dv_f, dv_b, dak16, dvx.reshape(t_rows, HKV * HD), dq_f, dq_b, tm)
    dp_all = _assemble_ctx(dp_all, dck16, dcv16, dcak16, dvc.reshape(c_rows, HKV * HD), t_rows, tm)
    gw_in_t = _matmul(dp_all, h_all, ta=True, tm=1536, tn=D, tk=2304, out_dtype=BF16, name="gw_in")
    in_state, dh = on_in_grad(gw_in_t, dp_all)
    grad_x, dsh, dsc, gnw_lat = _norm_bwd(dh, x2, mod3, norm_w, dxn, 0, seq, 0, te, "norm_bwd")
    dsh_c, dsc_c, gnw_ctx = _norm_bwd(dh, ctx2, mod3, norm_w, None, t_rows, c_rows, nb, tm, "norm_bwd_ctx")

    dlg = jnp.sum(dlg_scan[:, :, 0], axis=0).reshape(1, 2 * RH)
    misc = jnp.concatenate([gq, gk_lat + gk_ctx, dlg, jnp.sum(loss_b[:, 0, 0]).reshape(1, 1),
                            jnp.zeros((1, D - 2 * HD - 2 * RH - 1), F32)], axis=1)
    rows = []
    for b in range(nb):
        rows += [dsh[b], dsc[b], dgate[b]]
    rows += [dsh_c[0], dsc_c[0]] + [c[b:b + 1] for b in range(nb)] + [gnw_lat + gnw_ctx, misc]
    payload = jnp.concatenate(rows + [jnp.zeros((PAY_ROWS - len(rows), D), F32)], axis=0)
    return grad_x.reshape(nb, seq, D), out_state, in_state, payload


def _finish_small(gathered, nb, c_ctx, ret_log2_decay, w_ada16, dev):
    n_dev = gathered.shape[0]
    loc = 3 * D // n_dev
    dmod_all = gathered[:, :3 * nb].reshape(n_dev * nb, 3 * D)
    dmodc_parts = jnp.concatenate([gathered[:, 3 * nb:3 * nb + 2].reshape(n_dev, 2 * D), jnp.zeros((n_dev, D), F32)], axis=1)
    c_all = gathered[:, 3 * nb + 2:4 * nb + 2].reshape(n_dev * nb, D)
    nw_parts = gathered[:, 4 * nb + 2]
    misc_parts = gathered[:, 4 * nb + 3]
    n_rows = n_dev * nb + n_dev
    pad = (-n_rows) % 16
    c_rows = jnp.concatenate([c_all, jnp.broadcast_to(c_ctx.reshape(1, D), (n_dev, D)), jnp.zeros((pad, D), F32)], axis=0)
    dm_rows = jnp.concatenate([dmod_all, dmodc_parts, jnp.zeros((pad, 3 * D), F32)], axis=0)
    dm_loc_rows = lax.dynamic_slice_in_dim(dm_rows, dev * loc, loc, axis=1)
    r_pad = jnp.full((1, D), -1.0, F32).at[:, 2 * HD:2 * HD + 2 * RH].set(ret_log2_decay.reshape(1, 2 * RH))
    gb, gc, gnw, misc, gwa = _small_final(dmod_all, dmodc_parts, c_rows, dm_loc_rows, nw_parts, misc_parts,
                                          c_ctx.reshape(1, D), r_pad, w_ada16)
    return (gb, gc, gnw, misc[:, :HD], misc[:, HD:2 * HD], misc[:, 2 * HD:2 * HD + 2 * RH], gwa,
            misc[0, 2 * HD + 2 * RH])


def kernel(x, c, ctx, c_ctx, norm_w, w_ada, b_ada, w_in, ret_log2_decay, q_norm_w, k_norm_w, w_o_ret, w_o_att, w_out, loss_target, m_c_ctx, m_norm_w, m_w_ada, m_b_ada, m_w_in, m_ret_log2_decay, m_q_norm_w, m_k_norm_w, m_w_o_ret, m_w_o_att, m_w_out, v_c_ctx, v_norm_w, v_w_ada, v_b_ada, v_w_in, v_ret_log2_decay, v_q_norm_w, v_k_norm_w, v_w_o_ret, v_w_o_att, v_w_out):
    nb = x.shape[0]
    mx, my, mc = _mesh_pos()
    dev = 4 * mx + 2 * my + mc
    core = jnp.reshape(mc, (1,)).astype(jnp.int32)
    chip = jnp.reshape(2 * mx + my, (1,)).astype(jnp.int32)

    n_loc = 3 * D // N_DEV
    c8 = jnp.zeros((8, D), F32).at[:nb].set(c).at[nb].set(c_ctx)
    c_land = lax.dynamic_update_slice(lax.empty((N_DEV, 8, D), F32), c8[None], (dev, 0, 0))
    c_state, c_token = _exchange_start([c8[None]], [c_land], _bcast_routes(1), "gather_c_start")
    w_in_t = jnp.transpose(w_in[0])
    in_shard = w_in_t.astype(BF16)
    groups = lax.dynamic_update_slice(lax.empty((N_DEV,) + in_shard.shape, BF16), in_shard[None], (mc, 0, 0))
    groups = _pair_fill(groups, (0,), "gather_in_pair", after=(c_token,))
    _, (c_all,) = _exchange_wait(c_state, groups, "gather_c_wait")
    ada_shard = w_ada[0].astype(BF16)
    b_loc = lax.dynamic_slice(b_ada, (0, dev * n_loc), (1, n_loc))
    mod_cols = _mod_part(c_all.reshape(N_DEV * 8, D), ada_shard, b_loc)
    (mod_all,) = _all_gather([mod_cols], "gather_mod")
    mod = jnp.transpose(lax.dynamic_slice(mod_all, (0, dev * 8, 0), (N_DEV, 8, n_loc)), (1, 0, 2)).reshape(8, 3 * D)
    ada_land = lax.dynamic_update_slice(lax.empty((N_DEV,) + ada_shard.shape, BF16), ada_shard[None], (dev, 0, 0))

    (near_send, near_recv, near_bufs, near_routes), gin_token = _exchange_start(
        [in_shard[None]], [groups], _group_routes((1, 2)), "gather_in_start", after=(mod_all,))
    w_in_groups, wo_states, ada_states = [], [], []
    wo_shards = [w_[0].astype(BF16) for w_ in (w_o_ret, w_o_att, w_out)]
    wo_lands = [lax.dynamic_update_slice(lax.empty((N_DEV,) + s_.shape, BF16), s_[None], (dev, 0, 0)) for s_ in wo_shards]

    def _state(send, recv, src, groups, routes):
        return send, recv, [src, groups], routes

    def proj_in(h_all):
        src, groups = near_bufs
        px = _in_proj_group(h_all, groups, 0, 1, chip, None, (gin_token,), "in_proj_0")
        (src,), (groups,) = _exchange_wait(_state(near_send, near_recv, src, groups, near_routes), px,
                                           "gather_in_wait_near")
        groups = _pair_fill(groups, (1, 2), "gather_in_fill_near")
        (far_send, far_recv, (src, groups), far_routes), far_token = _exchange_start(
            [src], [groups], _group_routes((3,)), "gather_in_start_far")
        wo_state, wo_token = _exchange_start([s_[None] for s_ in wo_shards], wo_lands, _bcast_routes(3),
                                             "gather_wo_start", after=(far_token,))
        wo_states.append(wo_state)
        ada_state, ada_token = _exchange_start([ada_shard[None]], [ada_land], _bcast_routes(1), "gather_ada_start",
                                               after=(wo_token,))
        ada_states.append(ada_state)
        px = _in_proj_group(h_all, groups, 1, 2, chip, px, (ada_token,), "in_proj_near")
        (src,), (groups,) = _exchange_wait(_state(far_send, far_recv, src, groups, far_routes), px,
                                           "gather_in_wait_far")
        groups = _pair_fill(groups, (3,), "gather_in_fill_far")
        px = _in_proj_group(h_all, groups, 3, 1, chip, px, (), "in_proj_far")
        w_in_groups.append(groups)
        return px

    def get_w_o(after):
        _, (l_ret, l_att, l_out) = _exchange_wait(wo_states[0], after, "gather_wo_wait")
        return l_ret.reshape(RH * DV, D), l_att.reshape(D, D), l_out.reshape(D, D)

    def on_out_grads(grads):
        parts = [g_.reshape(N_DEV, g_.shape[0] // N_DEV, D) for g_ in grads]
        state, token = _reduce_scatter_start(parts, core, "rs_out")
        return state, (token,)

    def on_in_grad(grad, dp_all):
        parts = [grad.reshape(N_DEV, IN_COLS // N_DEV, D)]
        lands = [lax.empty((4,) + p_.shape[1:], BF16) for p_ in parts]
        pair_state, pair_token = _exchange_start(parts, lands, _pair_routes(1), "rs_in_pair_start")
        dh = _d_h_groups(dp_all, w_in_groups[0], chip, 0, 1, None, (pair_token,))
        parts, got = _exchange_wait(pair_state, dh, "rs_in_pair_wait")
        state, token = _reduce_scatter_send(parts, got, core, "rs_in")
        n_tiles = dp_all.shape[0] // _D_H_ROWS
        return state, _d_h_groups(dp_all, w_in_groups[0], chip, 1, n_tiles - 1, dh, (token,))

    grad_x, out_state, in_state, payload = _local_step(
        x, c, ctx, norm_w, ret_log2_decay, q_norm_w, k_norm_w, loss_target,
        mod, proj_in, get_w_o, on_out_grads, on_in_grad, started=(gin_token,))

    pay_land = lax.dynamic_update_slice(lax.empty((N_DEV,) + payload.shape, F32), payload[None], (dev, 0, 0))
    pay_state, pay_token = _exchange_start([payload[None]], [pay_land], _bcast_routes(1), "gather_small_start")

    out_res = _reduce_scatter_finish(out_state, pay_token, chip,
                                     [(w_[0], m_[0], v_[0]) for w_, m_, v_ in ((w_o_ret, m_w_o_ret, v_w_o_ret),
                                                                                (w_o_att, m_w_o_att, v_w_o_att),
                                                                                (w_out, m_w_out, v_w_out))], "rs_out")
    (in_res,) = _reduce_scatter_finish(in_state, out_res[0][0], chip,
                                       [(w_in_t, jnp.transpose(m_w_in[0]), jnp.transpose(v_w_in[0]))], "rs_in")

    _, (gathered,) = _exchange_wait(pay_state, in_res[0], "gather_small_wait")
    _, (l_ada,) = _exchange_wait(ada_states[0], gathered, "gather_ada_wait")
    w_ada16 = jnp.transpose(l_ada, (1, 0, 2)).reshape(D, 3 * D)
    gb, gc, gnw, gq, gk, gr, gwa, loss = _finish_small(gathered, nb, c_ctx, ret_log2_decay, w_ada16, dev)
    big = {4: [jnp.transpose(r)[None] for r in in_res]}
    for i, res in zip((8, 9, 10), out_res):
        big[i] = [r[None] for r in res]
    small_g = {0: gc.reshape(c_ctx.shape), 1: gnw, 2: gwa[None], 3: gb, 5: gr.reshape(ret_log2_decay.shape), 6: gq, 7: gk}
    weights = [c_ctx, norm_w, w_ada, b_ada, w_in, ret_log2_decay, q_norm_w, k_norm_w, w_o_ret, w_o_att, w_out]
    ms = [m_c_ctx, m_norm_w, m_w_ada, m_b_ada, m_w_in, m_ret_log2_decay, m_q_norm_w, m_k_norm_w, m_w_o_ret, m_w_o_att, m_w_out]
    vs = [v_c_ctx, v_norm_w, v_w_ada, v_b_ada, v_w_in, v_ret_log2_decay, v_q_norm_w, v_k_norm_w, v_w_o_ret, v_w_o_att, v_w_out]
    grads, deltas, new_ms, new_vs = [], [], [], []
    for i, (w, m, v) in enumerate(zip(weights, ms, vs)):
        if i in big:
            res = big[i]
        else:
            shape2 = (-1, w.shape[-1])
            g = small_g[i]
            res = [g] + [r.reshape(w.shape) for r in _adamw(w.reshape(shape2), g.reshape(shape2), m.reshape(shape2),
                                                             v.reshape(shape2), "adamw_%d" % i)]
        for lst, r in zip((grads, deltas, new_ms, new_vs), res):
            lst.append(r)
    return (loss, grad_x, *grads, *deltas, *new_ms, *new_vs)
```

```python
import numpy as np
import jax
import jax.numpy as jnp
from jax import lax
from jax.experimental import pallas as pl
from jax.experimental.pallas import tpu as pltpu

F32 = jnp.float32
BF16 = jnp.bfloat16

D = 1024
RH, DK, DV, CH = 4, 256, 512, 256
HQ, HKV, HD = 8, 2, 128
GRID_W = 64
ROPE_THETA = 10000.0
EPS = 1e-6
RK, RV, AK, AV, RQ, RG, AQ, AG, MR, MA = 0, 1024, 3072, 3328, 3584, 4608, 6656, 7680, 8704, 9728
IN_COLS = 10752
KV_COLS = 3584
N_DEV = 8
LR, B1, B2, ADAM_EPS, WD, STEP = 0.001, 0.9, 0.999, 1e-08, 0.01, 10
PAY_ROWS = 16
VMEM_LIMIT = 56 * 1024 * 1024
_D_H_ROWS = 1536
MESH_T = pl.DeviceIdType.MESH

NT = (((1,), (1,)), ((), ()))
TN = (((0,), (0,)), ((), ()))
SM_C = (HD ** -0.5) * float(np.log2(np.e))


def _params(sem):
    return pltpu.CompilerParams(dimension_semantics=sem, vmem_limit_bytes=VMEM_LIMIT)


def _pick(n, target, mult=8):
    best = None
    for t in range(mult, min(n, target) + 1, mult):
        if n % t == 0:
            best = t
    return best or n


def _dot(a, b, dn=None):
    if dn is None:
        return jnp.dot(a, b, preferred_element_type=F32)
    return lax.dot_general(a, b, dn, preferred_element_type=F32)


def _sig(v):
    return jax.nn.sigmoid(v)


def _silu(v):
    return v * _sig(v)


def _dsilu(v):
    s = _sig(v)
    return s * (1.0 + v * (1.0 - s))


def _sds(shape, dtype):
    return jax.ShapeDtypeStruct(shape, dtype)


def _matmul(a, b, *, ta=False, tb=False, tm, tn, tk, out_dtype, name, after=()):
    m = a.shape[1] if ta else a.shape[0]
    kdim = a.shape[0] if ta else a.shape[1]
    n = b.shape[0] if tb else b.shape[1]
    tm, tn, tk = _pick(m, tm, 128), _pick(n, tn, 128), _pick(kdim, tk, 128)
    nk = kdim // tk
    dn = (((0 if ta else 1,), (1 if tb else 0,)), ((), ()))

    def body(a_ref, b_ref, *rest):
        o_ref, acc_ref = rest[-2:]
        k = pl.program_id(2)
        part = _dot(a_ref[...].astype(BF16), b_ref[...].astype(BF16), dn)
        if nk == 1:
            o_ref[...] = part.astype(o_ref.dtype)
        else:
            @pl.when(k == 0)
            def _():
                acc_ref[...] = part

            @pl.when(k > 0)
            def _():
                acc_ref[...] += part

            @pl.when(k == nk - 1)
            def _():
                o_ref[...] = acc_ref[...].astype(o_ref.dtype)

    a_spec = pl.BlockSpec((tk, tm), lambda i, j, k: (k, i)) if ta else pl.BlockSpec((tm, tk), lambda i, j, k: (i, k))
    b_spec = pl.BlockSpec((tn, tk), lambda i, j, k: (j, k)) if tb else pl.BlockSpec((tk, tn), lambda i, j, k: (k, j))
    return pl.pallas_call(
        body, name=name, grid=(m // tm, n // tn, nk),
        in_specs=[a_spec, b_spec] + [pl.BlockSpec(memory_space=pl.ANY)] * len(after),
        out_specs=pl.BlockSpec((tm, tn), lambda i, j, k: (i, j)), out_shape=_sds((m, n), out_dtype),
        scratch_shapes=[pltpu.VMEM((tm, tn) if nk > 1 else (8, 128), F32)],
        compiler_params=_params(("parallel", "parallel", "arbitrary")),
    )(a, b, *after)


def _log_gamma(r):
    rp = jnp.full((8, 128), -1.0, F32).at[:2, :RH].set(r.reshape(2, RH))

    def body(r_ref, o_ref):
        o_ref[...] = jnp.log1p(-jnp.exp2(r_ref[...]))

    out = pl.pallas_call(body, name="log_gamma", out_shape=_sds((8, 128), F32))(rp)
    return out[:2, :RH]


def _mod_part(c_rows, w_ada_loc16, b_loc):
    def body(c_ref, w_ref, b_ref, o_ref):
        o_ref[...] = _dot(_silu(c_ref[...]).astype(BF16), w_ref[...]) + b_ref[...]

    return pl.pallas_call(
        body, name="mod_part", out_shape=_sds((c_rows.shape[0], w_ada_loc16.shape[1]), F32),
    )(c_rows, w_ada_loc16, b_loc)


def _norm_fwd(x2, mod3, norm_w, rows_all, row_off, rows_per_group, group0, h_prev, tm, name, after=()):
    rows = x2.shape[0]
    rb0 = row_off // tm
    bpg = rows_per_group // tm

    def body(*refs):
        x_ref, sh_ref, sc_ref, nw_ref, o_ref = refs[-5:]
        xv = x_ref[...]
        r = lax.rsqrt(jnp.mean(xv * xv, axis=-1, keepdims=True) + EPS)
        o_ref[...] = ((xv * r) * nw_ref[...] * (1.0 + sc_ref[...]) + sh_ref[...]).astype(BF16)

    in_specs = [pl.BlockSpec((tm, D), lambda i: (i, 0)),
                pl.BlockSpec((None, 1, D), lambda i: (group0 + i // bpg, 0, 0)),
                pl.BlockSpec((None, 1, D), lambda i: (group0 + i // bpg, 0, 1)),
                pl.BlockSpec((1, D), lambda i: (0, 0))]
    in_specs = [pl.BlockSpec(memory_space=pl.ANY)] * len(after) + in_specs
    args = list(after) + [x2, mod3, mod3, norm_w]
    alias = {}
    if h_prev is not None:
        in_specs.insert(0, pl.BlockSpec(memory_space=pl.ANY))
        args.insert(0, h_prev)
        alias = {0: 0}
    return pl.pallas_call(
        body, name=name, grid=(rows // tm,), in_specs=in_specs,
        out_specs=pl.BlockSpec((tm, D), lambda i: (rb0 + i, 0)), out_shape=_sds((rows_all, D), BF16),
        input_output_aliases=alias, compiler_params=_params(("parallel",)),
    )(*args)


def _decays(lg, fwd):
    ii = lax.broadcasted_iota(jnp.int32, (CH, CH), 0)
    jj = lax.broadcasted_iota(jnp.int32, (CH, CH), 1)
    ri = lax.broadcasted_iota(jnp.int32, (CH, 1), 0).astype(F32)
    rel = (ii - jj) if fwd else (jj - ii)
    relf = jnp.maximum(rel, 0).astype(F32)
    mask = jnp.where(rel >= 0, jnp.exp(lg * relf), 0.0)
    qe = (ri + 1.0) if fwd else (CH - ri)
    ke = (CH - 1.0 - ri) if fwd else ri
    return mask, relf, jnp.exp(lg * qe), qe, jnp.exp(lg * ke), ke


def _wide_specs(rowf):
    return [pl.BlockSpec((CH, 2 * DK), lambda b, c: (rowf(b, c), RQ // (2 * DK))),
            pl.BlockSpec((CH, 2 * DK), lambda b, c: (rowf(b, c), RQ // (2 * DK) + 1)),
            pl.BlockSpec((CH, RH * DK), lambda b, c: (rowf(b, c), RK // (RH * DK))),
            pl.BlockSpec((CH, 2 * DV), lambda b, c: (rowf(b, c), RV // (2 * DV))),
            pl.BlockSpec((CH, 2 * DV), lambda b, c: (rowf(b, c), RV // (2 * DV) + 1))]


def _head_qkv(refs, h):
    q0, q1, k, v0, v1 = refs
    lo = h % 2
    q = (q0, q1)[h // 2][:, lo * DK:(lo + 1) * DK].astype(F32)
    kk = k[:, h * DK:(h + 1) * DK].astype(F32) * (DK ** -0.5)
    v16 = (v0, v1)[h // 2][:, lo * DV:(lo + 1) * DV].astype(BF16)
    return q, kk, v16


def _ctx_specs(t_rows, cx):
    rb = t_rows // cx
    return [pl.BlockSpec((cx, RH * DK), lambda b, c: (rb + b, RK // (RH * DK))),
            pl.BlockSpec((cx, 2 * DV), lambda b, c: (rb + b, RV // (2 * DV))),
            pl.BlockSpec((cx, 2 * DV), lambda b, c: (rb + b, RV // (2 * DV) + 1))]


def _ctx_kv(refs, h):
    k, v0, v1 = refs
    kk = k[:, h * DK:(h + 1) * DK].astype(F32) * (DK ** -0.5)
    lo = h % 2
    return kk, (v0, v1)[h // 2][:, lo * DV:(lo + 1) * DV].astype(BF16)


def _ret_fwd(px, lg, nb, nc, cx):
    t_rows = nb * nc * CH

    def body(lg_ref, *refs):
        ins = (refs[0:5], refs[5:10])
        ctx_refs = refs[10:13]
        of_ref, ob_ref, hf_ref, hb_ref, sf, sb = refs[13:]
        c = pl.program_id(1)

        @pl.when(c == 0)
        def _():
            pos = lax.broadcasted_iota(jnp.int32, (cx, 1), 0).astype(F32)
            for h in range(RH):
                k, v16 = _ctx_kv(ctx_refs, h)
                sf[h] = _dot((k * jnp.exp(lg_ref[0, h] * (cx - 1.0 - pos))).astype(BF16), v16, TN)
                sb[h] = _dot((k * jnp.exp(lg_ref[1, h] * pos)).astype(BF16), v16, TN)

        for d, (o_ref, h_ref, s) in enumerate(((of_ref, hf_ref, sf), (ob_ref, hb_ref, sb))):
            for h in range(RH):
                lg_d = lg_ref[d, h]
                mask, _, qd, _, kd, _ = _decays(lg_d, d == 0)
                q, k, v16 = _head_qkv(ins[d], h)
                a = _dot(q.astype(BF16), k.astype(BF16), NT)
                st = s[h]
                st16 = st.astype(BF16)
                h_ref[h] = st16
                o = _dot((a * mask).astype(BF16), v16) + _dot((q * qd).astype(BF16), st16)
                o_ref[:, h * DV:(h + 1) * DV] = o.astype(BF16)
                s[h] = st * jnp.exp(lg_d * CH) + _dot((k * kd).astype(BF16), v16, TN)

    def fw(b, c):
        return b * nc + c

    def bw(b, c):
        return b * nc + nc - 1 - c

    in_specs = [pl.BlockSpec(memory_space=pltpu.SMEM)] + _wide_specs(fw) + _wide_specs(bw) + _ctx_specs(t_rows, cx)
    out_specs = [pl.BlockSpec((CH, RH * DV), lambda b, c: (fw(b, c), 0)),
                 pl.BlockSpec((CH, RH * DV), lambda b, c: (bw(b, c), 0)),
                 pl.BlockSpec((None, None, RH, DK, DV), lambda b, c: (b, c, 0, 0, 0)),
                 pl.BlockSpec((None, None, RH, DK, DV), lambda b, c: (b, nc - 1 - c, 0, 0, 0))]
    return pl.pallas_call(
        body, name="ret_fwd", grid=(nb, nc), in_specs=in_specs, out_specs=out_specs,
        out_shape=[_sds((t_rows, RH * DV), BF16)] * 2 + [_sds((nb, nc, RH, DK, DV), BF16)] * 2,
        scratch_shapes=[pltpu.VMEM((RH, DK, DV), F32), pltpu.VMEM((RH, DK, DV), F32)],
        compiler_params=_params(("parallel", "arbitrary")),
    )(lg, *([px] * 13))


def _rope_tables(seq):
    rows = seq // GRID_W
    row = np.repeat(np.arange(rows, dtype=np.float32), GRID_W)
    col = np.tile(np.arange(GRID_W, dtype=np.float32), rows)
    half = HD // 2
    freqs = (ROPE_THETA ** (-np.arange(0, half, 2, dtype=np.float32) / half)).astype(np.float32)
    ang = np.concatenate([row[:, None] * freqs, col[:, None] * freqs], axis=-1).astype(np.float32)
    cos = np.repeat(np.cos(ang), 2, axis=-1).astype(np.float32)
    sin = np.repeat(np.sin(ang), 2, axis=-1).astype(np.float32)
    sign = np.tile(np.array([-1.0, 1.0], np.float32), HD // 2)
    return jnp.asarray(cos), jnp.asarray(sin * sign)


def _swap_pairs(v):
    lane = lax.broadcasted_iota(jnp.int32, v.shape, 1)
    return jnp.where((lane & 1) == 0, pltpu.roll(v, HD - 1, 1), pltpu.roll(v, 1, 1))


def _qk_prep(px, nw, cos, sin, rows, row_off, col_off, heads, hb, seq, tm, name):
    rope = cos is not None
    rb0 = row_off // tm
    pb = seq // tm if rope else 1
    bw = hb * HD

    def body(*refs):
        if rope:
            x_ref, w_ref, c_ref, s_ref, o_ref = refs
        else:
            x_ref, w_ref, o_ref = refs
        for h in range(hb):
            sl = slice(h * HD, (h + 1) * HD)
            xv = x_ref[:, sl].astype(F32)
            r = lax.rsqrt(jnp.mean(xv * xv, axis=-1, keepdims=True) + EPS)
            t = (xv * r) * w_ref[...]
            if rope:
                t = t * c_ref[...] + _swap_pairs(t) * s_ref[...]
            o_ref[:, sl] = t.astype(BF16)

    in_specs = [pl.BlockSpec((tm, bw), lambda i, j: (rb0 + i, col_off // bw + j)),
                pl.BlockSpec((1, HD), lambda i, j: (0, 0))]
    args = [px, nw]
    if rope:
        in_specs += [pl.BlockSpec((tm, HD), lambda i, j: (i % pb, 0))] * 2
        args += [cos, sin]
    return pl.pallas_call(
        body, name=name, grid=(rows // tm, heads // hb), in_specs=in_specs,
        out_specs=pl.BlockSpec((tm, bw), lambda i, j: (i, j)), out_shape=_sds((rows, heads * HD), BF16),
        compiler_params=_params(("parallel", "parallel")),
    )(*args)


def _att_fwd(q16, kx16, kc16, px, nb, seq, cx, tq):
    t_rows = nb * seq
    nq = seq // tq
    rep = HQ // HKV
    gw = rep * HD

    def body(q_ref, kx_ref, kc_ref, vx_ref, vc_ref, g_ref, o_ref, y_ref, l_ref):
        kx = kx_ref[...]
        kc = kc_ref[...]
        vx = vx_ref[...].astype(BF16)
        vc = vc_ref[...].astype(BF16)
        l_ref[...] = jnp.zeros_like(l_ref)
        for r in range(rep):
            sl = slice(r * HD, (r + 1) * HD)
            q = q_ref[:, sl]
            s1 = _dot(q, kx, NT)
            s2 = _dot(q, kc, NT)
            m = jnp.maximum(jnp.max(s1, axis=-1, keepdims=True), jnp.max(s2, axis=-1, keepdims=True))
            e1 = jnp.exp2((s1 - m) * SM_C)
            e2 = jnp.exp2((s2 - m) * SM_C)
            tot = jnp.sum(e1, axis=-1, keepdims=True) + jnp.sum(e2, axis=-1, keepdims=True)
            o = (_dot(e1.astype(BF16), vx) + _dot(e2.astype(BF16), vc)) * (1.0 / tot)
            o_ref[:, sl] = o
            y_ref[:, sl] = (o * _silu(g_ref[:, sl].astype(F32))).astype(BF16)
            l_ref[:, r:r + 1] = m * SM_C + jnp.log(tot) * float(np.log2(np.e))

    qblk = pl.BlockSpec((tq, gw), lambda b, g, i: (b * nq + i, g))
    return pl.pallas_call(
        body, name="att_fwd", grid=(nb, HKV, nq),
        in_specs=[qblk,
                  pl.BlockSpec((seq, HD), lambda b, g, i: (b, g)),
                  pl.BlockSpec((cx, HD), lambda b, g, i: (b, g)),
                  pl.BlockSpec((seq, HD), lambda b, g, i: (b, AV // HD + g)),
                  pl.BlockSpec((cx, HD), lambda b, g, i: (t_rows // cx + b, AV // HD + g)),
                  pl.BlockSpec((tq, gw), lambda b, g, i: (b * nq + i, AG // gw + g))],
        out_specs=[qblk, qblk, pl.BlockSpec((tq, 128), lambda b, g, i: (b * nq + i, g))],
        out_shape=[_sds((t_rows, D), F32), _sds((t_rows, D), BF16), _sds((t_rows, HKV * 128), F32)],
        compiler_params=_params(("parallel", "parallel", "parallel")),
    )(q16, kx16, kc16, px, px, px)


def _gate_specs(tm, col0):
    hw = D // 2
    return [pl.BlockSpec((tm, hw), lambda i: (i, col0 // hw)), pl.BlockSpec((tm, hw), lambda i: (i, col0 // hw + 1))]


def _merge_out(o_f, o_b, yatt16, px, w_o_ret16, w_o_att16, w_out16, x2, tgt, mod3, nb, seq, tm):
    t_rows = nb * seq
    bpb = seq // tm
    hw = D // 2

    def body(of_ref, ob_ref, g0, g1, g2, g3, wr_ref, ya_ref, wa_ref, mr0, mr1, ma0, ma1, wo_ref, x_ref, t_ref, gt_ref,
             yr_ref, ar_ref, aa_ref, y_ref, dxn_ref, dout_ref, dg_ref, loss_ref):
        i = pl.program_id(1)
        for h, g_ref in enumerate((g0, g1, g2, g3)):
            sl = slice(h * DV, (h + 1) * DV)
            o = of_ref[:, sl].astype(F32) + ob_ref[:, sl].astype(F32)
            r = lax.rsqrt(jnp.mean(o * o, axis=-1, keepdims=True) + EPS)
            yr_ref[:, sl] = ((o * r) * _silu(g_ref[...].astype(F32))).astype(BF16)
        ar = _dot(yr_ref[...], wr_ref[...])
        aa = _dot(ya_ref[...], wa_ref[...])
        ar_ref[...] = ar.astype(BF16)
        aa_ref[...] = aa.astype(BF16)
        for j, (mr_ref, ma_ref) in enumerate(((mr0, ma0), (mr1, ma1))):
            sl = slice(j * hw, (j + 1) * hw)
            y_ref[:, sl] = (_sig(mr_ref[...].astype(F32)) * ar[:, sl]
                            + _sig(ma_ref[...].astype(F32)) * aa[:, sl]).astype(BF16)
        out = _dot(y_ref[...], wo_ref[...])
        gate = gt_ref[...]
        diff = x_ref[...] + gate * out - t_ref[...]
        dxn = diff * (1.0 / D)
        dxn_ref[...] = dxn
        dout_ref[...] = (gate * dxn).astype(BF16)
        dg = jnp.sum(dxn * out, axis=0, keepdims=True)
        ls = jnp.broadcast_to(jnp.sum(diff * diff) * (0.5 / D), (1, 128))

        @pl.when(i == 0)
        def _():
            dg_ref[...] = dg
            loss_ref[...] = ls

        @pl.when(i > 0)
        def _():
            dg_ref[...] += dg
            loss_ref[...] += ls

    def cols(width, col0):
        return pl.BlockSpec((tm, width), lambda b, i: (b * bpb + i, col0 // width))

    def whole(rows):
        return pl.BlockSpec((rows, D), lambda b, i: (0, 0))

    row, wide = cols(D, 0), cols(RH * DV, 0)
    gates = [cols(DV, RG + h * DV) for h in range(RH)]
    merge_gates = [cols(hw, MR), cols(hw, MR + hw), cols(hw, MA), cols(hw, MA + hw)]
    return pl.pallas_call(
        body, name="merge_out", grid=(nb, bpb),
        in_specs=[wide, wide] + gates + [whole(RH * DV), row, whole(D)] + merge_gates
        + [whole(D), row, row, pl.BlockSpec((None, 1, D), lambda b, i: (b, 0, 2))],
        out_specs=[wide, row, row, row, row, row, pl.BlockSpec((None, 1, D), lambda b, i: (b, 0, 0)),
                   pl.BlockSpec((None, 1, 128), lambda b, i: (b, 0, 0))],
        out_shape=[_sds((t_rows, RH * DV), BF16)] + [_sds((t_rows, D), BF16)] * 3
        + [_sds((t_rows, D), F32), _sds((t_rows, D), BF16), _sds((nb, 1, D), F32), _sds((nb, 1, 128), F32)],
        compiler_params=_params(("parallel", "arbitrary")),
    )(o_f, o_b, *([px] * RH), w_o_ret16, yatt16, w_o_att16, px, px, px, px, w_out16, x2, tgt, mod3)


def _bwd_branches(dout16, w_out16, w_o_ret16, w_o_att16, px, a_ret, a_att, o_f, o_b, o_att, rows_all, tm):
    t_rows = dout16.shape[0]
    hw = D // 2

    def body(do_ref, wo_ref, wr_ref, wa_ref, mr0, mr1, ma0, ma1, ar_ref, aa_ref, rg0, rg1, rg2, rg3, of_ref, ob_ref,
             ag0, ag1, oa_ref, dar_ref, daa_ref, dor_ref, dao_ref, dl_ref, dp_ref):
        dy_all = _dot(do_ref[...], wo_ref[...], NT)
        for j, (mr_ref, ma_ref) in enumerate(((mr0, ma0), (mr1, ma1))):
            sl = slice(j * hw, (j + 1) * hw)
            dy = dy_all[:, sl]
            sr = _sig(mr_ref[...].astype(F32))
            sa = _sig(ma_ref[...].astype(F32))
            dar_ref[:, sl] = (dy * sr).astype(BF16)
            daa_ref[:, sl] = (dy * sa).astype(BF16)
            dp_ref[:, MR - RG + j * hw:MR - RG + (j + 1) * hw] = (
                dy * ar_ref[:, sl].astype(F32) * sr * (1.0 - sr)).astype(BF16)
            dp_ref[:, MA - RG + j * hw:MA - RG + (j + 1) * hw] = (
                dy * aa_ref[:, sl].astype(F32) * sa * (1.0 - sa)).astype(BF16)
        da_ret = dar_ref[...]
        for h, g_ref in enumerate((rg0, rg1, rg2, rg3)):
            sl = slice(h * DV, (h + 1) * DV)
            dy = _dot(da_ret, wr_ref[sl, :], NT)
            g = g_ref[...].astype(F32)
            o = of_ref[:, sl].astype(F32) + ob_ref[:, sl].astype(F32)
            r = lax.rsqrt(jnp.mean(o * o, axis=-1, keepdims=True) + EPS)
            on = o * r
            sg = _sig(g)
            don = dy * (g * sg)
            dp_ref[:, sl] = (dy * on * (sg * (1.0 + g * (1.0 - sg)))).astype(BF16)
            dor_ref[:, sl] = (r * (don - on * jnp.mean(on * don, axis=-1, keepdims=True))).astype(BF16)
        dy_all = _dot(daa_ref[...], wa_ref[...], NT)
        dl_ref[...] = jnp.zeros_like(dl_ref)
        for j, g_ref in enumerate((ag0, ag1)):
            sl = slice(j * hw, (j + 1) * hw)
            dy = dy_all[:, sl]
            g = g_ref[...].astype(F32)
            sg = _sig(g)
            dao = dy * (g * sg)
            dao_ref[:, sl] = dao.astype(BF16)
            prod = dao * oa_ref[:, sl]
            for r in range(hw // HD):
                dl_ref[:, j * 128 + r:j * 128 + r + 1] = jnp.sum(prod[:, r * HD:(r + 1) * HD], axis=-1, keepdims=True)
            dp_ref[:, AG - RG + j * hw:AG - RG + (j + 1) * hw] = (
                dy * oa_ref[:, sl] * (sg * (1.0 + g * (1.0 - sg)))).astype(BF16)

    def gate(h):
        return pl.BlockSpec((tm, DV), lambda i: (i, RG // DV + h))

    def whole(rows):
        return pl.BlockSpec((rows, D), lambda i: (0, 0))

    row = pl.BlockSpec((tm, D), lambda i: (i, 0))
    wide = pl.BlockSpec((tm, RH * DV), lambda i: (i, 0))
    return pl.pallas_call(
        body, name="bwd_branches", grid=(t_rows // tm,),
        in_specs=[row, whole(D), whole(RH * DV), whole(D)] + _gate_specs(tm, MR) + _gate_specs(tm, MA) + [row, row]
        + [gate(h) for h in range(RH)] + [wide, wide] + _gate_specs(tm, AG) + [row],
        out_specs=[row, row, wide, row, pl.BlockSpec((tm, HKV * 128), lambda i: (i, 0)),
                   pl.BlockSpec((pl.Element(tm), pl.Element(IN_COLS - RG)), lambda i: (i * tm, RG))],
        out_shape=[_sds((t_rows, D), BF16)] * 2 + [_sds((t_rows, RH * DV), BF16), _sds((t_rows, D), BF16),
                                                  _sds((t_rows, HKV * 128), F32), _sds((rows_all, IN_COLS), BF16)],
        compiler_params=_params(("parallel",)),
    )(dout16, w_out16, w_o_ret16, w_o_att16, px, px, px, px, a_ret, a_att, *([px] * RH), o_f, o_b, px, px, o_att)


def _att_bwd(q16, kx16, kc16, px, dao16, delta, lse, q_norm_w, cos, sin, dp_all, nb, seq, cx, tq, after=()):
    t_rows = nb * seq
    nq = seq // tq
    rep = HQ // HKV
    gw = rep * HD
    scale = HD ** -0.5

    def body(q_ref, kx_ref, kc_ref, vx_ref, vc_ref, dao_ref, dl_ref, l_ref, xq_ref, w_ref, c_ref, s_ref, *rest):
        daq_ref, gq_ref, dkx_ref, dvx_ref, dkc_ref, dvc_ref = rest[1 + len(after):7 + len(after)]
        accs = rest[7 + len(after):]
        i = pl.program_id(2)
        first = jnp.logical_and(jnp.logical_and(pl.program_id(0) == 0, pl.program_id(1) == 0), i == 0)
        gq = jnp.zeros((1, HD), F32)
        kx = kx_ref[...]
        kc = kc_ref[...]
        vx = vx_ref[...].astype(BF16)
        vc = vc_ref[...].astype(BF16)
        @pl.when(i == 0)
        def _():
            for acc in accs:
                acc[...] = jnp.zeros_like(acc)

        dkx, dvx, dkc, dvc = [acc[...] for acc in accs]
        for r in range(rep):
            sl = slice(r * HD, (r + 1) * HD)
            q = q_ref[:, sl]
            lr = l_ref[:, r:r + 1]
            p1 = jnp.exp2(_dot(q, kx, NT) * SM_C - lr)
            p2 = jnp.exp2(_dot(q, kc, NT) * SM_C - lr)
            da16 = dao_ref[:, sl]
            delta = dl_ref[:, r:r + 1]
            ds1 = (p1 * (_dot(da16, vx, NT) - delta)).astype(BF16)
            ds2 = (p2 * (_dot(da16, vc, NT) - delta)).astype(BF16)
            dq = (_dot(ds1, kx) + _dot(ds2, kc)) * scale
            dkx += _dot(q, ds1, TN)
            dkc += _dot(q, ds2, TN)
            dvx += _dot(da16, p1.astype(BF16), TN)
            dvc += _dot(da16, p2.astype(BF16), TN)
            dt = dq * c_ref[...] + _swap_pairs(dq * s_ref[...])
            xv = xq_ref[:, sl].astype(F32)
            rn = lax.rsqrt(jnp.mean(xv * xv, axis=-1, keepdims=True) + EPS)
            xh = xv * rn
            dxh = dt * w_ref[...]
            daq_ref[:, sl] = (rn * (dxh - xh * jnp.mean(dxh * xh, axis=-1, keepdims=True))).astype(BF16)
            gq += jnp.sum(dt * xh, axis=0, keepdims=True)
        for acc, val in zip(accs, (dkx, dvx, dkc, dvc)):
            acc[...] = val

        @pl.when(first)
        def _():
            gq_ref[...] = gq

        @pl.when(jnp.logical_not(first))
        def _():
            gq_ref[...] += gq

        @pl.when(i == nq - 1)
        def _():
            dkx_ref[...] = dkx.T * scale
            dvx_ref[...] = dvx.T
            dkc_ref[...] = dkc.T * scale
            dvc_ref[...] = dvc.T

    qblk = pl.BlockSpec((tq, gw), lambda b, g, i: (b * nq + i, g))
    kxb = pl.BlockSpec((None, seq, HD), lambda b, g, i: (b, 0, g))
    kcb = pl.BlockSpec((None, cx, HD), lambda b, g, i: (b, 0, g))
    table = pl.BlockSpec((tq, HD), lambda b, g, i: (i, 0))
    one = pl.BlockSpec((1, HD), lambda b, g, i: (0, 0))
    lane = pl.BlockSpec((tq, 128), lambda b, g, i: (b * nq + i, g))
    return pl.pallas_call(
        body, name="att_bwd", grid=(nb, HKV, nq),
        in_specs=[qblk,
                  pl.BlockSpec((seq, HD), lambda b, g, i: (b, g)),
                  pl.BlockSpec((cx, HD), lambda b, g, i: (b, g)),
                  pl.BlockSpec((seq, HD), lambda b, g, i: (b, AV // HD + g)),
                  pl.BlockSpec((cx, HD), lambda b, g, i: (t_rows // cx + b, AV // HD + g)),
                  qblk, lane, lane,
                  pl.BlockSpec((tq, gw), lambda b, g, i: (b * nq + i, AQ // gw + g)), one, table, table]
        + [pl.BlockSpec(memory_space=pl.ANY)] * (1 + len(after)),
        out_specs=[pl.BlockSpec((tq, gw), lambda b, g, i: (b * nq + i, AQ // gw + g)), one, kxb, kxb, kcb, kcb],
        out_shape=[_sds(dp_all.shape, BF16), _sds((1, HD), F32), _sds((nb, seq, HKV * HD), F32),
                   _sds((nb, seq, HKV * HD), F32), _sds((nb, cx, HKV * HD), F32), _sds((nb, cx, HKV * HD), F32)],
        scratch_shapes=[pltpu.VMEM((HD, seq), F32), pltpu.VMEM((HD, seq), F32), pltpu.VMEM((HD, cx), F32),
                        pltpu.VMEM((HD, cx), F32)],
        input_output_aliases={12: 0},
        compiler_params=_params(("arbitrary", "arbitrary", "arbitrary")),
    )(q16, kx16, kc16, px, px, dao16, delta, lse, px, q_norm_w, cos, sin, dp_all, *after)


def _qk_prep_bwd(dt, px, nw, cos, sin, rows, row_off, col_off, heads, hb, seq, tm, name):
    rope = cos is not None
    rb0 = row_off // tm
    pb = seq // tm if rope else 1
    bw = hb * HD

    def body(*refs):
        if rope:
            d_ref, x_ref, w_ref, c_ref, s_ref, dx_ref, dw_ref = refs
        else:
            d_ref, x_ref, w_ref, dx_ref, dw_ref = refs
        first = jnp.logical_and(pl.program_id(0) == 0, pl.program_id(1) == 0)
        dw = jnp.zeros((1, HD), F32)
        for h in range(hb):
            sl = slice(h * HD, (h + 1) * HD)
            dtv = d_ref[:, sl]
            if rope:
                dtv = dtv * c_ref[...] + _swap_pairs(dtv * s_ref[...])
            xv = x_ref[:, sl].astype(F32)
            r = lax.rsqrt(jnp.mean(xv * xv, axis=-1, keepdims=True) + EPS)
            xh = xv * r
            dxh = dtv * w_ref[...]
            dx_ref[:, sl] = (r * (dxh - xh * jnp.mean(dxh * xh, axis=-1, keepdims=True))).astype(BF16)
            dw += jnp.sum(dtv * xh, axis=0, keepdims=True)

        @pl.when(first)
        def _():
            dw_ref[...] = dw

        @pl.when(jnp.logical_not(first))
        def _():
            dw_ref[...] += dw

    blk = pl.BlockSpec((tm, bw), lambda i, j: (i, j))
    in_specs = [blk, pl.BlockSpec((tm, bw), lambda i, j: (rb0 + i, col_off // bw + j)),
                pl.BlockSpec((1, HD), lambda i, j: (0, 0))]
    args = [dt, px, nw]
    if rope:
        in_specs += [pl.BlockSpec((tm, HD), lambda i, j: (i % pb, 0))] * 2
        args += [cos, sin]
    return pl.pallas_call(
        body, name=name, grid=(rows // tm, heads // hb), in_specs=in_specs,
        out_specs=[blk, pl.BlockSpec((1, HD), lambda i, j: (0, 0))],
        out_shape=[_sds((rows, heads * HD), BF16), _sds((1, HD), F32)],
        compiler_params=_params(("arbitrary", "arbitrary")),
    )(*args)


def _ret_bwd(px, lg, do16, hist_f, hist_b, nb, nc, cx):
    t_rows = nb * nc * CH

    def body(lg_ref, *refs):
        ins = (refs[0:5], refs[7:12])
        do_refs = (refs[5], refs[12])
        h_refs = (refs[6], refs[13])
        ctx_refs = refs[14:17]
        outs = (refs[17:20], refs[20:23])
        dck_ref, dcv_ref, dlg_ref = refs[23:26]
        dss = (refs[26], refs[27])
        c = pl.program_id(1)

        @pl.when(c == 0)
        def _():
            dss[0][...] = jnp.zeros_like(dss[0])
            dss[1][...] = jnp.zeros_like(dss[1])
            dlg_ref[...] = jnp.zeros_like(dlg_ref)

        for d in range(2):
            dq_ref, dk_ref, dv_ref = outs[d]
            for h in range(RH):
                lg_d = lg_ref[d, h]
                mask, relf, qd, qe, kd, ke = _decays(lg_d, d == 0)
                g_ch = jnp.exp(lg_d * CH)
                q, k, v16 = _head_qkv(ins[d], h)
                q16 = q.astype(BF16)
                k16 = k.astype(BF16)
                do16v = do_refs[d][:, h * DV:(h + 1) * DV]
                st16 = h_refs[d][h]
                dst = dss[d][h]
                dst16 = dst.astype(BF16)
                a = _dot(q16, k16, NT) * mask
                dp = _dot(do16v, v16, NT)
                da16 = (dp * mask).astype(BF16)
                dq_cross = _dot(do16v, st16, NT) * qd
                dq_ref[:, h * DK:(h + 1) * DK] = (_dot(da16, k16) + dq_cross).astype(BF16)
                dk_state = _dot(v16, dst16, NT) * kd
                dk_ref[:, h * DK:(h + 1) * DK] = ((_dot(da16, q16, TN) + dk_state) * (DK ** -0.5)).astype(BF16)
                dv = _dot(a.astype(BF16), do16v, TN) + _dot((k * kd).astype(BF16), dst16)
                dv_ref[:, h * DV:(h + 1) * DV] = dv.astype(BF16)
                dlg = (jnp.sum(relf * a * dp)
                       + jnp.sum(qe * jnp.sum(q * dq_cross, axis=-1, keepdims=True))
                       + jnp.sum(ke * jnp.sum(k * dk_state, axis=-1, keepdims=True))
                       + CH * g_ch * jnp.sum(dst * st16.astype(F32)))
                row = d * RH + h
                dlg_ref[row:row + 1, :] += jnp.broadcast_to(dlg, (1, 128))
                dss[d][h] = g_ch * dst + _dot((q * qd).astype(BF16), do16v, TN)

        @pl.when(c == nc - 1)
        def _():
            pos = lax.broadcasted_iota(jnp.int32, (cx, 1), 0).astype(F32)
            for h in range(RH):
                k, v16 = _ctx_kv(ctx_refs, h)
                dk = jnp.zeros((cx, DK), F32)
                dv = jnp.zeros((cx, DV), F32)
                for d, e in enumerate((cx - 1.0 - pos, pos)):
                    w = jnp.exp(lg_ref[d, h] * e)
                    ds16 = dss[d][h].astype(BF16)
                    t = _dot(v16, ds16, NT)
                    dk += t * w
                    dv += _dot((k * w).astype(BF16), ds16)
                    dlg = jnp.sum(e * w * jnp.sum(k * t, axis=-1, keepdims=True))
                    row = d * RH + h
                    dlg_ref[row:row + 1, :] += jnp.broadcast_to(dlg, (1, 128))
                dck_ref[:, h * DK:(h + 1) * DK] = (dk * (DK ** -0.5)).astype(BF16)
                dcv_ref[:, h * DV:(h + 1) * DV] = dv.astype(BF16)

    def fw(b, c):
        return b * nc + nc - 1 - c

    def bw(b, c):
        return b * nc + c

    def rows(rowf, width):
        return pl.BlockSpec((CH, width), lambda b, c: (rowf(b, c), 0))

    def hist(rowf):
        return pl.BlockSpec((None, None, RH, DK, DV), lambda b, c: (b, rowf(0, c), 0, 0, 0))

    in_specs = [pl.BlockSpec(memory_space=pltpu.SMEM)]
    out_specs = []
    for rowf in (fw, bw):
        in_specs += _wide_specs(rowf) + [rows(rowf, RH * DV), hist(rowf)]
        out_specs += [rows(rowf, RH * DK), rows(rowf, RH * DK), rows(rowf, RH * DV)]
    in_specs += _ctx_specs(t_rows, cx)
    out_specs += [pl.BlockSpec((cx, RH * DK), lambda b, c: (b, 0)), pl.BlockSpec((cx, RH * DV), lambda b, c: (b, 0)),
                  pl.BlockSpec((None, 8, 128), lambda b, c: (b, 0, 0))]
    qk = _sds((t_rows, RH * DK), BF16)
    vv = _sds((t_rows, RH * DV), BF16)
    return pl.pallas_call(
        body, name="ret_bwd", grid=(nb, nc), in_specs=in_specs, out_specs=out_specs,
        out_shape=[qk, qk, vv, qk, qk, vv, _sds((nb * cx, RH * DK), BF16), _sds((nb * cx, RH * DV), BF16),
                   _sds((nb, 8, 128), F32)],
        scratch_shapes=[pltpu.VMEM((RH, DK, DV), F32), pltpu.VMEM((RH, DK, DV), F32)],
        compiler_params=_params(("parallel", "arbitrary")),
    )(lg, *([px] * 5), do16, hist_f, *([px] * 5), do16, hist_b, *([px] * 3))


def _assemble_lat(dp_all, dk_f, dk_b, dv_f, dv_b, dak16, dvx, dq_f, dq_b, tm):
    t_rows = dk_f.shape[0]

    def body(_, dkf, dkb, dvf, dvb, dak, dav, dqf, dqb, o_ref):
        o_ref[:, RK:RK + RH * DK] = (dkf[...].astype(F32) + dkb[...].astype(F32)).astype(BF16)
        o_ref[:, RV:RV + RH * DV] = (dvf[...].astype(F32) + dvb[...].astype(F32)).astype(BF16)
        o_ref[:, AK:AK + HKV * HD] = dak[...]
        o_ref[:, AV:AV + HKV * HD] = dav[...].astype(BF16)
        o_ref[:, RQ:RQ + RH * DK] = (dqf[...].astype(F32) + dqb[...].astype(F32)).astype(BF16)

    args = (dk_f, dk_b, dv_f, dv_b, dak16, dvx, dq_f, dq_b)
    return pl.pallas_call(
        body, name="assemble_lat", grid=(t_rows // tm,),
        in_specs=[pl.BlockSpec(memory_space=pl.ANY)]
        + [pl.BlockSpec((tm, a.shape[1]), lambda i: (i, 0)) for a in args],
        out_specs=pl.BlockSpec((tm, RG), lambda i: (i, 0)), out_shape=_sds(dp_all.shape, BF16),
        input_output_aliases={0: 0},
        compiler_params=_params(("parallel",)),
    )(dp_all, *args)


def _assemble_ctx(dp_all, dck16, dcv16, dcak16, dvc, t_rows, tm):
    c_rows = dck16.shape[0]
    rb = t_rows // tm

    def body(_, dck, dcv, dcak, dcav, o_ref):
        o_ref[:, RK:RK + RH * DK] = dck[...]
        o_ref[:, RV:RV + RH * DV] = dcv[...]
        o_ref[:, AK:AK + HKV * HD] = dcak[...]
        o_ref[:, AV:AV + HKV * HD] = dcav[...].astype(BF16)
        o_ref[:, KV_COLS:] = jnp.zeros((tm, IN_COLS - KV_COLS), BF16)

    args = (dck16, dcv16, dcak16, dvc)
    return pl.pallas_call(
        body, name="assemble_ctx", grid=(c_rows // tm,),
        in_specs=[pl.BlockSpec(memory_space=pl.ANY)]
        + [pl.BlockSpec((tm, a.shape[1]), lambda i: (i, 0)) for a in args],
        out_specs=pl.BlockSpec((tm, IN_COLS), lambda i: (rb + i, 0)), out_shape=_sds(dp_all.shape, BF16),
        input_output_aliases={0: 0},
        compiler_params=_params(("parallel",)),
    )(dp_all, *args)


def _norm_bwd(dh, x2, mod3, norm_w, dxn, row_off, rows_per_group, group0, tm, name):
    with_dx = dxn is not None
    rows = x2.shape[0]
    rb0 = row_off // tm
    bpg = rows_per_group // tm
    ngroups = rows // rows_per_group

    def body(*refs):
        if with_dx:
            dh_ref, x_ref, sc_ref, nw_ref, dxn_ref, dx_ref, dsh_ref, dsc_ref, dnw_ref = refs
        else:
            dh_ref, x_ref, sc_ref, nw_ref, dsh_ref, dsc_ref, dnw_ref = refs
        i = pl.program_id(0)
        dhv = dh_ref[...]
        xv = x_ref[...]
        nw = nw_ref[...]
        r = lax.rsqrt(jnp.mean(xv * xv, axis=-1, keepdims=True) + EPS)
        xh = xv * r
        dm = dhv * (1.0 + sc_ref[...])
        dsh = jnp.sum(dhv, axis=0, keepdims=True)
        dsc = jnp.sum(dhv * (xh * nw), axis=0, keepdims=True)
        dnw = jnp.sum(dm * xh, axis=0, keepdims=True)
        if with_dx:
            dxh = dm * nw
            dx_ref[...] = dxn_ref[...] + r * (dxh - xh * jnp.mean(dxh * xh, axis=-1, keepdims=True))

        @pl.when(i % bpg == 0)
        def _():
            dsh_ref[...] = dsh
            dsc_ref[...] = dsc

        @pl.when(i % bpg != 0)
        def _():
            dsh_ref[...] += dsh
            dsc_ref[...] += dsc

        @pl.when(i == 0)
        def _():
            dnw_ref[...] = dnw

        @pl.when(i > 0)
        def _():
            dnw_ref[...] += dnw

    grp = pl.BlockSpec((None, 1, D), lambda i: (i // bpg, 0, 0))
    in_specs = [pl.BlockSpec((tm, D), lambda i: (rb0 + i, 0)), pl.BlockSpec((tm, D), lambda i: (i, 0)),
                pl.BlockSpec((None, 1, D), lambda i: (group0 + i // bpg, 0, 1)),
                pl.BlockSpec((1, D), lambda i: (0, 0))]
    args = [dh, x2, mod3, norm_w]
    out_specs = [grp, grp, pl.BlockSpec((1, D), lambda i: (0, 0))]
    out_shape = [_sds((ngroups, 1, D), F32), _sds((ngroups, 1, D), F32), _sds((1, D), F32)]
    if with_dx:
        in_specs.append(pl.BlockSpec((tm, D), lambda i: (i, 0)))
        args.append(dxn)
        out_specs.insert(0, pl.BlockSpec((tm, D), lambda i: (i, 0)))
        out_shape.insert(0, _sds((rows, D), F32))
    return pl.pallas_call(
        body, name=name, grid=(rows // tm,), in_specs=in_specs, out_specs=out_specs, out_shape=out_shape,
        compiler_params=_params(("arbitrary",)),
    )(*args)


def _small_final(dmod_all, dmodc_parts, c_rows, dm_loc_rows, nw_parts, misc_parts, c_ctx, r_pad, w_ada16):
    loc = dm_loc_rows.shape[1]

    def body(dm_ref, dmc_ref, c_ref, dml_ref, nwp_ref, mp_ref, cc_ref, r_ref, w_ref,
             gb_ref, gc_ref, gnw_ref, misc_ref, gwa_ref):
        dmc = jnp.sum(dmc_ref[...], axis=0, keepdims=True)
        gb_ref[...] = jnp.sum(dm_ref[...], axis=0, keepdims=True) + dmc
        dsc = _dot(jnp.broadcast_to(dmc, (8, 3 * D)).astype(BF16), w_ref[...], NT)[0:1, :]
        gc_ref[...] = dsc * _dsilu(cc_ref[...])
        gnw_ref[...] = jnp.sum(nwp_ref[...], axis=0, keepdims=True)
        misc = jnp.sum(mp_ref[...], axis=0, keepdims=True)
        y = jnp.exp2(r_ref[...])
        lane = lax.broadcasted_iota(jnp.int32, (1, D), 1)
        is_decay = jnp.logical_and(lane >= 2 * HD, lane < 2 * HD + 2 * RH)
        misc_ref[...] = misc * jnp.where(is_decay, -(y * np.float32(np.log(2.0))) / (1.0 - y), 1.0)
        gwa_ref[...] = _dot(_silu(c_ref[...]).astype(BF16), dml_ref[...].astype(BF16), TN)

    return pl.pallas_call(
        body, name="small_final",
        out_shape=[_sds((1, 3 * D), F32), _sds((1, D), F32), _sds((1, D), F32), _sds((1, D), F32), _sds((D, loc), F32)],
        compiler_params=pltpu.CompilerParams(vmem_limit_bytes=VMEM_LIMIT),
    )(dmod_all, dmodc_parts, c_rows, dm_loc_rows, nw_parts, misc_parts, c_ctx, r_pad, w_ada16)


def _adamw_math(w, g, m, v):
    nm = B1 * m + (1.0 - B1) * g
    nv = B2 * v + (1.0 - B2) * (g * g)
    return -LR * ((nm / (1.0 - B1 ** STEP)) / (jnp.sqrt(nv / (1.0 - B2 ** STEP)) + ADAM_EPS) + WD * w), nm, nv


def _adamw(w, g, m, v, name):
    rows, cols = w.shape
    tm = _pick(rows, 448, 8)

    def body(w_ref, g_ref, m_ref, v_ref, d_ref, nm_ref, nv_ref):
        d_ref[...], nm_ref[...], nv_ref[...] = _adamw_math(w_ref[...], g_ref[...], m_ref[...], v_ref[...])

    blk = pl.BlockSpec((tm, cols), lambda i: (i, 0))
    return pl.pallas_call(
        body, name=name, grid=(rows // tm,), in_specs=[blk] * 4, out_specs=[blk] * 3,
        out_shape=[_sds((rows, cols), F32)] * 3, compiler_params=_params(("parallel",)),
    )(w, g, m, v)


def _adamw_small(wgmv):
    n = len(wgmv)

    def body(*refs):
        ins, outs = refs[:4 * n], refs[4 * n:]
        for k in range(n):
            w, g, m, v = [r[...] for r in ins[4 * k:4 * k + 4]]
            outs[3 * k][...], outs[3 * k + 1][...], outs[3 * k + 2][...] = _adamw_math(w, g, m, v)

    out = pl.pallas_call(
        body, name="adamw_small", out_shape=[_sds(t[0].shape, F32) for t in wgmv for _ in range(3)],
    )(*[a for t in wgmv for a in t])
    return [out[3 * k:3 * k + 3] for k in range(n)]


def _mesh_pos():
    return lax.axis_index("x"), lax.axis_index("y"), lax.axis_index("c")


def _all_gather(arrs, name):
    n = len(arrs)

    def body(*refs):
        ins, outs = refs[:n], refs[n:2 * n]
        send_sems, recv_sems, local_sems = refs[2 * n:]
        x, y, c = _mesh_pos()
        me, sib = (x, y, c), (x, y, 1 - c)
        chips = [(1 - x, y), (x, 1 - y), (1 - x, 1 - y)]

        def slot(p):
            return 4 * p[0] + 2 * p[1] + p[2]

        def copy(a, k, block, to, own):
            dst = outs[a].at[slot(block)]
            return pltpu.make_async_remote_copy(
                src_ref=ins[a] if own else dst, dst_ref=dst, send_sem=send_sems.at[a, k], recv_sem=recv_sems.at[a, k],
                device_id=to, device_id_type=MESH_T)

        mine = [pltpu.make_async_copy(ins[a], outs[a].at[slot(me)], local_sems.at[a]) for a in range(n)]
        for cp in mine:
            cp.start()
        first = []
        for a in range(n):
            first.append(copy(a, 0, me, sib, True))
            first += [copy(a, 1 + j, me, (*chip, c), True) for j, chip in enumerate(chips)]
        for cp in first:
            cp.start()
        passed = []
        for j, chip in enumerate(chips):
            for a in range(n):
                copy(a, 1 + j, (*chip, c), me, False).wait_recv()
                fwd = copy(a, 4 + j, (*chip, c), sib, False)
                fwd.start()
                passed.append(fwd)
        for a in range(n):
            copy(a, 0, sib, me, False).wait_recv()
            for j, chip in enumerate(chips):
                copy(a, 4 + j, (*chip, 1 - c), me, False).wait_recv()
        for cp in first + passed:
            cp.wait_send()
        for cp in mine:
            cp.wait()

    hbm = pl.BlockSpec(memory_space=pl.ANY)
    return pl.pallas_call(
        body, name=name, in_specs=[hbm] * n, out_specs=[hbm] * n,
        out_shape=[_sds((N_DEV,) + a.shape, a.dtype) for a in arrs],
        scratch_shapes=[pltpu.SemaphoreType.DMA((n, 7)), pltpu.SemaphoreType.DMA((n, 7)), pltpu.SemaphoreType.DMA((n,))],
    )(*arrs)


def _pair_exchange(arrs, name):
    n = len(arrs)

    def body(*refs):
        ins, outs = refs[:n], refs[n:2 * n]
        send_sems, recv_sems = refs[2 * n:]
        x, y, c = _mesh_pos()
        sib = (x, y, 1 - c)
        sends = []
        for a in range(n):
            for k in range(4):
                sends.append(pltpu.make_async_remote_copy(
                    src_ref=ins[a].at[2 * k + 1 - c], dst_ref=outs[a].at[k], send_sem=send_sems.at[a, k],
                    recv_sem=recv_sems.at[a, k], device_id=sib, device_id_type=MESH_T))
        for cp in sends:
            cp.start()
        for cp in sends:
            cp.wait_recv()
        for cp in sends:
            cp.wait_send()

    hbm = pl.BlockSpec(memory_space=pl.ANY)
    return pl.pallas_call(
        body, name=name, in_specs=[hbm] * n, out_specs=[hbm] * n,
        out_shape=[_sds((4,) + a.shape[1:], a.dtype) for a in arrs],
        scratch_shapes=[pltpu.SemaphoreType.DMA((n, 4)), pltpu.SemaphoreType.DMA((n, 4))],
    )(*arrs)


def _pair_add(parts, gots, core, name):
    n = len(parts)
    cols = parts[0].shape[2]
    tiles = min(p.shape[1] for p in parts) // _pick(min(p.shape[1] for p in parts), 672, 16)

    def body(core_ref, *refs):
        for p_ref, g_ref, o_ref in zip(refs[:n], refs[n:2 * n], refs[2 * n:]):
            o_ref[...] = (p_ref[...].astype(F32) + g_ref[...].astype(F32)).astype(BF16)

    def blk(p):
        return pl.BlockSpec((None, p.shape[1] // tiles, cols), lambda k, i, cr: (k, i, 0))

    return pl.pallas_call(
        body, name=name,
        grid_spec=pltpu.PrefetchScalarGridSpec(
            num_scalar_prefetch=1, grid=(4, tiles),
            in_specs=[pl.BlockSpec((None, None, p.shape[1] // tiles, cols), lambda k, i, cr: (k, cr[0], i, 0))
                      for p in parts] + [blk(p) for p in parts],
            out_specs=[blk(p) for p in parts]),
        out_shape=[_sds((4,) + p.shape[1:], BF16) for p in parts], compiler_params=_params(("parallel", "parallel")),
    )(core, *[p.reshape(4, 2, *p.shape[1:]) for p in parts], *gots)


def _chip_sum_adamw(pair_sums, landed, chip, w, m, v, name):
    _, rows, cols = pair_sums.shape
    tm = _pick(rows, 448, 16)

    def body(chip_ref, s_ref, l_ref, w_ref, m_ref, v_ref, g_ref, d_ref, nm_ref, nv_ref):
        acc = s_ref[...].astype(F32)
        for j in range(3):
            acc = acc + l_ref[j].astype(F32)
        g_ref[...] = acc
        d_ref[...], nm_ref[...], nv_ref[...] = _adamw_math(w_ref[...], acc, m_ref[...], v_ref[...])

    blk = pl.BlockSpec((tm, cols), lambda i, ch: (i, 0))
    return pl.pallas_call(
        body, name=name,
        grid_spec=pltpu.PrefetchScalarGridSpec(
            num_scalar_prefetch=1, grid=(rows // tm,),
            in_specs=[pl.BlockSpec((None, tm, cols), lambda i, ch: (ch[0], i, 0)),
                      pl.BlockSpec((3, tm, cols), lambda i, ch: (0, i, 0)), blk, blk, blk],
            out_specs=[blk] * 4),
        out_shape=[_sds((rows, cols), F32)] * 4, compiler_params=_params(("parallel",)),
    )(chip, pair_sums, landed, w, m, v)


_HBM = pl.BlockSpec(memory_space=pltpu.HBM)
_SEM = pl.BlockSpec(memory_space=pltpu.SEMAPHORE)
_EFFECT = pltpu.SideEffectType.DATAFLOW_SIDE_EFFECTING


def _chip_routes(n):
    def plan(x, y, c):
        routes = []
        for a in range(n):
            for j in range(1, 4):
                px, py = x ^ (j >> 1), y ^ (j & 1)
                routes.append((a, 2 * px + py, (px, py, c), j - 1))
        return routes
    return plan, 3 * n


def _pair_routes(n):
    def plan(x, y, c):
        return [(a, 2 * k + 1 - c, (x, y, 1 - c), k) for a in range(n) for k in range(4)]
    return plan, 4 * n


def _bcast_routes(n):
    def plan(x, y, c):
        routes = []
        for a in range(n):
            for k in range(1, N_DEV):
                peer = (x ^ ((k >> 2) & 1), y ^ ((k >> 1) & 1), c ^ (k & 1))
                routes.append((a, 0, peer, 4 * x + 2 * y + c))
        return routes
    return plan, 7 * n


def _route_copies(srcs, lands, send_sems, recv_sems, routes):
    return [pltpu.make_async_remote_copy(
        src_ref=srcs[a].at[sb], dst_ref=lands[a].at[lb], send_sem=send_sems.at[r], recv_sem=recv_sems.at[r],
        device_id=peer, device_id_type=MESH_T) for r, (a, sb, peer, lb) in enumerate(routes)]


def _exchange_start(srcs, lands, routes, name, after=()):
    plan, count = routes
    n = len(srcs)
    n_in = 2 * n + len(after)

    def body(*refs):
        send_sems, recv_sems = refs[n_in], refs[n_in + 1]
        token = refs[-1]
        for cp in _route_copies(refs[:n], refs[n:2 * n], send_sems, recv_sems, plan(*_mesh_pos())):
            cp.start()
        token[...] = jnp.zeros_like(token)

    args = [pltpu.with_memory_space_constraint(a, pltpu.HBM) for a in list(srcs) + list(lands)]
    out = pl.pallas_call(
        body, name=name,
        out_shape=(pltpu.SemaphoreType.DMA((count,)), pltpu.SemaphoreType.DMA((count,)),
                   *[pltpu.HBM(a.shape, a.dtype) for a in args], _sds((8, 128), F32)),
        in_specs=[_HBM] * (2 * n) + [pl.BlockSpec(memory_space=pl.ANY)] * len(after),
        out_specs=(_SEM, _SEM, *([_HBM] * (2 * n)), pl.BlockSpec(memory_space=pltpu.VMEM)),
        input_output_aliases={i: 2 + i for i in range(2 * n)},
        compiler_params=pltpu.CompilerParams(has_side_effects=_EFFECT),
    )(*args, *after)
    return (out[0], out[1], list(out[2:2 + 2 * n]), routes), out[-1]


def _exchange_wait(state, after, name):
    send_sems, recv_sems, bufs, (plan, count) = state
    n = len(bufs) // 2

    def body(*refs):
        send_s, recv_s = refs[2 * n], refs[2 * n + 1]
        for cp in _route_copies(refs[:n], refs[n:2 * n], send_s, recv_s, plan(*_mesh_pos())):
            cp.wait_send()
            cp.wait_recv()

    out = pl.pallas_call(
        body, name=name, out_shape=tuple(pltpu.HBM(a.shape, a.dtype) for a in bufs),
        in_specs=[_HBM] * (2 * n) + [_SEM, _SEM, pl.BlockSpec(memory_space=pl.ANY)], out_specs=tuple([_HBM] * (2 * n)),
        input_output_aliases={i: i for i in range(2 * n)},
        compiler_params=pltpu.CompilerParams(has_side_effects=_EFFECT),
    )(*bufs, send_sems, recv_sems, after)
    return list(out[:n]), list(out[n:])


def _group_routes(js):
    def plan(x, y, c):
        return [(0, 0, (x ^ (j >> 1), y ^ (j & 1), c), 2 * j + c) for j in js]
    return plan, len(js)


def _pair_fill(groups, js, name, after=()):
    def body(*refs):
        g_ref, send_sems, recv_sems = refs[-3:]
        x, y, c = _mesh_pos()
        sends = []
        for n, j in enumerate(js):
            mine = g_ref.at[2 * j + c]
            sends.append(pltpu.make_async_remote_copy(
                src_ref=mine, dst_ref=mine, send_sem=send_sems.at[n], recv_sem=recv_sems.at[n],
                device_id=(x, y, 1 - c), device_id_type=MESH_T))
        for cp in sends:
            cp.start()
        for n, j in enumerate(js):
            pltpu.make_async_remote_copy(
                src_ref=g_ref.at[2 * j + c], dst_ref=g_ref.at[2 * j + 1 - c], send_sem=send_sems.at[n],
                recv_sem=recv_sems.at[n], device_id=(x, y, 1 - c), device_id_type=MESH_T).wait_recv()
        for cp in sends:
            cp.wait_send()

    hbm = pl.BlockSpec(memory_space=pl.ANY)
    return pl.pallas_call(
        body, name=name, in_specs=[hbm] * (1 + len(after)), out_specs=hbm, out_shape=_sds(groups.shape, groups.dtype),
        input_output_aliases={0: 0},
        scratch_shapes=[pltpu.SemaphoreType.DMA((len(js),)), pltpu.SemaphoreType.DMA((len(js),))],
    )(groups, *after)


def _in_proj_group(h_all, groups, j0, ng, chip, px_prev, after, name):
    rows_all = h_all.shape[0]
    gcols = IN_COLS // 4
    tm = _pick(rows_all, 1536, 128)
    g4 = groups.reshape(4, gcols, D)

    n_lead = (1 if px_prev is not None else 0) + len(after)
    lead = ([px_prev] if px_prev is not None else []) + list(after)

    def body(chip_ref, *refs):
        h_ref, w_ref, o_ref = refs[n_lead:]
        o_ref[...] = _dot(h_ref[...], w_ref[...], NT).astype(BF16)

    return pl.pallas_call(
        body, name=name,
        grid_spec=pltpu.PrefetchScalarGridSpec(
            num_scalar_prefetch=1, grid=(ng, rows_all // tm),
            in_specs=[pl.BlockSpec(memory_space=pl.ANY)] * n_lead
            + [pl.BlockSpec((tm, D), lambda n, i, ch: (i, 0)),
               pl.BlockSpec((None, gcols, D), lambda n, i, ch: (j0 + n, 0, 0))],
            out_specs=pl.BlockSpec((tm, gcols), lambda n, i, ch: (i, ch[0] ^ (j0 + n)))),
        out_shape=_sds((rows_all, IN_COLS), BF16),
        input_output_aliases={1: 0} if px_prev is not None else {},
        compiler_params=_params(("parallel", "parallel")),
    )(chip, *lead, h_all, g4)


def _d_h_groups(dp_all, groups, chip, i0, ni, dh_prev, after):
    rows_all = dp_all.shape[0]
    gcols = IN_COLS // 4
    tm = _D_H_ROWS
    g4 = groups.reshape(4, gcols, D)
    lead = ([dh_prev] if dh_prev is not None else []) + list(after)
    n_lead = len(lead)

    def body(chip_ref, *refs):
        a_ref, w_ref, o_ref = refs[n_lead:]
        j = pl.program_id(1)
        part = _dot(a_ref[...], w_ref[...])

        @pl.when(j == 0)
        def _():
            o_ref[...] = part

        @pl.when(j > 0)
        def _():
            o_ref[...] += part

    return pl.pallas_call(
        body, name="d_h_%d" % i0,
        grid_spec=pltpu.PrefetchScalarGridSpec(
            num_scalar_prefetch=1, grid=(ni, 4),
            in_specs=[pl.BlockSpec(memory_space=pl.ANY)] * n_lead
            + [pl.BlockSpec((tm, gcols), lambda i, j, ch: (i0 + i, ch[0] ^ j)),
               pl.BlockSpec((None, gcols, D), lambda i, j, ch: (j, 0, 0))],
            out_specs=pl.BlockSpec((tm, D), lambda i, j, ch: (i0 + i, 0))),
        out_shape=_sds((rows_all, D), F32),
        input_output_aliases={1: 0} if dh_prev is not None else {},
        compiler_params=_params(("parallel", "arbitrary")),
    )(chip, *lead, dp_all, g4)


def _reduce_scatter_start(parts, core, name):
    got = _pair_exchange(parts, name + "_pair")
    return _reduce_scatter_send(parts, got, core, name)


def _reduce_scatter_send(parts, got, core, name):
    sums = _pair_add(parts, got, core, name + "_add")
    lands = [lax.empty((3,) + s_.shape[1:], BF16) for s_ in sums]
    return _exchange_start(sums, lands, _chip_routes(len(sums)), name + "_start")


def _reduce_scatter_finish(rs_state, after, chip, wmv, name):
    sums, landed = _exchange_wait(rs_state, after, name + "_wait")
    return [_chip_sum_adamw(s_, l_, chip, *t, "%s_adamw_%d" % (name, i))
            for i, (s_, l_, t) in enumerate(zip(sums, landed, wmv))]


def _local_step(x, c, ctx, norm_w, ret_log2_decay, q_norm_w, k_norm_w, loss_target,
                mod, proj_in, get_w_o, on_out_grads, on_in_grad, started=()):
    nb, seq, _ = x.shape
    cx = ctx.shape[1]
    t_rows, c_rows = nb * seq, nb * cx
    rows_all = t_rows + c_rows
    nc = seq // CH
    tm = _pick(seq, 256, 128)
    te = _pick(seq, 512, 128)
    assert cx % tm == 0 and t_rows % cx == 0 and seq % GRID_W == 0

    x2 = x.reshape(t_rows, D)
    ctx2 = ctx.reshape(c_rows, D)
    tgt = loss_target.reshape(t_rows, D)
    lg = _log_gamma(ret_log2_decay)
    cos, sin = _rope_tables(seq)

    mod3 = mod[:, None, :]
    h_all = _norm_fwd(x2, mod3, norm_w, rows_all, 0, seq, 0, None, te, "norm_fwd", after=started)
    h_all = _norm_fwd(ctx2, mod3, norm_w, rows_all, t_rows, c_rows, nb, h_all, tm, "norm_fwd_ctx")
    px = proj_in(h_all)
    o_f, o_b, hist_f, hist_b = _ret_fwd(px, lg, nb, nc, cx)
    q16 = _qk_prep(px, q_norm_w, cos, sin, t_rows, 0, AQ, HQ, 4, seq, te, "q_prep")
    kx16 = _qk_prep(px, k_norm_w, cos, sin, t_rows, 0, AK, HKV, HKV, seq, te, "k_prep")
    kc16 = _qk_prep(px, k_norm_w, None, None, c_rows, t_rows, AK, HKV, HKV, seq, tm, "kc_prep")
    o_att, yatt16, lse = _att_fwd(q16, kx16, kc16, px, nb, seq, cx, te)
    w_o_ret16, w_o_att16, w_out16 = get_w_o(lse)
    yret16, a_ret, a_att, y16, dxn, dout16, dgate, loss_b = _merge_out(
        o_f, o_b, yatt16, px, w_o_ret16, w_o_att16, w_out16, x2, tgt, mod3, nb, seq, tm)

    gw_out = _matmul(y16, dout16, ta=True, tm=D, tn=D, tk=D, out_dtype=BF16, name="gw_out")
    da_ret16, da_att16, do16, dao16, delta, dp_all = _bwd_branches(
        dout16, w_out16, w_o_ret16, w_o_att16, px, a_ret, a_att, o_f, o_b, o_att, rows_all, tm)
    gw_o_ret = _matmul(yret16, da_ret16, ta=True, tm=D, tn=D, tk=D, out_dtype=BF16, name="gw_o_ret")
    gw_o_att = _matmul(yatt16, da_att16, ta=True, tm=D, tn=D, tk=D, out_dtype=BF16, name="gw_o_att")
    out_state, out_started = on_out_grads([gw_o_ret, gw_o_att, gw_out])
    dp_all, gq, dkx, dvx, dkc, dvc = _att_bwd(q16, kx16, kc16, px, dao16, delta, lse, q_norm_w, cos, sin, dp_all, nb,
                                              seq, cx, te, after=out_started)
    dak16, gk_lat = _qk_prep_bwd(dkx.reshape(t_rows, HKV * HD), px, k_norm_w, cos, sin, t_rows, 0, AK, HKV, HKV, seq, te,
                                 "k_prep_bwd")
    dcak16, gk_ctx = _qk_prep_bwd(dkc.reshape(c_rows, HKV * HD), px, k_norm_w, None, None, c_rows, t_rows, AK, HKV, HKV,
                                  seq, tm, "kc_prep_bwd")
    dq_f, dk_f, dv_f, dq_b, dk_b, dv_b, dck16, dcv16, dlg_scan = _ret_bwd(px, lg, do16, hist_f, hist_b, nb, nc, cx)
    dp_all = _assemble_lat(dp_all, dk_f, dk_b, dv_f, dv_b, dak16, dvx.reshape(t_rows, HKV * HD), dq_f, dq_b, tm)
    dp_all = _assemble_ctx(dp_all, dck16, dcv16, dcak16, dvc.reshape(c_rows, HKV * HD), t_rows, tm)
    gw_in_t = _matmul(dp_all, h_all, ta=True, tm=1536, tn=D, tk=2304, out_dtype=BF16, name="gw_in")
    in_state, dh = on_in_grad(gw_in_t, dp_all)
    grad_x, dsh, dsc, gnw_lat = _norm_bwd(dh, x2, mod3, norm_w, dxn, 0, seq, 0, te, "norm_bwd")
    dsh_c, dsc_c, gnw_ctx = _norm_bwd(dh, ctx2, mod3, norm_w, None, t_rows, c_rows, nb, tm, "norm_bwd_ctx")

    dlg = jnp.sum(dlg_scan[:, :, 0], axis=0).reshape(1, 2 * RH)
    misc = jnp.concatenate([gq, gk_lat + gk_ctx, dlg, jnp.sum(loss_b[:, 0, 0]).reshape(1, 1),
                            jnp.zeros((1, D - 2 * HD - 2 * RH - 1), F32)], axis=1)
    rows = []
    for b in range(nb):
        rows += [dsh[b], dsc[b], dgate[b]]
    rows += [dsh_c[0], dsc_c[0]] + [c[b:b + 1] for b in range(nb)] + [gnw_lat + gnw_ctx, misc]
    payload = jnp.concatenate(rows + [jnp.zeros((PAY_ROWS - len(rows), D), F32)], axis=0)
    return grad_x.reshape(nb, seq, D), out_state, in_state, payload


def _finish_small(gathered, nb, c_ctx, ret_log2_decay, w_ada16, dev):
    n_dev = gathered.shape[0]
    loc = 3 * D // n_dev
    dmod_all = gathered[:, :3 * nb].reshape(n_dev * nb, 3 * D)
    dmodc_parts = jnp.concatenate([gathered[:, 3 * nb:3 * nb + 2].reshape(n_dev, 2 * D), jnp.zeros((n_dev, D), F32)], axis=1)
    c_all = gathered[:, 3 * nb + 2:4 * nb + 2].reshape(n_dev * nb, D)
    nw_parts = gathered[:, 4 * nb + 2]
    misc_parts = gathered[:, 4 * nb + 3]
    n_rows = n_dev * nb + n_dev
    pad = (-n_rows) % 16
    c_rows = jnp.concatenate([c_all, jnp.broadcast_to(c_ctx.reshape(1, D), (n_dev, D)), jnp.zeros((pad, D), F32)], axis=0)
    dm_rows = jnp.concatenate([dmod_all, dmodc_parts, jnp.zeros((pad, 3 * D), F32)], axis=0)
    dm_loc_rows = lax.dynamic_slice_in_dim(dm_rows, dev * loc, loc, axis=1)
    r_pad = jnp.full((1, D), -1.0, F32).at[:, 2 * HD:2 * HD + 2 * RH].set(ret_log2_decay.reshape(1, 2 * RH))
    gb, gc, gnw, misc, gwa = _small_final(dmod_all, dmodc_parts, c_rows, dm_loc_rows, nw_parts, misc_parts,
                                          c_ctx.reshape(1, D), r_pad, w_ada16)
    return (gb, gc, gnw, misc[:, :HD], misc[:, HD:2 * HD], misc[:, 2 * HD:2 * HD + 2 * RH], gwa,
            misc[0, 2 * HD + 2 * RH])


def kernel(x, c, ctx, c_ctx, norm_w, w_ada, b_ada, w_in, ret_log2_decay, q_norm_w, k_norm_w, w_o_ret, w_o_att, w_out, loss_target, m_c_ctx, m_norm_w, m_w_ada, m_b_ada, m_w_in, m_ret_log2_decay, m_q_norm_w, m_k_norm_w, m_w_o_ret, m_w_o_att, m_w_out, v_c_ctx, v_norm_w, v_w_ada, v_b_ada, v_w_in, v_ret_log2_decay, v_q_norm_w, v_k_norm_w, v_w_o_ret, v_w_o_att, v_w_out):
    nb = x.shape[0]
    mx, my, mc = _mesh_pos()
    dev = 4 * mx + 2 * my + mc
    core = jnp.reshape(mc, (1,)).astype(jnp.int32)
    chip = jnp.reshape(2 * mx + my, (1,)).astype(jnp.int32)

    n_loc = 3 * D // N_DEV
    c8 = jnp.zeros((8, D), F32).at[:nb].set(c).at[nb].set(c_ctx)
    c_land = lax.dynamic_update_slice(lax.empty((N_DEV, 8, D), F32), c8[None], (dev, 0, 0))
    c_state, c_token = _exchange_start([c8[None]], [c_land], _bcast_routes(1), "gather_c_start")
    w_in_t = jnp.transpose(w_in[0])
    in_shard = w_in_t.astype(BF16)
    groups = lax.dynamic_update_slice(lax.empty((N_DEV,) + in_shard.shape, BF16), in_shard[None], (mc, 0, 0))
    groups = _pair_fill(groups, (0,), "gather_in_pair", after=(c_token,))
    (near_send, near_recv, near_bufs, near_routes), gin_token = _exchange_start(
        [in_shard[None]], [groups], _group_routes((1, 2)), "gather_in_start")
    _, (c_all,) = _exchange_wait(c_state, gin_token, "gather_c_wait")
    ada_shard = w_ada[0].astype(BF16)
    b_loc = lax.dynamic_slice(b_ada, (0, dev * n_loc), (1, n_loc))
    mod_cols = _mod_part(c_all.reshape(N_DEV * 8, D), ada_shard, b_loc)
    (mod_all,) = _all_gather([mod_cols], "gather_mod")
    mod = jnp.transpose(lax.dynamic_slice(mod_all, (0, dev * 8, 0), (N_DEV, 8, n_loc)), (1, 0, 2)).reshape(8, 3 * D)
    ada_land = lax.dynamic_update_slice(lax.empty((N_DEV,) + ada_shard.shape, BF16), ada_shard[None], (dev, 0, 0))

    w_in_groups, wo_states, ada_states = [], [], []
    wo_shards = [w_[0].astype(BF16) for w_ in (w_o_ret, w_o_att, w_out)]
    wo_lands = [lax.dynamic_update_slice(lax.empty((N_DEV,) + s_.shape, BF16), s_[None], (dev, 0, 0)) for s_ in wo_shards]

    def _state(send, recv, src, groups, routes):
        return send, recv, [src, groups], routes

    def proj_in(h_all):
        src, groups = near_bufs
        px = _in_proj_group(h_all, groups, 0, 1, chip, None, (gin_token,), "in_proj_0")
        (src,), (groups,) = _exchange_wait(_state(near_send, near_recv, src, groups, near_routes), px,
                                           "gather_in_wait_near")
        groups = _pair_fill(groups, (1, 2), "gather_in_fill_near")
        (far_send, far_recv, (src, groups), far_routes), far_token = _exchange_start(
            [src], [groups], _group_routes((3,)), "gather_in_start_far")
        wo_state, wo_token = _exchange_start([s_[None] for s_ in wo_shards], wo_lands, _bcast_routes(3),
                                             "gather_wo_start", after=(far_token,))
        wo_states.append(wo_state)
        ada_state, ada_token = _exchange_start([ada_shard[None]], [ada_land], _bcast_routes(1), "gather_ada_start",
                                               after=(wo_token,))
        ada_states.append(ada_state)
        px = _in_proj_group(h_all, groups, 1, 2, chip, px, (ada_token,), "in_proj_near")
        (src,), (groups,) = _exchange_wait(_state(far_send, far_recv, src, groups, far_routes), px,
                                           "gather_in_wait_far")
        groups = _pair_fill(groups, (3,), "gather_in_fill_far")
        px = _in_proj_group(h_all, groups, 3, 1, chip, px, (), "in_proj_far")
        w_in_groups.append(groups)
        return px

    def get_w_o(after):
        _, (l_ret, l_att, l_out) = _exchange_wait(wo_states[0], after, "gather_wo_wait")
        return l_ret.reshape(RH * DV, D), l_att.reshape(D, D), l_out.reshape(D, D)

    def on_out_grads(grads):
        parts = [g_.reshape(N_DEV, g_.shape[0] // N_DEV, D) for g_ in grads]
        state, token = _reduce_scatter_start(parts, core, "rs_out")
        return state, (token,)

    def on_in_grad(grad, dp_all):
        parts = [grad.reshape(N_DEV, IN_COLS // N_DEV, D)]
        lands = [lax.empty((4,) + p_.shape[1:], BF16) for p_ in parts]
        pair_state, pair_token = _exchange_start(parts, lands, _pair_routes(1), "rs_in_pair_start")
        dh = _d_h_groups(dp_all, w_in_groups[0], chip, 0, 1, None, (pair_token,))
        parts, got = _exchange_wait(pair_state, dh, "rs_in_pair_wait")
        state, token = _reduce_scatter_send(parts, got, core, "rs_in")
        n_tiles = dp_all.shape[0] // _D_H_ROWS
        return state, _d_h_groups(dp_all, w_in_groups[0], chip, 1, n_tiles - 1, dh, (token,))

    grad_x, out_state, in_state, payload = _local_step(
        x, c, ctx, norm_w, ret_log2_decay, q_norm_w, k_norm_w, loss_target,
        mod, proj_in, get_w_o, on_out_grads, on_in_grad, started=(gin_token,))

    pay_land = lax.dynamic_update_slice(lax.empty((N_DEV,) + payload.shape, F32), payload[None], (dev, 0, 0))
    pay_state, pay_token = _exchange_start([payload[None]], [pay_land], _bcast_routes(1), "gather_small_start")

    out_res = _reduce_scatter_finish(out_state, pay_token, chip,
                                     [(w_[0], m_[0], v_[0]) for w_, m_, v_ in ((w_o_ret, m_w_o_ret, v_w_o_ret),
                                                                                (w_o_att, m_w_o_att, v_w_o_att),
                                                                                (w_out, m_w_out, v_w_out))], "rs_out")
    (in_res,) = _reduce_scatter_finish(in_state, out_res[0][0], chip,
                                       [(w_in_t, jnp.transpose(m_w_in[0]), jnp.transpose(v_w_in[0]))], "rs_in")

    _, (gathered,) = _exchange_wait(pay_state, in_res[0], "gather_small_wait")
    _, (l_ada,) = _exchange_wait(ada_states[0], gathered, "gather_ada_wait")
    w_ada16 = jnp.transpose(l_ada, (1, 0, 2)).reshape(D, 3 * D)
    gb, gc, gnw, gq, gk, gr, gwa, loss = _finish_small(gathered, nb, c_ctx, ret_log2_decay, w_ada16, dev)
    big = {4: [jnp.transpose(r)[None] for r in in_res]}
    for i, res in zip((8, 9, 10), out_res):
        big[i] = [r[None] for r in res]
    small_g = {0: gc.reshape(c_ctx.shape), 1: gnw, 2: gwa[None], 3: gb, 5: gr.reshape(ret_log2_decay.shape), 6: gq, 7: gk}
    weights = [c_ctx, norm_w, w_ada, b_ada, w_in, ret_log2_decay, q_norm_w, k_norm_w, w_o_ret, w_o_att, w_out]
    ms = [m_c_ctx, m_norm_w, m_w_ada, m_b_ada, m_w_in, m_ret_log2_decay, m_q_norm_w, m_k_norm_w, m_w_o_ret, m_w_o_att, m_w_out]
    vs = [v_c_ctx, v_norm_w, v_w_ada, v_b_ada, v_w_in, v_ret_log2_decay, v_q_norm_w, v_k_norm_w, v_w_o_ret, v_w_o_att, v_w_out]
    def rows2(i):
        return [a.reshape(-1, weights[i].shape[-1]) for a in (weights[i], small_g[i], ms[i], vs[i])]

    small_ids = [i for i in small_g if i != 2]
    steps = dict(zip(small_ids, _adamw_small([rows2(i) for i in small_ids])))
    steps[2] = _adamw(*rows2(2), "adamw_w_ada")
    grads, deltas, new_ms, new_vs = [], [], [], []
    for i, w in enumerate(weights):
        res = big[i] if i in big else [small_g[i]] + [r.reshape(w.shape) for r in steps[i]]
        for lst, r in zip((grads, deltas, new_ms, new_vs), res):
            lst.append(r)
    return (loss, grad_x, *grads, *deltas, *new_ms, *new_vs)
```

```python
import numpy as np
import jax
import jax.numpy as jnp
from jax import lax
from jax.experimental import pallas as pl
from jax.experimental.pallas import tpu as pltpu

F32 = jnp.float32
BF16 = jnp.bfloat16

D = 1024
RH, DK, DV, CH = 4, 256, 512, 256
HQ, HKV, HD = 8, 2, 128
GRID_W = 64
ROPE_THETA = 10000.0
EPS = 1e-6
RK, RV, AK, AV, RQ, RG, AQ, AG, MR, MA = 0, 1024, 3072, 3328, 3584, 4608, 6656, 7680, 8704, 9728
IN_COLS = 10752
KV_COLS = 3584
N_DEV = 8
LR, B1, B2, ADAM_EPS, WD, STEP = 0.001, 0.9, 0.999, 1e-08, 0.01, 10
PAY_ROWS = 16
VMEM_LIMIT = 56 * 1024 * 1024
_D_H_ROWS = 1536
MESH_T = pl.DeviceIdType.MESH

NT = (((1,), (1,)), ((), ()))
TN = (((0,), (0,)), ((), ()))
SM_C = (HD ** -0.5) * float(np.log2(np.e))


def _params(sem):
    return pltpu.CompilerParams(dimension_semantics=sem, vmem_limit_bytes=VMEM_LIMIT)


def _pick(n, target, mult=8):
    best = None
    for t in range(mult, min(n, target) + 1, mult):
        if n % t == 0:
            best = t
    return best or n


def _dot(a, b, dn=None):
    if dn is None:
        return jnp.dot(a, b, preferred_element_type=F32)
    return lax.dot_general(a, b, dn, preferred_element_type=F32)


def _sig(v):
    return jax.nn.sigmoid(v)


def _silu(v):
    return v * _sig(v)


def _dsilu(v):
    s = _sig(v)
    return s * (1.0 + v * (1.0 - s))


def _sds(shape, dtype):
    return jax.ShapeDtypeStruct(shape, dtype)


def _matmul(a, b, *, ta=False, tb=False, tm, tn, tk, out_dtype, name, after=()):
    m = a.shape[1] if ta else a.shape[0]
    kdim = a.shape[0] if ta else a.shape[1]
    n = b.shape[0] if tb else b.shape[1]
    tm, tn, tk = _pick(m, tm, 128), _pick(n, tn, 128), _pick(kdim, tk, 128)
    nk = kdim // tk
    dn = (((0 if ta else 1,), (1 if tb else 0,)), ((), ()))

    def body(a_ref, b_ref, *rest):
        o_ref, acc_ref = rest[-2:]
        k = pl.program_id(2)
        part = _dot(a_ref[...].astype(BF16), b_ref[...].astype(BF16), dn)
        if nk == 1:
            o_ref[...] = part.astype(o_ref.dtype)
        else:
            @pl.when(k == 0)
            def _():
                acc_ref[...] = part

            @pl.when(k > 0)
            def _():
                acc_ref[...] += part

            @pl.when(k == nk - 1)
            def _():
                o_ref[...] = acc_ref[...].astype(o_ref.dtype)

    a_spec = pl.BlockSpec((tk, tm), lambda i, j, k: (k, i)) if ta else pl.BlockSpec((tm, tk), lambda i, j, k: (i, k))
    b_spec = pl.BlockSpec((tn, tk), lambda i, j, k: (j, k)) if tb else pl.BlockSpec((tk, tn), lambda i, j, k: (k, j))
    return pl.pallas_call(
        body, name=name, grid=(m // tm, n // tn, nk),
        in_specs=[a_spec, b_spec] + [pl.BlockSpec(memory_space=pl.ANY)] * len(after),
        out_specs=pl.BlockSpec((tm, tn), lambda i, j, k: (i, j)), out_shape=_sds((m, n), out_dtype),
        scratch_shapes=[pltpu.VMEM((tm, tn) if nk > 1 else (8, 128), F32)],
        compiler_params=_params(("parallel", "parallel", "arbitrary")),
    )(a, b, *after)


def _log_gamma(r):
    rp = jnp.full((8, 128), -1.0, F32).at[:2, :RH].set(r.reshape(2, RH))

    def body(r_ref, o_ref):
        o_ref[...] = jnp.log1p(-jnp.exp2(r_ref[...]))

    out = pl.pallas_call(body, name="log_gamma", out_shape=_sds((8, 128), F32))(rp)
    return out[:2, :RH]


def _mod_part(c_rows, w_ada_loc16, b_loc):
    def body(c_ref, w_ref, b_ref, o_ref):
        o_ref[...] = _dot(_silu(c_ref[...]).astype(BF16), w_ref[...]) + b_ref[...]

    return pl.pallas_call(
        body, name="mod_part", out_shape=_sds((c_rows.shape[0], w_ada_loc16.shape[1]), F32),
    )(c_rows, w_ada_loc16, b_loc)


def _norm_fwd(x2, mod3, norm_w, rows_all, row_off, rows_per_group, group0, h_prev, tm, name, after=()):
    rows = x2.shape[0]
    rb0 = row_off // tm
    bpg = rows_per_group // tm

    def body(*refs):
        x_ref, sh_ref, sc_ref, nw_ref, o_ref = refs[-5:]
        xv = x_ref[...]
        r = lax.rsqrt(jnp.mean(xv * xv, axis=-1, keepdims=True) + EPS)
        o_ref[...] = ((xv * r) * nw_ref[...] * (1.0 + sc_ref[...]) + sh_ref[...]).astype(BF16)

    in_specs = [pl.BlockSpec((tm, D), lambda i: (i, 0)),
                pl.BlockSpec((None, 1, D), lambda i: (group0 + i // bpg, 0, 0)),
                pl.BlockSpec((None, 1, D), lambda i: (group0 + i // bpg, 0, 1)),
                pl.BlockSpec((1, D), lambda i: (0, 0))]
    in_specs = [pl.BlockSpec(memory_space=pl.ANY)] * len(after) + in_specs
    args = list(after) + [x2, mod3, mod3, norm_w]
    alias = {}
    if h_prev is not None:
        in_specs.insert(0, pl.BlockSpec(memory_space=pl.ANY))
        args.insert(0, h_prev)
        alias = {0: 0}
    return pl.pallas_call(
        body, name=name, grid=(rows // tm,), in_specs=in_specs,
        out_specs=pl.BlockSpec((tm, D), lambda i: (rb0 + i, 0)), out_shape=_sds((rows_all, D), BF16),
        input_output_aliases=alias, compiler_params=_params(("parallel",)),
    )(*args)


def _decays(lg, fwd):
    ii = lax.broadcasted_iota(jnp.int32, (CH, CH), 0)
    jj = lax.broadcasted_iota(jnp.int32, (CH, CH), 1)
    ri = lax.broadcasted_iota(jnp.int32, (CH, 1), 0).astype(F32)
    rel = (ii - jj) if fwd else (jj - ii)
    relf = jnp.maximum(rel, 0).astype(F32)
    mask = jnp.where(rel >= 0, jnp.exp(lg * relf), 0.0)
    qe = (ri + 1.0) if fwd else (CH - ri)
    ke = (CH - 1.0 - ri) if fwd else ri
    return mask, relf, jnp.exp(lg * qe), qe, jnp.exp(lg * ke), ke


def _wide_specs(rowf):
    return [pl.BlockSpec((CH, 2 * DK), lambda b, c: (rowf(b, c), RQ // (2 * DK))),
            pl.BlockSpec((CH, 2 * DK), lambda b, c: (rowf(b, c), RQ // (2 * DK) + 1)),
            pl.BlockSpec((CH, RH * DK), lambda b, c: (rowf(b, c), RK // (RH * DK))),
            pl.BlockSpec((CH, 2 * DV), lambda b, c: (rowf(b, c), RV // (2 * DV))),
            pl.BlockSpec((CH, 2 * DV), lambda b, c: (rowf(b, c), RV // (2 * DV) + 1))]


def _head_qkv(refs, h):
    q0, q1, k, v0, v1 = refs
    lo = h % 2
    q = (q0, q1)[h // 2][:, lo * DK:(lo + 1) * DK].astype(F32)
    kk = k[:, h * DK:(h + 1) * DK].astype(F32) * (DK ** -0.5)
    v16 = (v0, v1)[h // 2][:, lo * DV:(lo + 1) * DV].astype(BF16)
    return q, kk, v16


def _ctx_specs(t_rows, cx):
    rb = t_rows // cx
    return [pl.BlockSpec((cx, RH * DK), lambda b, c: (rb + b, RK // (RH * DK))),
            pl.BlockSpec((cx, 2 * DV), lambda b, c: (rb + b, RV // (2 * DV))),
            pl.BlockSpec((cx, 2 * DV), lambda b, c: (rb + b, RV // (2 * DV) + 1))]


def _ctx_kv(refs, h):
    k, v0, v1 = refs
    kk = k[:, h * DK:(h + 1) * DK].astype(F32) * (DK ** -0.5)
    lo = h % 2
    return kk, (v0, v1)[h // 2][:, lo * DV:(lo + 1) * DV].astype(BF16)


def _ret_fwd(px, lg, nb, nc, cx):
    t_rows = nb * nc * CH

    def body(lg_ref, *refs):
        ins = (refs[0:5], refs[5:10])
        ctx_refs = refs[10:13]
        of_ref, ob_ref, hf_ref, hb_ref, sf, sb = refs[13:]
        c = pl.program_id(1)

        @pl.when(c == 0)
        def _():
            pos = lax.broadcasted_iota(jnp.int32, (cx, 1), 0).astype(F32)
            for h in range(RH):
                k, v16 = _ctx_kv(ctx_refs, h)
                sf[h] = _dot((k * jnp.exp(lg_ref[0, h] * (cx - 1.0 - pos))).astype(BF16), v16, TN)
                sb[h] = _dot((k * jnp.exp(lg_ref[1, h] * pos)).astype(BF16), v16, TN)

        for d, (o_ref, h_ref, s) in enumerate(((of_ref, hf_ref, sf), (ob_ref, hb_ref, sb))):
            for h in range(RH):
                lg_d = lg_ref[d, h]
                mask, _, qd, _, kd, _ = _decays(lg_d, d == 0)
                q, k, v16 = _head_qkv(ins[d], h)
                a = _dot(q.astype(BF16), k.astype(BF16), NT)
                st = s[h]
                st16 = st.astype(BF16)
                h_ref[h] = st16
                o = _dot((a * mask).astype(BF16), v16) + _dot((q * qd).astype(BF16), st16)
                o_ref[:, h * DV:(h + 1) * DV] = o.astype(BF16)
                s[h] = st * jnp.exp(lg_d * CH) + _dot((k * kd).astype(BF16), v16, TN)

    def fw(b, c):
        return b * nc + c

    def bw(b, c):
        return b * nc + nc - 1 - c

    in_specs = [pl.BlockSpec(memory_space=pltpu.SMEM)] + _wide_specs(fw) + _wide_specs(bw) + _ctx_specs(t_rows, cx)
    out_specs = [pl.BlockSpec((CH, RH * DV), lambda b, c: (fw(b, c), 0)),
                 pl.BlockSpec((CH, RH * DV), lambda b, c: (bw(b, c), 0)),
                 pl.BlockSpec((None, None, RH, DK, DV), lambda b, c: (b, c, 0, 0, 0)),
                 pl.BlockSpec((None, None, RH, DK, DV), lambda b, c: (b, nc - 1 - c, 0, 0, 0))]
    return pl.pallas_call(
        body, name="ret_fwd", grid=(nb, nc), in_specs=in_specs, out_specs=out_specs,
        out_shape=[_sds((t_rows, RH * DV), BF16)] * 2 + [_sds((nb, nc, RH, DK, DV), BF16)] * 2,
        scratch_shapes=[pltpu.VMEM((RH, DK, DV), F32), pltpu.VMEM((RH, DK, DV), F32)],
        compiler_params=_params(("parallel", "arbitrary")),
    )(lg, *([px] * 13))


def _rope_tables(seq):
    rows = seq // GRID_W
    row = np.repeat(np.arange(rows, dtype=np.float32), GRID_W)
    col = np.tile(np.arange(GRID_W, dtype=np.float32), rows)
    half = HD // 2
    freqs = (ROPE_THETA ** (-np.arange(0, half, 2, dtype=np.float32) / half)).astype(np.float32)
    ang = np.concatenate([row[:, None] * freqs, col[:, None] * freqs], axis=-1).astype(np.float32)
    cos = np.repeat(np.cos(ang), 2, axis=-1).astype(np.float32)
    sin = np.repeat(np.sin(ang), 2, axis=-1).astype(np.float32)
    sign = np.tile(np.array([-1.0, 1.0], np.float32), HD // 2)
    return jnp.asarray(cos), jnp.asarray(sin * sign)


def _swap_pairs(v):
    lane = lax.broadcasted_iota(jnp.int32, v.shape, 1)
    return jnp.where((lane & 1) == 0, pltpu.roll(v, HD - 1, 1), pltpu.roll(v, 1, 1))


def _qk_prep(px, nw, cos, sin, rows, row_off, col_off, heads, hb, seq, tm, name):
    rope = cos is not None
    rb0 = row_off // tm
    pb = seq // tm if rope else 1
    bw = hb * HD

    def body(*refs):
        if rope:
            x_ref, w_ref, c_ref, s_ref, o_ref = refs
        else:
            x_ref, w_ref, o_ref = refs
        for h in range(hb):
            sl = slice(h * HD, (h + 1) * HD)
            xv = x_ref[:, sl].astype(F32)
            r = lax.rsqrt(jnp.mean(xv * xv, axis=-1, keepdims=True) + EPS)
            t = (xv * r) * w_ref[...]
            if rope:
                t = t * c_ref[...] + _swap_pairs(t) * s_ref[...]
            o_ref[:, sl] = t.astype(BF16)

    in_specs = [pl.BlockSpec((tm, bw), lambda i, j: (rb0 + i, col_off // bw + j)),
                pl.BlockSpec((1, HD), lambda i, j: (0, 0))]
    args = [px, nw]
    if rope:
        in_specs += [pl.BlockSpec((tm, HD), lambda i, j: (i % pb, 0))] * 2
        args += [cos, sin]
    return pl.pallas_call(
        body, name=name, grid=(rows // tm, heads // hb), in_specs=in_specs,
        out_specs=pl.BlockSpec((tm, bw), lambda i, j: (i, j)), out_shape=_sds((rows, heads * HD), BF16),
        compiler_params=_params(("parallel", "parallel")),
    )(*args)


def _att_fwd(q16, kx16, kc16, px, nb, seq, cx, tq):
    t_rows = nb * seq
    nq = seq // tq
    rep = HQ // HKV
    gw = rep * HD

    def body(q_ref, kx_ref, kc_ref, vx_ref, vc_ref, g_ref, o_ref, y_ref, l_ref):
        kx = kx_ref[...]
        kc = kc_ref[...]
        vx = vx_ref[...].astype(BF16)
        vc = vc_ref[...].astype(BF16)
        l_ref[...] = jnp.zeros_like(l_ref)
        for r in range(rep):
            sl = slice(r * HD, (r + 1) * HD)
            q = q_ref[:, sl]
            s1 = _dot(q, kx, NT)
            s2 = _dot(q, kc, NT)
            m = jnp.maximum(jnp.max(s1, axis=-1, keepdims=True), jnp.max(s2, axis=-1, keepdims=True))
            e1 = jnp.exp2((s1 - m) * SM_C)
            e2 = jnp.exp2((s2 - m) * SM_C)
            tot = jnp.sum(e1, axis=-1, keepdims=True) + jnp.sum(e2, axis=-1, keepdims=True)
            o = (_dot(e1.astype(BF16), vx) + _dot(e2.astype(BF16), vc)) * (1.0 / tot)
            o_ref[:, sl] = o
            y_ref[:, sl] = (o * _silu(g_ref[:, sl].astype(F32))).astype(BF16)
            l_ref[:, r:r + 1] = m * SM_C + jnp.log(tot) * float(np.log2(np.e))

    qblk = pl.BlockSpec((tq, gw), lambda b, g, i: (b * nq + i, g))
    return pl.pallas_call(
        body, name="att_fwd", grid=(nb, HKV, nq),
        in_specs=[qblk,
                  pl.BlockSpec((seq, HD), lambda b, g, i: (b, g)),
                  pl.BlockSpec((cx, HD), lambda b, g, i: (b, g)),
                  pl.BlockSpec((seq, HD), lambda b, g, i: (b, AV // HD + g)),
                  pl.BlockSpec((cx, HD), lambda b, g, i: (t_rows // cx + b, AV // HD + g)),
                  pl.BlockSpec((tq, gw), lambda b, g, i: (b * nq + i, AG // gw + g))],
        out_specs=[qblk, qblk, pl.BlockSpec((tq, 128), lambda b, g, i: (b * nq + i, g))],
        out_shape=[_sds((t_rows, D), F32), _sds((t_rows, D), BF16), _sds((t_rows, HKV * 128), F32)],
        compiler_params=_params(("parallel", "parallel", "parallel")),
    )(q16, kx16, kc16, px, px, px)


def _gate_specs(tm, col0):
    hw = D // 2
    return [pl.BlockSpec((tm, hw), lambda i: (i, col0 // hw)), pl.BlockSpec((tm, hw), lambda i: (i, col0 // hw + 1))]


def _merge_out(o_f, o_b, yatt16, px, w_o_ret16, w_o_att16, w_out16, x2, tgt, mod3, nb, seq, tm):
    t_rows = nb * seq
    bpb = seq // tm
    hw = D // 2

    def body(of_ref, ob_ref, g0, g1, g2, g3, wr_ref, ya_ref, wa_ref, mr0, mr1, ma0, ma1, wo_ref, x_ref, t_ref, gt_ref,
             yr_ref, ar_ref, aa_ref, y_ref, dxn_ref, dout_ref, dg_ref, loss_ref):
        i = pl.program_id(1)
        for h, g_ref in enumerate((g0, g1, g2, g3)):
            sl = slice(h * DV, (h + 1) * DV)
            o = of_ref[:, sl].astype(F32) + ob_ref[:, sl].astype(F32)
            r = lax.rsqrt(jnp.mean(o * o, axis=-1, keepdims=True) + EPS)
            yr_ref[:, sl] = ((o * r) * _silu(g_ref[...].astype(F32))).astype(BF16)
        ar = _dot(yr_ref[...], wr_ref[...])
        aa = _dot(ya_ref[...], wa_ref[...])
        ar_ref[...] = ar.astype(BF16)
        aa_ref[...] = aa.astype(BF16)
        for j, (mr_ref, ma_ref) in enumerate(((mr0, ma0), (mr1, ma1))):
            sl = slice(j * hw, (j + 1) * hw)
            y_ref[:, sl] = (_sig(mr_ref[...].astype(F32)) * ar[:, sl]
                            + _sig(ma_ref[...].astype(F32)) * aa[:, sl]).astype(BF16)
        out = _dot(y_ref[...], wo_ref[...])
        gate = gt_ref[...]
        diff = x_ref[...] + gate * out - t_ref[...]
        dxn = diff * (1.0 / D)
        dxn_ref[...] = dxn
        dout_ref[...] = (gate * dxn).astype(BF16)
        dg = jnp.sum(dxn * out, axis=0, keepdims=True)
        ls = jnp.broadcast_to(jnp.sum(diff * diff) * (0.5 / D), (1, 128))

        @pl.when(i == 0)
        def _():
            dg_ref[...] = dg
            loss_ref[...] = ls

        @pl.when(i > 0)
        def _():
            dg_ref[...] += dg
            loss_ref[...] += ls

    def cols(width, col0):
        return pl.BlockSpec((tm, width), lambda b, i: (b * bpb + i, col0 // width))

    def whole(rows):
        return pl.BlockSpec((rows, D), lambda b, i: (0, 0))

    row, wide = cols(D, 0), cols(RH * DV, 0)
    gates = [cols(DV, RG + h * DV) for h in range(RH)]
    merge_gates = [cols(hw, MR), cols(hw, MR + hw), cols(hw, MA), cols(hw, MA + hw)]
    return pl.pallas_call(
        body, name="merge_out", grid=(nb, bpb),
        in_specs=[wide, wide] + gates + [whole(RH * DV), row, whole(D)] + merge_gates
        + [whole(D), row, row, pl.BlockSpec((None, 1, D), lambda b, i: (b, 0, 2))],
        out_specs=[wide, row, row, row, row, row, pl.BlockSpec((None, 1, D), lambda b, i: (b, 0, 0)),
                   pl.BlockSpec((None, 1, 128), lambda b, i: (b, 0, 0))],
        out_shape=[_sds((t_rows, RH * DV), BF16)] + [_sds((t_rows, D), BF16)] * 3
        + [_sds((t_rows, D), F32), _sds((t_rows, D), BF16), _sds((nb, 1, D), F32), _sds((nb, 1, 128), F32)],
        compiler_params=_params(("parallel", "arbitrary")),
    )(o_f, o_b, *([px] * RH), w_o_ret16, yatt16, w_o_att16, px, px, px, px, w_out16, x2, tgt, mod3)


def _bwd_branches(dout16, w_out16, w_o_ret16, w_o_att16, px, a_ret, a_att, o_f, o_b, o_att, rows_all, tm):
    t_rows = dout16.shape[0]
    hw = D // 2

    def body(do_ref, wo_ref, wr_ref, wa_ref, mr0, mr1, ma0, ma1, ar_ref, aa_ref, rg0, rg1, rg2, rg3, of_ref, ob_ref,
             ag0, ag1, oa_ref, dar_ref, daa_ref, dor_ref, dao_ref, dl_ref, dp_ref):
        dy_all = _dot(do_ref[...], wo_ref[...], NT)
        for j, (mr_ref, ma_ref) in enumerate(((mr0, ma0), (mr1, ma1))):
            sl = slice(j * hw, (j + 1) * hw)
            dy = dy_all[:, sl]
            sr = _sig(mr_ref[...].astype(F32))
            sa = _sig(ma_ref[...].astype(F32))
            dar_ref[:, sl] = (dy * sr).astype(BF16)
            daa_ref[:, sl] = (dy * sa).astype(BF16)
            dp_ref[:, MR - RG + j * hw:MR - RG + (j + 1) * hw] = (
                dy * ar_ref[:, sl].astype(F32) * sr * (1.0 - sr)).astype(BF16)
            dp_ref[:, MA - RG + j * hw:MA - RG + (j + 1) * hw] = (
                dy * aa_ref[:, sl].astype(F32) * sa * (1.0 - sa)).astype(BF16)
        da_ret = dar_ref[...]
        for h, g_ref in enumerate((rg0, rg1, rg2, rg3)):
            sl = slice(h * DV, (h + 1) * DV)
            dy = _dot(da_ret, wr_ref[sl, :], NT)
            g = g_ref[...].astype(F32)
            o = of_ref[:, sl].astype(F32) + ob_ref[:, sl].astype(F32)
            r = lax.rsqrt(jnp.mean(o * o, axis=-1, keepdims=True) + EPS)
            on = o * r
            sg = _sig(g)
            don = dy * (g * sg)
            dp_ref[:, sl] = (dy * on * (sg * (1.0 + g * (1.0 - sg)))).astype(BF16)
            dor_ref[:, sl] = (r * (don - on * jnp.mean(on * don, axis=-1, keepdims=True))).astype(BF16)
        dy_all = _dot(daa_ref[...], wa_ref[...], NT)
        dl_ref[...] = jnp.zeros_like(dl_ref)
        for j, g_ref in enumerate((ag0, ag1)):
            sl = slice(j * hw, (j + 1) * hw)
            dy = dy_all[:, sl]
            g = g_ref[...].astype(F32)
            sg = _sig(g)
            dao = dy * (g * sg)
            dao_ref[:, sl] = dao.astype(BF16)
            prod = dao * oa_ref[:, sl]
            for r in range(hw // HD):
                dl_ref[:, j * 128 + r:j * 128 + r + 1] = jnp.sum(prod[:, r * HD:(r + 1) * HD], axis=-1, keepdims=True)
            dp_ref[:, AG - RG + j * hw:AG - RG + (j + 1) * hw] = (
                dy * oa_ref[:, sl] * (sg * (1.0 + g * (1.0 - sg)))).astype(BF16)

    def gate(h):
        return pl.BlockSpec((tm, DV), lambda i: (i, RG // DV + h))

    def whole(rows):
        return pl.BlockSpec((rows, D), lambda i: (0, 0))

    row = pl.BlockSpec((tm, D), lambda i: (i, 0))
    wide = pl.BlockSpec((tm, RH * DV), lambda i: (i, 0))
    return pl.pallas_call(
        body, name="bwd_branches", grid=(t_rows // tm,),
        in_specs=[row, whole(D), whole(RH * DV), whole(D)] + _gate_specs(tm, MR) + _gate_specs(tm, MA) + [row, row]
        + [gate(h) for h in range(RH)] + [wide, wide] + _gate_specs(tm, AG) + [row],
        out_specs=[row, row, wide, row, pl.BlockSpec((tm, HKV * 128), lambda i: (i, 0)),
                   pl.BlockSpec((pl.Element(tm), pl.Element(IN_COLS - RG)), lambda i: (i * tm, RG))],
        out_shape=[_sds((t_rows, D), BF16)] * 2 + [_sds((t_rows, RH * DV), BF16), _sds((t_rows, D), BF16),
                                                  _sds((t_rows, HKV * 128), F32), _sds((rows_all, IN_COLS), BF16)],
        compiler_params=_params(("parallel",)),
    )(dout16, w_out16, w_o_ret16, w_o_att16, px, px, px, px, a_ret, a_att, *([px] * RH), o_f, o_b, px, px, o_att)


def _att_bwd(q16, kx16, kc16, px, dao16, delta, lse, q_norm_w, cos, sin, dp_all, nb, seq, cx, tq, after=()):
    t_rows = nb * seq
    nq = seq // tq
    rep = HQ // HKV
    gw = rep * HD
    scale = HD ** -0.5

    def body(q_ref, kx_ref, kc_ref, vx_ref, vc_ref, dao_ref, dl_ref, l_ref, xq_ref, w_ref, c_ref, s_ref, *rest):
        daq_ref, gq_ref, dkx_ref, dvx_ref, dkc_ref, dvc_ref = rest[1 + len(after):7 + len(after)]
        accs = rest[7 + len(after):]
        i = pl.program_id(2)
        first = jnp.logical_and(jnp.logical_and(pl.program_id(0) == 0, pl.program_id(1) == 0), i == 0)
        gq = jnp.zeros((1, HD), F32)
        kx = kx_ref[...]
        kc = kc_ref[...]
        vx = vx_ref[...].astype(BF16)
        vc = vc_ref[...].astype(BF16)
        @pl.when(i == 0)
        def _():
            for acc in accs:
                acc[...] = jnp.zeros_like(acc)

        dkx, dvx, dkc, dvc = [acc[...] for acc in accs]
        for r in range(rep):
            sl = slice(r * HD, (r + 1) * HD)
            q = q_ref[:, sl]
            lr = l_ref[:, r:r + 1]
            p1 = jnp.exp2(_dot(q, kx, NT) * SM_C - lr)
            p2 = jnp.exp2(_dot(q, kc, NT) * SM_C - lr)
            da16 = dao_ref[:, sl]
            delta = dl_ref[:, r:r + 1]
            ds1 = (p1 * (_dot(da16, vx, NT) - delta)).astype(BF16)
            ds2 = (p2 * (_dot(da16, vc, NT) - delta)).astype(BF16)
            dq = (_dot(ds1, kx) + _dot(ds2, kc)) * scale
            dkx += _dot(q, ds1, TN)
            dkc += _dot(q, ds2, TN)
            dvx += _dot(da16, p1.astype(BF16), TN)
            dvc += _dot(da16, p2.astype(BF16), TN)
            dt = dq * c_ref[...] + _swap_pairs(dq * s_ref[...])
            xv = xq_ref[:, sl].astype(F32)
            rn = lax.rsqrt(jnp.mean(xv * xv, axis=-1, keepdims=True) + EPS)
            xh = xv * rn
            dxh = dt * w_ref[...]
            daq_ref[:, sl] = (rn * (dxh - xh * jnp.mean(dxh * xh, axis=-1, keepdims=True))).astype(BF16)
            gq += jnp.sum(dt * xh, axis=0, keepdims=True)
        for acc, val in zip(accs, (dkx, dvx, dkc, dvc)):
            acc[...] = val

        @pl.when(first)
        def _():
            gq_ref[...] = gq

        @pl.when(jnp.logical_not(first))
        def _():
            gq_ref[...] += gq

        @pl.when(i == nq - 1)
        def _():
            dkx_ref[...] = dkx.T * scale
            dvx_ref[...] = dvx.T
            dkc_ref[...] = dkc.T * scale
            dvc_ref[...] = dvc.T

    qblk = pl.BlockSpec((tq, gw), lambda b, g, i: (b * nq + i, g))
    kxb = pl.BlockSpec((None, seq, HD), lambda b, g, i: (b, 0, g))
    kcb = pl.BlockSpec((None, cx, HD), lambda b, g, i: (b, 0, g))
    table = pl.BlockSpec((tq, HD), lambda b, g, i: (i, 0))
    one = pl.BlockSpec((1, HD), lambda b, g, i: (0, 0))
    lane = pl.BlockSpec((tq, 128), lambda b, g, i: (b * nq + i, g))
    return pl.pallas_call(
        body, name="att_bwd", grid=(nb, HKV, nq),
        in_specs=[qblk,
                  pl.BlockSpec((seq, HD), lambda b, g, i: (b, g)),
                  pl.BlockSpec((cx, HD), lambda b, g, i: (b, g)),
                  pl.BlockSpec((seq, HD), lambda b, g, i: (b, AV // HD + g)),
                  pl.BlockSpec((cx, HD), lambda b, g, i: (t_rows // cx + b, AV // HD + g)),
                  qblk, lane, lane,
                  pl.BlockSpec((tq, gw), lambda b, g, i: (b * nq + i, AQ // gw + g)), one, table, table]
        + [pl.BlockSpec(memory_space=pl.ANY)] * (1 + len(after)),
        out_specs=[pl.BlockSpec((tq, gw), lambda b, g, i: (b * nq + i, AQ // gw + g)), one, kxb, kxb, kcb, kcb],
        out_shape=[_sds(dp_all.shape, BF16), _sds((1, HD), F32), _sds((nb, seq, HKV * HD), F32),
                   _sds((nb, seq, HKV * HD), F32), _sds((nb, cx, HKV * HD), F32), _sds((nb, cx, HKV * HD), F32)],
        scratch_shapes=[pltpu.VMEM((HD, seq), F32), pltpu.VMEM((HD, seq), F32), pltpu.VMEM((HD, cx), F32),
                        pltpu.VMEM((HD, cx), F32)],
        input_output_aliases={12: 0},
        compiler_params=_params(("arbitrary", "arbitrary", "arbitrary")),
    )(q16, kx16, kc16, px, px, dao16, delta, lse, px, q_norm_w, cos, sin, dp_all, *after)


def _qk_prep_bwd(dt, px, nw, cos, sin, rows, row_off, col_off, heads, hb, seq, tm, name):
    rope = cos is not None
    rb0 = row_off // tm
    pb = seq // tm if rope else 1
    bw = hb * HD

    def body(*refs):
        if rope:
            d_ref, x_ref, w_ref, c_ref, s_ref, dx_ref, dw_ref = refs
        else:
            d_ref, x_ref, w_ref, dx_ref, dw_ref = refs
        first = jnp.logical_and(pl.program_id(0) == 0, pl.program_id(1) == 0)
        dw = jnp.zeros((1, HD), F32)
        for h in range(hb):
            sl = slice(h * HD, (h + 1) * HD)
            dtv = d_ref[:, sl]
            if rope:
                dtv = dtv * c_ref[...] + _swap_pairs(dtv * s_ref[...])
            xv = x_ref[:, sl].astype(F32)
            r = lax.rsqrt(jnp.mean(xv * xv, axis=-1, keepdims=True) + EPS)
            xh = xv * r
            dxh = dtv * w_ref[...]
            dx_ref[:, sl] = (r * (dxh - xh * jnp.mean(dxh * xh, axis=-1, keepdims=True))).astype(BF16)
            dw += jnp.sum(dtv * xh, axis=0, keepdims=True)

        @pl.when(first)
        def _():
            dw_ref[...] = dw

        @pl.when(jnp.logical_not(first))
        def _():
            dw_ref[...] += dw

    blk = pl.BlockSpec((tm, bw), lambda i, j: (i, j))
    in_specs = [blk, pl.BlockSpec((tm, bw), lambda i, j: (rb0 + i, col_off // bw + j)),
                pl.BlockSpec((1, HD), lambda i, j: (0, 0))]
    args = [dt, px, nw]
    if rope:
        in_specs += [pl.BlockSpec((tm, HD), lambda i, j: (i % pb, 0))] * 2
        args += [cos, sin]
    return pl.pallas_call(
        body, name=name, grid=(rows // tm, heads // hb), in_specs=in_specs,
        out_specs=[blk, pl.BlockSpec((1, HD), lambda i, j: (0, 0))],
        out_shape=[_sds((rows, heads * HD), BF16), _sds((1, HD), F32)],
        compiler_params=_params(("arbitrary", "arbitrary")),
    )(*args)


def _ret_bwd(px, lg, do16, hist_f, hist_b, nb, nc, cx):
    t_rows = nb * nc * CH

    def body(lg_ref, *refs):
        ins = (refs[0:5], refs[7:12])
        do_refs = (refs[5], refs[12])
        h_refs = (refs[6], refs[13])
        ctx_refs = refs[14:17]
        outs = (refs[17:20], refs[20:23])
        dck_ref, dcv_ref, dlg_ref = refs[23:26]
        dss = (refs[26], refs[27])
        c = pl.program_id(1)

        @pl.when(c == 0)
        def _():
            dss[0][...] = jnp.zeros_like(dss[0])
            dss[1][...] = jnp.zeros_like(dss[1])
            dlg_ref[...] = jnp.zeros_like(dlg_ref)

        for d in range(2):
            dq_ref, dk_ref, dv_ref = outs[d]
            for h in range(RH):
                lg_d = lg_ref[d, h]
                mask, relf, qd, qe, kd, ke = _decays(lg_d, d == 0)
                g_ch = jnp.exp(lg_d * CH)
                q, k, v16 = _head_qkv(ins[d], h)
                q16 = q.astype(BF16)
                k16 = k.astype(BF16)
                do16v = do_refs[d][:, h * DV:(h + 1) * DV]
                st16 = h_refs[d][h]
                dst = dss[d][h]
                dst16 = dst.astype(BF16)
                a = _dot(q16, k16, NT) * mask
                dp = _dot(do16v, v16, NT)
                da16 = (dp * mask).astype(BF16)
                dq_cross = _dot(do16v, st16, NT) * qd
                dq_ref[:, h * DK:(h + 1) * DK] = (_dot(da16, k16) + dq_cross).astype(BF16)
                dk_state = _dot(v16, dst16, NT) * kd
                dk_ref[:, h * DK:(h + 1) * DK] = ((_dot(da16, q16, TN) + dk_state) * (DK ** -0.5)).astype(BF16)
                dv = _dot(a.astype(BF16), do16v, TN) + _dot((k * kd).astype(BF16), dst16)
                dv_ref[:, h * DV:(h + 1) * DV] = dv.astype(BF16)
                dlg = (jnp.sum(relf * a * dp)
                       + jnp.sum(qe * jnp.sum(q * dq_cross, axis=-1, keepdims=True))
                       + jnp.sum(ke * jnp.sum(k * dk_state, axis=-1, keepdims=True))
                       + CH * g_ch * jnp.sum(dst * st16.astype(F32)))
                row = d * RH + h
                dlg_ref[row:row + 1, :] += jnp.broadcast_to(dlg, (1, 128))
                dss[d][h] = g_ch * dst + _dot((q * qd).astype(BF16), do16v, TN)

        @pl.when(c == nc - 1)
        def _():
            pos = lax.broadcasted_iota(jnp.int32, (cx, 1), 0).astype(F32)
            for h in range(RH):
                k, v16 = _ctx_kv(ctx_refs, h)
                dk = jnp.zeros((cx, DK), F32)
                dv = jnp.zeros((cx, DV), F32)
                for d, e in enumerate((cx - 1.0 - pos, pos)):
                    w = jnp.exp(lg_ref[d, h] * e)
                    ds16 = dss[d][h].astype(BF16)
                    t = _dot(v16, ds16, NT)
                    dk += t * w
                    dv += _dot((k * w).astype(BF16), ds16)
                    dlg = jnp.sum(e * w * jnp.sum(k * t, axis=-1, keepdims=True))
                    row = d * RH + h
                    dlg_ref[row:row + 1, :] += jnp.broadcast_to(dlg, (1, 128))
                dck_ref[:, h * DK:(h + 1) * DK] = (dk * (DK ** -0.5)).astype(BF16)
                dcv_ref[:, h * DV:(h + 1) * DV] = dv.astype(BF16)

    def fw(b, c):
        return b * nc + nc - 1 - c

    def bw(b, c):
        return b * nc + c

    def rows(rowf, width):
        return pl.BlockSpec((CH, width), lambda b, c: (rowf(b, c), 0))

    def hist(rowf):
        return pl.BlockSpec((None, None, RH, DK, DV), lambda b, c: (b, rowf(0, c), 0, 0, 0))

    in_specs = [pl.BlockSpec(memory_space=pltpu.SMEM)]
    out_specs = []
    for rowf in (fw, bw):
        in_specs += _wide_specs(rowf) + [rows(rowf, RH * DV), hist(rowf)]
        out_specs += [rows(rowf, RH * DK), rows(rowf, RH * DK), rows(rowf, RH * DV)]
    in_specs += _ctx_specs(t_rows, cx)
    out_specs += [pl.BlockSpec((cx, RH * DK), lambda b, c: (b, 0)), pl.BlockSpec((cx, RH * DV), lambda b, c: (b, 0)),
                  pl.BlockSpec((None, 8, 128), lambda b, c: (b, 0, 0))]
    qk = _sds((t_rows, RH * DK), BF16)
    vv = _sds((t_rows, RH * DV), BF16)
    return pl.pallas_call(
        body, name="ret_bwd", grid=(nb, nc), in_specs=in_specs, out_specs=out_specs,
        out_shape=[qk, qk, vv, qk, qk, vv, _sds((nb * cx, RH * DK), BF16), _sds((nb * cx, RH * DV), BF16),
                   _sds((nb, 8, 128), F32)],
        scratch_shapes=[pltpu.VMEM((RH, DK, DV), F32), pltpu.VMEM((RH, DK, DV), F32)],
        compiler_params=_params(("parallel", "arbitrary")),
    )(lg, *([px] * 5), do16, hist_f, *([px] * 5), do16, hist_b, *([px] * 3))


def _assemble_lat(dp_all, dk_f, dk_b, dv_f, dv_b, dak16, dvx, dq_f, dq_b, tm):
    t_rows = dk_f.shape[0]

    def body(_, dkf, dkb, dvf, dvb, dak, dav, dqf, dqb, o_ref):
        o_ref[:, RK:RK + RH * DK] = (dkf[...].astype(F32) + dkb[...].astype(F32)).astype(BF16)
        o_ref[:, RV:RV + RH * DV] = (dvf[...].astype(F32) + dvb[...].astype(F32)).astype(BF16)
        o_ref[:, AK:AK + HKV * HD] = dak[...]
        o_ref[:, AV:AV + HKV * HD] = dav[...].astype(BF16)
        o_ref[:, RQ:RQ + RH * DK] = (dqf[...].astype(F32) + dqb[...].astype(F32)).astype(BF16)

    args = (dk_f, dk_b, dv_f, dv_b, dak16, dvx, dq_f, dq_b)
    return pl.pallas_call(
        body, name="assemble_lat", grid=(t_rows // tm,),
        in_specs=[pl.BlockSpec(memory_space=pl.ANY)]
        + [pl.BlockSpec((tm, a.shape[1]), lambda i: (i, 0)) for a in args],
        out_specs=pl.BlockSpec((tm, RG), lambda i: (i, 0)), out_shape=_sds(dp_all.shape, BF16),
        input_output_aliases={0: 0},
        compiler_params=_params(("parallel",)),
    )(dp_all, *args)


def _assemble_ctx(dp_all, dck16, dcv16, dcak16, dvc, t_rows, tm):
    c_rows = dck16.shape[0]
    rb = t_rows // tm

    def body(_, dck, dcv, dcak, dcav, o_ref):
        o_ref[:, RK:RK + RH * DK] = dck[...]
        o_ref[:, RV:RV + RH * DV] = dcv[...]
        o_ref[:, AK:AK + HKV * HD] = dcak[...]
        o_ref[:, AV:AV + HKV * HD] = dcav[...].astype(BF16)
        o_ref[:, KV_COLS:] = jnp.zeros((tm, IN_COLS - KV_COLS), BF16)

    args = (dck16, dcv16, dcak16, dvc)
    return pl.pallas_call(
        body, name="assemble_ctx", grid=(c_rows // tm,),
        in_specs=[pl.BlockSpec(memory_space=pl.ANY)]
        + [pl.BlockSpec((tm, a.shape[1]), lambda i: (i, 0)) for a in args],
        out_specs=pl.BlockSpec((tm, IN_COLS), lambda i: (rb + i, 0)), out_shape=_sds(dp_all.shape, BF16),
        input_output_aliases={0: 0},
        compiler_params=_params(("parallel",)),
    )(dp_all, *args)


def _norm_bwd(dh, x2, mod3, norm_w, dxn, row_off, rows_per_group, group0, tm, name):
    with_dx = dxn is not None
    rows = x2.shape[0]
    rb0 = row_off // tm
    bpg = rows_per_group // tm
    ngroups = rows // rows_per_group

    def body(*refs):
        if with_dx:
            dh_ref, x_ref, sc_ref, nw_ref, dxn_ref, dx_ref, dsh_ref, dsc_ref, dnw_ref = refs
        else:
            dh_ref, x_ref, sc_ref, nw_ref, dsh_ref, dsc_ref, dnw_ref = refs
        i = pl.program_id(0)
        dhv = dh_ref[...]
        xv = x_ref[...]
        nw = nw_ref[...]
        r = lax.rsqrt(jnp.mean(xv * xv, axis=-1, keepdims=True) + EPS)
        xh = xv * r
        dm = dhv * (1.0 + sc_ref[...])
        dsh = jnp.sum(dhv, axis=0, keepdims=True)
        dsc = jnp.sum(dhv * (xh * nw), axis=0, keepdims=True)
        dnw = jnp.sum(dm * xh, axis=0, keepdims=True)
        if with_dx:
            dxh = dm * nw
            dx_ref[...] = dxn_ref[...] + r * (dxh - xh * jnp.mean(dxh * xh, axis=-1, keepdims=True))

        @pl.when(i % bpg == 0)
        def _():
            dsh_ref[...] = dsh
            dsc_ref[...] = dsc

        @pl.when(i % bpg != 0)
        def _():
            dsh_ref[...] += dsh
            dsc_ref[...] += dsc

        @pl.when(i == 0)
        def _():
            dnw_ref[...] = dnw

        @pl.when(i > 0)
        def _():
            dnw_ref[...] += dnw

    grp = pl.BlockSpec((None, 1, D), lambda i: (i // bpg, 0, 0))
    in_specs = [pl.BlockSpec((tm, D), lambda i: (rb0 + i, 0)), pl.BlockSpec((tm, D), lambda i: (i, 0)),
                pl.BlockSpec((None, 1, D), lambda i: (group0 + i // bpg, 0, 1)),
                pl.BlockSpec((1, D), lambda i: (0, 0))]
    args = [dh, x2, mod3, norm_w]
    out_specs = [grp, grp, pl.BlockSpec((1, D), lambda i: (0, 0))]
    out_shape = [_sds((ngroups, 1, D), F32), _sds((ngroups, 1, D), F32), _sds((1, D), F32)]
    if with_dx:
        in_specs.append(pl.BlockSpec((tm, D), lambda i: (i, 0)))
        args.append(dxn)
        out_specs.insert(0, pl.BlockSpec((tm, D), lambda i: (i, 0)))
        out_shape.insert(0, _sds((rows, D), F32))
    return pl.pallas_call(
        body, name=name, grid=(rows // tm,), in_specs=in_specs, out_specs=out_specs, out_shape=out_shape,
        compiler_params=_params(("arbitrary",)),
    )(*args)


def _small_final(dmod_all, dmodc_parts, c_rows, dm_loc_rows, nw_parts, misc_parts, c_ctx, r_pad, w_ada16):
    loc = dm_loc_rows.shape[1]

    def body(dm_ref, dmc_ref, c_ref, dml_ref, nwp_ref, mp_ref, cc_ref, r_ref, w_ref,
             gb_ref, gc_ref, gnw_ref, misc_ref, gwa_ref):
        dmc = jnp.sum(dmc_ref[...], axis=0, keepdims=True)
        gb_ref[...] = jnp.sum(dm_ref[...], axis=0, keepdims=True) + dmc
        dsc = _dot(jnp.broadcast_to(dmc, (8, 3 * D)).astype(BF16), w_ref[...], NT)[0:1, :]
        gc_ref[...] = dsc * _dsilu(cc_ref[...])
        gnw_ref[...] = jnp.sum(nwp_ref[...], axis=0, keepdims=True)
        misc = jnp.sum(mp_ref[...], axis=0, keepdims=True)
        y = jnp.exp2(r_ref[...])
        lane = lax.broadcasted_iota(jnp.int32, (1, D), 1)
        is_decay = jnp.logical_and(lane >= 2 * HD, lane < 2 * HD + 2 * RH)
        misc_ref[...] = misc * jnp.where(is_decay, -(y * np.float32(np.log(2.0))) / (1.0 - y), 1.0)
        gwa_ref[...] = _dot(_silu(c_ref[...]).astype(BF16), dml_ref[...].astype(BF16), TN)

    return pl.pallas_call(
        body, name="small_final",
        out_shape=[_sds((1, 3 * D), F32), _sds((1, D), F32), _sds((1, D), F32), _sds((1, D), F32), _sds((D, loc), F32)],
        compiler_params=pltpu.CompilerParams(vmem_limit_bytes=VMEM_LIMIT),
    )(dmod_all, dmodc_parts, c_rows, dm_loc_rows, nw_parts, misc_parts, c_ctx, r_pad, w_ada16)


def _adamw_math(w, g, m, v):
    nm = B1 * m + (1.0 - B1) * g
    nv = B2 * v + (1.0 - B2) * (g * g)
    return -LR * ((nm / (1.0 - B1 ** STEP)) / (jnp.sqrt(nv / (1.0 - B2 ** STEP)) + ADAM_EPS) + WD * w), nm, nv


def _adamw(w, g, m, v, name):
    rows, cols = w.shape
    tm = _pick(rows, 448, 8)

    def body(w_ref, g_ref, m_ref, v_ref, d_ref, nm_ref, nv_ref):
        d_ref[...], nm_ref[...], nv_ref[...] = _adamw_math(w_ref[...], g_ref[...], m_ref[...], v_ref[...])

    blk = pl.BlockSpec((tm, cols), lambda i: (i, 0))
    return pl.pallas_call(
        body, name=name, grid=(rows // tm,), in_specs=[blk] * 4, out_specs=[blk] * 3,
        out_shape=[_sds((rows, cols), F32)] * 3, compiler_params=_params(("parallel",)),
    )(w, g, m, v)


def _adamw_small(wgmv):
    n = len(wgmv)

    def body(*refs):
        ins, outs = refs[:4 * n], refs[4 * n:]
        for k in range(n):
            w, g, m, v = [r[...] for r in ins[4 * k:4 * k + 4]]
            outs[3 * k][...], outs[3 * k + 1][...], outs[3 * k + 2][...] = _adamw_math(w, g, m, v)

    out = pl.pallas_call(
        body, name="adamw_small", out_shape=[_sds(t[0].shape, F32) for t in wgmv for _ in range(3)],
    )(*[a for t in wgmv for a in t])
    return [out[3 * k:3 * k + 3] for k in range(n)]


def _mesh_pos():
    return lax.axis_index("x"), lax.axis_index("y"), lax.axis_index("c")


def _all_gather(arrs, name):
    n = len(arrs)

    def body(*refs):
        ins, outs = refs[:n], refs[n:2 * n]
        send_sems, recv_sems, local_sems = refs[2 * n:]
        x, y, c = _mesh_pos()
        me, sib = (x, y, c), (x, y, 1 - c)
        chips = [(1 - x, y), (x, 1 - y), (1 - x, 1 - y)]

        def slot(p):
            return 4 * p[0] + 2 * p[1] + p[2]

        def copy(a, k, block, to, own):
            dst = outs[a].at[slot(block)]
            return pltpu.make_async_remote_copy(
                src_ref=ins[a] if own else dst, dst_ref=dst, send_sem=send_sems.at[a, k], recv_sem=recv_sems.at[a, k],
                device_id=to, device_id_type=MESH_T)

        mine = [pltpu.make_async_copy(ins[a], outs[a].at[slot(me)], local_sems.at[a]) for a in range(n)]
        for cp in mine:
            cp.start()
        first = []
        for a in range(n):
            first.append(copy(a, 0, me, sib, True))
            first += [copy(a, 1 + j, me, (*chip, c), True) for j, chip in enumerate(chips)]
        for cp in first:
            cp.start()
        passed = []
        for j, chip in enumerate(chips):
            for a in range(n):
                copy(a, 1 + j, (*chip, c), me, False).wait_recv()
                fwd = copy(a, 4 + j, (*chip, c), sib, False)
                fwd.start()
                passed.append(fwd)
        for a in range(n):
            copy(a, 0, sib, me, False).wait_recv()
            for j, chip in enumerate(chips):
                copy(a, 4 + j, (*chip, 1 - c), me, False).wait_recv()
        for cp in first + passed:
            cp.wait_send()
        for cp in mine:
            cp.wait()

    hbm = pl.BlockSpec(memory_space=pl.ANY)
    return pl.pallas_call(
        body, name=name, in_specs=[hbm] * n, out_specs=[hbm] * n,
        out_shape=[_sds((N_DEV,) + a.shape, a.dtype) for a in arrs],
        scratch_shapes=[pltpu.SemaphoreType.DMA((n, 7)), pltpu.SemaphoreType.DMA((n, 7)), pltpu.SemaphoreType.DMA((n,))],
    )(*arrs)


def _pair_add(parts, gots, core, name):
    n = len(parts)
    cols = parts[0].shape[2]
    tiles = min(p.shape[1] for p in parts) // _pick(min(p.shape[1] for p in parts), 672, 16)

    def body(core_ref, *refs):
        for p_ref, g_ref, o_ref in zip(refs[:n], refs[n:2 * n], refs[2 * n:]):
            o_ref[...] = (p_ref[...].astype(F32) + g_ref[...].astype(F32)).astype(BF16)

    def blk(p):
        return pl.BlockSpec((None, p.shape[1] // tiles, cols), lambda k, i, cr: (k, i, 0))

    return pl.pallas_call(
        body, name=name,
        grid_spec=pltpu.PrefetchScalarGridSpec(
            num_scalar_prefetch=1, grid=(4, tiles),
            in_specs=[pl.BlockSpec((None, None, p.shape[1] // tiles, cols), lambda k, i, cr: (k, cr[0], i, 0))
                      for p in parts] + [blk(p) for p in parts],
            out_specs=[blk(p) for p in parts]),
        out_shape=[_sds((4,) + p.shape[1:], BF16) for p in parts], compiler_params=_params(("parallel", "parallel")),
    )(core, *[p.reshape(4, 2, *p.shape[1:]) for p in parts], *gots)


def _chip_sum_adamw(pair_sums, landed, chip, w, m, v, name):
    _, rows, cols = pair_sums.shape
    tm = _pick(rows, 448, 16)

    def body(chip_ref, s_ref, l_ref, w_ref, m_ref, v_ref, g_ref, d_ref, nm_ref, nv_ref):
        acc = s_ref[...].astype(F32)
        for j in range(3):
            acc = acc + l_ref[j].astype(F32)
        g_ref[...] = acc
        d_ref[...], nm_ref[...], nv_ref[...] = _adamw_math(w_ref[...], acc, m_ref[...], v_ref[...])

    blk = pl.BlockSpec((tm, cols), lambda i, ch: (i, 0))
    return pl.pallas_call(
        body, name=name,
        grid_spec=pltpu.PrefetchScalarGridSpec(
            num_scalar_prefetch=1, grid=(rows // tm,),
            in_specs=[pl.BlockSpec((None, tm, cols), lambda i, ch: (ch[0], i, 0)),
                      pl.BlockSpec((3, tm, cols), lambda i, ch: (0, i, 0)), blk, blk, blk],
            out_specs=[blk] * 4),
        out_shape=[_sds((rows, cols), F32)] * 4, compiler_params=_params(("parallel",)),
    )(chip, pair_sums, landed, w, m, v)


_HBM = pl.BlockSpec(memory_space=pltpu.HBM)
_SEM = pl.BlockSpec(memory_space=pltpu.SEMAPHORE)
_EFFECT = pltpu.SideEffectType.DATAFLOW_SIDE_EFFECTING


def _chip_routes(n):
    def plan(x, y, c):
        routes = []
        for a in range(n):
            for j in range(1, 4):
                px, py = x ^ (j >> 1), y ^ (j & 1)
                routes.append((a, 2 * px + py, (px, py, c), j - 1))
        return routes
    return plan, 3 * n


def _pair_routes(n):
    def plan(x, y, c):
        return [(a, 2 * k + 1 - c, (x, y, 1 - c), k) for a in range(n) for k in range(4)]
    return plan, 4 * n


def _bcast_routes(n):
    def plan(x, y, c):
        routes = []
        for a in range(n):
            for k in range(1, N_DEV):
                peer = (x ^ ((k >> 2) & 1), y ^ ((k >> 1) & 1), c ^ (k & 1))
                routes.append((a, 0, peer, 4 * x + 2 * y + c))
        return routes
    return plan, 7 * n


def _route_copies(srcs, lands, send_sems, recv_sems, routes):
    return [pltpu.make_async_remote_copy(
        src_ref=srcs[a].at[sb], dst_ref=lands[a].at[lb], send_sem=send_sems.at[r], recv_sem=recv_sems.at[r],
        device_id=peer, device_id_type=MESH_T) for r, (a, sb, peer, lb) in enumerate(routes)]


def _exchange_start(srcs, lands, routes, name, after=()):
    plan, count = routes
    n = len(srcs)
    n_in = 2 * n + len(after)

    def body(*refs):
        send_sems, recv_sems = refs[n_in], refs[n_in + 1]
        token = refs[-1]
        for cp in _route_copies(refs[:n], refs[n:2 * n], send_sems, recv_sems, plan(*_mesh_pos())):
            cp.start()
        token[...] = jnp.zeros_like(token)

    args = [pltpu.with_memory_space_constraint(a, pltpu.HBM) for a in list(srcs) + list(lands)]
    out = pl.pallas_call(
        body, name=name,
        out_shape=(pltpu.SemaphoreType.DMA((count,)), pltpu.SemaphoreType.DMA((count,)),
                   *[pltpu.HBM(a.shape, a.dtype) for a in args], _sds((8, 128), F32)),
        in_specs=[_HBM] * (2 * n) + [pl.BlockSpec(memory_space=pl.ANY)] * len(after),
        out_specs=(_SEM, _SEM, *([_HBM] * (2 * n)), pl.BlockSpec(memory_space=pltpu.VMEM)),
        input_output_aliases={i: 2 + i for i in range(2 * n)},
        compiler_params=pltpu.CompilerParams(has_side_effects=_EFFECT),
    )(*args, *after)
    return (out[0], out[1], list(out[2:2 + 2 * n]), routes), out[-1]


def _exchange_wait(state, after, name):
    send_sems, recv_sems, bufs, (plan, count) = state
    n = len(bufs) // 2

    def body(*refs):
        send_s, recv_s = refs[2 * n], refs[2 * n + 1]
        for cp in _route_copies(refs[:n], refs[n:2 * n], send_s, recv_s, plan(*_mesh_pos())):
            cp.wait_send()
            cp.wait_recv()

    out = pl.pallas_call(
        body, name=name, out_shape=tuple(pltpu.HBM(a.shape, a.dtype) for a in bufs),
        in_specs=[_HBM] * (2 * n) + [_SEM, _SEM, pl.BlockSpec(memory_space=pl.ANY)], out_specs=tuple([_HBM] * (2 * n)),
        input_output_aliases={i: i for i in range(2 * n)},
        compiler_params=pltpu.CompilerParams(has_side_effects=_EFFECT),
    )(*bufs, send_sems, recv_sems, after)
    return list(out[:n]), list(out[n:])


def _group_routes(js):
    def plan(x, y, c):
        return [(0, 0, (x ^ (j >> 1), y ^ (j & 1), c), 2 * j + c) for j in js]
    return plan, len(js)


def _pair_fill(groups, js, name, after=()):
    def body(*refs):
        g_ref, send_sems, recv_sems = refs[-3:]
        x, y, c = _mesh_pos()
        sends = []
        for n, j in enumerate(js):
            mine = g_ref.at[2 * j + c]
            sends.append(pltpu.make_async_remote_copy(
                src_ref=mine, dst_ref=mine, send_sem=send_sems.at[n], recv_sem=recv_sems.at[n],
                device_id=(x, y, 1 - c), device_id_type=MESH_T))
        for cp in sends:
            cp.start()
        for n, j in enumerate(js):
            pltpu.make_async_remote_copy(
                src_ref=g_ref.at[2 * j + c], dst_ref=g_ref.at[2 * j + 1 - c], send_sem=send_sems.at[n],
                recv_sem=recv_sems.at[n], device_id=(x, y, 1 - c), device_id_type=MESH_T).wait_recv()
        for cp in sends:
            cp.wait_send()

    hbm = pl.BlockSpec(memory_space=pl.ANY)
    return pl.pallas_call(
        body, name=name, in_specs=[hbm] * (1 + len(after)), out_specs=hbm, out_shape=_sds(groups.shape, groups.dtype),
        input_output_aliases={0: 0},
        scratch_shapes=[pltpu.SemaphoreType.DMA((len(js),)), pltpu.SemaphoreType.DMA((len(js),))],
    )(groups, *after)


def _in_proj_group(h_all, groups, j0, ng, chip, px_prev, after, name):
    rows_all = h_all.shape[0]
    gcols = IN_COLS // 4
    tm = _pick(rows_all, 1536, 128)
    g4 = groups.reshape(4, gcols, D)

    n_lead = (1 if px_prev is not None else 0) + len(after)
    lead = ([px_prev] if px_prev is not None else []) + list(after)

    def body(chip_ref, *refs):
        h_ref, w_ref, o_ref = refs[n_lead:]
        o_ref[...] = _dot(h_ref[...], w_ref[...], NT).astype(BF16)

    return pl.pallas_call(
        body, name=name,
        grid_spec=pltpu.PrefetchScalarGridSpec(
            num_scalar_prefetch=1, grid=(ng, rows_all // tm),
            in_specs=[pl.BlockSpec(memory_space=pl.ANY)] * n_lead
            + [pl.BlockSpec((tm, D), lambda n, i, ch: (i, 0)),
               pl.BlockSpec((None, gcols, D), lambda n, i, ch: (j0 + n, 0, 0))],
            out_specs=pl.BlockSpec((tm, gcols), lambda n, i, ch: (i, ch[0] ^ (j0 + n)))),
        out_shape=_sds((rows_all, IN_COLS), BF16),
        input_output_aliases={1: 0} if px_prev is not None else {},
        compiler_params=_params(("parallel", "parallel")),
    )(chip, *lead, h_all, g4)


def _d_h_groups(dp_all, groups, chip, i0, ni, dh_prev, after):
    rows_all = dp_all.shape[0]
    gcols = IN_COLS // 4
    tm = _D_H_ROWS
    g4 = groups.reshape(4, gcols, D)
    lead = ([dh_prev] if dh_prev is not None else []) + list(after)
    n_lead = len(lead)

    def body(chip_ref, *refs):
        a_ref, w_ref, o_ref = refs[n_lead:]
        j = pl.program_id(1)
        part = _dot(a_ref[...], w_ref[...])

        @pl.when(j == 0)
        def _():
            o_ref[...] = part

        @pl.when(j > 0)
        def _():
            o_ref[...] += part

    return pl.pallas_call(
        body, name="d_h_%d" % i0,
        grid_spec=pltpu.PrefetchScalarGridSpec(
            num_scalar_prefetch=1, grid=(ni, 4),
            in_specs=[pl.BlockSpec(memory_space=pl.ANY)] * n_lead
            + [pl.BlockSpec((tm, gcols), lambda i, j, ch: (i0 + i, ch[0] ^ j)),
               pl.BlockSpec((None, gcols, D), lambda i, j, ch: (j, 0, 0))],
            out_specs=pl.BlockSpec((tm, D), lambda i, j, ch: (i0 + i, 0))),
        out_shape=_sds((rows_all, D), F32),
        input_output_aliases={1: 0} if dh_prev is not None else {},
        compiler_params=_params(("parallel", "arbitrary")),
    )(chip, *lead, dp_all, g4)


def _reduce_scatter_send(parts, got, core, name):
    sums = _pair_add(parts, got, core, name + "_add")
    lands = [lax.empty((3,) + s_.shape[1:], BF16) for s_ in sums]
    return _exchange_start(sums, lands, _chip_routes(len(sums)), name + "_start")


def _reduce_scatter_finish(rs_state, after, chip, wmv, name):
    sums, landed = _exchange_wait(rs_state, after, name + "_wait")
    return [_chip_sum_adamw(s_, l_, chip, *t, "%s_adamw_%d" % (name, i))
            for i, (s_, l_, t) in enumerate(zip(sums, landed, wmv))]


def _local_step(x, c, ctx, norm_w, ret_log2_decay, q_norm_w, k_norm_w, loss_target,
                mod, proj_in, get_w_o, on_out_grads, on_in_grad, started=()):
    nb, seq, _ = x.shape
    cx = ctx.shape[1]
    t_rows, c_rows = nb * seq, nb * cx
    rows_all = t_rows + c_rows
    nc = seq // CH
    tm = _pick(seq, 256, 128)
    te = _pick(seq, 512, 128)
    assert cx % tm == 0 and t_rows % cx == 0 and seq % GRID_W == 0

    x2 = x.reshape(t_rows, D)
    ctx2 = ctx.reshape(c_rows, D)
    tgt = loss_target.reshape(t_rows, D)
    lg = _log_gamma(ret_log2_decay)
    cos, sin = _rope_tables(seq)

    mod3 = mod[:, None, :]
    h_all = _norm_fwd(x2, mod3, norm_w, rows_all, 0, seq, 0, None, te, "norm_fwd", after=started)
    h_all = _norm_fwd(ctx2, mod3, norm_w, rows_all, t_rows, c_rows, nb, h_all, tm, "norm_fwd_ctx")
    px = proj_in(h_all)
    o_f, o_b, hist_f, hist_b = _ret_fwd(px, lg, nb, nc, cx)
    q16 = _qk_prep(px, q_norm_w, cos, sin, t_rows, 0, AQ, HQ, 4, seq, te, "q_prep")
    kx16 = _qk_prep(px, k_norm_w, cos, sin, t_rows, 0, AK, HKV, HKV, seq, te, "k_prep")
    kc16 = _qk_prep(px, k_norm_w, None, None, c_rows, t_rows, AK, HKV, HKV, seq, tm, "kc_prep")
    o_att, yatt16, lse = _att_fwd(q16, kx16, kc16, px, nb, seq, cx, te)
    w_o_ret16, w_o_att16, w_out16 = get_w_o(lse)
    yret16, a_ret, a_att, y16, dxn, dout16, dgate, loss_b = _merge_out(
        o_f, o_b, yatt16, px, w_o_ret16, w_o_att16, w_out16, x2, tgt, mod3, nb, seq, tm)

    gw_out = _matmul(y16, dout16, ta=True, tm=D, tn=D, tk=D, out_dtype=BF16, name="gw_out")
    da_ret16, da_att16, do16, dao16, delta, dp_all = _bwd_branches(
        dout16, w_out16, w_o_ret16, w_o_att16, px, a_ret, a_att, o_f, o_b, o_att, rows_all, tm)
    gw_o_ret = _matmul(yret16, da_ret16, ta=True, tm=D, tn=D, tk=D, out_dtype=BF16, name="gw_o_ret")
    gw_o_att = _matmul(yatt16, da_att16, ta=True, tm=D, tn=D, tk=D, out_dtype=BF16, name="gw_o_att")
    out_send, out_started = on_out_grads([gw_o_ret, gw_o_att, gw_out])
    dp_all, gq, dkx, dvx, dkc, dvc = _att_bwd(q16, kx16, kc16, px, dao16, delta, lse, q_norm_w, cos, sin, dp_all, nb,
                                              seq, cx, te, after=out_started)
    out_state, out_sent = out_send(gq)
    dak16, gk_lat = _qk_prep_bwd(dkx.reshape(t_rows, HKV * HD), px, k_norm_w, cos, sin, t_rows, 0, AK, HKV, HKV, seq, te,
                                 "k_prep_bwd")
    dcak16, gk_ctx = _qk_prep_bwd(dkc.reshape(c_rows, HKV * HD), px, k_norm_w, None, None, c_rows, t_rows, AK, HKV, HKV,
                                  seq, tm, "kc_prep_bwd")
    dq_f, dk_f, dv_f, dq_b, dk_b, dv_b, dck16, dcv16, dlg_scan = _ret_bwd(px, lg, do16, hist_f, hist_b, nb, nc, cx)
    dp_all = _assemble_lat(dp_all, dk_f, dk_b, dv_f, dv_b, dak16, dvx.reshape(t_rows, HKV * HD), dq_f, dq_b, tm)
    dp_all = _assemble_ctx(dp_all, dck16, dcv16, dcak16, dvc.reshape(c_rows, HKV * HD), t_rows, tm)
    gw_in_t = _matmul(dp_all, h_all, ta=True, tm=1536, tn=D, tk=2304, out_dtype=BF16, name="gw_in", after=out_sent)
    in_state, dh = on_in_grad(gw_in_t, dp_all)
    grad_x, dsh, dsc, gnw_lat = _norm_bwd(dh, x2, mod3, norm_w, dxn, 0, seq, 0, te, "norm_bwd")
    dsh_c, dsc_c, gnw_ctx = _norm_bwd(dh, ctx2, mod3, norm_w, None, t_rows, c_rows, nb, tm, "norm_bwd_ctx")

    dlg = jnp.sum(dlg_scan[:, :, 0], axis=0).reshape(1, 2 * RH)
    misc = jnp.concatenate([gq, gk_lat + gk_ctx, dlg, jnp.sum(loss_b[:, 0, 0]).reshape(1, 1),
                            jnp.zeros((1, D - 2 * HD - 2 * RH - 1), F32)], axis=1)
    rows = []
    for b in range(nb):
        rows += [dsh[b], dsc[b], dgate[b]]
    rows += [dsh_c[0], dsc_c[0]] + [c[b:b + 1] for b in range(nb)] + [gnw_lat + gnw_ctx, misc]
    payload = jnp.concatenate(rows + [jnp.zeros((PAY_ROWS - len(rows), D), F32)], axis=0)
    return grad_x.reshape(nb, seq, D), out_state, in_state, payload


def _finish_small(gathered, nb, c_ctx, ret_log2_decay, w_ada16, dev):
    n_dev = gathered.shape[0]
    loc = 3 * D // n_dev
    dmod_all = gathered[:, :3 * nb].reshape(n_dev * nb, 3 * D)
    dmodc_parts = jnp.concatenate([gathered[:, 3 * nb:3 * nb + 2].reshape(n_dev, 2 * D), jnp.zeros((n_dev, D), F32)], axis=1)
    c_all = gathered[:, 3 * nb + 2:4 * nb + 2].reshape(n_dev * nb, D)
    nw_parts = gathered[:, 4 * nb + 2]
    misc_parts = gathered[:, 4 * nb + 3]
    n_rows = n_dev * nb + n_dev
    pad = (-n_rows) % 16
    c_rows = jnp.concatenate([c_all, jnp.broadcast_to(c_ctx.reshape(1, D), (n_dev, D)), jnp.zeros((pad, D), F32)], axis=0)
    dm_rows = jnp.concatenate([dmod_all, dmodc_parts, jnp.zeros((pad, 3 * D), F32)], axis=0)
    dm_loc_rows = lax.dynamic_slice_in_dim(dm_rows, dev * loc, loc, axis=1)
    r_pad = jnp.full((1, D), -1.0, F32).at[:, 2 * HD:2 * HD + 2 * RH].set(ret_log2_decay.reshape(1, 2 * RH))
    gb, gc, gnw, misc, gwa = _small_final(dmod_all, dmodc_parts, c_rows, dm_loc_rows, nw_parts, misc_parts,
                                          c_ctx.reshape(1, D), r_pad, w_ada16)
    return (gb, gc, gnw, misc[:, :HD], misc[:, HD:2 * HD], misc[:, 2 * HD:2 * HD + 2 * RH], gwa,
            misc[0, 2 * HD + 2 * RH])


def kernel(x, c, ctx, c_ctx, norm_w, w_ada, b_ada, w_in, ret_log2_decay, q_norm_w, k_norm_w, w_o_ret, w_o_att, w_out, loss_target, m_c_ctx, m_norm_w, m_w_ada, m_b_ada, m_w_in, m_ret_log2_decay, m_q_norm_w, m_k_norm_w, m_w_o_ret, m_w_o_att, m_w_out, v_c_ctx, v_norm_w, v_w_ada, v_b_ada, v_w_in, v_ret_log2_decay, v_q_norm_w, v_k_norm_w, v_w_o_ret, v_w_o_att, v_w_out):
    nb = x.shape[0]
    mx, my, mc = _mesh_pos()
    dev = 4 * mx + 2 * my + mc
    core = jnp.reshape(mc, (1,)).astype(jnp.int32)
    chip = jnp.reshape(2 * mx + my, (1,)).astype(jnp.int32)

    n_loc = 3 * D // N_DEV
    c8 = jnp.zeros((8, D), F32).at[:nb].set(c).at[nb].set(c_ctx)
    c_land = lax.dynamic_update_slice(lax.empty((N_DEV, 8, D), F32), c8[None], (dev, 0, 0))
    c_state, c_token = _exchange_start([c8[None]], [c_land], _bcast_routes(1), "gather_c_start")
    w_in_t = jnp.transpose(w_in[0])
    in_shard = w_in_t.astype(BF16)
    groups = lax.dynamic_update_slice(lax.empty((N_DEV,) + in_shard.shape, BF16), in_shard[None], (mc, 0, 0))
    groups = _pair_fill(groups, (0,), "gather_in_pair", after=(c_token,))
    _, (c_all,) = _exchange_wait(c_state, groups, "gather_c_wait")
    ada_shard = w_ada[0].astype(BF16)
    b_loc = lax.dynamic_slice(b_ada, (0, dev * n_loc), (1, n_loc))
    mod_cols = _mod_part(c_all.reshape(N_DEV * 8, D), ada_shard, b_loc)
    (mod_all,) = _all_gather([mod_cols], "gather_mod")
    mod = jnp.transpose(lax.dynamic_slice(mod_all, (0, dev * 8, 0), (N_DEV, 8, n_loc)), (1, 0, 2)).reshape(8, 3 * D)
    ada_land = lax.dynamic_update_slice(lax.empty((N_DEV,) + ada_shard.shape, BF16), ada_shard[None], (dev, 0, 0))

    (near_send, near_recv, near_bufs, near_routes), gin_token = _exchange_start(
        [in_shard[None]], [groups], _group_routes((1, 2)), "gather_in_start", after=(mod_all,))
    w_in_groups, wo_states, ada_states = [], [], []
    wo_shards = [w_[0].astype(BF16) for w_ in (w_o_ret, w_o_att, w_out)]
    wo_lands = [lax.dynamic_update_slice(lax.empty((N_DEV,) + s_.shape, BF16), s_[None], (dev, 0, 0)) for s_ in wo_shards]

    def _state(send, recv, src, groups, routes):
        return send, recv, [src, groups], routes

    def proj_in(h_all):
        src, groups = near_bufs
        px = _in_proj_group(h_all, groups, 0, 1, chip, None, (gin_token,), "in_proj_0")
        (src,), (groups,) = _exchange_wait(_state(near_send, near_recv, src, groups, near_routes), px,
                                           "gather_in_wait_near")
        groups = _pair_fill(groups, (1, 2), "gather_in_fill_near")
        (far_send, far_recv, (src, groups), far_routes), far_token = _exchange_start(
            [src], [groups], _group_routes((3,)), "gather_in_start_far")
        wo_state, wo_token = _exchange_start([s_[None] for s_ in wo_shards], wo_lands, _bcast_routes(3),
                                             "gather_wo_start", after=(far_token,))
        wo_states.append(wo_state)
        ada_state, ada_token = _exchange_start([ada_shard[None]], [ada_land], _bcast_routes(1), "gather_ada_start",
                                               after=(wo_token,))
        ada_states.append(ada_state)
        px = _in_proj_group(h_all, groups, 1, 2, chip, px, (ada_token,), "in_proj_near")
        (src,), (groups,) = _exchange_wait(_state(far_send, far_recv, src, groups, far_routes), px,
                                           "gather_in_wait_far")
        groups = _pair_fill(groups, (3,), "gather_in_fill_far")
        px = _in_proj_group(h_all, groups, 3, 1, chip, px, (), "in_proj_far")
        w_in_groups.append(groups)
        return px

    def get_w_o(after):
        _, (l_ret, l_att, l_out) = _exchange_wait(wo_states[0], after, "gather_wo_wait")
        return l_ret.reshape(RH * DV, D), l_att.reshape(D, D), l_out.reshape(D, D)

    def on_out_grads(grads):
        parts = [g_.reshape(N_DEV, g_.shape[0] // N_DEV, D) for g_ in grads]
        lands = [lax.empty((4,) + p_.shape[1:], BF16) for p_ in parts]
        pair_state, pair_token = _exchange_start(parts, lands, _pair_routes(len(parts)), "rs_out_pair_start")

        def send(after):
            parts_, got = _exchange_wait(pair_state, after, "rs_out_pair_wait")
            state, token = _reduce_scatter_send(parts_, got, core, "rs_out")
            return state, (token,)

        return send, (pair_token,)

    def on_in_grad(grad, dp_all):
        parts = [grad.reshape(N_DEV, IN_COLS // N_DEV, D)]
        lands = [lax.empty((4,) + p_.shape[1:], BF16) for p_ in parts]
        pair_state, pair_token = _exchange_start(parts, lands, _pair_routes(1), "rs_in_pair_start")
        dh = _d_h_groups(dp_all, w_in_groups[0], chip, 0, 1, None, (pair_token,))
        parts, got = _exchange_wait(pair_state, dh, "rs_in_pair_wait")
        state, token = _reduce_scatter_send(parts, got, core, "rs_in")
        n_tiles = dp_all.shape[0] // _D_H_ROWS
        return state, _d_h_groups(dp_all, w_in_groups[0], chip, 1, n_tiles - 1, dh, (token,))

    grad_x, out_state, in_state, payload = _local_step(
        x, c, ctx, norm_w, ret_log2_decay, q_norm_w, k_norm_w, loss_target,
        mod, proj_in, get_w_o, on_out_grads, on_in_grad, started=(gin_token,))

    pay_land = lax.dynamic_update_slice(lax.empty((N_DEV,) + payload.shape, F32), payload[None], (dev, 0, 0))
    pay_state, pay_token = _exchange_start([payload[None]], [pay_land], _bcast_routes(1), "gather_small_start")

    out_res = _reduce_scatter_finish(out_state, pay_token, chip,
                                     [(w_[0], m_[0], v_[0]) for w_, m_, v_ in ((w_o_ret, m_w_o_ret, v_w_o_ret),
                                                                                (w_o_att, m_w_o_att, v_w_o_att),
                                                                                (w_out, m_w_out, v_w_out))], "rs_out")
    (in_res,) = _reduce_scatter_finish(in_state, out_res[0][0], chip,
                                       [(w_in_t, jnp.transpose(m_w_in[0]), jnp.transpose(v_w_in[0]))], "rs_in")

    _, (gathered,) = _exchange_wait(pay_state, in_res[0], "gather_small_wait")
    _, (l_ada,) = _exchange_wait(ada_states[0], gathered, "gather_ada_wait")
    w_ada16 = jnp.transpose(l_ada, (1, 0, 2)).reshape(D, 3 * D)
    gb, gc, gnw, gq, gk, gr, gwa, loss = _finish_small(gathered, nb, c_ctx, ret_log2_decay, w_ada16, dev)
    big = {4: [jnp.transpose(r)[None] for r in in_res]}
    for i, res in zip((8, 9, 10), out_res):
        big[i] = [r[None] for r in res]
    small_g = {0: gc.reshape(c_ctx.shape), 1: gnw, 2: gwa[None], 3: gb, 5: gr.reshape(ret_log2_decay.shape), 6: gq, 7: gk}
    weights = [c_ctx, norm_w, w_ada, b_ada, w_in, ret_log2_decay, q_norm_w, k_norm_w, w_o_ret, w_o_att, w_out]
    ms = [m_c_ctx, m_norm_w, m_w_ada, m_b_ada, m_w_in, m_ret_log2_decay, m_q_norm_w, m_k_norm_w, m_w_o_ret, m_w_o_att, m_w_out]
    vs = [v_c_ctx, v_norm_w, v_w_ada, v_b_ada, v_w_in, v_ret_log2_decay, v_q_norm_w, v_k_norm_w, v_w_o_ret, v_w_o_att, v_w_out]
    def rows2(i):
        return [a.reshape(-1, weights[i].shape[-1]) for a in (weights[i], small_g[i], ms[i], vs[i])]

    small_ids = [i for i in small_g if i != 2]
    steps = dict(zip(small_ids, _adamw_small([rows2(i) for i in small_ids])))
    steps[2] = _adamw(*rows2(2), "adamw_w_ada")
    grads, deltas, new_ms, new_vs = [], [], [], []
    for i, w in enumerate(weights):
        res = big[i] if i in big else [small_g[i]] + [r.reshape(w.shape) for r in steps[i]]
        for lst, r in zip((grads, deltas, new_ms, new_vs), res):
            lst.append(r)
    return (loss, grad_x, *grads, *deltas, *new_ms, *new_vs)
```

```python
import numpy as np
import jax
import jax.numpy as jnp
from jax import lax
from jax.experimental import pallas as pl
from jax.experimental.pallas import tpu as pltpu

F32 = jnp.float32
BF16 = jnp.bfloat16

D = 1024
RH, DK, DV, CH = 4, 256, 512, 256
HQ, HKV, HD = 8, 2, 128
GRID_W = 64
ROPE_THETA = 10000.0
EPS = 1e-6
RK, RV, AK, AV, RQ, RG, AQ, AG, MR, MA = 0, 1024, 3072, 3328, 3584, 4608, 6656, 7680, 8704, 9728
IN_COLS = 10752
KV_COLS = 3584
N_DEV = 8
LR, B1, B2, ADAM_EPS, WD, STEP = 0.001, 0.9, 0.999, 1e-08, 0.01, 10
PAY_ROWS = 16
VMEM_LIMIT = 56 * 1024 * 1024
_D_H_ROWS = 1536
MESH_T = pl.DeviceIdType.MESH

NT = (((1,), (1,)), ((), ()))
TN = (((0,), (0,)), ((), ()))
SM_C = (HD ** -0.5) * float(np.log2(np.e))


def _params(sem):
    return pltpu.CompilerParams(dimension_semantics=sem, vmem_limit_bytes=VMEM_LIMIT)


def _pick(n, target, mult=8):
    best = None
    for t in range(mult, min(n, target) + 1, mult):
        if n % t == 0:
            best = t
    return best or n


def _dot(a, b, dn=None):
    if dn is None:
        return jnp.dot(a, b, preferred_element_type=F32)
    return lax.dot_general(a, b, dn, preferred_element_type=F32)


def _sig(v):
    return jax.nn.sigmoid(v)


def _silu(v):
    return v * _sig(v)


def _dsilu(v):
    s = _sig(v)
    return s * (1.0 + v * (1.0 - s))


def _sds(shape, dtype):
    return jax.ShapeDtypeStruct(shape, dtype)


def _matmul(a, b, *, ta=False, tb=False, tm, tn, tk, out_dtype, name, after=()):
    m = a.shape[1] if ta else a.shape[0]
    kdim = a.shape[0] if ta else a.shape[1]
    n = b.shape[0] if tb else b.shape[1]
    tm, tn, tk = _pick(m, tm, 128), _pick(n, tn, 128), _pick(kdim, tk, 128)
    nk = kdim // tk
    dn = (((0 if ta else 1,), (1 if tb else 0,)), ((), ()))

    def body(a_ref, b_ref, *rest):
        o_ref, acc_ref = rest[-2:]
        k = pl.program_id(2)
        part = _dot(a_ref[...].astype(BF16), b_ref[...].astype(BF16), dn)
        if nk == 1:
            o_ref[...] = part.astype(o_ref.dtype)
        else:
            @pl.when(k == 0)
            def _():
                acc_ref[...] = part

            @pl.when(k > 0)
            def _():
                acc_ref[...] += part

            @pl.when(k == nk - 1)
            def _():
                o_ref[...] = acc_ref[...].astype(o_ref.dtype)

    a_spec = pl.BlockSpec((tk, tm), lambda i, j, k: (k, i)) if ta else pl.BlockSpec((tm, tk), lambda i, j, k: (i, k))
    b_spec = pl.BlockSpec((tn, tk), lambda i, j, k: (j, k)) if tb else pl.BlockSpec((tk, tn), lambda i, j, k: (k, j))
    return pl.pallas_call(
        body, name=name, grid=(m // tm, n // tn, nk),
        in_specs=[a_spec, b_spec] + [pl.BlockSpec(memory_space=pl.ANY)] * len(after),
        out_specs=pl.BlockSpec((tm, tn), lambda i, j, k: (i, j)), out_shape=_sds((m, n), out_dtype),
        scratch_shapes=[pltpu.VMEM((tm, tn) if nk > 1 else (8, 128), F32)],
        compiler_params=_params(("parallel", "parallel", "arbitrary")),
    )(a, b, *after)


def _log_gamma(r):
    rp = jnp.full((8, 128), -1.0, F32).at[:2, :RH].set(r.reshape(2, RH))

    def body(r_ref, o_ref):
        o_ref[...] = jnp.log1p(-jnp.exp2(r_ref[...]))

    out = pl.pallas_call(body, name="log_gamma", out_shape=_sds((8, 128), F32))(rp)
    return out[:2, :RH]


def _mod_part(c_rows, w_ada_loc16, b_loc):
    def body(c_ref, w_ref, b_ref, o_ref):
        o_ref[...] = _dot(_silu(c_ref[...]).astype(BF16), w_ref[...]) + b_ref[...]

    return pl.pallas_call(
        body, name="mod_part", out_shape=_sds((c_rows.shape[0], w_ada_loc16.shape[1]), F32),
    )(c_rows, w_ada_loc16, b_loc)


def _norm_fwd(x2, mod3, norm_w, rows_all, row_off, rows_per_group, group0, h_prev, tm, name, after=()):
    rows = x2.shape[0]
    rb0 = row_off // tm
    bpg = rows_per_group // tm

    def body(*refs):
        x_ref, sh_ref, sc_ref, nw_ref, o_ref = refs[-5:]
        xv = x_ref[...]
        r = lax.rsqrt(jnp.mean(xv * xv, axis=-1, keepdims=True) + EPS)
        o_ref[...] = ((xv * r) * nw_ref[...] * (1.0 + sc_ref[...]) + sh_ref[...]).astype(BF16)

    in_specs = [pl.BlockSpec((tm, D), lambda i: (i, 0)),
                pl.BlockSpec((None, 1, D), lambda i: (group0 + i // bpg, 0, 0)),
                pl.BlockSpec((None, 1, D), lambda i: (group0 + i // bpg, 0, 1)),
                pl.BlockSpec((1, D), lambda i: (0, 0))]
    in_specs = [pl.BlockSpec(memory_space=pl.ANY)] * len(after) + in_specs
    args = list(after) + [x2, mod3, mod3, norm_w]
    alias = {}
    if h_prev is not None:
        in_specs.insert(0, pl.BlockSpec(memory_space=pl.ANY))
        args.insert(0, h_prev)
        alias = {0: 0}
    return pl.pallas_call(
        body, name=name, grid=(rows // tm,), in_specs=in_specs,
        out_specs=pl.BlockSpec((tm, D), lambda i: (rb0 + i, 0)), out_shape=_sds((rows_all, D), BF16),
        input_output_aliases=alias, compiler_params=_params(("parallel",)),
    )(*args)


def _decays(lg, fwd):
    ii = lax.broadcasted_iota(jnp.int32, (CH, CH), 0)
    jj = lax.broadcasted_iota(jnp.int32, (CH, CH), 1)
    ri = lax.broadcasted_iota(jnp.int32, (CH, 1), 0).astype(F32)
    rel = (ii - jj) if fwd else (jj - ii)
    relf = jnp.maximum(rel, 0).astype(F32)
    mask = jnp.where(rel >= 0, jnp.exp(lg * relf), 0.0)
    qe = (ri + 1.0) if fwd else (CH - ri)
    ke = (CH - 1.0 - ri) if fwd else ri
    return mask, relf, jnp.exp(lg * qe), qe, jnp.exp(lg * ke), ke


def _wide_specs(rowf):
    return [pl.BlockSpec((CH, 2 * DK), lambda b, c: (rowf(b, c), RQ // (2 * DK))),
            pl.BlockSpec((CH, 2 * DK), lambda b, c: (rowf(b, c), RQ // (2 * DK) + 1)),
            pl.BlockSpec((CH, RH * DK), lambda b, c: (rowf(b, c), RK // (RH * DK))),
            pl.BlockSpec((CH, 2 * DV), lambda b, c: (rowf(b, c), RV // (2 * DV))),
            pl.BlockSpec((CH, 2 * DV), lambda b, c: (rowf(b, c), RV // (2 * DV) + 1))]


def _head_qkv(refs, h):
    q0, q1, k, v0, v1 = refs
    lo = h % 2
    q = (q0, q1)[h // 2][:, lo * DK:(lo + 1) * DK].astype(F32)
    kk = k[:, h * DK:(h + 1) * DK].astype(F32) * (DK ** -0.5)
    v16 = (v0, v1)[h // 2][:, lo * DV:(lo + 1) * DV].astype(BF16)
    return q, kk, v16


def _ctx_specs(t_rows, cx):
    rb = t_rows // cx
    return [pl.BlockSpec((cx, RH * DK), lambda b, c: (rb + b, RK // (RH * DK))),
            pl.BlockSpec((cx, 2 * DV), lambda b, c: (rb + b, RV // (2 * DV))),
            pl.BlockSpec((cx, 2 * DV), lambda b, c: (rb + b, RV // (2 * DV) + 1))]


def _ctx_kv(refs, h):
    k, v0, v1 = refs
    kk = k[:, h * DK:(h + 1) * DK].astype(F32) * (DK ** -0.5)
    lo = h % 2
    return kk, (v0, v1)[h // 2][:, lo * DV:(lo + 1) * DV].astype(BF16)


def _ret_fwd(px, lg, nb, nc, cx):
    t_rows = nb * nc * CH

    def body(lg_ref, *refs):
        ins = (refs[0:5], refs[5:10])
        ctx_refs = refs[10:13]
        of_ref, ob_ref, hf_ref, hb_ref, sf, sb = refs[13:]
        c = pl.program_id(1)

        @pl.when(c == 0)
        def _():
            pos = lax.broadcasted_iota(jnp.int32, (cx, 1), 0).astype(F32)
            for h in range(RH):
                k, v16 = _ctx_kv(ctx_refs, h)
                sf[h] = _dot((k * jnp.exp(lg_ref[0, h] * (cx - 1.0 - pos))).astype(BF16), v16, TN)
                sb[h] = _dot((k * jnp.exp(lg_ref[1, h] * pos)).astype(BF16), v16, TN)

        for d, (o_ref, h_ref, s) in enumerate(((of_ref, hf_ref, sf), (ob_ref, hb_ref, sb))):
            for h in range(RH):
                lg_d = lg_ref[d, h]
                mask, _, qd, _, kd, _ = _decays(lg_d, d == 0)
                q, k, v16 = _head_qkv(ins[d], h)
                a = _dot(q.astype(BF16), k.astype(BF16), NT)
                st = s[h]
                st16 = st.astype(BF16)
                h_ref[h] = st16
                o = _dot((a * mask).astype(BF16), v16) + _dot((q * qd).astype(BF16), st16)
                o_ref[:, h * DV:(h + 1) * DV] = o.astype(BF16)
                s[h] = st * jnp.exp(lg_d * CH) + _dot((k * kd).astype(BF16), v16, TN)

    def fw(b, c):
        return b * nc + c

    def bw(b, c):
        return b * nc + nc - 1 - c

    in_specs = [pl.BlockSpec(memory_space=pltpu.SMEM)] + _wide_specs(fw) + _wide_specs(bw) + _ctx_specs(t_rows, cx)
    out_specs = [pl.BlockSpec((CH, RH * DV), lambda b, c: (fw(b, c), 0)),
                 pl.BlockSpec((CH, RH * DV), lambda b, c: (bw(b, c), 0)),
                 pl.BlockSpec((None, None, RH, DK, DV), lambda b, c: (b, c, 0, 0, 0)),
                 pl.BlockSpec((None, None, RH, DK, DV), lambda b, c: (b, nc - 1 - c, 0, 0, 0))]
    return pl.pallas_call(
        body, name="ret_fwd", grid=(nb, nc), in_specs=in_specs, out_specs=out_specs,
        out_shape=[_sds((t_rows, RH * DV), BF16)] * 2 + [_sds((nb, nc, RH, DK, DV), BF16)] * 2,
        scratch_shapes=[pltpu.VMEM((RH, DK, DV), F32), pltpu.VMEM((RH, DK, DV), F32)],
        compiler_params=_params(("parallel", "arbitrary")),
    )(lg, *([px] * 13))


def _rope_tables(seq):
    rows = seq // GRID_W
    row = np.repeat(np.arange(rows, dtype=np.float32), GRID_W)
    col = np.tile(np.arange(GRID_W, dtype=np.float32), rows)
    half = HD // 2
    freqs = (ROPE_THETA ** (-np.arange(0, half, 2, dtype=np.float32) / half)).astype(np.float32)
    ang = np.concatenate([row[:, None] * freqs, col[:, None] * freqs], axis=-1).astype(np.float32)
    cos = np.repeat(np.cos(ang), 2, axis=-1).astype(np.float32)
    sin = np.repeat(np.sin(ang), 2, axis=-1).astype(np.float32)
    sign = np.tile(np.array([-1.0, 1.0], np.float32), HD // 2)
    return jnp.asarray(cos), jnp.asarray(sin * sign)


def _swap_pairs(v):
    lane = lax.broadcasted_iota(jnp.int32, v.shape, 1)
    return jnp.where((lane & 1) == 0, pltpu.roll(v, HD - 1, 1), pltpu.roll(v, 1, 1))


def _qk_prep(px, nw, cos, sin, rows, row_off, col_off, heads, hb, seq, tm, name):
    rope = cos is not None
    rb0 = row_off // tm
    pb = seq // tm if rope else 1
    bw = hb * HD

    def body(*refs):
        if rope:
            x_ref, w_ref, c_ref, s_ref, o_ref = refs
        else:
            x_ref, w_ref, o_ref = refs
        for h in range(hb):
            sl = slice(h * HD, (h + 1) * HD)
            xv = x_ref[:, sl].astype(F32)
            r = lax.rsqrt(jnp.mean(xv * xv, axis=-1, keepdims=True) + EPS)
            t = (xv * r) * w_ref[...]
            if rope:
                t = t * c_ref[...] + _swap_pairs(t) * s_ref[...]
            o_ref[:, sl] = t.astype(BF16)

    in_specs = [pl.BlockSpec((tm, bw), lambda i, j: (rb0 + i, col_off // bw + j)),
                pl.BlockSpec((1, HD), lambda i, j: (0, 0))]
    args = [px, nw]
    if rope:
        in_specs += [pl.BlockSpec((tm, HD), lambda i, j: (i % pb, 0))] * 2
        args += [cos, sin]
    return pl.pallas_call(
        body, name=name, grid=(rows // tm, heads // hb), in_specs=in_specs,
        out_specs=pl.BlockSpec((tm, bw), lambda i, j: (i, j)), out_shape=_sds((rows, heads * HD), BF16),
        compiler_params=_params(("parallel", "parallel")),
    )(*args)


def _att_fwd(q16, kx16, kc16, px, nb, seq, cx, tq):
    t_rows = nb * seq
    nq = seq // tq
    rep = HQ // HKV
    gw = rep * HD

    def body(q_ref, kx_ref, kc_ref, vx_ref, vc_ref, g_ref, o_ref, y_ref, l_ref):
        kx = kx_ref[...]
        kc = kc_ref[...]
        vx = vx_ref[...].astype(BF16)
        vc = vc_ref[...].astype(BF16)
        l_ref[...] = jnp.zeros_like(l_ref)
        for r in range(rep):
            sl = slice(r * HD, (r + 1) * HD)
            q = q_ref[:, sl]
            s1 = _dot(q, kx, NT)
            s2 = _dot(q, kc, NT)
            m = jnp.maximum(jnp.max(s1, axis=-1, keepdims=True), jnp.max(s2, axis=-1, keepdims=True))
            e1 = jnp.exp2((s1 - m) * SM_C)
            e2 = jnp.exp2((s2 - m) * SM_C)
            tot = jnp.sum(e1, axis=-1, keepdims=True) + jnp.sum(e2, axis=-1, keepdims=True)
            o = (_dot(e1.astype(BF16), vx) + _dot(e2.astype(BF16), vc)) * (1.0 / tot)
            o_ref[:, sl] = o
            y_ref[:, sl] = (o * _silu(g_ref[:, sl].astype(F32))).astype(BF16)
            l_ref[:, r:r + 1] = m * SM_C + jnp.log(tot) * float(np.log2(np.e))

    qblk = pl.BlockSpec((tq, gw), lambda b, g, i: (b * nq + i, g))
    return pl.pallas_call(
        body, name="att_fwd", grid=(nb, HKV, nq),
        in_specs=[qblk,
                  pl.BlockSpec((seq, HD), lambda b, g, i: (b, g)),
                  pl.BlockSpec((cx, HD), lambda b, g, i: (b, g)),
                  pl.BlockSpec((seq, HD), lambda b, g, i: (b, AV // HD + g)),
                  pl.BlockSpec((cx, HD), lambda b, g, i: (t_rows // cx + b, AV // HD + g)),
                  pl.BlockSpec((tq, gw), lambda b, g, i: (b * nq + i, AG // gw + g))],
        out_specs=[qblk, qblk, pl.BlockSpec((tq, 128), lambda b, g, i: (b * nq + i, g))],
        out_shape=[_sds((t_rows, D), F32), _sds((t_rows, D), BF16), _sds((t_rows, HKV * 128), F32)],
        compiler_params=_params(("parallel", "parallel", "parallel")),
    )(q16, kx16, kc16, px, px, px)


def _gate_specs(tm, col0):
    hw = D // 2
    return [pl.BlockSpec((tm, hw), lambda i: (i, col0 // hw)), pl.BlockSpec((tm, hw), lambda i: (i, col0 // hw + 1))]


def _merge_out(o_f, o_b, yatt16, px, w_o_ret16, w_o_att16, w_out16, x2, tgt, mod3, nb, seq, tm):
    t_rows = nb * seq
    bpb = seq // tm
    hw = D // 2

    def body(of_ref, ob_ref, g0, g1, g2, g3, wr_ref, ya_ref, wa_ref, mr0, mr1, ma0, ma1, wo_ref, x_ref, t_ref, gt_ref,
             yr_ref, ar_ref, aa_ref, y_ref, dxn_ref, dout_ref, dg_ref, loss_ref):
        i = pl.program_id(1)
        for h, g_ref in enumerate((g0, g1, g2, g3)):
            sl = slice(h * DV, (h + 1) * DV)
            o = of_ref[:, sl].astype(F32) + ob_ref[:, sl].astype(F32)
            r = lax.rsqrt(jnp.mean(o * o, axis=-1, keepdims=True) + EPS)
            yr_ref[:, sl] = ((o * r) * _silu(g_ref[...].astype(F32))).astype(BF16)
        ar = _dot(yr_ref[...], wr_ref[...])
        aa = _dot(ya_ref[...], wa_ref[...])
        ar_ref[...] = ar.astype(BF16)
        aa_ref[...] = aa.astype(BF16)
        for j, (mr_ref, ma_ref) in enumerate(((mr0, ma0), (mr1, ma1))):
            sl = slice(j * hw, (j + 1) * hw)
            y_ref[:, sl] = (_sig(mr_ref[...].astype(F32)) * ar[:, sl]
                            + _sig(ma_ref[...].astype(F32)) * aa[:, sl]).astype(BF16)
        out = _dot(y_ref[...], wo_ref[...])
        gate = gt_ref[...]
        diff = x_ref[...] + gate * out - t_ref[...]
        dxn = diff * (1.0 / D)
        dxn_ref[...] = dxn
        dout_ref[...] = (gate * dxn).astype(BF16)
        dg = jnp.sum(dxn * out, axis=0, keepdims=True)
        ls = jnp.broadcast_to(jnp.sum(diff * diff) * (0.5 / D), (1, 128))

        @pl.when(i == 0)
        def _():
            dg_ref[...] = dg
            loss_ref[...] = ls

        @pl.when(i > 0)
        def _():
            dg_ref[...] += dg
            loss_ref[...] += ls

    def cols(width, col0):
        return pl.BlockSpec((tm, width), lambda b, i: (b * bpb + i, col0 // width))

    def whole(rows):
        return pl.BlockSpec((rows, D), lambda b, i: (0, 0))

    row, wide = cols(D, 0), cols(RH * DV, 0)
    gates = [cols(DV, RG + h * DV) for h in range(RH)]
    merge_gates = [cols(hw, MR), cols(hw, MR + hw), cols(hw, MA), cols(hw, MA + hw)]
    return pl.pallas_call(
        body, name="merge_out", grid=(nb, bpb),
        in_specs=[wide, wide] + gates + [whole(RH * DV), row, whole(D)] + merge_gates
        + [whole(D), row, row, pl.BlockSpec((None, 1, D), lambda b, i: (b, 0, 2))],
        out_specs=[wide, row, row, row, row, row, pl.BlockSpec((None, 1, D), lambda b, i: (b, 0, 0)),
                   pl.BlockSpec((None, 1, 128), lambda b, i: (b, 0, 0))],
        out_shape=[_sds((t_rows, RH * DV), BF16)] + [_sds((t_rows, D), BF16)] * 3
        + [_sds((t_rows, D), F32), _sds((t_rows, D), BF16), _sds((nb, 1, D), F32), _sds((nb, 1, 128), F32)],
        compiler_params=_params(("parallel", "arbitrary")),
    )(o_f, o_b, *([px] * RH), w_o_ret16, yatt16, w_o_att16, px, px, px, px, w_out16, x2, tgt, mod3)


def _bwd_branches(dout16, w_out16, w_o_ret16, w_o_att16, px, a_ret, a_att, o_f, o_b, o_att, rows_all, tm):
    t_rows = dout16.shape[0]
    hw = D // 2

    def body(do_ref, wo_ref, wr_ref, wa_ref, mr0, mr1, ma0, ma1, ar_ref, aa_ref, rg0, rg1, rg2, rg3, of_ref, ob_ref,
             ag0, ag1, oa_ref, dar_ref, daa_ref, dor_ref, dao_ref, dl_ref, dp_ref):
        dy_all = _dot(do_ref[...], wo_ref[...], NT)
        for j, (mr_ref, ma_ref) in enumerate(((mr0, ma0), (mr1, ma1))):
            sl = slice(j * hw, (j + 1) * hw)
            dy = dy_all[:, sl]
            sr = _sig(mr_ref[...].astype(F32))
            sa = _sig(ma_ref[...].astype(F32))
            dar_ref[:, sl] = (dy * sr).astype(BF16)
            daa_ref[:, sl] = (dy * sa).astype(BF16)
            dp_ref[:, MR - RG + j * hw:MR - RG + (j + 1) * hw] = (
                dy * ar_ref[:, sl].astype(F32) * sr * (1.0 - sr)).astype(BF16)
            dp_ref[:, MA - RG + j * hw:MA - RG + (j + 1) * hw] = (
                dy * aa_ref[:, sl].astype(F32) * sa * (1.0 - sa)).astype(BF16)
        da_ret = dar_ref[...]
        for h, g_ref in enumerate((rg0, rg1, rg2, rg3)):
            sl = slice(h * DV, (h + 1) * DV)
            dy = _dot(da_ret, wr_ref[sl, :], NT)
            g = g_ref[...].astype(F32)
            o = of_ref[:, sl].astype(F32) + ob_ref[:, sl].astype(F32)
            r = lax.rsqrt(jnp.mean(o * o, axis=-1, keepdims=True) + EPS)
            on = o * r
            sg = _sig(g)
            don = dy * (g * sg)
            dp_ref[:, sl] = (dy * on * (sg * (1.0 + g * (1.0 - sg)))).astype(BF16)
            dor_ref[:, sl] = (r * (don - on * jnp.mean(on * don, axis=-1, keepdims=True))).astype(BF16)
        dy_all = _dot(daa_ref[...], wa_ref[...], NT)
        dl_ref[...] = jnp.zeros_like(dl_ref)
        for j, g_ref in enumerate((ag0, ag1)):
            sl = slice(j * hw, (j + 1) * hw)
            dy = dy_all[:, sl]
            g = g_ref[...].astype(F32)
            sg = _sig(g)
            dao = dy * (g * sg)
            dao_ref[:, sl] = dao.astype(BF16)
            prod = dao * oa_ref[:, sl]
            for r in range(hw // HD):
                dl_ref[:, j * 128 + r:j * 128 + r + 1] = jnp.sum(prod[:, r * HD:(r + 1) * HD], axis=-1, keepdims=True)
            dp_ref[:, AG - RG + j * hw:AG - RG + (j + 1) * hw] = (
                dy * oa_ref[:, sl] * (sg * (1.0 + g * (1.0 - sg)))).astype(BF16)

    def gate(h):
        return pl.BlockSpec((tm, DV), lambda i: (i, RG // DV + h))

    def whole(rows):
        return pl.BlockSpec((rows, D), lambda i: (0, 0))

    row = pl.BlockSpec((tm, D), lambda i: (i, 0))
    wide = pl.BlockSpec((tm, RH * DV), lambda i: (i, 0))
    return pl.pallas_call(
        body, name="bwd_branches", grid=(t_rows // tm,),
        in_specs=[row, whole(D), whole(RH * DV), whole(D)] + _gate_specs(tm, MR) + _gate_specs(tm, MA) + [row, row]
        + [gate(h) for h in range(RH)] + [wide, wide] + _gate_specs(tm, AG) + [row],
        out_specs=[row, row, wide, row, pl.BlockSpec((tm, HKV * 128), lambda i: (i, 0)),
                   pl.BlockSpec((pl.Element(tm), pl.Element(IN_COLS - RG)), lambda i: (i * tm, RG))],
        out_shape=[_sds((t_rows, D), BF16)] * 2 + [_sds((t_rows, RH * DV), BF16), _sds((t_rows, D), BF16),
                                                  _sds((t_rows, HKV * 128), F32), _sds((rows_all, IN_COLS), BF16)],
        compiler_params=_params(("parallel",)),
    )(dout16, w_out16, w_o_ret16, w_o_att16, px, px, px, px, a_ret, a_att, *([px] * RH), o_f, o_b, px, px, o_att)


def _att_bwd(q16, kx16, kc16, px, dao16, delta, lse, q_norm_w, k_norm_w, cos, sin, dp_all, nb, seq, cx, tq, after=()):
    t_rows = nb * seq
    nq = seq // tq
    rep = HQ // HKV
    gw = rep * HD
    scale = HD ** -0.5

    def body(q_ref, kx_ref, kc_ref, vx_ref, vc_ref, dao_ref, dl_ref, l_ref, xq_ref, w_ref, c_ref, s_ref,
             xk_ref, xkc_ref, wk_ref, ck_ref, sk_ref, *rest):
        daq_ref, gq_ref, gk_ref, dkx_ref, dvx_ref, dkc_ref, dvc_ref = rest[1 + len(after):8 + len(after)]
        accs = rest[8 + len(after):]
        i = pl.program_id(2)
        head0 = jnp.logical_and(pl.program_id(0) == 0, pl.program_id(1) == 0)
        first = jnp.logical_and(head0, i == 0)
        gq = jnp.zeros((1, HD), F32)
        kx = kx_ref[...]
        kc = kc_ref[...]
        vx = vx_ref[...].astype(BF16)
        vc = vc_ref[...].astype(BF16)
        @pl.when(i == 0)
        def _():
            for acc in accs:
                acc[...] = jnp.zeros_like(acc)

        dkx, dvx, dkc, dvc = [acc[...] for acc in accs]
        for r in range(rep):
            sl = slice(r * HD, (r + 1) * HD)
            q = q_ref[:, sl]
            lr = l_ref[:, r:r + 1]
            p1 = jnp.exp2(_dot(q, kx, NT) * SM_C - lr)
            p2 = jnp.exp2(_dot(q, kc, NT) * SM_C - lr)
            da16 = dao_ref[:, sl]
            delta = dl_ref[:, r:r + 1]
            ds1 = (p1 * (_dot(da16, vx, NT) - delta)).astype(BF16)
            ds2 = (p2 * (_dot(da16, vc, NT) - delta)).astype(BF16)
            dq = (_dot(ds1, kx) + _dot(ds2, kc)) * scale
            dkx += _dot(q, ds1, TN)
            dkc += _dot(q, ds2, TN)
            dvx += _dot(da16, p1.astype(BF16), TN)
            dvc += _dot(da16, p2.astype(BF16), TN)
            dt = dq * c_ref[...] + _swap_pairs(dq * s_ref[...])
            xv = xq_ref[:, sl].astype(F32)
            rn = lax.rsqrt(jnp.mean(xv * xv, axis=-1, keepdims=True) + EPS)
            xh = xv * rn
            dxh = dt * w_ref[...]
            daq_ref[:, sl] = (rn * (dxh - xh * jnp.mean(dxh * xh, axis=-1, keepdims=True))).astype(BF16)
            gq += jnp.sum(dt * xh, axis=0, keepdims=True)
        for acc, val in zip(accs, (dkx, dvx, dkc, dvc)):
            acc[...] = val

        @pl.when(first)
        def _():
            gq_ref[...] = gq

        @pl.when(jnp.logical_not(first))
        def _():
            gq_ref[...] += gq

        def k_back(dk, x_ref, dk_ref):
            xv = x_ref[...].astype(F32)
            rn = lax.rsqrt(jnp.mean(xv * xv, axis=-1, keepdims=True) + EPS)
            xh = xv * rn
            dxh = dk * wk_ref[...]
            dk_ref[...] = (rn * (dxh - xh * jnp.mean(dxh * xh, axis=-1, keepdims=True))).astype(BF16)
            return jnp.sum(dk * xh, axis=0, keepdims=True)

        @pl.when(i == nq - 1)
        def _():
            dvx_ref[...] = dvx.T.astype(BF16)
            dvc_ref[...] = dvc.T.astype(BF16)
            dk = dkx.T * scale
            gk = (k_back(dk * ck_ref[...] + _swap_pairs(dk * sk_ref[...]), xk_ref, dkx_ref)
                  + k_back(dkc.T * scale, xkc_ref, dkc_ref))

            @pl.when(head0)
            def _():
                gk_ref[...] = gk

            @pl.when(jnp.logical_not(head0))
            def _():
                gk_ref[...] += gk

    qblk = pl.BlockSpec((tq, gw), lambda b, g, i: (b * nq + i, g))
    kxb = pl.BlockSpec((seq, HD), lambda b, g, i: (b, g))
    kcb = pl.BlockSpec((cx, HD), lambda b, g, i: (b, g))
    table = pl.BlockSpec((tq, HD), lambda b, g, i: (i, 0))
    tables = pl.BlockSpec((seq, HD), lambda b, g, i: (0, 0))
    one = pl.BlockSpec((1, HD), lambda b, g, i: (0, 0))
    lane = pl.BlockSpec((tq, 128), lambda b, g, i: (b * nq + i, g))
    return pl.pallas_call(
        body, name="att_bwd", grid=(nb, HKV, nq),
        in_specs=[qblk,
                  pl.BlockSpec((seq, HD), lambda b, g, i: (b, g)),
                  pl.BlockSpec((cx, HD), lambda b, g, i: (b, g)),
                  pl.BlockSpec((seq, HD), lambda b, g, i: (b, AV // HD + g)),
                  pl.BlockSpec((cx, HD), lambda b, g, i: (t_rows // cx + b, AV // HD + g)),
                  qblk, lane, lane,
                  pl.BlockSpec((tq, gw), lambda b, g, i: (b * nq + i, AQ // gw + g)), one, table, table,
                  pl.BlockSpec((seq, HD), lambda b, g, i: (b, AK // HD + g)),
                  pl.BlockSpec((cx, HD), lambda b, g, i: (t_rows // cx + b, AK // HD + g)), one, tables, tables]
        + [pl.BlockSpec(memory_space=pl.ANY)] * (1 + len(after)),
        out_specs=[pl.BlockSpec((tq, gw), lambda b, g, i: (b * nq + i, AQ // gw + g)), one, one, kxb, kxb, kcb, kcb],
        out_shape=[_sds(dp_all.shape, BF16), _sds((1, HD), F32), _sds((1, HD), F32), _sds((t_rows, HKV * HD), BF16),
                   _sds((t_rows, HKV * HD), BF16), _sds((nb * cx, HKV * HD), BF16), _sds((nb * cx, HKV * HD), BF16)],
        scratch_shapes=[pltpu.VMEM((HD, seq), F32), pltpu.VMEM((HD, seq), F32), pltpu.VMEM((HD, cx), F32),
                        pltpu.VMEM((HD, cx), F32)],
        input_output_aliases={17: 0},
        compiler_params=_params(("arbitrary", "arbitrary", "arbitrary")),
    )(q16, kx16, kc16, px, px, dao16, delta, lse, px, q_norm_w, cos, sin, px, px, k_norm_w, cos, sin, dp_all, *after)


def _ret_bwd(px, lg, do16, hist_f, hist_b, nb, nc, cx):
    t_rows = nb * nc * CH

    def body(lg_ref, *refs):
        ins = (refs[0:5], refs[7:12])
        do_refs = (refs[5], refs[12])
        h_refs = (refs[6], refs[13])
        ctx_refs = refs[14:17]
        outs = (refs[17:20], refs[20:23])
        dck_ref, dcv_ref, dlg_ref = refs[23:26]
        dss = (refs[26], refs[27])
        c = pl.program_id(1)

        @pl.when(c == 0)
        def _():
            dss[0][...] = jnp.zeros_like(dss[0])
            dss[1][...] = jnp.zeros_like(dss[1])
            dlg_ref[...] = jnp.zeros_like(dlg_ref)

        for d in range(2):
            dq_ref, dk_ref, dv_ref = outs[d]
            for h in range(RH):
                lg_d = lg_ref[d, h]
                mask, relf, qd, qe, kd, ke = _decays(lg_d, d == 0)
                g_ch = jnp.exp(lg_d * CH)
                q, k, v16 = _head_qkv(ins[d], h)
                q16 = q.astype(BF16)
                k16 = k.astype(BF16)
                do16v = do_refs[d][:, h * DV:(h + 1) * DV]
                st16 = h_refs[d][h]
                dst = dss[d][h]
                dst16 = dst.astype(BF16)
                a = _dot(q16, k16, NT) * mask
                dp = _dot(do16v, v16, NT)
                da16 = (dp * mask).astype(BF16)
                dq_cross = _dot(do16v, st16, NT) * qd
                dq_ref[:, h * DK:(h + 1) * DK] = (_dot(da16, k16) + dq_cross).astype(BF16)
                dk_state = _dot(v16, dst16, NT) * kd
                dk_ref[:, h * DK:(h + 1) * DK] = ((_dot(da16, q16, TN) + dk_state) * (DK ** -0.5)).astype(BF16)
                dv = _dot(a.astype(BF16), do16v, TN) + _dot((k * kd).astype(BF16), dst16)
                dv_ref[:, h * DV:(h + 1) * DV] = dv.astype(BF16)
                dlg = (jnp.sum(relf * a * dp)
                       + jnp.sum(qe * jnp.sum(q * dq_cross, axis=-1, keepdims=True))
                       + jnp.sum(ke * jnp.sum(k * dk_state, axis=-1, keepdims=True))
                       + CH * g_ch * jnp.sum(dst * st16.astype(F32)))
                row = d * RH + h
                dlg_ref[row:row + 1, :] += jnp.broadcast_to(dlg, (1, 128))
                dss[d][h] = g_ch * dst + _dot((q * qd).astype(BF16), do16v, TN)

        @pl.when(c == nc - 1)
        def _():
            pos = lax.broadcasted_iota(jnp.int32, (cx, 1), 0).astype(F32)
            for h in range(RH):
                k, v16 = _ctx_kv(ctx_refs, h)
                dk = jnp.zeros((cx, DK), F32)
                dv = jnp.zeros((cx, DV), F32)
                for d, e in enumerate((cx - 1.0 - pos, pos)):
                    w = jnp.exp(lg_ref[d, h] * e)
                    ds16 = dss[d][h].astype(BF16)
                    t = _dot(v16, ds16, NT)
                    dk += t * w
                    dv += _dot((k * w).astype(BF16), ds16)
                    dlg = jnp.sum(e * w * jnp.sum(k * t, axis=-1, keepdims=True))
                    row = d * RH + h
                    dlg_ref[row:row + 1, :] += jnp.broadcast_to(dlg, (1, 128))
                dck_ref[:, h * DK:(h + 1) * DK] = (dk * (DK ** -0.5)).astype(BF16)
                dcv_ref[:, h * DV:(h + 1) * DV] = dv.astype(BF16)

    def fw(b, c):
        return b * nc + nc - 1 - c

    def bw(b, c):
        return b * nc + c

    def rows(rowf, width):
        return pl.BlockSpec((CH, width), lambda b, c: (rowf(b, c), 0))

    def hist(rowf):
        return pl.BlockSpec((None, None, RH, DK, DV), lambda b, c: (b, rowf(0, c), 0, 0, 0))

    in_specs = [pl.BlockSpec(memory_space=pltpu.SMEM)]
    out_specs = []
    for rowf in (fw, bw):
        in_specs += _wide_specs(rowf) + [rows(rowf, RH * DV), hist(rowf)]
        out_specs += [rows(rowf, RH * DK), rows(rowf, RH * DK), rows(rowf, RH * DV)]
    in_specs += _ctx_specs(t_rows, cx)
    out_specs += [pl.BlockSpec((cx, RH * DK), lambda b, c: (b, 0)), pl.BlockSpec((cx, RH * DV), lambda b, c: (b, 0)),
                  pl.BlockSpec((None, 8, 128), lambda b, c: (b, 0, 0))]
    qk = _sds((t_rows, RH * DK), BF16)
    vv = _sds((t_rows, RH * DV), BF16)
    return pl.pallas_call(
        body, name="ret_bwd", grid=(nb, nc), in_specs=in_specs, out_specs=out_specs,
        out_shape=[qk, qk, vv, qk, qk, vv, _sds((nb * cx, RH * DK), BF16), _sds((nb * cx, RH * DV), BF16),
                   _sds((nb, 8, 128), F32)],
        scratch_shapes=[pltpu.VMEM((RH, DK, DV), F32), pltpu.VMEM((RH, DK, DV), F32)],
        compiler_params=_params(("parallel", "arbitrary")),
    )(lg, *([px] * 5), do16, hist_f, *([px] * 5), do16, hist_b, *([px] * 3))


def _assemble_lat(dp_all, dk_f, dk_b, dv_f, dv_b, dak16, dvx, dq_f, dq_b, tm):
    t_rows = dk_f.shape[0]

    def body(_, dkf, dkb, dvf, dvb, dak, dav, dqf, dqb, o_ref):
        o_ref[:, RK:RK + RH * DK] = (dkf[...].astype(F32) + dkb[...].astype(F32)).astype(BF16)
        o_ref[:, RV:RV + RH * DV] = (dvf[...].astype(F32) + dvb[...].astype(F32)).astype(BF16)
        o_ref[:, AK:AK + HKV * HD] = dak[...]
        o_ref[:, AV:AV + HKV * HD] = dav[...]
        o_ref[:, RQ:RQ + RH * DK] = (dqf[...].astype(F32) + dqb[...].astype(F32)).astype(BF16)

    args = (dk_f, dk_b, dv_f, dv_b, dak16, dvx, dq_f, dq_b)
    return pl.pallas_call(
        body, name="assemble_lat", grid=(t_rows // tm,),
        in_specs=[pl.BlockSpec(memory_space=pl.ANY)]
        + [pl.BlockSpec((tm, a.shape[1]), lambda i: (i, 0)) for a in args],
        out_specs=pl.BlockSpec((tm, RG), lambda i: (i, 0)), out_shape=_sds(dp_all.shape, BF16),
        input_output_aliases={0: 0},
        compiler_params=_params(("parallel",)),
    )(dp_all, *args)


def _assemble_ctx(dp_all, dck16, dcv16, dcak16, dvc, t_rows, tm):
    c_rows = dck16.shape[0]
    rb = t_rows // tm

    def body(_, dck, dcv, dcak, dcav, o_ref):
        o_ref[:, RK:RK + RH * DK] = dck[...]
        o_ref[:, RV:RV + RH * DV] = dcv[...]
        o_ref[:, AK:AK + HKV * HD] = dcak[...]
        o_ref[:, AV:AV + HKV * HD] = dcav[...]
        o_ref[:, KV_COLS:] = jnp.zeros((tm, IN_COLS - KV_COLS), BF16)

    args = (dck16, dcv16, dcak16, dvc)
    return pl.pallas_call(
        body, name="assemble_ctx", grid=(c_rows // tm,),
        in_specs=[pl.BlockSpec(memory_space=pl.ANY)]
        + [pl.BlockSpec((tm, a.shape[1]), lambda i: (i, 0)) for a in args],
        out_specs=pl.BlockSpec((tm, IN_COLS), lambda i: (rb + i, 0)), out_shape=_sds(dp_all.shape, BF16),
        input_output_aliases={0: 0},
        compiler_params=_params(("parallel",)),
    )(dp_all, *args)


def _norm_bwd(dh, x2, mod3, norm_w, dxn, row_off, rows_per_group, group0, tm, name):
    with_dx = dxn is not None
    rows = x2.shape[0]
    rb0 = row_off // tm
    bpg = rows_per_group // tm
    ngroups = rows // rows_per_group

    def body(*refs):
        if with_dx:
            dh_ref, x_ref, sc_ref, nw_ref, dxn_ref, dx_ref, dsh_ref, dsc_ref, dnw_ref = refs
        else:
            dh_ref, x_ref, sc_ref, nw_ref, dsh_ref, dsc_ref, dnw_ref = refs
        i = pl.program_id(0)
        dhv = dh_ref[...]
        xv = x_ref[...]
        nw = nw_ref[...]
        r = lax.rsqrt(jnp.mean(xv * xv, axis=-1, keepdims=True) + EPS)
        xh = xv * r
        dm = dhv * (1.0 + sc_ref[...])
        dsh = jnp.sum(dhv, axis=0, keepdims=True)
        dsc = jnp.sum(dhv * (xh * nw), axis=0, keepdims=True)
        dnw = jnp.sum(dm * xh, axis=0, keepdims=True)
        if with_dx:
            dxh = dm * nw
            dx_ref[...] = dxn_ref[...] + r * (dxh - xh * jnp.mean(dxh * xh, axis=-1, keepdims=True))

        @pl.when(i % bpg == 0)
        def _():
            dsh_ref[...] = dsh
            dsc_ref[...] = dsc

        @pl.when(i % bpg != 0)
        def _():
            dsh_ref[...] += dsh
            dsc_ref[...] += dsc

        @pl.when(i == 0)
        def _():
            dnw_ref[...] = dnw

        @pl.when(i > 0)
        def _():
            dnw_ref[...] += dnw

    grp = pl.BlockSpec((None, 1, D), lambda i: (i // bpg, 0, 0))
    in_specs = [pl.BlockSpec((tm, D), lambda i: (rb0 + i, 0)), pl.BlockSpec((tm, D), lambda i: (i, 0)),
                pl.BlockSpec((None, 1, D), lambda i: (group0 + i // bpg, 0, 1)),
                pl.BlockSpec((1, D), lambda i: (0, 0))]
    args = [dh, x2, mod3, norm_w]
    out_specs = [grp, grp, pl.BlockSpec((1, D), lambda i: (0, 0))]
    out_shape = [_sds((ngroups, 1, D), F32), _sds((ngroups, 1, D), F32), _sds((1, D), F32)]
    if with_dx:
        in_specs.append(pl.BlockSpec((tm, D), lambda i: (i, 0)))
        args.append(dxn)
        out_specs.insert(0, pl.BlockSpec((tm, D), lambda i: (i, 0)))
        out_shape.insert(0, _sds((rows, D), F32))
    return pl.pallas_call(
        body, name=name, grid=(rows // tm,), in_specs=in_specs, out_specs=out_specs, out_shape=out_shape,
        compiler_params=_params(("arbitrary",)),
    )(*args)


def _small_final(dmod_all, dmodc_parts, c_rows, dm_loc_rows, nw_parts, misc_parts, c_ctx, r_pad, w_ada16):
    loc = dm_loc_rows.shape[1]

    def body(dm_ref, dmc_ref, c_ref, dml_ref, nwp_ref, mp_ref, cc_ref, r_ref, w_ref,
             gb_ref, gc_ref, gnw_ref, misc_ref, gwa_ref):
        dmc = jnp.sum(dmc_ref[...], axis=0, keepdims=True)
        gb_ref[...] = jnp.sum(dm_ref[...], axis=0, keepdims=True) + dmc
        dsc = _dot(jnp.broadcast_to(dmc, (8, 3 * D)).astype(BF16), w_ref[...], NT)[0:1, :]
        gc_ref[...] = dsc * _dsilu(cc_ref[...])
        gnw_ref[...] = jnp.sum(nwp_ref[...], axis=0, keepdims=True)
        misc = jnp.sum(mp_ref[...], axis=0, keepdims=True)
        y = jnp.exp2(r_ref[...])
        lane = lax.broadcasted_iota(jnp.int32, (1, D), 1)
        is_decay = jnp.logical_and(lane >= 2 * HD, lane < 2 * HD + 2 * RH)
        misc_ref[...] = misc * jnp.where(is_decay, -(y * np.float32(np.log(2.0))) / (1.0 - y), 1.0)
        gwa_ref[...] = _dot(_silu(c_ref[...]).astype(BF16), dml_ref[...].astype(BF16), TN)

    return pl.pallas_call(
        body, name="small_final",
        out_shape=[_sds((1, 3 * D), F32), _sds((1, D), F32), _sds((1, D), F32), _sds((1, D), F32), _sds((D, loc), F32)],
        compiler_params=pltpu.CompilerParams(vmem_limit_bytes=VMEM_LIMIT),
    )(dmod_all, dmodc_parts, c_rows, dm_loc_rows, nw_parts, misc_parts, c_ctx, r_pad, w_ada16)


def _adamw_math(w, g, m, v):
    nm = B1 * m + (1.0 - B1) * g
    nv = B2 * v + (1.0 - B2) * (g * g)
    return -LR * ((nm / (1.0 - B1 ** STEP)) / (jnp.sqrt(nv / (1.0 - B2 ** STEP)) + ADAM_EPS) + WD * w), nm, nv


def _adamw(w, g, m, v, name):
    rows, cols = w.shape
    tm = _pick(rows, 448, 8)

    def body(w_ref, g_ref, m_ref, v_ref, d_ref, nm_ref, nv_ref):
        d_ref[...], nm_ref[...], nv_ref[...] = _adamw_math(w_ref[...], g_ref[...], m_ref[...], v_ref[...])

    blk = pl.BlockSpec((tm, cols), lambda i: (i, 0))
    return pl.pallas_call(
        body, name=name, grid=(rows // tm,), in_specs=[blk] * 4, out_specs=[blk] * 3,
        out_shape=[_sds((rows, cols), F32)] * 3, compiler_params=_params(("parallel",)),
    )(w, g, m, v)


def _adamw_small(wgmv):
    n = len(wgmv)

    def body(*refs):
        ins, outs = refs[:4 * n], refs[4 * n:]
        for k in range(n):
            w, g, m, v = [r[...] for r in ins[4 * k:4 * k + 4]]
            outs[3 * k][...], outs[3 * k + 1][...], outs[3 * k + 2][...] = _adamw_math(w, g, m, v)

    out = pl.pallas_call(
        body, name="adamw_small", out_shape=[_sds(t[0].shape, F32) for t in wgmv for _ in range(3)],
    )(*[a for t in wgmv for a in t])
    return [out[3 * k:3 * k + 3] for k in range(n)]


def _mesh_pos():
    return lax.axis_index("x"), lax.axis_index("y"), lax.axis_index("c")


def _all_gather(arrs, name):
    n = len(arrs)

    def body(*refs):
        ins, outs = refs[:n], refs[n:2 * n]
        send_sems, recv_sems, local_sems = refs[2 * n:]
        x, y, c = _mesh_pos()
        me, sib = (x, y, c), (x, y, 1 - c)
        chips = [(1 - x, y), (x, 1 - y), (1 - x, 1 - y)]

        def slot(p):
            return 4 * p[0] + 2 * p[1] + p[2]

        def copy(a, k, block, to, own):
            dst = outs[a].at[slot(block)]
            return pltpu.make_async_remote_copy(
                src_ref=ins[a] if own else dst, dst_ref=dst, send_sem=send_sems.at[a, k], recv_sem=recv_sems.at[a, k],
                device_id=to, device_id_type=MESH_T)

        mine = [pltpu.make_async_copy(ins[a], outs[a].at[slot(me)], local_sems.at[a]) for a in range(n)]
        for cp in mine:
            cp.start()
        first = []
        for a in range(n):
            first.append(copy(a, 0, me, sib, True))
            first += [copy(a, 1 + j, me, (*chip, c), True) for j, chip in enumerate(chips)]
        for cp in first:
            cp.start()
        passed = []
        for j, chip in enumerate(chips):
            for a in range(n):
                copy(a, 1 + j, (*chip, c), me, False).wait_recv()
                fwd = copy(a, 4 + j, (*chip, c), sib, False)
                fwd.start()
                passed.append(fwd)
        for a in range(n):
            copy(a, 0, sib, me, False).wait_recv()
            for j, chip in enumerate(chips):
                copy(a, 4 + j, (*chip, 1 - c), me, False).wait_recv()
        for cp in first + passed:
            cp.wait_send()
        for cp in mine:
            cp.wait()

    hbm = pl.BlockSpec(memory_space=pl.ANY)
    return pl.pallas_call(
        body, name=name, in_specs=[hbm] * n, out_specs=[hbm] * n,
        out_shape=[_sds((N_DEV,) + a.shape, a.dtype) for a in arrs],
        scratch_shapes=[pltpu.SemaphoreType.DMA((n, 7)), pltpu.SemaphoreType.DMA((n, 7)), pltpu.SemaphoreType.DMA((n,))],
    )(*arrs)


def _pair_add(parts, gots, core, name):
    n = len(parts)
    cols = parts[0].shape[2]
    tiles = min(p.shape[1] for p in parts) // _pick(min(p.shape[1] for p in parts), 672, 16)

    def body(core_ref, *refs):
        for p_ref, g_ref, o_ref in zip(refs[:n], refs[n:2 * n], refs[2 * n:]):
            o_ref[...] = (p_ref[...].astype(F32) + g_ref[...].astype(F32)).astype(BF16)

    def blk(p):
        return pl.BlockSpec((None, p.shape[1] // tiles, cols), lambda k, i, cr: (k, i, 0))

    return pl.pallas_call(
        body, name=name,
        grid_spec=pltpu.PrefetchScalarGridSpec(
            num_scalar_prefetch=1, grid=(4, tiles),
            in_specs=[pl.BlockSpec((None, None, p.shape[1] // tiles, cols), lambda k, i, cr: (k, cr[0], i, 0))
                      for p in parts] + [blk(p) for p in parts],
            out_specs=[blk(p) for p in parts]),
        out_shape=[_sds((4,) + p.shape[1:], BF16) for p in parts], compiler_params=_params(("parallel", "parallel")),
    )(core, *[p.reshape(4, 2, *p.shape[1:]) for p in parts], *gots)


def _chip_sum_adamw(pair_sums, landed, chip, w, m, v, name):
    _, rows, cols = pair_sums.shape
    tm = _pick(rows, 448, 16)

    def body(chip_ref, s_ref, l_ref, w_ref, m_ref, v_ref, g_ref, d_ref, nm_ref, nv_ref):
        acc = s_ref[...].astype(F32)
        for j in range(3):
            acc = acc + l_ref[j].astype(F32)
        g_ref[...] = acc
        d_ref[...], nm_ref[...], nv_ref[...] = _adamw_math(w_ref[...], acc, m_ref[...], v_ref[...])

    blk = pl.BlockSpec((tm, cols), lambda i, ch: (i, 0))
    return pl.pallas_call(
        body, name=name,
        grid_spec=pltpu.PrefetchScalarGridSpec(
            num_scalar_prefetch=1, grid=(rows // tm,),
            in_specs=[pl.BlockSpec((None, tm, cols), lambda i, ch: (ch[0], i, 0)),
                      pl.BlockSpec((3, tm, cols), lambda i, ch: (0, i, 0)), blk, blk, blk],
            out_specs=[blk] * 4),
        out_shape=[_sds((rows, cols), F32)] * 4, compiler_params=_params(("parallel",)),
    )(chip, pair_sums, landed, w, m, v)


_HBM = pl.BlockSpec(memory_space=pltpu.HBM)
_SEM = pl.BlockSpec(memory_space=pltpu.SEMAPHORE)
_EFFECT = pltpu.SideEffectType.DATAFLOW_SIDE_EFFECTING


def _chip_routes(n):
    def plan(x, y, c):
        routes = []
        for a in range(n):
            for j in range(1, 4):
                px, py = x ^ (j >> 1), y ^ (j & 1)
                routes.append((a, 2 * px + py, (px, py, c), j - 1))
        return routes
    return plan, 3 * n


def _pair_routes(n):
    def plan(x, y, c):
        return [(a, 2 * k + 1 - c, (x, y, 1 - c), k) for a in range(n) for k in range(4)]
    return plan, 4 * n


def _bcast_routes(n):
    def plan(x, y, c):
        routes = []
        for a in range(n):
            for k in range(1, N_DEV):
                peer = (x ^ ((k >> 2) & 1), y ^ ((k >> 1) & 1), c ^ (k & 1))
                routes.append((a, 0, peer, 4 * x + 2 * y + c))
        return routes
    return plan, 7 * n


def _route_copies(srcs, lands, send_sems, recv_sems, routes):
    return [pltpu.make_async_remote_copy(
        src_ref=srcs[a].at[sb], dst_ref=lands[a].at[lb], send_sem=send_sems.at[r], recv_sem=recv_sems.at[r],
        device_id=peer, device_id_type=MESH_T) for r, (a, sb, peer, lb) in enumerate(routes)]


def _exchange_start(srcs, lands, routes, name, after=()):
    plan, count = routes
    n = len(srcs)
    n_in = 2 * n + len(after)

    def body(*refs):
        send_sems, recv_sems = refs[n_in], refs[n_in + 1]
        token = refs[-1]
        for cp in _route_copies(refs[:n], refs[n:2 * n], send_sems, recv_sems, plan(*_mesh_pos())):
            cp.start()
        token[...] = jnp.zeros_like(token)

    args = [pltpu.with_memory_space_constraint(a, pltpu.HBM) for a in list(srcs) + list(lands)]
    out = pl.pallas_call(
        body, name=name,
        out_shape=(pltpu.SemaphoreType.DMA((count,)), pltpu.SemaphoreType.DMA((count,)),
                   *[pltpu.HBM(a.shape, a.dtype) for a in args], _sds((8, 128), F32)),
        in_specs=[_HBM] * (2 * n) + [pl.BlockSpec(memory_space=pl.ANY)] * len(after),
        out_specs=(_SEM, _SEM, *([_HBM] * (2 * n)), pl.BlockSpec(memory_space=pltpu.VMEM)),
        input_output_aliases={i: 2 + i for i in range(2 * n)},
        compiler_params=pltpu.CompilerParams(has_side_effects=_EFFECT),
    )(*args, *after)
    return (out[0], out[1], list(out[2:2 + 2 * n]), routes), out[-1]


def _exchange_wait(state, after, name):
    send_sems, recv_sems, bufs, (plan, count) = state
    n = len(bufs) // 2

    def body(*refs):
        send_s, recv_s = refs[2 * n], refs[2 * n + 1]
        for cp in _route_copies(refs[:n], refs[n:2 * n], send_s, recv_s, plan(*_mesh_pos())):
            cp.wait_send()
            cp.wait_recv()

    out = pl.pallas_call(
        body, name=name, out_shape=tuple(pltpu.HBM(a.shape, a.dtype) for a in bufs),
        in_specs=[_HBM] * (2 * n) + [_SEM, _SEM, pl.BlockSpec(memory_space=pl.ANY)], out_specs=tuple([_HBM] * (2 * n)),
        input_output_aliases={i: i for i in range(2 * n)},
        compiler_params=pltpu.CompilerParams(has_side_effects=_EFFECT),
    )(*bufs, send_sems, recv_sems, after)
    return list(out[:n]), list(out[n:])


def _group_routes(js):
    def plan(x, y, c):
        return [(0, 0, (x ^ (j >> 1), y ^ (j & 1), c), 2 * j + c) for j in js]
    return plan, len(js)


def _pair_fill(groups, js, name, after=()):
    def body(*refs):
        g_ref, send_sems, recv_sems = refs[-3:]
        x, y, c = _mesh_pos()
        sends = []
        for n, j in enumerate(js):
            mine = g_ref.at[2 * j + c]
            sends.append(pltpu.make_async_remote_copy(
                src_ref=mine, dst_ref=mine, send_sem=send_sems.at[n], recv_sem=recv_sems.at[n],
                device_id=(x, y, 1 - c), device_id_type=MESH_T))
        for cp in sends:
            cp.start()
        for n, j in enumerate(js):
            pltpu.make_async_remote_copy(
                src_ref=g_ref.at[2 * j + c], dst_ref=g_ref.at[2 * j + 1 - c], send_sem=send_sems.at[n],
                recv_sem=recv_sems.at[n], device_id=(x, y, 1 - c), device_id_type=MESH_T).wait_recv()
        for cp in sends:
            cp.wait_send()

    hbm = pl.BlockSpec(memory_space=pl.ANY)
    return pl.pallas_call(
        body, name=name, in_specs=[hbm] * (1 + len(after)), out_specs=hbm, out_shape=_sds(groups.shape, groups.dtype),
        input_output_aliases={0: 0},
        scratch_shapes=[pltpu.SemaphoreType.DMA((len(js),)), pltpu.SemaphoreType.DMA((len(js),))],
    )(groups, *after)


def _in_proj_group(h_all, groups, j0, ng, chip, px_prev, after, name):
    rows_all = h_all.shape[0]
    gcols = IN_COLS // 4
    tm = _pick(rows_all, 1536, 128)
    g4 = groups.reshape(4, gcols, D)

    n_lead = (1 if px_prev is not None else 0) + len(after)
    lead = ([px_prev] if px_prev is not None else []) + list(after)

    def body(chip_ref, *refs):
        h_ref, w_ref, o_ref = refs[n_lead:]
        o_ref[...] = _dot(h_ref[...], w_ref[...], NT).astype(BF16)

    return pl.pallas_call(
        body, name=name,
        grid_spec=pltpu.PrefetchScalarGridSpec(
            num_scalar_prefetch=1, grid=(ng, rows_all // tm),
            in_specs=[pl.BlockSpec(memory_space=pl.ANY)] * n_lead
            + [pl.BlockSpec((tm, D), lambda n, i, ch: (i, 0)),
               pl.BlockSpec((None, gcols, D), lambda n, i, ch: (j0 + n, 0, 0))],
            out_specs=pl.BlockSpec((tm, gcols), lambda n, i, ch: (i, ch[0] ^ (j0 + n)))),
        out_shape=_sds((rows_all, IN_COLS), BF16),
        input_output_aliases={1: 0} if px_prev is not None else {},
        compiler_params=_params(("parallel", "parallel")),
    )(chip, *lead, h_all, g4)


def _d_h_groups(dp_all, groups, chip, i0, ni, dh_prev, after):
    rows_all = dp_all.shape[0]
    gcols = IN_COLS // 4
    tm = _D_H_ROWS
    g4 = groups.reshape(4, gcols, D)
    lead = ([dh_prev] if dh_prev is not None else []) + list(after)
    n_lead = len(lead)

    def body(chip_ref, *refs):
        a_ref, w_ref, o_ref = refs[n_lead:]
        j = pl.program_id(1)
        part = _dot(a_ref[...], w_ref[...])

        @pl.when(j == 0)
        def _():
            o_ref[...] = part

        @pl.when(j > 0)
        def _():
            o_ref[...] += part

    return pl.pallas_call(
        body, name="d_h_%d" % i0,
        grid_spec=pltpu.PrefetchScalarGridSpec(
            num_scalar_prefetch=1, grid=(ni, 4),
            in_specs=[pl.BlockSpec(memory_space=pl.ANY)] * n_lead
            + [pl.BlockSpec((tm, gcols), lambda i, j, ch: (i0 + i, ch[0] ^ j)),
               pl.BlockSpec((None, gcols, D), lambda i, j, ch: (j, 0, 0))],
            out_specs=pl.BlockSpec((tm, D), lambda i, j, ch: (i0 + i, 0))),
        out_shape=_sds((rows_all, D), F32),
        input_output_aliases={1: 0} if dh_prev is not None else {},
        compiler_params=_params(("parallel", "arbitrary")),
    )(chip, *lead, dp_all, g4)


def _reduce_scatter_send(parts, got, core, name):
    sums = _pair_add(parts, got, core, name + "_add")
    lands = [lax.empty((3,) + s_.shape[1:], BF16) for s_ in sums]
    return _exchange_start(sums, lands, _chip_routes(len(sums)), name + "_start")


def _reduce_scatter_finish(rs_state, after, chip, wmv, name):
    sums, landed = _exchange_wait(rs_state, after, name + "_wait")
    return [_chip_sum_adamw(s_, l_, chip, *t, "%s_adamw_%d" % (name, i))
            for i, (s_, l_, t) in enumerate(zip(sums, landed, wmv))]


def _local_step(x, c, ctx, norm_w, ret_log2_decay, q_norm_w, k_norm_w, loss_target,
                mod, proj_in, get_w_o, on_out_grads, on_in_grad, started=()):
    nb, seq, _ = x.shape
    cx = ctx.shape[1]
    t_rows, c_rows = nb * seq, nb * cx
    rows_all = t_rows + c_rows
    nc = seq // CH
    tm = _pick(seq, 256, 128)
    te = _pick(seq, 512, 128)
    assert cx % tm == 0 and t_rows % cx == 0 and seq % GRID_W == 0

    x2 = x.reshape(t_rows, D)
    ctx2 = ctx.reshape(c_rows, D)
    tgt = loss_target.reshape(t_rows, D)
    lg = _log_gamma(ret_log2_decay)
    cos, sin = _rope_tables(seq)

    mod3 = mod[:, None, :]
    h_all = _norm_fwd(x2, mod3, norm_w, rows_all, 0, seq, 0, None, te, "norm_fwd", after=started)
    h_all = _norm_fwd(ctx2, mod3, norm_w, rows_all, t_rows, c_rows, nb, h_all, tm, "norm_fwd_ctx")
    px = proj_in(h_all)
    o_f, o_b, hist_f, hist_b = _ret_fwd(px, lg, nb, nc, cx)
    q16 = _qk_prep(px, q_norm_w, cos, sin, t_rows, 0, AQ, HQ, 4, seq, te, "q_prep")
    kx16 = _qk_prep(px, k_norm_w, cos, sin, t_rows, 0, AK, HKV, HKV, seq, te, "k_prep")
    kc16 = _qk_prep(px, k_norm_w, None, None, c_rows, t_rows, AK, HKV, HKV, seq, tm, "kc_prep")
    o_att, yatt16, lse = _att_fwd(q16, kx16, kc16, px, nb, seq, cx, te)
    w_o_ret16, w_o_att16, w_out16 = get_w_o(lse)
    yret16, a_ret, a_att, y16, dxn, dout16, dgate, loss_b = _merge_out(
        o_f, o_b, yatt16, px, w_o_ret16, w_o_att16, w_out16, x2, tgt, mod3, nb, seq, tm)

    gw_out = _matmul(y16, dout16, ta=True, tm=D, tn=D, tk=D, out_dtype=BF16, name="gw_out")
    da_ret16, da_att16, do16, dao16, delta, dp_all = _bwd_branches(
        dout16, w_out16, w_o_ret16, w_o_att16, px, a_ret, a_att, o_f, o_b, o_att, rows_all, tm)
    gw_o_ret = _matmul(yret16, da_ret16, ta=True, tm=D, tn=D, tk=D, out_dtype=BF16, name="gw_o_ret")
    gw_o_att = _matmul(yatt16, da_att16, ta=True, tm=D, tn=D, tk=D, out_dtype=BF16, name="gw_o_att")
    out_send, out_started = on_out_grads([gw_o_ret, gw_o_att, gw_out])
    dp_all, gq, gk, dak16, dav16, dcak16, dcav16 = _att_bwd(q16, kx16, kc16, px, dao16, delta, lse, q_norm_w, k_norm_w,
                                                            cos, sin, dp_all, nb, seq, cx, te, after=out_started)
    out_state, out_sent = out_send(gq)
    dq_f, dk_f, dv_f, dq_b, dk_b, dv_b, dck16, dcv16, dlg_scan = _ret_bwd(px, lg, do16, hist_f, hist_b, nb, nc, cx)
    dp_all = _assemble_lat(dp_all, dk_f, dk_b, dv_f, dv_b, dak16, dav16, dq_f, dq_b, tm)
    dp_all = _assemble_ctx(dp_all, dck16, dcv16, dcak16, dcav16, t_rows, tm)
    gw_in_t = _matmul(dp_all, h_all, ta=True, tm=1536, tn=D, tk=2304, out_dtype=BF16, name="gw_in", after=out_sent)
    in_state, dh = on_in_grad(gw_in_t, dp_all)
    grad_x, dsh, dsc, gnw_lat = _norm_bwd(dh, x2, mod3, norm_w, dxn, 0, seq, 0, te, "norm_bwd")
    dsh_c, dsc_c, gnw_ctx = _norm_bwd(dh, ctx2, mod3, norm_w, None, t_rows, c_rows, nb, tm, "norm_bwd_ctx")

    dlg = jnp.sum(dlg_scan[:, :, 0], axis=0).reshape(1, 2 * RH)
    misc = jnp.concatenate([gq, gk, dlg, jnp.sum(loss_b[:, 0, 0]).reshape(1, 1),
                            jnp.zeros((1, D - 2 * HD - 2 * RH - 1), F32)], axis=1)
    rows = []
    for b in range(nb):
        rows += [dsh[b], dsc[b], dgate[b]]
    rows += [dsh_c[0], dsc_c[0]] + [c[b:b + 1] for b in range(nb)] + [gnw_lat + gnw_ctx, misc]
    payload = jnp.concatenate(rows + [jnp.zeros((PAY_ROWS - len(rows), D), F32)], axis=0)
    return grad_x.reshape(nb, seq, D), out_state, in_state, payload


def _finish_small(gathered, nb, c_ctx, ret_log2_decay, w_ada16, dev):
    n_dev = gathered.shape[0]
    loc = 3 * D // n_dev
    dmod_all = gathered[:, :3 * nb].reshape(n_dev * nb, 3 * D)
    dmodc_parts = jnp.concatenate([gathered[:, 3 * nb:3 * nb + 2].reshape(n_dev, 2 * D), jnp.zeros((n_dev, D), F32)], axis=1)
    c_all = gathered[:, 3 * nb + 2:4 * nb + 2].reshape(n_dev * nb, D)
    nw_parts = gathered[:, 4 * nb + 2]
    misc_parts = gathered[:, 4 * nb + 3]
    n_rows = n_dev * nb + n_dev
    pad = (-n_rows) % 16
    c_rows = jnp.concatenate([c_all, jnp.broadcast_to(c_ctx.reshape(1, D), (n_dev, D)), jnp.zeros((pad, D), F32)], axis=0)
    dm_rows = jnp.concatenate([dmod_all, dmodc_parts, jnp.zeros((pad, 3 * D), F32)], axis=0)
    dm_loc_rows = lax.dynamic_slice_in_dim(dm_rows, dev * loc, loc, axis=1)
    r_pad = jnp.full((1, D), -1.0, F32).at[:, 2 * HD:2 * HD + 2 * RH].set(ret_log2_decay.reshape(1, 2 * RH))
    gb, gc, gnw, misc, gwa = _small_final(dmod_all, dmodc_parts, c_rows, dm_loc_rows, nw_parts, misc_parts,
                                          c_ctx.reshape(1, D), r_pad, w_ada16)
    return (gb, gc, gnw, misc[:, :HD], misc[:, HD:2 * HD], misc[:, 2 * HD:2 * HD + 2 * RH], gwa,
            misc[0, 2 * HD + 2 * RH])


def kernel(x, c, ctx, c_ctx, norm_w, w_ada, b_ada, w_in, ret_log2_decay, q_norm_w, k_norm_w, w_o_ret, w_o_att, w_out, loss_target, m_c_ctx, m_norm_w, m_w_ada, m_b_ada, m_w_in, m_ret_log2_decay, m_q_norm_w, m_k_norm_w, m_w_o_ret, m_w_o_att, m_w_out, v_c_ctx, v_norm_w, v_w_ada, v_b_ada, v_w_in, v_ret_log2_decay, v_q_norm_w, v_k_norm_w, v_w_o_ret, v_w_o_att, v_w_out):
    nb = x.shape[0]
    mx, my, mc = _mesh_pos()
    dev = 4 * mx + 2 * my + mc
    core = jnp.reshape(mc, (1,)).astype(jnp.int32)
    chip = jnp.reshape(2 * mx + my, (1,)).astype(jnp.int32)

    n_loc = 3 * D // N_DEV
    c8 = jnp.zeros((8, D), F32).at[:nb].set(c).at[nb].set(c_ctx)
    c_land = lax.dynamic_update_slice(lax.empty((N_DEV, 8, D), F32), c8[None], (dev, 0, 0))
    c_state, c_token = _exchange_start([c8[None]], [c_land], _bcast_routes(1), "gather_c_start")
    w_in_t = jnp.transpose(w_in[0])
    in_shard = w_in_t.astype(BF16)
    groups = lax.dynamic_update_slice(lax.empty((N_DEV,) + in_shard.shape, BF16), in_shard[None], (mc, 0, 0))
    groups = _pair_fill(groups, (0,), "gather_in_pair", after=(c_token,))
    _, (c_all,) = _exchange_wait(c_state, groups, "gather_c_wait")
    ada_shard = w_ada[0].astype(BF16)
    b_loc = lax.dynamic_slice(b_ada, (0, dev * n_loc), (1, n_loc))
    mod_cols = _mod_part(c_all.reshape(N_DEV * 8, D), ada_shard, b_loc)
    (mod_all,) = _all_gather([mod_cols], "gather_mod")
    mod = jnp.transpose(lax.dynamic_slice(mod_all, (0, dev * 8, 0), (N_DEV, 8, n_loc)), (1, 0, 2)).reshape(8, 3 * D)
    ada_land = lax.dynamic_update_slice(lax.empty((N_DEV,) + ada_shard.shape, BF16), ada_shard[None], (dev, 0, 0))

    (near_send, near_recv, near_bufs, near_routes), gin_token = _exchange_start(
        [in_shard[None]], [groups], _group_routes((1, 2)), "gather_in_start", after=(mod_all,))
    w_in_groups, wo_states, ada_states = [], [], []
    wo_shards = [w_[0].astype(BF16) for w_ in (w_o_ret, w_o_att, w_out)]
    wo_lands = [lax.dynamic_update_slice(lax.empty((N_DEV,) + s_.shape, BF16), s_[None], (dev, 0, 0)) for s_ in wo_shards]

    def _state(send, recv, src, groups, routes):
        return send, recv, [src, groups], routes

    def proj_in(h_all):
        src, groups = near_bufs
        px = _in_proj_group(h_all, groups, 0, 1, chip, None, (gin_token,), "in_proj_0")
        (src,), (groups,) = _exchange_wait(_state(near_send, near_recv, src, groups, near_routes), px,
                                           "gather_in_wait_near")
        groups = _pair_fill(groups, (1, 2), "gather_in_fill_near")
        (far_send, far_recv, (src, groups), far_routes), far_token = _exchange_start(
            [src], [groups], _group_routes((3,)), "gather_in_start_far")
        wo_state, wo_token = _exchange_start([s_[None] for s_ in wo_shards], wo_lands, _bcast_routes(3),
                                             "gather_wo_start", after=(far_token,))
        wo_states.append(wo_state)
        ada_state, ada_token = _exchange_start([ada_shard[None]], [ada_land], _bcast_routes(1), "gather_ada_start",
                                               after=(wo_token,))
        ada_states.append(ada_state)
        px = _in_proj_group(h_all, groups, 1, 2, chip, px, (ada_token,), "in_proj_near")
        (src,), (groups,) = _exchange_wait(_state(far_send, far_recv, src, groups, far_routes), px,
                                           "gather_in_wait_far")
        groups = _pair_fill(groups, (3,), "gather_in_fill_far")
        px = _in_proj_group(h_all, groups, 3, 1, chip, px, (), "in_proj_far")
        w_in_groups.append(groups)
        return px

    def get_w_o(after):
        _, (l_ret, l_att, l_out) = _exchange_wait(wo_states[0], after, "gather_wo_wait")
        return l_ret.reshape(RH * DV, D), l_att.reshape(D, D), l_out.reshape(D, D)

    def on_out_grads(grads):
        parts = [g_.reshape(N_DEV, g_.shape[0] // N_DEV, D) for g_ in grads]
        lands = [lax.empty((4,) + p_.shape[1:], BF16) for p_ in parts]
        pair_state, pair_token = _exchange_start(parts, lands, _pair_routes(len(parts)), "rs_out_pair_start")

        def send(after):
            parts_, got = _exchange_wait(pair_state, after, "rs_out_pair_wait")
            state, token = _reduce_scatter_send(parts_, got, core, "rs_out")
            return state, (token,)

        return send, (pair_token,)

    def on_in_grad(grad, dp_all):
        parts = [grad.reshape(N_DEV, IN_COLS // N_DEV, D)]
        lands = [lax.empty((4,) + p_.shape[1:], BF16) for p_ in parts]
        pair_state, pair_token = _exchange_start(parts, lands, _pair_routes(1), "rs_in_pair_start")
        dh = _d_h_groups(dp_all, w_in_groups[0], chip, 0, 1, None, (pair_token,))
        parts, got = _exchange_wait(pair_state, dh, "rs_in_pair_wait")
        state, token = _reduce_scatter_send(parts, got, core, "rs_in")
        n_tiles = dp_all.shape[0] // _D_H_ROWS
        return state, _d_h_groups(dp_all, w_in_groups[0], chip, 1, n_tiles - 1, dh, (token,))

    grad_x, out_state, in_state, payload = _local_step(
        x, c, ctx, norm_w, ret_log2_decay, q_norm_w, k_norm_w, loss_target,
        mod, proj_in, get_w_o, on_out_grads, on_in_grad, started=(gin_token,))

    pay_land = lax.dynamic_update_slice(lax.empty((N_DEV,) + payload.shape, F32), payload[None], (dev, 0, 0))
    pay_state, pay_token = _exchange_start([payload[None]], [pay_land], _bcast_routes(1), "gather_small_start")

    out_res = _reduce_scatter_finish(out_state, pay_token, chip,
                                     [(w_[0], m_[0], v_[0]) for w_, m_, v_ in ((w_o_ret, m_w_o_ret, v_w_o_ret),
                                                                                (w_o_att, m_w_o_att, v_w_o_att),
                                                                                (w_out, m_w_out, v_w_out))], "rs_out")
    (in_res,) = _reduce_scatter_finish(in_state, out_res[0][0], chip,
                                       [(w_in_t, jnp.transpose(m_w_in[0]), jnp.transpose(v_w_in[0]))], "rs_in")

    _, (gathered,) = _exchange_wait(pay_state, in_res[0], "gather_small_wait")
    _, (l_ada,) = _exchange_wait(ada_states[0], gathered, "gather_ada_wait")
    w_ada16 = jnp.transpose(l_ada, (1, 0, 2)).reshape(D, 3 * D)
    gb, gc, gnw, gq, gk, gr, gwa, loss = _finish_small(gathered, nb, c_ctx, ret_log2_decay, w_ada16, dev)
    big = {4: [jnp.transpose(r)[None] for r in in_res]}
    for i, res in zip((8, 9, 10), out_res):
        big[i] = [r[None] for r in res]
    small_g = {0: gc.reshape(c_ctx.shape), 1: gnw, 2: gwa[None], 3: gb, 5: gr.reshape(ret_log2_decay.shape), 6: gq, 7: gk}
    weights = [c_ctx, norm_w, w_ada, b_ada, w_in, ret_log2_decay, q_norm_w, k_norm_w, w_o_ret, w_o_att, w_out]
    ms = [m_c_ctx, m_norm_w, m_w_ada, m_b_ada, m_w_in, m_ret_log2_decay, m_q_norm_w, m_k_norm_w, m_w_o_ret, m_w_o_att, m_w_out]
    vs = [v_c_ctx, v_norm_w, v_w_ada, v_b_ada, v_w_in, v_ret_log2_decay, v_q_norm_w, v_k_norm_w, v_w_o_ret, v_w_o_att, v_w_out]
    def rows2(i):
        return [a.reshape(-1, weights[i].shape[-1]) for a in (weights[i], small_g[i], ms[i], vs[i])]

    small_ids = [i for i in small_g if i != 2]
    steps = dict(zip(small_ids, _adamw_small([rows2(i) for i in small_ids])))
    steps[2] = _adamw(*rows2(2), "adamw_w_ada")
    grads, deltas, new_ms, new_vs = [], [], [], []
    for i, w in enumerate(weights):
        res = big[i] if i in big else [small_g[i]] + [r.reshape(w.shape) for r in steps[i]]
        for lst, r in zip((grads, deltas, new_ms, new_vs), res):
            lst.append(r)
    return (loss, grad_x, *grads, *deltas, *new_ms, *new_vs)
```

```python
import numpy as np
import jax
import jax.numpy as jnp
from jax import lax
from jax.experimental import pallas as pl
from jax.experimental.pallas import tpu as pltpu

F32 = jnp.float32
BF16 = jnp.bfloat16

D = 1024
RH, DK, DV, CH = 4, 256, 512, 256
HQ, HKV, HD = 8, 2, 128
GRID_W = 64
ROPE_THETA = 10000.0
EPS = 1e-6
RK, RV, AK, AV, RQ, RG, AQ, AG, MR, MA = 0, 1024, 3072, 3328, 3584, 4608, 6656, 7680, 8704, 9728
IN_COLS = 10752
KV_COLS = 3584
N_DEV = 8
LR, B1, B2, ADAM_EPS, WD, STEP = 0.001, 0.9, 0.999, 1e-08, 0.01, 10
PAY_ROWS = 16
VMEM_LIMIT = 56 * 1024 * 1024
_D_H_ROWS = 1536
MESH_T = pl.DeviceIdType.MESH

NT = (((1,), (1,)), ((), ()))
TN = (((0,), (0,)), ((), ()))
SM_C = (HD ** -0.5) * float(np.log2(np.e))


def _params(sem):
    return pltpu.CompilerParams(dimension_semantics=sem, vmem_limit_bytes=VMEM_LIMIT)


def _pick(n, target, mult=8):
    best = None
    for t in range(mult, min(n, target) + 1, mult):
        if n % t == 0:
            best = t
    return best or n


def _dot(a, b, dn=None):
    if dn is None:
        return jnp.dot(a, b, preferred_element_type=F32)
    return lax.dot_general(a, b, dn, preferred_element_type=F32)


def _sig(v):
    return jax.nn.sigmoid(v)


def _silu(v):
    return v * _sig(v)


def _dsilu(v):
    s = _sig(v)
    return s * (1.0 + v * (1.0 - s))


def _sds(shape, dtype):
    return jax.ShapeDtypeStruct(shape, dtype)


def _matmul(a, b, *, ta=False, tb=False, tm, tn, tk, out_dtype, name, after=()):
    m = a.shape[1] if ta else a.shape[0]
    kdim = a.shape[0] if ta else a.shape[1]
    n = b.shape[0] if tb else b.shape[1]
    tm, tn, tk = _pick(m, tm, 128), _pick(n, tn, 128), _pick(kdim, tk, 128)
    nk = kdim // tk
    dn = (((0 if ta else 1,), (1 if tb else 0,)), ((), ()))

    def body(a_ref, b_ref, *rest):
        o_ref, acc_ref = rest[-2:]
        k = pl.program_id(2)
        part = _dot(a_ref[...].astype(BF16), b_ref[...].astype(BF16), dn)
        if nk == 1:
            o_ref[...] = part.astype(o_ref.dtype)
        else:
            @pl.when(k == 0)
            def _():
                acc_ref[...] = part

            @pl.when(k > 0)
            def _():
                acc_ref[...] += part

            @pl.when(k == nk - 1)
            def _():
                o_ref[...] = acc_ref[...].astype(o_ref.dtype)

    a_spec = pl.BlockSpec((tk, tm), lambda i, j, k: (k, i)) if ta else pl.BlockSpec((tm, tk), lambda i, j, k: (i, k))
    b_spec = pl.BlockSpec((tn, tk), lambda i, j, k: (j, k)) if tb else pl.BlockSpec((tk, tn), lambda i, j, k: (k, j))
    return pl.pallas_call(
        body, name=name, grid=(m // tm, n // tn, nk),
        in_specs=[a_spec, b_spec] + [pl.BlockSpec(memory_space=pl.ANY)] * len(after),
        out_specs=pl.BlockSpec((tm, tn), lambda i, j, k: (i, j)), out_shape=_sds((m, n), out_dtype),
        scratch_shapes=[pltpu.VMEM((tm, tn) if nk > 1 else (8, 128), F32)],
        compiler_params=_params(("parallel", "parallel", "arbitrary")),
    )(a, b, *after)


def _log_gamma(r):
    rp = jnp.full((8, 128), -1.0, F32).at[:2, :RH].set(r.reshape(2, RH))

    def body(r_ref, o_ref):
        o_ref[...] = jnp.log1p(-jnp.exp2(r_ref[...]))

    out = pl.pallas_call(body, name="log_gamma", out_shape=_sds((8, 128), F32))(rp)
    return out[:2, :RH]


def _mod_part(c_rows, w_ada_loc16, b_loc):
    def body(c_ref, w_ref, b_ref, o_ref):
        o_ref[...] = _dot(_silu(c_ref[...]).astype(BF16), w_ref[...]) + b_ref[...]

    return pl.pallas_call(
        body, name="mod_part", out_shape=_sds((c_rows.shape[0], w_ada_loc16.shape[1]), F32),
    )(c_rows, w_ada_loc16, b_loc)


def _norm_fwd(x2, mod3, norm_w, rows_all, row_off, rows_per_group, group0, h_prev, tm, name, after=()):
    rows = x2.shape[0]
    rb0 = row_off // tm
    bpg = rows_per_group // tm

    def body(*refs):
        x_ref, sh_ref, sc_ref, nw_ref, o_ref = refs[-5:]
        xv = x_ref[...]
        r = lax.rsqrt(jnp.mean(xv * xv, axis=-1, keepdims=True) + EPS)
        o_ref[...] = ((xv * r) * nw_ref[...] * (1.0 + sc_ref[...]) + sh_ref[...]).astype(BF16)

    in_specs = [pl.BlockSpec((tm, D), lambda i: (i, 0)),
                pl.BlockSpec((None, 1, D), lambda i: (group0 + i // bpg, 0, 0)),
                pl.BlockSpec((None, 1, D), lambda i: (group0 + i // bpg, 0, 1)),
                pl.BlockSpec((1, D), lambda i: (0, 0))]
    in_specs = [pl.BlockSpec(memory_space=pl.ANY)] * len(after) + in_specs
    args = list(after) + [x2, mod3, mod3, norm_w]
    alias = {}
    if h_prev is not None:
        in_specs.insert(0, pl.BlockSpec(memory_space=pl.ANY))
        args.insert(0, h_prev)
        alias = {0: 0}
    return pl.pallas_call(
        body, name=name, grid=(rows // tm,), in_specs=in_specs,
        out_specs=pl.BlockSpec((tm, D), lambda i: (rb0 + i, 0)), out_shape=_sds((rows_all, D), BF16),
        input_output_aliases=alias, compiler_params=_params(("parallel",)),
    )(*args)


def _decays(lg, fwd):
    ii = lax.broadcasted_iota(jnp.int32, (CH, CH), 0)
    jj = lax.broadcasted_iota(jnp.int32, (CH, CH), 1)
    ri = lax.broadcasted_iota(jnp.int32, (CH, 1), 0).astype(F32)
    rel = (ii - jj) if fwd else (jj - ii)
    relf = jnp.maximum(rel, 0).astype(F32)
    mask = jnp.where(rel >= 0, jnp.exp(lg * relf), 0.0)
    qe = (ri + 1.0) if fwd else (CH - ri)
    ke = (CH - 1.0 - ri) if fwd else ri
    return mask, relf, jnp.exp(lg * qe), qe, jnp.exp(lg * ke), ke


def _wide_specs(rowf):
    return [pl.BlockSpec((CH, 2 * DK), lambda b, c: (rowf(b, c), RQ // (2 * DK))),
            pl.BlockSpec((CH, 2 * DK), lambda b, c: (rowf(b, c), RQ // (2 * DK) + 1)),
            pl.BlockSpec((CH, RH * DK), lambda b, c: (rowf(b, c), RK // (RH * DK))),
            pl.BlockSpec((CH, 2 * DV), lambda b, c: (rowf(b, c), RV // (2 * DV))),
            pl.BlockSpec((CH, 2 * DV), lambda b, c: (rowf(b, c), RV // (2 * DV) + 1))]


def _head_qkv(refs, h):
    q0, q1, k, v0, v1 = refs
    lo = h % 2
    q = (q0, q1)[h // 2][:, lo * DK:(lo + 1) * DK].astype(F32)
    kk = k[:, h * DK:(h + 1) * DK].astype(F32) * (DK ** -0.5)
    v16 = (v0, v1)[h // 2][:, lo * DV:(lo + 1) * DV].astype(BF16)
    return q, kk, v16


def _ctx_specs(t_rows, cx):
    rb = t_rows // cx
    return [pl.BlockSpec((cx, RH * DK), lambda b, c: (rb + b, RK // (RH * DK))),
            pl.BlockSpec((cx, 2 * DV), lambda b, c: (rb + b, RV // (2 * DV))),
            pl.BlockSpec((cx, 2 * DV), lambda b, c: (rb + b, RV // (2 * DV) + 1))]


def _ctx_kv(refs, h):
    k, v0, v1 = refs
    kk = k[:, h * DK:(h + 1) * DK].astype(F32) * (DK ** -0.5)
    lo = h % 2
    return kk, (v0, v1)[h // 2][:, lo * DV:(lo + 1) * DV].astype(BF16)


def _ret_fwd(px, lg, nb, nc, cx):
    t_rows = nb * nc * CH

    def body(lg_ref, *refs):
        ins = (refs[0:5], refs[5:10])
        ctx_refs = refs[10:13]
        of_ref, ob_ref, hf_ref, hb_ref, sf, sb = refs[13:]
        c = pl.program_id(1)

        @pl.when(c == 0)
        def _():
            pos = lax.broadcasted_iota(jnp.int32, (cx, 1), 0).astype(F32)
            for h in range(RH):
                k, v16 = _ctx_kv(ctx_refs, h)
                sf[h] = _dot((k * jnp.exp(lg_ref[0, h] * (cx - 1.0 - pos))).astype(BF16), v16, TN)
                sb[h] = _dot((k * jnp.exp(lg_ref[1, h] * pos)).astype(BF16), v16, TN)

        for d, (o_ref, h_ref, s) in enumerate(((of_ref, hf_ref, sf), (ob_ref, hb_ref, sb))):
            for h in range(RH):
                lg_d = lg_ref[d, h]
                mask, _, qd, _, kd, _ = _decays(lg_d, d == 0)
                q, k, v16 = _head_qkv(ins[d], h)
                a = _dot(q.astype(BF16), k.astype(BF16), NT)
                st = s[h]
                st16 = st.astype(BF16)
                h_ref[h] = st16
                o = _dot((a * mask).astype(BF16), v16) + _dot((q * qd).astype(BF16), st16)
                o_ref[:, h * DV:(h + 1) * DV] = o.astype(BF16)
                s[h] = st * jnp.exp(lg_d * CH) + _dot((k * kd).astype(BF16), v16, TN)

    def fw(b, c):
        return b * nc + c

    def bw(b, c):
        return b * nc + nc - 1 - c

    in_specs = [pl.BlockSpec(memory_space=pltpu.SMEM)] + _wide_specs(fw) + _wide_specs(bw) + _ctx_specs(t_rows, cx)
    out_specs = [pl.BlockSpec((CH, RH * DV), lambda b, c: (fw(b, c), 0)),
                 pl.BlockSpec((CH, RH * DV), lambda b, c: (bw(b, c), 0)),
                 pl.BlockSpec((None, None, RH, DK, DV), lambda b, c: (b, c, 0, 0, 0)),
                 pl.BlockSpec((None, None, RH, DK, DV), lambda b, c: (b, nc - 1 - c, 0, 0, 0))]
    return pl.pallas_call(
        body, name="ret_fwd", grid=(nb, nc), in_specs=in_specs, out_specs=out_specs,
        out_shape=[_sds((t_rows, RH * DV), BF16)] * 2 + [_sds((nb, nc, RH, DK, DV), BF16)] * 2,
        scratch_shapes=[pltpu.VMEM((RH, DK, DV), F32), pltpu.VMEM((RH, DK, DV), F32)],
        compiler_params=_params(("parallel", "arbitrary")),
    )(lg, *([px] * 13))


def _rope_tables(seq):
    rows = seq // GRID_W
    row = np.repeat(np.arange(rows, dtype=np.float32), GRID_W)
    col = np.tile(np.arange(GRID_W, dtype=np.float32), rows)
    half = HD // 2
    freqs = (ROPE_THETA ** (-np.arange(0, half, 2, dtype=np.float32) / half)).astype(np.float32)
    ang = np.concatenate([row[:, None] * freqs, col[:, None] * freqs], axis=-1).astype(np.float32)
    cos = np.repeat(np.cos(ang), 2, axis=-1).astype(np.float32)
    sin = np.repeat(np.sin(ang), 2, axis=-1).astype(np.float32)
    sign = np.tile(np.array([-1.0, 1.0], np.float32), HD // 2)
    return jnp.asarray(cos), jnp.asarray(sin * sign)


def _swap_pairs(v):
    lane = lax.broadcasted_iota(jnp.int32, v.shape, 1)
    return jnp.where((lane & 1) == 0, pltpu.roll(v, HD - 1, 1), pltpu.roll(v, 1, 1))


def _qk_prep(px, nw, cos, sin, rows, row_off, col_off, heads, hb, seq, tm, name):
    rope = cos is not None
    rb0 = row_off // tm
    pb = seq // tm if rope else 1
    bw = hb * HD

    def body(*refs):
        if rope:
            x_ref, w_ref, c_ref, s_ref, o_ref = refs
        else:
            x_ref, w_ref, o_ref = refs
        for h in range(hb):
            sl = slice(h * HD, (h + 1) * HD)
            xv = x_ref[:, sl].astype(F32)
            r = lax.rsqrt(jnp.mean(xv * xv, axis=-1, keepdims=True) + EPS)
            t = (xv * r) * w_ref[...]
            if rope:
                t = t * c_ref[...] + _swap_pairs(t) * s_ref[...]
            o_ref[:, sl] = t.astype(BF16)

    in_specs = [pl.BlockSpec((tm, bw), lambda i, j: (rb0 + i, col_off // bw + j)),
                pl.BlockSpec((1, HD), lambda i, j: (0, 0))]
    args = [px, nw]
    if rope:
        in_specs += [pl.BlockSpec((tm, HD), lambda i, j: (i % pb, 0))] * 2
        args += [cos, sin]
    return pl.pallas_call(
        body, name=name, grid=(rows // tm, heads // hb), in_specs=in_specs,
        out_specs=pl.BlockSpec((tm, bw), lambda i, j: (i, j)), out_shape=_sds((rows, heads * HD), BF16),
        compiler_params=_params(("parallel", "parallel")),
    )(*args)


def _att_fwd(q16, kx16, kc16, px, nb, seq, cx, tq):
    t_rows = nb * seq
    nq = seq // tq
    rep = HQ // HKV
    gw = rep * HD

    def body(q_ref, kx_ref, kc_ref, vx_ref, vc_ref, g_ref, o_ref, y_ref, l_ref):
        kx = kx_ref[...]
        kc = kc_ref[...]
        vx = vx_ref[...].astype(BF16)
        vc = vc_ref[...].astype(BF16)
        l_ref[...] = jnp.zeros_like(l_ref)
        for r in range(rep):
            sl = slice(r * HD, (r + 1) * HD)
            q = q_ref[:, sl]
            s1 = _dot(q, kx, NT)
            s2 = _dot(q, kc, NT)
            m = jnp.maximum(jnp.max(s1, axis=-1, keepdims=True), jnp.max(s2, axis=-1, keepdims=True))
            e1 = jnp.exp2((s1 - m) * SM_C)
            e2 = jnp.exp2((s2 - m) * SM_C)
            tot = jnp.sum(e1, axis=-1, keepdims=True) + jnp.sum(e2, axis=-1, keepdims=True)
            o = (_dot(e1.astype(BF16), vx) + _dot(e2.astype(BF16), vc)) * (1.0 / tot)
            o_ref[:, sl] = o
            y_ref[:, sl] = (o * _silu(g_ref[:, sl].astype(F32))).astype(BF16)
            l_ref[:, r:r + 1] = m * SM_C + jnp.log(tot) * float(np.log2(np.e))

    qblk = pl.BlockSpec((tq, gw), lambda b, g, i: (b * nq + i, g))
    return pl.pallas_call(
        body, name="att_fwd", grid=(nb, HKV, nq),
        in_specs=[qblk,
                  pl.BlockSpec((seq, HD), lambda b, g, i: (b, g)),
                  pl.BlockSpec((cx, HD), lambda b, g, i: (b, g)),
                  pl.BlockSpec((seq, HD), lambda b, g, i: (b, AV // HD + g)),
                  pl.BlockSpec((cx, HD), lambda b, g, i: (t_rows // cx + b, AV // HD + g)),
                  pl.BlockSpec((tq, gw), lambda b, g, i: (b * nq + i, AG // gw + g))],
        out_specs=[qblk, qblk, pl.BlockSpec((tq, 128), lambda b, g, i: (b * nq + i, g))],
        out_shape=[_sds((t_rows, D), F32), _sds((t_rows, D), BF16), _sds((t_rows, HKV * 128), F32)],
        compiler_params=_params(("parallel", "parallel", "parallel")),
    )(q16, kx16, kc16, px, px, px)


def _gate_specs(tm, col0):
    hw = D // 2
    return [pl.BlockSpec((tm, hw), lambda i: (i, col0 // hw)), pl.BlockSpec((tm, hw), lambda i: (i, col0 // hw + 1))]


def _merge_out(o_f, o_b, yatt16, px, w_o_ret16, w_o_att16, w_out16, x2, tgt, mod3, nb, seq, tm):
    t_rows = nb * seq
    bpb = seq // tm
    hw = D // 2

    def body(of_ref, ob_ref, g0, g1, g2, g3, wr_ref, ya_ref, wa_ref, mr0, mr1, ma0, ma1, wo_ref, x_ref, t_ref, gt_ref,
             yr_ref, ar_ref, aa_ref, dxn_ref, dout_ref, dg_ref, loss_ref, gw_ref, y_ref, acc_ref):
        i = pl.program_id(1)
        first = jnp.logical_and(pl.program_id(0) == 0, i == 0)
        for h, g_ref in enumerate((g0, g1, g2, g3)):
            sl = slice(h * DV, (h + 1) * DV)
            o = of_ref[:, sl].astype(F32) + ob_ref[:, sl].astype(F32)
            r = lax.rsqrt(jnp.mean(o * o, axis=-1, keepdims=True) + EPS)
            yr_ref[:, sl] = ((o * r) * _silu(g_ref[...].astype(F32))).astype(BF16)
        ar = _dot(yr_ref[...], wr_ref[...])
        aa = _dot(ya_ref[...], wa_ref[...])
        ar_ref[...] = ar.astype(BF16)
        aa_ref[...] = aa.astype(BF16)
        for j, (mr_ref, ma_ref) in enumerate(((mr0, ma0), (mr1, ma1))):
            sl = slice(j * hw, (j + 1) * hw)
            y_ref[:, sl] = (_sig(mr_ref[...].astype(F32)) * ar[:, sl]
                            + _sig(ma_ref[...].astype(F32)) * aa[:, sl]).astype(BF16)
        out = _dot(y_ref[...], wo_ref[...])
        gate = gt_ref[...]
        diff = x_ref[...] + gate * out - t_ref[...]
        dxn = diff * (1.0 / D)
        dxn_ref[...] = dxn
        dout_ref[...] = (gate * dxn).astype(BF16)

        @pl.when(first)
        def _():
            acc_ref[...] = jnp.zeros_like(acc_ref)

        for j in range(2):
            sl = slice(j * hw, (j + 1) * hw)
            acc_ref[sl, :] += _dot(y_ref[:, sl], dout_ref[...], TN)

        @pl.when(jnp.logical_and(pl.program_id(0) == nb - 1, i == bpb - 1))
        def _():
            gw_ref[...] = acc_ref[...].astype(BF16)

        dg = jnp.sum(dxn * out, axis=0, keepdims=True)
        ls = jnp.broadcast_to(jnp.sum(diff * diff) * (0.5 / D), (1, 128))

        @pl.when(i == 0)
        def _():
            dg_ref[...] = dg
            loss_ref[...] = ls

        @pl.when(i > 0)
        def _():
            dg_ref[...] += dg
            loss_ref[...] += ls

    def cols(width, col0):
        return pl.BlockSpec((tm, width), lambda b, i: (b * bpb + i, col0 // width))

    def whole(rows):
        return pl.BlockSpec((rows, D), lambda b, i: (0, 0))

    row, wide = cols(D, 0), cols(RH * DV, 0)
    gates = [cols(DV, RG + h * DV) for h in range(RH)]
    merge_gates = [cols(hw, MR), cols(hw, MR + hw), cols(hw, MA), cols(hw, MA + hw)]
    return pl.pallas_call(
        body, name="merge_out", grid=(nb, bpb),
        in_specs=[wide, wide] + gates + [whole(RH * DV), row, whole(D)] + merge_gates
        + [whole(D), row, row, pl.BlockSpec((None, 1, D), lambda b, i: (b, 0, 2))],
        out_specs=[wide, row, row, row, row, pl.BlockSpec((None, 1, D), lambda b, i: (b, 0, 0)),
                   pl.BlockSpec((None, 1, 128), lambda b, i: (b, 0, 0)), whole(D)],
        out_shape=[_sds((t_rows, RH * DV), BF16)] + [_sds((t_rows, D), BF16)] * 2
        + [_sds((t_rows, D), F32), _sds((t_rows, D), BF16), _sds((nb, 1, D), F32), _sds((nb, 1, 128), F32),
           _sds((D, D), BF16)],
        scratch_shapes=[pltpu.VMEM((tm, D), BF16), pltpu.VMEM((D, D), F32)],
        compiler_params=_params(("arbitrary", "arbitrary")),
    )(o_f, o_b, *([px] * RH), w_o_ret16, yatt16, w_o_att16, px, px, px, px, w_out16, x2, tgt, mod3)


def _bwd_branches(dout16, w_out16, w_o_ret16, w_o_att16, px, a_ret, a_att, o_f, o_b, o_att, yatt16, rows_all, tm):
    t_rows = dout16.shape[0]
    hw = D // 2

    def body(do_ref, wo_ref, wr_ref, wa_ref, mr0, mr1, ma0, ma1, ar_ref, aa_ref, rg0, rg1, rg2, rg3, of_ref, ob_ref,
             ag0, ag1, oa_ref, ya_ref, dar_ref, dor_ref, dao_ref, dl_ref, dp_ref, gwa_ref, daa_ref, acca_ref):
        dy_all = _dot(do_ref[...], wo_ref[...], NT)
        for j, (mr_ref, ma_ref) in enumerate(((mr0, ma0), (mr1, ma1))):
            sl = slice(j * hw, (j + 1) * hw)
            dy = dy_all[:, sl]
            sr = _sig(mr_ref[...].astype(F32))
            sa = _sig(ma_ref[...].astype(F32))
            dar_ref[:, sl] = (dy * sr).astype(BF16)
            daa_ref[:, sl] = (dy * sa).astype(BF16)
            dp_ref[:, MR - RG + j * hw:MR - RG + (j + 1) * hw] = (
                dy * ar_ref[:, sl].astype(F32) * sr * (1.0 - sr)).astype(BF16)
            dp_ref[:, MA - RG + j * hw:MA - RG + (j + 1) * hw] = (
                dy * aa_ref[:, sl].astype(F32) * sa * (1.0 - sa)).astype(BF16)
        da_ret = dar_ref[...]

        @pl.when(pl.program_id(0) == 0)
        def _():
            acca_ref[...] = jnp.zeros_like(acca_ref)

        for j in range(2):
            sl = slice(j * hw, (j + 1) * hw)
            acca_ref[sl, :] += _dot(ya_ref[:, sl], daa_ref[...], TN)

        for h, g_ref in enumerate((rg0, rg1, rg2, rg3)):
            sl = slice(h * DV, (h + 1) * DV)
            dy = _dot(da_ret, wr_ref[sl, :], NT)
            g = g_ref[...].astype(F32)
            o = of_ref[:, sl].astype(F32) + ob_ref[:, sl].astype(F32)
            r = lax.rsqrt(jnp.mean(o * o, axis=-1, keepdims=True) + EPS)
            on = o * r
            sg = _sig(g)
            don = dy * (g * sg)
            dp_ref[:, sl] = (dy * on * (sg * (1.0 + g * (1.0 - sg)))).astype(BF16)
            dor_ref[:, sl] = (r * (don - on * jnp.mean(on * don, axis=-1, keepdims=True))).astype(BF16)
        dy_all = _dot(daa_ref[...], wa_ref[...], NT)
        dl_ref[...] = jnp.zeros_like(dl_ref)
        for j, g_ref in enumerate((ag0, ag1)):
            sl = slice(j * hw, (j + 1) * hw)
            dy = dy_all[:, sl]
            g = g_ref[...].astype(F32)
            sg = _sig(g)
            dao = dy * (g * sg)
            dao_ref[:, sl] = dao.astype(BF16)
            prod = dao * oa_ref[:, sl]
            for r in range(hw // HD):
                dl_ref[:, j * 128 + r:j * 128 + r + 1] = jnp.sum(prod[:, r * HD:(r + 1) * HD], axis=-1, keepdims=True)
            dp_ref[:, AG - RG + j * hw:AG - RG + (j + 1) * hw] = (
                dy * oa_ref[:, sl] * (sg * (1.0 + g * (1.0 - sg)))).astype(BF16)

        @pl.when(pl.program_id(0) == t_rows // tm - 1)
        def _():
            gwa_ref[...] = acca_ref[...].astype(BF16)

    def gate(h):
        return pl.BlockSpec((tm, DV), lambda i: (i, RG // DV + h))

    def whole(rows):
        return pl.BlockSpec((rows, D), lambda i: (0, 0), pipeline_mode=pl.Buffered(1))

    row = pl.BlockSpec((tm, D), lambda i: (i, 0))
    wide = pl.BlockSpec((tm, RH * DV), lambda i: (i, 0))
    return pl.pallas_call(
        body, name="bwd_branches", grid=(t_rows // tm,),
        in_specs=[row, whole(D), whole(RH * DV), whole(D)] + _gate_specs(tm, MR) + _gate_specs(tm, MA) + [row, row]
        + [gate(h) for h in range(RH)] + [wide, wide] + _gate_specs(tm, AG) + [row, row],
        out_specs=[row, wide, row, pl.BlockSpec((tm, HKV * 128), lambda i: (i, 0)),
                   pl.BlockSpec((pl.Element(tm), pl.Element(IN_COLS - RG)), lambda i: (i * tm, RG)), whole(D)],
        out_shape=[_sds((t_rows, D), BF16), _sds((t_rows, RH * DV), BF16), _sds((t_rows, D), BF16),
                   _sds((t_rows, HKV * 128), F32), _sds((rows_all, IN_COLS), BF16), _sds((D, D), BF16)],
        scratch_shapes=[pltpu.VMEM((tm, D), BF16), pltpu.VMEM((D, D), F32)],
        compiler_params=_params(("arbitrary",)),
    )(dout16, w_out16, w_o_ret16, w_o_att16, px, px, px, px, a_ret, a_att, *([px] * RH), o_f, o_b, px, px, o_att,
      yatt16)


def _att_bwd(q16, kx16, kc16, px, dao16, delta, lse, q_norm_w, k_norm_w, cos, sin, dp_all, nb, seq, cx, tq, after=()):
    t_rows = nb * seq
    nq = seq // tq
    rep = HQ // HKV
    gw = rep * HD
    scale = HD ** -0.5

    def body(q_ref, kx_ref, kc_ref, vx_ref, vc_ref, dao_ref, dl_ref, l_ref, xq_ref, w_ref, c_ref, s_ref,
             xk_ref, xkc_ref, wk_ref, ck_ref, sk_ref, *rest):
        daq_ref, gq_ref, gk_ref, dkx_ref, dvx_ref, dkc_ref, dvc_ref = rest[1 + len(after):8 + len(after)]
        accs = rest[8 + len(after):]
        i = pl.program_id(2)
        head0 = jnp.logical_and(pl.program_id(0) == 0, pl.program_id(1) == 0)
        first = jnp.logical_and(head0, i == 0)
        gq = jnp.zeros((1, HD), F32)
        kx = kx_ref[...]
        kc = kc_ref[...]
        vx = vx_ref[...].astype(BF16)
        vc = vc_ref[...].astype(BF16)
        @pl.when(i == 0)
        def _():
            for acc in accs:
                acc[...] = jnp.zeros_like(acc)

        dkx, dvx, dkc, dvc = [acc[...] for acc in accs]
        for r in range(rep):
            sl = slice(r * HD, (r + 1) * HD)
            q = q_ref[:, sl]
            lr = l_ref[:, r:r + 1]
            p1 = jnp.exp2(_dot(q, kx, NT) * SM_C - lr)
            p2 = jnp.exp2(_dot(q, kc, NT) * SM_C - lr)
            da16 = dao_ref[:, sl]
            delta = dl_ref[:, r:r + 1]
            ds1 = (p1 * (_dot(da16, vx, NT) - delta)).astype(BF16)
            ds2 = (p2 * (_dot(da16, vc, NT) - delta)).astype(BF16)
            dq = (_dot(ds1, kx) + _dot(ds2, kc)) * scale
            dkx += _dot(q, ds1, TN)
            dkc += _dot(q, ds2, TN)
            dvx += _dot(da16, p1.astype(BF16), TN)
            dvc += _dot(da16, p2.astype(BF16), TN)
            dt = dq * c_ref[...] + _swap_pairs(dq * s_ref[...])
            xv = xq_ref[:, sl].astype(F32)
            rn = lax.rsqrt(jnp.mean(xv * xv, axis=-1, keepdims=True) + EPS)
            xh = xv * rn
            dxh = dt * w_ref[...]
            daq_ref[:, sl] = (rn * (dxh - xh * jnp.mean(dxh * xh, axis=-1, keepdims=True))).astype(BF16)
            gq += jnp.sum(dt * xh, axis=0, keepdims=True)
        for acc, val in zip(accs, (dkx, dvx, dkc, dvc)):
            acc[...] = val

        @pl.when(first)
        def _():
            gq_ref[...] = gq

        @pl.when(jnp.logical_not(first))
        def _():
            gq_ref[...] += gq

        def k_back(dk, x_ref, dk_ref):
            xv = x_ref[...].astype(F32)
            rn = lax.rsqrt(jnp.mean(xv * xv, axis=-1, keepdims=True) + EPS)
            xh = xv * rn
            dxh = dk * wk_ref[...]
            dk_ref[...] = (rn * (dxh - xh * jnp.mean(dxh * xh, axis=-1, keepdims=True))).astype(BF16)
            return jnp.sum(dk * xh, axis=0, keepdims=True)

        @pl.when(i == nq - 1)
        def _():
            dvx_ref[...] = dvx.T.astype(BF16)
            dvc_ref[...] = dvc.T.astype(BF16)
            dk = dkx.T * scale
            gk = (k_back(dk * ck_ref[...] + _swap_pairs(dk * sk_ref[...]), xk_ref, dkx_ref)
                  + k_back(dkc.T * scale, xkc_ref, dkc_ref))

            @pl.when(head0)
            def _():
                gk_ref[...] = gk

            @pl.when(jnp.logical_not(head0))
            def _():
                gk_ref[...] += gk

    qblk = pl.BlockSpec((tq, gw), lambda b, g, i: (b * nq + i, g))
    kxb = pl.BlockSpec((seq, HD), lambda b, g, i: (b, g))
    kcb = pl.BlockSpec((cx, HD), lambda b, g, i: (b, g))
    table = pl.BlockSpec((tq, HD), lambda b, g, i: (i, 0))
    tables = pl.BlockSpec((seq, HD), lambda b, g, i: (0, 0))
    one = pl.BlockSpec((1, HD), lambda b, g, i: (0, 0))
    lane = pl.BlockSpec((tq, 128), lambda b, g, i: (b * nq + i, g))
    return pl.pallas_call(
        body, name="att_bwd", grid=(nb, HKV, nq),
        in_specs=[qblk,
                  pl.BlockSpec((seq, HD), lambda b, g, i: (b, g)),
                  pl.BlockSpec((cx, HD), lambda b, g, i: (b, g)),
                  pl.BlockSpec((seq, HD), lambda b, g, i: (b, AV // HD + g)),
                  pl.BlockSpec((cx, HD), lambda b, g, i: (t_rows // cx + b, AV // HD + g)),
                  qblk, lane, lane,
                  pl.BlockSpec((tq, gw), lambda b, g, i: (b * nq + i, AQ // gw + g)), one, table, table,
                  pl.BlockSpec((seq, HD), lambda b, g, i: (b, AK // HD + g)),
                  pl.BlockSpec((cx, HD), lambda b, g, i: (t_rows // cx + b, AK // HD + g)), one, tables, tables]
        + [pl.BlockSpec(memory_space=pl.ANY)] * (1 + len(after)),
        out_specs=[pl.BlockSpec((tq, gw), lambda b, g, i: (b * nq + i, AQ // gw + g)), one, one, kxb, kxb, kcb, kcb],
        out_shape=[_sds(dp_all.shape, BF16), _sds((1, HD), F32), _sds((1, HD), F32), _sds((t_rows, HKV * HD), BF16),
                   _sds((t_rows, HKV * HD), BF16), _sds((nb * cx, HKV * HD), BF16), _sds((nb * cx, HKV * HD), BF16)],
        scratch_shapes=[pltpu.VMEM((HD, seq), F32), pltpu.VMEM((HD, seq), F32), pltpu.VMEM((HD, cx), F32),
                        pltpu.VMEM((HD, cx), F32)],
        input_output_aliases={17: 0},
        compiler_params=_params(("arbitrary", "arbitrary", "arbitrary")),
    )(q16, kx16, kc16, px, px, dao16, delta, lse, px, q_norm_w, cos, sin, px, px, k_norm_w, cos, sin, dp_all, *after)


def _ret_bwd(px, lg, do16, hist_f, hist_b, nb, nc, cx):
    t_rows = nb * nc * CH

    def body(lg_ref, *refs):
        ins = (refs[0:5], refs[7:12])
        do_refs = (refs[5], refs[12])
        h_refs = (refs[6], refs[13])
        ctx_refs = refs[14:17]
        outs = (refs[17:20], refs[20:23])
        dck_ref, dcv_ref, dlg_ref = refs[23:26]
        dss = (refs[26], refs[27])
        c = pl.program_id(1)

        @pl.when(c == 0)
        def _():
            dss[0][...] = jnp.zeros_like(dss[0])
            dss[1][...] = jnp.zeros_like(dss[1])
            dlg_ref[...] = jnp.zeros_like(dlg_ref)

        for d in range(2):
            dq_ref, dk_ref, dv_ref = outs[d]
            for h in range(RH):
                lg_d = lg_ref[d, h]
                mask, relf, qd, qe, kd, ke = _decays(lg_d, d == 0)
                g_ch = jnp.exp(lg_d * CH)
                q, k, v16 = _head_qkv(ins[d], h)
                q16 = q.astype(BF16)
                k16 = k.astype(BF16)
                do16v = do_refs[d][:, h * DV:(h + 1) * DV]
                st16 = h_refs[d][h]
                dst = dss[d][h]
                dst16 = dst.astype(BF16)
                a = _dot(q16, k16, NT) * mask
                dp = _dot(do16v, v16, NT)
                da16 = (dp * mask).astype(BF16)
                dq_cross = _dot(do16v, st16, NT) * qd
                dq_ref[:, h * DK:(h + 1) * DK] = (_dot(da16, k16) + dq_cross).astype(BF16)
                dk_state = _dot(v16, dst16, NT) * kd
                dk_ref[:, h * DK:(h + 1) * DK] = ((_dot(da16, q16, TN) + dk_state) * (DK ** -0.5)).astype(BF16)
                dv = _dot(a.astype(BF16), do16v, TN) + _dot((k * kd).astype(BF16), dst16)
                dv_ref[:, h * DV:(h + 1) * DV] = dv.astype(BF16)
                dlg = (jnp.sum(relf * a * dp)
                       + jnp.sum(qe * jnp.sum(q * dq_cross, axis=-1, keepdims=True))
                       + jnp.sum(ke * jnp.sum(k * dk_state, axis=-1, keepdims=True))
                       + CH * g_ch * jnp.sum(dst * st16.astype(F32)))
                row = d * RH + h
                dlg_ref[row:row + 1, :] += jnp.broadcast_to(dlg, (1, 128))
                dss[d][h] = g_ch * dst + _dot((q * qd).astype(BF16), do16v, TN)

        @pl.when(c == nc - 1)
        def _():
            pos = lax.broadcasted_iota(jnp.int32, (cx, 1), 0).astype(F32)
            for h in range(RH):
                k, v16 = _ctx_kv(ctx_refs, h)
                dk = jnp.zeros((cx, DK), F32)
                dv = jnp.zeros((cx, DV), F32)
                for d, e in enumerate((cx - 1.0 - pos, pos)):
                    w = jnp.exp(lg_ref[d, h] * e)
                    ds16 = dss[d][h].astype(BF16)
                    t = _dot(v16, ds16, NT)
                    dk += t * w
                    dv += _dot((k * w).astype(BF16), ds16)
                    dlg = jnp.sum(e * w * jnp.sum(k * t, axis=-1, keepdims=True))
                    row = d * RH + h
                    dlg_ref[row:row + 1, :] += jnp.broadcast_to(dlg, (1, 128))
                dck_ref[:, h * DK:(h + 1) * DK] = (dk * (DK ** -0.5)).astype(BF16)
                dcv_ref[:, h * DV:(h + 1) * DV] = dv.astype(BF16)

    def fw(b, c):
        return b * nc + nc - 1 - c

    def bw(b, c):
        return b * nc + c

    def rows(rowf, width):
        return pl.BlockSpec((CH, width), lambda b, c: (rowf(b, c), 0))

    def hist(rowf):
        return pl.BlockSpec((None, None, RH, DK, DV), lambda b, c: (b, rowf(0, c), 0, 0, 0))

    in_specs = [pl.BlockSpec(memory_space=pltpu.SMEM)]
    out_specs = []
    for rowf in (fw, bw):
        in_specs += _wide_specs(rowf) + [rows(rowf, RH * DV), hist(rowf)]
        out_specs += [rows(rowf, RH * DK), rows(rowf, RH * DK), rows(rowf, RH * DV)]
    in_specs += _ctx_specs(t_rows, cx)
    out_specs += [pl.BlockSpec((cx, RH * DK), lambda b, c: (b, 0)), pl.BlockSpec((cx, RH * DV), lambda b, c: (b, 0)),
                  pl.BlockSpec((None, 8, 128), lambda b, c: (b, 0, 0))]
    qk = _sds((t_rows, RH * DK), BF16)
    vv = _sds((t_rows, RH * DV), BF16)
    return pl.pallas_call(
        body, name="ret_bwd", grid=(nb, nc), in_specs=in_specs, out_specs=out_specs,
        out_shape=[qk, qk, vv, qk, qk, vv, _sds((nb * cx, RH * DK), BF16), _sds((nb * cx, RH * DV), BF16),
                   _sds((nb, 8, 128), F32)],
        scratch_shapes=[pltpu.VMEM((RH, DK, DV), F32), pltpu.VMEM((RH, DK, DV), F32)],
        compiler_params=_params(("parallel", "arbitrary")),
    )(lg, *([px] * 5), do16, hist_f, *([px] * 5), do16, hist_b, *([px] * 3))


def _assemble_lat(dp_all, dk_f, dk_b, dv_f, dv_b, dak16, dvx, dq_f, dq_b, tm):
    t_rows = dk_f.shape[0]

    def body(_, dkf, dkb, dvf, dvb, dak, dav, dqf, dqb, o_ref):
        o_ref[:, RK:RK + RH * DK] = (dkf[...].astype(F32) + dkb[...].astype(F32)).astype(BF16)
        o_ref[:, RV:RV + RH * DV] = (dvf[...].astype(F32) + dvb[...].astype(F32)).astype(BF16)
        o_ref[:, AK:AK + HKV * HD] = dak[...]
        o_ref[:, AV:AV + HKV * HD] = dav[...]
        o_ref[:, RQ:RQ + RH * DK] = (dqf[...].astype(F32) + dqb[...].astype(F32)).astype(BF16)

    args = (dk_f, dk_b, dv_f, dv_b, dak16, dvx, dq_f, dq_b)
    return pl.pallas_call(
        body, name="assemble_lat", grid=(t_rows // tm,),
        in_specs=[pl.BlockSpec(memory_space=pl.ANY)]
        + [pl.BlockSpec((tm, a.shape[1]), lambda i: (i, 0)) for a in args],
        out_specs=pl.BlockSpec((tm, RG), lambda i: (i, 0)), out_shape=_sds(dp_all.shape, BF16),
        input_output_aliases={0: 0},
        compiler_params=_params(("parallel",)),
    )(dp_all, *args)


def _assemble_ctx(dp_all, dck16, dcv16, dcak16, dvc, t_rows, tm):
    c_rows = dck16.shape[0]
    rb = t_rows // tm

    def body(_, dck, dcv, dcak, dcav, o_ref):
        o_ref[:, RK:RK + RH * DK] = dck[...]
        o_ref[:, RV:RV + RH * DV] = dcv[...]
        o_ref[:, AK:AK + HKV * HD] = dcak[...]
        o_ref[:, AV:AV + HKV * HD] = dcav[...]
        o_ref[:, KV_COLS:] = jnp.zeros((tm, IN_COLS - KV_COLS), BF16)

    args = (dck16, dcv16, dcak16, dvc)
    return pl.pallas_call(
        body, name="assemble_ctx", grid=(c_rows // tm,),
        in_specs=[pl.BlockSpec(memory_space=pl.ANY)]
        + [pl.BlockSpec((tm, a.shape[1]), lambda i: (i, 0)) for a in args],
        out_specs=pl.BlockSpec((tm, IN_COLS), lambda i: (rb + i, 0)), out_shape=_sds(dp_all.shape, BF16),
        input_output_aliases={0: 0},
        compiler_params=_params(("parallel",)),
    )(dp_all, *args)


def _norm_bwd(dh, x2, mod3, norm_w, dxn, row_off, rows_per_group, group0, tm, name):
    with_dx = dxn is not None
    rows = x2.shape[0]
    rb0 = row_off // tm
    bpg = rows_per_group // tm
    ngroups = rows // rows_per_group

    def body(*refs):
        if with_dx:
            dh_ref, x_ref, sc_ref, nw_ref, dxn_ref, dx_ref, dsh_ref, dsc_ref, dnw_ref = refs
        else:
            dh_ref, x_ref, sc_ref, nw_ref, dsh_ref, dsc_ref, dnw_ref = refs
        i = pl.program_id(0)
        dhv = dh_ref[...]
        xv = x_ref[...]
        nw = nw_ref[...]
        r = lax.rsqrt(jnp.mean(xv * xv, axis=-1, keepdims=True) + EPS)
        xh = xv * r
        dm = dhv * (1.0 + sc_ref[...])
        dsh = jnp.sum(dhv, axis=0, keepdims=True)
        dsc = jnp.sum(dhv * (xh * nw), axis=0, keepdims=True)
        dnw = jnp.sum(dm * xh, axis=0, keepdims=True)
        if with_dx:
            dxh = dm * nw
            dx_ref[...] = dxn_ref[...] + r * (dxh - xh * jnp.mean(dxh * xh, axis=-1, keepdims=True))

        @pl.when(i % bpg == 0)
        def _():
            dsh_ref[...] = dsh
            dsc_ref[...] = dsc

        @pl.when(i % bpg != 0)
        def _():
            dsh_ref[...] += dsh
            dsc_ref[...] += dsc

        @pl.when(i == 0)
        def _():
            dnw_ref[...] = dnw

        @pl.when(i > 0)
        def _():
            dnw_ref[...] += dnw

    grp = pl.BlockSpec((None, 1, D), lambda i: (i // bpg, 0, 0))
    in_specs = [pl.BlockSpec((tm, D), lambda i: (rb0 + i, 0)), pl.BlockSpec((tm, D), lambda i: (i, 0)),
                pl.BlockSpec((None, 1, D), lambda i: (group0 + i // bpg, 0, 1)),
                pl.BlockSpec((1, D), lambda i: (0, 0))]
    args = [dh, x2, mod3, norm_w]
    out_specs = [grp, grp, pl.BlockSpec((1, D), lambda i: (0, 0))]
    out_shape = [_sds((ngroups, 1, D), F32), _sds((ngroups, 1, D), F32), _sds((1, D), F32)]
    if with_dx:
        in_specs.append(pl.BlockSpec((tm, D), lambda i: (i, 0)))
        args.append(dxn)
        out_specs.insert(0, pl.BlockSpec((tm, D), lambda i: (i, 0)))
        out_shape.insert(0, _sds((rows, D), F32))
    return pl.pallas_call(
        body, name=name, grid=(rows // tm,), in_specs=in_specs, out_specs=out_specs, out_shape=out_shape,
        compiler_params=_params(("arbitrary",)),
    )(*args)


def _small_final(dmod_all, dmodc_parts, c_rows, dm_loc_rows, nw_parts, misc_parts, c_ctx, r_pad, w_ada16):
    loc = dm_loc_rows.shape[1]

    def body(dm_ref, dmc_ref, c_ref, dml_ref, nwp_ref, mp_ref, cc_ref, r_ref, w_ref,
             gb_ref, gc_ref, gnw_ref, misc_ref, gwa_ref):
        dmc = jnp.sum(dmc_ref[...], axis=0, keepdims=True)
        gb_ref[...] = jnp.sum(dm_ref[...], axis=0, keepdims=True) + dmc
        dsc = _dot(jnp.broadcast_to(dmc, (8, 3 * D)).astype(BF16), w_ref[...], NT)[0:1, :]
        gc_ref[...] = dsc * _dsilu(cc_ref[...])
        gnw_ref[...] = jnp.sum(nwp_ref[...], axis=0, keepdims=True)
        misc = jnp.sum(mp_ref[...], axis=0, keepdims=True)
        y = jnp.exp2(r_ref[...])
        lane = lax.broadcasted_iota(jnp.int32, (1, D), 1)
        is_decay = jnp.logical_and(lane >= 2 * HD, lane < 2 * HD + 2 * RH)
        misc_ref[...] = misc * jnp.where(is_decay, -(y * np.float32(np.log(2.0))) / (1.0 - y), 1.0)
        gwa_ref[...] = _dot(_silu(c_ref[...]).astype(BF16), dml_ref[...].astype(BF16), TN)

    return pl.pallas_call(
        body, name="small_final",
        out_shape=[_sds((1, 3 * D), F32), _sds((1, D), F32), _sds((1, D), F32), _sds((1, D), F32), _sds((D, loc), F32)],
        compiler_params=pltpu.CompilerParams(vmem_limit_bytes=VMEM_LIMIT),
    )(dmod_all, dmodc_parts, c_rows, dm_loc_rows, nw_parts, misc_parts, c_ctx, r_pad, w_ada16)


def _adamw_math(w, g, m, v):
    nm = B1 * m + (1.0 - B1) * g
    nv = B2 * v + (1.0 - B2) * (g * g)
    return -LR * ((nm / (1.0 - B1 ** STEP)) / (jnp.sqrt(nv / (1.0 - B2 ** STEP)) + ADAM_EPS) + WD * w), nm, nv


def _adamw(w, g, m, v, name):
    rows, cols = w.shape
    tm = _pick(rows, 448, 8)

    def body(w_ref, g_ref, m_ref, v_ref, d_ref, nm_ref, nv_ref):
        d_ref[...], nm_ref[...], nv_ref[...] = _adamw_math(w_ref[...], g_ref[...], m_ref[...], v_ref[...])

    blk = pl.BlockSpec((tm, cols), lambda i: (i, 0))
    return pl.pallas_call(
        body, name=name, grid=(rows // tm,), in_specs=[blk] * 4, out_specs=[blk] * 3,
        out_shape=[_sds((rows, cols), F32)] * 3, compiler_params=_params(("parallel",)),
    )(w, g, m, v)


def _adamw_small(wgmv):
    n = len(wgmv)

    def body(*refs):
        ins, outs = refs[:4 * n], refs[4 * n:]
        for k in range(n):
            w, g, m, v = [r[...] for r in ins[4 * k:4 * k + 4]]
            outs[3 * k][...], outs[3 * k + 1][...], outs[3 * k + 2][...] = _adamw_math(w, g, m, v)

    out = pl.pallas_call(
        body, name="adamw_small", out_shape=[_sds(t[0].shape, F32) for t in wgmv for _ in range(3)],
    )(*[a for t in wgmv for a in t])
    return [out[3 * k:3 * k + 3] for k in range(n)]


def _mesh_pos():
    return lax.axis_index("x"), lax.axis_index("y"), lax.axis_index("c")


def _all_gather(arrs, name):
    n = len(arrs)

    def body(*refs):
        ins, outs = refs[:n], refs[n:2 * n]
        send_sems, recv_sems, local_sems = refs[2 * n:]
        x, y, c = _mesh_pos()
        me, sib = (x, y, c), (x, y, 1 - c)
        chips = [(1 - x, y), (x, 1 - y), (1 - x, 1 - y)]

        def slot(p):
            return 4 * p[0] + 2 * p[1] + p[2]

        def copy(a, k, block, to, own):
            dst = outs[a].at[slot(block)]
            return pltpu.make_async_remote_copy(
                src_ref=ins[a] if own else dst, dst_ref=dst, send_sem=send_sems.at[a, k], recv_sem=recv_sems.at[a, k],
                device_id=to, device_id_type=MESH_T)

        mine = [pltpu.make_async_copy(ins[a], outs[a].at[slot(me)], local_sems.at[a]) for a in range(n)]
        for cp in mine:
            cp.start()
        first = []
        for a in range(n):
            first.append(copy(a, 0, me, sib, True))
            first += [copy(a, 1 + j, me, (*chip, c), True) for j, chip in enumerate(chips)]
        for cp in first:
            cp.start()
        passed = []
        for j, chip in enumerate(chips):
            for a in range(n):
                copy(a, 1 + j, (*chip, c), me, False).wait_recv()
                fwd = copy(a, 4 + j, (*chip, c), sib, False)
                fwd.start()
                passed.append(fwd)
        for a in range(n):
            copy(a, 0, sib, me, False).wait_recv()
            for j, chip in enumerate(chips):
                copy(a, 4 + j, (*chip, 1 - c), me, False).wait_recv()
        for cp in first + passed:
            cp.wait_send()
        for cp in mine:
            cp.wait()

    hbm = pl.BlockSpec(memory_space=pl.ANY)
    return pl.pallas_call(
        body, name=name, in_specs=[hbm] * n, out_specs=[hbm] * n,
        out_shape=[_sds((N_DEV,) + a.shape, a.dtype) for a in arrs],
        scratch_shapes=[pltpu.SemaphoreType.DMA((n, 7)), pltpu.SemaphoreType.DMA((n, 7)), pltpu.SemaphoreType.DMA((n,))],
    )(*arrs)


def _pair_add(parts, gots, core, name):
    n = len(parts)
    cols = parts[0].shape[2]
    tiles = min(p.shape[1] for p in parts) // _pick(min(p.shape[1] for p in parts), 672, 16)

    def body(core_ref, *refs):
        for p_ref, g_ref, o_ref in zip(refs[:n], refs[n:2 * n], refs[2 * n:]):
            o_ref[...] = (p_ref[...].astype(F32) + g_ref[...].astype(F32)).astype(BF16)

    def blk(p):
        return pl.BlockSpec((None, p.shape[1] // tiles, cols), lambda k, i, cr: (k, i, 0))

    return pl.pallas_call(
        body, name=name,
        grid_spec=pltpu.PrefetchScalarGridSpec(
            num_scalar_prefetch=1, grid=(4, tiles),
            in_specs=[pl.BlockSpec((None, None, p.shape[1] // tiles, cols), lambda k, i, cr: (k, cr[0], i, 0))
                      for p in parts] + [blk(p) for p in parts],
            out_specs=[blk(p) for p in parts]),
        out_shape=[_sds((4,) + p.shape[1:], BF16) for p in parts], compiler_params=_params(("parallel", "parallel")),
    )(core, *[p.reshape(4, 2, *p.shape[1:]) for p in parts], *gots)


def _chip_sum_adamw(pair_sums, landed, chip, w, m, v, name):
    _, rows, cols = pair_sums.shape
    tm = _pick(rows, 448, 16)

    def body(chip_ref, s_ref, l_ref, w_ref, m_ref, v_ref, g_ref, d_ref, nm_ref, nv_ref):
        acc = s_ref[...].astype(F32)
        for j in range(3):
            acc = acc + l_ref[j].astype(F32)
        g_ref[...] = acc
        d_ref[...], nm_ref[...], nv_ref[...] = _adamw_math(w_ref[...], acc, m_ref[...], v_ref[...])

    blk = pl.BlockSpec((tm, cols), lambda i, ch: (i, 0))
    return pl.pallas_call(
        body, name=name,
        grid_spec=pltpu.PrefetchScalarGridSpec(
            num_scalar_prefetch=1, grid=(rows // tm,),
            in_specs=[pl.BlockSpec((None, tm, cols), lambda i, ch: (ch[0], i, 0)),
                      pl.BlockSpec((3, tm, cols), lambda i, ch: (0, i, 0)), blk, blk, blk],
            out_specs=[blk] * 4),
        out_shape=[_sds((rows, cols), F32)] * 4, compiler_params=_params(("parallel",)),
    )(chip, pair_sums, landed, w, m, v)


_HBM = pl.BlockSpec(memory_space=pltpu.HBM)
_SEM = pl.BlockSpec(memory_space=pltpu.SEMAPHORE)
_EFFECT = pltpu.SideEffectType.DATAFLOW_SIDE_EFFECTING


def _chip_routes(n):
    def plan(x, y, c):
        routes = []
        for a in range(n):
            for j in range(1, 4):
                px, py = x ^ (j >> 1), y ^ (j & 1)
                routes.append((a, 2 * px + py, (px, py, c), j - 1))
        return routes
    return plan, 3 * n


def _pair_routes(n):
    def plan(x, y, c):
        return [(a, 2 * k + 1 - c, (x, y, 1 - c), k) for a in range(n) for k in range(4)]
    return plan, 4 * n


def _bcast_routes(n):
    def plan(x, y, c):
        routes = []
        for a in range(n):
            for k in range(1, N_DEV):
                peer = (x ^ ((k >> 2) & 1), y ^ ((k >> 1) & 1), c ^ (k & 1))
                routes.append((a, 0, peer, 4 * x + 2 * y + c))
        return routes
    return plan, 7 * n


def _route_copies(srcs, lands, send_sems, recv_sems, routes):
    return [pltpu.make_async_remote_copy(
        src_ref=srcs[a].at[sb], dst_ref=lands[a].at[lb], send_sem=send_sems.at[r], recv_sem=recv_sems.at[r],
        device_id=peer, device_id_type=MESH_T) for r, (a, sb, peer, lb) in enumerate(routes)]


def _exchange_start(srcs, lands, routes, name, after=()):
    plan, count = routes
    n = len(srcs)
    n_in = 2 * n + len(after)

    def body(*refs):
        send_sems, recv_sems = refs[n_in], refs[n_in + 1]
        token = refs[-1]
        for cp in _route_copies(refs[:n], refs[n:2 * n], send_sems, recv_sems, plan(*_mesh_pos())):
            cp.start()
        token[...] = jnp.zeros_like(token)

    args = [pltpu.with_memory_space_constraint(a, pltpu.HBM) for a in list(srcs) + list(lands)]
    out = pl.pallas_call(
        body, name=name,
        out_shape=(pltpu.SemaphoreType.DMA((count,)), pltpu.SemaphoreType.DMA((count,)),
                   *[pltpu.HBM(a.shape, a.dtype) for a in args], _sds((8, 128), F32)),
        in_specs=[_HBM] * (2 * n) + [pl.BlockSpec(memory_space=pl.ANY)] * len(after),
        out_specs=(_SEM, _SEM, *([_HBM] * (2 * n)), pl.BlockSpec(memory_space=pltpu.VMEM)),
        input_output_aliases={i: 2 + i for i in range(2 * n)},
        compiler_params=pltpu.CompilerParams(has_side_effects=_EFFECT),
    )(*args, *after)
    return (out[0], out[1], list(out[2:2 + 2 * n]), routes), out[-1]


def _exchange_wait(state, after, name):
    send_sems, recv_sems, bufs, (plan, count) = state
    n = len(bufs) // 2

    def body(*refs):
        send_s, recv_s = refs[2 * n], refs[2 * n + 1]
        for cp in _route_copies(refs[:n], refs[n:2 * n], send_s, recv_s, plan(*_mesh_pos())):
            cp.wait_send()
            cp.wait_recv()

    out = pl.pallas_call(
        body, name=name, out_shape=tuple(pltpu.HBM(a.shape, a.dtype) for a in bufs),
        in_specs=[_HBM] * (2 * n) + [_SEM, _SEM, pl.BlockSpec(memory_space=pl.ANY)], out_specs=tuple([_HBM] * (2 * n)),
        input_output_aliases={i: i for i in range(2 * n)},
        compiler_params=pltpu.CompilerParams(has_side_effects=_EFFECT),
    )(*bufs, send_sems, recv_sems, after)
    return list(out[:n]), list(out[n:])


def _group_routes(js):
    def plan(x, y, c):
        return [(0, 0, (x ^ (j >> 1), y ^ (j & 1), c), 2 * j + c) for j in js]
    return plan, len(js)


def _pair_fill(groups, js, name, after=()):
    def body(*refs):
        g_ref, send_sems, recv_sems = refs[-3:]
        x, y, c = _mesh_pos()
        sends = []
        for n, j in enumerate(js):
            mine = g_ref.at[2 * j + c]
            sends.append(pltpu.make_async_remote_copy(
                src_ref=mine, dst_ref=mine, send_sem=send_sems.at[n], recv_sem=recv_sems.at[n],
                device_id=(x, y, 1 - c), device_id_type=MESH_T))
        for cp in sends:
            cp.start()
        for n, j in enumerate(js):
            pltpu.make_async_remote_copy(
                src_ref=g_ref.at[2 * j + c], dst_ref=g_ref.at[2 * j + 1 - c], send_sem=send_sems.at[n],
                recv_sem=recv_sems.at[n], device_id=(x, y, 1 - c), device_id_type=MESH_T).wait_recv()
        for cp in sends:
            cp.wait_send()

    hbm = pl.BlockSpec(memory_space=pl.ANY)
    return pl.pallas_call(
        body, name=name, in_specs=[hbm] * (1 + len(after)), out_specs=hbm, out_shape=_sds(groups.shape, groups.dtype),
        input_output_aliases={0: 0},
        scratch_shapes=[pltpu.SemaphoreType.DMA((len(js),)), pltpu.SemaphoreType.DMA((len(js),))],
    )(groups, *after)


def _in_proj_group(h_all, groups, j0, ng, chip, px_prev, after, name):
    rows_all = h_all.shape[0]
    gcols = IN_COLS // 4
    tm = _pick(rows_all, 1536, 128)
    g4 = groups.reshape(4, gcols, D)

    n_lead = (1 if px_prev is not None else 0) + len(after)
    lead = ([px_prev] if px_prev is not None else []) + list(after)

    def body(chip_ref, *refs):
        h_ref, w_ref, o_ref = refs[n_lead:]
        o_ref[...] = _dot(h_ref[...], w_ref[...], NT).astype(BF16)

    return pl.pallas_call(
        body, name=name,
        grid_spec=pltpu.PrefetchScalarGridSpec(
            num_scalar_prefetch=1, grid=(ng, rows_all // tm),
            in_specs=[pl.BlockSpec(memory_space=pl.ANY)] * n_lead
            + [pl.BlockSpec((tm, D), lambda n, i, ch: (i, 0)),
               pl.BlockSpec((None, gcols, D), lambda n, i, ch: (j0 + n, 0, 0))],
            out_specs=pl.BlockSpec((tm, gcols), lambda n, i, ch: (i, ch[0] ^ (j0 + n)))),
        out_shape=_sds((rows_all, IN_COLS), BF16),
        input_output_aliases={1: 0} if px_prev is not None else {},
        compiler_params=_params(("parallel", "parallel")),
    )(chip, *lead, h_all, g4)


def _d_h_groups(dp_all, groups, chip, i0, ni, dh_prev, after):
    rows_all = dp_all.shape[0]
    gcols = IN_COLS // 4
    tm = _D_H_ROWS
    g4 = groups.reshape(4, gcols, D)
    lead = ([dh_prev] if dh_prev is not None else []) + list(after)
    n_lead = len(lead)

    def body(chip_ref, *refs):
        a_ref, w_ref, o_ref = refs[n_lead:]
        j = pl.program_id(1)
        part = _dot(a_ref[...], w_ref[...])

        @pl.when(j == 0)
        def _():
            o_ref[...] = part

        @pl.when(j > 0)
        def _():
            o_ref[...] += part

    return pl.pallas_call(
        body, name="d_h_%d" % i0,
        grid_spec=pltpu.PrefetchScalarGridSpec(
            num_scalar_prefetch=1, grid=(ni, 4),
            in_specs=[pl.BlockSpec(memory_space=pl.ANY)] * n_lead
            + [pl.BlockSpec((tm, gcols), lambda i, j, ch: (i0 + i, ch[0] ^ j)),
               pl.BlockSpec((None, gcols, D), lambda i, j, ch: (j, 0, 0))],
            out_specs=pl.BlockSpec((tm, D), lambda i, j, ch: (i0 + i, 0))),
        out_shape=_sds((rows_all, D), F32),
        input_output_aliases={1: 0} if dh_prev is not None else {},
        compiler_params=_params(("parallel", "arbitrary")),
    )(chip, *lead, dp_all, g4)


def _reduce_scatter_send(parts, got, core, name):
    sums = _pair_add(parts, got, core, name + "_add")
    lands = [lax.empty((3,) + s_.shape[1:], BF16) for s_ in sums]
    return _exchange_start(sums, lands, _chip_routes(len(sums)), name + "_start")


def _reduce_scatter_finish(rs_state, after, chip, wmv, name):
    sums, landed = _exchange_wait(rs_state, after, name + "_wait")
    return [_chip_sum_adamw(s_, l_, chip, *t, "%s_adamw_%d" % (name, i))
            for i, (s_, l_, t) in enumerate(zip(sums, landed, wmv))]


def _local_step(x, c, ctx, norm_w, ret_log2_decay, q_norm_w, k_norm_w, loss_target,
                mod, proj_in, get_w_o, on_out_grads, on_in_grad, started=()):
    nb, seq, _ = x.shape
    cx = ctx.shape[1]
    t_rows, c_rows = nb * seq, nb * cx
    rows_all = t_rows + c_rows
    nc = seq // CH
    tm = _pick(seq, 256, 128)
    te = _pick(seq, 512, 128)
    assert cx % tm == 0 and t_rows % cx == 0 and seq % GRID_W == 0

    x2 = x.reshape(t_rows, D)
    ctx2 = ctx.reshape(c_rows, D)
    tgt = loss_target.reshape(t_rows, D)
    lg = _log_gamma(ret_log2_decay)
    cos, sin = _rope_tables(seq)

    mod3 = mod[:, None, :]
    h_all = _norm_fwd(x2, mod3, norm_w, rows_all, 0, seq, 0, None, te, "norm_fwd", after=started)
    h_all = _norm_fwd(ctx2, mod3, norm_w, rows_all, t_rows, c_rows, nb, h_all, tm, "norm_fwd_ctx")
    px = proj_in(h_all)
    o_f, o_b, hist_f, hist_b = _ret_fwd(px, lg, nb, nc, cx)
    q16 = _qk_prep(px, q_norm_w, cos, sin, t_rows, 0, AQ, HQ, 4, seq, te, "q_prep")
    kx16 = _qk_prep(px, k_norm_w, cos, sin, t_rows, 0, AK, HKV, HKV, seq, te, "k_prep")
    kc16 = _qk_prep(px, k_norm_w, None, None, c_rows, t_rows, AK, HKV, HKV, seq, tm, "kc_prep")
    o_att, yatt16, lse = _att_fwd(q16, kx16, kc16, px, nb, seq, cx, te)
    w_o_ret16, w_o_att16, w_out16 = get_w_o(lse)
    yret16, a_ret, a_att, dxn, dout16, dgate, loss_b, gw_out = _merge_out(
        o_f, o_b, yatt16, px, w_o_ret16, w_o_att16, w_out16, x2, tgt, mod3, nb, seq, tm)

    da_ret16, do16, dao16, delta, dp_all, gw_o_att = _bwd_branches(
        dout16, w_out16, w_o_ret16, w_o_att16, px, a_ret, a_att, o_f, o_b, o_att, yatt16, rows_all, tm)
    gw_o_ret = _matmul(yret16, da_ret16, ta=True, tm=D, tn=D, tk=D, out_dtype=BF16, name="gw_o_ret")
    out_send, out_started = on_out_grads([gw_o_ret, gw_o_att, gw_out])
    dp_all, gq, gk, dak16, dav16, dcak16, dcav16 = _att_bwd(q16, kx16, kc16, px, dao16, delta, lse, q_norm_w, k_norm_w,
                                                            cos, sin, dp_all, nb, seq, cx, te, after=out_started)
    out_state, out_sent = out_send(gq)
    dq_f, dk_f, dv_f, dq_b, dk_b, dv_b, dck16, dcv16, dlg_scan = _ret_bwd(px, lg, do16, hist_f, hist_b, nb, nc, cx)
    dp_all = _assemble_lat(dp_all, dk_f, dk_b, dv_f, dv_b, dak16, dav16, dq_f, dq_b, tm)
    dp_all = _assemble_ctx(dp_all, dck16, dcv16, dcak16, dcav16, t_rows, tm)
    gw_in_t = _matmul(dp_all, h_all, ta=True, tm=1536, tn=D, tk=2304, out_dtype=BF16, name="gw_in", after=out_sent)
    in_state, dh = on_in_grad(gw_in_t, dp_all)
    grad_x, dsh, dsc, gnw_lat = _norm_bwd(dh, x2, mod3, norm_w, dxn, 0, seq, 0, te, "norm_bwd")
    dsh_c, dsc_c, gnw_ctx = _norm_bwd(dh, ctx2, mod3, norm_w, None, t_rows, c_rows, nb, tm, "norm_bwd_ctx")

    dlg = jnp.sum(dlg_scan[:, :, 0], axis=0).reshape(1, 2 * RH)
    misc = jnp.concatenate([gq, gk, dlg, jnp.sum(loss_b[:, 0, 0]).reshape(1, 1),
                            jnp.zeros((1, D - 2 * HD - 2 * RH - 1), F32)], axis=1)
    rows = []
    for b in range(nb):
        rows += [dsh[b], dsc[b], dgate[b]]
    rows += [dsh_c[0], dsc_c[0]] + [c[b:b + 1] for b in range(nb)] + [gnw_lat + gnw_ctx, misc]
    payload = jnp.concatenate(rows + [jnp.zeros((PAY_ROWS - len(rows), D), F32)], axis=0)
    return grad_x.reshape(nb, seq, D), out_state, in_state, payload


def _finish_small(gathered, nb, c_ctx, ret_log2_decay, w_ada16, dev):
    n_dev = gathered.shape[0]
    loc = 3 * D // n_dev
    dmod_all = gathered[:, :3 * nb].reshape(n_dev * nb, 3 * D)
    dmodc_parts = jnp.concatenate([gathered[:, 3 * nb:3 * nb + 2].reshape(n_dev, 2 * D), jnp.zeros((n_dev, D), F32)], axis=1)
    c_all = gathered[:, 3 * nb + 2:4 * nb + 2].reshape(n_dev * nb, D)
    nw_parts = gathered[:, 4 * nb + 2]
    misc_parts = gathered[:, 4 * nb + 3]
    n_rows = n_dev * nb + n_dev
    pad = (-n_rows) % 16
    c_rows = jnp.concatenate([c_all, jnp.broadcast_to(c_ctx.reshape(1, D), (n_dev, D)), jnp.zeros((pad, D), F32)], axis=0)
    dm_rows = jnp.concatenate([dmod_all, dmodc_parts, jnp.zeros((pad, 3 * D), F32)], axis=0)
    dm_loc_rows = lax.dynamic_slice_in_dim(dm_rows, dev * loc, loc, axis=1)
    r_pad = jnp.full((1, D), -1.0, F32).at[:, 2 * HD:2 * HD + 2 * RH].set(ret_log2_decay.reshape(1, 2 * RH))
    gb, gc, gnw, misc, gwa = _small_final(dmod_all, dmodc_parts, c_rows, dm_loc_rows, nw_parts, misc_parts,
                                          c_ctx.reshape(1, D), r_pad, w_ada16)
    return (gb, gc, gnw, misc[:, :HD], misc[:, HD:2 * HD], misc[:, 2 * HD:2 * HD + 2 * RH], gwa,
            misc[0, 2 * HD + 2 * RH])


def kernel(x, c, ctx, c_ctx, norm_w, w_ada, b_ada, w_in, ret_log2_decay, q_norm_w, k_norm_w, w_o_ret, w_o_att, w_out, loss_target, m_c_ctx, m_norm_w, m_w_ada, m_b_ada, m_w_in, m_ret_log2_decay, m_q_norm_w, m_k_norm_w, m_w_o_ret, m_w_o_att, m_w_out, v_c_ctx, v_norm_w, v_w_ada, v_b_ada, v_w_in, v_ret_log2_decay, v_q_norm_w, v_k_norm_w, v_w_o_ret, v_w_o_att, v_w_out):
    nb = x.shape[0]
    mx, my, mc = _mesh_pos()
    dev = 4 * mx + 2 * my + mc
    core = jnp.reshape(mc, (1,)).astype(jnp.int32)
    chip = jnp.reshape(2 * mx + my, (1,)).astype(jnp.int32)

    n_loc = 3 * D // N_DEV
    c8 = jnp.zeros((8, D), F32).at[:nb].set(c).at[nb].set(c_ctx)
    c_land = lax.dynamic_update_slice(lax.empty((N_DEV, 8, D), F32), c8[None], (dev, 0, 0))
    c_state, c_token = _exchange_start([c8[None]], [c_land], _bcast_routes(1), "gather_c_start")
    w_in_t = jnp.transpose(w_in[0])
    in_shard = w_in_t.astype(BF16)
    groups = lax.dynamic_update_slice(lax.empty((N_DEV,) + in_shard.shape, BF16), in_shard[None], (mc, 0, 0))
    groups = _pair_fill(groups, (0,), "gather_in_pair", after=(c_token,))
    _, (c_all,) = _exchange_wait(c_state, groups, "gather_c_wait")
    ada_shard = w_ada[0].astype(BF16)
    b_loc = lax.dynamic_slice(b_ada, (0, dev * n_loc), (1, n_loc))
    mod_cols = _mod_part(c_all.reshape(N_DEV * 8, D), ada_shard, b_loc)
    (mod_all,) = _all_gather([mod_cols], "gather_mod")
    mod = jnp.transpose(lax.dynamic_slice(mod_all, (0, dev * 8, 0), (N_DEV, 8, n_loc)), (1, 0, 2)).reshape(8, 3 * D)
    ada_land = lax.dynamic_update_slice(lax.empty((N_DEV,) + ada_shard.shape, BF16), ada_shard[None], (dev, 0, 0))

    (near_send, near_recv, near_bufs, near_routes), gin_token = _exchange_start(
        [in_shard[None]], [groups], _group_routes((1, 2)), "gather_in_start", after=(mod_all,))
    w_in_groups, wo_states, ada_states = [], [], []
    wo_shards = [w_[0].astype(BF16) for w_ in (w_o_ret, w_o_att, w_out)]
    wo_lands = [lax.dynamic_update_slice(lax.empty((N_DEV,) + s_.shape, BF16), s_[None], (dev, 0, 0)) for s_ in wo_shards]

    def _state(send, recv, src, groups, routes):
        return send, recv, [src, groups], routes

    def proj_in(h_all):
        src, groups = near_bufs
        px = _in_proj_group(h_all, groups, 0, 1, chip, None, (gin_token,), "in_proj_0")
        (src,), (groups,) = _exchange_wait(_state(near_send, near_recv, src, groups, near_routes), px,
                                           "gather_in_wait_near")
        groups = _pair_fill(groups, (1, 2), "gather_in_fill_near")
        (far_send, far_recv, (src, groups), far_routes), far_token = _exchange_start(
            [src], [groups], _group_routes((3,)), "gather_in_start_far")
        wo_state, wo_token = _exchange_start([s_[None] for s_ in wo_shards], wo_lands, _bcast_routes(3),
                                             "gather_wo_start", after=(far_token,))
        wo_states.append(wo_state)
        ada_state, ada_token = _exchange_start([ada_shard[None]], [ada_land], _bcast_routes(1), "gather_ada_start",
                                               after=(wo_token,))
        ada_states.append(ada_state)
        px = _in_proj_group(h_all, groups, 1, 2, chip, px, (ada_token,), "in_proj_near")
        (src,), (groups,) = _exchange_wait(_state(far_send, far_recv, src, groups, far_routes), px,
                                           "gather_in_wait_far")
        groups = _pair_fill(groups, (3,), "gather_in_fill_far")
        px = _in_proj_group(h_all, groups, 3, 1, chip, px, (), "in_proj_far")
        w_in_groups.append(groups)
        return px

    def get_w_o(after):
        _, (l_ret, l_att, l_out) = _exchange_wait(wo_states[0], after, "gather_wo_wait")
        return l_ret.reshape(RH * DV, D), l_att.reshape(D, D), l_out.reshape(D, D)

    def on_out_grads(grads):
        parts = [g_.reshape(N_DEV, g_.shape[0] // N_DEV, D) for g_ in grads]
        lands = [lax.empty((4,) + p_.shape[1:], BF16) for p_ in parts]
        pair_state, pair_token = _exchange_start(parts, lands, _pair_routes(len(parts)), "rs_out_pair_start")

        def send(after):
            parts_, got = _exchange_wait(pair_state, after, "rs_out_pair_wait")
            state, token = _reduce_scatter_send(parts_, got, core, "rs_out")
            return state, (token,)

        return send, (pair_token,)

    def on_in_grad(grad, dp_all):
        parts = [grad.reshape(N_DEV, IN_COLS // N_DEV, D)]
        lands = [lax.empty((4,) + p_.shape[1:], BF16) for p_ in parts]
        pair_state, pair_token = _exchange_start(parts, lands, _pair_routes(1), "rs_in_pair_start")
        dh = _d_h_groups(dp_all, w_in_groups[0], chip, 0, 1, None, (pair_token,))
        parts, got = _exchange_wait(pair_state, dh, "rs_in_pair_wait")
        state, token = _reduce_scatter_send(parts, got, core, "rs_in")
        n_tiles = dp_all.shape[0] // _D_H_ROWS
        return state, _d_h_groups(dp_all, w_in_groups[0], chip, 1, n_tiles - 1, dh, (token,))

    grad_x, out_state, in_state, payload = _local_step(
        x, c, ctx, norm_w, ret_log2_decay, q_norm_w, k_norm_w, loss_target,
        mod, proj_in, get_w_o, on_out_grads, on_in_grad, started=(gin_token,))

    pay_land = lax.dynamic_update_slice(lax.empty((N_DEV,) + payload.shape, F32), payload[None], (dev, 0, 0))
    pay_state, pay_token = _exchange_start([payload[None]], [pay_land], _bcast_routes(1), "gather_small_start")

    out_res = _reduce_scatter_finish(out_state, pay_token, chip,
                                     [(w_[0], m_[0], v_[0]) for w_, m_, v_ in ((w_o_ret, m_w_o_ret, v_w_o_ret),
                                                                                (w_o_att, m_w_o_att, v_w_o_att),
                                                                                (w_out, m_w_out, v_w_out))], "rs_out")
    (in_res,) = _reduce_scatter_finish(in_state, out_res[0][0], chip,
                                       [(w_in_t, jnp.transpose(m_w_in[0]), jnp.transpose(v_w_in[0]))], "rs_in")

    _, (gathered,) = _exchange_wait(pay_state, in_res[0], "gather_small_wait")
    _, (l_ada,) = _exchange_wait(ada_states[0], gathered, "gather_ada_wait")
    w_ada16 = jnp.transpose(l_ada, (1, 0, 2)).reshape(D, 3 * D)
    gb, gc, gnw, gq, gk, gr, gwa, loss = _finish_small(gathered, nb, c_ctx, ret_log2_decay, w_ada16, dev)
    big = {4: [jnp.transpose(r)[None] for r in in_res]}
    for i, res in zip((8, 9, 10), out_res):
        big[i] = [r[None] for r in res]
    small_g = {0: gc.reshape(c_ctx.shape), 1: gnw, 2: gwa[None], 3: gb, 5: gr.reshape(ret_log2_decay.shape), 6: gq, 7: gk}
    weights = [c_ctx, norm_w, w_ada, b_ada, w_in, ret_log2_decay, q_norm_w, k_norm_w, w_o_ret, w_o_att, w_out]
    ms = [m_c_ctx, m_norm_w, m_w_ada, m_b_ada, m_w_in, m_ret_log2_decay, m_q_norm_w, m_k_norm_w, m_w_o_ret, m_w_o_att, m_w_out]
    vs = [v_c_ctx, v_norm_w, v_w_ada, v_b_ada, v_w_in, v_ret_log2_decay, v_q_norm_w, v_k_norm_w, v_w_o_ret, v_w_o_att, v_w_out]
    def rows2(i):
        return [a.reshape(-1, weights[i].shape[-1]) for a in (weights[i], small_g[i], ms[i], vs[i])]

    small_ids = [i for i in small_g if i != 2]
    steps = dict(zip(small_ids, _adamw_small([rows2(i) for i in small_ids])))
    steps[2] = _adamw(*rows2(2), "adamw_w_ada")
    grads, deltas, new_ms, new_vs = [], [], [], []
    for i, w in enumerate(weights):
        res = big[i] if i in big else [small_g[i]] + [r.reshape(w.shape) for r in steps[i]]
        for lst, r in zip((grads, deltas, new_ms, new_vs), res):
            lst.append(r)
    return (loss, grad_x, *grads, *deltas, *new_ms, *new_vs)
```

```python
import numpy as np
import jax
import jax.numpy as jnp
from jax import lax
from jax.experimental import pallas as pl
from jax.experimental.pallas import tpu as pltpu

F32 = jnp.float32
BF16 = jnp.bfloat16

D = 1024
RH, DK, DV, CH = 4, 256, 512, 256
HQ, HKV, HD = 8, 2, 128
GRID_W = 64
ROPE_THETA = 10000.0
EPS = 1e-6
RK, RV, AK, AV, RQ, RG, AQ, AG, MR, MA = 0, 1024, 3072, 3328, 3584, 4608, 6656, 7680, 8704, 9728
IN_COLS = 10752
KV_COLS = 3584
N_DEV = 8
LR, B1, B2, ADAM_EPS, WD, STEP = 0.001, 0.9, 0.999, 1e-08, 0.01, 10
PAY_ROWS = 16
VMEM_LIMIT = 56 * 1024 * 1024
_D_H_ROWS = 1536
MESH_T = pl.DeviceIdType.MESH

NT = (((1,), (1,)), ((), ()))
TN = (((0,), (0,)), ((), ()))
SM_C = (HD ** -0.5) * float(np.log2(np.e))


def _params(sem):
    return pltpu.CompilerParams(dimension_semantics=sem, vmem_limit_bytes=VMEM_LIMIT)


def _pick(n, target, mult=8):
    best = None
    for t in range(mult, min(n, target) + 1, mult):
        if n % t == 0:
            best = t
    return best or n


def _dot(a, b, dn=None):
    if dn is None:
        return jnp.dot(a, b, preferred_element_type=F32)
    return lax.dot_general(a, b, dn, preferred_element_type=F32)


def _sig(v):
    return jax.nn.sigmoid(v)


def _silu(v):
    return v * _sig(v)


def _dsilu(v):
    s = _sig(v)
    return s * (1.0 + v * (1.0 - s))


def _sds(shape, dtype):
    return jax.ShapeDtypeStruct(shape, dtype)


def _matmul(a, b, *, ta=False, tb=False, tm, tn, tk, out_dtype, name, after=()):
    m = a.shape[1] if ta else a.shape[0]
    kdim = a.shape[0] if ta else a.shape[1]
    n = b.shape[0] if tb else b.shape[1]
    tm, tn, tk = _pick(m, tm, 128), _pick(n, tn, 128), _pick(kdim, tk, 128)
    nk = kdim // tk
    dn = (((0 if ta else 1,), (1 if tb else 0,)), ((), ()))

    def body(a_ref, b_ref, *rest):
        o_ref, acc_ref = rest[-2:]
        k = pl.program_id(2)
        part = _dot(a_ref[...].astype(BF16), b_ref[...].astype(BF16), dn)
        if nk == 1:
            o_ref[...] = part.astype(o_ref.dtype)
        else:
            @pl.when(k == 0)
            def _():
                acc_ref[...] = part

            @pl.when(k > 0)
            def _():
                acc_ref[...] += part

            @pl.when(k == nk - 1)
            def _():
                o_ref[...] = acc_ref[...].astype(o_ref.dtype)

    a_spec = pl.BlockSpec((tk, tm), lambda i, j, k: (k, i)) if ta else pl.BlockSpec((tm, tk), lambda i, j, k: (i, k))
    b_spec = pl.BlockSpec((tn, tk), lambda i, j, k: (j, k)) if tb else pl.BlockSpec((tk, tn), lambda i, j, k: (k, j))
    return pl.pallas_call(
        body, name=name, grid=(m // tm, n // tn, nk),
        in_specs=[a_spec, b_spec] + [pl.BlockSpec(memory_space=pl.ANY)] * len(after),
        out_specs=pl.BlockSpec((tm, tn), lambda i, j, k: (i, j)), out_shape=_sds((m, n), out_dtype),
        scratch_shapes=[pltpu.VMEM((tm, tn) if nk > 1 else (8, 128), F32)],
        compiler_params=_params(("parallel", "parallel", "arbitrary")),
    )(a, b, *after)


def _log_gamma(r):
    rp = jnp.full((8, 128), -1.0, F32).at[:2, :RH].set(r.reshape(2, RH))

    def body(r_ref, o_ref):
        o_ref[...] = jnp.log1p(-jnp.exp2(r_ref[...]))

    out = pl.pallas_call(body, name="log_gamma", out_shape=_sds((8, 128), F32))(rp)
    return out[:2, :RH]


def _mod_part(c_rows, w_ada_loc16, b_loc):
    def body(c_ref, w_ref, b_ref, o_ref):
        o_ref[...] = _dot(_silu(c_ref[...]).astype(BF16), w_ref[...]) + b_ref[...]

    return pl.pallas_call(
        body, name="mod_part", out_shape=_sds((c_rows.shape[0], w_ada_loc16.shape[1]), F32),
    )(c_rows, w_ada_loc16, b_loc)


def _norm_fwd(x2, mod3, norm_w, rows_all, row_off, rows_per_group, group0, h_prev, tm, name, after=()):
    rows = x2.shape[0]
    rb0 = row_off // tm
    bpg = rows_per_group // tm

    def body(*refs):
        x_ref, sh_ref, sc_ref, nw_ref, o_ref = refs[-5:]
        xv = x_ref[...]
        r = lax.rsqrt(jnp.mean(xv * xv, axis=-1, keepdims=True) + EPS)
        o_ref[...] = ((xv * r) * nw_ref[...] * (1.0 + sc_ref[...]) + sh_ref[...]).astype(BF16)

    in_specs = [pl.BlockSpec((tm, D), lambda i: (i, 0)),
                pl.BlockSpec((None, 1, D), lambda i: (group0 + i // bpg, 0, 0)),
                pl.BlockSpec((None, 1, D), lambda i: (group0 + i // bpg, 0, 1)),
                pl.BlockSpec((1, D), lambda i: (0, 0))]
    in_specs = [pl.BlockSpec(memory_space=pl.ANY)] * len(after) + in_specs
    args = list(after) + [x2, mod3, mod3, norm_w]
    alias = {}
    if h_prev is not None:
        in_specs.insert(0, pl.BlockSpec(memory_space=pl.ANY))
        args.insert(0, h_prev)
        alias = {0: 0}
    return pl.pallas_call(
        body, name=name, grid=(rows // tm,), in_specs=in_specs,
        out_specs=pl.BlockSpec((tm, D), lambda i: (rb0 + i, 0)), out_shape=_sds((rows_all, D), BF16),
        input_output_aliases=alias, compiler_params=_params(("parallel",)),
    )(*args)


def _decays(lg, fwd):
    ii = lax.broadcasted_iota(jnp.int32, (CH, CH), 0)
    jj = lax.broadcasted_iota(jnp.int32, (CH, CH), 1)
    ri = lax.broadcasted_iota(jnp.int32, (CH, 1), 0).astype(F32)
    rel = (ii - jj) if fwd else (jj - ii)
    relf = jnp.maximum(rel, 0).astype(F32)
    mask = jnp.where(rel >= 0, jnp.exp(lg * relf), 0.0)
    qe = (ri + 1.0) if fwd else (CH - ri)
    ke = (CH - 1.0 - ri) if fwd else ri
    return mask, relf, jnp.exp(lg * qe), qe, jnp.exp(lg * ke), ke


def _wide_specs(rowf):
    return [pl.BlockSpec((CH, 2 * DK), lambda b, c: (rowf(b, c), RQ // (2 * DK))),
            pl.BlockSpec((CH, 2 * DK), lambda b, c: (rowf(b, c), RQ // (2 * DK) + 1)),
            pl.BlockSpec((CH, RH * DK), lambda b, c: (rowf(b, c), RK // (RH * DK))),
            pl.BlockSpec((CH, 2 * DV), lambda b, c: (rowf(b, c), RV // (2 * DV))),
            pl.BlockSpec((CH, 2 * DV), lambda b, c: (rowf(b, c), RV // (2 * DV) + 1))]


def _head_qkv(refs, h):
    q0, q1, k, v0, v1 = refs
    lo = h % 2
    q = (q0, q1)[h // 2][:, lo * DK:(lo + 1) * DK].astype(F32)
    kk = k[:, h * DK:(h + 1) * DK].astype(F32) * (DK ** -0.5)
    v16 = (v0, v1)[h // 2][:, lo * DV:(lo + 1) * DV].astype(BF16)
    return q, kk, v16


def _ctx_specs(t_rows, cx):
    rb = t_rows // cx
    return [pl.BlockSpec((cx, RH * DK), lambda b, c: (rb + b, RK // (RH * DK))),
            pl.BlockSpec((cx, 2 * DV), lambda b, c: (rb + b, RV // (2 * DV))),
            pl.BlockSpec((cx, 2 * DV), lambda b, c: (rb + b, RV // (2 * DV) + 1))]


def _ctx_kv(refs, h):
    k, v0, v1 = refs
    kk = k[:, h * DK:(h + 1) * DK].astype(F32) * (DK ** -0.5)
    lo = h % 2
    return kk, (v0, v1)[h // 2][:, lo * DV:(lo + 1) * DV].astype(BF16)


def _ret_fwd(px, lg, nb, nc, cx):
    t_rows = nb * nc * CH

    def body(lg_ref, *refs):
        ins = (refs[0:5], refs[5:10])
        ctx_refs = refs[10:13]
        of_ref, ob_ref, hf_ref, hb_ref, sf, sb = refs[13:]
        c = pl.program_id(1)

        @pl.when(c == 0)
        def _():
            pos = lax.broadcasted_iota(jnp.int32, (cx, 1), 0).astype(F32)
            for h in range(RH):
                k, v16 = _ctx_kv(ctx_refs, h)
                sf[h] = _dot((k * jnp.exp(lg_ref[0, h] * (cx - 1.0 - pos))).astype(BF16), v16, TN)
                sb[h] = _dot((k * jnp.exp(lg_ref[1, h] * pos)).astype(BF16), v16, TN)

        for d, (o_ref, h_ref, s) in enumerate(((of_ref, hf_ref, sf), (ob_ref, hb_ref, sb))):
            for h in range(RH):
                lg_d = lg_ref[d, h]
                mask, _, qd, _, kd, _ = _decays(lg_d, d == 0)
                q, k, v16 = _head_qkv(ins[d], h)
                a = _dot(q.astype(BF16), k.astype(BF16), NT)
                st = s[h]
                st16 = st.astype(BF16)
                h_ref[h] = st16
                o = _dot((a * mask).astype(BF16), v16) + _dot((q * qd).astype(BF16), st16)
                o_ref[:, h * DV:(h + 1) * DV] = o.astype(BF16)
                s[h] = st * jnp.exp(lg_d * CH) + _dot((k * kd).astype(BF16), v16, TN)

    def fw(b, c):
        return b * nc + c

    def bw(b, c):
        return b * nc + nc - 1 - c

    in_specs = [pl.BlockSpec(memory_space=pltpu.SMEM)] + _wide_specs(fw) + _wide_specs(bw) + _ctx_specs(t_rows, cx)
    out_specs = [pl.BlockSpec((CH, RH * DV), lambda b, c: (fw(b, c), 0)),
                 pl.BlockSpec((CH, RH * DV), lambda b, c: (bw(b, c), 0)),
                 pl.BlockSpec((None, None, RH, DK, DV), lambda b, c: (b, c, 0, 0, 0)),
                 pl.BlockSpec((None, None, RH, DK, DV), lambda b, c: (b, nc - 1 - c, 0, 0, 0))]
    return pl.pallas_call(
        body, name="ret_fwd", grid=(nb, nc), in_specs=in_specs, out_specs=out_specs,
        out_shape=[_sds((t_rows, RH * DV), BF16)] * 2 + [_sds((nb, nc, RH, DK, DV), BF16)] * 2,
        scratch_shapes=[pltpu.VMEM((RH, DK, DV), F32), pltpu.VMEM((RH, DK, DV), F32)],
        compiler_params=_params(("parallel", "arbitrary")),
    )(lg, *([px] * 13))


def _rope_tables(seq):
    rows = seq // GRID_W
    row = np.repeat(np.arange(rows, dtype=np.float32), GRID_W)
    col = np.tile(np.arange(GRID_W, dtype=np.float32), rows)
    half = HD // 2
    freqs = (ROPE_THETA ** (-np.arange(0, half, 2, dtype=np.float32) / half)).astype(np.float32)
    ang = np.concatenate([row[:, None] * freqs, col[:, None] * freqs], axis=-1).astype(np.float32)
    cos = np.repeat(np.cos(ang), 2, axis=-1).astype(np.float32)
    sin = np.repeat(np.sin(ang), 2, axis=-1).astype(np.float32)
    sign = np.tile(np.array([-1.0, 1.0], np.float32), HD // 2)
    return jnp.asarray(cos), jnp.asarray(sin * sign)


def _swap_pairs(v):
    lane = lax.broadcasted_iota(jnp.int32, v.shape, 1)
    return jnp.where((lane & 1) == 0, pltpu.roll(v, HD - 1, 1), pltpu.roll(v, 1, 1))


def _qk_prep(px, nw, cos, sin, rows, row_off, col_off, heads, hb, seq, tm, name):
    rope = cos is not None
    rb0 = row_off // tm
    pb = seq // tm if rope else 1
    bw = hb * HD

    def body(*refs):
        if rope:
            x_ref, w_ref, c_ref, s_ref, o_ref = refs
        else:
            x_ref, w_ref, o_ref = refs
        for h in range(hb):
            sl = slice(h * HD, (h + 1) * HD)
            xv = x_ref[:, sl].astype(F32)
            r = lax.rsqrt(jnp.mean(xv * xv, axis=-1, keepdims=True) + EPS)
            t = (xv * r) * w_ref[...]
            if rope:
                t = t * c_ref[...] + _swap_pairs(t) * s_ref[...]
            o_ref[:, sl] = t.astype(BF16)

    in_specs = [pl.BlockSpec((tm, bw), lambda i, j: (rb0 + i, col_off // bw + j)),
                pl.BlockSpec((1, HD), lambda i, j: (0, 0))]
    args = [px, nw]
    if rope:
        in_specs += [pl.BlockSpec((tm, HD), lambda i, j: (i % pb, 0))] * 2
        args += [cos, sin]
    return pl.pallas_call(
        body, name=name, grid=(rows // tm, heads // hb), in_specs=in_specs,
        out_specs=pl.BlockSpec((tm, bw), lambda i, j: (i, j)), out_shape=_sds((rows, heads * HD), BF16),
        compiler_params=_params(("parallel", "parallel")),
    )(*args)


def _att_fwd(q16, kx16, kc16, px, nb, seq, cx, tq):
    t_rows = nb * seq
    nq = seq // tq
    rep = HQ // HKV
    gw = rep * HD

    def body(q_ref, kx_ref, kc_ref, vx_ref, vc_ref, g_ref, o_ref, y_ref, l_ref):
        kx = kx_ref[...]
        kc = kc_ref[...]
        vx = vx_ref[...].astype(BF16)
        vc = vc_ref[...].astype(BF16)
        l_ref[...] = jnp.zeros_like(l_ref)
        for r in range(rep):
            sl = slice(r * HD, (r + 1) * HD)
            q = q_ref[:, sl]
            s1 = _dot(q, kx, NT)
            s2 = _dot(q, kc, NT)
            m = jnp.maximum(jnp.max(s1, axis=-1, keepdims=True), jnp.max(s2, axis=-1, keepdims=True))
            e1 = jnp.exp2((s1 - m) * SM_C)
            e2 = jnp.exp2((s2 - m) * SM_C)
            tot = jnp.sum(e1, axis=-1, keepdims=True) + jnp.sum(e2, axis=-1, keepdims=True)
            o = (_dot(e1.astype(BF16), vx) + _dot(e2.astype(BF16), vc)) * (1.0 / tot)
            o_ref[:, sl] = o
            y_ref[:, sl] = (o * _silu(g_ref[:, sl].astype(F32))).astype(BF16)
            l_ref[:, r:r + 1] = m * SM_C + jnp.log(tot) * float(np.log2(np.e))

    qblk = pl.BlockSpec((tq, gw), lambda b, g, i: (b * nq + i, g))
    return pl.pallas_call(
        body, name="att_fwd", grid=(nb, HKV, nq),
        in_specs=[qblk,
                  pl.BlockSpec((seq, HD), lambda b, g, i: (b, g)),
                  pl.BlockSpec((cx, HD), lambda b, g, i: (b, g)),
                  pl.BlockSpec((seq, HD), lambda b, g, i: (b, AV // HD + g)),
                  pl.BlockSpec((cx, HD), lambda b, g, i: (t_rows // cx + b, AV // HD + g)),
                  pl.BlockSpec((tq, gw), lambda b, g, i: (b * nq + i, AG // gw + g))],
        out_specs=[qblk, qblk, pl.BlockSpec((tq, 128), lambda b, g, i: (b * nq + i, g))],
        out_shape=[_sds((t_rows, D), F32), _sds((t_rows, D), BF16), _sds((t_rows, HKV * 128), F32)],
        compiler_params=_params(("parallel", "parallel", "parallel")),
    )(q16, kx16, kc16, px, px, px)


def _gate_specs(tm, col0):
    hw = D // 2
    return [pl.BlockSpec((tm, hw), lambda i: (i, col0 // hw)), pl.BlockSpec((tm, hw), lambda i: (i, col0 // hw + 1))]


def _merge_out(o_f, o_b, yatt16, px, w_o_ret16, w_o_att16, w_out16, x2, tgt, mod3, nb, seq, tm):
    t_rows = nb * seq
    bpb = seq // tm
    hw = D // 2

    def body(of_ref, ob_ref, g0, g1, g2, g3, wr_ref, ya_ref, wa_ref, mr0, mr1, ma0, ma1, wo_ref, x_ref, t_ref, gt_ref,
             yr_ref, ar_ref, aa_ref, dxn_ref, dout_ref, dg_ref, loss_ref, gw_ref, y_ref, acc_ref):
        i = pl.program_id(1)
        first = jnp.logical_and(pl.program_id(0) == 0, i == 0)
        for h, g_ref in enumerate((g0, g1, g2, g3)):
            sl = slice(h * DV, (h + 1) * DV)
            o = of_ref[:, sl].astype(F32) + ob_ref[:, sl].astype(F32)
            r = lax.rsqrt(jnp.mean(o * o, axis=-1, keepdims=True) + EPS)
            yr_ref[:, sl] = ((o * r) * _silu(g_ref[...].astype(F32))).astype(BF16)
        ar = _dot(yr_ref[...], wr_ref[...])
        aa = _dot(ya_ref[...], wa_ref[...])
        ar_ref[...] = ar.astype(BF16)
        aa_ref[...] = aa.astype(BF16)
        for j, (mr_ref, ma_ref) in enumerate(((mr0, ma0), (mr1, ma1))):
            sl = slice(j * hw, (j + 1) * hw)
            y_ref[:, sl] = (_sig(mr_ref[...].astype(F32)) * ar[:, sl]
                            + _sig(ma_ref[...].astype(F32)) * aa[:, sl]).astype(BF16)
        out = _dot(y_ref[...], wo_ref[...])
        gate = gt_ref[...]
        diff = x_ref[...] + gate * out - t_ref[...]
        dxn = diff * (1.0 / D)
        dxn_ref[...] = dxn
        dout_ref[...] = (gate * dxn).astype(BF16)

        @pl.when(first)
        def _():
            acc_ref[...] = jnp.zeros_like(acc_ref)

        for j in range(2):
            sl = slice(j * hw, (j + 1) * hw)
            acc_ref[sl, :] += _dot(y_ref[:, sl], dout_ref[...], TN)

        @pl.when(jnp.logical_and(pl.program_id(0) == nb - 1, i == bpb - 1))
        def _():
            gw_ref[...] = acc_ref[...].astype(BF16)

        dg = jnp.sum(dxn * out, axis=0, keepdims=True)
        ls = jnp.broadcast_to(jnp.sum(diff * diff) * (0.5 / D), (1, 128))

        @pl.when(i == 0)
        def _():
            dg_ref[...] = dg
            loss_ref[...] = ls

        @pl.when(i > 0)
        def _():
            dg_ref[...] += dg
            loss_ref[...] += ls

    def cols(width, col0):
        return pl.BlockSpec((tm, width), lambda b, i: (b * bpb + i, col0 // width))

    def whole(rows):
        return pl.BlockSpec((rows, D), lambda b, i: (0, 0))

    row, wide = cols(D, 0), cols(RH * DV, 0)
    gates = [cols(DV, RG + h * DV) for h in range(RH)]
    merge_gates = [cols(hw, MR), cols(hw, MR + hw), cols(hw, MA), cols(hw, MA + hw)]
    return pl.pallas_call(
        body, name="merge_out", grid=(nb, bpb),
        in_specs=[wide, wide] + gates + [whole(RH * DV), row, whole(D)] + merge_gates
        + [whole(D), row, row, pl.BlockSpec((None, 1, D), lambda b, i: (b, 0, 2))],
        out_specs=[wide, row, row, row, row, pl.BlockSpec((None, 1, D), lambda b, i: (b, 0, 0)),
                   pl.BlockSpec((None, 1, 128), lambda b, i: (b, 0, 0)), whole(D)],
        out_shape=[_sds((t_rows, RH * DV), BF16)] + [_sds((t_rows, D), BF16)] * 2
        + [_sds((t_rows, D), F32), _sds((t_rows, D), BF16), _sds((nb, 1, D), F32), _sds((nb, 1, 128), F32),
           _sds((D, D), BF16)],
        scratch_shapes=[pltpu.VMEM((tm, D), BF16), pltpu.VMEM((D, D), F32)],
        compiler_params=_params(("arbitrary", "arbitrary")),
    )(o_f, o_b, *([px] * RH), w_o_ret16, yatt16, w_o_att16, px, px, px, px, w_out16, x2, tgt, mod3)


def _bwd_branches(dout16, w_out16, w_o_ret16, w_o_att16, px, a_ret, a_att, o_f, o_b, o_att, yatt16, rows_all, tm):
    t_rows = dout16.shape[0]
    hw = D // 2

    def body(do_ref, wo_ref, wr_ref, wa_ref, mr0, mr1, ma0, ma1, ar_ref, aa_ref, rg0, rg1, rg2, rg3, of_ref, ob_ref,
             ag0, ag1, oa_ref, ya_ref, dar_ref, dor_ref, dao_ref, dl_ref, dp_ref, gwa_ref, daa_ref, acca_ref):
        dy_all = _dot(do_ref[...], wo_ref[...], NT)
        for j, (mr_ref, ma_ref) in enumerate(((mr0, ma0), (mr1, ma1))):
            sl = slice(j * hw, (j + 1) * hw)
            dy = dy_all[:, sl]
            sr = _sig(mr_ref[...].astype(F32))
            sa = _sig(ma_ref[...].astype(F32))
            dar_ref[:, sl] = (dy * sr).astype(BF16)
            daa_ref[:, sl] = (dy * sa).astype(BF16)
            dp_ref[:, MR - RG + j * hw:MR - RG + (j + 1) * hw] = (
                dy * ar_ref[:, sl].astype(F32) * sr * (1.0 - sr)).astype(BF16)
            dp_ref[:, MA - RG + j * hw:MA - RG + (j + 1) * hw] = (
                dy * aa_ref[:, sl].astype(F32) * sa * (1.0 - sa)).astype(BF16)
        da_ret = dar_ref[...]

        @pl.when(pl.program_id(0) == 0)
        def _():
            acca_ref[...] = jnp.zeros_like(acca_ref)

        for j in range(2):
            sl = slice(j * hw, (j + 1) * hw)
            acca_ref[sl, :] += _dot(ya_ref[:, sl], daa_ref[...], TN)

        for h, g_ref in enumerate((rg0, rg1, rg2, rg3)):
            sl = slice(h * DV, (h + 1) * DV)
            dy = _dot(da_ret, wr_ref[sl, :], NT)
            g = g_ref[...].astype(F32)
            o = of_ref[:, sl].astype(F32) + ob_ref[:, sl].astype(F32)
            r = lax.rsqrt(jnp.mean(o * o, axis=-1, keepdims=True) + EPS)
            on = o * r
            sg = _sig(g)
            don = dy * (g * sg)
            dp_ref[:, sl] = (dy * on * (sg * (1.0 + g * (1.0 - sg)))).astype(BF16)
            dor_ref[:, sl] = (r * (don - on * jnp.mean(on * don, axis=-1, keepdims=True))).astype(BF16)
        dy_all = _dot(daa_ref[...], wa_ref[...], NT)
        dl_ref[...] = jnp.zeros_like(dl_ref)
        for j, g_ref in enumerate((ag0, ag1)):
            sl = slice(j * hw, (j + 1) * hw)
            dy = dy_all[:, sl]
            g = g_ref[...].astype(F32)
            sg = _sig(g)
            dao = dy * (g * sg)
            dao_ref[:, sl] = dao.astype(BF16)
            prod = dao * oa_ref[:, sl]
            for r in range(hw // HD):
                dl_ref[:, j * 128 + r:j * 128 + r + 1] = jnp.sum(prod[:, r * HD:(r + 1) * HD], axis=-1, keepdims=True)
            dp_ref[:, AG - RG + j * hw:AG - RG + (j + 1) * hw] = (
                dy * oa_ref[:, sl] * (sg * (1.0 + g * (1.0 - sg)))).astype(BF16)

        @pl.when(pl.program_id(0) == t_rows // tm - 1)
        def _():
            gwa_ref[...] = acca_ref[...].astype(BF16)

    def gate(h):
        return pl.BlockSpec((tm, DV), lambda i: (i, RG // DV + h))

    def whole(rows):
        return pl.BlockSpec((rows, D), lambda i: (0, 0), pipeline_mode=pl.Buffered(1))

    row = pl.BlockSpec((tm, D), lambda i: (i, 0))
    wide = pl.BlockSpec((tm, RH * DV), lambda i: (i, 0))
    return pl.pallas_call(
        body, name="bwd_branches", grid=(t_rows // tm,),
        in_specs=[row, whole(D), whole(RH * DV), whole(D)] + _gate_specs(tm, MR) + _gate_specs(tm, MA) + [row, row]
        + [gate(h) for h in range(RH)] + [wide, wide] + _gate_specs(tm, AG) + [row, row],
        out_specs=[row, wide, row, pl.BlockSpec((tm, HKV * 128), lambda i: (i, 0)),
                   pl.BlockSpec((pl.Element(tm), pl.Element(IN_COLS - RG)), lambda i: (i * tm, RG)), whole(D)],
        out_shape=[_sds((t_rows, D), BF16), _sds((t_rows, RH * DV), BF16), _sds((t_rows, D), BF16),
                   _sds((t_rows, HKV * 128), F32), _sds((rows_all, IN_COLS), BF16), _sds((D, D), BF16)],
        scratch_shapes=[pltpu.VMEM((tm, D), BF16), pltpu.VMEM((D, D), F32)],
        compiler_params=_params(("arbitrary",)),
    )(dout16, w_out16, w_o_ret16, w_o_att16, px, px, px, px, a_ret, a_att, *([px] * RH), o_f, o_b, px, px, o_att,
      yatt16)


def _att_bwd(q16, kx16, kc16, px, dao16, delta, lse, q_norm_w, k_norm_w, cos, sin, dp_all, nb, seq, cx, tq, after=()):
    t_rows = nb * seq
    nq = seq // tq
    rep = HQ // HKV
    gw = rep * HD
    scale = HD ** -0.5

    def body(q_ref, kx_ref, kc_ref, vx_ref, vc_ref, dao_ref, dl_ref, l_ref, xq_ref, w_ref, c_ref, s_ref,
             xk_ref, xkc_ref, wk_ref, ck_ref, sk_ref, *rest):
        daq_ref, gq_ref, gk_ref, dkx_ref, dvx_ref, dkc_ref, dvc_ref = rest[1 + len(after):8 + len(after)]
        accs = rest[8 + len(after):]
        i = pl.program_id(2)
        head0 = jnp.logical_and(pl.program_id(0) == 0, pl.program_id(1) == 0)
        first = jnp.logical_and(head0, i == 0)
        gq = jnp.zeros((1, HD), F32)
        kx = kx_ref[...]
        kc = kc_ref[...]
        vx = vx_ref[...].astype(BF16)
        vc = vc_ref[...].astype(BF16)
        @pl.when(i == 0)
        def _():
            for acc in accs:
                acc[...] = jnp.zeros_like(acc)

        dkx, dvx, dkc, dvc = [acc[...] for acc in accs]
        for r in range(rep):
            sl = slice(r * HD, (r + 1) * HD)
            q = q_ref[:, sl]
            lr = l_ref[:, r:r + 1]
            p1 = jnp.exp2(_dot(q, kx, NT) * SM_C - lr)
            p2 = jnp.exp2(_dot(q, kc, NT) * SM_C - lr)
            da16 = dao_ref[:, sl]
            delta = dl_ref[:, r:r + 1]
            ds1 = (p1 * (_dot(da16, vx, NT) - delta)).astype(BF16)
            ds2 = (p2 * (_dot(da16, vc, NT) - delta)).astype(BF16)
            dq = (_dot(ds1, kx) + _dot(ds2, kc)) * scale
            dkx += _dot(q, ds1, TN)
            dkc += _dot(q, ds2, TN)
            dvx += _dot(da16, p1.astype(BF16), TN)
            dvc += _dot(da16, p2.astype(BF16), TN)
            dt = dq * c_ref[...] + _swap_pairs(dq * s_ref[...])
            xv = xq_ref[:, sl].astype(F32)
            rn = lax.rsqrt(jnp.mean(xv * xv, axis=-1, keepdims=True) + EPS)
            xh = xv * rn
            dxh = dt * w_ref[...]
            daq_ref[:, sl] = (rn * (dxh - xh * jnp.mean(dxh * xh, axis=-1, keepdims=True))).astype(BF16)
            gq += jnp.sum(dt * xh, axis=0, keepdims=True)
        for acc, val in zip(accs, (dkx, dvx, dkc, dvc)):
            acc[...] = val

        @pl.when(first)
        def _():
            gq_ref[...] = gq

        @pl.when(jnp.logical_not(first))
        def _():
            gq_ref[...] += gq

        def k_back(dk, x_ref, dk_ref):
            xv = x_ref[...].astype(F32)
            rn = lax.rsqrt(jnp.mean(xv * xv, axis=-1, keepdims=True) + EPS)
            xh = xv * rn
            dxh = dk * wk_ref[...]
            dk_ref[...] = (rn * (dxh - xh * jnp.mean(dxh * xh, axis=-1, keepdims=True))).astype(BF16)
            return jnp.sum(dk * xh, axis=0, keepdims=True)

        @pl.when(i == nq - 1)
        def _():
            dvx_ref[...] = dvx.T.astype(BF16)
            dvc_ref[...] = dvc.T.astype(BF16)
            dk = dkx.T * scale
            gk = (k_back(dk * ck_ref[...] + _swap_pairs(dk * sk_ref[...]), xk_ref, dkx_ref)
                  + k_back(dkc.T * scale, xkc_ref, dkc_ref))

            @pl.when(head0)
            def _():
                gk_ref[...] = gk

            @pl.when(jnp.logical_not(head0))
            def _():
                gk_ref[...] += gk

    qblk = pl.BlockSpec((tq, gw), lambda b, g, i: (b * nq + i, g))
    kxb = pl.BlockSpec((seq, HD), lambda b, g, i: (b, g))
    kcb = pl.BlockSpec((cx, HD), lambda b, g, i: (b, g))
    table = pl.BlockSpec((tq, HD), lambda b, g, i: (i, 0))
    tables = pl.BlockSpec((seq, HD), lambda b, g, i: (0, 0))
    one = pl.BlockSpec((1, HD), lambda b, g, i: (0, 0))
    lane = pl.BlockSpec((tq, 128), lambda b, g, i: (b * nq + i, g))
    return pl.pallas_call(
        body, name="att_bwd", grid=(nb, HKV, nq),
        in_specs=[qblk,
                  pl.BlockSpec((seq, HD), lambda b, g, i: (b, g)),
                  pl.BlockSpec((cx, HD), lambda b, g, i: (b, g)),
                  pl.BlockSpec((seq, HD), lambda b, g, i: (b, AV // HD + g)),
                  pl.BlockSpec((cx, HD), lambda b, g, i: (t_rows // cx + b, AV // HD + g)),
                  qblk, lane, lane,
                  pl.BlockSpec((tq, gw), lambda b, g, i: (b * nq + i, AQ // gw + g)), one, table, table,
                  pl.BlockSpec((seq, HD), lambda b, g, i: (b, AK // HD + g)),
                  pl.BlockSpec((cx, HD), lambda b, g, i: (t_rows // cx + b, AK // HD + g)), one, tables, tables]
        + [pl.BlockSpec(memory_space=pl.ANY)] * (1 + len(after)),
        out_specs=[pl.BlockSpec((tq, gw), lambda b, g, i: (b * nq + i, AQ // gw + g)), one, one, kxb, kxb, kcb, kcb],
        out_shape=[_sds(dp_all.shape, BF16), _sds((1, HD), F32), _sds((1, HD), F32), _sds((t_rows, HKV * HD), BF16),
                   _sds((t_rows, HKV * HD), BF16), _sds((nb * cx, HKV * HD), BF16), _sds((nb * cx, HKV * HD), BF16)],
        scratch_shapes=[pltpu.VMEM((HD, seq), F32), pltpu.VMEM((HD, seq), F32), pltpu.VMEM((HD, cx), F32),
                        pltpu.VMEM((HD, cx), F32)],
        input_output_aliases={17: 0},
        compiler_params=_params(("arbitrary", "arbitrary", "arbitrary")),
    )(q16, kx16, kc16, px, px, dao16, delta, lse, px, q_norm_w, cos, sin, px, px, k_norm_w, cos, sin, dp_all, *after)


def _ret_bwd(px, lg, do16, hist_f, hist_b, nb, nc, cx):
    t_rows = nb * nc * CH

    def body(lg_ref, *refs):
        ins = (refs[0:5], refs[7:12])
        do_refs = (refs[5], refs[12])
        h_refs = (refs[6], refs[13])
        ctx_refs = refs[14:17]
        outs = (refs[17:20], refs[20:23])
        dck_ref, dcv_ref, dlg_ref = refs[23:26]
        dss = (refs[26], refs[27])
        c = pl.program_id(1)

        @pl.when(c == 0)
        def _():
            dss[0][...] = jnp.zeros_like(dss[0])
            dss[1][...] = jnp.zeros_like(dss[1])
            dlg_ref[...] = jnp.zeros_like(dlg_ref)

        for d in range(2):
            dq_ref, dk_ref, dv_ref = outs[d]
            for h in range(RH):
                lg_d = lg_ref[d, h]
                mask, relf, qd, qe, kd, ke = _decays(lg_d, d == 0)
                g_ch = jnp.exp(lg_d * CH)
                q, k, v16 = _head_qkv(ins[d], h)
                q16 = q.astype(BF16)
                k16 = k.astype(BF16)
                do16v = do_refs[d][:, h * DV:(h + 1) * DV]
                st16 = h_refs[d][h]
                dst = dss[d][h]
                dst16 = dst.astype(BF16)
                a = _dot(q16, k16, NT) * mask
                dp = _dot(do16v, v16, NT)
                da16 = (dp * mask).astype(BF16)
                dq_cross = _dot(do16v, st16, NT) * qd
                dq_ref[:, h * DK:(h + 1) * DK] = (_dot(da16, k16) + dq_cross).astype(BF16)
                dk_state = _dot(v16, dst16, NT) * kd
                dk_ref[:, h * DK:(h + 1) * DK] = ((_dot(da16, q16, TN) + dk_state) * (DK ** -0.5)).astype(BF16)
                dv = _dot(a.astype(BF16), do16v, TN) + _dot((k * kd).astype(BF16), dst16)
                dv_ref[:, h * DV:(h + 1) * DV] = dv.astype(BF16)
                dlg = (jnp.sum(relf * a * dp)
                       + jnp.sum(qe * jnp.sum(q * dq_cross, axis=-1, keepdims=True))
                       + jnp.sum(ke * jnp.sum(k * dk_state, axis=-1, keepdims=True))
                       + CH * g_ch * jnp.sum(dst * st16.astype(F32)))
                row = d * RH + h
                dlg_ref[row:row + 1, :] += jnp.broadcast_to(dlg, (1, 128))
                dss[d][h] = g_ch * dst + _dot((q * qd).astype(BF16), do16v, TN)

        @pl.when(c == nc - 1)
        def _():
            pos = lax.broadcasted_iota(jnp.int32, (cx, 1), 0).astype(F32)
            for h in range(RH):
                k, v16 = _ctx_kv(ctx_refs, h)
                dk = jnp.zeros((cx, DK), F32)
                dv = jnp.zeros((cx, DV), F32)
                for d, e in enumerate((cx - 1.0 - pos, pos)):
                    w = jnp.exp(lg_ref[d, h] * e)
                    ds16 = dss[d][h].astype(BF16)
                    t = _dot(v16, ds16, NT)
                    dk += t * w
                    dv += _dot((k * w).astype(BF16), ds16)
                    dlg = jnp.sum(e * w * jnp.sum(k * t, axis=-1, keepdims=True))
                    row = d * RH + h
                    dlg_ref[row:row + 1, :] += jnp.broadcast_to(dlg, (1, 128))
                dck_ref[:, h * DK:(h + 1) * DK] = (dk * (DK ** -0.5)).astype(BF16)
                dcv_ref[:, h * DV:(h + 1) * DV] = dv.astype(BF16)

    def fw(b, c):
        return b * nc + nc - 1 - c

    def bw(b, c):
        return b * nc + c

    def rows(rowf, width):
        return pl.BlockSpec((CH, width), lambda b, c: (rowf(b, c), 0))

    def hist(rowf):
        return pl.BlockSpec((None, None, RH, DK, DV), lambda b, c: (b, rowf(0, c), 0, 0, 0))

    in_specs = [pl.BlockSpec(memory_space=pltpu.SMEM)]
    out_specs = []
    for rowf in (fw, bw):
        in_specs += _wide_specs(rowf) + [rows(rowf, RH * DV), hist(rowf)]
        out_specs += [rows(rowf, RH * DK), rows(rowf, RH * DK), rows(rowf, RH * DV)]
    in_specs += _ctx_specs(t_rows, cx)
    out_specs += [pl.BlockSpec((cx, RH * DK), lambda b, c: (b, 0)), pl.BlockSpec((cx, RH * DV), lambda b, c: (b, 0)),
                  pl.BlockSpec((None, 8, 128), lambda b, c: (b, 0, 0))]
    qk = _sds((t_rows, RH * DK), BF16)
    vv = _sds((t_rows, RH * DV), BF16)
    return pl.pallas_call(
        body, name="ret_bwd", grid=(nb, nc), in_specs=in_specs, out_specs=out_specs,
        out_shape=[qk, qk, vv, qk, qk, vv, _sds((nb * cx, RH * DK), BF16), _sds((nb * cx, RH * DV), BF16),
                   _sds((nb, 8, 128), F32)],
        scratch_shapes=[pltpu.VMEM((RH, DK, DV), F32), pltpu.VMEM((RH, DK, DV), F32)],
        compiler_params=_params(("parallel", "arbitrary")),
    )(lg, *([px] * 5), do16, hist_f, *([px] * 5), do16, hist_b, *([px] * 3))


def _assemble_lat(dp_all, dk_f, dk_b, dv_f, dv_b, dak16, dvx, dq_f, dq_b, tm):
    t_rows = dk_f.shape[0]

    def body(_, dkf, dkb, dvf, dvb, dak, dav, dqf, dqb, o_ref):
        o_ref[:, RK:RK + RH * DK] = (dkf[...].astype(F32) + dkb[...].astype(F32)).astype(BF16)
        o_ref[:, RV:RV + RH * DV] = (dvf[...].astype(F32) + dvb[...].astype(F32)).astype(BF16)
        o_ref[:, AK:AK + HKV * HD] = dak[...]
        o_ref[:, AV:AV + HKV * HD] = dav[...]
        o_ref[:, RQ:RQ + RH * DK] = (dqf[...].astype(F32) + dqb[...].astype(F32)).astype(BF16)

    args = (dk_f, dk_b, dv_f, dv_b, dak16, dvx, dq_f, dq_b)
    return pl.pallas_call(
        body, name="assemble_lat", grid=(t_rows // tm,),
        in_specs=[pl.BlockSpec(memory_space=pl.ANY)]
        + [pl.BlockSpec((tm, a.shape[1]), lambda i: (i, 0)) for a in args],
        out_specs=pl.BlockSpec((tm, RG), lambda i: (i, 0)), out_shape=_sds(dp_all.shape, BF16),
        input_output_aliases={0: 0},
        compiler_params=_params(("parallel",)),
    )(dp_all, *args)


def _assemble_ctx(dp_all, dck16, dcv16, dcak16, dvc, t_rows, tm):
    c_rows = dck16.shape[0]
    rb = t_rows // tm

    def body(_, dck, dcv, dcak, dcav, o_ref):
        o_ref[:, RK:RK + RH * DK] = dck[...]
        o_ref[:, RV:RV + RH * DV] = dcv[...]
        o_ref[:, AK:AK + HKV * HD] = dcak[...]
        o_ref[:, AV:AV + HKV * HD] = dcav[...]
        o_ref[:, KV_COLS:] = jnp.zeros((tm, IN_COLS - KV_COLS), BF16)

    args = (dck16, dcv16, dcak16, dvc)
    return pl.pallas_call(
        body, name="assemble_ctx", grid=(c_rows // tm,),
        in_specs=[pl.BlockSpec(memory_space=pl.ANY)]
        + [pl.BlockSpec((tm, a.shape[1]), lambda i: (i, 0)) for a in args],
        out_specs=pl.BlockSpec((tm, IN_COLS), lambda i: (rb + i, 0)), out_shape=_sds(dp_all.shape, BF16),
        input_output_aliases={0: 0},
        compiler_params=_params(("parallel",)),
    )(dp_all, *args)


def _norm_bwd(dh, x2, mod3, norm_w, dxn, row_off, rows_per_group, group0, tm, name):
    with_dx = dxn is not None
    rows = x2.shape[0]
    rb0 = row_off // tm
    bpg = rows_per_group // tm
    ngroups = rows // rows_per_group

    def body(*refs):
        if with_dx:
            dh_ref, x_ref, sc_ref, nw_ref, dxn_ref, dx_ref, dsh_ref, dsc_ref, dnw_ref = refs
        else:
            dh_ref, x_ref, sc_ref, nw_ref, dsh_ref, dsc_ref, dnw_ref = refs
        i = pl.program_id(0)
        dhv = dh_ref[...]
        xv = x_ref[...]
        nw = nw_ref[...]
        r = lax.rsqrt(jnp.mean(xv * xv, axis=-1, keepdims=True) + EPS)
        xh = xv * r
        dm = dhv * (1.0 + sc_ref[...])
        dsh = jnp.sum(dhv, axis=0, keepdims=True)
        dsc = jnp.sum(dhv * (xh * nw), axis=0, keepdims=True)
        dnw = jnp.sum(dm * xh, axis=0, keepdims=True)
        if with_dx:
            dxh = dm * nw
            dx_ref[...] = dxn_ref[...] + r * (dxh - xh * jnp.mean(dxh * xh, axis=-1, keepdims=True))

        @pl.when(i % bpg == 0)
        def _():
            dsh_ref[...] = dsh
            dsc_ref[...] = dsc

        @pl.when(i % bpg != 0)
        def _():
            dsh_ref[...] += dsh
            dsc_ref[...] += dsc

        @pl.when(i == 0)
        def _():
            dnw_ref[...] = dnw

        @pl.when(i > 0)
        def _():
            dnw_ref[...] += dnw

    grp = pl.BlockSpec((None, 1, D), lambda i: (i // bpg, 0, 0))
    in_specs = [pl.BlockSpec((tm, D), lambda i: (rb0 + i, 0)), pl.BlockSpec((tm, D), lambda i: (i, 0)),
                pl.BlockSpec((None, 1, D), lambda i: (group0 + i // bpg, 0, 1)),
                pl.BlockSpec((1, D), lambda i: (0, 0))]
    args = [dh, x2, mod3, norm_w]
    out_specs = [grp, grp, pl.BlockSpec((1, D), lambda i: (0, 0))]
    out_shape = [_sds((ngroups, 1, D), F32), _sds((ngroups, 1, D), F32), _sds((1, D), F32)]
    if with_dx:
        in_specs.append(pl.BlockSpec((tm, D), lambda i: (i, 0)))
        args.append(dxn)
        out_specs.insert(0, pl.BlockSpec((tm, D), lambda i: (i, 0)))
        out_shape.insert(0, _sds((rows, D), F32))
    return pl.pallas_call(
        body, name=name, grid=(rows // tm,), in_specs=in_specs, out_specs=out_specs, out_shape=out_shape,
        compiler_params=_params(("arbitrary",)),
    )(*args)


def _small_final(dmod_all, dmodc_parts, c_rows, dm_loc_rows, nw_parts, misc_parts, c_ctx, r_pad, w_ada16):
    loc = dm_loc_rows.shape[1]

    def body(dm_ref, dmc_ref, c_ref, dml_ref, nwp_ref, mp_ref, cc_ref, r_ref, w_ref,
             gb_ref, gc_ref, gnw_ref, misc_ref, gwa_ref):
        dmc = jnp.sum(dmc_ref[...], axis=0, keepdims=True)
        gb_ref[...] = jnp.sum(dm_ref[...], axis=0, keepdims=True) + dmc
        dsc = _dot(jnp.broadcast_to(dmc, (8, 3 * D)).astype(BF16), w_ref[...], NT)[0:1, :]
        gc_ref[...] = dsc * _dsilu(cc_ref[...])
        gnw_ref[...] = jnp.sum(nwp_ref[...], axis=0, keepdims=True)
        misc = jnp.sum(mp_ref[...], axis=0, keepdims=True)
        y = jnp.exp2(r_ref[...])
        lane = lax.broadcasted_iota(jnp.int32, (1, D), 1)
        is_decay = jnp.logical_and(lane >= 2 * HD, lane < 2 * HD + 2 * RH)
        misc_ref[...] = misc * jnp.where(is_decay, -(y * np.float32(np.log(2.0))) / (1.0 - y), 1.0)
        gwa_ref[...] = _dot(_silu(c_ref[...]).astype(BF16), dml_ref[...].astype(BF16), TN)

    return pl.pallas_call(
        body, name="small_final",
        out_shape=[_sds((1, 3 * D), F32), _sds((1, D), F32), _sds((1, D), F32), _sds((1, D), F32), _sds((D, loc), F32)],
        compiler_params=pltpu.CompilerParams(vmem_limit_bytes=VMEM_LIMIT),
    )(dmod_all, dmodc_parts, c_rows, dm_loc_rows, nw_parts, misc_parts, c_ctx, r_pad, w_ada16)


def _adamw_math(w, g, m, v):
    nm = B1 * m + (1.0 - B1) * g
    nv = B2 * v + (1.0 - B2) * (g * g)
    return -LR * ((nm / (1.0 - B1 ** STEP)) / (jnp.sqrt(nv / (1.0 - B2 ** STEP)) + ADAM_EPS) + WD * w), nm, nv


def _adamw(w, g, m, v, name):
    rows, cols = w.shape
    tm = _pick(rows, 448, 8)

    def body(w_ref, g_ref, m_ref, v_ref, d_ref, nm_ref, nv_ref):
        d_ref[...], nm_ref[...], nv_ref[...] = _adamw_math(w_ref[...], g_ref[...], m_ref[...], v_ref[...])

    blk = pl.BlockSpec((tm, cols), lambda i: (i, 0))
    return pl.pallas_call(
        body, name=name, grid=(rows // tm,), in_specs=[blk] * 4, out_specs=[blk] * 3,
        out_shape=[_sds((rows, cols), F32)] * 3, compiler_params=_params(("parallel",)),
    )(w, g, m, v)


def _adamw_small(wgmv):
    n = len(wgmv)

    def body(*refs):
        ins, outs = refs[:4 * n], refs[4 * n:]
        for k in range(n):
            w, g, m, v = [r[...] for r in ins[4 * k:4 * k + 4]]
            outs[3 * k][...], outs[3 * k + 1][...], outs[3 * k + 2][...] = _adamw_math(w, g, m, v)

    out = pl.pallas_call(
        body, name="adamw_small", out_shape=[_sds(t[0].shape, F32) for t in wgmv for _ in range(3)],
    )(*[a for t in wgmv for a in t])
    return [out[3 * k:3 * k + 3] for k in range(n)]


def _mesh_pos():
    return lax.axis_index("x"), lax.axis_index("y"), lax.axis_index("c")


def _all_gather(arrs, name):
    n = len(arrs)

    def body(*refs):
        ins, outs = refs[:n], refs[n:2 * n]
        send_sems, recv_sems, local_sems = refs[2 * n:]
        x, y, c = _mesh_pos()
        me, sib = (x, y, c), (x, y, 1 - c)
        chips = [(1 - x, y), (x, 1 - y), (1 - x, 1 - y)]

        def slot(p):
            return 4 * p[0] + 2 * p[1] + p[2]

        def copy(a, k, block, to, own):
            dst = outs[a].at[slot(block)]
            return pltpu.make_async_remote_copy(
                src_ref=ins[a] if own else dst, dst_ref=dst, send_sem=send_sems.at[a, k], recv_sem=recv_sems.at[a, k],
                device_id=to, device_id_type=MESH_T)

        mine = [pltpu.make_async_copy(ins[a], outs[a].at[slot(me)], local_sems.at[a]) for a in range(n)]
        for cp in mine:
            cp.start()
        first = []
        for a in range(n):
            first.append(copy(a, 0, me, sib, True))
            first += [copy(a, 1 + j, me, (*chip, c), True) for j, chip in enumerate(chips)]
        for cp in first:
            cp.start()
        passed = []
        for j, chip in enumerate(chips):
            for a in range(n):
                copy(a, 1 + j, (*chip, c), me, False).wait_recv()
                fwd = copy(a, 4 + j, (*chip, c), sib, False)
                fwd.start()
                passed.append(fwd)
        for a in range(n):
            copy(a, 0, sib, me, False).wait_recv()
            for j, chip in enumerate(chips):
                copy(a, 4 + j, (*chip, 1 - c), me, False).wait_recv()
        for cp in first + passed:
            cp.wait_send()
        for cp in mine:
            cp.wait()

    hbm = pl.BlockSpec(memory_space=pl.ANY)
    return pl.pallas_call(
        body, name=name, in_specs=[hbm] * n, out_specs=[hbm] * n,
        out_shape=[_sds((N_DEV,) + a.shape, a.dtype) for a in arrs],
        scratch_shapes=[pltpu.SemaphoreType.DMA((n, 7)), pltpu.SemaphoreType.DMA((n, 7)), pltpu.SemaphoreType.DMA((n,))],
    )(*arrs)


def _pair_add(parts, gots, core, name):
    n = len(parts)
    cols = parts[0].shape[2]
    tiles = min(p.shape[1] for p in parts) // _pick(min(p.shape[1] for p in parts), 672, 16)

    def body(core_ref, *refs):
        for p_ref, g_ref, o_ref in zip(refs[:n], refs[n:2 * n], refs[2 * n:]):
            o_ref[...] = (p_ref[...].astype(F32) + g_ref[...].astype(F32)).astype(BF16)

    def blk(p):
        return pl.BlockSpec((None, p.shape[1] // tiles, cols), lambda k, i, cr: (k, i, 0))

    return pl.pallas_call(
        body, name=name,
        grid_spec=pltpu.PrefetchScalarGridSpec(
            num_scalar_prefetch=1, grid=(4, tiles),
            in_specs=[pl.BlockSpec((None, None, p.shape[1] // tiles, cols), lambda k, i, cr: (k, cr[0], i, 0))
                      for p in parts] + [blk(p) for p in parts],
            out_specs=[blk(p) for p in parts]),
        out_shape=[_sds((4,) + p.shape[1:], BF16) for p in parts], compiler_params=_params(("parallel", "parallel")),
    )(core, *[p.reshape(4, 2, *p.shape[1:]) for p in parts], *gots)


def _chip_sum_adamw(pair_sums, landed, chip, wmv, name):
    n = len(pair_sums)
    cols = pair_sums[0].shape[2]
    fewest = min(s_.shape[1] for s_ in pair_sums)
    tiles = fewest // _pick(fewest, 448, 16)

    def body(chip_ref, *refs):
        for k in range(n):
            s_ref, l_ref, w_ref, m_ref, v_ref = refs[5 * k:5 * k + 5]
            g_ref, d_ref, nm_ref, nv_ref = refs[5 * n + 4 * k:5 * n + 4 * k + 4]
            acc = s_ref[...].astype(F32)
            for j in range(3):
                acc = acc + l_ref[j].astype(F32)
            g_ref[...] = acc
            d_ref[...], nm_ref[...], nv_ref[...] = _adamw_math(w_ref[...], acc, m_ref[...], v_ref[...])

    in_specs, out_specs, out_shape, args = [], [], [], []
    for s_, l_, t in zip(pair_sums, landed, wmv):
        tm = s_.shape[1] // tiles
        blk = pl.BlockSpec((tm, cols), lambda i, ch: (i, 0))
        in_specs += [pl.BlockSpec((None, tm, cols), lambda i, ch: (ch[0], i, 0)),
                     pl.BlockSpec((3, tm, cols), lambda i, ch: (0, i, 0)), blk, blk, blk]
        out_specs += [blk] * 4
        out_shape += [_sds(s_.shape[1:], F32)] * 4
        args += [s_, l_, *t]
    out = pl.pallas_call(
        body, name=name,
        grid_spec=pltpu.PrefetchScalarGridSpec(num_scalar_prefetch=1, grid=(tiles,), in_specs=in_specs,
                                               out_specs=out_specs),
        out_shape=out_shape, compiler_params=_params(("parallel",)),
    )(chip, *args)
    return [out[4 * k:4 * k + 4] for k in range(n)]


_HBM = pl.BlockSpec(memory_space=pltpu.HBM)
_SEM = pl.BlockSpec(memory_space=pltpu.SEMAPHORE)
_EFFECT = pltpu.SideEffectType.DATAFLOW_SIDE_EFFECTING


def _chip_routes(n):
    def plan(x, y, c):
        routes = []
        for a in range(n):
            for j in range(1, 4):
                px, py = x ^ (j >> 1), y ^ (j & 1)
                routes.append((a, 2 * px + py, (px, py, c), j - 1))
        return routes
    return plan, 3 * n


def _pair_routes(n):
    def plan(x, y, c):
        return [(a, 2 * k + 1 - c, (x, y, 1 - c), k) for a in range(n) for k in range(4)]
    return plan, 4 * n


def _bcast_routes(n):
    def plan(x, y, c):
        routes = []
        for a in range(n):
            for k in range(1, N_DEV):
                peer = (x ^ ((k >> 2) & 1), y ^ ((k >> 1) & 1), c ^ (k & 1))
                routes.append((a, 0, peer, 4 * x + 2 * y + c))
        return routes
    return plan, 7 * n


def _route_copies(srcs, lands, send_sems, recv_sems, routes):
    return [pltpu.make_async_remote_copy(
        src_ref=srcs[a].at[sb], dst_ref=lands[a].at[lb], send_sem=send_sems.at[r], recv_sem=recv_sems.at[r],
        device_id=peer, device_id_type=MESH_T) for r, (a, sb, peer, lb) in enumerate(routes)]


def _exchange_start(srcs, lands, routes, name, after=()):
    plan, count = routes
    n = len(srcs)
    n_in = 2 * n + len(after)

    def body(*refs):
        send_sems, recv_sems = refs[n_in], refs[n_in + 1]
        token = refs[-1]
        for cp in _route_copies(refs[:n], refs[n:2 * n], send_sems, recv_sems, plan(*_mesh_pos())):
            cp.start()
        token[...] = jnp.zeros_like(token)

    args = [pltpu.with_memory_space_constraint(a, pltpu.HBM) for a in list(srcs) + list(lands)]
    out = pl.pallas_call(
        body, name=name,
        out_shape=(pltpu.SemaphoreType.DMA((count,)), pltpu.SemaphoreType.DMA((count,)),
                   *[pltpu.HBM(a.shape, a.dtype) for a in args], _sds((8, 128), F32)),
        in_specs=[_HBM] * (2 * n) + [pl.BlockSpec(memory_space=pl.ANY)] * len(after),
        out_specs=(_SEM, _SEM, *([_HBM] * (2 * n)), pl.BlockSpec(memory_space=pltpu.VMEM)),
        input_output_aliases={i: 2 + i for i in range(2 * n)},
        compiler_params=pltpu.CompilerParams(has_side_effects=_EFFECT),
    )(*args, *after)
    return (out[0], out[1], list(out[2:2 + 2 * n]), routes), out[-1]


def _exchange_wait(state, after, name):
    send_sems, recv_sems, bufs, (plan, count) = state
    n = len(bufs) // 2

    def body(*refs):
        send_s, recv_s = refs[2 * n], refs[2 * n + 1]
        for cp in _route_copies(refs[:n], refs[n:2 * n], send_s, recv_s, plan(*_mesh_pos())):
            cp.wait_send()
            cp.wait_recv()

    out = pl.pallas_call(
        body, name=name, out_shape=tuple(pltpu.HBM(a.shape, a.dtype) for a in bufs),
        in_specs=[_HBM] * (2 * n) + [_SEM, _SEM, pl.BlockSpec(memory_space=pl.ANY)], out_specs=tuple([_HBM] * (2 * n)),
        input_output_aliases={i: i for i in range(2 * n)},
        compiler_params=pltpu.CompilerParams(has_side_effects=_EFFECT),
    )(*bufs, send_sems, recv_sems, after)
    return list(out[:n]), list(out[n:])


def _group_routes(js):
    def plan(x, y, c):
        return [(0, 0, (x ^ (j >> 1), y ^ (j & 1), c), 2 * j + c) for j in js]
    return plan, len(js)


def _pair_fill(groups, js, name, after=()):
    def body(*refs):
        g_ref, send_sems, recv_sems = refs[-3:]
        x, y, c = _mesh_pos()
        sends = []
        for n, j in enumerate(js):
            mine = g_ref.at[2 * j + c]
            sends.append(pltpu.make_async_remote_copy(
                src_ref=mine, dst_ref=mine, send_sem=send_sems.at[n], recv_sem=recv_sems.at[n],
                device_id=(x, y, 1 - c), device_id_type=MESH_T))
        for cp in sends:
            cp.start()
        for n, j in enumerate(js):
            pltpu.make_async_remote_copy(
                src_ref=g_ref.at[2 * j + c], dst_ref=g_ref.at[2 * j + 1 - c], send_sem=send_sems.at[n],
                recv_sem=recv_sems.at[n], device_id=(x, y, 1 - c), device_id_type=MESH_T).wait_recv()
        for cp in sends:
            cp.wait_send()

    hbm = pl.BlockSpec(memory_space=pl.ANY)
    return pl.pallas_call(
        body, name=name, in_specs=[hbm] * (1 + len(after)), out_specs=hbm, out_shape=_sds(groups.shape, groups.dtype),
        input_output_aliases={0: 0},
        scratch_shapes=[pltpu.SemaphoreType.DMA((len(js),)), pltpu.SemaphoreType.DMA((len(js),))],
    )(groups, *after)


def _in_proj_group(h_all, groups, j0, ng, chip, px_prev, after, name):
    rows_all = h_all.shape[0]
    gcols = IN_COLS // 4
    tm = _pick(rows_all, 1536, 128)
    g4 = groups.reshape(4, gcols, D)

    n_lead = (1 if px_prev is not None else 0) + len(after)
    lead = ([px_prev] if px_prev is not None else []) + list(after)

    def body(chip_ref, *refs):
        h_ref, w_ref, o_ref = refs[n_lead:]
        o_ref[...] = _dot(h_ref[...], w_ref[...], NT).astype(BF16)

    return pl.pallas_call(
        body, name=name,
        grid_spec=pltpu.PrefetchScalarGridSpec(
            num_scalar_prefetch=1, grid=(ng, rows_all // tm),
            in_specs=[pl.BlockSpec(memory_space=pl.ANY)] * n_lead
            + [pl.BlockSpec((tm, D), lambda n, i, ch: (i, 0)),
               pl.BlockSpec((None, gcols, D), lambda n, i, ch: (j0 + n, 0, 0))],
            out_specs=pl.BlockSpec((tm, gcols), lambda n, i, ch: (i, ch[0] ^ (j0 + n)))),
        out_shape=_sds((rows_all, IN_COLS), BF16),
        input_output_aliases={1: 0} if px_prev is not None else {},
        compiler_params=_params(("parallel", "parallel")),
    )(chip, *lead, h_all, g4)


def _d_h_groups(dp_all, groups, chip, i0, ni, dh_prev, after):
    rows_all = dp_all.shape[0]
    gcols = IN_COLS // 4
    tm = _D_H_ROWS
    g4 = groups.reshape(4, gcols, D)
    lead = ([dh_prev] if dh_prev is not None else []) + list(after)
    n_lead = len(lead)

    def body(chip_ref, *refs):
        a_ref, w_ref, o_ref = refs[n_lead:]
        j = pl.program_id(1)
        part = _dot(a_ref[...], w_ref[...])

        @pl.when(j == 0)
        def _():
            o_ref[...] = part

        @pl.when(j > 0)
        def _():
            o_ref[...] += part

    return pl.pallas_call(
        body, name="d_h_%d" % i0,
        grid_spec=pltpu.PrefetchScalarGridSpec(
            num_scalar_prefetch=1, grid=(ni, 4),
            in_specs=[pl.BlockSpec(memory_space=pl.ANY)] * n_lead
            + [pl.BlockSpec((tm, gcols), lambda i, j, ch: (i0 + i, ch[0] ^ j)),
               pl.BlockSpec((None, gcols, D), lambda i, j, ch: (j, 0, 0))],
            out_specs=pl.BlockSpec((tm, D), lambda i, j, ch: (i0 + i, 0))),
        out_shape=_sds((rows_all, D), F32),
        input_output_aliases={1: 0} if dh_prev is not None else {},
        compiler_params=_params(("parallel", "arbitrary")),
    )(chip, *lead, dp_all, g4)


def _reduce_scatter_send(parts, got, core, name):
    sums = _pair_add(parts, got, core, name + "_add")
    lands = [lax.empty((3,) + s_.shape[1:], BF16) for s_ in sums]
    return _exchange_start(sums, lands, _chip_routes(len(sums)), name + "_start")


def _reduce_scatter_finish(rs_state, after, chip, wmv, name):
    sums, landed = _exchange_wait(rs_state, after, name + "_wait")
    return _chip_sum_adamw(sums, landed, chip, wmv, name + "_adamw")


def _local_step(x, c, ctx, norm_w, ret_log2_decay, q_norm_w, k_norm_w, loss_target,
                mod, proj_in, get_w_o, on_out_grads, on_in_grad, started=()):
    nb, seq, _ = x.shape
    cx = ctx.shape[1]
    t_rows, c_rows = nb * seq, nb * cx
    rows_all = t_rows + c_rows
    nc = seq // CH
    tm = _pick(seq, 256, 128)
    te = _pick(seq, 512, 128)
    assert cx % tm == 0 and t_rows % cx == 0 and seq % GRID_W == 0

    x2 = x.reshape(t_rows, D)
    ctx2 = ctx.reshape(c_rows, D)
    tgt = loss_target.reshape(t_rows, D)
    lg = _log_gamma(ret_log2_decay)
    cos, sin = _rope_tables(seq)

    mod3 = mod[:, None, :]
    h_all = _norm_fwd(x2, mod3, norm_w, rows_all, 0, seq, 0, None, te, "norm_fwd", after=started)
    h_all = _norm_fwd(ctx2, mod3, norm_w, rows_all, t_rows, c_rows, nb, h_all, tm, "norm_fwd_ctx")
    px = proj_in(h_all)
    o_f, o_b, hist_f, hist_b = _ret_fwd(px, lg, nb, nc, cx)
    q16 = _qk_prep(px, q_norm_w, cos, sin, t_rows, 0, AQ, HQ, 4, seq, te, "q_prep")
    kx16 = _qk_prep(px, k_norm_w, cos, sin, t_rows, 0, AK, HKV, HKV, seq, te, "k_prep")
    kc16 = _qk_prep(px, k_norm_w, None, None, c_rows, t_rows, AK, HKV, HKV, seq, tm, "kc_prep")
    o_att, yatt16, lse = _att_fwd(q16, kx16, kc16, px, nb, seq, cx, te)
    w_o_ret16, w_o_att16, w_out16 = get_w_o(lse)
    yret16, a_ret, a_att, dxn, dout16, dgate, loss_b, gw_out = _merge_out(
        o_f, o_b, yatt16, px, w_o_ret16, w_o_att16, w_out16, x2, tgt, mod3, nb, seq, tm)

    da_ret16, do16, dao16, delta, dp_all, gw_o_att = _bwd_branches(
        dout16, w_out16, w_o_ret16, w_o_att16, px, a_ret, a_att, o_f, o_b, o_att, yatt16, rows_all, tm)
    gw_o_ret = _matmul(yret16, da_ret16, ta=True, tm=D, tn=D, tk=D, out_dtype=BF16, name="gw_o_ret")
    out_send, out_started = on_out_grads([gw_o_ret, gw_o_att, gw_out])
    dp_all, gq, gk, dak16, dav16, dcak16, dcav16 = _att_bwd(q16, kx16, kc16, px, dao16, delta, lse, q_norm_w, k_norm_w,
                                                            cos, sin, dp_all, nb, seq, cx, te, after=out_started)
    out_state, out_sent = out_send(gq)
    dq_f, dk_f, dv_f, dq_b, dk_b, dv_b, dck16, dcv16, dlg_scan = _ret_bwd(px, lg, do16, hist_f, hist_b, nb, nc, cx)
    dp_all = _assemble_lat(dp_all, dk_f, dk_b, dv_f, dv_b, dak16, dav16, dq_f, dq_b, tm)
    dp_all = _assemble_ctx(dp_all, dck16, dcv16, dcak16, dcav16, t_rows, tm)
    gw_in_t = _matmul(dp_all, h_all, ta=True, tm=1536, tn=D, tk=2304, out_dtype=BF16, name="gw_in", after=out_sent)
    in_state, dh = on_in_grad(gw_in_t, dp_all)
    grad_x, dsh, dsc, gnw_lat = _norm_bwd(dh, x2, mod3, norm_w, dxn, 0, seq, 0, te, "norm_bwd")
    dsh_c, dsc_c, gnw_ctx = _norm_bwd(dh, ctx2, mod3, norm_w, None, t_rows, c_rows, nb, tm, "norm_bwd_ctx")

    dlg = jnp.sum(dlg_scan[:, :, 0], axis=0).reshape(1, 2 * RH)
    misc = jnp.concatenate([gq, gk, dlg, jnp.sum(loss_b[:, 0, 0]).reshape(1, 1),
                            jnp.zeros((1, D - 2 * HD - 2 * RH - 1), F32)], axis=1)
    rows = []
    for b in range(nb):
        rows += [dsh[b], dsc[b], dgate[b]]
    rows += [dsh_c[0], dsc_c[0]] + [c[b:b + 1] for b in range(nb)] + [gnw_lat + gnw_ctx, misc]
    payload = jnp.concatenate(rows + [jnp.zeros((PAY_ROWS - len(rows), D), F32)], axis=0)
    return grad_x.reshape(nb, seq, D), out_state, in_state, payload


def _finish_small(gathered, nb, c_ctx, ret_log2_decay, w_ada16, dev):
    n_dev = gathered.shape[0]
    loc = 3 * D // n_dev
    dmod_all = gathered[:, :3 * nb].reshape(n_dev * nb, 3 * D)
    dmodc_parts = jnp.concatenate([gathered[:, 3 * nb:3 * nb + 2].reshape(n_dev, 2 * D), jnp.zeros((n_dev, D), F32)], axis=1)
    c_all = gathered[:, 3 * nb + 2:4 * nb + 2].reshape(n_dev * nb, D)
    nw_parts = gathered[:, 4 * nb + 2]
    misc_parts = gathered[:, 4 * nb + 3]
    n_rows = n_dev * nb + n_dev
    pad = (-n_rows) % 16
    c_rows = jnp.concatenate([c_all, jnp.broadcast_to(c_ctx.reshape(1, D), (n_dev, D)), jnp.zeros((pad, D), F32)], axis=0)
    dm_rows = jnp.concatenate([dmod_all, dmodc_parts, jnp.zeros((pad, 3 * D), F32)], axis=0)
    dm_loc_rows = lax.dynamic_slice_in_dim(dm_rows, dev * loc, loc, axis=1)
    r_pad = jnp.full((1, D), -1.0, F32).at[:, 2 * HD:2 * HD + 2 * RH].set(ret_log2_decay.reshape(1, 2 * RH))
    gb, gc, gnw, misc, gwa = _small_final(dmod_all, dmodc_parts, c_rows, dm_loc_rows, nw_parts, misc_parts,
                                          c_ctx.reshape(1, D), r_pad, w_ada16)
    return (gb, gc, gnw, misc[:, :HD], misc[:, HD:2 * HD], misc[:, 2 * HD:2 * HD + 2 * RH], gwa,
            misc[0, 2 * HD + 2 * RH])


def kernel(x, c, ctx, c_ctx, norm_w, w_ada, b_ada, w_in, ret_log2_decay, q_norm_w, k_norm_w, w_o_ret, w_o_att, w_out, loss_target, m_c_ctx, m_norm_w, m_w_ada, m_b_ada, m_w_in, m_ret_log2_decay, m_q_norm_w, m_k_norm_w, m_w_o_ret, m_w_o_att, m_w_out, v_c_ctx, v_norm_w, v_w_ada, v_b_ada, v_w_in, v_ret_log2_decay, v_q_norm_w, v_k_norm_w, v_w_o_ret, v_w_o_att, v_w_out):
    nb = x.shape[0]
    mx, my, mc = _mesh_pos()
    dev = 4 * mx + 2 * my + mc
    core = jnp.reshape(mc, (1,)).astype(jnp.int32)
    chip = jnp.reshape(2 * mx + my, (1,)).astype(jnp.int32)

    n_loc = 3 * D // N_DEV
    c8 = jnp.zeros((8, D), F32).at[:nb].set(c).at[nb].set(c_ctx)
    c_land = lax.dynamic_update_slice(lax.empty((N_DEV, 8, D), F32), c8[None], (dev, 0, 0))
    c_state, c_token = _exchange_start([c8[None]], [c_land], _bcast_routes(1), "gather_c_start")
    w_in_t = jnp.transpose(w_in[0])
    in_shard = w_in_t.astype(BF16)
    groups = lax.dynamic_update_slice(lax.empty((N_DEV,) + in_shard.shape, BF16), in_shard[None], (mc, 0, 0))
    groups = _pair_fill(groups, (0,), "gather_in_pair", after=(c_token,))
    _, (c_all,) = _exchange_wait(c_state, groups, "gather_c_wait")
    ada_shard = w_ada[0].astype(BF16)
    b_loc = lax.dynamic_slice(b_ada, (0, dev * n_loc), (1, n_loc))
    mod_cols = _mod_part(c_all.reshape(N_DEV * 8, D), ada_shard, b_loc)
    (mod_all,) = _all_gather([mod_cols], "gather_mod")
    mod = jnp.transpose(lax.dynamic_slice(mod_all, (0, dev * 8, 0), (N_DEV, 8, n_loc)), (1, 0, 2)).reshape(8, 3 * D)
    ada_land = lax.dynamic_update_slice(lax.empty((N_DEV,) + ada_shard.shape, BF16), ada_shard[None], (dev, 0, 0))

    (near_send, near_recv, near_bufs, near_routes), gin_token = _exchange_start(
        [in_shard[None]], [groups], _group_routes((1, 2)), "gather_in_start", after=(mod_all,))
    w_in_groups, wo_states, ada_states = [], [], []
    wo_shards = [w_[0].astype(BF16) for w_ in (w_o_ret, w_o_att, w_out)]
    wo_lands = [lax.dynamic_update_slice(lax.empty((N_DEV,) + s_.shape, BF16), s_[None], (dev, 0, 0)) for s_ in wo_shards]

    def _state(send, recv, src, groups, routes):
        return send, recv, [src, groups], routes

    def proj_in(h_all):
        src, groups = near_bufs
        px = _in_proj_group(h_all, groups, 0, 1, chip, None, (gin_token,), "in_proj_0")
        (src,), (groups,) = _exchange_wait(_state(near_send, near_recv, src, groups, near_routes), px,
                                           "gather_in_wait_near")
        groups = _pair_fill(groups, (1, 2), "gather_in_fill_near")
        (far_send, far_recv, (src, groups), far_routes), far_token = _exchange_start(
            [src], [groups], _group_routes((3,)), "gather_in_start_far")
        wo_state, wo_token = _exchange_start([s_[None] for s_ in wo_shards], wo_lands, _bcast_routes(3),
                                             "gather_wo_start", after=(far_token,))
        wo_states.append(wo_state)
        ada_state, ada_token = _exchange_start([ada_shard[None]], [ada_land], _bcast_routes(1), "gather_ada_start",
                                               after=(wo_token,))
        ada_states.append(ada_state)
        px = _in_proj_group(h_all, groups, 1, 2, chip, px, (ada_token,), "in_proj_near")
        (src,), (groups,) = _exchange_wait(_state(far_send, far_recv, src, groups, far_routes), px,
                                           "gather_in_wait_far")
        groups = _pair_fill(groups, (3,), "gather_in_fill_far")
        px = _in_proj_group(h_all, groups, 3, 1, chip, px, (), "in_proj_far")
        w_in_groups.append(groups)
        return px

    def get_w_o(after):
        _, (l_ret, l_att, l_out) = _exchange_wait(wo_states[0], after, "gather_wo_wait")
        return l_ret.reshape(RH * DV, D), l_att.reshape(D, D), l_out.reshape(D, D)

    def on_out_grads(grads):
        parts = [g_.reshape(N_DEV, g_.shape[0] // N_DEV, D) for g_ in grads]
        lands = [lax.empty((4,) + p_.shape[1:], BF16) for p_ in parts]
        pair_state, pair_token = _exchange_start(parts, lands, _pair_routes(len(parts)), "rs_out_pair_start")

        def send(after):
            parts_, got = _exchange_wait(pair_state, after, "rs_out_pair_wait")
            state, token = _reduce_scatter_send(parts_, got, core, "rs_out")
            return state, (token,)

        return send, (pair_token,)

    def on_in_grad(grad, dp_all):
        parts = [grad.reshape(N_DEV, IN_COLS // N_DEV, D)]
        lands = [lax.empty((4,) + p_.shape[1:], BF16) for p_ in parts]
        pair_state, pair_token = _exchange_start(parts, lands, _pair_routes(1), "rs_in_pair_start")
        dh = _d_h_groups(dp_all, w_in_groups[0], chip, 0, 1, None, (pair_token,))
        parts, got = _exchange_wait(pair_state, dh, "rs_in_pair_wait")
        state, token = _reduce_scatter_send(parts, got, core, "rs_in")
        n_tiles = dp_all.shape[0] // _D_H_ROWS
        return state, _d_h_groups(dp_all, w_in_groups[0], chip, 1, n_tiles - 1, dh, (token,))

    grad_x, out_state, in_state, payload = _local_step(
        x, c, ctx, norm_w, ret_log2_decay, q_norm_w, k_norm_w, loss_target,
        mod, proj_in, get_w_o, on_out_grads, on_in_grad, started=(gin_token,))

    pay_land = lax.dynamic_update_slice(lax.empty((N_DEV,) + payload.shape, F32), payload[None], (dev, 0, 0))
    pay_state, pay_token = _exchange_start([payload[None]], [pay_land], _bcast_routes(1), "gather_small_start")

    out_res = _reduce_scatter_finish(out_state, pay_token, chip,
                                     [(w_[0], m_[0], v_[0]) for w_, m_, v_ in ((w_o_ret, m_w_o_ret, v_w_o_ret),
                                                                                (w_o_att, m_w_o_att, v_w_o_att),
                                                                                (w_out, m_w_out, v_w_out))], "rs_out")
    (in_res,) = _reduce_scatter_finish(in_state, out_res[0][0], chip,
                                       [(w_in_t, jnp.transpose(m_w_in[0]), jnp.transpose(v_w_in[0]))], "rs_in")

    _, (gathered,) = _exchange_wait(pay_state, in_res[0], "gather_small_wait")
    _, (l_ada,) = _exchange_wait(ada_states[0], gathered, "gather_ada_wait")
    w_ada16 = jnp.transpose(l_ada, (1, 0, 2)).reshape(D, 3 * D)
    gb, gc, gnw, gq, gk, gr, gwa, loss = _finish_small(gathered, nb, c_ctx, ret_log2_decay, w_ada16, dev)
    big = {4: [jnp.transpose(r)[None] for r in in_res]}
    for i, res in zip((8, 9, 10), out_res):
        big[i] = [r[None] for r in res]
    small_g = {0: gc.reshape(c_ctx.shape), 1: gnw, 2: gwa[None], 3: gb, 5: gr.reshape(ret_log2_decay.shape), 6: gq, 7: gk}
    weights = [c_ctx, norm_w, w_ada, b_ada, w_in, ret_log2_decay, q_norm_w, k_norm_w, w_o_ret, w_o_att, w_out]
    ms = [m_c_ctx, m_norm_w, m_w_ada, m_b_ada, m_w_in, m_ret_log2_decay, m_q_norm_w, m_k_norm_w, m_w_o_ret, m_w_o_att, m_w_out]
    vs = [v_c_ctx, v_norm_w, v_w_ada, v_b_ada, v_w_in, v_ret_log2_decay, v_q_norm_w, v_k_norm_w, v_w_o_ret, v_w_o_att, v_w_out]
    def rows2(i):
        return [a.reshape(-1, weights[i].shape[-1]) for a in (weights[i], small_g[i], ms[i], vs[i])]

    small_ids = [i for i in small_g if i != 2]
    steps = dict(zip(small_ids, _adamw_small([rows2(i) for i in small_ids])))
    steps[2] = _adamw(*rows2(2), "adamw_w_ada")
    grads, deltas, new_ms, new_vs = [], [], [], []
    for i, w in enumerate(weights):
        res = big[i] if i in big else [small_g[i]] + [r.reshape(w.shape) for r in steps[i]]
        for lst, r in zip((grads, deltas, new_ms, new_vs), res):
            lst.append(r)
    return (loss, grad_x, *grads, *deltas, *new_ms, *new_vs)
```

```python
import numpy as np
import jax
import jax.numpy as jnp
from jax import lax
from jax.experimental import pallas as pl
from jax.experimental.pallas import tpu as pltpu

F32 = jnp.float32
BF16 = jnp.bfloat16

D = 1024
RH, DK, DV, CH = 4, 256, 512, 256
HQ, HKV, HD = 8, 2, 128
GRID_W = 64
ROPE_THETA = 10000.0
EPS = 1e-6
RK, RV, AK, AV, RQ, RG, AQ, AG, MR, MA = 0, 1024, 3072, 3328, 3584, 4608, 6656, 7680, 8704, 9728
IN_COLS = 10752
KV_COLS = 3584
N_DEV = 8
LR, B1, B2, ADAM_EPS, WD, STEP = 0.001, 0.9, 0.999, 1e-08, 0.01, 10
PAY_ROWS = 16
VMEM_LIMIT = 56 * 1024 * 1024
_D_H_ROWS = 1536
MESH_T = pl.DeviceIdType.MESH

NT = (((1,), (1,)), ((), ()))
TN = (((0,), (0,)), ((), ()))
SM_C = (HD ** -0.5) * float(np.log2(np.e))


def _params(sem):
    return pltpu.CompilerParams(dimension_semantics=sem, vmem_limit_bytes=VMEM_LIMIT)


def _pick(n, target, mult=8):
    best = None
    for t in range(mult, min(n, target) + 1, mult):
        if n % t == 0:
            best = t
    return best or n


def _dot(a, b, dn=None):
    if dn is None:
        return jnp.dot(a, b, preferred_element_type=F32)
    return lax.dot_general(a, b, dn, preferred_element_type=F32)


def _sig(v):
    return jax.nn.sigmoid(v)


def _silu(v):
    return v * _sig(v)


def _dsilu(v):
    s = _sig(v)
    return s * (1.0 + v * (1.0 - s))


def _sds(shape, dtype):
    return jax.ShapeDtypeStruct(shape, dtype)


def _matmul(a, b, *, ta=False, tb=False, tm, tn, tk, out_dtype, name, after=()):
    m = a.shape[1] if ta else a.shape[0]
    kdim = a.shape[0] if ta else a.shape[1]
    n = b.shape[0] if tb else b.shape[1]
    tm, tn, tk = _pick(m, tm, 128), _pick(n, tn, 128), _pick(kdim, tk, 128)
    nk = kdim // tk
    dn = (((0 if ta else 1,), (1 if tb else 0,)), ((), ()))

    def body(a_ref, b_ref, *rest):
        o_ref, acc_ref = rest[-2:]
        k = pl.program_id(2)
        part = _dot(a_ref[...].astype(BF16), b_ref[...].astype(BF16), dn)
        if nk == 1:
            o_ref[...] = part.astype(o_ref.dtype)
        else:
            @pl.when(k == 0)
            def _():
                acc_ref[...] = part

            @pl.when(k > 0)
            def _():
                acc_ref[...] += part

            @pl.when(k == nk - 1)
            def _():
                o_ref[...] = acc_ref[...].astype(o_ref.dtype)

    a_spec = pl.BlockSpec((tk, tm), lambda i, j, k: (k, i)) if ta else pl.BlockSpec((tm, tk), lambda i, j, k: (i, k))
    b_spec = pl.BlockSpec((tn, tk), lambda i, j, k: (j, k)) if tb else pl.BlockSpec((tk, tn), lambda i, j, k: (k, j))
    return pl.pallas_call(
        body, name=name, grid=(m // tm, n // tn, nk),
        in_specs=[a_spec, b_spec] + [pl.BlockSpec(memory_space=pl.ANY)] * len(after),
        out_specs=pl.BlockSpec((tm, tn), lambda i, j, k: (i, j)), out_shape=_sds((m, n), out_dtype),
        scratch_shapes=[pltpu.VMEM((tm, tn) if nk > 1 else (8, 128), F32)],
        compiler_params=_params(("parallel", "parallel", "arbitrary")),
    )(a, b, *after)


def _log_gamma(r):
    rp = jnp.full((8, 128), -1.0, F32).at[:2, :RH].set(r.reshape(2, RH))

    def body(r_ref, o_ref):
        o_ref[...] = jnp.log1p(-jnp.exp2(r_ref[...]))

    out = pl.pallas_call(body, name="log_gamma", out_shape=_sds((8, 128), F32))(rp)
    return out[:2, :RH]


def _mod_part(c_rows, w_ada_loc16, b_loc):
    def body(c_ref, w_ref, b_ref, o_ref):
        o_ref[...] = _dot(_silu(c_ref[...]).astype(BF16), w_ref[...]) + b_ref[...]

    return pl.pallas_call(
        body, name="mod_part", out_shape=_sds((c_rows.shape[0], w_ada_loc16.shape[1]), F32),
    )(c_rows, w_ada_loc16, b_loc)


def _norm_fwd(x2, mod3, norm_w, rows_all, row_off, rows_per_group, group0, h_prev, tm, name, after=()):
    rows = x2.shape[0]
    rb0 = row_off // tm
    bpg = rows_per_group // tm

    def body(*refs):
        x_ref, sh_ref, sc_ref, nw_ref, o_ref = refs[-5:]
        xv = x_ref[...]
        r = lax.rsqrt(jnp.mean(xv * xv, axis=-1, keepdims=True) + EPS)
        o_ref[...] = ((xv * r) * nw_ref[...] * (1.0 + sc_ref[...]) + sh_ref[...]).astype(BF16)

    in_specs = [pl.BlockSpec((tm, D), lambda i: (i, 0)),
                pl.BlockSpec((None, 1, D), lambda i: (group0 + i // bpg, 0, 0)),
                pl.BlockSpec((None, 1, D), lambda i: (group0 + i // bpg, 0, 1)),
                pl.BlockSpec((1, D), lambda i: (0, 0))]
    in_specs = [pl.BlockSpec(memory_space=pl.ANY)] * len(after) + in_specs
    args = list(after) + [x2, mod3, mod3, norm_w]
    alias = {}
    if h_prev is not None:
        in_specs.insert(0, pl.BlockSpec(memory_space=pl.ANY))
        args.insert(0, h_prev)
        alias = {0: 0}
    return pl.pallas_call(
        body, name=name, grid=(rows // tm,), in_specs=in_specs,
        out_specs=pl.BlockSpec((tm, D), lambda i: (rb0 + i, 0)), out_shape=_sds((rows_all, D), BF16),
        input_output_aliases=alias, compiler_params=_params(("parallel",)),
    )(*args)


def _decays(lg, fwd):
    ii = lax.broadcasted_iota(jnp.int32, (CH, CH), 0)
    jj = lax.broadcasted_iota(jnp.int32, (CH, CH), 1)
    ri = lax.broadcasted_iota(jnp.int32, (CH, 1), 0).astype(F32)
    rel = (ii - jj) if fwd else (jj - ii)
    relf = jnp.maximum(rel, 0).astype(F32)
    mask = jnp.where(rel >= 0, jnp.exp(lg * relf), 0.0)
    qe = (ri + 1.0) if fwd else (CH - ri)
    ke = (CH - 1.0 - ri) if fwd else ri
    return mask, relf, jnp.exp(lg * qe), qe, jnp.exp(lg * ke), ke


def _wide_specs(rowf):
    return [pl.BlockSpec((CH, 2 * DK), lambda b, c: (rowf(b, c), RQ // (2 * DK))),
            pl.BlockSpec((CH, 2 * DK), lambda b, c: (rowf(b, c), RQ // (2 * DK) + 1)),
            pl.BlockSpec((CH, RH * DK), lambda b, c: (rowf(b, c), RK // (RH * DK))),
            pl.BlockSpec((CH, 2 * DV), lambda b, c: (rowf(b, c), RV // (2 * DV))),
            pl.BlockSpec((CH, 2 * DV), lambda b, c: (rowf(b, c), RV // (2 * DV) + 1))]


def _head_qkv(refs, h):
    q0, q1, k, v0, v1 = refs
    lo = h % 2
    q = (q0, q1)[h // 2][:, lo * DK:(lo + 1) * DK].astype(F32)
    kk = k[:, h * DK:(h + 1) * DK].astype(F32) * (DK ** -0.5)
    v16 = (v0, v1)[h // 2][:, lo * DV:(lo + 1) * DV].astype(BF16)
    return q, kk, v16


def _ctx_specs(t_rows, cx):
    rb = t_rows // cx
    return [pl.BlockSpec((cx, RH * DK), lambda b, c: (rb + b, RK // (RH * DK))),
            pl.BlockSpec((cx, 2 * DV), lambda b, c: (rb + b, RV // (2 * DV))),
            pl.BlockSpec((cx, 2 * DV), lambda b, c: (rb + b, RV // (2 * DV) + 1))]


def _ctx_kv(refs, h):
    k, v0, v1 = refs
    kk = k[:, h * DK:(h + 1) * DK].astype(F32) * (DK ** -0.5)
    lo = h % 2
    return kk, (v0, v1)[h // 2][:, lo * DV:(lo + 1) * DV].astype(BF16)


def _ret_fwd(px, lg, nb, nc, cx):
    t_rows = nb * nc * CH

    def body(lg_ref, *refs):
        ins = (refs[0:5], refs[5:10])
        ctx_refs = refs[10:13]
        of_ref, ob_ref, hf_ref, hb_ref, sf, sb = refs[13:]
        c = pl.program_id(1)

        @pl.when(c == 0)
        def _():
            pos = lax.broadcasted_iota(jnp.int32, (cx, 1), 0).astype(F32)
            for h in range(RH):
                k, v16 = _ctx_kv(ctx_refs, h)
                sf[h] = _dot((k * jnp.exp(lg_ref[0, h] * (cx - 1.0 - pos))).astype(BF16), v16, TN)
                sb[h] = _dot((k * jnp.exp(lg_ref[1, h] * pos)).astype(BF16), v16, TN)

        for d, (o_ref, h_ref, s) in enumerate(((of_ref, hf_ref, sf), (ob_ref, hb_ref, sb))):
            for h in range(RH):
                lg_d = lg_ref[d, h]
                mask, _, qd, _, kd, _ = _decays(lg_d, d == 0)
                q, k, v16 = _head_qkv(ins[d], h)
                a = _dot(q.astype(BF16), k.astype(BF16), NT)
                st = s[h]
                st16 = st.astype(BF16)
                h_ref[h] = st16
                o = _dot((a * mask).astype(BF16), v16) + _dot((q * qd).astype(BF16), st16)
                o_ref[:, h * DV:(h + 1) * DV] = o.astype(BF16)
                s[h] = st * jnp.exp(lg_d * CH) + _dot((k * kd).astype(BF16), v16, TN)

    def fw(b, c):
        return b * nc + c

    def bw(b, c):
        return b * nc + nc - 1 - c

    in_specs = [pl.BlockSpec(memory_space=pltpu.SMEM)] + _wide_specs(fw) + _wide_specs(bw) + _ctx_specs(t_rows, cx)
    out_specs = [pl.BlockSpec((CH, RH * DV), lambda b, c: (fw(b, c), 0)),
                 pl.BlockSpec((CH, RH * DV), lambda b, c: (bw(b, c), 0)),
                 pl.BlockSpec((None, None, RH, DK, DV), lambda b, c: (b, c, 0, 0, 0)),
                 pl.BlockSpec((None, None, RH, DK, DV), lambda b, c: (b, nc - 1 - c, 0, 0, 0))]
    return pl.pallas_call(
        body, name="ret_fwd", grid=(nb, nc), in_specs=in_specs, out_specs=out_specs,
        out_shape=[_sds((t_rows, RH * DV), BF16)] * 2 + [_sds((nb, nc, RH, DK, DV), BF16)] * 2,
        scratch_shapes=[pltpu.VMEM((RH, DK, DV), F32), pltpu.VMEM((RH, DK, DV), F32)],
        compiler_params=_params(("parallel", "arbitrary")),
    )(lg, *([px] * 13))


def _rope_tables(seq):
    rows = seq // GRID_W
    row = np.repeat(np.arange(rows, dtype=np.float32), GRID_W)
    col = np.tile(np.arange(GRID_W, dtype=np.float32), rows)
    half = HD // 2
    freqs = (ROPE_THETA ** (-np.arange(0, half, 2, dtype=np.float32) / half)).astype(np.float32)
    ang = np.concatenate([row[:, None] * freqs, col[:, None] * freqs], axis=-1).astype(np.float32)
    cos = np.repeat(np.cos(ang), 2, axis=-1).astype(np.float32)
    sin = np.repeat(np.sin(ang), 2, axis=-1).astype(np.float32)
    sign = np.tile(np.array([-1.0, 1.0], np.float32), HD // 2)
    return jnp.asarray(cos), jnp.asarray(sin * sign)


def _swap_pairs(v):
    lane = lax.broadcasted_iota(jnp.int32, v.shape, 1)
    return jnp.where((lane & 1) == 0, pltpu.roll(v, HD - 1, 1), pltpu.roll(v, 1, 1))


def _qk_prep(px, nw, cos, sin, rows, row_off, col_off, heads, hb, seq, tm, name):
    rope = cos is not None
    rb0 = row_off // tm
    pb = seq // tm if rope else 1
    bw = hb * HD

    def body(*refs):
        if rope:
            x_ref, w_ref, c_ref, s_ref, o_ref = refs
        else:
            x_ref, w_ref, o_ref = refs
        for h in range(hb):
            sl = slice(h * HD, (h + 1) * HD)
            xv = x_ref[:, sl].astype(F32)
            r = lax.rsqrt(jnp.mean(xv * xv, axis=-1, keepdims=True) + EPS)
            t = (xv * r) * w_ref[...]
            if rope:
                t = t * c_ref[...] + _swap_pairs(t) * s_ref[...]
            o_ref[:, sl] = t.astype(BF16)

    in_specs = [pl.BlockSpec((tm, bw), lambda i, j: (rb0 + i, col_off // bw + j)),
                pl.BlockSpec((1, HD), lambda i, j: (0, 0))]
    args = [px, nw]
    if rope:
        in_specs += [pl.BlockSpec((tm, HD), lambda i, j: (i % pb, 0))] * 2
        args += [cos, sin]
    return pl.pallas_call(
        body, name=name, grid=(rows // tm, heads // hb), in_specs=in_specs,
        out_specs=pl.BlockSpec((tm, bw), lambda i, j: (i, j)), out_shape=_sds((rows, heads * HD), BF16),
        compiler_params=_params(("parallel", "parallel")),
    )(*args)


def _att_fwd(q16, kx16, kc16, px, nb, seq, cx, tq):
    t_rows = nb * seq
    nq = seq // tq
    rep = HQ // HKV
    gw = rep * HD

    def body(q_ref, kx_ref, kc_ref, vx_ref, vc_ref, g_ref, o_ref, y_ref, l_ref):
        kx = kx_ref[...]
        kc = kc_ref[...]
        vx = vx_ref[...].astype(BF16)
        vc = vc_ref[...].astype(BF16)
        l_ref[...] = jnp.zeros_like(l_ref)
        for r in range(rep):
            sl = slice(r * HD, (r + 1) * HD)
            q = q_ref[:, sl]
            s1 = _dot(q, kx, NT)
            s2 = _dot(q, kc, NT)
            m = jnp.maximum(jnp.max(s1, axis=-1, keepdims=True), jnp.max(s2, axis=-1, keepdims=True))
            e1 = jnp.exp2((s1 - m) * SM_C)
            e2 = jnp.exp2((s2 - m) * SM_C)
            tot = jnp.sum(e1, axis=-1, keepdims=True) + jnp.sum(e2, axis=-1, keepdims=True)
            o = (_dot(e1.astype(BF16), vx) + _dot(e2.astype(BF16), vc)) * (1.0 / tot)
            o_ref[:, sl] = o
            y_ref[:, sl] = (o * _silu(g_ref[:, sl].astype(F32))).astype(BF16)
            l_ref[:, r:r + 1] = m * SM_C + jnp.log(tot) * float(np.log2(np.e))

    qblk = pl.BlockSpec((tq, gw), lambda b, g, i: (b * nq + i, g))
    return pl.pallas_call(
        body, name="att_fwd", grid=(nb, HKV, nq),
        in_specs=[qblk,
                  pl.BlockSpec((seq, HD), lambda b, g, i: (b, g)),
                  pl.BlockSpec((cx, HD), lambda b, g, i: (b, g)),
                  pl.BlockSpec((seq, HD), lambda b, g, i: (b, AV // HD + g)),
                  pl.BlockSpec((cx, HD), lambda b, g, i: (t_rows // cx + b, AV // HD + g)),
                  pl.BlockSpec((tq, gw), lambda b, g, i: (b * nq + i, AG // gw + g))],
        out_specs=[qblk, qblk, pl.BlockSpec((tq, 128), lambda b, g, i: (b * nq + i, g))],
        out_shape=[_sds((t_rows, D), F32), _sds((t_rows, D), BF16), _sds((t_rows, HKV * 128), F32)],
        compiler_params=_params(("parallel", "parallel", "parallel")),
    )(q16, kx16, kc16, px, px, px)


def _gate_specs(tm, col0):
    hw = D // 2
    return [pl.BlockSpec((tm, hw), lambda i: (i, col0 // hw)), pl.BlockSpec((tm, hw), lambda i: (i, col0 // hw + 1))]


def _merge_out(o_f, o_b, yatt16, px, w_o_ret16, w_o_att16, w_out16, x2, tgt, mod3, nb, seq, tm):
    t_rows = nb * seq
    bpb = seq // tm
    hw = D // 2

    def body(of_ref, ob_ref, g0, g1, g2, g3, wr_ref, ya_ref, wa_ref, mr0, mr1, ma0, ma1, wo_ref, x_ref, t_ref, gt_ref,
             yr_ref, ar_ref, aa_ref, dxn_ref, dout_ref, dg_ref, loss_ref, gw_ref, y_ref, acc_ref):
        i = pl.program_id(1)
        first = jnp.logical_and(pl.program_id(0) == 0, i == 0)
        for h, g_ref in enumerate((g0, g1, g2, g3)):
            sl = slice(h * DV, (h + 1) * DV)
            o = of_ref[:, sl].astype(F32) + ob_ref[:, sl].astype(F32)
            r = lax.rsqrt(jnp.mean(o * o, axis=-1, keepdims=True) + EPS)
            yr_ref[:, sl] = ((o * r) * _silu(g_ref[...].astype(F32))).astype(BF16)
        ar = _dot(yr_ref[...], wr_ref[...])
        aa = _dot(ya_ref[...], wa_ref[...])
        ar_ref[...] = ar.astype(BF16)
        aa_ref[...] = aa.astype(BF16)
        for j, (mr_ref, ma_ref) in enumerate(((mr0, ma0), (mr1, ma1))):
            sl = slice(j * hw, (j + 1) * hw)
            y_ref[:, sl] = (_sig(mr_ref[...].astype(F32)) * ar[:, sl]
                            + _sig(ma_ref[...].astype(F32)) * aa[:, sl]).astype(BF16)
        out = _dot(y_ref[...], wo_ref[...])
        gate = gt_ref[...]
        diff = x_ref[...] + gate * out - t_ref[...]
        dxn = diff * (1.0 / D)
        dxn_ref[...] = dxn
        dout_ref[...] = (gate * dxn).astype(BF16)

        @pl.when(first)
        def _():
            acc_ref[...] = jnp.zeros_like(acc_ref)

        for j in range(2):
            sl = slice(j * hw, (j + 1) * hw)
            acc_ref[sl, :] += _dot(y_ref[:, sl], dout_ref[...], TN)

        @pl.when(jnp.logical_and(pl.program_id(0) == nb - 1, i == bpb - 1))
        def _():
            gw_ref[...] = acc_ref[...].astype(BF16)

        dg = jnp.sum(dxn * out, axis=0, keepdims=True)
        ls = jnp.broadcast_to(jnp.sum(diff * diff) * (0.5 / D), (1, 128))

        @pl.when(i == 0)
        def _():
            dg_ref[...] = dg
            loss_ref[...] = ls

        @pl.when(i > 0)
        def _():
            dg_ref[...] += dg
            loss_ref[...] += ls

    def cols(width, col0):
        return pl.BlockSpec((tm, width), lambda b, i: (b * bpb + i, col0 // width))

    def whole(rows):
        return pl.BlockSpec((rows, D), lambda b, i: (0, 0))

    row, wide = cols(D, 0), cols(RH * DV, 0)
    gates = [cols(DV, RG + h * DV) for h in range(RH)]
    merge_gates = [cols(hw, MR), cols(hw, MR + hw), cols(hw, MA), cols(hw, MA + hw)]
    return pl.pallas_call(
        body, name="merge_out", grid=(nb, bpb),
        in_specs=[wide, wide] + gates + [whole(RH * DV), row, whole(D)] + merge_gates
        + [whole(D), row, row, pl.BlockSpec((None, 1, D), lambda b, i: (b, 0, 2))],
        out_specs=[wide, row, row, row, row, pl.BlockSpec((None, 1, D), lambda b, i: (b, 0, 0)),
                   pl.BlockSpec((None, 1, 128), lambda b, i: (b, 0, 0)), whole(D)],
        out_shape=[_sds((t_rows, RH * DV), BF16)] + [_sds((t_rows, D), BF16)] * 2
        + [_sds((t_rows, D), F32), _sds((t_rows, D), BF16), _sds((nb, 1, D), F32), _sds((nb, 1, 128), F32),
           _sds((D, D), BF16)],
        scratch_shapes=[pltpu.VMEM((tm, D), BF16), pltpu.VMEM((D, D), F32)],
        compiler_params=_params(("arbitrary", "arbitrary")),
    )(o_f, o_b, *([px] * RH), w_o_ret16, yatt16, w_o_att16, px, px, px, px, w_out16, x2, tgt, mod3)


def _bwd_branches(dout16, w_out16, w_o_ret16, w_o_att16, px, a_ret, a_att, o_f, o_b, o_att, yatt16, rows_all, tm):
    t_rows = dout16.shape[0]
    hw = D // 2

    def body(do_ref, wo_ref, wr_ref, wa_ref, mr0, mr1, ma0, ma1, ar_ref, aa_ref, rg0, rg1, rg2, rg3, of_ref, ob_ref,
             ag0, ag1, oa_ref, ya_ref, dar_ref, dor_ref, dao_ref, dl_ref, dp_ref, gwa_ref, daa_ref, acca_ref):
        dy_all = _dot(do_ref[...], wo_ref[...], NT)
        for j, (mr_ref, ma_ref) in enumerate(((mr0, ma0), (mr1, ma1))):
            sl = slice(j * hw, (j + 1) * hw)
            dy = dy_all[:, sl]
            sr = _sig(mr_ref[...].astype(F32))
            sa = _sig(ma_ref[...].astype(F32))
            dar_ref[:, sl] = (dy * sr).astype(BF16)
            daa_ref[:, sl] = (dy * sa).astype(BF16)
            dp_ref[:, MR - RG + j * hw:MR - RG + (j + 1) * hw] = (
                dy * ar_ref[:, sl].astype(F32) * sr * (1.0 - sr)).astype(BF16)
            dp_ref[:, MA - RG + j * hw:MA - RG + (j + 1) * hw] = (
                dy * aa_ref[:, sl].astype(F32) * sa * (1.0 - sa)).astype(BF16)
        da_ret = dar_ref[...]

        @pl.when(pl.program_id(0) == 0)
        def _():
            acca_ref[...] = jnp.zeros_like(acca_ref)

        for j in range(2):
            sl = slice(j * hw, (j + 1) * hw)
            acca_ref[sl, :] += _dot(ya_ref[:, sl], daa_ref[...], TN)

        for h, g_ref in enumerate((rg0, rg1, rg2, rg3)):
            sl = slice(h * DV, (h + 1) * DV)
            dy = _dot(da_ret, wr_ref[sl, :], NT)
            g = g_ref[...].astype(F32)
            o = of_ref[:, sl].astype(F32) + ob_ref[:, sl].astype(F32)
            r = lax.rsqrt(jnp.mean(o * o, axis=-1, keepdims=True) + EPS)
            on = o * r
            sg = _sig(g)
            don = dy * (g * sg)
            dp_ref[:, sl] = (dy * on * (sg * (1.0 + g * (1.0 - sg)))).astype(BF16)
            dor_ref[:, sl] = (r * (don - on * jnp.mean(on * don, axis=-1, keepdims=True))).astype(BF16)
        dy_all = _dot(daa_ref[...], wa_ref[...], NT)
        dl_ref[...] = jnp.zeros_like(dl_ref)
        for j, g_ref in enumerate((ag0, ag1)):
            sl = slice(j * hw, (j + 1) * hw)
            dy = dy_all[:, sl]
            g = g_ref[...].astype(F32)
            sg = _sig(g)
            dao = dy * (g * sg)
            dao_ref[:, sl] = dao.astype(BF16)
            prod = dao * oa_ref[:, sl]
            for r in range(hw // HD):
                dl_ref[:, j * 128 + r:j * 128 + r + 1] = jnp.sum(prod[:, r * HD:(r + 1) * HD], axis=-1, keepdims=True)
            dp_ref[:, AG - RG + j * hw:AG - RG + (j + 1) * hw] = (
                dy * oa_ref[:, sl] * (sg * (1.0 + g * (1.0 - sg)))).astype(BF16)

        @pl.when(pl.program_id(0) == t_rows // tm - 1)
        def _():
            gwa_ref[...] = acca_ref[...].astype(BF16)

    def gate(h):
        return pl.BlockSpec((tm, DV), lambda i: (i, RG // DV + h))

    def whole(rows):
        return pl.BlockSpec((rows, D), lambda i: (0, 0), pipeline_mode=pl.Buffered(1))

    row = pl.BlockSpec((tm, D), lambda i: (i, 0))
    wide = pl.BlockSpec((tm, RH * DV), lambda i: (i, 0))
    return pl.pallas_call(
        body, name="bwd_branches", grid=(t_rows // tm,),
        in_specs=[row, whole(D), whole(RH * DV), whole(D)] + _gate_specs(tm, MR) + _gate_specs(tm, MA) + [row, row]
        + [gate(h) for h in range(RH)] + [wide, wide] + _gate_specs(tm, AG) + [row, row],
        out_specs=[row, wide, row, pl.BlockSpec((tm, HKV * 128), lambda i: (i, 0)),
                   pl.BlockSpec((pl.Element(tm), pl.Element(IN_COLS - RG)), lambda i: (i * tm, RG)), whole(D)],
        out_shape=[_sds((t_rows, D), BF16), _sds((t_rows, RH * DV), BF16), _sds((t_rows, D), BF16),
                   _sds((t_rows, HKV * 128), F32), _sds((rows_all, IN_COLS), BF16), _sds((D, D), BF16)],
        scratch_shapes=[pltpu.VMEM((tm, D), BF16), pltpu.VMEM((D, D), F32)],
        compiler_params=_params(("arbitrary",)),
    )(dout16, w_out16, w_o_ret16, w_o_att16, px, px, px, px, a_ret, a_att, *([px] * RH), o_f, o_b, px, px, o_att,
      yatt16)


def _att_bwd(q16, kx16, kc16, px, dao16, delta, lse, q_norm_w, k_norm_w, cos, sin, dp_all, nb, seq, cx, tq, after=()):
    t_rows = nb * seq
    nq = seq // tq
    rep = HQ // HKV
    gw = rep * HD
    scale = HD ** -0.5

    def body(q_ref, kx_ref, kc_ref, vx_ref, vc_ref, dao_ref, dl_ref, l_ref, xq_ref, w_ref, c_ref, s_ref,
             xk_ref, xkc_ref, wk_ref, ck_ref, sk_ref, *rest):
        daq_ref, gq_ref, gk_ref, dkx_ref, dvx_ref, dkc_ref, dvc_ref = rest[1 + len(after):8 + len(after)]
        accs = rest[8 + len(after):]
        i = pl.program_id(2)
        head0 = jnp.logical_and(pl.program_id(0) == 0, pl.program_id(1) == 0)
        first = jnp.logical_and(head0, i == 0)
        gq = jnp.zeros((1, HD), F32)
        kx = kx_ref[...]
        kc = kc_ref[...]
        vx = vx_ref[...].astype(BF16)
        vc = vc_ref[...].astype(BF16)
        @pl.when(i == 0)
        def _():
            for acc in accs:
                acc[...] = jnp.zeros_like(acc)

        dkx, dvx, dkc, dvc = [acc[...] for acc in accs]
        for r in range(rep):
            sl = slice(r * HD, (r + 1) * HD)
            q = q_ref[:, sl]
            lr = l_ref[:, r:r + 1]
            p1 = jnp.exp2(_dot(q, kx, NT) * SM_C - lr)
            p2 = jnp.exp2(_dot(q, kc, NT) * SM_C - lr)
            da16 = dao_ref[:, sl]
            delta = dl_ref[:, r:r + 1]
            ds1 = (p1 * (_dot(da16, vx, NT) - delta)).astype(BF16)
            ds2 = (p2 * (_dot(da16, vc, NT) - delta)).astype(BF16)
            dq = (_dot(ds1, kx) + _dot(ds2, kc)) * scale
            dkx += _dot(q, ds1, TN)
            dkc += _dot(q, ds2, TN)
            dvx += _dot(da16, p1.astype(BF16), TN)
            dvc += _dot(da16, p2.astype(BF16), TN)
            dt = dq * c_ref[...] + _swap_pairs(dq * s_ref[...])
            xv = xq_ref[:, sl].astype(F32)
            rn = lax.rsqrt(jnp.mean(xv * xv, axis=-1, keepdims=True) + EPS)
            xh = xv * rn
            dxh = dt * w_ref[...]
            daq_ref[:, sl] = (rn * (dxh - xh * jnp.mean(dxh * xh, axis=-1, keepdims=True))).astype(BF16)
            gq += jnp.sum(dt * xh, axis=0, keepdims=True)
        for acc, val in zip(accs, (dkx, dvx, dkc, dvc)):
            acc[...] = val

        @pl.when(first)
        def _():
            gq_ref[...] = gq

        @pl.when(jnp.logical_not(first))
        def _():
            gq_ref[...] += gq

        def k_back(dk, x_ref, dk_ref):
            xv = x_ref[...].astype(F32)
            rn = lax.rsqrt(jnp.mean(xv * xv, axis=-1, keepdims=True) + EPS)
            xh = xv * rn
            dxh = dk * wk_ref[...]
            dk_ref[...] = (rn * (dxh - xh * jnp.mean(dxh * xh, axis=-1, keepdims=True))).astype(BF16)
            return jnp.sum(dk * xh, axis=0, keepdims=True)

        @pl.when(i == nq - 1)
        def _():
            dvx_ref[...] = dvx.T.astype(BF16)
            dvc_ref[...] = dvc.T.astype(BF16)
            dk = dkx.T * scale
            gk = (k_back(dk * ck_ref[...] + _swap_pairs(dk * sk_ref[...]), xk_ref, dkx_ref)
                  + k_back(dkc.T * scale, xkc_ref, dkc_ref))

            @pl.when(head0)
            def _():
                gk_ref[...] = gk

            @pl.when(jnp.logical_not(head0))
            def _():
                gk_ref[...] += gk

    qblk = pl.BlockSpec((tq, gw), lambda b, g, i: (b * nq + i, g))
    kxb = pl.BlockSpec((seq, HD), lambda b, g, i: (b, g))
    kcb = pl.BlockSpec((cx, HD), lambda b, g, i: (b, g))
    table = pl.BlockSpec((tq, HD), lambda b, g, i: (i, 0))
    tables = pl.BlockSpec((seq, HD), lambda b, g, i: (0, 0))
    one = pl.BlockSpec((1, HD), lambda b, g, i: (0, 0))
    lane = pl.BlockSpec((tq, 128), lambda b, g, i: (b * nq + i, g))
    return pl.pallas_call(
        body, name="att_bwd", grid=(nb, HKV, nq),
        in_specs=[qblk,
                  pl.BlockSpec((seq, HD), lambda b, g, i: (b, g)),
                  pl.BlockSpec((cx, HD), lambda b, g, i: (b, g)),
                  pl.BlockSpec((seq, HD), lambda b, g, i: (b, AV // HD + g)),
                  pl.BlockSpec((cx, HD), lambda b, g, i: (t_rows // cx + b, AV // HD + g)),
                  qblk, lane, lane,
                  pl.BlockSpec((tq, gw), lambda b, g, i: (b * nq + i, AQ // gw + g)), one, table, table,
                  pl.BlockSpec((seq, HD), lambda b, g, i: (b, AK // HD + g)),
                  pl.BlockSpec((cx, HD), lambda b, g, i: (t_rows // cx + b, AK // HD + g)), one, tables, tables]
        + [pl.BlockSpec(memory_space=pl.ANY)] * (1 + len(after)),
        out_specs=[pl.BlockSpec((tq, gw), lambda b, g, i: (b * nq + i, AQ // gw + g)), one, one, kxb, kxb, kcb, kcb],
        out_shape=[_sds(dp_all.shape, BF16), _sds((1, HD), F32), _sds((1, HD), F32), _sds((t_rows, HKV * HD), BF16),
                   _sds((t_rows, HKV * HD), BF16), _sds((nb * cx, HKV * HD), BF16), _sds((nb * cx, HKV * HD), BF16)],
        scratch_shapes=[pltpu.VMEM((HD, seq), F32), pltpu.VMEM((HD, seq), F32), pltpu.VMEM((HD, cx), F32),
                        pltpu.VMEM((HD, cx), F32)],
        input_output_aliases={17: 0},
        compiler_params=_params(("arbitrary", "arbitrary", "arbitrary")),
    )(q16, kx16, kc16, px, px, dao16, delta, lse, px, q_norm_w, cos, sin, px, px, k_norm_w, cos, sin, dp_all, *after)


def _ret_bwd(px, lg, do16, hist_f, hist_b, nb, nc, cx):
    t_rows = nb * nc * CH

    def body(lg_ref, *refs):
        ins = (refs[0:5], refs[7:12])
        do_refs = (refs[5], refs[12])
        h_refs = (refs[6], refs[13])
        ctx_refs = refs[14:17]
        outs = (refs[17:20], refs[20:23])
        dck_ref, dcv_ref, dlg_ref = refs[23:26]
        dss = (refs[26], refs[27])
        c = pl.program_id(1)

        @pl.when(c == 0)
        def _():
            dss[0][...] = jnp.zeros_like(dss[0])
            dss[1][...] = jnp.zeros_like(dss[1])
            dlg_ref[...] = jnp.zeros_like(dlg_ref)

        for d in range(2):
            dq_ref, dk_ref, dv_ref = outs[d]
            for h in range(RH):
                lg_d = lg_ref[d, h]
                mask, relf, qd, qe, kd, ke = _decays(lg_d, d == 0)
                g_ch = jnp.exp(lg_d * CH)
                q, k, v16 = _head_qkv(ins[d], h)
                q16 = q.astype(BF16)
                k16 = k.astype(BF16)
                do16v = do_refs[d][:, h * DV:(h + 1) * DV]
                st16 = h_refs[d][h]
                dst = dss[d][h]
                dst16 = dst.astype(BF16)
                a = _dot(q16, k16, NT) * mask
                dp = _dot(do16v, v16, NT)
                da16 = (dp * mask).astype(BF16)
                dq_cross = _dot(do16v, st16, NT) * qd
                dq_ref[:, h * DK:(h + 1) * DK] = (_dot(da16, k16) + dq_cross).astype(BF16)
                dk_state = _dot(v16, dst16, NT) * kd
                dk_ref[:, h * DK:(h + 1) * DK] = ((_dot(da16, q16, TN) + dk_state) * (DK ** -0.5)).astype(BF16)
                dv = _dot(a.astype(BF16), do16v, TN) + _dot((k * kd).astype(BF16), dst16)
                dv_ref[:, h * DV:(h + 1) * DV] = dv.astype(BF16)
                dlg = (jnp.sum(relf * a * dp)
                       + jnp.sum(qe * jnp.sum(q * dq_cross, axis=-1, keepdims=True))
                       + jnp.sum(ke * jnp.sum(k * dk_state, axis=-1, keepdims=True))
                       + CH * g_ch * jnp.sum(dst * st16.astype(F32)))
                row = d * RH + h
                dlg_ref[row:row + 1, :] += jnp.broadcast_to(dlg, (1, 128))
                dss[d][h] = g_ch * dst + _dot((q * qd).astype(BF16), do16v, TN)

        @pl.when(c == nc - 1)
        def _():
            pos = lax.broadcasted_iota(jnp.int32, (cx, 1), 0).astype(F32)
            for h in range(RH):
                k, v16 = _ctx_kv(ctx_refs, h)
                dk = jnp.zeros((cx, DK), F32)
                dv = jnp.zeros((cx, DV), F32)
                for d, e in enumerate((cx - 1.0 - pos, pos)):
                    w = jnp.exp(lg_ref[d, h] * e)
                    ds16 = dss[d][h].astype(BF16)
                    t = _dot(v16, ds16, NT)
                    dk += t * w
                    dv += _dot((k * w).astype(BF16), ds16)
                    dlg = jnp.sum(e * w * jnp.sum(k * t, axis=-1, keepdims=True))
                    row = d * RH + h
                    dlg_ref[row:row + 1, :] += jnp.broadcast_to(dlg, (1, 128))
                dck_ref[:, h * DK:(h + 1) * DK] = (dk * (DK ** -0.5)).astype(BF16)
                dcv_ref[:, h * DV:(h + 1) * DV] = dv.astype(BF16)

    def fw(b, c):
        return b * nc + nc - 1 - c

    def bw(b, c):
        return b * nc + c

    def rows(rowf, width):
        return pl.BlockSpec((CH, width), lambda b, c: (rowf(b, c), 0))

    def hist(rowf):
        return pl.BlockSpec((None, None, RH, DK, DV), lambda b, c: (b, rowf(0, c), 0, 0, 0))

    in_specs = [pl.BlockSpec(memory_space=pltpu.SMEM)]
    out_specs = []
    for rowf in (fw, bw):
        in_specs += _wide_specs(rowf) + [rows(rowf, RH * DV), hist(rowf)]
        out_specs += [rows(rowf, RH * DK), rows(rowf, RH * DK), rows(rowf, RH * DV)]
    in_specs += _ctx_specs(t_rows, cx)
    out_specs += [pl.BlockSpec((cx, RH * DK), lambda b, c: (b, 0)), pl.BlockSpec((cx, RH * DV), lambda b, c: (b, 0)),
                  pl.BlockSpec((None, 8, 128), lambda b, c: (b, 0, 0))]
    qk = _sds((t_rows, RH * DK), BF16)
    vv = _sds((t_rows, RH * DV), BF16)
    return pl.pallas_call(
        body, name="ret_bwd", grid=(nb, nc), in_specs=in_specs, out_specs=out_specs,
        out_shape=[qk, qk, vv, qk, qk, vv, _sds((nb * cx, RH * DK), BF16), _sds((nb * cx, RH * DV), BF16),
                   _sds((nb, 8, 128), F32)],
        scratch_shapes=[pltpu.VMEM((RH, DK, DV), F32), pltpu.VMEM((RH, DK, DV), F32)],
        compiler_params=_params(("parallel", "arbitrary")),
    )(lg, *([px] * 5), do16, hist_f, *([px] * 5), do16, hist_b, *([px] * 3))


def _assemble_lat(dp_all, dk_f, dk_b, dv_f, dv_b, dak16, dvx, dq_f, dq_b, tm):
    t_rows = dk_f.shape[0]

    def body(_, dkf, dkb, dvf, dvb, dak, dav, dqf, dqb, o_ref):
        o_ref[:, RK:RK + RH * DK] = (dkf[...].astype(F32) + dkb[...].astype(F32)).astype(BF16)
        o_ref[:, RV:RV + RH * DV] = (dvf[...].astype(F32) + dvb[...].astype(F32)).astype(BF16)
        o_ref[:, AK:AK + HKV * HD] = dak[...]
        o_ref[:, AV:AV + HKV * HD] = dav[...]
        o_ref[:, RQ:RQ + RH * DK] = (dqf[...].astype(F32) + dqb[...].astype(F32)).astype(BF16)

    args = (dk_f, dk_b, dv_f, dv_b, dak16, dvx, dq_f, dq_b)
    return pl.pallas_call(
        body, name="assemble_lat", grid=(t_rows // tm,),
        in_specs=[pl.BlockSpec(memory_space=pl.ANY)]
        + [pl.BlockSpec((tm, a.shape[1]), lambda i: (i, 0)) for a in args],
        out_specs=pl.BlockSpec((tm, RG), lambda i: (i, 0)), out_shape=_sds(dp_all.shape, BF16),
        input_output_aliases={0: 0},
        compiler_params=_params(("parallel",)),
    )(dp_all, *args)


def _assemble_ctx(dp_all, dck16, dcv16, dcak16, dvc, t_rows, tm):
    c_rows = dck16.shape[0]
    rb = t_rows // tm

    def body(_, dck, dcv, dcak, dcav, o_ref):
        o_ref[:, RK:RK + RH * DK] = dck[...]
        o_ref[:, RV:RV + RH * DV] = dcv[...]
        o_ref[:, AK:AK + HKV * HD] = dcak[...]
        o_ref[:, AV:AV + HKV * HD] = dcav[...]
        o_ref[:, KV_COLS:] = jnp.zeros((tm, IN_COLS - KV_COLS), BF16)

    args = (dck16, dcv16, dcak16, dvc)
    return pl.pallas_call(
        body, name="assemble_ctx", grid=(c_rows // tm,),
        in_specs=[pl.BlockSpec(memory_space=pl.ANY)]
        + [pl.BlockSpec((tm, a.shape[1]), lambda i: (i, 0)) for a in args],
        out_specs=pl.BlockSpec((tm, IN_COLS), lambda i: (rb + i, 0)), out_shape=_sds(dp_all.shape, BF16),
        input_output_aliases={0: 0},
        compiler_params=_params(("parallel",)),
    )(dp_all, *args)


def _norm_bwd(dh, x2, mod3, norm_w, dxn, row_off, rows_per_group, group0, tm, name):
    with_dx = dxn is not None
    rows = x2.shape[0]
    rb0 = row_off // tm
    bpg = rows_per_group // tm
    ngroups = rows // rows_per_group

    def body(*refs):
        if with_dx:
            dh_ref, x_ref, sc_ref, nw_ref, dxn_ref, dx_ref, dsh_ref, dsc_ref, dnw_ref = refs
        else:
            dh_ref, x_ref, sc_ref, nw_ref, dsh_ref, dsc_ref, dnw_ref = refs
        i = pl.program_id(0)
        dhv = dh_ref[...]
        xv = x_ref[...]
        nw = nw_ref[...]
        r = lax.rsqrt(jnp.mean(xv * xv, axis=-1, keepdims=True) + EPS)
        xh = xv * r
        dm = dhv * (1.0 + sc_ref[...])
        dsh = jnp.sum(dhv, axis=0, keepdims=True)
        dsc = jnp.sum(dhv * (xh * nw), axis=0, keepdims=True)
        dnw = jnp.sum(dm * xh, axis=0, keepdims=True)
        if with_dx:
            dxh = dm * nw
            dx_ref[...] = dxn_ref[...] + r * (dxh - xh * jnp.mean(dxh * xh, axis=-1, keepdims=True))

        @pl.when(i % bpg == 0)
        def _():
            dsh_ref[...] = dsh
            dsc_ref[...] = dsc

        @pl.when(i % bpg != 0)
        def _():
            dsh_ref[...] += dsh
            dsc_ref[...] += dsc

        @pl.when(i == 0)
        def _():
            dnw_ref[...] = dnw

        @pl.when(i > 0)
        def _():
            dnw_ref[...] += dnw

    grp = pl.BlockSpec((None, 1, D), lambda i: (i // bpg, 0, 0))
    in_specs = [pl.BlockSpec((tm, D), lambda i: (rb0 + i, 0)), pl.BlockSpec((tm, D), lambda i: (i, 0)),
                pl.BlockSpec((None, 1, D), lambda i: (group0 + i // bpg, 0, 1)),
                pl.BlockSpec((1, D), lambda i: (0, 0))]
    args = [dh, x2, mod3, norm_w]
    out_specs = [grp, grp, pl.BlockSpec((1, D), lambda i: (0, 0))]
    out_shape = [_sds((ngroups, 1, D), F32), _sds((ngroups, 1, D), F32), _sds((1, D), F32)]
    if with_dx:
        in_specs.append(pl.BlockSpec((tm, D), lambda i: (i, 0)))
        args.append(dxn)
        out_specs.insert(0, pl.BlockSpec((tm, D), lambda i: (i, 0)))
        out_shape.insert(0, _sds((rows, D), F32))
    return pl.pallas_call(
        body, name=name, grid=(rows // tm,), in_specs=in_specs, out_specs=out_specs, out_shape=out_shape,
        compiler_params=_params(("arbitrary",)),
    )(*args)


def _small_final(dmod_all, dmodc_parts, c_rows, dm_loc_rows, nw_parts, misc_parts, c_ctx, r_pad, w_ada16):
    loc = dm_loc_rows.shape[1]

    def body(dm_ref, dmc_ref, c_ref, dml_ref, nwp_ref, mp_ref, cc_ref, r_ref, w_ref,
             gb_ref, gc_ref, gnw_ref, misc_ref, gwa_ref):
        dmc = jnp.sum(dmc_ref[...], axis=0, keepdims=True)
        gb_ref[...] = jnp.sum(dm_ref[...], axis=0, keepdims=True) + dmc
        dsc = _dot(jnp.broadcast_to(dmc, (8, 3 * D)).astype(BF16), w_ref[...], NT)[0:1, :]
        gc_ref[...] = dsc * _dsilu(cc_ref[...])
        gnw_ref[...] = jnp.sum(nwp_ref[...], axis=0, keepdims=True)
        misc = jnp.sum(mp_ref[...], axis=0, keepdims=True)
        y = jnp.exp2(r_ref[...])
        lane = lax.broadcasted_iota(jnp.int32, (1, D), 1)
        is_decay = jnp.logical_and(lane >= 2 * HD, lane < 2 * HD + 2 * RH)
        misc_ref[...] = misc * jnp.where(is_decay, -(y * np.float32(np.log(2.0))) / (1.0 - y), 1.0)
        gwa_ref[...] = _dot(_silu(c_ref[...]).astype(BF16), dml_ref[...].astype(BF16), TN)

    return pl.pallas_call(
        body, name="small_final",
        out_shape=[_sds((1, 3 * D), F32), _sds((1, D), F32), _sds((1, D), F32), _sds((1, D), F32), _sds((D, loc), F32)],
        compiler_params=pltpu.CompilerParams(vmem_limit_bytes=VMEM_LIMIT),
    )(dmod_all, dmodc_parts, c_rows, dm_loc_rows, nw_parts, misc_parts, c_ctx, r_pad, w_ada16)


def _adamw_math(w, g, m, v):
    nm = B1 * m + (1.0 - B1) * g
    nv = B2 * v + (1.0 - B2) * (g * g)
    return -LR * ((nm / (1.0 - B1 ** STEP)) / (jnp.sqrt(nv / (1.0 - B2 ** STEP)) + ADAM_EPS) + WD * w), nm, nv


def _adamw(w, g, m, v, name):
    rows, cols = w.shape
    tm = _pick(rows, 448, 8)

    def body(w_ref, g_ref, m_ref, v_ref, d_ref, nm_ref, nv_ref):
        d_ref[...], nm_ref[...], nv_ref[...] = _adamw_math(w_ref[...], g_ref[...], m_ref[...], v_ref[...])

    blk = pl.BlockSpec((tm, cols), lambda i: (i, 0))
    return pl.pallas_call(
        body, name=name, grid=(rows // tm,), in_specs=[blk] * 4, out_specs=[blk] * 3,
        out_shape=[_sds((rows, cols), F32)] * 3, compiler_params=_params(("parallel",)),
    )(w, g, m, v)


def _adamw_small(wgmv):
    n = len(wgmv)

    def body(*refs):
        ins, outs = refs[:4 * n], refs[4 * n:]
        for k in range(n):
            w, g, m, v = [r[...] for r in ins[4 * k:4 * k + 4]]
            outs[3 * k][...], outs[3 * k + 1][...], outs[3 * k + 2][...] = _adamw_math(w, g, m, v)

    out = pl.pallas_call(
        body, name="adamw_small", out_shape=[_sds(t[0].shape, F32) for t in wgmv for _ in range(3)],
    )(*[a for t in wgmv for a in t])
    return [out[3 * k:3 * k + 3] for k in range(n)]


def _mesh_pos():
    return lax.axis_index("x"), lax.axis_index("y"), lax.axis_index("c")


def _all_gather(arrs, name):
    n = len(arrs)

    def body(*refs):
        ins, outs = refs[:n], refs[n:2 * n]
        send_sems, recv_sems, local_sems = refs[2 * n:]
        x, y, c = _mesh_pos()
        me, sib = (x, y, c), (x, y, 1 - c)
        chips = [(1 - x, y), (x, 1 - y), (1 - x, 1 - y)]

        def slot(p):
            return 4 * p[0] + 2 * p[1] + p[2]

        def copy(a, k, block, to, own):
            dst = outs[a].at[slot(block)]
            return pltpu.make_async_remote_copy(
                src_ref=ins[a] if own else dst, dst_ref=dst, send_sem=send_sems.at[a, k], recv_sem=recv_sems.at[a, k],
                device_id=to, device_id_type=MESH_T)

        mine = [pltpu.make_async_copy(ins[a], outs[a].at[slot(me)], local_sems.at[a]) for a in range(n)]
        for cp in mine:
            cp.start()
        first = []
        for a in range(n):
            first.append(copy(a, 0, me, sib, True))
            first += [copy(a, 1 + j, me, (*chip, c), True) for j, chip in enumerate(chips)]
        for cp in first:
            cp.start()
        passed = []
        for j, chip in enumerate(chips):
            for a in range(n):
                copy(a, 1 + j, (*chip, c), me, False).wait_recv()
                fwd = copy(a, 4 + j, (*chip, c), sib, False)
                fwd.start()
                passed.append(fwd)
        for a in range(n):
            copy(a, 0, sib, me, False).wait_recv()
            for j, chip in enumerate(chips):
                copy(a, 4 + j, (*chip, 1 - c), me, False).wait_recv()
        for cp in first + passed:
            cp.wait_send()
        for cp in mine:
            cp.wait()

    hbm = pl.BlockSpec(memory_space=pl.ANY)
    return pl.pallas_call(
        body, name=name, in_specs=[hbm] * n, out_specs=[hbm] * n,
        out_shape=[_sds((N_DEV,) + a.shape, a.dtype) for a in arrs],
        scratch_shapes=[pltpu.SemaphoreType.DMA((n, 7)), pltpu.SemaphoreType.DMA((n, 7)), pltpu.SemaphoreType.DMA((n,))],
    )(*arrs)


def _pair_add(parts, gots, core, name):
    n = len(parts)
    cols = parts[0].shape[2]
    tiles = min(p.shape[1] for p in parts) // _pick(min(p.shape[1] for p in parts), 672, 16)

    def body(core_ref, *refs):
        for p_ref, g_ref, o_ref in zip(refs[:n], refs[n:2 * n], refs[2 * n:]):
            o_ref[...] = (p_ref[...].astype(F32) + g_ref[...].astype(F32)).astype(BF16)

    def blk(p):
        return pl.BlockSpec((None, p.shape[1] // tiles, cols), lambda k, i, cr: (k, i, 0))

    return pl.pallas_call(
        body, name=name,
        grid_spec=pltpu.PrefetchScalarGridSpec(
            num_scalar_prefetch=1, grid=(4, tiles),
            in_specs=[pl.BlockSpec((None, None, p.shape[1] // tiles, cols), lambda k, i, cr: (k, cr[0], i, 0))
                      for p in parts] + [blk(p) for p in parts],
            out_specs=[blk(p) for p in parts]),
        out_shape=[_sds((4,) + p.shape[1:], BF16) for p in parts], compiler_params=_params(("parallel", "parallel")),
    )(core, *[p.reshape(4, 2, *p.shape[1:]) for p in parts], *gots)


def _chip_sum_adamw(pair_sums, landed, chip, wmv, name):
    n = len(pair_sums)
    cols = pair_sums[0].shape[2]
    fewest = min(s_.shape[1] for s_ in pair_sums)
    tiles = fewest // _pick(fewest, 448, 16)

    def body(chip_ref, *refs):
        for k in range(n):
            s_ref, l_ref, w_ref, m_ref, v_ref = refs[5 * k:5 * k + 5]
            g_ref, d_ref, nm_ref, nv_ref = refs[5 * n + 4 * k:5 * n + 4 * k + 4]
            acc = s_ref[...].astype(F32)
            for j in range(3):
                acc = acc + l_ref[j].astype(F32)
            g_ref[...] = acc
            d_ref[...], nm_ref[...], nv_ref[...] = _adamw_math(w_ref[...], acc, m_ref[...], v_ref[...])

    in_specs, out_specs, out_shape, args = [], [], [], []
    for s_, l_, t in zip(pair_sums, landed, wmv):
        tm = s_.shape[1] // tiles
        blk = pl.BlockSpec((tm, cols), lambda i, ch: (i, 0))
        in_specs += [pl.BlockSpec((None, tm, cols), lambda i, ch: (ch[0], i, 0)),
                     pl.BlockSpec((3, tm, cols), lambda i, ch: (0, i, 0)), blk, blk, blk]
        out_specs += [blk] * 4
        out_shape += [_sds(s_.shape[1:], F32)] * 4
        args += [s_, l_, *t]
    out = pl.pallas_call(
        body, name=name,
        grid_spec=pltpu.PrefetchScalarGridSpec(num_scalar_prefetch=1, grid=(tiles,), in_specs=in_specs,
                                               out_specs=out_specs),
        out_shape=out_shape, compiler_params=_params(("parallel",)),
    )(chip, *args)
    return [out[4 * k:4 * k + 4] for k in range(n)]


_HBM = pl.BlockSpec(memory_space=pltpu.HBM)
_SEM = pl.BlockSpec(memory_space=pltpu.SEMAPHORE)
_EFFECT = pltpu.SideEffectType.DATAFLOW_SIDE_EFFECTING


def _chip_routes(n):
    def plan(x, y, c):
        routes = []
        for a in range(n):
            for j in range(1, 4):
                px, py = x ^ (j >> 1), y ^ (j & 1)
                routes.append((a, 2 * px + py, (px, py, c), j - 1))
        return routes
    return plan, 3 * n


def _pair_routes(n):
    def plan(x, y, c):
        return [(a, 2 * k + 1 - c, (x, y, 1 - c), k) for a in range(n) for k in range(4)]
    return plan, 4 * n


def _bcast_routes(n):
    def plan(x, y, c):
        routes = []
        for a in range(n):
            for k in range(1, N_DEV):
                peer = (x ^ ((k >> 2) & 1), y ^ ((k >> 1) & 1), c ^ (k & 1))
                routes.append((a, 0, peer, 4 * x + 2 * y + c))
        return routes
    return plan, 7 * n


def _route_copies(srcs, lands, send_sems, recv_sems, routes):
    return [pltpu.make_async_remote_copy(
        src_ref=srcs[a].at[sb], dst_ref=lands[a].at[lb], send_sem=send_sems.at[r], recv_sem=recv_sems.at[r],
        device_id=peer, device_id_type=MESH_T) for r, (a, sb, peer, lb) in enumerate(routes)]


def _exchange_start(srcs, lands, routes, name, after=()):
    plan, count = routes
    n = len(srcs)
    n_in = 2 * n + len(after)

    def body(*refs):
        send_sems, recv_sems = refs[n_in], refs[n_in + 1]
        token = refs[-1]
        for cp in _route_copies(refs[:n], refs[n:2 * n], send_sems, recv_sems, plan(*_mesh_pos())):
            cp.start()
        token[...] = jnp.zeros_like(token)

    args = [pltpu.with_memory_space_constraint(a, pltpu.HBM) for a in list(srcs) + list(lands)]
    out = pl.pallas_call(
        body, name=name,
        out_shape=(pltpu.SemaphoreType.DMA((count,)), pltpu.SemaphoreType.DMA((count,)),
                   *[pltpu.HBM(a.shape, a.dtype) for a in args], _sds((8, 128), F32)),
        in_specs=[_HBM] * (2 * n) + [pl.BlockSpec(memory_space=pl.ANY)] * len(after),
        out_specs=(_SEM, _SEM, *([_HBM] * (2 * n)), pl.BlockSpec(memory_space=pltpu.VMEM)),
        input_output_aliases={i: 2 + i for i in range(2 * n)},
        compiler_params=pltpu.CompilerParams(has_side_effects=_EFFECT),
    )(*args, *after)
    return (out[0], out[1], list(out[2:2 + 2 * n]), routes), out[-1]


def _exchange_wait(state, after, name):
    send_sems, recv_sems, bufs, (plan, count) = state
    n = len(bufs) // 2

    def body(*refs):
        send_s, recv_s = refs[2 * n], refs[2 * n + 1]
        for cp in _route_copies(refs[:n], refs[n:2 * n], send_s, recv_s, plan(*_mesh_pos())):
            cp.wait_send()
            cp.wait_recv()

    out = pl.pallas_call(
        body, name=name, out_shape=tuple(pltpu.HBM(a.shape, a.dtype) for a in bufs),
        in_specs=[_HBM] * (2 * n) + [_SEM, _SEM, pl.BlockSpec(memory_space=pl.ANY)], out_specs=tuple([_HBM] * (2 * n)),
        input_output_aliases={i: i for i in range(2 * n)},
        compiler_params=pltpu.CompilerParams(has_side_effects=_EFFECT),
    )(*bufs, send_sems, recv_sems, after)
    return list(out[:n]), list(out[n:])


def _group_routes(js):
    def plan(x, y, c):
        return [(0, 0, (x ^ (j >> 1), y ^ (j & 1), c), 2 * j + c) for j in js]
    return plan, len(js)


def _pair_fill(groups, js, name, after=()):
    def body(*refs):
        g_ref, send_sems, recv_sems = refs[-3:]
        x, y, c = _mesh_pos()
        sends = []
        for n, j in enumerate(js):
            mine = g_ref.at[2 * j + c]
            sends.append(pltpu.make_async_remote_copy(
                src_ref=mine, dst_ref=mine, send_sem=send_sems.at[n], recv_sem=recv_sems.at[n],
                device_id=(x, y, 1 - c), device_id_type=MESH_T))
        for cp in sends:
            cp.start()
        for n, j in enumerate(js):
            pltpu.make_async_remote_copy(
                src_ref=g_ref.at[2 * j + c], dst_ref=g_ref.at[2 * j + 1 - c], send_sem=send_sems.at[n],
                recv_sem=recv_sems.at[n], device_id=(x, y, 1 - c), device_id_type=MESH_T).wait_recv()
        for cp in sends:
            cp.wait_send()

    hbm = pl.BlockSpec(memory_space=pl.ANY)
    return pl.pallas_call(
        body, name=name, in_specs=[hbm] * (1 + len(after)), out_specs=hbm, out_shape=_sds(groups.shape, groups.dtype),
        input_output_aliases={0: 0},
        scratch_shapes=[pltpu.SemaphoreType.DMA((len(js),)), pltpu.SemaphoreType.DMA((len(js),))],
    )(groups, *after)


def _in_proj_group(h_all, groups, j0, ng, chip, px_prev, after, name):
    rows_all = h_all.shape[0]
    gcols = IN_COLS // 4
    tm = _pick(rows_all, 1536, 128)
    g4 = groups.reshape(4, gcols, D)

    n_lead = (1 if px_prev is not None else 0) + len(after)
    lead = ([px_prev] if px_prev is not None else []) + list(after)

    def body(chip_ref, *refs):
        h_ref, w_ref, o_ref = refs[n_lead:]
        o_ref[...] = _dot(h_ref[...], w_ref[...], NT).astype(BF16)

    return pl.pallas_call(
        body, name=name,
        grid_spec=pltpu.PrefetchScalarGridSpec(
            num_scalar_prefetch=1, grid=(ng, rows_all // tm),
            in_specs=[pl.BlockSpec(memory_space=pl.ANY)] * n_lead
            + [pl.BlockSpec((tm, D), lambda n, i, ch: (i, 0)),
               pl.BlockSpec((None, gcols, D), lambda n, i, ch: (j0 + n, 0, 0))],
            out_specs=pl.BlockSpec((tm, gcols), lambda n, i, ch: (i, ch[0] ^ (j0 + n)))),
        out_shape=_sds((rows_all, IN_COLS), BF16),
        input_output_aliases={1: 0} if px_prev is not None else {},
        compiler_params=_params(("parallel", "parallel")),
    )(chip, *lead, h_all, g4)


def _d_h_groups(dp_all, groups, chip, i0, ni, dh_prev, after):
    rows_all = dp_all.shape[0]
    gcols = IN_COLS // 4
    tm = _D_H_ROWS
    g4 = groups.reshape(4, gcols, D)
    lead = ([dh_prev] if dh_prev is not None else []) + list(after)
    n_lead = len(lead)

    def body(chip_ref, *refs):
        a_ref, w_ref, o_ref = refs[n_lead:]
        j = pl.program_id(1)
        part = _dot(a_ref[...], w_ref[...])

        @pl.when(j == 0)
        def _():
            o_ref[...] = part

        @pl.when(j > 0)
        def _():
            o_ref[...] += part

    return pl.pallas_call(
        body, name="d_h_%d" % i0,
        grid_spec=pltpu.PrefetchScalarGridSpec(
            num_scalar_prefetch=1, grid=(ni, 4),
            in_specs=[pl.BlockSpec(memory_space=pl.ANY)] * n_lead
            + [pl.BlockSpec((tm, gcols), lambda i, j, ch: (i0 + i, ch[0] ^ j)),
               pl.BlockSpec((None, gcols, D), lambda i, j, ch: (j, 0, 0))],
            out_specs=pl.BlockSpec((tm, D), lambda i, j, ch: (i0 + i, 0))),
        out_shape=_sds((rows_all, D), F32),
        input_output_aliases={1: 0} if dh_prev is not None else {},
        compiler_params=_params(("parallel", "arbitrary")),
    )(chip, *lead, dp_all, g4)


def _reduce_scatter_send(parts, got, core, name):
    sums = _pair_add(parts, got, core, name + "_add")
    lands = [lax.empty((3,) + s_.shape[1:], BF16) for s_ in sums]
    return _exchange_start(sums, lands, _chip_routes(len(sums)), name + "_start")


def _reduce_scatter_finish(rs_state, after, chip, wmv, name):
    sums, landed = _exchange_wait(rs_state, after, name + "_wait")
    return _chip_sum_adamw(sums, landed, chip, wmv, name + "_adamw")


def _local_step(x, c, ctx, norm_w, ret_log2_decay, q_norm_w, k_norm_w, loss_target,
                mod, proj_in, get_w_o, on_out_grads, on_in_grad, started=()):
    nb, seq, _ = x.shape
    cx = ctx.shape[1]
    t_rows, c_rows = nb * seq, nb * cx
    rows_all = t_rows + c_rows
    nc = seq // CH
    tm = _pick(seq, 256, 128)
    te = _pick(seq, 512, 128)
    assert cx % tm == 0 and t_rows % cx == 0 and seq % GRID_W == 0

    x2 = x.reshape(t_rows, D)
    ctx2 = ctx.reshape(c_rows, D)
    tgt = loss_target.reshape(t_rows, D)
    lg = _log_gamma(ret_log2_decay)
    cos, sin = _rope_tables(seq)

    mod3 = mod[:, None, :]
    h_all = _norm_fwd(x2, mod3, norm_w, rows_all, 0, seq, 0, None, te, "norm_fwd", after=started)
    h_all = _norm_fwd(ctx2, mod3, norm_w, rows_all, t_rows, c_rows, nb, h_all, tm, "norm_fwd_ctx")
    px = proj_in(h_all)
    o_f, o_b, hist_f, hist_b = _ret_fwd(px, lg, nb, nc, cx)
    q16 = _qk_prep(px, q_norm_w, cos, sin, t_rows, 0, AQ, HQ, 4, seq, te, "q_prep")
    kx16 = _qk_prep(px, k_norm_w, cos, sin, t_rows, 0, AK, HKV, HKV, seq, te, "k_prep")
    kc16 = _qk_prep(px, k_norm_w, None, None, c_rows, t_rows, AK, HKV, HKV, seq, tm, "kc_prep")
    o_att, yatt16, lse = _att_fwd(q16, kx16, kc16, px, nb, seq, cx, te)
    w_o_ret16, w_o_att16, w_out16 = get_w_o(lse)
    yret16, a_ret, a_att, dxn, dout16, dgate, loss_b, gw_out = _merge_out(
        o_f, o_b, yatt16, px, w_o_ret16, w_o_att16, w_out16, x2, tgt, mod3, nb, seq, tm)

    da_ret16, do16, dao16, delta, dp_all, gw_o_att = _bwd_branches(
        dout16, w_out16, w_o_ret16, w_o_att16, px, a_ret, a_att, o_f, o_b, o_att, yatt16, rows_all, tm)
    gw_o_ret = _matmul(yret16, da_ret16, ta=True, tm=D, tn=D, tk=D, out_dtype=BF16, name="gw_o_ret")
    out_send, out_started = on_out_grads([gw_o_ret, gw_o_att, gw_out])
    dp_all, gq, gk, dak16, dav16, dcak16, dcav16 = _att_bwd(q16, kx16, kc16, px, dao16, delta, lse, q_norm_w, k_norm_w,
                                                            cos, sin, dp_all, nb, seq, cx, te, after=out_started)
    out_state, out_sent = out_send(gq)
    dq_f, dk_f, dv_f, dq_b, dk_b, dv_b, dck16, dcv16, dlg_scan = _ret_bwd(px, lg, do16, hist_f, hist_b, nb, nc, cx)
    dp_all = _assemble_lat(dp_all, dk_f, dk_b, dv_f, dv_b, dak16, dav16, dq_f, dq_b, tm)
    dp_all = _assemble_ctx(dp_all, dck16, dcv16, dcak16, dcav16, t_rows, tm)
    gw_in_t = _matmul(dp_all, h_all, ta=True, tm=1536, tn=D, tk=2304, out_dtype=BF16, name="gw_in", after=out_sent)
    in_state, dh = on_in_grad(gw_in_t, dp_all)
    grad_x, dsh, dsc, gnw_lat = _norm_bwd(dh, x2, mod3, norm_w, dxn, 0, seq, 0, te, "norm_bwd")
    dsh_c, dsc_c, gnw_ctx = _norm_bwd(dh, ctx2, mod3, norm_w, None, t_rows, c_rows, nb, tm, "norm_bwd_ctx")

    dlg = jnp.sum(dlg_scan[:, :, 0], axis=0).reshape(1, 2 * RH)
    misc = jnp.concatenate([gq, gk, dlg, jnp.sum(loss_b[:, 0, 0]).reshape(1, 1),
                            jnp.zeros((1, D - 2 * HD - 2 * RH - 1), F32)], axis=1)
    rows = []
    for b in range(nb):
        rows += [dsh[b], dsc[b], dgate[b]]
    rows += [dsh_c[0], dsc_c[0]] + [c[b:b + 1] for b in range(nb)] + [gnw_lat + gnw_ctx, misc]
    payload = jnp.concatenate(rows + [jnp.zeros((PAY_ROWS - len(rows), D), F32)], axis=0)
    return grad_x.reshape(nb, seq, D), out_state, in_state, payload


def _finish_small(gathered, nb, c_ctx, ret_log2_decay, w_ada16, dev):
    n_dev = gathered.shape[0]
    loc = 3 * D // n_dev
    dmod_all = gathered[:, :3 * nb].reshape(n_dev * nb, 3 * D)
    dmodc_parts = jnp.concatenate([gathered[:, 3 * nb:3 * nb + 2].reshape(n_dev, 2 * D), jnp.zeros((n_dev, D), F32)], axis=1)
    c_all = gathered[:, 3 * nb + 2:4 * nb + 2].reshape(n_dev * nb, D)
    nw_parts = gathered[:, 4 * nb + 2]
    misc_parts = gathered[:, 4 * nb + 3]
    n_rows = n_dev * nb + n_dev
    pad = (-n_rows) % 16
    c_rows = jnp.concatenate([c_all, jnp.broadcast_to(c_ctx.reshape(1, D), (n_dev, D)), jnp.zeros((pad, D), F32)], axis=0)
    dm_rows = jnp.concatenate([dmod_all, dmodc_parts, jnp.zeros((pad, 3 * D), F32)], axis=0)
    dm_loc_rows = lax.dynamic_slice_in_dim(dm_rows, dev * loc, loc, axis=1)
    r_pad = jnp.full((1, D), -1.0, F32).at[:, 2 * HD:2 * HD + 2 * RH].set(ret_log2_decay.reshape(1, 2 * RH))
    gb, gc, gnw, misc, gwa = _small_final(dmod_all, dmodc_parts, c_rows, dm_loc_rows, nw_parts, misc_parts,
                                          c_ctx.reshape(1, D), r_pad, w_ada16)
    return (gb, gc, gnw, misc[:, :HD], misc[:, HD:2 * HD], misc[:, 2 * HD:2 * HD + 2 * RH], gwa,
            misc[0, 2 * HD + 2 * RH])


def kernel(x, c, ctx, c_ctx, norm_w, w_ada, b_ada, w_in, ret_log2_decay, q_norm_w, k_norm_w, w_o_ret, w_o_att, w_out, loss_target, m_c_ctx, m_norm_w, m_w_ada, m_b_ada, m_w_in, m_ret_log2_decay, m_q_norm_w, m_k_norm_w, m_w_o_ret, m_w_o_att, m_w_out, v_c_ctx, v_norm_w, v_w_ada, v_b_ada, v_w_in, v_ret_log2_decay, v_q_norm_w, v_k_norm_w, v_w_o_ret, v_w_o_att, v_w_out):
    nb = x.shape[0]
    mx, my, mc = _mesh_pos()
    dev = 4 * mx + 2 * my + mc
    core = jnp.reshape(mc, (1,)).astype(jnp.int32)
    chip = jnp.reshape(2 * mx + my, (1,)).astype(jnp.int32)

    n_loc = 3 * D // N_DEV
    c8 = jnp.zeros((8, D), F32).at[:nb].set(c).at[nb].set(c_ctx)
    c_land = lax.dynamic_update_slice(lax.empty((N_DEV, 8, D), F32), c8[None], (dev, 0, 0))
    c_state, c_token = _exchange_start([c8[None]], [c_land], _bcast_routes(1), "gather_c_start")
    w_in_t = jnp.transpose(w_in[0])
    in_shard = w_in_t.astype(BF16)
    groups = lax.dynamic_update_slice(lax.empty((N_DEV,) + in_shard.shape, BF16), in_shard[None], (mc, 0, 0))
    groups = _pair_fill(groups, (0,), "gather_in_pair", after=(c_token,))
    _, (c_all,) = _exchange_wait(c_state, groups, "gather_c_wait")
    ada_shard = w_ada[0].astype(BF16)
    b_loc = lax.dynamic_slice(b_ada, (0, dev * n_loc), (1, n_loc))
    mod_cols = _mod_part(c_all.reshape(N_DEV * 8, D), ada_shard, b_loc)
    (mod_all,) = _all_gather([mod_cols], "gather_mod")
    mod = jnp.transpose(lax.dynamic_slice(mod_all, (0, dev * 8, 0), (N_DEV, 8, n_loc)), (1, 0, 2)).reshape(8, 3 * D)
    ada_land = lax.dynamic_update_slice(lax.empty((N_DEV,) + ada_shard.shape, BF16), ada_shard[None], (dev, 0, 0))

    (near_send, near_recv, near_bufs, near_routes), gin_token = _exchange_start(
        [in_shard[None]], [groups], _group_routes((1, 2)), "gather_in_start", after=(mod_all,))
    w_in_groups, wo_states, ada_landed = [], [], []
    wo_shards = [w_[0].astype(BF16) for w_ in (w_o_ret, w_o_att, w_out)]
    wo_lands = [lax.dynamic_update_slice(lax.empty((N_DEV,) + s_.shape, BF16), s_[None], (dev, 0, 0)) for s_ in wo_shards]

    def _state(send, recv, src, groups, routes):
        return send, recv, [src, groups], routes

    def proj_in(h_all):
        src, groups = near_bufs
        px = _in_proj_group(h_all, groups, 0, 1, chip, None, (gin_token,), "in_proj_0")
        (src,), (groups,) = _exchange_wait(_state(near_send, near_recv, src, groups, near_routes), px,
                                           "gather_in_wait_near")
        groups = _pair_fill(groups, (1, 2), "gather_in_fill_near")
        (far_send, far_recv, (src, groups), far_routes), far_token = _exchange_start(
            [src], [groups], _group_routes((3,)), "gather_in_start_far")
        wo_state, wo_token = _exchange_start([s_[None] for s_ in wo_shards] + [ada_shard[None]], wo_lands + [ada_land],
                                             _bcast_routes(4), "gather_wo_start", after=(far_token,))
        wo_states.append(wo_state)
        px = _in_proj_group(h_all, groups, 1, 2, chip, px, (wo_token,), "in_proj_near")
        (src,), (groups,) = _exchange_wait(_state(far_send, far_recv, src, groups, far_routes), px,
                                           "gather_in_wait_far")
        groups = _pair_fill(groups, (3,), "gather_in_fill_far")
        px = _in_proj_group(h_all, groups, 3, 1, chip, px, (), "in_proj_far")
        w_in_groups.append(groups)
        return px

    def get_w_o(after):
        _, (l_ret, l_att, l_out, l_ada) = _exchange_wait(wo_states[0], after, "gather_wo_wait")
        ada_landed.append(l_ada)
        return l_ret.reshape(RH * DV, D), l_att.reshape(D, D), l_out.reshape(D, D)

    def on_out_grads(grads):
        parts = [g_.reshape(N_DEV, g_.shape[0] // N_DEV, D) for g_ in grads]
        lands = [lax.empty((4,) + p_.shape[1:], BF16) for p_ in parts]
        pair_state, pair_token = _exchange_start(parts, lands, _pair_routes(len(parts)), "rs_out_pair_start")

        def send(after):
            parts_, got = _exchange_wait(pair_state, after, "rs_out_pair_wait")
            state, token = _reduce_scatter_send(parts_, got, core, "rs_out")
            return state, (token,)

        return send, (pair_token,)

    def on_in_grad(grad, dp_all):
        parts = [grad.reshape(N_DEV, IN_COLS // N_DEV, D)]
        lands = [lax.empty((4,) + p_.shape[1:], BF16) for p_ in parts]
        pair_state, pair_token = _exchange_start(parts, lands, _pair_routes(1), "rs_in_pair_start")
        dh = _d_h_groups(dp_all, w_in_groups[0], chip, 0, 1, None, (pair_token,))
        parts, got = _exchange_wait(pair_state, dh, "rs_in_pair_wait")
        state, token = _reduce_scatter_send(parts, got, core, "rs_in")
        n_tiles = dp_all.shape[0] // _D_H_ROWS
        return state, _d_h_groups(dp_all, w_in_groups[0], chip, 1, n_tiles - 1, dh, (token,))

    grad_x, out_state, in_state, payload = _local_step(
        x, c, ctx, norm_w, ret_log2_decay, q_norm_w, k_norm_w, loss_target,
        mod, proj_in, get_w_o, on_out_grads, on_in_grad, started=(gin_token,))

    pay_land = lax.dynamic_update_slice(lax.empty((N_DEV,) + payload.shape, F32), payload[None], (dev, 0, 0))
    pay_state, pay_token = _exchange_start([payload[None]], [pay_land], _bcast_routes(1), "gather_small_start")

    out_res = _reduce_scatter_finish(out_state, pay_token, chip,
                                     [(w_[0], m_[0], v_[0]) for w_, m_, v_ in ((w_o_ret, m_w_o_ret, v_w_o_ret),
                                                                                (w_o_att, m_w_o_att, v_w_o_att),
                                                                                (w_out, m_w_out, v_w_out))], "rs_out")
    (in_res,) = _reduce_scatter_finish(in_state, out_res[0][0], chip,
                                       [(w_in_t, jnp.transpose(m_w_in[0]), jnp.transpose(v_w_in[0]))], "rs_in")

    _, (gathered,) = _exchange_wait(pay_state, in_res[0], "gather_small_wait")
    w_ada16 = jnp.transpose(ada_landed[0], (1, 0, 2)).reshape(D, 3 * D)
    gb, gc, gnw, gq, gk, gr, gwa, loss = _finish_small(gathered, nb, c_ctx, ret_log2_decay, w_ada16, dev)
    big = {4: [jnp.transpose(r)[None] for r in in_res]}
    for i, res in zip((8, 9, 10), out_res):
        big[i] = [r[None] for r in res]
    small_g = {0: gc.reshape(c_ctx.shape), 1: gnw, 2: gwa[None], 3: gb, 5: gr.reshape(ret_log2_decay.shape), 6: gq, 7: gk}
    weights = [c_ctx, norm_w, w_ada, b_ada, w_in, ret_log2_decay, q_norm_w, k_norm_w, w_o_ret, w_o_att, w_out]
    ms = [m_c_ctx, m_norm_w, m_w_ada, m_b_ada, m_w_in, m_ret_log2_decay, m_q_norm_w, m_k_norm_w, m_w_o_ret, m_w_o_att, m_w_out]
    vs = [v_c_ctx, v_norm_w, v_w_ada, v_b_ada, v_w_in, v_ret_log2_decay, v_q_norm_w, v_k_norm_w, v_w_o_ret, v_w_o_att, v_w_out]
    def rows2(i):
        return [a.reshape(-1, weights[i].shape[-1]) for a in (weights[i], small_g[i], ms[i], vs[i])]

    small_ids = [i for i in small_g if i != 2]
    steps = dict(zip(small_ids, _adamw_small([rows2(i) for i in small_ids])))
    steps[2] = _adamw(*rows2(2), "adamw_w_ada")
    grads, deltas, new_ms, new_vs = [], [], [], []
    for i, w in enumerate(weights):
        res = big[i] if i in big else [small_g[i]] + [r.reshape(w.shape) for r in steps[i]]
        for lst, r in zip((grads, deltas, new_ms, new_vs), res):
            lst.append(r)
    return (loss, grad_x, *grads, *deltas, *new_ms, *new_vs)
```

```python
import numpy as np
import jax
import jax.numpy as jnp
from jax import lax
from jax.experimental import pallas as pl
from jax.experimental.pallas import tpu as pltpu

F32 = jnp.float32
BF16 = jnp.bfloat16

D = 1024
RH, DK, DV, CH = 4, 256, 512, 256
HQ, HKV, HD = 8, 2, 128
GRID_W = 64
ROPE_THETA = 10000.0
EPS = 1e-6
RK, RV, AK, AV, RQ, RG, AQ, AG, MR, MA = 0, 1024, 3072, 3328, 3584, 4608, 6656, 7680, 8704, 9728
IN_COLS = 10752
KV_COLS = 3584
N_DEV = 8
LR, B1, B2, ADAM_EPS, WD, STEP = 0.001, 0.9, 0.999, 1e-08, 0.01, 10
PAY_ROWS = 16
VMEM_LIMIT = 56 * 1024 * 1024
_D_H_ROWS = 1536
MESH_T = pl.DeviceIdType.MESH

NT = (((1,), (1,)), ((), ()))
TN = (((0,), (0,)), ((), ()))
SM_C = (HD ** -0.5) * float(np.log2(np.e))


def _params(sem):
    return pltpu.CompilerParams(dimension_semantics=sem, vmem_limit_bytes=VMEM_LIMIT)


def _pick(n, target, mult=8):
    best = None
    for t in range(mult, min(n, target) + 1, mult):
        if n % t == 0:
            best = t
    return best or n


def _dot(a, b, dn=None):
    if dn is None:
        return jnp.dot(a, b, preferred_element_type=F32)
    return lax.dot_general(a, b, dn, preferred_element_type=F32)


def _sig(v):
    return jax.nn.sigmoid(v)


def _silu(v):
    return v * _sig(v)


def _dsilu(v):
    s = _sig(v)
    return s * (1.0 + v * (1.0 - s))


def _sds(shape, dtype):
    return jax.ShapeDtypeStruct(shape, dtype)


def _matmul(a, b, *, ta=False, tb=False, tm, tn, tk, out_dtype, name, after=()):
    m = a.shape[1] if ta else a.shape[0]
    kdim = a.shape[0] if ta else a.shape[1]
    n = b.shape[0] if tb else b.shape[1]
    tm, tn, tk = _pick(m, tm, 128), _pick(n, tn, 128), _pick(kdim, tk, 128)
    nk = kdim // tk
    dn = (((0 if ta else 1,), (1 if tb else 0,)), ((), ()))

    def body(a_ref, b_ref, *rest):
        o_ref, acc_ref = rest[-2:]
        k = pl.program_id(2)
        part = _dot(a_ref[...].astype(BF16), b_ref[...].astype(BF16), dn)
        if nk == 1:
            o_ref[...] = part.astype(o_ref.dtype)
        else:
            @pl.when(k == 0)
            def _():
                acc_ref[...] = part

            @pl.when(k > 0)
            def _():
                acc_ref[...] += part

            @pl.when(k == nk - 1)
            def _():
                o_ref[...] = acc_ref[...].astype(o_ref.dtype)

    a_spec = pl.BlockSpec((tk, tm), lambda i, j, k: (k, i)) if ta else pl.BlockSpec((tm, tk), lambda i, j, k: (i, k))
    b_spec = pl.BlockSpec((tn, tk), lambda i, j, k: (j, k)) if tb else pl.BlockSpec((tk, tn), lambda i, j, k: (k, j))
    return pl.pallas_call(
        body, name=name, grid=(m // tm, n // tn, nk),
        in_specs=[a_spec, b_spec] + [pl.BlockSpec(memory_space=pl.ANY)] * len(after),
        out_specs=pl.BlockSpec((tm, tn), lambda i, j, k: (i, j)), out_shape=_sds((m, n), out_dtype),
        scratch_shapes=[pltpu.VMEM((tm, tn) if nk > 1 else (8, 128), F32)],
        compiler_params=_params(("parallel", "parallel", "arbitrary")),
    )(a, b, *after)


def _log_gamma(r):
    rp = jnp.full((8, 128), -1.0, F32).at[:2, :RH].set(r.reshape(2, RH))

    def body(r_ref, o_ref):
        o_ref[...] = jnp.log1p(-jnp.exp2(r_ref[...]))

    out = pl.pallas_call(body, name="log_gamma", out_shape=_sds((8, 128), F32))(rp)
    return out[:2, :RH]


def _mod_part(c_rows, w_ada_loc16, b_loc):
    def body(c_ref, w_ref, b_ref, o_ref):
        o_ref[...] = _dot(_silu(c_ref[...]).astype(BF16), w_ref[...]) + b_ref[...]

    return pl.pallas_call(
        body, name="mod_part", out_shape=_sds((c_rows.shape[0], w_ada_loc16.shape[1]), F32),
    )(c_rows, w_ada_loc16, b_loc)


def _norm_fwd(x2, mod3, norm_w, rows_all, row_off, rows_per_group, group0, h_prev, tm, name, after=()):
    rows = x2.shape[0]
    rb0 = row_off // tm
    bpg = rows_per_group // tm

    def body(*refs):
        x_ref, sh_ref, sc_ref, nw_ref, o_ref = refs[-5:]
        xv = x_ref[...]
        r = lax.rsqrt(jnp.mean(xv * xv, axis=-1, keepdims=True) + EPS)
        o_ref[...] = ((xv * r) * nw_ref[...] * (1.0 + sc_ref[...]) + sh_ref[...]).astype(BF16)

    in_specs = [pl.BlockSpec((tm, D), lambda i: (i, 0)),
                pl.BlockSpec((None, 1, D), lambda i: (group0 + i // bpg, 0, 0)),
                pl.BlockSpec((None, 1, D), lambda i: (group0 + i // bpg, 0, 1)),
                pl.BlockSpec((1, D), lambda i: (0, 0))]
    in_specs = [pl.BlockSpec(memory_space=pl.ANY)] * len(after) + in_specs
    args = list(after) + [x2, mod3, mod3, norm_w]
    alias = {}
    if h_prev is not None:
        in_specs.insert(0, pl.BlockSpec(memory_space=pl.ANY))
        args.insert(0, h_prev)
        alias = {0: 0}
    return pl.pallas_call(
        body, name=name, grid=(rows // tm,), in_specs=in_specs,
        out_specs=pl.BlockSpec((tm, D), lambda i: (rb0 + i, 0)), out_shape=_sds((rows_all, D), BF16),
        input_output_aliases=alias, compiler_params=_params(("parallel",)),
    )(*args)


def _decays(lg, fwd):
    ii = lax.broadcasted_iota(jnp.int32, (CH, CH), 0)
    jj = lax.broadcasted_iota(jnp.int32, (CH, CH), 1)
    ri = lax.broadcasted_iota(jnp.int32, (CH, 1), 0).astype(F32)
    rel = (ii - jj) if fwd else (jj - ii)
    relf = jnp.maximum(rel, 0).astype(F32)
    mask = jnp.where(rel >= 0, jnp.exp(lg * relf), 0.0)
    qe = (ri + 1.0) if fwd else (CH - ri)
    ke = (CH - 1.0 - ri) if fwd else ri
    return mask, relf, jnp.exp(lg * qe), qe, jnp.exp(lg * ke), ke


def _wide_specs(rowf):
    return [pl.BlockSpec((CH, 2 * DK), lambda b, c: (rowf(b, c), RQ // (2 * DK))),
            pl.BlockSpec((CH, 2 * DK), lambda b, c: (rowf(b, c), RQ // (2 * DK) + 1)),
            pl.BlockSpec((CH, RH * DK), lambda b, c: (rowf(b, c), RK // (RH * DK))),
            pl.BlockSpec((CH, 2 * DV), lambda b, c: (rowf(b, c), RV // (2 * DV))),
            pl.BlockSpec((CH, 2 * DV), lambda b, c: (rowf(b, c), RV // (2 * DV) + 1))]


def _head_qkv(refs, h):
    q0, q1, k, v0, v1 = refs
    lo = h % 2
    q = (q0, q1)[h // 2][:, lo * DK:(lo + 1) * DK].astype(F32)
    kk = k[:, h * DK:(h + 1) * DK].astype(F32) * (DK ** -0.5)
    v16 = (v0, v1)[h // 2][:, lo * DV:(lo + 1) * DV].astype(BF16)
    return q, kk, v16


def _ctx_specs(t_rows, cx):
    rb = t_rows // cx
    return [pl.BlockSpec((cx, RH * DK), lambda b, c: (rb + b, RK // (RH * DK))),
            pl.BlockSpec((cx, 2 * DV), lambda b, c: (rb + b, RV // (2 * DV))),
            pl.BlockSpec((cx, 2 * DV), lambda b, c: (rb + b, RV // (2 * DV) + 1))]


def _ctx_kv(refs, h):
    k, v0, v1 = refs
    kk = k[:, h * DK:(h + 1) * DK].astype(F32) * (DK ** -0.5)
    lo = h % 2
    return kk, (v0, v1)[h // 2][:, lo * DV:(lo + 1) * DV].astype(BF16)


def _ret_fwd(px, lg, nb, nc, cx):
    t_rows = nb * nc * CH

    def body(lg_ref, *refs):
        ins = (refs[0:5], refs[5:10])
        ctx_refs = refs[10:13]
        of_ref, ob_ref, hf_ref, hb_ref, sf, sb = refs[13:]
        c = pl.program_id(1)

        @pl.when(c == 0)
        def _():
            pos = lax.broadcasted_iota(jnp.int32, (cx, 1), 0).astype(F32)
            for h in range(RH):
                k, v16 = _ctx_kv(ctx_refs, h)
                sf[h] = _dot((k * jnp.exp(lg_ref[0, h] * (cx - 1.0 - pos))).astype(BF16), v16, TN)
                sb[h] = _dot((k * jnp.exp(lg_ref[1, h] * pos)).astype(BF16), v16, TN)

        for d, (o_ref, h_ref, s) in enumerate(((of_ref, hf_ref, sf), (ob_ref, hb_ref, sb))):
            for h in range(RH):
                lg_d = lg_ref[d, h]
                mask, _, qd, _, kd, _ = _decays(lg_d, d == 0)
                q, k, v16 = _head_qkv(ins[d], h)
                a = _dot(q.astype(BF16), k.astype(BF16), NT)
                st = s[h]
                st16 = st.astype(BF16)
                h_ref[h] = st16
                o = _dot((a * mask).astype(BF16), v16) + _dot((q * qd).astype(BF16), st16)
                o_ref[:, h * DV:(h + 1) * DV] = o.astype(BF16)
                s[h] = st * jnp.exp(lg_d * CH) + _dot((k * kd).astype(BF16), v16, TN)

    def fw(b, c):
        return b * nc + c

    def bw(b, c):
        return b * nc + nc - 1 - c

    in_specs = [pl.BlockSpec(memory_space=pltpu.SMEM)] + _wide_specs(fw) + _wide_specs(bw) + _ctx_specs(t_rows, cx)
    out_specs = [pl.BlockSpec((CH, RH * DV), lambda b, c: (fw(b, c), 0)),
                 pl.BlockSpec((CH, RH * DV), lambda b, c: (bw(b, c), 0)),
                 pl.BlockSpec((None, None, RH, DK, DV), lambda b, c: (b, c, 0, 0, 0)),
                 pl.BlockSpec((None, None, RH, DK, DV), lambda b, c: (b, nc - 1 - c, 0, 0, 0))]
    return pl.pallas_call(
        body, name="ret_fwd", grid=(nb, nc), in_specs=in_specs, out_specs=out_specs,
        out_shape=[_sds((t_rows, RH * DV), BF16)] * 2 + [_sds((nb, nc, RH, DK, DV), BF16)] * 2,
        scratch_shapes=[pltpu.VMEM((RH, DK, DV), F32), pltpu.VMEM((RH, DK, DV), F32)],
        compiler_params=_params(("parallel", "arbitrary")),
    )(lg, *([px] * 13))


def _rope_tables(seq):
    rows = seq // GRID_W
    row = np.repeat(np.arange(rows, dtype=np.float32), GRID_W)
    col = np.tile(np.arange(GRID_W, dtype=np.float32), rows)
    half = HD // 2
    freqs = (ROPE_THETA ** (-np.arange(0, half, 2, dtype=np.float32) / half)).astype(np.float32)
    ang = np.concatenate([row[:, None] * freqs, col[:, None] * freqs], axis=-1).astype(np.float32)
    cos = np.repeat(np.cos(ang), 2, axis=-1).astype(np.float32)
    sin = np.repeat(np.sin(ang), 2, axis=-1).astype(np.float32)
    sign = np.tile(np.array([-1.0, 1.0], np.float32), HD // 2)
    return jnp.asarray(cos), jnp.asarray(sin * sign)


def _swap_pairs(v):
    lane = lax.broadcasted_iota(jnp.int32, v.shape, 1)
    return jnp.where((lane & 1) == 0, pltpu.roll(v, HD - 1, 1), pltpu.roll(v, 1, 1))


def _qk_prep(px, nw, cos, sin, rows, row_off, col_off, heads, hb, seq, tm, name):
    rope = cos is not None
    rb0 = row_off // tm
    pb = seq // tm if rope else 1
    bw = hb * HD

    def body(*refs):
        if rope:
            x_ref, w_ref, c_ref, s_ref, o_ref = refs
        else:
            x_ref, w_ref, o_ref = refs
        for h in range(hb):
            sl = slice(h * HD, (h + 1) * HD)
            xv = x_ref[:, sl].astype(F32)
            r = lax.rsqrt(jnp.mean(xv * xv, axis=-1, keepdims=True) + EPS)
            t = (xv * r) * w_ref[...]
            if rope:
                t = t * c_ref[...] + _swap_pairs(t) * s_ref[...]
            o_ref[:, sl] = t.astype(BF16)

    in_specs = [pl.BlockSpec((tm, bw), lambda i, j: (rb0 + i, col_off // bw + j)),
                pl.BlockSpec((1, HD), lambda i, j: (0, 0))]
    args = [px, nw]
    if rope:
        in_specs += [pl.BlockSpec((tm, HD), lambda i, j: (i % pb, 0))] * 2
        args += [cos, sin]
    return pl.pallas_call(
        body, name=name, grid=(rows // tm, heads // hb), in_specs=in_specs,
        out_specs=pl.BlockSpec((tm, bw), lambda i, j: (i, j)), out_shape=_sds((rows, heads * HD), BF16),
        compiler_params=_params(("parallel", "parallel")),
    )(*args)


def _att_fwd(q16, kx16, kc16, px, nb, seq, cx, tq):
    t_rows = nb * seq
    nq = seq // tq
    rep = HQ // HKV
    gw = rep * HD

    def body(q_ref, kx_ref, kc_ref, vx_ref, vc_ref, g_ref, o_ref, y_ref, l_ref):
        kx = kx_ref[...]
        kc = kc_ref[...]
        vx = vx_ref[...].astype(BF16)
        vc = vc_ref[...].astype(BF16)
        l_ref[...] = jnp.zeros_like(l_ref)
        for r in range(rep):
            sl = slice(r * HD, (r + 1) * HD)
            q = q_ref[:, sl]
            s1 = _dot(q, kx, NT)
            s2 = _dot(q, kc, NT)
            m = jnp.maximum(jnp.max(s1, axis=-1, keepdims=True), jnp.max(s2, axis=-1, keepdims=True))
            e1 = jnp.exp2((s1 - m) * SM_C)
            e2 = jnp.exp2((s2 - m) * SM_C)
            tot = jnp.sum(e1, axis=-1, keepdims=True) + jnp.sum(e2, axis=-1, keepdims=True)
            o = (_dot(e1.astype(BF16), vx) + _dot(e2.astype(BF16), vc)) * (1.0 / tot)
            o_ref[:, sl] = o
            y_ref[:, sl] = (o * _silu(g_ref[:, sl].astype(F32))).astype(BF16)
            l_ref[:, r:r + 1] = m * SM_C + jnp.log(tot) * float(np.log2(np.e))

    qblk = pl.BlockSpec((tq, gw), lambda b, g, i: (b * nq + i, g))
    return pl.pallas_call(
        body, name="att_fwd", grid=(nb, HKV, nq),
        in_specs=[qblk,
                  pl.BlockSpec((seq, HD), lambda b, g, i: (b, g)),
                  pl.BlockSpec((cx, HD), lambda b, g, i: (b, g)),
                  pl.BlockSpec((seq, HD), lambda b, g, i: (b, AV // HD + g)),
                  pl.BlockSpec((cx, HD), lambda b, g, i: (t_rows // cx + b, AV // HD + g)),
                  pl.BlockSpec((tq, gw), lambda b, g, i: (b * nq + i, AG // gw + g))],
        out_specs=[qblk, qblk, pl.BlockSpec((tq, 128), lambda b, g, i: (b * nq + i, g))],
        out_shape=[_sds((t_rows, D), F32), _sds((t_rows, D), BF16), _sds((t_rows, HKV * 128), F32)],
        compiler_params=_params(("parallel", "parallel", "parallel")),
    )(q16, kx16, kc16, px, px, px)


def _gate_specs(tm, col0):
    hw = D // 2
    return [pl.BlockSpec((tm, hw), lambda i: (i, col0 // hw)), pl.BlockSpec((tm, hw), lambda i: (i, col0 // hw + 1))]


def _merge_out(o_f, o_b, yatt16, px, w_o_ret16, w_o_att16, w_out16, x2, tgt, mod3, nb, seq, tm):
    t_rows = nb * seq
    bpb = seq // tm
    hw = D // 2

    def body(of_ref, ob_ref, g0, g1, g2, g3, wr_ref, ya_ref, wa_ref, mr0, mr1, ma0, ma1, wo_ref, x_ref, t_ref, gt_ref,
             yr_ref, ar_ref, aa_ref, dxn_ref, dout_ref, dg_ref, loss_ref, gw_ref, y_ref, acc_ref):
        i = pl.program_id(1)
        first = jnp.logical_and(pl.program_id(0) == 0, i == 0)
        for h, g_ref in enumerate((g0, g1, g2, g3)):
            sl = slice(h * DV, (h + 1) * DV)
            o = of_ref[:, sl].astype(F32) + ob_ref[:, sl].astype(F32)
            r = lax.rsqrt(jnp.mean(o * o, axis=-1, keepdims=True) + EPS)
            yr_ref[:, sl] = ((o * r) * _silu(g_ref[...].astype(F32))).astype(BF16)
        ar = _dot(yr_ref[...], wr_ref[...])
        aa = _dot(ya_ref[...], wa_ref[...])
        ar_ref[...] = ar.astype(BF16)
        aa_ref[...] = aa.astype(BF16)
        for j, (mr_ref, ma_ref) in enumerate(((mr0, ma0), (mr1, ma1))):
            sl = slice(j * hw, (j + 1) * hw)
            y_ref[:, sl] = (_sig(mr_ref[...].astype(F32)) * ar[:, sl]
                            + _sig(ma_ref[...].astype(F32)) * aa[:, sl]).astype(BF16)
        out = _dot(y_ref[...], wo_ref[...])
        gate = gt_ref[...]
        diff = x_ref[...] + gate * out - t_ref[...]
        dxn = diff * (1.0 / D)
        dxn_ref[...] = dxn
        dout_ref[...] = (gate * dxn).astype(BF16)

        @pl.when(first)
        def _():
            acc_ref[...] = jnp.zeros_like(acc_ref)

        for j in range(2):
            sl = slice(j * hw, (j + 1) * hw)
            acc_ref[sl, :] += _dot(y_ref[:, sl], dout_ref[...], TN)

        @pl.when(jnp.logical_and(pl.program_id(0) == nb - 1, i == bpb - 1))
        def _():
            gw_ref[...] = acc_ref[...].astype(BF16)

        dg = jnp.sum(dxn * out, axis=0, keepdims=True)
        ls = jnp.broadcast_to(jnp.sum(diff * diff) * (0.5 / D), (1, 128))

        @pl.when(i == 0)
        def _():
            dg_ref[...] = dg
            loss_ref[...] = ls

        @pl.when(i > 0)
        def _():
            dg_ref[...] += dg
            loss_ref[...] += ls

    def cols(width, col0):
        return pl.BlockSpec((tm, width), lambda b, i: (b * bpb + i, col0 // width))

    def whole(rows):
        return pl.BlockSpec((rows, D), lambda b, i: (0, 0))

    row, wide = cols(D, 0), cols(RH * DV, 0)
    gates = [cols(DV, RG + h * DV) for h in range(RH)]
    merge_gates = [cols(hw, MR), cols(hw, MR + hw), cols(hw, MA), cols(hw, MA + hw)]
    return pl.pallas_call(
        body, name="merge_out", grid=(nb, bpb),
        in_specs=[wide, wide] + gates + [whole(RH * DV), row, whole(D)] + merge_gates
        + [whole(D), row, row, pl.BlockSpec((None, 1, D), lambda b, i: (b, 0, 2))],
        out_specs=[wide, row, row, row, row, pl.BlockSpec((None, 1, D), lambda b, i: (b, 0, 0)),
                   pl.BlockSpec((None, 1, 128), lambda b, i: (b, 0, 0)), whole(D)],
        out_shape=[_sds((t_rows, RH * DV), BF16)] + [_sds((t_rows, D), BF16)] * 2
        + [_sds((t_rows, D), F32), _sds((t_rows, D), BF16), _sds((nb, 1, D), F32), _sds((nb, 1, 128), F32),
           _sds((D, D), BF16)],
        scratch_shapes=[pltpu.VMEM((tm, D), BF16), pltpu.VMEM((D, D), F32)],
        compiler_params=_params(("arbitrary", "arbitrary")),
    )(o_f, o_b, *([px] * RH), w_o_ret16, yatt16, w_o_att16, px, px, px, px, w_out16, x2, tgt, mod3)


def _bwd_branches(dout16, w_out16, w_o_ret16, w_o_att16, px, a_ret, a_att, o_f, o_b, o_att, yatt16, rows_all, tm):
    t_rows = dout16.shape[0]
    hw = D // 2

    def body(do_ref, wo_ref, wr_ref, wa_ref, mr0, mr1, ma0, ma1, ar_ref, aa_ref, rg0, rg1, rg2, rg3, of_ref, ob_ref,
             ag0, ag1, oa_ref, ya_ref, dar_ref, dor_ref, dao_ref, dl_ref, dp_ref, gwa_ref, daa_ref, acca_ref):
        dy_all = _dot(do_ref[...], wo_ref[...], NT)
        for j, (mr_ref, ma_ref) in enumerate(((mr0, ma0), (mr1, ma1))):
            sl = slice(j * hw, (j + 1) * hw)
            dy = dy_all[:, sl]
            sr = _sig(mr_ref[...].astype(F32))
            sa = _sig(ma_ref[...].astype(F32))
            dar_ref[:, sl] = (dy * sr).astype(BF16)
            daa_ref[:, sl] = (dy * sa).astype(BF16)
            dp_ref[:, MR - RG + j * hw:MR - RG + (j + 1) * hw] = (
                dy * ar_ref[:, sl].astype(F32) * sr * (1.0 - sr)).astype(BF16)
            dp_ref[:, MA - RG + j * hw:MA - RG + (j + 1) * hw] = (
                dy * aa_ref[:, sl].astype(F32) * sa * (1.0 - sa)).astype(BF16)
        da_ret = dar_ref[...]

        @pl.when(pl.program_id(0) == 0)
        def _():
            acca_ref[...] = jnp.zeros_like(acca_ref)

        for j in range(2):
            sl = slice(j * hw, (j + 1) * hw)
            acca_ref[sl, :] += _dot(ya_ref[:, sl], daa_ref[...], TN)

        for h, g_ref in enumerate((rg0, rg1, rg2, rg3)):
            sl = slice(h * DV, (h + 1) * DV)
            dy = _dot(da_ret, wr_ref[sl, :], NT)
            g = g_ref[...].astype(F32)
            o = of_ref[:, sl].astype(F32) + ob_ref[:, sl].astype(F32)
            r = lax.rsqrt(jnp.mean(o * o, axis=-1, keepdims=True) + EPS)
            on = o * r
            sg = _sig(g)
            don = dy * (g * sg)
            dp_ref[:, sl] = (dy * on * (sg * (1.0 + g * (1.0 - sg)))).astype(BF16)
            dor_ref[:, sl] = (r * (don - on * jnp.mean(on * don, axis=-1, keepdims=True))).astype(BF16)
        dy_all = _dot(daa_ref[...], wa_ref[...], NT)
        dl_ref[...] = jnp.zeros_like(dl_ref)
        for j, g_ref in enumerate((ag0, ag1)):
            sl = slice(j * hw, (j + 1) * hw)
            dy = dy_all[:, sl]
            g = g_ref[...].astype(F32)
            sg = _sig(g)
            dao = dy * (g * sg)
            dao_ref[:, sl] = dao.astype(BF16)
            prod = dao * oa_ref[:, sl]
            for r in range(hw // HD):
                dl_ref[:, j * 128 + r:j * 128 + r + 1] = jnp.sum(prod[:, r * HD:(r + 1) * HD], axis=-1, keepdims=True)
            dp_ref[:, AG - RG + j * hw:AG - RG + (j + 1) * hw] = (
                dy * oa_ref[:, sl] * (sg * (1.0 + g * (1.0 - sg)))).astype(BF16)

        @pl.when(pl.program_id(0) == t_rows // tm - 1)
        def _():
            gwa_ref[...] = acca_ref[...].astype(BF16)

    def gate(h):
        return pl.BlockSpec((tm, DV), lambda i: (i, RG // DV + h))

    def whole(rows):
        return pl.BlockSpec((rows, D), lambda i: (0, 0), pipeline_mode=pl.Buffered(1))

    row = pl.BlockSpec((tm, D), lambda i: (i, 0))
    wide = pl.BlockSpec((tm, RH * DV), lambda i: (i, 0))
    return pl.pallas_call(
        body, name="bwd_branches", grid=(t_rows // tm,),
        in_specs=[row, whole(D), whole(RH * DV), whole(D)] + _gate_specs(tm, MR) + _gate_specs(tm, MA) + [row, row]
        + [gate(h) for h in range(RH)] + [wide, wide] + _gate_specs(tm, AG) + [row, row],
        out_specs=[row, wide, row, pl.BlockSpec((tm, HKV * 128), lambda i: (i, 0)),
                   pl.BlockSpec((pl.Element(tm), pl.Element(IN_COLS - RG)), lambda i: (i * tm, RG)), whole(D)],
        out_shape=[_sds((t_rows, D), BF16), _sds((t_rows, RH * DV), BF16), _sds((t_rows, D), BF16),
                   _sds((t_rows, HKV * 128), F32), _sds((rows_all, IN_COLS), BF16), _sds((D, D), BF16)],
        scratch_shapes=[pltpu.VMEM((tm, D), BF16), pltpu.VMEM((D, D), F32)],
        compiler_params=_params(("arbitrary",)),
    )(dout16, w_out16, w_o_ret16, w_o_att16, px, px, px, px, a_ret, a_att, *([px] * RH), o_f, o_b, px, px, o_att,
      yatt16)


def _att_bwd(q16, kx16, kc16, px, dao16, delta, lse, q_norm_w, k_norm_w, cos, sin, dp_all, nb, seq, cx, tq, after=()):
    t_rows = nb * seq
    nq = seq // tq
    rep = HQ // HKV
    gw = rep * HD
    scale = HD ** -0.5

    def body(q_ref, kx_ref, kc_ref, vx_ref, vc_ref, dao_ref, dl_ref, l_ref, xq_ref, w_ref, c_ref, s_ref,
             xk_ref, xkc_ref, wk_ref, ck_ref, sk_ref, *rest):
        daq_ref, gq_ref, gk_ref, dkx_ref, dvx_ref, dkc_ref, dvc_ref = rest[1 + len(after):8 + len(after)]
        accs = rest[8 + len(after):]
        i = pl.program_id(2)
        head0 = jnp.logical_and(pl.program_id(0) == 0, pl.program_id(1) == 0)
        first = jnp.logical_and(head0, i == 0)
        gq = jnp.zeros((1, HD), F32)
        kx = kx_ref[...]
        kc = kc_ref[...]
        vx = vx_ref[...].astype(BF16)
        vc = vc_ref[...].astype(BF16)
        @pl.when(i == 0)
        def _():
            for acc in accs:
                acc[...] = jnp.zeros_like(acc)

        dkx, dvx, dkc, dvc = [acc[...] for acc in accs]
        for r in range(rep):
            sl = slice(r * HD, (r + 1) * HD)
            q = q_ref[:, sl]
            lr = l_ref[:, r:r + 1]
            p1 = jnp.exp2(_dot(q, kx, NT) * SM_C - lr)
            p2 = jnp.exp2(_dot(q, kc, NT) * SM_C - lr)
            da16 = dao_ref[:, sl]
            delta = dl_ref[:, r:r + 1]
            ds1 = (p1 * (_dot(da16, vx, NT) - delta)).astype(BF16)
            ds2 = (p2 * (_dot(da16, vc, NT) - delta)).astype(BF16)
            dq = (_dot(ds1, kx) + _dot(ds2, kc)) * scale
            dkx += _dot(q, ds1, TN)
            dkc += _dot(q, ds2, TN)
            dvx += _dot(da16, p1.astype(BF16), TN)
            dvc += _dot(da16, p2.astype(BF16), TN)
            dt = dq * c_ref[...] + _swap_pairs(dq * s_ref[...])
            xv = xq_ref[:, sl].astype(F32)
            rn = lax.rsqrt(jnp.mean(xv * xv, axis=-1, keepdims=True) + EPS)
            xh = xv * rn
            dxh = dt * w_ref[...]
            daq_ref[:, sl] = (rn * (dxh - xh * jnp.mean(dxh * xh, axis=-1, keepdims=True))).astype(BF16)
            gq += jnp.sum(dt * xh, axis=0, keepdims=True)
        for acc, val in zip(accs, (dkx, dvx, dkc, dvc)):
            acc[...] = val

        @pl.when(first)
        def _():
            gq_ref[...] = gq

        @pl.when(jnp.logical_not(first))
        def _():
            gq_ref[...] += gq

        def k_back(dk, x_ref, dk_ref):
            xv = x_ref[...].astype(F32)
            rn = lax.rsqrt(jnp.mean(xv * xv, axis=-1, keepdims=True) + EPS)
            xh = xv * rn
            dxh = dk * wk_ref[...]
            dk_ref[...] = (rn * (dxh - xh * jnp.mean(dxh * xh, axis=-1, keepdims=True))).astype(BF16)
            return jnp.sum(dk * xh, axis=0, keepdims=True)

        @pl.when(i == nq - 1)
        def _():
            dvx_ref[...] = dvx.T.astype(BF16)
            dvc_ref[...] = dvc.T.astype(BF16)
            dk = dkx.T * scale
            gk = (k_back(dk * ck_ref[...] + _swap_pairs(dk * sk_ref[...]), xk_ref, dkx_ref)
                  + k_back(dkc.T * scale, xkc_ref, dkc_ref))

            @pl.when(head0)
            def _():
                gk_ref[...] = gk

            @pl.when(jnp.logical_not(head0))
            def _():
                gk_ref[...] += gk

    qblk = pl.BlockSpec((tq, gw), lambda b, g, i: (b * nq + i, g))
    kxb = pl.BlockSpec((seq, HD), lambda b, g, i: (b, g))
    kcb = pl.BlockSpec((cx, HD), lambda b, g, i: (b, g))
    table = pl.BlockSpec((tq, HD), lambda b, g, i: (i, 0))
    tables = pl.BlockSpec((seq, HD), lambda b, g, i: (0, 0))
    one = pl.BlockSpec((1, HD), lambda b, g, i: (0, 0))
    lane = pl.BlockSpec((tq, 128), lambda b, g, i: (b * nq + i, g))
    return pl.pallas_call(
        body, name="att_bwd", grid=(nb, HKV, nq),
        in_specs=[qblk,
                  pl.BlockSpec((seq, HD), lambda b, g, i: (b, g)),
                  pl.BlockSpec((cx, HD), lambda b, g, i: (b, g)),
                  pl.BlockSpec((seq, HD), lambda b, g, i: (b, AV // HD + g)),
                  pl.BlockSpec((cx, HD), lambda b, g, i: (t_rows // cx + b, AV // HD + g)),
                  qblk, lane, lane,
                  pl.BlockSpec((tq, gw), lambda b, g, i: (b * nq + i, AQ // gw + g)), one, table, table,
                  pl.BlockSpec((seq, HD), lambda b, g, i: (b, AK // HD + g)),
                  pl.BlockSpec((cx, HD), lambda b, g, i: (t_rows // cx + b, AK // HD + g)), one, tables, tables]
        + [pl.BlockSpec(memory_space=pl.ANY)] * (1 + len(after)),
        out_specs=[pl.BlockSpec((tq, gw), lambda b, g, i: (b * nq + i, AQ // gw + g)), one, one, kxb, kxb, kcb, kcb],
        out_shape=[_sds(dp_all.shape, BF16), _sds((1, HD), F32), _sds((1, HD), F32), _sds((t_rows, HKV * HD), BF16),
                   _sds((t_rows, HKV * HD), BF16), _sds((nb * cx, HKV * HD), BF16), _sds((nb * cx, HKV * HD), BF16)],
        scratch_shapes=[pltpu.VMEM((HD, seq), F32), pltpu.VMEM((HD, seq), F32), pltpu.VMEM((HD, cx), F32),
                        pltpu.VMEM((HD, cx), F32)],
        input_output_aliases={17: 0},
        compiler_params=_params(("arbitrary", "arbitrary", "arbitrary")),
    )(q16, kx16, kc16, px, px, dao16, delta, lse, px, q_norm_w, cos, sin, px, px, k_norm_w, cos, sin, dp_all, *after)


def _ret_bwd(px, lg, do16, hist_f, hist_b, nb, nc, cx):
    t_rows = nb * nc * CH

    def body(lg_ref, *refs):
        ins = (refs[0:5], refs[7:12])
        do_refs = (refs[5], refs[12])
        h_refs = (refs[6], refs[13])
        ctx_refs = refs[14:17]
        outs = (refs[17:20], refs[20:23])
        dck_ref, dcv_ref, dlg_ref = refs[23:26]
        dss = (refs[26], refs[27])
        c = pl.program_id(1)

        @pl.when(c == 0)
        def _():
            dss[0][...] = jnp.zeros_like(dss[0])
            dss[1][...] = jnp.zeros_like(dss[1])
            dlg_ref[...] = jnp.zeros_like(dlg_ref)

        for d in range(2):
            dq_ref, dk_ref, dv_ref = outs[d]
            for h in range(RH):
                lg_d = lg_ref[d, h]
                mask, relf, qd, qe, kd, ke = _decays(lg_d, d == 0)
                g_ch = jnp.exp(lg_d * CH)
                q, k, v16 = _head_qkv(ins[d], h)
                q16 = q.astype(BF16)
                k16 = k.astype(BF16)
                do16v = do_refs[d][:, h * DV:(h + 1) * DV]
                st16 = h_refs[d][h]
                dst = dss[d][h]
                dst16 = dst.astype(BF16)
                a = _dot(q16, k16, NT) * mask
                dp = _dot(do16v, v16, NT)
                da16 = (dp * mask).astype(BF16)
                dq_cross = _dot(do16v, st16, NT) * qd
                dq_ref[:, h * DK:(h + 1) * DK] = (_dot(da16, k16) + dq_cross).astype(BF16)
                dk_state = _dot(v16, dst16, NT) * kd
                dk_ref[:, h * DK:(h + 1) * DK] = ((_dot(da16, q16, TN) + dk_state) * (DK ** -0.5)).astype(BF16)
                dv = _dot(a.astype(BF16), do16v, TN) + _dot((k * kd).astype(BF16), dst16)
                dv_ref[:, h * DV:(h + 1) * DV] = dv.astype(BF16)
                dlg = (jnp.sum(relf * a * dp)
                       + jnp.sum(qe * jnp.sum(q * dq_cross, axis=-1, keepdims=True))
                       + jnp.sum(ke * jnp.sum(k * dk_state, axis=-1, keepdims=True))
                       + CH * g_ch * jnp.sum(dst * st16.astype(F32)))
                row = d * RH + h
                dlg_ref[row:row + 1, :] += jnp.broadcast_to(dlg, (1, 128))
                dss[d][h] = g_ch * dst + _dot((q * qd).astype(BF16), do16v, TN)

        @pl.when(c == nc - 1)
        def _():
            pos = lax.broadcasted_iota(jnp.int32, (cx, 1), 0).astype(F32)
            for h in range(RH):
                k, v16 = _ctx_kv(ctx_refs, h)
                dk = jnp.zeros((cx, DK), F32)
                dv = jnp.zeros((cx, DV), F32)
                for d, e in enumerate((cx - 1.0 - pos, pos)):
                    w = jnp.exp(lg_ref[d, h] * e)
                    ds16 = dss[d][h].astype(BF16)
                    t = _dot(v16, ds16, NT)
                    dk += t * w
                    dv += _dot((k * w).astype(BF16), ds16)
                    dlg = jnp.sum(e * w * jnp.sum(k * t, axis=-1, keepdims=True))
                    row = d * RH + h
                    dlg_ref[row:row + 1, :] += jnp.broadcast_to(dlg, (1, 128))
                dck_ref[:, h * DK:(h + 1) * DK] = (dk * (DK ** -0.5)).astype(BF16)
                dcv_ref[:, h * DV:(h + 1) * DV] = dv.astype(BF16)

    def fw(b, c):
        return b * nc + nc - 1 - c

    def bw(b, c):
        return b * nc + c

    def rows(rowf, width):
        return pl.BlockSpec((CH, width), lambda b, c: (rowf(b, c), 0))

    def hist(rowf):
        return pl.BlockSpec((None, None, RH, DK, DV), lambda b, c: (b, rowf(0, c), 0, 0, 0))

    in_specs = [pl.BlockSpec(memory_space=pltpu.SMEM)]
    out_specs = []
    for rowf in (fw, bw):
        in_specs += _wide_specs(rowf) + [rows(rowf, RH * DV), hist(rowf)]
        out_specs += [rows(rowf, RH * DK), rows(rowf, RH * DK), rows(rowf, RH * DV)]
    in_specs += _ctx_specs(t_rows, cx)
    out_specs += [pl.BlockSpec((cx, RH * DK), lambda b, c: (b, 0)), pl.BlockSpec((cx, RH * DV), lambda b, c: (b, 0)),
                  pl.BlockSpec((None, 8, 128), lambda b, c: (b, 0, 0))]
    qk = _sds((t_rows, RH * DK), BF16)
    vv = _sds((t_rows, RH * DV), BF16)
    return pl.pallas_call(
        body, name="ret_bwd", grid=(nb, nc), in_specs=in_specs, out_specs=out_specs,
        out_shape=[qk, qk, vv, qk, qk, vv, _sds((nb * cx, RH * DK), BF16), _sds((nb * cx, RH * DV), BF16),
                   _sds((nb, 8, 128), F32)],
        scratch_shapes=[pltpu.VMEM((RH, DK, DV), F32), pltpu.VMEM((RH, DK, DV), F32)],
        compiler_params=_params(("parallel", "arbitrary")),
    )(lg, *([px] * 5), do16, hist_f, *([px] * 5), do16, hist_b, *([px] * 3))


def _assemble_lat(dp_all, dk_f, dk_b, dv_f, dv_b, dak16, dvx, dq_f, dq_b, tm):
    t_rows = dk_f.shape[0]

    def body(_, dkf, dkb, dvf, dvb, dak, dav, dqf, dqb, o_ref):
        o_ref[:, RK:RK + RH * DK] = (dkf[...].astype(F32) + dkb[...].astype(F32)).astype(BF16)
        o_ref[:, RV:RV + RH * DV] = (dvf[...].astype(F32) + dvb[...].astype(F32)).astype(BF16)
        o_ref[:, AK:AK + HKV * HD] = dak[...]
        o_ref[:, AV:AV + HKV * HD] = dav[...]
        o_ref[:, RQ:RQ + RH * DK] = (dqf[...].astype(F32) + dqb[...].astype(F32)).astype(BF16)

    args = (dk_f, dk_b, dv_f, dv_b, dak16, dvx, dq_f, dq_b)
    return pl.pallas_call(
        body, name="assemble_lat", grid=(t_rows // tm,),
        in_specs=[pl.BlockSpec(memory_space=pl.ANY)]
        + [pl.BlockSpec((tm, a.shape[1]), lambda i: (i, 0)) for a in args],
        out_specs=pl.BlockSpec((tm, RG), lambda i: (i, 0)), out_shape=_sds(dp_all.shape, BF16),
        input_output_aliases={0: 0},
        compiler_params=_params(("parallel",)),
    )(dp_all, *args)


def _assemble_ctx(dp_all, dck16, dcv16, dcak16, dvc, t_rows, tm):
    c_rows = dck16.shape[0]
    rb = t_rows // tm

    def body(_, dck, dcv, dcak, dcav, o_ref):
        o_ref[:, RK:RK + RH * DK] = dck[...]
        o_ref[:, RV:RV + RH * DV] = dcv[...]
        o_ref[:, AK:AK + HKV * HD] = dcak[...]
        o_ref[:, AV:AV + HKV * HD] = dcav[...]
        o_ref[:, KV_COLS:] = jnp.zeros((tm, IN_COLS - KV_COLS), BF16)

    args = (dck16, dcv16, dcak16, dvc)
    return pl.pallas_call(
        body, name="assemble_ctx", grid=(c_rows // tm,),
        in_specs=[pl.BlockSpec(memory_space=pl.ANY)]
        + [pl.BlockSpec((tm, a.shape[1]), lambda i: (i, 0)) for a in args],
        out_specs=pl.BlockSpec((tm, IN_COLS), lambda i: (rb + i, 0)), out_shape=_sds(dp_all.shape, BF16),
        input_output_aliases={0: 0},
        compiler_params=_params(("parallel",)),
    )(dp_all, *args)


def _norm_bwd(dh, x2, mod3, norm_w, dxn, row_off, rows_per_group, group0, tm, name):
    with_dx = dxn is not None
    rows = x2.shape[0]
    rb0 = row_off // tm
    bpg = rows_per_group // tm
    ngroups = rows // rows_per_group

    def body(*refs):
        if with_dx:
            dh_ref, x_ref, sc_ref, nw_ref, dxn_ref, dx_ref, dsh_ref, dsc_ref, dnw_ref = refs
        else:
            dh_ref, x_ref, sc_ref, nw_ref, dsh_ref, dsc_ref, dnw_ref = refs
        i = pl.program_id(0)
        dhv = dh_ref[...]
        xv = x_ref[...]
        nw = nw_ref[...]
        r = lax.rsqrt(jnp.mean(xv * xv, axis=-1, keepdims=True) + EPS)
        xh = xv * r
        dm = dhv * (1.0 + sc_ref[...])
        dsh = jnp.sum(dhv, axis=0, keepdims=True)
        dsc = jnp.sum(dhv * (xh * nw), axis=0, keepdims=True)
        dnw = jnp.sum(dm * xh, axis=0, keepdims=True)
        if with_dx:
            dxh = dm * nw
            dx_ref[...] = dxn_ref[...] + r * (dxh - xh * jnp.mean(dxh * xh, axis=-1, keepdims=True))

        @pl.when(i % bpg == 0)
        def _():
            dsh_ref[...] = dsh
            dsc_ref[...] = dsc

        @pl.when(i % bpg != 0)
        def _():
            dsh_ref[...] += dsh
            dsc_ref[...] += dsc

        @pl.when(i == 0)
        def _():
            dnw_ref[...] = dnw

        @pl.when(i > 0)
        def _():
            dnw_ref[...] += dnw

    grp = pl.BlockSpec((None, 1, D), lambda i: (i // bpg, 0, 0))
    in_specs = [pl.BlockSpec((tm, D), lambda i: (rb0 + i, 0)), pl.BlockSpec((tm, D), lambda i: (i, 0)),
                pl.BlockSpec((None, 1, D), lambda i: (group0 + i // bpg, 0, 1)),
                pl.BlockSpec((1, D), lambda i: (0, 0))]
    args = [dh, x2, mod3, norm_w]
    out_specs = [grp, grp, pl.BlockSpec((1, D), lambda i: (0, 0))]
    out_shape = [_sds((ngroups, 1, D), F32), _sds((ngroups, 1, D), F32), _sds((1, D), F32)]
    if with_dx:
        in_specs.append(pl.BlockSpec((tm, D), lambda i: (i, 0)))
        args.append(dxn)
        out_specs.insert(0, pl.BlockSpec((tm, D), lambda i: (i, 0)))
        out_shape.insert(0, _sds((rows, D), F32))
    return pl.pallas_call(
        body, name=name, grid=(rows // tm,), in_specs=in_specs, out_specs=out_specs, out_shape=out_shape,
        compiler_params=_params(("arbitrary",)),
    )(*args)


def _small_final(dmod_all, dmodc_parts, c_rows, dm_loc_rows, nw_parts, misc_parts, c_ctx, r_pad, w_ada16):
    loc = dm_loc_rows.shape[1]

    def body(dm_ref, dmc_ref, c_ref, dml_ref, nwp_ref, mp_ref, cc_ref, r_ref, w_ref,
             gb_ref, gc_ref, gnw_ref, misc_ref, gwa_ref):
        dmc = jnp.sum(dmc_ref[...], axis=0, keepdims=True)
        gb_ref[...] = jnp.sum(dm_ref[...], axis=0, keepdims=True) + dmc
        dsc = _dot(jnp.broadcast_to(dmc, (8, 3 * D)).astype(BF16), w_ref[...], NT)[0:1, :]
        gc_ref[...] = dsc * _dsilu(cc_ref[...])
        gnw_ref[...] = jnp.sum(nwp_ref[...], axis=0, keepdims=True)
        misc = jnp.sum(mp_ref[...], axis=0, keepdims=True)
        y = jnp.exp2(r_ref[...])
        lane = lax.broadcasted_iota(jnp.int32, (1, D), 1)
        is_decay = jnp.logical_and(lane >= 2 * HD, lane < 2 * HD + 2 * RH)
        misc_ref[...] = misc * jnp.where(is_decay, -(y * np.float32(np.log(2.0))) / (1.0 - y), 1.0)
        gwa_ref[...] = _dot(_silu(c_ref[...]).astype(BF16), dml_ref[...].astype(BF16), TN)

    return pl.pallas_call(
        body, name="small_final",
        out_shape=[_sds((1, 3 * D), F32), _sds((1, D), F32), _sds((1, D), F32), _sds((1, D), F32), _sds((D, loc), F32)],
        compiler_params=pltpu.CompilerParams(vmem_limit_bytes=VMEM_LIMIT),
    )(dmod_all, dmodc_parts, c_rows, dm_loc_rows, nw_parts, misc_parts, c_ctx, r_pad, w_ada16)


def _adamw_math(w, g, m, v):
    nm = B1 * m + (1.0 - B1) * g
    nv = B2 * v + (1.0 - B2) * (g * g)
    return -LR * ((nm / (1.0 - B1 ** STEP)) / (jnp.sqrt(nv / (1.0 - B2 ** STEP)) + ADAM_EPS) + WD * w), nm, nv


def _adamw(w, g, m, v, name):
    rows, cols = w.shape
    tm = _pick(rows, 448, 8)

    def body(w_ref, g_ref, m_ref, v_ref, d_ref, nm_ref, nv_ref):
        d_ref[...], nm_ref[...], nv_ref[...] = _adamw_math(w_ref[...], g_ref[...], m_ref[...], v_ref[...])

    blk = pl.BlockSpec((tm, cols), lambda i: (i, 0))
    return pl.pallas_call(
        body, name=name, grid=(rows // tm,), in_specs=[blk] * 4, out_specs=[blk] * 3,
        out_shape=[_sds((rows, cols), F32)] * 3, compiler_params=_params(("parallel",)),
    )(w, g, m, v)


def _adamw_small(wgmv):
    n = len(wgmv)

    def body(*refs):
        ins, outs = refs[:4 * n], refs[4 * n:]
        for k in range(n):
            w, g, m, v = [r[...] for r in ins[4 * k:4 * k + 4]]
            outs[3 * k][...], outs[3 * k + 1][...], outs[3 * k + 2][...] = _adamw_math(w, g, m, v)

    out = pl.pallas_call(
        body, name="adamw_small", out_shape=[_sds(t[0].shape, F32) for t in wgmv for _ in range(3)],
    )(*[a for t in wgmv for a in t])
    return [out[3 * k:3 * k + 3] for k in range(n)]


def _mesh_pos():
    return lax.axis_index("x"), lax.axis_index("y"), lax.axis_index("c")


def _all_gather(arrs, name):
    n = len(arrs)

    def body(*refs):
        ins, outs = refs[:n], refs[n:2 * n]
        send_sems, recv_sems, local_sems = refs[2 * n:]
        x, y, c = _mesh_pos()
        me, sib = (x, y, c), (x, y, 1 - c)
        chips = [(1 - x, y), (x, 1 - y), (1 - x, 1 - y)]

        def slot(p):
            return 4 * p[0] + 2 * p[1] + p[2]

        def copy(a, k, block, to, own):
            dst = outs[a].at[slot(block)]
            return pltpu.make_async_remote_copy(
                src_ref=ins[a] if own else dst, dst_ref=dst, send_sem=send_sems.at[a, k], recv_sem=recv_sems.at[a, k],
                device_id=to, device_id_type=MESH_T)

        mine = [pltpu.make_async_copy(ins[a], outs[a].at[slot(me)], local_sems.at[a]) for a in range(n)]
        for cp in mine:
            cp.start()
        first = []
        for a in range(n):
            first.append(copy(a, 0, me, sib, True))
            first += [copy(a, 1 + j, me, (*chip, c), True) for j, chip in enumerate(chips)]
        for cp in first:
            cp.start()
        passed = []
        for j, chip in enumerate(chips):
            for a in range(n):
                copy(a, 1 + j, (*chip, c), me, False).wait_recv()
                fwd = copy(a, 4 + j, (*chip, c), sib, False)
                fwd.start()
                passed.append(fwd)
        for a in range(n):
            copy(a, 0, sib, me, False).wait_recv()
            for j, chip in enumerate(chips):
                copy(a, 4 + j, (*chip, 1 - c), me, False).wait_recv()
        for cp in first + passed:
            cp.wait_send()
        for cp in mine:
            cp.wait()

    hbm = pl.BlockSpec(memory_space=pl.ANY)
    return pl.pallas_call(
        body, name=name, in_specs=[hbm] * n, out_specs=[hbm] * n,
        out_shape=[_sds((N_DEV,) + a.shape, a.dtype) for a in arrs],
        scratch_shapes=[pltpu.SemaphoreType.DMA((n, 7)), pltpu.SemaphoreType.DMA((n, 7)), pltpu.SemaphoreType.DMA((n,))],
    )(*arrs)


def _pair_add(parts, gots, core, name):
    n = len(parts)
    cols = parts[0].shape[2]
    tiles = min(p.shape[1] for p in parts) // _pick(min(p.shape[1] for p in parts), 672, 16)

    def body(core_ref, *refs):
        for p_ref, g_ref, o_ref in zip(refs[:n], refs[n:2 * n], refs[2 * n:]):
            o_ref[...] = (p_ref[...].astype(F32) + g_ref[...].astype(F32)).astype(BF16)

    def blk(p):
        return pl.BlockSpec((None, p.shape[1] // tiles, cols), lambda k, i, cr: (k, i, 0))

    return pl.pallas_call(
        body, name=name,
        grid_spec=pltpu.PrefetchScalarGridSpec(
            num_scalar_prefetch=1, grid=(4, tiles),
            in_specs=[pl.BlockSpec((None, None, p.shape[1] // tiles, cols), lambda k, i, cr: (k, cr[0], i, 0))
                      for p in parts] + [blk(p) for p in parts],
            out_specs=[blk(p) for p in parts]),
        out_shape=[_sds((4,) + p.shape[1:], BF16) for p in parts], compiler_params=_params(("parallel", "parallel")),
    )(core, *[p.reshape(4, 2, *p.shape[1:]) for p in parts], *gots)


def _chip_sum_adamw(pair_sums, landed, chip, wmv, name):
    n = len(pair_sums)
    cols = pair_sums[0].shape[2]
    fewest = min(s_.shape[1] for s_ in pair_sums)
    tiles = fewest // _pick(fewest, 448, 16)

    def body(chip_ref, *refs):
        for k in range(n):
            s_ref, l_ref, w_ref, m_ref, v_ref = refs[5 * k:5 * k + 5]
            g_ref, d_ref, nm_ref, nv_ref = refs[5 * n + 4 * k:5 * n + 4 * k + 4]
            acc = s_ref[...].astype(F32)
            for j in range(3):
                acc = acc + l_ref[j].astype(F32)
            g_ref[...] = acc
            d_ref[...], nm_ref[...], nv_ref[...] = _adamw_math(w_ref[...], acc, m_ref[...], v_ref[...])

    in_specs, out_specs, out_shape, args = [], [], [], []
    for s_, l_, t in zip(pair_sums, landed, wmv):
        tm = s_.shape[1] // tiles
        blk = pl.BlockSpec((tm, cols), lambda i, ch: (i, 0))
        in_specs += [pl.BlockSpec((None, tm, cols), lambda i, ch: (ch[0], i, 0)),
                     pl.BlockSpec((3, tm, cols), lambda i, ch: (0, i, 0)), blk, blk, blk]
        out_specs += [blk] * 4
        out_shape += [_sds(s_.shape[1:], F32)] * 4
        args += [s_, l_, *t]
    out = pl.pallas_call(
        body, name=name,
        grid_spec=pltpu.PrefetchScalarGridSpec(num_scalar_prefetch=1, grid=(tiles,), in_specs=in_specs,
                                               out_specs=out_specs),
        out_shape=out_shape, compiler_params=_params(("parallel",)),
    )(chip, *args)
    return [out[4 * k:4 * k + 4] for k in range(n)]


_HBM = pl.BlockSpec(memory_space=pltpu.HBM)
_SEM = pl.BlockSpec(memory_space=pltpu.SEMAPHORE)
_EFFECT = pltpu.SideEffectType.DATAFLOW_SIDE_EFFECTING


def _chip_routes(n):
    def plan(x, y, c):
        routes = []
        for a in range(n):
            for j in range(1, 4):
                px, py = x ^ (j >> 1), y ^ (j & 1)
                routes.append((a, 2 * px + py, (px, py, c), j - 1))
        return routes
    return plan, 3 * n


def _pair_routes(n):
    def plan(x, y, c):
        return [(a, 2 * k + 1 - c, (x, y, 1 - c), k) for a in range(n) for k in range(4)]
    return plan, 4 * n


def _bcast_routes(n):
    def plan(x, y, c):
        routes = []
        for a in range(n):
            for k in range(1, N_DEV):
                peer = (x ^ ((k >> 2) & 1), y ^ ((k >> 1) & 1), c ^ (k & 1))
                routes.append((a, 0, peer, 4 * x + 2 * y + c))
        return routes
    return plan, 7 * n


def _route_copies(srcs, lands, send_sems, recv_sems, routes):
    return [pltpu.make_async_remote_copy(
        src_ref=srcs[a].at[sb], dst_ref=lands[a].at[lb], send_sem=send_sems.at[r], recv_sem=recv_sems.at[r],
        device_id=peer, device_id_type=MESH_T) for r, (a, sb, peer, lb) in enumerate(routes)]


def _exchange_start(srcs, lands, routes, name, after=()):
    plan, count = routes
    n = len(srcs)
    n_in = 2 * n + len(after)

    def body(*refs):
        send_sems, recv_sems = refs[n_in], refs[n_in + 1]
        token = refs[-1]
        for cp in _route_copies(refs[:n], refs[n:2 * n], send_sems, recv_sems, plan(*_mesh_pos())):
            cp.start()
        token[...] = jnp.zeros_like(token)

    args = [pltpu.with_memory_space_constraint(a, pltpu.HBM) for a in list(srcs) + list(lands)]
    out = pl.pallas_call(
        body, name=name,
        out_shape=(pltpu.SemaphoreType.DMA((count,)), pltpu.SemaphoreType.DMA((count,)),
                   *[pltpu.HBM(a.shape, a.dtype) for a in args], _sds((8, 128), F32)),
        in_specs=[_HBM] * (2 * n) + [pl.BlockSpec(memory_space=pl.ANY)] * len(after),
        out_specs=(_SEM, _SEM, *([_HBM] * (2 * n)), pl.BlockSpec(memory_space=pltpu.VMEM)),
        input_output_aliases={i: 2 + i for i in range(2 * n)},
        compiler_params=pltpu.CompilerParams(has_side_effects=_EFFECT),
    )(*args, *after)
    return (out[0], out[1], list(out[2:2 + 2 * n]), routes), out[-1]


def _exchange_wait(state, after, name):
    send_sems, recv_sems, bufs, (plan, count) = state
    n = len(bufs) // 2

    def body(*refs):
        send_s, recv_s = refs[2 * n], refs[2 * n + 1]
        for cp in _route_copies(refs[:n], refs[n:2 * n], send_s, recv_s, plan(*_mesh_pos())):
            cp.wait_send()
            cp.wait_recv()

    out = pl.pallas_call(
        body, name=name, out_shape=tuple(pltpu.HBM(a.shape, a.dtype) for a in bufs),
        in_specs=[_HBM] * (2 * n) + [_SEM, _SEM, pl.BlockSpec(memory_space=pl.ANY)], out_specs=tuple([_HBM] * (2 * n)),
        input_output_aliases={i: i for i in range(2 * n)},
        compiler_params=pltpu.CompilerParams(has_side_effects=_EFFECT),
    )(*bufs, send_sems, recv_sems, after)
    return list(out[:n]), list(out[n:])


def _group_routes(js):
    def plan(x, y, c):
        return [(0, 0, (x ^ (j >> 1), y ^ (j & 1), c), 2 * j + c) for j in js]
    return plan, len(js)


def _pair_fill(groups, js, name, after=()):
    def body(*refs):
        g_ref, send_sems, recv_sems = refs[-3:]
        x, y, c = _mesh_pos()
        sends = []
        for n, j in enumerate(js):
            mine = g_ref.at[2 * j + c]
            sends.append(pltpu.make_async_remote_copy(
                src_ref=mine, dst_ref=mine, send_sem=send_sems.at[n], recv_sem=recv_sems.at[n],
                device_id=(x, y, 1 - c), device_id_type=MESH_T))
        for cp in sends:
            cp.start()
        for n, j in enumerate(js):
            pltpu.make_async_remote_copy(
                src_ref=g_ref.at[2 * j + c], dst_ref=g_ref.at[2 * j + 1 - c], send_sem=send_sems.at[n],
                recv_sem=recv_sems.at[n], device_id=(x, y, 1 - c), device_id_type=MESH_T).wait_recv()
        for cp in sends:
            cp.wait_send()

    hbm = pl.BlockSpec(memory_space=pl.ANY)
    return pl.pallas_call(
        body, name=name, in_specs=[hbm] * (1 + len(after)), out_specs=hbm, out_shape=_sds(groups.shape, groups.dtype),
        input_output_aliases={0: 0},
        scratch_shapes=[pltpu.SemaphoreType.DMA((len(js),)), pltpu.SemaphoreType.DMA((len(js),))],
    )(groups, *after)


def _in_proj_group(h_all, groups, j0, ng, chip, px_prev, after, name):
    rows_all = h_all.shape[0]
    gcols = IN_COLS // 4
    tm = _pick(rows_all, 1536, 128)
    g4 = groups.reshape(4, gcols, D)

    n_lead = (1 if px_prev is not None else 0) + len(after)
    lead = ([px_prev] if px_prev is not None else []) + list(after)

    def body(chip_ref, *refs):
        h_ref, w_ref, o_ref = refs[n_lead:]
        o_ref[...] = _dot(h_ref[...], w_ref[...], NT).astype(BF16)

    return pl.pallas_call(
        body, name=name,
        grid_spec=pltpu.PrefetchScalarGridSpec(
            num_scalar_prefetch=1, grid=(ng, rows_all // tm),
            in_specs=[pl.BlockSpec(memory_space=pl.ANY)] * n_lead
            + [pl.BlockSpec((tm, D), lambda n, i, ch: (i, 0)),
               pl.BlockSpec((None, gcols, D), lambda n, i, ch: (j0 + n, 0, 0))],
            out_specs=pl.BlockSpec((tm, gcols), lambda n, i, ch: (i, ch[0] ^ (j0 + n)))),
        out_shape=_sds((rows_all, IN_COLS), BF16),
        input_output_aliases={1: 0} if px_prev is not None else {},
        compiler_params=_params(("parallel", "parallel")),
    )(chip, *lead, h_all, g4)


def _d_h_groups(dp_all, groups, chip, i0, ni, dh_prev, after):
    rows_all = dp_all.shape[0]
    gcols = IN_COLS // 4
    tm = _D_H_ROWS
    g4 = groups.reshape(4, gcols, D)
    lead = ([dh_prev] if dh_prev is not None else []) + list(after)
    n_lead = len(lead)

    def body(chip_ref, *refs):
        a_ref, w_ref, o_ref = refs[n_lead:]
        j = pl.program_id(1)
        part = _dot(a_ref[...], w_ref[...])

        @pl.when(j == 0)
        def _():
            o_ref[...] = part

        @pl.when(j > 0)
        def _():
            o_ref[...] += part

    return pl.pallas_call(
        body, name="d_h_%d" % i0,
        grid_spec=pltpu.PrefetchScalarGridSpec(
            num_scalar_prefetch=1, grid=(ni, 4),
            in_specs=[pl.BlockSpec(memory_space=pl.ANY)] * n_lead
            + [pl.BlockSpec((tm, gcols), lambda i, j, ch: (i0 + i, ch[0] ^ j)),
               pl.BlockSpec((None, gcols, D), lambda i, j, ch: (j, 0, 0))],
            out_specs=pl.BlockSpec((tm, D), lambda i, j, ch: (i0 + i, 0))),
        out_shape=_sds((rows_all, D), F32),
        input_output_aliases={1: 0} if dh_prev is not None else {},
        compiler_params=_params(("parallel", "arbitrary")),
    )(chip, *lead, dp_all, g4)


def _reduce_scatter_send(parts, got, core, name):
    sums = _pair_add(parts, got, core, name + "_add")
    lands = [lax.empty((3,) + s_.shape[1:], BF16) for s_ in sums]
    return _exchange_start(sums, lands, _chip_routes(len(sums)), name + "_start")


def _reduce_scatter_finish(rs_state, after, chip, wmv, name):
    sums, landed = _exchange_wait(rs_state, after, name + "_wait")
    return _chip_sum_adamw(sums, landed, chip, wmv, name + "_adamw")


def _local_step(x, c, ctx, norm_w, ret_log2_decay, q_norm_w, k_norm_w, loss_target,
                mod, proj_in, get_w_o, on_out_grads, on_in_grad, started=()):
    nb, seq, _ = x.shape
    cx = ctx.shape[1]
    t_rows, c_rows = nb * seq, nb * cx
    rows_all = t_rows + c_rows
    nc = seq // CH
    tm = _pick(seq, 256, 128)
    te = _pick(seq, 512, 128)
    assert cx % tm == 0 and t_rows % cx == 0 and seq % GRID_W == 0

    x2 = x.reshape(t_rows, D)
    ctx2 = ctx.reshape(c_rows, D)
    tgt = loss_target.reshape(t_rows, D)
    lg = _log_gamma(ret_log2_decay)
    cos, sin = _rope_tables(seq)

    mod3 = mod[:, None, :]
    h_all = _norm_fwd(x2, mod3, norm_w, rows_all, 0, seq, 0, None, te, "norm_fwd", after=started)
    h_all = _norm_fwd(ctx2, mod3, norm_w, rows_all, t_rows, c_rows, nb, h_all, tm, "norm_fwd_ctx")
    px = proj_in(h_all)
    o_f, o_b, hist_f, hist_b = _ret_fwd(px, lg, nb, nc, cx)
    q16 = _qk_prep(px, q_norm_w, cos, sin, t_rows, 0, AQ, HQ, 4, seq, te, "q_prep")
    kx16 = _qk_prep(px, k_norm_w, cos, sin, t_rows, 0, AK, HKV, HKV, seq, te, "k_prep")
    kc16 = _qk_prep(px, k_norm_w, None, None, c_rows, t_rows, AK, HKV, HKV, seq, tm, "kc_prep")
    o_att, yatt16, lse = _att_fwd(q16, kx16, kc16, px, nb, seq, cx, te)
    w_o_ret16, w_o_att16, w_out16 = get_w_o(lse)
    yret16, a_ret, a_att, dxn, dout16, dgate, loss_b, gw_out = _merge_out(
        o_f, o_b, yatt16, px, w_o_ret16, w_o_att16, w_out16, x2, tgt, mod3, nb, seq, tm)

    da_ret16, do16, dao16, delta, dp_all, gw_o_att = _bwd_branches(
        dout16, w_out16, w_o_ret16, w_o_att16, px, a_ret, a_att, o_f, o_b, o_att, yatt16, rows_all, tm)
    gw_o_ret = _matmul(yret16, da_ret16, ta=True, tm=D, tn=D, tk=D, out_dtype=BF16, name="gw_o_ret")
    out_send, out_started = on_out_grads([gw_o_ret, gw_o_att, gw_out])
    dp_all, gq, gk, dak16, dav16, dcak16, dcav16 = _att_bwd(q16, kx16, kc16, px, dao16, delta, lse, q_norm_w, k_norm_w,
                                                            cos, sin, dp_all, nb, seq, cx, 2 * te, after=out_started)
    out_state, out_sent = out_send(gq)
    dq_f, dk_f, dv_f, dq_b, dk_b, dv_b, dck16, dcv16, dlg_scan = _ret_bwd(px, lg, do16, hist_f, hist_b, nb, nc, cx)
    dp_all = _assemble_lat(dp_all, dk_f, dk_b, dv_f, dv_b, dak16, dav16, dq_f, dq_b, tm)
    dp_all = _assemble_ctx(dp_all, dck16, dcv16, dcak16, dcav16, t_rows, tm)
    gw_in_t = _matmul(dp_all, h_all, ta=True, tm=1536, tn=D, tk=2304, out_dtype=BF16, name="gw_in", after=out_sent)
    in_state, dh = on_in_grad(gw_in_t, dp_all)
    grad_x, dsh, dsc, gnw_lat = _norm_bwd(dh, x2, mod3, norm_w, dxn, 0, seq, 0, te, "norm_bwd")
    dsh_c, dsc_c, gnw_ctx = _norm_bwd(dh, ctx2, mod3, norm_w, None, t_rows, c_rows, nb, tm, "norm_bwd_ctx")

    dlg = jnp.sum(dlg_scan[:, :, 0], axis=0).reshape(1, 2 * RH)
    misc = jnp.concatenate([gq, gk, dlg, jnp.sum(loss_b[:, 0, 0]).reshape(1, 1),
                            jnp.zeros((1, D - 2 * HD - 2 * RH - 1), F32)], axis=1)
    rows = []
    for b in range(nb):
        rows += [dsh[b], dsc[b], dgate[b]]
    rows += [dsh_c[0], dsc_c[0]] + [c[b:b + 1] for b in range(nb)] + [gnw_lat + gnw_ctx, misc]
    payload = jnp.concatenate(rows + [jnp.zeros((PAY_ROWS - len(rows), D), F32)], axis=0)
    return grad_x.reshape(nb, seq, D), out_state, in_state, payload


def _finish_small(gathered, nb, c_ctx, ret_log2_decay, w_ada16, dev):
    n_dev = gathered.shape[0]
    loc = 3 * D // n_dev
    dmod_all = gathered[:, :3 * nb].reshape(n_dev * nb, 3 * D)
    dmodc_parts = jnp.concatenate([gathered[:, 3 * nb:3 * nb + 2].reshape(n_dev, 2 * D), jnp.zeros((n_dev, D), F32)], axis=1)
    c_all = gathered[:, 3 * nb + 2:4 * nb + 2].reshape(n_dev * nb, D)
    nw_parts = gathered[:, 4 * nb + 2]
    misc_parts = gathered[:, 4 * nb + 3]
    n_rows = n_dev * nb + n_dev
    pad = (-n_rows) % 16
    c_rows = jnp.concatenate([c_all, jnp.broadcast_to(c_ctx.reshape(1, D), (n_dev, D)), jnp.zeros((pad, D), F32)], axis=0)
    dm_rows = jnp.concatenate([dmod_all, dmodc_parts, jnp.zeros((pad, 3 * D), F32)], axis=0)
    dm_loc_rows = lax.dynamic_slice_in_dim(dm_rows, dev * loc, loc, axis=1)
    r_pad = jnp.full((1, D), -1.0, F32).at[:, 2 * HD:2 * HD + 2 * RH].set(ret_log2_decay.reshape(1, 2 * RH))
    gb, gc, gnw, misc, gwa = _small_final(dmod_all, dmodc_parts, c_rows, dm_loc_rows, nw_parts, misc_parts,
                                          c_ctx.reshape(1, D), r_pad, w_ada16)
    return (gb, gc, gnw, misc[:, :HD], misc[:, HD:2 * HD], misc[:, 2 * HD:2 * HD + 2 * RH], gwa,
            misc[0, 2 * HD + 2 * RH])


def kernel(x, c, ctx, c_ctx, norm_w, w_ada, b_ada, w_in, ret_log2_decay, q_norm_w, k_norm_w, w_o_ret, w_o_att, w_out, loss_target, m_c_ctx, m_norm_w, m_w_ada, m_b_ada, m_w_in, m_ret_log2_decay, m_q_norm_w, m_k_norm_w, m_w_o_ret, m_w_o_att, m_w_out, v_c_ctx, v_norm_w, v_w_ada, v_b_ada, v_w_in, v_ret_log2_decay, v_q_norm_w, v_k_norm_w, v_w_o_ret, v_w_o_att, v_w_out):
    nb = x.shape[0]
    mx, my, mc = _mesh_pos()
    dev = 4 * mx + 2 * my + mc
    core = jnp.reshape(mc, (1,)).astype(jnp.int32)
    chip = jnp.reshape(2 * mx + my, (1,)).astype(jnp.int32)

    n_loc = 3 * D // N_DEV
    c8 = jnp.zeros((8, D), F32).at[:nb].set(c).at[nb].set(c_ctx)
    c_land = lax.dynamic_update_slice(lax.empty((N_DEV, 8, D), F32), c8[None], (dev, 0, 0))
    c_state, c_token = _exchange_start([c8[None]], [c_land], _bcast_routes(1), "gather_c_start")
    w_in_t = jnp.transpose(w_in[0])
    in_shard = w_in_t.astype(BF16)
    groups = lax.dynamic_update_slice(lax.empty((N_DEV,) + in_shard.shape, BF16), in_shard[None], (mc, 0, 0))
    groups = _pair_fill(groups, (0,), "gather_in_pair", after=(c_token,))
    _, (c_all,) = _exchange_wait(c_state, groups, "gather_c_wait")
    ada_shard = w_ada[0].astype(BF16)
    b_loc = lax.dynamic_slice(b_ada, (0, dev * n_loc), (1, n_loc))
    mod_cols = _mod_part(c_all.reshape(N_DEV * 8, D), ada_shard, b_loc)
    (mod_all,) = _all_gather([mod_cols], "gather_mod")
    mod = jnp.transpose(lax.dynamic_slice(mod_all, (0, dev * 8, 0), (N_DEV, 8, n_loc)), (1, 0, 2)).reshape(8, 3 * D)
    ada_land = lax.dynamic_update_slice(lax.empty((N_DEV,) + ada_shard.shape, BF16), ada_shard[None], (dev, 0, 0))

    (near_send, near_recv, near_bufs, near_routes), gin_token = _exchange_start(
        [in_shard[None]], [groups], _group_routes((1, 2)), "gather_in_start", after=(mod_all,))
    w_in_groups, wo_states, ada_landed = [], [], []
    wo_shards = [w_[0].astype(BF16) for w_ in (w_o_ret, w_o_att, w_out)]
    wo_lands = [lax.dynamic_update_slice(lax.empty((N_DEV,) + s_.shape, BF16), s_[None], (dev, 0, 0)) for s_ in wo_shards]

    def _state(send, recv, src, groups, routes):
        return send, recv, [src, groups], routes

    def proj_in(h_all):
        src, groups = near_bufs
        px = _in_proj_group(h_all, groups, 0, 1, chip, None, (gin_token,), "in_proj_0")
        (src,), (groups,) = _exchange_wait(_state(near_send, near_recv, src, groups, near_routes), px,
                                           "gather_in_wait_near")
        groups = _pair_fill(groups, (1, 2), "gather_in_fill_near")
        (far_send, far_recv, (src, groups), far_routes), far_token = _exchange_start(
            [src], [groups], _group_routes((3,)), "gather_in_start_far")
        wo_state, wo_token = _exchange_start([s_[None] for s_ in wo_shards] + [ada_shard[None]], wo_lands + [ada_land],
                                             _bcast_routes(4), "gather_wo_start", after=(far_token,))
        wo_states.append(wo_state)
        px = _in_proj_group(h_all, groups, 1, 2, chip, px, (wo_token,), "in_proj_near")
        (src,), (groups,) = _exchange_wait(_state(far_send, far_recv, src, groups, far_routes), px,
                                           "gather_in_wait_far")
        groups = _pair_fill(groups, (3,), "gather_in_fill_far")
        px = _in_proj_group(h_all, groups, 3, 1, chip, px, (), "in_proj_far")
        w_in_groups.append(groups)
        return px

    def get_w_o(after):
        _, (l_ret, l_att, l_out, l_ada) = _exchange_wait(wo_states[0], after, "gather_wo_wait")
        ada_landed.append(l_ada)
        return l_ret.reshape(RH * DV, D), l_att.reshape(D, D), l_out.reshape(D, D)

    def on_out_grads(grads):
        parts = [g_.reshape(N_DEV, g_.shape[0] // N_DEV, D) for g_ in grads]
        lands = [lax.empty((4,) + p_.shape[1:], BF16) for p_ in parts]
        pair_state, pair_token = _exchange_start(parts, lands, _pair_routes(len(parts)), "rs_out_pair_start")

        def send(after):
            parts_, got = _exchange_wait(pair_state, after, "rs_out_pair_wait")
            state, token = _reduce_scatter_send(parts_, got, core, "rs_out")
            return state, (token,)

        return send, (pair_token,)

    def on_in_grad(grad, dp_all):
        parts = [grad.reshape(N_DEV, IN_COLS // N_DEV, D)]
        lands = [lax.empty((4,) + p_.shape[1:], BF16) for p_ in parts]
        pair_state, pair_token = _exchange_start(parts, lands, _pair_routes(1), "rs_in_pair_start")
        dh = _d_h_groups(dp_all, w_in_groups[0], chip, 0, 1, None, (pair_token,))
        parts, got = _exchange_wait(pair_state, dh, "rs_in_pair_wait")
        state, token = _reduce_scatter_send(parts, got, core, "rs_in")
        n_tiles = dp_all.shape[0] // _D_H_ROWS
        return state, _d_h_groups(dp_all, w_in_groups[0], chip, 1, n_tiles - 1, dh, (token,))

    grad_x, out_state, in_state, payload = _local_step(
        x, c, ctx, norm_w, ret_log2_decay, q_norm_w, k_norm_w, loss_target,
        mod, proj_in, get_w_o, on_out_grads, on_in_grad, started=(gin_token,))

    pay_land = lax.dynamic_update_slice(lax.empty((N_DEV,) + payload.shape, F32), payload[None], (dev, 0, 0))
    pay_state, pay_token = _exchange_start([payload[None]], [pay_land], _bcast_routes(1), "gather_small_start")

    out_res = _reduce_scatter_finish(out_state, pay_token, chip,
                                     [(w_[0], m_[0], v_[0]) for w_, m_, v_ in ((w_o_ret, m_w_o_ret, v_w_o_ret),
                                                                                (w_o_att, m_w_o_att, v_w_o_att),
                                                                                (w_out, m_w_out, v_w_out))], "rs_out")
    (in_res,) = _reduce_scatter_finish(in_state, out_res[0][0], chip,
                                       [(w_in_t, jnp.transpose(m_w_in[0]), jnp.transpose(v_w_in[0]))], "rs_in")

    _, (gathered,) = _exchange_wait(pay_state, in_res[0], "gather_small_wait")
    w_ada16 = jnp.transpose(ada_landed[0], (1, 0, 2)).reshape(D, 3 * D)
    gb, gc, gnw, gq, gk, gr, gwa, loss = _finish_small(gathered, nb, c_ctx, ret_log2_decay, w_ada16, dev)
    big = {4: [jnp.transpose(r)[None] for r in in_res]}
    for i, res in zip((8, 9, 10), out_res):
        big[i] = [r[None] for r in res]
    small_g = {0: gc.reshape(c_ctx.shape), 1: gnw, 2: gwa[None], 3: gb, 5: gr.reshape(ret_log2_decay.shape), 6: gq, 7: gk}
    weights = [c_ctx, norm_w, w_ada, b_ada, w_in, ret_log2_decay, q_norm_w, k_norm_w, w_o_ret, w_o_att, w_out]
    ms = [m_c_ctx, m_norm_w, m_w_ada, m_b_ada, m_w_in, m_ret_log2_decay, m_q_norm_w, m_k_norm_w, m_w_o_ret, m_w_o_att, m_w_out]
    vs = [v_c_ctx, v_norm_w, v_w_ada, v_b_ada, v_w_in, v_ret_log2_decay, v_q_norm_w, v_k_norm_w, v_w_o_ret, v_w_o_att, v_w_out]
    def rows2(i):
        return [a.reshape(-1, weights[i].shape[-1]) for a in (weights[i], small_g[i], ms[i], vs[i])]

    small_ids = [i for i in small_g if i != 2]
    steps = dict(zip(small_ids, _adamw_small([rows2(i) for i in small_ids])))
    steps[2] = _adamw(*rows2(2), "adamw_w_ada")
    grads, deltas, new_ms, new_vs = [], [], [], []
    for i, w in enumerate(weights):
        res = big[i] if i in big else [small_g[i]] + [r.reshape(w.shape) for r in steps[i]]
        for lst, r in zip((grads, deltas, new_ms, new_vs), res):
            lst.append(r)
    return (loss, grad_x, *grads, *deltas, *new_ms, *new_vs)
```

```python
import numpy as np
import jax
import jax.numpy as jnp
from jax import lax
from jax.experimental import pallas as pl
from jax.experimental.pallas import tpu as pltpu

F32 = jnp.float32
BF16 = jnp.bfloat16

D = 1024
RH, DK, DV, CH = 4, 256, 512, 256
HQ, HKV, HD = 8, 2, 128
GRID_W = 64
ROPE_THETA = 10000.0
EPS = 1e-6
RK, RV, AK, AV, RQ, RG, AQ, AG, MR, MA = 0, 1024, 3072, 3328, 3584, 4608, 6656, 7680, 8704, 9728
IN_COLS = 10752
KV_COLS = 3584
N_DEV = 8
LR, B1, B2, ADAM_EPS, WD, STEP = 0.001, 0.9, 0.999, 1e-08, 0.01, 10
PAY_ROWS = 16
VMEM_LIMIT = 56 * 1024 * 1024
_D_H_ROWS = 1536
MESH_T = pl.DeviceIdType.MESH

NT = (((1,), (1,)), ((), ()))
TN = (((0,), (0,)), ((), ()))
SM_C = (HD ** -0.5) * float(np.log2(np.e))


def _params(sem):
    return pltpu.CompilerParams(dimension_semantics=sem, vmem_limit_bytes=VMEM_LIMIT)


def _pick(n, target, mult=8):
    best = None
    for t in range(mult, min(n, target) + 1, mult):
        if n % t == 0:
            best = t
    return best or n


def _dot(a, b, dn=None):
    if dn is None:
        return jnp.dot(a, b, preferred_element_type=F32)
    return lax.dot_general(a, b, dn, preferred_element_type=F32)


def _sig(v):
    return jax.nn.sigmoid(v)


def _silu(v):
    return v * _sig(v)


def _dsilu(v):
    s = _sig(v)
    return s * (1.0 + v * (1.0 - s))


def _sds(shape, dtype):
    return jax.ShapeDtypeStruct(shape, dtype)


def _matmul(a, b, *, ta=False, tb=False, tm, tn, tk, out_dtype, name, after=()):
    m = a.shape[1] if ta else a.shape[0]
    kdim = a.shape[0] if ta else a.shape[1]
    n = b.shape[0] if tb else b.shape[1]
    tm, tn, tk = _pick(m, tm, 128), _pick(n, tn, 128), _pick(kdim, tk, 128)
    nk = kdim // tk
    dn = (((0 if ta else 1,), (1 if tb else 0,)), ((), ()))

    def body(a_ref, b_ref, *rest):
        o_ref, acc_ref = rest[-2:]
        k = pl.program_id(2)
        part = _dot(a_ref[...].astype(BF16), b_ref[...].astype(BF16), dn)
        if nk == 1:
            o_ref[...] = part.astype(o_ref.dtype)
        else:
            @pl.when(k == 0)
            def _():
                acc_ref[...] = part

            @pl.when(k > 0)
            def _():
                acc_ref[...] += part

            @pl.when(k == nk - 1)
            def _():
                o_ref[...] = acc_ref[...].astype(o_ref.dtype)

    a_spec = pl.BlockSpec((tk, tm), lambda i, j, k: (k, i)) if ta else pl.BlockSpec((tm, tk), lambda i, j, k: (i, k))
    b_spec = pl.BlockSpec((tn, tk), lambda i, j, k: (j, k)) if tb else pl.BlockSpec((tk, tn), lambda i, j, k: (k, j))
    return pl.pallas_call(
        body, name=name, grid=(m // tm, n // tn, nk),
        in_specs=[a_spec, b_spec] + [pl.BlockSpec(memory_space=pl.ANY)] * len(after),
        out_specs=pl.BlockSpec((tm, tn), lambda i, j, k: (i, j)), out_shape=_sds((m, n), out_dtype),
        scratch_shapes=[pltpu.VMEM((tm, tn) if nk > 1 else (8, 128), F32)],
        compiler_params=_params(("parallel", "parallel", "arbitrary")),
    )(a, b, *after)


def _log_gamma(r):
    rp = jnp.full((8, 128), -1.0, F32).at[:2, :RH].set(r.reshape(2, RH))

    def body(r_ref, o_ref):
        o_ref[...] = jnp.log1p(-jnp.exp2(r_ref[...]))

    out = pl.pallas_call(body, name="log_gamma", out_shape=_sds((8, 128), F32))(rp)
    return out[:2, :RH]


def _mod_part(c_rows, w_ada_loc16, b_loc):
    def body(c_ref, w_ref, b_ref, o_ref):
        o_ref[...] = _dot(_silu(c_ref[...]).astype(BF16), w_ref[...]) + b_ref[...]

    return pl.pallas_call(
        body, name="mod_part", out_shape=_sds((c_rows.shape[0], w_ada_loc16.shape[1]), F32),
    )(c_rows, w_ada_loc16, b_loc)


def _norm_fwd(x2, mod3, norm_w, rows_all, row_off, rows_per_group, group0, h_prev, tm, name, after=()):
    rows = x2.shape[0]
    rb0 = row_off // tm
    bpg = rows_per_group // tm

    def body(*refs):
        x_ref, sh_ref, sc_ref, nw_ref, o_ref = refs[-5:]
        xv = x_ref[...]
        r = lax.rsqrt(jnp.mean(xv * xv, axis=-1, keepdims=True) + EPS)
        o_ref[...] = ((xv * r) * nw_ref[...] * (1.0 + sc_ref[...]) + sh_ref[...]).astype(BF16)

    in_specs = [pl.BlockSpec((tm, D), lambda i: (i, 0)),
                pl.BlockSpec((None, 1, D), lambda i: (group0 + i // bpg, 0, 0)),
                pl.BlockSpec((None, 1, D), lambda i: (group0 + i // bpg, 0, 1)),
                pl.BlockSpec((1, D), lambda i: (0, 0))]
    in_specs = [pl.BlockSpec(memory_space=pl.ANY)] * len(after) + in_specs
    args = list(after) + [x2, mod3, mod3, norm_w]
    alias = {}
    if h_prev is not None:
        in_specs.insert(0, pl.BlockSpec(memory_space=pl.ANY))
        args.insert(0, h_prev)
        alias = {0: 0}
    return pl.pallas_call(
        body, name=name, grid=(rows // tm,), in_specs=in_specs,
        out_specs=pl.BlockSpec((tm, D), lambda i: (rb0 + i, 0)), out_shape=_sds((rows_all, D), BF16),
        input_output_aliases=alias, compiler_params=_params(("parallel",)),
    )(*args)


def _decays(lg, fwd):
    ii = lax.broadcasted_iota(jnp.int32, (CH, CH), 0)
    jj = lax.broadcasted_iota(jnp.int32, (CH, CH), 1)
    ri = lax.broadcasted_iota(jnp.int32, (CH, 1), 0).astype(F32)
    rel = (ii - jj) if fwd else (jj - ii)
    relf = jnp.maximum(rel, 0).astype(F32)
    mask = jnp.where(rel >= 0, jnp.exp(lg * relf), 0.0)
    qe = (ri + 1.0) if fwd else (CH - ri)
    ke = (CH - 1.0 - ri) if fwd else ri
    return mask, relf, jnp.exp(lg * qe), qe, jnp.exp(lg * ke), ke


def _wide_specs(rowf):
    return [pl.BlockSpec((CH, 2 * DK), lambda b, c: (rowf(b, c), RQ // (2 * DK))),
            pl.BlockSpec((CH, 2 * DK), lambda b, c: (rowf(b, c), RQ // (2 * DK) + 1)),
            pl.BlockSpec((CH, RH * DK), lambda b, c: (rowf(b, c), RK // (RH * DK))),
            pl.BlockSpec((CH, 2 * DV), lambda b, c: (rowf(b, c), RV // (2 * DV))),
            pl.BlockSpec((CH, 2 * DV), lambda b, c: (rowf(b, c), RV // (2 * DV) + 1))]


def _head_qkv(refs, h):
    q0, q1, k, v0, v1 = refs
    lo = h % 2
    q = (q0, q1)[h // 2][:, lo * DK:(lo + 1) * DK].astype(F32)
    kk = k[:, h * DK:(h + 1) * DK].astype(F32) * (DK ** -0.5)
    v16 = (v0, v1)[h // 2][:, lo * DV:(lo + 1) * DV].astype(BF16)
    return q, kk, v16


def _ctx_specs(t_rows, cx):
    rb = t_rows // cx
    return [pl.BlockSpec((cx, RH * DK), lambda b, c: (rb + b, RK // (RH * DK))),
            pl.BlockSpec((cx, 2 * DV), lambda b, c: (rb + b, RV // (2 * DV))),
            pl.BlockSpec((cx, 2 * DV), lambda b, c: (rb + b, RV // (2 * DV) + 1))]


def _ctx_kv(refs, h):
    k, v0, v1 = refs
    kk = k[:, h * DK:(h + 1) * DK].astype(F32) * (DK ** -0.5)
    lo = h % 2
    return kk, (v0, v1)[h // 2][:, lo * DV:(lo + 1) * DV].astype(BF16)


def _ret_fwd(px, lg, nb, nc, cx):
    t_rows = nb * nc * CH

    def body(lg_ref, *refs):
        ins = (refs[0:5], refs[5:10])
        ctx_refs = refs[10:13]
        of_ref, ob_ref, hf_ref, hb_ref, sf, sb = refs[13:]
        c = pl.program_id(1)

        @pl.when(c == 0)
        def _():
            pos = lax.broadcasted_iota(jnp.int32, (cx, 1), 0).astype(F32)
            for h in range(RH):
                k, v16 = _ctx_kv(ctx_refs, h)
                sf[h] = _dot((k * jnp.exp(lg_ref[0, h] * (cx - 1.0 - pos))).astype(BF16), v16, TN)
                sb[h] = _dot((k * jnp.exp(lg_ref[1, h] * pos)).astype(BF16), v16, TN)

        for d, (o_ref, h_ref, s) in enumerate(((of_ref, hf_ref, sf), (ob_ref, hb_ref, sb))):
            for h in range(RH):
                lg_d = lg_ref[d, h]
                mask, _, qd, _, kd, _ = _decays(lg_d, d == 0)
                q, k, v16 = _head_qkv(ins[d], h)
                a = _dot(q.astype(BF16), k.astype(BF16), NT)
                st = s[h]
                st16 = st.astype(BF16)
                h_ref[h] = st16
                o = _dot((a * mask).astype(BF16), v16) + _dot((q * qd).astype(BF16), st16)
                o_ref[:, h * DV:(h + 1) * DV] = o.astype(BF16)
                s[h] = st * jnp.exp(lg_d * CH) + _dot((k * kd).astype(BF16), v16, TN)

    def fw(b, c):
        return b * nc + c

    def bw(b, c):
        return b * nc + nc - 1 - c

    in_specs = [pl.BlockSpec(memory_space=pltpu.SMEM)] + _wide_specs(fw) + _wide_specs(bw) + _ctx_specs(t_rows, cx)
    out_specs = [pl.BlockSpec((CH, RH * DV), lambda b, c: (fw(b, c), 0)),
                 pl.BlockSpec((CH, RH * DV), lambda b, c: (bw(b, c), 0)),
                 pl.BlockSpec((None, None, RH, DK, DV), lambda b, c: (b, c, 0, 0, 0)),
                 pl.BlockSpec((None, None, RH, DK, DV), lambda b, c: (b, nc - 1 - c, 0, 0, 0))]
    return pl.pallas_call(
        body, name="ret_fwd", grid=(nb, nc), in_specs=in_specs, out_specs=out_specs,
        out_shape=[_sds((t_rows, RH * DV), BF16)] * 2 + [_sds((nb, nc, RH, DK, DV), BF16)] * 2,
        scratch_shapes=[pltpu.VMEM((RH, DK, DV), F32), pltpu.VMEM((RH, DK, DV), F32)],
        compiler_params=_params(("parallel", "arbitrary")),
    )(lg, *([px] * 13))


def _rope_tables(seq):
    rows = seq // GRID_W
    row = np.repeat(np.arange(rows, dtype=np.float32), GRID_W)
    col = np.tile(np.arange(GRID_W, dtype=np.float32), rows)
    half = HD // 2
    freqs = (ROPE_THETA ** (-np.arange(0, half, 2, dtype=np.float32) / half)).astype(np.float32)
    ang = np.concatenate([row[:, None] * freqs, col[:, None] * freqs], axis=-1).astype(np.float32)
    cos = np.repeat(np.cos(ang), 2, axis=-1).astype(np.float32)
    sin = np.repeat(np.sin(ang), 2, axis=-1).astype(np.float32)
    sign = np.tile(np.array([-1.0, 1.0], np.float32), HD // 2)
    return jnp.asarray(cos), jnp.asarray(sin * sign)


def _swap_pairs(v):
    lane = lax.broadcasted_iota(jnp.int32, v.shape, 1)
    return jnp.where((lane & 1) == 0, pltpu.roll(v, HD - 1, 1), pltpu.roll(v, 1, 1))


def _qk_prep(px, nw, cos, sin, rows, row_off, col_off, heads, hb, seq, tm, name):
    rope = cos is not None
    rb0 = row_off // tm
    pb = seq // tm if rope else 1
    bw = hb * HD

    def body(*refs):
        if rope:
            x_ref, w_ref, c_ref, s_ref, o_ref = refs
        else:
            x_ref, w_ref, o_ref = refs
        for h in range(hb):
            sl = slice(h * HD, (h + 1) * HD)
            xv = x_ref[:, sl].astype(F32)
            r = lax.rsqrt(jnp.mean(xv * xv, axis=-1, keepdims=True) + EPS)
            t = (xv * r) * w_ref[...]
            if rope:
                t = t * c_ref[...] + _swap_pairs(t) * s_ref[...]
            o_ref[:, sl] = t.astype(BF16)

    in_specs = [pl.BlockSpec((tm, bw), lambda i, j: (rb0 + i, col_off // bw + j)),
                pl.BlockSpec((1, HD), lambda i, j: (0, 0))]
    args = [px, nw]
    if rope:
        in_specs += [pl.BlockSpec((tm, HD), lambda i, j: (i % pb, 0))] * 2
        args += [cos, sin]
    return pl.pallas_call(
        body, name=name, grid=(rows // tm, heads // hb), in_specs=in_specs,
        out_specs=pl.BlockSpec((tm, bw), lambda i, j: (i, j)), out_shape=_sds((rows, heads * HD), BF16),
        compiler_params=_params(("parallel", "parallel")),
    )(*args)


def _att_fwd(q16, kx16, kc16, px, nb, seq, cx, tq):
    t_rows = nb * seq
    nq = seq // tq
    rep = HQ // HKV
    gw = rep * HD

    def body(q_ref, kx_ref, kc_ref, vx_ref, vc_ref, g_ref, o_ref, y_ref, l_ref):
        kx = kx_ref[...]
        kc = kc_ref[...]
        vx = vx_ref[...].astype(BF16)
        vc = vc_ref[...].astype(BF16)
        l_ref[...] = jnp.zeros_like(l_ref)
        for r in range(rep):
            sl = slice(r * HD, (r + 1) * HD)
            q = q_ref[:, sl]
            s1 = _dot(q, kx, NT)
            s2 = _dot(q, kc, NT)
            m = jnp.maximum(jnp.max(s1, axis=-1, keepdims=True), jnp.max(s2, axis=-1, keepdims=True))
            e1 = jnp.exp2((s1 - m) * SM_C)
            e2 = jnp.exp2((s2 - m) * SM_C)
            tot = jnp.sum(e1, axis=-1, keepdims=True) + jnp.sum(e2, axis=-1, keepdims=True)
            o = (_dot(e1.astype(BF16), vx) + _dot(e2.astype(BF16), vc)) * (1.0 / tot)
            o_ref[:, sl] = o
            y_ref[:, sl] = (o * _silu(g_ref[:, sl].astype(F32))).astype(BF16)
            l_ref[:, r:r + 1] = m * SM_C + jnp.log(tot) * float(np.log2(np.e))

    qblk = pl.BlockSpec((tq, gw), lambda b, g, i: (b * nq + i, g))
    return pl.pallas_call(
        body, name="att_fwd", grid=(nb, HKV, nq),
        in_specs=[qblk,
                  pl.BlockSpec((seq, HD), lambda b, g, i: (b, g)),
                  pl.BlockSpec((cx, HD), lambda b, g, i: (b, g)),
                  pl.BlockSpec((seq, HD), lambda b, g, i: (b, AV // HD + g)),
                  pl.BlockSpec((cx, HD), lambda b, g, i: (t_rows // cx + b, AV // HD + g)),
                  pl.BlockSpec((tq, gw), lambda b, g, i: (b * nq + i, AG // gw + g))],
        out_specs=[qblk, qblk, pl.BlockSpec((tq, 128), lambda b, g, i: (b * nq + i, g))],
        out_shape=[_sds((t_rows, D), F32), _sds((t_rows, D), BF16), _sds((t_rows, HKV * 128), F32)],
        compiler_params=_params(("parallel", "parallel", "parallel")),
    )(q16, kx16, kc16, px, px, px)


def _gate_specs(tm, col0):
    hw = D // 2
    return [pl.BlockSpec((tm, hw), lambda i: (i, col0 // hw)), pl.BlockSpec((tm, hw), lambda i: (i, col0 // hw + 1))]


def _merge_out(o_f, o_b, yatt16, px, w_o_ret16, w_o_att16, w_out16, x2, tgt, mod3, nb, seq, tm):
    t_rows = nb * seq
    bpb = seq // tm
    hw = D // 2

    def body(of_ref, ob_ref, g0, g1, g2, g3, wr_ref, ya_ref, wa_ref, mr0, mr1, ma0, ma1, wo_ref, x_ref, t_ref, gt_ref,
             yr_ref, ar_ref, aa_ref, dxn_ref, dout_ref, dg_ref, loss_ref, gw_ref, y_ref, acc_ref):
        i = pl.program_id(1)
        first = jnp.logical_and(pl.program_id(0) == 0, i == 0)
        for h, g_ref in enumerate((g0, g1, g2, g3)):
            sl = slice(h * DV, (h + 1) * DV)
            o = of_ref[:, sl].astype(F32) + ob_ref[:, sl].astype(F32)
            r = lax.rsqrt(jnp.mean(o * o, axis=-1, keepdims=True) + EPS)
            yr_ref[:, sl] = ((o * r) * _silu(g_ref[...].astype(F32))).astype(BF16)
        ar = _dot(yr_ref[...], wr_ref[...])
        aa = _dot(ya_ref[...], wa_ref[...])
        ar_ref[...] = ar.astype(BF16)
        aa_ref[...] = aa.astype(BF16)
        for j, (mr_ref, ma_ref) in enumerate(((mr0, ma0), (mr1, ma1))):
            sl = slice(j * hw, (j + 1) * hw)
            y_ref[:, sl] = (_sig(mr_ref[...].astype(F32)) * ar[:, sl]
                            + _sig(ma_ref[...].astype(F32)) * aa[:, sl]).astype(BF16)
        out = _dot(y_ref[...], wo_ref[...])
        gate = gt_ref[...]
        diff = x_ref[...] + gate * out - t_ref[...]
        dxn = diff * (1.0 / D)
        dxn_ref[...] = dxn
        dout_ref[...] = (gate * dxn).astype(BF16)

        @pl.when(first)
        def _():
            acc_ref[...] = jnp.zeros_like(acc_ref)

        for j in range(2):
            sl = slice(j * hw, (j + 1) * hw)
            acc_ref[sl, :] += _dot(y_ref[:, sl], dout_ref[...], TN)

        @pl.when(jnp.logical_and(pl.program_id(0) == nb - 1, i == bpb - 1))
        def _():
            gw_ref[...] = acc_ref[...].astype(BF16)

        dg = jnp.sum(dxn * out, axis=0, keepdims=True)
        ls = jnp.broadcast_to(jnp.sum(diff * diff) * (0.5 / D), (1, 128))

        @pl.when(i == 0)
        def _():
            dg_ref[...] = dg
            loss_ref[...] = ls

        @pl.when(i > 0)
        def _():
            dg_ref[...] += dg
            loss_ref[...] += ls

    def cols(width, col0):
        return pl.BlockSpec((tm, width), lambda b, i: (b * bpb + i, col0 // width))

    def whole(rows):
        return pl.BlockSpec((rows, D), lambda b, i: (0, 0))

    row, wide = cols(D, 0), cols(RH * DV, 0)
    gates = [cols(DV, RG + h * DV) for h in range(RH)]
    merge_gates = [cols(hw, MR), cols(hw, MR + hw), cols(hw, MA), cols(hw, MA + hw)]
    return pl.pallas_call(
        body, name="merge_out", grid=(nb, bpb),
        in_specs=[wide, wide] + gates + [whole(RH * DV), row, whole(D)] + merge_gates
        + [whole(D), row, row, pl.BlockSpec((None, 1, D), lambda b, i: (b, 0, 2))],
        out_specs=[wide, row, row, row, row, pl.BlockSpec((None, 1, D), lambda b, i: (b, 0, 0)),
                   pl.BlockSpec((None, 1, 128), lambda b, i: (b, 0, 0)), whole(D)],
        out_shape=[_sds((t_rows, RH * DV), BF16)] + [_sds((t_rows, D), BF16)] * 2
        + [_sds((t_rows, D), F32), _sds((t_rows, D), BF16), _sds((nb, 1, D), F32), _sds((nb, 1, 128), F32),
           _sds((D, D), BF16)],
        scratch_shapes=[pltpu.VMEM((tm, D), BF16), pltpu.VMEM((D, D), F32)],
        compiler_params=_params(("arbitrary", "arbitrary")),
    )(o_f, o_b, *([px] * RH), w_o_ret16, yatt16, w_o_att16, px, px, px, px, w_out16, x2, tgt, mod3)


def _bwd_branches(dout16, w_out16, w_o_ret16, w_o_att16, px, a_ret, a_att, o_f, o_b, o_att, yatt16, rows_all, tm):
    t_rows = dout16.shape[0]
    hw = D // 2

    def body(do_ref, wo_ref, wr_ref, wa_ref, mr0, mr1, ma0, ma1, ar_ref, aa_ref, rg0, rg1, rg2, rg3, of_ref, ob_ref,
             ag0, ag1, oa_ref, ya_ref, dar_ref, dor_ref, dao_ref, dl_ref, dp_ref, gwa_ref, daa_ref, acca_ref):
        dy_all = _dot(do_ref[...], wo_ref[...], NT)
        for j, (mr_ref, ma_ref) in enumerate(((mr0, ma0), (mr1, ma1))):
            sl = slice(j * hw, (j + 1) * hw)
            dy = dy_all[:, sl]
            sr = _sig(mr_ref[...].astype(F32))
            sa = _sig(ma_ref[...].astype(F32))
            dar_ref[:, sl] = (dy * sr).astype(BF16)
            daa_ref[:, sl] = (dy * sa).astype(BF16)
            dp_ref[:, MR - RG + j * hw:MR - RG + (j + 1) * hw] = (
                dy * ar_ref[:, sl].astype(F32) * sr * (1.0 - sr)).astype(BF16)
            dp_ref[:, MA - RG + j * hw:MA - RG + (j + 1) * hw] = (
                dy * aa_ref[:, sl].astype(F32) * sa * (1.0 - sa)).astype(BF16)
        da_ret = dar_ref[...]

        @pl.when(pl.program_id(0) == 0)
        def _():
            acca_ref[...] = jnp.zeros_like(acca_ref)

        for j in range(2):
            sl = slice(j * hw, (j + 1) * hw)
            acca_ref[sl, :] += _dot(ya_ref[:, sl], daa_ref[...], TN)

        for h, g_ref in enumerate((rg0, rg1, rg2, rg3)):
            sl = slice(h * DV, (h + 1) * DV)
            dy = _dot(da_ret, wr_ref[sl, :], NT)
            g = g_ref[...].astype(F32)
            o = of_ref[:, sl].astype(F32) + ob_ref[:, sl].astype(F32)
            r = lax.rsqrt(jnp.mean(o * o, axis=-1, keepdims=True) + EPS)
            on = o * r
            sg = _sig(g)
            don = dy * (g * sg)
            dp_ref[:, sl] = (dy * on * (sg * (1.0 + g * (1.0 - sg)))).astype(BF16)
            dor_ref[:, sl] = (r * (don - on * jnp.mean(on * don, axis=-1, keepdims=True))).astype(BF16)
        dy_all = _dot(daa_ref[...], wa_ref[...], NT)
        dl_ref[...] = jnp.zeros_like(dl_ref)
        for j, g_ref in enumerate((ag0, ag1)):
            sl = slice(j * hw, (j + 1) * hw)
            dy = dy_all[:, sl]
            g = g_ref[...].astype(F32)
            sg = _sig(g)
            dao = dy * (g * sg)
            dao_ref[:, sl] = dao.astype(BF16)
            prod = dao * oa_ref[:, sl]
            for r in range(hw // HD):
                dl_ref[:, j * 128 + r:j * 128 + r + 1] = jnp.sum(prod[:, r * HD:(r + 1) * HD], axis=-1, keepdims=True)
            dp_ref[:, AG - RG + j * hw:AG - RG + (j + 1) * hw] = (
                dy * oa_ref[:, sl] * (sg * (1.0 + g * (1.0 - sg)))).astype(BF16)

        @pl.when(pl.program_id(0) == t_rows // tm - 1)
        def _():
            gwa_ref[...] = acca_ref[...].astype(BF16)

    def gate(h):
        return pl.BlockSpec((tm, DV), lambda i: (i, RG // DV + h))

    def whole(rows):
        return pl.BlockSpec((rows, D), lambda i: (0, 0), pipeline_mode=pl.Buffered(1))

    row = pl.BlockSpec((tm, D), lambda i: (i, 0))
    wide = pl.BlockSpec((tm, RH * DV), lambda i: (i, 0))
    return pl.pallas_call(
        body, name="bwd_branches", grid=(t_rows // tm,),
        in_specs=[row, whole(D), whole(RH * DV), whole(D)] + _gate_specs(tm, MR) + _gate_specs(tm, MA) + [row, row]
        + [gate(h) for h in range(RH)] + [wide, wide] + _gate_specs(tm, AG) + [row, row],
        out_specs=[row, wide, row, pl.BlockSpec((tm, HKV * 128), lambda i: (i, 0)),
                   pl.BlockSpec((pl.Element(tm), pl.Element(IN_COLS - RG)), lambda i: (i * tm, RG)), whole(D)],
        out_shape=[_sds((t_rows, D), BF16), _sds((t_rows, RH * DV), BF16), _sds((t_rows, D), BF16),
                   _sds((t_rows, HKV * 128), F32), _sds((rows_all, IN_COLS), BF16), _sds((D, D), BF16)],
        scratch_shapes=[pltpu.VMEM((tm, D), BF16), pltpu.VMEM((D, D), F32)],
        compiler_params=_params(("arbitrary",)),
    )(dout16, w_out16, w_o_ret16, w_o_att16, px, px, px, px, a_ret, a_att, *([px] * RH), o_f, o_b, px, px, o_att,
      yatt16)


def _att_bwd(q16, kx16, kc16, px, dao16, delta, lse, q_norm_w, k_norm_w, cos, sin, dp_all, nb, seq, cx, tq, after=()):
    t_rows = nb * seq
    nq = seq // tq
    rep = HQ // HKV
    gw = rep * HD
    scale = HD ** -0.5

    def body(q_ref, kx_ref, kc_ref, vx_ref, vc_ref, dao_ref, dl_ref, l_ref, xq_ref, w_ref, c_ref, s_ref,
             xk_ref, xkc_ref, wk_ref, ck_ref, sk_ref, *rest):
        daq_ref, gq_ref, gk_ref, dkx_ref, dvx_ref, dkc_ref, dvc_ref = rest[1 + len(after):8 + len(after)]
        accs = rest[8 + len(after):]
        i = pl.program_id(2)
        head0 = jnp.logical_and(pl.program_id(0) == 0, pl.program_id(1) == 0)
        first = jnp.logical_and(head0, i == 0)
        gq = jnp.zeros((1, HD), F32)
        kx = kx_ref[...]
        kc = kc_ref[...]
        vx = vx_ref[...].astype(BF16)
        vc = vc_ref[...].astype(BF16)
        @pl.when(i == 0)
        def _():
            for acc in accs:
                acc[...] = jnp.zeros_like(acc)

        dkx, dvx, dkc, dvc = [acc[...] for acc in accs]
        for r in range(rep):
            sl = slice(r * HD, (r + 1) * HD)
            q = q_ref[:, sl]
            lr = l_ref[:, r:r + 1]
            p1 = jnp.exp2(_dot(q, kx, NT) * SM_C - lr)
            p2 = jnp.exp2(_dot(q, kc, NT) * SM_C - lr)
            da16 = dao_ref[:, sl]
            delta = dl_ref[:, r:r + 1]
            ds1 = (p1 * (_dot(da16, vx, NT) - delta)).astype(BF16)
            ds2 = (p2 * (_dot(da16, vc, NT) - delta)).astype(BF16)
            dq = (_dot(ds1, kx) + _dot(ds2, kc)) * scale
            dkx += _dot(q, ds1, TN)
            dkc += _dot(q, ds2, TN)
            dvx += _dot(da16, p1.astype(BF16), TN)
            dvc += _dot(da16, p2.astype(BF16), TN)
            dt = dq * c_ref[...] + _swap_pairs(dq * s_ref[...])
            xv = xq_ref[:, sl].astype(F32)
            rn = lax.rsqrt(jnp.mean(xv * xv, axis=-1, keepdims=True) + EPS)
            xh = xv * rn
            dxh = dt * w_ref[...]
            daq_ref[:, sl] = (rn * (dxh - xh * jnp.mean(dxh * xh, axis=-1, keepdims=True))).astype(BF16)
            gq += jnp.sum(dt * xh, axis=0, keepdims=True)
        for acc, val in zip(accs, (dkx, dvx, dkc, dvc)):
            acc[...] = val

        @pl.when(first)
        def _():
            gq_ref[...] = gq

        @pl.when(jnp.logical_not(first))
        def _():
            gq_ref[...] += gq

        def k_back(dk, x_ref, dk_ref):
            xv = x_ref[...].astype(F32)
            rn = lax.rsqrt(jnp.mean(xv * xv, axis=-1, keepdims=True) + EPS)
            xh = xv * rn
            dxh = dk * wk_ref[...]
            dk_ref[...] = (rn * (dxh - xh * jnp.mean(dxh * xh, axis=-1, keepdims=True))).astype(BF16)
            return jnp.sum(dk * xh, axis=0, keepdims=True)

        @pl.when(i == nq - 1)
        def _():
            dvx_ref[...] = dvx.T.astype(BF16)
            dvc_ref[...] = dvc.T.astype(BF16)
            dk = dkx.T * scale
            gk = (k_back(dk * ck_ref[...] + _swap_pairs(dk * sk_ref[...]), xk_ref, dkx_ref)
                  + k_back(dkc.T * scale, xkc_ref, dkc_ref))

            @pl.when(head0)
            def _():
                gk_ref[...] = gk

            @pl.when(jnp.logical_not(head0))
            def _():
                gk_ref[...] += gk

    qblk = pl.BlockSpec((tq, gw), lambda b, g, i: (b * nq + i, g))
    kxb = pl.BlockSpec((seq, HD), lambda b, g, i: (b, g))
    kcb = pl.BlockSpec((cx, HD), lambda b, g, i: (b, g))
    table = pl.BlockSpec((tq, HD), lambda b, g, i: (i, 0))
    tables = pl.BlockSpec((seq, HD), lambda b, g, i: (0, 0))
    one = pl.BlockSpec((1, HD), lambda b, g, i: (0, 0))
    lane = pl.BlockSpec((tq, 128), lambda b, g, i: (b * nq + i, g))
    return pl.pallas_call(
        body, name="att_bwd", grid=(nb, HKV, nq),
        in_specs=[qblk,
                  pl.BlockSpec((seq, HD), lambda b, g, i: (b, g)),
                  pl.BlockSpec((cx, HD), lambda b, g, i: (b, g)),
                  pl.BlockSpec((seq, HD), lambda b, g, i: (b, AV // HD + g)),
                  pl.BlockSpec((cx, HD), lambda b, g, i: (t_rows // cx + b, AV // HD + g)),
                  qblk, lane, lane,
                  pl.BlockSpec((tq, gw), lambda b, g, i: (b * nq + i, AQ // gw + g)), one, table, table,
                  pl.BlockSpec((seq, HD), lambda b, g, i: (b, AK // HD + g)),
                  pl.BlockSpec((cx, HD), lambda b, g, i: (t_rows // cx + b, AK // HD + g)), one, tables, tables]
        + [pl.BlockSpec(memory_space=pl.ANY)] * (1 + len(after)),
        out_specs=[pl.BlockSpec((tq, gw), lambda b, g, i: (b * nq + i, AQ // gw + g)), one, one, kxb, kxb, kcb, kcb],
        out_shape=[_sds(dp_all.shape, BF16), _sds((1, HD), F32), _sds((1, HD), F32), _sds((t_rows, HKV * HD), BF16),
                   _sds((t_rows, HKV * HD), BF16), _sds((nb * cx, HKV * HD), BF16), _sds((nb * cx, HKV * HD), BF16)],
        scratch_shapes=[pltpu.VMEM((HD, seq), F32), pltpu.VMEM((HD, seq), F32), pltpu.VMEM((HD, cx), F32),
                        pltpu.VMEM((HD, cx), F32)],
        input_output_aliases={17: 0},
        compiler_params=_params(("arbitrary", "arbitrary", "arbitrary")),
    )(q16, kx16, kc16, px, px, dao16, delta, lse, px, q_norm_w, cos, sin, px, px, k_norm_w, cos, sin, dp_all, *after)


def _ret_bwd(px, lg, do16, hist_f, hist_b, nb, nc, cx):
    t_rows = nb * nc * CH

    def body(lg_ref, *refs):
        ins = (refs[0:5], refs[7:12])
        do_refs = (refs[5], refs[12])
        h_refs = (refs[6], refs[13])
        ctx_refs = refs[14:17]
        outs = (refs[17:20], refs[20:23])
        dck_ref, dcv_ref, dlg_ref = refs[23:26]
        dss = (refs[26], refs[27])
        c = pl.program_id(1)

        @pl.when(c == 0)
        def _():
            dss[0][...] = jnp.zeros_like(dss[0])
            dss[1][...] = jnp.zeros_like(dss[1])
            dlg_ref[...] = jnp.zeros_like(dlg_ref)

        for d in range(2):
            dq_ref, dk_ref, dv_ref = outs[d]
            for h in range(RH):
                lg_d = lg_ref[d, h]
                mask, relf, qd, qe, kd, ke = _decays(lg_d, d == 0)
                g_ch = jnp.exp(lg_d * CH)
                q, k, v16 = _head_qkv(ins[d], h)
                q16 = q.astype(BF16)
                k16 = k.astype(BF16)
                do16v = do_refs[d][:, h * DV:(h + 1) * DV]
                st16 = h_refs[d][h]
                dst = dss[d][h]
                dst16 = dst.astype(BF16)
                a = _dot(q16, k16, NT) * mask
                dp = _dot(do16v, v16, NT)
                da16 = (dp * mask).astype(BF16)
                dq_cross = _dot(do16v, st16, NT) * qd
                dq_ref[:, h * DK:(h + 1) * DK] = (_dot(da16, k16) + dq_cross).astype(BF16)
                dk_state = _dot(v16, dst16, NT) * kd
                dk_ref[:, h * DK:(h + 1) * DK] = ((_dot(da16, q16, TN) + dk_state) * (DK ** -0.5)).astype(BF16)
                dv = _dot(a.astype(BF16), do16v, TN) + _dot((k * kd).astype(BF16), dst16)
                dv_ref[:, h * DV:(h + 1) * DV] = dv.astype(BF16)
                dlg = (jnp.sum(relf * a * dp)
                       + jnp.sum(qe * jnp.sum(q * dq_cross, axis=-1, keepdims=True))
                       + jnp.sum(ke * jnp.sum(k * dk_state, axis=-1, keepdims=True))
                       + CH * g_ch * jnp.sum(dst * st16.astype(F32)))
                row = d * RH + h
                dlg_ref[row:row + 1, :] += jnp.broadcast_to(dlg, (1, 128))
                dss[d][h] = g_ch * dst + _dot((q * qd).astype(BF16), do16v, TN)

        @pl.when(c == nc - 1)
        def _():
            pos = lax.broadcasted_iota(jnp.int32, (cx, 1), 0).astype(F32)
            for h in range(RH):
                k, v16 = _ctx_kv(ctx_refs, h)
                dk = jnp.zeros((cx, DK), F32)
                dv = jnp.zeros((cx, DV), F32)
                for d, e in enumerate((cx - 1.0 - pos, pos)):
                    w = jnp.exp(lg_ref[d, h] * e)
                    ds16 = dss[d][h].astype(BF16)
                    t = _dot(v16, ds16, NT)
                    dk += t * w
                    dv += _dot((k * w).astype(BF16), ds16)
                    dlg = jnp.sum(e * w * jnp.sum(k * t, axis=-1, keepdims=True))
                    row = d * RH + h
                    dlg_ref[row:row + 1, :] += jnp.broadcast_to(dlg, (1, 128))
                dck_ref[:, h * DK:(h + 1) * DK] = (dk * (DK ** -0.5)).astype(BF16)
                dcv_ref[:, h * DV:(h + 1) * DV] = dv.astype(BF16)

    def fw(b, c):
        return b * nc + nc - 1 - c

    def bw(b, c):
        return b * nc + c

    def rows(rowf, width):
        return pl.BlockSpec((CH, width), lambda b, c: (rowf(b, c), 0))

    def hist(rowf):
        return pl.BlockSpec((None, None, RH, DK, DV), lambda b, c: (b, rowf(0, c), 0, 0, 0))

    in_specs = [pl.BlockSpec(memory_space=pltpu.SMEM)]
    out_specs = []
    for rowf in (fw, bw):
        in_specs += _wide_specs(rowf) + [rows(rowf, RH * DV), hist(rowf)]
        out_specs += [rows(rowf, RH * DK), rows(rowf, RH * DK), rows(rowf, RH * DV)]
    in_specs += _ctx_specs(t_rows, cx)
    out_specs += [pl.BlockSpec((cx, RH * DK), lambda b, c: (b, 0)), pl.BlockSpec((cx, RH * DV), lambda b, c: (b, 0)),
                  pl.BlockSpec((None, 8, 128), lambda b, c: (b, 0, 0))]
    qk = _sds((t_rows, RH * DK), BF16)
    vv = _sds((t_rows, RH * DV), BF16)
    return pl.pallas_call(
        body, name="ret_bwd", grid=(nb, nc), in_specs=in_specs, out_specs=out_specs,
        out_shape=[qk, qk, vv, qk, qk, vv, _sds((nb * cx, RH * DK), BF16), _sds((nb * cx, RH * DV), BF16),
                   _sds((nb, 8, 128), F32)],
        scratch_shapes=[pltpu.VMEM((RH, DK, DV), F32), pltpu.VMEM((RH, DK, DV), F32)],
        compiler_params=_params(("parallel", "arbitrary")),
    )(lg, *([px] * 5), do16, hist_f, *([px] * 5), do16, hist_b, *([px] * 3))


def _assemble_lat(dp_all, dk_f, dk_b, dv_f, dv_b, dak16, dvx, dq_f, dq_b, tm):
    t_rows = dk_f.shape[0]

    def body(_, dkf, dkb, dvf, dvb, dak, dav, dqf, dqb, o_ref):
        o_ref[:, RK:RK + RH * DK] = (dkf[...].astype(F32) + dkb[...].astype(F32)).astype(BF16)
        o_ref[:, RV:RV + RH * DV] = (dvf[...].astype(F32) + dvb[...].astype(F32)).astype(BF16)
        o_ref[:, AK:AK + HKV * HD] = dak[...]
        o_ref[:, AV:AV + HKV * HD] = dav[...]
        o_ref[:, RQ:RQ + RH * DK] = (dqf[...].astype(F32) + dqb[...].astype(F32)).astype(BF16)

    args = (dk_f, dk_b, dv_f, dv_b, dak16, dvx, dq_f, dq_b)
    return pl.pallas_call(
        body, name="assemble_lat", grid=(t_rows // tm,),
        in_specs=[pl.BlockSpec(memory_space=pl.ANY)]
        + [pl.BlockSpec((tm, a.shape[1]), lambda i: (i, 0)) for a in args],
        out_specs=pl.BlockSpec((tm, RG), lambda i: (i, 0)), out_shape=_sds(dp_all.shape, BF16),
        input_output_aliases={0: 0},
        compiler_params=_params(("parallel",)),
    )(dp_all, *args)


def _assemble_ctx(dp_all, dck16, dcv16, dcak16, dvc, t_rows, tm):
    c_rows = dck16.shape[0]
    rb = t_rows // tm

    def body(_, dck, dcv, dcak, dcav, o_ref):
        o_ref[:, RK:RK + RH * DK] = dck[...]
        o_ref[:, RV:RV + RH * DV] = dcv[...]
        o_ref[:, AK:AK + HKV * HD] = dcak[...]
        o_ref[:, AV:AV + HKV * HD] = dcav[...]
        o_ref[:, KV_COLS:] = jnp.zeros((tm, IN_COLS - KV_COLS), BF16)

    args = (dck16, dcv16, dcak16, dvc)
    return pl.pallas_call(
        body, name="assemble_ctx", grid=(c_rows // tm,),
        in_specs=[pl.BlockSpec(memory_space=pl.ANY)]
        + [pl.BlockSpec((tm, a.shape[1]), lambda i: (i, 0)) for a in args],
        out_specs=pl.BlockSpec((tm, IN_COLS), lambda i: (rb + i, 0)), out_shape=_sds(dp_all.shape, BF16),
        input_output_aliases={0: 0},
        compiler_params=_params(("parallel",)),
    )(dp_all, *args)


def _norm_bwd(dh, x2, mod3, norm_w, dxn, row_off, rows_per_group, group0, tm, name):
    with_dx = dxn is not None
    rows = x2.shape[0]
    rb0 = row_off // tm
    bpg = rows_per_group // tm
    ngroups = rows // rows_per_group

    def body(*refs):
        if with_dx:
            dh_ref, x_ref, sc_ref, nw_ref, dxn_ref, dx_ref, dsh_ref, dsc_ref, dnw_ref = refs
        else:
            dh_ref, x_ref, sc_ref, nw_ref, dsh_ref, dsc_ref, dnw_ref = refs
        i = pl.program_id(0)
        dhv = dh_ref[...]
        xv = x_ref[...]
        nw = nw_ref[...]
        r = lax.rsqrt(jnp.mean(xv * xv, axis=-1, keepdims=True) + EPS)
        xh = xv * r
        dm = dhv * (1.0 + sc_ref[...])
        dsh = jnp.sum(dhv, axis=0, keepdims=True)
        dsc = jnp.sum(dhv * (xh * nw), axis=0, keepdims=True)
        dnw = jnp.sum(dm * xh, axis=0, keepdims=True)
        if with_dx:
            dxh = dm * nw
            dx_ref[...] = dxn_ref[...] + r * (dxh - xh * jnp.mean(dxh * xh, axis=-1, keepdims=True))

        @pl.when(i % bpg == 0)
        def _():
            dsh_ref[...] = dsh
            dsc_ref[...] = dsc

        @pl.when(i % bpg != 0)
        def _():
            dsh_ref[...] += dsh
            dsc_ref[...] += dsc

        @pl.when(i == 0)
        def _():
            dnw_ref[...] = dnw

        @pl.when(i > 0)
        def _():
            dnw_ref[...] += dnw

    grp = pl.BlockSpec((None, 1, D), lambda i: (i // bpg, 0, 0))
    in_specs = [pl.BlockSpec((tm, D), lambda i: (rb0 + i, 0)), pl.BlockSpec((tm, D), lambda i: (i, 0)),
                pl.BlockSpec((None, 1, D), lambda i: (group0 + i // bpg, 0, 1)),
                pl.BlockSpec((1, D), lambda i: (0, 0))]
    args = [dh, x2, mod3, norm_w]
    out_specs = [grp, grp, pl.BlockSpec((1, D), lambda i: (0, 0))]
    out_shape = [_sds((ngroups, 1, D), F32), _sds((ngroups, 1, D), F32), _sds((1, D), F32)]
    if with_dx:
        in_specs.append(pl.BlockSpec((tm, D), lambda i: (i, 0)))
        args.append(dxn)
        out_specs.insert(0, pl.BlockSpec((tm, D), lambda i: (i, 0)))
        out_shape.insert(0, _sds((rows, D), F32))
    return pl.pallas_call(
        body, name=name, grid=(rows // tm,), in_specs=in_specs, out_specs=out_specs, out_shape=out_shape,
        compiler_params=_params(("arbitrary",)),
    )(*args)


def _small_final(dmod_all, dmodc_parts, c_rows, dm_loc_rows, nw_parts, misc_parts, c_ctx, r_pad, w_ada16):
    loc = dm_loc_rows.shape[1]

    def body(dm_ref, dmc_ref, c_ref, dml_ref, nwp_ref, mp_ref, cc_ref, r_ref, w_ref,
             gb_ref, gc_ref, gnw_ref, misc_ref, gwa_ref):
        dmc = jnp.sum(dmc_ref[...], axis=0, keepdims=True)
        gb_ref[...] = jnp.sum(dm_ref[...], axis=0, keepdims=True) + dmc
        dsc = _dot(jnp.broadcast_to(dmc, (8, 3 * D)).astype(BF16), w_ref[...], NT)[0:1, :]
        gc_ref[...] = dsc * _dsilu(cc_ref[...])
        gnw_ref[...] = jnp.sum(nwp_ref[...], axis=0, keepdims=True)
        misc = jnp.sum(mp_ref[...], axis=0, keepdims=True)
        y = jnp.exp2(r_ref[...])
        lane = lax.broadcasted_iota(jnp.int32, (1, D), 1)
        is_decay = jnp.logical_and(lane >= 2 * HD, lane < 2 * HD + 2 * RH)
        misc_ref[...] = misc * jnp.where(is_decay, -(y * np.float32(np.log(2.0))) / (1.0 - y), 1.0)
        gwa_ref[...] = _dot(_silu(c_ref[...]).astype(BF16), dml_ref[...].astype(BF16), TN)

    return pl.pallas_call(
        body, name="small_final",
        out_shape=[_sds((1, 3 * D), F32), _sds((1, D), F32), _sds((1, D), F32), _sds((1, D), F32), _sds((D, loc), F32)],
        compiler_params=pltpu.CompilerParams(vmem_limit_bytes=VMEM_LIMIT),
    )(dmod_all, dmodc_parts, c_rows, dm_loc_rows, nw_parts, misc_parts, c_ctx, r_pad, w_ada16)


def _adamw_math(w, g, m, v):
    nm = B1 * m + (1.0 - B1) * g
    nv = B2 * v + (1.0 - B2) * (g * g)
    return -LR * ((nm / (1.0 - B1 ** STEP)) / (jnp.sqrt(nv / (1.0 - B2 ** STEP)) + ADAM_EPS) + WD * w), nm, nv


def _adamw(w, g, m, v, name):
    rows, cols = w.shape
    tm = _pick(rows, 448, 8)

    def body(w_ref, g_ref, m_ref, v_ref, d_ref, nm_ref, nv_ref):
        d_ref[...], nm_ref[...], nv_ref[...] = _adamw_math(w_ref[...], g_ref[...], m_ref[...], v_ref[...])

    blk = pl.BlockSpec((tm, cols), lambda i: (i, 0))
    return pl.pallas_call(
        body, name=name, grid=(rows // tm,), in_specs=[blk] * 4, out_specs=[blk] * 3,
        out_shape=[_sds((rows, cols), F32)] * 3, compiler_params=_params(("parallel",)),
    )(w, g, m, v)


def _adamw_small(wgmv):
    n = len(wgmv)

    def body(*refs):
        ins, outs = refs[:4 * n], refs[4 * n:]
        for k in range(n):
            w, g, m, v = [r[...] for r in ins[4 * k:4 * k + 4]]
            outs[3 * k][...], outs[3 * k + 1][...], outs[3 * k + 2][...] = _adamw_math(w, g, m, v)

    out = pl.pallas_call(
        body, name="adamw_small", out_shape=[_sds(t[0].shape, F32) for t in wgmv for _ in range(3)],
    )(*[a for t in wgmv for a in t])
    return [out[3 * k:3 * k + 3] for k in range(n)]


def _mesh_pos():
    return lax.axis_index("x"), lax.axis_index("y"), lax.axis_index("c")


def _all_gather(arrs, name):
    n = len(arrs)

    def body(*refs):
        ins, outs = refs[:n], refs[n:2 * n]
        send_sems, recv_sems, local_sems = refs[2 * n:]
        x, y, c = _mesh_pos()
        me, sib = (x, y, c), (x, y, 1 - c)
        chips = [(1 - x, y), (x, 1 - y), (1 - x, 1 - y)]

        def slot(p):
            return 4 * p[0] + 2 * p[1] + p[2]

        def copy(a, k, block, to, own):
            dst = outs[a].at[slot(block)]
            return pltpu.make_async_remote_copy(
                src_ref=ins[a] if own else dst, dst_ref=dst, send_sem=send_sems.at[a, k], recv_sem=recv_sems.at[a, k],
                device_id=to, device_id_type=MESH_T)

        mine = [pltpu.make_async_copy(ins[a], outs[a].at[slot(me)], local_sems.at[a]) for a in range(n)]
        for cp in mine:
            cp.start()
        first = []
        for a in range(n):
            first.append(copy(a, 0, me, sib, True))
            first += [copy(a, 1 + j, me, (*chip, c), True) for j, chip in enumerate(chips)]
        for cp in first:
            cp.start()
        passed = []
        for j, chip in enumerate(chips):
            for a in range(n):
                copy(a, 1 + j, (*chip, c), me, False).wait_recv()
                fwd = copy(a, 4 + j, (*chip, c), sib, False)
                fwd.start()
                passed.append(fwd)
        for a in range(n):
            copy(a, 0, sib, me, False).wait_recv()
            for j, chip in enumerate(chips):
                copy(a, 4 + j, (*chip, 1 - c), me, False).wait_recv()
        for cp in first + passed:
            cp.wait_send()
        for cp in mine:
            cp.wait()

    hbm = pl.BlockSpec(memory_space=pl.ANY)
    return pl.pallas_call(
        body, name=name, in_specs=[hbm] * n, out_specs=[hbm] * n,
        out_shape=[_sds((N_DEV,) + a.shape, a.dtype) for a in arrs],
        scratch_shapes=[pltpu.SemaphoreType.DMA((n, 7)), pltpu.SemaphoreType.DMA((n, 7)), pltpu.SemaphoreType.DMA((n,))],
    )(*arrs)


def _pair_add(parts, gots, core, name):
    n = len(parts)
    cols = parts[0].shape[2]
    tiles = min(p.shape[1] for p in parts) // _pick(min(p.shape[1] for p in parts), 672, 16)

    def body(core_ref, *refs):
        for p_ref, g_ref, o_ref in zip(refs[:n], refs[n:2 * n], refs[2 * n:]):
            o_ref[...] = (p_ref[...].astype(F32) + g_ref[...].astype(F32)).astype(BF16)

    def blk(p):
        return pl.BlockSpec((None, p.shape[1] // tiles, cols), lambda k, i, cr: (k, i, 0))

    return pl.pallas_call(
        body, name=name,
        grid_spec=pltpu.PrefetchScalarGridSpec(
            num_scalar_prefetch=1, grid=(4, tiles),
            in_specs=[pl.BlockSpec((None, None, p.shape[1] // tiles, cols), lambda k, i, cr: (k, cr[0], i, 0))
                      for p in parts] + [blk(p) for p in parts],
            out_specs=[blk(p) for p in parts]),
        out_shape=[_sds((4,) + p.shape[1:], BF16) for p in parts], compiler_params=_params(("parallel", "parallel")),
    )(core, *[p.reshape(4, 2, *p.shape[1:]) for p in parts], *gots)


def _chip_sum_adamw(pair_sums, landed, chip, wmv, name):
    n = len(pair_sums)
    cols = pair_sums[0].shape[2]
    fewest = min(s_.shape[1] for s_ in pair_sums)
    tiles = fewest // _pick(fewest, 448, 16)

    def body(chip_ref, *refs):
        for k in range(n):
            s_ref, l_ref, w_ref, m_ref, v_ref = refs[5 * k:5 * k + 5]
            g_ref, d_ref, nm_ref, nv_ref = refs[5 * n + 4 * k:5 * n + 4 * k + 4]
            acc = s_ref[...].astype(F32)
            for j in range(3):
                acc = acc + l_ref[j].astype(F32)
            g_ref[...] = acc
            d_ref[...], nm_ref[...], nv_ref[...] = _adamw_math(w_ref[...], acc, m_ref[...], v_ref[...])

    in_specs, out_specs, out_shape, args = [], [], [], []
    for s_, l_, t in zip(pair_sums, landed, wmv):
        tm = s_.shape[1] // tiles
        blk = pl.BlockSpec((tm, cols), lambda i, ch: (i, 0))
        in_specs += [pl.BlockSpec((None, tm, cols), lambda i, ch: (ch[0], i, 0)),
                     pl.BlockSpec((3, tm, cols), lambda i, ch: (0, i, 0)), blk, blk, blk]
        out_specs += [blk] * 4
        out_shape += [_sds(s_.shape[1:], F32)] * 4
        args += [s_, l_, *t]
    out = pl.pallas_call(
        body, name=name,
        grid_spec=pltpu.PrefetchScalarGridSpec(num_scalar_prefetch=1, grid=(tiles,), in_specs=in_specs,
                                               out_specs=out_specs),
        out_shape=out_shape, compiler_params=_params(("parallel",)),
    )(chip, *args)
    return [out[4 * k:4 * k + 4] for k in range(n)]


_HBM = pl.BlockSpec(memory_space=pltpu.HBM)
_SEM = pl.BlockSpec(memory_space=pltpu.SEMAPHORE)
_EFFECT = pltpu.SideEffectType.DATAFLOW_SIDE_EFFECTING


def _chip_routes(n):
    def plan(x, y, c):
        routes = []
        for a in range(n):
            for j in range(1, 4):
                px, py = x ^ (j >> 1), y ^ (j & 1)
                routes.append((a, 2 * px + py, (px, py, c), j - 1))
        return routes
    return plan, 3 * n


def _pair_routes(n):
    def plan(x, y, c):
        return [(a, 2 * k + 1 - c, (x, y, 1 - c), k) for a in range(n) for k in range(4)]
    return plan, 4 * n


def _bcast_routes(n):
    def plan(x, y, c):
        routes = []
        for a in range(n):
            for k in range(1, N_DEV):
                peer = (x ^ ((k >> 2) & 1), y ^ ((k >> 1) & 1), c ^ (k & 1))
                routes.append((a, 0, peer, 4 * x + 2 * y + c))
        return routes
    return plan, 7 * n


def _route_copies(srcs, lands, send_sems, recv_sems, routes):
    return [pltpu.make_async_remote_copy(
        src_ref=srcs[a].at[sb], dst_ref=lands[a].at[lb], send_sem=send_sems.at[r], recv_sem=recv_sems.at[r],
        device_id=peer, device_id_type=MESH_T) for r, (a, sb, peer, lb) in enumerate(routes)]


def _exchange_start(srcs, lands, routes, name, after=()):
    plan, count = routes
    n = len(srcs)
    n_in = 2 * n + len(after)

    def body(*refs):
        send_sems, recv_sems = refs[n_in], refs[n_in + 1]
        token = refs[-1]
        for cp in _route_copies(refs[:n], refs[n:2 * n], send_sems, recv_sems, plan(*_mesh_pos())):
            cp.start()
        token[...] = jnp.zeros_like(token)

    args = [pltpu.with_memory_space_constraint(a, pltpu.HBM) for a in list(srcs) + list(lands)]
    out = pl.pallas_call(
        body, name=name,
        out_shape=(pltpu.SemaphoreType.DMA((count,)), pltpu.SemaphoreType.DMA((count,)),
                   *[pltpu.HBM(a.shape, a.dtype) for a in args], _sds((8, 128), F32)),
        in_specs=[_HBM] * (2 * n) + [pl.BlockSpec(memory_space=pl.ANY)] * len(after),
        out_specs=(_SEM, _SEM, *([_HBM] * (2 * n)), pl.BlockSpec(memory_space=pltpu.VMEM)),
        input_output_aliases={i: 2 + i for i in range(2 * n)},
        compiler_params=pltpu.CompilerParams(has_side_effects=_EFFECT),
    )(*args, *after)
    return (out[0], out[1], list(out[2:2 + 2 * n]), routes), out[-1]


def _exchange_wait(state, after, name):
    send_sems, recv_sems, bufs, (plan, count) = state
    n = len(bufs) // 2

    def body(*refs):
        send_s, recv_s = refs[2 * n], refs[2 * n + 1]
        for cp in _route_copies(refs[:n], refs[n:2 * n], send_s, recv_s, plan(*_mesh_pos())):
            cp.wait_send()
            cp.wait_recv()

    out = pl.pallas_call(
        body, name=name, out_shape=tuple(pltpu.HBM(a.shape, a.dtype) for a in bufs),
        in_specs=[_HBM] * (2 * n) + [_SEM, _SEM, pl.BlockSpec(memory_space=pl.ANY)], out_specs=tuple([_HBM] * (2 * n)),
        input_output_aliases={i: i for i in range(2 * n)},
        compiler_params=pltpu.CompilerParams(has_side_effects=_EFFECT),
    )(*bufs, send_sems, recv_sems, after)
    return list(out[:n]), list(out[n:])


def _group_routes(js):
    def plan(x, y, c):
        return [(0, 0, (x ^ (j >> 1), y ^ (j & 1), c), 2 * j + c) for j in js]
    return plan, len(js)


def _pair_fill(groups, js, name, after=()):
    def body(*refs):
        g_ref, send_sems, recv_sems = refs[-3:]
        x, y, c = _mesh_pos()
        sends = []
        for n, j in enumerate(js):
            mine = g_ref.at[2 * j + c]
            sends.append(pltpu.make_async_remote_copy(
                src_ref=mine, dst_ref=mine, send_sem=send_sems.at[n], recv_sem=recv_sems.at[n],
                device_id=(x, y, 1 - c), device_id_type=MESH_T))
        for cp in sends:
            cp.start()
        for n, j in enumerate(js):
            pltpu.make_async_remote_copy(
                src_ref=g_ref.at[2 * j + c], dst_ref=g_ref.at[2 * j + 1 - c], send_sem=send_sems.at[n],
                recv_sem=recv_sems.at[n], device_id=(x, y, 1 - c), device_id_type=MESH_T).wait_recv()
        for cp in sends:
            cp.wait_send()

    hbm = pl.BlockSpec(memory_space=pl.ANY)
    return pl.pallas_call(
        body, name=name, in_specs=[hbm] * (1 + len(after)), out_specs=hbm, out_shape=_sds(groups.shape, groups.dtype),
        input_output_aliases={0: 0},
        scratch_shapes=[pltpu.SemaphoreType.DMA((len(js),)), pltpu.SemaphoreType.DMA((len(js),))],
    )(groups, *after)


def _in_proj_group(h_all, groups, j0, ng, chip, px_prev, after, name):
    rows_all = h_all.shape[0]
    gcols = IN_COLS // 4
    tm = _pick(rows_all, 1536, 128)
    g4 = groups.reshape(4, gcols, D)

    n_lead = (1 if px_prev is not None else 0) + len(after)
    lead = ([px_prev] if px_prev is not None else []) + list(after)

    def body(chip_ref, *refs):
        h_ref, w_ref, o_ref = refs[n_lead:]
        o_ref[...] = _dot(h_ref[...], w_ref[...], NT).astype(BF16)

    return pl.pallas_call(
        body, name=name,
        grid_spec=pltpu.PrefetchScalarGridSpec(
            num_scalar_prefetch=1, grid=(ng, rows_all // tm),
            in_specs=[pl.BlockSpec(memory_space=pl.ANY)] * n_lead
            + [pl.BlockSpec((tm, D), lambda n, i, ch: (i, 0)),
               pl.BlockSpec((None, gcols, D), lambda n, i, ch: (j0 + n, 0, 0))],
            out_specs=pl.BlockSpec((tm, gcols), lambda n, i, ch: (i, ch[0] ^ (j0 + n)))),
        out_shape=_sds((rows_all, IN_COLS), BF16),
        input_output_aliases={1: 0} if px_prev is not None else {},
        compiler_params=_params(("parallel", "parallel")),
    )(chip, *lead, h_all, g4)


def _d_h_groups(dp_all, groups, chip, i0, ni, dh_prev, after):
    rows_all = dp_all.shape[0]
    gcols = IN_COLS // 4
    tm = _D_H_ROWS
    g4 = groups.reshape(4, gcols, D)
    lead = ([dh_prev] if dh_prev is not None else []) + list(after)
    n_lead = len(lead)

    def body(chip_ref, *refs):
        a_ref, w_ref, o_ref = refs[n_lead:]
        j = pl.program_id(1)
        part = _dot(a_ref[...], w_ref[...])

        @pl.when(j == 0)
        def _():
            o_ref[...] = part

        @pl.when(j > 0)
        def _():
            o_ref[...] += part

    return pl.pallas_call(
        body, name="d_h_%d" % i0,
        grid_spec=pltpu.PrefetchScalarGridSpec(
            num_scalar_prefetch=1, grid=(ni, 4),
            in_specs=[pl.BlockSpec(memory_space=pl.ANY)] * n_lead
            + [pl.BlockSpec((tm, gcols), lambda i, j, ch: (i0 + i, ch[0] ^ j)),
               pl.BlockSpec((None, gcols, D), lambda i, j, ch: (j, 0, 0))],
            out_specs=pl.BlockSpec((tm, D), lambda i, j, ch: (i0 + i, 0))),
        out_shape=_sds((rows_all, D), F32),
        input_output_aliases={1: 0} if dh_prev is not None else {},
        compiler_params=_params(("parallel", "arbitrary")),
    )(chip, *lead, dp_all, g4)


def _reduce_scatter_send(parts, got, core, name):
    sums = _pair_add(parts, got, core, name + "_add")
    lands = [lax.empty((3,) + s_.shape[1:], BF16) for s_ in sums]
    return _exchange_start(sums, lands, _chip_routes(len(sums)), name + "_start")


def _reduce_scatter_finish(rs_state, after, chip, wmv, name):
    sums, landed = _exchange_wait(rs_state, after, name + "_wait")
    return _chip_sum_adamw(sums, landed, chip, wmv, name + "_adamw")


def _local_step(x, c, ctx, norm_w, ret_log2_decay, q_norm_w, k_norm_w, loss_target,
                mod, proj_in, get_w_o, on_out_grads, on_in_grad, started=()):
    nb, seq, _ = x.shape
    cx = ctx.shape[1]
    t_rows, c_rows = nb * seq, nb * cx
    rows_all = t_rows + c_rows
    nc = seq // CH
    tm = _pick(seq, 256, 128)
    te = _pick(seq, 512, 128)
    assert cx % tm == 0 and t_rows % cx == 0 and seq % GRID_W == 0

    x2 = x.reshape(t_rows, D)
    ctx2 = ctx.reshape(c_rows, D)
    tgt = loss_target.reshape(t_rows, D)
    lg = _log_gamma(ret_log2_decay)
    cos, sin = _rope_tables(seq)

    mod3 = mod[:, None, :]
    h_all = _norm_fwd(x2, mod3, norm_w, rows_all, 0, seq, 0, None, te, "norm_fwd", after=started)
    h_all = _norm_fwd(ctx2, mod3, norm_w, rows_all, t_rows, c_rows, nb, h_all, tm, "norm_fwd_ctx")
    px = proj_in(h_all)
    o_f, o_b, hist_f, hist_b = _ret_fwd(px, lg, nb, nc, cx)
    q16 = _qk_prep(px, q_norm_w, cos, sin, t_rows, 0, AQ, HQ, 4, seq, te, "q_prep")
    kx16 = _qk_prep(px, k_norm_w, cos, sin, t_rows, 0, AK, HKV, HKV, seq, te, "k_prep")
    kc16 = _qk_prep(px, k_norm_w, None, None, c_rows, t_rows, AK, HKV, HKV, seq, tm, "kc_prep")
    o_att, yatt16, lse = _att_fwd(q16, kx16, kc16, px, nb, seq, cx, 2 * te)
    w_o_ret16, w_o_att16, w_out16 = get_w_o(lse)
    yret16, a_ret, a_att, dxn, dout16, dgate, loss_b, gw_out = _merge_out(
        o_f, o_b, yatt16, px, w_o_ret16, w_o_att16, w_out16, x2, tgt, mod3, nb, seq, tm)

    da_ret16, do16, dao16, delta, dp_all, gw_o_att = _bwd_branches(
        dout16, w_out16, w_o_ret16, w_o_att16, px, a_ret, a_att, o_f, o_b, o_att, yatt16, rows_all, tm)
    gw_o_ret = _matmul(yret16, da_ret16, ta=True, tm=D, tn=D, tk=D, out_dtype=BF16, name="gw_o_ret")
    out_send, out_started = on_out_grads([gw_o_ret, gw_o_att, gw_out])
    dp_all, gq, gk, dak16, dav16, dcak16, dcav16 = _att_bwd(q16, kx16, kc16, px, dao16, delta, lse, q_norm_w, k_norm_w,
                                                            cos, sin, dp_all, nb, seq, cx, 2 * te, after=out_started)
    out_state, out_sent = out_send(gq)
    dq_f, dk_f, dv_f, dq_b, dk_b, dv_b, dck16, dcv16, dlg_scan = _ret_bwd(px, lg, do16, hist_f, hist_b, nb, nc, cx)
    dp_all = _assemble_lat(dp_all, dk_f, dk_b, dv_f, dv_b, dak16, dav16, dq_f, dq_b, tm)
    dp_all = _assemble_ctx(dp_all, dck16, dcv16, dcak16, dcav16, t_rows, tm)
    gw_in_t = _matmul(dp_all, h_all, ta=True, tm=1536, tn=D, tk=2304, out_dtype=BF16, name="gw_in", after=out_sent)
    in_state, dh = on_in_grad(gw_in_t, dp_all)
    grad_x, dsh, dsc, gnw_lat = _norm_bwd(dh, x2, mod3, norm_w, dxn, 0, seq, 0, te, "norm_bwd")
    dsh_c, dsc_c, gnw_ctx = _norm_bwd(dh, ctx2, mod3, norm_w, None, t_rows, c_rows, nb, tm, "norm_bwd_ctx")

    dlg = jnp.sum(dlg_scan[:, :, 0], axis=0).reshape(1, 2 * RH)
    misc = jnp.concatenate([gq, gk, dlg, jnp.sum(loss_b[:, 0, 0]).reshape(1, 1),
                            jnp.zeros((1, D - 2 * HD - 2 * RH - 1), F32)], axis=1)
    rows = []
    for b in range(nb):
        rows += [dsh[b], dsc[b], dgate[b]]
    rows += [dsh_c[0], dsc_c[0]] + [c[b:b + 1] for b in range(nb)] + [gnw_lat + gnw_ctx, misc]
    payload = jnp.concatenate(rows + [jnp.zeros((PAY_ROWS - len(rows), D), F32)], axis=0)
    return grad_x.reshape(nb, seq, D), out_state, in_state, payload


def _finish_small(gathered, nb, c_ctx, ret_log2_decay, w_ada16, dev):
    n_dev = gathered.shape[0]
    loc = 3 * D // n_dev
    dmod_all = gathered[:, :3 * nb].reshape(n_dev * nb, 3 * D)
    dmodc_parts = jnp.concatenate([gathered[:, 3 * nb:3 * nb + 2].reshape(n_dev, 2 * D), jnp.zeros((n_dev, D), F32)], axis=1)
    c_all = gathered[:, 3 * nb + 2:4 * nb + 2].reshape(n_dev * nb, D)
    nw_parts = gathered[:, 4 * nb + 2]
    misc_parts = gathered[:, 4 * nb + 3]
    n_rows = n_dev * nb + n_dev
    pad = (-n_rows) % 16
    c_rows = jnp.concatenate([c_all, jnp.broadcast_to(c_ctx.reshape(1, D), (n_dev, D)), jnp.zeros((pad, D), F32)], axis=0)
    dm_rows = jnp.concatenate([dmod_all, dmodc_parts, jnp.zeros((pad, 3 * D), F32)], axis=0)
    dm_loc_rows = lax.dynamic_slice_in_dim(dm_rows, dev * loc, loc, axis=1)
    r_pad = jnp.full((1, D), -1.0, F32).at[:, 2 * HD:2 * HD + 2 * RH].set(ret_log2_decay.reshape(1, 2 * RH))
    gb, gc, gnw, misc, gwa = _small_final(dmod_all, dmodc_parts, c_rows, dm_loc_rows, nw_parts, misc_parts,
                                          c_ctx.reshape(1, D), r_pad, w_ada16)
    return (gb, gc, gnw, misc[:, :HD], misc[:, HD:2 * HD], misc[:, 2 * HD:2 * HD + 2 * RH], gwa,
            misc[0, 2 * HD + 2 * RH])


def kernel(x, c, ctx, c_ctx, norm_w, w_ada, b_ada, w_in, ret_log2_decay, q_norm_w, k_norm_w, w_o_ret, w_o_att, w_out, loss_target, m_c_ctx, m_norm_w, m_w_ada, m_b_ada, m_w_in, m_ret_log2_decay, m_q_norm_w, m_k_norm_w, m_w_o_ret, m_w_o_att, m_w_out, v_c_ctx, v_norm_w, v_w_ada, v_b_ada, v_w_in, v_ret_log2_decay, v_q_norm_w, v_k_norm_w, v_w_o_ret, v_w_o_att, v_w_out):
    nb = x.shape[0]
    mx, my, mc = _mesh_pos()
    dev = 4 * mx + 2 * my + mc
    core = jnp.reshape(mc, (1,)).astype(jnp.int32)
    chip = jnp.reshape(2 * mx + my, (1,)).astype(jnp.int32)

    n_loc = 3 * D // N_DEV
    c8 = jnp.zeros((8, D), F32).at[:nb].set(c).at[nb].set(c_ctx)
    c_land = lax.dynamic_update_slice(lax.empty((N_DEV, 8, D), F32), c8[None], (dev, 0, 0))
    c_state, c_token = _exchange_start([c8[None]], [c_land], _bcast_routes(1), "gather_c_start")
    w_in_t = jnp.transpose(w_in[0])
    in_shard = w_in_t.astype(BF16)
    groups = lax.dynamic_update_slice(lax.empty((N_DEV,) + in_shard.shape, BF16), in_shard[None], (mc, 0, 0))
    groups = _pair_fill(groups, (0,), "gather_in_pair", after=(c_token,))
    _, (c_all,) = _exchange_wait(c_state, groups, "gather_c_wait")
    ada_shard = w_ada[0].astype(BF16)
    b_loc = lax.dynamic_slice(b_ada, (0, dev * n_loc), (1, n_loc))
    mod_cols = _mod_part(c_all.reshape(N_DEV * 8, D), ada_shard, b_loc)
    (mod_all,) = _all_gather([mod_cols], "gather_mod")
    mod = jnp.transpose(lax.dynamic_slice(mod_all, (0, dev * 8, 0), (N_DEV, 8, n_loc)), (1, 0, 2)).reshape(8, 3 * D)
    ada_land = lax.dynamic_update_slice(lax.empty((N_DEV,) + ada_shard.shape, BF16), ada_shard[None], (dev, 0, 0))

    (near_send, near_recv, near_bufs, near_routes), gin_token = _exchange_start(
        [in_shard[None]], [groups], _group_routes((1, 2)), "gather_in_start", after=(mod_all,))
    w_in_groups, wo_states, ada_landed = [], [], []
    wo_shards = [w_[0].astype(BF16) for w_ in (w_o_ret, w_o_att, w_out)]
    wo_lands = [lax.dynamic_update_slice(lax.empty((N_DEV,) + s_.shape, BF16), s_[None], (dev, 0, 0)) for s_ in wo_shards]

    def _state(send, recv, src, groups, routes):
        return send, recv, [src, groups], routes

    def proj_in(h_all):
        src, groups = near_bufs
        px = _in_proj_group(h_all, groups, 0, 1, chip, None, (gin_token,), "in_proj_0")
        (src,), (groups,) = _exchange_wait(_state(near_send, near_recv, src, groups, near_routes), px,
                                           "gather_in_wait_near")
        groups = _pair_fill(groups, (1, 2), "gather_in_fill_near")
        (far_send, far_recv, (src, groups), far_routes), far_token = _exchange_start(
            [src], [groups], _group_routes((3,)), "gather_in_start_far")
        wo_state, wo_token = _exchange_start([s_[None] for s_ in wo_shards] + [ada_shard[None]], wo_lands + [ada_land],
                                             _bcast_routes(4), "gather_wo_start", after=(far_token,))
        wo_states.append(wo_state)
        px = _in_proj_group(h_all, groups, 1, 2, chip, px, (wo_token,), "in_proj_near")
        (src,), (groups,) = _exchange_wait(_state(far_send, far_recv, src, groups, far_routes), px,
                                           "gather_in_wait_far")
        groups = _pair_fill(groups, (3,), "gather_in_fill_far")
        px = _in_proj_group(h_all, groups, 3, 1, chip, px, (), "in_proj_far")
        w_in_groups.append(groups)
        return px

    def get_w_o(after):
        _, (l_ret, l_att, l_out, l_ada) = _exchange_wait(wo_states[0], after, "gather_wo_wait")
        ada_landed.append(l_ada)
        return l_ret.reshape(RH * DV, D), l_att.reshape(D, D), l_out.reshape(D, D)

    def on_out_grads(grads):
        parts = [g_.reshape(N_DEV, g_.shape[0] // N_DEV, D) for g_ in grads]
        lands = [lax.empty((4,) + p_.shape[1:], BF16) for p_ in parts]
        pair_state, pair_token = _exchange_start(parts, lands, _pair_routes(len(parts)), "rs_out_pair_start")

        def send(after):
            parts_, got = _exchange_wait(pair_state, after, "rs_out_pair_wait")
            state, token = _reduce_scatter_send(parts_, got, core, "rs_out")
            return state, (token,)

        return send, (pair_token,)

    def on_in_grad(grad, dp_all):
        parts = [grad.reshape(N_DEV, IN_COLS // N_DEV, D)]
        lands = [lax.empty((4,) + p_.shape[1:], BF16) for p_ in parts]
        pair_state, pair_token = _exchange_start(parts, lands, _pair_routes(1), "rs_in_pair_start")
        dh = _d_h_groups(dp_all, w_in_groups[0], chip, 0, 1, None, (pair_token,))
        parts, got = _exchange_wait(pair_state, dh, "rs_in_pair_wait")
        state, token = _reduce_scatter_send(parts, got, core, "rs_in")
        n_tiles = dp_all.shape[0] // _D_H_ROWS
        return state, _d_h_groups(dp_all, w_in_groups[0], chip, 1, n_tiles - 1, dh, (token,))

    grad_x, out_state, in_state, payload = _local_step(
        x, c, ctx, norm_w, ret_log2_decay, q_norm_w, k_norm_w, loss_target,
        mod, proj_in, get_w_o, on_out_grads, on_in_grad, started=(gin_token,))

    pay_land = lax.dynamic_update_slice(lax.empty((N_DEV,) + payload.shape, F32), payload[None], (dev, 0, 0))
    pay_state, pay_token = _exchange_start([payload[None]], [pay_land], _bcast_routes(1), "gather_small_start")

    out_res = _reduce_scatter_finish(out_state, pay_token, chip,
                                     [(w_[0], m_[0], v_[0]) for w_, m_, v_ in ((w_o_ret, m_w_o_ret, v_w_o_ret),
                                                                                (w_o_att, m_w_o_att, v_w_o_att),
                                                                                (w_out, m_w_out, v_w_out))], "rs_out")
    (in_res,) = _reduce_scatter_finish(in_state, out_res[0][0], chip,
                                       [(w_in_t, jnp.transpose(m_w_in[0]), jnp.transpose(v_w_in[0]))], "rs_in")

    _, (gathered,) = _exchange_wait(pay_state, in_res[0], "gather_small_wait")
    w_ada16 = jnp.transpose(ada_landed[0], (1, 0, 2)).reshape(D, 3 * D)
    gb, gc, gnw, gq, gk, gr, gwa, loss = _finish_small(gathered, nb, c_ctx, ret_log2_decay, w_ada16, dev)
    big = {4: [jnp.transpose(r)[None] for r in in_res]}
    for i, res in zip((8, 9, 10), out_res):
        big[i] = [r[None] for r in res]
    small_g = {0: gc.reshape(c_ctx.shape), 1: gnw, 2: gwa[None], 3: gb, 5: gr.reshape(ret_log2_decay.shape), 6: gq, 7: gk}
    weights = [c_ctx, norm_w, w_ada, b_ada, w_in, ret_log2_decay, q_norm_w, k_norm_w, w_o_ret, w_o_att, w_out]
    ms = [m_c_ctx, m_norm_w, m_w_ada, m_b_ada, m_w_in, m_ret_log2_decay, m_q_norm_w, m_k_norm_w, m_w_o_ret, m_w_o_att, m_w_out]
    vs = [v_c_ctx, v_norm_w, v_w_ada, v_b_ada, v_w_in, v_ret_log2_decay, v_q_norm_w, v_k_norm_w, v_w_o_ret, v_w_o_att, v_w_out]
    def rows2(i):
        return [a.reshape(-1, weights[i].shape[-1]) for a in (weights[i], small_g[i], ms[i], vs[i])]

    small_ids = [i for i in small_g if i != 2]
    steps = dict(zip(small_ids, _adamw_small([rows2(i) for i in small_ids])))
    steps[2] = _adamw(*rows2(2), "adamw_w_ada")
    grads, deltas, new_ms, new_vs = [], [], [], []
    for i, w in enumerate(weights):
        res = big[i] if i in big else [small_g[i]] + [r.reshape(w.shape) for r in steps[i]]
        for lst, r in zip((grads, deltas, new_ms, new_vs), res):
            lst.append(r)
    return (loss, grad_x, *grads, *deltas, *new_ms, *new_vs)
```
